```python
import jax, jax.numpy as jnp
from jax import lax
import numpy as np

D_MODEL = 1024
BATCH = 8
SEQ = 8192
DEPTH = 1

CTX_LEN = 256
GRID_W = 64
HG_HEADS = 8
HG_DK = 128
HG_DV = 128
HG_WK = HG_HEADS * HG_DK
HG_WV = HG_HEADS * HG_DV
HG_CHUNK = 32
RT_HEADS = 8
RT_DK = D_MODEL // RT_HEADS
RT_DV = 2 * D_MODEL // RT_HEADS
RT_WK = RT_HEADS * RT_DK
RT_WV = RT_HEADS * RT_DV
RT_CHUNK = 64
ROPE_BASE = 10000.0
D_FF = ((8 * D_MODEL // 3 + 255) // 256) * 256
EPS = 1e-6
GN_EPS = 1e-5
_SPLIT_SIZES = (HG_WK, HG_WK, HG_WK, HG_WV, HG_WV, RT_WK, RT_WK, RT_WV, RT_WV, D_MODEL, D_MODEL)
D_IN = sum(_SPLIT_SIZES)
_SPLIT_POINTS = tuple(int(v) for v in np.cumsum(_SPLIT_SIZES)[:-1])

kernel_name = "hybrid_hgrn2_retention_dit_block"


def _rmsnorm(x, w):
    xf = x.astype(jnp.float32)
    y = xf * lax.rsqrt(jnp.mean(xf * xf, axis=-1, keepdims=True) + EPS)
    return (y * w.astype(jnp.float32)).astype(x.dtype)


def _modulate(h, shift, scale):
    return h * (1.0 + scale) + shift


def _heads(t, n_heads):
    B, L, _ = t.shape
    return t.reshape(B, L, n_heads, -1).transpose(0, 2, 1, 3)


def _merge_heads(t):
    B, n, L, d = t.shape
    return t.transpose(0, 2, 1, 3).reshape(B, L, n * d)


def _chunk(t, C):
    B, H, L, d = t.shape
    return jnp.moveaxis(t.reshape(B, H, L // C, C, d), 2, 0)


def _unchunk(t):
    n, B, H, C, d = t.shape
    return jnp.moveaxis(t, 0, 2).reshape(B, H, n * C, d)


def _flip(t):
    return jnp.flip(t, axis=2)


def _axial_rope(L):
    rows = L // GRID_W
    row = jnp.repeat(jnp.arange(rows, dtype=jnp.float32), GRID_W)
    col = jnp.tile(jnp.arange(GRID_W, dtype=jnp.float32), rows)
    n_f = RT_DK // 4
    freqs = ROPE_BASE ** (-jnp.arange(n_f, dtype=jnp.float32) / n_f)
    ang = jnp.concatenate([row[:, None] * freqs, col[:, None] * freqs], axis=-1)
    return jnp.cos(ang), jnp.sin(ang)


def _apply_rope(t, cos, sin):
    half = t.shape[-1] // 2
    t1, t2 = t[..., :half], t[..., half:]
    return jnp.concatenate([t1 * cos - t2 * sin, t1 * sin + t2 * cos], axis=-1)


def _hgrn2_scan(q, k, v, lf, s0):
    C = HG_CHUNK
    causal = jnp.tril(jnp.ones((C, C), dtype=bool))[:, :, None]

    def step(s, blk):
        qc, kc, vc, lfc = blk
        b = jnp.cumsum(lfc, axis=2)
        rel = jnp.where(causal, b[:, :, :, None, :] - b[:, :, None, :, :], -jnp.inf)
        att = jnp.sum(qc[:, :, :, None, :] * kc[:, :, None, :, :] * jnp.exp(rel), axis=-1)
        o = (jnp.einsum('bhts,bhse->bhte', att, vc)
             + jnp.einsum('bhtd,bhde->bhte', qc * jnp.exp(b), s))
        b_last = b[:, :, -1:, :]
        s = (jnp.exp(b_last[:, :, 0, :])[..., None] * s
             + jnp.einsum('bhsd,bhse->bhde', kc * jnp.exp(b_last - b), vc))
        return s, o

    _, o = lax.scan(step, s0, (_chunk(q, C), _chunk(k, C), _chunk(v, C), _chunk(lf, C)))
    return _unchunk(o)


def _hgrn2_final_state(k, v, lf):
    b = jnp.cumsum(lf, axis=2)
    return jnp.einsum('bhsd,bhse->bhde', k * jnp.exp(b[:, :, -1:, :] - b), v)


def _retention_scan(q, k, v, log_gamma, s0):
    C = RT_CHUNK
    idx = jnp.arange(C, dtype=jnp.float32)
    diff = idx[:, None] - idx[None, :]
    lg = log_gamma[:, None, None]
    dmat = jnp.where(diff >= 0, jnp.exp(lg * jnp.maximum(diff, 0.0)), 0.0)
    q_decay = jnp.exp(log_gamma[:, None] * (idx + 1.0))[None, :, :, None]
    k_decay = jnp.exp(log_gamma[:, None] * (C - 1.0 - idx))[None, :, :, None]
    s_decay = jnp.exp(log_gamma * C)[None, :, None, None]

    def step(s, blk):
        qc, kc, vc = blk
        att = jnp.einsum('bhtd,bhsd->bhts', qc, kc) * dmat
        o = (jnp.einsum('bhts,bhse->bhte', att, vc)
             + jnp.einsum('bhtd,bhde->bhte', qc * q_decay, s))
        s = s_decay * s + jnp.einsum('bhsd,bhse->bhde', kc * k_decay, vc)
        return s, o

    _, o = lax.scan(step, s0, (_chunk(q, C), _chunk(k, C), _chunk(v, C)))
    return _unchunk(o)


def _retention_final_state(k, v, log_gamma):
    L = k.shape[2]
    w = jnp.exp(log_gamma[:, None] * (L - 1.0 - jnp.arange(L, dtype=jnp.float32)))
    return jnp.einsum('bhsd,hs,bhse->bhde', k, w, v)


def _head_rms(t):
    return t * lax.rsqrt(jnp.mean(t * t, axis=-1, keepdims=True) + EPS)


def _group_norm(t):
    mu = jnp.mean(t, axis=-1, keepdims=True)
    var = jnp.mean(jnp.square(t - mu), axis=-1, keepdims=True)
    return (t - mu) * lax.rsqrt(var + GN_EPS)


def _mixer_inputs(h, w_in, lb_f, lb_b, rope):
    p = jnp.matmul(h, w_in).astype(jnp.float32)
    hq, hf_f, hf_b, hi, hg, rq, rk, rv, rg, ga, gb = jnp.split(p, _SPLIT_POINTS, axis=-1)

    def forget(z, lb):
        k = _heads((1.0 - lb) * jax.nn.sigmoid(-z), HG_HEADS)
        lf = _heads(jnp.log(lb + (1.0 - lb) * jax.nn.sigmoid(z)), HG_HEADS)
        return k, lf

    hk_f, hlf_f = forget(hf_f, lb_f)
    hk_b, hlf_b = forget(hf_b, lb_b)
    rq = _heads(rq, RT_HEADS) * RT_DK ** -0.5
    rk = _heads(rk, RT_HEADS)
    if rope is not None:
        rq = _apply_rope(rq, rope[0], rope[1])
        rk = _apply_rope(rk, rope[0], rope[1])
    return dict(hq=_heads(hq, HG_HEADS) * HG_DK ** -0.5, hk_f=hk_f, hlf_f=hlf_f,
                hk_b=hk_b, hlf_b=hlf_b, hv=_heads(hi, HG_HEADS), hg=hg,
                rq=rq, rk=rk, rv=_heads(rv, RT_HEADS), rg=rg, ga=ga, gb=gb)


def _context_states(p, log_g_f, log_g_b):
    s_hf = _hgrn2_final_state(p["hk_f"], p["hv"], p["hlf_f"])
    s_hb = _hgrn2_final_state(_flip(p["hk_b"]), _flip(p["hv"]), _flip(p["hlf_b"]))
    s_rf = _retention_final_state(p["rk"], p["rv"], log_g_f)
    s_rb = _retention_final_state(_flip(p["rk"]), _flip(p["rv"]), log_g_b)
    return s_hf, s_hb, s_rf, s_rb


def _mix(p, s_hf, s_hb, s_rf, s_rb, log_g_f, log_g_b, hg_norm_w, w_pa, w_pb, w_out):
    hg = (_hgrn2_scan(p["hq"], p["hk_f"], p["hv"], p["hlf_f"], s_hf)
          + _flip(_hgrn2_scan(_flip(p["hq"]), _flip(p["hk_b"]), _flip(p["hv"]),
                              _flip(p["hlf_b"]), s_hb)))
    rt = (_retention_scan(p["rq"], p["rk"], p["rv"], log_g_f, s_rf)
          + _flip(_retention_scan(_flip(p["rq"]), _flip(p["rk"]), _flip(p["rv"]), log_g_b, s_rb)))
    ya = _merge_heads(_head_rms(hg)) * hg_norm_w * jax.nn.silu(p["hg"])
    yb = _merge_heads(_group_norm(rt)) * jax.nn.silu(p["rg"])
    merged = (jax.nn.sigmoid(p["ga"]) * jnp.matmul(ya, w_pa)
              + jax.nn.sigmoid(p["gb"]) * jnp.matmul(yb, w_pb))
    return jnp.matmul(merged, w_out)


def _swiglu(h, w_gate, w_up, w_down):
    return jnp.matmul(jax.nn.silu(jnp.matmul(h, w_gate)) * jnp.matmul(h, w_up), w_down)


def _fwd_setup_inputs(seed: int = 0) -> dict:
    key = jax.random.key(seed)
    ks = jax.random.split(key, 24)
    f32 = jnp.float32

    def nrm(k, shape, scale):
        return jax.random.normal(k, shape, f32) * scale

    gamma = 1.0 - 2.0 ** (-5.0 - jnp.arange(RT_HEADS, dtype=f32))
    theta0 = jnp.log(gamma) - jnp.log1p(-gamma)
    return {
        "x": nrm(ks[0], (BATCH, SEQ, D_MODEL), 1.0),
        "c": nrm(ks[1], (BATCH, D_MODEL), 1.0),
        "ctx": nrm(ks[2], (BATCH, CTX_LEN, D_MODEL), 1.0),
        "c_ctx": nrm(ks[3], (D_MODEL,), 1.0),
        "w_mod": nrm(ks[4], (DEPTH, D_MODEL, 6 * D_MODEL), 0.5 * D_MODEL ** -0.5),
        "b_mod": nrm(ks[5], (DEPTH, 6 * D_MODEL), 0.02),
        "norm1_w": 1.0 + nrm(ks[6], (DEPTH, D_MODEL), 0.02),
        "norm2_w": 1.0 + nrm(ks[7], (DEPTH, D_MODEL), 0.02),
        "w_in": nrm(ks[8], (DEPTH, D_MODEL, D_IN), D_MODEL ** -0.5),
        "hg_lb_fwd": nrm(ks[9], (DEPTH + 1, HG_WK), 0.1),
        "hg_lb_bwd": nrm(ks[10], (DEPTH + 1, HG_WK), 0.1),
        "hg_norm_w": 1.0 + nrm(ks[11], (DEPTH, HG_WV), 0.02),
        "rt_theta_fwd": theta0 + nrm(ks[12], (DEPTH, RT_HEADS), 0.1),
        "rt_theta_bwd": theta0 + nrm(ks[13], (DEPTH, RT_HEADS), 0.1),
        "w_proj_hgrn": nrm(ks[14], (DEPTH, HG_WV, D_MODEL), HG_WV ** -0.5),
        "w_proj_ret": nrm(ks[15], (DEPTH, RT_WV, D_MODEL), RT_WV ** -0.5),
        "w_out": nrm(ks[16], (DEPTH, D_MODEL, D_MODEL), D_MODEL ** -0.5),
        "w_ffn_gate": nrm(ks[17], (DEPTH, D_MODEL, D_FF), D_MODEL ** -0.5),
        "w_ffn_up": nrm(ks[18], (DEPTH, D_MODEL, D_FF), D_MODEL ** -0.5),
        "w_ffn_down": nrm(ks[19], (DEPTH, D_FF, D_MODEL), D_FF ** -0.5),
        "final_norm_w": 1.0 + nrm(ks[20], (D_MODEL,), 0.02),
    }


def _fwd_reference(x, c, ctx, c_ctx, w_mod, b_mod, norm1_w, norm2_w, w_in, hg_lb_fwd, hg_lb_bwd,
              hg_norm_w, rt_theta_fwd, rt_theta_bwd, w_proj_hgrn, w_proj_ret, w_out,
              w_ffn_gate, w_ffn_up, w_ffn_down, final_norm_w):
    B, L, _ = x.shape
    f32 = jnp.float32
    rope = _axial_rope(L)
    lb_f_all = jnp.cumsum(jax.nn.softmax(hg_lb_fwd.astype(f32), axis=0), axis=0)
    lb_b_all = jnp.cumsum(jax.nn.softmax(hg_lb_bwd.astype(f32), axis=0), axis=0)
    hg_zero = jnp.zeros((B, HG_HEADS, HG_DK, HG_DV), f32)
    rt_zero = jnp.zeros((B, RT_HEADS, RT_DK, RT_DV), f32)

    for l in range(DEPTH):
        last = l == DEPTH - 1
        mod_x = (jnp.matmul(jax.nn.silu(c), w_mod[l]) + b_mod[l])[:, None, :]
        mod_c = jnp.matmul(jax.nn.silu(c_ctx), w_mod[l]) + b_mod[l]
        sh1, sc1, g1, sh2, sc2, g2 = jnp.split(mod_x, 6, axis=-1)
        sh1c, sc1c, g1c, sh2c, sc2c, g2c = jnp.split(mod_c, 6, axis=-1)
        log_g_f = jax.nn.log_sigmoid(rt_theta_fwd[l].astype(f32))
        log_g_b = jax.nn.log_sigmoid(rt_theta_bwd[l].astype(f32))
        branch_w = (hg_norm_w[l], w_proj_hgrn[l], w_proj_ret[l], w_out[l])

        hc = _modulate(_rmsnorm(ctx, norm1_w[l]), sh1c, sc1c)
        pc = _mixer_inputs(hc, w_in[l], lb_f_all[l], lb_b_all[l], None)
        s_hf, s_hb, s_rf, s_rb = _context_states(pc, log_g_f, log_g_b)
        if not last:
            ctx_mix = _mix(pc, hg_zero, hg_zero, rt_zero, rt_zero, log_g_f, log_g_b, *branch_w)
            ctx = ctx + g1c * ctx_mix.astype(ctx.dtype)
            hc2 = _modulate(_rmsnorm(ctx, norm2_w[l]), sh2c, sc2c)
            ctx = ctx + g2c * _swiglu(hc2, w_ffn_gate[l], w_ffn_up[l], w_ffn_down[l])

        hx = _modulate(_rmsnorm(x, norm1_w[l]), sh1, sc1)
        px = _mixer_inputs(hx, w_in[l], lb_f_all[l], lb_b_all[l], rope)
        x_mix = _mix(px, s_hf, s_hb, s_rf, s_rb, log_g_f, log_g_b, *branch_w)
        x = x + g1 * x_mix.astype(x.dtype)
        hx2 = _modulate(_rmsnorm(x, norm2_w[l]), sh2, sc2)
        x = x + g2 * _swiglu(hx2, w_ffn_gate[l], w_ffn_up[l], w_ffn_down[l])

    return _rmsnorm(x, final_norm_w)


import jax as _jax
import jax.numpy as _jnp

TWIN_FORMAT = 'train_step'
FWD_PARAMS = ['x', 'c', 'ctx', 'c_ctx', 'w_mod', 'b_mod', 'norm1_w', 'norm2_w', 'w_in', 'hg_lb_fwd', 'hg_lb_bwd', 'hg_norm_w', 'rt_theta_fwd', 'rt_theta_bwd', 'w_proj_hgrn', 'w_proj_ret', 'w_out', 'w_ffn_gate', 'w_ffn_up', 'w_ffn_down', 'final_norm_w']
TWIN_WEIGHTS = ['c_ctx', 'w_mod', 'b_mod', 'norm1_w', 'norm2_w', 'w_in', 'hg_lb_fwd', 'hg_lb_bwd', 'hg_norm_w', 'rt_theta_fwd', 'rt_theta_bwd', 'w_proj_hgrn', 'w_proj_ret', 'w_out', 'w_ffn_gate', 'w_ffn_up', 'w_ffn_down', 'final_norm_w']
TWIN_DIFF_INPUT = 'x'
TWIN_INPUTS = ['x', 'c', 'ctx', 'c_ctx', 'w_mod', 'b_mod', 'norm1_w', 'norm2_w', 'w_in', 'hg_lb_fwd', 'hg_lb_bwd', 'hg_norm_w', 'rt_theta_fwd', 'rt_theta_bwd', 'w_proj_hgrn', 'w_proj_ret', 'w_out', 'w_ffn_gate', 'w_ffn_up', 'w_ffn_down', 'final_norm_w', 'loss_target', 'm_c_ctx', 'm_w_mod', 'm_b_mod', 'm_norm1_w', 'm_norm2_w', 'm_w_in', 'm_hg_lb_fwd', 'm_hg_lb_bwd', 'm_hg_norm_w', 'm_rt_theta_fwd', 'm_rt_theta_bwd', 'm_w_proj_hgrn', 'm_w_proj_ret', 'm_w_out', 'm_w_ffn_gate', 'm_w_ffn_up', 'm_w_ffn_down', 'm_final_norm_w', 'v_c_ctx', 'v_w_mod', 'v_b_mod', 'v_norm1_w', 'v_norm2_w', 'v_w_in', 'v_hg_lb_fwd', 'v_hg_lb_bwd', 'v_hg_norm_w', 'v_rt_theta_fwd', 'v_rt_theta_bwd', 'v_w_proj_hgrn', 'v_w_proj_ret', 'v_w_out', 'v_w_ffn_gate', 'v_w_ffn_up', 'v_w_ffn_down', 'v_final_norm_w']
TWIN_OUTPUTS = ['loss', 'grad_x', 'grad_c_ctx', 'grad_w_mod', 'grad_b_mod', 'grad_norm1_w', 'grad_norm2_w', 'grad_w_in', 'grad_hg_lb_fwd', 'grad_hg_lb_bwd', 'grad_hg_norm_w', 'grad_rt_theta_fwd', 'grad_rt_theta_bwd', 'grad_w_proj_hgrn', 'grad_w_proj_ret', 'grad_w_out', 'grad_w_ffn_gate', 'grad_w_ffn_up', 'grad_w_ffn_down', 'grad_final_norm_w', 'delta_c_ctx', 'delta_w_mod', 'delta_b_mod', 'delta_norm1_w', 'delta_norm2_w', 'delta_w_in', 'delta_hg_lb_fwd', 'delta_hg_lb_bwd', 'delta_hg_norm_w', 'delta_rt_theta_fwd', 'delta_rt_theta_bwd', 'delta_w_proj_hgrn', 'delta_w_proj_ret', 'delta_w_out', 'delta_w_ffn_gate', 'delta_w_ffn_up', 'delta_w_ffn_down', 'delta_final_norm_w', 'new_m_c_ctx', 'new_m_w_mod', 'new_m_b_mod', 'new_m_norm1_w', 'new_m_norm2_w', 'new_m_w_in', 'new_m_hg_lb_fwd', 'new_m_hg_lb_bwd', 'new_m_hg_norm_w', 'new_m_rt_theta_fwd', 'new_m_rt_theta_bwd', 'new_m_w_proj_hgrn', 'new_m_w_proj_ret', 'new_m_w_out', 'new_m_w_ffn_gate', 'new_m_w_ffn_up', 'new_m_w_ffn_down', 'new_m_final_norm_w', 'new_v_c_ctx', 'new_v_w_mod', 'new_v_b_mod', 'new_v_norm1_w', 'new_v_norm2_w', 'new_v_w_in', 'new_v_hg_lb_fwd', 'new_v_hg_lb_bwd', 'new_v_hg_norm_w', 'new_v_rt_theta_fwd', 'new_v_rt_theta_bwd', 'new_v_w_proj_hgrn', 'new_v_w_proj_ret', 'new_v_w_out', 'new_v_w_ffn_gate', 'new_v_w_ffn_up', 'new_v_w_ffn_down', 'new_v_final_norm_w']
TWIN_LEAF_KINDS = {'loss': 'loss', 'grad_x': 'grad_x', 'grad_c_ctx': 'grad_w', 'grad_w_mod': 'grad_w', 'grad_b_mod': 'grad_w', 'grad_norm1_w': 'grad_w', 'grad_norm2_w': 'grad_w', 'grad_w_in': 'grad_w', 'grad_hg_lb_fwd': 'grad_w', 'grad_hg_lb_bwd': 'grad_w', 'grad_hg_norm_w': 'grad_w', 'grad_rt_theta_fwd': 'grad_w', 'grad_rt_theta_bwd': 'grad_w', 'grad_w_proj_hgrn': 'grad_w', 'grad_w_proj_ret': 'grad_w', 'grad_w_out': 'grad_w', 'grad_w_ffn_gate': 'grad_w', 'grad_w_ffn_up': 'grad_w', 'grad_w_ffn_down': 'grad_w', 'grad_final_norm_w': 'grad_w', 'delta_c_ctx': 'delta_w', 'delta_w_mod': 'delta_w', 'delta_b_mod': 'delta_w', 'delta_norm1_w': 'delta_w', 'delta_norm2_w': 'delta_w', 'delta_w_in': 'delta_w', 'delta_hg_lb_fwd': 'delta_w', 'delta_hg_lb_bwd': 'delta_w', 'delta_hg_norm_w': 'delta_w', 'delta_rt_theta_fwd': 'delta_w', 'delta_rt_theta_bwd': 'delta_w', 'delta_w_proj_hgrn': 'delta_w', 'delta_w_proj_ret': 'delta_w', 'delta_w_out': 'delta_w', 'delta_w_ffn_gate': 'delta_w', 'delta_w_ffn_up': 'delta_w', 'delta_w_ffn_down': 'delta_w', 'delta_final_norm_w': 'delta_w', 'new_m_c_ctx': 'new_m', 'new_m_w_mod': 'new_m', 'new_m_b_mod': 'new_m', 'new_m_norm1_w': 'new_m', 'new_m_norm2_w': 'new_m', 'new_m_w_in': 'new_m', 'new_m_hg_lb_fwd': 'new_m', 'new_m_hg_lb_bwd': 'new_m', 'new_m_hg_norm_w': 'new_m', 'new_m_rt_theta_fwd': 'new_m', 'new_m_rt_theta_bwd': 'new_m', 'new_m_w_proj_hgrn': 'new_m', 'new_m_w_proj_ret': 'new_m', 'new_m_w_out': 'new_m', 'new_m_w_ffn_gate': 'new_m', 'new_m_w_ffn_up': 'new_m', 'new_m_w_ffn_down': 'new_m', 'new_m_final_norm_w': 'new_m', 'new_v_c_ctx': 'new_v', 'new_v_w_mod': 'new_v', 'new_v_b_mod': 'new_v', 'new_v_norm1_w': 'new_v', 'new_v_norm2_w': 'new_v', 'new_v_w_in': 'new_v', 'new_v_hg_lb_fwd': 'new_v', 'new_v_hg_lb_bwd': 'new_v', 'new_v_hg_norm_w': 'new_v', 'new_v_rt_theta_fwd': 'new_v', 'new_v_rt_theta_bwd': 'new_v', 'new_v_w_proj_hgrn': 'new_v', 'new_v_w_proj_ret': 'new_v', 'new_v_w_out': 'new_v', 'new_v_w_ffn_gate': 'new_v', 'new_v_w_ffn_up': 'new_v', 'new_v_w_ffn_down': 'new_v', 'new_v_final_norm_w': 'new_v'}


def _forward(args):
    return _fwd_reference(*[args[k] for k in FWD_PARAMS])


def _output_shape():
    def fwd():
        inp = _fwd_setup_inputs(0)
        return _fwd_reference(*[inp[k] for k in FWD_PARAMS])
    out = _jax.eval_shape(fwd)
    return out.shape, out.dtype

N_MICROBATCH = 1
ADAM_LR = 0.001
ADAM_B1 = 0.9
ADAM_B2 = 0.999
ADAM_EPS = 1e-08
ADAM_WD = 0.01
ADAM_STEP = 10
PER_EXAMPLE_BATCH_AXIS = {'x': 0, 'c': 0, 'ctx': 0, 'loss_target': 0}
SHARED_INPUTS = []
_WEIGHT_DTYPES = {'c_ctx': _jnp.float32, 'w_mod': _jnp.float32, 'b_mod': _jnp.float32, 'norm1_w': _jnp.float32, 'norm2_w': _jnp.float32, 'w_in': _jnp.float32, 'hg_lb_fwd': _jnp.float32, 'hg_lb_bwd': _jnp.float32, 'hg_norm_w': _jnp.float32, 'rt_theta_fwd': _jnp.float32, 'rt_theta_bwd': _jnp.float32, 'w_proj_hgrn': _jnp.float32, 'w_proj_ret': _jnp.float32, 'w_out': _jnp.float32, 'w_ffn_gate': _jnp.float32, 'w_ffn_up': _jnp.float32, 'w_ffn_down': _jnp.float32, 'final_norm_w': _jnp.float32}
MOMENT_SCALE = {'c_ctx': 1.961319e-02, 'w_mod': 7.262830e-02, 'b_mod': 1.254521e-01, 'norm1_w': 8.729695e-02, 'norm2_w': 7.328732e-02, 'w_in': 2.484112e-02, 'hg_lb_fwd': 1.061694e-02, 'hg_lb_bwd': 1.054132e-02, 'hg_norm_w': 3.067209e-02, 'rt_theta_fwd': 9.623295e-02, 'rt_theta_bwd': 9.015195e-02, 'w_proj_hgrn': 2.746633e-02, 'w_proj_ret': 2.739974e-02, 'w_out': 3.877805e-02, 'w_ffn_gate': 3.292915e-02, 'w_ffn_up': 3.185210e-02, 'w_ffn_down': 5.288960e-02, 'final_norm_w': 6.405918e+01}


def _to_microbatches(a, axis):
    t = _jnp.moveaxis(a, axis, 0)
    t = t.reshape((N_MICROBATCH, t.shape[0] // N_MICROBATCH) + t.shape[1:])
    return _jnp.moveaxis(t, 1, axis + 1)


def setup_inputs(seed: int = 0) -> dict:
    inp = _fwd_setup_inputs(seed)
    key = _jax.random.fold_in(_jax.random.key(seed), 7919)
    shape, _ = _output_shape()
    out = dict(inp)
    out["loss_target"] = _jax.random.normal(_jax.random.fold_in(key, 0), shape, _jnp.float32)
    for i, name in enumerate(TWIN_WEIGHTS):
        w = inp[name].astype(_jnp.float32)
        if MOMENT_SCALE is None:
            s = _jnp.sqrt(_jnp.mean(_jnp.square(w)) + 1e-30)
        else:
            s = MOMENT_SCALE[name]
        km, kv = _jax.random.split(_jax.random.fold_in(key, i + 1))
        out[name] = w
        out["m_" + name] = s * _jax.random.normal(km, w.shape, _jnp.float32)
        out["v_" + name] = (s * s) * _jax.random.uniform(kv, w.shape, _jnp.float32, 0.5, 1.5)
    if N_MICROBATCH > 1:
        for name, axis in PER_EXAMPLE_BATCH_AXIS.items():
            out[name] = _to_microbatches(out[name], axis)
    return {'x': out['x'], 'c': out['c'], 'ctx': out['ctx'], 'c_ctx': out['c_ctx'], 'w_mod': out['w_mod'], 'b_mod': out['b_mod'], 'norm1_w': out['norm1_w'], 'norm2_w': out['norm2_w'], 'w_in': out['w_in'], 'hg_lb_fwd': out['hg_lb_fwd'], 'hg_lb_bwd': out['hg_lb_bwd'], 'hg_norm_w': out['hg_norm_w'], 'rt_theta_fwd': out['rt_theta_fwd'], 'rt_theta_bwd': out['rt_theta_bwd'], 'w_proj_hgrn': out['w_proj_hgrn'], 'w_proj_ret': out['w_proj_ret'], 'w_out': out['w_out'], 'w_ffn_gate': out['w_ffn_gate'], 'w_ffn_up': out['w_ffn_up'], 'w_ffn_down': out['w_ffn_down'], 'final_norm_w': out['final_norm_w'], 'loss_target': out['loss_target'], 'm_c_ctx': out['m_c_ctx'], 'm_w_mod': out['m_w_mod'], 'm_b_mod': out['m_b_mod'], 'm_norm1_w': out['m_norm1_w'], 'm_norm2_w': out['m_norm2_w'], 'm_w_in': out['m_w_in'], 'm_hg_lb_fwd': out['m_hg_lb_fwd'], 'm_hg_lb_bwd': out['m_hg_lb_bwd'], 'm_hg_norm_w': out['m_hg_norm_w'], 'm_rt_theta_fwd': out['m_rt_theta_fwd'], 'm_rt_theta_bwd': out['m_rt_theta_bwd'], 'm_w_proj_hgrn': out['m_w_proj_hgrn'], 'm_w_proj_ret': out['m_w_proj_ret'], 'm_w_out': out['m_w_out'], 'm_w_ffn_gate': out['m_w_ffn_gate'], 'm_w_ffn_up': out['m_w_ffn_up'], 'm_w_ffn_down': out['m_w_ffn_down'], 'm_final_norm_w': out['m_final_norm_w'], 'v_c_ctx': out['v_c_ctx'], 'v_w_mod': out['v_w_mod'], 'v_b_mod': out['v_b_mod'], 'v_norm1_w': out['v_norm1_w'], 'v_norm2_w': out['v_norm2_w'], 'v_w_in': out['v_w_in'], 'v_hg_lb_fwd': out['v_hg_lb_fwd'], 'v_hg_lb_bwd': out['v_hg_lb_bwd'], 'v_hg_norm_w': out['v_hg_norm_w'], 'v_rt_theta_fwd': out['v_rt_theta_fwd'], 'v_rt_theta_bwd': out['v_rt_theta_bwd'], 'v_w_proj_hgrn': out['v_w_proj_hgrn'], 'v_w_proj_ret': out['v_w_proj_ret'], 'v_w_out': out['v_w_out'], 'v_w_ffn_gate': out['v_w_ffn_gate'], 'v_w_ffn_up': out['v_w_ffn_up'], 'v_w_ffn_down': out['v_w_ffn_down'], 'v_final_norm_w': out['v_final_norm_w']}


def _loss(weights, diff, rest, loss_target):
    with _jax.named_scope("forward"):
        args = {**rest, TWIN_DIFF_INPUT: diff, **{k: w.astype(_WEIGHT_DTYPES[k]) for k, w in weights.items()}}
        y = _forward(args)
    with _jax.named_scope("loss_head"):
        err = _jnp.square(y.astype(_jnp.float32) - loss_target)
        return 0.5 * _jnp.sum(_jnp.mean(err, axis=-1)) if err.ndim else 0.5 * err


def _adamw(w, g, m, v):
    m = ADAM_B1 * m + (1.0 - ADAM_B1) * g
    v = ADAM_B2 * v + (1.0 - ADAM_B2) * _jnp.square(g)
    m_hat = m / (1.0 - ADAM_B1 ** ADAM_STEP)
    v_hat = v / (1.0 - ADAM_B2 ** ADAM_STEP)
    delta = -ADAM_LR * (m_hat / (_jnp.sqrt(v_hat) + ADAM_EPS) + ADAM_WD * w)
    return delta, m, v


def reference(x, c, ctx, c_ctx, w_mod, b_mod, norm1_w, norm2_w, w_in, hg_lb_fwd, hg_lb_bwd, hg_norm_w, rt_theta_fwd, rt_theta_bwd, w_proj_hgrn, w_proj_ret, w_out, w_ffn_gate, w_ffn_up, w_ffn_down, final_norm_w, loss_target, m_c_ctx, m_w_mod, m_b_mod, m_norm1_w, m_norm2_w, m_w_in, m_hg_lb_fwd, m_hg_lb_bwd, m_hg_norm_w, m_rt_theta_fwd, m_rt_theta_bwd, m_w_proj_hgrn, m_w_proj_ret, m_w_out, m_w_ffn_gate, m_w_ffn_up, m_w_ffn_down, m_final_norm_w, v_c_ctx, v_w_mod, v_b_mod, v_norm1_w, v_norm2_w, v_w_in, v_hg_lb_fwd, v_hg_lb_bwd, v_hg_norm_w, v_rt_theta_fwd, v_rt_theta_bwd, v_w_proj_hgrn, v_w_proj_ret, v_w_out, v_w_ffn_gate, v_w_ffn_up, v_w_ffn_down, v_final_norm_w):
    given = dict(x=x, c=c, ctx=ctx, c_ctx=c_ctx, w_mod=w_mod, b_mod=b_mod, norm1_w=norm1_w, norm2_w=norm2_w, w_in=w_in, hg_lb_fwd=hg_lb_fwd, hg_lb_bwd=hg_lb_bwd, hg_norm_w=hg_norm_w, rt_theta_fwd=rt_theta_fwd, rt_theta_bwd=rt_theta_bwd, w_proj_hgrn=w_proj_hgrn, w_proj_ret=w_proj_ret, w_out=w_out, w_ffn_gate=w_ffn_gate, w_ffn_up=w_ffn_up, w_ffn_down=w_ffn_down, final_norm_w=final_norm_w, loss_target=loss_target, m_c_ctx=m_c_ctx, m_w_mod=m_w_mod, m_b_mod=m_b_mod, m_norm1_w=m_norm1_w, m_norm2_w=m_norm2_w, m_w_in=m_w_in, m_hg_lb_fwd=m_hg_lb_fwd, m_hg_lb_bwd=m_hg_lb_bwd, m_hg_norm_w=m_hg_norm_w, m_rt_theta_fwd=m_rt_theta_fwd, m_rt_theta_bwd=m_rt_theta_bwd, m_w_proj_hgrn=m_w_proj_hgrn, m_w_proj_ret=m_w_proj_ret, m_w_out=m_w_out, m_w_ffn_gate=m_w_ffn_gate, m_w_ffn_up=m_w_ffn_up, m_w_ffn_down=m_w_ffn_down, m_final_norm_w=m_final_norm_w, v_c_ctx=v_c_ctx, v_w_mod=v_w_mod, v_b_mod=v_b_mod, v_norm1_w=v_norm1_w, v_norm2_w=v_norm2_w, v_w_in=v_w_in, v_hg_lb_fwd=v_hg_lb_fwd, v_hg_lb_bwd=v_hg_lb_bwd, v_hg_norm_w=v_hg_norm_w, v_rt_theta_fwd=v_rt_theta_fwd, v_rt_theta_bwd=v_rt_theta_bwd, v_w_proj_hgrn=v_w_proj_hgrn, v_w_proj_ret=v_w_proj_ret, v_w_out=v_w_out, v_w_ffn_gate=v_w_ffn_gate, v_w_ffn_up=v_w_ffn_up, v_w_ffn_down=v_w_ffn_down, v_final_norm_w=v_final_norm_w)
    weights = {n: given[n] for n in TWIN_WEIGHTS}
    shared = {n: given[n] for n in SHARED_INPUTS}
    per_example = {n: given[n] for n in ['x', 'c', 'ctx']}
    grad_fn = _jax.value_and_grad(_loss, argnums=(0, 1))

    def one_microbatch(ex, loss_target):
        ex = dict(ex)
        diff = ex.pop(TWIN_DIFF_INPUT)
        return grad_fn(weights, diff, {**shared, **ex}, loss_target)

    if N_MICROBATCH == 1:
        loss, (grad_w, grad_x) = one_microbatch(per_example, given["loss_target"])
    else:
        def body(carry, xs):
            loss_sum, grad_sum = carry
            l_k, (gw_k, gx_k) = one_microbatch(xs[0], xs[1])
            with _jax.named_scope("update"):
                return (loss_sum + l_k, _jax.tree.map(_jnp.add, grad_sum, gw_k)), gx_k

        init = (_jnp.zeros((), _jnp.float32), _jax.tree.map(_jnp.zeros_like, weights))
        (loss, grad_w), grad_x = _jax.lax.scan(body, init, (per_example, given["loss_target"]))
    with _jax.named_scope("update"):
        delta_w, new_m, new_v = {}, {}, {}
        for n in TWIN_WEIGHTS:
            delta_w[n], new_m[n], new_v[n] = _adamw(weights[n], grad_w[n], given["m_" + n], given["v_" + n])
    return (loss, grad_x, *[grad_w[n] for n in TWIN_WEIGHTS], *[delta_w[n] for n in TWIN_WEIGHTS],
            *[new_m[n] for n in TWIN_WEIGHTS], *[new_v[n] for n in TWIN_WEIGHTS])
```

```python
import functools

import jax
import jax.numpy as jnp
from jax import lax
from jax.experimental import pallas as pl
from jax.experimental.pallas import tpu as pltpu

F32 = jnp.float32
BF16 = jnp.bfloat16
HI = lax.Precision.HIGHEST

D = 1024
HEADS = 8
HG_D = 128
RT_DK = 128
RT_DV = 256
D_FF = 2816
D_IN = 13312
N_SHARD = 4
IN_SH = D_IN // N_SHARD
FF_SH = D_FF // N_SHARD
HG_CHUNK = 32
SCAN_ROWS = 256
EPS = 1e-6
GN_EPS = 1e-5
Q_SCALE = 128.0 ** -0.5
VMEM_LIMIT = 56 * 1024 * 1024

COL_HQ, COL_HFF, COL_HFB, COL_HI, COL_HG = 0, 8, 16, 24, 32
COL_RQ, COL_RK, COL_RV, COL_RG, COL_GA, COL_GB = 40, 48, 56, 72, 88, 96

ADAM_LR, ADAM_B1, ADAM_B2, ADAM_EPS, ADAM_WD, ADAM_STEP = 0.001, 0.9, 0.999, 1e-08, 0.01, 10


def _params(*sem):
    return pltpu.CompilerParams(dimension_semantics=sem, vmem_limit_bytes=VMEM_LIMIT)


def _dot(a, b, ca=1, cb=0, prec=None):
    return lax.dot_general(a, b, (((ca,), (cb,)), ((), ())), precision=prec, preferred_element_type=F32)


def _bdot(a, b, ca=1, cb=0):
    return _dot(a.astype(BF16), b.astype(BF16), ca, cb)


def _sigmoid(z):
    return 1.0 / (1.0 + jnp.exp(-z))


def _rowsum(a):
    return jnp.sum(a, axis=0, keepdims=True)


def _lanemean(a):
    return jnp.mean(a, axis=-1, keepdims=True)


def normmod_matmul(x, nw, sh, sc, w4, name):
    L = x.shape[0]
    tm = min(512, L)
    tn = IN_SH // 2

    def body(x_ref, nw_ref, sh_ref, sc_ref, w_ref, p_ref, hx_ref, hx_scr):
        @pl.when((pl.program_id(1) == 0) & (pl.program_id(2) == 0))
        def _():
            xv = x_ref[...]
            n = xv * lax.rsqrt(_lanemean(xv * xv) + EPS) * nw_ref[...]
            h = (n * (1.0 + sc_ref[...]) + sh_ref[...]).astype(BF16)
            hx_scr[...] = h
            hx_ref[...] = h

        p_ref[...] = _dot(hx_scr[...], w_ref[...])

    vec = pl.BlockSpec((1, D), lambda i, k, j: (0, 0))
    return pl.pallas_call(
        body, name=name,
        grid=(L // tm, N_SHARD, 2),
        in_specs=[pl.BlockSpec((tm, D), lambda i, k, j: (i, 0)), vec, vec, vec,
                  pl.BlockSpec((None, D, tn), lambda i, k, j: (k, 0, j))],
        out_specs=[pl.BlockSpec((tm, tn), lambda i, k, j: (i, 2 * k + j)),
                   pl.BlockSpec((tm, D), lambda i, k, j: (i, 0))],
        out_shape=[jax.ShapeDtypeStruct((L, D_IN), F32), jax.ShapeDtypeStruct((L, D), BF16)],
        scratch_shapes=[pltpu.VMEM((tm, D), BF16)],
        compiler_params=_params("parallel", "arbitrary", "arbitrary"),
    )(x, nw, sh, sc, w4)


def _hgrn_gates(z, lb):
    sg = _sigmoid(z)
    sgn = _sigmoid(-z)
    f = lb + (1.0 - lb) * sg
    k = (1.0 - lb) * sgn
    return sg, sgn, f, k


def _tri(n, reverse):
    r = lax.broadcasted_iota(jnp.int32, (n, n), 0)
    c = lax.broadcasted_iota(jnp.int32, (n, n), 1)
    return jnp.where((r <= c) if reverse else (r >= c), 1.0, 0.0).astype(F32)


def _decay3(b, reverse):
    C = b.shape[0]
    t = lax.broadcasted_iota(jnp.int32, (C, C, 1), 0)
    s = lax.broadcasted_iota(jnp.int32, (C, C, 1), 1)
    mask = (t <= s) if reverse else (t >= s)
    return jnp.exp(jnp.where(mask, b[:, None, :] - b[None, :, :], -jnp.inf))


def _hgrn_state_step(k, v, b, s_t, last):
    b_last = b[last:last + 1]
    return s_t * jnp.exp(b_last) + _bdot(v, k * jnp.exp(b_last - b), 0, 0)


def hgrn_scan_fwd(p, lb, s0, col_z, reverse, name):
    L = p.shape[0]
    nB = L // SCAN_ROWS
    nC = SCAN_ROWS // HG_CHUNK
    C = HG_CHUNK
    last = 0 if reverse else C - 1

    def bmap(b):
        return (nB - 1 - b) if reverse else b

    def body(q_ref, z_ref, v_ref, lb_ref, s0_ref, o_ref, sfin_ref, sblk_ref, s_scr):
        blk = pl.program_id(1)

        @pl.when(blk == 0)
        def _():
            s_scr[...] = s0_ref[...]

        sblk_ref[...] = s_scr[...]
        lb = lb_ref[...]
        tri = _tri(C, reverse)

        def chunk(ci, carry):
            c = (nC - 1 - ci) if reverse else ci
            rows = pl.ds(pl.multiple_of(c * C, C), C)
            q = q_ref[rows, :] * Q_SCALE
            v = v_ref[rows, :]
            _, _, f, k = _hgrn_gates(z_ref[rows, :], lb)
            b = _dot(tri, jnp.log(f), prec=HI)
            s_t = s_scr[...]
            e3 = _decay3(b, reverse)
            att3 = jnp.sum(q[:, None, :] * k[None, :, :] * e3, axis=-1, keepdims=True)
            o = jnp.sum(att3 * v[None, :, :], axis=1) + _bdot(q * jnp.exp(b), s_t, 1, 1)
            o_ref[rows, :] = o
            s_scr[...] = _hgrn_state_step(k, v, b, s_t, last)
            return carry

        lax.fori_loop(0, nC, chunk, 0)

        @pl.when(blk == nB - 1)
        def _():
            sfin_ref[...] = s_scr[...]

    def col(c0):
        return pl.BlockSpec((SCAN_ROWS, HG_D), lambda h, b: (bmap(b), c0 + h))

    return pl.pallas_call(
        body, name=name,
        grid=(HEADS, nB),
        in_specs=[col(COL_HQ), col(col_z), col(COL_HI),
                  pl.BlockSpec((1, HG_D), lambda h, b: (0, h)),
                  pl.BlockSpec((None, HG_D, HG_D), lambda h, b: (h, 0, 0))],
        out_specs=[pl.BlockSpec((SCAN_ROWS, HG_D), lambda h, b: (bmap(b), h)),
                   pl.BlockSpec((None, HG_D, HG_D), lambda h, b: (h, 0, 0)),
                   pl.BlockSpec((None, None, HG_D, HG_D), lambda h, b: (bmap(b), h, 0, 0))],
        out_shape=[jax.ShapeDtypeStruct((L, D), F32),
                   jax.ShapeDtypeStruct((HEADS, HG_D, HG_D), F32),
                   jax.ShapeDtypeStruct((nB, HEADS, HG_D, HG_D), F32)],
        scratch_shapes=[pltpu.VMEM((HG_D, HG_D), F32)],
        compiler_params=_params("parallel", "arbitrary"),
    )(p, p, p, lb, s0)


def hgrn_scan_bwd(p, lb, s_blocks, d_o, ds_fin, prev, col_z, reverse, name):
    L = p.shape[0]
    nB = L // SCAN_ROWS
    nC = SCAN_ROWS // HG_CHUNK
    C = HG_CHUNK
    last = 0 if reverse else C - 1
    has_prev = prev is not None
    out_dt = BF16 if has_prev else F32

    def bmap(b):
        return b if reverse else (nB - 1 - b)

    def body(*refs):
        q_ref, z_ref, v_ref, lb_ref, sblk_ref, do_ref, dsf_ref = refs[:7]
        refs = refs[7:]
        if has_prev:
            pq_ref, pv_ref = refs[:2]
            refs = refs[2:]
        dq_ref, dz_ref, dv_ref, dlb_ref, ds0_ref, st_scr, ds_scr = refs
        blk = pl.program_id(1)

        @pl.when(blk == 0)
        def _():
            ds_scr[...] = dsf_ref[...]
            dlb_ref[...] = jnp.zeros_like(dlb_ref)

        lb = lb_ref[...]
        tri = _tri(C, reverse)
        row = lax.broadcasted_iota(jnp.int32, (C, HG_D), 0)

        def load(c):
            rows = pl.ds(pl.multiple_of(c * C, C), C)
            z = z_ref[rows, :]
            sg, sgn, f, k = _hgrn_gates(z, lb)
            b = _dot(tri, jnp.log(f), prec=HI)
            return rows, sg, sgn, f, k, b

        def recompute(ci, s_t):
            c = (nC - 1 - ci) if reverse else ci
            rows, _, _, _, k, b = load(c)
            st_scr[c] = s_t
            return _hgrn_state_step(k, v_ref[rows, :], b, s_t, last)

        lax.fori_loop(0, nC, recompute, sblk_ref[...])

        def chunk(ci, carry):
            c = ci if reverse else (nC - 1 - ci)
            rows, sg, sgn, f, k, b = load(c)
            q = q_ref[rows, :] * Q_SCALE
            v = v_ref[rows, :]
            d_o = do_ref[rows, :]
            s_t = st_scr[c]
            ds_t = ds_scr[...]
            e3 = _decay3(b, reverse)
            eb = jnp.exp(b)
            b_last = b[last:last + 1]
            eb_last = jnp.exp(b_last)
            kdec = jnp.exp(b_last - b)
            qe = q * eb
            ke = k * kdec
            datt3 = jnp.sum(d_o[:, None, :] * v[None, :, :], axis=-1, keepdims=True)
            p3 = datt3 * e3
            dq_tot = _bdot(d_o, s_t, 1, 0) * eb + jnp.sum(p3 * k[None, :, :], axis=1)
            dke = _bdot(v, ds_t, 1, 0)
            dk_tot = dke * kdec + jnp.sum(p3 * q[:, None, :], axis=0)
            att3 = jnp.sum(q[:, None, :] * k[None, :, :] * e3, axis=-1, keepdims=True)
            dv = jnp.sum(att3 * d_o[:, None, :], axis=0) + _bdot(ke, ds_t, 1, 1)
            db_last = _rowsum(dke * ke) + eb_last * _rowsum(ds_t * s_t)
            db = q * dq_tot - k * dk_tot + jnp.where(row == last, db_last, 0.0)
            dlf = _dot(tri, db, 0, 0, prec=HI)
            g = dlf / f - dk_tot
            dz_ref[rows, :] = (g * (1.0 - lb) * sg * sgn).astype(BF16)
            dlb_ref[...] += _rowsum(g * sgn)
            dq = dq_tot * Q_SCALE
            if has_prev:
                dq = dq + pq_ref[rows, :]
                dv = dv + pv_ref[rows, :]
            dq_ref[rows, :] = dq.astype(out_dt)
            dv_ref[rows, :] = dv.astype(out_dt)
            ds_scr[...] = ds_t * eb_last + _bdot(d_o, qe, 0, 0)
            return carry

        lax.fori_loop(0, nC, chunk, 0)

        @pl.when(blk == nB - 1)
        def _():
            ds0_ref[...] = ds_scr[...]

    def col(c0):
        return pl.BlockSpec((SCAN_ROWS, HG_D), lambda h, b: (bmap(b), c0 + h))

    tile = pl.BlockSpec((SCAN_ROWS, HG_D), lambda h, b: (bmap(b), h))
    state = pl.BlockSpec((None, HG_D, HG_D), lambda h, b: (h, 0, 0))
    in_specs = [col(COL_HQ), col(col_z), col(COL_HI),
                pl.BlockSpec((1, HG_D), lambda h, b: (0, h)),
                pl.BlockSpec((None, None, HG_D, HG_D), lambda h, b: (bmap(b), h, 0, 0)),
                tile, state]
    args = [p, p, p, lb, s_blocks, d_o, ds_fin]
    if has_prev:
        in_specs += [tile, tile]
        args += list(prev)
    return pl.pallas_call(
        body, name=name,
        grid=(HEADS, nB),
        in_specs=in_specs,
        out_specs=[tile, tile, tile, pl.BlockSpec((1, HG_D), lambda h, b: (0, h)), state],
        out_shape=[jax.ShapeDtypeStruct((L, D), out_dt), jax.ShapeDtypeStruct((L, D), BF16),
                   jax.ShapeDtypeStruct((L, D), out_dt), jax.ShapeDtypeStruct((1, D), F32),
                   jax.ShapeDtypeStruct((HEADS, HG_D, HG_D), F32)],
        scratch_shapes=[pltpu.VMEM((nC, HG_D, HG_D), F32), pltpu.VMEM((HG_D, HG_D), F32)],
        compiler_params=_params("parallel", "arbitrary"),
    )(*args)


def _rope(t, cosf, sinf):
    return t * cosf + pltpu.roll(t, RT_DK // 2, 1) * sinf


def _rope_t(d, cosf, sinf):
    return d * cosf + pltpu.roll(d * sinf, RT_DK // 2, 1)


def _ret_decays(lg, reverse):
    C = SCAN_ROWS
    t = lax.broadcasted_iota(jnp.int32, (C, C), 0)
    s = lax.broadcasted_iota(jnp.int32, (C, C), 1)
    delta = ((s - t) if reverse else (t - s)).astype(F32)
    dmat = jnp.where(delta >= 0, jnp.exp(lg * jnp.maximum(delta, 0.0)), 0.0)
    r = lax.broadcasted_iota(jnp.int32, (C, RT_DK), 0)
    pos = ((C - 1 - r) if reverse else r).astype(F32)
    lg1 = lg[:, :RT_DK]
    qdec = jnp.exp(lg1 * (pos + 1.0))
    kdec = jnp.exp(lg1 * (C - 1.0 - pos))
    sdec = jnp.exp(lg1 * float(C))
    return dmat, delta, pos, qdec, kdec, sdec


def ret_scan_fwd(p, cosf, sinf, lg, s0, reverse, name):
    L = p.shape[0]
    C = SCAN_ROWS
    nB = L // C

    def bmap(b):
        return (nB - 1 - b) if reverse else b

    def body(q_ref, k_ref, v_ref, cos_ref, sin_ref, lg_ref, s0_ref, o_ref, sfin_ref, sblk_ref, s_scr):
        blk = pl.program_id(1)

        @pl.when(blk == 0)
        def _():
            s_scr[...] = s0_ref[...]

        s_t = s_scr[...]
        sblk_ref[...] = s_t
        cosf, sinf = cos_ref[...], sin_ref[...]
        dmat, _, _, qdec, kdec, sdec = _ret_decays(lg_ref[...], reverse)
        q = _rope(q_ref[...] * Q_SCALE, cosf, sinf)
        k = _rope(k_ref[...], cosf, sinf)
        v = v_ref[...]
        att = _bdot(q, k, 1, 1) * dmat
        o_ref[...] = _bdot(att, v) + _bdot(q * qdec, s_t, 1, 1)
        s_new = s_t * sdec + _bdot(v, k * kdec, 0, 0)
        s_scr[...] = s_new

        @pl.when(blk == nB - 1)
        def _():
            sfin_ref[...] = s_new

    def col(c0):
        return pl.BlockSpec((C, RT_DK), lambda h, b: (bmap(b), c0 + h))

    tab = pl.BlockSpec((C, RT_DK), lambda h, b: (bmap(b), 0))
    state = pl.BlockSpec((None, RT_DV, RT_DK), lambda h, b: (h, 0, 0))
    return pl.pallas_call(
        body, name=name,
        grid=(HEADS, nB),
        in_specs=[col(COL_RQ), col(COL_RK),
                  pl.BlockSpec((C, RT_DV), lambda h, b: (bmap(b), COL_RV // 2 + h)),
                  tab, tab, pl.BlockSpec((None, 1, RT_DV), lambda h, b: (h, 0, 0)), state],
        out_specs=[pl.BlockSpec((C, RT_DV), lambda h, b: (bmap(b), h)), state,
                   pl.BlockSpec((None, None, RT_DV, RT_DK), lambda h, b: (bmap(b), h, 0, 0))],
        out_shape=[jax.ShapeDtypeStruct((L, HEADS * RT_DV), F32),
                   jax.ShapeDtypeStruct((HEADS, RT_DV, RT_DK), F32),
                   jax.ShapeDtypeStruct((nB, HEADS, RT_DV, RT_DK), F32)],
        scratch_shapes=[pltpu.VMEM((RT_DV, RT_DK), F32)],
        compiler_params=_params("parallel", "arbitrary"),
    )(p, p, p, cosf, sinf, lg, s0)


def ret_scan_bwd(p, cosf, sinf, lg, s_blocks, d_o, ds_fin, prev, reverse, name):
    L = p.shape[0]
    C = SCAN_ROWS
    nB = L // C
    has_prev = prev is not None
    out_dt = BF16 if has_prev else F32

    def bmap(b):
        return b if reverse else (nB - 1 - b)

    def body(*refs):
        q_ref, k_ref, v_ref, cos_ref, sin_ref, lg_ref, sblk_ref, do_ref, dsf_ref = refs[:9]
        refs = refs[9:]
        if has_prev:
            pq_ref, pk_ref, pv_ref = refs[:3]
            refs = refs[3:]
        dq_ref, dk_ref, dv_ref, dlg_ref, ds0_ref, ds_scr = refs
        blk = pl.program_id(1)

        @pl.when(blk == 0)
        def _():
            ds_scr[...] = dsf_ref[...]
            dlg_ref[...] = jnp.zeros_like(dlg_ref)

        s_t = sblk_ref[...]
        ds_t = ds_scr[...]
        cosf, sinf = cos_ref[...], sin_ref[...]
        dmat, delta, pos, qdec, kdec, sdec = _ret_decays(lg_ref[...], reverse)
        q = _rope(q_ref[...] * Q_SCALE, cosf, sinf)
        k = _rope(k_ref[...], cosf, sinf)
        v = v_ref[...]
        d_o = do_ref[...]
        att_raw = _bdot(q, k, 1, 1)
        datt_m = _bdot(d_o, v, 1, 1) * dmat
        dqd = _bdot(d_o, s_t, 1, 0)
        dkd = _bdot(v, ds_t, 1, 0)
        dq = _bdot(datt_m, k) + dqd * qdec
        dk = _bdot(datt_m, q, 0, 0) + dkd * kdec
        dv = _bdot(att_raw * dmat, d_o, 0, 0) + _bdot(k * kdec, ds_t, 1, 1)
        ds_new = ds_t * sdec + _bdot(d_o, q * qdec, 0, 0)
        ds_scr[...] = ds_new
        t1 = jnp.sum(_rowsum(datt_m * att_raw * delta), axis=-1, keepdims=True)
        t23 = jnp.sum(_rowsum((pos + 1.0) * qdec * q * dqd + (C - 1.0 - pos) * kdec * k * dkd), axis=-1, keepdims=True)
        t4 = jnp.sum(_rowsum(ds_t * s_t * sdec), axis=-1, keepdims=True) * float(C)
        dlg_ref[...] += jnp.broadcast_to(t1 + t23 + t4, (1, RT_DK))
        if has_prev:
            dq = _rope_t(dq + pq_ref[...], cosf, sinf) * Q_SCALE
            dk = _rope_t(dk + pk_ref[...], cosf, sinf)
            dv = dv + pv_ref[...]
        dq_ref[...] = dq.astype(out_dt)
        dk_ref[...] = dk.astype(out_dt)
        dv_ref[...] = dv.astype(out_dt)

        @pl.when(blk == nB - 1)
        def _():
            ds0_ref[...] = ds_new

    def col(c0):
        return pl.BlockSpec((C, RT_DK), lambda h, b: (bmap(b), c0 + h))

    tab = pl.BlockSpec((C, RT_DK), lambda h, b: (bmap(b), 0))
    state = pl.BlockSpec((None, RT_DV, RT_DK), lambda h, b: (h, 0, 0))
    tk = pl.BlockSpec((C, RT_DK), lambda h, b: (bmap(b), h))
    tv = pl.BlockSpec((C, RT_DV), lambda h, b: (bmap(b), h))
    in_specs = [col(COL_RQ), col(COL_RK),
                pl.BlockSpec((C, RT_DV), lambda h, b: (bmap(b), COL_RV // 2 + h)),
                tab, tab, pl.BlockSpec((None, 1, RT_DV), lambda h, b: (h, 0, 0)),
                pl.BlockSpec((None, None, RT_DV, RT_DK), lambda h, b: (bmap(b), h, 0, 0)),
                tv, state]
    args = [p, p, p, cosf, sinf, lg, s_blocks, d_o, ds_fin]
    if has_prev:
        in_specs += [tk, tk, tv]
        args += list(prev)
    return pl.pallas_call(
        body, name=name,
        grid=(HEADS, nB),
        in_specs=in_specs,
        out_specs=[tk, tk, tv, pl.BlockSpec((None, 1, RT_DK), lambda h, b: (h, 0, 0)), state],
        out_shape=[jax.ShapeDtypeStruct((L, D), out_dt), jax.ShapeDtypeStruct((L, D), out_dt),
                   jax.ShapeDtypeStruct((L, HEADS * RT_DV), out_dt),
                   jax.ShapeDtypeStruct((HEADS, 1, RT_DK), F32),
                   jax.ShapeDtypeStruct((HEADS, RT_DV, RT_DK), F32)],
        scratch_shapes=[pltpu.VMEM((RT_DV, RT_DK), F32)],
        compiler_params=_params("parallel", "arbitrary"),
    )(*args)


def _silu_parts(h):
    s = _sigmoid(h)
    return h * s, s * (1.0 + h * (1.0 - s))


def _head_rms(o):
    outs, rs = [], []
    for h in range(HEADS):
        oh = o[:, h * HG_D:(h + 1) * HG_D]
        r = lax.rsqrt(_lanemean(oh * oh) + EPS)
        outs.append(oh * r)
        rs.append(r)
    return outs, rs


def _group_norm(o):
    outs, rs = [], []
    for h in range(HEADS):
        oh = o[:, h * RT_DV:(h + 1) * RT_DV]
        c = oh - _lanemean(oh)
        r = lax.rsqrt(_lanemean(c * c) + GN_EPS)
        outs.append(c * r)
        rs.append(r)
    return outs, rs


MIX_ROWS = 128


def _mix_specs(L):
    def t(w, c=0):
        return pl.BlockSpec((MIX_ROWS, w), lambda i: (i, c))

    return t


def mix_fwd(ohf, ohb, orf, orb, p, x, g1, hgw, w_pa, w_pb, w_out, name):
    L = x.shape[0]
    t = _mix_specs(L)

    def body(ohf_ref, ohb_ref, orf_ref, orb_ref, hg_ref, rg0_ref, rg1_ref, ga_ref, gb_ref, x_ref, g1_ref, hgw_ref,
             wpa_ref, wpb_ref, wout_ref, x1_ref, xmix_ref, merged_ref, ya_ref, yb_ref):
        nh, _ = _head_rms(ohf_ref[...] + ohb_ref[...])
        ya = jnp.concatenate(nh, axis=1) * hgw_ref[...] * _silu_parts(hg_ref[...])[0]
        gn, _ = _group_norm(orf_ref[...] + orb_ref[...])
        rg = jnp.concatenate([rg0_ref[...], rg1_ref[...]], axis=1)
        yb = jnp.concatenate(gn, axis=1) * _silu_parts(rg)[0]
        ya16, yb16 = ya.astype(BF16), yb.astype(BF16)
        merged = (_sigmoid(ga_ref[...]) * _dot(ya16, wpa_ref[...])
                  + _sigmoid(gb_ref[...]) * _dot(yb16, wpb_ref[...])).astype(BF16)
        x_mix = _dot(merged, wout_ref[...])
        x1_ref[...] = x_ref[...] + g1_ref[...] * x_mix
        xmix_ref[...] = x_mix
        merged_ref[...] = merged
        ya_ref[...] = ya16
        yb_ref[...] = yb16

    vec = pl.BlockSpec((1, D), lambda i: (0, 0))

    def full(a):
        return pl.BlockSpec(a.shape, lambda i: (0, 0))

    return pl.pallas_call(
        body, name=name,
        grid=(L // MIX_ROWS,),
        in_specs=[t(D), t(D), t(2 * D), t(2 * D), t(D, COL_HG // 8), t(D, COL_RG // 8), t(D, COL_RG // 8 + 1),
                  t(D, COL_GA // 8), t(D, COL_GB // 8), t(D), vec, vec, full(w_pa), full(w_pb), full(w_out)],
        out_specs=[t(D), t(D), t(D), t(D), t(2 * D)],
        out_shape=[jax.ShapeDtypeStruct((L, D), F32), jax.ShapeDtypeStruct((L, D), F32),
                   jax.ShapeDtypeStruct((L, D), BF16), jax.ShapeDtypeStruct((L, D), BF16),
                   jax.ShapeDtypeStruct((L, 2 * D), BF16)],
        compiler_params=_params("parallel"),
    )(ohf, ohb, orf, orb, p, p, p, p, p, x, g1, hgw, w_pa, w_pb, w_out)


def mix_bwd(dx1, x_mix, ya, yb, ohf, ohb, orf, orb, p, g1, hgw, w_pa, w_pb, w_out, name):
    L = dx1.shape[0]
    t = _mix_specs(L)

    def body(dx1_ref, xmix_ref, ya_ref, yb_ref, ohf_ref, ohb_ref, orf_ref, orb_ref, hg_ref, rg0_ref, rg1_ref,
             ga_ref, gb_ref, g1_ref, hgw_ref, wpa_ref, wpb_ref, wout_ref,
             dxm_ref, da_ref, db_ref, dga_ref, dgb_ref, dhg_ref, drg_ref, dohg_ref, dort_ref, sums_ref):
        @pl.when(pl.program_id(0) == 0)
        def _():
            sums_ref[...] = jnp.zeros_like(sums_ref)

        dx1 = dx1_ref[...]
        dxm = (g1_ref[...] * dx1).astype(BF16)
        dxm_ref[...] = dxm
        dmerged = _dot(dxm, wout_ref[...], 1, 1)
        a = _dot(ya_ref[...], wpa_ref[...])
        bm = _dot(yb_ref[...], wpb_ref[...])
        sa, sb = _sigmoid(ga_ref[...]), _sigmoid(gb_ref[...])
        d_a = (dmerged * sa).astype(BF16)
        d_b = (dmerged * sb).astype(BF16)
        da_ref[...] = d_a
        db_ref[...] = d_b
        dga_ref[...] = (dmerged * a * sa * (1.0 - sa)).astype(BF16)
        dgb_ref[...] = (dmerged * bm * sb * (1.0 - sb)).astype(BF16)
        dya = _dot(d_a, wpa_ref[...], 1, 1)
        dyb = _dot(d_b, wpb_ref[...], 1, 1)

        hgw = hgw_ref[...]
        silu_h, dsilu_h = _silu_parts(hg_ref[...])
        nh, rh = _head_rms(ohf_ref[...] + ohb_ref[...])
        n = jnp.concatenate(nh, axis=1)
        dhg_ref[...] = (dya * n * hgw * dsilu_h).astype(BF16)
        dn = dya * hgw * silu_h
        douts = []
        for h in range(HEADS):
            dnh = dn[:, h * HG_D:(h + 1) * HG_D]
            douts.append(rh[h] * (dnh - nh[h] * _lanemean(dnh * nh[h])))
        dohg_ref[...] = jnp.concatenate(douts, axis=1)

        rg = jnp.concatenate([rg0_ref[...], rg1_ref[...]], axis=1)
        silu_r, dsilu_r = _silu_parts(rg)
        gn, rr = _group_norm(orf_ref[...] + orb_ref[...])
        g = jnp.concatenate(gn, axis=1)
        drg_ref[...] = (dyb * g * dsilu_r).astype(BF16)
        dgn = dyb * silu_r
        douts = []
        for h in range(HEADS):
            dgh = dgn[:, h * RT_DV:(h + 1) * RT_DV]
            douts.append(rr[h] * (dgh - _lanemean(dgh) - gn[h] * _lanemean(dgh * gn[h])))
        dort_ref[...] = jnp.concatenate(douts, axis=1)

        sums_ref[0:1, :] += _rowsum(dx1 * xmix_ref[...])
        sums_ref[1:2, :] += _rowsum(dya * n * silu_h)

    vec = pl.BlockSpec((1, D), lambda i: (0, 0))

    def full(a):
        return pl.BlockSpec(a.shape, lambda i: (0, 0))

    bf = functools.partial(jax.ShapeDtypeStruct, dtype=BF16)
    return pl.pallas_call(
        body, name=name,
        grid=(L // MIX_ROWS,),
        in_specs=[t(D), t(D), t(D), t(2 * D), t(D), t(D), t(2 * D), t(2 * D),
                  t(D, COL_HG // 8), t(D, COL_RG // 8), t(D, COL_RG // 8 + 1), t(D, COL_GA // 8), t(D, COL_GB // 8),
                  vec, vec, full(w_pa), full(w_pb), full(w_out)],
        out_specs=[t(D), t(D), t(D), t(D), t(D), t(D), t(2 * D), t(D), t(2 * D),
                   pl.BlockSpec((8, D), lambda i: (0, 0))],
        out_shape=[bf((L, D)), bf((L, D)), bf((L, D)), bf((L, D)), bf((L, D)), bf((L, D)), bf((L, 2 * D)),
                   jax.ShapeDtypeStruct((L, D), F32), jax.ShapeDtypeStruct((L, 2 * D), F32),
                   jax.ShapeDtypeStruct((8, D), F32)],
        compiler_params=_params("arbitrary"),
    )(dx1, x_mix, ya, yb, ohf, ohb, orf, orb, p, p, p, p, p, g1, hgw, w_pa, w_pb, w_out)


FFN_ROWS = 256


def ffn_fwd(x1, target, nw2, sh2, sc2, g2, fw, wg, wu, wd, name):
    L = x1.shape[0]
    tm = FFN_ROWS

    def body(x1_ref, tgt_ref, nw2_ref, sh2_ref, sc2_ref, g2_ref, fw_ref, wg_ref, wu_ref, wd_ref,
             hx2_ref, g_ref, u_ref, h_ref, f_ref, dx2_ref, sums_ref, hx_scr, acc):
        i, j = pl.program_id(0), pl.program_id(1)

        @pl.when((i == 0) & (j == 0))
        def _():
            sums_ref[...] = jnp.zeros_like(sums_ref)

        @pl.when(j == 0)
        def _():
            xv = x1_ref[...]
            n = xv * lax.rsqrt(_lanemean(xv * xv) + EPS) * nw2_ref[...]
            h = (n * (1.0 + sc2_ref[...]) + sh2_ref[...]).astype(BF16)
            hx_scr[...] = h
            hx2_ref[...] = h
            acc[...] = jnp.zeros_like(acc)

        hx = hx_scr[...]
        g = _dot(hx, wg_ref[...])
        u = _dot(hx, wu_ref[...])
        hh = (_silu_parts(g)[0] * u).astype(BF16)
        g_ref[...] = g
        u_ref[...] = u
        h_ref[...] = hh
        acc[...] += _dot(hh, wd_ref[...])

        @pl.when(j == N_SHARD - 1)
        def _():
            f = acc[...]
            f_ref[...] = f
            x2 = x1_ref[...] + g2_ref[...] * f
            r = lax.rsqrt(_lanemean(x2 * x2) + EPS)
            fw = fw_ref[...]
            e = x2 * r * fw - tgt_ref[...]
            dy = e * (1.0 / D)
            dyw = dy * fw
            dx2_ref[...] = r * dyw - x2 * (r * r * r) * _lanemean(dyw * x2)
            sums_ref[0:1, :] += _rowsum(dy * x2 * r)
            sums_ref[1:2, :] += _rowsum(e * e) * (0.5 / D)

    row = pl.BlockSpec((tm, D), lambda i, j: (i, 0))
    vec = pl.BlockSpec((1, D), lambda i, j: (0, 0))
    sh = pl.BlockSpec((None, tm, FF_SH), lambda i, j: (j, i, 0))
    return pl.pallas_call(
        body, name=name,
        grid=(L // tm, N_SHARD),
        in_specs=[row, row, vec, vec, vec, vec, vec,
                  pl.BlockSpec((None, D, FF_SH), lambda i, j: (j, 0, 0)),
                  pl.BlockSpec((None, D, FF_SH), lambda i, j: (j, 0, 0)),
                  pl.BlockSpec((None, FF_SH, D), lambda i, j: (j, 0, 0))],
        out_specs=[row, sh, sh, sh, row, row, pl.BlockSpec((8, D), lambda i, j: (0, 0))],
        out_shape=[jax.ShapeDtypeStruct((L, D), BF16),
                   jax.ShapeDtypeStruct((N_SHARD, L, FF_SH), F32), jax.ShapeDtypeStruct((N_SHARD, L, FF_SH), F32),
                   jax.ShapeDtypeStruct((N_SHARD, L, FF_SH), BF16),
                   jax.ShapeDtypeStruct((L, D), F32), jax.ShapeDtypeStruct((L, D), F32),
                   jax.ShapeDtypeStruct((8, D), F32)],
        scratch_shapes=[pltpu.VMEM((tm, D), BF16), pltpu.VMEM((tm, D), F32)],
        compiler_params=_params("arbitrary", "arbitrary"),
    )(x1, target, nw2, sh2, sc2, g2, fw, wg, wu, wd)


def ffn_bwd(dx2, x1, f, g, u, nw2, sc2, g2, wg, wu, wd, name):
    L = x1.shape[0]
    tm = FFN_ROWS

    def body(dx2_ref, x1_ref, f_ref, g_ref, u_ref, nw2_ref, sc2_ref, g2_ref, wg_ref, wu_ref, wd_ref,
             df_ref, dg_ref, du_ref, dx1_ref, sums_ref, df_scr, acc):
        i, j = pl.program_id(0), pl.program_id(1)

        @pl.when((i == 0) & (j == 0))
        def _():
            sums_ref[...] = jnp.zeros_like(sums_ref)

        @pl.when(j == 0)
        def _():
            dx2 = dx2_ref[...]
            df = (g2_ref[...] * dx2).astype(BF16)
            df_scr[...] = df
            df_ref[...] = df
            sums_ref[0:1, :] += _rowsum(dx2 * f_ref[...])
            acc[...] = jnp.zeros_like(acc)

        dh = _dot(df_scr[...], wd_ref[...], 1, 1)
        gv, uv = g_ref[...], u_ref[...]
        silu_g, dsilu_g = _silu_parts(gv)
        dg = (dh * uv * dsilu_g).astype(BF16)
        du = (dh * silu_g).astype(BF16)
        dg_ref[...] = dg
        du_ref[...] = du
        acc[...] += _dot(dg, wg_ref[...], 1, 1) + _dot(du, wu_ref[...], 1, 1)

        @pl.when(j == N_SHARD - 1)
        def _():
            dhx = acc[...]
            xv = x1_ref[...]
            r = lax.rsqrt(_lanemean(xv * xv) + EPS)
            n0 = xv * r
            nw = nw2_ref[...]
            dn2 = dhx * (1.0 + sc2_ref[...])
            dn0 = dn2 * nw
            dx1_ref[...] = dx2_ref[...] + r * (dn0 - n0 * _lanemean(dn0 * n0))
            sums_ref[1:2, :] += _rowsum(dhx)
            sums_ref[2:3, :] += _rowsum(dhx * n0 * nw)
            sums_ref[3:4, :] += _rowsum(dn2 * n0)

    row = pl.BlockSpec((tm, D), lambda i, j: (i, 0))
    vec = pl.BlockSpec((1, D), lambda i, j: (0, 0))
    sh = pl.BlockSpec((None, tm, FF_SH), lambda i, j: (j, i, 0))
    return pl.pallas_call(
        body, name=name,
        grid=(L // tm, N_SHARD),
        in_specs=[row, row, row, sh, sh, vec, vec, vec,
                  pl.BlockSpec((None, D, FF_SH), lambda i, j: (j, 0, 0)),
                  pl.BlockSpec((None, D, FF_SH), lambda i, j: (j, 0, 0)),
                  pl.BlockSpec((None, FF_SH, D), lambda i, j: (j, 0, 0))],
        out_specs=[row, sh, sh, row, pl.BlockSpec((8, D), lambda i, j: (0, 0))],
        out_shape=[jax.ShapeDtypeStruct((L, D), BF16),
                   jax.ShapeDtypeStruct((N_SHARD, L, FF_SH), BF16), jax.ShapeDtypeStruct((N_SHARD, L, FF_SH), BF16),
                   jax.ShapeDtypeStruct((L, D), F32), jax.ShapeDtypeStruct((8, D), F32)],
        scratch_shapes=[pltpu.VMEM((tm, D), BF16), pltpu.VMEM((tm, D), F32)],
        compiler_params=_params("arbitrary", "arbitrary"),
    )(dx2, x1, f, g, u, nw2, sc2, g2, wg, wu, wd)


def matmul_tn(a, b, name, acc_init=None):
    na, K, M = a.shape
    nb, _, N = b.shape
    n = max(na, nb)
    tk = min(512, K)
    tn = N if N <= 1024 else N // 2
    nk = K // tk
    has_init = acc_init is not None

    def body(*refs):
        if has_init:
            a_ref, b_ref, init_ref, o_ref = refs
        else:
            a_ref, b_ref, o_ref = refs
        kk = pl.program_id(2)

        @pl.when(kk == 0)
        def _():
            o_ref[...] = init_ref[...] if has_init else jnp.zeros_like(o_ref)

        o_ref[...] += _dot(a_ref[...], b_ref[...], 0, 0)

    out_spec = pl.BlockSpec((None, M, tn), lambda s, j, kk: (s, 0, j))
    in_specs = [pl.BlockSpec((None, tk, M), lambda s, j, kk: (s if na > 1 else 0, kk, 0)),
                pl.BlockSpec((None, tk, tn), lambda s, j, kk: (s if nb > 1 else 0, kk, j))]
    args = [a, b]
    if has_init:
        in_specs.append(out_spec)
        args.append(acc_init)
    return pl.pallas_call(
        body, name=name,
        grid=(n, N // tn, nk),
        in_specs=in_specs,
        out_specs=out_spec,
        out_shape=jax.ShapeDtypeStruct((n, M, N), F32),
        compiler_params=_params("parallel", "parallel", "arbitrary"),
    )(*args)


def dhx_normbwd(dp4, w4, x, dx_res, nw, sc, name):
    L = x.shape[0]
    tm = min(512, L)
    tn = IN_SH // 2

    def body(dp_ref, w_ref, x_ref, res_ref, nw_ref, sc_ref, dx_ref, sums_ref, acc):
        i, k, j = pl.program_id(0), pl.program_id(1), pl.program_id(2)
        first = (k == 0) & (j == 0)

        @pl.when((i == 0) & first)
        def _():
            sums_ref[...] = jnp.zeros_like(sums_ref)

        @pl.when(first)
        def _():
            acc[...] = jnp.zeros_like(acc)

        acc[...] += _dot(dp_ref[...], w_ref[...], 1, 1)

        @pl.when((k == N_SHARD - 1) & (j == 1))
        def _():
            dhx = acc[...]
            xv = x_ref[...]
            r = lax.rsqrt(_lanemean(xv * xv) + EPS)
            n0 = xv * r
            nw = nw_ref[...]
            dn = dhx * (1.0 + sc_ref[...])
            dn0 = dn * nw
            dx_ref[...] = res_ref[...] + r * (dn0 - n0 * _lanemean(dn0 * n0))
            sums_ref[0:1, :] += _rowsum(dhx)
            sums_ref[1:2, :] += _rowsum(dhx * n0 * nw)
            sums_ref[2:3, :] += _rowsum(dn * n0)

    row = pl.BlockSpec((tm, D), lambda i, k, j: (i, 0))
    vec = pl.BlockSpec((1, D), lambda i, k, j: (0, 0))
    return pl.pallas_call(
        body, name=name,
        grid=(L // tm, N_SHARD, 2),
        in_specs=[pl.BlockSpec((None, tm, tn), lambda i, k, j: (k, i, j)),
                  pl.BlockSpec((None, D, tn), lambda i, k, j: (k, 0, j)),
                  row, row, vec, vec],
        out_specs=[row, pl.BlockSpec((8, D), lambda i, k, j: (0, 0))],
        out_shape=[jax.ShapeDtypeStruct((L, D), F32), jax.ShapeDtypeStruct((8, D), F32)],
        scratch_shapes=[pltpu.VMEM((tm, D), F32)],
        compiler_params=_params("arbitrary", "arbitrary", "arbitrary"),
    )(dp4, w4, x, dx_res, nw, sc)


SMALL_ROWS = 24


def _rope_tables(L):
    rows = L // 64
    row = jnp.repeat(jnp.arange(rows, dtype=F32), 64)
    col = jnp.tile(jnp.arange(64, dtype=F32), rows)
    freqs = 10000.0 ** (-jnp.arange(RT_DK // 4, dtype=F32) / (RT_DK // 4))
    ang = jnp.concatenate([row[:, None] * freqs, col[:, None] * freqs], axis=-1)
    cos, sin = jnp.cos(ang), jnp.sin(ang)
    return jnp.concatenate([cos, cos], axis=1), jnp.concatenate([-sin, sin], axis=1)


def _shard_major(pieces):
    dp = jnp.concatenate(pieces, axis=1)
    return dp.reshape(dp.shape[0], N_SHARD, IN_SH).transpose(1, 0, 2)


def _lane0(a):
    return a[:, 0, 0]


def _pack_small(rows):
    out = [r.reshape(1, D) for r in rows]
    out += [jnp.zeros((1, D), F32)] * (SMALL_ROWS - len(out))
    return jnp.concatenate(out, axis=0)


def local_step(x, ctx, target, mod_x, mod_c, lb_f, lb_b, lg_f, lg_b, nw1, nw2, hgw, fw, w):
    L, Lc = x.shape[0], ctx.shape[0]
    sh1, sc1, g1, sh2, sc2, g2 = (mod_x[i:i + 1] for i in range(6))
    sh1c, sc1c = mod_c[0:1], mod_c[1:2]
    cosf, sinf = _rope_tables(L)
    cosc, sinc = jnp.ones((Lc, RT_DK), F32), jnp.zeros((Lc, RT_DK), F32)
    zero_h = jnp.zeros((HEADS, HG_D, HG_D), F32)
    zero_r = jnp.zeros((HEADS, RT_DV, RT_DK), F32)

    pc, hxc = normmod_matmul(ctx, nw1, sh1c, sc1c, w["w_in"], "ctx_in_proj")
    _, s_hf, cb_hf = hgrn_scan_fwd(pc, lb_f, zero_h, COL_HFF, False, "ctx_hgrn_f")
    _, s_hb, cb_hb = hgrn_scan_fwd(pc, lb_b, zero_h, COL_HFB, True, "ctx_hgrn_b")
    _, s_rf, cb_rf = ret_scan_fwd(pc, cosc, sinc, lg_f, zero_r, False, "ctx_ret_f")
    _, s_rb, cb_rb = ret_scan_fwd(pc, cosc, sinc, lg_b, zero_r, True, "ctx_ret_b")
    p, hx = normmod_matmul(x, nw1, sh1, sc1, w["w_in"], "in_proj")
    ohf, _, xb_hf = hgrn_scan_fwd(p, lb_f, s_hf, COL_HFF, False, "hgrn_f")
    ohb, _, xb_hb = hgrn_scan_fwd(p, lb_b, s_hb, COL_HFB, True, "hgrn_b")
    orf, _, xb_rf = ret_scan_fwd(p, cosf, sinf, lg_f, s_rf, False, "ret_f")
    orb, _, xb_rb = ret_scan_fwd(p, cosf, sinf, lg_b, s_rb, True, "ret_b")
    x1, x_mix, merged, ya, yb = mix_fwd(ohf, ohb, orf, orb, p, x, g1, hgw, w["w_pa"], w["w_pb"], w["w_out"], "mix_fwd")
    hx2, gg, uu, hh, ff, dx2, sums_f = ffn_fwd(x1, target, nw2, sh2, sc2, g2, fw, w["wg"], w["wu"], w["wd"], "ffn_fwd")

    d_f, d_g, d_u, dx1, sums_fb = ffn_bwd(dx2, x1, ff, gg, uu, nw2, sc2, g2, w["wg"], w["wu"], w["wd"], "ffn_bwd")
    grads = {
        "wg": matmul_tn(hx2[None], d_g, "dw_ffn_gate"),
        "wu": matmul_tn(hx2[None], d_u, "dw_ffn_up"),
        "wd": matmul_tn(hh, d_f[None], "dw_ffn_down"),
    }
    dxm, d_a, d_b, dga, dgb, dhg, drg, dohg, dort, sums_m = mix_bwd(
        dx1, x_mix, ya, yb, ohf, ohb, orf, orb, p, g1, hgw, w["w_pa"], w["w_pb"], w["w_out"], "mix_bwd")
    grads["w_out"] = matmul_tn(merged[None], dxm[None], "dw_out").reshape(N_SHARD, D // N_SHARD, D)
    grads["w_pa"] = matmul_tn(ya[None], d_a[None], "dw_proj_hgrn").reshape(N_SHARD, D // N_SHARD, D)
    grads["w_pb"] = matmul_tn(yb[None], d_b[None], "dw_proj_ret").reshape(N_SHARD, 2 * D // N_SHARD, D)

    rq1, rk1, rv1, dlgf_x, ds_rf = ret_scan_bwd(p, cosf, sinf, lg_f, xb_rf, dort, zero_r, None, False, "ret_f_bwd")
    drq, drk, drv, dlgb_x, ds_rb = ret_scan_bwd(p, cosf, sinf, lg_b, xb_rb, dort, zero_r, (rq1, rk1, rv1), True, "ret_b_bwd")
    hq1, dzf, hv1, dlbf_x, ds_hf = hgrn_scan_bwd(p, lb_f, xb_hf, dohg, zero_h, None, COL_HFF, False, "hgrn_f_bwd")
    dhq, dzb, dhv, dlbb_x, ds_hb = hgrn_scan_bwd(p, lb_b, xb_hb, dohg, zero_h, (hq1, hv1), COL_HFB, True, "hgrn_b_bwd")
    dp4 = _shard_major([dhq, dzf, dzb, dhv, dhg, drq, drk, drv, drg, dga, dgb])
    dx, sums_x = dhx_normbwd(dp4, w["w_in"], x, dx1, nw1, sc1, "dx_in_proj")
    dw_in = matmul_tn(hx[None], dp4, "dw_in")

    zc = jnp.zeros((Lc, D), F32)
    zc2 = jnp.zeros((Lc, 2 * D), F32)
    crq1, crk1, crv1, dlgf_c, _ = ret_scan_bwd(pc, cosc, sinc, lg_f, cb_rf, zc2, ds_rf, None, False, "ctx_ret_f_bwd")
    cdrq, cdrk, cdrv, dlgb_c, _ = ret_scan_bwd(pc, cosc, sinc, lg_b, cb_rb, zc2, ds_rb, (crq1, crk1, crv1), True, "ctx_ret_b_bwd")
    chq1, cdzf, chv1, dlbf_c, _ = hgrn_scan_bwd(pc, lb_f, cb_hf, zc, ds_hf, None, COL_HFF, False, "ctx_hgrn_f_bwd")
    cdhq, cdzb, cdhv, dlbb_c, _ = hgrn_scan_bwd(pc, lb_b, cb_hb, zc, ds_hb, (chq1, chv1), COL_HFB, True, "ctx_hgrn_b_bwd")
    zb = jnp.zeros((Lc, D), BF16)
    zb2 = jnp.zeros((Lc, 2 * D), BF16)
    dpc4 = _shard_major([cdhq, cdzf, cdzb, cdhv, zb, cdrq, cdrk, cdrv, zb2, zb, zb])
    _, sums_c = dhx_normbwd(dpc4, w["w_in"], ctx, zc, nw1, sc1c, "dctx_in_proj")
    grads["w_in"] = matmul_tn(hxc[None], dpc4, "dw_in_ctx", acc_init=dw_in)

    def lg_row(f, b):
        return jnp.concatenate([_lane0(f), _lane0(b), jnp.zeros((D - 2 * HEADS,), F32)])

    small = _pack_small([
        sums_x[0], sums_x[1], sums_m[0], sums_fb[1], sums_fb[2], sums_fb[0],
        sums_c[0], sums_c[1],
        sums_x[2], sums_c[2], sums_fb[3], sums_m[1], sums_f[0],
        dlbf_x, dlbf_c, dlbb_x, dlbb_c,
        lg_row(dlgf_x, dlgb_x), lg_row(dlgf_c, dlgb_c),
        sums_f[1],
    ])
    return dx, grads, small


MESH = pl.DeviceIdType.MESH
ANY = pl.BlockSpec(memory_space=pl.ANY)
N_DEV = 8


def _place():
    return lax.axis_index("x"), lax.axis_index("y"), lax.axis_index("c")


def _other_chips(x, y):
    return [(1 - x, y), (x, 1 - y), (1 - x, 1 - y)]


def allgather8(xs, name):
    m, n = xs.shape

    def body(x_ref, out_ref, send_sems, recv_sems, local_sem):
        x, y, c = _place()
        me, sibling = (x, y, c), (x, y, 1 - c)
        chips = _other_chips(x, y)

        def rows(px, py, pc):
            return out_ref.at[pl.ds((4 * px + 2 * py + pc) * m, m), :]

        def copy(k, block, to, src=None):
            return pltpu.make_async_remote_copy(
                src_ref=rows(*block) if src is None else src, dst_ref=rows(*block),
                send_sem=send_sems.at[k], recv_sem=recv_sems.at[k], device_id=to, device_id_type=MESH)

        mine = pltpu.make_async_copy(x_ref, rows(*me), local_sem)
        mine.start()
        first = [copy(0, me, sibling, src=x_ref)]
        first += [copy(1 + j, me, (*chip, c), src=x_ref) for j, chip in enumerate(chips)]
        for cp in first:
            cp.start()
        passed = [copy(4 + j, (*chip, c), sibling) for j, chip in enumerate(chips)]
        for j, chip in enumerate(chips):
            copy(1 + j, (*chip, c), me).wait_recv()
            passed[j].start()
        copy(0, sibling, me).wait_recv()
        for j, chip in enumerate(chips):
            copy(4 + j, (*chip, 1 - c), me).wait_recv()
        for cp in first + passed:
            cp.wait_send()
        mine.wait()

    return pl.pallas_call(
        body, name=name,
        out_shape=jax.ShapeDtypeStruct((N_DEV * m, n), xs.dtype),
        in_specs=[pl.BlockSpec(memory_space=pltpu.VMEM)],
        out_specs=pl.BlockSpec(memory_space=pltpu.VMEM),
        scratch_shapes=[pltpu.SemaphoreType.DMA((7,)), pltpu.SemaphoreType.DMA((7,)), pltpu.SemaphoreType.DMA],
    )(xs)


def gather_weights(shards, name):
    n = len(shards)

    def body(*refs):
        ins, outs = refs[:n], refs[n:2 * n]
        send_sems, recv_sems, local_sems = refs[2 * n:]
        x, y, c = _place()
        chips = _other_chips(x, y)
        local, sent = [], []
        for i in range(n):
            mine = outs[i].at[2 * x + y]
            lc = pltpu.make_async_copy(ins[i], mine, local_sems.at[i])
            lc.start()
            local.append(lc)
            for j, (px, py) in enumerate(chips):
                rc = pltpu.make_async_remote_copy(
                    src_ref=ins[i], dst_ref=mine, send_sem=send_sems.at[3 * i + j], recv_sem=recv_sems.at[3 * i + j],
                    device_id=(px, py, c), device_id_type=MESH)
                rc.start()
                sent.append(rc)
        for i in range(n):
            for j, (px, py) in enumerate(chips):
                pltpu.make_async_remote_copy(
                    src_ref=ins[i], dst_ref=outs[i].at[2 * px + py], send_sem=send_sems.at[3 * i + j],
                    recv_sem=recv_sems.at[3 * i + j], device_id=(px, py, c), device_id_type=MESH).wait_recv()
        for rc in sent:
            rc.wait_send()
        for lc in local:
            lc.wait()

    return pl.pallas_call(
        body, name=name,
        out_shape=[jax.ShapeDtypeStruct((N_SHARD,) + s.shape, s.dtype) for s in shards],
        in_specs=[ANY] * n, out_specs=[ANY] * n,
        scratch_shapes=[pltpu.SemaphoreType.DMA((3 * n,)), pltpu.SemaphoreType.DMA((3 * n,)),
                        pltpu.SemaphoreType.DMA((n,))],
    )(*shards)


def rs_to_sibling(gs, name):
    n = len(gs)

    def body(*refs):
        ins, outs = refs[:n], refs[n:2 * n]
        send_sems, recv_sems = refs[2 * n:]
        x, y, c = _place()
        copies = []
        for i in range(n):
            h = gs[i].shape[1] // 2
            src = ins[i].at[:, pl.ds(pl.multiple_of((1 - c) * h, 8), h), :]
            cp = pltpu.make_async_remote_copy(src_ref=src, dst_ref=outs[i], send_sem=send_sems.at[i],
                                              recv_sem=recv_sems.at[i], device_id=(x, y, 1 - c), device_id_type=MESH)
            cp.start()
            copies.append(cp)
        for cp in copies:
            cp.wait()

    return pl.pallas_call(
        body, name=name,
        out_shape=[jax.ShapeDtypeStruct((N_SHARD, g.shape[1] // 2, g.shape[2]), g.dtype) for g in gs],
        in_specs=[ANY] * n, out_specs=[ANY] * n,
        scratch_shapes=[pltpu.SemaphoreType.DMA((n,)), pltpu.SemaphoreType.DMA((n,))],
    )(*gs)


def rs_to_chips(parts, name):
    n = len(parts)

    def body(*refs):
        ins, outs = refs[:n], refs[n:2 * n]
        send_sems, recv_sems = refs[2 * n:]
        x, y, c = _place()
        copies = []
        for i in range(n):
            for j, (px, py) in enumerate(_other_chips(x, y)):
                cp = pltpu.make_async_remote_copy(
                    src_ref=ins[i].at[2 * px + py], dst_ref=outs[i].at[j], send_sem=send_sems.at[3 * i + j],
                    recv_sem=recv_sems.at[3 * i + j], device_id=(px, py, c), device_id_type=MESH)
                cp.start()
                copies.append(cp)
        for cp in copies:
            cp.wait()

    return pl.pallas_call(
        body, name=name,
        out_shape=[jax.ShapeDtypeStruct((3,) + a.shape[1:], a.dtype) for a in parts],
        in_specs=[ANY] * n, out_specs=[ANY] * n,
        scratch_shapes=[pltpu.SemaphoreType.DMA((3 * n,)), pltpu.SemaphoreType.DMA((3 * n,))],
    )(*parts)


def rs_join_halves(halves, name):
    n = len(halves)

    def body(*refs):
        ins, outs = refs[:n], refs[n:2 * n]
        send_sems, recv_sems, local_sems = refs[2 * n:]
        x, y, c = _place()
        started = []
        for i in range(n):
            h = halves[i].shape[0]
            mine = outs[i].at[pl.ds(pl.multiple_of(c * h, 8), h), :]
            lc = pltpu.make_async_copy(ins[i], mine, local_sems.at[i])
            lc.start()
            rc = pltpu.make_async_remote_copy(src_ref=ins[i], dst_ref=mine, send_sem=send_sems.at[i],
                                              recv_sem=recv_sems.at[i], device_id=(x, y, 1 - c), device_id_type=MESH)
            rc.start()
            started.append((lc, rc))
        for i, (lc, rc) in enumerate(started):
            h = halves[i].shape[0]
            theirs = outs[i].at[pl.ds(pl.multiple_of((1 - c) * h, 8), h), :]
            pltpu.make_async_remote_copy(src_ref=ins[i], dst_ref=theirs, send_sem=send_sems.at[i],
                                         recv_sem=recv_sems.at[i], device_id=(x, y, 1 - c), device_id_type=MESH).wait_recv()
            rc.wait_send()
            lc.wait()

    return pl.pallas_call(
        body, name=name,
        out_shape=[jax.ShapeDtypeStruct((2 * a.shape[0], a.shape[1]), a.dtype) for a in halves],
        in_specs=[ANY] * n, out_specs=[ANY] * n,
        scratch_shapes=[pltpu.SemaphoreType.DMA((n,)), pltpu.SemaphoreType.DMA((n,)), pltpu.SemaphoreType.DMA((n,))],
    )(*halves)


def _row_tile(rows, cols, limit_bytes=2 * 1024 * 1024):
    best = 8
    for t in range(8, rows + 1, 8):
        if rows % t == 0 and t * cols * 4 <= limit_bytes:
            best = t
    return best


def rs_add_sibling(g, recv, c, name):
    _, R, C = g.shape
    h = R // 2
    tr = _row_tile(h, C)
    nt = h // tr

    def body(c_ref, g_ref, r_ref, o_ref):
        o_ref[...] = g_ref[...] + r_ref[...]

    blk = pl.BlockSpec((None, tr, C), lambda k, i, c_ref: (k, i, 0))
    return pl.pallas_call(
        body, name=name,
        grid_spec=pltpu.PrefetchScalarGridSpec(
            num_scalar_prefetch=1, grid=(N_SHARD, nt),
            in_specs=[pl.BlockSpec((None, tr, C), lambda k, i, c_ref: (k, c_ref[0] * nt + i, 0)), blk],
            out_specs=blk),
        out_shape=jax.ShapeDtypeStruct((N_SHARD, h, C), F32),
        compiler_params=_params("parallel", "parallel"),
    )(c, g, recv)


def rs_add_chips(part, recv, chip, name):
    _, h, C = part.shape
    tr = _row_tile(h, C)

    def body(k_ref, p_ref, r_ref, o_ref):
        o_ref[...] = ((p_ref[...] + r_ref[0]) + r_ref[1]) + r_ref[2]

    return pl.pallas_call(
        body, name=name,
        grid_spec=pltpu.PrefetchScalarGridSpec(
            num_scalar_prefetch=1, grid=(h // tr,),
            in_specs=[pl.BlockSpec((None, tr, C), lambda i, k_ref: (k_ref[0], i, 0)),
                      pl.BlockSpec((3, tr, C), lambda i, k_ref: (0, i, 0))],
            out_specs=pl.BlockSpec((tr, C), lambda i, k_ref: (i, 0))),
        out_shape=jax.ShapeDtypeStruct((h, C), F32),
        compiler_params=_params("parallel"),
    )(chip, part, recv)


def _adamw_math(w, g, m, v):
    m = ADAM_B1 * m + (1.0 - ADAM_B1) * g
    v = ADAM_B2 * v + (1.0 - ADAM_B2) * (g * g)
    m_hat = m / (1.0 - ADAM_B1 ** ADAM_STEP)
    v_hat = v / (1.0 - ADAM_B2 ** ADAM_STEP)
    delta = -ADAM_LR * (m_hat / (jnp.sqrt(v_hat) + ADAM_EPS) + ADAM_WD * w)
    return delta, m, v


def adamw(w, g, m, v, name):
    R, C = w.shape
    tr = _row_tile(R, C, 1024 * 1024)

    def body(w_ref, g_ref, m_ref, v_ref, d_ref, nm_ref, nv_ref):
        d_ref[...], nm_ref[...], nv_ref[...] = _adamw_math(w_ref[...], g_ref[...], m_ref[...], v_ref[...])

    blk = pl.BlockSpec((tr, C), lambda i: (i, 0))
    return pl.pallas_call(
        body, name=name, grid=(R // tr,), in_specs=[blk] * 4, out_specs=[blk] * 3,
        out_shape=[jax.ShapeDtypeStruct((R, C), F32)] * 3,
        compiler_params=_params("parallel"),
    )(w, g, m, v)


MOD_SH = 6 * D // N_SHARD
PK_ROWS = 16


def mod_fwd(call16, w_sh, b_sh, name):
    def body(c_ref, w_ref, b_ref, o_ref):
        o_ref[...] = _dot(_silu_parts(c_ref[...])[0], w_ref[...], prec=HI) + b_ref[...]

    return pl.pallas_call(body, name=name, out_shape=jax.ShapeDtypeStruct((16, MOD_SH), F32),
                          compiler_params=_params())(call16, w_sh, b_sh)


def prep_small(lbf2, lbb2, theta_row, name):
    def body(f_ref, b_ref, t_ref, lbf_ref, lbb_ref, lg_ref):
        lbf_ref[...] = _sigmoid(f_ref[0:1, :] - f_ref[1:2, :])
        lbb_ref[...] = _sigmoid(b_ref[0:1, :] - b_ref[1:2, :])
        t = t_ref[...]
        lg_ref[...] = jnp.minimum(t, 0.0) - jnp.log(1.0 + jnp.exp(-jnp.abs(t)))

    row = jax.ShapeDtypeStruct((1, D), F32)
    return pl.pallas_call(body, name=name, out_shape=[row, row, row], compiler_params=_params())(lbf2, lbb2, theta_row)


def small_grads(g3, lbf, lbb, theta_row, name):
    def body(g_ref, lbf_ref, lbb_ref, t_ref, pk_ref, aux_ref):
        s = g_ref[0]
        for d in range(1, N_DEV):
            s = s + g_ref[d]
        pk_ref[...] = jnp.zeros_like(pk_ref)
        aux_ref[...] = jnp.zeros_like(aux_ref)
        pk_ref[1:7, :] = s[0:6]
        pk_ref[1:3, :] += s[6:8]
        pk_ref[7:8, :] = s[8:9] + s[9:10]
        pk_ref[8:9, :] = s[10:11]
        lbf, lbb = lbf_ref[...], lbb_ref[...]
        daf = (s[13:14] + s[14:15]) * lbf * (1.0 - lbf)
        dab = (s[15:16] + s[16:17]) * lbb * (1.0 - lbb)
        pk_ref[9:10, :] = daf
        pk_ref[10:11, :] = -daf
        pk_ref[11:12, :] = dab
        pk_ref[12:13, :] = -dab
        pk_ref[13:14, :] = s[11:12]
        pk_ref[14:15, :] = (s[17:18] + s[18:19]) * _sigmoid(-t_ref[...])
        pk_ref[15:16, :] = s[12:13]
        aux_ref[0:2, :] = s[6:8]
        aux_ref[2:3, :] = jnp.broadcast_to(jnp.sum(s[19:20], axis=-1, keepdims=True), (1, D))

    return pl.pallas_call(body, name=name,
                          out_shape=[jax.ShapeDtypeStruct((PK_ROWS, D), F32), jax.ShapeDtypeStruct((8, D), F32)],
                          compiler_params=_params())(g3, lbf, lbb, theta_row)


def mod_bwd(call16, dmod_sh, w_sh, name):
    def body(c_ref, d_ref, w_ref, dw_ref, ds_ref):
        dm = d_ref[...]
        dw_ref[...] = _dot(_silu_parts(c_ref[...])[0], dm, 0, 0, prec=HI)
        ds_ref[...] = jnp.zeros_like(ds_ref)
        ds_ref[0:1, :] = _dot(dm[8:9, :], w_ref[...], 1, 1, prec=HI)

    return pl.pallas_call(body, name=name,
                          out_shape=[jax.ShapeDtypeStruct((D, MOD_SH), F32), jax.ShapeDtypeStruct((8, D), F32)],
                          compiler_params=_params())(call16, dmod_sh, w_sh)


def adamw_small(g4, pk_g, pk_w, pk_m, pk_v, name):
    def body(g4_ref, g_ref, w_ref, m_ref, v_ref, go_ref, d_ref, nm_ref, nv_ref):
        w = w_ref[...]
        ds = ((g4_ref[0:1, :] + g4_ref[16:17, :]) + g4_ref[32:33, :]) + g4_ref[48:49, :]
        row = lax.broadcasted_iota(jnp.int32, (PK_ROWS, D), 0)
        g = jnp.where(row == 0, ds * _silu_parts(w[0:1, :])[1], g_ref[...])
        go_ref[...] = g
        d_ref[...], nm_ref[...], nv_ref[...] = _adamw_math(w, g, m_ref[...], v_ref[...])

    pk = jax.ShapeDtypeStruct((PK_ROWS, D), F32)
    return pl.pallas_call(body, name=name, out_shape=[pk, pk, pk, pk], compiler_params=_params())(g4, pk_g, pk_w, pk_m, pk_v)


def _pack_params(c_ctx, b_mod, n1, n2, lbf, lbb, hgn, th_f, th_b, fin):
    theta = jnp.concatenate([th_f.reshape(HEADS), th_b.reshape(HEADS), jnp.zeros((D - 2 * HEADS,), F32)])
    return jnp.concatenate([c_ctx.reshape(1, D), b_mod.reshape(6, D), n1.reshape(1, D), n2.reshape(1, D), lbf, lbb,
                            hgn.reshape(1, D), theta.reshape(1, D), fin.reshape(1, D)], axis=0)


def _unpack_params(pk):
    return (pk[0], pk[1:7].reshape(1, 6 * D), pk[7:8], pk[8:9], pk[9:11], pk[11:13], pk[13:14],
            pk[14, 0:HEADS].reshape(1, HEADS), pk[14, HEADS:2 * HEADS].reshape(1, HEADS), pk[15])


def kernel(x, c, ctx, c_ctx, w_mod, b_mod, norm1_w, norm2_w, w_in, hg_lb_fwd, hg_lb_bwd, hg_norm_w, rt_theta_fwd, rt_theta_bwd, w_proj_hgrn, w_proj_ret, w_out, w_ffn_gate, w_ffn_up, w_ffn_down, final_norm_w, loss_target, m_c_ctx, m_w_mod, m_b_mod, m_norm1_w, m_norm2_w, m_w_in, m_hg_lb_fwd, m_hg_lb_bwd, m_hg_norm_w, m_rt_theta_fwd, m_rt_theta_bwd, m_w_proj_hgrn, m_w_proj_ret, m_w_out, m_w_ffn_gate, m_w_ffn_up, m_w_ffn_down, m_final_norm_w, v_c_ctx, v_w_mod, v_b_mod, v_norm1_w, v_norm2_w, v_w_in, v_hg_lb_fwd, v_hg_lb_bwd, v_hg_norm_w, v_rt_theta_fwd, v_rt_theta_bwd, v_w_proj_hgrn, v_w_proj_ret, v_w_out, v_w_ffn_gate, v_w_ffn_up, v_w_ffn_down, v_final_norm_w):
    xi, yi, ci = _place()
    dev = 4 * xi + 2 * yi + ci
    chip = 2 * xi + yi
    core_arg = jnp.reshape(ci, (1,)).astype(jnp.int32)
    chip_arg = jnp.reshape(chip, (1,)).astype(jnp.int32)

    c_all = allgather8(jnp.concatenate([c, jnp.zeros((7, D), F32)], axis=0), "gather_c").reshape(N_DEV, 8, D)[:, 0]
    call16 = jnp.concatenate([c_all, c_ctx.reshape(1, D), jnp.zeros((7, D), F32)], axis=0)
    b_sh = lax.dynamic_slice_in_dim(b_mod, chip * MOD_SH, MOD_SH, axis=1)
    mod_sh = mod_fwd(call16, w_mod[0], b_sh, "mod_fwd")
    mod_g = allgather8(mod_sh, "gather_mod").reshape(N_DEV, 16, MOD_SH)
    mod_all = jnp.concatenate([mod_g[0], mod_g[2], mod_g[4], mod_g[6]], axis=1)
    mod_x = lax.dynamic_index_in_dim(mod_all, dev, axis=0, keepdims=False).reshape(6, D)
    mod_c = mod_all[8].reshape(6, D)

    pk_w = _pack_params(c_ctx, b_mod, norm1_w, norm2_w, hg_lb_fwd, hg_lb_bwd, hg_norm_w, rt_theta_fwd, rt_theta_bwd, final_norm_w)
    theta_row = pk_w[14:15]
    lb_f, lb_b, lg_row = prep_small(hg_lb_fwd, hg_lb_bwd, theta_row, "prep_small")
    lg_f = jnp.broadcast_to(lg_row[0, 0:HEADS].reshape(HEADS, 1, 1), (HEADS, 1, RT_DV))
    lg_b = jnp.broadcast_to(lg_row[0, HEADS:2 * HEADS].reshape(HEADS, 1, 1), (HEADS, 1, RT_DV))

    shards = [w_in[0], w_proj_hgrn[0], w_proj_ret[0], w_out[0], w_ffn_gate[0], w_ffn_up[0], w_ffn_down[0]]
    g_in, g_pa, g_pb, g_out, g_wg, g_wu, g_wd = gather_weights([s.astype(BF16) for s in shards], "gather_weights")
    w = {"w_in": g_in, "w_pa": g_pa.reshape(D, D), "w_pb": g_pb.reshape(2 * D, D), "w_out": g_out.reshape(D, D),
         "wg": g_wg, "wu": g_wu, "wd": g_wd}

    dx, grads, small = local_step(x[0], ctx[0], loss_target[0], mod_x, mod_c, lb_f, lb_b, lg_f, lg_b,
                                  norm1_w, norm2_w, hg_norm_w, final_norm_w.reshape(1, D), w)

    order = ["w_in", "w_pa", "w_pb", "w_out", "wg", "wu", "wd"]
    gs = [grads[k] for k in order]
    from_sibling = rs_to_sibling(gs, "rs_to_sibling")
    chip_sums = [rs_add_sibling(g, r, core_arg, "rs_add_sibling_" + k) for g, r, k in zip(gs, from_sibling, order)]
    from_chips = rs_to_chips(chip_sums, "rs_to_chips")
    halves = [rs_add_chips(a, r, chip_arg, "rs_add_chips_" + k) for a, r, k in zip(chip_sums, from_chips, order)]
    full = dict(zip(order, rs_join_halves(halves, "rs_join_halves")))

    g3 = allgather8(small, "gather_small").reshape(N_DEV, SMALL_ROWS, D)
    pk_g, aux = small_grads(g3, lb_f, lb_b, theta_row, "small_grads")
    loss = aux[2, 0]
    dmod16 = jnp.concatenate([
        g3[:, 0:6, :].reshape(N_DEV, 6 * D),
        jnp.concatenate([aux[0], aux[1], jnp.zeros((4 * D,), F32)]).reshape(1, 6 * D),
        jnp.zeros((7, 6 * D), F32)], axis=0)
    dmod_sh = lax.dynamic_slice_in_dim(dmod16, chip * MOD_SH, MOD_SH, axis=1)
    g_wmod, dsilu = mod_bwd(call16, dmod_sh, w_mod[0], "mod_bwd")
    g4 = allgather8(dsilu, "gather_dsilu")
    pk_m = _pack_params(m_c_ctx, m_b_mod, m_norm1_w, m_norm2_w, m_hg_lb_fwd, m_hg_lb_bwd, m_hg_norm_w, m_rt_theta_fwd, m_rt_theta_bwd, m_final_norm_w)
    pk_v = _pack_params(v_c_ctx, v_b_mod, v_norm1_w, v_norm2_w, v_hg_lb_fwd, v_hg_lb_bwd, v_hg_norm_w, v_rt_theta_fwd, v_rt_theta_bwd, v_final_norm_w)
    pk_g, pk_d, pk_nm, pk_nv = adamw_small(g4, pk_g, pk_w, pk_m, pk_v, "adamw_small")

    big = {
        "w_mod": (g_wmod, w_mod, m_w_mod, v_w_mod),
        "w_in": (full["w_in"], w_in, m_w_in, v_w_in),
        "w_pa": (full["w_pa"], w_proj_hgrn, m_w_proj_hgrn, v_w_proj_hgrn),
        "w_pb": (full["w_pb"], w_proj_ret, m_w_proj_ret, v_w_proj_ret),
        "w_out": (full["w_out"], w_out, m_w_out, v_w_out),
        "wg": (full["wg"], w_ffn_gate, m_w_ffn_gate, v_w_ffn_gate),
        "wu": (full["wu"], w_ffn_up, m_w_ffn_up, v_w_ffn_up),
        "wd": (full["wd"], w_ffn_down, m_w_ffn_down, v_w_ffn_down),
    }
    res = {}
    for k, (g, wt, mt, vt) in big.items():
        d, nm, nv = adamw(wt[0], g, mt[0], vt[0], "adamw_" + k)
        res[k] = (g[None], d[None], nm[None], nv[None])

    sm = [_unpack_params(p) for p in (pk_g, pk_d, pk_nm, pk_nv)]
    outs = []
    for t in range(4):
        (s_cctx, s_bmod, s_n1, s_n2, s_lbf, s_lbb, s_hgn, s_thf, s_thb, s_fin) = sm[t]
        outs.append([s_cctx, res["w_mod"][t], s_bmod, s_n1, s_n2, res["w_in"][t], s_lbf, s_lbb, s_hgn, s_thf, s_thb,
                     res["w_pa"][t], res["w_pb"][t], res["w_out"][t], res["wg"][t], res["wu"][t], res["wd"][t], s_fin])
    return (loss, dx[None], *outs[0], *outs[1], *outs[2], *outs[3])
```

```python
import functools

import jax
import jax.numpy as jnp
from jax import lax
from jax.experimental import pallas as pl
from jax.experimental.pallas import tpu as pltpu

F32 = jnp.float32
BF16 = jnp.bfloat16
HI = lax.Precision.HIGHEST

D = 1024
HEADS = 8
HG_D = 128
RT_DK = 128
RT_DV = 256
D_FF = 2816
D_IN = 13312
N_SHARD = 4
IN_SH = D_IN // N_SHARD
FF_SH = D_FF // N_SHARD
HG_CHUNK = 32
SCAN_ROWS = 256
HG_GROUP = 8
EPS = 1e-6
GN_EPS = 1e-5
Q_SCALE = 128.0 ** -0.5
VMEM_LIMIT = 56 * 1024 * 1024

COL_HQ, COL_HFF, COL_HFB, COL_HI, COL_HG = 0, 8, 16, 24, 32
COL_RQ, COL_RK, COL_RV, COL_RG, COL_GA, COL_GB = 40, 48, 56, 72, 88, 96

ADAM_LR, ADAM_B1, ADAM_B2, ADAM_EPS, ADAM_WD, ADAM_STEP = 0.001, 0.9, 0.999, 1e-08, 0.01, 10


def _params(*sem):
    return pltpu.CompilerParams(dimension_semantics=sem, vmem_limit_bytes=VMEM_LIMIT)


def _dot(a, b, ca=1, cb=0, prec=None):
    return lax.dot_general(a, b, (((ca,), (cb,)), ((), ())), precision=prec, preferred_element_type=F32)


def _bdot(a, b, ca=1, cb=0):
    return _dot(a.astype(BF16), b.astype(BF16), ca, cb)


def _sigmoid(z):
    return 1.0 / (1.0 + jnp.exp(-z))


def _rowsum(a):
    return jnp.sum(a, axis=0, keepdims=True)


def _lanemean(a):
    return jnp.mean(a, axis=-1, keepdims=True)


def normmod_matmul(x, nw, sh, sc, w4, name):
    L = x.shape[0]
    tm = min(512, L)
    tn = IN_SH // 2

    def body(x_ref, nw_ref, sh_ref, sc_ref, w_ref, p_ref, hx_ref, hx_scr):
        @pl.when((pl.program_id(1) == 0) & (pl.program_id(2) == 0))
        def _():
            xv = x_ref[...]
            n = xv * lax.rsqrt(_lanemean(xv * xv) + EPS) * nw_ref[...]
            h = (n * (1.0 + sc_ref[...]) + sh_ref[...]).astype(BF16)
            hx_scr[...] = h
            hx_ref[...] = h

        p_ref[...] = _dot(hx_scr[...], w_ref[...])

    vec = pl.BlockSpec((1, D), lambda i, k, j: (0, 0))
    return pl.pallas_call(
        body, name=name,
        grid=(L // tm, N_SHARD, 2),
        in_specs=[pl.BlockSpec((tm, D), lambda i, k, j: (i, 0)), vec, vec, vec,
                  pl.BlockSpec((None, D, tn), lambda i, k, j: (k, 0, j))],
        out_specs=[pl.BlockSpec((tm, tn), lambda i, k, j: (i, 2 * k + j)),
                   pl.BlockSpec((tm, D), lambda i, k, j: (i, 0))],
        out_shape=[jax.ShapeDtypeStruct((L, D_IN), F32), jax.ShapeDtypeStruct((L, D), BF16)],
        scratch_shapes=[pltpu.VMEM((tm, D), BF16)],
        compiler_params=_params("parallel", "arbitrary", "arbitrary"),
    )(x, nw, sh, sc, w4)


def _hgrn_gates(z, lb):
    sg = _sigmoid(z)
    sgn = _sigmoid(-z)
    f = lb + (1.0 - lb) * sg
    k = (1.0 - lb) * sgn
    return sg, sgn, f, k


def _tri(n, reverse):
    r = lax.broadcasted_iota(jnp.int32, (n, n), 0)
    c = lax.broadcasted_iota(jnp.int32, (n, n), 1)
    return jnp.where((r <= c) if reverse else (r >= c), 1.0, 0.0).astype(F32)


def _decay3(b, reverse):
    C = b.shape[0]
    t = lax.broadcasted_iota(jnp.int32, (C, C, 1), 0)
    s = lax.broadcasted_iota(jnp.int32, (C, C, 1), 1)
    mask = (t <= s) if reverse else (t >= s)
    return jnp.exp(jnp.where(mask, b[:, None, :] - b[None, :, :], -jnp.inf))


def _hgrn_state_step(k, v, b, s_t, last):
    b_last = b[last:last + 1]
    return s_t * jnp.exp(b_last) + _bdot(v, k * jnp.exp(b_last - b), 0, 0)


def hgrn_scan_fwd(p, lb, s0, col_z, reverse, name):
    L = p.shape[0]
    nB = L // SCAN_ROWS
    nC = SCAN_ROWS // HG_CHUNK
    C = HG_CHUNK
    G, W = HG_GROUP, HG_GROUP * HG_D
    last = 0 if reverse else C - 1

    def bmap(b):
        return (nB - 1 - b) if reverse else b

    def body(q_ref, z_ref, v_ref, lb_ref, s0_ref, o_ref, sfin_ref, sblk_ref, s_scr):
        blk = pl.program_id(1)

        @pl.when(blk == 0)
        def _():
            s_scr[...] = s0_ref[...]

        sblk_ref[...] = s_scr[...]
        tri = _tri(C, reverse)

        def chunk(ci, carry):
            c = (nC - 1 - ci) if reverse else ci
            rows = pl.ds(pl.multiple_of(c * C, C), C)
            for j in range(G):
                lanes = slice(j * HG_D, (j + 1) * HG_D)
                q = q_ref[rows, lanes] * Q_SCALE
                v = v_ref[rows, lanes]
                _, _, f, k = _hgrn_gates(z_ref[rows, lanes], lb_ref[:, lanes])
                b = _dot(tri, jnp.log(f), prec=HI)
                s_t = s_scr[j]
                e3 = _decay3(b, reverse)
                att3 = jnp.sum(q[:, None, :] * k[None, :, :] * e3, axis=-1, keepdims=True)
                o_ref[rows, lanes] = jnp.sum(att3 * v[None, :, :], axis=1) + _bdot(q * jnp.exp(b), s_t, 1, 1)
                s_scr[j] = _hgrn_state_step(k, v, b, s_t, last)
            return carry

        lax.fori_loop(0, nC, chunk, 0)

        @pl.when(blk == nB - 1)
        def _():
            sfin_ref[...] = s_scr[...]

    def col(c0):
        return pl.BlockSpec((SCAN_ROWS, W), lambda h, b: (bmap(b), c0 // G + h))

    state = pl.BlockSpec((G, HG_D, HG_D), lambda h, b: (h, 0, 0))
    return pl.pallas_call(
        body, name=name,
        grid=(HEADS // G, nB),
        in_specs=[col(COL_HQ), col(col_z), col(COL_HI), pl.BlockSpec((1, W), lambda h, b: (0, h)), state],
        out_specs=[pl.BlockSpec((SCAN_ROWS, W), lambda h, b: (bmap(b), h)), state,
                   pl.BlockSpec((None, G, HG_D, HG_D), lambda h, b: (bmap(b), h, 0, 0))],
        out_shape=[jax.ShapeDtypeStruct((L, D), F32),
                   jax.ShapeDtypeStruct((HEADS, HG_D, HG_D), F32),
                   jax.ShapeDtypeStruct((nB, HEADS, HG_D, HG_D), F32)],
        scratch_shapes=[pltpu.VMEM((G, HG_D, HG_D), F32)],
        compiler_params=_params("parallel", "arbitrary"),
    )(p, p, p, lb, s0)


def hgrn_scan_bwd(p, lb, s_blocks, d_o, ds_fin, prev, col_z, reverse, name):
    L = p.shape[0]
    nB = L // SCAN_ROWS
    nC = SCAN_ROWS // HG_CHUNK
    C = HG_CHUNK
    G, W = HG_GROUP, HG_GROUP * HG_D
    last = 0 if reverse else C - 1
    has_prev = prev is not None
    out_dt = BF16 if has_prev else F32

    def bmap(b):
        return b if reverse else (nB - 1 - b)

    def body(*refs):
        q_ref, z_ref, v_ref, lb_ref, sblk_ref, do_ref, dsf_ref = refs[:7]
        refs = refs[7:]
        if has_prev:
            pq_ref, pv_ref = refs[:2]
            refs = refs[2:]
        dq_ref, dz_ref, dv_ref, dlb_ref, ds0_ref, st_scr, run_scr, ds_scr = refs
        blk = pl.program_id(1)

        @pl.when(blk == 0)
        def _():
            ds_scr[...] = dsf_ref[...]
            dlb_ref[...] = jnp.zeros_like(dlb_ref)

        tri = _tri(C, reverse)
        row = lax.broadcasted_iota(jnp.int32, (C, HG_D), 0)

        def load(rows, lanes):
            sg, sgn, f, k = _hgrn_gates(z_ref[rows, lanes], lb_ref[:, lanes])
            b = _dot(tri, jnp.log(f), prec=HI)
            return sg, sgn, f, k, b

        run_scr[...] = sblk_ref[...]

        def recompute(ci, carry):
            c = (nC - 1 - ci) if reverse else ci
            rows = pl.ds(pl.multiple_of(c * C, C), C)
            for j in range(G):
                lanes = slice(j * HG_D, (j + 1) * HG_D)
                _, _, _, k, b = load(rows, lanes)
                s_t = run_scr[j]
                st_scr[c, j] = s_t
                run_scr[j] = _hgrn_state_step(k, v_ref[rows, lanes], b, s_t, last)
            return carry

        lax.fori_loop(0, nC, recompute, 0)

        def chunk(ci, carry):
            c = ci if reverse else (nC - 1 - ci)
            rows = pl.ds(pl.multiple_of(c * C, C), C)
            for j in range(G):
                lanes = slice(j * HG_D, (j + 1) * HG_D)
                lb = lb_ref[:, lanes]
                sg, sgn, f, k, b = load(rows, lanes)
                q = q_ref[rows, lanes] * Q_SCALE
                v = v_ref[rows, lanes]
                d_o = do_ref[rows, lanes]
                s_t = st_scr[c, j]
                ds_t = ds_scr[j]
                e3 = _decay3(b, reverse)
                eb = jnp.exp(b)
                b_last = b[last:last + 1]
                eb_last = jnp.exp(b_last)
                kdec = jnp.exp(b_last - b)
                qe = q * eb
                ke = k * kdec
                datt3 = jnp.sum(d_o[:, None, :] * v[None, :, :], axis=-1, keepdims=True)
                p3 = datt3 * e3
                dq_tot = _bdot(d_o, s_t, 1, 0) * eb + jnp.sum(p3 * k[None, :, :], axis=1)
                dke = _bdot(v, ds_t, 1, 0)
                dk_tot = dke * kdec + jnp.sum(p3 * q[:, None, :], axis=0)
                att3 = jnp.sum(q[:, None, :] * k[None, :, :] * e3, axis=-1, keepdims=True)
                dv = jnp.sum(att3 * d_o[:, None, :], axis=0) + _bdot(ke, ds_t, 1, 1)
                db_last = _rowsum(dke * ke) + eb_last * _rowsum(ds_t * s_t)
                db = q * dq_tot - k * dk_tot + jnp.where(row == last, db_last, 0.0)
                dlf = _dot(tri, db, 0, 0, prec=HI)
                g = dlf / f - dk_tot
                dz_ref[rows, lanes] = (g * (1.0 - lb) * sg * sgn).astype(BF16)
                dlb_ref[:, lanes] += _rowsum(g * sgn)
                dq = dq_tot * Q_SCALE
                if has_prev:
                    dq = dq + pq_ref[rows, lanes]
                    dv = dv + pv_ref[rows, lanes]
                dq_ref[rows, lanes] = dq.astype(out_dt)
                dv_ref[rows, lanes] = dv.astype(out_dt)
                ds_scr[j] = ds_t * eb_last + _bdot(d_o, qe, 0, 0)
            return carry

        lax.fori_loop(0, nC, chunk, 0)

        @pl.when(blk == nB - 1)
        def _():
            ds0_ref[...] = ds_scr[...]

    def col(c0):
        return pl.BlockSpec((SCAN_ROWS, W), lambda h, b: (bmap(b), c0 // G + h))

    tile = pl.BlockSpec((SCAN_ROWS, W), lambda h, b: (bmap(b), h))
    state = pl.BlockSpec((G, HG_D, HG_D), lambda h, b: (h, 0, 0))
    in_specs = [col(COL_HQ), col(col_z), col(COL_HI),
                pl.BlockSpec((1, W), lambda h, b: (0, h)),
                pl.BlockSpec((None, G, HG_D, HG_D), lambda h, b: (bmap(b), h, 0, 0)),
                tile, state]
    args = [p, p, p, lb, s_blocks, d_o, ds_fin]
    if has_prev:
        in_specs += [tile, tile]
        args += list(prev)
    return pl.pallas_call(
        body, name=name,
        grid=(HEADS // G, nB),
        in_specs=in_specs,
        out_specs=[tile, tile, tile, pl.BlockSpec((1, W), lambda h, b: (0, h)), state],
        out_shape=[jax.ShapeDtypeStruct((L, D), out_dt), jax.ShapeDtypeStruct((L, D), BF16),
                   jax.ShapeDtypeStruct((L, D), out_dt), jax.ShapeDtypeStruct((1, D), F32),
                   jax.ShapeDtypeStruct((HEADS, HG_D, HG_D), F32)],
        scratch_shapes=[pltpu.VMEM((nC, G, HG_D, HG_D), F32), pltpu.VMEM((G, HG_D, HG_D), F32),
                        pltpu.VMEM((G, HG_D, HG_D), F32)],
        compiler_params=_params("parallel", "arbitrary"),
    )(*args)


def _rope(t, cosf, sinf):
    return t * cosf + pltpu.roll(t, RT_DK // 2, 1) * sinf


def _rope_t(d, cosf, sinf):
    return d * cosf + pltpu.roll(d * sinf, RT_DK // 2, 1)


def _ret_decays(lg, reverse):
    C = SCAN_ROWS
    t = lax.broadcasted_iota(jnp.int32, (C, C), 0)
    s = lax.broadcasted_iota(jnp.int32, (C, C), 1)
    delta = ((s - t) if reverse else (t - s)).astype(F32)
    dmat = jnp.where(delta >= 0, jnp.exp(lg * jnp.maximum(delta, 0.0)), 0.0)
    r = lax.broadcasted_iota(jnp.int32, (C, RT_DK), 0)
    pos = ((C - 1 - r) if reverse else r).astype(F32)
    lg1 = lg[:, :RT_DK]
    qdec = jnp.exp(lg1 * (pos + 1.0))
    kdec = jnp.exp(lg1 * (C - 1.0 - pos))
    sdec = jnp.exp(lg1 * float(C))
    return dmat, delta, pos, qdec, kdec, sdec


def ret_scan_fwd(p, cosf, sinf, lg, s0, reverse, name):
    L = p.shape[0]
    C = SCAN_ROWS
    nB = L // C

    def bmap(b):
        return (nB - 1 - b) if reverse else b

    def body(q_ref, k_ref, v_ref, cos_ref, sin_ref, lg_ref, s0_ref, o_ref, sfin_ref, sblk_ref, s_scr):
        blk = pl.program_id(1)

        @pl.when(blk == 0)
        def _():
            s_scr[...] = s0_ref[...]

        s_t = s_scr[...]
        sblk_ref[...] = s_t
        cosf, sinf = cos_ref[...], sin_ref[...]
        dmat, _, _, qdec, kdec, sdec = _ret_decays(lg_ref[...], reverse)
        q = _rope(q_ref[...] * Q_SCALE, cosf, sinf)
        k = _rope(k_ref[...], cosf, sinf)
        v = v_ref[...]
        att = _bdot(q, k, 1, 1) * dmat
        o_ref[...] = _bdot(att, v) + _bdot(q * qdec, s_t, 1, 1)
        s_new = s_t * sdec + _bdot(v, k * kdec, 0, 0)
        s_scr[...] = s_new

        @pl.when(blk == nB - 1)
        def _():
            sfin_ref[...] = s_new

    def col(c0):
        return pl.BlockSpec((C, RT_DK), lambda h, b: (bmap(b), c0 + h))

    tab = pl.BlockSpec((C, RT_DK), lambda h, b: (bmap(b), 0))
    state = pl.BlockSpec((None, RT_DV, RT_DK), lambda h, b: (h, 0, 0))
    return pl.pallas_call(
        body, name=name,
        grid=(HEADS, nB),
        in_specs=[col(COL_RQ), col(COL_RK),
                  pl.BlockSpec((C, RT_DV), lambda h, b: (bmap(b), COL_RV // 2 + h)),
                  tab, tab, pl.BlockSpec((None, 1, RT_DV), lambda h, b: (h, 0, 0)), state],
        out_specs=[pl.BlockSpec((C, RT_DV), lambda h, b: (bmap(b), h)), state,
                   pl.BlockSpec((None, None, RT_DV, RT_DK), lambda h, b: (bmap(b), h, 0, 0))],
        out_shape=[jax.ShapeDtypeStruct((L, HEADS * RT_DV), F32),
                   jax.ShapeDtypeStruct((HEADS, RT_DV, RT_DK), F32),
                   jax.ShapeDtypeStruct((nB, HEADS, RT_DV, RT_DK), F32)],
        scratch_shapes=[pltpu.VMEM((RT_DV, RT_DK), F32)],
        compiler_params=_params("parallel", "arbitrary"),
    )(p, p, p, cosf, sinf, lg, s0)


def ret_scan_bwd(p, cosf, sinf, lg, s_blocks, d_o, ds_fin, prev, reverse, name):
    L = p.shape[0]
    C = SCAN_ROWS
    nB = L // C
    has_prev = prev is not None
    out_dt = BF16 if has_prev else F32

    def bmap(b):
        return b if reverse else (nB - 1 - b)

    def body(*refs):
        q_ref, k_ref, v_ref, cos_ref, sin_ref, lg_ref, sblk_ref, do_ref, dsf_ref = refs[:9]
        refs = refs[9:]
        if has_prev:
            pq_ref, pk_ref, pv_ref = refs[:3]
            refs = refs[3:]
        dq_ref, dk_ref, dv_ref, dlg_ref, ds0_ref, ds_scr = refs
        blk = pl.program_id(1)

        @pl.when(blk == 0)
        def _():
            ds_scr[...] = dsf_ref[...]
            dlg_ref[...] = jnp.zeros_like(dlg_ref)

        s_t = sblk_ref[...]
        ds_t = ds_scr[...]
        cosf, sinf = cos_ref[...], sin_ref[...]
        dmat, delta, pos, qdec, kdec, sdec = _ret_decays(lg_ref[...], reverse)
        q = _rope(q_ref[...] * Q_SCALE, cosf, sinf)
        k = _rope(k_ref[...], cosf, sinf)
        v = v_ref[...]
        d_o = do_ref[...]
        att_raw = _bdot(q, k, 1, 1)
        datt_m = _bdot(d_o, v, 1, 1) * dmat
        dqd = _bdot(d_o, s_t, 1, 0)
        dkd = _bdot(v, ds_t, 1, 0)
        dq = _bdot(datt_m, k) + dqd * qdec
        dk = _bdot(datt_m, q, 0, 0) + dkd * kdec
        dv = _bdot(att_raw * dmat, d_o, 0, 0) + _bdot(k * kdec, ds_t, 1, 1)
        ds_new = ds_t * sdec + _bdot(d_o, q * qdec, 0, 0)
        ds_scr[...] = ds_new
        t1 = jnp.sum(_rowsum(datt_m * att_raw * delta), axis=-1, keepdims=True)
        t23 = jnp.sum(_rowsum((pos + 1.0) * qdec * q * dqd + (C - 1.0 - pos) * kdec * k * dkd), axis=-1, keepdims=True)
        t4 = jnp.sum(_rowsum(ds_t * s_t * sdec), axis=-1, keepdims=True) * float(C)
        dlg_ref[...] += jnp.broadcast_to(t1 + t23 + t4, (1, RT_DK))
        if has_prev:
            dq = _rope_t(dq + pq_ref[...], cosf, sinf) * Q_SCALE
            dk = _rope_t(dk + pk_ref[...], cosf, sinf)
            dv = dv + pv_ref[...]
        dq_ref[...] = dq.astype(out_dt)
        dk_ref[...] = dk.astype(out_dt)
        dv_ref[...] = dv.astype(out_dt)

        @pl.when(blk == nB - 1)
        def _():
            ds0_ref[...] = ds_new

    def col(c0):
        return pl.BlockSpec((C, RT_DK), lambda h, b: (bmap(b), c0 + h))

    tab = pl.BlockSpec((C, RT_DK), lambda h, b: (bmap(b), 0))
    state = pl.BlockSpec((None, RT_DV, RT_DK), lambda h, b: (h, 0, 0))
    tk = pl.BlockSpec((C, RT_DK), lambda h, b: (bmap(b), h))
    tv = pl.BlockSpec((C, RT_DV), lambda h, b: (bmap(b), h))
    in_specs = [col(COL_RQ), col(COL_RK),
                pl.BlockSpec((C, RT_DV), lambda h, b: (bmap(b), COL_RV // 2 + h)),
                tab, tab, pl.BlockSpec((None, 1, RT_DV), lambda h, b: (h, 0, 0)),
                pl.BlockSpec((None, None, RT_DV, RT_DK), lambda h, b: (bmap(b), h, 0, 0)),
                tv, state]
    args = [p, p, p, cosf, sinf, lg, s_blocks, d_o, ds_fin]
    if has_prev:
        in_specs += [tk, tk, tv]
        args += list(prev)
    return pl.pallas_call(
        body, name=name,
        grid=(HEADS, nB),
        in_specs=in_specs,
        out_specs=[tk, tk, tv, pl.BlockSpec((None, 1, RT_DK), lambda h, b: (h, 0, 0)), state],
        out_shape=[jax.ShapeDtypeStruct((L, D), out_dt), jax.ShapeDtypeStruct((L, D), out_dt),
                   jax.ShapeDtypeStruct((L, HEADS * RT_DV), out_dt),
                   jax.ShapeDtypeStruct((HEADS, 1, RT_DK), F32),
                   jax.ShapeDtypeStruct((HEADS, RT_DV, RT_DK), F32)],
        scratch_shapes=[pltpu.VMEM((RT_DV, RT_DK), F32)],
        compiler_params=_params("parallel", "arbitrary"),
    )(*args)


def _silu_parts(h):
    s = _sigmoid(h)
    return h * s, s * (1.0 + h * (1.0 - s))


def _head_rms(o):
    outs, rs = [], []
    for h in range(HEADS):
        oh = o[:, h * HG_D:(h + 1) * HG_D]
        r = lax.rsqrt(_lanemean(oh * oh) + EPS)
        outs.append(oh * r)
        rs.append(r)
    return outs, rs


def _group_norm(o):
    outs, rs = [], []
    for h in range(HEADS):
        oh = o[:, h * RT_DV:(h + 1) * RT_DV]
        c = oh - _lanemean(oh)
        r = lax.rsqrt(_lanemean(c * c) + GN_EPS)
        outs.append(c * r)
        rs.append(r)
    return outs, rs


MIX_ROWS = 128


def _mix_specs(L):
    def t(w, c=0):
        return pl.BlockSpec((MIX_ROWS, w), lambda i: (i, c))

    return t


def mix_fwd(ohf, ohb, orf, orb, p, x, g1, hgw, w_pa, w_pb, w_out, name):
    L = x.shape[0]
    t = _mix_specs(L)

    def body(ohf_ref, ohb_ref, orf_ref, orb_ref, hg_ref, rg0_ref, rg1_ref, ga_ref, gb_ref, x_ref, g1_ref, hgw_ref,
             wpa_ref, wpb_ref, wout_ref, x1_ref, xmix_ref, merged_ref, ya_ref, yb_ref):
        nh, _ = _head_rms(ohf_ref[...] + ohb_ref[...])
        ya = jnp.concatenate(nh, axis=1) * hgw_ref[...] * _silu_parts(hg_ref[...])[0]
        gn, _ = _group_norm(orf_ref[...] + orb_ref[...])
        rg = jnp.concatenate([rg0_ref[...], rg1_ref[...]], axis=1)
        yb = jnp.concatenate(gn, axis=1) * _silu_parts(rg)[0]
        ya16, yb16 = ya.astype(BF16), yb.astype(BF16)
        merged = (_sigmoid(ga_ref[...]) * _dot(ya16, wpa_ref[...])
                  + _sigmoid(gb_ref[...]) * _dot(yb16, wpb_ref[...])).astype(BF16)
        x_mix = _dot(merged, wout_ref[...])
        x1_ref[...] = x_ref[...] + g1_ref[...] * x_mix
        xmix_ref[...] = x_mix
        merged_ref[...] = merged
        ya_ref[...] = ya16
        yb_ref[...] = yb16

    vec = pl.BlockSpec((1, D), lambda i: (0, 0))

    def full(a):
        return pl.BlockSpec(a.shape, lambda i: (0, 0))

    return pl.pallas_call(
        body, name=name,
        grid=(L // MIX_ROWS,),
        in_specs=[t(D), t(D), t(2 * D), t(2 * D), t(D, COL_HG // 8), t(D, COL_RG // 8), t(D, COL_RG // 8 + 1),
                  t(D, COL_GA // 8), t(D, COL_GB // 8), t(D), vec, vec, full(w_pa), full(w_pb), full(w_out)],
        out_specs=[t(D), t(D), t(D), t(D), t(2 * D)],
        out_shape=[jax.ShapeDtypeStruct((L, D), F32), jax.ShapeDtypeStruct((L, D), F32),
                   jax.ShapeDtypeStruct((L, D), BF16), jax.ShapeDtypeStruct((L, D), BF16),
                   jax.ShapeDtypeStruct((L, 2 * D), BF16)],
        compiler_params=_params("parallel"),
    )(ohf, ohb, orf, orb, p, p, p, p, p, x, g1, hgw, w_pa, w_pb, w_out)


def mix_bwd(dx1, x_mix, ya, yb, ohf, ohb, orf, orb, p, g1, hgw, w_pa, w_pb, w_out, name):
    L = dx1.shape[0]
    t = _mix_specs(L)

    def body(dx1_ref, xmix_ref, ya_ref, yb_ref, ohf_ref, ohb_ref, orf_ref, orb_ref, hg_ref, rg0_ref, rg1_ref,
             ga_ref, gb_ref, g1_ref, hgw_ref, wpa_ref, wpb_ref, wout_ref,
             dxm_ref, da_ref, db_ref, dga_ref, dgb_ref, dhg_ref, drg_ref, dohg_ref, dort_ref, sums_ref):
        @pl.when(pl.program_id(0) == 0)
        def _():
            sums_ref[...] = jnp.zeros_like(sums_ref)

        dx1 = dx1_ref[...]
        dxm = (g1_ref[...] * dx1).astype(BF16)
        dxm_ref[...] = dxm
        dmerged = _dot(dxm, wout_ref[...], 1, 1)
        a = _dot(ya_ref[...], wpa_ref[...])
        bm = _dot(yb_ref[...], wpb_ref[...])
        sa, sb = _sigmoid(ga_ref[...]), _sigmoid(gb_ref[...])
        d_a = (dmerged * sa).astype(BF16)
        d_b = (dmerged * sb).astype(BF16)
        da_ref[...] = d_a
        db_ref[...] = d_b
        dga_ref[...] = (dmerged * a * sa * (1.0 - sa)).astype(BF16)
        dgb_ref[...] = (dmerged * bm * sb * (1.0 - sb)).astype(BF16)
        dya = _dot(d_a, wpa_ref[...], 1, 1)
        dyb = _dot(d_b, wpb_ref[...], 1, 1)

        hgw = hgw_ref[...]
        silu_h, dsilu_h = _silu_parts(hg_ref[...])
        nh, rh = _head_rms(ohf_ref[...] + ohb_ref[...])
        n = jnp.concatenate(nh, axis=1)
        dhg_ref[...] = (dya * n * hgw * dsilu_h).astype(BF16)
        dn = dya * hgw * silu_h
        douts = []
        for h in range(HEADS):
            dnh = dn[:, h * HG_D:(h + 1) * HG_D]
            douts.append(rh[h] * (dnh - nh[h] * _lanemean(dnh * nh[h])))
        dohg_ref[...] = jnp.concatenate(douts, axis=1)

        rg = jnp.concatenate([rg0_ref[...], rg1_ref[...]], axis=1)
        silu_r, dsilu_r = _silu_parts(rg)
        gn, rr = _group_norm(orf_ref[...] + orb_ref[...])
        g = jnp.concatenate(gn, axis=1)
        drg_ref[...] = (dyb * g * dsilu_r).astype(BF16)
        dgn = dyb * silu_r
        douts = []
        for h in range(HEADS):
            dgh = dgn[:, h * RT_DV:(h + 1) * RT_DV]
            douts.append(rr[h] * (dgh - _lanemean(dgh) - gn[h] * _lanemean(dgh * gn[h])))
        dort_ref[...] = jnp.concatenate(douts, axis=1)

        sums_ref[0:1, :] += _rowsum(dx1 * xmix_ref[...])
        sums_ref[1:2, :] += _rowsum(dya * n * silu_h)

    vec = pl.BlockSpec((1, D), lambda i: (0, 0))

    def full(a):
        return pl.BlockSpec(a.shape, lambda i: (0, 0))

    bf = functools.partial(jax.ShapeDtypeStruct, dtype=BF16)
    return pl.pallas_call(
        body, name=name,
        grid=(L // MIX_ROWS,),
        in_specs=[t(D), t(D), t(D), t(2 * D), t(D), t(D), t(2 * D), t(2 * D),
                  t(D, COL_HG // 8), t(D, COL_RG // 8), t(D, COL_RG // 8 + 1), t(D, COL_GA // 8), t(D, COL_GB // 8),
                  vec, vec, full(w_pa), full(w_pb), full(w_out)],
        out_specs=[t(D), t(D), t(D), t(D), t(D), t(D), t(2 * D), t(D), t(2 * D),
                   pl.BlockSpec((8, D), lambda i: (0, 0))],
        out_shape=[bf((L, D)), bf((L, D)), bf((L, D)), bf((L, D)), bf((L, D)), bf((L, D)), bf((L, 2 * D)),
                   jax.ShapeDtypeStruct((L, D), F32), jax.ShapeDtypeStruct((L, 2 * D), F32),
                   jax.ShapeDtypeStruct((8, D), F32)],
        compiler_params=_params("arbitrary"),
    )(dx1, x_mix, ya, yb, ohf, ohb, orf, orb, p, p, p, p, p, g1, hgw, w_pa, w_pb, w_out)


FFN_ROWS = 256


def ffn_fwd(x1, target, nw2, sh2, sc2, g2, fw, wg, wu, wd, name):
    L = x1.shape[0]
    tm = FFN_ROWS

    def body(x1_ref, tgt_ref, nw2_ref, sh2_ref, sc2_ref, g2_ref, fw_ref, wg_ref, wu_ref, wd_ref,
             hx2_ref, g_ref, u_ref, h_ref, f_ref, dx2_ref, sums_ref, hx_scr, acc):
        i, j = pl.program_id(0), pl.program_id(1)

        @pl.when((i == 0) & (j == 0))
        def _():
            sums_ref[...] = jnp.zeros_like(sums_ref)

        @pl.when(j == 0)
        def _():
            xv = x1_ref[...]
            n = xv * lax.rsqrt(_lanemean(xv * xv) + EPS) * nw2_ref[...]
            h = (n * (1.0 + sc2_ref[...]) + sh2_ref[...]).astype(BF16)
            hx_scr[...] = h
            hx2_ref[...] = h
            acc[...] = jnp.zeros_like(acc)

        hx = hx_scr[...]
        g = _dot(hx, wg_ref[...])
        u = _dot(hx, wu_ref[...])
        hh = (_silu_parts(g)[0] * u).astype(BF16)
        g_ref[...] = g
        u_ref[...] = u
        h_ref[...] = hh
        acc[...] += _dot(hh, wd_ref[...])

        @pl.when(j == N_SHARD - 1)
        def _():
            f = acc[...]
            f_ref[...] = f
            x2 = x1_ref[...] + g2_ref[...] * f
            r = lax.rsqrt(_lanemean(x2 * x2) + EPS)
            fw = fw_ref[...]
            e = x2 * r * fw - tgt_ref[...]
            dy = e * (1.0 / D)
            dyw = dy * fw
            dx2_ref[...] = r * dyw - x2 * (r * r * r) * _lanemean(dyw * x2)
            sums_ref[0:1, :] += _rowsum(dy * x2 * r)
            sums_ref[1:2, :] += _rowsum(e * e) * (0.5 / D)

    row = pl.BlockSpec((tm, D), lambda i, j: (i, 0))
    vec = pl.BlockSpec((1, D), lambda i, j: (0, 0))
    sh = pl.BlockSpec((None, tm, FF_SH), lambda i, j: (j, i, 0))
    return pl.pallas_call(
        body, name=name,
        grid=(L // tm, N_SHARD),
        in_specs=[row, row, vec, vec, vec, vec, vec,
                  pl.BlockSpec((None, D, FF_SH), lambda i, j: (j, 0, 0)),
                  pl.BlockSpec((None, D, FF_SH), lambda i, j: (j, 0, 0)),
                  pl.BlockSpec((None, FF_SH, D), lambda i, j: (j, 0, 0))],
        out_specs=[row, sh, sh, sh, row, row, pl.BlockSpec((8, D), lambda i, j: (0, 0))],
        out_shape=[jax.ShapeDtypeStruct((L, D), BF16),
                   jax.ShapeDtypeStruct((N_SHARD, L, FF_SH), F32), jax.ShapeDtypeStruct((N_SHARD, L, FF_SH), F32),
                   jax.ShapeDtypeStruct((N_SHARD, L, FF_SH), BF16),
                   jax.ShapeDtypeStruct((L, D), F32), jax.ShapeDtypeStruct((L, D), F32),
                   jax.ShapeDtypeStruct((8, D), F32)],
        scratch_shapes=[pltpu.VMEM((tm, D), BF16), pltpu.VMEM((tm, D), F32)],
        compiler_params=_params("arbitrary", "arbitrary"),
    )(x1, target, nw2, sh2, sc2, g2, fw, wg, wu, wd)


def ffn_bwd(dx2, x1, f, g, u, nw2, sc2, g2, wg, wu, wd, name):
    L = x1.shape[0]
    tm = FFN_ROWS

    def body(dx2_ref, x1_ref, f_ref, g_ref, u_ref, nw2_ref, sc2_ref, g2_ref, wg_ref, wu_ref, wd_ref,
             df_ref, dg_ref, du_ref, dx1_ref, sums_ref, df_scr, acc):
        i, j = pl.program_id(0), pl.program_id(1)

        @pl.when((i == 0) & (j == 0))
        def _():
            sums_ref[...] = jnp.zeros_like(sums_ref)

        @pl.when(j == 0)
        def _():
            dx2 = dx2_ref[...]
            df = (g2_ref[...] * dx2).astype(BF16)
            df_scr[...] = df
            df_ref[...] = df
            sums_ref[0:1, :] += _rowsum(dx2 * f_ref[...])
            acc[...] = jnp.zeros_like(acc)

        dh = _dot(df_scr[...], wd_ref[...], 1, 1)
        gv, uv = g_ref[...], u_ref[...]
        silu_g, dsilu_g = _silu_parts(gv)
        dg = (dh * uv * dsilu_g).astype(BF16)
        du = (dh * silu_g).astype(BF16)
        dg_ref[...] = dg
        du_ref[...] = du
        acc[...] += _dot(dg, wg_ref[...], 1, 1) + _dot(du, wu_ref[...], 1, 1)

        @pl.when(j == N_SHARD - 1)
        def _():
            dhx = acc[...]
            xv = x1_ref[...]
            r = lax.rsqrt(_lanemean(xv * xv) + EPS)
            n0 = xv * r
            nw = nw2_ref[...]
            dn2 = dhx * (1.0 + sc2_ref[...])
            dn0 = dn2 * nw
            dx1_ref[...] = dx2_ref[...] + r * (dn0 - n0 * _lanemean(dn0 * n0))
            sums_ref[1:2, :] += _rowsum(dhx)
            sums_ref[2:3, :] += _rowsum(dhx * n0 * nw)
            sums_ref[3:4, :] += _rowsum(dn2 * n0)

    row = pl.BlockSpec((tm, D), lambda i, j: (i, 0))
    vec = pl.BlockSpec((1, D), lambda i, j: (0, 0))
    sh = pl.BlockSpec((None, tm, FF_SH), lambda i, j: (j, i, 0))
    return pl.pallas_call(
        body, name=name,
        grid=(L // tm, N_SHARD),
        in_specs=[row, row, row, sh, sh, vec, vec, vec,
                  pl.BlockSpec((None, D, FF_SH), lambda i, j: (j, 0, 0)),
                  pl.BlockSpec((None, D, FF_SH), lambda i, j: (j, 0, 0)),
                  pl.BlockSpec((None, FF_SH, D), lambda i, j: (j, 0, 0))],
        out_specs=[row, sh, sh, row, pl.BlockSpec((8, D), lambda i, j: (0, 0))],
        out_shape=[jax.ShapeDtypeStruct((L, D), BF16),
                   jax.ShapeDtypeStruct((N_SHARD, L, FF_SH), BF16), jax.ShapeDtypeStruct((N_SHARD, L, FF_SH), BF16),
                   jax.ShapeDtypeStruct((L, D), F32), jax.ShapeDtypeStruct((8, D), F32)],
        scratch_shapes=[pltpu.VMEM((tm, D), BF16), pltpu.VMEM((tm, D), F32)],
        compiler_params=_params("arbitrary", "arbitrary"),
    )(dx2, x1, f, g, u, nw2, sc2, g2, wg, wu, wd)


def matmul_tn(a, b, name, acc_init=None):
    na, K, M = a.shape
    nb, _, N = b.shape
    n = max(na, nb)
    tk = min(512, K)
    tn = N if N <= 1024 else N // 2
    nk = K // tk
    has_init = acc_init is not None

    def body(*refs):
        if has_init:
            a_ref, b_ref, init_ref, o_ref = refs
        else:
            a_ref, b_ref, o_ref = refs
        kk = pl.program_id(2)

        @pl.when(kk == 0)
        def _():
            o_ref[...] = init_ref[...] if has_init else jnp.zeros_like(o_ref)

        o_ref[...] += _dot(a_ref[...], b_ref[...], 0, 0)

    out_spec = pl.BlockSpec((None, M, tn), lambda s, j, kk: (s, 0, j))
    in_specs = [pl.BlockSpec((None, tk, M), lambda s, j, kk: (s if na > 1 else 0, kk, 0)),
                pl.BlockSpec((None, tk, tn), lambda s, j, kk: (s if nb > 1 else 0, kk, j))]
    args = [a, b]
    if has_init:
        in_specs.append(out_spec)
        args.append(acc_init)
    return pl.pallas_call(
        body, name=name,
        grid=(n, N // tn, nk),
        in_specs=in_specs,
        out_specs=out_spec,
        out_shape=jax.ShapeDtypeStruct((n, M, N), F32),
        compiler_params=_params("parallel", "parallel", "arbitrary"),
    )(*args)


def dhx_normbwd(dp4, w4, x, dx_res, nw, sc, name):
    L = x.shape[0]
    tm = min(512, L)
    tn = IN_SH // 2

    def body(dp_ref, w_ref, x_ref, res_ref, nw_ref, sc_ref, dx_ref, sums_ref, acc):
        i, k, j = pl.program_id(0), pl.program_id(1), pl.program_id(2)
        first = (k == 0) & (j == 0)

        @pl.when((i == 0) & first)
        def _():
            sums_ref[...] = jnp.zeros_like(sums_ref)

        @pl.when(first)
        def _():
            acc[...] = jnp.zeros_like(acc)

        acc[...] += _dot(dp_ref[...], w_ref[...], 1, 1)

        @pl.when((k == N_SHARD - 1) & (j == 1))
        def _():
            dhx = acc[...]
            xv = x_ref[...]
            r = lax.rsqrt(_lanemean(xv * xv) + EPS)
            n0 = xv * r
            nw = nw_ref[...]
            dn = dhx * (1.0 + sc_ref[...])
            dn0 = dn * nw
            dx_ref[...] = res_ref[...] + r * (dn0 - n0 * _lanemean(dn0 * n0))
            sums_ref[0:1, :] += _rowsum(dhx)
            sums_ref[1:2, :] += _rowsum(dhx * n0 * nw)
            sums_ref[2:3, :] += _rowsum(dn * n0)

    row = pl.BlockSpec((tm, D), lambda i, k, j: (i, 0))
    vec = pl.BlockSpec((1, D), lambda i, k, j: (0, 0))
    return pl.pallas_call(
        body, name=name,
        grid=(L // tm, N_SHARD, 2),
        in_specs=[pl.BlockSpec((None, tm, tn), lambda i, k, j: (k, i, j)),
                  pl.BlockSpec((None, D, tn), lambda i, k, j: (k, 0, j)),
                  row, row, vec, vec],
        out_specs=[row, pl.BlockSpec((8, D), lambda i, k, j: (0, 0))],
        out_shape=[jax.ShapeDtypeStruct((L, D), F32), jax.ShapeDtypeStruct((8, D), F32)],
        scratch_shapes=[pltpu.VMEM((tm, D), F32)],
        compiler_params=_params("arbitrary", "arbitrary", "arbitrary"),
    )(dp4, w4, x, dx_res, nw, sc)


SMALL_ROWS = 24


def _rope_tables(L):
    rows = L // 64
    row = jnp.repeat(jnp.arange(rows, dtype=F32), 64)
    col = jnp.tile(jnp.arange(64, dtype=F32), rows)
    freqs = 10000.0 ** (-jnp.arange(RT_DK // 4, dtype=F32) / (RT_DK // 4))
    ang = jnp.concatenate([row[:, None] * freqs, col[:, None] * freqs], axis=-1)
    cos, sin = jnp.cos(ang), jnp.sin(ang)
    return jnp.concatenate([cos, cos], axis=1), jnp.concatenate([-sin, sin], axis=1)


def _shard_major(pieces):
    dp = jnp.concatenate(pieces, axis=1)
    return dp.reshape(dp.shape[0], N_SHARD, IN_SH).transpose(1, 0, 2)


def _lane0(a):
    return a[:, 0, 0]


def _pack_small(rows):
    out = [r.reshape(1, D) for r in rows]
    out += [jnp.zeros((1, D), F32)] * (SMALL_ROWS - len(out))
    return jnp.concatenate(out, axis=0)


def local_step(x, ctx, target, mod_x, mod_c, lb_f, lb_b, lg_f, lg_b, nw1, nw2, hgw, fw, w):
    L, Lc = x.shape[0], ctx.shape[0]
    sh1, sc1, g1, sh2, sc2, g2 = (mod_x[i:i + 1] for i in range(6))
    sh1c, sc1c = mod_c[0:1], mod_c[1:2]
    cosf, sinf = _rope_tables(L)
    cosc, sinc = jnp.ones((Lc, RT_DK), F32), jnp.zeros((Lc, RT_DK), F32)
    zero_h = jnp.zeros((HEADS, HG_D, HG_D), F32)
    zero_r = jnp.zeros((HEADS, RT_DV, RT_DK), F32)

    pc, hxc = normmod_matmul(ctx, nw1, sh1c, sc1c, w["w_in"], "ctx_in_proj")
    _, s_hf, cb_hf = hgrn_scan_fwd(pc, lb_f, zero_h, COL_HFF, False, "ctx_hgrn_f")
    _, s_hb, cb_hb = hgrn_scan_fwd(pc, lb_b, zero_h, COL_HFB, True, "ctx_hgrn_b")
    _, s_rf, cb_rf = ret_scan_fwd(pc, cosc, sinc, lg_f, zero_r, False, "ctx_ret_f")
    _, s_rb, cb_rb = ret_scan_fwd(pc, cosc, sinc, lg_b, zero_r, True, "ctx_ret_b")
    p, hx = normmod_matmul(x, nw1, sh1, sc1, w["w_in"], "in_proj")
    ohf, _, xb_hf = hgrn_scan_fwd(p, lb_f, s_hf, COL_HFF, False, "hgrn_f")
    ohb, _, xb_hb = hgrn_scan_fwd(p, lb_b, s_hb, COL_HFB, True, "hgrn_b")
    orf, _, xb_rf = ret_scan_fwd(p, cosf, sinf, lg_f, s_rf, False, "ret_f")
    orb, _, xb_rb = ret_scan_fwd(p, cosf, sinf, lg_b, s_rb, True, "ret_b")
    x1, x_mix, merged, ya, yb = mix_fwd(ohf, ohb, orf, orb, p, x, g1, hgw, w["w_pa"], w["w_pb"], w["w_out"], "mix_fwd")
    hx2, gg, uu, hh, ff, dx2, sums_f = ffn_fwd(x1, target, nw2, sh2, sc2, g2, fw, w["wg"], w["wu"], w["wd"], "ffn_fwd")

    d_f, d_g, d_u, dx1, sums_fb = ffn_bwd(dx2, x1, ff, gg, uu, nw2, sc2, g2, w["wg"], w["wu"], w["wd"], "ffn_bwd")
    grads = {
        "wg": matmul_tn(hx2[None], d_g, "dw_ffn_gate"),
        "wu": matmul_tn(hx2[None], d_u, "dw_ffn_up"),
        "wd": matmul_tn(hh, d_f[None], "dw_ffn_down"),
    }
    dxm, d_a, d_b, dga, dgb, dhg, drg, dohg, dort, sums_m = mix_bwd(
        dx1, x_mix, ya, yb, ohf, ohb, orf, orb, p, g1, hgw, w["w_pa"], w["w_pb"], w["w_out"], "mix_bwd")
    grads["w_out"] = matmul_tn(merged[None], dxm[None], "dw_out").reshape(N_SHARD, D // N_SHARD, D)
    grads["w_pa"] = matmul_tn(ya[None], d_a[None], "dw_proj_hgrn").reshape(N_SHARD, D // N_SHARD, D)
    grads["w_pb"] = matmul_tn(yb[None], d_b[None], "dw_proj_ret").reshape(N_SHARD, 2 * D // N_SHARD, D)

    rq1, rk1, rv1, dlgf_x, ds_rf = ret_scan_bwd(p, cosf, sinf, lg_f, xb_rf, dort, zero_r, None, False, "ret_f_bwd")
    drq, drk, drv, dlgb_x, ds_rb = ret_scan_bwd(p, cosf, sinf, lg_b, xb_rb, dort, zero_r, (rq1, rk1, rv1), True, "ret_b_bwd")
    hq1, dzf, hv1, dlbf_x, ds_hf = hgrn_scan_bwd(p, lb_f, xb_hf, dohg, zero_h, None, COL_HFF, False, "hgrn_f_bwd")
    dhq, dzb, dhv, dlbb_x, ds_hb = hgrn_scan_bwd(p, lb_b, xb_hb, dohg, zero_h, (hq1, hv1), COL_HFB, True, "hgrn_b_bwd")
    dp4 = _shard_major([dhq, dzf, dzb, dhv, dhg, drq, drk, drv, drg, dga, dgb])
    dx, sums_x = dhx_normbwd(dp4, w["w_in"], x, dx1, nw1, sc1, "dx_in_proj")
    dw_in = matmul_tn(hx[None], dp4, "dw_in")

    zc = jnp.zeros((Lc, D), F32)
    zc2 = jnp.zeros((Lc, 2 * D), F32)
    crq1, crk1, crv1, dlgf_c, _ = ret_scan_bwd(pc, cosc, sinc, lg_f, cb_rf, zc2, ds_rf, None, False, "ctx_ret_f_bwd")
    cdrq, cdrk, cdrv, dlgb_c, _ = ret_scan_bwd(pc, cosc, sinc, lg_b, cb_rb, zc2, ds_rb, (crq1, crk1, crv1), True, "ctx_ret_b_bwd")
    chq1, cdzf, chv1, dlbf_c, _ = hgrn_scan_bwd(pc, lb_f, cb_hf, zc, ds_hf, None, COL_HFF, False, "ctx_hgrn_f_bwd")
    cdhq, cdzb, cdhv, dlbb_c, _ = hgrn_scan_bwd(pc, lb_b, cb_hb, zc, ds_hb, (chq1, chv1), COL_HFB, True, "ctx_hgrn_b_bwd")
    zb = jnp.zeros((Lc, D), BF16)
    zb2 = jnp.zeros((Lc, 2 * D), BF16)
    dpc4 = _shard_major([cdhq, cdzf, cdzb, cdhv, zb, cdrq, cdrk, cdrv, zb2, zb, zb])
    _, sums_c = dhx_normbwd(dpc4, w["w_in"], ctx, zc, nw1, sc1c, "dctx_in_proj")
    grads["w_in"] = matmul_tn(hxc[None], dpc4, "dw_in_ctx", acc_init=dw_in)

    def lg_row(f, b):
        return jnp.concatenate([_lane0(f), _lane0(b), jnp.zeros((D - 2 * HEADS,), F32)])

    small = _pack_small([
        sums_x[0], sums_x[1], sums_m[0], sums_fb[1], sums_fb[2], sums_fb[0],
        sums_c[0], sums_c[1],
        sums_x[2], sums_c[2], sums_fb[3], sums_m[1], sums_f[0],
        dlbf_x, dlbf_c, dlbb_x, dlbb_c,
        lg_row(dlgf_x, dlgb_x), lg_row(dlgf_c, dlgb_c),
        sums_f[1],
    ])
    return dx, grads, small


MESH = pl.DeviceIdType.MESH
ANY = pl.BlockSpec(memory_space=pl.ANY)
N_DEV = 8


def _place():
    return lax.axis_index("x"), lax.axis_index("y"), lax.axis_index("c")


def _other_chips(x, y):
    return [(1 - x, y), (x, 1 - y), (1 - x, 1 - y)]


def allgather8(xs, name):
    m, n = xs.shape

    def body(x_ref, out_ref, send_sems, recv_sems, local_sem):
        x, y, c = _place()
        me, sibling = (x, y, c), (x, y, 1 - c)
        chips = _other_chips(x, y)

        def rows(px, py, pc):
            return out_ref.at[pl.ds((4 * px + 2 * py + pc) * m, m), :]

        def copy(k, block, to, src=None):
            return pltpu.make_async_remote_copy(
                src_ref=rows(*block) if src is None else src, dst_ref=rows(*block),
                send_sem=send_sems.at[k], recv_sem=recv_sems.at[k], device_id=to, device_id_type=MESH)

        mine = pltpu.make_async_copy(x_ref, rows(*me), local_sem)
        mine.start()
        first = [copy(0, me, sibling, src=x_ref)]
        first += [copy(1 + j, me, (*chip, c), src=x_ref) for j, chip in enumerate(chips)]
        for cp in first:
            cp.start()
        passed = [copy(4 + j, (*chip, c), sibling) for j, chip in enumerate(chips)]
        for j, chip in enumerate(chips):
            copy(1 + j, (*chip, c), me).wait_recv()
            passed[j].start()
        copy(0, sibling, me).wait_recv()
        for j, chip in enumerate(chips):
            copy(4 + j, (*chip, 1 - c), me).wait_recv()
        for cp in first + passed:
            cp.wait_send()
        mine.wait()

    return pl.pallas_call(
        body, name=name,
        out_shape=jax.ShapeDtypeStruct((N_DEV * m, n), xs.dtype),
        in_specs=[pl.BlockSpec(memory_space=pltpu.VMEM)],
        out_specs=pl.BlockSpec(memory_space=pltpu.VMEM),
        scratch_shapes=[pltpu.SemaphoreType.DMA((7,)), pltpu.SemaphoreType.DMA((7,)), pltpu.SemaphoreType.DMA],
    )(xs)


def gather_weights(bufs, name):
    n = len(bufs)

    def body(*refs):
        outs = refs[n:2 * n]
        send_sems, recv_sems = refs[2 * n:]
        x, y, c = _place()
        chips = _other_chips(x, y)

        def half(i, chip_xy, core):
            h = bufs[i].shape[1] // 2
            return outs[i].at[2 * chip_xy[0] + chip_xy[1], pl.ds(pl.multiple_of(core * h, 16), h), :]

        def copy(i, k, piece, to):
            return pltpu.make_async_remote_copy(
                src_ref=piece, dst_ref=piece, send_sem=send_sems.at[6 * i + k], recv_sem=recv_sems.at[6 * i + k],
                device_id=to, device_id_type=MESH)

        started = []
        for i in range(n):
            for j, chip in enumerate(chips):
                cp = copy(i, j, half(i, (x, y), c), (*chip, c))
                cp.start()
                started.append(cp)
        for i in range(n):
            for j, chip in enumerate(chips):
                copy(i, j, half(i, chip, c), (*chip, c)).wait_recv()
                fw = copy(i, 3 + j, half(i, chip, c), (x, y, 1 - c))
                fw.start()
                started.append(fw)
        for i in range(n):
            for j, chip in enumerate(chips):
                copy(i, 3 + j, half(i, chip, 1 - c), (x, y, 1 - c)).wait_recv()
        for cp in started:
            cp.wait_send()

    return pl.pallas_call(
        body, name=name,
        out_shape=[jax.ShapeDtypeStruct(b.shape, b.dtype) for b in bufs],
        in_specs=[ANY] * n, out_specs=[ANY] * n,
        input_output_aliases={i: i for i in range(n)},
        scratch_shapes=[pltpu.SemaphoreType.DMA((6 * n,)), pltpu.SemaphoreType.DMA((6 * n,))],
    )(*bufs)


def rs_to_sibling(payloads, name):
    n = len(payloads)

    def body(*refs):
        ins, outs = refs[:n], refs[n:2 * n]
        send_sems, recv_sems = refs[2 * n:]
        x, y, c = _place()
        copies = []
        for i in range(n):
            cp = pltpu.make_async_remote_copy(src_ref=ins[i], dst_ref=outs[i], send_sem=send_sems.at[i],
                                              recv_sem=recv_sems.at[i], device_id=(x, y, 1 - c), device_id_type=MESH)
            cp.start()
            copies.append(cp)
        for cp in copies:
            cp.wait()

    return pl.pallas_call(
        body, name=name,
        out_shape=[jax.ShapeDtypeStruct(g.shape, g.dtype) for g in payloads],
        in_specs=[ANY] * n, out_specs=[ANY] * n,
        scratch_shapes=[pltpu.SemaphoreType.DMA((n,)), pltpu.SemaphoreType.DMA((n,))],
    )(*payloads)


def rs_to_chips(parts, name):
    n = len(parts)

    def body(*refs):
        ins, outs = refs[:n], refs[n:2 * n]
        send_sems, recv_sems = refs[2 * n:]
        x, y, c = _place()
        copies = []
        for i in range(n):
            for j, (px, py) in enumerate(_other_chips(x, y)):
                cp = pltpu.make_async_remote_copy(
                    src_ref=ins[i].at[2 * px + py], dst_ref=outs[i].at[j], send_sem=send_sems.at[3 * i + j],
                    recv_sem=recv_sems.at[3 * i + j], device_id=(px, py, c), device_id_type=MESH)
                cp.start()
                copies.append(cp)
        for cp in copies:
            cp.wait()

    return pl.pallas_call(
        body, name=name,
        out_shape=[jax.ShapeDtypeStruct((3,) + a.shape[1:], a.dtype) for a in parts],
        in_specs=[ANY] * n, out_specs=[ANY] * n,
        scratch_shapes=[pltpu.SemaphoreType.DMA((3 * n,)), pltpu.SemaphoreType.DMA((3 * n,))],
    )(*parts)


def rs_join_halves(fulls, name):
    n = len(fulls)

    def body(*refs):
        outs = refs[n:2 * n]
        send_sems, recv_sems = refs[2 * n:]
        x, y, c = _place()

        def copy(i, core):
            h = fulls[i].shape[0] // 2
            rows = outs[i].at[pl.ds(pl.multiple_of(core * h, 8), h), :]
            return pltpu.make_async_remote_copy(src_ref=rows, dst_ref=rows, send_sem=send_sems.at[i],
                                                recv_sem=recv_sems.at[i], device_id=(x, y, 1 - c), device_id_type=MESH)

        sent = [copy(i, c) for i in range(n)]
        for cp in sent:
            cp.start()
        for i in range(n):
            copy(i, 1 - c).wait_recv()
        for cp in sent:
            cp.wait_send()

    return pl.pallas_call(
        body, name=name,
        out_shape=[jax.ShapeDtypeStruct(a.shape, a.dtype) for a in fulls],
        in_specs=[ANY] * n, out_specs=[ANY] * n,
        input_output_aliases={i: i for i in range(n)},
        scratch_shapes=[pltpu.SemaphoreType.DMA((n,)), pltpu.SemaphoreType.DMA((n,))],
    )(*fulls)


def _row_tile(rows, cols, limit_bytes=2 * 1024 * 1024, mult=8):
    best = mult
    for t in range(mult, rows + 1, mult):
        if rows % t == 0 and t * cols * 4 <= limit_bytes:
            best = t
    return best


def rs_add_sibling(g, recv, c, name):
    _, R, C = g.shape
    h = R // 2
    tr = _row_tile(h, C, mult=16)
    nt = h // tr

    def body(c_ref, g_ref, r_ref, o_ref, o16_ref):
        s = g_ref[...] + r_ref[...].astype(F32)
        o_ref[...] = s
        o16_ref[...] = s.astype(BF16)

    blk = pl.BlockSpec((None, tr, C), lambda k, i, c_ref: (k, i, 0))
    return pl.pallas_call(
        body, name=name,
        grid_spec=pltpu.PrefetchScalarGridSpec(
            num_scalar_prefetch=1, grid=(N_SHARD, nt),
            in_specs=[pl.BlockSpec((None, tr, C), lambda k, i, c_ref: (k, c_ref[0] * nt + i, 0)), blk],
            out_specs=[blk, blk]),
        out_shape=[jax.ShapeDtypeStruct((N_SHARD, h, C), F32), jax.ShapeDtypeStruct((N_SHARD, h, C), BF16)],
        compiler_params=_params("parallel", "parallel"),
    )(c, g, recv)


def rs_add_chips(part, recv, place, name):
    _, h, C = part.shape
    tr = _row_tile(h, C, mult=16)
    nt = h // tr

    def body(k_ref, p_ref, r_ref, o_ref):
        o_ref[...] = ((p_ref[...] + r_ref[0].astype(F32)) + r_ref[1].astype(F32)) + r_ref[2].astype(F32)

    return pl.pallas_call(
        body, name=name,
        grid_spec=pltpu.PrefetchScalarGridSpec(
            num_scalar_prefetch=1, grid=(nt,),
            in_specs=[pl.BlockSpec((None, tr, C), lambda i, k_ref: (k_ref[0], i, 0)),
                      pl.BlockSpec((3, tr, C), lambda i, k_ref: (0, i, 0))],
            out_specs=pl.BlockSpec((tr, C), lambda i, k_ref: (k_ref[1] * nt + i, 0))),
        out_shape=jax.ShapeDtypeStruct((2 * h, C), F32),
        compiler_params=_params("parallel"),
    )(place, part, recv)


def _adamw_math(w, g, m, v):
    m = ADAM_B1 * m + (1.0 - ADAM_B1) * g
    v = ADAM_B2 * v + (1.0 - ADAM_B2) * (g * g)
    m_hat = m / (1.0 - ADAM_B1 ** ADAM_STEP)
    v_hat = v / (1.0 - ADAM_B2 ** ADAM_STEP)
    delta = -ADAM_LR * (m_hat / (jnp.sqrt(v_hat) + ADAM_EPS) + ADAM_WD * w)
    return delta, m, v


def adamw(w, g, m, v, name):
    R, C = w.shape
    tr = _row_tile(R, C, 1024 * 1024)

    def body(w_ref, g_ref, m_ref, v_ref, d_ref, nm_ref, nv_ref):
        d_ref[...], nm_ref[...], nv_ref[...] = _adamw_math(w_ref[...], g_ref[...], m_ref[...], v_ref[...])

    blk = pl.BlockSpec((tr, C), lambda i: (i, 0))
    return pl.pallas_call(
        body, name=name, grid=(R // tr,), in_specs=[blk] * 4, out_specs=[blk] * 3,
        out_shape=[jax.ShapeDtypeStruct((R, C), F32)] * 3,
        compiler_params=_params("parallel"),
    )(w, g, m, v)


MOD_SH = 6 * D // N_SHARD
PK_ROWS = 16


def mod_fwd(call16, w_sh, b_sh, name):
    def body(c_ref, w_ref, b_ref, o_ref):
        o_ref[...] = _dot(_silu_parts(c_ref[...])[0], w_ref[...], prec=HI) + b_ref[...]

    return pl.pallas_call(body, name=name, out_shape=jax.ShapeDtypeStruct((16, MOD_SH), F32),
                          compiler_params=_params())(call16, w_sh, b_sh)


def prep_small(lbf2, lbb2, theta_row, name):
    def body(f_ref, b_ref, t_ref, lbf_ref, lbb_ref, lg_ref):
        lbf_ref[...] = _sigmoid(f_ref[0:1, :] - f_ref[1:2, :])
        lbb_ref[...] = _sigmoid(b_ref[0:1, :] - b_ref[1:2, :])
        t = t_ref[...]
        lg_ref[...] = jnp.minimum(t, 0.0) - jnp.log(1.0 + jnp.exp(-jnp.abs(t)))

    row = jax.ShapeDtypeStruct((1, D), F32)
    return pl.pallas_call(body, name=name, out_shape=[row, row, row], compiler_params=_params())(lbf2, lbb2, theta_row)


def small_grads(g3, lbf, lbb, theta_row, name):
    def body(g_ref, lbf_ref, lbb_ref, t_ref, pk_ref, aux_ref):
        s = g_ref[0]
        for d in range(1, N_DEV):
            s = s + g_ref[d]
        pk_ref[...] = jnp.zeros_like(pk_ref)
        aux_ref[...] = jnp.zeros_like(aux_ref)
        pk_ref[1:7, :] = s[0:6]
        pk_ref[1:3, :] += s[6:8]
        pk_ref[7:8, :] = s[8:9] + s[9:10]
        pk_ref[8:9, :] = s[10:11]
        lbf, lbb = lbf_ref[...], lbb_ref[...]
        daf = (s[13:14] + s[14:15]) * lbf * (1.0 - lbf)
        dab = (s[15:16] + s[16:17]) * lbb * (1.0 - lbb)
        pk_ref[9:10, :] = daf
        pk_ref[10:11, :] = -daf
        pk_ref[11:12, :] = dab
        pk_ref[12:13, :] = -dab
        pk_ref[13:14, :] = s[11:12]
        pk_ref[14:15, :] = (s[17:18] + s[18:19]) * _sigmoid(-t_ref[...])
        pk_ref[15:16, :] = s[12:13]
        aux_ref[0:2, :] = s[6:8]
        aux_ref[2:3, :] = jnp.broadcast_to(jnp.sum(s[19:20], axis=-1, keepdims=True), (1, D))

    return pl.pallas_call(body, name=name,
                          out_shape=[jax.ShapeDtypeStruct((PK_ROWS, D), F32), jax.ShapeDtypeStruct((8, D), F32)],
                          compiler_params=_params())(g3, lbf, lbb, theta_row)


def mod_bwd(call16, dmod_sh, w_sh, name):
    def body(c_ref, d_ref, w_ref, dw_ref, ds_ref):
        dm = d_ref[...]
        dw_ref[...] = _dot(_silu_parts(c_ref[...])[0], dm, 0, 0, prec=HI)
        ds_ref[...] = jnp.zeros_like(ds_ref)
        ds_ref[0:1, :] = _dot(dm[8:9, :], w_ref[...], 1, 1, prec=HI)

    return pl.pallas_call(body, name=name,
                          out_shape=[jax.ShapeDtypeStruct((D, MOD_SH), F32), jax.ShapeDtypeStruct((8, D), F32)],
                          compiler_params=_params())(call16, dmod_sh, w_sh)


def adamw_small(g4, pk_g, pk_w, pk_m, pk_v, name):
    def body(g4_ref, g_ref, w_ref, m_ref, v_ref, go_ref, d_ref, nm_ref, nv_ref):
        w = w_ref[...]
        ds = ((g4_ref[0:1, :] + g4_ref[16:17, :]) + g4_ref[32:33, :]) + g4_ref[48:49, :]
        row = lax.broadcasted_iota(jnp.int32, (PK_ROWS, D), 0)
        g = jnp.where(row == 0, ds * _silu_parts(w[0:1, :])[1], g_ref[...])
        go_ref[...] = g
        d_ref[...], nm_ref[...], nv_ref[...] = _adamw_math(w, g, m_ref[...], v_ref[...])

    pk = jax.ShapeDtypeStruct((PK_ROWS, D), F32)
    return pl.pallas_call(body, name=name, out_shape=[pk, pk, pk, pk], compiler_params=_params())(g4, pk_g, pk_w, pk_m, pk_v)


def _pack_params(c_ctx, b_mod, n1, n2, lbf, lbb, hgn, th_f, th_b, fin):
    theta = jnp.concatenate([th_f.reshape(HEADS), th_b.reshape(HEADS), jnp.zeros((D - 2 * HEADS,), F32)])
    return jnp.concatenate([c_ctx.reshape(1, D), b_mod.reshape(6, D), n1.reshape(1, D), n2.reshape(1, D), lbf, lbb,
                            hgn.reshape(1, D), theta.reshape(1, D), fin.reshape(1, D)], axis=0)


def _unpack_params(pk):
    return (pk[0], pk[1:7].reshape(1, 6 * D), pk[7:8], pk[8:9], pk[9:11], pk[11:13], pk[13:14],
            pk[14, 0:HEADS].reshape(1, HEADS), pk[14, HEADS:2 * HEADS].reshape(1, HEADS), pk[15])


def kernel(x, c, ctx, c_ctx, w_mod, b_mod, norm1_w, norm2_w, w_in, hg_lb_fwd, hg_lb_bwd, hg_norm_w, rt_theta_fwd, rt_theta_bwd, w_proj_hgrn, w_proj_ret, w_out, w_ffn_gate, w_ffn_up, w_ffn_down, final_norm_w, loss_target, m_c_ctx, m_w_mod, m_b_mod, m_norm1_w, m_norm2_w, m_w_in, m_hg_lb_fwd, m_hg_lb_bwd, m_hg_norm_w, m_rt_theta_fwd, m_rt_theta_bwd, m_w_proj_hgrn, m_w_proj_ret, m_w_out, m_w_ffn_gate, m_w_ffn_up, m_w_ffn_down, m_final_norm_w, v_c_ctx, v_w_mod, v_b_mod, v_norm1_w, v_norm2_w, v_w_in, v_hg_lb_fwd, v_hg_lb_bwd, v_hg_norm_w, v_rt_theta_fwd, v_rt_theta_bwd, v_w_proj_hgrn, v_w_proj_ret, v_w_out, v_w_ffn_gate, v_w_ffn_up, v_w_ffn_down, v_final_norm_w):
    xi, yi, ci = _place()
    dev = 4 * xi + 2 * yi + ci
    chip = 2 * xi + yi
    core_arg = jnp.reshape(ci, (1,)).astype(jnp.int32)
    place_arg = jnp.stack([chip, ci]).astype(jnp.int32)

    c_all = allgather8(jnp.concatenate([c, jnp.zeros((7, D), F32)], axis=0), "gather_c").reshape(N_DEV, 8, D)[:, 0]
    call16 = jnp.concatenate([c_all, c_ctx.reshape(1, D), jnp.zeros((7, D), F32)], axis=0)
    b_sh = lax.dynamic_slice_in_dim(b_mod, chip * MOD_SH, MOD_SH, axis=1)
    mod_sh = mod_fwd(call16, w_mod[0], b_sh, "mod_fwd")
    mod_g = allgather8(mod_sh, "gather_mod").reshape(N_DEV, 16, MOD_SH)
    mod_all = jnp.concatenate([mod_g[0], mod_g[2], mod_g[4], mod_g[6]], axis=1)
    mod_x = lax.dynamic_index_in_dim(mod_all, dev, axis=0, keepdims=False).reshape(6, D)
    mod_c = mod_all[8].reshape(6, D)

    pk_w = _pack_params(c_ctx, b_mod, norm1_w, norm2_w, hg_lb_fwd, hg_lb_bwd, hg_norm_w, rt_theta_fwd, rt_theta_bwd, final_norm_w)
    theta_row = pk_w[14:15]
    lb_f, lb_b, lg_row = prep_small(hg_lb_fwd, hg_lb_bwd, theta_row, "prep_small")
    lg_f = jnp.broadcast_to(lg_row[0, 0:HEADS].reshape(HEADS, 1, 1), (HEADS, 1, RT_DV))
    lg_b = jnp.broadcast_to(lg_row[0, HEADS:2 * HEADS].reshape(HEADS, 1, 1), (HEADS, 1, RT_DV))

    shards = [w_in[0], w_proj_hgrn[0], w_proj_ret[0], w_out[0], w_ffn_gate[0], w_ffn_up[0], w_ffn_down[0]]
    placed = [lax.dynamic_update_index_in_dim(jnp.zeros((N_SHARD,) + s.shape, BF16), s.astype(BF16), chip, 0)
              for s in shards]
    g_in, g_pa, g_pb, g_out, g_wg, g_wu, g_wd = gather_weights(placed, "gather_weights")
    w = {"w_in": g_in, "w_pa": g_pa.reshape(D, D), "w_pb": g_pb.reshape(2 * D, D), "w_out": g_out.reshape(D, D),
         "wg": g_wg, "wu": g_wu, "wd": g_wd}

    dx, grads, small = local_step(x[0], ctx[0], loss_target[0], mod_x, mod_c, lb_f, lb_b, lg_f, lg_b,
                                  norm1_w, norm2_w, hg_norm_w, final_norm_w.reshape(1, D), w)

    order = ["w_in", "w_pa", "w_pb", "w_out", "wg", "wu", "wd"]
    gs = [grads[k] for k in order]
    to_sibling = [lax.dynamic_slice_in_dim(g, (1 - ci) * (g.shape[1] // 2), g.shape[1] // 2, axis=1).astype(BF16)
                  for g in gs]
    from_sibling = rs_to_sibling(to_sibling, "rs_to_sibling")
    chip_sums = [rs_add_sibling(g, r, core_arg, "rs_add_sibling_" + k) for g, r, k in zip(gs, from_sibling, order)]
    from_chips = rs_to_chips([a16 for _, a16 in chip_sums], "rs_to_chips")
    halves = [rs_add_chips(a, r, place_arg, "rs_add_chips_" + k) for (a, _), r, k in zip(chip_sums, from_chips, order)]
    full = dict(zip(order, rs_join_halves(halves, "rs_join_halves")))

    g3 = allgather8(small, "gather_small").reshape(N_DEV, SMALL_ROWS, D)
    pk_g, aux = small_grads(g3, lb_f, lb_b, theta_row, "small_grads")
    loss = aux[2, 0]
    dmod16 = jnp.concatenate([
        g3[:, 0:6, :].reshape(N_DEV, 6 * D),
        jnp.concatenate([aux[0], aux[1], jnp.zeros((4 * D,), F32)]).reshape(1, 6 * D),
        jnp.zeros((7, 6 * D), F32)], axis=0)
    dmod_sh = lax.dynamic_slice_in_dim(dmod16, chip * MOD_SH, MOD_SH, axis=1)
    g_wmod, dsilu = mod_bwd(call16, dmod_sh, w_mod[0], "mod_bwd")
    g4 = allgather8(dsilu, "gather_dsilu")
    pk_m = _pack_params(m_c_ctx, m_b_mod, m_norm1_w, m_norm2_w, m_hg_lb_fwd, m_hg_lb_bwd, m_hg_norm_w, m_rt_theta_fwd, m_rt_theta_bwd, m_final_norm_w)
    pk_v = _pack_params(v_c_ctx, v_b_mod, v_norm1_w, v_norm2_w, v_hg_lb_fwd, v_hg_lb_bwd, v_hg_norm_w, v_rt_theta_fwd, v_rt_theta_bwd, v_final_norm_w)
    pk_g, pk_d, pk_nm, pk_nv = adamw_small(g4, pk_g, pk_w, pk_m, pk_v, "adamw_small")

    big = {
        "w_mod": (g_wmod, w_mod, m_w_mod, v_w_mod),
        "w_in": (full["w_in"], w_in, m_w_in, v_w_in),
        "w_pa": (full["w_pa"], w_proj_hgrn, m_w_proj_hgrn, v_w_proj_hgrn),
        "w_pb": (full["w_pb"], w_proj_ret, m_w_proj_ret, v_w_proj_ret),
        "w_out": (full["w_out"], w_out, m_w_out, v_w_out),
        "wg": (full["wg"], w_ffn_gate, m_w_ffn_gate, v_w_ffn_gate),
        "wu": (full["wu"], w_ffn_up, m_w_ffn_up, v_w_ffn_up),
        "wd": (full["wd"], w_ffn_down, m_w_ffn_down, v_w_ffn_down),
    }
    res = {}
    for k, (g, wt, mt, vt) in big.items():
        d, nm, nv = adamw(wt[0], g, mt[0], vt[0], "adamw_" + k)
        res[k] = (g[None], d[None], nm[None], nv[None])

    sm = [_unpack_params(p) for p in (pk_g, pk_d, pk_nm, pk_nv)]
    outs = []
    for t in range(4):
        (s_cctx, s_bmod, s_n1, s_n2, s_lbf, s_lbb, s_hgn, s_thf, s_thb, s_fin) = sm[t]
        outs.append([s_cctx, res["w_mod"][t], s_bmod, s_n1, s_n2, res["w_in"][t], s_lbf, s_lbb, s_hgn, s_thf, s_thb,
                     res["w_pa"][t], res["w_pb"][t], res["w_out"][t], res["wg"][t], res["wu"][t], res["wd"][t], s_fin])
    return (loss, dx[None], *outs[0], *outs[1], *outs[2], *outs[3])
```

```python
import functools

import jax
import jax.numpy as jnp
from jax import lax
from jax.experimental import pallas as pl
from jax.experimental.pallas import tpu as pltpu

F32 = jnp.float32
BF16 = jnp.bfloat16
HI = lax.Precision.HIGHEST

D = 1024
HEADS = 8
HG_D = 128
RT_DK = 128
RT_DV = 256
D_FF = 2816
D_IN = 13312
N_SHARD = 4
IN_SH = D_IN // N_SHARD
FF_SH = D_FF // N_SHARD
HG_CHUNK = 32
SCAN_ROWS = 256
HG_GROUP = 8
EPS = 1e-6
GN_EPS = 1e-5
Q_SCALE = 128.0 ** -0.5
VMEM_LIMIT = 56 * 1024 * 1024

COL_HQ, COL_HFF, COL_HFB, COL_HI, COL_HG = 0, 8, 16, 24, 32
COL_RQ, COL_RK, COL_RV, COL_RG, COL_GA, COL_GB = 40, 48, 56, 72, 88, 96

ADAM_LR, ADAM_B1, ADAM_B2, ADAM_EPS, ADAM_WD, ADAM_STEP = 0.001, 0.9, 0.999, 1e-08, 0.01, 10


def _params(*sem):
    return pltpu.CompilerParams(dimension_semantics=sem, vmem_limit_bytes=VMEM_LIMIT)


def _dot(a, b, ca=1, cb=0, prec=None):
    return lax.dot_general(a, b, (((ca,), (cb,)), ((), ())), precision=prec, preferred_element_type=F32)


def _bdot(a, b, ca=1, cb=0):
    return _dot(a.astype(BF16), b.astype(BF16), ca, cb)


def _sigmoid(z):
    return 1.0 / (1.0 + jnp.exp(-z))


def _rowsum(a):
    return jnp.sum(a, axis=0, keepdims=True)


def _lanemean(a):
    return jnp.mean(a, axis=-1, keepdims=True)


def normmod_matmul(x, nw, sh, sc, w4, name):
    L = x.shape[0]
    tm = min(512, L)
    tn = IN_SH // 2

    def body(x_ref, nw_ref, sh_ref, sc_ref, w_ref, p_ref, hx_ref, hx_scr):
        @pl.when((pl.program_id(1) == 0) & (pl.program_id(2) == 0))
        def _():
            xv = x_ref[...]
            n = xv * lax.rsqrt(_lanemean(xv * xv) + EPS) * nw_ref[...]
            h = (n * (1.0 + sc_ref[...]) + sh_ref[...]).astype(BF16)
            hx_scr[...] = h
            hx_ref[...] = h

        p_ref[...] = _dot(hx_scr[...], w_ref[...])

    vec = pl.BlockSpec((1, D), lambda i, k, j: (0, 0))
    return pl.pallas_call(
        body, name=name,
        grid=(L // tm, N_SHARD, 2),
        in_specs=[pl.BlockSpec((tm, D), lambda i, k, j: (i, 0)), vec, vec, vec,
                  pl.BlockSpec((None, D, tn), lambda i, k, j: (k, 0, j))],
        out_specs=[pl.BlockSpec((tm, tn), lambda i, k, j: (i, 2 * k + j)),
                   pl.BlockSpec((tm, D), lambda i, k, j: (i, 0))],
        out_shape=[jax.ShapeDtypeStruct((L, D_IN), F32), jax.ShapeDtypeStruct((L, D), BF16)],
        scratch_shapes=[pltpu.VMEM((tm, D), BF16)],
        compiler_params=_params("parallel", "arbitrary", "arbitrary"),
    )(x, nw, sh, sc, w4)


def _hgrn_gates(z, lb):
    sg = _sigmoid(z)
    sgn = _sigmoid(-z)
    f = lb + (1.0 - lb) * sg
    k = (1.0 - lb) * sgn
    return sg, sgn, f, k


def _tri_chunks(n, chunk, reverse):
    r = lax.broadcasted_iota(jnp.int32, (n, n), 0)
    c = lax.broadcasted_iota(jnp.int32, (n, n), 1)
    same = (r // chunk) == (c // chunk)
    return jnp.where(same & ((r <= c) if reverse else (r >= c)), 1.0, 0.0).astype(F32)


def _decay3(b, reverse):
    C = b.shape[0]
    t = lax.broadcasted_iota(jnp.int32, (C, C, 1), 0)
    s = lax.broadcasted_iota(jnp.int32, (C, C, 1), 1)
    mask = (t <= s) if reverse else (t >= s)
    return jnp.exp(jnp.where(mask, b[:, None, :] - b[None, :, :], -jnp.inf))


HG_SUB = 16


def _hgrn_pairs(reverse):
    pairs = []
    size = HG_SUB
    while size < HG_CHUNK:
        for lo in range(0, HG_CHUNK, 2 * size):
            first, second = slice(lo, lo + size), slice(lo + size, lo + 2 * size)
            if reverse:
                pairs.append((first, second, lo + size))
            else:
                pairs.append((second, first, lo + size - 1))
        size *= 2
    return pairs


def _hgrn_intra_fwd(q, k, v, b, reverse):
    blocks = []
    for lo in range(0, HG_CHUNK, HG_SUB):
        r = slice(lo, lo + HG_SUB)
        att3 = jnp.sum(q[r][:, None, :] * k[r][None, :, :] * _decay3(b[r], reverse), axis=-1, keepdims=True)
        blocks.append(jnp.sum(att3 * v[r][None, :, :], axis=1))
    for qr, kr, ref in _hgrn_pairs(reverse):
        beta = b[ref:ref + 1]
        att = _bdot(q[qr] * jnp.exp(b[qr] - beta), k[kr] * jnp.exp(beta - b[kr]), 1, 1)
        part = _bdot(att, v[kr])
        n = part.shape[0] // HG_SUB
        for i in range(n):
            blocks[qr.start // HG_SUB + i] += part[i * HG_SUB:(i + 1) * HG_SUB]
    return jnp.concatenate(blocks, axis=0)


def _hgrn_intra_bwd(q, k, v, b, d_o, reverse):
    nb = HG_CHUNK // HG_SUB
    dq, dk, dv = [None] * nb, [None] * nb, [None] * nb
    for i in range(nb):
        r = slice(i * HG_SUB, (i + 1) * HG_SUB)
        e3 = _decay3(b[r], reverse)
        p3 = jnp.sum(d_o[r][:, None, :] * v[r][None, :, :], axis=-1, keepdims=True) * e3
        dq[i] = jnp.sum(p3 * k[r][None, :, :], axis=1)
        dk[i] = jnp.sum(p3 * q[r][:, None, :], axis=0)
        att3 = jnp.sum(q[r][:, None, :] * k[r][None, :, :] * e3, axis=-1, keepdims=True)
        dv[i] = jnp.sum(att3 * d_o[r][:, None, :], axis=0)

    def add(acc, rows, part):
        for i in range(part.shape[0] // HG_SUB):
            acc[rows.start // HG_SUB + i] += part[i * HG_SUB:(i + 1) * HG_SUB]

    for qr, kr, ref in _hgrn_pairs(reverse):
        beta = b[ref:ref + 1]
        fq, fk = jnp.exp(b[qr] - beta), jnp.exp(beta - b[kr])
        qt, kt = q[qr] * fq, k[kr] * fk
        att = _bdot(qt, kt, 1, 1)
        datt = _bdot(d_o[qr], v[kr], 1, 1)
        add(dq, qr, _bdot(datt, kt) * fq)
        add(dk, kr, _bdot(datt, qt, 0, 0) * fk)
        add(dv, kr, _bdot(att, d_o[qr], 0, 0))
    return jnp.concatenate(dq, axis=0), jnp.concatenate(dk, axis=0), jnp.concatenate(dv, axis=0)


def _hgrn_state_step(k, v, b, s_t, last):
    b_last = b[last:last + 1]
    return s_t * jnp.exp(b_last) + _bdot(v, k * jnp.exp(b_last - b), 0, 0)


def hgrn_scan_fwd(p, lb, s0, col_z, reverse, name):
    L = p.shape[0]
    nB = L // SCAN_ROWS
    nC = SCAN_ROWS // HG_CHUNK
    C = HG_CHUNK
    G, W = HG_GROUP, HG_GROUP * HG_D
    last = 0 if reverse else C - 1

    def bmap(b):
        return (nB - 1 - b) if reverse else b

    def body(q_ref, z_ref, v_ref, lb_ref, s0_ref, o_ref, sfin_ref, sblk_ref, s_scr, k_scr, b_scr):
        blk = pl.program_id(1)

        @pl.when(blk == 0)
        def _():
            s_scr[...] = s0_ref[...]

        sblk_ref[...] = s_scr[...]
        _, _, f_all, k_all = _hgrn_gates(z_ref[...], lb_ref[...])
        k_scr[...] = k_all
        b_scr[...] = _dot(_tri_chunks(SCAN_ROWS, C, reverse), jnp.log(f_all), prec=HI)

        def chunk(ci, carry):
            c = (nC - 1 - ci) if reverse else ci
            rows = pl.ds(pl.multiple_of(c * C, C), C)
            for j in range(G):
                lanes = slice(j * HG_D, (j + 1) * HG_D)
                q = q_ref[rows, lanes] * Q_SCALE
                v = v_ref[rows, lanes]
                k = k_scr[rows, lanes]
                b = b_scr[rows, lanes]
                s_t = s_scr[j]
                o_ref[rows, lanes] = _hgrn_intra_fwd(q, k, v, b, reverse) + _bdot(q * jnp.exp(b), s_t, 1, 1)
                s_scr[j] = _hgrn_state_step(k, v, b, s_t, last)
            return carry

        lax.fori_loop(0, nC, chunk, 0)

        @pl.when(blk == nB - 1)
        def _():
            sfin_ref[...] = s_scr[...]

    def col(c0):
        return pl.BlockSpec((SCAN_ROWS, W), lambda h, b: (bmap(b), c0 // G + h))

    state = pl.BlockSpec((G, HG_D, HG_D), lambda h, b: (h, 0, 0))
    return pl.pallas_call(
        body, name=name,
        grid=(HEADS // G, nB),
        in_specs=[col(COL_HQ), col(col_z), col(COL_HI), pl.BlockSpec((1, W), lambda h, b: (0, h)), state],
        out_specs=[pl.BlockSpec((SCAN_ROWS, W), lambda h, b: (bmap(b), h)), state,
                   pl.BlockSpec((None, G, HG_D, HG_D), lambda h, b: (bmap(b), h, 0, 0))],
        out_shape=[jax.ShapeDtypeStruct((L, D), F32),
                   jax.ShapeDtypeStruct((HEADS, HG_D, HG_D), F32),
                   jax.ShapeDtypeStruct((nB, HEADS, HG_D, HG_D), F32)],
        scratch_shapes=[pltpu.VMEM((G, HG_D, HG_D), F32), pltpu.VMEM((SCAN_ROWS, W), F32),
                        pltpu.VMEM((SCAN_ROWS, W), F32)],
        compiler_params=_params("parallel", "arbitrary"),
    )(p, p, p, lb, s0)


def hgrn_scan_bwd(p, lb, s_blocks, d_o, ds_fin, prev, col_z, reverse, name):
    L = p.shape[0]
    nB = L // SCAN_ROWS
    nC = SCAN_ROWS // HG_CHUNK
    C = HG_CHUNK
    G, W = HG_GROUP, HG_GROUP * HG_D
    last = 0 if reverse else C - 1
    has_prev = prev is not None
    out_dt = BF16 if has_prev else F32

    def bmap(b):
        return b if reverse else (nB - 1 - b)

    def body(*refs):
        q_ref, z_ref, v_ref, lb_ref, sblk_ref, do_ref, dsf_ref = refs[:7]
        refs = refs[7:]
        if has_prev:
            pq_ref, pv_ref = refs[:2]
            refs = refs[2:]
        dq_ref, dz_ref, dv_ref, dlb_ref, ds0_ref, st_scr, run_scr, ds_scr, k_scr, b_scr, db_scr, dk_scr = refs
        blk = pl.program_id(1)

        @pl.when(blk == 0)
        def _():
            ds_scr[...] = dsf_ref[...]
            dlb_ref[...] = jnp.zeros_like(dlb_ref)

        tri = _tri_chunks(SCAN_ROWS, C, reverse)
        row = lax.broadcasted_iota(jnp.int32, (C, HG_D), 0)
        _, _, f_all, k_all = _hgrn_gates(z_ref[...], lb_ref[...])
        k_scr[...] = k_all
        b_scr[...] = _dot(tri, jnp.log(f_all), prec=HI)
        run_scr[...] = sblk_ref[...]

        def recompute(ci, carry):
            c = (nC - 1 - ci) if reverse else ci
            rows = pl.ds(pl.multiple_of(c * C, C), C)
            for j in range(G):
                lanes = slice(j * HG_D, (j + 1) * HG_D)
                s_t = run_scr[j]
                st_scr[c, j] = s_t
                run_scr[j] = _hgrn_state_step(k_scr[rows, lanes], v_ref[rows, lanes], b_scr[rows, lanes], s_t, last)
            return carry

        lax.fori_loop(0, nC, recompute, 0)

        def chunk(ci, carry):
            c = ci if reverse else (nC - 1 - ci)
            rows = pl.ds(pl.multiple_of(c * C, C), C)
            for j in range(G):
                lanes = slice(j * HG_D, (j + 1) * HG_D)
                k = k_scr[rows, lanes]
                b = b_scr[rows, lanes]
                q = q_ref[rows, lanes] * Q_SCALE
                v = v_ref[rows, lanes]
                d_o = do_ref[rows, lanes]
                s_t = st_scr[c, j]
                ds_t = ds_scr[j]
                eb = jnp.exp(b)
                b_last = b[last:last + 1]
                eb_last = jnp.exp(b_last)
                kdec = jnp.exp(b_last - b)
                qe = q * eb
                ke = k * kdec
                dq_in, dk_in, dv_in = _hgrn_intra_bwd(q, k, v, b, d_o, reverse)
                dq_tot = _bdot(d_o, s_t, 1, 0) * eb + dq_in
                dke = _bdot(v, ds_t, 1, 0)
                dk_tot = dke * kdec + dk_in
                dv = dv_in + _bdot(ke, ds_t, 1, 1)
                db_last = _rowsum(dke * ke) + eb_last * _rowsum(ds_t * s_t)
                db_scr[rows, lanes] = q * dq_tot - k * dk_tot + jnp.where(row == last, db_last, 0.0)
                dk_scr[rows, lanes] = dk_tot
                dq = dq_tot * Q_SCALE
                if has_prev:
                    dq = dq + pq_ref[rows, lanes]
                    dv = dv + pv_ref[rows, lanes]
                dq_ref[rows, lanes] = dq.astype(out_dt)
                dv_ref[rows, lanes] = dv.astype(out_dt)
                ds_scr[j] = ds_t * eb_last + _bdot(d_o, qe, 0, 0)
            return carry

        lax.fori_loop(0, nC, chunk, 0)

        lb = lb_ref[...]
        sg, sgn, f, _ = _hgrn_gates(z_ref[...], lb)
        g = _dot(tri, db_scr[...], 0, 0, prec=HI) / f - dk_scr[...]
        dz_ref[...] = (g * (1.0 - lb) * sg * sgn).astype(BF16)
        dlb_ref[...] += _rowsum(g * sgn)

        @pl.when(blk == nB - 1)
        def _():
            ds0_ref[...] = ds_scr[...]

    def col(c0):
        return pl.BlockSpec((SCAN_ROWS, W), lambda h, b: (bmap(b), c0 // G + h))

    tile = pl.BlockSpec((SCAN_ROWS, W), lambda h, b: (bmap(b), h))
    state = pl.BlockSpec((G, HG_D, HG_D), lambda h, b: (h, 0, 0))
    in_specs = [col(COL_HQ), col(col_z), col(COL_HI),
                pl.BlockSpec((1, W), lambda h, b: (0, h)),
                pl.BlockSpec((None, G, HG_D, HG_D), lambda h, b: (bmap(b), h, 0, 0)),
                tile, state]
    args = [p, p, p, lb, s_blocks, d_o, ds_fin]
    if has_prev:
        in_specs += [tile, tile]
        args += list(prev)
    return pl.pallas_call(
        body, name=name,
        grid=(HEADS // G, nB),
        in_specs=in_specs,
        out_specs=[tile, tile, tile, pl.BlockSpec((1, W), lambda h, b: (0, h)), state],
        out_shape=[jax.ShapeDtypeStruct((L, D), out_dt), jax.ShapeDtypeStruct((L, D), BF16),
                   jax.ShapeDtypeStruct((L, D), out_dt), jax.ShapeDtypeStruct((1, D), F32),
                   jax.ShapeDtypeStruct((HEADS, HG_D, HG_D), F32)],
        scratch_shapes=[pltpu.VMEM((nC, G, HG_D, HG_D), F32), pltpu.VMEM((G, HG_D, HG_D), F32),
                        pltpu.VMEM((G, HG_D, HG_D), F32)] + [pltpu.VMEM((SCAN_ROWS, W), F32)] * 4,
        compiler_params=_params("parallel", "arbitrary"),
    )(*args)


def _rope(t, cosf, sinf):
    return t * cosf + pltpu.roll(t, RT_DK // 2, 1) * sinf


def _rope_t(d, cosf, sinf):
    return d * cosf + pltpu.roll(d * sinf, RT_DK // 2, 1)


def _ret_decays(lg, reverse):
    C = SCAN_ROWS
    t = lax.broadcasted_iota(jnp.int32, (C, C), 0)
    s = lax.broadcasted_iota(jnp.int32, (C, C), 1)
    delta = ((s - t) if reverse else (t - s)).astype(F32)
    dmat = jnp.where(delta >= 0, jnp.exp(lg * jnp.maximum(delta, 0.0)), 0.0)
    r = lax.broadcasted_iota(jnp.int32, (C, RT_DK), 0)
    pos = ((C - 1 - r) if reverse else r).astype(F32)
    lg1 = lg[:, :RT_DK]
    qdec = jnp.exp(lg1 * (pos + 1.0))
    kdec = jnp.exp(lg1 * (C - 1.0 - pos))
    sdec = jnp.exp(lg1 * float(C))
    return dmat, delta, pos, qdec, kdec, sdec


def ret_scan_fwd(p, cosf, sinf, lg, s0, reverse, name):
    L = p.shape[0]
    C = SCAN_ROWS
    nB = L // C

    def bmap(b):
        return (nB - 1 - b) if reverse else b

    def body(q_ref, k_ref, v_ref, cos_ref, sin_ref, lg_ref, s0_ref, o_ref, sfin_ref, sblk_ref, s_scr):
        blk = pl.program_id(1)

        @pl.when(blk == 0)
        def _():
            s_scr[...] = s0_ref[...]

        s_t = s_scr[...]
        sblk_ref[...] = s_t
        cosf, sinf = cos_ref[...], sin_ref[...]
        dmat, _, _, qdec, kdec, sdec = _ret_decays(lg_ref[...], reverse)
        q = _rope(q_ref[...] * Q_SCALE, cosf, sinf)
        k = _rope(k_ref[...], cosf, sinf)
        v = v_ref[...]
        att = _bdot(q, k, 1, 1) * dmat
        o_ref[...] = _bdot(att, v) + _bdot(q * qdec, s_t, 1, 1)
        s_new = s_t * sdec + _bdot(v, k * kdec, 0, 0)
        s_scr[...] = s_new

        @pl.when(blk == nB - 1)
        def _():
            sfin_ref[...] = s_new

    def col(c0):
        return pl.BlockSpec((C, RT_DK), lambda h, b: (bmap(b), c0 + h))

    tab = pl.BlockSpec((C, RT_DK), lambda h, b: (bmap(b), 0))
    state = pl.BlockSpec((None, RT_DV, RT_DK), lambda h, b: (h, 0, 0))
    return pl.pallas_call(
        body, name=name,
        grid=(HEADS, nB),
        in_specs=[col(COL_RQ), col(COL_RK),
                  pl.BlockSpec((C, RT_DV), lambda h, b: (bmap(b), COL_RV // 2 + h)),
                  tab, tab, pl.BlockSpec((None, 1, RT_DV), lambda h, b: (h, 0, 0)), state],
        out_specs=[pl.BlockSpec((C, RT_DV), lambda h, b: (bmap(b), h)), state,
                   pl.BlockSpec((None, None, RT_DV, RT_DK), lambda h, b: (bmap(b), h, 0, 0))],
        out_shape=[jax.ShapeDtypeStruct((L, HEADS * RT_DV), F32),
                   jax.ShapeDtypeStruct((HEADS, RT_DV, RT_DK), F32),
                   jax.ShapeDtypeStruct((nB, HEADS, RT_DV, RT_DK), F32)],
        scratch_shapes=[pltpu.VMEM((RT_DV, RT_DK), F32)],
        compiler_params=_params("parallel", "arbitrary"),
    )(p, p, p, cosf, sinf, lg, s0)


def ret_scan_bwd(p, cosf, sinf, lg, s_blocks, d_o, ds_fin, prev, reverse, name):
    L = p.shape[0]
    C = SCAN_ROWS
    nB = L // C
    has_prev = prev is not None
    out_dt = BF16 if has_prev else F32

    def bmap(b):
        return b if reverse else (nB - 1 - b)

    def body(*refs):
        q_ref, k_ref, v_ref, cos_ref, sin_ref, lg_ref, sblk_ref, do_ref, dsf_ref = refs[:9]
        refs = refs[9:]
        if has_prev:
            pq_ref, pk_ref, pv_ref = refs[:3]
            refs = refs[3:]
        dq_ref, dk_ref, dv_ref, dlg_ref, ds0_ref, ds_scr = refs
        blk = pl.program_id(1)

        @pl.when(blk == 0)
        def _():
            ds_scr[...] = dsf_ref[...]
            dlg_ref[...] = jnp.zeros_like(dlg_ref)

        s_t = sblk_ref[...]
        ds_t = ds_scr[...]
        cosf, sinf = cos_ref[...], sin_ref[...]
        dmat, delta, pos, qdec, kdec, sdec = _ret_decays(lg_ref[...], reverse)
        q = _rope(q_ref[...] * Q_SCALE, cosf, sinf)
        k = _rope(k_ref[...], cosf, sinf)
        v = v_ref[...]
        d_o = do_ref[...]
        att_raw = _bdot(q, k, 1, 1)
        datt_m = _bdot(d_o, v, 1, 1) * dmat
        dqd = _bdot(d_o, s_t, 1, 0)
        dkd = _bdot(v, ds_t, 1, 0)
        dq = _bdot(datt_m, k) + dqd * qdec
        dk = _bdot(datt_m, q, 0, 0) + dkd * kdec
        dv = _bdot(att_raw * dmat, d_o, 0, 0) + _bdot(k * kdec, ds_t, 1, 1)
        ds_new = ds_t * sdec + _bdot(d_o, q * qdec, 0, 0)
        ds_scr[...] = ds_new
        t1 = jnp.sum(_rowsum(datt_m * att_raw * delta), axis=-1, keepdims=True)
        t23 = jnp.sum(_rowsum((pos + 1.0) * qdec * q * dqd + (C - 1.0 - pos) * kdec * k * dkd), axis=-1, keepdims=True)
        t4 = jnp.sum(_rowsum(ds_t * s_t * sdec), axis=-1, keepdims=True) * float(C)
        dlg_ref[...] += jnp.broadcast_to(t1 + t23 + t4, (1, RT_DK))
        if has_prev:
            dq = _rope_t(dq + pq_ref[...], cosf, sinf) * Q_SCALE
            dk = _rope_t(dk + pk_ref[...], cosf, sinf)
            dv = dv + pv_ref[...]
        dq_ref[...] = dq.astype(out_dt)
        dk_ref[...] = dk.astype(out_dt)
        dv_ref[...] = dv.astype(out_dt)

        @pl.when(blk == nB - 1)
        def _():
            ds0_ref[...] = ds_new

    def col(c0):
        return pl.BlockSpec((C, RT_DK), lambda h, b: (bmap(b), c0 + h))

    tab = pl.BlockSpec((C, RT_DK), lambda h, b: (bmap(b), 0))
    state = pl.BlockSpec((None, RT_DV, RT_DK), lambda h, b: (h, 0, 0))
    tk = pl.BlockSpec((C, RT_DK), lambda h, b: (bmap(b), h))
    tv = pl.BlockSpec((C, RT_DV), lambda h, b: (bmap(b), h))
    in_specs = [col(COL_RQ), col(COL_RK),
                pl.BlockSpec((C, RT_DV), lambda h, b: (bmap(b), COL_RV // 2 + h)),
                tab, tab, pl.BlockSpec((None, 1, RT_DV), lambda h, b: (h, 0, 0)),
                pl.BlockSpec((None, None, RT_DV, RT_DK), lambda h, b: (bmap(b), h, 0, 0)),
                tv, state]
    args = [p, p, p, cosf, sinf, lg, s_blocks, d_o, ds_fin]
    if has_prev:
        in_specs += [tk, tk, tv]
        args += list(prev)
    return pl.pallas_call(
        body, name=name,
        grid=(HEADS, nB),
        in_specs=in_specs,
        out_specs=[tk, tk, tv, pl.BlockSpec((None, 1, RT_DK), lambda h, b: (h, 0, 0)), state],
        out_shape=[jax.ShapeDtypeStruct((L, D), out_dt), jax.ShapeDtypeStruct((L, D), out_dt),
                   jax.ShapeDtypeStruct((L, HEADS * RT_DV), out_dt),
                   jax.ShapeDtypeStruct((HEADS, 1, RT_DK), F32),
                   jax.ShapeDtypeStruct((HEADS, RT_DV, RT_DK), F32)],
        scratch_shapes=[pltpu.VMEM((RT_DV, RT_DK), F32)],
        compiler_params=_params("parallel", "arbitrary"),
    )(*args)


def _silu_parts(h):
    s = _sigmoid(h)
    return h * s, s * (1.0 + h * (1.0 - s))


def _head_rms(o):
    outs, rs = [], []
    for h in range(HEADS):
        oh = o[:, h * HG_D:(h + 1) * HG_D]
        r = lax.rsqrt(_lanemean(oh * oh) + EPS)
        outs.append(oh * r)
        rs.append(r)
    return outs, rs


def _group_norm(o):
    outs, rs = [], []
    for h in range(HEADS):
        oh = o[:, h * RT_DV:(h + 1) * RT_DV]
        c = oh - _lanemean(oh)
        r = lax.rsqrt(_lanemean(c * c) + GN_EPS)
        outs.append(c * r)
        rs.append(r)
    return outs, rs


MIX_ROWS = 128


def _mix_specs(L):
    def t(w, c=0):
        return pl.BlockSpec((MIX_ROWS, w), lambda i: (i, c))

    return t


def mix_fwd(ohf, ohb, orf, orb, p, x, g1, hgw, w_pa, w_pb, w_out, name):
    L = x.shape[0]
    t = _mix_specs(L)

    def body(ohf_ref, ohb_ref, orf_ref, orb_ref, hg_ref, rg0_ref, rg1_ref, ga_ref, gb_ref, x_ref, g1_ref, hgw_ref,
             wpa_ref, wpb_ref, wout_ref, x1_ref, xmix_ref, merged_ref, ya_ref, yb_ref):
        nh, _ = _head_rms(ohf_ref[...] + ohb_ref[...])
        ya = jnp.concatenate(nh, axis=1) * hgw_ref[...] * _silu_parts(hg_ref[...])[0]
        gn, _ = _group_norm(orf_ref[...] + orb_ref[...])
        rg = jnp.concatenate([rg0_ref[...], rg1_ref[...]], axis=1)
        yb = jnp.concatenate(gn, axis=1) * _silu_parts(rg)[0]
        ya16, yb16 = ya.astype(BF16), yb.astype(BF16)
        merged = (_sigmoid(ga_ref[...]) * _dot(ya16, wpa_ref[...])
                  + _sigmoid(gb_ref[...]) * _dot(yb16, wpb_ref[...])).astype(BF16)
        x_mix = _dot(merged, wout_ref[...])
        x1_ref[...] = x_ref[...] + g1_ref[...] * x_mix
        xmix_ref[...] = x_mix
        merged_ref[...] = merged
        ya_ref[...] = ya16
        yb_ref[...] = yb16

    vec = pl.BlockSpec((1, D), lambda i: (0, 0))

    def full(a):
        return pl.BlockSpec(a.shape, lambda i: (0, 0))

    return pl.pallas_call(
        body, name=name,
        grid=(L // MIX_ROWS,),
        in_specs=[t(D), t(D), t(2 * D), t(2 * D), t(D, COL_HG // 8), t(D, COL_RG // 8), t(D, COL_RG // 8 + 1),
                  t(D, COL_GA // 8), t(D, COL_GB // 8), t(D), vec, vec, full(w_pa), full(w_pb), full(w_out)],
        out_specs=[t(D), t(D), t(D), t(D), t(2 * D)],
        out_shape=[jax.ShapeDtypeStruct((L, D), F32), jax.ShapeDtypeStruct((L, D), F32),
                   jax.ShapeDtypeStruct((L, D), BF16), jax.ShapeDtypeStruct((L, D), BF16),
                   jax.ShapeDtypeStruct((L, 2 * D), BF16)],
        compiler_params=_params("parallel"),
    )(ohf, ohb, orf, orb, p, p, p, p, p, x, g1, hgw, w_pa, w_pb, w_out)


def mix_bwd(dx1, x_mix, ya, yb, ohf, ohb, orf, orb, p, g1, hgw, w_pa, w_pb, w_out, name):
    L = dx1.shape[0]
    t = _mix_specs(L)

    def body(dx1_ref, xmix_ref, ya_ref, yb_ref, ohf_ref, ohb_ref, orf_ref, orb_ref, hg_ref, rg0_ref, rg1_ref,
             ga_ref, gb_ref, g1_ref, hgw_ref, wpa_ref, wpb_ref, wout_ref,
             dxm_ref, da_ref, db_ref, dga_ref, dgb_ref, dhg_ref, drg_ref, dohg_ref, dort_ref, sums_ref):
        @pl.when(pl.program_id(0) == 0)
        def _():
            sums_ref[...] = jnp.zeros_like(sums_ref)

        dx1 = dx1_ref[...]
        dxm = (g1_ref[...] * dx1).astype(BF16)
        dxm_ref[...] = dxm
        dmerged = _dot(dxm, wout_ref[...], 1, 1)
        a = _dot(ya_ref[...], wpa_ref[...])
        bm = _dot(yb_ref[...], wpb_ref[...])
        sa, sb = _sigmoid(ga_ref[...]), _sigmoid(gb_ref[...])
        d_a = (dmerged * sa).astype(BF16)
        d_b = (dmerged * sb).astype(BF16)
        da_ref[...] = d_a
        db_ref[...] = d_b
        dga_ref[...] = (dmerged * a * sa * (1.0 - sa)).astype(BF16)
        dgb_ref[...] = (dmerged * bm * sb * (1.0 - sb)).astype(BF16)
        dya = _dot(d_a, wpa_ref[...], 1, 1)
        dyb = _dot(d_b, wpb_ref[...], 1, 1)

        hgw = hgw_ref[...]
        silu_h, dsilu_h = _silu_parts(hg_ref[...])
        nh, rh = _head_rms(ohf_ref[...] + ohb_ref[...])
        n = jnp.concatenate(nh, axis=1)
        dhg_ref[...] = (dya * n * hgw * dsilu_h).astype(BF16)
        dn = dya * hgw * silu_h
        douts = []
        for h in range(HEADS):
            dnh = dn[:, h * HG_D:(h + 1) * HG_D]
            douts.append(rh[h] * (dnh - nh[h] * _lanemean(dnh * nh[h])))
        dohg_ref[...] = jnp.concatenate(douts, axis=1)

        rg = jnp.concatenate([rg0_ref[...], rg1_ref[...]], axis=1)
        silu_r, dsilu_r = _silu_parts(rg)
        gn, rr = _group_norm(orf_ref[...] + orb_ref[...])
        g = jnp.concatenate(gn, axis=1)
        drg_ref[...] = (dyb * g * dsilu_r).astype(BF16)
        dgn = dyb * silu_r
        douts = []
        for h in range(HEADS):
            dgh = dgn[:, h * RT_DV:(h + 1) * RT_DV]
            douts.append(rr[h] * (dgh - _lanemean(dgh) - gn[h] * _lanemean(dgh * gn[h])))
        dort_ref[...] = jnp.concatenate(douts, axis=1)

        sums_ref[0:1, :] += _rowsum(dx1 * xmix_ref[...])
        sums_ref[1:2, :] += _rowsum(dya * n * silu_h)

    vec = pl.BlockSpec((1, D), lambda i: (0, 0))

    def full(a):
        return pl.BlockSpec(a.shape, lambda i: (0, 0))

    bf = functools.partial(jax.ShapeDtypeStruct, dtype=BF16)
    return pl.pallas_call(
        body, name=name,
        grid=(L // MIX_ROWS,),
        in_specs=[t(D), t(D), t(D), t(2 * D), t(D), t(D), t(2 * D), t(2 * D),
                  t(D, COL_HG // 8), t(D, COL_RG // 8), t(D, COL_RG // 8 + 1), t(D, COL_GA // 8), t(D, COL_GB // 8),
                  vec, vec, full(w_pa), full(w_pb), full(w_out)],
        out_specs=[t(D), t(D), t(D), t(D), t(D), t(D), t(2 * D), t(D), t(2 * D),
                   pl.BlockSpec((8, D), lambda i: (0, 0))],
        out_shape=[bf((L, D)), bf((L, D)), bf((L, D)), bf((L, D)), bf((L, D)), bf((L, D)), bf((L, 2 * D)),
                   jax.ShapeDtypeStruct((L, D), F32), jax.ShapeDtypeStruct((L, 2 * D), F32),
                   jax.ShapeDtypeStruct((8, D), F32)],
        compiler_params=_params("arbitrary"),
    )(dx1, x_mix, ya, yb, ohf, ohb, orf, orb, p, p, p, p, p, g1, hgw, w_pa, w_pb, w_out)


FFN_ROWS = 256


def ffn_fwd(x1, target, nw2, sh2, sc2, g2, fw, wg, wu, wd, name):
    L = x1.shape[0]
    tm = FFN_ROWS

    def body(x1_ref, tgt_ref, nw2_ref, sh2_ref, sc2_ref, g2_ref, fw_ref, wg_ref, wu_ref, wd_ref,
             hx2_ref, g_ref, u_ref, h_ref, f_ref, dx2_ref, sums_ref, hx_scr, acc):
        i, j = pl.program_id(0), pl.program_id(1)

        @pl.when((i == 0) & (j == 0))
        def _():
            sums_ref[...] = jnp.zeros_like(sums_ref)

        @pl.when(j == 0)
        def _():
            xv = x1_ref[...]
            n = xv * lax.rsqrt(_lanemean(xv * xv) + EPS) * nw2_ref[...]
            h = (n * (1.0 + sc2_ref[...]) + sh2_ref[...]).astype(BF16)
            hx_scr[...] = h
            hx2_ref[...] = h
            acc[...] = jnp.zeros_like(acc)

        hx = hx_scr[...]
        g = _dot(hx, wg_ref[...])
        u = _dot(hx, wu_ref[...])
        hh = (_silu_parts(g)[0] * u).astype(BF16)
        g_ref[...] = g
        u_ref[...] = u
        h_ref[...] = hh
        acc[...] += _dot(hh, wd_ref[...])

        @pl.when(j == N_SHARD - 1)
        def _():
            f = acc[...]
            f_ref[...] = f
            x2 = x1_ref[...] + g2_ref[...] * f
            r = lax.rsqrt(_lanemean(x2 * x2) + EPS)
            fw = fw_ref[...]
            e = x2 * r * fw - tgt_ref[...]
            dy = e * (1.0 / D)
            dyw = dy * fw
            dx2_ref[...] = r * dyw - x2 * (r * r * r) * _lanemean(dyw * x2)
            sums_ref[0:1, :] += _rowsum(dy * x2 * r)
            sums_ref[1:2, :] += _rowsum(e * e) * (0.5 / D)

    row = pl.BlockSpec((tm, D), lambda i, j: (i, 0))
    vec = pl.BlockSpec((1, D), lambda i, j: (0, 0))
    sh = pl.BlockSpec((None, tm, FF_SH), lambda i, j: (j, i, 0))
    return pl.pallas_call(
        body, name=name,
        grid=(L // tm, N_SHARD),
        in_specs=[row, row, vec, vec, vec, vec, vec,
                  pl.BlockSpec((None, D, FF_SH), lambda i, j: (j, 0, 0)),
                  pl.BlockSpec((None, D, FF_SH), lambda i, j: (j, 0, 0)),
                  pl.BlockSpec((None, FF_SH, D), lambda i, j: (j, 0, 0))],
        out_specs=[row, sh, sh, sh, row, row, pl.BlockSpec((8, D), lambda i, j: (0, 0))],
        out_shape=[jax.ShapeDtypeStruct((L, D), BF16),
                   jax.ShapeDtypeStruct((N_SHARD, L, FF_SH), F32), jax.ShapeDtypeStruct((N_SHARD, L, FF_SH), F32),
                   jax.ShapeDtypeStruct((N_SHARD, L, FF_SH), BF16),
                   jax.ShapeDtypeStruct((L, D), F32), jax.ShapeDtypeStruct((L, D), F32),
                   jax.ShapeDtypeStruct((8, D), F32)],
        scratch_shapes=[pltpu.VMEM((tm, D), BF16), pltpu.VMEM((tm, D), F32)],
        compiler_params=_params("arbitrary", "arbitrary"),
    )(x1, target, nw2, sh2, sc2, g2, fw, wg, wu, wd)


def ffn_bwd(dx2, x1, f, g, u, nw2, sc2, g2, wg, wu, wd, name):
    L = x1.shape[0]
    tm = FFN_ROWS

    def body(dx2_ref, x1_ref, f_ref, g_ref, u_ref, nw2_ref, sc2_ref, g2_ref, wg_ref, wu_ref, wd_ref,
             df_ref, dg_ref, du_ref, dx1_ref, sums_ref, df_scr, acc):
        i, j = pl.program_id(0), pl.program_id(1)

        @pl.when((i == 0) & (j == 0))
        def _():
            sums_ref[...] = jnp.zeros_like(sums_ref)

        @pl.when(j == 0)
        def _():
            dx2 = dx2_ref[...]
            df = (g2_ref[...] * dx2).astype(BF16)
            df_scr[...] = df
            df_ref[...] = df
            sums_ref[0:1, :] += _rowsum(dx2 * f_ref[...])
            acc[...] = jnp.zeros_like(acc)

        dh = _dot(df_scr[...], wd_ref[...], 1, 1)
        gv, uv = g_ref[...], u_ref[...]
        silu_g, dsilu_g = _silu_parts(gv)
        dg = (dh * uv * dsilu_g).astype(BF16)
        du = (dh * silu_g).astype(BF16)
        dg_ref[...] = dg
        du_ref[...] = du
        acc[...] += _dot(dg, wg_ref[...], 1, 1) + _dot(du, wu_ref[...], 1, 1)

        @pl.when(j == N_SHARD - 1)
        def _():
            dhx = acc[...]
            xv = x1_ref[...]
            r = lax.rsqrt(_lanemean(xv * xv) + EPS)
            n0 = xv * r
            nw = nw2_ref[...]
            dn2 = dhx * (1.0 + sc2_ref[...])
            dn0 = dn2 * nw
            dx1_ref[...] = dx2_ref[...] + r * (dn0 - n0 * _lanemean(dn0 * n0))
            sums_ref[1:2, :] += _rowsum(dhx)
            sums_ref[2:3, :] += _rowsum(dhx * n0 * nw)
            sums_ref[3:4, :] += _rowsum(dn2 * n0)

    row = pl.BlockSpec((tm, D), lambda i, j: (i, 0))
    vec = pl.BlockSpec((1, D), lambda i, j: (0, 0))
    sh = pl.BlockSpec((None, tm, FF_SH), lambda i, j: (j, i, 0))
    return pl.pallas_call(
        body, name=name,
        grid=(L // tm, N_SHARD),
        in_specs=[row, row, row, sh, sh, vec, vec, vec,
                  pl.BlockSpec((None, D, FF_SH), lambda i, j: (j, 0, 0)),
                  pl.BlockSpec((None, D, FF_SH), lambda i, j: (j, 0, 0)),
                  pl.BlockSpec((None, FF_SH, D), lambda i, j: (j, 0, 0))],
        out_specs=[row, sh, sh, row, pl.BlockSpec((8, D), lambda i, j: (0, 0))],
        out_shape=[jax.ShapeDtypeStruct((L, D), BF16),
                   jax.ShapeDtypeStruct((N_SHARD, L, FF_SH), BF16), jax.ShapeDtypeStruct((N_SHARD, L, FF_SH), BF16),
                   jax.ShapeDtypeStruct((L, D), F32), jax.ShapeDtypeStruct((8, D), F32)],
        scratch_shapes=[pltpu.VMEM((tm, D), BF16), pltpu.VMEM((tm, D), F32)],
        compiler_params=_params("arbitrary", "arbitrary"),
    )(dx2, x1, f, g, u, nw2, sc2, g2, wg, wu, wd)


def matmul_tn(a, b, name, acc_init=None):
    na, K, M = a.shape
    nb, _, N = b.shape
    n = max(na, nb)
    tk = min(512, K)
    tn = N if N <= 1024 else N // 2
    nk = K // tk
    has_init = acc_init is not None

    def body(*refs):
        if has_init:
            a_ref, b_ref, init_ref, o_ref = refs
        else:
            a_ref, b_ref, o_ref = refs
        kk = pl.program_id(2)

        @pl.when(kk == 0)
        def _():
            o_ref[...] = init_ref[...] if has_init else jnp.zeros_like(o_ref)

        o_ref[...] += _dot(a_ref[...], b_ref[...], 0, 0)

    out_spec = pl.BlockSpec((None, M, tn), lambda s, j, kk: (s, 0, j))
    in_specs = [pl.BlockSpec((None, tk, M), lambda s, j, kk: (s if na > 1 else 0, kk, 0)),
                pl.BlockSpec((None, tk, tn), lambda s, j, kk: (s if nb > 1 else 0, kk, j))]
    args = [a, b]
    if has_init:
        in_specs.append(out_spec)
        args.append(acc_init)
    return pl.pallas_call(
        body, name=name,
        grid=(n, N // tn, nk),
        in_specs=in_specs,
        out_specs=out_spec,
        out_shape=jax.ShapeDtypeStruct((n, M, N), F32),
        compiler_params=_params("parallel", "parallel", "arbitrary"),
    )(*args)


def dhx_normbwd(dp4, w4, x, dx_res, nw, sc, name):
    L = x.shape[0]
    tm = min(512, L)
    tn = IN_SH // 2

    def body(dp_ref, w_ref, x_ref, res_ref, nw_ref, sc_ref, dx_ref, sums_ref, acc):
        i, k, j = pl.program_id(0), pl.program_id(1), pl.program_id(2)
        first = (k == 0) & (j == 0)

        @pl.when((i == 0) & first)
        def _():
            sums_ref[...] = jnp.zeros_like(sums_ref)

        @pl.when(first)
        def _():
            acc[...] = jnp.zeros_like(acc)

        acc[...] += _dot(dp_ref[...], w_ref[...], 1, 1)

        @pl.when((k == N_SHARD - 1) & (j == 1))
        def _():
            dhx = acc[...]
            xv = x_ref[...]
            r = lax.rsqrt(_lanemean(xv * xv) + EPS)
            n0 = xv * r
            nw = nw_ref[...]
            dn = dhx * (1.0 + sc_ref[...])
            dn0 = dn * nw
            dx_ref[...] = res_ref[...] + r * (dn0 - n0 * _lanemean(dn0 * n0))
            sums_ref[0:1, :] += _rowsum(dhx)
            sums_ref[1:2, :] += _rowsum(dhx * n0 * nw)
            sums_ref[2:3, :] += _rowsum(dn * n0)

    row = pl.BlockSpec((tm, D), lambda i, k, j: (i, 0))
    vec = pl.BlockSpec((1, D), lambda i, k, j: (0, 0))
    return pl.pallas_call(
        body, name=name,
        grid=(L // tm, N_SHARD, 2),
        in_specs=[pl.BlockSpec((None, tm, tn), lambda i, k, j: (k, i, j)),
                  pl.BlockSpec((None, D, tn), lambda i, k, j: (k, 0, j)),
                  row, row, vec, vec],
        out_specs=[row, pl.BlockSpec((8, D), lambda i, k, j: (0, 0))],
        out_shape=[jax.ShapeDtypeStruct((L, D), F32), jax.ShapeDtypeStruct((8, D), F32)],
        scratch_shapes=[pltpu.VMEM((tm, D), F32)],
        compiler_params=_params("arbitrary", "arbitrary", "arbitrary"),
    )(dp4, w4, x, dx_res, nw, sc)


SMALL_ROWS = 24


def _rope_tables(L):
    rows = L // 64
    row = jnp.repeat(jnp.arange(rows, dtype=F32), 64)
    col = jnp.tile(jnp.arange(64, dtype=F32), rows)
    freqs = 10000.0 ** (-jnp.arange(RT_DK // 4, dtype=F32) / (RT_DK // 4))
    ang = jnp.concatenate([row[:, None] * freqs, col[:, None] * freqs], axis=-1)
    cos, sin = jnp.cos(ang), jnp.sin(ang)
    return jnp.concatenate([cos, cos], axis=1), jnp.concatenate([-sin, sin], axis=1)


def _shard_major(pieces):
    dp = jnp.concatenate(pieces, axis=1)
    return dp.reshape(dp.shape[0], N_SHARD, IN_SH).transpose(1, 0, 2)


def _lane0(a):
    return a[:, 0, 0]


def _pack_small(rows):
    out = [r.reshape(1, D) for r in rows]
    out += [jnp.zeros((1, D), F32)] * (SMALL_ROWS - len(out))
    return jnp.concatenate(out, axis=0)


def local_step(x, ctx, target, mod_x, mod_c, lb_f, lb_b, lg_f, lg_b, nw1, nw2, hgw, fw, w):
    L, Lc = x.shape[0], ctx.shape[0]
    sh1, sc1, g1, sh2, sc2, g2 = (mod_x[i:i + 1] for i in range(6))
    sh1c, sc1c = mod_c[0:1], mod_c[1:2]
    cosf, sinf = _rope_tables(L)
    cosc, sinc = jnp.ones((Lc, RT_DK), F32), jnp.zeros((Lc, RT_DK), F32)
    zero_h = jnp.zeros((HEADS, HG_D, HG_D), F32)
    zero_r = jnp.zeros((HEADS, RT_DV, RT_DK), F32)

    pc, hxc = normmod_matmul(ctx, nw1, sh1c, sc1c, w["w_in"], "ctx_in_proj")
    _, s_hf, cb_hf = hgrn_scan_fwd(pc, lb_f, zero_h, COL_HFF, False, "ctx_hgrn_f")
    _, s_hb, cb_hb = hgrn_scan_fwd(pc, lb_b, zero_h, COL_HFB, True, "ctx_hgrn_b")
    _, s_rf, cb_rf = ret_scan_fwd(pc, cosc, sinc, lg_f, zero_r, False, "ctx_ret_f")
    _, s_rb, cb_rb = ret_scan_fwd(pc, cosc, sinc, lg_b, zero_r, True, "ctx_ret_b")
    p, hx = normmod_matmul(x, nw1, sh1, sc1, w["w_in"], "in_proj")
    ohf, _, xb_hf = hgrn_scan_fwd(p, lb_f, s_hf, COL_HFF, False, "hgrn_f")
    ohb, _, xb_hb = hgrn_scan_fwd(p, lb_b, s_hb, COL_HFB, True, "hgrn_b")
    orf, _, xb_rf = ret_scan_fwd(p, cosf, sinf, lg_f, s_rf, False, "ret_f")
    orb, _, xb_rb = ret_scan_fwd(p, cosf, sinf, lg_b, s_rb, True, "ret_b")
    x1, x_mix, merged, ya, yb = mix_fwd(ohf, ohb, orf, orb, p, x, g1, hgw, w["w_pa"], w["w_pb"], w["w_out"], "mix_fwd")
    hx2, gg, uu, hh, ff, dx2, sums_f = ffn_fwd(x1, target, nw2, sh2, sc2, g2, fw, w["wg"], w["wu"], w["wd"], "ffn_fwd")

    d_f, d_g, d_u, dx1, sums_fb = ffn_bwd(dx2, x1, ff, gg, uu, nw2, sc2, g2, w["wg"], w["wu"], w["wd"], "ffn_bwd")
    grads = {
        "wg": matmul_tn(hx2[None], d_g, "dw_ffn_gate"),
        "wu": matmul_tn(hx2[None], d_u, "dw_ffn_up"),
        "wd": matmul_tn(hh, d_f[None], "dw_ffn_down"),
    }
    dxm, d_a, d_b, dga, dgb, dhg, drg, dohg, dort, sums_m = mix_bwd(
        dx1, x_mix, ya, yb, ohf, ohb, orf, orb, p, g1, hgw, w["w_pa"], w["w_pb"], w["w_out"], "mix_bwd")
    grads["w_out"] = matmul_tn(merged[None], dxm[None], "dw_out").reshape(N_SHARD, D // N_SHARD, D)
    grads["w_pa"] = matmul_tn(ya[None], d_a[None], "dw_proj_hgrn").reshape(N_SHARD, D // N_SHARD, D)
    grads["w_pb"] = matmul_tn(yb[None], d_b[None], "dw_proj_ret").reshape(N_SHARD, 2 * D // N_SHARD, D)

    rq1, rk1, rv1, dlgf_x, ds_rf = ret_scan_bwd(p, cosf, sinf, lg_f, xb_rf, dort, zero_r, None, False, "ret_f_bwd")
    drq, drk, drv, dlgb_x, ds_rb = ret_scan_bwd(p, cosf, sinf, lg_b, xb_rb, dort, zero_r, (rq1, rk1, rv1), True, "ret_b_bwd")
    hq1, dzf, hv1, dlbf_x, ds_hf = hgrn_scan_bwd(p, lb_f, xb_hf, dohg, zero_h, None, COL_HFF, False, "hgrn_f_bwd")
    dhq, dzb, dhv, dlbb_x, ds_hb = hgrn_scan_bwd(p, lb_b, xb_hb, dohg, zero_h, (hq1, hv1), COL_HFB, True, "hgrn_b_bwd")
    dp4 = _shard_major([dhq, dzf, dzb, dhv, dhg, drq, drk, drv, drg, dga, dgb])
    dx, sums_x = dhx_normbwd(dp4, w["w_in"], x, dx1, nw1, sc1, "dx_in_proj")
    dw_in = matmul_tn(hx[None], dp4, "dw_in")

    zc = jnp.zeros((Lc, D), F32)
    zc2 = jnp.zeros((Lc, 2 * D), F32)
    crq1, crk1, crv1, dlgf_c, _ = ret_scan_bwd(pc, cosc, sinc, lg_f, cb_rf, zc2, ds_rf, None, False, "ctx_ret_f_bwd")
    cdrq, cdrk, cdrv, dlgb_c, _ = ret_scan_bwd(pc, cosc, sinc, lg_b, cb_rb, zc2, ds_rb, (crq1, crk1, crv1), True, "ctx_ret_b_bwd")
    chq1, cdzf, chv1, dlbf_c, _ = hgrn_scan_bwd(pc, lb_f, cb_hf, zc, ds_hf, None, COL_HFF, False, "ctx_hgrn_f_bwd")
    cdhq, cdzb, cdhv, dlbb_c, _ = hgrn_scan_bwd(pc, lb_b, cb_hb, zc, ds_hb, (chq1, chv1), COL_HFB, True, "ctx_hgrn_b_bwd")
    zb = jnp.zeros((Lc, D), BF16)
    zb2 = jnp.zeros((Lc, 2 * D), BF16)
    dpc4 = _shard_major([cdhq, cdzf, cdzb, cdhv, zb, cdrq, cdrk, cdrv, zb2, zb, zb])
    _, sums_c = dhx_normbwd(dpc4, w["w_in"], ctx, zc, nw1, sc1c, "dctx_in_proj")
    grads["w_in"] = matmul_tn(hxc[None], dpc4, "dw_in_ctx", acc_init=dw_in)

    def lg_row(f, b):
        return jnp.concatenate([_lane0(f), _lane0(b), jnp.zeros((D - 2 * HEADS,), F32)])

    small = _pack_small([
        sums_x[0], sums_x[1], sums_m[0], sums_fb[1], sums_fb[2], sums_fb[0],
        sums_c[0], sums_c[1],
        sums_x[2], sums_c[2], sums_fb[3], sums_m[1], sums_f[0],
        dlbf_x, dlbf_c, dlbb_x, dlbb_c,
        lg_row(dlgf_x, dlgb_x), lg_row(dlgf_c, dlgb_c),
        sums_f[1],
    ])
    return dx, grads, small


MESH = pl.DeviceIdType.MESH
ANY = pl.BlockSpec(memory_space=pl.ANY)
N_DEV = 8


def _place():
    return lax.axis_index("x"), lax.axis_index("y"), lax.axis_index("c")


def _other_chips(x, y):
    return [(1 - x, y), (x, 1 - y), (1 - x, 1 - y)]


def allgather8(xs, name):
    m, n = xs.shape

    def body(x_ref, out_ref, send_sems, recv_sems, local_sem):
        x, y, c = _place()
        me, sibling = (x, y, c), (x, y, 1 - c)
        chips = _other_chips(x, y)

        def rows(px, py, pc):
            return out_ref.at[pl.ds((4 * px + 2 * py + pc) * m, m), :]

        def copy(k, block, to, src=None):
            return pltpu.make_async_remote_copy(
                src_ref=rows(*block) if src is None else src, dst_ref=rows(*block),
                send_sem=send_sems.at[k], recv_sem=recv_sems.at[k], device_id=to, device_id_type=MESH)

        mine = pltpu.make_async_copy(x_ref, rows(*me), local_sem)
        mine.start()
        first = [copy(0, me, sibling, src=x_ref)]
        first += [copy(1 + j, me, (*chip, c), src=x_ref) for j, chip in enumerate(chips)]
        for cp in first:
            cp.start()
        passed = [copy(4 + j, (*chip, c), sibling) for j, chip in enumerate(chips)]
        for j, chip in enumerate(chips):
            copy(1 + j, (*chip, c), me).wait_recv()
            passed[j].start()
        copy(0, sibling, me).wait_recv()
        for j, chip in enumerate(chips):
            copy(4 + j, (*chip, 1 - c), me).wait_recv()
        for cp in first + passed:
            cp.wait_send()
        mine.wait()

    return pl.pallas_call(
        body, name=name,
        out_shape=jax.ShapeDtypeStruct((N_DEV * m, n), xs.dtype),
        in_specs=[pl.BlockSpec(memory_space=pltpu.VMEM)],
        out_specs=pl.BlockSpec(memory_space=pltpu.VMEM),
        scratch_shapes=[pltpu.SemaphoreType.DMA((7,)), pltpu.SemaphoreType.DMA((7,)), pltpu.SemaphoreType.DMA],
    )(xs)


def gather_weights(bufs, name):
    n = len(bufs)

    def body(*refs):
        outs = refs[n:2 * n]
        send_sems, recv_sems = refs[2 * n:]
        x, y, c = _place()
        chips = _other_chips(x, y)

        def half(i, chip_xy, core):
            h = bufs[i].shape[1] // 2
            return outs[i].at[2 * chip_xy[0] + chip_xy[1], pl.ds(pl.multiple_of(core * h, 16), h), :]

        def copy(i, k, piece, to):
            return pltpu.make_async_remote_copy(
                src_ref=piece, dst_ref=piece, send_sem=send_sems.at[6 * i + k], recv_sem=recv_sems.at[6 * i + k],
                device_id=to, device_id_type=MESH)

        started = []
        for i in range(n):
            for j, chip in enumerate(chips):
                cp = copy(i, j, half(i, (x, y), c), (*chip, c))
                cp.start()
                started.append(cp)
        for i in range(n):
            for j, chip in enumerate(chips):
                copy(i, j, half(i, chip, c), (*chip, c)).wait_recv()
                fw = copy(i, 3 + j, half(i, chip, c), (x, y, 1 - c))
                fw.start()
                started.append(fw)
        for i in range(n):
            for j, chip in enumerate(chips):
                copy(i, 3 + j, half(i, chip, 1 - c), (x, y, 1 - c)).wait_recv()
        for cp in started:
            cp.wait_send()

    return pl.pallas_call(
        body, name=name,
        out_shape=[jax.ShapeDtypeStruct(b.shape, b.dtype) for b in bufs],
        in_specs=[ANY] * n, out_specs=[ANY] * n,
        input_output_aliases={i: i for i in range(n)},
        scratch_shapes=[pltpu.SemaphoreType.DMA((6 * n,)), pltpu.SemaphoreType.DMA((6 * n,))],
    )(*bufs)


def rs_to_sibling(payloads, name):
    n = len(payloads)

    def body(*refs):
        ins, outs = refs[:n], refs[n:2 * n]
        send_sems, recv_sems = refs[2 * n:]
        x, y, c = _place()
        copies = []
        for i in range(n):
            cp = pltpu.make_async_remote_copy(src_ref=ins[i], dst_ref=outs[i], send_sem=send_sems.at[i],
                                              recv_sem=recv_sems.at[i], device_id=(x, y, 1 - c), device_id_type=MESH)
            cp.start()
            copies.append(cp)
        for cp in copies:
            cp.wait()

    return pl.pallas_call(
        body, name=name,
        out_shape=[jax.ShapeDtypeStruct(g.shape, g.dtype) for g in payloads],
        in_specs=[ANY] * n, out_specs=[ANY] * n,
        scratch_shapes=[pltpu.SemaphoreType.DMA((n,)), pltpu.SemaphoreType.DMA((n,))],
    )(*payloads)


def rs_to_chips(parts, name):
    n = len(parts)

    def body(*refs):
        ins, outs = refs[:n], refs[n:2 * n]
        send_sems, recv_sems = refs[2 * n:]
        x, y, c = _place()
        copies = []
        for i in range(n):
            for j, (px, py) in enumerate(_other_chips(x, y)):
                cp = pltpu.make_async_remote_copy(
                    src_ref=ins[i].at[2 * px + py], dst_ref=outs[i].at[j], send_sem=send_sems.at[3 * i + j],
                    recv_sem=recv_sems.at[3 * i + j], device_id=(px, py, c), device_id_type=MESH)
                cp.start()
                copies.append(cp)
        for cp in copies:
            cp.wait()

    return pl.pallas_call(
        body, name=name,
        out_shape=[jax.ShapeDtypeStruct((3,) + a.shape[1:], a.dtype) for a in parts],
        in_specs=[ANY] * n, out_specs=[ANY] * n,
        scratch_shapes=[pltpu.SemaphoreType.DMA((3 * n,)), pltpu.SemaphoreType.DMA((3 * n,))],
    )(*parts)


def rs_join_halves(fulls, name):
    n = len(fulls)

    def body(*refs):
        outs = refs[n:2 * n]
        send_sems, recv_sems = refs[2 * n:]
        x, y, c = _place()

        def copy(i, core):
            h = fulls[i].shape[0] // 2
            rows = outs[i].at[pl.ds(pl.multiple_of(core * h, 8), h), :]
            return pltpu.make_async_remote_copy(src_ref=rows, dst_ref=rows, send_sem=send_sems.at[i],
                                                recv_sem=recv_sems.at[i], device_id=(x, y, 1 - c), device_id_type=MESH)

        sent = [copy(i, c) for i in range(n)]
        for cp in sent:
            cp.start()
        for i in range(n):
            copy(i, 1 - c).wait_recv()
        for cp in sent:
            cp.wait_send()

    return pl.pallas_call(
        body, name=name,
        out_shape=[jax.ShapeDtypeStruct(a.shape, a.dtype) for a in fulls],
        in_specs=[ANY] * n, out_specs=[ANY] * n,
        input_output_aliases={i: i for i in range(n)},
        scratch_shapes=[pltpu.SemaphoreType.DMA((n,)), pltpu.SemaphoreType.DMA((n,))],
    )(*fulls)


def _row_tile(rows, cols, limit_bytes=2 * 1024 * 1024, mult=8):
    best = mult
    for t in range(mult, rows + 1, mult):
        if rows % t == 0 and t * cols * 4 <= limit_bytes:
            best = t
    return best


def rs_add_sibling(g, recv, c, name):
    _, R, C = g.shape
    h = R // 2
    tr = _row_tile(h, C, mult=16)
    nt = h // tr

    def body(c_ref, g_ref, r_ref, o_ref, o16_ref):
        s = g_ref[...] + r_ref[...].astype(F32)
        o_ref[...] = s
        o16_ref[...] = s.astype(BF16)

    blk = pl.BlockSpec((None, tr, C), lambda k, i, c_ref: (k, i, 0))
    return pl.pallas_call(
        body, name=name,
        grid_spec=pltpu.PrefetchScalarGridSpec(
            num_scalar_prefetch=1, grid=(N_SHARD, nt),
            in_specs=[pl.BlockSpec((None, tr, C), lambda k, i, c_ref: (k, c_ref[0] * nt + i, 0)), blk],
            out_specs=[blk, blk]),
        out_shape=[jax.ShapeDtypeStruct((N_SHARD, h, C), F32), jax.ShapeDtypeStruct((N_SHARD, h, C), BF16)],
        compiler_params=_params("parallel", "parallel"),
    )(c, g, recv)


def rs_add_chips(part, recv, place, name):
    _, h, C = part.shape
    tr = _row_tile(h, C, mult=16)
    nt = h // tr

    def body(k_ref, p_ref, r_ref, o_ref):
        o_ref[...] = ((p_ref[...] + r_ref[0].astype(F32)) + r_ref[1].astype(F32)) + r_ref[2].astype(F32)

    return pl.pallas_call(
        body, name=name,
        grid_spec=pltpu.PrefetchScalarGridSpec(
            num_scalar_prefetch=1, grid=(nt,),
            in_specs=[pl.BlockSpec((None, tr, C), lambda i, k_ref: (k_ref[0], i, 0)),
                      pl.BlockSpec((3, tr, C), lambda i, k_ref: (0, i, 0))],
            out_specs=pl.BlockSpec((tr, C), lambda i, k_ref: (k_ref[1] * nt + i, 0))),
        out_shape=jax.ShapeDtypeStruct((2 * h, C), F32),
        compiler_params=_params("parallel"),
    )(place, part, recv)


def _adamw_math(w, g, m, v):
    m = ADAM_B1 * m + (1.0 - ADAM_B1) * g
    v = ADAM_B2 * v + (1.0 - ADAM_B2) * (g * g)
    m_hat = m / (1.0 - ADAM_B1 ** ADAM_STEP)
    v_hat = v / (1.0 - ADAM_B2 ** ADAM_STEP)
    delta = -ADAM_LR * (m_hat / (jnp.sqrt(v_hat) + ADAM_EPS) + ADAM_WD * w)
    return delta, m, v


def adamw(w, g, m, v, name):
    R, C = w.shape
    tr = _row_tile(R, C, 1024 * 1024)

    def body(w_ref, g_ref, m_ref, v_ref, d_ref, nm_ref, nv_ref):
        d_ref[...], nm_ref[...], nv_ref[...] = _adamw_math(w_ref[...], g_ref[...], m_ref[...], v_ref[...])

    blk = pl.BlockSpec((tr, C), lambda i: (i, 0))
    return pl.pallas_call(
        body, name=name, grid=(R // tr,), in_specs=[blk] * 4, out_specs=[blk] * 3,
        out_shape=[jax.ShapeDtypeStruct((R, C), F32)] * 3,
        compiler_params=_params("parallel"),
    )(w, g, m, v)


MOD_SH = 6 * D // N_SHARD
PK_ROWS = 16


def mod_fwd(call16, w_sh, b_sh, name):
    def body(c_ref, w_ref, b_ref, o_ref):
        o_ref[...] = _dot(_silu_parts(c_ref[...])[0], w_ref[...], prec=HI) + b_ref[...]

    return pl.pallas_call(body, name=name, out_shape=jax.ShapeDtypeStruct((16, MOD_SH), F32),
                          compiler_params=_params())(call16, w_sh, b_sh)


def prep_small(lbf2, lbb2, theta_row, name):
    def body(f_ref, b_ref, t_ref, lbf_ref, lbb_ref, lg_ref):
        lbf_ref[...] = _sigmoid(f_ref[0:1, :] - f_ref[1:2, :])
        lbb_ref[...] = _sigmoid(b_ref[0:1, :] - b_ref[1:2, :])
        t = t_ref[...]
        lg_ref[...] = jnp.minimum(t, 0.0) - jnp.log(1.0 + jnp.exp(-jnp.abs(t)))

    row = jax.ShapeDtypeStruct((1, D), F32)
    return pl.pallas_call(body, name=name, out_shape=[row, row, row], compiler_params=_params())(lbf2, lbb2, theta_row)


def small_grads(g3, lbf, lbb, theta_row, name):
    def body(g_ref, lbf_ref, lbb_ref, t_ref, pk_ref, aux_ref):
        s = g_ref[0]
        for d in range(1, N_DEV):
            s = s + g_ref[d]
        pk_ref[...] = jnp.zeros_like(pk_ref)
        aux_ref[...] = jnp.zeros_like(aux_ref)
        pk_ref[1:7, :] = s[0:6]
        pk_ref[1:3, :] += s[6:8]
        pk_ref[7:8, :] = s[8:9] + s[9:10]
        pk_ref[8:9, :] = s[10:11]
        lbf, lbb = lbf_ref[...], lbb_ref[...]
        daf = (s[13:14] + s[14:15]) * lbf * (1.0 - lbf)
        dab = (s[15:16] + s[16:17]) * lbb * (1.0 - lbb)
        pk_ref[9:10, :] = daf
        pk_ref[10:11, :] = -daf
        pk_ref[11:12, :] = dab
        pk_ref[12:13, :] = -dab
        pk_ref[13:14, :] = s[11:12]
        pk_ref[14:15, :] = (s[17:18] + s[18:19]) * _sigmoid(-t_ref[...])
        pk_ref[15:16, :] = s[12:13]
        aux_ref[0:2, :] = s[6:8]
        aux_ref[2:3, :] = jnp.broadcast_to(jnp.sum(s[19:20], axis=-1, keepdims=True), (1, D))

    return pl.pallas_call(body, name=name,
                          out_shape=[jax.ShapeDtypeStruct((PK_ROWS, D), F32), jax.ShapeDtypeStruct((8, D), F32)],
                          compiler_params=_params())(g3, lbf, lbb, theta_row)


def mod_bwd(call16, dmod_sh, w_sh, name):
    def body(c_ref, d_ref, w_ref, dw_ref, ds_ref):
        dm = d_ref[...]
        dw_ref[...] = _dot(_silu_parts(c_ref[...])[0], dm, 0, 0, prec=HI)
        ds_ref[...] = jnp.zeros_like(ds_ref)
        ds_ref[0:1, :] = _dot(dm[8:9, :], w_ref[...], 1, 1, prec=HI)

    return pl.pallas_call(body, name=name,
                          out_shape=[jax.ShapeDtypeStruct((D, MOD_SH), F32), jax.ShapeDtypeStruct((8, D), F32)],
                          compiler_params=_params())(call16, dmod_sh, w_sh)


def adamw_small(g4, pk_g, pk_w, pk_m, pk_v, name):
    def body(g4_ref, g_ref, w_ref, m_ref, v_ref, go_ref, d_ref, nm_ref, nv_ref):
        w = w_ref[...]
        ds = ((g4_ref[0:1, :] + g4_ref[16:17, :]) + g4_ref[32:33, :]) + g4_ref[48:49, :]
        row = lax.broadcasted_iota(jnp.int32, (PK_ROWS, D), 0)
        g = jnp.where(row == 0, ds * _silu_parts(w[0:1, :])[1], g_ref[...])
        go_ref[...] = g
        d_ref[...], nm_ref[...], nv_ref[...] = _adamw_math(w, g, m_ref[...], v_ref[...])

    pk = jax.ShapeDtypeStruct((PK_ROWS, D), F32)
    return pl.pallas_call(body, name=name, out_shape=[pk, pk, pk, pk], compiler_params=_params())(g4, pk_g, pk_w, pk_m, pk_v)


def _pack_params(c_ctx, b_mod, n1, n2, lbf, lbb, hgn, th_f, th_b, fin):
    theta = jnp.concatenate([th_f.reshape(HEADS), th_b.reshape(HEADS), jnp.zeros((D - 2 * HEADS,), F32)])
    return jnp.concatenate([c_ctx.reshape(1, D), b_mod.reshape(6, D), n1.reshape(1, D), n2.reshape(1, D), lbf, lbb,
                            hgn.reshape(1, D), theta.reshape(1, D), fin.reshape(1, D)], axis=0)


def _unpack_params(pk):
    return (pk[0], pk[1:7].reshape(1, 6 * D), pk[7:8], pk[8:9], pk[9:11], pk[11:13], pk[13:14],
            pk[14, 0:HEADS].reshape(1, HEADS), pk[14, HEADS:2 * HEADS].reshape(1, HEADS), pk[15])


def kernel(x, c, ctx, c_ctx, w_mod, b_mod, norm1_w, norm2_w, w_in, hg_lb_fwd, hg_lb_bwd, hg_norm_w, rt_theta_fwd, rt_theta_bwd, w_proj_hgrn, w_proj_ret, w_out, w_ffn_gate, w_ffn_up, w_ffn_down, final_norm_w, loss_target, m_c_ctx, m_w_mod, m_b_mod, m_norm1_w, m_norm2_w, m_w_in, m_hg_lb_fwd, m_hg_lb_bwd, m_hg_norm_w, m_rt_theta_fwd, m_rt_theta_bwd, m_w_proj_hgrn, m_w_proj_ret, m_w_out, m_w_ffn_gate, m_w_ffn_up, m_w_ffn_down, m_final_norm_w, v_c_ctx, v_w_mod, v_b_mod, v_norm1_w, v_norm2_w, v_w_in, v_hg_lb_fwd, v_hg_lb_bwd, v_hg_norm_w, v_rt_theta_fwd, v_rt_theta_bwd, v_w_proj_hgrn, v_w_proj_ret, v_w_out, v_w_ffn_gate, v_w_ffn_up, v_w_ffn_down, v_final_norm_w):
    xi, yi, ci = _place()
    dev = 4 * xi + 2 * yi + ci
    chip = 2 * xi + yi
    core_arg = jnp.reshape(ci, (1,)).astype(jnp.int32)
    place_arg = jnp.stack([chip, ci]).astype(jnp.int32)

    c_all = allgather8(jnp.concatenate([c, jnp.zeros((7, D), F32)], axis=0), "gather_c").reshape(N_DEV, 8, D)[:, 0]
    call16 = jnp.concatenate([c_all, c_ctx.reshape(1, D), jnp.zeros((7, D), F32)], axis=0)
    b_sh = lax.dynamic_slice_in_dim(b_mod, chip * MOD_SH, MOD_SH, axis=1)
    mod_sh = mod_fwd(call16, w_mod[0], b_sh, "mod_fwd")
    mod_g = allgather8(mod_sh, "gather_mod").reshape(N_DEV, 16, MOD_SH)
    mod_all = jnp.concatenate([mod_g[0], mod_g[2], mod_g[4], mod_g[6]], axis=1)
    mod_x = lax.dynamic_index_in_dim(mod_all, dev, axis=0, keepdims=False).reshape(6, D)
    mod_c = mod_all[8].reshape(6, D)

    pk_w = _pack_params(c_ctx, b_mod, norm1_w, norm2_w, hg_lb_fwd, hg_lb_bwd, hg_norm_w, rt_theta_fwd, rt_theta_bwd, final_norm_w)
    theta_row = pk_w[14:15]
    lb_f, lb_b, lg_row = prep_small(hg_lb_fwd, hg_lb_bwd, theta_row, "prep_small")
    lg_f = jnp.broadcast_to(lg_row[0, 0:HEADS].reshape(HEADS, 1, 1), (HEADS, 1, RT_DV))
    lg_b = jnp.broadcast_to(lg_row[0, HEADS:2 * HEADS].reshape(HEADS, 1, 1), (HEADS, 1, RT_DV))

    shards = [w_in[0], w_proj_hgrn[0], w_proj_ret[0], w_out[0], w_ffn_gate[0], w_ffn_up[0], w_ffn_down[0]]
    placed = [lax.dynamic_update_index_in_dim(jnp.zeros((N_SHARD,) + s.shape, BF16), s.astype(BF16), chip, 0)
              for s in shards]
    g_in, g_pa, g_pb, g_out, g_wg, g_wu, g_wd = gather_weights(placed, "gather_weights")
    w = {"w_in": g_in, "w_pa": g_pa.reshape(D, D), "w_pb": g_pb.reshape(2 * D, D), "w_out": g_out.reshape(D, D),
         "wg": g_wg, "wu": g_wu, "wd": g_wd}

    dx, grads, small = local_step(x[0], ctx[0], loss_target[0], mod_x, mod_c, lb_f, lb_b, lg_f, lg_b,
                                  norm1_w, norm2_w, hg_norm_w, final_norm_w.reshape(1, D), w)

    order = ["w_in", "w_pa", "w_pb", "w_out", "wg", "wu", "wd"]
    gs = [grads[k] for k in order]
    to_sibling = [lax.dynamic_slice_in_dim(g, (1 - ci) * (g.shape[1] // 2), g.shape[1] // 2, axis=1).astype(BF16)
                  for g in gs]
    from_sibling = rs_to_sibling(to_sibling, "rs_to_sibling")
    chip_sums = [rs_add_sibling(g, r, core_arg, "rs_add_sibling_" + k) for g, r, k in zip(gs, from_sibling, order)]
    from_chips = rs_to_chips([a16 for _, a16 in chip_sums], "rs_to_chips")
    halves = [rs_add_chips(a, r, place_arg, "rs_add_chips_" + k) for (a, _), r, k in zip(chip_sums, from_chips, order)]
    full = dict(zip(order, rs_join_halves(halves, "rs_join_halves")))

    g3 = allgather8(small, "gather_small").reshape(N_DEV, SMALL_ROWS, D)
    pk_g, aux = small_grads(g3, lb_f, lb_b, theta_row, "small_grads")
    loss = aux[2, 0]
    dmod16 = jnp.concatenate([
        g3[:, 0:6, :].reshape(N_DEV, 6 * D),
        jnp.concatenate([aux[0], aux[1], jnp.zeros((4 * D,), F32)]).reshape(1, 6 * D),
        jnp.zeros((7, 6 * D), F32)], axis=0)
    dmod_sh = lax.dynamic_slice_in_dim(dmod16, chip * MOD_SH, MOD_SH, axis=1)
    g_wmod, dsilu = mod_bwd(call16, dmod_sh, w_mod[0], "mod_bwd")
    g4 = allgather8(dsilu, "gather_dsilu")
    pk_m = _pack_params(m_c_ctx, m_b_mod, m_norm1_w, m_norm2_w, m_hg_lb_fwd, m_hg_lb_bwd, m_hg_norm_w, m_rt_theta_fwd, m_rt_theta_bwd, m_final_norm_w)
    pk_v = _pack_params(v_c_ctx, v_b_mod, v_norm1_w, v_norm2_w, v_hg_lb_fwd, v_hg_lb_bwd, v_hg_norm_w, v_rt_theta_fwd, v_rt_theta_bwd, v_final_norm_w)
    pk_g, pk_d, pk_nm, pk_nv = adamw_small(g4, pk_g, pk_w, pk_m, pk_v, "adamw_small")

    big = {
        "w_mod": (g_wmod, w_mod, m_w_mod, v_w_mod),
        "w_in": (full["w_in"], w_in, m_w_in, v_w_in),
        "w_pa": (full["w_pa"], w_proj_hgrn, m_w_proj_hgrn, v_w_proj_hgrn),
        "w_pb": (full["w_pb"], w_proj_ret, m_w_proj_ret, v_w_proj_ret),
        "w_out": (full["w_out"], w_out, m_w_out, v_w_out),
        "wg": (full["wg"], w_ffn_gate, m_w_ffn_gate, v_w_ffn_gate),
        "wu": (full["wu"], w_ffn_up, m_w_ffn_up, v_w_ffn_up),
        "wd": (full["wd"], w_ffn_down, m_w_ffn_down, v_w_ffn_down),
    }
    res = {}
    for k, (g, wt, mt, vt) in big.items():
        d, nm, nv = adamw(wt[0], g, mt[0], vt[0], "adamw_" + k)
        res[k] = (g[None], d[None], nm[None], nv[None])

    sm = [_unpack_params(p) for p in (pk_g, pk_d, pk_nm, pk_nv)]
    outs = []
    for t in range(4):
        (s_cctx, s_bmod, s_n1, s_n2, s_lbf, s_lbb, s_hgn, s_thf, s_thb, s_fin) = sm[t]
        outs.append([s_cctx, res["w_mod"][t], s_bmod, s_n1, s_n2, res["w_in"][t], s_lbf, s_lbb, s_hgn, s_thf, s_thb,
                     res["w_pa"][t], res["w_pb"][t], res["w_out"][t], res["wg"][t], res["wu"][t], res["wd"][t], s_fin])
    return (loss, dx[None], *outs[0], *outs[1], *outs[2], *outs[3])
```

```python
import functools

import jax
import jax.numpy as jnp
from jax import lax
from jax.experimental import pallas as pl
from jax.experimental.pallas import tpu as pltpu

F32 = jnp.float32
BF16 = jnp.bfloat16
HI = lax.Precision.HIGHEST

D = 1024
HEADS = 8
HG_D = 128
RT_DK = 128
RT_DV = 256
D_FF = 2816
D_IN = 13312
N_SHARD = 4
IN_SH = D_IN // N_SHARD
FF_SH = D_FF // N_SHARD
HG_CHUNK = 32
SCAN_ROWS = 256
HG_GROUP = 8
RT_GROUP = 4
PROJ_ROWS = 1024
EPS = 1e-6
GN_EPS = 1e-5
Q_SCALE = 128.0 ** -0.5
VMEM_LIMIT = 56 * 1024 * 1024

COL_HQ, COL_HFF, COL_HFB, COL_HI, COL_HG = 0, 8, 16, 24, 32
COL_RQ, COL_RK, COL_RV, COL_RG, COL_GA, COL_GB = 40, 48, 56, 72, 88, 96

ADAM_LR, ADAM_B1, ADAM_B2, ADAM_EPS, ADAM_WD, ADAM_STEP = 0.001, 0.9, 0.999, 1e-08, 0.01, 10


def _params(*sem):
    return pltpu.CompilerParams(dimension_semantics=sem, vmem_limit_bytes=VMEM_LIMIT)


def _dot(a, b, ca=1, cb=0, prec=None):
    return lax.dot_general(a, b, (((ca,), (cb,)), ((), ())), precision=prec, preferred_element_type=F32)


def _bdot(a, b, ca=1, cb=0):
    return _dot(a.astype(BF16), b.astype(BF16), ca, cb)


def _sigmoid(z):
    return 1.0 / (1.0 + jnp.exp(-z))


def _rowsum(a):
    return jnp.sum(a, axis=0, keepdims=True)


def _lanemean(a):
    return jnp.mean(a, axis=-1, keepdims=True)


def normmod_matmul(x, nw, sh, sc, w4, name):
    L = x.shape[0]
    tm = min(PROJ_ROWS, L)
    tn = IN_SH // 2

    def body(x_ref, nw_ref, sh_ref, sc_ref, w_ref, p_ref, hx_ref, hx_scr):
        @pl.when((pl.program_id(1) == 0) & (pl.program_id(2) == 0))
        def _():
            xv = x_ref[...]
            n = xv * lax.rsqrt(_lanemean(xv * xv) + EPS) * nw_ref[...]
            h = (n * (1.0 + sc_ref[...]) + sh_ref[...]).astype(BF16)
            hx_scr[...] = h
            hx_ref[...] = h

        p_ref[...] = _dot(hx_scr[...], w_ref[...])

    vec = pl.BlockSpec((1, D), lambda i, k, j: (0, 0))
    return pl.pallas_call(
        body, name=name,
        grid=(L // tm, N_SHARD, 2),
        in_specs=[pl.BlockSpec((tm, D), lambda i, k, j: (i, 0)), vec, vec, vec,
                  pl.BlockSpec((None, D, tn), lambda i, k, j: (k, 0, j))],
        out_specs=[pl.BlockSpec((tm, tn), lambda i, k, j: (i, 2 * k + j)),
                   pl.BlockSpec((tm, D), lambda i, k, j: (i, 0))],
        out_shape=[jax.ShapeDtypeStruct((L, D_IN), F32), jax.ShapeDtypeStruct((L, D), BF16)],
        scratch_shapes=[pltpu.VMEM((tm, D), BF16)],
        compiler_params=_params("parallel", "arbitrary", "arbitrary"),
    )(x, nw, sh, sc, w4)


def _hgrn_gates(z, lb):
    sg = _sigmoid(z)
    sgn = _sigmoid(-z)
    f = lb + (1.0 - lb) * sg
    k = (1.0 - lb) * sgn
    return sg, sgn, f, k


def _tri_chunks(n, chunk, reverse):
    r = lax.broadcasted_iota(jnp.int32, (n, n), 0)
    c = lax.broadcasted_iota(jnp.int32, (n, n), 1)
    same = (r // chunk) == (c // chunk)
    return jnp.where(same & ((r <= c) if reverse else (r >= c)), 1.0, 0.0).astype(F32)


def _decay3(b, reverse):
    C = b.shape[0]
    t = lax.broadcasted_iota(jnp.int32, (C, C, 1), 0)
    s = lax.broadcasted_iota(jnp.int32, (C, C, 1), 1)
    mask = (t <= s) if reverse else (t >= s)
    return jnp.exp(jnp.where(mask, b[:, None, :] - b[None, :, :], -jnp.inf))


HG_SUB = 16


def _hgrn_pairs(reverse):
    pairs = []
    size = HG_SUB
    while size < HG_CHUNK:
        for lo in range(0, HG_CHUNK, 2 * size):
            first, second = slice(lo, lo + size), slice(lo + size, lo + 2 * size)
            if reverse:
                pairs.append((first, second, lo + size))
            else:
                pairs.append((second, first, lo + size - 1))
        size *= 2
    return pairs


def _hgrn_intra_fwd(q, k, v, b, reverse):
    blocks = []
    for lo in range(0, HG_CHUNK, HG_SUB):
        r = slice(lo, lo + HG_SUB)
        att3 = jnp.sum(q[r][:, None, :] * k[r][None, :, :] * _decay3(b[r], reverse), axis=-1, keepdims=True)
        blocks.append(jnp.sum(att3 * v[r][None, :, :], axis=1))
    for qr, kr, ref in _hgrn_pairs(reverse):
        beta = b[ref:ref + 1]
        att = _bdot(q[qr] * jnp.exp(b[qr] - beta), k[kr] * jnp.exp(beta - b[kr]), 1, 1)
        part = _bdot(att, v[kr])
        n = part.shape[0] // HG_SUB
        for i in range(n):
            blocks[qr.start // HG_SUB + i] += part[i * HG_SUB:(i + 1) * HG_SUB]
    return jnp.concatenate(blocks, axis=0)


def _hgrn_intra_bwd(q, k, v, b, d_o, reverse):
    nb = HG_CHUNK // HG_SUB
    dq, dk, dv = [None] * nb, [None] * nb, [None] * nb
    for i in range(nb):
        r = slice(i * HG_SUB, (i + 1) * HG_SUB)
        e3 = _decay3(b[r], reverse)
        p3 = jnp.sum(d_o[r][:, None, :] * v[r][None, :, :], axis=-1, keepdims=True) * e3
        dq[i] = jnp.sum(p3 * k[r][None, :, :], axis=1)
        dk[i] = jnp.sum(p3 * q[r][:, None, :], axis=0)
        att3 = jnp.sum(q[r][:, None, :] * k[r][None, :, :] * e3, axis=-1, keepdims=True)
        dv[i] = jnp.sum(att3 * d_o[r][:, None, :], axis=0)

    def add(acc, rows, part):
        for i in range(part.shape[0] // HG_SUB):
            acc[rows.start // HG_SUB + i] += part[i * HG_SUB:(i + 1) * HG_SUB]

    for qr, kr, ref in _hgrn_pairs(reverse):
        beta = b[ref:ref + 1]
        fq, fk = jnp.exp(b[qr] - beta), jnp.exp(beta - b[kr])
        qt, kt = q[qr] * fq, k[kr] * fk
        att = _bdot(qt, kt, 1, 1)
        datt = _bdot(d_o[qr], v[kr], 1, 1)
        add(dq, qr, _bdot(datt, kt) * fq)
        add(dk, kr, _bdot(datt, qt, 0, 0) * fk)
        add(dv, kr, _bdot(att, d_o[qr], 0, 0))
    return jnp.concatenate(dq, axis=0), jnp.concatenate(dk, axis=0), jnp.concatenate(dv, axis=0)


def _hgrn_state_step(k, v, b, s_t, last):
    b_last = b[last:last + 1]
    return s_t * jnp.exp(b_last) + _bdot(v, k * jnp.exp(b_last - b), 0, 0)


def hgrn_scan_fwd(p, lb, s0, col_z, reverse, name):
    L = p.shape[0]
    nB = L // SCAN_ROWS
    nC = SCAN_ROWS // HG_CHUNK
    C = HG_CHUNK
    G, W = HG_GROUP, HG_GROUP * HG_D
    last = 0 if reverse else C - 1

    def bmap(b):
        return (nB - 1 - b) if reverse else b

    def body(q_ref, z_ref, v_ref, lb_ref, s0_ref, o_ref, sfin_ref, sblk_ref, s_scr, k_scr, b_scr):
        blk = pl.program_id(1)

        @pl.when(blk == 0)
        def _():
            s_scr[...] = s0_ref[...]

        sblk_ref[...] = s_scr[...]
        _, _, f_all, k_all = _hgrn_gates(z_ref[...], lb_ref[...])
        k_scr[...] = k_all
        b_scr[...] = _dot(_tri_chunks(SCAN_ROWS, C, reverse), jnp.log(f_all), prec=HI)

        def chunk(ci, carry):
            c = (nC - 1 - ci) if reverse else ci
            rows = pl.ds(pl.multiple_of(c * C, C), C)
            for j in range(G):
                lanes = slice(j * HG_D, (j + 1) * HG_D)
                q = q_ref[rows, lanes] * Q_SCALE
                v = v_ref[rows, lanes]
                k = k_scr[rows, lanes]
                b = b_scr[rows, lanes]
                s_t = s_scr[j]
                o_ref[rows, lanes] = _hgrn_intra_fwd(q, k, v, b, reverse) + _bdot(q * jnp.exp(b), s_t, 1, 1)
                s_scr[j] = _hgrn_state_step(k, v, b, s_t, last)
            return carry

        lax.fori_loop(0, nC, chunk, 0)

        @pl.when(blk == nB - 1)
        def _():
            sfin_ref[...] = s_scr[...]

    def col(c0):
        return pl.BlockSpec((SCAN_ROWS, W), lambda h, b: (bmap(b), c0 // G + h))

    state = pl.BlockSpec((G, HG_D, HG_D), lambda h, b: (h, 0, 0))
    return pl.pallas_call(
        body, name=name,
        grid=(HEADS // G, nB),
        in_specs=[col(COL_HQ), col(col_z), col(COL_HI), pl.BlockSpec((1, W), lambda h, b: (0, h)), state],
        out_specs=[pl.BlockSpec((SCAN_ROWS, W), lambda h, b: (bmap(b), h)), state,
                   pl.BlockSpec((None, G, HG_D, HG_D), lambda h, b: (bmap(b), h, 0, 0))],
        out_shape=[jax.ShapeDtypeStruct((L, D), F32),
                   jax.ShapeDtypeStruct((HEADS, HG_D, HG_D), F32),
                   jax.ShapeDtypeStruct((nB, HEADS, HG_D, HG_D), F32)],
        scratch_shapes=[pltpu.VMEM((G, HG_D, HG_D), F32), pltpu.VMEM((SCAN_ROWS, W), F32),
                        pltpu.VMEM((SCAN_ROWS, W), F32)],
        compiler_params=_params("parallel", "arbitrary"),
    )(p, p, p, lb, s0)


def hgrn_scan_bwd(p, lb, s_blocks, d_o, ds_fin, prev, col_z, reverse, name):
    L = p.shape[0]
    nB = L // SCAN_ROWS
    nC = SCAN_ROWS // HG_CHUNK
    C = HG_CHUNK
    G, W = HG_GROUP, HG_GROUP * HG_D
    last = 0 if reverse else C - 1
    has_prev = prev is not None
    out_dt = BF16 if has_prev else F32

    def bmap(b):
        return b if reverse else (nB - 1 - b)

    def body(*refs):
        q_ref, z_ref, v_ref, lb_ref, sblk_ref, do_ref, dsf_ref = refs[:7]
        refs = refs[7:]
        if has_prev:
            pq_ref, pv_ref = refs[:2]
            refs = refs[2:]
        dq_ref, dz_ref, dv_ref, dlb_ref, ds0_ref, st_scr, run_scr, ds_scr, k_scr, b_scr, db_scr, dk_scr = refs
        blk = pl.program_id(1)

        @pl.when(blk == 0)
        def _():
            ds_scr[...] = dsf_ref[...]
            dlb_ref[...] = jnp.zeros_like(dlb_ref)

        tri = _tri_chunks(SCAN_ROWS, C, reverse)
        row = lax.broadcasted_iota(jnp.int32, (C, HG_D), 0)
        _, _, f_all, k_all = _hgrn_gates(z_ref[...], lb_ref[...])
        k_scr[...] = k_all
        b_scr[...] = _dot(tri, jnp.log(f_all), prec=HI)
        run_scr[...] = sblk_ref[...]

        def recompute(ci, carry):
            c = (nC - 1 - ci) if reverse else ci
            rows = pl.ds(pl.multiple_of(c * C, C), C)
            for j in range(G):
                lanes = slice(j * HG_D, (j + 1) * HG_D)
                s_t = run_scr[j]
                st_scr[c, j] = s_t
                run_scr[j] = _hgrn_state_step(k_scr[rows, lanes], v_ref[rows, lanes], b_scr[rows, lanes], s_t, last)
            return carry

        lax.fori_loop(0, nC, recompute, 0)

        def chunk(ci, carry):
            c = ci if reverse else (nC - 1 - ci)
            rows = pl.ds(pl.multiple_of(c * C, C), C)
            for j in range(G):
                lanes = slice(j * HG_D, (j + 1) * HG_D)
                k = k_scr[rows, lanes]
                b = b_scr[rows, lanes]
                q = q_ref[rows, lanes] * Q_SCALE
                v = v_ref[rows, lanes]
                d_o = do_ref[rows, lanes]
                s_t = st_scr[c, j]
                ds_t = ds_scr[j]
                eb = jnp.exp(b)
                b_last = b[last:last + 1]
                eb_last = jnp.exp(b_last)
                kdec = jnp.exp(b_last - b)
                qe = q * eb
                ke = k * kdec
                dq_in, dk_in, dv_in = _hgrn_intra_bwd(q, k, v, b, d_o, reverse)
                dq_tot = _bdot(d_o, s_t, 1, 0) * eb + dq_in
                dke = _bdot(v, ds_t, 1, 0)
                dk_tot = dke * kdec + dk_in
                dv = dv_in + _bdot(ke, ds_t, 1, 1)
                db_last = _rowsum(dke * ke) + eb_last * _rowsum(ds_t * s_t)
                db_scr[rows, lanes] = q * dq_tot - k * dk_tot + jnp.where(row == last, db_last, 0.0)
                dk_scr[rows, lanes] = dk_tot
                dq = dq_tot * Q_SCALE
                if has_prev:
                    dq = dq + pq_ref[rows, lanes]
                    dv = dv + pv_ref[rows, lanes]
                dq_ref[rows, lanes] = dq.astype(out_dt)
                dv_ref[rows, lanes] = dv.astype(out_dt)
                ds_scr[j] = ds_t * eb_last + _bdot(d_o, qe, 0, 0)
            return carry

        lax.fori_loop(0, nC, chunk, 0)

        lb = lb_ref[...]
        sg, sgn, f, _ = _hgrn_gates(z_ref[...], lb)
        g = _dot(tri, db_scr[...], 0, 0, prec=HI) / f - dk_scr[...]
        dz_ref[...] = (g * (1.0 - lb) * sg * sgn).astype(BF16)
        dlb_ref[...] += _rowsum(g * sgn)

        @pl.when(blk == nB - 1)
        def _():
            ds0_ref[...] = ds_scr[...]

    def col(c0):
        return pl.BlockSpec((SCAN_ROWS, W), lambda h, b: (bmap(b), c0 // G + h))

    tile = pl.BlockSpec((SCAN_ROWS, W), lambda h, b: (bmap(b), h))
    state = pl.BlockSpec((G, HG_D, HG_D), lambda h, b: (h, 0, 0))
    in_specs = [col(COL_HQ), col(col_z), col(COL_HI),
                pl.BlockSpec((1, W), lambda h, b: (0, h)),
                pl.BlockSpec((None, G, HG_D, HG_D), lambda h, b: (bmap(b), h, 0, 0)),
                tile, state]
    args = [p, p, p, lb, s_blocks, d_o, ds_fin]
    if has_prev:
        in_specs += [tile, tile]
        args += list(prev)
    return pl.pallas_call(
        body, name=name,
        grid=(HEADS // G, nB),
        in_specs=in_specs,
        out_specs=[tile, tile, tile, pl.BlockSpec((1, W), lambda h, b: (0, h)), state],
        out_shape=[jax.ShapeDtypeStruct((L, D), out_dt), jax.ShapeDtypeStruct((L, D), BF16),
                   jax.ShapeDtypeStruct((L, D), out_dt), jax.ShapeDtypeStruct((1, D), F32),
                   jax.ShapeDtypeStruct((HEADS, HG_D, HG_D), F32)],
        scratch_shapes=[pltpu.VMEM((nC, G, HG_D, HG_D), F32), pltpu.VMEM((G, HG_D, HG_D), F32),
                        pltpu.VMEM((G, HG_D, HG_D), F32)] + [pltpu.VMEM((SCAN_ROWS, W), F32)] * 4,
        compiler_params=_params("parallel", "arbitrary"),
    )(*args)


def _rope(t, cosf, sinf):
    return t * cosf + pltpu.roll(t, RT_DK // 2, 1) * sinf


def _rope_t(d, cosf, sinf):
    return d * cosf + pltpu.roll(d * sinf, RT_DK // 2, 1)


def _ret_decays(lg, reverse):
    C = SCAN_ROWS
    t = lax.broadcasted_iota(jnp.int32, (C, C), 0)
    s = lax.broadcasted_iota(jnp.int32, (C, C), 1)
    delta = ((s - t) if reverse else (t - s)).astype(F32)
    dmat = jnp.where(delta >= 0, jnp.exp(lg * jnp.maximum(delta, 0.0)), 0.0)
    r = lax.broadcasted_iota(jnp.int32, (C, RT_DK), 0)
    pos = ((C - 1 - r) if reverse else r).astype(F32)
    lg1 = lg[:, :RT_DK]
    qdec = jnp.exp(lg1 * (pos + 1.0))
    kdec = jnp.exp(lg1 * (C - 1.0 - pos))
    sdec = jnp.exp(lg1 * float(C))
    return dmat, delta, pos, qdec, kdec, sdec


def ret_scan_fwd(p, cosf, sinf, lg, s0, reverse, name):
    L = p.shape[0]
    C = SCAN_ROWS
    nB = L // C

    def bmap(b):
        return (nB - 1 - b) if reverse else b

    G = RT_GROUP

    def body(q_ref, k_ref, v_ref, cos_ref, sin_ref, lg_ref, s0_ref, o_ref, sfin_ref, sblk_ref, s_scr):
        blk = pl.program_id(1)

        @pl.when(blk == 0)
        def _():
            s_scr[...] = s0_ref[...]

        sblk_ref[...] = s_scr[...]
        cosf, sinf = cos_ref[...], sin_ref[...]
        for j in range(G):
            lk, lv = slice(j * RT_DK, (j + 1) * RT_DK), slice(j * RT_DV, (j + 1) * RT_DV)
            s_t = s_scr[j]
            dmat, _, _, qdec, kdec, sdec = _ret_decays(lg_ref[j], reverse)
            q = _rope(q_ref[:, lk] * Q_SCALE, cosf, sinf)
            k = _rope(k_ref[:, lk], cosf, sinf)
            v = v_ref[:, lv]
            att = _bdot(q, k, 1, 1) * dmat
            o_ref[:, lv] = _bdot(att, v) + _bdot(q * qdec, s_t, 1, 1)
            s_scr[j] = s_t * sdec + _bdot(v, k * kdec, 0, 0)

        @pl.when(blk == nB - 1)
        def _():
            sfin_ref[...] = s_scr[...]

    def col(c0):
        return pl.BlockSpec((C, G * RT_DK), lambda h, b: (bmap(b), c0 // G + h))

    tab = pl.BlockSpec((C, RT_DK), lambda h, b: (bmap(b), 0))
    state = pl.BlockSpec((G, RT_DV, RT_DK), lambda h, b: (h, 0, 0))
    return pl.pallas_call(
        body, name=name,
        grid=(HEADS // G, nB),
        in_specs=[col(COL_RQ), col(COL_RK),
                  pl.BlockSpec((C, G * RT_DV), lambda h, b: (bmap(b), COL_RV // (2 * G) + h)),
                  tab, tab, pl.BlockSpec((G, 1, RT_DV), lambda h, b: (h, 0, 0)), state],
        out_specs=[pl.BlockSpec((C, G * RT_DV), lambda h, b: (bmap(b), h)), state,
                   pl.BlockSpec((None, G, RT_DV, RT_DK), lambda h, b: (bmap(b), h, 0, 0))],
        out_shape=[jax.ShapeDtypeStruct((L, HEADS * RT_DV), F32),
                   jax.ShapeDtypeStruct((HEADS, RT_DV, RT_DK), F32),
                   jax.ShapeDtypeStruct((nB, HEADS, RT_DV, RT_DK), F32)],
        scratch_shapes=[pltpu.VMEM((G, RT_DV, RT_DK), F32)],
        compiler_params=_params("parallel", "arbitrary"),
    )(p, p, p, cosf, sinf, lg, s0)


def ret_scan_bwd(p, cosf, sinf, lg, s_blocks, d_o, ds_fin, prev, reverse, name):
    L = p.shape[0]
    C = SCAN_ROWS
    nB = L // C
    has_prev = prev is not None
    out_dt = BF16 if has_prev else F32
    G = RT_GROUP

    def bmap(b):
        return b if reverse else (nB - 1 - b)

    def body(*refs):
        q_ref, k_ref, v_ref, cos_ref, sin_ref, lg_ref, sblk_ref, do_ref, dsf_ref = refs[:9]
        refs = refs[9:]
        if has_prev:
            pq_ref, pk_ref, pv_ref = refs[:3]
            refs = refs[3:]
        dq_ref, dk_ref, dv_ref, dlg_ref, ds0_ref, ds_scr = refs
        blk = pl.program_id(1)

        @pl.when(blk == 0)
        def _():
            ds_scr[...] = dsf_ref[...]
            dlg_ref[...] = jnp.zeros_like(dlg_ref)

        cosf, sinf = cos_ref[...], sin_ref[...]
        for j in range(G):
            lk, lv = slice(j * RT_DK, (j + 1) * RT_DK), slice(j * RT_DV, (j + 1) * RT_DV)
            s_t = sblk_ref[j]
            ds_t = ds_scr[j]
            dmat, delta, pos, qdec, kdec, sdec = _ret_decays(lg_ref[j], reverse)
            q = _rope(q_ref[:, lk] * Q_SCALE, cosf, sinf)
            k = _rope(k_ref[:, lk], cosf, sinf)
            v = v_ref[:, lv]
            d_o = do_ref[:, lv]
            att_raw = _bdot(q, k, 1, 1)
            datt_m = _bdot(d_o, v, 1, 1) * dmat
            dqd = _bdot(d_o, s_t, 1, 0)
            dkd = _bdot(v, ds_t, 1, 0)
            dq = _bdot(datt_m, k) + dqd * qdec
            dk = _bdot(datt_m, q, 0, 0) + dkd * kdec
            dv = _bdot(att_raw * dmat, d_o, 0, 0) + _bdot(k * kdec, ds_t, 1, 1)
            ds_scr[j] = ds_t * sdec + _bdot(d_o, q * qdec, 0, 0)
            t1 = jnp.sum(_rowsum(datt_m * att_raw * delta), axis=-1, keepdims=True)
            t23 = jnp.sum(_rowsum((pos + 1.0) * qdec * q * dqd + (C - 1.0 - pos) * kdec * k * dkd), axis=-1, keepdims=True)
            t4 = jnp.sum(_rowsum(ds_t * s_t * sdec), axis=-1, keepdims=True) * float(C)
            dlg_ref[j] += jnp.broadcast_to(t1 + t23 + t4, (1, RT_DK))
            if has_prev:
                dq = _rope_t(dq + pq_ref[:, lk], cosf, sinf) * Q_SCALE
                dk = _rope_t(dk + pk_ref[:, lk], cosf, sinf)
                dv = dv + pv_ref[:, lv]
            dq_ref[:, lk] = dq.astype(out_dt)
            dk_ref[:, lk] = dk.astype(out_dt)
            dv_ref[:, lv] = dv.astype(out_dt)

        @pl.when(blk == nB - 1)
        def _():
            ds0_ref[...] = ds_scr[...]

    def col(c0):
        return pl.BlockSpec((C, G * RT_DK), lambda h, b: (bmap(b), c0 // G + h))

    tab = pl.BlockSpec((C, RT_DK), lambda h, b: (bmap(b), 0))
    state = pl.BlockSpec((G, RT_DV, RT_DK), lambda h, b: (h, 0, 0))
    tk = pl.BlockSpec((C, G * RT_DK), lambda h, b: (bmap(b), h))
    tv = pl.BlockSpec((C, G * RT_DV), lambda h, b: (bmap(b), h))
    in_specs = [col(COL_RQ), col(COL_RK),
                pl.BlockSpec((C, G * RT_DV), lambda h, b: (bmap(b), COL_RV // (2 * G) + h)),
                tab, tab, pl.BlockSpec((G, 1, RT_DV), lambda h, b: (h, 0, 0)),
                pl.BlockSpec((None, G, RT_DV, RT_DK), lambda h, b: (bmap(b), h, 0, 0)),
                tv, state]
    args = [p, p, p, cosf, sinf, lg, s_blocks, d_o, ds_fin]
    if has_prev:
        in_specs += [tk, tk, tv]
        args += list(prev)
    return pl.pallas_call(
        body, name=name,
        grid=(HEADS // G, nB),
        in_specs=in_specs,
        out_specs=[tk, tk, tv, pl.BlockSpec((G, 1, RT_DK), lambda h, b: (h, 0, 0)), state],
        out_shape=[jax.ShapeDtypeStruct((L, D), out_dt), jax.ShapeDtypeStruct((L, D), out_dt),
                   jax.ShapeDtypeStruct((L, HEADS * RT_DV), out_dt),
                   jax.ShapeDtypeStruct((HEADS, 1, RT_DK), F32),
                   jax.ShapeDtypeStruct((HEADS, RT_DV, RT_DK), F32)],
        scratch_shapes=[pltpu.VMEM((G, RT_DV, RT_DK), F32)],
        compiler_params=_params("parallel", "arbitrary"),
    )(*args)


def _silu_parts(h):
    s = _sigmoid(h)
    return h * s, s * (1.0 + h * (1.0 - s))


def _head_rms(o):
    outs, rs = [], []
    for h in range(HEADS):
        oh = o[:, h * HG_D:(h + 1) * HG_D]
        r = lax.rsqrt(_lanemean(oh * oh) + EPS)
        outs.append(oh * r)
        rs.append(r)
    return outs, rs


def _group_norm(o):
    outs, rs = [], []
    for h in range(HEADS):
        oh = o[:, h * RT_DV:(h + 1) * RT_DV]
        c = oh - _lanemean(oh)
        r = lax.rsqrt(_lanemean(c * c) + GN_EPS)
        outs.append(c * r)
        rs.append(r)
    return outs, rs


MIX_ROWS = 256
MIX_BWD_ROWS = 128


def _mix_specs(rows):
    def t(w, c=0):
        return pl.BlockSpec((rows, w), lambda i: (i, c))

    return t


def mix_fwd(ohf, ohb, orf, orb, p, x, g1, hgw, w_pa, w_pb, w_out, name):
    L = x.shape[0]
    t = _mix_specs(MIX_ROWS)

    def body(ohf_ref, ohb_ref, orf_ref, orb_ref, hg_ref, rg0_ref, rg1_ref, ga_ref, gb_ref, x_ref, g1_ref, hgw_ref,
             wpa_ref, wpb_ref, wout_ref, x1_ref, xmix_ref, merged_ref, ya_ref, yb_ref):
        nh, _ = _head_rms(ohf_ref[...] + ohb_ref[...])
        ya = jnp.concatenate(nh, axis=1) * hgw_ref[...] * _silu_parts(hg_ref[...])[0]
        gn, _ = _group_norm(orf_ref[...] + orb_ref[...])
        rg = jnp.concatenate([rg0_ref[...], rg1_ref[...]], axis=1)
        yb = jnp.concatenate(gn, axis=1) * _silu_parts(rg)[0]
        ya16, yb16 = ya.astype(BF16), yb.astype(BF16)
        merged = (_sigmoid(ga_ref[...]) * _dot(ya16, wpa_ref[...])
                  + _sigmoid(gb_ref[...]) * _dot(yb16, wpb_ref[...])).astype(BF16)
        x_mix = _dot(merged, wout_ref[...])
        x1_ref[...] = x_ref[...] + g1_ref[...] * x_mix
        xmix_ref[...] = x_mix
        merged_ref[...] = merged
        ya_ref[...] = ya16
        yb_ref[...] = yb16

    vec = pl.BlockSpec((1, D), lambda i: (0, 0))

    def full(a):
        return pl.BlockSpec(a.shape, lambda i: (0, 0), pipeline_mode=pl.Buffered(1))

    return pl.pallas_call(
        body, name=name,
        grid=(L // MIX_ROWS,),
        in_specs=[t(D), t(D), t(2 * D), t(2 * D), t(D, COL_HG // 8), t(D, COL_RG // 8), t(D, COL_RG // 8 + 1),
                  t(D, COL_GA // 8), t(D, COL_GB // 8), t(D), vec, vec, full(w_pa), full(w_pb), full(w_out)],
        out_specs=[t(D), t(D), t(D), t(D), t(2 * D)],
        out_shape=[jax.ShapeDtypeStruct((L, D), F32), jax.ShapeDtypeStruct((L, D), F32),
                   jax.ShapeDtypeStruct((L, D), BF16), jax.ShapeDtypeStruct((L, D), BF16),
                   jax.ShapeDtypeStruct((L, 2 * D), BF16)],
        compiler_params=_params("parallel"),
    )(ohf, ohb, orf, orb, p, p, p, p, p, x, g1, hgw, w_pa, w_pb, w_out)


def mix_bwd(dx1, x_mix, ya, yb, ohf, ohb, orf, orb, p, g1, hgw, w_pa, w_pb, w_out, name):
    L = dx1.shape[0]
    t = _mix_specs(MIX_BWD_ROWS)

    def body(dx1_ref, xmix_ref, ya_ref, yb_ref, ohf_ref, ohb_ref, orf_ref, orb_ref, hg_ref, rg0_ref, rg1_ref,
             ga_ref, gb_ref, g1_ref, hgw_ref, wpa_ref, wpb_ref, wout_ref,
             dxm_ref, da_ref, db_ref, dga_ref, dgb_ref, dhg_ref, drg_ref, dohg_ref, dort_ref, sums_ref):
        @pl.when(pl.program_id(0) == 0)
        def _():
            sums_ref[...] = jnp.zeros_like(sums_ref)

        dx1 = dx1_ref[...]
        dxm = (g1_ref[...] * dx1).astype(BF16)
        dxm_ref[...] = dxm
        dmerged = _dot(dxm, wout_ref[...], 1, 1)
        a = _dot(ya_ref[...], wpa_ref[...])
        bm = _dot(yb_ref[...], wpb_ref[...])
        sa, sb = _sigmoid(ga_ref[...]), _sigmoid(gb_ref[...])
        d_a = (dmerged * sa).astype(BF16)
        d_b = (dmerged * sb).astype(BF16)
        da_ref[...] = d_a
        db_ref[...] = d_b
        dga_ref[...] = (dmerged * a * sa * (1.0 - sa)).astype(BF16)
        dgb_ref[...] = (dmerged * bm * sb * (1.0 - sb)).astype(BF16)
        dya = _dot(d_a, wpa_ref[...], 1, 1)
        dyb = _dot(d_b, wpb_ref[...], 1, 1)

        hgw = hgw_ref[...]
        silu_h, dsilu_h = _silu_parts(hg_ref[...])
        nh, rh = _head_rms(ohf_ref[...] + ohb_ref[...])
        n = jnp.concatenate(nh, axis=1)
        dhg_ref[...] = (dya * n * hgw * dsilu_h).astype(BF16)
        dn = dya * hgw * silu_h
        douts = []
        for h in range(HEADS):
            dnh = dn[:, h * HG_D:(h + 1) * HG_D]
            douts.append(rh[h] * (dnh - nh[h] * _lanemean(dnh * nh[h])))
        dohg_ref[...] = jnp.concatenate(douts, axis=1)

        rg = jnp.concatenate([rg0_ref[...], rg1_ref[...]], axis=1)
        silu_r, dsilu_r = _silu_parts(rg)
        gn, rr = _group_norm(orf_ref[...] + orb_ref[...])
        g = jnp.concatenate(gn, axis=1)
        drg_ref[...] = (dyb * g * dsilu_r).astype(BF16)
        dgn = dyb * silu_r
        douts = []
        for h in range(HEADS):
            dgh = dgn[:, h * RT_DV:(h + 1) * RT_DV]
            douts.append(rr[h] * (dgh - _lanemean(dgh) - gn[h] * _lanemean(dgh * gn[h])))
        dort_ref[...] = jnp.concatenate(douts, axis=1)

        sums_ref[0:1, :] += _rowsum(dx1 * xmix_ref[...])
        sums_ref[1:2, :] += _rowsum(dya * n * silu_h)

    vec = pl.BlockSpec((1, D), lambda i: (0, 0))

    def full(a):
        return pl.BlockSpec(a.shape, lambda i: (0, 0), pipeline_mode=pl.Buffered(1))

    bf = functools.partial(jax.ShapeDtypeStruct, dtype=BF16)
    return pl.pallas_call(
        body, name=name,
        grid=(L // MIX_BWD_ROWS,),
        in_specs=[t(D), t(D), t(D), t(2 * D), t(D), t(D), t(2 * D), t(2 * D),
                  t(D, COL_HG // 8), t(D, COL_RG // 8), t(D, COL_RG // 8 + 1), t(D, COL_GA // 8), t(D, COL_GB // 8),
                  vec, vec, full(w_pa), full(w_pb), full(w_out)],
        out_specs=[t(D), t(D), t(D), t(D), t(D), t(D), t(2 * D), t(D), t(2 * D),
                   pl.BlockSpec((8, D), lambda i: (0, 0))],
        out_shape=[bf((L, D)), bf((L, D)), bf((L, D)), bf((L, D)), bf((L, D)), bf((L, D)), bf((L, 2 * D)),
                   jax.ShapeDtypeStruct((L, D), F32), jax.ShapeDtypeStruct((L, 2 * D), F32),
                   jax.ShapeDtypeStruct((8, D), F32)],
        compiler_params=_params("arbitrary"),
    )(dx1, x_mix, ya, yb, ohf, ohb, orf, orb, p, p, p, p, p, g1, hgw, w_pa, w_pb, w_out)


FFN_ROWS = 512


def ffn_fwd(x1, target, nw2, sh2, sc2, g2, fw, wg, wu, wd, name):
    L = x1.shape[0]
    tm = FFN_ROWS

    def body(x1_ref, tgt_ref, nw2_ref, sh2_ref, sc2_ref, g2_ref, fw_ref, wg_ref, wu_ref, wd_ref,
             hx2_ref, g_ref, u_ref, h_ref, f_ref, dx2_ref, sums_ref, hx_scr, acc):
        i, j = pl.program_id(0), pl.program_id(1)

        @pl.when((i == 0) & (j == 0))
        def _():
            sums_ref[...] = jnp.zeros_like(sums_ref)

        @pl.when(j == 0)
        def _():
            xv = x1_ref[...]
            n = xv * lax.rsqrt(_lanemean(xv * xv) + EPS) * nw2_ref[...]
            h = (n * (1.0 + sc2_ref[...]) + sh2_ref[...]).astype(BF16)
            hx_scr[...] = h
            hx2_ref[...] = h
            acc[...] = jnp.zeros_like(acc)

        hx = hx_scr[...]
        g = _dot(hx, wg_ref[...])
        u = _dot(hx, wu_ref[...])
        hh = (_silu_parts(g)[0] * u).astype(BF16)
        g_ref[...] = g
        u_ref[...] = u
        h_ref[...] = hh
        acc[...] += _dot(hh, wd_ref[...])

        @pl.when(j == N_SHARD - 1)
        def _():
            f = acc[...]
            f_ref[...] = f
            x2 = x1_ref[...] + g2_ref[...] * f
            r = lax.rsqrt(_lanemean(x2 * x2) + EPS)
            fw = fw_ref[...]
            e = x2 * r * fw - tgt_ref[...]
            dy = e * (1.0 / D)
            dyw = dy * fw
            dx2_ref[...] = r * dyw - x2 * (r * r * r) * _lanemean(dyw * x2)
            sums_ref[0:1, :] += _rowsum(dy * x2 * r)
            sums_ref[1:2, :] += _rowsum(e * e) * (0.5 / D)

    row = pl.BlockSpec((tm, D), lambda i, j: (i, 0))
    vec = pl.BlockSpec((1, D), lambda i, j: (0, 0))
    sh = pl.BlockSpec((None, tm, FF_SH), lambda i, j: (j, i, 0))
    return pl.pallas_call(
        body, name=name,
        grid=(L // tm, N_SHARD),
        in_specs=[row, row, vec, vec, vec, vec, vec,
                  pl.BlockSpec((None, D, FF_SH), lambda i, j: (j, 0, 0)),
                  pl.BlockSpec((None, D, FF_SH), lambda i, j: (j, 0, 0)),
                  pl.BlockSpec((None, FF_SH, D), lambda i, j: (j, 0, 0))],
        out_specs=[row, sh, sh, sh, row, row, pl.BlockSpec((8, D), lambda i, j: (0, 0))],
        out_shape=[jax.ShapeDtypeStruct((L, D), BF16),
                   jax.ShapeDtypeStruct((N_SHARD, L, FF_SH), F32), jax.ShapeDtypeStruct((N_SHARD, L, FF_SH), F32),
                   jax.ShapeDtypeStruct((N_SHARD, L, FF_SH), BF16),
                   jax.ShapeDtypeStruct((L, D), F32), jax.ShapeDtypeStruct((L, D), F32),
                   jax.ShapeDtypeStruct((8, D), F32)],
        scratch_shapes=[pltpu.VMEM((tm, D), BF16), pltpu.VMEM((tm, D), F32)],
        compiler_params=_params("arbitrary", "arbitrary"),
    )(x1, target, nw2, sh2, sc2, g2, fw, wg, wu, wd)


def ffn_bwd(dx2, x1, f, g, u, nw2, sc2, g2, wg, wu, wd, name):
    L = x1.shape[0]
    tm = FFN_ROWS

    def body(dx2_ref, x1_ref, f_ref, g_ref, u_ref, nw2_ref, sc2_ref, g2_ref, wg_ref, wu_ref, wd_ref,
             df_ref, dg_ref, du_ref, dx1_ref, sums_ref, df_scr, acc):
        i, j = pl.program_id(0), pl.program_id(1)

        @pl.when((i == 0) & (j == 0))
        def _():
            sums_ref[...] = jnp.zeros_like(sums_ref)

        @pl.when(j == 0)
        def _():
            dx2 = dx2_ref[...]
            df = (g2_ref[...] * dx2).astype(BF16)
            df_scr[...] = df
            df_ref[...] = df
            sums_ref[0:1, :] += _rowsum(dx2 * f_ref[...])
            acc[...] = jnp.zeros_like(acc)

        dh = _dot(df_scr[...], wd_ref[...], 1, 1)
        gv, uv = g_ref[...], u_ref[...]
        silu_g, dsilu_g = _silu_parts(gv)
        dg = (dh * uv * dsilu_g).astype(BF16)
        du = (dh * silu_g).astype(BF16)
        dg_ref[...] = dg
        du_ref[...] = du
        acc[...] += _dot(dg, wg_ref[...], 1, 1) + _dot(du, wu_ref[...], 1, 1)

        @pl.when(j == N_SHARD - 1)
        def _():
            dhx = acc[...]
            xv = x1_ref[...]
            r = lax.rsqrt(_lanemean(xv * xv) + EPS)
            n0 = xv * r
            nw = nw2_ref[...]
            dn2 = dhx * (1.0 + sc2_ref[...])
            dn0 = dn2 * nw
            dx1_ref[...] = dx2_ref[...] + r * (dn0 - n0 * _lanemean(dn0 * n0))
            sums_ref[1:2, :] += _rowsum(dhx)
            sums_ref[2:3, :] += _rowsum(dhx * n0 * nw)
            sums_ref[3:4, :] += _rowsum(dn2 * n0)

    row = pl.BlockSpec((tm, D), lambda i, j: (i, 0))
    vec = pl.BlockSpec((1, D), lambda i, j: (0, 0))
    sh = pl.BlockSpec((None, tm, FF_SH), lambda i, j: (j, i, 0))
    return pl.pallas_call(
        body, name=name,
        grid=(L // tm, N_SHARD),
        in_specs=[row, row, row, sh, sh, vec, vec, vec,
                  pl.BlockSpec((None, D, FF_SH), lambda i, j: (j, 0, 0)),
                  pl.BlockSpec((None, D, FF_SH), lambda i, j: (j, 0, 0)),
                  pl.BlockSpec((None, FF_SH, D), lambda i, j: (j, 0, 0))],
        out_specs=[row, sh, sh, row, pl.BlockSpec((8, D), lambda i, j: (0, 0))],
        out_shape=[jax.ShapeDtypeStruct((L, D), BF16),
                   jax.ShapeDtypeStruct((N_SHARD, L, FF_SH), BF16), jax.ShapeDtypeStruct((N_SHARD, L, FF_SH), BF16),
                   jax.ShapeDtypeStruct((L, D), F32), jax.ShapeDtypeStruct((8, D), F32)],
        scratch_shapes=[pltpu.VMEM((tm, D), BF16), pltpu.VMEM((tm, D), F32)],
        compiler_params=_params("arbitrary", "arbitrary"),
    )(dx2, x1, f, g, u, nw2, sc2, g2, wg, wu, wd)


def matmul_tn(a, b, name, acc_init=None):
    na, K, M = a.shape
    nb, _, N = b.shape
    n = max(na, nb)
    tk = min(512, K)
    tn = N if N <= 1024 else N // 2
    nk = K // tk
    has_init = acc_init is not None

    def body(*refs):
        if has_init:
            a_ref, b_ref, init_ref, o_ref = refs
        else:
            a_ref, b_ref, o_ref = refs
        kk = pl.program_id(2)

        @pl.when(kk == 0)
        def _():
            o_ref[...] = init_ref[...] if has_init else jnp.zeros_like(o_ref)

        o_ref[...] += _dot(a_ref[...], b_ref[...], 0, 0)

    out_spec = pl.BlockSpec((None, M, tn), lambda s, j, kk: (s, 0, j))
    in_specs = [pl.BlockSpec((None, tk, M), lambda s, j, kk: (s if na > 1 else 0, kk, 0)),
                pl.BlockSpec((None, tk, tn), lambda s, j, kk: (s if nb > 1 else 0, kk, j))]
    args = [a, b]
    if has_init:
        in_specs.append(out_spec)
        args.append(acc_init)
    return pl.pallas_call(
        body, name=name,
        grid=(n, N // tn, nk),
        in_specs=in_specs,
        out_specs=out_spec,
        out_shape=jax.ShapeDtypeStruct((n, M, N), F32),
        compiler_params=_params("parallel", "parallel", "arbitrary"),
    )(*args)


def dhx_normbwd(dp4, w4, x, dx_res, nw, sc, name):
    L = x.shape[0]
    tm = min(PROJ_ROWS, L)
    tn = IN_SH // 2

    def body(dp_ref, w_ref, x_ref, res_ref, nw_ref, sc_ref, dx_ref, sums_ref, acc):
        i, k, j = pl.program_id(0), pl.program_id(1), pl.program_id(2)
        first = (k == 0) & (j == 0)

        @pl.when((i == 0) & first)
        def _():
            sums_ref[...] = jnp.zeros_like(sums_ref)

        @pl.when(first)
        def _():
            acc[...] = jnp.zeros_like(acc)

        acc[...] += _dot(dp_ref[...], w_ref[...], 1, 1)

        @pl.when((k == N_SHARD - 1) & (j == 1))
        def _():
            dhx = acc[...]
            xv = x_ref[...]
            r = lax.rsqrt(_lanemean(xv * xv) + EPS)
            n0 = xv * r
            nw = nw_ref[...]
            dn = dhx * (1.0 + sc_ref[...])
            dn0 = dn * nw
            dx_ref[...] = res_ref[...] + r * (dn0 - n0 * _lanemean(dn0 * n0))
            sums_ref[0:1, :] += _rowsum(dhx)
            sums_ref[1:2, :] += _rowsum(dhx * n0 * nw)
            sums_ref[2:3, :] += _rowsum(dn * n0)

    row = pl.BlockSpec((tm, D), lambda i, k, j: (i, 0))
    vec = pl.BlockSpec((1, D), lambda i, k, j: (0, 0))
    return pl.pallas_call(
        body, name=name,
        grid=(L // tm, N_SHARD, 2),
        in_specs=[pl.BlockSpec((None, tm, tn), lambda i, k, j: (k, i, j)),
                  pl.BlockSpec((None, D, tn), lambda i, k, j: (k, 0, j)),
                  row, row, vec, vec],
        out_specs=[row, pl.BlockSpec((8, D), lambda i, k, j: (0, 0))],
        out_shape=[jax.ShapeDtypeStruct((L, D), F32), jax.ShapeDtypeStruct((8, D), F32)],
        scratch_shapes=[pltpu.VMEM((tm, D), F32)],
        compiler_params=_params("arbitrary", "arbitrary", "arbitrary"),
    )(dp4, w4, x, dx_res, nw, sc)


SMALL_ROWS = 24


def _rope_tables(L):
    rows = L // 64
    row = jnp.repeat(jnp.arange(rows, dtype=F32), 64)
    col = jnp.tile(jnp.arange(64, dtype=F32), rows)
    freqs = 10000.0 ** (-jnp.arange(RT_DK // 4, dtype=F32) / (RT_DK // 4))
    ang = jnp.concatenate([row[:, None] * freqs, col[:, None] * freqs], axis=-1)
    cos, sin = jnp.cos(ang), jnp.sin(ang)
    return jnp.concatenate([cos, cos], axis=1), jnp.concatenate([-sin, sin], axis=1)


def _shard_major(pieces):
    dp = jnp.concatenate(pieces, axis=1)
    return dp.reshape(dp.shape[0], N_SHARD, IN_SH).transpose(1, 0, 2)


def _lane0(a):
    return a[:, 0, 0]


def _pack_small(rows):
    out = [r.reshape(1, D) for r in rows]
    out += [jnp.zeros((1, D), F32)] * (SMALL_ROWS - len(out))
    return jnp.concatenate(out, axis=0)


def local_step(x, ctx, target, mod_x, mod_c, lb_f, lb_b, lg_f, lg_b, nw1, nw2, hgw, fw, w):
    L, Lc = x.shape[0], ctx.shape[0]
    sh1, sc1, g1, sh2, sc2, g2 = (mod_x[i:i + 1] for i in range(6))
    sh1c, sc1c = mod_c[0:1], mod_c[1:2]
    cosf, sinf = _rope_tables(L)
    cosc, sinc = jnp.ones((Lc, RT_DK), F32), jnp.zeros((Lc, RT_DK), F32)
    zero_h = jnp.zeros((HEADS, HG_D, HG_D), F32)
    zero_r = jnp.zeros((HEADS, RT_DV, RT_DK), F32)

    pc, hxc = normmod_matmul(ctx, nw1, sh1c, sc1c, w["w_in"], "ctx_in_proj")
    _, s_hf, cb_hf = hgrn_scan_fwd(pc, lb_f, zero_h, COL_HFF, False, "ctx_hgrn_f")
    _, s_hb, cb_hb = hgrn_scan_fwd(pc, lb_b, zero_h, COL_HFB, True, "ctx_hgrn_b")
    _, s_rf, cb_rf = ret_scan_fwd(pc, cosc, sinc, lg_f, zero_r, False, "ctx_ret_f")
    _, s_rb, cb_rb = ret_scan_fwd(pc, cosc, sinc, lg_b, zero_r, True, "ctx_ret_b")
    p, hx = normmod_matmul(x, nw1, sh1, sc1, w["w_in"], "in_proj")
    ohf, _, xb_hf = hgrn_scan_fwd(p, lb_f, s_hf, COL_HFF, False, "hgrn_f")
    ohb, _, xb_hb = hgrn_scan_fwd(p, lb_b, s_hb, COL_HFB, True, "hgrn_b")
    orf, _, xb_rf = ret_scan_fwd(p, cosf, sinf, lg_f, s_rf, False, "ret_f")
    orb, _, xb_rb = ret_scan_fwd(p, cosf, sinf, lg_b, s_rb, True, "ret_b")
    x1, x_mix, merged, ya, yb = mix_fwd(ohf, ohb, orf, orb, p, x, g1, hgw, w["w_pa"], w["w_pb"], w["w_out"], "mix_fwd")
    hx2, gg, uu, hh, ff, dx2, sums_f = ffn_fwd(x1, target, nw2, sh2, sc2, g2, fw, w["wg"], w["wu"], w["wd"], "ffn_fwd")

    d_f, d_g, d_u, dx1, sums_fb = ffn_bwd(dx2, x1, ff, gg, uu, nw2, sc2, g2, w["wg"], w["wu"], w["wd"], "ffn_bwd")
    grads = {
        "wg": matmul_tn(hx2[None], d_g, "dw_ffn_gate"),
        "wu": matmul_tn(hx2[None], d_u, "dw_ffn_up"),
        "wd": matmul_tn(hh, d_f[None], "dw_ffn_down"),
    }
    dxm, d_a, d_b, dga, dgb, dhg, drg, dohg, dort, sums_m = mix_bwd(
        dx1, x_mix, ya, yb, ohf, ohb, orf, orb, p, g1, hgw, w["w_pa"], w["w_pb"], w["w_out"], "mix_bwd")
    grads["w_out"] = matmul_tn(merged[None], dxm[None], "dw_out").reshape(N_SHARD, D // N_SHARD, D)
    grads["w_pa"] = matmul_tn(ya[None], d_a[None], "dw_proj_hgrn").reshape(N_SHARD, D // N_SHARD, D)
    grads["w_pb"] = matmul_tn(yb[None], d_b[None], "dw_proj_ret").reshape(N_SHARD, 2 * D // N_SHARD, D)

    rq1, rk1, rv1, dlgf_x, ds_rf = ret_scan_bwd(p, cosf, sinf, lg_f, xb_rf, dort, zero_r, None, False, "ret_f_bwd")
    drq, drk, drv, dlgb_x, ds_rb = ret_scan_bwd(p, cosf, sinf, lg_b, xb_rb, dort, zero_r, (rq1, rk1, rv1), True, "ret_b_bwd")
    hq1, dzf, hv1, dlbf_x, ds_hf = hgrn_scan_bwd(p, lb_f, xb_hf, dohg, zero_h, None, COL_HFF, False, "hgrn_f_bwd")
    dhq, dzb, dhv, dlbb_x, ds_hb = hgrn_scan_bwd(p, lb_b, xb_hb, dohg, zero_h, (hq1, hv1), COL_HFB, True, "hgrn_b_bwd")
    dp4 = _shard_major([dhq, dzf, dzb, dhv, dhg, drq, drk, drv, drg, dga, dgb])
    dx, sums_x = dhx_normbwd(dp4, w["w_in"], x, dx1, nw1, sc1, "dx_in_proj")
    dw_in = matmul_tn(hx[None], dp4, "dw_in")

    zc = jnp.zeros((Lc, D), F32)
    zc2 = jnp.zeros((Lc, 2 * D), F32)
    crq1, crk1, crv1, dlgf_c, _ = ret_scan_bwd(pc, cosc, sinc, lg_f, cb_rf, zc2, ds_rf, None, False, "ctx_ret_f_bwd")
    cdrq, cdrk, cdrv, dlgb_c, _ = ret_scan_bwd(pc, cosc, sinc, lg_b, cb_rb, zc2, ds_rb, (crq1, crk1, crv1), True, "ctx_ret_b_bwd")
    chq1, cdzf, chv1, dlbf_c, _ = hgrn_scan_bwd(pc, lb_f, cb_hf, zc, ds_hf, None, COL_HFF, False, "ctx_hgrn_f_bwd")
    cdhq, cdzb, cdhv, dlbb_c, _ = hgrn_scan_bwd(pc, lb_b, cb_hb, zc, ds_hb, (chq1, chv1), COL_HFB, True, "ctx_hgrn_b_bwd")
    zb = jnp.zeros((Lc, D), BF16)
    zb2 = jnp.zeros((Lc, 2 * D), BF16)
    dpc4 = _shard_major([cdhq, cdzf, cdzb, cdhv, zb, cdrq, cdrk, cdrv, zb2, zb, zb])
    _, sums_c = dhx_normbwd(dpc4, w["w_in"], ctx, zc, nw1, sc1c, "dctx_in_proj")
    grads["w_in"] = matmul_tn(hxc[None], dpc4, "dw_in_ctx", acc_init=dw_in)

    def lg_row(f, b):
        return jnp.concatenate([_lane0(f), _lane0(b), jnp.zeros((D - 2 * HEADS,), F32)])

    small = _pack_small([
        sums_x[0], sums_x[1], sums_m[0], sums_fb[1], sums_fb[2], sums_fb[0],
        sums_c[0], sums_c[1],
        sums_x[2], sums_c[2], sums_fb[3], sums_m[1], sums_f[0],
        dlbf_x, dlbf_c, dlbb_x, dlbb_c,
        lg_row(dlgf_x, dlgb_x), lg_row(dlgf_c, dlgb_c),
        sums_f[1],
    ])
    return dx, grads, small


MESH = pl.DeviceIdType.MESH
ANY = pl.BlockSpec(memory_space=pl.ANY)
N_DEV = 8


def _place():
    return lax.axis_index("x"), lax.axis_index("y"), lax.axis_index("c")


def _other_chips(x, y):
    return [(1 - x, y), (x, 1 - y), (1 - x, 1 - y)]


def allgather8(xs, name):
    m, n = xs.shape

    def body(x_ref, out_ref, send_sems, recv_sems, local_sem):
        x, y, c = _place()
        me, sibling = (x, y, c), (x, y, 1 - c)
        chips = _other_chips(x, y)

        def rows(px, py, pc):
            return out_ref.at[pl.ds((4 * px + 2 * py + pc) * m, m), :]

        def copy(k, block, to, src=None):
            return pltpu.make_async_remote_copy(
                src_ref=rows(*block) if src is None else src, dst_ref=rows(*block),
                send_sem=send_sems.at[k], recv_sem=recv_sems.at[k], device_id=to, device_id_type=MESH)

        mine = pltpu.make_async_copy(x_ref, rows(*me), local_sem)
        mine.start()
        first = [copy(0, me, sibling, src=x_ref)]
        first += [copy(1 + j, me, (*chip, c), src=x_ref) for j, chip in enumerate(chips)]
        for cp in first:
            cp.start()
        passed = [copy(4 + j, (*chip, c), sibling) for j, chip in enumerate(chips)]
        for j, chip in enumerate(chips):
            copy(1 + j, (*chip, c), me).wait_recv()
            passed[j].start()
        copy(0, sibling, me).wait_recv()
        for j, chip in enumerate(chips):
            copy(4 + j, (*chip, 1 - c), me).wait_recv()
        for cp in first + passed:
            cp.wait_send()
        mine.wait()

    return pl.pallas_call(
        body, name=name,
        out_shape=jax.ShapeDtypeStruct((N_DEV * m, n), xs.dtype),
        in_specs=[pl.BlockSpec(memory_space=pltpu.VMEM)],
        out_specs=pl.BlockSpec(memory_space=pltpu.VMEM),
        scratch_shapes=[pltpu.SemaphoreType.DMA((7,)), pltpu.SemaphoreType.DMA((7,)), pltpu.SemaphoreType.DMA],
    )(xs)


def gather_weights(bufs, name):
    n = len(bufs)

    def body(*refs):
        outs = refs[n:2 * n]
        send_sems, recv_sems = refs[2 * n:]
        x, y, c = _place()
        chips = _other_chips(x, y)

        def half(i, chip_xy, core):
            h = bufs[i].shape[1] // 2
            return outs[i].at[2 * chip_xy[0] + chip_xy[1], pl.ds(pl.multiple_of(core * h, 16), h), :]

        def copy(i, k, piece, to):
            return pltpu.make_async_remote_copy(
                src_ref=piece, dst_ref=piece, send_sem=send_sems.at[6 * i + k], recv_sem=recv_sems.at[6 * i + k],
                device_id=to, device_id_type=MESH)

        started = []
        for i in range(n):
            for j, chip in enumerate(chips):
                cp = copy(i, j, half(i, (x, y), c), (*chip, c))
                cp.start()
                started.append(cp)
        for i in range(n):
            for j, chip in enumerate(chips):
                copy(i, j, half(i, chip, c), (*chip, c)).wait_recv()
                fw = copy(i, 3 + j, half(i, chip, c), (x, y, 1 - c))
                fw.start()
                started.append(fw)
        for i in range(n):
            for j, chip in enumerate(chips):
                copy(i, 3 + j, half(i, chip, 1 - c), (x, y, 1 - c)).wait_recv()
        for cp in started:
            cp.wait_send()

    return pl.pallas_call(
        body, name=name,
        out_shape=[jax.ShapeDtypeStruct(b.shape, b.dtype) for b in bufs],
        in_specs=[ANY] * n, out_specs=[ANY] * n,
        input_output_aliases={i: i for i in range(n)},
        scratch_shapes=[pltpu.SemaphoreType.DMA((6 * n,)), pltpu.SemaphoreType.DMA((6 * n,))],
    )(*bufs)


def rs_to_sibling(payloads, name):
    n = len(payloads)

    def body(*refs):
        ins, outs = refs[:n], refs[n:2 * n]
        send_sems, recv_sems = refs[2 * n:]
        x, y, c = _place()
        copies = []
        for i in range(n):
            cp = pltpu.make_async_remote_copy(src_ref=ins[i], dst_ref=outs[i], send_sem=send_sems.at[i],
                                              recv_sem=recv_sems.at[i], device_id=(x, y, 1 - c), device_id_type=MESH)
            cp.start()
            copies.append(cp)
        for cp in copies:
            cp.wait()

    return pl.pallas_call(
        body, name=name,
        out_shape=[jax.ShapeDtypeStruct(g.shape, g.dtype) for g in payloads],
        in_specs=[ANY] * n, out_specs=[ANY] * n,
        scratch_shapes=[pltpu.SemaphoreType.DMA((n,)), pltpu.SemaphoreType.DMA((n,))],
    )(*payloads)


def rs_to_chips(parts, name):
    n = len(parts)

    def body(*refs):
        ins, outs = refs[:n], refs[n:2 * n]
        send_sems, recv_sems = refs[2 * n:]
        x, y, c = _place()
        copies = []
        for i in range(n):
            for j, (px, py) in enumerate(_other_chips(x, y)):
                cp = pltpu.make_async_remote_copy(
                    src_ref=ins[i].at[2 * px + py], dst_ref=outs[i].at[j], send_sem=send_sems.at[3 * i + j],
                    recv_sem=recv_sems.at[3 * i + j], device_id=(px, py, c), device_id_type=MESH)
                cp.start()
                copies.append(cp)
        for cp in copies:
            cp.wait()

    return pl.pallas_call(
        body, name=name,
        out_shape=[jax.ShapeDtypeStruct((3,) + a.shape[1:], a.dtype) for a in parts],
        in_specs=[ANY] * n, out_specs=[ANY] * n,
        scratch_shapes=[pltpu.SemaphoreType.DMA((3 * n,)), pltpu.SemaphoreType.DMA((3 * n,))],
    )(*parts)


def rs_join_halves(fulls, name):
    n = len(fulls)

    def body(*refs):
        outs = refs[n:2 * n]
        send_sems, recv_sems = refs[2 * n:]
        x, y, c = _place()

        def copy(i, core):
            h = fulls[i].shape[0] // 2
            rows = outs[i].at[pl.ds(pl.multiple_of(core * h, 8), h), :]
            return pltpu.make_async_remote_copy(src_ref=rows, dst_ref=rows, send_sem=send_sems.at[i],
                                                recv_sem=recv_sems.at[i], device_id=(x, y, 1 - c), device_id_type=MESH)

        sent = [copy(i, c) for i in range(n)]
        for cp in sent:
            cp.start()
        for i in range(n):
            copy(i, 1 - c).wait_recv()
        for cp in sent:
            cp.wait_send()

    return pl.pallas_call(
        body, name=name,
        out_shape=[jax.ShapeDtypeStruct(a.shape, a.dtype) for a in fulls],
        in_specs=[ANY] * n, out_specs=[ANY] * n,
        input_output_aliases={i: i for i in range(n)},
        scratch_shapes=[pltpu.SemaphoreType.DMA((n,)), pltpu.SemaphoreType.DMA((n,))],
    )(*fulls)


def _row_tile(rows, cols, limit_bytes=2 * 1024 * 1024, mult=8):
    best = mult
    for t in range(mult, rows + 1, mult):
        if rows % t == 0 and t * cols * 4 <= limit_bytes:
            best = t
    return best


def rs_add_sibling(g, recv, c, name):
    _, R, C = g.shape
    h = R // 2
    tr = _row_tile(h, C, mult=16)
    nt = h // tr

    def body(c_ref, g_ref, r_ref, o_ref, o16_ref):
        s = g_ref[...] + r_ref[...].astype(F32)
        o_ref[...] = s
        o16_ref[...] = s.astype(BF16)

    blk = pl.BlockSpec((None, tr, C), lambda k, i, c_ref: (k, i, 0))
    return pl.pallas_call(
        body, name=name,
        grid_spec=pltpu.PrefetchScalarGridSpec(
            num_scalar_prefetch=1, grid=(N_SHARD, nt),
            in_specs=[pl.BlockSpec((None, tr, C), lambda k, i, c_ref: (k, c_ref[0] * nt + i, 0)), blk],
            out_specs=[blk, blk]),
        out_shape=[jax.ShapeDtypeStruct((N_SHARD, h, C), F32), jax.ShapeDtypeStruct((N_SHARD, h, C), BF16)],
        compiler_params=_params("parallel", "parallel"),
    )(c, g, recv)


def rs_add_chips(part, recv, place, name):
    _, h, C = part.shape
    tr = _row_tile(h, C, mult=16)
    nt = h // tr

    def body(k_ref, p_ref, r_ref, o_ref):
        o_ref[...] = ((p_ref[...] + r_ref[0].astype(F32)) + r_ref[1].astype(F32)) + r_ref[2].astype(F32)

    return pl.pallas_call(
        body, name=name,
        grid_spec=pltpu.PrefetchScalarGridSpec(
            num_scalar_prefetch=1, grid=(nt,),
            in_specs=[pl.BlockSpec((None, tr, C), lambda i, k_ref: (k_ref[0], i, 0)),
                      pl.BlockSpec((3, tr, C), lambda i, k_ref: (0, i, 0))],
            out_specs=pl.BlockSpec((tr, C), lambda i, k_ref: (k_ref[1] * nt + i, 0))),
        out_shape=jax.ShapeDtypeStruct((2 * h, C), F32),
        compiler_params=_params("parallel"),
    )(place, part, recv)


def _adamw_math(w, g, m, v):
    m = ADAM_B1 * m + (1.0 - ADAM_B1) * g
    v = ADAM_B2 * v + (1.0 - ADAM_B2) * (g * g)
    m_hat = m / (1.0 - ADAM_B1 ** ADAM_STEP)
    v_hat = v / (1.0 - ADAM_B2 ** ADAM_STEP)
    delta = -ADAM_LR * (m_hat / (jnp.sqrt(v_hat) + ADAM_EPS) + ADAM_WD * w)
    return delta, m, v


def adamw(w, g, m, v, name):
    R, C = w.shape
    tr = _row_tile(R, C, 1024 * 1024)

    def body(w_ref, g_ref, m_ref, v_ref, d_ref, nm_ref, nv_ref):
        d_ref[...], nm_ref[...], nv_ref[...] = _adamw_math(w_ref[...], g_ref[...], m_ref[...], v_ref[...])

    blk = pl.BlockSpec((tr, C), lambda i: (i, 0))
    return pl.pallas_call(
        body, name=name, grid=(R // tr,), in_specs=[blk] * 4, out_specs=[blk] * 3,
        out_shape=[jax.ShapeDtypeStruct((R, C), F32)] * 3,
        compiler_params=_params("parallel"),
    )(w, g, m, v)


MOD_SH = 6 * D // N_SHARD
PK_ROWS = 16


def mod_fwd(call16, w_sh, b_sh, name):
    def body(c_ref, w_ref, b_ref, o_ref):
        o_ref[...] = _dot(_silu_parts(c_ref[...])[0], w_ref[...], prec=HI) + b_ref[...]

    return pl.pallas_call(body, name=name, out_shape=jax.ShapeDtypeStruct((16, MOD_SH), F32),
                          compiler_params=_params())(call16, w_sh, b_sh)


def prep_small(lbf2, lbb2, theta_row, name):
    def body(f_ref, b_ref, t_ref, lbf_ref, lbb_ref, lg_ref):
        lbf_ref[...] = _sigmoid(f_ref[0:1, :] - f_ref[1:2, :])
        lbb_ref[...] = _sigmoid(b_ref[0:1, :] - b_ref[1:2, :])
        t = t_ref[...]
        lg_ref[...] = jnp.minimum(t, 0.0) - jnp.log(1.0 + jnp.exp(-jnp.abs(t)))

    row = jax.ShapeDtypeStruct((1, D), F32)
    return pl.pallas_call(body, name=name, out_shape=[row, row, row], compiler_params=_params())(lbf2, lbb2, theta_row)


def small_grads(g3, lbf, lbb, theta_row, name):
    def body(g_ref, lbf_ref, lbb_ref, t_ref, pk_ref, aux_ref):
        s = g_ref[0]
        for d in range(1, N_DEV):
            s = s + g_ref[d]
        pk_ref[...] = jnp.zeros_like(pk_ref)
        aux_ref[...] = jnp.zeros_like(aux_ref)
        pk_ref[1:7, :] = s[0:6]
        pk_ref[1:3, :] += s[6:8]
        pk_ref[7:8, :] = s[8:9] + s[9:10]
        pk_ref[8:9, :] = s[10:11]
        lbf, lbb = lbf_ref[...], lbb_ref[...]
        daf = (s[13:14] + s[14:15]) * lbf * (1.0 - lbf)
        dab = (s[15:16] + s[16:17]) * lbb * (1.0 - lbb)
        pk_ref[9:10, :] = daf
        pk_ref[10:11, :] = -daf
        pk_ref[11:12, :] = dab
        pk_ref[12:13, :] = -dab
        pk_ref[13:14, :] = s[11:12]
        pk_ref[14:15, :] = (s[17:18] + s[18:19]) * _sigmoid(-t_ref[...])
        pk_ref[15:16, :] = s[12:13]
        aux_ref[0:2, :] = s[6:8]
        aux_ref[2:3, :] = jnp.broadcast_to(jnp.sum(s[19:20], axis=-1, keepdims=True), (1, D))

    return pl.pallas_call(body, name=name,
                          out_shape=[jax.ShapeDtypeStruct((PK_ROWS, D), F32), jax.ShapeDtypeStruct((8, D), F32)],
                          compiler_params=_params())(g3, lbf, lbb, theta_row)


def mod_bwd(call16, dmod_sh, w_sh, name):
    def body(c_ref, d_ref, w_ref, dw_ref, ds_ref):
        dm = d_ref[...]
        dw_ref[...] = _dot(_silu_parts(c_ref[...])[0], dm, 0, 0, prec=HI)
        ds_ref[...] = jnp.zeros_like(ds_ref)
        ds_ref[0:1, :] = _dot(dm[8:9, :], w_ref[...], 1, 1, prec=HI)

    return pl.pallas_call(body, name=name,
                          out_shape=[jax.ShapeDtypeStruct((D, MOD_SH), F32), jax.ShapeDtypeStruct((8, D), F32)],
                          compiler_params=_params())(call16, dmod_sh, w_sh)


def adamw_small(g4, pk_g, pk_w, pk_m, pk_v, name):
    def body(g4_ref, g_ref, w_ref, m_ref, v_ref, go_ref, d_ref, nm_ref, nv_ref):
        w = w_ref[...]
        ds = ((g4_ref[0:1, :] + g4_ref[16:17, :]) + g4_ref[32:33, :]) + g4_ref[48:49, :]
        row = lax.broadcasted_iota(jnp.int32, (PK_ROWS, D), 0)
        g = jnp.where(row == 0, ds * _silu_parts(w[0:1, :])[1], g_ref[...])
        go_ref[...] = g
        d_ref[...], nm_ref[...], nv_ref[...] = _adamw_math(w, g, m_ref[...], v_ref[...])

    pk = jax.ShapeDtypeStruct((PK_ROWS, D), F32)
    return pl.pallas_call(body, name=name, out_shape=[pk, pk, pk, pk], compiler_params=_params())(g4, pk_g, pk_w, pk_m, pk_v)


def _pack_params(c_ctx, b_mod, n1, n2, lbf, lbb, hgn, th_f, th_b, fin):
    theta = jnp.concatenate([th_f.reshape(HEADS), th_b.reshape(HEADS), jnp.zeros((D - 2 * HEADS,), F32)])
    return jnp.concatenate([c_ctx.reshape(1, D), b_mod.reshape(6, D), n1.reshape(1, D), n2.reshape(1, D), lbf, lbb,
                            hgn.reshape(1, D), theta.reshape(1, D), fin.reshape(1, D)], axis=0)


def _unpack_params(pk):
    return (pk[0], pk[1:7].reshape(1, 6 * D), pk[7:8], pk[8:9], pk[9:11], pk[11:13], pk[13:14],
            pk[14, 0:HEADS].reshape(1, HEADS), pk[14, HEADS:2 * HEADS].reshape(1, HEADS), pk[15])


def kernel(x, c, ctx, c_ctx, w_mod, b_mod, norm1_w, norm2_w, w_in, hg_lb_fwd, hg_lb_bwd, hg_norm_w, rt_theta_fwd, rt_theta_bwd, w_proj_hgrn, w_proj_ret, w_out, w_ffn_gate, w_ffn_up, w_ffn_down, final_norm_w, loss_target, m_c_ctx, m_w_mod, m_b_mod, m_norm1_w, m_norm2_w, m_w_in, m_hg_lb_fwd, m_hg_lb_bwd, m_hg_norm_w, m_rt_theta_fwd, m_rt_theta_bwd, m_w_proj_hgrn, m_w_proj_ret, m_w_out, m_w_ffn_gate, m_w_ffn_up, m_w_ffn_down, m_final_norm_w, v_c_ctx, v_w_mod, v_b_mod, v_norm1_w, v_norm2_w, v_w_in, v_hg_lb_fwd, v_hg_lb_bwd, v_hg_norm_w, v_rt_theta_fwd, v_rt_theta_bwd, v_w_proj_hgrn, v_w_proj_ret, v_w_out, v_w_ffn_gate, v_w_ffn_up, v_w_ffn_down, v_final_norm_w):
    xi, yi, ci = _place()
    dev = 4 * xi + 2 * yi + ci
    chip = 2 * xi + yi
    core_arg = jnp.reshape(ci, (1,)).astype(jnp.int32)
    place_arg = jnp.stack([chip, ci]).astype(jnp.int32)

    c_all = allgather8(jnp.concatenate([c, jnp.zeros((7, D), F32)], axis=0), "gather_c").reshape(N_DEV, 8, D)[:, 0]
    call16 = jnp.concatenate([c_all, c_ctx.reshape(1, D), jnp.zeros((7, D), F32)], axis=0)
    b_sh = lax.dynamic_slice_in_dim(b_mod, chip * MOD_SH, MOD_SH, axis=1)
    mod_sh = mod_fwd(call16, w_mod[0], b_sh, "mod_fwd")
    mod_g = allgather8(mod_sh, "gather_mod").reshape(N_DEV, 16, MOD_SH)
    mod_all = jnp.concatenate([mod_g[0], mod_g[2], mod_g[4], mod_g[6]], axis=1)
    mod_x = lax.dynamic_index_in_dim(mod_all, dev, axis=0, keepdims=False).reshape(6, D)
    mod_c = mod_all[8].reshape(6, D)

    pk_w = _pack_params(c_ctx, b_mod, norm1_w, norm2_w, hg_lb_fwd, hg_lb_bwd, hg_norm_w, rt_theta_fwd, rt_theta_bwd, final_norm_w)
    theta_row = pk_w[14:15]
    lb_f, lb_b, lg_row = prep_small(hg_lb_fwd, hg_lb_bwd, theta_row, "prep_small")
    lg_f = jnp.broadcast_to(lg_row[0, 0:HEADS].reshape(HEADS, 1, 1), (HEADS, 1, RT_DV))
    lg_b = jnp.broadcast_to(lg_row[0, HEADS:2 * HEADS].reshape(HEADS, 1, 1), (HEADS, 1, RT_DV))

    shards = [w_in[0], w_proj_hgrn[0], w_proj_ret[0], w_out[0], w_ffn_gate[0], w_ffn_up[0], w_ffn_down[0]]
    placed = [lax.dynamic_update_index_in_dim(jnp.zeros((N_SHARD,) + s.shape, BF16), s.astype(BF16), chip, 0)
              for s in shards]
    g_in, g_pa, g_pb, g_out, g_wg, g_wu, g_wd = gather_weights(placed, "gather_weights")
    w = {"w_in": g_in, "w_pa": g_pa.reshape(D, D), "w_pb": g_pb.reshape(2 * D, D), "w_out": g_out.reshape(D, D),
         "wg": g_wg, "wu": g_wu, "wd": g_wd}

    dx, grads, small = local_step(x[0], ctx[0], loss_target[0], mod_x, mod_c, lb_f, lb_b, lg_f, lg_b,
                                  norm1_w, norm2_w, hg_norm_w, final_norm_w.reshape(1, D), w)

    order = ["w_in", "w_pa", "w_pb", "w_out", "wg", "wu", "wd"]
    gs = [grads[k] for k in order]
    to_sibling = [lax.dynamic_slice_in_dim(g, (1 - ci) * (g.shape[1] // 2), g.shape[1] // 2, axis=1).astype(BF16)
                  for g in gs]
    from_sibling = rs_to_sibling(to_sibling, "rs_to_sibling")
    chip_sums = [rs_add_sibling(g, r, core_arg, "rs_add_sibling_" + k) for g, r, k in zip(gs, from_sibling, order)]
    from_chips = rs_to_chips([a16 for _, a16 in chip_sums], "rs_to_chips")
    halves = [rs_add_chips(a, r, place_arg, "rs_add_chips_" + k) for (a, _), r, k in zip(chip_sums, from_chips, order)]
    full = dict(zip(order, rs_join_halves(halves, "rs_join_halves")))

    g3 = allgather8(small, "gather_small").reshape(N_DEV, SMALL_ROWS, D)
    pk_g, aux = small_grads(g3, lb_f, lb_b, theta_row, "small_grads")
    loss = aux[2, 0]
    dmod16 = jnp.concatenate([
        g3[:, 0:6, :].reshape(N_DEV, 6 * D),
        jnp.concatenate([aux[0], aux[1], jnp.zeros((4 * D,), F32)]).reshape(1, 6 * D),
        jnp.zeros((7, 6 * D), F32)], axis=0)
    dmod_sh = lax.dynamic_slice_in_dim(dmod16, chip * MOD_SH, MOD_SH, axis=1)
    g_wmod, dsilu = mod_bwd(call16, dmod_sh, w_mod[0], "mod_bwd")
    g4 = allgather8(dsilu, "gather_dsilu")
    pk_m = _pack_params(m_c_ctx, m_b_mod, m_norm1_w, m_norm2_w, m_hg_lb_fwd, m_hg_lb_bwd, m_hg_norm_w, m_rt_theta_fwd, m_rt_theta_bwd, m_final_norm_w)
    pk_v = _pack_params(v_c_ctx, v_b_mod, v_norm1_w, v_norm2_w, v_hg_lb_fwd, v_hg_lb_bwd, v_hg_norm_w, v_rt_theta_fwd, v_rt_theta_bwd, v_final_norm_w)
    pk_g, pk_d, pk_nm, pk_nv = adamw_small(g4, pk_g, pk_w, pk_m, pk_v, "adamw_small")

    big = {
        "w_mod": (g_wmod, w_mod, m_w_mod, v_w_mod),
        "w_in": (full["w_in"], w_in, m_w_in, v_w_in),
        "w_pa": (full["w_pa"], w_proj_hgrn, m_w_proj_hgrn, v_w_proj_hgrn),
        "w_pb": (full["w_pb"], w_proj_ret, m_w_proj_ret, v_w_proj_ret),
        "w_out": (full["w_out"], w_out, m_w_out, v_w_out),
        "wg": (full["wg"], w_ffn_gate, m_w_ffn_gate, v_w_ffn_gate),
        "wu": (full["wu"], w_ffn_up, m_w_ffn_up, v_w_ffn_up),
        "wd": (full["wd"], w_ffn_down, m_w_ffn_down, v_w_ffn_down),
    }
    res = {}
    for k, (g, wt, mt, vt) in big.items():
        d, nm, nv = adamw(wt[0], g, mt[0], vt[0], "adamw_" + k)
        res[k] = (g[None], d[None], nm[None], nv[None])

    sm = [_unpack_params(p) for p in (pk_g, pk_d, pk_nm, pk_nv)]
    outs = []
    for t in range(4):
        (s_cctx, s_bmod, s_n1, s_n2, s_lbf, s_lbb, s_hgn, s_thf, s_thb, s_fin) = sm[t]
        outs.append([s_cctx, res["w_mod"][t], s_bmod, s_n1, s_n2, res["w_in"][t], s_lbf, s_lbb, s_hgn, s_thf, s_thb,
                     res["w_pa"][t], res["w_pb"][t], res["w_out"][t], res["wg"][t], res["wu"][t], res["wd"][t], s_fin])
    return (loss, dx[None], *outs[0], *outs[1], *outs[2], *outs[3])
```

```python
import functools

import jax
import jax.numpy as jnp
from jax import lax
from jax.experimental import pallas as pl
from jax.experimental.pallas import tpu as pltpu

F32 = jnp.float32
BF16 = jnp.bfloat16
HI = lax.Precision.HIGHEST

D = 1024
HEADS = 8
HG_D = 128
RT_DK = 128
RT_DV = 256
D_FF = 2816
D_IN = 13312
N_SHARD = 4
IN_SH = D_IN // N_SHARD
FF_SH = D_FF // N_SHARD
HG_CHUNK = 32
SCAN_ROWS = 256
HG_GROUP = 8
RT_GROUP = 4
PROJ_ROWS = 1024
EPS = 1e-6
GN_EPS = 1e-5
Q_SCALE = 128.0 ** -0.5
VMEM_LIMIT = 56 * 1024 * 1024

COL_HQ, COL_HFF, COL_HFB, COL_HI, COL_HG = 0, 8, 16, 24, 32
COL_RQ, COL_RK, COL_RV, COL_RG, COL_GA, COL_GB = 40, 48, 56, 72, 88, 96

ADAM_LR, ADAM_B1, ADAM_B2, ADAM_EPS, ADAM_WD, ADAM_STEP = 0.001, 0.9, 0.999, 1e-08, 0.01, 10


def _params(*sem):
    return pltpu.CompilerParams(dimension_semantics=sem, vmem_limit_bytes=VMEM_LIMIT)


def _dot(a, b, ca=1, cb=0, prec=None):
    return lax.dot_general(a, b, (((ca,), (cb,)), ((), ())), precision=prec, preferred_element_type=F32)


def _bdot(a, b, ca=1, cb=0):
    return _dot(a.astype(BF16), b.astype(BF16), ca, cb)


def _sigmoid(z):
    return 1.0 / (1.0 + jnp.exp(-z))


def _rowsum(a):
    return jnp.sum(a, axis=0, keepdims=True)


def _lanemean(a):
    return jnp.mean(a, axis=-1, keepdims=True)


def _grid_step(grid):
    pos, total = 0, 1
    for d, size in enumerate(grid):
        pos = pos * size + pl.program_id(d)
        total *= size
    return pos, total


def normmod_matmul(x, nw, sh, sc, w4, name, gather=()):
    L = x.shape[0]
    tm = min(PROJ_ROWS, L)
    tn = IN_SH // 2
    grid = (L // tm, N_SHARD, 2)
    ng = len(gather)

    def body(x_ref, nw_ref, sh_ref, sc_ref, w_ref, *refs):
        p_ref, hx_ref = refs[ng:ng + 2]
        hx_scr = refs[2 * ng + 2]
        if ng:
            start, forward, finish = _gather_phases(refs[:ng], refs[ng + 2:2 * ng + 2], *refs[2 * ng + 3:])
            pos, total = _grid_step(grid)
            pl.when(pos == 0)(start)
            pl.when(pos == total // 2)(forward)

        @pl.when((pl.program_id(1) == 0) & (pl.program_id(2) == 0))
        def _():
            xv = x_ref[...]
            n = xv * lax.rsqrt(_lanemean(xv * xv) + EPS) * nw_ref[...]
            h = (n * (1.0 + sc_ref[...]) + sh_ref[...]).astype(BF16)
            hx_scr[...] = h
            hx_ref[...] = h

        p_ref[...] = _dot(hx_scr[...], w_ref[...])
        if ng:
            pl.when(pos == total - 1)(finish)

    vec = pl.BlockSpec((1, D), lambda i, k, j: (0, 0))
    return pl.pallas_call(
        body, name=name,
        grid=grid,
        in_specs=[pl.BlockSpec((tm, D), lambda i, k, j: (i, 0)), vec, vec, vec,
                  pl.BlockSpec((None, D, tn), lambda i, k, j: (k, 0, j))] + [ANY] * ng,
        out_specs=[pl.BlockSpec((tm, tn), lambda i, k, j: (i, 2 * k + j)),
                   pl.BlockSpec((tm, D), lambda i, k, j: (i, 0))] + [ANY] * ng,
        out_shape=[jax.ShapeDtypeStruct((L, D_IN), F32), jax.ShapeDtypeStruct((L, D), BF16)]
        + [jax.ShapeDtypeStruct((N_SHARD,) + s.shape, s.dtype) for s in gather],
        scratch_shapes=[pltpu.VMEM((tm, D), BF16)] + (_gather_scratch(ng) if ng else []),
        compiler_params=_params("arbitrary", "arbitrary", "arbitrary"),
    )(x, nw, sh, sc, w4, *gather)


def _hgrn_gates(z, lb):
    sg = _sigmoid(z)
    sgn = _sigmoid(-z)
    f = lb + (1.0 - lb) * sg
    k = (1.0 - lb) * sgn
    return sg, sgn, f, k


def _tri_chunks(n, chunk, reverse):
    r = lax.broadcasted_iota(jnp.int32, (n, n), 0)
    c = lax.broadcasted_iota(jnp.int32, (n, n), 1)
    same = (r // chunk) == (c // chunk)
    return jnp.where(same & ((r <= c) if reverse else (r >= c)), 1.0, 0.0).astype(F32)


def _decay3(b, reverse):
    C = b.shape[0]
    t = lax.broadcasted_iota(jnp.int32, (C, C, 1), 0)
    s = lax.broadcasted_iota(jnp.int32, (C, C, 1), 1)
    mask = (t <= s) if reverse else (t >= s)
    return jnp.exp(jnp.where(mask, b[:, None, :] - b[None, :, :], -jnp.inf))


HG_SUB = 16


def _hgrn_pairs(reverse):
    pairs = []
    size = HG_SUB
    while size < HG_CHUNK:
        for lo in range(0, HG_CHUNK, 2 * size):
            first, second = slice(lo, lo + size), slice(lo + size, lo + 2 * size)
            if reverse:
                pairs.append((first, second, lo + size))
            else:
                pairs.append((second, first, lo + size - 1))
        size *= 2
    return pairs


def _hgrn_intra_fwd(q, k, v, b, reverse):
    blocks = []
    for lo in range(0, HG_CHUNK, HG_SUB):
        r = slice(lo, lo + HG_SUB)
        att3 = jnp.sum(q[r][:, None, :] * k[r][None, :, :] * _decay3(b[r], reverse), axis=-1, keepdims=True)
        blocks.append(jnp.sum(att3 * v[r][None, :, :], axis=1))
    for qr, kr, ref in _hgrn_pairs(reverse):
        beta = b[ref:ref + 1]
        att = _bdot(q[qr] * jnp.exp(b[qr] - beta), k[kr] * jnp.exp(beta - b[kr]), 1, 1)
        part = _bdot(att, v[kr])
        n = part.shape[0] // HG_SUB
        for i in range(n):
            blocks[qr.start // HG_SUB + i] += part[i * HG_SUB:(i + 1) * HG_SUB]
    return jnp.concatenate(blocks, axis=0)


def _hgrn_intra_bwd(q, k, v, b, d_o, reverse):
    nb = HG_CHUNK // HG_SUB
    dq, dk, dv = [None] * nb, [None] * nb, [None] * nb
    for i in range(nb):
        r = slice(i * HG_SUB, (i + 1) * HG_SUB)
        e3 = _decay3(b[r], reverse)
        p3 = jnp.sum(d_o[r][:, None, :] * v[r][None, :, :], axis=-1, keepdims=True) * e3
        dq[i] = jnp.sum(p3 * k[r][None, :, :], axis=1)
        dk[i] = jnp.sum(p3 * q[r][:, None, :], axis=0)
        att3 = jnp.sum(q[r][:, None, :] * k[r][None, :, :] * e3, axis=-1, keepdims=True)
        dv[i] = jnp.sum(att3 * d_o[r][:, None, :], axis=0)

    def add(acc, rows, part):
        for i in range(part.shape[0] // HG_SUB):
            acc[rows.start // HG_SUB + i] += part[i * HG_SUB:(i + 1) * HG_SUB]

    for qr, kr, ref in _hgrn_pairs(reverse):
        beta = b[ref:ref + 1]
        fq, fk = jnp.exp(b[qr] - beta), jnp.exp(beta - b[kr])
        qt, kt = q[qr] * fq, k[kr] * fk
        att = _bdot(qt, kt, 1, 1)
        datt = _bdot(d_o[qr], v[kr], 1, 1)
        add(dq, qr, _bdot(datt, kt) * fq)
        add(dk, kr, _bdot(datt, qt, 0, 0) * fk)
        add(dv, kr, _bdot(att, d_o[qr], 0, 0))
    return jnp.concatenate(dq, axis=0), jnp.concatenate(dk, axis=0), jnp.concatenate(dv, axis=0)


def _hgrn_state_step(k, v, b, s_t, last):
    b_last = b[last:last + 1]
    return s_t * jnp.exp(b_last) + _bdot(v, k * jnp.exp(b_last - b), 0, 0)


def hgrn_scan_fwd(p, lb, s0, col_z, reverse, name):
    L = p.shape[0]
    nB = L // SCAN_ROWS
    nC = SCAN_ROWS // HG_CHUNK
    C = HG_CHUNK
    G, W = HG_GROUP, HG_GROUP * HG_D
    last = 0 if reverse else C - 1

    def bmap(b):
        return (nB - 1 - b) if reverse else b

    def body(q_ref, z_ref, v_ref, lb_ref, s0_ref, o_ref, sfin_ref, sblk_ref, s_scr, k_scr, b_scr):
        blk = pl.program_id(1)

        @pl.when(blk == 0)
        def _():
            s_scr[...] = s0_ref[...]

        sblk_ref[...] = s_scr[...]
        _, _, f_all, k_all = _hgrn_gates(z_ref[...], lb_ref[...])
        k_scr[...] = k_all
        b_scr[...] = _dot(_tri_chunks(SCAN_ROWS, C, reverse), jnp.log(f_all), prec=HI)

        def chunk(ci, carry):
            c = (nC - 1 - ci) if reverse else ci
            rows = pl.ds(pl.multiple_of(c * C, C), C)
            for j in range(G):
                lanes = slice(j * HG_D, (j + 1) * HG_D)
                q = q_ref[rows, lanes] * Q_SCALE
                v = v_ref[rows, lanes]
                k = k_scr[rows, lanes]
                b = b_scr[rows, lanes]
                s_t = s_scr[j]
                o_ref[rows, lanes] = _hgrn_intra_fwd(q, k, v, b, reverse) + _bdot(q * jnp.exp(b), s_t, 1, 1)
                s_scr[j] = _hgrn_state_step(k, v, b, s_t, last)
            return carry

        lax.fori_loop(0, nC, chunk, 0)

        @pl.when(blk == nB - 1)
        def _():
            sfin_ref[...] = s_scr[...]

    def col(c0):
        return pl.BlockSpec((SCAN_ROWS, W), lambda h, b: (bmap(b), c0 // G + h))

    state = pl.BlockSpec((G, HG_D, HG_D), lambda h, b: (h, 0, 0))
    return pl.pallas_call(
        body, name=name,
        grid=(HEADS // G, nB),
        in_specs=[col(COL_HQ), col(col_z), col(COL_HI), pl.BlockSpec((1, W), lambda h, b: (0, h)), state],
        out_specs=[pl.BlockSpec((SCAN_ROWS, W), lambda h, b: (bmap(b), h)), state,
                   pl.BlockSpec((None, G, HG_D, HG_D), lambda h, b: (bmap(b), h, 0, 0))],
        out_shape=[jax.ShapeDtypeStruct((L, D), F32),
                   jax.ShapeDtypeStruct((HEADS, HG_D, HG_D), F32),
                   jax.ShapeDtypeStruct((nB, HEADS, HG_D, HG_D), F32)],
        scratch_shapes=[pltpu.VMEM((G, HG_D, HG_D), F32), pltpu.VMEM((SCAN_ROWS, W), F32),
                        pltpu.VMEM((SCAN_ROWS, W), F32)],
        compiler_params=_params("parallel", "arbitrary"),
    )(p, p, p, lb, s0)


def hgrn_scan_bwd(p, lb, s_blocks, d_o, ds_fin, prev, col_z, reverse, name):
    L = p.shape[0]
    nB = L // SCAN_ROWS
    nC = SCAN_ROWS // HG_CHUNK
    C = HG_CHUNK
    G, W = HG_GROUP, HG_GROUP * HG_D
    last = 0 if reverse else C - 1
    has_prev = prev is not None
    out_dt = BF16 if has_prev else F32

    def bmap(b):
        return b if reverse else (nB - 1 - b)

    def body(*refs):
        q_ref, z_ref, v_ref, lb_ref, sblk_ref, do_ref, dsf_ref = refs[:7]
        refs = refs[7:]
        if has_prev:
            pq_ref, pv_ref = refs[:2]
            refs = refs[2:]
        dq_ref, dz_ref, dv_ref, dlb_ref, ds0_ref, st_scr, run_scr, ds_scr, k_scr, b_scr, db_scr, dk_scr = refs
        blk = pl.program_id(1)

        @pl.when(blk == 0)
        def _():
            ds_scr[...] = dsf_ref[...]
            dlb_ref[...] = jnp.zeros_like(dlb_ref)

        tri = _tri_chunks(SCAN_ROWS, C, reverse)
        row = lax.broadcasted_iota(jnp.int32, (C, HG_D), 0)
        _, _, f_all, k_all = _hgrn_gates(z_ref[...], lb_ref[...])
        k_scr[...] = k_all
        b_scr[...] = _dot(tri, jnp.log(f_all), prec=HI)
        run_scr[...] = sblk_ref[...]

        def recompute(ci, carry):
            c = (nC - 1 - ci) if reverse else ci
            rows = pl.ds(pl.multiple_of(c * C, C), C)
            for j in range(G):
                lanes = slice(j * HG_D, (j + 1) * HG_D)
                s_t = run_scr[j]
                st_scr[c, j] = s_t
                run_scr[j] = _hgrn_state_step(k_scr[rows, lanes], v_ref[rows, lanes], b_scr[rows, lanes], s_t, last)
            return carry

        lax.fori_loop(0, nC, recompute, 0)

        def chunk(ci, carry):
            c = ci if reverse else (nC - 1 - ci)
            rows = pl.ds(pl.multiple_of(c * C, C), C)
            for j in range(G):
                lanes = slice(j * HG_D, (j + 1) * HG_D)
                k = k_scr[rows, lanes]
                b = b_scr[rows, lanes]
                q = q_ref[rows, lanes] * Q_SCALE
                v = v_ref[rows, lanes]
                d_o = do_ref[rows, lanes]
                s_t = st_scr[c, j]
                ds_t = ds_scr[j]
                eb = jnp.exp(b)
                b_last = b[last:last + 1]
                eb_last = jnp.exp(b_last)
                kdec = jnp.exp(b_last - b)
                qe = q * eb
                ke = k * kdec
                dq_in, dk_in, dv_in = _hgrn_intra_bwd(q, k, v, b, d_o, reverse)
                dq_tot = _bdot(d_o, s_t, 1, 0) * eb + dq_in
                dke = _bdot(v, ds_t, 1, 0)
                dk_tot = dke * kdec + dk_in
                dv = dv_in + _bdot(ke, ds_t, 1, 1)
                db_last = _rowsum(dke * ke) + eb_last * _rowsum(ds_t * s_t)
                db_scr[rows, lanes] = q * dq_tot - k * dk_tot + jnp.where(row == last, db_last, 0.0)
                dk_scr[rows, lanes] = dk_tot
                dq = dq_tot * Q_SCALE
                if has_prev:
                    dq = dq + pq_ref[rows, lanes]
                    dv = dv + pv_ref[rows, lanes]
                dq_ref[rows, lanes] = dq.astype(out_dt)
                dv_ref[rows, lanes] = dv.astype(out_dt)
                ds_scr[j] = ds_t * eb_last + _bdot(d_o, qe, 0, 0)
            return carry

        lax.fori_loop(0, nC, chunk, 0)

        lb = lb_ref[...]
        sg, sgn, f, _ = _hgrn_gates(z_ref[...], lb)
        g = _dot(tri, db_scr[...], 0, 0, prec=HI) / f - dk_scr[...]
        dz_ref[...] = (g * (1.0 - lb) * sg * sgn).astype(BF16)
        dlb_ref[...] += _rowsum(g * sgn)

        @pl.when(blk == nB - 1)
        def _():
            ds0_ref[...] = ds_scr[...]

    def col(c0):
        return pl.BlockSpec((SCAN_ROWS, W), lambda h, b: (bmap(b), c0 // G + h))

    tile = pl.BlockSpec((SCAN_ROWS, W), lambda h, b: (bmap(b), h))
    state = pl.BlockSpec((G, HG_D, HG_D), lambda h, b: (h, 0, 0))
    in_specs = [col(COL_HQ), col(col_z), col(COL_HI),
                pl.BlockSpec((1, W), lambda h, b: (0, h)),
                pl.BlockSpec((None, G, HG_D, HG_D), lambda h, b: (bmap(b), h, 0, 0)),
                tile, state]
    args = [p, p, p, lb, s_blocks, d_o, ds_fin]
    if has_prev:
        in_specs += [tile, tile]
        args += list(prev)
    return pl.pallas_call(
        body, name=name,
        grid=(HEADS // G, nB),
        in_specs=in_specs,
        out_specs=[tile, tile, tile, pl.BlockSpec((1, W), lambda h, b: (0, h)), state],
        out_shape=[jax.ShapeDtypeStruct((L, D), out_dt), jax.ShapeDtypeStruct((L, D), BF16),
                   jax.ShapeDtypeStruct((L, D), out_dt), jax.ShapeDtypeStruct((1, D), F32),
                   jax.ShapeDtypeStruct((HEADS, HG_D, HG_D), F32)],
        scratch_shapes=[pltpu.VMEM((nC, G, HG_D, HG_D), F32), pltpu.VMEM((G, HG_D, HG_D), F32),
                        pltpu.VMEM((G, HG_D, HG_D), F32)] + [pltpu.VMEM((SCAN_ROWS, W), F32)] * 4,
        compiler_params=_params("parallel", "arbitrary"),
    )(*args)


def _rope(t, cosf, sinf):
    return t * cosf + pltpu.roll(t, RT_DK // 2, 1) * sinf


def _rope_t(d, cosf, sinf):
    return d * cosf + pltpu.roll(d * sinf, RT_DK // 2, 1)


def _ret_decays(lg, reverse):
    C = SCAN_ROWS
    t = lax.broadcasted_iota(jnp.int32, (C, C), 0)
    s = lax.broadcasted_iota(jnp.int32, (C, C), 1)
    delta = ((s - t) if reverse else (t - s)).astype(F32)
    dmat = jnp.where(delta >= 0, jnp.exp(lg * jnp.maximum(delta, 0.0)), 0.0)
    r = lax.broadcasted_iota(jnp.int32, (C, RT_DK), 0)
    pos = ((C - 1 - r) if reverse else r).astype(F32)
    lg1 = lg[:, :RT_DK]
    qdec = jnp.exp(lg1 * (pos + 1.0))
    kdec = jnp.exp(lg1 * (C - 1.0 - pos))
    sdec = jnp.exp(lg1 * float(C))
    return dmat, delta, pos, qdec, kdec, sdec


def ret_scan_fwd(p, cosf, sinf, lg, s0, reverse, name):
    L = p.shape[0]
    C = SCAN_ROWS
    nB = L // C

    def bmap(b):
        return (nB - 1 - b) if reverse else b

    G = RT_GROUP

    def body(q_ref, k_ref, v_ref, cos_ref, sin_ref, lg_ref, s0_ref, o_ref, sfin_ref, sblk_ref, s_scr):
        blk = pl.program_id(1)

        @pl.when(blk == 0)
        def _():
            s_scr[...] = s0_ref[...]

        sblk_ref[...] = s_scr[...]
        cosf, sinf = cos_ref[...], sin_ref[...]
        for j in range(G):
            lk, lv = slice(j * RT_DK, (j + 1) * RT_DK), slice(j * RT_DV, (j + 1) * RT_DV)
            s_t = s_scr[j]
            dmat, _, _, qdec, kdec, sdec = _ret_decays(lg_ref[j], reverse)
            q = _rope(q_ref[:, lk] * Q_SCALE, cosf, sinf)
            k = _rope(k_ref[:, lk], cosf, sinf)
            v = v_ref[:, lv]
            att = _bdot(q, k, 1, 1) * dmat
            o_ref[:, lv] = _bdot(att, v) + _bdot(q * qdec, s_t, 1, 1)
            s_scr[j] = s_t * sdec + _bdot(v, k * kdec, 0, 0)

        @pl.when(blk == nB - 1)
        def _():
            sfin_ref[...] = s_scr[...]

    def col(c0):
        return pl.BlockSpec((C, G * RT_DK), lambda h, b: (bmap(b), c0 // G + h))

    tab = pl.BlockSpec((C, RT_DK), lambda h, b: (bmap(b), 0))
    state = pl.BlockSpec((G, RT_DV, RT_DK), lambda h, b: (h, 0, 0))
    return pl.pallas_call(
        body, name=name,
        grid=(HEADS // G, nB),
        in_specs=[col(COL_RQ), col(COL_RK),
                  pl.BlockSpec((C, G * RT_DV), lambda h, b: (bmap(b), COL_RV // (2 * G) + h)),
                  tab, tab, pl.BlockSpec((G, 1, RT_DV), lambda h, b: (h, 0, 0)), state],
        out_specs=[pl.BlockSpec((C, G * RT_DV), lambda h, b: (bmap(b), h)), state,
                   pl.BlockSpec((None, G, RT_DV, RT_DK), lambda h, b: (bmap(b), h, 0, 0))],
        out_shape=[jax.ShapeDtypeStruct((L, HEADS * RT_DV), F32),
                   jax.ShapeDtypeStruct((HEADS, RT_DV, RT_DK), F32),
                   jax.ShapeDtypeStruct((nB, HEADS, RT_DV, RT_DK), F32)],
        scratch_shapes=[pltpu.VMEM((G, RT_DV, RT_DK), F32)],
        compiler_params=_params("parallel", "arbitrary"),
    )(p, p, p, cosf, sinf, lg, s0)


def ret_scan_bwd(p, cosf, sinf, lg, s_blocks, d_o, ds_fin, prev, reverse, name):
    L = p.shape[0]
    C = SCAN_ROWS
    nB = L // C
    has_prev = prev is not None
    out_dt = BF16 if has_prev else F32
    G = RT_GROUP

    def bmap(b):
        return b if reverse else (nB - 1 - b)

    def body(*refs):
        q_ref, k_ref, v_ref, cos_ref, sin_ref, lg_ref, sblk_ref, do_ref, dsf_ref = refs[:9]
        refs = refs[9:]
        if has_prev:
            pq_ref, pk_ref, pv_ref = refs[:3]
            refs = refs[3:]
        dq_ref, dk_ref, dv_ref, dlg_ref, ds0_ref, ds_scr = refs
        blk = pl.program_id(1)

        @pl.when(blk == 0)
        def _():
            ds_scr[...] = dsf_ref[...]
            dlg_ref[...] = jnp.zeros_like(dlg_ref)

        cosf, sinf = cos_ref[...], sin_ref[...]
        for j in range(G):
            lk, lv = slice(j * RT_DK, (j + 1) * RT_DK), slice(j * RT_DV, (j + 1) * RT_DV)
            s_t = sblk_ref[j]
            ds_t = ds_scr[j]
            dmat, delta, pos, qdec, kdec, sdec = _ret_decays(lg_ref[j], reverse)
            q = _rope(q_ref[:, lk] * Q_SCALE, cosf, sinf)
            k = _rope(k_ref[:, lk], cosf, sinf)
            v = v_ref[:, lv]
            d_o = do_ref[:, lv]
            att_raw = _bdot(q, k, 1, 1)
            datt_m = _bdot(d_o, v, 1, 1) * dmat
            dqd = _bdot(d_o, s_t, 1, 0)
            dkd = _bdot(v, ds_t, 1, 0)
            dq = _bdot(datt_m, k) + dqd * qdec
            dk = _bdot(datt_m, q, 0, 0) + dkd * kdec
            dv = _bdot(att_raw * dmat, d_o, 0, 0) + _bdot(k * kdec, ds_t, 1, 1)
            ds_scr[j] = ds_t * sdec + _bdot(d_o, q * qdec, 0, 0)
            t1 = jnp.sum(_rowsum(datt_m * att_raw * delta), axis=-1, keepdims=True)
            t23 = jnp.sum(_rowsum((pos + 1.0) * qdec * q * dqd + (C - 1.0 - pos) * kdec * k * dkd), axis=-1, keepdims=True)
            t4 = jnp.sum(_rowsum(ds_t * s_t * sdec), axis=-1, keepdims=True) * float(C)
            dlg_ref[j] += jnp.broadcast_to(t1 + t23 + t4, (1, RT_DK))
            if has_prev:
                dq = _rope_t(dq + pq_ref[:, lk], cosf, sinf) * Q_SCALE
                dk = _rope_t(dk + pk_ref[:, lk], cosf, sinf)
                dv = dv + pv_ref[:, lv]
            dq_ref[:, lk] = dq.astype(out_dt)
            dk_ref[:, lk] = dk.astype(out_dt)
            dv_ref[:, lv] = dv.astype(out_dt)

        @pl.when(blk == nB - 1)
        def _():
            ds0_ref[...] = ds_scr[...]

    def col(c0):
        return pl.BlockSpec((C, G * RT_DK), lambda h, b: (bmap(b), c0 // G + h))

    tab = pl.BlockSpec((C, RT_DK), lambda h, b: (bmap(b), 0))
    state = pl.BlockSpec((G, RT_DV, RT_DK), lambda h, b: (h, 0, 0))
    tk = pl.BlockSpec((C, G * RT_DK), lambda h, b: (bmap(b), h))
    tv = pl.BlockSpec((C, G * RT_DV), lambda h, b: (bmap(b), h))
    in_specs = [col(COL_RQ), col(COL_RK),
                pl.BlockSpec((C, G * RT_DV), lambda h, b: (bmap(b), COL_RV // (2 * G) + h)),
                tab, tab, pl.BlockSpec((G, 1, RT_DV), lambda h, b: (h, 0, 0)),
                pl.BlockSpec((None, G, RT_DV, RT_DK), lambda h, b: (bmap(b), h, 0, 0)),
                tv, state]
    args = [p, p, p, cosf, sinf, lg, s_blocks, d_o, ds_fin]
    if has_prev:
        in_specs += [tk, tk, tv]
        args += list(prev)
    return pl.pallas_call(
        body, name=name,
        grid=(HEADS // G, nB),
        in_specs=in_specs,
        out_specs=[tk, tk, tv, pl.BlockSpec((G, 1, RT_DK), lambda h, b: (h, 0, 0)), state],
        out_shape=[jax.ShapeDtypeStruct((L, D), out_dt), jax.ShapeDtypeStruct((L, D), out_dt),
                   jax.ShapeDtypeStruct((L, HEADS * RT_DV), out_dt),
                   jax.ShapeDtypeStruct((HEADS, 1, RT_DK), F32),
                   jax.ShapeDtypeStruct((HEADS, RT_DV, RT_DK), F32)],
        scratch_shapes=[pltpu.VMEM((G, RT_DV, RT_DK), F32)],
        compiler_params=_params("parallel", "arbitrary"),
    )(*args)


def _silu_parts(h):
    s = _sigmoid(h)
    return h * s, s * (1.0 + h * (1.0 - s))


def _head_rms(o):
    outs, rs = [], []
    for h in range(HEADS):
        oh = o[:, h * HG_D:(h + 1) * HG_D]
        r = lax.rsqrt(_lanemean(oh * oh) + EPS)
        outs.append(oh * r)
        rs.append(r)
    return outs, rs


def _group_norm(o):
    outs, rs = [], []
    for h in range(HEADS):
        oh = o[:, h * RT_DV:(h + 1) * RT_DV]
        c = oh - _lanemean(oh)
        r = lax.rsqrt(_lanemean(c * c) + GN_EPS)
        outs.append(c * r)
        rs.append(r)
    return outs, rs


MIX_ROWS = 256
MIX_BWD_ROWS = 128


def _mix_specs(rows):
    def t(w, c=0):
        return pl.BlockSpec((rows, w), lambda i: (i, c))

    return t


def mix_fwd(ohf, ohb, orf, orb, p, x, g1, hgw, w_pa, w_pb, w_out, name):
    L = x.shape[0]
    t = _mix_specs(MIX_ROWS)

    def body(ohf_ref, ohb_ref, orf_ref, orb_ref, hg_ref, rg0_ref, rg1_ref, ga_ref, gb_ref, x_ref, g1_ref, hgw_ref,
             wpa_ref, wpb_ref, wout_ref, x1_ref, xmix_ref, merged_ref, ya_ref, yb_ref):
        nh, _ = _head_rms(ohf_ref[...] + ohb_ref[...])
        ya = jnp.concatenate(nh, axis=1) * hgw_ref[...] * _silu_parts(hg_ref[...])[0]
        gn, _ = _group_norm(orf_ref[...] + orb_ref[...])
        rg = jnp.concatenate([rg0_ref[...], rg1_ref[...]], axis=1)
        yb = jnp.concatenate(gn, axis=1) * _silu_parts(rg)[0]
        ya16, yb16 = ya.astype(BF16), yb.astype(BF16)
        merged = (_sigmoid(ga_ref[...]) * _dot(ya16, wpa_ref[...])
                  + _sigmoid(gb_ref[...]) * _dot(yb16, wpb_ref[...])).astype(BF16)
        x_mix = _dot(merged, wout_ref[...])
        x1_ref[...] = x_ref[...] + g1_ref[...] * x_mix
        xmix_ref[...] = x_mix
        merged_ref[...] = merged
        ya_ref[...] = ya16
        yb_ref[...] = yb16

    vec = pl.BlockSpec((1, D), lambda i: (0, 0))

    def full(a):
        return pl.BlockSpec(a.shape, lambda i: (0, 0), pipeline_mode=pl.Buffered(1))

    return pl.pallas_call(
        body, name=name,
        grid=(L // MIX_ROWS,),
        in_specs=[t(D), t(D), t(2 * D), t(2 * D), t(D, COL_HG // 8), t(D, COL_RG // 8), t(D, COL_RG // 8 + 1),
                  t(D, COL_GA // 8), t(D, COL_GB // 8), t(D), vec, vec, full(w_pa), full(w_pb), full(w_out)],
        out_specs=[t(D), t(D), t(D), t(D), t(2 * D)],
        out_shape=[jax.ShapeDtypeStruct((L, D), F32), jax.ShapeDtypeStruct((L, D), F32),
                   jax.ShapeDtypeStruct((L, D), BF16), jax.ShapeDtypeStruct((L, D), BF16),
                   jax.ShapeDtypeStruct((L, 2 * D), BF16)],
        compiler_params=_params("parallel"),
    )(ohf, ohb, orf, orb, p, p, p, p, p, x, g1, hgw, w_pa, w_pb, w_out)


def mix_bwd(dx1, x_mix, ya, yb, ohf, ohb, orf, orb, p, g1, hgw, w_pa, w_pb, w_out, name):
    L = dx1.shape[0]
    t = _mix_specs(MIX_BWD_ROWS)

    def body(dx1_ref, xmix_ref, ya_ref, yb_ref, ohf_ref, ohb_ref, orf_ref, orb_ref, hg_ref, rg0_ref, rg1_ref,
             ga_ref, gb_ref, g1_ref, hgw_ref, wpa_ref, wpb_ref, wout_ref,
             dxm_ref, da_ref, db_ref, dga_ref, dgb_ref, dhg_ref, drg_ref, dohg_ref, dort_ref, sums_ref):
        @pl.when(pl.program_id(0) == 0)
        def _():
            sums_ref[...] = jnp.zeros_like(sums_ref)

        dx1 = dx1_ref[...]
        dxm = (g1_ref[...] * dx1).astype(BF16)
        dxm_ref[...] = dxm
        dmerged = _dot(dxm, wout_ref[...], 1, 1)
        a = _dot(ya_ref[...], wpa_ref[...])
        bm = _dot(yb_ref[...], wpb_ref[...])
        sa, sb = _sigmoid(ga_ref[...]), _sigmoid(gb_ref[...])
        d_a = (dmerged * sa).astype(BF16)
        d_b = (dmerged * sb).astype(BF16)
        da_ref[...] = d_a
        db_ref[...] = d_b
        dga_ref[...] = (dmerged * a * sa * (1.0 - sa)).astype(BF16)
        dgb_ref[...] = (dmerged * bm * sb * (1.0 - sb)).astype(BF16)
        dya = _dot(d_a, wpa_ref[...], 1, 1)
        dyb = _dot(d_b, wpb_ref[...], 1, 1)

        hgw = hgw_ref[...]
        silu_h, dsilu_h = _silu_parts(hg_ref[...])
        nh, rh = _head_rms(ohf_ref[...] + ohb_ref[...])
        n = jnp.concatenate(nh, axis=1)
        dhg_ref[...] = (dya * n * hgw * dsilu_h).astype(BF16)
        dn = dya * hgw * silu_h
        douts = []
        for h in range(HEADS):
            dnh = dn[:, h * HG_D:(h + 1) * HG_D]
            douts.append(rh[h] * (dnh - nh[h] * _lanemean(dnh * nh[h])))
        dohg_ref[...] = jnp.concatenate(douts, axis=1)

        rg = jnp.concatenate([rg0_ref[...], rg1_ref[...]], axis=1)
        silu_r, dsilu_r = _silu_parts(rg)
        gn, rr = _group_norm(orf_ref[...] + orb_ref[...])
        g = jnp.concatenate(gn, axis=1)
        drg_ref[...] = (dyb * g * dsilu_r).astype(BF16)
        dgn = dyb * silu_r
        douts = []
        for h in range(HEADS):
            dgh = dgn[:, h * RT_DV:(h + 1) * RT_DV]
            douts.append(rr[h] * (dgh - _lanemean(dgh) - gn[h] * _lanemean(dgh * gn[h])))
        dort_ref[...] = jnp.concatenate(douts, axis=1)

        sums_ref[0:1, :] += _rowsum(dx1 * xmix_ref[...])
        sums_ref[1:2, :] += _rowsum(dya * n * silu_h)

    vec = pl.BlockSpec((1, D), lambda i: (0, 0))

    def full(a):
        return pl.BlockSpec(a.shape, lambda i: (0, 0), pipeline_mode=pl.Buffered(1))

    bf = functools.partial(jax.ShapeDtypeStruct, dtype=BF16)
    return pl.pallas_call(
        body, name=name,
        grid=(L // MIX_BWD_ROWS,),
        in_specs=[t(D), t(D), t(D), t(2 * D), t(D), t(D), t(2 * D), t(2 * D),
                  t(D, COL_HG // 8), t(D, COL_RG // 8), t(D, COL_RG // 8 + 1), t(D, COL_GA // 8), t(D, COL_GB // 8),
                  vec, vec, full(w_pa), full(w_pb), full(w_out)],
        out_specs=[t(D), t(D), t(D), t(D), t(D), t(D), t(2 * D), t(D), t(2 * D),
                   pl.BlockSpec((8, D), lambda i: (0, 0))],
        out_shape=[bf((L, D)), bf((L, D)), bf((L, D)), bf((L, D)), bf((L, D)), bf((L, D)), bf((L, 2 * D)),
                   jax.ShapeDtypeStruct((L, D), F32), jax.ShapeDtypeStruct((L, 2 * D), F32),
                   jax.ShapeDtypeStruct((8, D), F32)],
        compiler_params=_params("arbitrary"),
    )(dx1, x_mix, ya, yb, ohf, ohb, orf, orb, p, p, p, p, p, g1, hgw, w_pa, w_pb, w_out)


FFN_ROWS = 512


def ffn_fwd(x1, target, nw2, sh2, sc2, g2, fw, wg, wu, wd, name):
    L = x1.shape[0]
    tm = min(FFN_ROWS, L)

    def body(x1_ref, tgt_ref, nw2_ref, sh2_ref, sc2_ref, g2_ref, fw_ref, wg_ref, wu_ref, wd_ref,
             hx2_ref, g_ref, u_ref, h_ref, f_ref, dx2_ref, sums_ref, hx_scr, acc):
        i, j = pl.program_id(0), pl.program_id(1)

        @pl.when((i == 0) & (j == 0))
        def _():
            sums_ref[...] = jnp.zeros_like(sums_ref)

        @pl.when(j == 0)
        def _():
            xv = x1_ref[...]
            n = xv * lax.rsqrt(_lanemean(xv * xv) + EPS) * nw2_ref[...]
            h = (n * (1.0 + sc2_ref[...]) + sh2_ref[...]).astype(BF16)
            hx_scr[...] = h
            hx2_ref[...] = h
            acc[...] = jnp.zeros_like(acc)

        hx = hx_scr[...]
        g = _dot(hx, wg_ref[...])
        u = _dot(hx, wu_ref[...])
        hh = (_silu_parts(g)[0] * u).astype(BF16)
        g_ref[...] = g
        u_ref[...] = u
        h_ref[...] = hh
        acc[...] += _dot(hh, wd_ref[...])

        @pl.when(j == N_SHARD - 1)
        def _():
            f = acc[...]
            f_ref[...] = f
            x2 = x1_ref[...] + g2_ref[...] * f
            r = lax.rsqrt(_lanemean(x2 * x2) + EPS)
            fw = fw_ref[...]
            e = x2 * r * fw - tgt_ref[...]
            dy = e * (1.0 / D)
            dyw = dy * fw
            dx2_ref[...] = r * dyw - x2 * (r * r * r) * _lanemean(dyw * x2)
            sums_ref[0:1, :] += _rowsum(dy * x2 * r)
            sums_ref[1:2, :] += _rowsum(e * e) * (0.5 / D)

    row = pl.BlockSpec((tm, D), lambda i, j: (i, 0))
    vec = pl.BlockSpec((1, D), lambda i, j: (0, 0))
    sh = pl.BlockSpec((None, tm, FF_SH), lambda i, j: (j, i, 0))
    return pl.pallas_call(
        body, name=name,
        grid=(L // tm, N_SHARD),
        in_specs=[row, row, vec, vec, vec, vec, vec,
                  pl.BlockSpec((None, D, FF_SH), lambda i, j: (j, 0, 0)),
                  pl.BlockSpec((None, D, FF_SH), lambda i, j: (j, 0, 0)),
                  pl.BlockSpec((None, FF_SH, D), lambda i, j: (j, 0, 0))],
        out_specs=[row, sh, sh, sh, row, row, pl.BlockSpec((8, D), lambda i, j: (0, 0))],
        out_shape=[jax.ShapeDtypeStruct((L, D), BF16),
                   jax.ShapeDtypeStruct((N_SHARD, L, FF_SH), F32), jax.ShapeDtypeStruct((N_SHARD, L, FF_SH), F32),
                   jax.ShapeDtypeStruct((N_SHARD, L, FF_SH), BF16),
                   jax.ShapeDtypeStruct((L, D), F32), jax.ShapeDtypeStruct((L, D), F32),
                   jax.ShapeDtypeStruct((8, D), F32)],
        scratch_shapes=[pltpu.VMEM((tm, D), BF16), pltpu.VMEM((tm, D), F32)],
        compiler_params=_params("arbitrary", "arbitrary"),
    )(x1, target, nw2, sh2, sc2, g2, fw, wg, wu, wd)


def ffn_bwd(dx2, x1, f, g, u, nw2, sc2, g2, wg, wu, wd, name):
    L = x1.shape[0]
    tm = min(FFN_ROWS, L)

    def body(dx2_ref, x1_ref, f_ref, g_ref, u_ref, nw2_ref, sc2_ref, g2_ref, wg_ref, wu_ref, wd_ref,
             df_ref, dg_ref, du_ref, dx1_ref, sums_ref, df_scr, acc):
        i, j = pl.program_id(0), pl.program_id(1)

        @pl.when((i == 0) & (j == 0))
        def _():
            sums_ref[...] = jnp.zeros_like(sums_ref)

        @pl.when(j == 0)
        def _():
            dx2 = dx2_ref[...]
            df = (g2_ref[...] * dx2).astype(BF16)
            df_scr[...] = df
            df_ref[...] = df
            sums_ref[0:1, :] += _rowsum(dx2 * f_ref[...])
            acc[...] = jnp.zeros_like(acc)

        dh = _dot(df_scr[...], wd_ref[...], 1, 1)
        gv, uv = g_ref[...], u_ref[...]
        silu_g, dsilu_g = _silu_parts(gv)
        dg = (dh * uv * dsilu_g).astype(BF16)
        du = (dh * silu_g).astype(BF16)
        dg_ref[...] = dg
        du_ref[...] = du
        acc[...] += _dot(dg, wg_ref[...], 1, 1) + _dot(du, wu_ref[...], 1, 1)

        @pl.when(j == N_SHARD - 1)
        def _():
            dhx = acc[...]
            xv = x1_ref[...]
            r = lax.rsqrt(_lanemean(xv * xv) + EPS)
            n0 = xv * r
            nw = nw2_ref[...]
            dn2 = dhx * (1.0 + sc2_ref[...])
            dn0 = dn2 * nw
            dx1_ref[...] = dx2_ref[...] + r * (dn0 - n0 * _lanemean(dn0 * n0))
            sums_ref[1:2, :] += _rowsum(dhx)
            sums_ref[2:3, :] += _rowsum(dhx * n0 * nw)
            sums_ref[3:4, :] += _rowsum(dn2 * n0)

    row = pl.BlockSpec((tm, D), lambda i, j: (i, 0))
    vec = pl.BlockSpec((1, D), lambda i, j: (0, 0))
    sh = pl.BlockSpec((None, tm, FF_SH), lambda i, j: (j, i, 0))
    return pl.pallas_call(
        body, name=name,
        grid=(L // tm, N_SHARD),
        in_specs=[row, row, row, sh, sh, vec, vec, vec,
                  pl.BlockSpec((None, D, FF_SH), lambda i, j: (j, 0, 0)),
                  pl.BlockSpec((None, D, FF_SH), lambda i, j: (j, 0, 0)),
                  pl.BlockSpec((None, FF_SH, D), lambda i, j: (j, 0, 0))],
        out_specs=[row, sh, sh, row, pl.BlockSpec((8, D), lambda i, j: (0, 0))],
        out_shape=[jax.ShapeDtypeStruct((L, D), BF16),
                   jax.ShapeDtypeStruct((N_SHARD, L, FF_SH), BF16), jax.ShapeDtypeStruct((N_SHARD, L, FF_SH), BF16),
                   jax.ShapeDtypeStruct((L, D), F32), jax.ShapeDtypeStruct((8, D), F32)],
        scratch_shapes=[pltpu.VMEM((tm, D), BF16), pltpu.VMEM((tm, D), F32)],
        compiler_params=_params("arbitrary", "arbitrary"),
    )(dx2, x1, f, g, u, nw2, sc2, g2, wg, wu, wd)


def matmul_tn(a, b, name, acc_init=None, to_chips=()):
    na, K, M = a.shape
    nb, _, N = b.shape
    n = max(na, nb)
    tk = min(512, K)
    tn = N if N <= 1024 else N // 2
    nk = K // tk
    grid = (n, N // tn, nk)
    has_init = acc_init is not None
    nx = len(to_chips)

    def body(a_ref, b_ref, *refs):
        init_ref = refs[0] if has_init else None
        refs = refs[1:] if has_init else refs
        o_ref = refs[nx]
        if nx:
            start, finish = _to_chips_phases(refs[:nx], refs[nx + 1:2 * nx + 1], *refs[2 * nx + 1:])
            pos, total = _grid_step(grid)
            pl.when(pos == 0)(start)
        kk = pl.program_id(2)

        @pl.when(kk == 0)
        def _():
            o_ref[...] = init_ref[...] if has_init else jnp.zeros_like(o_ref)

        o_ref[...] += _dot(a_ref[...], b_ref[...], 0, 0)
        if nx:
            pl.when(pos == total - 1)(finish)

    out_spec = pl.BlockSpec((None, M, tn), lambda s, j, kk: (s, 0, j))
    in_specs = [pl.BlockSpec((None, tk, M), lambda s, j, kk: (s if na > 1 else 0, kk, 0)),
                pl.BlockSpec((None, tk, tn), lambda s, j, kk: (s if nb > 1 else 0, kk, j))]
    args = [a, b]
    if has_init:
        in_specs.append(out_spec)
        args.append(acc_init)
    out = pl.pallas_call(
        body, name=name,
        grid=grid,
        in_specs=in_specs + [ANY] * nx,
        out_specs=[out_spec] + [ANY] * nx,
        out_shape=[jax.ShapeDtypeStruct((n, M, N), F32)] + _to_chips_shapes(to_chips),
        scratch_shapes=_to_chips_scratch(nx) if nx else [],
        compiler_params=_params(*(("arbitrary",) * 3 if nx else ("parallel", "parallel", "arbitrary"))),
    )(*args, *to_chips)
    return out if nx else out[0]


def dhx_normbwd(dp4, w4, x, dx_res, nw, sc, name, to_chips=()):
    L = x.shape[0]
    tm = min(PROJ_ROWS, L)
    tn = IN_SH // 2
    grid = (L // tm, N_SHARD, 2)
    nx = len(to_chips)

    def body(dp_ref, w_ref, x_ref, res_ref, nw_ref, sc_ref, *refs):
        dx_ref, sums_ref = refs[nx:nx + 2]
        acc = refs[2 * nx + 2]
        if nx:
            start, finish = _to_chips_phases(refs[:nx], refs[nx + 2:2 * nx + 2], *refs[2 * nx + 3:])
            pos, total = _grid_step(grid)
            pl.when(pos == 0)(start)
            pl.when(pos == total - 1)(finish)
        i, k, j = pl.program_id(0), pl.program_id(1), pl.program_id(2)
        first = (k == 0) & (j == 0)

        @pl.when((i == 0) & first)
        def _():
            sums_ref[...] = jnp.zeros_like(sums_ref)

        @pl.when(first)
        def _():
            acc[...] = jnp.zeros_like(acc)

        acc[...] += _dot(dp_ref[...], w_ref[...], 1, 1)

        @pl.when((k == N_SHARD - 1) & (j == 1))
        def _():
            dhx = acc[...]
            xv = x_ref[...]
            r = lax.rsqrt(_lanemean(xv * xv) + EPS)
            n0 = xv * r
            nw = nw_ref[...]
            dn = dhx * (1.0 + sc_ref[...])
            dn0 = dn * nw
            dx_ref[...] = res_ref[...] + r * (dn0 - n0 * _lanemean(dn0 * n0))
            sums_ref[0:1, :] += _rowsum(dhx)
            sums_ref[1:2, :] += _rowsum(dhx * n0 * nw)
            sums_ref[2:3, :] += _rowsum(dn * n0)

    row = pl.BlockSpec((tm, D), lambda i, k, j: (i, 0))
    vec = pl.BlockSpec((1, D), lambda i, k, j: (0, 0))
    return pl.pallas_call(
        body, name=name,
        grid=grid,
        in_specs=[pl.BlockSpec((None, tm, tn), lambda i, k, j: (k, i, j)),
                  pl.BlockSpec((None, D, tn), lambda i, k, j: (k, 0, j)),
                  row, row, vec, vec] + [ANY] * nx,
        out_specs=[row, pl.BlockSpec((8, D), lambda i, k, j: (0, 0))] + [ANY] * nx,
        out_shape=[jax.ShapeDtypeStruct((L, D), F32), jax.ShapeDtypeStruct((8, D), F32)] + _to_chips_shapes(to_chips),
        scratch_shapes=[pltpu.VMEM((tm, D), F32)] + (_to_chips_scratch(nx) if nx else []),
        compiler_params=_params("arbitrary", "arbitrary", "arbitrary"),
    )(dp4, w4, x, dx_res, nw, sc, *to_chips)


SMALL_ROWS = 24


def _rope_tables(L):
    rows = L // 64
    row = jnp.repeat(jnp.arange(rows, dtype=F32), 64)
    col = jnp.tile(jnp.arange(64, dtype=F32), rows)
    freqs = 10000.0 ** (-jnp.arange(RT_DK // 4, dtype=F32) / (RT_DK // 4))
    ang = jnp.concatenate([row[:, None] * freqs, col[:, None] * freqs], axis=-1)
    cos, sin = jnp.cos(ang), jnp.sin(ang)
    return jnp.concatenate([cos, cos], axis=1), jnp.concatenate([-sin, sin], axis=1)


def _shard_major(pieces):
    dp = jnp.concatenate(pieces, axis=1)
    return dp.reshape(dp.shape[0], N_SHARD, IN_SH).transpose(1, 0, 2)


def _lane0(a):
    return a[:, 0, 0]


def _pack_small(rows):
    out = [r.reshape(1, D) for r in rows]
    out += [jnp.zeros((1, D), F32)] * (SMALL_ROWS - len(out))
    return jnp.concatenate(out, axis=0)


def _sibling_sums(gs, names, place):
    core, core_arg, _ = place
    payload = [lax.dynamic_slice_in_dim(g, (1 - core) * (g.shape[1] // 2), g.shape[1] // 2, axis=1).astype(BF16)
               for g in gs]
    received = rs_to_sibling(payload, "rs_to_sibling_" + names[0])
    return [rs_add_sibling(g, r, core_arg, "rs_add_sibling_" + k) for g, r, k in zip(gs, received, names)]


def local_step(x, ctx, target, mod_x, mod_c, lb_f, lb_b, lg_f, lg_b, nw1, nw2, hgw, fw, w, rest=None, place=None):
    L, Lc = x.shape[0], ctx.shape[0]
    sh1, sc1, g1, sh2, sc2, g2 = (mod_x[i:i + 1] for i in range(6))
    sh1c, sc1c = mod_c[0:1], mod_c[1:2]
    cosf, sinf = _rope_tables(L)
    cosc, sinc = jnp.ones((Lc, RT_DK), F32), jnp.zeros((Lc, RT_DK), F32)
    zero_h = jnp.zeros((HEADS, HG_D, HG_D), F32)
    zero_r = jnp.zeros((HEADS, RT_DV, RT_DK), F32)

    pc, hxc = normmod_matmul(ctx, nw1, sh1c, sc1c, w["w_in"], "ctx_in_proj")
    _, s_hf, cb_hf = hgrn_scan_fwd(pc, lb_f, zero_h, COL_HFF, False, "ctx_hgrn_f")
    _, s_hb, cb_hb = hgrn_scan_fwd(pc, lb_b, zero_h, COL_HFB, True, "ctx_hgrn_b")
    _, s_rf, cb_rf = ret_scan_fwd(pc, cosc, sinc, lg_f, zero_r, False, "ctx_ret_f")
    _, s_rb, cb_rb = ret_scan_fwd(pc, cosc, sinc, lg_b, zero_r, True, "ctx_ret_b")
    if rest is None:
        p, hx = normmod_matmul(x, nw1, sh1, sc1, w["w_in"], "in_proj")
    else:
        p, hx, g_pa, g_pb, g_out, g_wg, g_wu, g_wd = normmod_matmul(x, nw1, sh1, sc1, w["w_in"], "in_proj", gather=rest)
        w = dict(w, w_pa=g_pa.reshape(D, D), w_pb=g_pb.reshape(2 * D, D), w_out=g_out.reshape(D, D),
                 wg=g_wg, wu=g_wu, wd=g_wd)
    ohf, _, xb_hf = hgrn_scan_fwd(p, lb_f, s_hf, COL_HFF, False, "hgrn_f")
    ohb, _, xb_hb = hgrn_scan_fwd(p, lb_b, s_hb, COL_HFB, True, "hgrn_b")
    orf, _, xb_rf = ret_scan_fwd(p, cosf, sinf, lg_f, s_rf, False, "ret_f")
    orb, _, xb_rb = ret_scan_fwd(p, cosf, sinf, lg_b, s_rb, True, "ret_b")
    x1, x_mix, merged, ya, yb = mix_fwd(ohf, ohb, orf, orb, p, x, g1, hgw, w["w_pa"], w["w_pb"], w["w_out"], "mix_fwd")
    hx2, gg, uu, hh, ff, dx2, sums_f = ffn_fwd(x1, target, nw2, sh2, sc2, g2, fw, w["wg"], w["wu"], w["wd"], "ffn_fwd")

    d_f, d_g, d_u, dx1, sums_fb = ffn_bwd(dx2, x1, ff, gg, uu, nw2, sc2, g2, w["wg"], w["wu"], w["wd"], "ffn_bwd")
    grads = {
        "wg": matmul_tn(hx2[None], d_g, "dw_ffn_gate"),
        "wu": matmul_tn(hx2[None], d_u, "dw_ffn_up"),
        "wd": matmul_tn(hh, d_f[None], "dw_ffn_down"),
    }
    dxm, d_a, d_b, dga, dgb, dhg, drg, dohg, dort, sums_m = mix_bwd(
        dx1, x_mix, ya, yb, ohf, ohb, orf, orb, p, g1, hgw, w["w_pa"], w["w_pb"], w["w_out"], "mix_bwd")
    grads["w_out"] = matmul_tn(merged[None], dxm[None], "dw_out").reshape(N_SHARD, D // N_SHARD, D)
    grads["w_pa"] = matmul_tn(ya[None], d_a[None], "dw_proj_hgrn").reshape(N_SHARD, D // N_SHARD, D)
    grads["w_pb"] = matmul_tn(yb[None], d_b[None], "dw_proj_ret").reshape(N_SHARD, 2 * D // N_SHARD, D)

    rq1, rk1, rv1, dlgf_x, ds_rf = ret_scan_bwd(p, cosf, sinf, lg_f, xb_rf, dort, zero_r, None, False, "ret_f_bwd")
    drq, drk, drv, dlgb_x, ds_rb = ret_scan_bwd(p, cosf, sinf, lg_b, xb_rb, dort, zero_r, (rq1, rk1, rv1), True, "ret_b_bwd")
    hq1, dzf, hv1, dlbf_x, ds_hf = hgrn_scan_bwd(p, lb_f, xb_hf, dohg, zero_h, None, COL_HFF, False, "hgrn_f_bwd")
    dhq, dzb, dhv, dlbb_x, ds_hb = hgrn_scan_bwd(p, lb_b, xb_hb, dohg, zero_h, (hq1, hv1), COL_HFB, True, "hgrn_b_bwd")
    dp4 = _shard_major([dhq, dzf, dzb, dhv, dhg, drq, drk, drv, drg, dga, dgb])
    others = ["w_pa", "w_pb", "w_out", "wg", "wu", "wd"]
    if place is None:
        dw_in = matmul_tn(hx[None], dp4, "dw_in")
    else:
        sums_o = _sibling_sums([grads[k] for k in others], others, place)
        dw_in, *recv_o = matmul_tn(hx[None], dp4, "dw_in", to_chips=[a16 for _, a16 in sums_o])

    zc = jnp.zeros((Lc, D), F32)
    zc2 = jnp.zeros((Lc, 2 * D), F32)
    crq1, crk1, crv1, dlgf_c, _ = ret_scan_bwd(pc, cosc, sinc, lg_f, cb_rf, zc2, ds_rf, None, False, "ctx_ret_f_bwd")
    cdrq, cdrk, cdrv, dlgb_c, _ = ret_scan_bwd(pc, cosc, sinc, lg_b, cb_rb, zc2, ds_rb, (crq1, crk1, crv1), True, "ctx_ret_b_bwd")
    chq1, cdzf, chv1, dlbf_c, _ = hgrn_scan_bwd(pc, lb_f, cb_hf, zc, ds_hf, None, COL_HFF, False, "ctx_hgrn_f_bwd")
    cdhq, cdzb, cdhv, dlbb_c, _ = hgrn_scan_bwd(pc, lb_b, cb_hb, zc, ds_hb, (chq1, chv1), COL_HFB, True, "ctx_hgrn_b_bwd")
    zb = jnp.zeros((Lc, D), BF16)
    zb2 = jnp.zeros((Lc, 2 * D), BF16)
    dpc4 = _shard_major([cdhq, cdzf, cdzb, cdhv, zb, cdrq, cdrk, cdrv, zb2, zb, zb])
    _, sums_c = dhx_normbwd(dpc4, w["w_in"], ctx, zc, nw1, sc1c, "dctx_in_proj")
    grads["w_in"] = matmul_tn(hxc[None], dpc4, "dw_in_ctx", acc_init=dw_in)
    if place is None:
        dx, sums_x = dhx_normbwd(dp4, w["w_in"], x, dx1, nw1, sc1, "dx_in_proj")
    else:
        sums_i = _sibling_sums([grads["w_in"]], ["w_in"], place)
        dx, sums_x, recv_i = dhx_normbwd(dp4, w["w_in"], x, dx1, nw1, sc1, "dx_in_proj", to_chips=[sums_i[0][1]])
        names = ["w_in"] + others
        halves = [rs_add_chips(a, r, place[2], "rs_add_chips_" + k)
                  for (a, _), r, k in zip(sums_i + sums_o, [recv_i] + recv_o, names)]
        grads = dict(zip(names, rs_join_halves(halves, "rs_join_halves")))

    def lg_row(f, b):
        return jnp.concatenate([_lane0(f), _lane0(b), jnp.zeros((D - 2 * HEADS,), F32)])

    small = _pack_small([
        sums_x[0], sums_x[1], sums_m[0], sums_fb[1], sums_fb[2], sums_fb[0],
        sums_c[0], sums_c[1],
        sums_x[2], sums_c[2], sums_fb[3], sums_m[1], sums_f[0],
        dlbf_x, dlbf_c, dlbb_x, dlbb_c,
        lg_row(dlgf_x, dlgb_x), lg_row(dlgf_c, dlgb_c),
        sums_f[1],
    ])
    return dx, grads, small


MESH = pl.DeviceIdType.MESH
ANY = pl.BlockSpec(memory_space=pl.ANY)
N_DEV = 8


def _place():
    return lax.axis_index("x"), lax.axis_index("y"), lax.axis_index("c")


def _other_chips(x, y):
    return [(1 - x, y), (x, 1 - y), (1 - x, 1 - y)]


def allgather8(xs, name):
    m, n = xs.shape

    def body(x_ref, out_ref, send_sems, recv_sems, local_sem):
        x, y, c = _place()
        me, sibling = (x, y, c), (x, y, 1 - c)
        chips = _other_chips(x, y)

        def rows(px, py, pc):
            return out_ref.at[pl.ds((4 * px + 2 * py + pc) * m, m), :]

        def copy(k, block, to, src=None):
            return pltpu.make_async_remote_copy(
                src_ref=rows(*block) if src is None else src, dst_ref=rows(*block),
                send_sem=send_sems.at[k], recv_sem=recv_sems.at[k], device_id=to, device_id_type=MESH)

        mine = pltpu.make_async_copy(x_ref, rows(*me), local_sem)
        mine.start()
        first = [copy(0, me, sibling, src=x_ref)]
        first += [copy(1 + j, me, (*chip, c), src=x_ref) for j, chip in enumerate(chips)]
        for cp in first:
            cp.start()
        passed = [copy(4 + j, (*chip, c), sibling) for j, chip in enumerate(chips)]
        for j, chip in enumerate(chips):
            copy(1 + j, (*chip, c), me).wait_recv()
            passed[j].start()
        copy(0, sibling, me).wait_recv()
        for j, chip in enumerate(chips):
            copy(4 + j, (*chip, 1 - c), me).wait_recv()
        for cp in first + passed:
            cp.wait_send()
        mine.wait()

    return pl.pallas_call(
        body, name=name,
        out_shape=jax.ShapeDtypeStruct((N_DEV * m, n), xs.dtype),
        in_specs=[pl.BlockSpec(memory_space=pltpu.VMEM)],
        out_specs=pl.BlockSpec(memory_space=pltpu.VMEM),
        scratch_shapes=[pltpu.SemaphoreType.DMA((7,)), pltpu.SemaphoreType.DMA((7,)), pltpu.SemaphoreType.DMA],
    )(xs)


def _gather_phases(ins, outs, send_sems, recv_sems, local_sems):
    n = len(ins)
    x, y, c = _place()
    chips = _other_chips(x, y)

    def rows(i, core):
        h = ins[i].shape[0] // 2
        return pl.ds(pl.multiple_of(core * h, 16), h)

    def landed(i, chip, core):
        return outs[i].at[2 * chip[0] + chip[1], rows(i, core), :]

    def copy(i, k, src, dst, to):
        return pltpu.make_async_remote_copy(src_ref=src, dst_ref=dst, send_sem=send_sems.at[6 * i + k],
                                            recv_sem=recv_sems.at[6 * i + k], device_id=to, device_id_type=MESH)

    def local(i):
        return pltpu.make_async_copy(ins[i], outs[i].at[2 * x + y], local_sems.at[i])

    def send(i, j):
        return copy(i, j, ins[i].at[rows(i, c), :], landed(i, (x, y), c), (*chips[j], c))

    def arrived(i, j, core, k):
        return copy(i, k, ins[i].at[rows(i, core), :], landed(i, chips[j], core), (x, y, 1 - c))

    def passed(i, j):
        return copy(i, 3 + j, landed(i, chips[j], c), landed(i, chips[j], c), (x, y, 1 - c))

    def start():
        for i in range(n):
            local(i).start()
            for j in range(3):
                send(i, j).start()

    def forward():
        for i in range(n):
            for j in range(3):
                arrived(i, j, c, j).wait_recv()
                passed(i, j).start()

    def finish():
        for i in range(n):
            for j in range(3):
                arrived(i, j, 1 - c, 3 + j).wait_recv()
        for i in range(n):
            for j in range(3):
                send(i, j).wait_send()
                passed(i, j).wait_send()
            local(i).wait()

    return start, forward, finish


def _gather_scratch(n):
    return [pltpu.SemaphoreType.DMA((6 * n,)), pltpu.SemaphoreType.DMA((6 * n,)), pltpu.SemaphoreType.DMA((n,))]


def gather_weights(shards, name):
    n = len(shards)

    def body(*refs):
        start, forward, finish = _gather_phases(refs[:n], refs[n:2 * n], *refs[2 * n:])
        start()
        forward()
        finish()

    return pl.pallas_call(
        body, name=name,
        out_shape=[jax.ShapeDtypeStruct((N_SHARD,) + s.shape, s.dtype) for s in shards],
        in_specs=[ANY] * n, out_specs=[ANY] * n,
        scratch_shapes=_gather_scratch(n),
    )(*shards)


def rs_to_sibling(payloads, name):
    n = len(payloads)

    def body(*refs):
        ins, outs = refs[:n], refs[n:2 * n]
        send_sems, recv_sems = refs[2 * n:]
        x, y, c = _place()
        copies = []
        for i in range(n):
            cp = pltpu.make_async_remote_copy(src_ref=ins[i], dst_ref=outs[i], send_sem=send_sems.at[i],
                                              recv_sem=recv_sems.at[i], device_id=(x, y, 1 - c), device_id_type=MESH)
            cp.start()
            copies.append(cp)
        for cp in copies:
            cp.wait()

    return pl.pallas_call(
        body, name=name,
        out_shape=[jax.ShapeDtypeStruct(g.shape, g.dtype) for g in payloads],
        in_specs=[ANY] * n, out_specs=[ANY] * n,
        scratch_shapes=[pltpu.SemaphoreType.DMA((n,)), pltpu.SemaphoreType.DMA((n,))],
    )(*payloads)


def _to_chips_phases(ins, outs, send_sems, recv_sems):
    def copies():
        x, y, c = _place()
        return [pltpu.make_async_remote_copy(
            src_ref=ins[i].at[2 * px + py], dst_ref=outs[i].at[j], send_sem=send_sems.at[3 * i + j],
            recv_sem=recv_sems.at[3 * i + j], device_id=(px, py, c), device_id_type=MESH)
            for i in range(len(ins)) for j, (px, py) in enumerate(_other_chips(x, y))]

    def start():
        for cp in copies():
            cp.start()

    def finish():
        for cp in copies():
            cp.wait()

    return start, finish


def _to_chips_shapes(parts):
    return [jax.ShapeDtypeStruct((3,) + a.shape[1:], a.dtype) for a in parts]


def _to_chips_scratch(n):
    return [pltpu.SemaphoreType.DMA((3 * n,)), pltpu.SemaphoreType.DMA((3 * n,))]


def rs_join_halves(fulls, name):
    n = len(fulls)

    def body(*refs):
        outs = refs[n:2 * n]
        send_sems, recv_sems = refs[2 * n:]
        x, y, c = _place()

        def copy(i, core):
            h = fulls[i].shape[0] // 2
            rows = outs[i].at[pl.ds(pl.multiple_of(core * h, 8), h), :]
            return pltpu.make_async_remote_copy(src_ref=rows, dst_ref=rows, send_sem=send_sems.at[i],
                                                recv_sem=recv_sems.at[i], device_id=(x, y, 1 - c), device_id_type=MESH)

        sent = [copy(i, c) for i in range(n)]
        for cp in sent:
            cp.start()
        for i in range(n):
            copy(i, 1 - c).wait_recv()
        for cp in sent:
            cp.wait_send()

    return pl.pallas_call(
        body, name=name,
        out_shape=[jax.ShapeDtypeStruct(a.shape, a.dtype) for a in fulls],
        in_specs=[ANY] * n, out_specs=[ANY] * n,
        input_output_aliases={i: i for i in range(n)},
        scratch_shapes=[pltpu.SemaphoreType.DMA((n,)), pltpu.SemaphoreType.DMA((n,))],
    )(*fulls)


def _row_tile(rows, cols, limit_bytes=2 * 1024 * 1024, mult=8):
    best = mult
    for t in range(mult, rows + 1, mult):
        if rows % t == 0 and t * cols * 4 <= limit_bytes:
            best = t
    return best


def rs_add_sibling(g, recv, c, name):
    _, R, C = g.shape
    h = R // 2
    tr = _row_tile(h, C, mult=16)
    nt = h // tr

    def body(c_ref, g_ref, r_ref, o_ref, o16_ref):
        s = g_ref[...] + r_ref[...].astype(F32)
        o_ref[...] = s
        o16_ref[...] = s.astype(BF16)

    blk = pl.BlockSpec((None, tr, C), lambda k, i, c_ref: (k, i, 0))
    return pl.pallas_call(
        body, name=name,
        grid_spec=pltpu.PrefetchScalarGridSpec(
            num_scalar_prefetch=1, grid=(N_SHARD, nt),
            in_specs=[pl.BlockSpec((None, tr, C), lambda k, i, c_ref: (k, c_ref[0] * nt + i, 0)), blk],
            out_specs=[blk, blk]),
        out_shape=[jax.ShapeDtypeStruct((N_SHARD, h, C), F32), jax.ShapeDtypeStruct((N_SHARD, h, C), BF16)],
        compiler_params=_params("parallel", "parallel"),
    )(c, g, recv)


def rs_add_chips(part, recv, place, name):
    _, h, C = part.shape
    tr = _row_tile(h, C, mult=16)
    nt = h // tr

    def body(k_ref, p_ref, r_ref, o_ref):
        o_ref[...] = ((p_ref[...] + r_ref[0].astype(F32)) + r_ref[1].astype(F32)) + r_ref[2].astype(F32)

    return pl.pallas_call(
        body, name=name,
        grid_spec=pltpu.PrefetchScalarGridSpec(
            num_scalar_prefetch=1, grid=(nt,),
            in_specs=[pl.BlockSpec((None, tr, C), lambda i, k_ref: (k_ref[0], i, 0)),
                      pl.BlockSpec((3, tr, C), lambda i, k_ref: (0, i, 0))],
            out_specs=pl.BlockSpec((tr, C), lambda i, k_ref: (k_ref[1] * nt + i, 0))),
        out_shape=jax.ShapeDtypeStruct((2 * h, C), F32),
        compiler_params=_params("parallel"),
    )(place, part, recv)


def _adamw_math(w, g, m, v):
    m = ADAM_B1 * m + (1.0 - ADAM_B1) * g
    v = ADAM_B2 * v + (1.0 - ADAM_B2) * (g * g)
    m_hat = m / (1.0 - ADAM_B1 ** ADAM_STEP)
    v_hat = v / (1.0 - ADAM_B2 ** ADAM_STEP)
    delta = -ADAM_LR * (m_hat / (jnp.sqrt(v_hat) + ADAM_EPS) + ADAM_WD * w)
    return delta, m, v


def adamw(w, g, m, v, name):
    R, C = w.shape
    tr = _row_tile(R, C, 1024 * 1024)

    def body(w_ref, g_ref, m_ref, v_ref, d_ref, nm_ref, nv_ref):
        d_ref[...], nm_ref[...], nv_ref[...] = _adamw_math(w_ref[...], g_ref[...], m_ref[...], v_ref[...])

    blk = pl.BlockSpec((tr, C), lambda i: (i, 0))
    return pl.pallas_call(
        body, name=name, grid=(R // tr,), in_specs=[blk] * 4, out_specs=[blk] * 3,
        out_shape=[jax.ShapeDtypeStruct((R, C), F32)] * 3,
        compiler_params=_params("parallel"),
    )(w, g, m, v)


MOD_SH = 6 * D // N_SHARD
PK_ROWS = 16


def mod_fwd(call16, w_sh, b_sh, name):
    def body(c_ref, w_ref, b_ref, o_ref):
        o_ref[...] = _dot(_silu_parts(c_ref[...])[0], w_ref[...], prec=HI) + b_ref[...]

    return pl.pallas_call(body, name=name, out_shape=jax.ShapeDtypeStruct((16, MOD_SH), F32),
                          compiler_params=_params())(call16, w_sh, b_sh)


def prep_small(lbf2, lbb2, theta_row, name):
    def body(f_ref, b_ref, t_ref, lbf_ref, lbb_ref, lg_ref):
        lbf_ref[...] = _sigmoid(f_ref[0:1, :] - f_ref[1:2, :])
        lbb_ref[...] = _sigmoid(b_ref[0:1, :] - b_ref[1:2, :])
        t = t_ref[...]
        lg_ref[...] = jnp.minimum(t, 0.0) - jnp.log(1.0 + jnp.exp(-jnp.abs(t)))

    row = jax.ShapeDtypeStruct((1, D), F32)
    return pl.pallas_call(body, name=name, out_shape=[row, row, row], compiler_params=_params())(lbf2, lbb2, theta_row)


def small_grads(g3, lbf, lbb, theta_row, name):
    def body(g_ref, lbf_ref, lbb_ref, t_ref, pk_ref, aux_ref):
        s = g_ref[0]
        for d in range(1, N_DEV):
            s = s + g_ref[d]
        pk_ref[...] = jnp.zeros_like(pk_ref)
        aux_ref[...] = jnp.zeros_like(aux_ref)
        pk_ref[1:7, :] = s[0:6]
        pk_ref[1:3, :] += s[6:8]
        pk_ref[7:8, :] = s[8:9] + s[9:10]
        pk_ref[8:9, :] = s[10:11]
        lbf, lbb = lbf_ref[...], lbb_ref[...]
        daf = (s[13:14] + s[14:15]) * lbf * (1.0 - lbf)
        dab = (s[15:16] + s[16:17]) * lbb * (1.0 - lbb)
        pk_ref[9:10, :] = daf
        pk_ref[10:11, :] = -daf
        pk_ref[11:12, :] = dab
        pk_ref[12:13, :] = -dab
        pk_ref[13:14, :] = s[11:12]
        pk_ref[14:15, :] = (s[17:18] + s[18:19]) * _sigmoid(-t_ref[...])
        pk_ref[15:16, :] = s[12:13]
        aux_ref[0:2, :] = s[6:8]
        aux_ref[2:3, :] = jnp.broadcast_to(jnp.sum(s[19:20], axis=-1, keepdims=True), (1, D))

    return pl.pallas_call(body, name=name,
                          out_shape=[jax.ShapeDtypeStruct((PK_ROWS, D), F32), jax.ShapeDtypeStruct((8, D), F32)],
                          compiler_params=_params())(g3, lbf, lbb, theta_row)


def mod_bwd(call16, dmod_sh, w_sh, name):
    def body(c_ref, d_ref, w_ref, dw_ref, ds_ref):
        dm = d_ref[...]
        dw_ref[...] = _dot(_silu_parts(c_ref[...])[0], dm, 0, 0, prec=HI)
        ds_ref[...] = jnp.zeros_like(ds_ref)
        ds_ref[0:1, :] = _dot(dm[8:9, :], w_ref[...], 1, 1, prec=HI)

    return pl.pallas_call(body, name=name,
                          out_shape=[jax.ShapeDtypeStruct((D, MOD_SH), F32), jax.ShapeDtypeStruct((8, D), F32)],
                          compiler_params=_params())(call16, dmod_sh, w_sh)


def adamw_small(g4, pk_g, pk_w, pk_m, pk_v, name):
    def body(g4_ref, g_ref, w_ref, m_ref, v_ref, go_ref, d_ref, nm_ref, nv_ref):
        w = w_ref[...]
        ds = ((g4_ref[0:1, :] + g4_ref[16:17, :]) + g4_ref[32:33, :]) + g4_ref[48:49, :]
        row = lax.broadcasted_iota(jnp.int32, (PK_ROWS, D), 0)
        g = jnp.where(row == 0, ds * _silu_parts(w[0:1, :])[1], g_ref[...])
        go_ref[...] = g
        d_ref[...], nm_ref[...], nv_ref[...] = _adamw_math(w, g, m_ref[...], v_ref[...])

    pk = jax.ShapeDtypeStruct((PK_ROWS, D), F32)
    return pl.pallas_call(body, name=name, out_shape=[pk, pk, pk, pk], compiler_params=_params())(g4, pk_g, pk_w, pk_m, pk_v)


def _pack_params(c_ctx, b_mod, n1, n2, lbf, lbb, hgn, th_f, th_b, fin):
    theta = jnp.concatenate([th_f.reshape(HEADS), th_b.reshape(HEADS), jnp.zeros((D - 2 * HEADS,), F32)])
    return jnp.concatenate([c_ctx.reshape(1, D), b_mod.reshape(6, D), n1.reshape(1, D), n2.reshape(1, D), lbf, lbb,
                            hgn.reshape(1, D), theta.reshape(1, D), fin.reshape(1, D)], axis=0)


def _unpack_params(pk):
    return (pk[0], pk[1:7].reshape(1, 6 * D), pk[7:8], pk[8:9], pk[9:11], pk[11:13], pk[13:14],
            pk[14, 0:HEADS].reshape(1, HEADS), pk[14, HEADS:2 * HEADS].reshape(1, HEADS), pk[15])


def kernel(x, c, ctx, c_ctx, w_mod, b_mod, norm1_w, norm2_w, w_in, hg_lb_fwd, hg_lb_bwd, hg_norm_w, rt_theta_fwd, rt_theta_bwd, w_proj_hgrn, w_proj_ret, w_out, w_ffn_gate, w_ffn_up, w_ffn_down, final_norm_w, loss_target, m_c_ctx, m_w_mod, m_b_mod, m_norm1_w, m_norm2_w, m_w_in, m_hg_lb_fwd, m_hg_lb_bwd, m_hg_norm_w, m_rt_theta_fwd, m_rt_theta_bwd, m_w_proj_hgrn, m_w_proj_ret, m_w_out, m_w_ffn_gate, m_w_ffn_up, m_w_ffn_down, m_final_norm_w, v_c_ctx, v_w_mod, v_b_mod, v_norm1_w, v_norm2_w, v_w_in, v_hg_lb_fwd, v_hg_lb_bwd, v_hg_norm_w, v_rt_theta_fwd, v_rt_theta_bwd, v_w_proj_hgrn, v_w_proj_ret, v_w_out, v_w_ffn_gate, v_w_ffn_up, v_w_ffn_down, v_final_norm_w):
    xi, yi, ci = _place()
    dev = 4 * xi + 2 * yi + ci
    chip = 2 * xi + yi
    core_arg = jnp.reshape(ci, (1,)).astype(jnp.int32)
    place_arg = jnp.stack([chip, ci]).astype(jnp.int32)

    c_all = allgather8(jnp.concatenate([c, jnp.zeros((7, D), F32)], axis=0), "gather_c").reshape(N_DEV, 8, D)[:, 0]
    call16 = jnp.concatenate([c_all, c_ctx.reshape(1, D), jnp.zeros((7, D), F32)], axis=0)
    b_sh = lax.dynamic_slice_in_dim(b_mod, chip * MOD_SH, MOD_SH, axis=1)
    mod_sh = mod_fwd(call16, w_mod[0], b_sh, "mod_fwd")
    mod_g = allgather8(mod_sh, "gather_mod").reshape(N_DEV, 16, MOD_SH)
    mod_all = jnp.concatenate([mod_g[0], mod_g[2], mod_g[4], mod_g[6]], axis=1)
    mod_x = lax.dynamic_index_in_dim(mod_all, dev, axis=0, keepdims=False).reshape(6, D)
    mod_c = mod_all[8].reshape(6, D)

    pk_w = _pack_params(c_ctx, b_mod, norm1_w, norm2_w, hg_lb_fwd, hg_lb_bwd, hg_norm_w, rt_theta_fwd, rt_theta_bwd, final_norm_w)
    theta_row = pk_w[14:15]
    lb_f, lb_b, lg_row = prep_small(hg_lb_fwd, hg_lb_bwd, theta_row, "prep_small")
    lg_f = jnp.broadcast_to(lg_row[0, 0:HEADS].reshape(HEADS, 1, 1), (HEADS, 1, RT_DV))
    lg_b = jnp.broadcast_to(lg_row[0, HEADS:2 * HEADS].reshape(HEADS, 1, 1), (HEADS, 1, RT_DV))

    (g_in,) = gather_weights([w_in[0].astype(BF16)], "gather_w_in")
    rest = [s[0].astype(BF16) for s in (w_proj_hgrn, w_proj_ret, w_out, w_ffn_gate, w_ffn_up, w_ffn_down)]

    dx, full, small = local_step(x[0], ctx[0], loss_target[0], mod_x, mod_c, lb_f, lb_b, lg_f, lg_b,
                                 norm1_w, norm2_w, hg_norm_w, final_norm_w.reshape(1, D), {"w_in": g_in}, rest,
                                 (ci, core_arg, place_arg))

    g3 = allgather8(small, "gather_small").reshape(N_DEV, SMALL_ROWS, D)
    pk_g, aux = small_grads(g3, lb_f, lb_b, theta_row, "small_grads")
    loss = aux[2, 0]
    dmod16 = jnp.concatenate([
        g3[:, 0:6, :].reshape(N_DEV, 6 * D),
        jnp.concatenate([aux[0], aux[1], jnp.zeros((4 * D,), F32)]).reshape(1, 6 * D),
        jnp.zeros((7, 6 * D), F32)], axis=0)
    dmod_sh = lax.dynamic_slice_in_dim(dmod16, chip * MOD_SH, MOD_SH, axis=1)
    g_wmod, dsilu = mod_bwd(call16, dmod_sh, w_mod[0], "mod_bwd")
    g4 = allgather8(dsilu, "gather_dsilu")
    pk_m = _pack_params(m_c_ctx, m_b_mod, m_norm1_w, m_norm2_w, m_hg_lb_fwd, m_hg_lb_bwd, m_hg_norm_w, m_rt_theta_fwd, m_rt_theta_bwd, m_final_norm_w)
    pk_v = _pack_params(v_c_ctx, v_b_mod, v_norm1_w, v_norm2_w, v_hg_lb_fwd, v_hg_lb_bwd, v_hg_norm_w, v_rt_theta_fwd, v_rt_theta_bwd, v_final_norm_w)
    pk_g, pk_d, pk_nm, pk_nv = adamw_small(g4, pk_g, pk_w, pk_m, pk_v, "adamw_small")

    big = {
        "w_mod": (g_wmod, w_mod, m_w_mod, v_w_mod),
        "w_in": (full["w_in"], w_in, m_w_in, v_w_in),
        "w_pa": (full["w_pa"], w_proj_hgrn, m_w_proj_hgrn, v_w_proj_hgrn),
        "w_pb": (full["w_pb"], w_proj_ret, m_w_proj_ret, v_w_proj_ret),
        "w_out": (full["w_out"], w_out, m_w_out, v_w_out),
        "wg": (full["wg"], w_ffn_gate, m_w_ffn_gate, v_w_ffn_gate),
        "wu": (full["wu"], w_ffn_up, m_w_ffn_up, v_w_ffn_up),
        "wd": (full["wd"], w_ffn_down, m_w_ffn_down, v_w_ffn_down),
    }
    res = {}
    for k, (g, wt, mt, vt) in big.items():
        d, nm, nv = adamw(wt[0], g, mt[0], vt[0], "adamw_" + k)
        res[k] = (g[None], d[None], nm[None], nv[None])

    sm = [_unpack_params(p) for p in (pk_g, pk_d, pk_nm, pk_nv)]
    outs = []
    for t in range(4):
        (s_cctx, s_bmod, s_n1, s_n2, s_lbf, s_lbb, s_hgn, s_thf, s_thb, s_fin) = sm[t]
        outs.append([s_cctx, res["w_mod"][t], s_bmod, s_n1, s_n2, res["w_in"][t], s_lbf, s_lbb, s_hgn, s_thf, s_thb,
                     res["w_pa"][t], res["w_pb"][t], res["w_out"][t], res["wg"][t], res["wu"][t], res["wd"][t], s_fin])
    return (loss, dx[None], *outs[0], *outs[1], *outs[2], *outs[3])
```

```python
import functools

import jax
import jax.numpy as jnp
from jax import lax
from jax.experimental import pallas as pl
from jax.experimental.pallas import tpu as pltpu

F32 = jnp.float32
BF16 = jnp.bfloat16
HI = lax.Precision.HIGHEST

D = 1024
HEADS = 8
HG_D = 128
RT_DK = 128
RT_DV = 256
D_FF = 2816
D_IN = 13312
N_SHARD = 4
IN_SH = D_IN // N_SHARD
FF_SH = D_FF // N_SHARD
HG_CHUNK = 32
SCAN_ROWS = 256
HG_GROUP = 8
RT_GROUP = 4
PROJ_ROWS = 1024
EPS = 1e-6
GN_EPS = 1e-5
Q_SCALE = 128.0 ** -0.5
VMEM_LIMIT = 56 * 1024 * 1024

COL_HQ, COL_HFF, COL_HFB, COL_HI, COL_HG = 0, 8, 16, 24, 32
COL_RQ, COL_RK, COL_RV, COL_RG, COL_GA, COL_GB = 40, 48, 56, 72, 88, 96

ADAM_LR, ADAM_B1, ADAM_B2, ADAM_EPS, ADAM_WD, ADAM_STEP = 0.001, 0.9, 0.999, 1e-08, 0.01, 10


def _params(*sem):
    return pltpu.CompilerParams(dimension_semantics=sem, vmem_limit_bytes=VMEM_LIMIT)


def _dot(a, b, ca=1, cb=0, prec=None):
    return lax.dot_general(a, b, (((ca,), (cb,)), ((), ())), precision=prec, preferred_element_type=F32)


def _bdot(a, b, ca=1, cb=0):
    return _dot(a.astype(BF16), b.astype(BF16), ca, cb)


def _sigmoid(z):
    return 1.0 / (1.0 + jnp.exp(-z))


def _rowsum(a):
    return jnp.sum(a, axis=0, keepdims=True)


def _lanemean(a):
    return jnp.mean(a, axis=-1, keepdims=True)


def _grid_step(grid):
    pos, total = 0, 1
    for d, size in enumerate(grid):
        pos = pos * size + pl.program_id(d)
        total *= size
    return pos, total


def normmod_matmul(x, nw, sh, sc, w4, name, gather=()):
    L = x.shape[0]
    tm = min(PROJ_ROWS, L)
    tn = IN_SH // 2
    grid = (L // tm, N_SHARD, 2)
    ng = len(gather)

    def body(x_ref, nw_ref, sh_ref, sc_ref, w_ref, *refs):
        p_ref, hx_ref = refs[ng:ng + 2]
        hx_scr = refs[2 * ng + 2]
        if ng:
            start, forward, finish = _gather_phases(refs[:ng], refs[ng + 2:2 * ng + 2], *refs[2 * ng + 3:])
            pos, total = _grid_step(grid)
            pl.when(pos == 0)(start)
            pl.when(pos == total // 2)(forward)

        @pl.when((pl.program_id(1) == 0) & (pl.program_id(2) == 0))
        def _():
            xv = x_ref[...]
            n = xv * lax.rsqrt(_lanemean(xv * xv) + EPS) * nw_ref[...]
            h = (n * (1.0 + sc_ref[...]) + sh_ref[...]).astype(BF16)
            hx_scr[...] = h
            hx_ref[...] = h

        p_ref[...] = _dot(hx_scr[...], w_ref[...])
        if ng:
            pl.when(pos == total - 1)(finish)

    vec = pl.BlockSpec((1, D), lambda i, k, j: (0, 0))
    return pl.pallas_call(
        body, name=name,
        grid=grid,
        in_specs=[pl.BlockSpec((tm, D), lambda i, k, j: (i, 0)), vec, vec, vec,
                  pl.BlockSpec((None, D, tn), lambda i, k, j: (k, 0, j))] + [ANY] * ng,
        out_specs=[pl.BlockSpec((tm, tn), lambda i, k, j: (i, 2 * k + j)),
                   pl.BlockSpec((tm, D), lambda i, k, j: (i, 0))] + [ANY] * ng,
        out_shape=[jax.ShapeDtypeStruct((L, D_IN), F32), jax.ShapeDtypeStruct((L, D), BF16)]
        + [jax.ShapeDtypeStruct((N_SHARD,) + s.shape, s.dtype) for s in gather],
        scratch_shapes=[pltpu.VMEM((tm, D), BF16)] + (_gather_scratch(ng) if ng else []),
        compiler_params=_params("arbitrary", "arbitrary", "arbitrary"),
    )(x, nw, sh, sc, w4, *gather)


def _hgrn_gates(z, lb):
    sg = _sigmoid(z)
    sgn = _sigmoid(-z)
    f = lb + (1.0 - lb) * sg
    k = (1.0 - lb) * sgn
    return sg, sgn, f, k


def _tri_chunks(n, chunk, reverse):
    r = lax.broadcasted_iota(jnp.int32, (n, n), 0)
    c = lax.broadcasted_iota(jnp.int32, (n, n), 1)
    same = (r // chunk) == (c // chunk)
    return jnp.where(same & ((r <= c) if reverse else (r >= c)), 1.0, 0.0).astype(F32)


def _decay3(b, reverse):
    C = b.shape[0]
    t = lax.broadcasted_iota(jnp.int32, (C, C, 1), 0)
    s = lax.broadcasted_iota(jnp.int32, (C, C, 1), 1)
    mask = (t <= s) if reverse else (t >= s)
    return jnp.exp(jnp.where(mask, b[:, None, :] - b[None, :, :], -jnp.inf))


HG_SUB = 16


def _hgrn_pairs(reverse):
    pairs = []
    size = HG_SUB
    while size < HG_CHUNK:
        for lo in range(0, HG_CHUNK, 2 * size):
            first, second = slice(lo, lo + size), slice(lo + size, lo + 2 * size)
            if reverse:
                pairs.append((first, second, lo + size))
            else:
                pairs.append((second, first, lo + size - 1))
        size *= 2
    return pairs


def _hgrn_intra_fwd(q, k, v, b, reverse):
    blocks = []
    for lo in range(0, HG_CHUNK, HG_SUB):
        r = slice(lo, lo + HG_SUB)
        att3 = jnp.sum(q[r][:, None, :] * k[r][None, :, :] * _decay3(b[r], reverse), axis=-1, keepdims=True)
        blocks.append(jnp.sum(att3 * v[r][None, :, :], axis=1))
    for qr, kr, ref in _hgrn_pairs(reverse):
        beta = b[ref:ref + 1]
        att = _bdot(q[qr] * jnp.exp(b[qr] - beta), k[kr] * jnp.exp(beta - b[kr]), 1, 1)
        part = _bdot(att, v[kr])
        n = part.shape[0] // HG_SUB
        for i in range(n):
            blocks[qr.start // HG_SUB + i] += part[i * HG_SUB:(i + 1) * HG_SUB]
    return jnp.concatenate(blocks, axis=0)


def _hgrn_intra_bwd(q, k, v, b, d_o, reverse):
    nb = HG_CHUNK // HG_SUB
    dq, dk, dv = [None] * nb, [None] * nb, [None] * nb
    for i in range(nb):
        r = slice(i * HG_SUB, (i + 1) * HG_SUB)
        e3 = _decay3(b[r], reverse)
        p3 = jnp.sum(d_o[r][:, None, :] * v[r][None, :, :], axis=-1, keepdims=True) * e3
        dq[i] = jnp.sum(p3 * k[r][None, :, :], axis=1)
        dk[i] = jnp.sum(p3 * q[r][:, None, :], axis=0)
        att3 = jnp.sum(q[r][:, None, :] * k[r][None, :, :] * e3, axis=-1, keepdims=True)
        dv[i] = jnp.sum(att3 * d_o[r][:, None, :], axis=0)

    def add(acc, rows, part):
        for i in range(part.shape[0] // HG_SUB):
            acc[rows.start // HG_SUB + i] += part[i * HG_SUB:(i + 1) * HG_SUB]

    for qr, kr, ref in _hgrn_pairs(reverse):
        beta = b[ref:ref + 1]
        fq, fk = jnp.exp(b[qr] - beta), jnp.exp(beta - b[kr])
        qt, kt = q[qr] * fq, k[kr] * fk
        att = _bdot(qt, kt, 1, 1)
        datt = _bdot(d_o[qr], v[kr], 1, 1)
        add(dq, qr, _bdot(datt, kt) * fq)
        add(dk, kr, _bdot(datt, qt, 0, 0) * fk)
        add(dv, kr, _bdot(att, d_o[qr], 0, 0))
    return jnp.concatenate(dq, axis=0), jnp.concatenate(dk, axis=0), jnp.concatenate(dv, axis=0)


def _hgrn_state_step(k, v, b, s_t, last):
    b_last = b[last:last + 1]
    return s_t * jnp.exp(b_last) + _bdot(v, k * jnp.exp(b_last - b), 0, 0)


def hgrn_scan_fwd(p, lb, s0, col_z, reverse, name):
    L = p.shape[0]
    nB = L // SCAN_ROWS
    nC = SCAN_ROWS // HG_CHUNK
    C = HG_CHUNK
    G, W = HG_GROUP, HG_GROUP * HG_D
    last = 0 if reverse else C - 1

    def bmap(b):
        return (nB - 1 - b) if reverse else b

    def body(q_ref, z_ref, v_ref, lb_ref, s0_ref, o_ref, sfin_ref, sblk_ref, s_scr, k_scr, b_scr):
        blk = pl.program_id(1)

        @pl.when(blk == 0)
        def _():
            s_scr[...] = s0_ref[...]

        sblk_ref[...] = s_scr[...]
        _, _, f_all, k_all = _hgrn_gates(z_ref[...], lb_ref[...])
        k_scr[...] = k_all
        b_scr[...] = _dot(_tri_chunks(SCAN_ROWS, C, reverse), jnp.log(f_all), prec=HI)

        def chunk(ci, carry):
            c = (nC - 1 - ci) if reverse else ci
            rows = pl.ds(pl.multiple_of(c * C, C), C)
            for j in range(G):
                lanes = slice(j * HG_D, (j + 1) * HG_D)
                q = q_ref[rows, lanes] * Q_SCALE
                v = v_ref[rows, lanes]
                k = k_scr[rows, lanes]
                b = b_scr[rows, lanes]
                s_t = s_scr[j]
                o_ref[rows, lanes] = _hgrn_intra_fwd(q, k, v, b, reverse) + _bdot(q * jnp.exp(b), s_t, 1, 1)
                s_scr[j] = _hgrn_state_step(k, v, b, s_t, last)
            return carry

        lax.fori_loop(0, nC, chunk, 0)

        @pl.when(blk == nB - 1)
        def _():
            sfin_ref[...] = s_scr[...]

    def col(c0):
        return pl.BlockSpec((SCAN_ROWS, W), lambda h, b: (bmap(b), c0 // G + h))

    state = pl.BlockSpec((G, HG_D, HG_D), lambda h, b: (h, 0, 0))
    return pl.pallas_call(
        body, name=name,
        grid=(HEADS // G, nB),
        in_specs=[col(COL_HQ), col(col_z), col(COL_HI), pl.BlockSpec((1, W), lambda h, b: (0, h)), state],
        out_specs=[pl.BlockSpec((SCAN_ROWS, W), lambda h, b: (bmap(b), h)), state,
                   pl.BlockSpec((None, G, HG_D, HG_D), lambda h, b: (bmap(b), h, 0, 0))],
        out_shape=[jax.ShapeDtypeStruct((L, D), F32),
                   jax.ShapeDtypeStruct((HEADS, HG_D, HG_D), F32),
                   jax.ShapeDtypeStruct((nB, HEADS, HG_D, HG_D), F32)],
        scratch_shapes=[pltpu.VMEM((G, HG_D, HG_D), F32), pltpu.VMEM((SCAN_ROWS, W), F32),
                        pltpu.VMEM((SCAN_ROWS, W), F32)],
        compiler_params=_params("parallel", "arbitrary"),
    )(p, p, p, lb, s0)


def hgrn_scan_bwd(p, lb, s_blocks, d_o, ds_fin, prev, col_z, reverse, name):
    L = p.shape[0]
    nB = L // SCAN_ROWS
    nC = SCAN_ROWS // HG_CHUNK
    C = HG_CHUNK
    G, W = HG_GROUP, HG_GROUP * HG_D
    last = 0 if reverse else C - 1
    has_prev = prev is not None
    out_dt = BF16 if has_prev else F32

    def bmap(b):
        return b if reverse else (nB - 1 - b)

    def body(*refs):
        q_ref, z_ref, v_ref, lb_ref, sblk_ref, do_ref, dsf_ref = refs[:7]
        refs = refs[7:]
        if has_prev:
            pq_ref, pv_ref = refs[:2]
            refs = refs[2:]
        dq_ref, dz_ref, dv_ref, dlb_ref, ds0_ref, st_scr, run_scr, ds_scr, k_scr, b_scr, db_scr, dk_scr = refs
        blk = pl.program_id(1)

        @pl.when(blk == 0)
        def _():
            ds_scr[...] = dsf_ref[...]
            dlb_ref[...] = jnp.zeros_like(dlb_ref)

        tri = _tri_chunks(SCAN_ROWS, C, reverse)
        row = lax.broadcasted_iota(jnp.int32, (C, HG_D), 0)
        _, _, f_all, k_all = _hgrn_gates(z_ref[...], lb_ref[...])
        k_scr[...] = k_all
        b_scr[...] = _dot(tri, jnp.log(f_all), prec=HI)
        run_scr[...] = sblk_ref[...]

        def recompute(ci, carry):
            c = (nC - 1 - ci) if reverse else ci
            rows = pl.ds(pl.multiple_of(c * C, C), C)
            for j in range(G):
                lanes = slice(j * HG_D, (j + 1) * HG_D)
                s_t = run_scr[j]
                st_scr[c, j] = s_t
                run_scr[j] = _hgrn_state_step(k_scr[rows, lanes], v_ref[rows, lanes], b_scr[rows, lanes], s_t, last)
            return carry

        lax.fori_loop(0, nC, recompute, 0)

        def chunk(ci, carry):
            c = ci if reverse else (nC - 1 - ci)
            rows = pl.ds(pl.multiple_of(c * C, C), C)
            for j in range(G):
                lanes = slice(j * HG_D, (j + 1) * HG_D)
                k = k_scr[rows, lanes]
                b = b_scr[rows, lanes]
                q = q_ref[rows, lanes] * Q_SCALE
                v = v_ref[rows, lanes]
                d_o = do_ref[rows, lanes]
                s_t = st_scr[c, j]
                ds_t = ds_scr[j]
                eb = jnp.exp(b)
                b_last = b[last:last + 1]
                eb_last = jnp.exp(b_last)
                kdec = jnp.exp(b_last - b)
                qe = q * eb
                ke = k * kdec
                dq_in, dk_in, dv_in = _hgrn_intra_bwd(q, k, v, b, d_o, reverse)
                dq_tot = _bdot(d_o, s_t, 1, 0) * eb + dq_in
                dke = _bdot(v, ds_t, 1, 0)
                dk_tot = dke * kdec + dk_in
                dv = dv_in + _bdot(ke, ds_t, 1, 1)
                db_last = _rowsum(dke * ke) + eb_last * _rowsum(ds_t * s_t)
                db_scr[rows, lanes] = q * dq_tot - k * dk_tot + jnp.where(row == last, db_last, 0.0)
                dk_scr[rows, lanes] = dk_tot
                dq = dq_tot * Q_SCALE
                if has_prev:
                    dq = dq + pq_ref[rows, lanes]
                    dv = dv + pv_ref[rows, lanes]
                dq_ref[rows, lanes] = dq.astype(out_dt)
                dv_ref[rows, lanes] = dv.astype(out_dt)
                ds_scr[j] = ds_t * eb_last + _bdot(d_o, qe, 0, 0)
            return carry

        lax.fori_loop(0, nC, chunk, 0)

        lb = lb_ref[...]
        sg, sgn, f, _ = _hgrn_gates(z_ref[...], lb)
        g = _dot(tri, db_scr[...], 0, 0, prec=HI) / f - dk_scr[...]
        dz_ref[...] = (g * (1.0 - lb) * sg * sgn).astype(BF16)
        dlb_ref[...] += _rowsum(g * sgn)

        @pl.when(blk == nB - 1)
        def _():
            ds0_ref[...] = ds_scr[...]

    def col(c0):
        return pl.BlockSpec((SCAN_ROWS, W), lambda h, b: (bmap(b), c0 // G + h))

    tile = pl.BlockSpec((SCAN_ROWS, W), lambda h, b: (bmap(b), h))
    state = pl.BlockSpec((G, HG_D, HG_D), lambda h, b: (h, 0, 0))
    in_specs = [col(COL_HQ), col(col_z), col(COL_HI),
                pl.BlockSpec((1, W), lambda h, b: (0, h)),
                pl.BlockSpec((None, G, HG_D, HG_D), lambda h, b: (bmap(b), h, 0, 0)),
                tile, state]
    args = [p, p, p, lb, s_blocks, d_o, ds_fin]
    if has_prev:
        in_specs += [tile, tile]
        args += list(prev)
    return pl.pallas_call(
        body, name=name,
        grid=(HEADS // G, nB),
        in_specs=in_specs,
        out_specs=[tile, tile, tile, pl.BlockSpec((1, W), lambda h, b: (0, h)), state],
        out_shape=[jax.ShapeDtypeStruct((L, D), out_dt), jax.ShapeDtypeStruct((L, D), BF16),
                   jax.ShapeDtypeStruct((L, D), out_dt), jax.ShapeDtypeStruct((1, D), F32),
                   jax.ShapeDtypeStruct((HEADS, HG_D, HG_D), F32)],
        scratch_shapes=[pltpu.VMEM((nC, G, HG_D, HG_D), F32), pltpu.VMEM((G, HG_D, HG_D), F32),
                        pltpu.VMEM((G, HG_D, HG_D), F32)] + [pltpu.VMEM((SCAN_ROWS, W), F32)] * 4,
        compiler_params=_params("parallel", "arbitrary"),
    )(*args)


def _rope(t, cosf, sinf):
    return t * cosf + pltpu.roll(t, RT_DK // 2, 1) * sinf


def _rope_t(d, cosf, sinf):
    return d * cosf + pltpu.roll(d * sinf, RT_DK // 2, 1)


def _ret_decays(lg, reverse):
    C = SCAN_ROWS
    t = lax.broadcasted_iota(jnp.int32, (C, C), 0)
    s = lax.broadcasted_iota(jnp.int32, (C, C), 1)
    delta = ((s - t) if reverse else (t - s)).astype(F32)
    dmat = jnp.where(delta >= 0, jnp.exp(lg * jnp.maximum(delta, 0.0)), 0.0)
    r = lax.broadcasted_iota(jnp.int32, (C, RT_DK), 0)
    pos = ((C - 1 - r) if reverse else r).astype(F32)
    lg1 = lg[:, :RT_DK]
    qdec = jnp.exp(lg1 * (pos + 1.0))
    kdec = jnp.exp(lg1 * (C - 1.0 - pos))
    sdec = jnp.exp(lg1 * float(C))
    return dmat, delta, pos, qdec, kdec, sdec


def ret_scan_fwd(p, cosf, sinf, lg, s0, reverse, name):
    L = p.shape[0]
    C = SCAN_ROWS
    nB = L // C

    def bmap(b):
        return (nB - 1 - b) if reverse else b

    G = RT_GROUP

    def body(q_ref, k_ref, v_ref, cos_ref, sin_ref, lg_ref, s0_ref, o_ref, sfin_ref, sblk_ref, s_scr):
        blk = pl.program_id(1)

        @pl.when(blk == 0)
        def _():
            s_scr[...] = s0_ref[...]

        sblk_ref[...] = s_scr[...]
        cosf, sinf = cos_ref[...], sin_ref[...]
        for j in range(G):
            lk, lv = slice(j * RT_DK, (j + 1) * RT_DK), slice(j * RT_DV, (j + 1) * RT_DV)
            s_t = s_scr[j]
            dmat, _, _, qdec, kdec, sdec = _ret_decays(lg_ref[j], reverse)
            q = _rope(q_ref[:, lk] * Q_SCALE, cosf, sinf)
            k = _rope(k_ref[:, lk], cosf, sinf)
            v = v_ref[:, lv]
            att = _bdot(q, k, 1, 1) * dmat
            o_ref[:, lv] = _bdot(att, v) + _bdot(q * qdec, s_t, 1, 1)
            s_scr[j] = s_t * sdec + _bdot(v, k * kdec, 0, 0)

        @pl.when(blk == nB - 1)
        def _():
            sfin_ref[...] = s_scr[...]

    def col(c0):
        return pl.BlockSpec((C, G * RT_DK), lambda h, b: (bmap(b), c0 // G + h))

    tab = pl.BlockSpec((C, RT_DK), lambda h, b: (bmap(b), 0))
    state = pl.BlockSpec((G, RT_DV, RT_DK), lambda h, b: (h, 0, 0))
    return pl.pallas_call(
        body, name=name,
        grid=(HEADS // G, nB),
        in_specs=[col(COL_RQ), col(COL_RK),
                  pl.BlockSpec((C, G * RT_DV), lambda h, b: (bmap(b), COL_RV // (2 * G) + h)),
                  tab, tab, pl.BlockSpec((G, 1, RT_DV), lambda h, b: (h, 0, 0)), state],
        out_specs=[pl.BlockSpec((C, G * RT_DV), lambda h, b: (bmap(b), h)), state,
                   pl.BlockSpec((None, G, RT_DV, RT_DK), lambda h, b: (bmap(b), h, 0, 0))],
        out_shape=[jax.ShapeDtypeStruct((L, HEADS * RT_DV), F32),
                   jax.ShapeDtypeStruct((HEADS, RT_DV, RT_DK), F32),
                   jax.ShapeDtypeStruct((nB, HEADS, RT_DV, RT_DK), F32)],
        scratch_shapes=[pltpu.VMEM((G, RT_DV, RT_DK), F32)],
        compiler_params=_params("parallel", "arbitrary"),
    )(p, p, p, cosf, sinf, lg, s0)


def ret_scan_bwd(p, cosf, sinf, lg, s_blocks, d_o, ds_fin, prev, reverse, name):
    L = p.shape[0]
    C = SCAN_ROWS
    nB = L // C
    has_prev = prev is not None
    out_dt = BF16 if has_prev else F32
    G = RT_GROUP

    def bmap(b):
        return b if reverse else (nB - 1 - b)

    def body(*refs):
        q_ref, k_ref, v_ref, cos_ref, sin_ref, lg_ref, sblk_ref, do_ref, dsf_ref = refs[:9]
        refs = refs[9:]
        if has_prev:
            pq_ref, pk_ref, pv_ref = refs[:3]
            refs = refs[3:]
        dq_ref, dk_ref, dv_ref, dlg_ref, ds0_ref, ds_scr = refs
        blk = pl.program_id(1)

        @pl.when(blk == 0)
        def _():
            ds_scr[...] = dsf_ref[...]
            dlg_ref[...] = jnp.zeros_like(dlg_ref)

        cosf, sinf = cos_ref[...], sin_ref[...]
        for j in range(G):
            lk, lv = slice(j * RT_DK, (j + 1) * RT_DK), slice(j * RT_DV, (j + 1) * RT_DV)
            s_t = sblk_ref[j]
            ds_t = ds_scr[j]
            dmat, delta, pos, qdec, kdec, sdec = _ret_decays(lg_ref[j], reverse)
            q = _rope(q_ref[:, lk] * Q_SCALE, cosf, sinf)
            k = _rope(k_ref[:, lk], cosf, sinf)
            v = v_ref[:, lv]
            d_o = do_ref[:, lv]
            att_raw = _bdot(q, k, 1, 1)
            datt_m = _bdot(d_o, v, 1, 1) * dmat
            dqd = _bdot(d_o, s_t, 1, 0)
            dkd = _bdot(v, ds_t, 1, 0)
            dq = _bdot(datt_m, k) + dqd * qdec
            dk = _bdot(datt_m, q, 0, 0) + dkd * kdec
            dv = _bdot(att_raw * dmat, d_o, 0, 0) + _bdot(k * kdec, ds_t, 1, 1)
            ds_scr[j] = ds_t * sdec + _bdot(d_o, q * qdec, 0, 0)
            t1 = jnp.sum(_rowsum(datt_m * att_raw * delta), axis=-1, keepdims=True)
            t23 = jnp.sum(_rowsum((pos + 1.0) * qdec * q * dqd + (C - 1.0 - pos) * kdec * k * dkd), axis=-1, keepdims=True)
            t4 = jnp.sum(_rowsum(ds_t * s_t * sdec), axis=-1, keepdims=True) * float(C)
            dlg_ref[j] += jnp.broadcast_to(t1 + t23 + t4, (1, RT_DK))
            if has_prev:
                dq = _rope_t(dq + pq_ref[:, lk], cosf, sinf) * Q_SCALE
                dk = _rope_t(dk + pk_ref[:, lk], cosf, sinf)
                dv = dv + pv_ref[:, lv]
            dq_ref[:, lk] = dq.astype(out_dt)
            dk_ref[:, lk] = dk.astype(out_dt)
            dv_ref[:, lv] = dv.astype(out_dt)

        @pl.when(blk == nB - 1)
        def _():
            ds0_ref[...] = ds_scr[...]

    def col(c0):
        return pl.BlockSpec((C, G * RT_DK), lambda h, b: (bmap(b), c0 // G + h))

    tab = pl.BlockSpec((C, RT_DK), lambda h, b: (bmap(b), 0))
    state = pl.BlockSpec((G, RT_DV, RT_DK), lambda h, b: (h, 0, 0))
    tk = pl.BlockSpec((C, G * RT_DK), lambda h, b: (bmap(b), h))
    tv = pl.BlockSpec((C, G * RT_DV), lambda h, b: (bmap(b), h))
    in_specs = [col(COL_RQ), col(COL_RK),
                pl.BlockSpec((C, G * RT_DV), lambda h, b: (bmap(b), COL_RV // (2 * G) + h)),
                tab, tab, pl.BlockSpec((G, 1, RT_DV), lambda h, b: (h, 0, 0)),
                pl.BlockSpec((None, G, RT_DV, RT_DK), lambda h, b: (bmap(b), h, 0, 0)),
                tv, state]
    args = [p, p, p, cosf, sinf, lg, s_blocks, d_o, ds_fin]
    if has_prev:
        in_specs += [tk, tk, tv]
        args += list(prev)
    return pl.pallas_call(
        body, name=name,
        grid=(HEADS // G, nB),
        in_specs=in_specs,
        out_specs=[tk, tk, tv, pl.BlockSpec((G, 1, RT_DK), lambda h, b: (h, 0, 0)), state],
        out_shape=[jax.ShapeDtypeStruct((L, D), out_dt), jax.ShapeDtypeStruct((L, D), out_dt),
                   jax.ShapeDtypeStruct((L, HEADS * RT_DV), out_dt),
                   jax.ShapeDtypeStruct((HEADS, 1, RT_DK), F32),
                   jax.ShapeDtypeStruct((HEADS, RT_DV, RT_DK), F32)],
        scratch_shapes=[pltpu.VMEM((G, RT_DV, RT_DK), F32)],
        compiler_params=_params("parallel", "arbitrary"),
    )(*args)


def _silu_parts(h):
    s = _sigmoid(h)
    return h * s, s * (1.0 + h * (1.0 - s))


def _head_rms(o):
    outs, rs = [], []
    for h in range(HEADS):
        oh = o[:, h * HG_D:(h + 1) * HG_D]
        r = lax.rsqrt(_lanemean(oh * oh) + EPS)
        outs.append(oh * r)
        rs.append(r)
    return outs, rs


def _group_norm(o):
    outs, rs = [], []
    for h in range(HEADS):
        oh = o[:, h * RT_DV:(h + 1) * RT_DV]
        c = oh - _lanemean(oh)
        r = lax.rsqrt(_lanemean(c * c) + GN_EPS)
        outs.append(c * r)
        rs.append(r)
    return outs, rs


MIX_ROWS = 256
MIX_BWD_ROWS = 128


def _mix_specs(rows):
    def t(w, c=0):
        return pl.BlockSpec((rows, w), lambda i: (i, c))

    return t


def mix_fwd(ohf, ohb, orf, orb, p, x, g1, hgw, w_pa, w_pb, w_out, name):
    L = x.shape[0]
    t = _mix_specs(MIX_ROWS)

    def body(ohf_ref, ohb_ref, orf_ref, orb_ref, hg_ref, rg0_ref, rg1_ref, ga_ref, gb_ref, x_ref, g1_ref, hgw_ref,
             wpa_ref, wpb_ref, wout_ref, x1_ref, xmix_ref, merged_ref, ya_ref, yb_ref):
        nh, _ = _head_rms(ohf_ref[...] + ohb_ref[...])
        ya = jnp.concatenate(nh, axis=1) * hgw_ref[...] * _silu_parts(hg_ref[...])[0]
        gn, _ = _group_norm(orf_ref[...] + orb_ref[...])
        rg = jnp.concatenate([rg0_ref[...], rg1_ref[...]], axis=1)
        yb = jnp.concatenate(gn, axis=1) * _silu_parts(rg)[0]
        ya16, yb16 = ya.astype(BF16), yb.astype(BF16)
        merged = (_sigmoid(ga_ref[...]) * _dot(ya16, wpa_ref[...])
                  + _sigmoid(gb_ref[...]) * _dot(yb16, wpb_ref[...])).astype(BF16)
        x_mix = _dot(merged, wout_ref[...])
        x1_ref[...] = x_ref[...] + g1_ref[...] * x_mix
        xmix_ref[...] = x_mix
        merged_ref[...] = merged
        ya_ref[...] = ya16
        yb_ref[...] = yb16

    vec = pl.BlockSpec((1, D), lambda i: (0, 0))

    def full(a):
        return pl.BlockSpec(a.shape, lambda i: (0, 0), pipeline_mode=pl.Buffered(1))

    return pl.pallas_call(
        body, name=name,
        grid=(L // MIX_ROWS,),
        in_specs=[t(D), t(D), t(2 * D), t(2 * D), t(D, COL_HG // 8), t(D, COL_RG // 8), t(D, COL_RG // 8 + 1),
                  t(D, COL_GA // 8), t(D, COL_GB // 8), t(D), vec, vec, full(w_pa), full(w_pb), full(w_out)],
        out_specs=[t(D), t(D), t(D), t(D), t(2 * D)],
        out_shape=[jax.ShapeDtypeStruct((L, D), F32), jax.ShapeDtypeStruct((L, D), F32),
                   jax.ShapeDtypeStruct((L, D), BF16), jax.ShapeDtypeStruct((L, D), BF16),
                   jax.ShapeDtypeStruct((L, 2 * D), BF16)],
        compiler_params=_params("parallel"),
    )(ohf, ohb, orf, orb, p, p, p, p, p, x, g1, hgw, w_pa, w_pb, w_out)


def mix_bwd(dx1, x_mix, ya, yb, ohf, ohb, orf, orb, p, g1, hgw, w_pa, w_pb, w_out, name):
    L = dx1.shape[0]
    t = _mix_specs(MIX_BWD_ROWS)

    def body(dx1_ref, xmix_ref, ya_ref, yb_ref, ohf_ref, ohb_ref, orf_ref, orb_ref, hg_ref, rg0_ref, rg1_ref,
             ga_ref, gb_ref, g1_ref, hgw_ref, wpa_ref, wpb_ref, wout_ref,
             dxm_ref, da_ref, db_ref, dga_ref, dgb_ref, dhg_ref, drg_ref, dohg_ref, dort_ref, sums_ref):
        @pl.when(pl.program_id(0) == 0)
        def _():
            sums_ref[...] = jnp.zeros_like(sums_ref)

        dx1 = dx1_ref[...]
        dxm = (g1_ref[...] * dx1).astype(BF16)
        dxm_ref[...] = dxm
        dmerged = _dot(dxm, wout_ref[...], 1, 1)
        a = _dot(ya_ref[...], wpa_ref[...])
        bm = _dot(yb_ref[...], wpb_ref[...])
        sa, sb = _sigmoid(ga_ref[...]), _sigmoid(gb_ref[...])
        d_a = (dmerged * sa).astype(BF16)
        d_b = (dmerged * sb).astype(BF16)
        da_ref[...] = d_a
        db_ref[...] = d_b
        dga_ref[...] = (dmerged * a * sa * (1.0 - sa)).astype(BF16)
        dgb_ref[...] = (dmerged * bm * sb * (1.0 - sb)).astype(BF16)
        dya = _dot(d_a, wpa_ref[...], 1, 1)
        dyb = _dot(d_b, wpb_ref[...], 1, 1)

        hgw = hgw_ref[...]
        silu_h, dsilu_h = _silu_parts(hg_ref[...])
        nh, rh = _head_rms(ohf_ref[...] + ohb_ref[...])
        n = jnp.concatenate(nh, axis=1)
        dhg_ref[...] = (dya * n * hgw * dsilu_h).astype(BF16)
        dn = dya * hgw * silu_h
        douts = []
        for h in range(HEADS):
            dnh = dn[:, h * HG_D:(h + 1) * HG_D]
            douts.append(rh[h] * (dnh - nh[h] * _lanemean(dnh * nh[h])))
        dohg_ref[...] = jnp.concatenate(douts, axis=1)

        rg = jnp.concatenate([rg0_ref[...], rg1_ref[...]], axis=1)
        silu_r, dsilu_r = _silu_parts(rg)
        gn, rr = _group_norm(orf_ref[...] + orb_ref[...])
        g = jnp.concatenate(gn, axis=1)
        drg_ref[...] = (dyb * g * dsilu_r).astype(BF16)
        dgn = dyb * silu_r
        douts = []
        for h in range(HEADS):
            dgh = dgn[:, h * RT_DV:(h + 1) * RT_DV]
            douts.append(rr[h] * (dgh - _lanemean(dgh) - gn[h] * _lanemean(dgh * gn[h])))
        dort_ref[...] = jnp.concatenate(douts, axis=1)

        sums_ref[0:1, :] += _rowsum(dx1 * xmix_ref[...])
        sums_ref[1:2, :] += _rowsum(dya * n * silu_h)

    vec = pl.BlockSpec((1, D), lambda i: (0, 0))

    def full(a):
        return pl.BlockSpec(a.shape, lambda i: (0, 0), pipeline_mode=pl.Buffered(1))

    bf = functools.partial(jax.ShapeDtypeStruct, dtype=BF16)
    return pl.pallas_call(
        body, name=name,
        grid=(L // MIX_BWD_ROWS,),
        in_specs=[t(D), t(D), t(D), t(2 * D), t(D), t(D), t(2 * D), t(2 * D),
                  t(D, COL_HG // 8), t(D, COL_RG // 8), t(D, COL_RG // 8 + 1), t(D, COL_GA // 8), t(D, COL_GB // 8),
                  vec, vec, full(w_pa), full(w_pb), full(w_out)],
        out_specs=[t(D), t(D), t(D), t(D), t(D), t(D), t(2 * D), t(D), t(2 * D),
                   pl.BlockSpec((8, D), lambda i: (0, 0))],
        out_shape=[bf((L, D)), bf((L, D)), bf((L, D)), bf((L, D)), bf((L, D)), bf((L, D)), bf((L, 2 * D)),
                   jax.ShapeDtypeStruct((L, D), F32), jax.ShapeDtypeStruct((L, 2 * D), F32),
                   jax.ShapeDtypeStruct((8, D), F32)],
        compiler_params=_params("arbitrary"),
    )(dx1, x_mix, ya, yb, ohf, ohb, orf, orb, p, p, p, p, p, g1, hgw, w_pa, w_pb, w_out)


FFN_ROWS = 512


def ffn_fwd(x1, target, nw2, sh2, sc2, g2, fw, wg, wu, wd, name):
    L = x1.shape[0]
    tm = min(FFN_ROWS, L)

    def body(x1_ref, tgt_ref, nw2_ref, sh2_ref, sc2_ref, g2_ref, fw_ref, wg_ref, wu_ref, wd_ref,
             hx2_ref, g_ref, u_ref, h_ref, f_ref, dx2_ref, sums_ref, hx_scr, acc):
        i, j = pl.program_id(0), pl.program_id(1)

        @pl.when((i == 0) & (j == 0))
        def _():
            sums_ref[...] = jnp.zeros_like(sums_ref)

        @pl.when(j == 0)
        def _():
            xv = x1_ref[...]
            n = xv * lax.rsqrt(_lanemean(xv * xv) + EPS) * nw2_ref[...]
            h = (n * (1.0 + sc2_ref[...]) + sh2_ref[...]).astype(BF16)
            hx_scr[...] = h
            hx2_ref[...] = h
            acc[...] = jnp.zeros_like(acc)

        hx = hx_scr[...]
        g = _dot(hx, wg_ref[...])
        u = _dot(hx, wu_ref[...])
        hh = (_silu_parts(g)[0] * u).astype(BF16)
        g_ref[...] = g
        u_ref[...] = u
        h_ref[...] = hh
        acc[...] += _dot(hh, wd_ref[...])

        @pl.when(j == N_SHARD - 1)
        def _():
            f = acc[...]
            f_ref[...] = f
            x2 = x1_ref[...] + g2_ref[...] * f
            r = lax.rsqrt(_lanemean(x2 * x2) + EPS)
            fw = fw_ref[...]
            e = x2 * r * fw - tgt_ref[...]
            dy = e * (1.0 / D)
            dyw = dy * fw
            dx2_ref[...] = r * dyw - x2 * (r * r * r) * _lanemean(dyw * x2)
            sums_ref[0:1, :] += _rowsum(dy * x2 * r)
            sums_ref[1:2, :] += _rowsum(e * e) * (0.5 / D)

    row = pl.BlockSpec((tm, D), lambda i, j: (i, 0))
    vec = pl.BlockSpec((1, D), lambda i, j: (0, 0))
    sh = pl.BlockSpec((None, tm, FF_SH), lambda i, j: (j, i, 0))
    return pl.pallas_call(
        body, name=name,
        grid=(L // tm, N_SHARD),
        in_specs=[row, row, vec, vec, vec, vec, vec,
                  pl.BlockSpec((None, D, FF_SH), lambda i, j: (j, 0, 0)),
                  pl.BlockSpec((None, D, FF_SH), lambda i, j: (j, 0, 0)),
                  pl.BlockSpec((None, FF_SH, D), lambda i, j: (j, 0, 0))],
        out_specs=[row, sh, sh, sh, row, row, pl.BlockSpec((8, D), lambda i, j: (0, 0))],
        out_shape=[jax.ShapeDtypeStruct((L, D), BF16),
                   jax.ShapeDtypeStruct((N_SHARD, L, FF_SH), F32), jax.ShapeDtypeStruct((N_SHARD, L, FF_SH), F32),
                   jax.ShapeDtypeStruct((N_SHARD, L, FF_SH), BF16),
                   jax.ShapeDtypeStruct((L, D), F32), jax.ShapeDtypeStruct((L, D), F32),
                   jax.ShapeDtypeStruct((8, D), F32)],
        scratch_shapes=[pltpu.VMEM((tm, D), BF16), pltpu.VMEM((tm, D), F32)],
        compiler_params=_params("arbitrary", "arbitrary"),
    )(x1, target, nw2, sh2, sc2, g2, fw, wg, wu, wd)


def ffn_bwd(dx2, x1, f, g, u, nw2, sc2, g2, wg, wu, wd, name):
    L = x1.shape[0]
    tm = min(FFN_ROWS, L)

    def body(dx2_ref, x1_ref, f_ref, g_ref, u_ref, nw2_ref, sc2_ref, g2_ref, wg_ref, wu_ref, wd_ref,
             df_ref, dg_ref, du_ref, dx1_ref, sums_ref, df_scr, acc):
        i, j = pl.program_id(0), pl.program_id(1)

        @pl.when((i == 0) & (j == 0))
        def _():
            sums_ref[...] = jnp.zeros_like(sums_ref)

        @pl.when(j == 0)
        def _():
            dx2 = dx2_ref[...]
            df = (g2_ref[...] * dx2).astype(BF16)
            df_scr[...] = df
            df_ref[...] = df
            sums_ref[0:1, :] += _rowsum(dx2 * f_ref[...])
            acc[...] = jnp.zeros_like(acc)

        dh = _dot(df_scr[...], wd_ref[...], 1, 1)
        gv, uv = g_ref[...], u_ref[...]
        silu_g, dsilu_g = _silu_parts(gv)
        dg = (dh * uv * dsilu_g).astype(BF16)
        du = (dh * silu_g).astype(BF16)
        dg_ref[...] = dg
        du_ref[...] = du
        acc[...] += _dot(dg, wg_ref[...], 1, 1) + _dot(du, wu_ref[...], 1, 1)

        @pl.when(j == N_SHARD - 1)
        def _():
            dhx = acc[...]
            xv = x1_ref[...]
            r = lax.rsqrt(_lanemean(xv * xv) + EPS)
            n0 = xv * r
            nw = nw2_ref[...]
            dn2 = dhx * (1.0 + sc2_ref[...])
            dn0 = dn2 * nw
            dx1_ref[...] = dx2_ref[...] + r * (dn0 - n0 * _lanemean(dn0 * n0))
            sums_ref[1:2, :] += _rowsum(dhx)
            sums_ref[2:3, :] += _rowsum(dhx * n0 * nw)
            sums_ref[3:4, :] += _rowsum(dn2 * n0)

    row = pl.BlockSpec((tm, D), lambda i, j: (i, 0))
    vec = pl.BlockSpec((1, D), lambda i, j: (0, 0))
    sh = pl.BlockSpec((None, tm, FF_SH), lambda i, j: (j, i, 0))
    return pl.pallas_call(
        body, name=name,
        grid=(L // tm, N_SHARD),
        in_specs=[row, row, row, sh, sh, vec, vec, vec,
                  pl.BlockSpec((None, D, FF_SH), lambda i, j: (j, 0, 0)),
                  pl.BlockSpec((None, D, FF_SH), lambda i, j: (j, 0, 0)),
                  pl.BlockSpec((None, FF_SH, D), lambda i, j: (j, 0, 0))],
        out_specs=[row, sh, sh, row, pl.BlockSpec((8, D), lambda i, j: (0, 0))],
        out_shape=[jax.ShapeDtypeStruct((L, D), BF16),
                   jax.ShapeDtypeStruct((N_SHARD, L, FF_SH), BF16), jax.ShapeDtypeStruct((N_SHARD, L, FF_SH), BF16),
                   jax.ShapeDtypeStruct((L, D), F32), jax.ShapeDtypeStruct((8, D), F32)],
        scratch_shapes=[pltpu.VMEM((tm, D), BF16), pltpu.VMEM((tm, D), F32)],
        compiler_params=_params("arbitrary", "arbitrary"),
    )(dx2, x1, f, g, u, nw2, sc2, g2, wg, wu, wd)


def matmul_tn(a, b, name, acc_init=None, to_chips=()):
    na, K, M = a.shape
    nb, _, N = b.shape
    n = max(na, nb)
    tk = min(512, K)
    tn = N if N <= 1024 else N // 2
    nk = K // tk
    grid = (n, N // tn, nk)
    has_init = acc_init is not None
    nx = len(to_chips)

    def body(a_ref, b_ref, *refs):
        init_ref = refs[0] if has_init else None
        refs = refs[1:] if has_init else refs
        o_ref = refs[nx]
        if nx:
            start, finish = _to_chips_phases(refs[:nx], refs[nx + 1:2 * nx + 1], *refs[2 * nx + 1:])
            pos, total = _grid_step(grid)
            pl.when(pos == 0)(start)
        kk = pl.program_id(2)

        @pl.when(kk == 0)
        def _():
            o_ref[...] = init_ref[...] if has_init else jnp.zeros_like(o_ref)

        o_ref[...] += _dot(a_ref[...], b_ref[...], 0, 0)
        if nx:
            pl.when(pos == total - 1)(finish)

    out_spec = pl.BlockSpec((None, M, tn), lambda s, j, kk: (s, 0, j))
    in_specs = [pl.BlockSpec((None, tk, M), lambda s, j, kk: (s if na > 1 else 0, kk, 0)),
                pl.BlockSpec((None, tk, tn), lambda s, j, kk: (s if nb > 1 else 0, kk, j))]
    args = [a, b]
    if has_init:
        in_specs.append(out_spec)
        args.append(acc_init)
    out = pl.pallas_call(
        body, name=name,
        grid=grid,
        in_specs=in_specs + [ANY] * nx,
        out_specs=[out_spec] + [ANY] * nx,
        out_shape=[jax.ShapeDtypeStruct((n, M, N), F32)] + _to_chips_shapes(to_chips),
        scratch_shapes=_to_chips_scratch(nx) if nx else [],
        compiler_params=_params(*(("arbitrary",) * 3 if nx else ("parallel", "parallel", "arbitrary"))),
    )(*args, *to_chips)
    return out if nx else out[0]


def matmul_tn_pair(a, b1, b2, name):
    K, M = a.shape
    n, _, N = b1.shape
    tk = min(512, K)

    def body(a_ref, b1_ref, b2_ref, o1_ref, o2_ref):
        @pl.when(pl.program_id(1) == 0)
        def _():
            o1_ref[...] = jnp.zeros_like(o1_ref)
            o2_ref[...] = jnp.zeros_like(o2_ref)

        at = a_ref[...].T
        o1_ref[...] += _dot(at, b1_ref[...])
        o2_ref[...] += _dot(at, b2_ref[...])

    b_spec = pl.BlockSpec((None, tk, N), lambda s, kk: (s, kk, 0))
    o_spec = pl.BlockSpec((None, M, N), lambda s, kk: (s, 0, 0))
    return pl.pallas_call(
        body, name=name,
        grid=(n, K // tk),
        in_specs=[pl.BlockSpec((tk, M), lambda s, kk: (kk, 0)), b_spec, b_spec],
        out_specs=[o_spec, o_spec],
        out_shape=[jax.ShapeDtypeStruct((n, M, N), F32)] * 2,
        compiler_params=_params("parallel", "arbitrary"),
    )(a, b1, b2)


def dhx_normbwd(dp4, w4, x, dx_res, nw, sc, name, to_chips=()):
    L = x.shape[0]
    tm = min(PROJ_ROWS, L)
    tn = IN_SH // 2
    grid = (L // tm, N_SHARD, 2)
    nx = len(to_chips)

    def body(dp_ref, w_ref, x_ref, res_ref, nw_ref, sc_ref, *refs):
        dx_ref, sums_ref = refs[nx:nx + 2]
        acc = refs[2 * nx + 2]
        if nx:
            start, finish = _to_chips_phases(refs[:nx], refs[nx + 2:2 * nx + 2], *refs[2 * nx + 3:])
            pos, total = _grid_step(grid)
            pl.when(pos == 0)(start)
            pl.when(pos == total - 1)(finish)
        i, k, j = pl.program_id(0), pl.program_id(1), pl.program_id(2)
        first = (k == 0) & (j == 0)

        @pl.when((i == 0) & first)
        def _():
            sums_ref[...] = jnp.zeros_like(sums_ref)

        @pl.when(first)
        def _():
            acc[...] = jnp.zeros_like(acc)

        acc[...] += _dot(dp_ref[...], w_ref[...], 1, 1)

        @pl.when((k == N_SHARD - 1) & (j == 1))
        def _():
            dhx = acc[...]
            xv = x_ref[...]
            r = lax.rsqrt(_lanemean(xv * xv) + EPS)
            n0 = xv * r
            nw = nw_ref[...]
            dn = dhx * (1.0 + sc_ref[...])
            dn0 = dn * nw
            dx_ref[...] = res_ref[...] + r * (dn0 - n0 * _lanemean(dn0 * n0))
            sums_ref[0:1, :] += _rowsum(dhx)
            sums_ref[1:2, :] += _rowsum(dhx * n0 * nw)
            sums_ref[2:3, :] += _rowsum(dn * n0)

    row = pl.BlockSpec((tm, D), lambda i, k, j: (i, 0))
    vec = pl.BlockSpec((1, D), lambda i, k, j: (0, 0))
    return pl.pallas_call(
        body, name=name,
        grid=grid,
        in_specs=[pl.BlockSpec((None, tm, tn), lambda i, k, j: (k, i, j)),
                  pl.BlockSpec((None, D, tn), lambda i, k, j: (k, 0, j)),
                  row, row, vec, vec] + [ANY] * nx,
        out_specs=[row, pl.BlockSpec((8, D), lambda i, k, j: (0, 0))] + [ANY] * nx,
        out_shape=[jax.ShapeDtypeStruct((L, D), F32), jax.ShapeDtypeStruct((8, D), F32)] + _to_chips_shapes(to_chips),
        scratch_shapes=[pltpu.VMEM((tm, D), F32)] + (_to_chips_scratch(nx) if nx else []),
        compiler_params=_params("arbitrary", "arbitrary", "arbitrary"),
    )(dp4, w4, x, dx_res, nw, sc, *to_chips)


SMALL_ROWS = 24


def _rope_tables(L):
    rows = L // 64
    row = jnp.repeat(jnp.arange(rows, dtype=F32), 64)
    col = jnp.tile(jnp.arange(64, dtype=F32), rows)
    freqs = 10000.0 ** (-jnp.arange(RT_DK // 4, dtype=F32) / (RT_DK // 4))
    ang = jnp.concatenate([row[:, None] * freqs, col[:, None] * freqs], axis=-1)
    cos, sin = jnp.cos(ang), jnp.sin(ang)
    return jnp.concatenate([cos, cos], axis=1), jnp.concatenate([-sin, sin], axis=1)


def _shard_major(pieces):
    dp = jnp.concatenate(pieces, axis=1)
    return dp.reshape(dp.shape[0], N_SHARD, IN_SH).transpose(1, 0, 2)


def _lane0(a):
    return a[:, 0, 0]


def _pack_small(rows):
    out = [r.reshape(1, D) for r in rows]
    out += [jnp.zeros((1, D), F32)] * (SMALL_ROWS - len(out))
    return jnp.concatenate(out, axis=0)


def _sibling_sums(gs, names, place):
    core, core_arg, _ = place
    payload = [lax.dynamic_slice_in_dim(g, (1 - core) * (g.shape[1] // 2), g.shape[1] // 2, axis=1).astype(BF16)
               for g in gs]
    received = rs_to_sibling(payload, "rs_to_sibling_" + names[0])
    return [rs_add_sibling(g, r, core_arg, "rs_add_sibling_" + k) for g, r, k in zip(gs, received, names)]


def local_step(x, ctx, target, mod_x, mod_c, lb_f, lb_b, lg_f, lg_b, nw1, nw2, hgw, fw, w, rest=None, place=None):
    L, Lc = x.shape[0], ctx.shape[0]
    sh1, sc1, g1, sh2, sc2, g2 = (mod_x[i:i + 1] for i in range(6))
    sh1c, sc1c = mod_c[0:1], mod_c[1:2]
    cosf, sinf = _rope_tables(L)
    cosc, sinc = jnp.ones((Lc, RT_DK), F32), jnp.zeros((Lc, RT_DK), F32)
    zero_h = jnp.zeros((HEADS, HG_D, HG_D), F32)
    zero_r = jnp.zeros((HEADS, RT_DV, RT_DK), F32)

    pc, hxc = normmod_matmul(ctx, nw1, sh1c, sc1c, w["w_in"], "ctx_in_proj")
    _, s_hf, cb_hf = hgrn_scan_fwd(pc, lb_f, zero_h, COL_HFF, False, "ctx_hgrn_f")
    _, s_hb, cb_hb = hgrn_scan_fwd(pc, lb_b, zero_h, COL_HFB, True, "ctx_hgrn_b")
    _, s_rf, cb_rf = ret_scan_fwd(pc, cosc, sinc, lg_f, zero_r, False, "ctx_ret_f")
    _, s_rb, cb_rb = ret_scan_fwd(pc, cosc, sinc, lg_b, zero_r, True, "ctx_ret_b")
    if rest is None:
        p, hx = normmod_matmul(x, nw1, sh1, sc1, w["w_in"], "in_proj")
    else:
        p, hx, g_pa, g_pb, g_out, g_wg, g_wu, g_wd = normmod_matmul(x, nw1, sh1, sc1, w["w_in"], "in_proj", gather=rest)
        w = dict(w, w_pa=g_pa.reshape(D, D), w_pb=g_pb.reshape(2 * D, D), w_out=g_out.reshape(D, D),
                 wg=g_wg, wu=g_wu, wd=g_wd)
    ohf, _, xb_hf = hgrn_scan_fwd(p, lb_f, s_hf, COL_HFF, False, "hgrn_f")
    ohb, _, xb_hb = hgrn_scan_fwd(p, lb_b, s_hb, COL_HFB, True, "hgrn_b")
    orf, _, xb_rf = ret_scan_fwd(p, cosf, sinf, lg_f, s_rf, False, "ret_f")
    orb, _, xb_rb = ret_scan_fwd(p, cosf, sinf, lg_b, s_rb, True, "ret_b")
    x1, x_mix, merged, ya, yb = mix_fwd(ohf, ohb, orf, orb, p, x, g1, hgw, w["w_pa"], w["w_pb"], w["w_out"], "mix_fwd")
    hx2, gg, uu, hh, ff, dx2, sums_f = ffn_fwd(x1, target, nw2, sh2, sc2, g2, fw, w["wg"], w["wu"], w["wd"], "ffn_fwd")

    d_f, d_g, d_u, dx1, sums_fb = ffn_bwd(dx2, x1, ff, gg, uu, nw2, sc2, g2, w["wg"], w["wu"], w["wd"], "ffn_bwd")
    dw_gate, dw_up = matmul_tn_pair(hx2, d_g, d_u, "dw_ffn_gate_up")
    grads = {"wg": dw_gate, "wu": dw_up, "wd": matmul_tn(hh, d_f[None], "dw_ffn_down")}
    dxm, d_a, d_b, dga, dgb, dhg, drg, dohg, dort, sums_m = mix_bwd(
        dx1, x_mix, ya, yb, ohf, ohb, orf, orb, p, g1, hgw, w["w_pa"], w["w_pb"], w["w_out"], "mix_bwd")
    grads["w_out"] = matmul_tn(merged[None], dxm[None], "dw_out").reshape(N_SHARD, D // N_SHARD, D)
    grads["w_pa"] = matmul_tn(ya[None], d_a[None], "dw_proj_hgrn").reshape(N_SHARD, D // N_SHARD, D)
    grads["w_pb"] = matmul_tn(yb[None], d_b[None], "dw_proj_ret").reshape(N_SHARD, 2 * D // N_SHARD, D)

    rq1, rk1, rv1, dlgf_x, ds_rf = ret_scan_bwd(p, cosf, sinf, lg_f, xb_rf, dort, zero_r, None, False, "ret_f_bwd")
    drq, drk, drv, dlgb_x, ds_rb = ret_scan_bwd(p, cosf, sinf, lg_b, xb_rb, dort, zero_r, (rq1, rk1, rv1), True, "ret_b_bwd")
    hq1, dzf, hv1, dlbf_x, ds_hf = hgrn_scan_bwd(p, lb_f, xb_hf, dohg, zero_h, None, COL_HFF, False, "hgrn_f_bwd")
    dhq, dzb, dhv, dlbb_x, ds_hb = hgrn_scan_bwd(p, lb_b, xb_hb, dohg, zero_h, (hq1, hv1), COL_HFB, True, "hgrn_b_bwd")
    dp4 = _shard_major([dhq, dzf, dzb, dhv, dhg, drq, drk, drv, drg, dga, dgb])
    others = ["w_pa", "w_pb", "w_out", "wg", "wu", "wd"]
    if place is None:
        dw_in = matmul_tn(hx[None], dp4, "dw_in")
    else:
        sums_o = _sibling_sums([grads[k] for k in others], others, place)
        dw_in, *recv_o = matmul_tn(hx[None], dp4, "dw_in", to_chips=[a16 for _, a16 in sums_o])

    zc = jnp.zeros((Lc, D), F32)
    zc2 = jnp.zeros((Lc, 2 * D), F32)
    crq1, crk1, crv1, dlgf_c, _ = ret_scan_bwd(pc, cosc, sinc, lg_f, cb_rf, zc2, ds_rf, None, False, "ctx_ret_f_bwd")
    cdrq, cdrk, cdrv, dlgb_c, _ = ret_scan_bwd(pc, cosc, sinc, lg_b, cb_rb, zc2, ds_rb, (crq1, crk1, crv1), True, "ctx_ret_b_bwd")
    chq1, cdzf, chv1, dlbf_c, _ = hgrn_scan_bwd(pc, lb_f, cb_hf, zc, ds_hf, None, COL_HFF, False, "ctx_hgrn_f_bwd")
    cdhq, cdzb, cdhv, dlbb_c, _ = hgrn_scan_bwd(pc, lb_b, cb_hb, zc, ds_hb, (chq1, chv1), COL_HFB, True, "ctx_hgrn_b_bwd")
    zb = jnp.zeros((Lc, D), BF16)
    zb2 = jnp.zeros((Lc, 2 * D), BF16)
    dpc4 = _shard_major([cdhq, cdzf, cdzb, cdhv, zb, cdrq, cdrk, cdrv, zb2, zb, zb])
    _, sums_c = dhx_normbwd(dpc4, w["w_in"], ctx, zc, nw1, sc1c, "dctx_in_proj")
    grads["w_in"] = matmul_tn(hxc[None], dpc4, "dw_in_ctx", acc_init=dw_in)
    if place is None:
        dx, sums_x = dhx_normbwd(dp4, w["w_in"], x, dx1, nw1, sc1, "dx_in_proj")
    else:
        sums_i = _sibling_sums([grads["w_in"]], ["w_in"], place)
        dx, sums_x, recv_i = dhx_normbwd(dp4, w["w_in"], x, dx1, nw1, sc1, "dx_in_proj", to_chips=[sums_i[0][1]])
        names = ["w_in"] + others
        halves = [rs_add_chips(a, r, place[2], "rs_add_chips_" + k)
                  for (a, _), r, k in zip(sums_i + sums_o, [recv_i] + recv_o, names)]
        grads = dict(zip(names, rs_join_halves(halves, "rs_join_halves")))

    def lg_row(f, b):
        return jnp.concatenate([_lane0(f), _lane0(b), jnp.zeros((D - 2 * HEADS,), F32)])

    small = _pack_small([
        sums_x[0], sums_x[1], sums_m[0], sums_fb[1], sums_fb[2], sums_fb[0],
        sums_c[0], sums_c[1],
        sums_x[2], sums_c[2], sums_fb[3], sums_m[1], sums_f[0],
        dlbf_x, dlbf_c, dlbb_x, dlbb_c,
        lg_row(dlgf_x, dlgb_x), lg_row(dlgf_c, dlgb_c),
        sums_f[1],
    ])
    return dx, grads, small


MESH = pl.DeviceIdType.MESH
ANY = pl.BlockSpec(memory_space=pl.ANY)
N_DEV = 8


def _place():
    return lax.axis_index("x"), lax.axis_index("y"), lax.axis_index("c")


def _other_chips(x, y):
    return [(1 - x, y), (x, 1 - y), (1 - x, 1 - y)]


def allgather8(xs, name):
    m, n = xs.shape

    def body(x_ref, out_ref, send_sems, recv_sems, local_sem):
        x, y, c = _place()
        me, sibling = (x, y, c), (x, y, 1 - c)
        chips = _other_chips(x, y)

        def rows(px, py, pc):
            return out_ref.at[pl.ds((4 * px + 2 * py + pc) * m, m), :]

        def copy(k, block, to, src=None):
            return pltpu.make_async_remote_copy(
                src_ref=rows(*block) if src is None else src, dst_ref=rows(*block),
                send_sem=send_sems.at[k], recv_sem=recv_sems.at[k], device_id=to, device_id_type=MESH)

        mine = pltpu.make_async_copy(x_ref, rows(*me), local_sem)
        mine.start()
        first = [copy(0, me, sibling, src=x_ref)]
        first += [copy(1 + j, me, (*chip, c), src=x_ref) for j, chip in enumerate(chips)]
        for cp in first:
            cp.start()
        passed = [copy(4 + j, (*chip, c), sibling) for j, chip in enumerate(chips)]
        for j, chip in enumerate(chips):
            copy(1 + j, (*chip, c), me).wait_recv()
            passed[j].start()
        copy(0, sibling, me).wait_recv()
        for j, chip in enumerate(chips):
            copy(4 + j, (*chip, 1 - c), me).wait_recv()
        for cp in first + passed:
            cp.wait_send()
        mine.wait()

    return pl.pallas_call(
        body, name=name,
        out_shape=jax.ShapeDtypeStruct((N_DEV * m, n), xs.dtype),
        in_specs=[pl.BlockSpec(memory_space=pltpu.VMEM)],
        out_specs=pl.BlockSpec(memory_space=pltpu.VMEM),
        scratch_shapes=[pltpu.SemaphoreType.DMA((7,)), pltpu.SemaphoreType.DMA((7,)), pltpu.SemaphoreType.DMA],
    )(xs)


def _gather_phases(ins, outs, send_sems, recv_sems, local_sems):
    n = len(ins)
    x, y, c = _place()
    chips = _other_chips(x, y)

    def rows(i, core):
        h = ins[i].shape[0] // 2
        return pl.ds(pl.multiple_of(core * h, 16), h)

    def landed(i, chip, core):
        return outs[i].at[2 * chip[0] + chip[1], rows(i, core), :]

    def copy(i, k, src, dst, to):
        return pltpu.make_async_remote_copy(src_ref=src, dst_ref=dst, send_sem=send_sems.at[6 * i + k],
                                            recv_sem=recv_sems.at[6 * i + k], device_id=to, device_id_type=MESH)

    def local(i):
        r = ins[i].shape[0] // LOCAL_PIECES
        return [pltpu.make_async_copy(ins[i].at[pl.ds(q * r, r), :], outs[i].at[2 * x + y, pl.ds(q * r, r), :],
                                      local_sems.at[LOCAL_PIECES * i + q]) for q in range(LOCAL_PIECES)]

    def send(i, j):
        return copy(i, j, ins[i].at[rows(i, c), :], landed(i, (x, y), c), (*chips[j], c))

    def arrived(i, j, core, k):
        return copy(i, k, ins[i].at[rows(i, core), :], landed(i, chips[j], core), (x, y, 1 - c))

    def passed(i, j):
        return copy(i, 3 + j, landed(i, chips[j], c), landed(i, chips[j], c), (x, y, 1 - c))

    def start():
        for i in range(n):
            for lc in local(i):
                lc.start()
            for j in range(3):
                send(i, j).start()

    def forward():
        for i in range(n):
            for j in range(3):
                arrived(i, j, c, j).wait_recv()
                passed(i, j).start()

    def finish():
        for i in range(n):
            for j in range(3):
                arrived(i, j, 1 - c, 3 + j).wait_recv()
        for i in range(n):
            for j in range(3):
                send(i, j).wait_send()
                passed(i, j).wait_send()
            for lc in local(i):
                lc.wait()

    return start, forward, finish


LOCAL_PIECES = 4


def _gather_scratch(n):
    return [pltpu.SemaphoreType.DMA((6 * n,)), pltpu.SemaphoreType.DMA((6 * n,)),
            pltpu.SemaphoreType.DMA((LOCAL_PIECES * n,))]


def gather_weights(shards, name):
    n = len(shards)

    def body(*refs):
        start, forward, finish = _gather_phases(refs[:n], refs[n:2 * n], *refs[2 * n:])
        start()
        forward()
        finish()

    return pl.pallas_call(
        body, name=name,
        out_shape=[jax.ShapeDtypeStruct((N_SHARD,) + s.shape, s.dtype) for s in shards],
        in_specs=[ANY] * n, out_specs=[ANY] * n,
        scratch_shapes=_gather_scratch(n),
    )(*shards)


def rs_to_sibling(payloads, name):
    n = len(payloads)

    def body(*refs):
        ins, outs = refs[:n], refs[n:2 * n]
        send_sems, recv_sems = refs[2 * n:]
        x, y, c = _place()
        copies = []
        for i in range(n):
            cp = pltpu.make_async_remote_copy(src_ref=ins[i], dst_ref=outs[i], send_sem=send_sems.at[i],
                                              recv_sem=recv_sems.at[i], device_id=(x, y, 1 - c), device_id_type=MESH)
            cp.start()
            copies.append(cp)
        for cp in copies:
            cp.wait()

    return pl.pallas_call(
        body, name=name,
        out_shape=[jax.ShapeDtypeStruct(g.shape, g.dtype) for g in payloads],
        in_specs=[ANY] * n, out_specs=[ANY] * n,
        scratch_shapes=[pltpu.SemaphoreType.DMA((n,)), pltpu.SemaphoreType.DMA((n,))],
    )(*payloads)


def _to_chips_phases(ins, outs, send_sems, recv_sems):
    def copies():
        x, y, c = _place()
        return [pltpu.make_async_remote_copy(
            src_ref=ins[i].at[2 * px + py], dst_ref=outs[i].at[j], send_sem=send_sems.at[3 * i + j],
            recv_sem=recv_sems.at[3 * i + j], device_id=(px, py, c), device_id_type=MESH)
            for i in range(len(ins)) for j, (px, py) in enumerate(_other_chips(x, y))]

    def start():
        for cp in copies():
            cp.start()

    def finish():
        for cp in copies():
            cp.wait()

    return start, finish


def _to_chips_shapes(parts):
    return [jax.ShapeDtypeStruct((3,) + a.shape[1:], a.dtype) for a in parts]


def _to_chips_scratch(n):
    return [pltpu.SemaphoreType.DMA((3 * n,)), pltpu.SemaphoreType.DMA((3 * n,))]


def rs_join_halves(fulls, name):
    n = len(fulls)

    def body(*refs):
        outs = refs[n:2 * n]
        send_sems, recv_sems = refs[2 * n:]
        x, y, c = _place()

        def copy(i, core):
            h = fulls[i].shape[0] // 2
            rows = outs[i].at[pl.ds(pl.multiple_of(core * h, 8), h), :]
            return pltpu.make_async_remote_copy(src_ref=rows, dst_ref=rows, send_sem=send_sems.at[i],
                                                recv_sem=recv_sems.at[i], device_id=(x, y, 1 - c), device_id_type=MESH)

        sent = [copy(i, c) for i in range(n)]
        for cp in sent:
            cp.start()
        for i in range(n):
            copy(i, 1 - c).wait_recv()
        for cp in sent:
            cp.wait_send()

    return pl.pallas_call(
        body, name=name,
        out_shape=[jax.ShapeDtypeStruct(a.shape, a.dtype) for a in fulls],
        in_specs=[ANY] * n, out_specs=[ANY] * n,
        input_output_aliases={i: i for i in range(n)},
        scratch_shapes=[pltpu.SemaphoreType.DMA((n,)), pltpu.SemaphoreType.DMA((n,))],
    )(*fulls)


def _row_tile(rows, cols, limit_bytes=2 * 1024 * 1024, mult=8):
    best = mult
    for t in range(mult, rows + 1, mult):
        if rows % t == 0 and t * cols * 4 <= limit_bytes:
            best = t
    return best


def rs_add_sibling(g, recv, c, name):
    _, R, C = g.shape
    h = R // 2
    tr = _row_tile(h, C, mult=16)
    nt = h // tr

    def body(c_ref, g_ref, r_ref, o_ref, o16_ref):
        s = g_ref[...] + r_ref[...].astype(F32)
        o_ref[...] = s
        o16_ref[...] = s.astype(BF16)

    blk = pl.BlockSpec((None, tr, C), lambda k, i, c_ref: (k, i, 0))
    return pl.pallas_call(
        body, name=name,
        grid_spec=pltpu.PrefetchScalarGridSpec(
            num_scalar_prefetch=1, grid=(N_SHARD, nt),
            in_specs=[pl.BlockSpec((None, tr, C), lambda k, i, c_ref: (k, c_ref[0] * nt + i, 0)), blk],
            out_specs=[blk, blk]),
        out_shape=[jax.ShapeDtypeStruct((N_SHARD, h, C), F32), jax.ShapeDtypeStruct((N_SHARD, h, C), BF16)],
        compiler_params=_params("parallel", "parallel"),
    )(c, g, recv)


def rs_add_chips(part, recv, place, name):
    _, h, C = part.shape
    tr = _row_tile(h, C, mult=16)
    nt = h // tr

    def body(k_ref, p_ref, r_ref, o_ref):
        o_ref[...] = ((p_ref[...] + r_ref[0].astype(F32)) + r_ref[1].astype(F32)) + r_ref[2].astype(F32)

    return pl.pallas_call(
        body, name=name,
        grid_spec=pltpu.PrefetchScalarGridSpec(
            num_scalar_prefetch=1, grid=(nt,),
            in_specs=[pl.BlockSpec((None, tr, C), lambda i, k_ref: (k_ref[0], i, 0)),
                      pl.BlockSpec((3, tr, C), lambda i, k_ref: (0, i, 0))],
            out_specs=pl.BlockSpec((tr, C), lambda i, k_ref: (k_ref[1] * nt + i, 0))),
        out_shape=jax.ShapeDtypeStruct((2 * h, C), F32),
        compiler_params=_params("parallel"),
    )(place, part, recv)


def _adamw_math(w, g, m, v):
    m = ADAM_B1 * m + (1.0 - ADAM_B1) * g
    v = ADAM_B2 * v + (1.0 - ADAM_B2) * (g * g)
    m_hat = m / (1.0 - ADAM_B1 ** ADAM_STEP)
    v_hat = v / (1.0 - ADAM_B2 ** ADAM_STEP)
    delta = -ADAM_LR * (m_hat / (jnp.sqrt(v_hat) + ADAM_EPS) + ADAM_WD * w)
    return delta, m, v


def adamw(w, g, m, v, name):
    R, C = w.shape
    tr = _row_tile(R, C, 1024 * 1024)

    def body(w_ref, g_ref, m_ref, v_ref, d_ref, nm_ref, nv_ref):
        d_ref[...], nm_ref[...], nv_ref[...] = _adamw_math(w_ref[...], g_ref[...], m_ref[...], v_ref[...])

    blk = pl.BlockSpec((tr, C), lambda i: (i, 0))
    return pl.pallas_call(
        body, name=name, grid=(R // tr,), in_specs=[blk] * 4, out_specs=[blk] * 3,
        out_shape=[jax.ShapeDtypeStruct((R, C), F32)] * 3,
        compiler_params=_params("parallel"),
    )(w, g, m, v)


MOD_SH = 6 * D // N_SHARD
PK_ROWS = 16


def mod_fwd(call16, w_sh, b_sh, name):
    def body(c_ref, w_ref, b_ref, o_ref):
        o_ref[...] = _dot(_silu_parts(c_ref[...])[0], w_ref[...], prec=HI) + b_ref[...]

    return pl.pallas_call(body, name=name, out_shape=jax.ShapeDtypeStruct((16, MOD_SH), F32),
                          compiler_params=_params())(call16, w_sh, b_sh)


def prep_small(lbf2, lbb2, theta_row, name):
    def body(f_ref, b_ref, t_ref, lbf_ref, lbb_ref, lg_ref):
        lbf_ref[...] = _sigmoid(f_ref[0:1, :] - f_ref[1:2, :])
        lbb_ref[...] = _sigmoid(b_ref[0:1, :] - b_ref[1:2, :])
        t = t_ref[...]
        lg_ref[...] = jnp.minimum(t, 0.0) - jnp.log(1.0 + jnp.exp(-jnp.abs(t)))

    row = jax.ShapeDtypeStruct((1, D), F32)
    return pl.pallas_call(body, name=name, out_shape=[row, row, row], compiler_params=_params())(lbf2, lbb2, theta_row)


def small_grads(g3, lbf, lbb, theta_row, name):
    def body(g_ref, lbf_ref, lbb_ref, t_ref, pk_ref, aux_ref):
        s = g_ref[0]
        for d in range(1, N_DEV):
            s = s + g_ref[d]
        pk_ref[...] = jnp.zeros_like(pk_ref)
        aux_ref[...] = jnp.zeros_like(aux_ref)
        pk_ref[1:7, :] = s[0:6]
        pk_ref[1:3, :] += s[6:8]
        pk_ref[7:8, :] = s[8:9] + s[9:10]
        pk_ref[8:9, :] = s[10:11]
        lbf, lbb = lbf_ref[...], lbb_ref[...]
        daf = (s[13:14] + s[14:15]) * lbf * (1.0 - lbf)
        dab = (s[15:16] + s[16:17]) * lbb * (1.0 - lbb)
        pk_ref[9:10, :] = daf
        pk_ref[10:11, :] = -daf
        pk_ref[11:12, :] = dab
        pk_ref[12:13, :] = -dab
        pk_ref[13:14, :] = s[11:12]
        pk_ref[14:15, :] = (s[17:18] + s[18:19]) * _sigmoid(-t_ref[...])
        pk_ref[15:16, :] = s[12:13]
        aux_ref[0:2, :] = s[6:8]
        aux_ref[2:3, :] = jnp.broadcast_to(jnp.sum(s[19:20], axis=-1, keepdims=True), (1, D))

    return pl.pallas_call(body, name=name,
                          out_shape=[jax.ShapeDtypeStruct((PK_ROWS, D), F32), jax.ShapeDtypeStruct((8, D), F32)],
                          compiler_params=_params())(g3, lbf, lbb, theta_row)


def mod_bwd(call16, dmod_sh, w_sh, name):
    def body(c_ref, d_ref, w_ref, dw_ref, ds_ref):
        dm = d_ref[...]
        dw_ref[...] = _dot(_silu_parts(c_ref[...])[0], dm, 0, 0, prec=HI)
        ds_ref[...] = jnp.zeros_like(ds_ref)
        ds_ref[0:1, :] = _dot(dm[8:9, :], w_ref[...], 1, 1, prec=HI)

    return pl.pallas_call(body, name=name,
                          out_shape=[jax.ShapeDtypeStruct((D, MOD_SH), F32), jax.ShapeDtypeStruct((8, D), F32)],
                          compiler_params=_params())(call16, dmod_sh, w_sh)


def adamw_small(g4, pk_g, pk_w, pk_m, pk_v, name):
    def body(g4_ref, g_ref, w_ref, m_ref, v_ref, go_ref, d_ref, nm_ref, nv_ref):
        w = w_ref[...]
        ds = ((g4_ref[0:1, :] + g4_ref[16:17, :]) + g4_ref[32:33, :]) + g4_ref[48:49, :]
        row = lax.broadcasted_iota(jnp.int32, (PK_ROWS, D), 0)
        g = jnp.where(row == 0, ds * _silu_parts(w[0:1, :])[1], g_ref[...])
        go_ref[...] = g
        d_ref[...], nm_ref[...], nv_ref[...] = _adamw_math(w, g, m_ref[...], v_ref[...])

    pk = jax.ShapeDtypeStruct((PK_ROWS, D), F32)
    return pl.pallas_call(body, name=name, out_shape=[pk, pk, pk, pk], compiler_params=_params())(g4, pk_g, pk_w, pk_m, pk_v)


def _pack_params(c_ctx, b_mod, n1, n2, lbf, lbb, hgn, th_f, th_b, fin):
    theta = jnp.concatenate([th_f.reshape(HEADS), th_b.reshape(HEADS), jnp.zeros((D - 2 * HEADS,), F32)])
    return jnp.concatenate([c_ctx.reshape(1, D), b_mod.reshape(6, D), n1.reshape(1, D), n2.reshape(1, D), lbf, lbb,
                            hgn.reshape(1, D), theta.reshape(1, D), fin.reshape(1, D)], axis=0)


def _unpack_params(pk):
    return (pk[0], pk[1:7].reshape(1, 6 * D), pk[7:8], pk[8:9], pk[9:11], pk[11:13], pk[13:14],
            pk[14, 0:HEADS].reshape(1, HEADS), pk[14, HEADS:2 * HEADS].reshape(1, HEADS), pk[15])


def kernel(x, c, ctx, c_ctx, w_mod, b_mod, norm1_w, norm2_w, w_in, hg_lb_fwd, hg_lb_bwd, hg_norm_w, rt_theta_fwd, rt_theta_bwd, w_proj_hgrn, w_proj_ret, w_out, w_ffn_gate, w_ffn_up, w_ffn_down, final_norm_w, loss_target, m_c_ctx, m_w_mod, m_b_mod, m_norm1_w, m_norm2_w, m_w_in, m_hg_lb_fwd, m_hg_lb_bwd, m_hg_norm_w, m_rt_theta_fwd, m_rt_theta_bwd, m_w_proj_hgrn, m_w_proj_ret, m_w_out, m_w_ffn_gate, m_w_ffn_up, m_w_ffn_down, m_final_norm_w, v_c_ctx, v_w_mod, v_b_mod, v_norm1_w, v_norm2_w, v_w_in, v_hg_lb_fwd, v_hg_lb_bwd, v_hg_norm_w, v_rt_theta_fwd, v_rt_theta_bwd, v_w_proj_hgrn, v_w_proj_ret, v_w_out, v_w_ffn_gate, v_w_ffn_up, v_w_ffn_down, v_final_norm_w):
    xi, yi, ci = _place()
    dev = 4 * xi + 2 * yi + ci
    chip = 2 * xi + yi
    core_arg = jnp.reshape(ci, (1,)).astype(jnp.int32)
    place_arg = jnp.stack([chip, ci]).astype(jnp.int32)

    c_all = allgather8(jnp.concatenate([c, jnp.zeros((7, D), F32)], axis=0), "gather_c").reshape(N_DEV, 8, D)[:, 0]
    call16 = jnp.concatenate([c_all, c_ctx.reshape(1, D), jnp.zeros((7, D), F32)], axis=0)
    b_sh = lax.dynamic_slice_in_dim(b_mod, chip * MOD_SH, MOD_SH, axis=1)
    mod_sh = mod_fwd(call16, w_mod[0], b_sh, "mod_fwd")
    mod_g = allgather8(mod_sh, "gather_mod").reshape(N_DEV, 16, MOD_SH)
    mod_all = jnp.concatenate([mod_g[0], mod_g[2], mod_g[4], mod_g[6]], axis=1)
    mod_x = lax.dynamic_index_in_dim(mod_all, dev, axis=0, keepdims=False).reshape(6, D)
    mod_c = mod_all[8].reshape(6, D)

    pk_w = _pack_params(c_ctx, b_mod, norm1_w, norm2_w, hg_lb_fwd, hg_lb_bwd, hg_norm_w, rt_theta_fwd, rt_theta_bwd, final_norm_w)
    theta_row = pk_w[14:15]
    lb_f, lb_b, lg_row = prep_small(hg_lb_fwd, hg_lb_bwd, theta_row, "prep_small")
    lg_f = jnp.broadcast_to(lg_row[0, 0:HEADS].reshape(HEADS, 1, 1), (HEADS, 1, RT_DV))
    lg_b = jnp.broadcast_to(lg_row[0, HEADS:2 * HEADS].reshape(HEADS, 1, 1), (HEADS, 1, RT_DV))

    (g_in,) = gather_weights([w_in[0].astype(BF16)], "gather_w_in")
    rest = [s[0].astype(BF16) for s in (w_proj_hgrn, w_proj_ret, w_out, w_ffn_gate, w_ffn_up, w_ffn_down)]

    dx, full, small = local_step(x[0], ctx[0], loss_target[0], mod_x, mod_c, lb_f, lb_b, lg_f, lg_b,
                                 norm1_w, norm2_w, hg_norm_w, final_norm_w.reshape(1, D), {"w_in": g_in}, rest,
                                 (ci, core_arg, place_arg))

    g3 = allgather8(small, "gather_small").reshape(N_DEV, SMALL_ROWS, D)
    pk_g, aux = small_grads(g3, lb_f, lb_b, theta_row, "small_grads")
    loss = aux[2, 0]
    dmod16 = jnp.concatenate([
        g3[:, 0:6, :].reshape(N_DEV, 6 * D),
        jnp.concatenate([aux[0], aux[1], jnp.zeros((4 * D,), F32)]).reshape(1, 6 * D),
        jnp.zeros((7, 6 * D), F32)], axis=0)
    dmod_sh = lax.dynamic_slice_in_dim(dmod16, chip * MOD_SH, MOD_SH, axis=1)
    g_wmod, dsilu = mod_bwd(call16, dmod_sh, w_mod[0], "mod_bwd")
    g4 = allgather8(dsilu, "gather_dsilu")
    pk_m = _pack_params(m_c_ctx, m_b_mod, m_norm1_w, m_norm2_w, m_hg_lb_fwd, m_hg_lb_bwd, m_hg_norm_w, m_rt_theta_fwd, m_rt_theta_bwd, m_final_norm_w)
    pk_v = _pack_params(v_c_ctx, v_b_mod, v_norm1_w, v_norm2_w, v_hg_lb_fwd, v_hg_lb_bwd, v_hg_norm_w, v_rt_theta_fwd, v_rt_theta_bwd, v_final_norm_w)
    pk_g, pk_d, pk_nm, pk_nv = adamw_small(g4, pk_g, pk_w, pk_m, pk_v, "adamw_small")

    big = {
        "w_mod": (g_wmod, w_mod, m_w_mod, v_w_mod),
        "w_in": (full["w_in"], w_in, m_w_in, v_w_in),
        "w_pa": (full["w_pa"], w_proj_hgrn, m_w_proj_hgrn, v_w_proj_hgrn),
        "w_pb": (full["w_pb"], w_proj_ret, m_w_proj_ret, v_w_proj_ret),
        "w_out": (full["w_out"], w_out, m_w_out, v_w_out),
        "wg": (full["wg"], w_ffn_gate, m_w_ffn_gate, v_w_ffn_gate),
        "wu": (full["wu"], w_ffn_up, m_w_ffn_up, v_w_ffn_up),
        "wd": (full["wd"], w_ffn_down, m_w_ffn_down, v_w_ffn_down),
    }
    res = {}
    for k, (g, wt, mt, vt) in big.items():
        d, nm, nv = adamw(wt[0], g, mt[0], vt[0], "adamw_" + k)
        res[k] = (g[None], d[None], nm[None], nv[None])

    sm = [_unpack_params(p) for p in (pk_g, pk_d, pk_nm, pk_nv)]
    outs = []
    for t in range(4):
        (s_cctx, s_bmod, s_n1, s_n2, s_lbf, s_lbb, s_hgn, s_thf, s_thb, s_fin) = sm[t]
        outs.append([s_cctx, res["w_mod"][t], s_bmod, s_n1, s_n2, res["w_in"][t], s_lbf, s_lbb, s_hgn, s_thf, s_thb,
                     res["w_pa"][t], res["w_pb"][t], res["w_out"][t], res["wg"][t], res["wu"][t], res["wd"][t], s_fin])
    return (loss, dx[None], *outs[0], *outs[1], *outs[2], *outs[3])
```

```python
import functools

import jax
import jax.numpy as jnp
from jax import lax
from jax.experimental import pallas as pl
from jax.experimental.pallas import tpu as pltpu

F32 = jnp.float32
BF16 = jnp.bfloat16
HI = lax.Precision.HIGHEST

D = 1024
HEADS = 8
HG_D = 128
RT_DK = 128
RT_DV = 256
D_FF = 2816
D_IN = 13312
N_SHARD = 4
IN_SH = D_IN // N_SHARD
FF_SH = D_FF // N_SHARD
HG_CHUNK = 32
SCAN_ROWS = 256
HG_GROUP = 8
RT_GROUP = 4
PROJ_ROWS = 1024
EPS = 1e-6
GN_EPS = 1e-5
Q_SCALE = 128.0 ** -0.5
VMEM_LIMIT = 56 * 1024 * 1024

COL_HQ, COL_HFF, COL_HFB, COL_HI, COL_HG = 0, 8, 16, 24, 32
COL_RQ, COL_RK, COL_RV, COL_RG, COL_GA, COL_GB = 40, 48, 56, 72, 88, 96

ADAM_LR, ADAM_B1, ADAM_B2, ADAM_EPS, ADAM_WD, ADAM_STEP = 0.001, 0.9, 0.999, 1e-08, 0.01, 10


def _params(*sem):
    return pltpu.CompilerParams(dimension_semantics=sem, vmem_limit_bytes=VMEM_LIMIT)


def _dot(a, b, ca=1, cb=0, prec=None):
    return lax.dot_general(a, b, (((ca,), (cb,)), ((), ())), precision=prec, preferred_element_type=F32)


def _bdot(a, b, ca=1, cb=0):
    return _dot(a.astype(BF16), b.astype(BF16), ca, cb)


def _sigmoid(z):
    return 1.0 / (1.0 + jnp.exp(-z))


def _rowsum(a):
    return jnp.sum(a, axis=0, keepdims=True)


def _lanemean(a):
    return jnp.mean(a, axis=-1, keepdims=True)


def _grid_step(grid):
    pos, total = 0, 1
    for d, size in enumerate(grid):
        pos = pos * size + pl.program_id(d)
        total *= size
    return pos, total


def normmod_matmul(x, nw, sh, sc, w4, name, gather=()):
    L = x.shape[0]
    tm = min(PROJ_ROWS, L)
    tn = IN_SH // 2
    grid = (L // tm, N_SHARD, 2)
    ng = len(gather)

    def body(x_ref, nw_ref, sh_ref, sc_ref, w_ref, *refs):
        p_ref, hx_ref = refs[ng:ng + 2]
        hx_scr = refs[2 * ng + 2]
        if ng:
            start, forward, finish = _gather_phases(refs[:ng], refs[ng + 2:2 * ng + 2], *refs[2 * ng + 3:])
            pos, total = _grid_step(grid)
            pl.when(pos == 0)(start)
            pl.when(pos == total // 2)(forward)

        @pl.when((pl.program_id(1) == 0) & (pl.program_id(2) == 0))
        def _():
            xv = x_ref[...]
            n = xv * lax.rsqrt(_lanemean(xv * xv) + EPS) * nw_ref[...]
            h = (n * (1.0 + sc_ref[...]) + sh_ref[...]).astype(BF16)
            hx_scr[...] = h
            hx_ref[...] = h

        p_ref[...] = _dot(hx_scr[...], w_ref[...])
        if ng:
            pl.when(pos == total - 1)(finish)

    vec = pl.BlockSpec((1, D), lambda i, k, j: (0, 0))
    return pl.pallas_call(
        body, name=name,
        grid=grid,
        in_specs=[pl.BlockSpec((tm, D), lambda i, k, j: (i, 0)), vec, vec, vec,
                  pl.BlockSpec((D, tn), lambda i, k, j: (0, 2 * k + j))] + [ANY] * ng,
        out_specs=[pl.BlockSpec((tm, tn), lambda i, k, j: (i, 2 * k + j)),
                   pl.BlockSpec((tm, D), lambda i, k, j: (i, 0))] + [ANY] * ng,
        out_shape=[jax.ShapeDtypeStruct((L, D_IN), F32), jax.ShapeDtypeStruct((L, D), BF16)]
        + [jax.ShapeDtypeStruct((N_SHARD,) + s.shape, s.dtype) for s in gather],
        scratch_shapes=[pltpu.VMEM((tm, D), BF16)] + (_gather_scratch(ng) if ng else []),
        compiler_params=_params("arbitrary", "arbitrary", "arbitrary"),
    )(x, nw, sh, sc, w4, *gather)


def _hgrn_gates(z, lb):
    sg = _sigmoid(z)
    sgn = _sigmoid(-z)
    f = lb + (1.0 - lb) * sg
    k = (1.0 - lb) * sgn
    return sg, sgn, f, k


def _tri_chunks(n, chunk, reverse):
    r = lax.broadcasted_iota(jnp.int32, (n, n), 0)
    c = lax.broadcasted_iota(jnp.int32, (n, n), 1)
    same = (r // chunk) == (c // chunk)
    return jnp.where(same & ((r <= c) if reverse else (r >= c)), 1.0, 0.0).astype(F32)


def _decay3(b, reverse):
    C = b.shape[0]
    t = lax.broadcasted_iota(jnp.int32, (C, C, 1), 0)
    s = lax.broadcasted_iota(jnp.int32, (C, C, 1), 1)
    mask = (t <= s) if reverse else (t >= s)
    return jnp.exp(jnp.where(mask, b[:, None, :] - b[None, :, :], -jnp.inf))


HG_SUB = 16


def _hgrn_pairs(reverse):
    pairs = []
    size = HG_SUB
    while size < HG_CHUNK:
        for lo in range(0, HG_CHUNK, 2 * size):
            first, second = slice(lo, lo + size), slice(lo + size, lo + 2 * size)
            if reverse:
                pairs.append((first, second, lo + size))
            else:
                pairs.append((second, first, lo + size - 1))
        size *= 2
    return pairs


def _hgrn_intra_fwd(q, k, v, b, reverse):
    blocks = []
    for lo in range(0, HG_CHUNK, HG_SUB):
        r = slice(lo, lo + HG_SUB)
        att3 = jnp.sum(q[r][:, None, :] * k[r][None, :, :] * _decay3(b[r], reverse), axis=-1, keepdims=True)
        blocks.append(jnp.sum(att3 * v[r][None, :, :], axis=1))
    for qr, kr, ref in _hgrn_pairs(reverse):
        beta = b[ref:ref + 1]
        att = _bdot(q[qr] * jnp.exp(b[qr] - beta), k[kr] * jnp.exp(beta - b[kr]), 1, 1)
        part = _bdot(att, v[kr])
        n = part.shape[0] // HG_SUB
        for i in range(n):
            blocks[qr.start // HG_SUB + i] += part[i * HG_SUB:(i + 1) * HG_SUB]
    return jnp.concatenate(blocks, axis=0)


def _hgrn_intra_bwd(q, k, v, b, d_o, reverse):
    nb = HG_CHUNK // HG_SUB
    dq, dk, dv = [None] * nb, [None] * nb, [None] * nb
    for i in range(nb):
        r = slice(i * HG_SUB, (i + 1) * HG_SUB)
        e3 = _decay3(b[r], reverse)
        p3 = jnp.sum(d_o[r][:, None, :] * v[r][None, :, :], axis=-1, keepdims=True) * e3
        dq[i] = jnp.sum(p3 * k[r][None, :, :], axis=1)
        dk[i] = jnp.sum(p3 * q[r][:, None, :], axis=0)
        att3 = jnp.sum(q[r][:, None, :] * k[r][None, :, :] * e3, axis=-1, keepdims=True)
        dv[i] = jnp.sum(att3 * d_o[r][:, None, :], axis=0)

    def add(acc, rows, part):
        for i in range(part.shape[0] // HG_SUB):
            acc[rows.start // HG_SUB + i] += part[i * HG_SUB:(i + 1) * HG_SUB]

    for qr, kr, ref in _hgrn_pairs(reverse):
        beta = b[ref:ref + 1]
        fq, fk = jnp.exp(b[qr] - beta), jnp.exp(beta - b[kr])
        qt, kt = q[qr] * fq, k[kr] * fk
        att = _bdot(qt, kt, 1, 1)
        datt = _bdot(d_o[qr], v[kr], 1, 1)
        add(dq, qr, _bdot(datt, kt) * fq)
        add(dk, kr, _bdot(datt, qt, 0, 0) * fk)
        add(dv, kr, _bdot(att, d_o[qr], 0, 0))
    return jnp.concatenate(dq, axis=0), jnp.concatenate(dk, axis=0), jnp.concatenate(dv, axis=0)


def _hgrn_state_step(k, v, b, s_t, last):
    b_last = b[last:last + 1]
    return s_t * jnp.exp(b_last) + _bdot(v, k * jnp.exp(b_last - b), 0, 0)


def hgrn_scan_fwd(p, lb, s0, col_z, reverse, name):
    L = p.shape[0]
    nB = L // SCAN_ROWS
    nC = SCAN_ROWS // HG_CHUNK
    C = HG_CHUNK
    G, W = HG_GROUP, HG_GROUP * HG_D
    last = 0 if reverse else C - 1

    def bmap(b):
        return (nB - 1 - b) if reverse else b

    def body(q_ref, z_ref, v_ref, lb_ref, s0_ref, o_ref, sfin_ref, sblk_ref, s_scr, k_scr, b_scr):
        blk = pl.program_id(1)

        @pl.when(blk == 0)
        def _():
            s_scr[...] = s0_ref[...]

        sblk_ref[...] = s_scr[...]
        _, _, f_all, k_all = _hgrn_gates(z_ref[...], lb_ref[...])
        k_scr[...] = k_all
        b_scr[...] = _dot(_tri_chunks(SCAN_ROWS, C, reverse), jnp.log(f_all), prec=HI)

        def chunk(ci, carry):
            c = (nC - 1 - ci) if reverse else ci
            rows = pl.ds(pl.multiple_of(c * C, C), C)
            for j in range(G):
                lanes = slice(j * HG_D, (j + 1) * HG_D)
                q = q_ref[rows, lanes] * Q_SCALE
                v = v_ref[rows, lanes]
                k = k_scr[rows, lanes]
                b = b_scr[rows, lanes]
                s_t = s_scr[j]
                o_ref[rows, lanes] = _hgrn_intra_fwd(q, k, v, b, reverse) + _bdot(q * jnp.exp(b), s_t, 1, 1)
                s_scr[j] = _hgrn_state_step(k, v, b, s_t, last)
            return carry

        lax.fori_loop(0, nC, chunk, 0)

        @pl.when(blk == nB - 1)
        def _():
            sfin_ref[...] = s_scr[...]

    def col(c0):
        return pl.BlockSpec((SCAN_ROWS, W), lambda h, b: (bmap(b), c0 // G + h))

    state = pl.BlockSpec((G, HG_D, HG_D), lambda h, b: (h, 0, 0))
    return pl.pallas_call(
        body, name=name,
        grid=(HEADS // G, nB),
        in_specs=[col(COL_HQ), col(col_z), col(COL_HI), pl.BlockSpec((1, W), lambda h, b: (0, h)), state],
        out_specs=[pl.BlockSpec((SCAN_ROWS, W), lambda h, b: (bmap(b), h)), state,
                   pl.BlockSpec((None, G, HG_D, HG_D), lambda h, b: (bmap(b), h, 0, 0))],
        out_shape=[jax.ShapeDtypeStruct((L, D), F32),
                   jax.ShapeDtypeStruct((HEADS, HG_D, HG_D), F32),
                   jax.ShapeDtypeStruct((nB, HEADS, HG_D, HG_D), F32)],
        scratch_shapes=[pltpu.VMEM((G, HG_D, HG_D), F32), pltpu.VMEM((SCAN_ROWS, W), F32),
                        pltpu.VMEM((SCAN_ROWS, W), F32)],
        compiler_params=_params("parallel", "arbitrary"),
    )(p, p, p, lb, s0)


def hgrn_scan_bwd(p, lb, s_blocks, d_o, ds_fin, prev, col_z, reverse, name):
    L = p.shape[0]
    nB = L // SCAN_ROWS
    nC = SCAN_ROWS // HG_CHUNK
    C = HG_CHUNK
    G, W = HG_GROUP, HG_GROUP * HG_D
    last = 0 if reverse else C - 1
    has_prev = prev is not None
    out_dt = BF16 if has_prev else F32

    def bmap(b):
        return b if reverse else (nB - 1 - b)

    def body(*refs):
        q_ref, z_ref, v_ref, lb_ref, sblk_ref, do_ref, dsf_ref = refs[:7]
        refs = refs[7:]
        if has_prev:
            pq_ref, pv_ref = refs[:2]
            refs = refs[2:]
        dq_ref, dz_ref, dv_ref, dlb_ref, ds0_ref, st_scr, run_scr, ds_scr, k_scr, b_scr, db_scr, dk_scr = refs
        blk = pl.program_id(1)

        @pl.when(blk == 0)
        def _():
            ds_scr[...] = dsf_ref[...]
            dlb_ref[...] = jnp.zeros_like(dlb_ref)

        tri = _tri_chunks(SCAN_ROWS, C, reverse)
        row = lax.broadcasted_iota(jnp.int32, (C, HG_D), 0)
        _, _, f_all, k_all = _hgrn_gates(z_ref[...], lb_ref[...])
        k_scr[...] = k_all
        b_scr[...] = _dot(tri, jnp.log(f_all), prec=HI)
        run_scr[...] = sblk_ref[...]

        def recompute(ci, carry):
            c = (nC - 1 - ci) if reverse else ci
            rows = pl.ds(pl.multiple_of(c * C, C), C)
            for j in range(G):
                lanes = slice(j * HG_D, (j + 1) * HG_D)
                s_t = run_scr[j]
                st_scr[c, j] = s_t
                run_scr[j] = _hgrn_state_step(k_scr[rows, lanes], v_ref[rows, lanes], b_scr[rows, lanes], s_t, last)
            return carry

        lax.fori_loop(0, nC, recompute, 0)

        def chunk(ci, carry):
            c = ci if reverse else (nC - 1 - ci)
            rows = pl.ds(pl.multiple_of(c * C, C), C)
            for j in range(G):
                lanes = slice(j * HG_D, (j + 1) * HG_D)
                k = k_scr[rows, lanes]
                b = b_scr[rows, lanes]
                q = q_ref[rows, lanes] * Q_SCALE
                v = v_ref[rows, lanes]
                d_o = do_ref[rows, lanes]
                s_t = st_scr[c, j]
                ds_t = ds_scr[j]
                eb = jnp.exp(b)
                b_last = b[last:last + 1]
                eb_last = jnp.exp(b_last)
                kdec = jnp.exp(b_last - b)
                qe = q * eb
                ke = k * kdec
                dq_in, dk_in, dv_in = _hgrn_intra_bwd(q, k, v, b, d_o, reverse)
                dq_tot = _bdot(d_o, s_t, 1, 0) * eb + dq_in
                dke = _bdot(v, ds_t, 1, 0)
                dk_tot = dke * kdec + dk_in
                dv = dv_in + _bdot(ke, ds_t, 1, 1)
                db_last = _rowsum(dke * ke) + eb_last * _rowsum(ds_t * s_t)
                db_scr[rows, lanes] = q * dq_tot - k * dk_tot + jnp.where(row == last, db_last, 0.0)
                dk_scr[rows, lanes] = dk_tot
                dq = dq_tot * Q_SCALE
                if has_prev:
                    dq = dq + pq_ref[rows, lanes]
                    dv = dv + pv_ref[rows, lanes]
                dq_ref[rows, lanes] = dq.astype(out_dt)
                dv_ref[rows, lanes] = dv.astype(out_dt)
                ds_scr[j] = ds_t * eb_last + _bdot(d_o, qe, 0, 0)
            return carry

        lax.fori_loop(0, nC, chunk, 0)

        lb = lb_ref[...]
        sg, sgn, f, _ = _hgrn_gates(z_ref[...], lb)
        g = _dot(tri, db_scr[...], 0, 0, prec=HI) / f - dk_scr[...]
        dz_ref[...] = (g * (1.0 - lb) * sg * sgn).astype(BF16)
        dlb_ref[...] += _rowsum(g * sgn)

        @pl.when(blk == nB - 1)
        def _():
            ds0_ref[...] = ds_scr[...]

    def col(c0):
        return pl.BlockSpec((SCAN_ROWS, W), lambda h, b: (bmap(b), c0 // G + h))

    tile = pl.BlockSpec((SCAN_ROWS, W), lambda h, b: (bmap(b), h))
    state = pl.BlockSpec((G, HG_D, HG_D), lambda h, b: (h, 0, 0))
    in_specs = [col(COL_HQ), col(col_z), col(COL_HI),
                pl.BlockSpec((1, W), lambda h, b: (0, h)),
                pl.BlockSpec((None, G, HG_D, HG_D), lambda h, b: (bmap(b), h, 0, 0)),
                tile, state]
    args = [p, p, p, lb, s_blocks, d_o, ds_fin]
    if has_prev:
        in_specs += [tile, tile]
        args += list(prev)
    return pl.pallas_call(
        body, name=name,
        grid=(HEADS // G, nB),
        in_specs=in_specs,
        out_specs=[tile, tile, tile, pl.BlockSpec((1, W), lambda h, b: (0, h)), state],
        out_shape=[jax.ShapeDtypeStruct((L, D), out_dt), jax.ShapeDtypeStruct((L, D), BF16),
                   jax.ShapeDtypeStruct((L, D), out_dt), jax.ShapeDtypeStruct((1, D), F32),
                   jax.ShapeDtypeStruct((HEADS, HG_D, HG_D), F32)],
        scratch_shapes=[pltpu.VMEM((nC, G, HG_D, HG_D), F32), pltpu.VMEM((G, HG_D, HG_D), F32),
                        pltpu.VMEM((G, HG_D, HG_D), F32)] + [pltpu.VMEM((SCAN_ROWS, W), F32)] * 4,
        compiler_params=_params("parallel", "arbitrary"),
    )(*args)


def _rope(t, cosf, sinf):
    return t * cosf + pltpu.roll(t, RT_DK // 2, 1) * sinf


def _rope_t(d, cosf, sinf):
    return d * cosf + pltpu.roll(d * sinf, RT_DK // 2, 1)


def _ret_decays(lg, reverse):
    C = SCAN_ROWS
    t = lax.broadcasted_iota(jnp.int32, (C, C), 0)
    s = lax.broadcasted_iota(jnp.int32, (C, C), 1)
    delta = ((s - t) if reverse else (t - s)).astype(F32)
    dmat = jnp.where(delta >= 0, jnp.exp(lg * jnp.maximum(delta, 0.0)), 0.0)
    r = lax.broadcasted_iota(jnp.int32, (C, RT_DK), 0)
    pos = ((C - 1 - r) if reverse else r).astype(F32)
    lg1 = lg[:, :RT_DK]
    qdec = jnp.exp(lg1 * (pos + 1.0))
    kdec = jnp.exp(lg1 * (C - 1.0 - pos))
    sdec = jnp.exp(lg1 * float(C))
    return dmat, delta, pos, qdec, kdec, sdec


def ret_scan_fwd(p, cosf, sinf, lg, s0, reverse, name):
    L = p.shape[0]
    C = SCAN_ROWS
    nB = L // C

    def bmap(b):
        return (nB - 1 - b) if reverse else b

    G = RT_GROUP

    def body(q_ref, k_ref, v_ref, cos_ref, sin_ref, lg_ref, s0_ref, o_ref, sfin_ref, sblk_ref, s_scr):
        blk = pl.program_id(1)

        @pl.when(blk == 0)
        def _():
            s_scr[...] = s0_ref[...]

        sblk_ref[...] = s_scr[...]
        cosf, sinf = cos_ref[...], sin_ref[...]
        for j in range(G):
            lk, lv = slice(j * RT_DK, (j + 1) * RT_DK), slice(j * RT_DV, (j + 1) * RT_DV)
            s_t = s_scr[j]
            dmat, _, _, qdec, kdec, sdec = _ret_decays(lg_ref[j], reverse)
            q = _rope(q_ref[:, lk] * Q_SCALE, cosf, sinf)
            k = _rope(k_ref[:, lk], cosf, sinf)
            v = v_ref[:, lv]
            att = _bdot(q, k, 1, 1) * dmat
            o_ref[:, lv] = _bdot(att, v) + _bdot(q * qdec, s_t, 1, 1)
            s_scr[j] = s_t * sdec + _bdot(v, k * kdec, 0, 0)

        @pl.when(blk == nB - 1)
        def _():
            sfin_ref[...] = s_scr[...]

    def col(c0):
        return pl.BlockSpec((C, G * RT_DK), lambda h, b: (bmap(b), c0 // G + h))

    tab = pl.BlockSpec((C, RT_DK), lambda h, b: (bmap(b), 0))
    state = pl.BlockSpec((G, RT_DV, RT_DK), lambda h, b: (h, 0, 0))
    return pl.pallas_call(
        body, name=name,
        grid=(HEADS // G, nB),
        in_specs=[col(COL_RQ), col(COL_RK),
                  pl.BlockSpec((C, G * RT_DV), lambda h, b: (bmap(b), COL_RV // (2 * G) + h)),
                  tab, tab, pl.BlockSpec((G, 1, RT_DV), lambda h, b: (h, 0, 0)), state],
        out_specs=[pl.BlockSpec((C, G * RT_DV), lambda h, b: (bmap(b), h)), state,
                   pl.BlockSpec((None, G, RT_DV, RT_DK), lambda h, b: (bmap(b), h, 0, 0))],
        out_shape=[jax.ShapeDtypeStruct((L, HEADS * RT_DV), F32),
                   jax.ShapeDtypeStruct((HEADS, RT_DV, RT_DK), F32),
                   jax.ShapeDtypeStruct((nB, HEADS, RT_DV, RT_DK), F32)],
        scratch_shapes=[pltpu.VMEM((G, RT_DV, RT_DK), F32)],
        compiler_params=_params("parallel", "arbitrary"),
    )(p, p, p, cosf, sinf, lg, s0)


def ret_scan_bwd(p, cosf, sinf, lg, s_blocks, d_o, ds_fin, prev, reverse, name):
    L = p.shape[0]
    C = SCAN_ROWS
    nB = L // C
    has_prev = prev is not None
    out_dt = BF16 if has_prev else F32
    G = RT_GROUP

    def bmap(b):
        return b if reverse else (nB - 1 - b)

    def body(*refs):
        q_ref, k_ref, v_ref, cos_ref, sin_ref, lg_ref, sblk_ref, do_ref, dsf_ref = refs[:9]
        refs = refs[9:]
        if has_prev:
            pq_ref, pk_ref, pv_ref = refs[:3]
            refs = refs[3:]
        dq_ref, dk_ref, dv_ref, dlg_ref, ds0_ref, ds_scr = refs
        blk = pl.program_id(1)

        @pl.when(blk == 0)
        def _():
            ds_scr[...] = dsf_ref[...]
            dlg_ref[...] = jnp.zeros_like(dlg_ref)

        cosf, sinf = cos_ref[...], sin_ref[...]
        for j in range(G):
            lk, lv = slice(j * RT_DK, (j + 1) * RT_DK), slice(j * RT_DV, (j + 1) * RT_DV)
            s_t = sblk_ref[j]
            ds_t = ds_scr[j]
            dmat, delta, pos, qdec, kdec, sdec = _ret_decays(lg_ref[j], reverse)
            q = _rope(q_ref[:, lk] * Q_SCALE, cosf, sinf)
            k = _rope(k_ref[:, lk], cosf, sinf)
            v = v_ref[:, lv]
            d_o = do_ref[:, lv]
            att_raw = _bdot(q, k, 1, 1)
            datt_m = _bdot(d_o, v, 1, 1) * dmat
            dqd = _bdot(d_o, s_t, 1, 0)
            dkd = _bdot(v, ds_t, 1, 0)
            dq = _bdot(datt_m, k) + dqd * qdec
            dk = _bdot(datt_m, q, 0, 0) + dkd * kdec
            dv = _bdot(att_raw * dmat, d_o, 0, 0) + _bdot(k * kdec, ds_t, 1, 1)
            ds_scr[j] = ds_t * sdec + _bdot(d_o, q * qdec, 0, 0)
            t1 = jnp.sum(_rowsum(datt_m * att_raw * delta), axis=-1, keepdims=True)
            t23 = jnp.sum(_rowsum((pos + 1.0) * qdec * q * dqd + (C - 1.0 - pos) * kdec * k * dkd), axis=-1, keepdims=True)
            t4 = jnp.sum(_rowsum(ds_t * s_t * sdec), axis=-1, keepdims=True) * float(C)
            dlg_ref[j] += jnp.broadcast_to(t1 + t23 + t4, (1, RT_DK))
            if has_prev:
                dq = _rope_t(dq + pq_ref[:, lk], cosf, sinf) * Q_SCALE
                dk = _rope_t(dk + pk_ref[:, lk], cosf, sinf)
                dv = dv + pv_ref[:, lv]
            dq_ref[:, lk] = dq.astype(out_dt)
            dk_ref[:, lk] = dk.astype(out_dt)
            dv_ref[:, lv] = dv.astype(out_dt)

        @pl.when(blk == nB - 1)
        def _():
            ds0_ref[...] = ds_scr[...]

    def col(c0):
        return pl.BlockSpec((C, G * RT_DK), lambda h, b: (bmap(b), c0 // G + h))

    tab = pl.BlockSpec((C, RT_DK), lambda h, b: (bmap(b), 0))
    state = pl.BlockSpec((G, RT_DV, RT_DK), lambda h, b: (h, 0, 0))
    tk = pl.BlockSpec((C, G * RT_DK), lambda h, b: (bmap(b), h))
    tv = pl.BlockSpec((C, G * RT_DV), lambda h, b: (bmap(b), h))
    in_specs = [col(COL_RQ), col(COL_RK),
                pl.BlockSpec((C, G * RT_DV), lambda h, b: (bmap(b), COL_RV // (2 * G) + h)),
                tab, tab, pl.BlockSpec((G, 1, RT_DV), lambda h, b: (h, 0, 0)),
                pl.BlockSpec((None, G, RT_DV, RT_DK), lambda h, b: (bmap(b), h, 0, 0)),
                tv, state]
    args = [p, p, p, cosf, sinf, lg, s_blocks, d_o, ds_fin]
    if has_prev:
        in_specs += [tk, tk, tv]
        args += list(prev)
    return pl.pallas_call(
        body, name=name,
        grid=(HEADS // G, nB),
        in_specs=in_specs,
        out_specs=[tk, tk, tv, pl.BlockSpec((G, 1, RT_DK), lambda h, b: (h, 0, 0)), state],
        out_shape=[jax.ShapeDtypeStruct((L, D), out_dt), jax.ShapeDtypeStruct((L, D), out_dt),
                   jax.ShapeDtypeStruct((L, HEADS * RT_DV), out_dt),
                   jax.ShapeDtypeStruct((HEADS, 1, RT_DK), F32),
                   jax.ShapeDtypeStruct((HEADS, RT_DV, RT_DK), F32)],
        scratch_shapes=[pltpu.VMEM((G, RT_DV, RT_DK), F32)],
        compiler_params=_params("parallel", "arbitrary"),
    )(*args)


def _silu_parts(h):
    s = _sigmoid(h)
    return h * s, s * (1.0 + h * (1.0 - s))


def _head_rms(o):
    outs, rs = [], []
    for h in range(HEADS):
        oh = o[:, h * HG_D:(h + 1) * HG_D]
        r = lax.rsqrt(_lanemean(oh * oh) + EPS)
        outs.append(oh * r)
        rs.append(r)
    return outs, rs


def _group_norm(o):
    outs, rs = [], []
    for h in range(HEADS):
        oh = o[:, h * RT_DV:(h + 1) * RT_DV]
        c = oh - _lanemean(oh)
        r = lax.rsqrt(_lanemean(c * c) + GN_EPS)
        outs.append(c * r)
        rs.append(r)
    return outs, rs


MIX_ROWS = 256
MIX_BWD_ROWS = 128


def _mix_specs(rows):
    def t(w, c=0):
        return pl.BlockSpec((rows, w), lambda i: (i, c))

    return t


def mix_fwd(ohf, ohb, orf, orb, p, x, g1, hgw, w_pa, w_pb, w_out, name):
    L = x.shape[0]
    t = _mix_specs(MIX_ROWS)

    def body(ohf_ref, ohb_ref, orf_ref, orb_ref, hg_ref, rg0_ref, rg1_ref, ga_ref, gb_ref, x_ref, g1_ref, hgw_ref,
             wpa_ref, wpb_ref, wout_ref, x1_ref, xmix_ref, merged_ref, ya_ref, yb_ref):
        nh, _ = _head_rms(ohf_ref[...] + ohb_ref[...])
        ya = jnp.concatenate(nh, axis=1) * hgw_ref[...] * _silu_parts(hg_ref[...])[0]
        gn, _ = _group_norm(orf_ref[...] + orb_ref[...])
        rg = jnp.concatenate([rg0_ref[...], rg1_ref[...]], axis=1)
        yb = jnp.concatenate(gn, axis=1) * _silu_parts(rg)[0]
        ya16, yb16 = ya.astype(BF16), yb.astype(BF16)
        merged = (_sigmoid(ga_ref[...]) * _dot(ya16, wpa_ref[...])
                  + _sigmoid(gb_ref[...]) * _dot(yb16, wpb_ref[...])).astype(BF16)
        x_mix = _dot(merged, wout_ref[...])
        x1_ref[...] = x_ref[...] + g1_ref[...] * x_mix
        xmix_ref[...] = x_mix
        merged_ref[...] = merged
        ya_ref[...] = ya16
        yb_ref[...] = yb16

    vec = pl.BlockSpec((1, D), lambda i: (0, 0))

    def full(a):
        return pl.BlockSpec(a.shape, lambda i: (0, 0), pipeline_mode=pl.Buffered(1))

    return pl.pallas_call(
        body, name=name,
        grid=(L // MIX_ROWS,),
        in_specs=[t(D), t(D), t(2 * D), t(2 * D), t(D, COL_HG // 8), t(D, COL_RG // 8), t(D, COL_RG // 8 + 1),
                  t(D, COL_GA // 8), t(D, COL_GB // 8), t(D), vec, vec, full(w_pa), full(w_pb), full(w_out)],
        out_specs=[t(D), t(D), t(D), t(D), t(2 * D)],
        out_shape=[jax.ShapeDtypeStruct((L, D), F32), jax.ShapeDtypeStruct((L, D), F32),
                   jax.ShapeDtypeStruct((L, D), BF16), jax.ShapeDtypeStruct((L, D), BF16),
                   jax.ShapeDtypeStruct((L, 2 * D), BF16)],
        compiler_params=_params("parallel"),
    )(ohf, ohb, orf, orb, p, p, p, p, p, x, g1, hgw, w_pa, w_pb, w_out)


def mix_bwd(dx1, x_mix, ya, yb, ohf, ohb, orf, orb, p, g1, hgw, w_pa, w_pb, w_out, name):
    L = dx1.shape[0]
    t = _mix_specs(MIX_BWD_ROWS)

    def body(dx1_ref, xmix_ref, ya_ref, yb_ref, ohf_ref, ohb_ref, orf_ref, orb_ref, hg_ref, rg0_ref, rg1_ref,
             ga_ref, gb_ref, g1_ref, hgw_ref, wpa_ref, wpb_ref, wout_ref,
             dxm_ref, da_ref, db_ref, dga_ref, dgb_ref, dhg_ref, drg_ref, dohg_ref, dort_ref, sums_ref):
        @pl.when(pl.program_id(0) == 0)
        def _():
            sums_ref[...] = jnp.zeros_like(sums_ref)

        dx1 = dx1_ref[...]
        dxm = (g1_ref[...] * dx1).astype(BF16)
        dxm_ref[...] = dxm
        dmerged = _dot(dxm, wout_ref[...], 1, 1)
        a = _dot(ya_ref[...], wpa_ref[...])
        bm = _dot(yb_ref[...], wpb_ref[...])
        sa, sb = _sigmoid(ga_ref[...]), _sigmoid(gb_ref[...])
        d_a = (dmerged * sa).astype(BF16)
        d_b = (dmerged * sb).astype(BF16)
        da_ref[...] = d_a
        db_ref[...] = d_b
        dga_ref[...] = (dmerged * a * sa * (1.0 - sa)).astype(BF16)
        dgb_ref[...] = (dmerged * bm * sb * (1.0 - sb)).astype(BF16)
        dya = _dot(d_a, wpa_ref[...], 1, 1)
        dyb = _dot(d_b, wpb_ref[...], 1, 1)

        hgw = hgw_ref[...]
        silu_h, dsilu_h = _silu_parts(hg_ref[...])
        nh, rh = _head_rms(ohf_ref[...] + ohb_ref[...])
        n = jnp.concatenate(nh, axis=1)
        dhg_ref[...] = (dya * n * hgw * dsilu_h).astype(BF16)
        dn = dya * hgw * silu_h
        douts = []
        for h in range(HEADS):
            dnh = dn[:, h * HG_D:(h + 1) * HG_D]
            douts.append(rh[h] * (dnh - nh[h] * _lanemean(dnh * nh[h])))
        dohg_ref[...] = jnp.concatenate(douts, axis=1)

        rg = jnp.concatenate([rg0_ref[...], rg1_ref[...]], axis=1)
        silu_r, dsilu_r = _silu_parts(rg)
        gn, rr = _group_norm(orf_ref[...] + orb_ref[...])
        g = jnp.concatenate(gn, axis=1)
        drg_ref[...] = (dyb * g * dsilu_r).astype(BF16)
        dgn = dyb * silu_r
        douts = []
        for h in range(HEADS):
            dgh = dgn[:, h * RT_DV:(h + 1) * RT_DV]
            douts.append(rr[h] * (dgh - _lanemean(dgh) - gn[h] * _lanemean(dgh * gn[h])))
        dort_ref[...] = jnp.concatenate(douts, axis=1)

        sums_ref[0:1, :] += _rowsum(dx1 * xmix_ref[...])
        sums_ref[1:2, :] += _rowsum(dya * n * silu_h)

    vec = pl.BlockSpec((1, D), lambda i: (0, 0))

    def full(a):
        return pl.BlockSpec(a.shape, lambda i: (0, 0), pipeline_mode=pl.Buffered(1))

    bf = functools.partial(jax.ShapeDtypeStruct, dtype=BF16)
    return pl.pallas_call(
        body, name=name,
        grid=(L // MIX_BWD_ROWS,),
        in_specs=[t(D), t(D), t(D), t(2 * D), t(D), t(D), t(2 * D), t(2 * D),
                  t(D, COL_HG // 8), t(D, COL_RG // 8), t(D, COL_RG // 8 + 1), t(D, COL_GA // 8), t(D, COL_GB // 8),
                  vec, vec, full(w_pa), full(w_pb), full(w_out)],
        out_specs=[t(D), t(D), t(D), t(D), t(D), t(D), t(2 * D), t(D), t(2 * D),
                   pl.BlockSpec((8, D), lambda i: (0, 0))],
        out_shape=[bf((L, D)), bf((L, D)), bf((L, D)), bf((L, D)), bf((L, D)), bf((L, D)), bf((L, 2 * D)),
                   jax.ShapeDtypeStruct((L, D), F32), jax.ShapeDtypeStruct((L, 2 * D), F32),
                   jax.ShapeDtypeStruct((8, D), F32)],
        compiler_params=_params("arbitrary"),
    )(dx1, x_mix, ya, yb, ohf, ohb, orf, orb, p, p, p, p, p, g1, hgw, w_pa, w_pb, w_out)


FFN_ROWS = 512


def ffn_fwd(x1, target, nw2, sh2, sc2, g2, fw, wg, wu, wd, name):
    L = x1.shape[0]
    tm = min(FFN_ROWS, L)

    def body(x1_ref, tgt_ref, nw2_ref, sh2_ref, sc2_ref, g2_ref, fw_ref, wg_ref, wu_ref, wd_ref,
             hx2_ref, g_ref, u_ref, h_ref, f_ref, dx2_ref, sums_ref, hx_scr, acc):
        i, j = pl.program_id(0), pl.program_id(1)

        @pl.when((i == 0) & (j == 0))
        def _():
            sums_ref[...] = jnp.zeros_like(sums_ref)

        @pl.when(j == 0)
        def _():
            xv = x1_ref[...]
            n = xv * lax.rsqrt(_lanemean(xv * xv) + EPS) * nw2_ref[...]
            h = (n * (1.0 + sc2_ref[...]) + sh2_ref[...]).astype(BF16)
            hx_scr[...] = h
            hx2_ref[...] = h
            acc[...] = jnp.zeros_like(acc)

        hx = hx_scr[...]
        g = _dot(hx, wg_ref[...])
        u = _dot(hx, wu_ref[...])
        hh = (_silu_parts(g)[0] * u).astype(BF16)
        g_ref[...] = g
        u_ref[...] = u
        h_ref[...] = hh
        acc[...] += _dot(hh, wd_ref[...])

        @pl.when(j == N_SHARD - 1)
        def _():
            f = acc[...]
            f_ref[...] = f
            x2 = x1_ref[...] + g2_ref[...] * f
            r = lax.rsqrt(_lanemean(x2 * x2) + EPS)
            fw = fw_ref[...]
            e = x2 * r * fw - tgt_ref[...]
            dy = e * (1.0 / D)
            dyw = dy * fw
            dx2_ref[...] = r * dyw - x2 * (r * r * r) * _lanemean(dyw * x2)
            sums_ref[0:1, :] += _rowsum(dy * x2 * r)
            sums_ref[1:2, :] += _rowsum(e * e) * (0.5 / D)

    row = pl.BlockSpec((tm, D), lambda i, j: (i, 0))
    vec = pl.BlockSpec((1, D), lambda i, j: (0, 0))
    sh = pl.BlockSpec((None, tm, FF_SH), lambda i, j: (j, i, 0))
    return pl.pallas_call(
        body, name=name,
        grid=(L // tm, N_SHARD),
        in_specs=[row, row, vec, vec, vec, vec, vec,
                  pl.BlockSpec((None, D, FF_SH), lambda i, j: (j, 0, 0)),
                  pl.BlockSpec((None, D, FF_SH), lambda i, j: (j, 0, 0)),
                  pl.BlockSpec((None, FF_SH, D), lambda i, j: (j, 0, 0))],
        out_specs=[row, sh, sh, sh, row, row, pl.BlockSpec((8, D), lambda i, j: (0, 0))],
        out_shape=[jax.ShapeDtypeStruct((L, D), BF16),
                   jax.ShapeDtypeStruct((N_SHARD, L, FF_SH), F32), jax.ShapeDtypeStruct((N_SHARD, L, FF_SH), F32),
                   jax.ShapeDtypeStruct((N_SHARD, L, FF_SH), BF16),
                   jax.ShapeDtypeStruct((L, D), F32), jax.ShapeDtypeStruct((L, D), F32),
                   jax.ShapeDtypeStruct((8, D), F32)],
        scratch_shapes=[pltpu.VMEM((tm, D), BF16), pltpu.VMEM((tm, D), F32)],
        compiler_params=_params("arbitrary", "arbitrary"),
    )(x1, target, nw2, sh2, sc2, g2, fw, wg, wu, wd)


def ffn_bwd(dx2, x1, f, g, u, nw2, sc2, g2, wg, wu, wd, name):
    L = x1.shape[0]
    tm = min(FFN_ROWS, L)

    def body(dx2_ref, x1_ref, f_ref, g_ref, u_ref, nw2_ref, sc2_ref, g2_ref, wg_ref, wu_ref, wd_ref,
             df_ref, dg_ref, du_ref, dx1_ref, sums_ref, df_scr, acc):
        i, j = pl.program_id(0), pl.program_id(1)

        @pl.when((i == 0) & (j == 0))
        def _():
            sums_ref[...] = jnp.zeros_like(sums_ref)

        @pl.when(j == 0)
        def _():
            dx2 = dx2_ref[...]
            df = (g2_ref[...] * dx2).astype(BF16)
            df_scr[...] = df
            df_ref[...] = df
            sums_ref[0:1, :] += _rowsum(dx2 * f_ref[...])
            acc[...] = jnp.zeros_like(acc)

        dh = _dot(df_scr[...], wd_ref[...], 1, 1)
        gv, uv = g_ref[...], u_ref[...]
        silu_g, dsilu_g = _silu_parts(gv)
        dg = (dh * uv * dsilu_g).astype(BF16)
        du = (dh * silu_g).astype(BF16)
        dg_ref[...] = dg
        du_ref[...] = du
        acc[...] += _dot(dg, wg_ref[...], 1, 1) + _dot(du, wu_ref[...], 1, 1)

        @pl.when(j == N_SHARD - 1)
        def _():
            dhx = acc[...]
            xv = x1_ref[...]
            r = lax.rsqrt(_lanemean(xv * xv) + EPS)
            n0 = xv * r
            nw = nw2_ref[...]
            dn2 = dhx * (1.0 + sc2_ref[...])
            dn0 = dn2 * nw
            dx1_ref[...] = dx2_ref[...] + r * (dn0 - n0 * _lanemean(dn0 * n0))
            sums_ref[1:2, :] += _rowsum(dhx)
            sums_ref[2:3, :] += _rowsum(dhx * n0 * nw)
            sums_ref[3:4, :] += _rowsum(dn2 * n0)

    row = pl.BlockSpec((tm, D), lambda i, j: (i, 0))
    vec = pl.BlockSpec((1, D), lambda i, j: (0, 0))
    sh = pl.BlockSpec((None, tm, FF_SH), lambda i, j: (j, i, 0))
    return pl.pallas_call(
        body, name=name,
        grid=(L // tm, N_SHARD),
        in_specs=[row, row, row, sh, sh, vec, vec, vec,
                  pl.BlockSpec((None, D, FF_SH), lambda i, j: (j, 0, 0)),
                  pl.BlockSpec((None, D, FF_SH), lambda i, j: (j, 0, 0)),
                  pl.BlockSpec((None, FF_SH, D), lambda i, j: (j, 0, 0))],
        out_specs=[row, sh, sh, row, pl.BlockSpec((8, D), lambda i, j: (0, 0))],
        out_shape=[jax.ShapeDtypeStruct((L, D), BF16),
                   jax.ShapeDtypeStruct((N_SHARD, L, FF_SH), BF16), jax.ShapeDtypeStruct((N_SHARD, L, FF_SH), BF16),
                   jax.ShapeDtypeStruct((L, D), F32), jax.ShapeDtypeStruct((8, D), F32)],
        scratch_shapes=[pltpu.VMEM((tm, D), BF16), pltpu.VMEM((tm, D), F32)],
        compiler_params=_params("arbitrary", "arbitrary"),
    )(dx2, x1, f, g, u, nw2, sc2, g2, wg, wu, wd)


def matmul_tn(a, b, name, acc_init=None, to_chips=()):
    na, K, M = a.shape
    nb, _, N = b.shape
    n = max(na, nb)
    tk = min(512, K)
    tn = N if N <= 1024 else N // 2
    nk = K // tk
    grid = (n, N // tn, nk)
    has_init = acc_init is not None
    nx = len(to_chips)

    def body(a_ref, b_ref, *refs):
        init_ref = refs[0] if has_init else None
        refs = refs[1:] if has_init else refs
        o_ref = refs[nx]
        if nx:
            start, finish = _to_chips_phases(refs[:nx], refs[nx + 1:2 * nx + 1], *refs[2 * nx + 1:])
            pos, total = _grid_step(grid)
            pl.when(pos == 0)(start)
        kk = pl.program_id(2)

        @pl.when(kk == 0)
        def _():
            o_ref[...] = init_ref[...] if has_init else jnp.zeros_like(o_ref)

        o_ref[...] += _dot(a_ref[...], b_ref[...], 0, 0)
        if nx:
            pl.when(pos == total - 1)(finish)

    out_spec = pl.BlockSpec((None, M, tn), lambda s, j, kk: (s, 0, j))
    in_specs = [pl.BlockSpec((None, tk, M), lambda s, j, kk: (s if na > 1 else 0, kk, 0)),
                pl.BlockSpec((None, tk, tn), lambda s, j, kk: (s if nb > 1 else 0, kk, j))]
    args = [a, b]
    if has_init:
        in_specs.append(out_spec)
        args.append(acc_init)
    out = pl.pallas_call(
        body, name=name,
        grid=grid,
        in_specs=in_specs + [ANY] * nx,
        out_specs=[out_spec] + [ANY] * nx,
        out_shape=[jax.ShapeDtypeStruct((n, M, N), F32)] + _to_chips_shapes(to_chips),
        scratch_shapes=_to_chips_scratch(nx) if nx else [],
        compiler_params=_params(*(("arbitrary",) * 3 if nx else ("parallel", "parallel", "arbitrary"))),
    )(*args, *to_chips)
    return out if nx else out[0]


def matmul_tn_pair(a, b1, b2, name):
    K, M = a.shape
    n, _, N = b1.shape
    tk = min(512, K)

    def body(a_ref, b1_ref, b2_ref, o1_ref, o2_ref):
        @pl.when(pl.program_id(1) == 0)
        def _():
            o1_ref[...] = jnp.zeros_like(o1_ref)
            o2_ref[...] = jnp.zeros_like(o2_ref)

        at = a_ref[...].T
        o1_ref[...] += _dot(at, b1_ref[...])
        o2_ref[...] += _dot(at, b2_ref[...])

    b_spec = pl.BlockSpec((None, tk, N), lambda s, kk: (s, kk, 0))
    o_spec = pl.BlockSpec((None, M, N), lambda s, kk: (s, 0, 0))
    return pl.pallas_call(
        body, name=name,
        grid=(n, K // tk),
        in_specs=[pl.BlockSpec((tk, M), lambda s, kk: (kk, 0)), b_spec, b_spec],
        out_specs=[o_spec, o_spec],
        out_shape=[jax.ShapeDtypeStruct((n, M, N), F32)] * 2,
        compiler_params=_params("parallel", "arbitrary"),
    )(a, b1, b2)


PIECE_COLS = 1024
N_PIECE_BLOCKS = D_IN // PIECE_COLS
DX_ROWS = 512


def _piece_blocks(pieces):
    out, col = [], 0
    for arr, width in pieces:
        if arr is not None:
            out.append((arr, col // PIECE_COLS, width // PIECE_COLS))
        col += width
    assert col == D_IN
    return out


def matmul_tn_pieces(a, pieces, name, acc_init=None, to_chips=()):
    K, M = a.shape
    blocks = _piece_blocks(pieces)
    tk = min(512, K)
    grid = (N_PIECE_BLOCKS, K // tk)
    has_init = acc_init is not None
    nx, npc = len(to_chips), len(blocks)

    def body(a_ref, *refs):
        p_refs = refs[:npc]
        refs = refs[npc:]
        init_ref = refs[0] if has_init else None
        refs = refs[1:] if has_init else refs
        o_ref = refs[nx]
        if nx:
            start, finish = _to_chips_phases(refs[:nx], refs[nx + 1:2 * nx + 1], *refs[2 * nx + 1:])
            pos, total = _grid_step(grid)
            pl.when(pos == 0)(start)
        blk, kk = pl.program_id(0), pl.program_id(1)

        @pl.when(kk == 0)
        def _():
            o_ref[...] = init_ref[...] if has_init else jnp.zeros_like(o_ref)

        for p_ref, (_, b0, nb) in zip(p_refs, blocks):
            @pl.when((blk >= b0) & (blk < b0 + nb))
            def _(p_ref=p_ref):
                o_ref[...] += _dot(a_ref[...], p_ref[...], 0, 0)

        if nx:
            pl.when(pos == total - 1)(finish)

    def piece_spec(b0, nb):
        def index(blk, kk):
            mine = (blk >= b0) & (blk < b0 + nb)
            return jnp.where(mine, kk, 0), jnp.clip(blk - b0, 0, nb - 1)
        return pl.BlockSpec((tk, PIECE_COLS), index)

    out_spec = pl.BlockSpec((M, PIECE_COLS), lambda blk, kk: (0, blk))
    in_specs = [pl.BlockSpec((tk, M), lambda blk, kk: (kk, 0))] + [piece_spec(b0, nb) for _, b0, nb in blocks]
    args = [a] + [arr for arr, _, _ in blocks]
    if has_init:
        in_specs.append(out_spec)
        args.append(acc_init)
    out = pl.pallas_call(
        body, name=name,
        grid=grid,
        in_specs=in_specs + [ANY] * nx,
        out_specs=[out_spec] + [ANY] * nx,
        out_shape=[jax.ShapeDtypeStruct((M, D_IN), F32)] + _to_chips_shapes(to_chips),
        scratch_shapes=_to_chips_scratch(nx) if nx else [],
        compiler_params=_params("arbitrary", "arbitrary"),
    )(*args, *to_chips)
    return out if nx else out[0]


def dhx_normbwd(pieces, w, x, dx_res, nw, sc, name, to_chips=()):
    L = x.shape[0]
    tm = min(DX_ROWS, L)
    blocks = _piece_blocks(pieces)
    grid = (L // tm, N_PIECE_BLOCKS)
    nx, npc = len(to_chips), len(blocks)

    def body(*refs):
        p_refs = refs[:npc]
        w_ref, x_ref, res_ref, nw_ref, sc_ref = refs[npc:npc + 5]
        refs = refs[npc + 5:]
        dx_ref, sums_ref = refs[nx:nx + 2]
        acc = refs[2 * nx + 2]
        if nx:
            start, finish = _to_chips_phases(refs[:nx], refs[nx + 2:2 * nx + 2], *refs[2 * nx + 3:])
            pos, total = _grid_step(grid)
            pl.when(pos == 0)(start)
            pl.when(pos == total - 1)(finish)
        i, blk = pl.program_id(0), pl.program_id(1)

        @pl.when((i == 0) & (blk == 0))
        def _():
            sums_ref[...] = jnp.zeros_like(sums_ref)

        @pl.when(blk == 0)
        def _():
            acc[...] = jnp.zeros_like(acc)

        for p_ref, (_, b0, nb) in zip(p_refs, blocks):
            @pl.when((blk >= b0) & (blk < b0 + nb))
            def _(p_ref=p_ref):
                acc[...] += _dot(p_ref[...], w_ref[...], 1, 1)

        @pl.when(blk == N_PIECE_BLOCKS - 1)
        def _():
            dhx = acc[...]
            xv = x_ref[...]
            r = lax.rsqrt(_lanemean(xv * xv) + EPS)
            n0 = xv * r
            nw = nw_ref[...]
            dn = dhx * (1.0 + sc_ref[...])
            dn0 = dn * nw
            dx_ref[...] = res_ref[...] + r * (dn0 - n0 * _lanemean(dn0 * n0))
            sums_ref[0:1, :] += _rowsum(dhx)
            sums_ref[1:2, :] += _rowsum(dhx * n0 * nw)
            sums_ref[2:3, :] += _rowsum(dn * n0)

    def piece_spec(b0, nb):
        return pl.BlockSpec((tm, PIECE_COLS), lambda i, blk: (i, jnp.clip(blk - b0, 0, nb - 1)))

    row = pl.BlockSpec((tm, D), lambda i, blk: (i, 0))
    vec = pl.BlockSpec((1, D), lambda i, blk: (0, 0))
    return pl.pallas_call(
        body, name=name,
        grid=grid,
        in_specs=[piece_spec(b0, nb) for _, b0, nb in blocks]
        + [pl.BlockSpec((D, PIECE_COLS), lambda i, blk: (0, blk)), row, row, vec, vec] + [ANY] * nx,
        out_specs=[row, pl.BlockSpec((8, D), lambda i, blk: (0, 0))] + [ANY] * nx,
        out_shape=[jax.ShapeDtypeStruct((L, D), F32), jax.ShapeDtypeStruct((8, D), F32)] + _to_chips_shapes(to_chips),
        scratch_shapes=[pltpu.VMEM((tm, D), F32)] + (_to_chips_scratch(nx) if nx else []),
        compiler_params=_params("arbitrary", "arbitrary"),
    )(*[arr for arr, _, _ in blocks], w, x, dx_res, nw, sc, *to_chips)


SMALL_ROWS = 24


def _rope_tables(L):
    rows = L // 64
    row = jnp.repeat(jnp.arange(rows, dtype=F32), 64)
    col = jnp.tile(jnp.arange(64, dtype=F32), rows)
    freqs = 10000.0 ** (-jnp.arange(RT_DK // 4, dtype=F32) / (RT_DK // 4))
    ang = jnp.concatenate([row[:, None] * freqs, col[:, None] * freqs], axis=-1)
    cos, sin = jnp.cos(ang), jnp.sin(ang)
    return jnp.concatenate([cos, cos], axis=1), jnp.concatenate([-sin, sin], axis=1)


def _pieces(hq, hf_f, hf_b, hi, hg, rq, rk, rv, rg, ga, gb):
    widths = (D, D, D, D, D, D, D, 2 * D, 2 * D, D, D)
    return list(zip((hq, hf_f, hf_b, hi, hg, rq, rk, rv, rg, ga, gb), widths))


def _lane0(a):
    return a[:, 0, 0]


def _pack_small(rows):
    out = [r.reshape(1, D) for r in rows]
    out += [jnp.zeros((1, D), F32)] * (SMALL_ROWS - len(out))
    return jnp.concatenate(out, axis=0)


def _sibling_sums(gs, names, place):
    core, core_arg, _ = place

    def other_half(g):
        if g.ndim == 2:
            g = g.reshape(g.shape[0], N_SHARD, g.shape[1] // N_SHARD).transpose(1, 0, 2)
        h = g.shape[1] // 2
        return lax.dynamic_slice_in_dim(g, (1 - core) * h, h, axis=1).astype(BF16)

    payload = [other_half(g) for g in gs]
    received = rs_to_sibling(payload, "rs_to_sibling_" + names[0])
    return [rs_add_sibling(g, r, core_arg, "rs_add_sibling_" + k) for g, r, k in zip(gs, received, names)]


def local_step(x, ctx, target, mod_x, mod_c, lb_f, lb_b, lg_f, lg_b, nw1, nw2, hgw, fw, w, rest=None, place=None):
    L, Lc = x.shape[0], ctx.shape[0]
    sh1, sc1, g1, sh2, sc2, g2 = (mod_x[i:i + 1] for i in range(6))
    sh1c, sc1c = mod_c[0:1], mod_c[1:2]
    cosf, sinf = _rope_tables(L)
    cosc, sinc = jnp.ones((Lc, RT_DK), F32), jnp.zeros((Lc, RT_DK), F32)
    zero_h = jnp.zeros((HEADS, HG_D, HG_D), F32)
    zero_r = jnp.zeros((HEADS, RT_DV, RT_DK), F32)

    pc, hxc = normmod_matmul(ctx, nw1, sh1c, sc1c, w["w_in"], "ctx_in_proj")
    _, s_hf, cb_hf = hgrn_scan_fwd(pc, lb_f, zero_h, COL_HFF, False, "ctx_hgrn_f")
    _, s_hb, cb_hb = hgrn_scan_fwd(pc, lb_b, zero_h, COL_HFB, True, "ctx_hgrn_b")
    _, s_rf, cb_rf = ret_scan_fwd(pc, cosc, sinc, lg_f, zero_r, False, "ctx_ret_f")
    _, s_rb, cb_rb = ret_scan_fwd(pc, cosc, sinc, lg_b, zero_r, True, "ctx_ret_b")
    if rest is None:
        p, hx = normmod_matmul(x, nw1, sh1, sc1, w["w_in"], "in_proj")
    else:
        p, hx, g_pa, g_pb, g_out, g_wg, g_wu, g_wd = normmod_matmul(x, nw1, sh1, sc1, w["w_in"], "in_proj", gather=rest)
        w = dict(w, w_pa=g_pa.reshape(D, D), w_pb=g_pb.reshape(2 * D, D), w_out=g_out.reshape(D, D),
                 wg=g_wg, wu=g_wu, wd=g_wd)
    ohf, _, xb_hf = hgrn_scan_fwd(p, lb_f, s_hf, COL_HFF, False, "hgrn_f")
    ohb, _, xb_hb = hgrn_scan_fwd(p, lb_b, s_hb, COL_HFB, True, "hgrn_b")
    orf, _, xb_rf = ret_scan_fwd(p, cosf, sinf, lg_f, s_rf, False, "ret_f")
    orb, _, xb_rb = ret_scan_fwd(p, cosf, sinf, lg_b, s_rb, True, "ret_b")
    x1, x_mix, merged, ya, yb = mix_fwd(ohf, ohb, orf, orb, p, x, g1, hgw, w["w_pa"], w["w_pb"], w["w_out"], "mix_fwd")
    hx2, gg, uu, hh, ff, dx2, sums_f = ffn_fwd(x1, target, nw2, sh2, sc2, g2, fw, w["wg"], w["wu"], w["wd"], "ffn_fwd")

    d_f, d_g, d_u, dx1, sums_fb = ffn_bwd(dx2, x1, ff, gg, uu, nw2, sc2, g2, w["wg"], w["wu"], w["wd"], "ffn_bwd")
    dw_gate, dw_up = matmul_tn_pair(hx2, d_g, d_u, "dw_ffn_gate_up")
    grads = {"wg": dw_gate, "wu": dw_up, "wd": matmul_tn(hh, d_f[None], "dw_ffn_down")}
    dxm, d_a, d_b, dga, dgb, dhg, drg, dohg, dort, sums_m = mix_bwd(
        dx1, x_mix, ya, yb, ohf, ohb, orf, orb, p, g1, hgw, w["w_pa"], w["w_pb"], w["w_out"], "mix_bwd")
    grads["w_out"] = matmul_tn(merged[None], dxm[None], "dw_out").reshape(N_SHARD, D // N_SHARD, D)
    grads["w_pa"] = matmul_tn(ya[None], d_a[None], "dw_proj_hgrn").reshape(N_SHARD, D // N_SHARD, D)
    grads["w_pb"] = matmul_tn(yb[None], d_b[None], "dw_proj_ret").reshape(N_SHARD, 2 * D // N_SHARD, D)

    rq1, rk1, rv1, dlgf_x, ds_rf = ret_scan_bwd(p, cosf, sinf, lg_f, xb_rf, dort, zero_r, None, False, "ret_f_bwd")
    drq, drk, drv, dlgb_x, ds_rb = ret_scan_bwd(p, cosf, sinf, lg_b, xb_rb, dort, zero_r, (rq1, rk1, rv1), True, "ret_b_bwd")
    hq1, dzf, hv1, dlbf_x, ds_hf = hgrn_scan_bwd(p, lb_f, xb_hf, dohg, zero_h, None, COL_HFF, False, "hgrn_f_bwd")
    dhq, dzb, dhv, dlbb_x, ds_hb = hgrn_scan_bwd(p, lb_b, xb_hb, dohg, zero_h, (hq1, hv1), COL_HFB, True, "hgrn_b_bwd")
    dp = _pieces(dhq, dzf, dzb, dhv, dhg, drq, drk, drv, drg, dga, dgb)
    others = ["w_pa", "w_pb", "w_out", "wg", "wu", "wd"]
    if place is None:
        dw_in = matmul_tn_pieces(hx, dp, "dw_in")
    else:
        sums_o = _sibling_sums([grads[k] for k in others], others, place)
        dw_in, *recv_o = matmul_tn_pieces(hx, dp, "dw_in", to_chips=[a16 for _, a16 in sums_o])

    zc = jnp.zeros((Lc, D), F32)
    zc2 = jnp.zeros((Lc, 2 * D), F32)
    crq1, crk1, crv1, dlgf_c, _ = ret_scan_bwd(pc, cosc, sinc, lg_f, cb_rf, zc2, ds_rf, None, False, "ctx_ret_f_bwd")
    cdrq, cdrk, cdrv, dlgb_c, _ = ret_scan_bwd(pc, cosc, sinc, lg_b, cb_rb, zc2, ds_rb, (crq1, crk1, crv1), True, "ctx_ret_b_bwd")
    chq1, cdzf, chv1, dlbf_c, _ = hgrn_scan_bwd(pc, lb_f, cb_hf, zc, ds_hf, None, COL_HFF, False, "ctx_hgrn_f_bwd")
    cdhq, cdzb, cdhv, dlbb_c, _ = hgrn_scan_bwd(pc, lb_b, cb_hb, zc, ds_hb, (chq1, chv1), COL_HFB, True, "ctx_hgrn_b_bwd")
    dpc = _pieces(cdhq, cdzf, cdzb, cdhv, None, cdrq, cdrk, cdrv, None, None, None)
    _, sums_c = dhx_normbwd(dpc, w["w_in"], ctx, zc, nw1, sc1c, "dctx_in_proj")
    grads["w_in"] = matmul_tn_pieces(hxc, dpc, "dw_in_ctx", acc_init=dw_in)
    if place is None:
        dx, sums_x = dhx_normbwd(dp, w["w_in"], x, dx1, nw1, sc1, "dx_in_proj")
    else:
        sums_i = _sibling_sums([grads["w_in"]], ["w_in"], place)
        dx, sums_x, recv_i = dhx_normbwd(dp, w["w_in"], x, dx1, nw1, sc1, "dx_in_proj", to_chips=[sums_i[0][1]])
        names = ["w_in"] + others
        halves = [rs_add_chips(a, r, place[2], "rs_add_chips_" + k)
                  for (a, _), r, k in zip(sums_i + sums_o, [recv_i] + recv_o, names)]
        grads = dict(zip(names, rs_join_halves(halves, "rs_join_halves")))

    def lg_row(f, b):
        return jnp.concatenate([_lane0(f), _lane0(b), jnp.zeros((D - 2 * HEADS,), F32)])

    small = _pack_small([
        sums_x[0], sums_x[1], sums_m[0], sums_fb[1], sums_fb[2], sums_fb[0],
        sums_c[0], sums_c[1],
        sums_x[2], sums_c[2], sums_fb[3], sums_m[1], sums_f[0],
        dlbf_x, dlbf_c, dlbb_x, dlbb_c,
        lg_row(dlgf_x, dlgb_x), lg_row(dlgf_c, dlgb_c),
        sums_f[1],
    ])
    return dx, grads, small


MESH = pl.DeviceIdType.MESH
ANY = pl.BlockSpec(memory_space=pl.ANY)
N_DEV = 8


def _place():
    return lax.axis_index("x"), lax.axis_index("y"), lax.axis_index("c")


def _other_chips(x, y):
    return [(1 - x, y), (x, 1 - y), (1 - x, 1 - y)]


def allgather8(xs, name):
    m, n = xs.shape

    def body(x_ref, out_ref, send_sems, recv_sems, local_sem):
        x, y, c = _place()
        me, sibling = (x, y, c), (x, y, 1 - c)
        chips = _other_chips(x, y)

        def rows(px, py, pc):
            return out_ref.at[pl.ds((4 * px + 2 * py + pc) * m, m), :]

        def copy(k, block, to, src=None):
            return pltpu.make_async_remote_copy(
                src_ref=rows(*block) if src is None else src, dst_ref=rows(*block),
                send_sem=send_sems.at[k], recv_sem=recv_sems.at[k], device_id=to, device_id_type=MESH)

        mine = pltpu.make_async_copy(x_ref, rows(*me), local_sem)
        mine.start()
        first = [copy(0, me, sibling, src=x_ref)]
        first += [copy(1 + j, me, (*chip, c), src=x_ref) for j, chip in enumerate(chips)]
        for cp in first:
            cp.start()
        passed = [copy(4 + j, (*chip, c), sibling) for j, chip in enumerate(chips)]
        for j, chip in enumerate(chips):
            copy(1 + j, (*chip, c), me).wait_recv()
            passed[j].start()
        copy(0, sibling, me).wait_recv()
        for j, chip in enumerate(chips):
            copy(4 + j, (*chip, 1 - c), me).wait_recv()
        for cp in first + passed:
            cp.wait_send()
        mine.wait()

    return pl.pallas_call(
        body, name=name,
        out_shape=jax.ShapeDtypeStruct((N_DEV * m, n), xs.dtype),
        in_specs=[pl.BlockSpec(memory_space=pltpu.VMEM)],
        out_specs=pl.BlockSpec(memory_space=pltpu.VMEM),
        scratch_shapes=[pltpu.SemaphoreType.DMA((7,)), pltpu.SemaphoreType.DMA((7,)), pltpu.SemaphoreType.DMA],
    )(xs)


def _gather_phases(ins, outs, send_sems, recv_sems, local_sems):
    n = len(ins)
    x, y, c = _place()
    chips = _other_chips(x, y)

    def rows(i, core):
        h = ins[i].shape[0] // 2
        return pl.ds(pl.multiple_of(core * h, 16), h)

    def region(i, k, rs):
        if len(outs[i].shape) == 2:
            cols = ins[i].shape[1]
            return outs[i].at[rs, pl.ds(pl.multiple_of(k * cols, 128), cols)]
        return outs[i].at[k, rs, :]

    def landed(i, chip, core):
        return region(i, 2 * chip[0] + chip[1], rows(i, core))

    def copy(i, k, src, dst, to):
        return pltpu.make_async_remote_copy(src_ref=src, dst_ref=dst, send_sem=send_sems.at[6 * i + k],
                                            recv_sem=recv_sems.at[6 * i + k], device_id=to, device_id_type=MESH)

    def local(i):
        r = ins[i].shape[0] // LOCAL_PIECES
        return [pltpu.make_async_copy(ins[i].at[pl.ds(q * r, r), :], region(i, 2 * x + y, pl.ds(q * r, r)),
                                      local_sems.at[LOCAL_PIECES * i + q]) for q in range(LOCAL_PIECES)]

    def send(i, j):
        return copy(i, j, ins[i].at[rows(i, c), :], landed(i, (x, y), c), (*chips[j], c))

    def arrived(i, j, core, k):
        return copy(i, k, ins[i].at[rows(i, core), :], landed(i, chips[j], core), (x, y, 1 - c))

    def passed(i, j):
        return copy(i, 3 + j, landed(i, chips[j], c), landed(i, chips[j], c), (x, y, 1 - c))

    def start():
        for i in range(n):
            for lc in local(i):
                lc.start()
            for j in range(3):
                send(i, j).start()

    def forward():
        for i in range(n):
            for j in range(3):
                arrived(i, j, c, j).wait_recv()
                passed(i, j).start()

    def finish():
        for i in range(n):
            for j in range(3):
                arrived(i, j, 1 - c, 3 + j).wait_recv()
        for i in range(n):
            for j in range(3):
                send(i, j).wait_send()
                passed(i, j).wait_send()
            for lc in local(i):
                lc.wait()

    return start, forward, finish


LOCAL_PIECES = 4


def _gather_scratch(n):
    return [pltpu.SemaphoreType.DMA((6 * n,)), pltpu.SemaphoreType.DMA((6 * n,)),
            pltpu.SemaphoreType.DMA((LOCAL_PIECES * n,))]


def gather_columns(shard, name):
    def body(in_ref, out_ref, *sems):
        start, forward, finish = _gather_phases([in_ref], [out_ref], *sems)
        start()
        forward()
        finish()

    return pl.pallas_call(
        body, name=name,
        out_shape=jax.ShapeDtypeStruct((shard.shape[0], N_SHARD * shard.shape[1]), shard.dtype),
        in_specs=[ANY], out_specs=ANY,
        scratch_shapes=_gather_scratch(1),
    )(shard)


def rs_to_sibling(payloads, name):
    n = len(payloads)

    def body(*refs):
        ins, outs = refs[:n], refs[n:2 * n]
        send_sems, recv_sems = refs[2 * n:]
        x, y, c = _place()
        copies = []
        for i in range(n):
            cp = pltpu.make_async_remote_copy(src_ref=ins[i], dst_ref=outs[i], send_sem=send_sems.at[i],
                                              recv_sem=recv_sems.at[i], device_id=(x, y, 1 - c), device_id_type=MESH)
            cp.start()
            copies.append(cp)
        for cp in copies:
            cp.wait()

    return pl.pallas_call(
        body, name=name,
        out_shape=[jax.ShapeDtypeStruct(g.shape, g.dtype) for g in payloads],
        in_specs=[ANY] * n, out_specs=[ANY] * n,
        scratch_shapes=[pltpu.SemaphoreType.DMA((n,)), pltpu.SemaphoreType.DMA((n,))],
    )(*payloads)


def _to_chips_phases(ins, outs, send_sems, recv_sems):
    def copies():
        x, y, c = _place()
        return [pltpu.make_async_remote_copy(
            src_ref=ins[i].at[2 * px + py], dst_ref=outs[i].at[j], send_sem=send_sems.at[3 * i + j],
            recv_sem=recv_sems.at[3 * i + j], device_id=(px, py, c), device_id_type=MESH)
            for i in range(len(ins)) for j, (px, py) in enumerate(_other_chips(x, y))]

    def start():
        for cp in copies():
            cp.start()

    def finish():
        for cp in copies():
            cp.wait()

    return start, finish


def _to_chips_shapes(parts):
    return [jax.ShapeDtypeStruct((3,) + a.shape[1:], a.dtype) for a in parts]


def _to_chips_scratch(n):
    return [pltpu.SemaphoreType.DMA((3 * n,)), pltpu.SemaphoreType.DMA((3 * n,))]


def rs_join_halves(fulls, name):
    n = len(fulls)

    def body(*refs):
        outs = refs[n:2 * n]
        send_sems, recv_sems = refs[2 * n:]
        x, y, c = _place()

        def copy(i, core):
            h = fulls[i].shape[0] // 2
            rows = outs[i].at[pl.ds(pl.multiple_of(core * h, 8), h), :]
            return pltpu.make_async_remote_copy(src_ref=rows, dst_ref=rows, send_sem=send_sems.at[i],
                                                recv_sem=recv_sems.at[i], device_id=(x, y, 1 - c), device_id_type=MESH)

        sent = [copy(i, c) for i in range(n)]
        for cp in sent:
            cp.start()
        for i in range(n):
            copy(i, 1 - c).wait_recv()
        for cp in sent:
            cp.wait_send()

    return pl.pallas_call(
        body, name=name,
        out_shape=[jax.ShapeDtypeStruct(a.shape, a.dtype) for a in fulls],
        in_specs=[ANY] * n, out_specs=[ANY] * n,
        input_output_aliases={i: i for i in range(n)},
        scratch_shapes=[pltpu.SemaphoreType.DMA((n,)), pltpu.SemaphoreType.DMA((n,))],
    )(*fulls)


def _row_tile(rows, cols, limit_bytes=2 * 1024 * 1024, mult=8):
    best = mult
    for t in range(mult, rows + 1, mult):
        if rows % t == 0 and t * cols * 4 <= limit_bytes:
            best = t
    return best


def rs_add_sibling(g, recv, c, name):
    _, h, C = recv.shape
    tr = _row_tile(h, C, mult=16)
    nt = h // tr

    def body(c_ref, g_ref, r_ref, o_ref, o16_ref):
        s = g_ref[...] + r_ref[...].astype(F32)
        o_ref[...] = s
        o16_ref[...] = s.astype(BF16)

    blk = pl.BlockSpec((None, tr, C), lambda k, i, c_ref: (k, i, 0))
    if g.ndim == 2:
        g_spec = pl.BlockSpec((tr, C), lambda k, i, c_ref: (c_ref[0] * nt + i, k))
    else:
        g_spec = pl.BlockSpec((None, tr, C), lambda k, i, c_ref: (k, c_ref[0] * nt + i, 0))
    return pl.pallas_call(
        body, name=name,
        grid_spec=pltpu.PrefetchScalarGridSpec(
            num_scalar_prefetch=1, grid=(N_SHARD, nt),
            in_specs=[g_spec, blk],
            out_specs=[blk, blk]),
        out_shape=[jax.ShapeDtypeStruct((N_SHARD, h, C), F32), jax.ShapeDtypeStruct((N_SHARD, h, C), BF16)],
        compiler_params=_params("parallel", "parallel"),
    )(c, g, recv)


def rs_add_chips(part, recv, place, name):
    _, h, C = part.shape
    tr = _row_tile(h, C, mult=16)
    nt = h // tr

    def body(k_ref, p_ref, r_ref, o_ref):
        o_ref[...] = ((p_ref[...] + r_ref[0].astype(F32)) + r_ref[1].astype(F32)) + r_ref[2].astype(F32)

    return pl.pallas_call(
        body, name=name,
        grid_spec=pltpu.PrefetchScalarGridSpec(
            num_scalar_prefetch=1, grid=(nt,),
            in_specs=[pl.BlockSpec((None, tr, C), lambda i, k_ref: (k_ref[0], i, 0)),
                      pl.BlockSpec((3, tr, C), lambda i, k_ref: (0, i, 0))],
            out_specs=pl.BlockSpec((tr, C), lambda i, k_ref: (k_ref[1] * nt + i, 0))),
        out_shape=jax.ShapeDtypeStruct((2 * h, C), F32),
        compiler_params=_params("parallel"),
    )(place, part, recv)


def _adamw_math(w, g, m, v):
    m = ADAM_B1 * m + (1.0 - ADAM_B1) * g
    v = ADAM_B2 * v + (1.0 - ADAM_B2) * (g * g)
    m_hat = m / (1.0 - ADAM_B1 ** ADAM_STEP)
    v_hat = v / (1.0 - ADAM_B2 ** ADAM_STEP)
    delta = -ADAM_LR * (m_hat / (jnp.sqrt(v_hat) + ADAM_EPS) + ADAM_WD * w)
    return delta, m, v


def adamw(w, g, m, v, name):
    R, C = w.shape
    tr = _row_tile(R, C, 1024 * 1024)

    def body(w_ref, g_ref, m_ref, v_ref, d_ref, nm_ref, nv_ref):
        d_ref[...], nm_ref[...], nv_ref[...] = _adamw_math(w_ref[...], g_ref[...], m_ref[...], v_ref[...])

    blk = pl.BlockSpec((tr, C), lambda i: (i, 0))
    return pl.pallas_call(
        body, name=name, grid=(R // tr,), in_specs=[blk] * 4, out_specs=[blk] * 3,
        out_shape=[jax.ShapeDtypeStruct((R, C), F32)] * 3,
        compiler_params=_params("parallel"),
    )(w, g, m, v)


MOD_SH = 6 * D // N_SHARD
PK_ROWS = 16


def mod_fwd(call16, w_sh, b_sh, name):
    def body(c_ref, w_ref, b_ref, o_ref):
        o_ref[...] = _dot(_silu_parts(c_ref[...])[0], w_ref[...], prec=HI) + b_ref[...]

    return pl.pallas_call(body, name=name, out_shape=jax.ShapeDtypeStruct((16, MOD_SH), F32),
                          compiler_params=_params())(call16, w_sh, b_sh)


def prep_small(lbf2, lbb2, theta_row, name):
    def body(f_ref, b_ref, t_ref, lbf_ref, lbb_ref, lg_ref):
        lbf_ref[...] = _sigmoid(f_ref[0:1, :] - f_ref[1:2, :])
        lbb_ref[...] = _sigmoid(b_ref[0:1, :] - b_ref[1:2, :])
        t = t_ref[...]
        lg_ref[...] = jnp.minimum(t, 0.0) - jnp.log(1.0 + jnp.exp(-jnp.abs(t)))

    row = jax.ShapeDtypeStruct((1, D), F32)
    return pl.pallas_call(body, name=name, out_shape=[row, row, row], compiler_params=_params())(lbf2, lbb2, theta_row)


def small_grads(g3, lbf, lbb, theta_row, name):
    def body(g_ref, lbf_ref, lbb_ref, t_ref, pk_ref, aux_ref):
        s = g_ref[0]
        for d in range(1, N_DEV):
            s = s + g_ref[d]
        pk_ref[...] = jnp.zeros_like(pk_ref)
        aux_ref[...] = jnp.zeros_like(aux_ref)
        pk_ref[1:7, :] = s[0:6]
        pk_ref[1:3, :] += s[6:8]
        pk_ref[7:8, :] = s[8:9] + s[9:10]
        pk_ref[8:9, :] = s[10:11]
        lbf, lbb = lbf_ref[...], lbb_ref[...]
        daf = (s[13:14] + s[14:15]) * lbf * (1.0 - lbf)
        dab = (s[15:16] + s[16:17]) * lbb * (1.0 - lbb)
        pk_ref[9:10, :] = daf
        pk_ref[10:11, :] = -daf
        pk_ref[11:12, :] = dab
        pk_ref[12:13, :] = -dab
        pk_ref[13:14, :] = s[11:12]
        pk_ref[14:15, :] = (s[17:18] + s[18:19]) * _sigmoid(-t_ref[...])
        pk_ref[15:16, :] = s[12:13]
        aux_ref[0:2, :] = s[6:8]
        aux_ref[2:3, :] = jnp.broadcast_to(jnp.sum(s[19:20], axis=-1, keepdims=True), (1, D))

    return pl.pallas_call(body, name=name,
                          out_shape=[jax.ShapeDtypeStruct((PK_ROWS, D), F32), jax.ShapeDtypeStruct((8, D), F32)],
                          compiler_params=_params())(g3, lbf, lbb, theta_row)


def mod_bwd(call16, dmod_sh, w_sh, name):
    def body(c_ref, d_ref, w_ref, dw_ref, ds_ref):
        dm = d_ref[...]
        dw_ref[...] = _dot(_silu_parts(c_ref[...])[0], dm, 0, 0, prec=HI)
        ds_ref[...] = jnp.zeros_like(ds_ref)
        ds_ref[0:1, :] = _dot(dm[8:9, :], w_ref[...], 1, 1, prec=HI)

    return pl.pallas_call(body, name=name,
                          out_shape=[jax.ShapeDtypeStruct((D, MOD_SH), F32), jax.ShapeDtypeStruct((8, D), F32)],
                          compiler_params=_params())(call16, dmod_sh, w_sh)


def adamw_small(g4, pk_g, pk_w, pk_m, pk_v, name):
    def body(g4_ref, g_ref, w_ref, m_ref, v_ref, go_ref, d_ref, nm_ref, nv_ref):
        w = w_ref[...]
        ds = ((g4_ref[0:1, :] + g4_ref[16:17, :]) + g4_ref[32:33, :]) + g4_ref[48:49, :]
        row = lax.broadcasted_iota(jnp.int32, (PK_ROWS, D), 0)
        g = jnp.where(row == 0, ds * _silu_parts(w[0:1, :])[1], g_ref[...])
        go_ref[...] = g
        d_ref[...], nm_ref[...], nv_ref[...] = _adamw_math(w, g, m_ref[...], v_ref[...])

    pk = jax.ShapeDtypeStruct((PK_ROWS, D), F32)
    return pl.pallas_call(body, name=name, out_shape=[pk, pk, pk, pk], compiler_params=_params())(g4, pk_g, pk_w, pk_m, pk_v)


def _pack_params(c_ctx, b_mod, n1, n2, lbf, lbb, hgn, th_f, th_b, fin):
    theta = jnp.concatenate([th_f.reshape(HEADS), th_b.reshape(HEADS), jnp.zeros((D - 2 * HEADS,), F32)])
    return jnp.concatenate([c_ctx.reshape(1, D), b_mod.reshape(6, D), n1.reshape(1, D), n2.reshape(1, D), lbf, lbb,
                            hgn.reshape(1, D), theta.reshape(1, D), fin.reshape(1, D)], axis=0)


def _unpack_params(pk):
    return (pk[0], pk[1:7].reshape(1, 6 * D), pk[7:8], pk[8:9], pk[9:11], pk[11:13], pk[13:14],
            pk[14, 0:HEADS].reshape(1, HEADS), pk[14, HEADS:2 * HEADS].reshape(1, HEADS), pk[15])


def kernel(x, c, ctx, c_ctx, w_mod, b_mod, norm1_w, norm2_w, w_in, hg_lb_fwd, hg_lb_bwd, hg_norm_w, rt_theta_fwd, rt_theta_bwd, w_proj_hgrn, w_proj_ret, w_out, w_ffn_gate, w_ffn_up, w_ffn_down, final_norm_w, loss_target, m_c_ctx, m_w_mod, m_b_mod, m_norm1_w, m_norm2_w, m_w_in, m_hg_lb_fwd, m_hg_lb_bwd, m_hg_norm_w, m_rt_theta_fwd, m_rt_theta_bwd, m_w_proj_hgrn, m_w_proj_ret, m_w_out, m_w_ffn_gate, m_w_ffn_up, m_w_ffn_down, m_final_norm_w, v_c_ctx, v_w_mod, v_b_mod, v_norm1_w, v_norm2_w, v_w_in, v_hg_lb_fwd, v_hg_lb_bwd, v_hg_norm_w, v_rt_theta_fwd, v_rt_theta_bwd, v_w_proj_hgrn, v_w_proj_ret, v_w_out, v_w_ffn_gate, v_w_ffn_up, v_w_ffn_down, v_final_norm_w):
    xi, yi, ci = _place()
    dev = 4 * xi + 2 * yi + ci
    chip = 2 * xi + yi
    core_arg = jnp.reshape(ci, (1,)).astype(jnp.int32)
    place_arg = jnp.stack([chip, ci]).astype(jnp.int32)

    c_all = allgather8(jnp.concatenate([c, jnp.zeros((7, D), F32)], axis=0), "gather_c").reshape(N_DEV, 8, D)[:, 0]
    call16 = jnp.concatenate([c_all, c_ctx.reshape(1, D), jnp.zeros((7, D), F32)], axis=0)
    b_sh = lax.dynamic_slice_in_dim(b_mod, chip * MOD_SH, MOD_SH, axis=1)
    mod_sh = mod_fwd(call16, w_mod[0], b_sh, "mod_fwd")
    mod_g = allgather8(mod_sh, "gather_mod").reshape(N_DEV, 16, MOD_SH)
    mod_all = jnp.concatenate([mod_g[0], mod_g[2], mod_g[4], mod_g[6]], axis=1)
    mod_x = lax.dynamic_index_in_dim(mod_all, dev, axis=0, keepdims=False).reshape(6, D)
    mod_c = mod_all[8].reshape(6, D)

    pk_w = _pack_params(c_ctx, b_mod, norm1_w, norm2_w, hg_lb_fwd, hg_lb_bwd, hg_norm_w, rt_theta_fwd, rt_theta_bwd, final_norm_w)
    theta_row = pk_w[14:15]
    lb_f, lb_b, lg_row = prep_small(hg_lb_fwd, hg_lb_bwd, theta_row, "prep_small")
    lg_f = jnp.broadcast_to(lg_row[0, 0:HEADS].reshape(HEADS, 1, 1), (HEADS, 1, RT_DV))
    lg_b = jnp.broadcast_to(lg_row[0, HEADS:2 * HEADS].reshape(HEADS, 1, 1), (HEADS, 1, RT_DV))

    g_in = gather_columns(w_in[0].astype(BF16), "gather_w_in")
    rest = [s[0].astype(BF16) for s in (w_proj_hgrn, w_proj_ret, w_out, w_ffn_gate, w_ffn_up, w_ffn_down)]

    dx, full, small = local_step(x[0], ctx[0], loss_target[0], mod_x, mod_c, lb_f, lb_b, lg_f, lg_b,
                                 norm1_w, norm2_w, hg_norm_w, final_norm_w.reshape(1, D), {"w_in": g_in}, rest,
                                 (ci, core_arg, place_arg))

    g3 = allgather8(small, "gather_small").reshape(N_DEV, SMALL_ROWS, D)
    pk_g, aux = small_grads(g3, lb_f, lb_b, theta_row, "small_grads")
    loss = aux[2, 0]
    dmod16 = jnp.concatenate([
        g3[:, 0:6, :].reshape(N_DEV, 6 * D),
        jnp.concatenate([aux[0], aux[1], jnp.zeros((4 * D,), F32)]).reshape(1, 6 * D),
        jnp.zeros((7, 6 * D), F32)], axis=0)
    dmod_sh = lax.dynamic_slice_in_dim(dmod16, chip * MOD_SH, MOD_SH, axis=1)
    g_wmod, dsilu = mod_bwd(call16, dmod_sh, w_mod[0], "mod_bwd")
    g4 = allgather8(dsilu, "gather_dsilu")
    pk_m = _pack_params(m_c_ctx, m_b_mod, m_norm1_w, m_norm2_w, m_hg_lb_fwd, m_hg_lb_bwd, m_hg_norm_w, m_rt_theta_fwd, m_rt_theta_bwd, m_final_norm_w)
    pk_v = _pack_params(v_c_ctx, v_b_mod, v_norm1_w, v_norm2_w, v_hg_lb_fwd, v_hg_lb_bwd, v_hg_norm_w, v_rt_theta_fwd, v_rt_theta_bwd, v_final_norm_w)
    pk_g, pk_d, pk_nm, pk_nv = adamw_small(g4, pk_g, pk_w, pk_m, pk_v, "adamw_small")

    big = {
        "w_mod": (g_wmod, w_mod, m_w_mod, v_w_mod),
        "w_in": (full["w_in"], w_in, m_w_in, v_w_in),
        "w_pa": (full["w_pa"], w_proj_hgrn, m_w_proj_hgrn, v_w_proj_hgrn),
        "w_pb": (full["w_pb"], w_proj_ret, m_w_proj_ret, v_w_proj_ret),
        "w_out": (full["w_out"], w_out, m_w_out, v_w_out),
        "wg": (full["wg"], w_ffn_gate, m_w_ffn_gate, v_w_ffn_gate),
        "wu": (full["wu"], w_ffn_up, m_w_ffn_up, v_w_ffn_up),
        "wd": (full["wd"], w_ffn_down, m_w_ffn_down, v_w_ffn_down),
    }
    res = {}
    for k, (g, wt, mt, vt) in big.items():
        d, nm, nv = adamw(wt[0], g, mt[0], vt[0], "adamw_" + k)
        res[k] = (g[None], d[None], nm[None], nv[None])

    sm = [_unpack_params(p) for p in (pk_g, pk_d, pk_nm, pk_nv)]
    outs = []
    for t in range(4):
        (s_cctx, s_bmod, s_n1, s_n2, s_lbf, s_lbb, s_hgn, s_thf, s_thb, s_fin) = sm[t]
        outs.append([s_cctx, res["w_mod"][t], s_bmod, s_n1, s_n2, res["w_in"][t], s_lbf, s_lbb, s_hgn, s_thf, s_thb,
                     res["w_pa"][t], res["w_pb"][t], res["w_out"][t], res["wg"][t], res["wu"][t], res["wd"][t], s_fin])
    return (loss, dx[None], *outs[0], *outs[1], *outs[2], *outs[3])
```

```python
import functools

import jax
import jax.numpy as jnp
from jax import lax
from jax.experimental import pallas as pl
from jax.experimental.pallas import tpu as pltpu

F32 = jnp.float32
BF16 = jnp.bfloat16
HI = lax.Precision.HIGHEST

D = 1024
HEADS = 8
HG_D = 128
RT_DK = 128
RT_DV = 256
D_FF = 2816
D_IN = 13312
N_SHARD = 4
IN_SH = D_IN // N_SHARD
FF_SH = D_FF // N_SHARD
HG_CHUNK = 32
SCAN_ROWS = 256
HG_GROUP = 8
RT_GROUP = 4
PROJ_ROWS = 1024
EPS = 1e-6
GN_EPS = 1e-5
Q_SCALE = 128.0 ** -0.5
VMEM_LIMIT = 56 * 1024 * 1024

COL_HQ, COL_HFF, COL_HFB, COL_HI, COL_HG = 0, 8, 16, 24, 32
COL_RQ, COL_RK, COL_RV, COL_RG, COL_GA, COL_GB = 40, 48, 56, 72, 88, 96

ADAM_LR, ADAM_B1, ADAM_B2, ADAM_EPS, ADAM_WD, ADAM_STEP = 0.001, 0.9, 0.999, 1e-08, 0.01, 10


def _params(*sem):
    return pltpu.CompilerParams(dimension_semantics=sem, vmem_limit_bytes=VMEM_LIMIT)


def _dot(a, b, ca=1, cb=0, prec=None):
    return lax.dot_general(a, b, (((ca,), (cb,)), ((), ())), precision=prec, preferred_element_type=F32)


def _bdot(a, b, ca=1, cb=0):
    return _dot(a.astype(BF16), b.astype(BF16), ca, cb)


def _sigmoid(z):
    return 1.0 / (1.0 + jnp.exp(-z))


def _rowsum(a):
    return jnp.sum(a, axis=0, keepdims=True)


def _lanemean(a):
    return jnp.mean(a, axis=-1, keepdims=True)


def _grid_step(grid):
    pos, total = 0, 1
    for d, size in enumerate(grid):
        pos = pos * size + pl.program_id(d)
        total *= size
    return pos, total


def normmod_matmul(x, nw, sh, sc, w4, name, gather=()):
    L = x.shape[0]
    tm = min(PROJ_ROWS, L)
    tn = IN_SH // 2
    grid = (L // tm, N_SHARD, 2)
    ng = len(gather)

    def body(x_ref, nw_ref, sh_ref, sc_ref, w_ref, *refs):
        p_ref, hx_ref = refs[ng:ng + 2]
        hx_scr = refs[2 * ng + 2]
        if ng:
            start, forward, finish = _gather_phases(refs[:ng], refs[ng + 2:2 * ng + 2], *refs[2 * ng + 3:])
            pos, total = _grid_step(grid)
            pl.when(pos == 0)(start)
            pl.when(pos == total // 2)(forward)

        @pl.when((pl.program_id(1) == 0) & (pl.program_id(2) == 0))
        def _():
            xv = x_ref[...]
            n = xv * lax.rsqrt(_lanemean(xv * xv) + EPS) * nw_ref[...]
            h = (n * (1.0 + sc_ref[...]) + sh_ref[...]).astype(BF16)
            hx_scr[...] = h
            hx_ref[...] = h

        p_ref[...] = _dot(hx_scr[...], w_ref[...])
        if ng:
            pl.when(pos == total - 1)(finish)

    vec = pl.BlockSpec((1, D), lambda i, k, j: (0, 0))
    return pl.pallas_call(
        body, name=name,
        grid=grid,
        in_specs=[pl.BlockSpec((tm, D), lambda i, k, j: (i, 0)), vec, vec, vec,
                  pl.BlockSpec((D, tn), lambda i, k, j: (0, 2 * k + j))] + [ANY] * ng,
        out_specs=[pl.BlockSpec((tm, tn), lambda i, k, j: (i, 2 * k + j)),
                   pl.BlockSpec((tm, D), lambda i, k, j: (i, 0))] + [ANY] * ng,
        out_shape=[jax.ShapeDtypeStruct((L, D_IN), F32), jax.ShapeDtypeStruct((L, D), BF16)]
        + [jax.ShapeDtypeStruct((N_SHARD,) + s.shape, s.dtype) for s in gather],
        scratch_shapes=[pltpu.VMEM((tm, D), BF16)] + (_gather_scratch(ng) if ng else []),
        compiler_params=_params("arbitrary", "arbitrary", "arbitrary"),
    )(x, nw, sh, sc, w4, *gather)


def _hgrn_gates(z, lb):
    sg = _sigmoid(z)
    sgn = _sigmoid(-z)
    f = lb + (1.0 - lb) * sg
    k = (1.0 - lb) * sgn
    return sg, sgn, f, k


def _tri_chunks(n, chunk, reverse):
    r = lax.broadcasted_iota(jnp.int32, (n, n), 0)
    c = lax.broadcasted_iota(jnp.int32, (n, n), 1)
    same = (r // chunk) == (c // chunk)
    return jnp.where(same & ((r <= c) if reverse else (r >= c)), 1.0, 0.0).astype(F32)


def _decay3(b, reverse):
    C = b.shape[0]
    t = lax.broadcasted_iota(jnp.int32, (C, C, 1), 0)
    s = lax.broadcasted_iota(jnp.int32, (C, C, 1), 1)
    mask = (t <= s) if reverse else (t >= s)
    return jnp.exp(jnp.where(mask, b[:, None, :] - b[None, :, :], -jnp.inf))


HG_SUB = 16


def _hgrn_pairs(reverse):
    pairs = []
    size = HG_SUB
    while size < HG_CHUNK:
        for lo in range(0, HG_CHUNK, 2 * size):
            first, second = slice(lo, lo + size), slice(lo + size, lo + 2 * size)
            if reverse:
                pairs.append((first, second, lo + size))
            else:
                pairs.append((second, first, lo + size - 1))
        size *= 2
    return pairs


def _hgrn_intra_fwd(q, k, v, b, reverse):
    blocks = []
    for lo in range(0, HG_CHUNK, HG_SUB):
        r = slice(lo, lo + HG_SUB)
        att3 = jnp.sum(q[r][:, None, :] * k[r][None, :, :] * _decay3(b[r], reverse), axis=-1, keepdims=True)
        blocks.append(jnp.sum(att3 * v[r][None, :, :], axis=1))
    for qr, kr, ref in _hgrn_pairs(reverse):
        beta = b[ref:ref + 1]
        att = _bdot(q[qr] * jnp.exp(b[qr] - beta), k[kr] * jnp.exp(beta - b[kr]), 1, 1)
        part = _bdot(att, v[kr])
        n = part.shape[0] // HG_SUB
        for i in range(n):
            blocks[qr.start // HG_SUB + i] += part[i * HG_SUB:(i + 1) * HG_SUB]
    return jnp.concatenate(blocks, axis=0)


def _hgrn_intra_bwd(q, k, v, b, d_o, reverse):
    nb = HG_CHUNK // HG_SUB
    dq, dk, dv = [None] * nb, [None] * nb, [None] * nb
    for i in range(nb):
        r = slice(i * HG_SUB, (i + 1) * HG_SUB)
        e3 = _decay3(b[r], reverse)
        p3 = jnp.sum(d_o[r][:, None, :] * v[r][None, :, :], axis=-1, keepdims=True) * e3
        dq[i] = jnp.sum(p3 * k[r][None, :, :], axis=1)
        dk[i] = jnp.sum(p3 * q[r][:, None, :], axis=0)
        att3 = jnp.sum(q[r][:, None, :] * k[r][None, :, :] * e3, axis=-1, keepdims=True)
        dv[i] = jnp.sum(att3 * d_o[r][:, None, :], axis=0)

    def add(acc, rows, part):
        for i in range(part.shape[0] // HG_SUB):
            acc[rows.start // HG_SUB + i] += part[i * HG_SUB:(i + 1) * HG_SUB]

    for qr, kr, ref in _hgrn_pairs(reverse):
        beta = b[ref:ref + 1]
        fq, fk = jnp.exp(b[qr] - beta), jnp.exp(beta - b[kr])
        qt, kt = q[qr] * fq, k[kr] * fk
        att = _bdot(qt, kt, 1, 1)
        datt = _bdot(d_o[qr], v[kr], 1, 1)
        add(dq, qr, _bdot(datt, kt) * fq)
        add(dk, kr, _bdot(datt, qt, 0, 0) * fk)
        add(dv, kr, _bdot(att, d_o[qr], 0, 0))
    return jnp.concatenate(dq, axis=0), jnp.concatenate(dk, axis=0), jnp.concatenate(dv, axis=0)


def _hgrn_state_step(k, v, b, s_t, last):
    b_last = b[last:last + 1]
    return s_t * jnp.exp(b_last) + _bdot(v, k * jnp.exp(b_last - b), 0, 0)


def hgrn_scan_fwd(p, lb, s0, col_z, reverse, name):
    L = p.shape[0]
    nB = L // SCAN_ROWS
    nC = SCAN_ROWS // HG_CHUNK
    C = HG_CHUNK
    G, W = HG_GROUP, HG_GROUP * HG_D
    last = 0 if reverse else C - 1

    def bmap(b):
        return (nB - 1 - b) if reverse else b

    def body(q_ref, z_ref, v_ref, lb_ref, s0_ref, o_ref, sfin_ref, sblk_ref, s_scr, k_scr, b_scr):
        blk = pl.program_id(1)

        @pl.when(blk == 0)
        def _():
            s_scr[...] = s0_ref[...]

        sblk_ref[...] = s_scr[...]
        _, _, f_all, k_all = _hgrn_gates(z_ref[...], lb_ref[...])
        k_scr[...] = k_all
        b_scr[...] = _dot(_tri_chunks(SCAN_ROWS, C, reverse), jnp.log(f_all), prec=HI)

        def chunk(ci, carry):
            c = (nC - 1 - ci) if reverse else ci
            rows = pl.ds(pl.multiple_of(c * C, C), C)
            for j in range(G):
                lanes = slice(j * HG_D, (j + 1) * HG_D)
                q = q_ref[rows, lanes] * Q_SCALE
                v = v_ref[rows, lanes]
                k = k_scr[rows, lanes]
                b = b_scr[rows, lanes]
                s_t = s_scr[j]
                o_ref[rows, lanes] = _hgrn_intra_fwd(q, k, v, b, reverse) + _bdot(q * jnp.exp(b), s_t, 1, 1)
                s_scr[j] = _hgrn_state_step(k, v, b, s_t, last)
            return carry

        lax.fori_loop(0, nC, chunk, 0)

        @pl.when(blk == nB - 1)
        def _():
            sfin_ref[...] = s_scr[...]

    def col(c0):
        return pl.BlockSpec((SCAN_ROWS, W), lambda h, b: (bmap(b), c0 // G + h))

    state = pl.BlockSpec((G, HG_D, HG_D), lambda h, b: (h, 0, 0))
    return pl.pallas_call(
        body, name=name,
        grid=(HEADS // G, nB),
        in_specs=[col(COL_HQ), col(col_z), col(COL_HI), pl.BlockSpec((1, W), lambda h, b: (0, h)), state],
        out_specs=[pl.BlockSpec((SCAN_ROWS, W), lambda h, b: (bmap(b), h)), state,
                   pl.BlockSpec((None, G, HG_D, HG_D), lambda h, b: (bmap(b), h, 0, 0))],
        out_shape=[jax.ShapeDtypeStruct((L, D), F32),
                   jax.ShapeDtypeStruct((HEADS, HG_D, HG_D), F32),
                   jax.ShapeDtypeStruct((nB, HEADS, HG_D, HG_D), F32)],
        scratch_shapes=[pltpu.VMEM((G, HG_D, HG_D), F32), pltpu.VMEM((SCAN_ROWS, W), F32),
                        pltpu.VMEM((SCAN_ROWS, W), F32)],
        compiler_params=_params("parallel", "arbitrary"),
    )(p, p, p, lb, s0)


def hgrn_scan_bwd(p, lb, s_blocks, d_o, ds_fin, prev, col_z, reverse, name):
    L = p.shape[0]
    nB = L // SCAN_ROWS
    nC = SCAN_ROWS // HG_CHUNK
    C = HG_CHUNK
    G, W = HG_GROUP, HG_GROUP * HG_D
    last = 0 if reverse else C - 1
    has_prev = prev is not None
    out_dt = BF16 if has_prev else F32

    def bmap(b):
        return b if reverse else (nB - 1 - b)

    def body(*refs):
        q_ref, z_ref, v_ref, lb_ref, sblk_ref, do_ref, dsf_ref = refs[:7]
        refs = refs[7:]
        if has_prev:
            pq_ref, pv_ref = refs[:2]
            refs = refs[2:]
        dq_ref, dz_ref, dv_ref, dlb_ref, ds0_ref, st_scr, run_scr, ds_scr, k_scr, b_scr, db_scr, dk_scr = refs
        blk = pl.program_id(1)

        @pl.when(blk == 0)
        def _():
            ds_scr[...] = dsf_ref[...]
            dlb_ref[...] = jnp.zeros_like(dlb_ref)

        tri = _tri_chunks(SCAN_ROWS, C, reverse)
        row = lax.broadcasted_iota(jnp.int32, (C, HG_D), 0)
        _, _, f_all, k_all = _hgrn_gates(z_ref[...], lb_ref[...])
        k_scr[...] = k_all
        b_scr[...] = _dot(tri, jnp.log(f_all), prec=HI)
        run_scr[...] = sblk_ref[...]

        def recompute(ci, carry):
            c = (nC - 1 - ci) if reverse else ci
            rows = pl.ds(pl.multiple_of(c * C, C), C)
            for j in range(G):
                lanes = slice(j * HG_D, (j + 1) * HG_D)
                s_t = run_scr[j]
                st_scr[c, j] = s_t
                run_scr[j] = _hgrn_state_step(k_scr[rows, lanes], v_ref[rows, lanes], b_scr[rows, lanes], s_t, last)
            return carry

        lax.fori_loop(0, nC, recompute, 0)

        def chunk(ci, carry):
            c = ci if reverse else (nC - 1 - ci)
            rows = pl.ds(pl.multiple_of(c * C, C), C)
            for j in range(G):
                lanes = slice(j * HG_D, (j + 1) * HG_D)
                k = k_scr[rows, lanes]
                b = b_scr[rows, lanes]
                q = q_ref[rows, lanes] * Q_SCALE
                v = v_ref[rows, lanes]
                d_o = do_ref[rows, lanes]
                s_t = st_scr[c, j]
                ds_t = ds_scr[j]
                eb = jnp.exp(b)
                b_last = b[last:last + 1]
                eb_last = jnp.exp(b_last)
                kdec = jnp.exp(b_last - b)
                qe = q * eb
                ke = k * kdec
                dq_in, dk_in, dv_in = _hgrn_intra_bwd(q, k, v, b, d_o, reverse)
                dq_tot = _bdot(d_o, s_t, 1, 0) * eb + dq_in
                dke = _bdot(v, ds_t, 1, 0)
                dk_tot = dke * kdec + dk_in
                dv = dv_in + _bdot(ke, ds_t, 1, 1)
                db_last = _rowsum(dke * ke) + eb_last * _rowsum(ds_t * s_t)
                db_scr[rows, lanes] = q * dq_tot - k * dk_tot + jnp.where(row == last, db_last, 0.0)
                dk_scr[rows, lanes] = dk_tot
                dq = dq_tot * Q_SCALE
                if has_prev:
                    dq = dq + pq_ref[rows, lanes]
                    dv = dv + pv_ref[rows, lanes]
                dq_ref[rows, lanes] = dq.astype(out_dt)
                dv_ref[rows, lanes] = dv.astype(out_dt)
                ds_scr[j] = ds_t * eb_last + _bdot(d_o, qe, 0, 0)
            return carry

        lax.fori_loop(0, nC, chunk, 0)

        lb = lb_ref[...]
        sg, sgn, f, _ = _hgrn_gates(z_ref[...], lb)
        g = _dot(tri, db_scr[...], 0, 0, prec=HI) / f - dk_scr[...]
        dz_ref[...] = (g * (1.0 - lb) * sg * sgn).astype(BF16)
        dlb_ref[...] += _rowsum(g * sgn)

        @pl.when(blk == nB - 1)
        def _():
            ds0_ref[...] = ds_scr[...]

    def col(c0):
        return pl.BlockSpec((SCAN_ROWS, W), lambda h, b: (bmap(b), c0 // G + h))

    tile = pl.BlockSpec((SCAN_ROWS, W), lambda h, b: (bmap(b), h))
    state = pl.BlockSpec((G, HG_D, HG_D), lambda h, b: (h, 0, 0))
    in_specs = [col(COL_HQ), col(col_z), col(COL_HI),
                pl.BlockSpec((1, W), lambda h, b: (0, h)),
                pl.BlockSpec((None, G, HG_D, HG_D), lambda h, b: (bmap(b), h, 0, 0)),
                tile, state]
    args = [p, p, p, lb, s_blocks, d_o, ds_fin]
    if has_prev:
        in_specs += [tile, tile]
        args += list(prev)
    return pl.pallas_call(
        body, name=name,
        grid=(HEADS // G, nB),
        in_specs=in_specs,
        out_specs=[tile, tile, tile, pl.BlockSpec((1, W), lambda h, b: (0, h)), state],
        out_shape=[jax.ShapeDtypeStruct((L, D), out_dt), jax.ShapeDtypeStruct((L, D), BF16),
                   jax.ShapeDtypeStruct((L, D), out_dt), jax.ShapeDtypeStruct((1, D), F32),
                   jax.ShapeDtypeStruct((HEADS, HG_D, HG_D), F32)],
        scratch_shapes=[pltpu.VMEM((nC, G, HG_D, HG_D), F32), pltpu.VMEM((G, HG_D, HG_D), F32),
                        pltpu.VMEM((G, HG_D, HG_D), F32)] + [pltpu.VMEM((SCAN_ROWS, W), F32)] * 4,
        compiler_params=_params("parallel", "arbitrary"),
    )(*args)


def _rope(t, cosf, sinf):
    return t * cosf + pltpu.roll(t, RT_DK // 2, 1) * sinf


def _rope_t(d, cosf, sinf):
    return d * cosf + pltpu.roll(d * sinf, RT_DK // 2, 1)


def _ret_decays(lg, reverse):
    C = SCAN_ROWS
    t = lax.broadcasted_iota(jnp.int32, (C, C), 0)
    s = lax.broadcasted_iota(jnp.int32, (C, C), 1)
    delta = ((s - t) if reverse else (t - s)).astype(F32)
    dmat = jnp.where(delta >= 0, jnp.exp(lg * jnp.maximum(delta, 0.0)), 0.0)
    r = lax.broadcasted_iota(jnp.int32, (C, RT_DK), 0)
    pos = ((C - 1 - r) if reverse else r).astype(F32)
    lg1 = lg[:, :RT_DK]
    qdec = jnp.exp(lg1 * (pos + 1.0))
    kdec = jnp.exp(lg1 * (C - 1.0 - pos))
    sdec = jnp.exp(lg1 * float(C))
    return dmat, delta, pos, qdec, kdec, sdec


def ret_scan_fwd(p, cosf, sinf, lg, s0, reverse, name):
    L = p.shape[0]
    C = SCAN_ROWS
    nB = L // C

    def bmap(b):
        return (nB - 1 - b) if reverse else b

    G = RT_GROUP

    def body(q_ref, k_ref, v_ref, cos_ref, sin_ref, lg_ref, s0_ref, o_ref, sfin_ref, sblk_ref, s_scr):
        blk = pl.program_id(1)

        @pl.when(blk == 0)
        def _():
            s_scr[...] = s0_ref[...]

        sblk_ref[...] = s_scr[...]
        cosf, sinf = cos_ref[...], sin_ref[...]
        for j in range(G):
            lk, lv = slice(j * RT_DK, (j + 1) * RT_DK), slice(j * RT_DV, (j + 1) * RT_DV)
            s_t = s_scr[j]
            dmat, _, _, qdec, kdec, sdec = _ret_decays(lg_ref[j], reverse)
            q = _rope(q_ref[:, lk] * Q_SCALE, cosf, sinf)
            k = _rope(k_ref[:, lk], cosf, sinf)
            v = v_ref[:, lv]
            att = _bdot(q, k, 1, 1) * dmat
            o_ref[:, lv] = _bdot(att, v) + _bdot(q * qdec, s_t, 1, 1)
            s_scr[j] = s_t * sdec + _bdot(v, k * kdec, 0, 0)

        @pl.when(blk == nB - 1)
        def _():
            sfin_ref[...] = s_scr[...]

    def col(c0):
        return pl.BlockSpec((C, G * RT_DK), lambda h, b: (bmap(b), c0 // G + h))

    tab = pl.BlockSpec((C, RT_DK), lambda h, b: (bmap(b), 0))
    state = pl.BlockSpec((G, RT_DV, RT_DK), lambda h, b: (h, 0, 0))
    return pl.pallas_call(
        body, name=name,
        grid=(HEADS // G, nB),
        in_specs=[col(COL_RQ), col(COL_RK),
                  pl.BlockSpec((C, G * RT_DV), lambda h, b: (bmap(b), COL_RV // (2 * G) + h)),
                  tab, tab, pl.BlockSpec((G, 1, RT_DV), lambda h, b: (h, 0, 0)), state],
        out_specs=[pl.BlockSpec((C, G * RT_DV), lambda h, b: (bmap(b), h)), state,
                   pl.BlockSpec((None, G, RT_DV, RT_DK), lambda h, b: (bmap(b), h, 0, 0))],
        out_shape=[jax.ShapeDtypeStruct((L, HEADS * RT_DV), F32),
                   jax.ShapeDtypeStruct((HEADS, RT_DV, RT_DK), F32),
                   jax.ShapeDtypeStruct((nB, HEADS, RT_DV, RT_DK), F32)],
        scratch_shapes=[pltpu.VMEM((G, RT_DV, RT_DK), F32)],
        compiler_params=_params("parallel", "arbitrary"),
    )(p, p, p, cosf, sinf, lg, s0)


def ret_scan_bwd(p, cosf, sinf, lg, s_blocks, d_o, ds_fin, prev, reverse, name):
    L = p.shape[0]
    C = SCAN_ROWS
    nB = L // C
    has_prev = prev is not None
    out_dt = BF16 if has_prev else F32
    G = RT_GROUP

    def bmap(b):
        return b if reverse else (nB - 1 - b)

    def body(*refs):
        q_ref, k_ref, v_ref, cos_ref, sin_ref, lg_ref, sblk_ref, do_ref, dsf_ref = refs[:9]
        refs = refs[9:]
        if has_prev:
            pq_ref, pk_ref, pv_ref = refs[:3]
            refs = refs[3:]
        dq_ref, dk_ref, dv_ref, dlg_ref, ds0_ref, ds_scr = refs
        blk = pl.program_id(1)

        @pl.when(blk == 0)
        def _():
            ds_scr[...] = dsf_ref[...]
            dlg_ref[...] = jnp.zeros_like(dlg_ref)

        cosf, sinf = cos_ref[...], sin_ref[...]
        for j in range(G):
            lk, lv = slice(j * RT_DK, (j + 1) * RT_DK), slice(j * RT_DV, (j + 1) * RT_DV)
            s_t = sblk_ref[j]
            ds_t = ds_scr[j]
            dmat, delta, pos, qdec, kdec, sdec = _ret_decays(lg_ref[j], reverse)
            q = _rope(q_ref[:, lk] * Q_SCALE, cosf, sinf)
            k = _rope(k_ref[:, lk], cosf, sinf)
            v = v_ref[:, lv]
            d_o = do_ref[:, lv]
            att_raw = _bdot(q, k, 1, 1)
            datt_m = _bdot(d_o, v, 1, 1) * dmat
            dqd = _bdot(d_o, s_t, 1, 0)
            dkd = _bdot(v, ds_t, 1, 0)
            dq = _bdot(datt_m, k) + dqd * qdec
            dk = _bdot(datt_m, q, 0, 0) + dkd * kdec
            dv = _bdot(att_raw * dmat, d_o, 0, 0) + _bdot(k * kdec, ds_t, 1, 1)
            ds_scr[j] = ds_t * sdec + _bdot(d_o, q * qdec, 0, 0)
            t1 = jnp.sum(_rowsum(datt_m * att_raw * delta), axis=-1, keepdims=True)
            t23 = jnp.sum(_rowsum((pos + 1.0) * qdec * q * dqd + (C - 1.0 - pos) * kdec * k * dkd), axis=-1, keepdims=True)
            t4 = jnp.sum(_rowsum(ds_t * s_t * sdec), axis=-1, keepdims=True) * float(C)
            dlg_ref[j] += jnp.broadcast_to(t1 + t23 + t4, (1, RT_DK))
            if has_prev:
                dq = _rope_t(dq + pq_ref[:, lk], cosf, sinf) * Q_SCALE
                dk = _rope_t(dk + pk_ref[:, lk], cosf, sinf)
                dv = dv + pv_ref[:, lv]
            dq_ref[:, lk] = dq.astype(out_dt)
            dk_ref[:, lk] = dk.astype(out_dt)
            dv_ref[:, lv] = dv.astype(out_dt)

        @pl.when(blk == nB - 1)
        def _():
            ds0_ref[...] = ds_scr[...]

    def col(c0):
        return pl.BlockSpec((C, G * RT_DK), lambda h, b: (bmap(b), c0 // G + h))

    tab = pl.BlockSpec((C, RT_DK), lambda h, b: (bmap(b), 0))
    state = pl.BlockSpec((G, RT_DV, RT_DK), lambda h, b: (h, 0, 0))
    tk = pl.BlockSpec((C, G * RT_DK), lambda h, b: (bmap(b), h))
    tv = pl.BlockSpec((C, G * RT_DV), lambda h, b: (bmap(b), h))
    in_specs = [col(COL_RQ), col(COL_RK),
                pl.BlockSpec((C, G * RT_DV), lambda h, b: (bmap(b), COL_RV // (2 * G) + h)),
                tab, tab, pl.BlockSpec((G, 1, RT_DV), lambda h, b: (h, 0, 0)),
                pl.BlockSpec((None, G, RT_DV, RT_DK), lambda h, b: (bmap(b), h, 0, 0)),
                tv, state]
    args = [p, p, p, cosf, sinf, lg, s_blocks, d_o, ds_fin]
    if has_prev:
        in_specs += [tk, tk, tv]
        args += list(prev)
    return pl.pallas_call(
        body, name=name,
        grid=(HEADS // G, nB),
        in_specs=in_specs,
        out_specs=[tk, tk, tv, pl.BlockSpec((G, 1, RT_DK), lambda h, b: (h, 0, 0)), state],
        out_shape=[jax.ShapeDtypeStruct((L, D), out_dt), jax.ShapeDtypeStruct((L, D), out_dt),
                   jax.ShapeDtypeStruct((L, HEADS * RT_DV), out_dt),
                   jax.ShapeDtypeStruct((HEADS, 1, RT_DK), F32),
                   jax.ShapeDtypeStruct((HEADS, RT_DV, RT_DK), F32)],
        scratch_shapes=[pltpu.VMEM((G, RT_DV, RT_DK), F32)],
        compiler_params=_params("parallel", "arbitrary"),
    )(*args)


def _silu_parts(h):
    s = _sigmoid(h)
    return h * s, s * (1.0 + h * (1.0 - s))


def _head_rms(o):
    outs, rs = [], []
    for h in range(HEADS):
        oh = o[:, h * HG_D:(h + 1) * HG_D]
        r = lax.rsqrt(_lanemean(oh * oh) + EPS)
        outs.append(oh * r)
        rs.append(r)
    return outs, rs


def _group_norm(o):
    outs, rs = [], []
    for h in range(HEADS):
        oh = o[:, h * RT_DV:(h + 1) * RT_DV]
        c = oh - _lanemean(oh)
        r = lax.rsqrt(_lanemean(c * c) + GN_EPS)
        outs.append(c * r)
        rs.append(r)
    return outs, rs


MIX_ROWS = 256
MIX_BWD_ROWS = 128


def _mix_specs(rows):
    def t(w, c=0):
        return pl.BlockSpec((rows, w), lambda i: (i, c))

    return t


def mix_fwd(ohf, ohb, orf, orb, p, x, g1, hgw, w_pa, w_pb, w_out, name):
    L = x.shape[0]
    t = _mix_specs(MIX_ROWS)

    def body(ohf_ref, ohb_ref, orf_ref, orb_ref, hg_ref, rg0_ref, rg1_ref, ga_ref, gb_ref, x_ref, g1_ref, hgw_ref,
             wpa_ref, wpb_ref, wout_ref, x1_ref, xmix_ref, merged_ref, ya_ref, yb_ref):
        nh, _ = _head_rms(ohf_ref[...] + ohb_ref[...])
        ya = jnp.concatenate(nh, axis=1) * hgw_ref[...] * _silu_parts(hg_ref[...])[0]
        gn, _ = _group_norm(orf_ref[...] + orb_ref[...])
        rg = jnp.concatenate([rg0_ref[...], rg1_ref[...]], axis=1)
        yb = jnp.concatenate(gn, axis=1) * _silu_parts(rg)[0]
        ya16, yb16 = ya.astype(BF16), yb.astype(BF16)
        merged = (_sigmoid(ga_ref[...]) * _dot(ya16, wpa_ref[...])
                  + _sigmoid(gb_ref[...]) * _dot(yb16, wpb_ref[...])).astype(BF16)
        x_mix = _dot(merged, wout_ref[...])
        x1_ref[...] = x_ref[...] + g1_ref[...] * x_mix
        xmix_ref[...] = x_mix
        merged_ref[...] = merged
        ya_ref[...] = ya16
        yb_ref[...] = yb16

    vec = pl.BlockSpec((1, D), lambda i: (0, 0))

    def full(a):
        return pl.BlockSpec(a.shape, lambda i: (0, 0), pipeline_mode=pl.Buffered(1))

    return pl.pallas_call(
        body, name=name,
        grid=(L // MIX_ROWS,),
        in_specs=[t(D), t(D), t(2 * D), t(2 * D), t(D, COL_HG // 8), t(D, COL_RG // 8), t(D, COL_RG // 8 + 1),
                  t(D, COL_GA // 8), t(D, COL_GB // 8), t(D), vec, vec, full(w_pa), full(w_pb), full(w_out)],
        out_specs=[t(D), t(D), t(D), t(D), t(2 * D)],
        out_shape=[jax.ShapeDtypeStruct((L, D), F32), jax.ShapeDtypeStruct((L, D), F32),
                   jax.ShapeDtypeStruct((L, D), BF16), jax.ShapeDtypeStruct((L, D), BF16),
                   jax.ShapeDtypeStruct((L, 2 * D), BF16)],
        compiler_params=_params("parallel"),
    )(ohf, ohb, orf, orb, p, p, p, p, p, x, g1, hgw, w_pa, w_pb, w_out)


def mix_bwd(dx1, x_mix, ya, yb, ohf, ohb, orf, orb, p, g1, hgw, w_pa, w_pb, w_out, name):
    L = dx1.shape[0]
    t = _mix_specs(MIX_BWD_ROWS)

    def body(dx1_ref, xmix_ref, ya_ref, yb_ref, ohf_ref, ohb_ref, orf_ref, orb_ref, hg_ref, rg0_ref, rg1_ref,
             ga_ref, gb_ref, g1_ref, hgw_ref, wpa_ref, wpb_ref, wout_ref,
             dxm_ref, da_ref, db_ref, dga_ref, dgb_ref, dhg_ref, drg_ref, dohg_ref, dort_ref, sums_ref):
        @pl.when(pl.program_id(0) == 0)
        def _():
            sums_ref[...] = jnp.zeros_like(sums_ref)

        dx1 = dx1_ref[...]
        dxm = (g1_ref[...] * dx1).astype(BF16)
        dxm_ref[...] = dxm
        dmerged = _dot(dxm, wout_ref[...], 1, 1)
        a = _dot(ya_ref[...], wpa_ref[...])
        bm = _dot(yb_ref[...], wpb_ref[...])
        sa, sb = _sigmoid(ga_ref[...]), _sigmoid(gb_ref[...])
        d_a = (dmerged * sa).astype(BF16)
        d_b = (dmerged * sb).astype(BF16)
        da_ref[...] = d_a
        db_ref[...] = d_b
        dga_ref[...] = (dmerged * a * sa * (1.0 - sa)).astype(BF16)
        dgb_ref[...] = (dmerged * bm * sb * (1.0 - sb)).astype(BF16)
        dya = _dot(d_a, wpa_ref[...], 1, 1)
        dyb = _dot(d_b, wpb_ref[...], 1, 1)

        hgw = hgw_ref[...]
        silu_h, dsilu_h = _silu_parts(hg_ref[...])
        nh, rh = _head_rms(ohf_ref[...] + ohb_ref[...])
        n = jnp.concatenate(nh, axis=1)
        dhg_ref[...] = (dya * n * hgw * dsilu_h).astype(BF16)
        dn = dya * hgw * silu_h
        douts = []
        for h in range(HEADS):
            dnh = dn[:, h * HG_D:(h + 1) * HG_D]
            douts.append(rh[h] * (dnh - nh[h] * _lanemean(dnh * nh[h])))
        dohg_ref[...] = jnp.concatenate(douts, axis=1)

        rg = jnp.concatenate([rg0_ref[...], rg1_ref[...]], axis=1)
        silu_r, dsilu_r = _silu_parts(rg)
        gn, rr = _group_norm(orf_ref[...] + orb_ref[...])
        g = jnp.concatenate(gn, axis=1)
        drg_ref[...] = (dyb * g * dsilu_r).astype(BF16)
        dgn = dyb * silu_r
        douts = []
        for h in range(HEADS):
            dgh = dgn[:, h * RT_DV:(h + 1) * RT_DV]
            douts.append(rr[h] * (dgh - _lanemean(dgh) - gn[h] * _lanemean(dgh * gn[h])))
        dort_ref[...] = jnp.concatenate(douts, axis=1)

        sums_ref[0:1, :] += _rowsum(dx1 * xmix_ref[...])
        sums_ref[1:2, :] += _rowsum(dya * n * silu_h)

    vec = pl.BlockSpec((1, D), lambda i: (0, 0))

    def full(a):
        return pl.BlockSpec(a.shape, lambda i: (0, 0), pipeline_mode=pl.Buffered(1))

    bf = functools.partial(jax.ShapeDtypeStruct, dtype=BF16)
    return pl.pallas_call(
        body, name=name,
        grid=(L // MIX_BWD_ROWS,),
        in_specs=[t(D), t(D), t(D), t(2 * D), t(D), t(D), t(2 * D), t(2 * D),
                  t(D, COL_HG // 8), t(D, COL_RG // 8), t(D, COL_RG // 8 + 1), t(D, COL_GA // 8), t(D, COL_GB // 8),
                  vec, vec, full(w_pa), full(w_pb), full(w_out)],
        out_specs=[t(D), t(D), t(D), t(D), t(D), t(D), t(2 * D), t(D), t(2 * D),
                   pl.BlockSpec((8, D), lambda i: (0, 0))],
        out_shape=[bf((L, D)), bf((L, D)), bf((L, D)), bf((L, D)), bf((L, D)), bf((L, D)), bf((L, 2 * D)),
                   jax.ShapeDtypeStruct((L, D), F32), jax.ShapeDtypeStruct((L, 2 * D), F32),
                   jax.ShapeDtypeStruct((8, D), F32)],
        compiler_params=_params("arbitrary"),
    )(dx1, x_mix, ya, yb, ohf, ohb, orf, orb, p, p, p, p, p, g1, hgw, w_pa, w_pb, w_out)


FFN_ROWS = 512


def ffn_fwd(x1, target, nw2, sh2, sc2, g2, fw, wg, wu, wd, name):
    L = x1.shape[0]
    tm = min(FFN_ROWS, L)

    def body(x1_ref, tgt_ref, nw2_ref, sh2_ref, sc2_ref, g2_ref, fw_ref, wg_ref, wu_ref, wd_ref,
             hx2_ref, g_ref, u_ref, h_ref, f_ref, dx2_ref, sums_ref, hx_scr, acc):
        i, j = pl.program_id(0), pl.program_id(1)

        @pl.when((i == 0) & (j == 0))
        def _():
            sums_ref[...] = jnp.zeros_like(sums_ref)

        @pl.when(j == 0)
        def _():
            xv = x1_ref[...]
            n = xv * lax.rsqrt(_lanemean(xv * xv) + EPS) * nw2_ref[...]
            h = (n * (1.0 + sc2_ref[...]) + sh2_ref[...]).astype(BF16)
            hx_scr[...] = h
            hx2_ref[...] = h
            acc[...] = jnp.zeros_like(acc)

        hx = hx_scr[...]
        g = _dot(hx, wg_ref[...])
        u = _dot(hx, wu_ref[...])
        hh = (_silu_parts(g)[0] * u).astype(BF16)
        g_ref[...] = g
        u_ref[...] = u
        h_ref[...] = hh
        acc[...] += _dot(hh, wd_ref[...])

        @pl.when(j == N_SHARD - 1)
        def _():
            f = acc[...]
            f_ref[...] = f
            x2 = x1_ref[...] + g2_ref[...] * f
            r = lax.rsqrt(_lanemean(x2 * x2) + EPS)
            fw = fw_ref[...]
            e = x2 * r * fw - tgt_ref[...]
            dy = e * (1.0 / D)
            dyw = dy * fw
            dx2_ref[...] = r * dyw - x2 * (r * r * r) * _lanemean(dyw * x2)
            sums_ref[0:1, :] += _rowsum(dy * x2 * r)
            sums_ref[1:2, :] += _rowsum(e * e) * (0.5 / D)

    row = pl.BlockSpec((tm, D), lambda i, j: (i, 0))
    vec = pl.BlockSpec((1, D), lambda i, j: (0, 0))
    sh = pl.BlockSpec((None, tm, FF_SH), lambda i, j: (j, i, 0))
    return pl.pallas_call(
        body, name=name,
        grid=(L // tm, N_SHARD),
        in_specs=[row, row, vec, vec, vec, vec, vec,
                  pl.BlockSpec((None, D, FF_SH), lambda i, j: (j, 0, 0)),
                  pl.BlockSpec((None, D, FF_SH), lambda i, j: (j, 0, 0)),
                  pl.BlockSpec((None, FF_SH, D), lambda i, j: (j, 0, 0))],
        out_specs=[row, sh, sh, sh, row, row, pl.BlockSpec((8, D), lambda i, j: (0, 0))],
        out_shape=[jax.ShapeDtypeStruct((L, D), BF16),
                   jax.ShapeDtypeStruct((N_SHARD, L, FF_SH), F32), jax.ShapeDtypeStruct((N_SHARD, L, FF_SH), F32),
                   jax.ShapeDtypeStruct((N_SHARD, L, FF_SH), BF16),
                   jax.ShapeDtypeStruct((L, D), F32), jax.ShapeDtypeStruct((L, D), F32),
                   jax.ShapeDtypeStruct((8, D), F32)],
        scratch_shapes=[pltpu.VMEM((tm, D), BF16), pltpu.VMEM((tm, D), F32)],
        compiler_params=_params("arbitrary", "arbitrary"),
    )(x1, target, nw2, sh2, sc2, g2, fw, wg, wu, wd)


def ffn_bwd(dx2, x1, f, g, u, nw2, sc2, g2, wg, wu, wd, name):
    L = x1.shape[0]
    tm = min(FFN_ROWS, L)

    def body(dx2_ref, x1_ref, f_ref, g_ref, u_ref, nw2_ref, sc2_ref, g2_ref, wg_ref, wu_ref, wd_ref,
             df_ref, dg_ref, du_ref, dx1_ref, sums_ref, df_scr, acc):
        i, j = pl.program_id(0), pl.program_id(1)

        @pl.when((i == 0) & (j == 0))
        def _():
            sums_ref[...] = jnp.zeros_like(sums_ref)

        @pl.when(j == 0)
        def _():
            dx2 = dx2_ref[...]
            df = (g2_ref[...] * dx2).astype(BF16)
            df_scr[...] = df
            df_ref[...] = df
            sums_ref[0:1, :] += _rowsum(dx2 * f_ref[...])
            acc[...] = jnp.zeros_like(acc)

        dh = _dot(df_scr[...], wd_ref[...], 1, 1)
        gv, uv = g_ref[...], u_ref[...]
        silu_g, dsilu_g = _silu_parts(gv)
        dg = (dh * uv * dsilu_g).astype(BF16)
        du = (dh * silu_g).astype(BF16)
        dg_ref[...] = dg
        du_ref[...] = du
        acc[...] += _dot(dg, wg_ref[...], 1, 1) + _dot(du, wu_ref[...], 1, 1)

        @pl.when(j == N_SHARD - 1)
        def _():
            dhx = acc[...]
            xv = x1_ref[...]
            r = lax.rsqrt(_lanemean(xv * xv) + EPS)
            n0 = xv * r
            nw = nw2_ref[...]
            dn2 = dhx * (1.0 + sc2_ref[...])
            dn0 = dn2 * nw
            dx1_ref[...] = dx2_ref[...] + r * (dn0 - n0 * _lanemean(dn0 * n0))
            sums_ref[1:2, :] += _rowsum(dhx)
            sums_ref[2:3, :] += _rowsum(dhx * n0 * nw)
            sums_ref[3:4, :] += _rowsum(dn2 * n0)

    row = pl.BlockSpec((tm, D), lambda i, j: (i, 0))
    vec = pl.BlockSpec((1, D), lambda i, j: (0, 0))
    sh = pl.BlockSpec((None, tm, FF_SH), lambda i, j: (j, i, 0))
    return pl.pallas_call(
        body, name=name,
        grid=(L // tm, N_SHARD),
        in_specs=[row, row, row, sh, sh, vec, vec, vec,
                  pl.BlockSpec((None, D, FF_SH), lambda i, j: (j, 0, 0)),
                  pl.BlockSpec((None, D, FF_SH), lambda i, j: (j, 0, 0)),
                  pl.BlockSpec((None, FF_SH, D), lambda i, j: (j, 0, 0))],
        out_specs=[row, sh, sh, row, pl.BlockSpec((8, D), lambda i, j: (0, 0))],
        out_shape=[jax.ShapeDtypeStruct((L, D), BF16),
                   jax.ShapeDtypeStruct((N_SHARD, L, FF_SH), BF16), jax.ShapeDtypeStruct((N_SHARD, L, FF_SH), BF16),
                   jax.ShapeDtypeStruct((L, D), F32), jax.ShapeDtypeStruct((8, D), F32)],
        scratch_shapes=[pltpu.VMEM((tm, D), BF16), pltpu.VMEM((tm, D), F32)],
        compiler_params=_params("arbitrary", "arbitrary"),
    )(dx2, x1, f, g, u, nw2, sc2, g2, wg, wu, wd)


def matmul_tn(a, b, name, acc_init=None, to_chips=()):
    na, K, M = a.shape
    nb, _, N = b.shape
    n = max(na, nb)
    tk = min(512, K)
    tn = N if N <= 1024 else N // 2
    nk = K // tk
    grid = (n, N // tn, nk)
    has_init = acc_init is not None
    nx = len(to_chips)

    def body(a_ref, b_ref, *refs):
        init_ref = refs[0] if has_init else None
        refs = refs[1:] if has_init else refs
        o_ref = refs[nx]
        if nx:
            start, finish = _to_chips_phases(refs[:nx], refs[nx + 1:2 * nx + 1], *refs[2 * nx + 1:])
            pos, total = _grid_step(grid)
            pl.when(pos == 0)(start)
        kk = pl.program_id(2)

        @pl.when(kk == 0)
        def _():
            o_ref[...] = init_ref[...] if has_init else jnp.zeros_like(o_ref)

        o_ref[...] += _dot(a_ref[...], b_ref[...], 0, 0)
        if nx:
            pl.when(pos == total - 1)(finish)

    out_spec = pl.BlockSpec((None, M, tn), lambda s, j, kk: (s, 0, j))
    in_specs = [pl.BlockSpec((None, tk, M), lambda s, j, kk: (s if na > 1 else 0, kk, 0)),
                pl.BlockSpec((None, tk, tn), lambda s, j, kk: (s if nb > 1 else 0, kk, j))]
    args = [a, b]
    if has_init:
        in_specs.append(out_spec)
        args.append(acc_init)
    out = pl.pallas_call(
        body, name=name,
        grid=grid,
        in_specs=in_specs + [ANY] * nx,
        out_specs=[out_spec] + [ANY] * nx,
        out_shape=[jax.ShapeDtypeStruct((n, M, N), F32)] + _to_chips_shapes(to_chips),
        scratch_shapes=_to_chips_scratch(nx) if nx else [],
        compiler_params=_params(*(("arbitrary",) * 3 if nx else ("parallel", "parallel", "arbitrary"))),
    )(*args, *to_chips)
    return out if nx else out[0]


def matmul_tn_pair(a, b1, b2, name):
    K, M = a.shape
    n, _, N = b1.shape
    tk = min(512, K)

    def body(a_ref, b1_ref, b2_ref, o1_ref, o2_ref):
        @pl.when(pl.program_id(1) == 0)
        def _():
            o1_ref[...] = jnp.zeros_like(o1_ref)
            o2_ref[...] = jnp.zeros_like(o2_ref)

        at = a_ref[...].T
        o1_ref[...] += _dot(at, b1_ref[...])
        o2_ref[...] += _dot(at, b2_ref[...])

    b_spec = pl.BlockSpec((None, tk, N), lambda s, kk: (s, kk, 0))
    o_spec = pl.BlockSpec((None, M, N), lambda s, kk: (s, 0, 0))
    return pl.pallas_call(
        body, name=name,
        grid=(n, K // tk),
        in_specs=[pl.BlockSpec((tk, M), lambda s, kk: (kk, 0)), b_spec, b_spec],
        out_specs=[o_spec, o_spec],
        out_shape=[jax.ShapeDtypeStruct((n, M, N), F32)] * 2,
        compiler_params=_params("parallel", "arbitrary"),
    )(a, b1, b2)


PIECE_COLS = 1024
N_PIECE_BLOCKS = D_IN // PIECE_COLS


def _piece_blocks(pieces):
    out, col = [], 0
    for arr, width in pieces:
        if arr is not None:
            out.append((arr, col // PIECE_COLS, width // PIECE_COLS))
        col += width
    assert col == D_IN
    return out


def _piece_feed(p_refs, blocks, buf, sems, tile_of, pos, total):
    def present(blk):
        ok = None
        for _, b0, nb in blocks:
            mine = (blk >= b0) & (blk < b0 + nb)
            ok = mine if ok is None else ok | mine
        return ok

    def fetch(step):
        blk, rows = tile_of(step)
        for p_ref, (_, b0, nb) in zip(p_refs, blocks):
            for t in range(nb):
                @pl.when(blk == b0 + t)
                def _(p_ref=p_ref, t=t):
                    pltpu.make_async_copy(p_ref.at[rows, pl.ds(t * PIECE_COLS, PIECE_COLS)], buf.at[step % 2],
                                          sems.at[step % 2]).start()

    @pl.when(pos == 0)
    def _():
        fetch(pos)

    @pl.when(pos + 1 < total)
    def _():
        fetch(pos + 1)

    def landed():
        slot = pos % 2
        pltpu.make_async_copy(p_refs[0].at[pl.ds(0, buf.shape[1]), pl.ds(0, PIECE_COLS)], buf.at[slot],
                              sems.at[slot]).wait()
        return buf.at[slot]

    return present(tile_of(pos)[0]), landed


def matmul_tn_pieces(a, pieces, name, acc_init=None, to_chips=()):
    K, M = a.shape
    blocks = _piece_blocks(pieces)
    tk = min(1024, K)
    nk = K // tk
    grid = (N_PIECE_BLOCKS, nk)
    has_init = acc_init is not None
    nx, npc = len(to_chips), len(blocks)

    def body(a_ref, *refs):
        p_refs = refs[:npc]
        refs = refs[npc:]
        init_ref = refs[0] if has_init else None
        refs = refs[1:] if has_init else refs
        o_ref = refs[nx]
        buf, sems = refs[2 * nx + 1:2 * nx + 3]
        pos, total = _grid_step(grid)
        if nx:
            start, finish = _to_chips_phases(refs[:nx], refs[nx + 1:2 * nx + 1], *refs[2 * nx + 3:])
            pl.when(pos == 0)(start)
        here, landed = _piece_feed(p_refs, blocks, buf, sems,
                                   lambda s: (s // nk, pl.ds(pl.multiple_of((s % nk) * tk, tk), tk)), pos, total)

        @pl.when(pl.program_id(1) == 0)
        def _():
            o_ref[...] = init_ref[...] if has_init else jnp.zeros_like(o_ref)

        @pl.when(here)
        def _():
            o_ref[...] += _dot(a_ref[...], landed()[...], 0, 0)

        if nx:
            pl.when(pos == total - 1)(finish)

    out_spec = pl.BlockSpec((M, PIECE_COLS), lambda blk, kk: (0, blk))
    in_specs = [pl.BlockSpec((tk, M), lambda blk, kk: (kk, 0))] + [ANY] * npc
    args = [a] + [arr for arr, _, _ in blocks]
    if has_init:
        in_specs.append(out_spec)
        args.append(acc_init)
    out = pl.pallas_call(
        body, name=name,
        grid=grid,
        in_specs=in_specs + [ANY] * nx,
        out_specs=[out_spec] + [ANY] * nx,
        out_shape=[jax.ShapeDtypeStruct((M, D_IN), F32)] + _to_chips_shapes(to_chips),
        scratch_shapes=[pltpu.VMEM((2, tk, PIECE_COLS), BF16), pltpu.SemaphoreType.DMA((2,))]
        + (_to_chips_scratch(nx) if nx else []),
        compiler_params=_params("arbitrary", "arbitrary"),
    )(*args, *to_chips)
    return out if nx else out[0]


def dhx_normbwd(pieces, w, x, dx_res, nw, sc, name, to_chips=()):
    L = x.shape[0]
    tm = min(PROJ_ROWS, L)
    blocks = _piece_blocks(pieces)
    grid = (L // tm, N_PIECE_BLOCKS)
    nx, npc = len(to_chips), len(blocks)

    def body(*refs):
        p_refs = refs[:npc]
        w_ref, x_ref, res_ref, nw_ref, sc_ref = refs[npc:npc + 5]
        refs = refs[npc + 5:]
        dx_ref, sums_ref = refs[nx:nx + 2]
        acc, buf, sems = refs[2 * nx + 2:2 * nx + 5]
        pos, total = _grid_step(grid)
        if nx:
            start, finish = _to_chips_phases(refs[:nx], refs[nx + 2:2 * nx + 2], *refs[2 * nx + 5:])
            pl.when(pos == 0)(start)
            pl.when(pos == total - 1)(finish)
        here, landed = _piece_feed(
            p_refs, blocks, buf, sems,
            lambda s: (s % N_PIECE_BLOCKS, pl.ds(pl.multiple_of((s // N_PIECE_BLOCKS) * tm, tm), tm)), pos, total)
        i, blk = pl.program_id(0), pl.program_id(1)

        @pl.when((i == 0) & (blk == 0))
        def _():
            sums_ref[...] = jnp.zeros_like(sums_ref)

        @pl.when(blk == 0)
        def _():
            acc[...] = jnp.zeros_like(acc)

        @pl.when(here)
        def _():
            acc[...] += _dot(landed()[...], w_ref[...], 1, 1)

        @pl.when(blk == N_PIECE_BLOCKS - 1)
        def _():
            dhx = acc[...]
            xv = x_ref[...]
            r = lax.rsqrt(_lanemean(xv * xv) + EPS)
            n0 = xv * r
            nw = nw_ref[...]
            dn = dhx * (1.0 + sc_ref[...])
            dn0 = dn * nw
            dx_ref[...] = res_ref[...] + r * (dn0 - n0 * _lanemean(dn0 * n0))
            sums_ref[0:1, :] += _rowsum(dhx)
            sums_ref[1:2, :] += _rowsum(dhx * n0 * nw)
            sums_ref[2:3, :] += _rowsum(dn * n0)

    row = pl.BlockSpec((tm, D), lambda i, blk: (i, 0))
    vec = pl.BlockSpec((1, D), lambda i, blk: (0, 0))
    return pl.pallas_call(
        body, name=name,
        grid=grid,
        in_specs=[ANY] * npc + [pl.BlockSpec((D, PIECE_COLS), lambda i, blk: (0, blk)), row, row, vec, vec] + [ANY] * nx,
        out_specs=[row, pl.BlockSpec((8, D), lambda i, blk: (0, 0))] + [ANY] * nx,
        out_shape=[jax.ShapeDtypeStruct((L, D), F32), jax.ShapeDtypeStruct((8, D), F32)] + _to_chips_shapes(to_chips),
        scratch_shapes=[pltpu.VMEM((tm, D), F32), pltpu.VMEM((2, tm, PIECE_COLS), BF16), pltpu.SemaphoreType.DMA((2,))]
        + (_to_chips_scratch(nx) if nx else []),
        compiler_params=_params("arbitrary", "arbitrary"),
    )(*[arr for arr, _, _ in blocks], w, x, dx_res, nw, sc, *to_chips)


SMALL_ROWS = 24


def _rope_tables(L):
    rows = L // 64
    row = jnp.repeat(jnp.arange(rows, dtype=F32), 64)
    col = jnp.tile(jnp.arange(64, dtype=F32), rows)
    freqs = 10000.0 ** (-jnp.arange(RT_DK // 4, dtype=F32) / (RT_DK // 4))
    ang = jnp.concatenate([row[:, None] * freqs, col[:, None] * freqs], axis=-1)
    cos, sin = jnp.cos(ang), jnp.sin(ang)
    return jnp.concatenate([cos, cos], axis=1), jnp.concatenate([-sin, sin], axis=1)


def _pieces(hq, hf_f, hf_b, hi, hg, rq, rk, rv, rg, ga, gb):
    widths = (D, D, D, D, D, D, D, 2 * D, 2 * D, D, D)
    return list(zip((hq, hf_f, hf_b, hi, hg, rq, rk, rv, rg, ga, gb), widths))


def _lane0(a):
    return a[:, 0, 0]


def _pack_small(rows):
    out = [r.reshape(1, D) for r in rows]
    out += [jnp.zeros((1, D), F32)] * (SMALL_ROWS - len(out))
    return jnp.concatenate(out, axis=0)


def _sibling_sums(gs, names, place):
    core, core_arg, _ = place

    def other_half(g):
        if g.ndim == 2:
            g = g.reshape(g.shape[0], N_SHARD, g.shape[1] // N_SHARD).transpose(1, 0, 2)
        h = g.shape[1] // 2
        return lax.dynamic_slice_in_dim(g, (1 - core) * h, h, axis=1).astype(BF16)

    payload = [other_half(g) for g in gs]
    received = rs_to_sibling(payload, "rs_to_sibling_" + names[0])
    return [rs_add_sibling(g, r, core_arg, "rs_add_sibling_" + k) for g, r, k in zip(gs, received, names)]


def local_step(x, ctx, target, mod_x, mod_c, lb_f, lb_b, lg_f, lg_b, nw1, nw2, hgw, fw, w, rest=None, place=None):
    L, Lc = x.shape[0], ctx.shape[0]
    sh1, sc1, g1, sh2, sc2, g2 = (mod_x[i:i + 1] for i in range(6))
    sh1c, sc1c = mod_c[0:1], mod_c[1:2]
    cosf, sinf = _rope_tables(L)
    cosc, sinc = jnp.ones((Lc, RT_DK), F32), jnp.zeros((Lc, RT_DK), F32)
    zero_h = jnp.zeros((HEADS, HG_D, HG_D), F32)
    zero_r = jnp.zeros((HEADS, RT_DV, RT_DK), F32)

    pc, hxc = normmod_matmul(ctx, nw1, sh1c, sc1c, w["w_in"], "ctx_in_proj")
    _, s_hf, cb_hf = hgrn_scan_fwd(pc, lb_f, zero_h, COL_HFF, False, "ctx_hgrn_f")
    _, s_hb, cb_hb = hgrn_scan_fwd(pc, lb_b, zero_h, COL_HFB, True, "ctx_hgrn_b")
    _, s_rf, cb_rf = ret_scan_fwd(pc, cosc, sinc, lg_f, zero_r, False, "ctx_ret_f")
    _, s_rb, cb_rb = ret_scan_fwd(pc, cosc, sinc, lg_b, zero_r, True, "ctx_ret_b")
    if rest is None:
        p, hx = normmod_matmul(x, nw1, sh1, sc1, w["w_in"], "in_proj")
    else:
        p, hx, g_pa, g_pb, g_out, g_wg, g_wu, g_wd = normmod_matmul(x, nw1, sh1, sc1, w["w_in"], "in_proj", gather=rest)
        w = dict(w, w_pa=g_pa.reshape(D, D), w_pb=g_pb.reshape(2 * D, D), w_out=g_out.reshape(D, D),
                 wg=g_wg, wu=g_wu, wd=g_wd)
    ohf, _, xb_hf = hgrn_scan_fwd(p, lb_f, s_hf, COL_HFF, False, "hgrn_f")
    ohb, _, xb_hb = hgrn_scan_fwd(p, lb_b, s_hb, COL_HFB, True, "hgrn_b")
    orf, _, xb_rf = ret_scan_fwd(p, cosf, sinf, lg_f, s_rf, False, "ret_f")
    orb, _, xb_rb = ret_scan_fwd(p, cosf, sinf, lg_b, s_rb, True, "ret_b")
    x1, x_mix, merged, ya, yb = mix_fwd(ohf, ohb, orf, orb, p, x, g1, hgw, w["w_pa"], w["w_pb"], w["w_out"], "mix_fwd")
    hx2, gg, uu, hh, ff, dx2, sums_f = ffn_fwd(x1, target, nw2, sh2, sc2, g2, fw, w["wg"], w["wu"], w["wd"], "ffn_fwd")

    d_f, d_g, d_u, dx1, sums_fb = ffn_bwd(dx2, x1, ff, gg, uu, nw2, sc2, g2, w["wg"], w["wu"], w["wd"], "ffn_bwd")
    dw_gate, dw_up = matmul_tn_pair(hx2, d_g, d_u, "dw_ffn_gate_up")
    grads = {"wg": dw_gate, "wu": dw_up, "wd": matmul_tn(hh, d_f[None], "dw_ffn_down")}
    dxm, d_a, d_b, dga, dgb, dhg, drg, dohg, dort, sums_m = mix_bwd(
        dx1, x_mix, ya, yb, ohf, ohb, orf, orb, p, g1, hgw, w["w_pa"], w["w_pb"], w["w_out"], "mix_bwd")
    grads["w_out"] = matmul_tn(merged[None], dxm[None], "dw_out").reshape(N_SHARD, D // N_SHARD, D)
    grads["w_pa"] = matmul_tn(ya[None], d_a[None], "dw_proj_hgrn").reshape(N_SHARD, D // N_SHARD, D)
    grads["w_pb"] = matmul_tn(yb[None], d_b[None], "dw_proj_ret").reshape(N_SHARD, 2 * D // N_SHARD, D)

    rq1, rk1, rv1, dlgf_x, ds_rf = ret_scan_bwd(p, cosf, sinf, lg_f, xb_rf, dort, zero_r, None, False, "ret_f_bwd")
    drq, drk, drv, dlgb_x, ds_rb = ret_scan_bwd(p, cosf, sinf, lg_b, xb_rb, dort, zero_r, (rq1, rk1, rv1), True, "ret_b_bwd")
    hq1, dzf, hv1, dlbf_x, ds_hf = hgrn_scan_bwd(p, lb_f, xb_hf, dohg, zero_h, None, COL_HFF, False, "hgrn_f_bwd")
    dhq, dzb, dhv, dlbb_x, ds_hb = hgrn_scan_bwd(p, lb_b, xb_hb, dohg, zero_h, (hq1, hv1), COL_HFB, True, "hgrn_b_bwd")
    dp = _pieces(dhq, dzf, dzb, dhv, dhg, drq, drk, drv, drg, dga, dgb)
    others = ["w_pa", "w_pb", "w_out", "wg", "wu", "wd"]
    if place is None:
        dw_in = matmul_tn_pieces(hx, dp, "dw_in")
    else:
        sums_o = _sibling_sums([grads[k] for k in others], others, place)
        dw_in, *recv_o = matmul_tn_pieces(hx, dp, "dw_in", to_chips=[a16 for _, a16 in sums_o])

    zc = jnp.zeros((Lc, D), F32)
    zc2 = jnp.zeros((Lc, 2 * D), F32)
    crq1, crk1, crv1, dlgf_c, _ = ret_scan_bwd(pc, cosc, sinc, lg_f, cb_rf, zc2, ds_rf, None, False, "ctx_ret_f_bwd")
    cdrq, cdrk, cdrv, dlgb_c, _ = ret_scan_bwd(pc, cosc, sinc, lg_b, cb_rb, zc2, ds_rb, (crq1, crk1, crv1), True, "ctx_ret_b_bwd")
    chq1, cdzf, chv1, dlbf_c, _ = hgrn_scan_bwd(pc, lb_f, cb_hf, zc, ds_hf, None, COL_HFF, False, "ctx_hgrn_f_bwd")
    cdhq, cdzb, cdhv, dlbb_c, _ = hgrn_scan_bwd(pc, lb_b, cb_hb, zc, ds_hb, (chq1, chv1), COL_HFB, True, "ctx_hgrn_b_bwd")
    dpc = _pieces(cdhq, cdzf, cdzb, cdhv, None, cdrq, cdrk, cdrv, None, None, None)
    _, sums_c = dhx_normbwd(dpc, w["w_in"], ctx, zc, nw1, sc1c, "dctx_in_proj")
    grads["w_in"] = matmul_tn_pieces(hxc, dpc, "dw_in_ctx", acc_init=dw_in)
    if place is None:
        dx, sums_x = dhx_normbwd(dp, w["w_in"], x, dx1, nw1, sc1, "dx_in_proj")
    else:
        sums_i = _sibling_sums([grads["w_in"]], ["w_in"], place)
        dx, sums_x, recv_i = dhx_normbwd(dp, w["w_in"], x, dx1, nw1, sc1, "dx_in_proj", to_chips=[sums_i[0][1]])
        names = ["w_in"] + others
        halves = [rs_add_chips(a, r, place[2], "rs_add_chips_" + k)
                  for (a, _), r, k in zip(sums_i + sums_o, [recv_i] + recv_o, names)]
        grads = dict(zip(names, rs_join_halves(halves, "rs_join_halves")))

    def lg_row(f, b):
        return jnp.concatenate([_lane0(f), _lane0(b), jnp.zeros((D - 2 * HEADS,), F32)])

    small = _pack_small([
        sums_x[0], sums_x[1], sums_m[0], sums_fb[1], sums_fb[2], sums_fb[0],
        sums_c[0], sums_c[1],
        sums_x[2], sums_c[2], sums_fb[3], sums_m[1], sums_f[0],
        dlbf_x, dlbf_c, dlbb_x, dlbb_c,
        lg_row(dlgf_x, dlgb_x), lg_row(dlgf_c, dlgb_c),
        sums_f[1],
    ])
    return dx, grads, small


MESH = pl.DeviceIdType.MESH
ANY = pl.BlockSpec(memory_space=pl.ANY)
N_DEV = 8


def _place():
    return lax.axis_index("x"), lax.axis_index("y"), lax.axis_index("c")


def _other_chips(x, y):
    return [(1 - x, y), (x, 1 - y), (1 - x, 1 - y)]


def allgather8(xs, name):
    m, n = xs.shape

    def body(x_ref, out_ref, send_sems, recv_sems, local_sem):
        x, y, c = _place()
        me, sibling = (x, y, c), (x, y, 1 - c)
        chips = _other_chips(x, y)

        def rows(px, py, pc):
            return out_ref.at[pl.ds((4 * px + 2 * py + pc) * m, m), :]

        def copy(k, block, to, src=None):
            return pltpu.make_async_remote_copy(
                src_ref=rows(*block) if src is None else src, dst_ref=rows(*block),
                send_sem=send_sems.at[k], recv_sem=recv_sems.at[k], device_id=to, device_id_type=MESH)

        mine = pltpu.make_async_copy(x_ref, rows(*me), local_sem)
        mine.start()
        first = [copy(0, me, sibling, src=x_ref)]
        first += [copy(1 + j, me, (*chip, c), src=x_ref) for j, chip in enumerate(chips)]
        for cp in first:
            cp.start()
        passed = [copy(4 + j, (*chip, c), sibling) for j, chip in enumerate(chips)]
        for j, chip in enumerate(chips):
            copy(1 + j, (*chip, c), me).wait_recv()
            passed[j].start()
        copy(0, sibling, me).wait_recv()
        for j, chip in enumerate(chips):
            copy(4 + j, (*chip, 1 - c), me).wait_recv()
        for cp in first + passed:
            cp.wait_send()
        mine.wait()

    return pl.pallas_call(
        body, name=name,
        out_shape=jax.ShapeDtypeStruct((N_DEV * m, n), xs.dtype),
        in_specs=[pl.BlockSpec(memory_space=pltpu.VMEM)],
        out_specs=pl.BlockSpec(memory_space=pltpu.VMEM),
        scratch_shapes=[pltpu.SemaphoreType.DMA((7,)), pltpu.SemaphoreType.DMA((7,)), pltpu.SemaphoreType.DMA],
    )(xs)


def _gather_phases(ins, outs, send_sems, recv_sems, local_sems):
    n = len(ins)
    x, y, c = _place()
    chips = _other_chips(x, y)

    def rows(i, core):
        h = ins[i].shape[0] // 2
        return pl.ds(pl.multiple_of(core * h, 16), h)

    def region(i, k, rs):
        if len(outs[i].shape) == 2:
            cols = ins[i].shape[1]
            return outs[i].at[rs, pl.ds(pl.multiple_of(k * cols, 128), cols)]
        return outs[i].at[k, rs, :]

    def landed(i, chip, core):
        return region(i, 2 * chip[0] + chip[1], rows(i, core))

    def copy(i, k, src, dst, to):
        return pltpu.make_async_remote_copy(src_ref=src, dst_ref=dst, send_sem=send_sems.at[6 * i + k],
                                            recv_sem=recv_sems.at[6 * i + k], device_id=to, device_id_type=MESH)

    def local(i):
        r = ins[i].shape[0] // LOCAL_PIECES
        return [pltpu.make_async_copy(ins[i].at[pl.ds(q * r, r), :], region(i, 2 * x + y, pl.ds(q * r, r)),
                                      local_sems.at[LOCAL_PIECES * i + q]) for q in range(LOCAL_PIECES)]

    def send(i, j):
        return copy(i, j, ins[i].at[rows(i, c), :], landed(i, (x, y), c), (*chips[j], c))

    def arrived(i, j, core, k):
        return copy(i, k, ins[i].at[rows(i, core), :], landed(i, chips[j], core), (x, y, 1 - c))

    def passed(i, j):
        return copy(i, 3 + j, landed(i, chips[j], c), landed(i, chips[j], c), (x, y, 1 - c))

    def start():
        for i in range(n):
            for lc in local(i):
                lc.start()
            for j in range(3):
                send(i, j).start()

    def forward():
        for i in range(n):
            for j in range(3):
                arrived(i, j, c, j).wait_recv()
                passed(i, j).start()

    def finish():
        for i in range(n):
            for j in range(3):
                arrived(i, j, 1 - c, 3 + j).wait_recv()
        for i in range(n):
            for j in range(3):
                send(i, j).wait_send()
                passed(i, j).wait_send()
            for lc in local(i):
                lc.wait()

    return start, forward, finish


LOCAL_PIECES = 4


def _gather_scratch(n):
    return [pltpu.SemaphoreType.DMA((6 * n,)), pltpu.SemaphoreType.DMA((6 * n,)),
            pltpu.SemaphoreType.DMA((LOCAL_PIECES * n,))]


def gather_columns(shard, name):
    def body(in_ref, out_ref, *sems):
        start, forward, finish = _gather_phases([in_ref], [out_ref], *sems)
        start()
        forward()
        finish()

    return pl.pallas_call(
        body, name=name,
        out_shape=jax.ShapeDtypeStruct((shard.shape[0], N_SHARD * shard.shape[1]), shard.dtype),
        in_specs=[ANY], out_specs=ANY,
        scratch_shapes=_gather_scratch(1),
    )(shard)


def rs_to_sibling(payloads, name):
    n = len(payloads)

    def body(*refs):
        ins, outs = refs[:n], refs[n:2 * n]
        send_sems, recv_sems = refs[2 * n:]
        x, y, c = _place()
        copies = []
        for i in range(n):
            cp = pltpu.make_async_remote_copy(src_ref=ins[i], dst_ref=outs[i], send_sem=send_sems.at[i],
                                              recv_sem=recv_sems.at[i], device_id=(x, y, 1 - c), device_id_type=MESH)
            cp.start()
            copies.append(cp)
        for cp in copies:
            cp.wait()

    return pl.pallas_call(
        body, name=name,
        out_shape=[jax.ShapeDtypeStruct(g.shape, g.dtype) for g in payloads],
        in_specs=[ANY] * n, out_specs=[ANY] * n,
        scratch_shapes=[pltpu.SemaphoreType.DMA((n,)), pltpu.SemaphoreType.DMA((n,))],
    )(*payloads)


def _to_chips_phases(ins, outs, send_sems, recv_sems):
    def copies():
        x, y, c = _place()
        return [pltpu.make_async_remote_copy(
            src_ref=ins[i].at[2 * px + py], dst_ref=outs[i].at[j], send_sem=send_sems.at[3 * i + j],
            recv_sem=recv_sems.at[3 * i + j], device_id=(px, py, c), device_id_type=MESH)
            for i in range(len(ins)) for j, (px, py) in enumerate(_other_chips(x, y))]

    def start():
        for cp in copies():
            cp.start()

    def finish():
        for cp in copies():
            cp.wait()

    return start, finish


def _to_chips_shapes(parts):
    return [jax.ShapeDtypeStruct((3,) + a.shape[1:], a.dtype) for a in parts]


def _to_chips_scratch(n):
    return [pltpu.SemaphoreType.DMA((3 * n,)), pltpu.SemaphoreType.DMA((3 * n,))]


def rs_join_halves(fulls, name):
    n = len(fulls)

    def body(*refs):
        outs = refs[n:2 * n]
        send_sems, recv_sems = refs[2 * n:]
        x, y, c = _place()

        def copy(i, core):
            h = fulls[i].shape[0] // 2
            rows = outs[i].at[pl.ds(pl.multiple_of(core * h, 8), h), :]
            return pltpu.make_async_remote_copy(src_ref=rows, dst_ref=rows, send_sem=send_sems.at[i],
                                                recv_sem=recv_sems.at[i], device_id=(x, y, 1 - c), device_id_type=MESH)

        sent = [copy(i, c) for i in range(n)]
        for cp in sent:
            cp.start()
        for i in range(n):
            copy(i, 1 - c).wait_recv()
        for cp in sent:
            cp.wait_send()

    return pl.pallas_call(
        body, name=name,
        out_shape=[jax.ShapeDtypeStruct(a.shape, a.dtype) for a in fulls],
        in_specs=[ANY] * n, out_specs=[ANY] * n,
        input_output_aliases={i: i for i in range(n)},
        scratch_shapes=[pltpu.SemaphoreType.DMA((n,)), pltpu.SemaphoreType.DMA((n,))],
    )(*fulls)


def _row_tile(rows, cols, limit_bytes=2 * 1024 * 1024, mult=8):
    best = mult
    for t in range(mult, rows + 1, mult):
        if rows % t == 0 and t * cols * 4 <= limit_bytes:
            best = t
    return best


def rs_add_sibling(g, recv, c, name):
    _, h, C = recv.shape
    tr = _row_tile(h, C, mult=16)
    nt = h // tr

    def body(c_ref, g_ref, r_ref, o_ref, o16_ref):
        s = g_ref[...] + r_ref[...].astype(F32)
        o_ref[...] = s
        o16_ref[...] = s.astype(BF16)

    blk = pl.BlockSpec((None, tr, C), lambda k, i, c_ref: (k, i, 0))
    if g.ndim == 2:
        g_spec = pl.BlockSpec((tr, C), lambda k, i, c_ref: (c_ref[0] * nt + i, k))
    else:
        g_spec = pl.BlockSpec((None, tr, C), lambda k, i, c_ref: (k, c_ref[0] * nt + i, 0))
    return pl.pallas_call(
        body, name=name,
        grid_spec=pltpu.PrefetchScalarGridSpec(
            num_scalar_prefetch=1, grid=(N_SHARD, nt),
            in_specs=[g_spec, blk],
            out_specs=[blk, blk]),
        out_shape=[jax.ShapeDtypeStruct((N_SHARD, h, C), F32), jax.ShapeDtypeStruct((N_SHARD, h, C), BF16)],
        compiler_params=_params("parallel", "parallel"),
    )(c, g, recv)


def rs_add_chips(part, recv, place, name):
    _, h, C = part.shape
    tr = _row_tile(h, C, mult=16)
    nt = h // tr

    def body(k_ref, p_ref, r_ref, o_ref):
        o_ref[...] = ((p_ref[...] + r_ref[0].astype(F32)) + r_ref[1].astype(F32)) + r_ref[2].astype(F32)

    return pl.pallas_call(
        body, name=name,
        grid_spec=pltpu.PrefetchScalarGridSpec(
            num_scalar_prefetch=1, grid=(nt,),
            in_specs=[pl.BlockSpec((None, tr, C), lambda i, k_ref: (k_ref[0], i, 0)),
                      pl.BlockSpec((3, tr, C), lambda i, k_ref: (0, i, 0))],
            out_specs=pl.BlockSpec((tr, C), lambda i, k_ref: (k_ref[1] * nt + i, 0))),
        out_shape=jax.ShapeDtypeStruct((2 * h, C), F32),
        compiler_params=_params("parallel"),
    )(place, part, recv)


def _adamw_math(w, g, m, v):
    m = ADAM_B1 * m + (1.0 - ADAM_B1) * g
    v = ADAM_B2 * v + (1.0 - ADAM_B2) * (g * g)
    m_hat = m / (1.0 - ADAM_B1 ** ADAM_STEP)
    v_hat = v / (1.0 - ADAM_B2 ** ADAM_STEP)
    delta = -ADAM_LR * (m_hat / (jnp.sqrt(v_hat) + ADAM_EPS) + ADAM_WD * w)
    return delta, m, v


def adamw(w, g, m, v, name):
    R, C = w.shape
    tr = _row_tile(R, C, 1024 * 1024)

    def body(w_ref, g_ref, m_ref, v_ref, d_ref, nm_ref, nv_ref):
        d_ref[...], nm_ref[...], nv_ref[...] = _adamw_math(w_ref[...], g_ref[...], m_ref[...], v_ref[...])

    blk = pl.BlockSpec((tr, C), lambda i: (i, 0))
    return pl.pallas_call(
        body, name=name, grid=(R // tr,), in_specs=[blk] * 4, out_specs=[blk] * 3,
        out_shape=[jax.ShapeDtypeStruct((R, C), F32)] * 3,
        compiler_params=_params("parallel"),
    )(w, g, m, v)


MOD_SH = 6 * D // N_SHARD
PK_ROWS = 16


def mod_fwd(call16, w_sh, b_sh, name):
    def body(c_ref, w_ref, b_ref, o_ref):
        o_ref[...] = _dot(_silu_parts(c_ref[...])[0], w_ref[...], prec=HI) + b_ref[...]

    return pl.pallas_call(body, name=name, out_shape=jax.ShapeDtypeStruct((16, MOD_SH), F32),
                          compiler_params=_params())(call16, w_sh, b_sh)


def prep_small(lbf2, lbb2, theta_row, name):
    def body(f_ref, b_ref, t_ref, lbf_ref, lbb_ref, lg_ref):
        lbf_ref[...] = _sigmoid(f_ref[0:1, :] - f_ref[1:2, :])
        lbb_ref[...] = _sigmoid(b_ref[0:1, :] - b_ref[1:2, :])
        t = t_ref[...]
        lg_ref[...] = jnp.minimum(t, 0.0) - jnp.log(1.0 + jnp.exp(-jnp.abs(t)))

    row = jax.ShapeDtypeStruct((1, D), F32)
    return pl.pallas_call(body, name=name, out_shape=[row, row, row], compiler_params=_params())(lbf2, lbb2, theta_row)


def small_grads(g3, lbf, lbb, theta_row, name):
    def body(g_ref, lbf_ref, lbb_ref, t_ref, pk_ref, aux_ref):
        s = g_ref[0]
        for d in range(1, N_DEV):
            s = s + g_ref[d]
        pk_ref[...] = jnp.zeros_like(pk_ref)
        aux_ref[...] = jnp.zeros_like(aux_ref)
        pk_ref[1:7, :] = s[0:6]
        pk_ref[1:3, :] += s[6:8]
        pk_ref[7:8, :] = s[8:9] + s[9:10]
        pk_ref[8:9, :] = s[10:11]
        lbf, lbb = lbf_ref[...], lbb_ref[...]
        daf = (s[13:14] + s[14:15]) * lbf * (1.0 - lbf)
        dab = (s[15:16] + s[16:17]) * lbb * (1.0 - lbb)
        pk_ref[9:10, :] = daf
        pk_ref[10:11, :] = -daf
        pk_ref[11:12, :] = dab
        pk_ref[12:13, :] = -dab
        pk_ref[13:14, :] = s[11:12]
        pk_ref[14:15, :] = (s[17:18] + s[18:19]) * _sigmoid(-t_ref[...])
        pk_ref[15:16, :] = s[12:13]
        aux_ref[0:2, :] = s[6:8]
        aux_ref[2:3, :] = jnp.broadcast_to(jnp.sum(s[19:20], axis=-1, keepdims=True), (1, D))

    return pl.pallas_call(body, name=name,
                          out_shape=[jax.ShapeDtypeStruct((PK_ROWS, D), F32), jax.ShapeDtypeStruct((8, D), F32)],
                          compiler_params=_params())(g3, lbf, lbb, theta_row)


def mod_bwd(call16, dmod_sh, w_sh, name):
    def body(c_ref, d_ref, w_ref, dw_ref, ds_ref):
        dm = d_ref[...]
        dw_ref[...] = _dot(_silu_parts(c_ref[...])[0], dm, 0, 0, prec=HI)
        ds_ref[...] = jnp.zeros_like(ds_ref)
        ds_ref[0:1, :] = _dot(dm[8:9, :], w_ref[...], 1, 1, prec=HI)

    return pl.pallas_call(body, name=name,
                          out_shape=[jax.ShapeDtypeStruct((D, MOD_SH), F32), jax.ShapeDtypeStruct((8, D), F32)],
                          compiler_params=_params())(call16, dmod_sh, w_sh)


def adamw_small(g4, pk_g, pk_w, pk_m, pk_v, name):
    def body(g4_ref, g_ref, w_ref, m_ref, v_ref, go_ref, d_ref, nm_ref, nv_ref):
        w = w_ref[...]
        ds = ((g4_ref[0:1, :] + g4_ref[16:17, :]) + g4_ref[32:33, :]) + g4_ref[48:49, :]
        row = lax.broadcasted_iota(jnp.int32, (PK_ROWS, D), 0)
        g = jnp.where(row == 0, ds * _silu_parts(w[0:1, :])[1], g_ref[...])
        go_ref[...] = g
        d_ref[...], nm_ref[...], nv_ref[...] = _adamw_math(w, g, m_ref[...], v_ref[...])

    pk = jax.ShapeDtypeStruct((PK_ROWS, D), F32)
    return pl.pallas_call(body, name=name, out_shape=[pk, pk, pk, pk], compiler_params=_params())(g4, pk_g, pk_w, pk_m, pk_v)


def _pack_params(c_ctx, b_mod, n1, n2, lbf, lbb, hgn, th_f, th_b, fin):
    theta = jnp.concatenate([th_f.reshape(HEADS), th_b.reshape(HEADS), jnp.zeros((D - 2 * HEADS,), F32)])
    return jnp.concatenate([c_ctx.reshape(1, D), b_mod.reshape(6, D), n1.reshape(1, D), n2.reshape(1, D), lbf, lbb,
                            hgn.reshape(1, D), theta.reshape(1, D), fin.reshape(1, D)], axis=0)


def _unpack_params(pk):
    return (pk[0], pk[1:7].reshape(1, 6 * D), pk[7:8], pk[8:9], pk[9:11], pk[11:13], pk[13:14],
            pk[14, 0:HEADS].reshape(1, HEADS), pk[14, HEADS:2 * HEADS].reshape(1, HEADS), pk[15])


def kernel(x, c, ctx, c_ctx, w_mod, b_mod, norm1_w, norm2_w, w_in, hg_lb_fwd, hg_lb_bwd, hg_norm_w, rt_theta_fwd, rt_theta_bwd, w_proj_hgrn, w_proj_ret, w_out, w_ffn_gate, w_ffn_up, w_ffn_down, final_norm_w, loss_target, m_c_ctx, m_w_mod, m_b_mod, m_norm1_w, m_norm2_w, m_w_in, m_hg_lb_fwd, m_hg_lb_bwd, m_hg_norm_w, m_rt_theta_fwd, m_rt_theta_bwd, m_w_proj_hgrn, m_w_proj_ret, m_w_out, m_w_ffn_gate, m_w_ffn_up, m_w_ffn_down, m_final_norm_w, v_c_ctx, v_w_mod, v_b_mod, v_norm1_w, v_norm2_w, v_w_in, v_hg_lb_fwd, v_hg_lb_bwd, v_hg_norm_w, v_rt_theta_fwd, v_rt_theta_bwd, v_w_proj_hgrn, v_w_proj_ret, v_w_out, v_w_ffn_gate, v_w_ffn_up, v_w_ffn_down, v_final_norm_w):
    xi, yi, ci = _place()
    dev = 4 * xi + 2 * yi + ci
    chip = 2 * xi + yi
    core_arg = jnp.reshape(ci, (1,)).astype(jnp.int32)
    place_arg = jnp.stack([chip, ci]).astype(jnp.int32)

    c_all = allgather8(jnp.concatenate([c, jnp.zeros((7, D), F32)], axis=0), "gather_c").reshape(N_DEV, 8, D)[:, 0]
    call16 = jnp.concatenate([c_all, c_ctx.reshape(1, D), jnp.zeros((7, D), F32)], axis=0)
    b_sh = lax.dynamic_slice_in_dim(b_mod, chip * MOD_SH, MOD_SH, axis=1)
    mod_sh = mod_fwd(call16, w_mod[0], b_sh, "mod_fwd")
    mod_g = allgather8(mod_sh, "gather_mod").reshape(N_DEV, 16, MOD_SH)
    mod_all = jnp.concatenate([mod_g[0], mod_g[2], mod_g[4], mod_g[6]], axis=1)
    mod_x = lax.dynamic_index_in_dim(mod_all, dev, axis=0, keepdims=False).reshape(6, D)
    mod_c = mod_all[8].reshape(6, D)

    pk_w = _pack_params(c_ctx, b_mod, norm1_w, norm2_w, hg_lb_fwd, hg_lb_bwd, hg_norm_w, rt_theta_fwd, rt_theta_bwd, final_norm_w)
    theta_row = pk_w[14:15]
    lb_f, lb_b, lg_row = prep_small(hg_lb_fwd, hg_lb_bwd, theta_row, "prep_small")
    lg_f = jnp.broadcast_to(lg_row[0, 0:HEADS].reshape(HEADS, 1, 1), (HEADS, 1, RT_DV))
    lg_b = jnp.broadcast_to(lg_row[0, HEADS:2 * HEADS].reshape(HEADS, 1, 1), (HEADS, 1, RT_DV))

    g_in = gather_columns(w_in[0].astype(BF16), "gather_w_in")
    rest = [s[0].astype(BF16) for s in (w_proj_hgrn, w_proj_ret, w_out, w_ffn_gate, w_ffn_up, w_ffn_down)]

    dx, full, small = local_step(x[0], ctx[0], loss_target[0], mod_x, mod_c, lb_f, lb_b, lg_f, lg_b,
                                 norm1_w, norm2_w, hg_norm_w, final_norm_w.reshape(1, D), {"w_in": g_in}, rest,
                                 (ci, core_arg, place_arg))

    g3 = allgather8(small, "gather_small").reshape(N_DEV, SMALL_ROWS, D)
    pk_g, aux = small_grads(g3, lb_f, lb_b, theta_row, "small_grads")
    loss = aux[2, 0]
    dmod16 = jnp.concatenate([
        g3[:, 0:6, :].reshape(N_DEV, 6 * D),
        jnp.concatenate([aux[0], aux[1], jnp.zeros((4 * D,), F32)]).reshape(1, 6 * D),
        jnp.zeros((7, 6 * D), F32)], axis=0)
    dmod_sh = lax.dynamic_slice_in_dim(dmod16, chip * MOD_SH, MOD_SH, axis=1)
    g_wmod, dsilu = mod_bwd(call16, dmod_sh, w_mod[0], "mod_bwd")
    g4 = allgather8(dsilu, "gather_dsilu")
    pk_m = _pack_params(m_c_ctx, m_b_mod, m_norm1_w, m_norm2_w, m_hg_lb_fwd, m_hg_lb_bwd, m_hg_norm_w, m_rt_theta_fwd, m_rt_theta_bwd, m_final_norm_w)
    pk_v = _pack_params(v_c_ctx, v_b_mod, v_norm1_w, v_norm2_w, v_hg_lb_fwd, v_hg_lb_bwd, v_hg_norm_w, v_rt_theta_fwd, v_rt_theta_bwd, v_final_norm_w)
    pk_g, pk_d, pk_nm, pk_nv = adamw_small(g4, pk_g, pk_w, pk_m, pk_v, "adamw_small")

    big = {
        "w_mod": (g_wmod, w_mod, m_w_mod, v_w_mod),
        "w_in": (full["w_in"], w_in, m_w_in, v_w_in),
        "w_pa": (full["w_pa"], w_proj_hgrn, m_w_proj_hgrn, v_w_proj_hgrn),
        "w_pb": (full["w_pb"], w_proj_ret, m_w_proj_ret, v_w_proj_ret),
        "w_out": (full["w_out"], w_out, m_w_out, v_w_out),
        "wg": (full["wg"], w_ffn_gate, m_w_ffn_gate, v_w_ffn_gate),
        "wu": (full["wu"], w_ffn_up, m_w_ffn_up, v_w_ffn_up),
        "wd": (full["wd"], w_ffn_down, m_w_ffn_down, v_w_ffn_down),
    }
    res = {}
    for k, (g, wt, mt, vt) in big.items():
        d, nm, nv = adamw(wt[0], g, mt[0], vt[0], "adamw_" + k)
        res[k] = (g[None], d[None], nm[None], nv[None])

    sm = [_unpack_params(p) for p in (pk_g, pk_d, pk_nm, pk_nv)]
    outs = []
    for t in range(4):
        (s_cctx, s_bmod, s_n1, s_n2, s_lbf, s_lbb, s_hgn, s_thf, s_thb, s_fin) = sm[t]
        outs.append([s_cctx, res["w_mod"][t], s_bmod, s_n1, s_n2, res["w_in"][t], s_lbf, s_lbb, s_hgn, s_thf, s_thb,
                     res["w_pa"][t], res["w_pb"][t], res["w_out"][t], res["wg"][t], res["wu"][t], res["wd"][t], s_fin])
    return (loss, dx[None], *outs[0], *outs[1], *outs[2], *outs[3])
```

```python
import functools

import jax
import jax.numpy as jnp
from jax import lax
from jax.experimental import pallas as pl
from jax.experimental.pallas import tpu as pltpu

F32 = jnp.float32
BF16 = jnp.bfloat16
HI = lax.Precision.HIGHEST

D = 1024
HEADS = 8
HG_D = 128
RT_DK = 128
RT_DV = 256
D_FF = 2816
D_IN = 13312
N_SHARD = 4
IN_SH = D_IN // N_SHARD
FF_SH = D_FF // N_SHARD
HG_CHUNK = 32
SCAN_ROWS = 256
HG_GROUP = 8
RT_GROUP = 4
PROJ_ROWS = 1024
EPS = 1e-6
GN_EPS = 1e-5
Q_SCALE = 128.0 ** -0.5
VMEM_LIMIT = 56 * 1024 * 1024

COL_HQ, COL_HFF, COL_HFB, COL_HI, COL_HG = 0, 8, 16, 24, 32
COL_RQ, COL_RK, COL_RV, COL_RG, COL_GA, COL_GB = 40, 48, 56, 72, 88, 96

ADAM_LR, ADAM_B1, ADAM_B2, ADAM_EPS, ADAM_WD, ADAM_STEP = 0.001, 0.9, 0.999, 1e-08, 0.01, 10


def _params(*sem):
    return pltpu.CompilerParams(dimension_semantics=sem, vmem_limit_bytes=VMEM_LIMIT)


def _dot(a, b, ca=1, cb=0, prec=None):
    return lax.dot_general(a, b, (((ca,), (cb,)), ((), ())), precision=prec, preferred_element_type=F32)


def _bdot(a, b, ca=1, cb=0):
    return _dot(a.astype(BF16), b.astype(BF16), ca, cb)


def _sigmoid(z):
    return 1.0 / (1.0 + jnp.exp(-z))


def _rowsum(a):
    return jnp.sum(a, axis=0, keepdims=True)


def _lanemean(a):
    return jnp.mean(a, axis=-1, keepdims=True)


def _grid_step(grid):
    pos, total = 0, 1
    for d, size in enumerate(grid):
        pos = pos * size + pl.program_id(d)
        total *= size
    return pos, total


def normmod_matmul(x, nw, sh, sc, w, name):
    L = x.shape[0]
    tm = min(PROJ_ROWS, L)
    tn = IN_SH // 2

    def body(x_ref, nw_ref, sh_ref, sc_ref, w_ref, p_ref, hx_ref, hx_scr):
        @pl.when(pl.program_id(1) == 0)
        def _():
            xv = x_ref[...]
            n = xv * lax.rsqrt(_lanemean(xv * xv) + EPS) * nw_ref[...]
            h = (n * (1.0 + sc_ref[...]) + sh_ref[...]).astype(BF16)
            hx_scr[...] = h
            hx_ref[...] = h

        p_ref[...] = _dot(hx_scr[...], w_ref[...])

    vec = pl.BlockSpec((1, D), lambda i, j: (0, 0))
    return pl.pallas_call(
        body, name=name,
        grid=(L // tm, D_IN // tn),
        in_specs=[pl.BlockSpec((tm, D), lambda i, j: (i, 0)), vec, vec, vec,
                  pl.BlockSpec((D, tn), lambda i, j: (0, j))],
        out_specs=[pl.BlockSpec((tm, tn), lambda i, j: (i, j)), pl.BlockSpec((tm, D), lambda i, j: (i, 0))],
        out_shape=[jax.ShapeDtypeStruct((L, D_IN), F32), jax.ShapeDtypeStruct((L, D), BF16)],
        scratch_shapes=[pltpu.VMEM((tm, D), BF16)],
        compiler_params=_params("parallel", "arbitrary"),
    )(x, nw, sh, sc, w)


def _w_halves(w_src, col0, tn, wbuf, wsems, pos):
    @pl.when(pos == 0)
    def _():
        for h in range(2):
            pltpu.make_async_copy(w_src.at[:, pl.ds(pl.multiple_of(col0 + h * tn, 128), tn)], wbuf.at[h],
                                  wsems.at[h]).start()

    for h in range(2):
        @pl.when(pos == h)
        def _(h=h):
            pltpu.make_async_copy(w_src.at[:, pl.ds(0, tn)], wbuf.at[h], wsems.at[h]).wait()


def in_proj_own(x, nw, sh, sc, w_shard, shard_arg, name):
    L = x.shape[0]
    tm = min(PROJ_ROWS, L)
    tn = IN_SH // 2
    grid = (L // tm, 2)

    def body(k_ref, x_ref, nw_ref, sh_ref, sc_ref, w_ref, p_ref, hx_ref, wfull_ref, hx_scr, wbuf, wsems, *sems):
        pos, total = _grid_step(grid)
        start, forward, finish = _gather_phases([w_ref], [wfull_ref], *sems, relations=(0, 1))
        pl.when(pos == 0)(start)
        _w_halves(w_ref, 0, tn, wbuf, wsems, pos)

        @pl.when(pl.program_id(1) == 0)
        def _():
            xv = x_ref[...]
            n = xv * lax.rsqrt(_lanemean(xv * xv) + EPS) * nw_ref[...]
            h = (n * (1.0 + sc_ref[...]) + sh_ref[...]).astype(BF16)
            hx_scr[...] = h
            hx_ref[...] = h

        p_ref[...] = _dot(hx_scr[...], wbuf[pl.program_id(1)])

        @pl.when(pos == total - 1)
        def _():
            forward()
            finish()

    vec = pl.BlockSpec((1, D), lambda i, j, k: (0, 0))
    return pl.pallas_call(
        body, name=name,
        grid_spec=pltpu.PrefetchScalarGridSpec(
            num_scalar_prefetch=1, grid=grid,
            in_specs=[pl.BlockSpec((tm, D), lambda i, j, k: (i, 0)), vec, vec, vec, ANY],
            out_specs=[pl.BlockSpec((tm, tn), lambda i, j, k: (i, 2 * k[0] + j)),
                       pl.BlockSpec((tm, D), lambda i, j, k: (i, 0)), ANY],
            scratch_shapes=[pltpu.VMEM((tm, D), BF16), pltpu.VMEM((2, D, tn), BF16), pltpu.SemaphoreType.DMA((2,))]
            + _gather_scratch(1)),
        out_shape=[jax.ShapeDtypeStruct((L, D_IN), F32), jax.ShapeDtypeStruct((L, D), BF16),
                   jax.ShapeDtypeStruct((D, D_IN), BF16)],
        compiler_params=_params("arbitrary", "arbitrary"),
    )(shard_arg, x, nw, sh, sc, w_shard)


def in_proj_next(hx, w_full, shard_arg, p, name, diag_from=None, gather=()):
    L = hx.shape[0]
    tm = min(PROJ_ROWS, L)
    tn = IN_SH // 2
    grid = (L // tm, 2)
    diag = diag_from is not None
    ng = len(gather)
    assert not (diag and ng)

    def body(k_ref, hx_ref, wf_in, p_in, *refs):
        n_src = 1 if diag else ng
        srcs = refs[:n_src]
        p_ref = refs[n_src]
        dsts = refs[n_src + 1:2 * n_src + 1]
        wbuf, wsems = refs[2 * n_src + 1:2 * n_src + 3]
        sems = refs[2 * n_src + 3:]
        pos, total = _grid_step(grid)
        w_src = dsts[0] if diag else wf_in
        if n_src:
            relations = (2,) if diag else (0, 1, 2)
            start, forward, finish = _gather_phases(srcs, dsts, *sems, relations=relations, own=not diag)
            pl.when(pos == 0)(start)
        _w_halves(w_src, k_ref[0] * IN_SH, tn, wbuf, wsems, pos)
        p_ref[...] = _dot(hx_ref[...], wbuf[pl.program_id(1)])
        if n_src:
            @pl.when(pos == total - 1)
            def _():
                forward()
                finish()

    srcs = [diag_from] if diag else list(gather)
    out_shape = [jax.ShapeDtypeStruct((L, D_IN), F32)]
    if diag:
        out_shape.append(jax.ShapeDtypeStruct(w_full.shape, w_full.dtype))
    out_shape += [jax.ShapeDtypeStruct((N_SHARD,) + s.shape, s.dtype) for s in gather]
    aliases = {3: 0, 2: 1} if diag else {3: 0}
    return pl.pallas_call(
        body, name=name,
        grid_spec=pltpu.PrefetchScalarGridSpec(
            num_scalar_prefetch=1, grid=grid,
            in_specs=[pl.BlockSpec((tm, D), lambda i, j, k: (i, 0)), ANY, ANY] + [ANY] * len(srcs),
            out_specs=[pl.BlockSpec((tm, tn), lambda i, j, k: (i, 2 * k[0] + j))] + [ANY] * len(srcs),
            scratch_shapes=[pltpu.VMEM((2, D, tn), BF16), pltpu.SemaphoreType.DMA((2,))]
            + (_gather_scratch(len(srcs)) if srcs else [])),
        out_shape=out_shape,
        input_output_aliases=aliases,
        compiler_params=_params("arbitrary", "arbitrary"),
    )(shard_arg, hx, w_full, p, *srcs)


def _hgrn_gates(z, lb):
    sg = _sigmoid(z)
    sgn = _sigmoid(-z)
    f = lb + (1.0 - lb) * sg
    k = (1.0 - lb) * sgn
    return sg, sgn, f, k


def _tri_chunks(n, chunk, reverse):
    r = lax.broadcasted_iota(jnp.int32, (n, n), 0)
    c = lax.broadcasted_iota(jnp.int32, (n, n), 1)
    same = (r // chunk) == (c // chunk)
    return jnp.where(same & ((r <= c) if reverse else (r >= c)), 1.0, 0.0).astype(F32)


def _decay3(b, reverse):
    C = b.shape[0]
    t = lax.broadcasted_iota(jnp.int32, (C, C, 1), 0)
    s = lax.broadcasted_iota(jnp.int32, (C, C, 1), 1)
    mask = (t <= s) if reverse else (t >= s)
    return jnp.exp(jnp.where(mask, b[:, None, :] - b[None, :, :], -jnp.inf))


HG_SUB = 16


def _hgrn_pairs(reverse):
    pairs = []
    size = HG_SUB
    while size < HG_CHUNK:
        for lo in range(0, HG_CHUNK, 2 * size):
            first, second = slice(lo, lo + size), slice(lo + size, lo + 2 * size)
            if reverse:
                pairs.append((first, second, lo + size))
            else:
                pairs.append((second, first, lo + size - 1))
        size *= 2
    return pairs


def _hgrn_intra_fwd(q, k, v, b, reverse):
    blocks = []
    for lo in range(0, HG_CHUNK, HG_SUB):
        r = slice(lo, lo + HG_SUB)
        att3 = jnp.sum(q[r][:, None, :] * k[r][None, :, :] * _decay3(b[r], reverse), axis=-1, keepdims=True)
        blocks.append(jnp.sum(att3 * v[r][None, :, :], axis=1))
    for qr, kr, ref in _hgrn_pairs(reverse):
        beta = b[ref:ref + 1]
        att = _bdot(q[qr] * jnp.exp(b[qr] - beta), k[kr] * jnp.exp(beta - b[kr]), 1, 1)
        part = _bdot(att, v[kr])
        n = part.shape[0] // HG_SUB
        for i in range(n):
            blocks[qr.start // HG_SUB + i] += part[i * HG_SUB:(i + 1) * HG_SUB]
    return jnp.concatenate(blocks, axis=0)


def _hgrn_intra_bwd(q, k, v, b, d_o, reverse):
    nb = HG_CHUNK // HG_SUB
    dq, dk, dv = [None] * nb, [None] * nb, [None] * nb
    for i in range(nb):
        r = slice(i * HG_SUB, (i + 1) * HG_SUB)
        e3 = _decay3(b[r], reverse)
        p3 = jnp.sum(d_o[r][:, None, :] * v[r][None, :, :], axis=-1, keepdims=True) * e3
        dq[i] = jnp.sum(p3 * k[r][None, :, :], axis=1)
        dk[i] = jnp.sum(p3 * q[r][:, None, :], axis=0)
        att3 = jnp.sum(q[r][:, None, :] * k[r][None, :, :] * e3, axis=-1, keepdims=True)
        dv[i] = jnp.sum(att3 * d_o[r][:, None, :], axis=0)

    def add(acc, rows, part):
        for i in range(part.shape[0] // HG_SUB):
            acc[rows.start // HG_SUB + i] += part[i * HG_SUB:(i + 1) * HG_SUB]

    for qr, kr, ref in _hgrn_pairs(reverse):
        beta = b[ref:ref + 1]
        fq, fk = jnp.exp(b[qr] - beta), jnp.exp(beta - b[kr])
        qt, kt = q[qr] * fq, k[kr] * fk
        att = _bdot(qt, kt, 1, 1)
        datt = _bdot(d_o[qr], v[kr], 1, 1)
        add(dq, qr, _bdot(datt, kt) * fq)
        add(dk, kr, _bdot(datt, qt, 0, 0) * fk)
        add(dv, kr, _bdot(att, d_o[qr], 0, 0))
    return jnp.concatenate(dq, axis=0), jnp.concatenate(dk, axis=0), jnp.concatenate(dv, axis=0)


def _hgrn_state_step(k, v, b, s_t, last):
    b_last = b[last:last + 1]
    return s_t * jnp.exp(b_last) + _bdot(v, k * jnp.exp(b_last - b), 0, 0)


def hgrn_scan_fwd(p, lb, s0, col_z, reverse, name):
    L = p.shape[0]
    nB = L // SCAN_ROWS
    nC = SCAN_ROWS // HG_CHUNK
    C = HG_CHUNK
    G, W = HG_GROUP, HG_GROUP * HG_D
    last = 0 if reverse else C - 1

    def bmap(b):
        return (nB - 1 - b) if reverse else b

    def body(q_ref, z_ref, v_ref, lb_ref, s0_ref, o_ref, sfin_ref, sblk_ref, s_scr, k_scr, b_scr):
        blk = pl.program_id(1)

        @pl.when(blk == 0)
        def _():
            s_scr[...] = s0_ref[...]

        sblk_ref[...] = s_scr[...]
        _, _, f_all, k_all = _hgrn_gates(z_ref[...], lb_ref[...])
        k_scr[...] = k_all
        b_scr[...] = _dot(_tri_chunks(SCAN_ROWS, C, reverse), jnp.log(f_all), prec=HI)

        def chunk(ci, carry):
            c = (nC - 1 - ci) if reverse else ci
            rows = pl.ds(pl.multiple_of(c * C, C), C)
            for j in range(G):
                lanes = slice(j * HG_D, (j + 1) * HG_D)
                q = q_ref[rows, lanes] * Q_SCALE
                v = v_ref[rows, lanes]
                k = k_scr[rows, lanes]
                b = b_scr[rows, lanes]
                s_t = s_scr[j]
                o_ref[rows, lanes] = _hgrn_intra_fwd(q, k, v, b, reverse) + _bdot(q * jnp.exp(b), s_t, 1, 1)
                s_scr[j] = _hgrn_state_step(k, v, b, s_t, last)
            return carry

        lax.fori_loop(0, nC, chunk, 0)

        @pl.when(blk == nB - 1)
        def _():
            sfin_ref[...] = s_scr[...]

    def col(c0):
        return pl.BlockSpec((SCAN_ROWS, W), lambda h, b: (bmap(b), c0 // G + h))

    state = pl.BlockSpec((G, HG_D, HG_D), lambda h, b: (h, 0, 0))
    return pl.pallas_call(
        body, name=name,
        grid=(HEADS // G, nB),
        in_specs=[col(COL_HQ), col(col_z), col(COL_HI), pl.BlockSpec((1, W), lambda h, b: (0, h)), state],
        out_specs=[pl.BlockSpec((SCAN_ROWS, W), lambda h, b: (bmap(b), h)), state,
                   pl.BlockSpec((None, G, HG_D, HG_D), lambda h, b: (bmap(b), h, 0, 0))],
        out_shape=[jax.ShapeDtypeStruct((L, D), F32),
                   jax.ShapeDtypeStruct((HEADS, HG_D, HG_D), F32),
                   jax.ShapeDtypeStruct((nB, HEADS, HG_D, HG_D), F32)],
        scratch_shapes=[pltpu.VMEM((G, HG_D, HG_D), F32), pltpu.VMEM((SCAN_ROWS, W), F32),
                        pltpu.VMEM((SCAN_ROWS, W), F32)],
        compiler_params=_params("parallel", "arbitrary"),
    )(p, p, p, lb, s0)


def hgrn_scan_bwd(p, lb, s_blocks, d_o, ds_fin, prev, col_z, reverse, name):
    L = p.shape[0]
    nB = L // SCAN_ROWS
    nC = SCAN_ROWS // HG_CHUNK
    C = HG_CHUNK
    G, W = HG_GROUP, HG_GROUP * HG_D
    last = 0 if reverse else C - 1
    has_prev = prev is not None
    out_dt = BF16 if has_prev else F32

    def bmap(b):
        return b if reverse else (nB - 1 - b)

    def body(*refs):
        q_ref, z_ref, v_ref, lb_ref, sblk_ref, do_ref, dsf_ref = refs[:7]
        refs = refs[7:]
        if has_prev:
            pq_ref, pv_ref = refs[:2]
            refs = refs[2:]
        dq_ref, dz_ref, dv_ref, dlb_ref, ds0_ref, st_scr, run_scr, ds_scr, k_scr, b_scr, db_scr, dk_scr = refs
        blk = pl.program_id(1)

        @pl.when(blk == 0)
        def _():
            ds_scr[...] = dsf_ref[...]
            dlb_ref[...] = jnp.zeros_like(dlb_ref)

        tri = _tri_chunks(SCAN_ROWS, C, reverse)
        row = lax.broadcasted_iota(jnp.int32, (C, HG_D), 0)
        _, _, f_all, k_all = _hgrn_gates(z_ref[...], lb_ref[...])
        k_scr[...] = k_all
        b_scr[...] = _dot(tri, jnp.log(f_all), prec=HI)
        run_scr[...] = sblk_ref[...]

        def recompute(ci, carry):
            c = (nC - 1 - ci) if reverse else ci
            rows = pl.ds(pl.multiple_of(c * C, C), C)
            for j in range(G):
                lanes = slice(j * HG_D, (j + 1) * HG_D)
                s_t = run_scr[j]
                st_scr[c, j] = s_t
                run_scr[j] = _hgrn_state_step(k_scr[rows, lanes], v_ref[rows, lanes], b_scr[rows, lanes], s_t, last)
            return carry

        lax.fori_loop(0, nC, recompute, 0)

        def chunk(ci, carry):
            c = ci if reverse else (nC - 1 - ci)
            rows = pl.ds(pl.multiple_of(c * C, C), C)
            for j in range(G):
                lanes = slice(j * HG_D, (j + 1) * HG_D)
                k = k_scr[rows, lanes]
                b = b_scr[rows, lanes]
                q = q_ref[rows, lanes] * Q_SCALE
                v = v_ref[rows, lanes]
                d_o = do_ref[rows, lanes]
                s_t = st_scr[c, j]
                ds_t = ds_scr[j]
                eb = jnp.exp(b)
                b_last = b[last:last + 1]
                eb_last = jnp.exp(b_last)
                kdec = jnp.exp(b_last - b)
                qe = q * eb
                ke = k * kdec
                dq_in, dk_in, dv_in = _hgrn_intra_bwd(q, k, v, b, d_o, reverse)
                dq_tot = _bdot(d_o, s_t, 1, 0) * eb + dq_in
                dke = _bdot(v, ds_t, 1, 0)
                dk_tot = dke * kdec + dk_in
                dv = dv_in + _bdot(ke, ds_t, 1, 1)
                db_last = _rowsum(dke * ke) + eb_last * _rowsum(ds_t * s_t)
                db_scr[rows, lanes] = q * dq_tot - k * dk_tot + jnp.where(row == last, db_last, 0.0)
                dk_scr[rows, lanes] = dk_tot
                dq = dq_tot * Q_SCALE
                if has_prev:
                    dq = dq + pq_ref[rows, lanes]
                    dv = dv + pv_ref[rows, lanes]
                dq_ref[rows, lanes] = dq.astype(out_dt)
                dv_ref[rows, lanes] = dv.astype(out_dt)
                ds_scr[j] = ds_t * eb_last + _bdot(d_o, qe, 0, 0)
            return carry

        lax.fori_loop(0, nC, chunk, 0)

        lb = lb_ref[...]
        sg, sgn, f, _ = _hgrn_gates(z_ref[...], lb)
        g = _dot(tri, db_scr[...], 0, 0, prec=HI) / f - dk_scr[...]
        dz_ref[...] = (g * (1.0 - lb) * sg * sgn).astype(BF16)
        dlb_ref[...] += _rowsum(g * sgn)

        @pl.when(blk == nB - 1)
        def _():
            ds0_ref[...] = ds_scr[...]

    def col(c0):
        return pl.BlockSpec((SCAN_ROWS, W), lambda h, b: (bmap(b), c0 // G + h))

    tile = pl.BlockSpec((SCAN_ROWS, W), lambda h, b: (bmap(b), h))
    state = pl.BlockSpec((G, HG_D, HG_D), lambda h, b: (h, 0, 0))
    in_specs = [col(COL_HQ), col(col_z), col(COL_HI),
                pl.BlockSpec((1, W), lambda h, b: (0, h)),
                pl.BlockSpec((None, G, HG_D, HG_D), lambda h, b: (bmap(b), h, 0, 0)),
                tile, state]
    args = [p, p, p, lb, s_blocks, d_o, ds_fin]
    if has_prev:
        in_specs += [tile, tile]
        args += list(prev)
    return pl.pallas_call(
        body, name=name,
        grid=(HEADS // G, nB),
        in_specs=in_specs,
        out_specs=[tile, tile, tile, pl.BlockSpec((1, W), lambda h, b: (0, h)), state],
        out_shape=[jax.ShapeDtypeStruct((L, D), out_dt), jax.ShapeDtypeStruct((L, D), BF16),
                   jax.ShapeDtypeStruct((L, D), out_dt), jax.ShapeDtypeStruct((1, D), F32),
                   jax.ShapeDtypeStruct((HEADS, HG_D, HG_D), F32)],
        scratch_shapes=[pltpu.VMEM((nC, G, HG_D, HG_D), F32), pltpu.VMEM((G, HG_D, HG_D), F32),
                        pltpu.VMEM((G, HG_D, HG_D), F32)] + [pltpu.VMEM((SCAN_ROWS, W), F32)] * 4,
        compiler_params=_params("parallel", "arbitrary"),
    )(*args)


def _rope(t, cosf, sinf):
    return t * cosf + pltpu.roll(t, RT_DK // 2, 1) * sinf


def _rope_t(d, cosf, sinf):
    return d * cosf + pltpu.roll(d * sinf, RT_DK // 2, 1)


def _ret_decays(lg, reverse):
    C = SCAN_ROWS
    t = lax.broadcasted_iota(jnp.int32, (C, C), 0)
    s = lax.broadcasted_iota(jnp.int32, (C, C), 1)
    delta = ((s - t) if reverse else (t - s)).astype(F32)
    dmat = jnp.where(delta >= 0, jnp.exp(lg * jnp.maximum(delta, 0.0)), 0.0)
    r = lax.broadcasted_iota(jnp.int32, (C, RT_DK), 0)
    pos = ((C - 1 - r) if reverse else r).astype(F32)
    lg1 = lg[:, :RT_DK]
    qdec = jnp.exp(lg1 * (pos + 1.0))
    kdec = jnp.exp(lg1 * (C - 1.0 - pos))
    sdec = jnp.exp(lg1 * float(C))
    return dmat, delta, pos, qdec, kdec, sdec


def ret_scan_fwd(p, cosf, sinf, lg, s0, reverse, name):
    L = p.shape[0]
    C = SCAN_ROWS
    nB = L // C

    def bmap(b):
        return (nB - 1 - b) if reverse else b

    G = RT_GROUP

    def body(q_ref, k_ref, v_ref, cos_ref, sin_ref, lg_ref, s0_ref, o_ref, sfin_ref, sblk_ref, s_scr):
        blk = pl.program_id(1)

        @pl.when(blk == 0)
        def _():
            s_scr[...] = s0_ref[...]

        sblk_ref[...] = s_scr[...]
        cosf, sinf = cos_ref[...], sin_ref[...]
        for j in range(G):
            lk, lv = slice(j * RT_DK, (j + 1) * RT_DK), slice(j * RT_DV, (j + 1) * RT_DV)
            s_t = s_scr[j]
            dmat, _, _, qdec, kdec, sdec = _ret_decays(lg_ref[j], reverse)
            q = _rope(q_ref[:, lk] * Q_SCALE, cosf, sinf)
            k = _rope(k_ref[:, lk], cosf, sinf)
            v = v_ref[:, lv]
            att = _bdot(q, k, 1, 1) * dmat
            o_ref[:, lv] = _bdot(att, v) + _bdot(q * qdec, s_t, 1, 1)
            s_scr[j] = s_t * sdec + _bdot(v, k * kdec, 0, 0)

        @pl.when(blk == nB - 1)
        def _():
            sfin_ref[...] = s_scr[...]

    def col(c0):
        return pl.BlockSpec((C, G * RT_DK), lambda h, b: (bmap(b), c0 // G + h))

    tab = pl.BlockSpec((C, RT_DK), lambda h, b: (bmap(b), 0))
    state = pl.BlockSpec((G, RT_DV, RT_DK), lambda h, b: (h, 0, 0))
    return pl.pallas_call(
        body, name=name,
        grid=(HEADS // G, nB),
        in_specs=[col(COL_RQ), col(COL_RK),
                  pl.BlockSpec((C, G * RT_DV), lambda h, b: (bmap(b), COL_RV // (2 * G) + h)),
                  tab, tab, pl.BlockSpec((G, 1, RT_DV), lambda h, b: (h, 0, 0)), state],
        out_specs=[pl.BlockSpec((C, G * RT_DV), lambda h, b: (bmap(b), h)), state,
                   pl.BlockSpec((None, G, RT_DV, RT_DK), lambda h, b: (bmap(b), h, 0, 0))],
        out_shape=[jax.ShapeDtypeStruct((L, HEADS * RT_DV), F32),
                   jax.ShapeDtypeStruct((HEADS, RT_DV, RT_DK), F32),
                   jax.ShapeDtypeStruct((nB, HEADS, RT_DV, RT_DK), F32)],
        scratch_shapes=[pltpu.VMEM((G, RT_DV, RT_DK), F32)],
        compiler_params=_params("parallel", "arbitrary"),
    )(p, p, p, cosf, sinf, lg, s0)


def ret_scan_bwd(p, cosf, sinf, lg, s_blocks, d_o, ds_fin, prev, reverse, name):
    L = p.shape[0]
    C = SCAN_ROWS
    nB = L // C
    has_prev = prev is not None
    out_dt = BF16 if has_prev else F32
    G = RT_GROUP

    def bmap(b):
        return b if reverse else (nB - 1 - b)

    def body(*refs):
        q_ref, k_ref, v_ref, cos_ref, sin_ref, lg_ref, sblk_ref, do_ref, dsf_ref = refs[:9]
        refs = refs[9:]
        if has_prev:
            pq_ref, pk_ref, pv_ref = refs[:3]
            refs = refs[3:]
        dq_ref, dk_ref, dv_ref, dlg_ref, ds0_ref, ds_scr = refs
        blk = pl.program_id(1)

        @pl.when(blk == 0)
        def _():
            ds_scr[...] = dsf_ref[...]
            dlg_ref[...] = jnp.zeros_like(dlg_ref)

        cosf, sinf = cos_ref[...], sin_ref[...]
        for j in range(G):
            lk, lv = slice(j * RT_DK, (j + 1) * RT_DK), slice(j * RT_DV, (j + 1) * RT_DV)
            s_t = sblk_ref[j]
            ds_t = ds_scr[j]
            dmat, delta, pos, qdec, kdec, sdec = _ret_decays(lg_ref[j], reverse)
            q = _rope(q_ref[:, lk] * Q_SCALE, cosf, sinf)
            k = _rope(k_ref[:, lk], cosf, sinf)
            v = v_ref[:, lv]
            d_o = do_ref[:, lv]
            att_raw = _bdot(q, k, 1, 1)
            datt_m = _bdot(d_o, v, 1, 1) * dmat
            dqd = _bdot(d_o, s_t, 1, 0)
            dkd = _bdot(v, ds_t, 1, 0)
            dq = _bdot(datt_m, k) + dqd * qdec
            dk = _bdot(datt_m, q, 0, 0) + dkd * kdec
            dv = _bdot(att_raw * dmat, d_o, 0, 0) + _bdot(k * kdec, ds_t, 1, 1)
            ds_scr[j] = ds_t * sdec + _bdot(d_o, q * qdec, 0, 0)
            t1 = jnp.sum(_rowsum(datt_m * att_raw * delta), axis=-1, keepdims=True)
            t23 = jnp.sum(_rowsum((pos + 1.0) * qdec * q * dqd + (C - 1.0 - pos) * kdec * k * dkd), axis=-1, keepdims=True)
            t4 = jnp.sum(_rowsum(ds_t * s_t * sdec), axis=-1, keepdims=True) * float(C)
            dlg_ref[j] += jnp.broadcast_to(t1 + t23 + t4, (1, RT_DK))
            if has_prev:
                dq = _rope_t(dq + pq_ref[:, lk], cosf, sinf) * Q_SCALE
                dk = _rope_t(dk + pk_ref[:, lk], cosf, sinf)
                dv = dv + pv_ref[:, lv]
            dq_ref[:, lk] = dq.astype(out_dt)
            dk_ref[:, lk] = dk.astype(out_dt)
            dv_ref[:, lv] = dv.astype(out_dt)

        @pl.when(blk == nB - 1)
        def _():
            ds0_ref[...] = ds_scr[...]

    def col(c0):
        return pl.BlockSpec((C, G * RT_DK), lambda h, b: (bmap(b), c0 // G + h))

    tab = pl.BlockSpec((C, RT_DK), lambda h, b: (bmap(b), 0))
    state = pl.BlockSpec((G, RT_DV, RT_DK), lambda h, b: (h, 0, 0))
    tk = pl.BlockSpec((C, G * RT_DK), lambda h, b: (bmap(b), h))
    tv = pl.BlockSpec((C, G * RT_DV), lambda h, b: (bmap(b), h))
    in_specs = [col(COL_RQ), col(COL_RK),
                pl.BlockSpec((C, G * RT_DV), lambda h, b: (bmap(b), COL_RV // (2 * G) + h)),
                tab, tab, pl.BlockSpec((G, 1, RT_DV), lambda h, b: (h, 0, 0)),
                pl.BlockSpec((None, G, RT_DV, RT_DK), lambda h, b: (bmap(b), h, 0, 0)),
                tv, state]
    args = [p, p, p, cosf, sinf, lg, s_blocks, d_o, ds_fin]
    if has_prev:
        in_specs += [tk, tk, tv]
        args += list(prev)
    return pl.pallas_call(
        body, name=name,
        grid=(HEADS // G, nB),
        in_specs=in_specs,
        out_specs=[tk, tk, tv, pl.BlockSpec((G, 1, RT_DK), lambda h, b: (h, 0, 0)), state],
        out_shape=[jax.ShapeDtypeStruct((L, D), out_dt), jax.ShapeDtypeStruct((L, D), out_dt),
                   jax.ShapeDtypeStruct((L, HEADS * RT_DV), out_dt),
                   jax.ShapeDtypeStruct((HEADS, 1, RT_DK), F32),
                   jax.ShapeDtypeStruct((HEADS, RT_DV, RT_DK), F32)],
        scratch_shapes=[pltpu.VMEM((G, RT_DV, RT_DK), F32)],
        compiler_params=_params("parallel", "arbitrary"),
    )(*args)


def _silu_parts(h):
    s = _sigmoid(h)
    return h * s, s * (1.0 + h * (1.0 - s))


def _head_rms(o):
    outs, rs = [], []
    for h in range(HEADS):
        oh = o[:, h * HG_D:(h + 1) * HG_D]
        r = lax.rsqrt(_lanemean(oh * oh) + EPS)
        outs.append(oh * r)
        rs.append(r)
    return outs, rs


def _group_norm(o):
    outs, rs = [], []
    for h in range(HEADS):
        oh = o[:, h * RT_DV:(h + 1) * RT_DV]
        c = oh - _lanemean(oh)
        r = lax.rsqrt(_lanemean(c * c) + GN_EPS)
        outs.append(c * r)
        rs.append(r)
    return outs, rs


MIX_ROWS = 256
MIX_BWD_ROWS = 128


def _mix_specs(rows):
    def t(w, c=0):
        return pl.BlockSpec((rows, w), lambda i: (i, c))

    return t


def mix_fwd(ohf, ohb, orf, orb, p, x, g1, hgw, w_pa, w_pb, w_out, name):
    L = x.shape[0]
    t = _mix_specs(MIX_ROWS)

    def body(ohf_ref, ohb_ref, orf_ref, orb_ref, hg_ref, rg0_ref, rg1_ref, ga_ref, gb_ref, x_ref, g1_ref, hgw_ref,
             wpa_ref, wpb_ref, wout_ref, x1_ref, xmix_ref, merged_ref, ya_ref, yb_ref):
        nh, _ = _head_rms(ohf_ref[...] + ohb_ref[...])
        ya = jnp.concatenate(nh, axis=1) * hgw_ref[...] * _silu_parts(hg_ref[...])[0]
        gn, _ = _group_norm(orf_ref[...] + orb_ref[...])
        rg = jnp.concatenate([rg0_ref[...], rg1_ref[...]], axis=1)
        yb = jnp.concatenate(gn, axis=1) * _silu_parts(rg)[0]
        ya16, yb16 = ya.astype(BF16), yb.astype(BF16)
        merged = (_sigmoid(ga_ref[...]) * _dot(ya16, wpa_ref[...])
                  + _sigmoid(gb_ref[...]) * _dot(yb16, wpb_ref[...])).astype(BF16)
        x_mix = _dot(merged, wout_ref[...])
        x1_ref[...] = x_ref[...] + g1_ref[...] * x_mix
        xmix_ref[...] = x_mix
        merged_ref[...] = merged
        ya_ref[...] = ya16
        yb_ref[...] = yb16

    vec = pl.BlockSpec((1, D), lambda i: (0, 0))

    def full(a):
        return pl.BlockSpec(a.shape, lambda i: (0, 0), pipeline_mode=pl.Buffered(1))

    return pl.pallas_call(
        body, name=name,
        grid=(L // MIX_ROWS,),
        in_specs=[t(D), t(D), t(2 * D), t(2 * D), t(D, COL_HG // 8), t(D, COL_RG // 8), t(D, COL_RG // 8 + 1),
                  t(D, COL_GA // 8), t(D, COL_GB // 8), t(D), vec, vec, full(w_pa), full(w_pb), full(w_out)],
        out_specs=[t(D), t(D), t(D), t(D), t(2 * D)],
        out_shape=[jax.ShapeDtypeStruct((L, D), F32), jax.ShapeDtypeStruct((L, D), F32),
                   jax.ShapeDtypeStruct((L, D), BF16), jax.ShapeDtypeStruct((L, D), BF16),
                   jax.ShapeDtypeStruct((L, 2 * D), BF16)],
        compiler_params=_params("parallel"),
    )(ohf, ohb, orf, orb, p, p, p, p, p, x, g1, hgw, w_pa, w_pb, w_out)


def mix_bwd(dx1, x_mix, ya, yb, ohf, ohb, orf, orb, p, g1, hgw, w_pa, w_pb, w_out, name):
    L = dx1.shape[0]
    t = _mix_specs(MIX_BWD_ROWS)

    def body(dx1_ref, xmix_ref, ya_ref, yb_ref, ohf_ref, ohb_ref, orf_ref, orb_ref, hg_ref, rg0_ref, rg1_ref,
             ga_ref, gb_ref, g1_ref, hgw_ref, wpa_ref, wpb_ref, wout_ref,
             dxm_ref, da_ref, db_ref, dga_ref, dgb_ref, dhg_ref, drg_ref, dohg_ref, dort_ref, sums_ref):
        @pl.when(pl.program_id(0) == 0)
        def _():
            sums_ref[...] = jnp.zeros_like(sums_ref)

        dx1 = dx1_ref[...]
        dxm = (g1_ref[...] * dx1).astype(BF16)
        dxm_ref[...] = dxm
        dmerged = _dot(dxm, wout_ref[...], 1, 1)
        a = _dot(ya_ref[...], wpa_ref[...])
        bm = _dot(yb_ref[...], wpb_ref[...])
        sa, sb = _sigmoid(ga_ref[...]), _sigmoid(gb_ref[...])
        d_a = (dmerged * sa).astype(BF16)
        d_b = (dmerged * sb).astype(BF16)
        da_ref[...] = d_a
        db_ref[...] = d_b
        dga_ref[...] = (dmerged * a * sa * (1.0 - sa)).astype(BF16)
        dgb_ref[...] = (dmerged * bm * sb * (1.0 - sb)).astype(BF16)
        dya = _dot(d_a, wpa_ref[...], 1, 1)
        dyb = _dot(d_b, wpb_ref[...], 1, 1)

        hgw = hgw_ref[...]
        silu_h, dsilu_h = _silu_parts(hg_ref[...])
        nh, rh = _head_rms(ohf_ref[...] + ohb_ref[...])
        n = jnp.concatenate(nh, axis=1)
        dhg_ref[...] = (dya * n * hgw * dsilu_h).astype(BF16)
        dn = dya * hgw * silu_h
        douts = []
        for h in range(HEADS):
            dnh = dn[:, h * HG_D:(h + 1) * HG_D]
            douts.append(rh[h] * (dnh - nh[h] * _lanemean(dnh * nh[h])))
        dohg_ref[...] = jnp.concatenate(douts, axis=1)

        rg = jnp.concatenate([rg0_ref[...], rg1_ref[...]], axis=1)
        silu_r, dsilu_r = _silu_parts(rg)
        gn, rr = _group_norm(orf_ref[...] + orb_ref[...])
        g = jnp.concatenate(gn, axis=1)
        drg_ref[...] = (dyb * g * dsilu_r).astype(BF16)
        dgn = dyb * silu_r
        douts = []
        for h in range(HEADS):
            dgh = dgn[:, h * RT_DV:(h + 1) * RT_DV]
            douts.append(rr[h] * (dgh - _lanemean(dgh) - gn[h] * _lanemean(dgh * gn[h])))
        dort_ref[...] = jnp.concatenate(douts, axis=1)

        sums_ref[0:1, :] += _rowsum(dx1 * xmix_ref[...])
        sums_ref[1:2, :] += _rowsum(dya * n * silu_h)

    vec = pl.BlockSpec((1, D), lambda i: (0, 0))

    def full(a):
        return pl.BlockSpec(a.shape, lambda i: (0, 0), pipeline_mode=pl.Buffered(1))

    bf = functools.partial(jax.ShapeDtypeStruct, dtype=BF16)
    return pl.pallas_call(
        body, name=name,
        grid=(L // MIX_BWD_ROWS,),
        in_specs=[t(D), t(D), t(D), t(2 * D), t(D), t(D), t(2 * D), t(2 * D),
                  t(D, COL_HG // 8), t(D, COL_RG // 8), t(D, COL_RG // 8 + 1), t(D, COL_GA // 8), t(D, COL_GB // 8),
                  vec, vec, full(w_pa), full(w_pb), full(w_out)],
        out_specs=[t(D), t(D), t(D), t(D), t(D), t(D), t(2 * D), t(D), t(2 * D),
                   pl.BlockSpec((8, D), lambda i: (0, 0))],
        out_shape=[bf((L, D)), bf((L, D)), bf((L, D)), bf((L, D)), bf((L, D)), bf((L, D)), bf((L, 2 * D)),
                   jax.ShapeDtypeStruct((L, D), F32), jax.ShapeDtypeStruct((L, 2 * D), F32),
                   jax.ShapeDtypeStruct((8, D), F32)],
        compiler_params=_params("arbitrary"),
    )(dx1, x_mix, ya, yb, ohf, ohb, orf, orb, p, p, p, p, p, g1, hgw, w_pa, w_pb, w_out)


FFN_ROWS = 512


def ffn_fwd(x1, target, nw2, sh2, sc2, g2, fw, wg, wu, wd, name):
    L = x1.shape[0]
    tm = min(FFN_ROWS, L)

    def body(x1_ref, tgt_ref, nw2_ref, sh2_ref, sc2_ref, g2_ref, fw_ref, wg_ref, wu_ref, wd_ref,
             hx2_ref, g_ref, u_ref, h_ref, f_ref, dx2_ref, sums_ref, hx_scr, acc):
        i, j = pl.program_id(0), pl.program_id(1)

        @pl.when((i == 0) & (j == 0))
        def _():
            sums_ref[...] = jnp.zeros_like(sums_ref)

        @pl.when(j == 0)
        def _():
            xv = x1_ref[...]
            n = xv * lax.rsqrt(_lanemean(xv * xv) + EPS) * nw2_ref[...]
            h = (n * (1.0 + sc2_ref[...]) + sh2_ref[...]).astype(BF16)
            hx_scr[...] = h
            hx2_ref[...] = h
            acc[...] = jnp.zeros_like(acc)

        hx = hx_scr[...]
        g = _dot(hx, wg_ref[...])
        u = _dot(hx, wu_ref[...])
        hh = (_silu_parts(g)[0] * u).astype(BF16)
        g_ref[...] = g
        u_ref[...] = u
        h_ref[...] = hh
        acc[...] += _dot(hh, wd_ref[...])

        @pl.when(j == N_SHARD - 1)
        def _():
            f = acc[...]
            f_ref[...] = f
            x2 = x1_ref[...] + g2_ref[...] * f
            r = lax.rsqrt(_lanemean(x2 * x2) + EPS)
            fw = fw_ref[...]
            e = x2 * r * fw - tgt_ref[...]
            dy = e * (1.0 / D)
            dyw = dy * fw
            dx2_ref[...] = r * dyw - x2 * (r * r * r) * _lanemean(dyw * x2)
            sums_ref[0:1, :] += _rowsum(dy * x2 * r)
            sums_ref[1:2, :] += _rowsum(e * e) * (0.5 / D)

    row = pl.BlockSpec((tm, D), lambda i, j: (i, 0))
    vec = pl.BlockSpec((1, D), lambda i, j: (0, 0))
    sh = pl.BlockSpec((None, tm, FF_SH), lambda i, j: (j, i, 0))
    return pl.pallas_call(
        body, name=name,
        grid=(L // tm, N_SHARD),
        in_specs=[row, row, vec, vec, vec, vec, vec,
                  pl.BlockSpec((None, D, FF_SH), lambda i, j: (j, 0, 0)),
                  pl.BlockSpec((None, D, FF_SH), lambda i, j: (j, 0, 0)),
                  pl.BlockSpec((None, FF_SH, D), lambda i, j: (j, 0, 0))],
        out_specs=[row, sh, sh, sh, row, row, pl.BlockSpec((8, D), lambda i, j: (0, 0))],
        out_shape=[jax.ShapeDtypeStruct((L, D), BF16),
                   jax.ShapeDtypeStruct((N_SHARD, L, FF_SH), F32), jax.ShapeDtypeStruct((N_SHARD, L, FF_SH), F32),
                   jax.ShapeDtypeStruct((N_SHARD, L, FF_SH), BF16),
                   jax.ShapeDtypeStruct((L, D), F32), jax.ShapeDtypeStruct((L, D), F32),
                   jax.ShapeDtypeStruct((8, D), F32)],
        scratch_shapes=[pltpu.VMEM((tm, D), BF16), pltpu.VMEM((tm, D), F32)],
        compiler_params=_params("arbitrary", "arbitrary"),
    )(x1, target, nw2, sh2, sc2, g2, fw, wg, wu, wd)


def ffn_bwd(dx2, x1, f, g, u, nw2, sc2, g2, wg, wu, wd, name):
    L = x1.shape[0]
    tm = min(FFN_ROWS, L)

    def body(dx2_ref, x1_ref, f_ref, g_ref, u_ref, nw2_ref, sc2_ref, g2_ref, wg_ref, wu_ref, wd_ref,
             df_ref, dg_ref, du_ref, dx1_ref, sums_ref, df_scr, acc):
        i, j = pl.program_id(0), pl.program_id(1)

        @pl.when((i == 0) & (j == 0))
        def _():
            sums_ref[...] = jnp.zeros_like(sums_ref)

        @pl.when(j == 0)
        def _():
            dx2 = dx2_ref[...]
            df = (g2_ref[...] * dx2).astype(BF16)
            df_scr[...] = df
            df_ref[...] = df
            sums_ref[0:1, :] += _rowsum(dx2 * f_ref[...])
            acc[...] = jnp.zeros_like(acc)

        dh = _dot(df_scr[...], wd_ref[...], 1, 1)
        gv, uv = g_ref[...], u_ref[...]
        silu_g, dsilu_g = _silu_parts(gv)
        dg = (dh * uv * dsilu_g).astype(BF16)
        du = (dh * silu_g).astype(BF16)
        dg_ref[...] = dg
        du_ref[...] = du
        acc[...] += _dot(dg, wg_ref[...], 1, 1) + _dot(du, wu_ref[...], 1, 1)

        @pl.when(j == N_SHARD - 1)
        def _():
            dhx = acc[...]
            xv = x1_ref[...]
            r = lax.rsqrt(_lanemean(xv * xv) + EPS)
            n0 = xv * r
            nw = nw2_ref[...]
            dn2 = dhx * (1.0 + sc2_ref[...])
            dn0 = dn2 * nw
            dx1_ref[...] = dx2_ref[...] + r * (dn0 - n0 * _lanemean(dn0 * n0))
            sums_ref[1:2, :] += _rowsum(dhx)
            sums_ref[2:3, :] += _rowsum(dhx * n0 * nw)
            sums_ref[3:4, :] += _rowsum(dn2 * n0)

    row = pl.BlockSpec((tm, D), lambda i, j: (i, 0))
    vec = pl.BlockSpec((1, D), lambda i, j: (0, 0))
    sh = pl.BlockSpec((None, tm, FF_SH), lambda i, j: (j, i, 0))
    return pl.pallas_call(
        body, name=name,
        grid=(L // tm, N_SHARD),
        in_specs=[row, row, row, sh, sh, vec, vec, vec,
                  pl.BlockSpec((None, D, FF_SH), lambda i, j: (j, 0, 0)),
                  pl.BlockSpec((None, D, FF_SH), lambda i, j: (j, 0, 0)),
                  pl.BlockSpec((None, FF_SH, D), lambda i, j: (j, 0, 0))],
        out_specs=[row, sh, sh, row, pl.BlockSpec((8, D), lambda i, j: (0, 0))],
        out_shape=[jax.ShapeDtypeStruct((L, D), BF16),
                   jax.ShapeDtypeStruct((N_SHARD, L, FF_SH), BF16), jax.ShapeDtypeStruct((N_SHARD, L, FF_SH), BF16),
                   jax.ShapeDtypeStruct((L, D), F32), jax.ShapeDtypeStruct((8, D), F32)],
        scratch_shapes=[pltpu.VMEM((tm, D), BF16), pltpu.VMEM((tm, D), F32)],
        compiler_params=_params("arbitrary", "arbitrary"),
    )(dx2, x1, f, g, u, nw2, sc2, g2, wg, wu, wd)


def matmul_tn(a, b, name, acc_init=None, to_chips=()):
    na, K, M = a.shape
    nb, _, N = b.shape
    n = max(na, nb)
    tk = min(512, K)
    tn = N if N <= 1024 else N // 2
    nk = K // tk
    grid = (n, N // tn, nk)
    has_init = acc_init is not None
    nx = len(to_chips)

    def body(a_ref, b_ref, *refs):
        init_ref = refs[0] if has_init else None
        refs = refs[1:] if has_init else refs
        o_ref = refs[nx]
        if nx:
            start, finish = _to_chips_phases(refs[:nx], refs[nx + 1:2 * nx + 1], *refs[2 * nx + 1:])
            pos, total = _grid_step(grid)
            pl.when(pos == 0)(start)
        kk = pl.program_id(2)

        @pl.when(kk == 0)
        def _():
            o_ref[...] = init_ref[...] if has_init else jnp.zeros_like(o_ref)

        o_ref[...] += _dot(a_ref[...], b_ref[...], 0, 0)
        if nx:
            pl.when(pos == total - 1)(finish)

    out_spec = pl.BlockSpec((None, M, tn), lambda s, j, kk: (s, 0, j))
    in_specs = [pl.BlockSpec((None, tk, M), lambda s, j, kk: (s if na > 1 else 0, kk, 0)),
                pl.BlockSpec((None, tk, tn), lambda s, j, kk: (s if nb > 1 else 0, kk, j))]
    args = [a, b]
    if has_init:
        in_specs.append(out_spec)
        args.append(acc_init)
    out = pl.pallas_call(
        body, name=name,
        grid=grid,
        in_specs=in_specs + [ANY] * nx,
        out_specs=[out_spec] + [ANY] * nx,
        out_shape=[jax.ShapeDtypeStruct((n, M, N), F32)] + _to_chips_shapes(to_chips),
        scratch_shapes=_to_chips_scratch(nx) if nx else [],
        compiler_params=_params(*(("arbitrary",) * 3 if nx else ("parallel", "parallel", "arbitrary"))),
    )(*args, *to_chips)
    return out if nx else out[0]


def matmul_tn_pair(a, b1, b2, name):
    K, M = a.shape
    n, _, N = b1.shape
    tk = min(512, K)

    def body(a_ref, b1_ref, b2_ref, o1_ref, o2_ref):
        @pl.when(pl.program_id(1) == 0)
        def _():
            o1_ref[...] = jnp.zeros_like(o1_ref)
            o2_ref[...] = jnp.zeros_like(o2_ref)

        at = a_ref[...].T
        o1_ref[...] += _dot(at, b1_ref[...])
        o2_ref[...] += _dot(at, b2_ref[...])

    b_spec = pl.BlockSpec((None, tk, N), lambda s, kk: (s, kk, 0))
    o_spec = pl.BlockSpec((None, M, N), lambda s, kk: (s, 0, 0))
    return pl.pallas_call(
        body, name=name,
        grid=(n, K // tk),
        in_specs=[pl.BlockSpec((tk, M), lambda s, kk: (kk, 0)), b_spec, b_spec],
        out_specs=[o_spec, o_spec],
        out_shape=[jax.ShapeDtypeStruct((n, M, N), F32)] * 2,
        compiler_params=_params("parallel", "arbitrary"),
    )(a, b1, b2)


PIECE_COLS = 1024
N_PIECE_BLOCKS = D_IN // PIECE_COLS


def _piece_blocks(pieces):
    out, col = [], 0
    for arr, width in pieces:
        if arr is not None:
            out.append((arr, col // PIECE_COLS, width // PIECE_COLS))
        col += width
    assert col == D_IN
    return out


def _piece_feed(p_refs, blocks, buf, sems, tile_of, pos, total):
    def present(blk):
        ok = None
        for _, b0, nb in blocks:
            mine = (blk >= b0) & (blk < b0 + nb)
            ok = mine if ok is None else ok | mine
        return ok

    def fetch(step):
        blk, rows = tile_of(step)
        for p_ref, (_, b0, nb) in zip(p_refs, blocks):
            for t in range(nb):
                @pl.when(blk == b0 + t)
                def _(p_ref=p_ref, t=t):
                    pltpu.make_async_copy(p_ref.at[rows, pl.ds(t * PIECE_COLS, PIECE_COLS)], buf.at[step % 2],
                                          sems.at[step % 2]).start()

    @pl.when(pos == 0)
    def _():
        fetch(pos)

    @pl.when(pos + 1 < total)
    def _():
        fetch(pos + 1)

    def landed():
        slot = pos % 2
        pltpu.make_async_copy(p_refs[0].at[pl.ds(0, buf.shape[1]), pl.ds(0, PIECE_COLS)], buf.at[slot],
                              sems.at[slot]).wait()
        return buf.at[slot]

    return present(tile_of(pos)[0]), landed


def matmul_tn_pieces(a, pieces, name, acc_init=None, to_chips=()):
    K, M = a.shape
    blocks = _piece_blocks(pieces)
    tk = min(1024, K)
    nk = K // tk
    grid = (N_PIECE_BLOCKS, nk)
    has_init = acc_init is not None
    nx, npc = len(to_chips), len(blocks)

    def body(a_ref, *refs):
        p_refs = refs[:npc]
        refs = refs[npc:]
        init_ref = refs[0] if has_init else None
        refs = refs[1:] if has_init else refs
        o_ref = refs[nx]
        buf, sems = refs[2 * nx + 1:2 * nx + 3]
        pos, total = _grid_step(grid)
        if nx:
            start, finish = _to_chips_phases(refs[:nx], refs[nx + 1:2 * nx + 1], *refs[2 * nx + 3:])
            pl.when(pos == 0)(start)
        here, landed = _piece_feed(p_refs, blocks, buf, sems,
                                   lambda s: (s // nk, pl.ds(pl.multiple_of((s % nk) * tk, tk), tk)), pos, total)

        @pl.when(pl.program_id(1) == 0)
        def _():
            o_ref[...] = init_ref[...] if has_init else jnp.zeros_like(o_ref)

        @pl.when(here)
        def _():
            o_ref[...] += _dot(a_ref[...], landed()[...], 0, 0)

        if nx:
            pl.when(pos == total - 1)(finish)

    out_spec = pl.BlockSpec((M, PIECE_COLS), lambda blk, kk: (0, blk))
    in_specs = [pl.BlockSpec((tk, M), lambda blk, kk: (kk, 0))] + [ANY] * npc
    args = [a] + [arr for arr, _, _ in blocks]
    if has_init:
        in_specs.append(out_spec)
        args.append(acc_init)
    out = pl.pallas_call(
        body, name=name,
        grid=grid,
        in_specs=in_specs + [ANY] * nx,
        out_specs=[out_spec] + [ANY] * nx,
        out_shape=[jax.ShapeDtypeStruct((M, D_IN), F32)] + _to_chips_shapes(to_chips),
        scratch_shapes=[pltpu.VMEM((2, tk, PIECE_COLS), BF16), pltpu.SemaphoreType.DMA((2,))]
        + (_to_chips_scratch(nx) if nx else []),
        compiler_params=_params("arbitrary", "arbitrary"),
    )(*args, *to_chips)
    return out if nx else out[0]


def dhx_normbwd(pieces, w, x, dx_res, nw, sc, name, to_chips=()):
    L = x.shape[0]
    tm = min(PROJ_ROWS, L)
    blocks = _piece_blocks(pieces)
    grid = (L // tm, N_PIECE_BLOCKS)
    nx, npc = len(to_chips), len(blocks)

    def body(*refs):
        p_refs = refs[:npc]
        w_ref, x_ref, res_ref, nw_ref, sc_ref = refs[npc:npc + 5]
        refs = refs[npc + 5:]
        dx_ref, sums_ref = refs[nx:nx + 2]
        acc, buf, sems = refs[2 * nx + 2:2 * nx + 5]
        pos, total = _grid_step(grid)
        if nx:
            start, finish = _to_chips_phases(refs[:nx], refs[nx + 2:2 * nx + 2], *refs[2 * nx + 5:])
            pl.when(pos == 0)(start)
            pl.when(pos == total - 1)(finish)
        here, landed = _piece_feed(
            p_refs, blocks, buf, sems,
            lambda s: (s % N_PIECE_BLOCKS, pl.ds(pl.multiple_of((s // N_PIECE_BLOCKS) * tm, tm), tm)), pos, total)
        i, blk = pl.program_id(0), pl.program_id(1)

        @pl.when((i == 0) & (blk == 0))
        def _():
            sums_ref[...] = jnp.zeros_like(sums_ref)

        @pl.when(blk == 0)
        def _():
            acc[...] = jnp.zeros_like(acc)

        @pl.when(here)
        def _():
            acc[...] += _dot(landed()[...], w_ref[...], 1, 1)

        @pl.when(blk == N_PIECE_BLOCKS - 1)
        def _():
            dhx = acc[...]
            xv = x_ref[...]
            r = lax.rsqrt(_lanemean(xv * xv) + EPS)
            n0 = xv * r
            nw = nw_ref[...]
            dn = dhx * (1.0 + sc_ref[...])
            dn0 = dn * nw
            dx_ref[...] = res_ref[...] + r * (dn0 - n0 * _lanemean(dn0 * n0))
            sums_ref[0:1, :] += _rowsum(dhx)
            sums_ref[1:2, :] += _rowsum(dhx * n0 * nw)
            sums_ref[2:3, :] += _rowsum(dn * n0)

    row = pl.BlockSpec((tm, D), lambda i, blk: (i, 0))
    vec = pl.BlockSpec((1, D), lambda i, blk: (0, 0))
    return pl.pallas_call(
        body, name=name,
        grid=grid,
        in_specs=[ANY] * npc + [pl.BlockSpec((D, PIECE_COLS), lambda i, blk: (0, blk)), row, row, vec, vec] + [ANY] * nx,
        out_specs=[row, pl.BlockSpec((8, D), lambda i, blk: (0, 0))] + [ANY] * nx,
        out_shape=[jax.ShapeDtypeStruct((L, D), F32), jax.ShapeDtypeStruct((8, D), F32)] + _to_chips_shapes(to_chips),
        scratch_shapes=[pltpu.VMEM((tm, D), F32), pltpu.VMEM((2, tm, PIECE_COLS), BF16), pltpu.SemaphoreType.DMA((2,))]
        + (_to_chips_scratch(nx) if nx else []),
        compiler_params=_params("arbitrary", "arbitrary"),
    )(*[arr for arr, _, _ in blocks], w, x, dx_res, nw, sc, *to_chips)


SMALL_ROWS = 24


def _rope_tables(L):
    rows = L // 64
    freqs = 10000.0 ** (-jnp.arange(RT_DK // 4, dtype=F32) / (RT_DK // 4))
    a_row = jnp.arange(rows, dtype=F32)[:, None] * freqs
    a_col = jnp.arange(64, dtype=F32)[:, None] * freqs

    def spread(f):
        return jnp.concatenate([jnp.repeat(f(a_row), 64, axis=0), jnp.tile(f(a_col), (rows, 1))], axis=-1)

    cos, sin = spread(jnp.cos), spread(jnp.sin)
    return jnp.concatenate([cos, cos], axis=1), jnp.concatenate([-sin, sin], axis=1)


def _pieces(hq, hf_f, hf_b, hi, hg, rq, rk, rv, rg, ga, gb):
    widths = (D, D, D, D, D, D, D, 2 * D, 2 * D, D, D)
    return list(zip((hq, hf_f, hf_b, hi, hg, rq, rk, rv, rg, ga, gb), widths))


def _lane0(a):
    return a[:, 0, 0]


def _pack_small(rows):
    out = [r.reshape(1, D) for r in rows]
    out += [jnp.zeros((1, D), F32)] * (SMALL_ROWS - len(out))
    return jnp.concatenate(out, axis=0)


def _sibling_sums(gs, names, place):
    core, core_arg, _ = place

    def other_half(g):
        axis = g.ndim - 2
        h = g.shape[axis] // 2
        return lax.dynamic_slice_in_dim(g, (1 - core) * h, h, axis=axis).astype(BF16)

    payload = [other_half(g) for g in gs]
    received = rs_to_sibling(payload, "rs_to_sibling_" + names[0])
    return [rs_add_sibling(g, r, core_arg, "rs_add_sibling_" + k) for g, r, k in zip(gs, received, names)]


def _staged_in_proj(x, nw, sh, sc, w_shard, rest, chip):
    cx, cy = chip // 2, chip % 2

    def arg(k):
        return jnp.reshape(k, (1,)).astype(jnp.int32)

    p, hx, w_full = in_proj_own(x, nw, sh, sc, w_shard, arg(chip), "in_proj_own")
    p, w_full = in_proj_next(hx, w_full, arg(2 * (1 - cx) + cy), p, "in_proj_x", diag_from=w_shard)
    w_pa, w_pb, w_out, w_wd = rest[0], rest[1], rest[2], rest[5]
    p, g_pa, g_pb, g_out, g_wd = in_proj_next(hx, w_full, arg(2 * cx + 1 - cy), p, "in_proj_y",
                                              gather=[w_pa, w_pb, w_out, w_wd])
    p, g_wg, g_wu = in_proj_next(hx, w_full, arg(3 - chip), p, "in_proj_diag", gather=[rest[3], rest[4]])
    w = {"w_in": w_full, "w_pa": g_pa.reshape(D, D), "w_pb": g_pb.reshape(2 * D, D), "w_out": g_out.reshape(D, D),
         "wg": g_wg, "wu": g_wu, "wd": g_wd}
    return p, hx, w


def local_step(x, ctx, target, mod_x, mod_c, lb_f, lb_b, lg_f, lg_b, nw1, nw2, hgw, fw, w, rest=None, place=None):
    L, Lc = x.shape[0], ctx.shape[0]
    sh1, sc1, g1, sh2, sc2, g2 = (mod_x[i:i + 1] for i in range(6))
    sh1c, sc1c = mod_c[0:1], mod_c[1:2]
    cosf, sinf = _rope_tables(L)
    cosc, sinc = jnp.ones((Lc, RT_DK), F32), jnp.zeros((Lc, RT_DK), F32)
    zero_h = jnp.zeros((HEADS, HG_D, HG_D), F32)
    zero_r = jnp.zeros((HEADS, RT_DV, RT_DK), F32)

    if rest is None:
        p, hx = normmod_matmul(x, nw1, sh1, sc1, w["w_in"], "in_proj")
    else:
        p, hx, w = _staged_in_proj(x, nw1, sh1, sc1, w["w_in_shard"], rest, place[2][0])
    pc, hxc = normmod_matmul(ctx, nw1, sh1c, sc1c, w["w_in"], "ctx_in_proj")
    _, s_hf, cb_hf = hgrn_scan_fwd(pc, lb_f, zero_h, COL_HFF, False, "ctx_hgrn_f")
    _, s_hb, cb_hb = hgrn_scan_fwd(pc, lb_b, zero_h, COL_HFB, True, "ctx_hgrn_b")
    _, s_rf, cb_rf = ret_scan_fwd(pc, cosc, sinc, lg_f, zero_r, False, "ctx_ret_f")
    _, s_rb, cb_rb = ret_scan_fwd(pc, cosc, sinc, lg_b, zero_r, True, "ctx_ret_b")
    ohf, _, xb_hf = hgrn_scan_fwd(p, lb_f, s_hf, COL_HFF, False, "hgrn_f")
    ohb, _, xb_hb = hgrn_scan_fwd(p, lb_b, s_hb, COL_HFB, True, "hgrn_b")
    orf, _, xb_rf = ret_scan_fwd(p, cosf, sinf, lg_f, s_rf, False, "ret_f")
    orb, _, xb_rb = ret_scan_fwd(p, cosf, sinf, lg_b, s_rb, True, "ret_b")
    x1, x_mix, merged, ya, yb = mix_fwd(ohf, ohb, orf, orb, p, x, g1, hgw, w["w_pa"], w["w_pb"], w["w_out"], "mix_fwd")
    hx2, gg, uu, hh, ff, dx2, sums_f = ffn_fwd(x1, target, nw2, sh2, sc2, g2, fw, w["wg"], w["wu"], w["wd"], "ffn_fwd")

    d_f, d_g, d_u, dx1, sums_fb = ffn_bwd(dx2, x1, ff, gg, uu, nw2, sc2, g2, w["wg"], w["wu"], w["wd"], "ffn_bwd")
    dw_gate, dw_up = matmul_tn_pair(hx2, d_g, d_u, "dw_ffn_gate_up")
    grads = {"wg": dw_gate, "wu": dw_up, "wd": matmul_tn(hh, d_f[None], "dw_ffn_down")}
    dxm, d_a, d_b, dga, dgb, dhg, drg, dohg, dort, sums_m = mix_bwd(
        dx1, x_mix, ya, yb, ohf, ohb, orf, orb, p, g1, hgw, w["w_pa"], w["w_pb"], w["w_out"], "mix_bwd")
    grads["w_out"] = matmul_tn(merged[None], dxm[None], "dw_out").reshape(N_SHARD, D // N_SHARD, D)
    grads["w_pa"] = matmul_tn(ya[None], d_a[None], "dw_proj_hgrn").reshape(N_SHARD, D // N_SHARD, D)
    grads["w_pb"] = matmul_tn(yb[None], d_b[None], "dw_proj_ret").reshape(N_SHARD, 2 * D // N_SHARD, D)

    rq1, rk1, rv1, dlgf_x, ds_rf = ret_scan_bwd(p, cosf, sinf, lg_f, xb_rf, dort, zero_r, None, False, "ret_f_bwd")
    drq, drk, drv, dlgb_x, ds_rb = ret_scan_bwd(p, cosf, sinf, lg_b, xb_rb, dort, zero_r, (rq1, rk1, rv1), True, "ret_b_bwd")
    hq1, dzf, hv1, dlbf_x, ds_hf = hgrn_scan_bwd(p, lb_f, xb_hf, dohg, zero_h, None, COL_HFF, False, "hgrn_f_bwd")
    dhq, dzb, dhv, dlbb_x, ds_hb = hgrn_scan_bwd(p, lb_b, xb_hb, dohg, zero_h, (hq1, hv1), COL_HFB, True, "hgrn_b_bwd")
    dp = _pieces(dhq, dzf, dzb, dhv, dhg, drq, drk, drv, drg, dga, dgb)
    others = ["w_pa", "w_pb", "w_out", "wg", "wu", "wd"]
    if place is None:
        dw_in = matmul_tn_pieces(hx, dp, "dw_in")
    else:
        sums_o = _sibling_sums([grads[k] for k in others], others, place)
        dw_in, *recv_o = matmul_tn_pieces(hx, dp, "dw_in", to_chips=[a16 for _, a16 in sums_o])

    zc = jnp.zeros((Lc, D), F32)
    zc2 = jnp.zeros((Lc, 2 * D), F32)
    crq1, crk1, crv1, dlgf_c, _ = ret_scan_bwd(pc, cosc, sinc, lg_f, cb_rf, zc2, ds_rf, None, False, "ctx_ret_f_bwd")
    cdrq, cdrk, cdrv, dlgb_c, _ = ret_scan_bwd(pc, cosc, sinc, lg_b, cb_rb, zc2, ds_rb, (crq1, crk1, crv1), True, "ctx_ret_b_bwd")
    chq1, cdzf, chv1, dlbf_c, _ = hgrn_scan_bwd(pc, lb_f, cb_hf, zc, ds_hf, None, COL_HFF, False, "ctx_hgrn_f_bwd")
    cdhq, cdzb, cdhv, dlbb_c, _ = hgrn_scan_bwd(pc, lb_b, cb_hb, zc, ds_hb, (chq1, chv1), COL_HFB, True, "ctx_hgrn_b_bwd")
    dpc = _pieces(cdhq, cdzf, cdzb, cdhv, None, cdrq, cdrk, cdrv, None, None, None)
    _, sums_c = dhx_normbwd(dpc, w["w_in"], ctx, zc, nw1, sc1c, "dctx_in_proj")
    grads["w_in"] = matmul_tn_pieces(hxc, dpc, "dw_in_ctx", acc_init=dw_in)
    if place is None:
        dx, sums_x = dhx_normbwd(dp, w["w_in"], x, dx1, nw1, sc1, "dx_in_proj")
    else:
        sums_i = _sibling_sums([grads["w_in"]], ["w_in"], place)
        dx, sums_x, recv_i = dhx_normbwd(dp, w["w_in"], x, dx1, nw1, sc1, "dx_in_proj", to_chips=[sums_i[0][1]])
        names = ["w_in"] + others
        halves = [rs_add_chips(a, r, place[2], "rs_add_chips_" + k)
                  for (a, _), r, k in zip(sums_i + sums_o, [recv_i] + recv_o, names)]
        grads = dict(zip(names, rs_join_halves(halves, "rs_join_halves")))

    def lg_row(f, b):
        return jnp.concatenate([_lane0(f), _lane0(b), jnp.zeros((D - 2 * HEADS,), F32)])

    small = _pack_small([
        sums_x[0], sums_x[1], sums_m[0], sums_fb[1], sums_fb[2], sums_fb[0],
        sums_c[0], sums_c[1],
        sums_x[2], sums_c[2], sums_fb[3], sums_m[1], sums_f[0],
        dlbf_x, dlbf_c, dlbb_x, dlbb_c,
        lg_row(dlgf_x, dlgb_x), lg_row(dlgf_c, dlgb_c),
        sums_f[1],
    ])
    return dx, grads, small


MESH = pl.DeviceIdType.MESH
ANY = pl.BlockSpec(memory_space=pl.ANY)
N_DEV = 8


def _place():
    return lax.axis_index("x"), lax.axis_index("y"), lax.axis_index("c")


def _other_chips(x, y):
    return [(1 - x, y), (x, 1 - y), (1 - x, 1 - y)]


def allgather8(xs, name):
    m, n = xs.shape

    def body(x_ref, out_ref, send_sems, recv_sems, local_sem):
        x, y, c = _place()
        me, sibling = (x, y, c), (x, y, 1 - c)
        chips = _other_chips(x, y)

        def rows(px, py, pc):
            return out_ref.at[pl.ds((4 * px + 2 * py + pc) * m, m), :]

        def copy(k, block, to, src=None):
            return pltpu.make_async_remote_copy(
                src_ref=rows(*block) if src is None else src, dst_ref=rows(*block),
                send_sem=send_sems.at[k], recv_sem=recv_sems.at[k], device_id=to, device_id_type=MESH)

        mine = pltpu.make_async_copy(x_ref, rows(*me), local_sem)
        mine.start()
        first = [copy(0, me, sibling, src=x_ref)]
        first += [copy(1 + j, me, (*chip, c), src=x_ref) for j, chip in enumerate(chips)]
        for cp in first:
            cp.start()
        passed = [copy(4 + j, (*chip, c), sibling) for j, chip in enumerate(chips)]
        for j, chip in enumerate(chips):
            copy(1 + j, (*chip, c), me).wait_recv()
            passed[j].start()
        copy(0, sibling, me).wait_recv()
        for j, chip in enumerate(chips):
            copy(4 + j, (*chip, 1 - c), me).wait_recv()
        for cp in first + passed:
            cp.wait_send()
        mine.wait()

    return pl.pallas_call(
        body, name=name,
        out_shape=jax.ShapeDtypeStruct((N_DEV * m, n), xs.dtype),
        in_specs=[pl.BlockSpec(memory_space=pltpu.VMEM)],
        out_specs=pl.BlockSpec(memory_space=pltpu.VMEM),
        scratch_shapes=[pltpu.SemaphoreType.DMA((7,)), pltpu.SemaphoreType.DMA((7,)), pltpu.SemaphoreType.DMA],
    )(xs)


def _gather_phases(ins, outs, send_sems, recv_sems, local_sems, relations=(0, 1, 2), own=True):
    n = len(ins)
    x, y, c = _place()
    chips = _other_chips(x, y)

    def rows(i, core):
        h = ins[i].shape[0] // 2
        return pl.ds(pl.multiple_of(core * h, 16), h)

    def region(i, k, rs):
        if len(outs[i].shape) == 2:
            cols = ins[i].shape[1]
            return outs[i].at[rs, pl.ds(pl.multiple_of(k * cols, 128), cols)]
        return outs[i].at[k, rs, :]

    def landed(i, chip, core):
        return region(i, 2 * chip[0] + chip[1], rows(i, core))

    def copy(i, k, src, dst, to):
        return pltpu.make_async_remote_copy(src_ref=src, dst_ref=dst, send_sem=send_sems.at[6 * i + k],
                                            recv_sem=recv_sems.at[6 * i + k], device_id=to, device_id_type=MESH)

    def local(i):
        r = ins[i].shape[0] // LOCAL_PIECES
        return [pltpu.make_async_copy(ins[i].at[pl.ds(q * r, r), :], region(i, 2 * x + y, pl.ds(q * r, r)),
                                      local_sems.at[LOCAL_PIECES * i + q]) for q in range(LOCAL_PIECES)]

    def send(i, j):
        return copy(i, j, ins[i].at[rows(i, c), :], landed(i, (x, y), c), (*chips[j], c))

    def arrived(i, j, core, k):
        return copy(i, k, ins[i].at[rows(i, core), :], landed(i, chips[j], core), (x, y, 1 - c))

    def passed(i, j):
        return copy(i, 3 + j, landed(i, chips[j], c), landed(i, chips[j], c), (x, y, 1 - c))

    def start():
        for i in range(n):
            for lc in local(i) if own else ():
                lc.start()
            for j in relations:
                send(i, j).start()

    def forward():
        for i in range(n):
            for j in relations:
                arrived(i, j, c, j).wait_recv()
                passed(i, j).start()

    def finish():
        for i in range(n):
            for j in relations:
                arrived(i, j, 1 - c, 3 + j).wait_recv()
        for i in range(n):
            for j in relations:
                send(i, j).wait_send()
                passed(i, j).wait_send()
            for lc in local(i) if own else ():
                lc.wait()

    return start, forward, finish


LOCAL_PIECES = 4


def _gather_scratch(n):
    return [pltpu.SemaphoreType.DMA((6 * n,)), pltpu.SemaphoreType.DMA((6 * n,)),
            pltpu.SemaphoreType.DMA((LOCAL_PIECES * n,))]


def rs_to_sibling(payloads, name):
    n = len(payloads)

    def body(*refs):
        ins, outs = refs[:n], refs[n:2 * n]
        send_sems, recv_sems = refs[2 * n:]
        x, y, c = _place()
        copies = []
        for i in range(n):
            cp = pltpu.make_async_remote_copy(src_ref=ins[i], dst_ref=outs[i], send_sem=send_sems.at[i],
                                              recv_sem=recv_sems.at[i], device_id=(x, y, 1 - c), device_id_type=MESH)
            cp.start()
            copies.append(cp)
        for cp in copies:
            cp.wait()

    return pl.pallas_call(
        body, name=name,
        out_shape=[jax.ShapeDtypeStruct(g.shape, g.dtype) for g in payloads],
        in_specs=[ANY] * n, out_specs=[ANY] * n,
        scratch_shapes=[pltpu.SemaphoreType.DMA((n,)), pltpu.SemaphoreType.DMA((n,))],
    )(*payloads)


def _to_chips_phases(ins, outs, send_sems, recv_sems):
    def copies():
        x, y, c = _place()
        return [pltpu.make_async_remote_copy(
            src_ref=ins[i].at[2 * px + py], dst_ref=outs[i].at[j], send_sem=send_sems.at[3 * i + j],
            recv_sem=recv_sems.at[3 * i + j], device_id=(px, py, c), device_id_type=MESH)
            for i in range(len(ins)) for j, (px, py) in enumerate(_other_chips(x, y))]

    def start():
        for cp in copies():
            cp.start()

    def finish():
        for cp in copies():
            cp.wait()

    return start, finish


def _to_chips_shapes(parts):
    return [jax.ShapeDtypeStruct((3,) + a.shape[1:], a.dtype) for a in parts]


def _to_chips_scratch(n):
    return [pltpu.SemaphoreType.DMA((3 * n,)), pltpu.SemaphoreType.DMA((3 * n,))]


def rs_join_halves(fulls, name):
    n = len(fulls)

    def body(*refs):
        outs = refs[n:2 * n]
        send_sems, recv_sems = refs[2 * n:]
        x, y, c = _place()

        def copy(i, core):
            h = fulls[i].shape[0] // 2
            rows = outs[i].at[pl.ds(pl.multiple_of(core * h, 8), h), :]
            return pltpu.make_async_remote_copy(src_ref=rows, dst_ref=rows, send_sem=send_sems.at[i],
                                                recv_sem=recv_sems.at[i], device_id=(x, y, 1 - c), device_id_type=MESH)

        sent = [copy(i, c) for i in range(n)]
        for cp in sent:
            cp.start()
        for i in range(n):
            copy(i, 1 - c).wait_recv()
        for cp in sent:
            cp.wait_send()

    return pl.pallas_call(
        body, name=name,
        out_shape=[jax.ShapeDtypeStruct(a.shape, a.dtype) for a in fulls],
        in_specs=[ANY] * n, out_specs=[ANY] * n,
        input_output_aliases={i: i for i in range(n)},
        scratch_shapes=[pltpu.SemaphoreType.DMA((n,)), pltpu.SemaphoreType.DMA((n,))],
    )(*fulls)


def _row_tile(rows, cols, limit_bytes=2 * 1024 * 1024, mult=8):
    best = mult
    for t in range(mult, rows + 1, mult):
        if rows % t == 0 and t * cols * 4 <= limit_bytes:
            best = t
    return best


def rs_add_sibling(g, recv, c, name):
    if g.ndim == 2:
        h, C = recv.shape[0], recv.shape[1] // N_SHARD
    else:
        _, h, C = recv.shape
    tr = _row_tile(h, C, mult=16)
    nt = h // tr

    def body(c_ref, g_ref, r_ref, o_ref, o16_ref):
        s = g_ref[...] + r_ref[...].astype(F32)
        o_ref[...] = s
        o16_ref[...] = s.astype(BF16)

    blk = pl.BlockSpec((None, tr, C), lambda k, i, c_ref: (k, i, 0))
    if g.ndim == 2:
        g_spec = pl.BlockSpec((tr, C), lambda k, i, c_ref: (c_ref[0] * nt + i, k))
        r_spec = pl.BlockSpec((tr, C), lambda k, i, c_ref: (i, k))
    else:
        g_spec = pl.BlockSpec((None, tr, C), lambda k, i, c_ref: (k, c_ref[0] * nt + i, 0))
        r_spec = blk
    return pl.pallas_call(
        body, name=name,
        grid_spec=pltpu.PrefetchScalarGridSpec(
            num_scalar_prefetch=1, grid=(N_SHARD, nt),
            in_specs=[g_spec, r_spec],
            out_specs=[blk, blk]),
        out_shape=[jax.ShapeDtypeStruct((N_SHARD, h, C), F32), jax.ShapeDtypeStruct((N_SHARD, h, C), BF16)],
        compiler_params=_params("parallel", "parallel"),
    )(c, g, recv)


def rs_add_chips(part, recv, place, name):
    _, h, C = part.shape
    tr = _row_tile(h, C, mult=16)
    nt = h // tr

    def body(k_ref, p_ref, r_ref, o_ref):
        o_ref[...] = ((p_ref[...] + r_ref[0].astype(F32)) + r_ref[1].astype(F32)) + r_ref[2].astype(F32)

    return pl.pallas_call(
        body, name=name,
        grid_spec=pltpu.PrefetchScalarGridSpec(
            num_scalar_prefetch=1, grid=(nt,),
            in_specs=[pl.BlockSpec((None, tr, C), lambda i, k_ref: (k_ref[0], i, 0)),
                      pl.BlockSpec((3, tr, C), lambda i, k_ref: (0, i, 0))],
            out_specs=pl.BlockSpec((tr, C), lambda i, k_ref: (k_ref[1] * nt + i, 0))),
        out_shape=jax.ShapeDtypeStruct((2 * h, C), F32),
        compiler_params=_params("parallel"),
    )(place, part, recv)


def _adamw_math(w, g, m, v):
    m = ADAM_B1 * m + (1.0 - ADAM_B1) * g
    v = ADAM_B2 * v + (1.0 - ADAM_B2) * (g * g)
    m_hat = m / (1.0 - ADAM_B1 ** ADAM_STEP)
    v_hat = v / (1.0 - ADAM_B2 ** ADAM_STEP)
    delta = -ADAM_LR * (m_hat / (jnp.sqrt(v_hat) + ADAM_EPS) + ADAM_WD * w)
    return delta, m, v


def adamw(w, g, m, v, name):
    R, C = w.shape
    tr = _row_tile(R, C, 1024 * 1024)

    def body(w_ref, g_ref, m_ref, v_ref, d_ref, nm_ref, nv_ref):
        d_ref[...], nm_ref[...], nv_ref[...] = _adamw_math(w_ref[...], g_ref[...], m_ref[...], v_ref[...])

    blk = pl.BlockSpec((tr, C), lambda i: (i, 0))
    return pl.pallas_call(
        body, name=name, grid=(R // tr,), in_specs=[blk] * 4, out_specs=[blk] * 3,
        out_shape=[jax.ShapeDtypeStruct((R, C), F32)] * 3,
        compiler_params=_params("parallel"),
    )(w, g, m, v)


MOD_SH = 6 * D // N_SHARD
PK_ROWS = 16


def mod_fwd(call16, w_sh, b_sh, name):
    def body(c_ref, w_ref, b_ref, o_ref):
        o_ref[...] = _dot(_silu_parts(c_ref[...])[0], w_ref[...], prec=HI) + b_ref[...]

    return pl.pallas_call(body, name=name, out_shape=jax.ShapeDtypeStruct((16, MOD_SH), F32),
                          compiler_params=_params())(call16, w_sh, b_sh)


def prep_small(lbf2, lbb2, theta_row, name):
    def body(f_ref, b_ref, t_ref, lbf_ref, lbb_ref, lg_ref):
        lbf_ref[...] = _sigmoid(f_ref[0:1, :] - f_ref[1:2, :])
        lbb_ref[...] = _sigmoid(b_ref[0:1, :] - b_ref[1:2, :])
        t = t_ref[...]
        lg_ref[...] = jnp.minimum(t, 0.0) - jnp.log(1.0 + jnp.exp(-jnp.abs(t)))

    row = jax.ShapeDtypeStruct((1, D), F32)
    return pl.pallas_call(body, name=name, out_shape=[row, row, row], compiler_params=_params())(lbf2, lbb2, theta_row)


def small_grads(g3, lbf, lbb, theta_row, name):
    def body(g_ref, lbf_ref, lbb_ref, t_ref, pk_ref, aux_ref):
        s = g_ref[0]
        for d in range(1, N_DEV):
            s = s + g_ref[d]
        pk_ref[...] = jnp.zeros_like(pk_ref)
        aux_ref[...] = jnp.zeros_like(aux_ref)
        pk_ref[1:7, :] = s[0:6]
        pk_ref[1:3, :] += s[6:8]
        pk_ref[7:8, :] = s[8:9] + s[9:10]
        pk_ref[8:9, :] = s[10:11]
        lbf, lbb = lbf_ref[...], lbb_ref[...]
        daf = (s[13:14] + s[14:15]) * lbf * (1.0 - lbf)
        dab = (s[15:16] + s[16:17]) * lbb * (1.0 - lbb)
        pk_ref[9:10, :] = daf
        pk_ref[10:11, :] = -daf
        pk_ref[11:12, :] = dab
        pk_ref[12:13, :] = -dab
        pk_ref[13:14, :] = s[11:12]
        pk_ref[14:15, :] = (s[17:18] + s[18:19]) * _sigmoid(-t_ref[...])
        pk_ref[15:16, :] = s[12:13]
        aux_ref[0:2, :] = s[6:8]
        aux_ref[2:3, :] = jnp.broadcast_to(jnp.sum(s[19:20], axis=-1, keepdims=True), (1, D))

    return pl.pallas_call(body, name=name,
                          out_shape=[jax.ShapeDtypeStruct((PK_ROWS, D), F32), jax.ShapeDtypeStruct((8, D), F32)],
                          compiler_params=_params())(g3, lbf, lbb, theta_row)


def mod_bwd(call16, dmod_sh, w_sh, name):
    def body(c_ref, d_ref, w_ref, dw_ref, ds_ref):
        dm = d_ref[...]
        dw_ref[...] = _dot(_silu_parts(c_ref[...])[0], dm, 0, 0, prec=HI)
        ds_ref[...] = jnp.zeros_like(ds_ref)
        ds_ref[0:1, :] = _dot(dm[8:9, :], w_ref[...], 1, 1, prec=HI)

    return pl.pallas_call(body, name=name,
                          out_shape=[jax.ShapeDtypeStruct((D, MOD_SH), F32), jax.ShapeDtypeStruct((8, D), F32)],
                          compiler_params=_params())(call16, dmod_sh, w_sh)


def adamw_small(g4, pk_g, pk_w, pk_m, pk_v, name):
    def body(g4_ref, g_ref, w_ref, m_ref, v_ref, go_ref, d_ref, nm_ref, nv_ref):
        w = w_ref[...]
        ds = ((g4_ref[0:1, :] + g4_ref[16:17, :]) + g4_ref[32:33, :]) + g4_ref[48:49, :]
        row = lax.broadcasted_iota(jnp.int32, (PK_ROWS, D), 0)
        g = jnp.where(row == 0, ds * _silu_parts(w[0:1, :])[1], g_ref[...])
        go_ref[...] = g
        d_ref[...], nm_ref[...], nv_ref[...] = _adamw_math(w, g, m_ref[...], v_ref[...])

    pk = jax.ShapeDtypeStruct((PK_ROWS, D), F32)
    return pl.pallas_call(body, name=name, out_shape=[pk, pk, pk, pk], compiler_params=_params())(g4, pk_g, pk_w, pk_m, pk_v)


def _pack_params(c_ctx, b_mod, n1, n2, lbf, lbb, hgn, th_f, th_b, fin):
    theta = jnp.concatenate([th_f.reshape(HEADS), th_b.reshape(HEADS), jnp.zeros((D - 2 * HEADS,), F32)])
    return jnp.concatenate([c_ctx.reshape(1, D), b_mod.reshape(6, D), n1.reshape(1, D), n2.reshape(1, D), lbf, lbb,
                            hgn.reshape(1, D), theta.reshape(1, D), fin.reshape(1, D)], axis=0)


def _unpack_params(pk):
    return (pk[0], pk[1:7].reshape(1, 6 * D), pk[7:8], pk[8:9], pk[9:11], pk[11:13], pk[13:14],
            pk[14, 0:HEADS].reshape(1, HEADS), pk[14, HEADS:2 * HEADS].reshape(1, HEADS), pk[15])


def kernel(x, c, ctx, c_ctx, w_mod, b_mod, norm1_w, norm2_w, w_in, hg_lb_fwd, hg_lb_bwd, hg_norm_w, rt_theta_fwd, rt_theta_bwd, w_proj_hgrn, w_proj_ret, w_out, w_ffn_gate, w_ffn_up, w_ffn_down, final_norm_w, loss_target, m_c_ctx, m_w_mod, m_b_mod, m_norm1_w, m_norm2_w, m_w_in, m_hg_lb_fwd, m_hg_lb_bwd, m_hg_norm_w, m_rt_theta_fwd, m_rt_theta_bwd, m_w_proj_hgrn, m_w_proj_ret, m_w_out, m_w_ffn_gate, m_w_ffn_up, m_w_ffn_down, m_final_norm_w, v_c_ctx, v_w_mod, v_b_mod, v_norm1_w, v_norm2_w, v_w_in, v_hg_lb_fwd, v_hg_lb_bwd, v_hg_norm_w, v_rt_theta_fwd, v_rt_theta_bwd, v_w_proj_hgrn, v_w_proj_ret, v_w_out, v_w_ffn_gate, v_w_ffn_up, v_w_ffn_down, v_final_norm_w):
    xi, yi, ci = _place()
    dev = 4 * xi + 2 * yi + ci
    chip = 2 * xi + yi
    core_arg = jnp.reshape(ci, (1,)).astype(jnp.int32)
    place_arg = jnp.stack([chip, ci]).astype(jnp.int32)

    c_all = allgather8(jnp.concatenate([c, jnp.zeros((7, D), F32)], axis=0), "gather_c").reshape(N_DEV, 8, D)[:, 0]
    call16 = jnp.concatenate([c_all, c_ctx.reshape(1, D), jnp.zeros((7, D), F32)], axis=0)
    b_sh = lax.dynamic_slice_in_dim(b_mod, chip * MOD_SH, MOD_SH, axis=1)
    mod_sh = mod_fwd(call16, w_mod[0], b_sh, "mod_fwd")
    mod_g = allgather8(mod_sh, "gather_mod").reshape(N_DEV, 16, MOD_SH)
    mod_all = jnp.concatenate([mod_g[0], mod_g[2], mod_g[4], mod_g[6]], axis=1)
    mod_x = lax.dynamic_index_in_dim(mod_all, dev, axis=0, keepdims=False).reshape(6, D)
    mod_c = mod_all[8].reshape(6, D)

    pk_w = _pack_params(c_ctx, b_mod, norm1_w, norm2_w, hg_lb_fwd, hg_lb_bwd, hg_norm_w, rt_theta_fwd, rt_theta_bwd, final_norm_w)
    theta_row = pk_w[14:15]
    lb_f, lb_b, lg_row = prep_small(hg_lb_fwd, hg_lb_bwd, theta_row, "prep_small")
    lg_f = jnp.broadcast_to(lg_row[0, 0:HEADS].reshape(HEADS, 1, 1), (HEADS, 1, RT_DV))
    lg_b = jnp.broadcast_to(lg_row[0, HEADS:2 * HEADS].reshape(HEADS, 1, 1), (HEADS, 1, RT_DV))

    rest = [s[0].astype(BF16) for s in (w_proj_hgrn, w_proj_ret, w_out, w_ffn_gate, w_ffn_up, w_ffn_down)]

    dx, full, small = local_step(x[0], ctx[0], loss_target[0], mod_x, mod_c, lb_f, lb_b, lg_f, lg_b,
                                 norm1_w, norm2_w, hg_norm_w, final_norm_w.reshape(1, D),
                                 {"w_in_shard": w_in[0].astype(BF16)}, rest, (ci, core_arg, place_arg))

    g3 = allgather8(small, "gather_small").reshape(N_DEV, SMALL_ROWS, D)
    pk_g, aux = small_grads(g3, lb_f, lb_b, theta_row, "small_grads")
    loss = aux[2, 0]
    dmod16 = jnp.concatenate([
        g3[:, 0:6, :].reshape(N_DEV, 6 * D),
        jnp.concatenate([aux[0], aux[1], jnp.zeros((4 * D,), F32)]).reshape(1, 6 * D),
        jnp.zeros((7, 6 * D), F32)], axis=0)
    dmod_sh = lax.dynamic_slice_in_dim(dmod16, chip * MOD_SH, MOD_SH, axis=1)
    g_wmod, dsilu = mod_bwd(call16, dmod_sh, w_mod[0], "mod_bwd")
    g4 = allgather8(dsilu, "gather_dsilu")
    pk_m = _pack_params(m_c_ctx, m_b_mod, m_norm1_w, m_norm2_w, m_hg_lb_fwd, m_hg_lb_bwd, m_hg_norm_w, m_rt_theta_fwd, m_rt_theta_bwd, m_final_norm_w)
    pk_v = _pack_params(v_c_ctx, v_b_mod, v_norm1_w, v_norm2_w, v_hg_lb_fwd, v_hg_lb_bwd, v_hg_norm_w, v_rt_theta_fwd, v_rt_theta_bwd, v_final_norm_w)
    pk_g, pk_d, pk_nm, pk_nv = adamw_small(g4, pk_g, pk_w, pk_m, pk_v, "adamw_small")

    big = {
        "w_mod": (g_wmod, w_mod, m_w_mod, v_w_mod),
        "w_in": (full["w_in"], w_in, m_w_in, v_w_in),
        "w_pa": (full["w_pa"], w_proj_hgrn, m_w_proj_hgrn, v_w_proj_hgrn),
        "w_pb": (full["w_pb"], w_proj_ret, m_w_proj_ret, v_w_proj_ret),
        "w_out": (full["w_out"], w_out, m_w_out, v_w_out),
        "wg": (full["wg"], w_ffn_gate, m_w_ffn_gate, v_w_ffn_gate),
        "wu": (full["wu"], w_ffn_up, m_w_ffn_up, v_w_ffn_up),
        "wd": (full["wd"], w_ffn_down, m_w_ffn_down, v_w_ffn_down),
    }
    res = {}
    for k, (g, wt, mt, vt) in big.items():
        d, nm, nv = adamw(wt[0], g, mt[0], vt[0], "adamw_" + k)
        res[k] = (g[None], d[None], nm[None], nv[None])

    sm = [_unpack_params(p) for p in (pk_g, pk_d, pk_nm, pk_nv)]
    outs = []
    for t in range(4):
        (s_cctx, s_bmod, s_n1, s_n2, s_lbf, s_lbb, s_hgn, s_thf, s_thb, s_fin) = sm[t]
        outs.append([s_cctx, res["w_mod"][t], s_bmod, s_n1, s_n2, res["w_in"][t], s_lbf, s_lbb, s_hgn, s_thf, s_thb,
                     res["w_pa"][t], res["w_pb"][t], res["w_out"][t], res["wg"][t], res["wu"][t], res["wd"][t], s_fin])
    return (loss, dx[None], *outs[0], *outs[1], *outs[2], *outs[3])
```

```python
import functools

import jax
import jax.numpy as jnp
from jax import lax
from jax.experimental import pallas as pl
from jax.experimental.pallas import tpu as pltpu

F32 = jnp.float32
BF16 = jnp.bfloat16
HI = lax.Precision.HIGHEST

D = 1024
HEADS = 8
HG_D = 128
RT_DK = 128
RT_DV = 256
D_FF = 2816
D_IN = 13312
N_SHARD = 4
IN_SH = D_IN // N_SHARD
FF_SH = D_FF // N_SHARD
HG_CHUNK = 32
SCAN_ROWS = 256
HG_GROUP = 8
RT_GROUP = 4
PROJ_ROWS = 1024
EPS = 1e-6
GN_EPS = 1e-5
Q_SCALE = 128.0 ** -0.5
VMEM_LIMIT = 56 * 1024 * 1024

COL_HQ, COL_HFF, COL_HFB, COL_HI, COL_HG = 0, 8, 16, 24, 32
COL_RQ, COL_RK, COL_RV, COL_RG, COL_GA, COL_GB = 40, 48, 56, 72, 88, 96

ADAM_LR, ADAM_B1, ADAM_B2, ADAM_EPS, ADAM_WD, ADAM_STEP = 0.001, 0.9, 0.999, 1e-08, 0.01, 10


def _params(*sem):
    return pltpu.CompilerParams(dimension_semantics=sem, vmem_limit_bytes=VMEM_LIMIT)


def _dot(a, b, ca=1, cb=0, prec=None):
    return lax.dot_general(a, b, (((ca,), (cb,)), ((), ())), precision=prec, preferred_element_type=F32)


def _bdot(a, b, ca=1, cb=0):
    return _dot(a.astype(BF16), b.astype(BF16), ca, cb)


def _sigmoid(z):
    return 1.0 / (1.0 + jnp.exp(-z))


def _rowsum(a):
    return jnp.sum(a, axis=0, keepdims=True)


def _lanemean(a):
    return jnp.mean(a, axis=-1, keepdims=True)


def _grid_step(grid):
    pos, total = 0, 1
    for d, size in enumerate(grid):
        pos = pos * size + pl.program_id(d)
        total *= size
    return pos, total


def normmod_matmul(x, nw, sh, sc, w, name):
    L = x.shape[0]
    tm = min(PROJ_ROWS, L)
    tn = IN_SH // 2

    def body(x_ref, nw_ref, sh_ref, sc_ref, w_ref, p_ref, hx_ref, hx_scr):
        @pl.when(pl.program_id(1) == 0)
        def _():
            xv = x_ref[...]
            n = xv * lax.rsqrt(_lanemean(xv * xv) + EPS) * nw_ref[...]
            h = (n * (1.0 + sc_ref[...]) + sh_ref[...]).astype(BF16)
            hx_scr[...] = h
            hx_ref[...] = h

        p_ref[...] = _dot(hx_scr[...], w_ref[...])

    vec = pl.BlockSpec((1, D), lambda i, j: (0, 0))
    return pl.pallas_call(
        body, name=name,
        grid=(L // tm, D_IN // tn),
        in_specs=[pl.BlockSpec((tm, D), lambda i, j: (i, 0)), vec, vec, vec,
                  pl.BlockSpec((D, tn), lambda i, j: (0, j))],
        out_specs=[pl.BlockSpec((tm, tn), lambda i, j: (i, j)), pl.BlockSpec((tm, D), lambda i, j: (i, 0))],
        out_shape=[jax.ShapeDtypeStruct((L, D_IN), F32), jax.ShapeDtypeStruct((L, D), BF16)],
        scratch_shapes=[pltpu.VMEM((tm, D), BF16)],
        compiler_params=_params("parallel", "arbitrary"),
    )(x, nw, sh, sc, w)


def _w_halves(w_src, col0, tn, wbuf, wsems, pos):
    @pl.when(pos == 0)
    def _():
        for h in range(2):
            pltpu.make_async_copy(w_src.at[:, pl.ds(pl.multiple_of(col0 + h * tn, 128), tn)], wbuf.at[h],
                                  wsems.at[h]).start()

    for h in range(2):
        @pl.when(pos == h)
        def _(h=h):
            pltpu.make_async_copy(w_src.at[:, pl.ds(0, tn)], wbuf.at[h], wsems.at[h]).wait()


def in_proj_own(x, nw, sh, sc, w_shard, shard_arg, name):
    L = x.shape[0]
    tm = min(PROJ_ROWS, L)
    tn = IN_SH // 2
    grid = (L // tm, 2)

    def body(k_ref, x_ref, nw_ref, sh_ref, sc_ref, w_ref, p_ref, hx_ref, wfull_ref, hx_scr, wbuf, wsems, psems,
             *sems):
        pos, total = _grid_step(grid)
        start, forward, finish = _gather_phases([w_ref], [wfull_ref], *sems, relations=(0, 1))
        pl.when(pos == 0)(start)
        _w_halves(w_ref, 0, tn, wbuf, wsems, pos)

        def place(h):
            col = pl.multiple_of(k_ref[0] * IN_SH + h * tn, 128)
            return pltpu.make_async_copy(wbuf.at[h], wfull_ref.at[:, pl.ds(col, tn)], psems.at[h])

        for h in range(2):
            @pl.when(pos == h)
            def _(h=h):
                place(h).start()

        @pl.when(pl.program_id(1) == 0)
        def _():
            xv = x_ref[...]
            n = xv * lax.rsqrt(_lanemean(xv * xv) + EPS) * nw_ref[...]
            h = (n * (1.0 + sc_ref[...]) + sh_ref[...]).astype(BF16)
            hx_scr[...] = h
            hx_ref[...] = h

        p_ref[...] = _dot(hx_scr[...], wbuf[pl.program_id(1)])

        @pl.when(pos == total - 1)
        def _():
            forward()
            finish()
            place(0).wait()
            place(1).wait()

    vec = pl.BlockSpec((1, D), lambda i, j, k: (0, 0))
    return pl.pallas_call(
        body, name=name,
        grid_spec=pltpu.PrefetchScalarGridSpec(
            num_scalar_prefetch=1, grid=grid,
            in_specs=[pl.BlockSpec((tm, D), lambda i, j, k: (i, 0)), vec, vec, vec, ANY],
            out_specs=[pl.BlockSpec((tm, tn), lambda i, j, k: (i, 2 * k[0] + j)),
                       pl.BlockSpec((tm, D), lambda i, j, k: (i, 0)), ANY],
            scratch_shapes=[pltpu.VMEM((tm, D), BF16), pltpu.VMEM((2, D, tn), BF16), pltpu.SemaphoreType.DMA((2,)),
                            pltpu.SemaphoreType.DMA((2,))] + _gather_scratch(1)),
        out_shape=[jax.ShapeDtypeStruct((L, D_IN), F32), jax.ShapeDtypeStruct((L, D), BF16),
                   jax.ShapeDtypeStruct((D, D_IN), BF16)],
        compiler_params=_params("arbitrary", "arbitrary"),
    )(shard_arg, x, nw, sh, sc, w_shard)


def in_proj_next(hx, w_full, shard_arg, p, name, diag_from=None, gather=()):
    L = hx.shape[0]
    tm = min(PROJ_ROWS, L)
    tn = IN_SH // 2
    grid = (L // tm, 2)
    diag = diag_from is not None
    ng = len(gather)
    assert not (diag and ng)

    def body(k_ref, hx_ref, wf_in, p_in, *refs):
        n_src = 1 if diag else ng
        srcs = refs[:n_src]
        p_ref = refs[n_src]
        dsts = refs[n_src + 1:2 * n_src + 1]
        wbuf, wsems = refs[2 * n_src + 1:2 * n_src + 3]
        sems = refs[2 * n_src + 3:2 * n_src + 6]
        stage = refs[2 * n_src + 6:]
        pos, total = _grid_step(grid)
        w_src = dsts[0] if diag else wf_in
        if diag:
            start, forward, finish = _gather_phases(srcs, dsts, *sems, relations=(2,))
        elif ng:
            start, forward, finish = _gather_phases(srcs, dsts, *sems, stage=stage)
        if n_src:
            pl.when(pos == 0)(start)
        _w_halves(w_src, k_ref[0] * IN_SH, tn, wbuf, wsems, pos)
        p_ref[...] = _dot(hx_ref[...], wbuf[pl.program_id(1)])
        if n_src:
            @pl.when(pos == total - 1)
            def _():
                forward()
                finish()

    srcs = [diag_from] if diag else list(gather)
    out_shape = [jax.ShapeDtypeStruct((L, D_IN), F32)]
    if diag:
        out_shape.append(jax.ShapeDtypeStruct(w_full.shape, w_full.dtype))
    out_shape += [jax.ShapeDtypeStruct((N_SHARD,) + s.shape, s.dtype) for s in gather]
    aliases = {3: 0, 2: 1} if diag else {3: 0}
    return pl.pallas_call(
        body, name=name,
        grid_spec=pltpu.PrefetchScalarGridSpec(
            num_scalar_prefetch=1, grid=grid,
            in_specs=[pl.BlockSpec((tm, D), lambda i, j, k: (i, 0)), ANY, ANY] + [ANY] * len(srcs),
            out_specs=[pl.BlockSpec((tm, tn), lambda i, j, k: (i, 2 * k[0] + j))] + [ANY] * len(srcs),
            scratch_shapes=[pltpu.VMEM((2, D, tn), BF16), pltpu.SemaphoreType.DMA((2,))]
            + (_gather_scratch(len(srcs)) if srcs else []) + [pltpu.VMEM(s.shape, s.dtype) for s in gather]),
        out_shape=out_shape,
        input_output_aliases=aliases,
        compiler_params=_params("arbitrary", "arbitrary"),
    )(shard_arg, hx, w_full, p, *srcs)


def _hgrn_gates(z, lb):
    sg = _sigmoid(z)
    sgn = _sigmoid(-z)
    f = lb + (1.0 - lb) * sg
    k = (1.0 - lb) * sgn
    return sg, sgn, f, k


def _tri_chunks(n, chunk, reverse):
    r = lax.broadcasted_iota(jnp.int32, (n, n), 0)
    c = lax.broadcasted_iota(jnp.int32, (n, n), 1)
    same = (r // chunk) == (c // chunk)
    return jnp.where(same & ((r <= c) if reverse else (r >= c)), 1.0, 0.0).astype(F32)


def _decay3(b, reverse):
    C = b.shape[0]
    t = lax.broadcasted_iota(jnp.int32, (C, C, 1), 0)
    s = lax.broadcasted_iota(jnp.int32, (C, C, 1), 1)
    mask = (t <= s) if reverse else (t >= s)
    return jnp.exp(jnp.where(mask, b[:, None, :] - b[None, :, :], -jnp.inf))


HG_SUB = 16


def _hgrn_pairs(reverse):
    pairs = []
    size = HG_SUB
    while size < HG_CHUNK:
        for lo in range(0, HG_CHUNK, 2 * size):
            first, second = slice(lo, lo + size), slice(lo + size, lo + 2 * size)
            if reverse:
                pairs.append((first, second, lo + size))
            else:
                pairs.append((second, first, lo + size - 1))
        size *= 2
    return pairs


def _hgrn_intra_fwd(q, k, v, b, reverse):
    blocks = []
    for lo in range(0, HG_CHUNK, HG_SUB):
        r = slice(lo, lo + HG_SUB)
        att3 = jnp.sum(q[r][:, None, :] * k[r][None, :, :] * _decay3(b[r], reverse), axis=-1, keepdims=True)
        blocks.append(jnp.sum(att3 * v[r][None, :, :], axis=1))
    for qr, kr, ref in _hgrn_pairs(reverse):
        beta = b[ref:ref + 1]
        att = _bdot(q[qr] * jnp.exp(b[qr] - beta), k[kr] * jnp.exp(beta - b[kr]), 1, 1)
        part = _bdot(att, v[kr])
        n = part.shape[0] // HG_SUB
        for i in range(n):
            blocks[qr.start // HG_SUB + i] += part[i * HG_SUB:(i + 1) * HG_SUB]
    return jnp.concatenate(blocks, axis=0)


def _hgrn_intra_bwd(q, k, v, b, d_o, reverse):
    nb = HG_CHUNK // HG_SUB
    dq, dk, dv = [None] * nb, [None] * nb, [None] * nb
    for i in range(nb):
        r = slice(i * HG_SUB, (i + 1) * HG_SUB)
        e3 = _decay3(b[r], reverse)
        p3 = jnp.sum(d_o[r][:, None, :] * v[r][None, :, :], axis=-1, keepdims=True) * e3
        dq[i] = jnp.sum(p3 * k[r][None, :, :], axis=1)
        dk[i] = jnp.sum(p3 * q[r][:, None, :], axis=0)
        att3 = jnp.sum(q[r][:, None, :] * k[r][None, :, :] * e3, axis=-1, keepdims=True)
        dv[i] = jnp.sum(att3 * d_o[r][:, None, :], axis=0)

    def add(acc, rows, part):
        for i in range(part.shape[0] // HG_SUB):
            acc[rows.start // HG_SUB + i] += part[i * HG_SUB:(i + 1) * HG_SUB]

    for qr, kr, ref in _hgrn_pairs(reverse):
        beta = b[ref:ref + 1]
        fq, fk = jnp.exp(b[qr] - beta), jnp.exp(beta - b[kr])
        qt, kt = q[qr] * fq, k[kr] * fk
        att = _bdot(qt, kt, 1, 1)
        datt = _bdot(d_o[qr], v[kr], 1, 1)
        add(dq, qr, _bdot(datt, kt) * fq)
        add(dk, kr, _bdot(datt, qt, 0, 0) * fk)
        add(dv, kr, _bdot(att, d_o[qr], 0, 0))
    return jnp.concatenate(dq, axis=0), jnp.concatenate(dk, axis=0), jnp.concatenate(dv, axis=0)


def _hgrn_state_step(k, v, b, s_t, last):
    b_last = b[last:last + 1]
    return s_t * jnp.exp(b_last) + _bdot(v, k * jnp.exp(b_last - b), 0, 0)


def hgrn_scan_fwd(p, lb, s0, col_z, reverse, name):
    L = p.shape[0]
    nB = L // SCAN_ROWS
    nC = SCAN_ROWS // HG_CHUNK
    C = HG_CHUNK
    G, W = HG_GROUP, HG_GROUP * HG_D
    last = 0 if reverse else C - 1

    def bmap(b):
        return (nB - 1 - b) if reverse else b

    def body(q_ref, z_ref, v_ref, lb_ref, s0_ref, o_ref, sfin_ref, sblk_ref, s_scr, k_scr, b_scr):
        blk = pl.program_id(1)

        @pl.when(blk == 0)
        def _():
            s_scr[...] = s0_ref[...]

        sblk_ref[...] = s_scr[...]
        _, _, f_all, k_all = _hgrn_gates(z_ref[...], lb_ref[...])
        k_scr[...] = k_all
        b_scr[...] = _dot(_tri_chunks(SCAN_ROWS, C, reverse), jnp.log(f_all), prec=HI)

        def chunk(ci, carry):
            c = (nC - 1 - ci) if reverse else ci
            rows = pl.ds(pl.multiple_of(c * C, C), C)
            for j in range(G):
                lanes = slice(j * HG_D, (j + 1) * HG_D)
                q = q_ref[rows, lanes] * Q_SCALE
                v = v_ref[rows, lanes]
                k = k_scr[rows, lanes]
                b = b_scr[rows, lanes]
                s_t = s_scr[j]
                o_ref[rows, lanes] = _hgrn_intra_fwd(q, k, v, b, reverse) + _bdot(q * jnp.exp(b), s_t, 1, 1)
                s_scr[j] = _hgrn_state_step(k, v, b, s_t, last)
            return carry

        lax.fori_loop(0, nC, chunk, 0)

        @pl.when(blk == nB - 1)
        def _():
            sfin_ref[...] = s_scr[...]

    def col(c0):
        return pl.BlockSpec((SCAN_ROWS, W), lambda h, b: (bmap(b), c0 // G + h))

    state = pl.BlockSpec((G, HG_D, HG_D), lambda h, b: (h, 0, 0))
    return pl.pallas_call(
        body, name=name,
        grid=(HEADS // G, nB),
        in_specs=[col(COL_HQ), col(col_z), col(COL_HI), pl.BlockSpec((1, W), lambda h, b: (0, h)), state],
        out_specs=[pl.BlockSpec((SCAN_ROWS, W), lambda h, b: (bmap(b), h)), state,
                   pl.BlockSpec((None, G, HG_D, HG_D), lambda h, b: (bmap(b), h, 0, 0))],
        out_shape=[jax.ShapeDtypeStruct((L, D), F32),
                   jax.ShapeDtypeStruct((HEADS, HG_D, HG_D), F32),
                   jax.ShapeDtypeStruct((nB, HEADS, HG_D, HG_D), F32)],
        scratch_shapes=[pltpu.VMEM((G, HG_D, HG_D), F32), pltpu.VMEM((SCAN_ROWS, W), F32),
                        pltpu.VMEM((SCAN_ROWS, W), F32)],
        compiler_params=_params("parallel", "arbitrary"),
    )(p, p, p, lb, s0)


def hgrn_scan_bwd(p, lb, s_blocks, d_o, ds_fin, prev, col_z, reverse, name):
    L = p.shape[0]
    nB = L // SCAN_ROWS
    nC = SCAN_ROWS // HG_CHUNK
    C = HG_CHUNK
    G, W = HG_GROUP, HG_GROUP * HG_D
    last = 0 if reverse else C - 1
    has_prev = prev is not None
    out_dt = BF16 if has_prev else F32

    def bmap(b):
        return b if reverse else (nB - 1 - b)

    def body(*refs):
        q_ref, z_ref, v_ref, lb_ref, sblk_ref, do_ref, dsf_ref = refs[:7]
        refs = refs[7:]
        if has_prev:
            pq_ref, pv_ref = refs[:2]
            refs = refs[2:]
        dq_ref, dz_ref, dv_ref, dlb_ref, ds0_ref, st_scr, run_scr, ds_scr, k_scr, b_scr, db_scr, dk_scr = refs
        blk = pl.program_id(1)

        @pl.when(blk == 0)
        def _():
            ds_scr[...] = dsf_ref[...]
            dlb_ref[...] = jnp.zeros_like(dlb_ref)

        tri = _tri_chunks(SCAN_ROWS, C, reverse)
        row = lax.broadcasted_iota(jnp.int32, (C, HG_D), 0)
        _, _, f_all, k_all = _hgrn_gates(z_ref[...], lb_ref[...])
        k_scr[...] = k_all
        b_scr[...] = _dot(tri, jnp.log(f_all), prec=HI)
        run_scr[...] = sblk_ref[...]

        def recompute(ci, carry):
            c = (nC - 1 - ci) if reverse else ci
            rows = pl.ds(pl.multiple_of(c * C, C), C)
            for j in range(G):
                lanes = slice(j * HG_D, (j + 1) * HG_D)
                s_t = run_scr[j]
                st_scr[c, j] = s_t
                run_scr[j] = _hgrn_state_step(k_scr[rows, lanes], v_ref[rows, lanes], b_scr[rows, lanes], s_t, last)
            return carry

        lax.fori_loop(0, nC, recompute, 0)

        def chunk(ci, carry):
            c = ci if reverse else (nC - 1 - ci)
            rows = pl.ds(pl.multiple_of(c * C, C), C)
            for j in range(G):
                lanes = slice(j * HG_D, (j + 1) * HG_D)
                k = k_scr[rows, lanes]
                b = b_scr[rows, lanes]
                q = q_ref[rows, lanes] * Q_SCALE
                v = v_ref[rows, lanes]
                d_o = do_ref[rows, lanes]
                s_t = st_scr[c, j]
                ds_t = ds_scr[j]
                eb = jnp.exp(b)
                b_last = b[last:last + 1]
                eb_last = jnp.exp(b_last)
                kdec = jnp.exp(b_last - b)
                qe = q * eb
                ke = k * kdec
                dq_in, dk_in, dv_in = _hgrn_intra_bwd(q, k, v, b, d_o, reverse)
                dq_tot = _bdot(d_o, s_t, 1, 0) * eb + dq_in
                dke = _bdot(v, ds_t, 1, 0)
                dk_tot = dke * kdec + dk_in
                dv = dv_in + _bdot(ke, ds_t, 1, 1)
                db_last = _rowsum(dke * ke) + eb_last * _rowsum(ds_t * s_t)
                db_scr[rows, lanes] = q * dq_tot - k * dk_tot + jnp.where(row == last, db_last, 0.0)
                dk_scr[rows, lanes] = dk_tot
                dq = dq_tot * Q_SCALE
                if has_prev:
                    dq = dq + pq_ref[rows, lanes]
                    dv = dv + pv_ref[rows, lanes]
                dq_ref[rows, lanes] = dq.astype(out_dt)
                dv_ref[rows, lanes] = dv.astype(out_dt)
                ds_scr[j] = ds_t * eb_last + _bdot(d_o, qe, 0, 0)
            return carry

        lax.fori_loop(0, nC, chunk, 0)

        lb = lb_ref[...]
        sg, sgn, f, _ = _hgrn_gates(z_ref[...], lb)
        g = _dot(tri, db_scr[...], 0, 0, prec=HI) / f - dk_scr[...]
        dz_ref[...] = (g * (1.0 - lb) * sg * sgn).astype(BF16)
        dlb_ref[...] += _rowsum(g * sgn)

        @pl.when(blk == nB - 1)
        def _():
            ds0_ref[...] = ds_scr[...]

    def col(c0):
        return pl.BlockSpec((SCAN_ROWS, W), lambda h, b: (bmap(b), c0 // G + h))

    tile = pl.BlockSpec((SCAN_ROWS, W), lambda h, b: (bmap(b), h))
    state = pl.BlockSpec((G, HG_D, HG_D), lambda h, b: (h, 0, 0))
    in_specs = [col(COL_HQ), col(col_z), col(COL_HI),
                pl.BlockSpec((1, W), lambda h, b: (0, h)),
                pl.BlockSpec((None, G, HG_D, HG_D), lambda h, b: (bmap(b), h, 0, 0)),
                tile, state]
    args = [p, p, p, lb, s_blocks, d_o, ds_fin]
    if has_prev:
        in_specs += [tile, tile]
        args += list(prev)
    return pl.pallas_call(
        body, name=name,
        grid=(HEADS // G, nB),
        in_specs=in_specs,
        out_specs=[tile, tile, tile, pl.BlockSpec((1, W), lambda h, b: (0, h)), state],
        out_shape=[jax.ShapeDtypeStruct((L, D), out_dt), jax.ShapeDtypeStruct((L, D), BF16),
                   jax.ShapeDtypeStruct((L, D), out_dt), jax.ShapeDtypeStruct((1, D), F32),
                   jax.ShapeDtypeStruct((HEADS, HG_D, HG_D), F32)],
        scratch_shapes=[pltpu.VMEM((nC, G, HG_D, HG_D), F32), pltpu.VMEM((G, HG_D, HG_D), F32),
                        pltpu.VMEM((G, HG_D, HG_D), F32)] + [pltpu.VMEM((SCAN_ROWS, W), F32)] * 4,
        compiler_params=_params("parallel", "arbitrary"),
    )(*args)


def _rope(t, cosf, sinf):
    return t * cosf + pltpu.roll(t, RT_DK // 2, 1) * sinf


def _rope_t(d, cosf, sinf):
    return d * cosf + pltpu.roll(d * sinf, RT_DK // 2, 1)


def _ret_decays(lg, reverse):
    C = SCAN_ROWS
    t = lax.broadcasted_iota(jnp.int32, (C, C), 0)
    s = lax.broadcasted_iota(jnp.int32, (C, C), 1)
    delta = ((s - t) if reverse else (t - s)).astype(F32)
    dmat = jnp.where(delta >= 0, jnp.exp(lg * jnp.maximum(delta, 0.0)), 0.0)
    r = lax.broadcasted_iota(jnp.int32, (C, RT_DK), 0)
    pos = ((C - 1 - r) if reverse else r).astype(F32)
    lg1 = lg[:, :RT_DK]
    qdec = jnp.exp(lg1 * (pos + 1.0))
    kdec = jnp.exp(lg1 * (C - 1.0 - pos))
    sdec = jnp.exp(lg1 * float(C))
    return dmat, delta, pos, qdec, kdec, sdec


def ret_scan_fwd(p, cosf, sinf, lg, s0, reverse, name):
    L = p.shape[0]
    C = SCAN_ROWS
    nB = L // C

    def bmap(b):
        return (nB - 1 - b) if reverse else b

    G = RT_GROUP

    def body(q_ref, k_ref, v_ref, cos_ref, sin_ref, lg_ref, s0_ref, o_ref, sfin_ref, sblk_ref, s_scr):
        blk = pl.program_id(1)

        @pl.when(blk == 0)
        def _():
            s_scr[...] = s0_ref[...]

        sblk_ref[...] = s_scr[...]
        cosf, sinf = cos_ref[...], sin_ref[...]
        for j in range(G):
            lk, lv = slice(j * RT_DK, (j + 1) * RT_DK), slice(j * RT_DV, (j + 1) * RT_DV)
            s_t = s_scr[j]
            dmat, _, _, qdec, kdec, sdec = _ret_decays(lg_ref[j], reverse)
            q = _rope(q_ref[:, lk] * Q_SCALE, cosf, sinf)
            k = _rope(k_ref[:, lk], cosf, sinf)
            v = v_ref[:, lv]
            att = _bdot(q, k, 1, 1) * dmat
            o_ref[:, lv] = _bdot(att, v) + _bdot(q * qdec, s_t, 1, 1)
            s_scr[j] = s_t * sdec + _bdot(v, k * kdec, 0, 0)

        @pl.when(blk == nB - 1)
        def _():
            sfin_ref[...] = s_scr[...]

    def col(c0):
        return pl.BlockSpec((C, G * RT_DK), lambda h, b: (bmap(b), c0 // G + h))

    tab = pl.BlockSpec((C, RT_DK), lambda h, b: (bmap(b), 0))
    state = pl.BlockSpec((G, RT_DV, RT_DK), lambda h, b: (h, 0, 0))
    return pl.pallas_call(
        body, name=name,
        grid=(HEADS // G, nB),
        in_specs=[col(COL_RQ), col(COL_RK),
                  pl.BlockSpec((C, G * RT_DV), lambda h, b: (bmap(b), COL_RV // (2 * G) + h)),
                  tab, tab, pl.BlockSpec((G, 1, RT_DV), lambda h, b: (h, 0, 0)), state],
        out_specs=[pl.BlockSpec((C, G * RT_DV), lambda h, b: (bmap(b), h)), state,
                   pl.BlockSpec((None, G, RT_DV, RT_DK), lambda h, b: (bmap(b), h, 0, 0))],
        out_shape=[jax.ShapeDtypeStruct((L, HEADS * RT_DV), F32),
                   jax.ShapeDtypeStruct((HEADS, RT_DV, RT_DK), F32),
                   jax.ShapeDtypeStruct((nB, HEADS, RT_DV, RT_DK), F32)],
        scratch_shapes=[pltpu.VMEM((G, RT_DV, RT_DK), F32)],
        compiler_params=_params("parallel", "arbitrary"),
    )(p, p, p, cosf, sinf, lg, s0)


def ret_scan_bwd(p, cosf, sinf, lg, s_blocks, d_o, ds_fin, prev, reverse, name):
    L = p.shape[0]
    C = SCAN_ROWS
    nB = L // C
    has_prev = prev is not None
    out_dt = BF16 if has_prev else F32
    G = RT_GROUP

    def bmap(b):
        return b if reverse else (nB - 1 - b)

    def body(*refs):
        q_ref, k_ref, v_ref, cos_ref, sin_ref, lg_ref, sblk_ref, do_ref, dsf_ref = refs[:9]
        refs = refs[9:]
        if has_prev:
            pq_ref, pk_ref, pv_ref = refs[:3]
            refs = refs[3:]
        dq_ref, dk_ref, dv_ref, dlg_ref, ds0_ref, ds_scr = refs
        blk = pl.program_id(1)

        @pl.when(blk == 0)
        def _():
            ds_scr[...] = dsf_ref[...]
            dlg_ref[...] = jnp.zeros_like(dlg_ref)

        cosf, sinf = cos_ref[...], sin_ref[...]
        for j in range(G):
            lk, lv = slice(j * RT_DK, (j + 1) * RT_DK), slice(j * RT_DV, (j + 1) * RT_DV)
            s_t = sblk_ref[j]
            ds_t = ds_scr[j]
            dmat, delta, pos, qdec, kdec, sdec = _ret_decays(lg_ref[j], reverse)
            q = _rope(q_ref[:, lk] * Q_SCALE, cosf, sinf)
            k = _rope(k_ref[:, lk], cosf, sinf)
            v = v_ref[:, lv]
            d_o = do_ref[:, lv]
            att_raw = _bdot(q, k, 1, 1)
            datt_m = _bdot(d_o, v, 1, 1) * dmat
            dqd = _bdot(d_o, s_t, 1, 0)
            dkd = _bdot(v, ds_t, 1, 0)
            dq = _bdot(datt_m, k) + dqd * qdec
            dk = _bdot(datt_m, q, 0, 0) + dkd * kdec
            dv = _bdot(att_raw * dmat, d_o, 0, 0) + _bdot(k * kdec, ds_t, 1, 1)
            ds_scr[j] = ds_t * sdec + _bdot(d_o, q * qdec, 0, 0)
            t1 = jnp.sum(_rowsum(datt_m * att_raw * delta), axis=-1, keepdims=True)
            t23 = jnp.sum(_rowsum((pos + 1.0) * qdec * q * dqd + (C - 1.0 - pos) * kdec * k * dkd), axis=-1, keepdims=True)
            t4 = jnp.sum(_rowsum(ds_t * s_t * sdec), axis=-1, keepdims=True) * float(C)
            dlg_ref[j] += jnp.broadcast_to(t1 + t23 + t4, (1, RT_DK))
            if has_prev:
                dq = _rope_t(dq + pq_ref[:, lk], cosf, sinf) * Q_SCALE
                dk = _rope_t(dk + pk_ref[:, lk], cosf, sinf)
                dv = dv + pv_ref[:, lv]
            dq_ref[:, lk] = dq.astype(out_dt)
            dk_ref[:, lk] = dk.astype(out_dt)
            dv_ref[:, lv] = dv.astype(out_dt)

        @pl.when(blk == nB - 1)
        def _():
            ds0_ref[...] = ds_scr[...]

    def col(c0):
        return pl.BlockSpec((C, G * RT_DK), lambda h, b: (bmap(b), c0 // G + h))

    tab = pl.BlockSpec((C, RT_DK), lambda h, b: (bmap(b), 0))
    state = pl.BlockSpec((G, RT_DV, RT_DK), lambda h, b: (h, 0, 0))
    tk = pl.BlockSpec((C, G * RT_DK), lambda h, b: (bmap(b), h))
    tv = pl.BlockSpec((C, G * RT_DV), lambda h, b: (bmap(b), h))
    in_specs = [col(COL_RQ), col(COL_RK),
                pl.BlockSpec((C, G * RT_DV), lambda h, b: (bmap(b), COL_RV // (2 * G) + h)),
                tab, tab, pl.BlockSpec((G, 1, RT_DV), lambda h, b: (h, 0, 0)),
                pl.BlockSpec((None, G, RT_DV, RT_DK), lambda h, b: (bmap(b), h, 0, 0)),
                tv, state]
    args = [p, p, p, cosf, sinf, lg, s_blocks, d_o, ds_fin]
    if has_prev:
        in_specs += [tk, tk, tv]
        args += list(prev)
    return pl.pallas_call(
        body, name=name,
        grid=(HEADS // G, nB),
        in_specs=in_specs,
        out_specs=[tk, tk, tv, pl.BlockSpec((G, 1, RT_DK), lambda h, b: (h, 0, 0)), state],
        out_shape=[jax.ShapeDtypeStruct((L, D), out_dt), jax.ShapeDtypeStruct((L, D), out_dt),
                   jax.ShapeDtypeStruct((L, HEADS * RT_DV), out_dt),
                   jax.ShapeDtypeStruct((HEADS, 1, RT_DK), F32),
                   jax.ShapeDtypeStruct((HEADS, RT_DV, RT_DK), F32)],
        scratch_shapes=[pltpu.VMEM((G, RT_DV, RT_DK), F32)],
        compiler_params=_params("parallel", "arbitrary"),
    )(*args)


def _silu_parts(h):
    s = _sigmoid(h)
    return h * s, s * (1.0 + h * (1.0 - s))


def _head_rms(o):
    outs, rs = [], []
    for h in range(HEADS):
        oh = o[:, h * HG_D:(h + 1) * HG_D]
        r = lax.rsqrt(_lanemean(oh * oh) + EPS)
        outs.append(oh * r)
        rs.append(r)
    return outs, rs


def _group_norm(o):
    outs, rs = [], []
    for h in range(HEADS):
        oh = o[:, h * RT_DV:(h + 1) * RT_DV]
        c = oh - _lanemean(oh)
        r = lax.rsqrt(_lanemean(c * c) + GN_EPS)
        outs.append(c * r)
        rs.append(r)
    return outs, rs


MIX_ROWS = 256
MIX_BWD_ROWS = 128


def _mix_specs(rows):
    def t(w, c=0):
        return pl.BlockSpec((rows, w), lambda i: (i, c))

    return t


def mix_fwd(ohf, ohb, orf, orb, p, x, g1, hgw, w_pa, w_pb, w_out, name):
    L = x.shape[0]
    t = _mix_specs(MIX_ROWS)

    def body(ohf_ref, ohb_ref, orf_ref, orb_ref, hg_ref, rg0_ref, rg1_ref, ga_ref, gb_ref, x_ref, g1_ref, hgw_ref,
             wpa_ref, wpb_ref, wout_ref, x1_ref, xmix_ref, merged_ref, ya_ref, yb_ref):
        nh, _ = _head_rms(ohf_ref[...] + ohb_ref[...])
        ya = jnp.concatenate(nh, axis=1) * hgw_ref[...] * _silu_parts(hg_ref[...])[0]
        gn, _ = _group_norm(orf_ref[...] + orb_ref[...])
        rg = jnp.concatenate([rg0_ref[...], rg1_ref[...]], axis=1)
        yb = jnp.concatenate(gn, axis=1) * _silu_parts(rg)[0]
        ya16, yb16 = ya.astype(BF16), yb.astype(BF16)
        merged = (_sigmoid(ga_ref[...]) * _dot(ya16, wpa_ref[...])
                  + _sigmoid(gb_ref[...]) * _dot(yb16, wpb_ref[...])).astype(BF16)
        x_mix = _dot(merged, wout_ref[...])
        x1_ref[...] = x_ref[...] + g1_ref[...] * x_mix
        xmix_ref[...] = x_mix
        merged_ref[...] = merged
        ya_ref[...] = ya16
        yb_ref[...] = yb16

    vec = pl.BlockSpec((1, D), lambda i: (0, 0))

    def full(a):
        return pl.BlockSpec(a.shape, lambda i: (0, 0), pipeline_mode=pl.Buffered(1))

    return pl.pallas_call(
        body, name=name,
        grid=(L // MIX_ROWS,),
        in_specs=[t(D), t(D), t(2 * D), t(2 * D), t(D, COL_HG // 8), t(D, COL_RG // 8), t(D, COL_RG // 8 + 1),
                  t(D, COL_GA // 8), t(D, COL_GB // 8), t(D), vec, vec, full(w_pa), full(w_pb), full(w_out)],
        out_specs=[t(D), t(D), t(D), t(D), t(2 * D)],
        out_shape=[jax.ShapeDtypeStruct((L, D), F32), jax.ShapeDtypeStruct((L, D), F32),
                   jax.ShapeDtypeStruct((L, D), BF16), jax.ShapeDtypeStruct((L, D), BF16),
                   jax.ShapeDtypeStruct((L, 2 * D), BF16)],
        compiler_params=_params("parallel"),
    )(ohf, ohb, orf, orb, p, p, p, p, p, x, g1, hgw, w_pa, w_pb, w_out)


def mix_bwd(dx1, x_mix, ya, yb, ohf, ohb, orf, orb, p, g1, hgw, w_pa, w_pb, w_out, name):
    L = dx1.shape[0]
    t = _mix_specs(MIX_BWD_ROWS)

    def body(dx1_ref, xmix_ref, ya_ref, yb_ref, ohf_ref, ohb_ref, orf_ref, orb_ref, hg_ref, rg0_ref, rg1_ref,
             ga_ref, gb_ref, g1_ref, hgw_ref, wpa_ref, wpb_ref, wout_ref,
             dxm_ref, da_ref, db_ref, dga_ref, dgb_ref, dhg_ref, drg_ref, dohg_ref, dort_ref, sums_ref):
        @pl.when(pl.program_id(0) == 0)
        def _():
            sums_ref[...] = jnp.zeros_like(sums_ref)

        dx1 = dx1_ref[...]
        dxm = (g1_ref[...] * dx1).astype(BF16)
        dxm_ref[...] = dxm
        dmerged = _dot(dxm, wout_ref[...], 1, 1)
        a = _dot(ya_ref[...], wpa_ref[...])
        bm = _dot(yb_ref[...], wpb_ref[...])
        sa, sb = _sigmoid(ga_ref[...]), _sigmoid(gb_ref[...])
        d_a = (dmerged * sa).astype(BF16)
        d_b = (dmerged * sb).astype(BF16)
        da_ref[...] = d_a
        db_ref[...] = d_b
        dga_ref[...] = (dmerged * a * sa * (1.0 - sa)).astype(BF16)
        dgb_ref[...] = (dmerged * bm * sb * (1.0 - sb)).astype(BF16)
        dya = _dot(d_a, wpa_ref[...], 1, 1)
        dyb = _dot(d_b, wpb_ref[...], 1, 1)

        hgw = hgw_ref[...]
        silu_h, dsilu_h = _silu_parts(hg_ref[...])
        nh, rh = _head_rms(ohf_ref[...] + ohb_ref[...])
        n = jnp.concatenate(nh, axis=1)
        dhg_ref[...] = (dya * n * hgw * dsilu_h).astype(BF16)
        dn = dya * hgw * silu_h
        douts = []
        for h in range(HEADS):
            dnh = dn[:, h * HG_D:(h + 1) * HG_D]
            douts.append(rh[h] * (dnh - nh[h] * _lanemean(dnh * nh[h])))
        dohg_ref[...] = jnp.concatenate(douts, axis=1)

        rg = jnp.concatenate([rg0_ref[...], rg1_ref[...]], axis=1)
        silu_r, dsilu_r = _silu_parts(rg)
        gn, rr = _group_norm(orf_ref[...] + orb_ref[...])
        g = jnp.concatenate(gn, axis=1)
        drg_ref[...] = (dyb * g * dsilu_r).astype(BF16)
        dgn = dyb * silu_r
        douts = []
        for h in range(HEADS):
            dgh = dgn[:, h * RT_DV:(h + 1) * RT_DV]
            douts.append(rr[h] * (dgh - _lanemean(dgh) - gn[h] * _lanemean(dgh * gn[h])))
        dort_ref[...] = jnp.concatenate(douts, axis=1)

        sums_ref[0:1, :] += _rowsum(dx1 * xmix_ref[...])
        sums_ref[1:2, :] += _rowsum(dya * n * silu_h)

    vec = pl.BlockSpec((1, D), lambda i: (0, 0))

    def full(a):
        return pl.BlockSpec(a.shape, lambda i: (0, 0), pipeline_mode=pl.Buffered(1))

    bf = functools.partial(jax.ShapeDtypeStruct, dtype=BF16)
    return pl.pallas_call(
        body, name=name,
        grid=(L // MIX_BWD_ROWS,),
        in_specs=[t(D), t(D), t(D), t(2 * D), t(D), t(D), t(2 * D), t(2 * D),
                  t(D, COL_HG // 8), t(D, COL_RG // 8), t(D, COL_RG // 8 + 1), t(D, COL_GA // 8), t(D, COL_GB // 8),
                  vec, vec, full(w_pa), full(w_pb), full(w_out)],
        out_specs=[t(D), t(D), t(D), t(D), t(D), t(D), t(2 * D), t(D), t(2 * D),
                   pl.BlockSpec((8, D), lambda i: (0, 0))],
        out_shape=[bf((L, D)), bf((L, D)), bf((L, D)), bf((L, D)), bf((L, D)), bf((L, D)), bf((L, 2 * D)),
                   jax.ShapeDtypeStruct((L, D), F32), jax.ShapeDtypeStruct((L, 2 * D), F32),
                   jax.ShapeDtypeStruct((8, D), F32)],
        compiler_params=_params("arbitrary"),
    )(dx1, x_mix, ya, yb, ohf, ohb, orf, orb, p, p, p, p, p, g1, hgw, w_pa, w_pb, w_out)


FFN_ROWS = 512


def ffn_fwd(x1, target, nw2, sh2, sc2, g2, fw, wg, wu, wd, name):
    L = x1.shape[0]
    tm = min(FFN_ROWS, L)

    def body(x1_ref, tgt_ref, nw2_ref, sh2_ref, sc2_ref, g2_ref, fw_ref, wg_ref, wu_ref, wd_ref,
             hx2_ref, g_ref, u_ref, h_ref, f_ref, dx2_ref, sums_ref, hx_scr, acc):
        i, j = pl.program_id(0), pl.program_id(1)

        @pl.when((i == 0) & (j == 0))
        def _():
            sums_ref[...] = jnp.zeros_like(sums_ref)

        @pl.when(j == 0)
        def _():
            xv = x1_ref[...]
            n = xv * lax.rsqrt(_lanemean(xv * xv) + EPS) * nw2_ref[...]
            h = (n * (1.0 + sc2_ref[...]) + sh2_ref[...]).astype(BF16)
            hx_scr[...] = h
            hx2_ref[...] = h
            acc[...] = jnp.zeros_like(acc)

        hx = hx_scr[...]
        g = _dot(hx, wg_ref[...])
        u = _dot(hx, wu_ref[...])
        hh = (_silu_parts(g)[0] * u).astype(BF16)
        g_ref[...] = g
        u_ref[...] = u
        h_ref[...] = hh
        acc[...] += _dot(hh, wd_ref[...])

        @pl.when(j == N_SHARD - 1)
        def _():
            f = acc[...]
            f_ref[...] = f
            x2 = x1_ref[...] + g2_ref[...] * f
            r = lax.rsqrt(_lanemean(x2 * x2) + EPS)
            fw = fw_ref[...]
            e = x2 * r * fw - tgt_ref[...]
            dy = e * (1.0 / D)
            dyw = dy * fw
            dx2_ref[...] = r * dyw - x2 * (r * r * r) * _lanemean(dyw * x2)
            sums_ref[0:1, :] += _rowsum(dy * x2 * r)
            sums_ref[1:2, :] += _rowsum(e * e) * (0.5 / D)

    row = pl.BlockSpec((tm, D), lambda i, j: (i, 0))
    vec = pl.BlockSpec((1, D), lambda i, j: (0, 0))
    sh = pl.BlockSpec((None, tm, FF_SH), lambda i, j: (j, i, 0))
    return pl.pallas_call(
        body, name=name,
        grid=(L // tm, N_SHARD),
        in_specs=[row, row, vec, vec, vec, vec, vec,
                  pl.BlockSpec((None, D, FF_SH), lambda i, j: (j, 0, 0)),
                  pl.BlockSpec((None, D, FF_SH), lambda i, j: (j, 0, 0)),
                  pl.BlockSpec((None, FF_SH, D), lambda i, j: (j, 0, 0))],
        out_specs=[row, sh, sh, sh, row, row, pl.BlockSpec((8, D), lambda i, j: (0, 0))],
        out_shape=[jax.ShapeDtypeStruct((L, D), BF16),
                   jax.ShapeDtypeStruct((N_SHARD, L, FF_SH), F32), jax.ShapeDtypeStruct((N_SHARD, L, FF_SH), F32),
                   jax.ShapeDtypeStruct((N_SHARD, L, FF_SH), BF16),
                   jax.ShapeDtypeStruct((L, D), F32), jax.ShapeDtypeStruct((L, D), F32),
                   jax.ShapeDtypeStruct((8, D), F32)],
        scratch_shapes=[pltpu.VMEM((tm, D), BF16), pltpu.VMEM((tm, D), F32)],
        compiler_params=_params("arbitrary", "arbitrary"),
    )(x1, target, nw2, sh2, sc2, g2, fw, wg, wu, wd)


def ffn_bwd(dx2, x1, f, g, u, nw2, sc2, g2, wg, wu, wd, name):
    L = x1.shape[0]
    tm = min(FFN_ROWS, L)

    def body(dx2_ref, x1_ref, f_ref, g_ref, u_ref, nw2_ref, sc2_ref, g2_ref, wg_ref, wu_ref, wd_ref,
             df_ref, dg_ref, du_ref, dx1_ref, sums_ref, df_scr, acc):
        i, j = pl.program_id(0), pl.program_id(1)

        @pl.when((i == 0) & (j == 0))
        def _():
            sums_ref[...] = jnp.zeros_like(sums_ref)

        @pl.when(j == 0)
        def _():
            dx2 = dx2_ref[...]
            df = (g2_ref[...] * dx2).astype(BF16)
            df_scr[...] = df
            df_ref[...] = df
            sums_ref[0:1, :] += _rowsum(dx2 * f_ref[...])
            acc[...] = jnp.zeros_like(acc)

        dh = _dot(df_scr[...], wd_ref[...], 1, 1)
        gv, uv = g_ref[...], u_ref[...]
        silu_g, dsilu_g = _silu_parts(gv)
        dg = (dh * uv * dsilu_g).astype(BF16)
        du = (dh * silu_g).astype(BF16)
        dg_ref[...] = dg
        du_ref[...] = du
        acc[...] += _dot(dg, wg_ref[...], 1, 1) + _dot(du, wu_ref[...], 1, 1)

        @pl.when(j == N_SHARD - 1)
        def _():
            dhx = acc[...]
            xv = x1_ref[...]
            r = lax.rsqrt(_lanemean(xv * xv) + EPS)
            n0 = xv * r
            nw = nw2_ref[...]
            dn2 = dhx * (1.0 + sc2_ref[...])
            dn0 = dn2 * nw
            dx1_ref[...] = dx2_ref[...] + r * (dn0 - n0 * _lanemean(dn0 * n0))
            sums_ref[1:2, :] += _rowsum(dhx)
            sums_ref[2:3, :] += _rowsum(dhx * n0 * nw)
            sums_ref[3:4, :] += _rowsum(dn2 * n0)

    row = pl.BlockSpec((tm, D), lambda i, j: (i, 0))
    vec = pl.BlockSpec((1, D), lambda i, j: (0, 0))
    sh = pl.BlockSpec((None, tm, FF_SH), lambda i, j: (j, i, 0))
    return pl.pallas_call(
        body, name=name,
        grid=(L // tm, N_SHARD),
        in_specs=[row, row, row, sh, sh, vec, vec, vec,
                  pl.BlockSpec((None, D, FF_SH), lambda i, j: (j, 0, 0)),
                  pl.BlockSpec((None, D, FF_SH), lambda i, j: (j, 0, 0)),
                  pl.BlockSpec((None, FF_SH, D), lambda i, j: (j, 0, 0))],
        out_specs=[row, sh, sh, row, pl.BlockSpec((8, D), lambda i, j: (0, 0))],
        out_shape=[jax.ShapeDtypeStruct((L, D), BF16),
                   jax.ShapeDtypeStruct((N_SHARD, L, FF_SH), BF16), jax.ShapeDtypeStruct((N_SHARD, L, FF_SH), BF16),
                   jax.ShapeDtypeStruct((L, D), F32), jax.ShapeDtypeStruct((8, D), F32)],
        scratch_shapes=[pltpu.VMEM((tm, D), BF16), pltpu.VMEM((tm, D), F32)],
        compiler_params=_params("arbitrary", "arbitrary"),
    )(dx2, x1, f, g, u, nw2, sc2, g2, wg, wu, wd)


def matmul_tn(a, b, name, acc_init=None, to_chips=()):
    na, K, M = a.shape
    nb, _, N = b.shape
    n = max(na, nb)
    tk = min(512, K)
    tn = N if N <= 1024 else N // 2
    nk = K // tk
    grid = (n, N // tn, nk)
    has_init = acc_init is not None
    nx = len(to_chips)

    def body(a_ref, b_ref, *refs):
        init_ref = refs[0] if has_init else None
        refs = refs[1:] if has_init else refs
        o_ref = refs[nx]
        if nx:
            start, finish = _to_chips_phases(refs[:nx], refs[nx + 1:2 * nx + 1], *refs[2 * nx + 1:])
            pos, total = _grid_step(grid)
            pl.when(pos == 0)(start)
        kk = pl.program_id(2)

        @pl.when(kk == 0)
        def _():
            o_ref[...] = init_ref[...] if has_init else jnp.zeros_like(o_ref)

        o_ref[...] += _dot(a_ref[...], b_ref[...], 0, 0)
        if nx:
            pl.when(pos == total - 1)(finish)

    out_spec = pl.BlockSpec((None, M, tn), lambda s, j, kk: (s, 0, j))
    in_specs = [pl.BlockSpec((None, tk, M), lambda s, j, kk: (s if na > 1 else 0, kk, 0)),
                pl.BlockSpec((None, tk, tn), lambda s, j, kk: (s if nb > 1 else 0, kk, j))]
    args = [a, b]
    if has_init:
        in_specs.append(out_spec)
        args.append(acc_init)
    out = pl.pallas_call(
        body, name=name,
        grid=grid,
        in_specs=in_specs + [ANY] * nx,
        out_specs=[out_spec] + [ANY] * nx,
        out_shape=[jax.ShapeDtypeStruct((n, M, N), F32)] + _to_chips_shapes(to_chips),
        scratch_shapes=_to_chips_scratch(nx) if nx else [],
        compiler_params=_params(*(("arbitrary",) * 3 if nx else ("parallel", "parallel", "arbitrary"))),
    )(*args, *to_chips)
    return out if nx else out[0]


def matmul_tn_pair(a, b1, b2, name):
    K, M = a.shape
    n, _, N = b1.shape
    tk = min(512, K)

    def body(a_ref, b1_ref, b2_ref, o1_ref, o2_ref):
        @pl.when(pl.program_id(1) == 0)
        def _():
            o1_ref[...] = jnp.zeros_like(o1_ref)
            o2_ref[...] = jnp.zeros_like(o2_ref)

        at = a_ref[...].T
        o1_ref[...] += _dot(at, b1_ref[...])
        o2_ref[...] += _dot(at, b2_ref[...])

    b_spec = pl.BlockSpec((None, tk, N), lambda s, kk: (s, kk, 0))
    o_spec = pl.BlockSpec((None, M, N), lambda s, kk: (s, 0, 0))
    return pl.pallas_call(
        body, name=name,
        grid=(n, K // tk),
        in_specs=[pl.BlockSpec((tk, M), lambda s, kk: (kk, 0)), b_spec, b_spec],
        out_specs=[o_spec, o_spec],
        out_shape=[jax.ShapeDtypeStruct((n, M, N), F32)] * 2,
        compiler_params=_params("parallel", "arbitrary"),
    )(a, b1, b2)


PIECE_COLS = 1024
N_PIECE_BLOCKS = D_IN // PIECE_COLS


def _piece_blocks(pieces):
    out, col = [], 0
    for arr, width in pieces:
        if arr is not None:
            out.append((arr, col // PIECE_COLS, width // PIECE_COLS))
        col += width
    assert col == D_IN
    return out


def _piece_feed(p_refs, blocks, buf, sems, tile_of, pos, total):
    def present(blk):
        ok = None
        for _, b0, nb in blocks:
            mine = (blk >= b0) & (blk < b0 + nb)
            ok = mine if ok is None else ok | mine
        return ok

    def fetch(step):
        blk, rows = tile_of(step)
        for p_ref, (_, b0, nb) in zip(p_refs, blocks):
            for t in range(nb):
                @pl.when(blk == b0 + t)
                def _(p_ref=p_ref, t=t):
                    pltpu.make_async_copy(p_ref.at[rows, pl.ds(t * PIECE_COLS, PIECE_COLS)], buf.at[step % 2],
                                          sems.at[step % 2]).start()

    @pl.when(pos == 0)
    def _():
        fetch(pos)

    @pl.when(pos + 1 < total)
    def _():
        fetch(pos + 1)

    def landed():
        slot = pos % 2
        pltpu.make_async_copy(p_refs[0].at[pl.ds(0, buf.shape[1]), pl.ds(0, PIECE_COLS)], buf.at[slot],
                              sems.at[slot]).wait()
        return buf.at[slot]

    return present(tile_of(pos)[0]), landed


def matmul_tn_pieces(a, pieces, name, acc_init=None, to_chips=()):
    K, M = a.shape
    blocks = _piece_blocks(pieces)
    tk = min(1024, K)
    nk = K // tk
    grid = (N_PIECE_BLOCKS, nk)
    has_init = acc_init is not None
    nx, npc = len(to_chips), len(blocks)

    def body(a_ref, *refs):
        p_refs = refs[:npc]
        refs = refs[npc:]
        init_ref = refs[0] if has_init else None
        refs = refs[1:] if has_init else refs
        o_ref = refs[nx]
        buf, sems = refs[2 * nx + 1:2 * nx + 3]
        pos, total = _grid_step(grid)
        if nx:
            start, finish = _to_chips_phases(refs[:nx], refs[nx + 1:2 * nx + 1], *refs[2 * nx + 3:])
            pl.when(pos == 0)(start)
        here, landed = _piece_feed(p_refs, blocks, buf, sems,
                                   lambda s: (s // nk, pl.ds(pl.multiple_of((s % nk) * tk, tk), tk)), pos, total)

        @pl.when(pl.program_id(1) == 0)
        def _():
            o_ref[...] = init_ref[...] if has_init else jnp.zeros_like(o_ref)

        @pl.when(here)
        def _():
            o_ref[...] += _dot(a_ref[...], landed()[...], 0, 0)

        if nx:
            pl.when(pos == total - 1)(finish)

    out_spec = pl.BlockSpec((M, PIECE_COLS), lambda blk, kk: (0, blk))
    in_specs = [pl.BlockSpec((tk, M), lambda blk, kk: (kk, 0))] + [ANY] * npc
    args = [a] + [arr for arr, _, _ in blocks]
    if has_init:
        in_specs.append(out_spec)
        args.append(acc_init)
    out = pl.pallas_call(
        body, name=name,
        grid=grid,
        in_specs=in_specs + [ANY] * nx,
        out_specs=[out_spec] + [ANY] * nx,
        out_shape=[jax.ShapeDtypeStruct((M, D_IN), F32)] + _to_chips_shapes(to_chips),
        scratch_shapes=[pltpu.VMEM((2, tk, PIECE_COLS), BF16), pltpu.SemaphoreType.DMA((2,))]
        + (_to_chips_scratch(nx) if nx else []),
        compiler_params=_params("arbitrary", "arbitrary"),
    )(*args, *to_chips)
    return out if nx else out[0]


def dhx_normbwd(pieces, w, x, dx_res, nw, sc, name, to_chips=()):
    L = x.shape[0]
    tm = min(PROJ_ROWS, L)
    blocks = _piece_blocks(pieces)
    grid = (L // tm, N_PIECE_BLOCKS)
    nx, npc = len(to_chips), len(blocks)

    def body(*refs):
        p_refs = refs[:npc]
        w_ref, x_ref, res_ref, nw_ref, sc_ref = refs[npc:npc + 5]
        refs = refs[npc + 5:]
        dx_ref, sums_ref = refs[nx:nx + 2]
        acc, buf, sems = refs[2 * nx + 2:2 * nx + 5]
        pos, total = _grid_step(grid)
        if nx:
            start, finish = _to_chips_phases(refs[:nx], refs[nx + 2:2 * nx + 2], *refs[2 * nx + 5:])
            pl.when(pos == 0)(start)
            pl.when(pos == total - 1)(finish)
        here, landed = _piece_feed(
            p_refs, blocks, buf, sems,
            lambda s: (s % N_PIECE_BLOCKS, pl.ds(pl.multiple_of((s // N_PIECE_BLOCKS) * tm, tm), tm)), pos, total)
        i, blk = pl.program_id(0), pl.program_id(1)

        @pl.when((i == 0) & (blk == 0))
        def _():
            sums_ref[...] = jnp.zeros_like(sums_ref)

        @pl.when(blk == 0)
        def _():
            acc[...] = jnp.zeros_like(acc)

        @pl.when(here)
        def _():
            acc[...] += _dot(landed()[...], w_ref[...], 1, 1)

        @pl.when(blk == N_PIECE_BLOCKS - 1)
        def _():
            dhx = acc[...]
            xv = x_ref[...]
            r = lax.rsqrt(_lanemean(xv * xv) + EPS)
            n0 = xv * r
            nw = nw_ref[...]
            dn = dhx * (1.0 + sc_ref[...])
            dn0 = dn * nw
            dx_ref[...] = res_ref[...] + r * (dn0 - n0 * _lanemean(dn0 * n0))
            sums_ref[0:1, :] += _rowsum(dhx)
            sums_ref[1:2, :] += _rowsum(dhx * n0 * nw)
            sums_ref[2:3, :] += _rowsum(dn * n0)

    row = pl.BlockSpec((tm, D), lambda i, blk: (i, 0))
    vec = pl.BlockSpec((1, D), lambda i, blk: (0, 0))
    return pl.pallas_call(
        body, name=name,
        grid=grid,
        in_specs=[ANY] * npc + [pl.BlockSpec((D, PIECE_COLS), lambda i, blk: (0, blk)), row, row, vec, vec] + [ANY] * nx,
        out_specs=[row, pl.BlockSpec((8, D), lambda i, blk: (0, 0))] + [ANY] * nx,
        out_shape=[jax.ShapeDtypeStruct((L, D), F32), jax.ShapeDtypeStruct((8, D), F32)] + _to_chips_shapes(to_chips),
        scratch_shapes=[pltpu.VMEM((tm, D), F32), pltpu.VMEM((2, tm, PIECE_COLS), BF16), pltpu.SemaphoreType.DMA((2,))]
        + (_to_chips_scratch(nx) if nx else []),
        compiler_params=_params("arbitrary", "arbitrary"),
    )(*[arr for arr, _, _ in blocks], w, x, dx_res, nw, sc, *to_chips)


SMALL_ROWS = 24


def _rope_tables(L):
    rows = L // 64
    freqs = 10000.0 ** (-jnp.arange(RT_DK // 4, dtype=F32) / (RT_DK // 4))
    a_row = jnp.arange(rows, dtype=F32)[:, None] * freqs
    a_col = jnp.arange(64, dtype=F32)[:, None] * freqs

    def spread(f):
        return jnp.concatenate([jnp.repeat(f(a_row), 64, axis=0), jnp.tile(f(a_col), (rows, 1))], axis=-1)

    cos, sin = spread(jnp.cos), spread(jnp.sin)
    return jnp.concatenate([cos, cos], axis=1), jnp.concatenate([-sin, sin], axis=1)


def _pieces(hq, hf_f, hf_b, hi, hg, rq, rk, rv, rg, ga, gb):
    widths = (D, D, D, D, D, D, D, 2 * D, 2 * D, D, D)
    return list(zip((hq, hf_f, hf_b, hi, hg, rq, rk, rv, rg, ga, gb), widths))


def _lane0(a):
    return a[:, 0, 0]


def _pack_small(rows):
    out = [r.reshape(1, D) for r in rows]
    out += [jnp.zeros((1, D), F32)] * (SMALL_ROWS - len(out))
    return jnp.concatenate(out, axis=0)


def _sibling_sums(gs, names, place):
    core, core_arg, _ = place

    def other_half(g):
        axis = g.ndim - 2
        h = g.shape[axis] // 2
        return lax.dynamic_slice_in_dim(g, (1 - core) * h, h, axis=axis).astype(BF16)

    payload = [other_half(g) for g in gs]
    received = rs_to_sibling(payload, "rs_to_sibling_" + names[0])
    return [rs_add_sibling(g, r, core_arg, "rs_add_sibling_" + k) for g, r, k in zip(gs, received, names)]


def _staged_in_proj(x, nw, sh, sc, w_shard, rest, chip):
    cx, cy = chip // 2, chip % 2

    def arg(k):
        return jnp.reshape(k, (1,)).astype(jnp.int32)

    p, hx, w_full = in_proj_own(x, nw, sh, sc, w_shard, arg(chip), "in_proj_own")
    p, w_full = in_proj_next(hx, w_full, arg(2 * (1 - cx) + cy), p, "in_proj_x", diag_from=w_shard)
    w_pa, w_pb, w_out, w_wd = rest[0], rest[1], rest[2], rest[5]
    p, g_pa, g_pb, g_out, g_wd = in_proj_next(hx, w_full, arg(2 * cx + 1 - cy), p, "in_proj_y",
                                              gather=[w_pa, w_pb, w_out, w_wd])
    p, g_wg, g_wu = in_proj_next(hx, w_full, arg(3 - chip), p, "in_proj_diag", gather=[rest[3], rest[4]])
    w = {"w_in": w_full, "w_pa": g_pa.reshape(D, D), "w_pb": g_pb.reshape(2 * D, D), "w_out": g_out.reshape(D, D),
         "wg": g_wg, "wu": g_wu, "wd": g_wd}
    return p, hx, w


def local_step(x, ctx, target, mod_x, mod_c, lb_f, lb_b, lg_f, lg_b, nw1, nw2, hgw, fw, w, rest=None, place=None):
    L, Lc = x.shape[0], ctx.shape[0]
    sh1, sc1, g1, sh2, sc2, g2 = (mod_x[i:i + 1] for i in range(6))
    sh1c, sc1c = mod_c[0:1], mod_c[1:2]
    cosf, sinf = _rope_tables(L)
    cosc, sinc = jnp.ones((Lc, RT_DK), F32), jnp.zeros((Lc, RT_DK), F32)
    zero_h = jnp.zeros((HEADS, HG_D, HG_D), F32)
    zero_r = jnp.zeros((HEADS, RT_DV, RT_DK), F32)

    if rest is None:
        p, hx = normmod_matmul(x, nw1, sh1, sc1, w["w_in"], "in_proj")
    else:
        p, hx, w = _staged_in_proj(x, nw1, sh1, sc1, w["w_in_shard"], rest, place[2][0])
    pc, hxc = normmod_matmul(ctx, nw1, sh1c, sc1c, w["w_in"], "ctx_in_proj")
    _, s_hf, cb_hf = hgrn_scan_fwd(pc, lb_f, zero_h, COL_HFF, False, "ctx_hgrn_f")
    _, s_hb, cb_hb = hgrn_scan_fwd(pc, lb_b, zero_h, COL_HFB, True, "ctx_hgrn_b")
    _, s_rf, cb_rf = ret_scan_fwd(pc, cosc, sinc, lg_f, zero_r, False, "ctx_ret_f")
    _, s_rb, cb_rb = ret_scan_fwd(pc, cosc, sinc, lg_b, zero_r, True, "ctx_ret_b")
    ohf, _, xb_hf = hgrn_scan_fwd(p, lb_f, s_hf, COL_HFF, False, "hgrn_f")
    ohb, _, xb_hb = hgrn_scan_fwd(p, lb_b, s_hb, COL_HFB, True, "hgrn_b")
    orf, _, xb_rf = ret_scan_fwd(p, cosf, sinf, lg_f, s_rf, False, "ret_f")
    orb, _, xb_rb = ret_scan_fwd(p, cosf, sinf, lg_b, s_rb, True, "ret_b")
    x1, x_mix, merged, ya, yb = mix_fwd(ohf, ohb, orf, orb, p, x, g1, hgw, w["w_pa"], w["w_pb"], w["w_out"], "mix_fwd")
    hx2, gg, uu, hh, ff, dx2, sums_f = ffn_fwd(x1, target, nw2, sh2, sc2, g2, fw, w["wg"], w["wu"], w["wd"], "ffn_fwd")

    d_f, d_g, d_u, dx1, sums_fb = ffn_bwd(dx2, x1, ff, gg, uu, nw2, sc2, g2, w["wg"], w["wu"], w["wd"], "ffn_bwd")
    dw_gate, dw_up = matmul_tn_pair(hx2, d_g, d_u, "dw_ffn_gate_up")
    grads = {"wg": dw_gate, "wu": dw_up, "wd": matmul_tn(hh, d_f[None], "dw_ffn_down")}
    dxm, d_a, d_b, dga, dgb, dhg, drg, dohg, dort, sums_m = mix_bwd(
        dx1, x_mix, ya, yb, ohf, ohb, orf, orb, p, g1, hgw, w["w_pa"], w["w_pb"], w["w_out"], "mix_bwd")
    grads["w_out"] = matmul_tn(merged[None], dxm[None], "dw_out").reshape(N_SHARD, D // N_SHARD, D)
    grads["w_pa"] = matmul_tn(ya[None], d_a[None], "dw_proj_hgrn").reshape(N_SHARD, D // N_SHARD, D)
    grads["w_pb"] = matmul_tn(yb[None], d_b[None], "dw_proj_ret").reshape(N_SHARD, 2 * D // N_SHARD, D)

    rq1, rk1, rv1, dlgf_x, ds_rf = ret_scan_bwd(p, cosf, sinf, lg_f, xb_rf, dort, zero_r, None, False, "ret_f_bwd")
    drq, drk, drv, dlgb_x, ds_rb = ret_scan_bwd(p, cosf, sinf, lg_b, xb_rb, dort, zero_r, (rq1, rk1, rv1), True, "ret_b_bwd")
    hq1, dzf, hv1, dlbf_x, ds_hf = hgrn_scan_bwd(p, lb_f, xb_hf, dohg, zero_h, None, COL_HFF, False, "hgrn_f_bwd")
    dhq, dzb, dhv, dlbb_x, ds_hb = hgrn_scan_bwd(p, lb_b, xb_hb, dohg, zero_h, (hq1, hv1), COL_HFB, True, "hgrn_b_bwd")
    dp = _pieces(dhq, dzf, dzb, dhv, dhg, drq, drk, drv, drg, dga, dgb)
    others = ["w_pa", "w_pb", "w_out", "wg", "wu", "wd"]
    if place is None:
        dw_in = matmul_tn_pieces(hx, dp, "dw_in")
    else:
        sums_o = _sibling_sums([grads[k] for k in others], others, place)
        dw_in, *recv_o = matmul_tn_pieces(hx, dp, "dw_in", to_chips=[a16 for _, a16 in sums_o])

    zc = jnp.zeros((Lc, D), F32)
    zc2 = jnp.zeros((Lc, 2 * D), F32)
    crq1, crk1, crv1, dlgf_c, _ = ret_scan_bwd(pc, cosc, sinc, lg_f, cb_rf, zc2, ds_rf, None, False, "ctx_ret_f_bwd")
    cdrq, cdrk, cdrv, dlgb_c, _ = ret_scan_bwd(pc, cosc, sinc, lg_b, cb_rb, zc2, ds_rb, (crq1, crk1, crv1), True, "ctx_ret_b_bwd")
    chq1, cdzf, chv1, dlbf_c, _ = hgrn_scan_bwd(pc, lb_f, cb_hf, zc, ds_hf, None, COL_HFF, False, "ctx_hgrn_f_bwd")
    cdhq, cdzb, cdhv, dlbb_c, _ = hgrn_scan_bwd(pc, lb_b, cb_hb, zc, ds_hb, (chq1, chv1), COL_HFB, True, "ctx_hgrn_b_bwd")
    dpc = _pieces(cdhq, cdzf, cdzb, cdhv, None, cdrq, cdrk, cdrv, None, None, None)
    _, sums_c = dhx_normbwd(dpc, w["w_in"], ctx, zc, nw1, sc1c, "dctx_in_proj")
    grads["w_in"] = matmul_tn_pieces(hxc, dpc, "dw_in_ctx", acc_init=dw_in)
    if place is None:
        dx, sums_x = dhx_normbwd(dp, w["w_in"], x, dx1, nw1, sc1, "dx_in_proj")
    else:
        sums_i = _sibling_sums([grads["w_in"]], ["w_in"], place)
        dx, sums_x, recv_i = dhx_normbwd(dp, w["w_in"], x, dx1, nw1, sc1, "dx_in_proj", to_chips=[sums_i[0][1]])
        names = ["w_in"] + others
        halves = [rs_add_chips(a, r, place[2], "rs_add_chips_" + k)
                  for (a, _), r, k in zip(sums_i + sums_o, [recv_i] + recv_o, names)]
        grads = dict(zip(names, rs_join_halves(halves, "rs_join_halves")))

    def lg_row(f, b):
        return jnp.concatenate([_lane0(f), _lane0(b), jnp.zeros((D - 2 * HEADS,), F32)])

    small = _pack_small([
        sums_x[0], sums_x[1], sums_m[0], sums_fb[1], sums_fb[2], sums_fb[0],
        sums_c[0], sums_c[1],
        sums_x[2], sums_c[2], sums_fb[3], sums_m[1], sums_f[0],
        dlbf_x, dlbf_c, dlbb_x, dlbb_c,
        lg_row(dlgf_x, dlgb_x), lg_row(dlgf_c, dlgb_c),
        sums_f[1],
    ])
    return dx, grads, small


MESH = pl.DeviceIdType.MESH
ANY = pl.BlockSpec(memory_space=pl.ANY)
N_DEV = 8


def _place():
    return lax.axis_index("x"), lax.axis_index("y"), lax.axis_index("c")


def _other_chips(x, y):
    return [(1 - x, y), (x, 1 - y), (1 - x, 1 - y)]


def allgather8(xs, name):
    m, n = xs.shape

    def body(x_ref, out_ref, send_sems, recv_sems, local_sem):
        x, y, c = _place()
        me, sibling = (x, y, c), (x, y, 1 - c)
        chips = _other_chips(x, y)

        def rows(px, py, pc):
            return out_ref.at[pl.ds((4 * px + 2 * py + pc) * m, m), :]

        def copy(k, block, to, src=None):
            return pltpu.make_async_remote_copy(
                src_ref=rows(*block) if src is None else src, dst_ref=rows(*block),
                send_sem=send_sems.at[k], recv_sem=recv_sems.at[k], device_id=to, device_id_type=MESH)

        mine = pltpu.make_async_copy(x_ref, rows(*me), local_sem)
        mine.start()
        first = [copy(0, me, sibling, src=x_ref)]
        first += [copy(1 + j, me, (*chip, c), src=x_ref) for j, chip in enumerate(chips)]
        for cp in first:
            cp.start()
        passed = [copy(4 + j, (*chip, c), sibling) for j, chip in enumerate(chips)]
        for j, chip in enumerate(chips):
            copy(1 + j, (*chip, c), me).wait_recv()
            passed[j].start()
        copy(0, sibling, me).wait_recv()
        for j, chip in enumerate(chips):
            copy(4 + j, (*chip, 1 - c), me).wait_recv()
        for cp in first + passed:
            cp.wait_send()
        mine.wait()

    return pl.pallas_call(
        body, name=name,
        out_shape=jax.ShapeDtypeStruct((N_DEV * m, n), xs.dtype),
        in_specs=[pl.BlockSpec(memory_space=pltpu.VMEM)],
        out_specs=pl.BlockSpec(memory_space=pltpu.VMEM),
        scratch_shapes=[pltpu.SemaphoreType.DMA((7,)), pltpu.SemaphoreType.DMA((7,)), pltpu.SemaphoreType.DMA],
    )(xs)


def _gather_phases(ins, outs, send_sems, recv_sems, local_sems, relations=(0, 1, 2), stage=None):
    n = len(ins)
    x, y, c = _place()
    chips = _other_chips(x, y)

    def rows(i, core):
        h = ins[i].shape[0] // 2
        return pl.ds(pl.multiple_of(core * h, 16), h)

    def region(i, k, rs):
        if len(outs[i].shape) == 2:
            cols = ins[i].shape[1]
            return outs[i].at[rs, pl.ds(pl.multiple_of(k * cols, 128), cols)]
        return outs[i].at[k, rs, :]

    def landed(i, chip, core):
        return region(i, 2 * chip[0] + chip[1], rows(i, core))

    def copy(i, k, src, dst, to):
        return pltpu.make_async_remote_copy(src_ref=src, dst_ref=dst, send_sem=send_sems.at[6 * i + k],
                                            recv_sem=recv_sems.at[6 * i + k], device_id=to, device_id_type=MESH)

    def lift(i):
        return pltpu.make_async_copy(ins[i], stage[i], local_sems.at[i])

    def drop(i):
        return pltpu.make_async_copy(stage[i], region(i, 2 * x + y, pl.ds(0, ins[i].shape[0])), local_sems.at[i])

    def send(i, j):
        return copy(i, j, ins[i].at[rows(i, c), :], landed(i, (x, y), c), (*chips[j], c))

    def arrived(i, j, core, k):
        return copy(i, k, ins[i].at[rows(i, core), :], landed(i, chips[j], core), (x, y, 1 - c))

    def passed(i, j):
        return copy(i, 3 + j, landed(i, chips[j], c), landed(i, chips[j], c), (x, y, 1 - c))

    def start():
        for i in range(n):
            if stage is not None:
                lift(i).start()
            for j in relations:
                send(i, j).start()

    def forward():
        for i in range(n):
            if stage is not None:
                lift(i).wait()
                drop(i).start()
            for j in relations:
                arrived(i, j, c, j).wait_recv()
                passed(i, j).start()

    def finish():
        for i in range(n):
            for j in relations:
                arrived(i, j, 1 - c, 3 + j).wait_recv()
        for i in range(n):
            for j in relations:
                send(i, j).wait_send()
                passed(i, j).wait_send()
            if stage is not None:
                drop(i).wait()

    return start, forward, finish


def _gather_scratch(n):
    return [pltpu.SemaphoreType.DMA((6 * n,)), pltpu.SemaphoreType.DMA((6 * n,)), pltpu.SemaphoreType.DMA((n,))]


def rs_to_sibling(payloads, name):
    n = len(payloads)

    def body(*refs):
        ins, outs = refs[:n], refs[n:2 * n]
        send_sems, recv_sems = refs[2 * n:]
        x, y, c = _place()
        copies = []
        for i in range(n):
            cp = pltpu.make_async_remote_copy(src_ref=ins[i], dst_ref=outs[i], send_sem=send_sems.at[i],
                                              recv_sem=recv_sems.at[i], device_id=(x, y, 1 - c), device_id_type=MESH)
            cp.start()
            copies.append(cp)
        for cp in copies:
            cp.wait()

    return pl.pallas_call(
        body, name=name,
        out_shape=[jax.ShapeDtypeStruct(g.shape, g.dtype) for g in payloads],
        in_specs=[ANY] * n, out_specs=[ANY] * n,
        scratch_shapes=[pltpu.SemaphoreType.DMA((n,)), pltpu.SemaphoreType.DMA((n,))],
    )(*payloads)


def _to_chips_phases(ins, outs, send_sems, recv_sems):
    def copies():
        x, y, c = _place()
        return [pltpu.make_async_remote_copy(
            src_ref=ins[i].at[2 * px + py], dst_ref=outs[i].at[j], send_sem=send_sems.at[3 * i + j],
            recv_sem=recv_sems.at[3 * i + j], device_id=(px, py, c), device_id_type=MESH)
            for i in range(len(ins)) for j, (px, py) in enumerate(_other_chips(x, y))]

    def start():
        for cp in copies():
            cp.start()

    def finish():
        for cp in copies():
            cp.wait()

    return start, finish


def _to_chips_shapes(parts):
    return [jax.ShapeDtypeStruct((3,) + a.shape[1:], a.dtype) for a in parts]


def _to_chips_scratch(n):
    return [pltpu.SemaphoreType.DMA((3 * n,)), pltpu.SemaphoreType.DMA((3 * n,))]


def rs_join_halves(fulls, name):
    n = len(fulls)

    def body(*refs):
        outs = refs[n:2 * n]
        send_sems, recv_sems = refs[2 * n:]
        x, y, c = _place()

        def copy(i, core):
            h = fulls[i].shape[0] // 2
            rows = outs[i].at[pl.ds(pl.multiple_of(core * h, 8), h), :]
            return pltpu.make_async_remote_copy(src_ref=rows, dst_ref=rows, send_sem=send_sems.at[i],
                                                recv_sem=recv_sems.at[i], device_id=(x, y, 1 - c), device_id_type=MESH)

        sent = [copy(i, c) for i in range(n)]
        for cp in sent:
            cp.start()
        for i in range(n):
            copy(i, 1 - c).wait_recv()
        for cp in sent:
            cp.wait_send()

    return pl.pallas_call(
        body, name=name,
        out_shape=[jax.ShapeDtypeStruct(a.shape, a.dtype) for a in fulls],
        in_specs=[ANY] * n, out_specs=[ANY] * n,
        input_output_aliases={i: i for i in range(n)},
        scratch_shapes=[pltpu.SemaphoreType.DMA((n,)), pltpu.SemaphoreType.DMA((n,))],
    )(*fulls)


def _row_tile(rows, cols, limit_bytes=2 * 1024 * 1024, mult=8):
    best = mult
    for t in range(mult, rows + 1, mult):
        if rows % t == 0 and t * cols * 4 <= limit_bytes:
            best = t
    return best


def rs_add_sibling(g, recv, c, name):
    if g.ndim == 2:
        h, C = recv.shape[0], recv.shape[1] // N_SHARD
    else:
        _, h, C = recv.shape
    tr = _row_tile(h, C, mult=16)
    nt = h // tr

    def body(c_ref, g_ref, r_ref, o_ref, o16_ref):
        s = g_ref[...] + r_ref[...].astype(F32)
        o_ref[...] = s
        o16_ref[...] = s.astype(BF16)

    blk = pl.BlockSpec((None, tr, C), lambda k, i, c_ref: (k, i, 0))
    if g.ndim == 2:
        g_spec = pl.BlockSpec((tr, C), lambda k, i, c_ref: (c_ref[0] * nt + i, k))
        r_spec = pl.BlockSpec((tr, C), lambda k, i, c_ref: (i, k))
    else:
        g_spec = pl.BlockSpec((None, tr, C), lambda k, i, c_ref: (k, c_ref[0] * nt + i, 0))
        r_spec = blk
    return pl.pallas_call(
        body, name=name,
        grid_spec=pltpu.PrefetchScalarGridSpec(
            num_scalar_prefetch=1, grid=(N_SHARD, nt),
            in_specs=[g_spec, r_spec],
            out_specs=[blk, blk]),
        out_shape=[jax.ShapeDtypeStruct((N_SHARD, h, C), F32), jax.ShapeDtypeStruct((N_SHARD, h, C), BF16)],
        compiler_params=_params("parallel", "parallel"),
    )(c, g, recv)


def rs_add_chips(part, recv, place, name):
    _, h, C = part.shape
    tr = _row_tile(h, C, mult=16)
    nt = h // tr

    def body(k_ref, p_ref, r_ref, o_ref):
        o_ref[...] = ((p_ref[...] + r_ref[0].astype(F32)) + r_ref[1].astype(F32)) + r_ref[2].astype(F32)

    return pl.pallas_call(
        body, name=name,
        grid_spec=pltpu.PrefetchScalarGridSpec(
            num_scalar_prefetch=1, grid=(nt,),
            in_specs=[pl.BlockSpec((None, tr, C), lambda i, k_ref: (k_ref[0], i, 0)),
                      pl.BlockSpec((3, tr, C), lambda i, k_ref: (0, i, 0))],
            out_specs=pl.BlockSpec((tr, C), lambda i, k_ref: (k_ref[1] * nt + i, 0))),
        out_shape=jax.ShapeDtypeStruct((2 * h, C), F32),
        compiler_params=_params("parallel"),
    )(place, part, recv)


def _adamw_math(w, g, m, v):
    m = ADAM_B1 * m + (1.0 - ADAM_B1) * g
    v = ADAM_B2 * v + (1.0 - ADAM_B2) * (g * g)
    m_hat = m / (1.0 - ADAM_B1 ** ADAM_STEP)
    v_hat = v / (1.0 - ADAM_B2 ** ADAM_STEP)
    delta = -ADAM_LR * (m_hat / (jnp.sqrt(v_hat) + ADAM_EPS) + ADAM_WD * w)
    return delta, m, v


def adamw(w, g, m, v, name):
    R, C = w.shape
    tr = _row_tile(R, C, 1024 * 1024)

    def body(w_ref, g_ref, m_ref, v_ref, d_ref, nm_ref, nv_ref):
        d_ref[...], nm_ref[...], nv_ref[...] = _adamw_math(w_ref[...], g_ref[...], m_ref[...], v_ref[...])

    blk = pl.BlockSpec((tr, C), lambda i: (i, 0))
    return pl.pallas_call(
        body, name=name, grid=(R // tr,), in_specs=[blk] * 4, out_specs=[blk] * 3,
        out_shape=[jax.ShapeDtypeStruct((R, C), F32)] * 3,
        compiler_params=_params("parallel"),
    )(w, g, m, v)


MOD_SH = 6 * D // N_SHARD
PK_ROWS = 16


def mod_fwd(call16, w_sh, b_sh, name):
    def body(c_ref, w_ref, b_ref, o_ref):
        o_ref[...] = _dot(_silu_parts(c_ref[...])[0], w_ref[...], prec=HI) + b_ref[...]

    return pl.pallas_call(body, name=name, out_shape=jax.ShapeDtypeStruct((16, MOD_SH), F32),
                          compiler_params=_params())(call16, w_sh, b_sh)


def prep_small(lbf2, lbb2, theta_row, name):
    def body(f_ref, b_ref, t_ref, lbf_ref, lbb_ref, lg_ref):
        lbf_ref[...] = _sigmoid(f_ref[0:1, :] - f_ref[1:2, :])
        lbb_ref[...] = _sigmoid(b_ref[0:1, :] - b_ref[1:2, :])
        t = t_ref[...]
        lg_ref[...] = jnp.minimum(t, 0.0) - jnp.log(1.0 + jnp.exp(-jnp.abs(t)))

    row = jax.ShapeDtypeStruct((1, D), F32)
    return pl.pallas_call(body, name=name, out_shape=[row, row, row], compiler_params=_params())(lbf2, lbb2, theta_row)


def small_grads(g3, lbf, lbb, theta_row, name):
    def body(g_ref, lbf_ref, lbb_ref, t_ref, pk_ref, aux_ref):
        s = g_ref[0]
        for d in range(1, N_DEV):
            s = s + g_ref[d]
        pk_ref[...] = jnp.zeros_like(pk_ref)
        aux_ref[...] = jnp.zeros_like(aux_ref)
        pk_ref[1:7, :] = s[0:6]
        pk_ref[1:3, :] += s[6:8]
        pk_ref[7:8, :] = s[8:9] + s[9:10]
        pk_ref[8:9, :] = s[10:11]
        lbf, lbb = lbf_ref[...], lbb_ref[...]
        daf = (s[13:14] + s[14:15]) * lbf * (1.0 - lbf)
        dab = (s[15:16] + s[16:17]) * lbb * (1.0 - lbb)
        pk_ref[9:10, :] = daf
        pk_ref[10:11, :] = -daf
        pk_ref[11:12, :] = dab
        pk_ref[12:13, :] = -dab
        pk_ref[13:14, :] = s[11:12]
        pk_ref[14:15, :] = (s[17:18] + s[18:19]) * _sigmoid(-t_ref[...])
        pk_ref[15:16, :] = s[12:13]
        aux_ref[0:2, :] = s[6:8]
        aux_ref[2:3, :] = jnp.broadcast_to(jnp.sum(s[19:20], axis=-1, keepdims=True), (1, D))

    return pl.pallas_call(body, name=name,
                          out_shape=[jax.ShapeDtypeStruct((PK_ROWS, D), F32), jax.ShapeDtypeStruct((8, D), F32)],
                          compiler_params=_params())(g3, lbf, lbb, theta_row)


def mod_bwd(call16, dmod_sh, w_sh, name):
    def body(c_ref, d_ref, w_ref, dw_ref, ds_ref):
        dm = d_ref[...]
        dw_ref[...] = _dot(_silu_parts(c_ref[...])[0], dm, 0, 0, prec=HI)
        ds_ref[...] = jnp.zeros_like(ds_ref)
        ds_ref[0:1, :] = _dot(dm[8:9, :], w_ref[...], 1, 1, prec=HI)

    return pl.pallas_call(body, name=name,
                          out_shape=[jax.ShapeDtypeStruct((D, MOD_SH), F32), jax.ShapeDtypeStruct((8, D), F32)],
                          compiler_params=_params())(call16, dmod_sh, w_sh)


def adamw_small(g4, pk_g, pk_w, pk_m, pk_v, name):
    def body(g4_ref, g_ref, w_ref, m_ref, v_ref, go_ref, d_ref, nm_ref, nv_ref):
        w = w_ref[...]
        ds = ((g4_ref[0:1, :] + g4_ref[16:17, :]) + g4_ref[32:33, :]) + g4_ref[48:49, :]
        row = lax.broadcasted_iota(jnp.int32, (PK_ROWS, D), 0)
        g = jnp.where(row == 0, ds * _silu_parts(w[0:1, :])[1], g_ref[...])
        go_ref[...] = g
        d_ref[...], nm_ref[...], nv_ref[...] = _adamw_math(w, g, m_ref[...], v_ref[...])

    pk = jax.ShapeDtypeStruct((PK_ROWS, D), F32)
    return pl.pallas_call(body, name=name, out_shape=[pk, pk, pk, pk], compiler_params=_params())(g4, pk_g, pk_w, pk_m, pk_v)


def _pack_params(c_ctx, b_mod, n1, n2, lbf, lbb, hgn, th_f, th_b, fin):
    theta = jnp.concatenate([th_f.reshape(HEADS), th_b.reshape(HEADS), jnp.zeros((D - 2 * HEADS,), F32)])
    return jnp.concatenate([c_ctx.reshape(1, D), b_mod.reshape(6, D), n1.reshape(1, D), n2.reshape(1, D), lbf, lbb,
                            hgn.reshape(1, D), theta.reshape(1, D), fin.reshape(1, D)], axis=0)


def _unpack_params(pk):
    return (pk[0], pk[1:7].reshape(1, 6 * D), pk[7:8], pk[8:9], pk[9:11], pk[11:13], pk[13:14],
            pk[14, 0:HEADS].reshape(1, HEADS), pk[14, HEADS:2 * HEADS].reshape(1, HEADS), pk[15])


def kernel(x, c, ctx, c_ctx, w_mod, b_mod, norm1_w, norm2_w, w_in, hg_lb_fwd, hg_lb_bwd, hg_norm_w, rt_theta_fwd, rt_theta_bwd, w_proj_hgrn, w_proj_ret, w_out, w_ffn_gate, w_ffn_up, w_ffn_down, final_norm_w, loss_target, m_c_ctx, m_w_mod, m_b_mod, m_norm1_w, m_norm2_w, m_w_in, m_hg_lb_fwd, m_hg_lb_bwd, m_hg_norm_w, m_rt_theta_fwd, m_rt_theta_bwd, m_w_proj_hgrn, m_w_proj_ret, m_w_out, m_w_ffn_gate, m_w_ffn_up, m_w_ffn_down, m_final_norm_w, v_c_ctx, v_w_mod, v_b_mod, v_norm1_w, v_norm2_w, v_w_in, v_hg_lb_fwd, v_hg_lb_bwd, v_hg_norm_w, v_rt_theta_fwd, v_rt_theta_bwd, v_w_proj_hgrn, v_w_proj_ret, v_w_out, v_w_ffn_gate, v_w_ffn_up, v_w_ffn_down, v_final_norm_w):
    xi, yi, ci = _place()
    dev = 4 * xi + 2 * yi + ci
    chip = 2 * xi + yi
    core_arg = jnp.reshape(ci, (1,)).astype(jnp.int32)
    place_arg = jnp.stack([chip, ci]).astype(jnp.int32)

    c_all = allgather8(jnp.concatenate([c, jnp.zeros((7, D), F32)], axis=0), "gather_c").reshape(N_DEV, 8, D)[:, 0]
    call16 = jnp.concatenate([c_all, c_ctx.reshape(1, D), jnp.zeros((7, D), F32)], axis=0)
    b_sh = lax.dynamic_slice_in_dim(b_mod, chip * MOD_SH, MOD_SH, axis=1)
    mod_sh = mod_fwd(call16, w_mod[0], b_sh, "mod_fwd")
    mod_g = allgather8(mod_sh, "gather_mod").reshape(N_DEV, 16, MOD_SH)
    mod_all = jnp.concatenate([mod_g[0], mod_g[2], mod_g[4], mod_g[6]], axis=1)
    mod_x = lax.dynamic_index_in_dim(mod_all, dev, axis=0, keepdims=False).reshape(6, D)
    mod_c = mod_all[8].reshape(6, D)

    pk_w = _pack_params(c_ctx, b_mod, norm1_w, norm2_w, hg_lb_fwd, hg_lb_bwd, hg_norm_w, rt_theta_fwd, rt_theta_bwd, final_norm_w)
    theta_row = pk_w[14:15]
    lb_f, lb_b, lg_row = prep_small(hg_lb_fwd, hg_lb_bwd, theta_row, "prep_small")
    lg_f = jnp.broadcast_to(lg_row[0, 0:HEADS].reshape(HEADS, 1, 1), (HEADS, 1, RT_DV))
    lg_b = jnp.broadcast_to(lg_row[0, HEADS:2 * HEADS].reshape(HEADS, 1, 1), (HEADS, 1, RT_DV))

    rest = [s[0].astype(BF16) for s in (w_proj_hgrn, w_proj_ret, w_out, w_ffn_gate, w_ffn_up, w_ffn_down)]

    dx, full, small = local_step(x[0], ctx[0], loss_target[0], mod_x, mod_c, lb_f, lb_b, lg_f, lg_b,
                                 norm1_w, norm2_w, hg_norm_w, final_norm_w.reshape(1, D),
                                 {"w_in_shard": w_in[0].astype(BF16)}, rest, (ci, core_arg, place_arg))

    g3 = allgather8(small, "gather_small").reshape(N_DEV, SMALL_ROWS, D)
    pk_g, aux = small_grads(g3, lb_f, lb_b, theta_row, "small_grads")
    loss = aux[2, 0]
    dmod16 = jnp.concatenate([
        g3[:, 0:6, :].reshape(N_DEV, 6 * D),
        jnp.concatenate([aux[0], aux[1], jnp.zeros((4 * D,), F32)]).reshape(1, 6 * D),
        jnp.zeros((7, 6 * D), F32)], axis=0)
    dmod_sh = lax.dynamic_slice_in_dim(dmod16, chip * MOD_SH, MOD_SH, axis=1)
    g_wmod, dsilu = mod_bwd(call16, dmod_sh, w_mod[0], "mod_bwd")
    g4 = allgather8(dsilu, "gather_dsilu")
    pk_m = _pack_params(m_c_ctx, m_b_mod, m_norm1_w, m_norm2_w, m_hg_lb_fwd, m_hg_lb_bwd, m_hg_norm_w, m_rt_theta_fwd, m_rt_theta_bwd, m_final_norm_w)
    pk_v = _pack_params(v_c_ctx, v_b_mod, v_norm1_w, v_norm2_w, v_hg_lb_fwd, v_hg_lb_bwd, v_hg_norm_w, v_rt_theta_fwd, v_rt_theta_bwd, v_final_norm_w)
    pk_g, pk_d, pk_nm, pk_nv = adamw_small(g4, pk_g, pk_w, pk_m, pk_v, "adamw_small")

    big = {
        "w_mod": (g_wmod, w_mod, m_w_mod, v_w_mod),
        "w_in": (full["w_in"], w_in, m_w_in, v_w_in),
        "w_pa": (full["w_pa"], w_proj_hgrn, m_w_proj_hgrn, v_w_proj_hgrn),
        "w_pb": (full["w_pb"], w_proj_ret, m_w_proj_ret, v_w_proj_ret),
        "w_out": (full["w_out"], w_out, m_w_out, v_w_out),
        "wg": (full["wg"], w_ffn_gate, m_w_ffn_gate, v_w_ffn_gate),
        "wu": (full["wu"], w_ffn_up, m_w_ffn_up, v_w_ffn_up),
        "wd": (full["wd"], w_ffn_down, m_w_ffn_down, v_w_ffn_down),
    }
    res = {}
    for k, (g, wt, mt, vt) in big.items():
        d, nm, nv = adamw(wt[0], g, mt[0], vt[0], "adamw_" + k)
        res[k] = (g[None], d[None], nm[None], nv[None])

    sm = [_unpack_params(p) for p in (pk_g, pk_d, pk_nm, pk_nv)]
    outs = []
    for t in range(4):
        (s_cctx, s_bmod, s_n1, s_n2, s_lbf, s_lbb, s_hgn, s_thf, s_thb, s_fin) = sm[t]
        outs.append([s_cctx, res["w_mod"][t], s_bmod, s_n1, s_n2, res["w_in"][t], s_lbf, s_lbb, s_hgn, s_thf, s_thb,
                     res["w_pa"][t], res["w_pb"][t], res["w_out"][t], res["wg"][t], res["wu"][t], res["wd"][t], s_fin])
    return (loss, dx[None], *outs[0], *outs[1], *outs[2], *outs[3])
```

```python
import functools

import jax
import jax.numpy as jnp
from jax import lax
from jax.experimental import pallas as pl
from jax.experimental.pallas import tpu as pltpu

F32 = jnp.float32
BF16 = jnp.bfloat16
HI = lax.Precision.HIGHEST

D = 1024
HEADS = 8
HG_D = 128
RT_DK = 128
RT_DV = 256
D_FF = 2816
D_IN = 13312
N_SHARD = 4
IN_SH = D_IN // N_SHARD
FF_SH = D_FF // N_SHARD
HG_CHUNK = 32
SCAN_ROWS = 256
HG_GROUP = 8
RT_GROUP = 4
PROJ_ROWS = 1024
EPS = 1e-6
GN_EPS = 1e-5
Q_SCALE = 128.0 ** -0.5
VMEM_LIMIT = 56 * 1024 * 1024

COL_HQ, COL_HFF, COL_HFB, COL_HI, COL_HG = 0, 8, 16, 24, 32
COL_RQ, COL_RK, COL_RV, COL_RG, COL_GA, COL_GB = 40, 48, 56, 72, 88, 96

ADAM_LR, ADAM_B1, ADAM_B2, ADAM_EPS, ADAM_WD, ADAM_STEP = 0.001, 0.9, 0.999, 1e-08, 0.01, 10


def _params(*sem):
    return pltpu.CompilerParams(dimension_semantics=sem, vmem_limit_bytes=VMEM_LIMIT)


def _dot(a, b, ca=1, cb=0, prec=None):
    return lax.dot_general(a, b, (((ca,), (cb,)), ((), ())), precision=prec, preferred_element_type=F32)


def _bdot(a, b, ca=1, cb=0):
    return _dot(a.astype(BF16), b.astype(BF16), ca, cb)


def _sigmoid(z):
    return 1.0 / (1.0 + jnp.exp(-z))


def _rowsum(a):
    return jnp.sum(a, axis=0, keepdims=True)


def _lanemean(a):
    return jnp.mean(a, axis=-1, keepdims=True)


def _grid_step(grid):
    pos, total = 0, 1
    for d, size in enumerate(grid):
        pos = pos * size + pl.program_id(d)
        total *= size
    return pos, total


def normmod_matmul(x, nw, sh, sc, w, name):
    L = x.shape[0]
    tm = min(PROJ_ROWS, L)
    tn = IN_SH // 2

    def body(x_ref, nw_ref, sh_ref, sc_ref, w_ref, p_ref, hx_ref, hx_scr):
        @pl.when(pl.program_id(1) == 0)
        def _():
            xv = x_ref[...]
            n = xv * lax.rsqrt(_lanemean(xv * xv) + EPS) * nw_ref[...]
            h = (n * (1.0 + sc_ref[...]) + sh_ref[...]).astype(BF16)
            hx_scr[...] = h
            hx_ref[...] = h

        p_ref[...] = _dot(hx_scr[...], w_ref[...])

    vec = pl.BlockSpec((1, D), lambda i, j: (0, 0))
    return pl.pallas_call(
        body, name=name,
        grid=(L // tm, D_IN // tn),
        in_specs=[pl.BlockSpec((tm, D), lambda i, j: (i, 0)), vec, vec, vec,
                  pl.BlockSpec((D, tn), lambda i, j: (0, j))],
        out_specs=[pl.BlockSpec((tm, tn), lambda i, j: (i, j)), pl.BlockSpec((tm, D), lambda i, j: (i, 0))],
        out_shape=[jax.ShapeDtypeStruct((L, D_IN), F32), jax.ShapeDtypeStruct((L, D), BF16)],
        scratch_shapes=[pltpu.VMEM((tm, D), BF16)],
        compiler_params=_params("parallel", "arbitrary"),
    )(x, nw, sh, sc, w)


def _w_halves(w_src, col0, tn, wbuf, wsems, pos):
    @pl.when(pos == 0)
    def _():
        for h in range(2):
            pltpu.make_async_copy(w_src.at[:, pl.ds(pl.multiple_of(col0 + h * tn, 128), tn)], wbuf.at[h],
                                  wsems.at[h]).start()

    for h in range(2):
        @pl.when(pos == h)
        def _(h=h):
            pltpu.make_async_copy(w_src.at[:, pl.ds(0, tn)], wbuf.at[h], wsems.at[h]).wait()


def in_proj_own(x, nw, sh, sc, w_shard, shard_arg, name):
    L = x.shape[0]
    tm = min(PROJ_ROWS, L)
    tn = IN_SH // 2
    grid = (L // tm, 2)

    def body(k_ref, x_ref, nw_ref, sh_ref, sc_ref, w_ref, p_ref, hx_ref, wfull_ref, hx_scr, wbuf, wsems, psems,
             *sems):
        pos, total = _grid_step(grid)
        start, forward, finish = _gather_phases([w_ref], [wfull_ref], *sems, relations=(0, 1))
        pl.when(pos == 0)(start)
        _w_halves(w_ref, 0, tn, wbuf, wsems, pos)

        def place(h):
            col = pl.multiple_of(k_ref[0] * IN_SH + h * tn, 128)
            return pltpu.make_async_copy(wbuf.at[h], wfull_ref.at[:, pl.ds(col, tn)], psems.at[h])

        for h in range(2):
            @pl.when(pos == h)
            def _(h=h):
                place(h).start()

        @pl.when(pl.program_id(1) == 0)
        def _():
            xv = x_ref[...]
            n = xv * lax.rsqrt(_lanemean(xv * xv) + EPS) * nw_ref[...]
            h = (n * (1.0 + sc_ref[...]) + sh_ref[...]).astype(BF16)
            hx_scr[...] = h
            hx_ref[...] = h

        p_ref[...] = _dot(hx_scr[...], wbuf[pl.program_id(1)])

        @pl.when(pos == total - 1)
        def _():
            forward()
            finish()
            place(0).wait()
            place(1).wait()

    vec = pl.BlockSpec((1, D), lambda i, j, k: (0, 0))
    return pl.pallas_call(
        body, name=name,
        grid_spec=pltpu.PrefetchScalarGridSpec(
            num_scalar_prefetch=1, grid=grid,
            in_specs=[pl.BlockSpec((tm, D), lambda i, j, k: (i, 0)), vec, vec, vec, ANY],
            out_specs=[pl.BlockSpec((tm, tn), lambda i, j, k: (i, 2 * k[0] + j)),
                       pl.BlockSpec((tm, D), lambda i, j, k: (i, 0)), ANY],
            scratch_shapes=[pltpu.VMEM((tm, D), BF16), pltpu.VMEM((2, D, tn), BF16), pltpu.SemaphoreType.DMA((2,)),
                            pltpu.SemaphoreType.DMA((2,))] + _gather_scratch(1)),
        out_shape=[jax.ShapeDtypeStruct((L, D_IN), F32), jax.ShapeDtypeStruct((L, D), BF16),
                   jax.ShapeDtypeStruct((D, D_IN), BF16)],
        compiler_params=_params("arbitrary", "arbitrary"),
    )(shard_arg, x, nw, sh, sc, w_shard)


def in_proj_next(hx, w_full, shard_arg, p, name, diag_from=None, gather=()):
    L = hx.shape[0]
    tm = min(PROJ_ROWS, L)
    tn = IN_SH // 2
    grid = (L // tm, 2)
    diag = diag_from is not None
    ng = len(gather)
    assert not (diag and ng)

    def body(k_ref, hx_ref, wf_in, p_in, *refs):
        n_src = 1 if diag else ng
        srcs = refs[:n_src]
        p_ref = refs[n_src]
        dsts = refs[n_src + 1:2 * n_src + 1]
        wbuf, wsems = refs[2 * n_src + 1:2 * n_src + 3]
        sems = refs[2 * n_src + 3:2 * n_src + 6]
        stage = refs[2 * n_src + 6:]
        pos, total = _grid_step(grid)
        w_src = dsts[0] if diag else wf_in
        if diag:
            start, forward, finish = _gather_phases(srcs, dsts, *sems, relations=(2,))
        elif ng:
            start, forward, finish = _gather_phases(srcs, dsts, *sems, stage=stage)
        if n_src:
            pl.when(pos == 0)(start)
        _w_halves(w_src, k_ref[0] * IN_SH, tn, wbuf, wsems, pos)
        p_ref[...] = _dot(hx_ref[...], wbuf[pl.program_id(1)])
        if n_src:
            @pl.when(pos == total - 1)
            def _():
                forward()
                finish()

    srcs = [diag_from] if diag else list(gather)
    out_shape = [jax.ShapeDtypeStruct((L, D_IN), F32)]
    if diag:
        out_shape.append(jax.ShapeDtypeStruct(w_full.shape, w_full.dtype))
    out_shape += [jax.ShapeDtypeStruct((N_SHARD,) + s.shape, s.dtype) for s in gather]
    aliases = {3: 0, 2: 1} if diag else {3: 0}
    return pl.pallas_call(
        body, name=name,
        grid_spec=pltpu.PrefetchScalarGridSpec(
            num_scalar_prefetch=1, grid=grid,
            in_specs=[pl.BlockSpec((tm, D), lambda i, j, k: (i, 0)), ANY, ANY] + [ANY] * len(srcs),
            out_specs=[pl.BlockSpec((tm, tn), lambda i, j, k: (i, 2 * k[0] + j))] + [ANY] * len(srcs),
            scratch_shapes=[pltpu.VMEM((2, D, tn), BF16), pltpu.SemaphoreType.DMA((2,))]
            + (_gather_scratch(len(srcs)) if srcs else []) + [pltpu.VMEM(s.shape, s.dtype) for s in gather]),
        out_shape=out_shape,
        input_output_aliases=aliases,
        compiler_params=_params("arbitrary", "arbitrary"),
    )(shard_arg, hx, w_full, p, *srcs)


def _hgrn_gates(z, lb):
    sg = _sigmoid(z)
    sgn = _sigmoid(-z)
    f = lb + (1.0 - lb) * sg
    k = (1.0 - lb) * sgn
    return sg, sgn, f, k


def _tri_chunks(n, chunk, reverse):
    r = lax.broadcasted_iota(jnp.int32, (n, n), 0)
    c = lax.broadcasted_iota(jnp.int32, (n, n), 1)
    same = (r // chunk) == (c // chunk)
    return jnp.where(same & ((r <= c) if reverse else (r >= c)), 1.0, 0.0).astype(F32)


def _decay3(b, reverse):
    C = b.shape[0]
    t = lax.broadcasted_iota(jnp.int32, (C, C, 1), 0)
    s = lax.broadcasted_iota(jnp.int32, (C, C, 1), 1)
    mask = (t <= s) if reverse else (t >= s)
    return jnp.exp(jnp.where(mask, b[:, None, :] - b[None, :, :], -jnp.inf))


HG_SUB = 8


def _hgrn_pairs(reverse):
    pairs = []
    size = HG_SUB
    while size < HG_CHUNK:
        for lo in range(0, HG_CHUNK, 2 * size):
            first, second = slice(lo, lo + size), slice(lo + size, lo + 2 * size)
            if reverse:
                pairs.append((first, second, lo + size))
            else:
                pairs.append((second, first, lo + size - 1))
        size *= 2
    return pairs


def _head_mask(g, nq, nk):
    r = lax.broadcasted_iota(jnp.int32, (g * nq, g * nk), 0) // nq
    c = lax.broadcasted_iota(jnp.int32, (g * nq, g * nk), 1) // nk
    return jnp.where(r == c, 1.0, 0.0).astype(F32)


def _hgrn_masks(g, reverse):
    return [_head_mask(g, qr.stop - qr.start, kr.stop - kr.start) for qr, kr, _ in _hgrn_pairs(reverse)]


def _stack(xs):
    return jnp.concatenate(xs, axis=0)


def _unstack(x, g):
    n = x.shape[0] // g
    return [x[h * n:(h + 1) * n] for h in range(g)]


def _add_blocks(acc, rows, part):
    for i in range(part.shape[0] // HG_SUB):
        acc[rows.start // HG_SUB + i] += part[i * HG_SUB:(i + 1) * HG_SUB]


def _hgrn_intra_fwd(qs, ks, vs, bs, masks, reverse):
    g = len(qs)
    blocks = []
    for q, k, v, b in zip(qs, ks, vs, bs):
        mine = []
        for lo in range(0, HG_CHUNK, HG_SUB):
            r = slice(lo, lo + HG_SUB)
            att3 = jnp.sum(q[r][:, None, :] * k[r][None, :, :] * _decay3(b[r], reverse), axis=-1, keepdims=True)
            mine.append(jnp.sum(att3 * v[r][None, :, :], axis=1))
        blocks.append(mine)
    for (qr, kr, ref), mask in zip(_hgrn_pairs(reverse), masks):
        qt = _stack([q[qr] * jnp.exp(b[qr] - b[ref:ref + 1]) for q, b in zip(qs, bs)])
        kt = _stack([k[kr] * jnp.exp(b[ref:ref + 1] - b[kr]) for k, b in zip(ks, bs)])
        att = _bdot(qt, kt, 1, 1) * mask
        for mine, part in zip(blocks, _unstack(_bdot(att, _stack([v[kr] for v in vs])), g)):
            _add_blocks(mine, qr, part)
    return [jnp.concatenate(mine, axis=0) for mine in blocks]


def _hgrn_intra_bwd(qs, ks, vs, bs, d_os, masks, reverse):
    g = len(qs)
    nb = HG_CHUNK // HG_SUB
    dqs, dks, dvs = [], [], []
    for q, k, v, b, d_o in zip(qs, ks, vs, bs, d_os):
        dq, dk, dv = [None] * nb, [None] * nb, [None] * nb
        for i in range(nb):
            r = slice(i * HG_SUB, (i + 1) * HG_SUB)
            e3 = _decay3(b[r], reverse)
            p3 = jnp.sum(d_o[r][:, None, :] * v[r][None, :, :], axis=-1, keepdims=True) * e3
            dq[i] = jnp.sum(p3 * k[r][None, :, :], axis=1)
            dk[i] = jnp.sum(p3 * q[r][:, None, :], axis=0)
            att3 = jnp.sum(q[r][:, None, :] * k[r][None, :, :] * e3, axis=-1, keepdims=True)
            dv[i] = jnp.sum(att3 * d_o[r][:, None, :], axis=0)
        dqs.append(dq)
        dks.append(dk)
        dvs.append(dv)
    for (qr, kr, ref), mask in zip(_hgrn_pairs(reverse), masks):
        fqs = [jnp.exp(b[qr] - b[ref:ref + 1]) for b in bs]
        fks = [jnp.exp(b[ref:ref + 1] - b[kr]) for b in bs]
        qt = _stack([q[qr] * f for q, f in zip(qs, fqs)])
        kt = _stack([k[kr] * f for k, f in zip(ks, fks)])
        do_q = _stack([d_o[qr] for d_o in d_os])
        att = _bdot(qt, kt, 1, 1) * mask
        datt = _bdot(do_q, _stack([v[kr] for v in vs]), 1, 1) * mask
        for dq, part, f in zip(dqs, _unstack(_bdot(datt, kt), g), fqs):
            _add_blocks(dq, qr, part * f)
        for dk, part, f in zip(dks, _unstack(_bdot(datt, qt, 0, 0), g), fks):
            _add_blocks(dk, kr, part * f)
        for dv, part in zip(dvs, _unstack(_bdot(att, do_q, 0, 0), g)):
            _add_blocks(dv, kr, part)

    def cat(parts):
        return [jnp.concatenate(p, axis=0) for p in parts]

    return cat(dqs), cat(dks), cat(dvs)


def _hgrn_state_step(k, v, b, s_t, last):
    b_last = b[last:last + 1]
    return s_t * jnp.exp(b_last) + _bdot(v, k * jnp.exp(b_last - b), 0, 0)


def hgrn_scan_fwd(p, lb, s0, col_z, reverse, name):
    L = p.shape[0]
    nB = L // SCAN_ROWS
    nC = SCAN_ROWS // HG_CHUNK
    C = HG_CHUNK
    G, W = HG_GROUP, HG_GROUP * HG_D
    last = 0 if reverse else C - 1

    def bmap(b):
        return (nB - 1 - b) if reverse else b

    def body(q_ref, z_ref, v_ref, lb_ref, s0_ref, o_ref, sfin_ref, sblk_ref, s_scr, k_scr, b_scr):
        blk = pl.program_id(1)

        @pl.when(blk == 0)
        def _():
            s_scr[...] = s0_ref[...]

        sblk_ref[...] = s_scr[...]
        _, _, f_all, k_all = _hgrn_gates(z_ref[...], lb_ref[...])
        k_scr[...] = k_all
        b_scr[...] = _dot(_tri_chunks(SCAN_ROWS, C, reverse), jnp.log(f_all), prec=HI)

        masks = _hgrn_masks(G, reverse)
        heads = [slice(j * HG_D, (j + 1) * HG_D) for j in range(G)]

        def chunk(ci, carry):
            c = (nC - 1 - ci) if reverse else ci
            rows = pl.ds(pl.multiple_of(c * C, C), C)
            qs = [q_ref[rows, lanes] * Q_SCALE for lanes in heads]
            vs = [v_ref[rows, lanes] for lanes in heads]
            ks = [k_scr[rows, lanes] for lanes in heads]
            bs = [b_scr[rows, lanes] for lanes in heads]
            o_in = _hgrn_intra_fwd(qs, ks, vs, bs, masks, reverse)
            for j, lanes in enumerate(heads):
                s_t = s_scr[j]
                o_ref[rows, lanes] = o_in[j] + _bdot(qs[j] * jnp.exp(bs[j]), s_t, 1, 1)
                s_scr[j] = _hgrn_state_step(ks[j], vs[j], bs[j], s_t, last)
            return carry

        lax.fori_loop(0, nC, chunk, 0)

        @pl.when(blk == nB - 1)
        def _():
            sfin_ref[...] = s_scr[...]

    def col(c0):
        return pl.BlockSpec((SCAN_ROWS, W), lambda h, b: (bmap(b), c0 // G + h))

    state = pl.BlockSpec((G, HG_D, HG_D), lambda h, b: (h, 0, 0))
    return pl.pallas_call(
        body, name=name,
        grid=(HEADS // G, nB),
        in_specs=[col(COL_HQ), col(col_z), col(COL_HI), pl.BlockSpec((1, W), lambda h, b: (0, h)), state],
        out_specs=[pl.BlockSpec((SCAN_ROWS, W), lambda h, b: (bmap(b), h)), state,
                   pl.BlockSpec((None, G, HG_D, HG_D), lambda h, b: (bmap(b), h, 0, 0))],
        out_shape=[jax.ShapeDtypeStruct((L, D), F32),
                   jax.ShapeDtypeStruct((HEADS, HG_D, HG_D), F32),
                   jax.ShapeDtypeStruct((nB, HEADS, HG_D, HG_D), F32)],
        scratch_shapes=[pltpu.VMEM((G, HG_D, HG_D), F32), pltpu.VMEM((SCAN_ROWS, W), F32),
                        pltpu.VMEM((SCAN_ROWS, W), F32)],
        compiler_params=_params("parallel", "arbitrary"),
    )(p, p, p, lb, s0)


def hgrn_scan_bwd(p, lb, s_blocks, d_o, ds_fin, prev, col_z, reverse, name):
    L = p.shape[0]
    nB = L // SCAN_ROWS
    nC = SCAN_ROWS // HG_CHUNK
    C = HG_CHUNK
    G, W = HG_GROUP, HG_GROUP * HG_D
    last = 0 if reverse else C - 1
    has_prev = prev is not None
    out_dt = BF16 if has_prev else F32

    def bmap(b):
        return b if reverse else (nB - 1 - b)

    def body(*refs):
        q_ref, z_ref, v_ref, lb_ref, sblk_ref, do_ref, dsf_ref = refs[:7]
        refs = refs[7:]
        if has_prev:
            pq_ref, pv_ref = refs[:2]
            refs = refs[2:]
        dq_ref, dz_ref, dv_ref, dlb_ref, ds0_ref, st_scr, run_scr, ds_scr, k_scr, b_scr, db_scr, dk_scr = refs
        blk = pl.program_id(1)

        @pl.when(blk == 0)
        def _():
            ds_scr[...] = dsf_ref[...]
            dlb_ref[...] = jnp.zeros_like(dlb_ref)

        tri = _tri_chunks(SCAN_ROWS, C, reverse)
        row = lax.broadcasted_iota(jnp.int32, (C, HG_D), 0)
        _, _, f_all, k_all = _hgrn_gates(z_ref[...], lb_ref[...])
        k_scr[...] = k_all
        b_scr[...] = _dot(tri, jnp.log(f_all), prec=HI)
        run_scr[...] = sblk_ref[...]

        def recompute(ci, carry):
            c = (nC - 1 - ci) if reverse else ci
            rows = pl.ds(pl.multiple_of(c * C, C), C)
            for j in range(G):
                lanes = slice(j * HG_D, (j + 1) * HG_D)
                s_t = run_scr[j]
                st_scr[c, j] = s_t
                run_scr[j] = _hgrn_state_step(k_scr[rows, lanes], v_ref[rows, lanes], b_scr[rows, lanes], s_t, last)
            return carry

        lax.fori_loop(0, nC, recompute, 0)

        masks = _hgrn_masks(G, reverse)
        heads = [slice(j * HG_D, (j + 1) * HG_D) for j in range(G)]

        def chunk(ci, carry):
            c = ci if reverse else (nC - 1 - ci)
            rows = pl.ds(pl.multiple_of(c * C, C), C)
            ks = [k_scr[rows, lanes] for lanes in heads]
            bs = [b_scr[rows, lanes] for lanes in heads]
            qs = [q_ref[rows, lanes] * Q_SCALE for lanes in heads]
            vs = [v_ref[rows, lanes] for lanes in heads]
            d_os = [do_ref[rows, lanes] for lanes in heads]
            dq_ins, dk_ins, dv_ins = _hgrn_intra_bwd(qs, ks, vs, bs, d_os, masks, reverse)
            for j, lanes in enumerate(heads):
                k, b, q, v, d_o = ks[j], bs[j], qs[j], vs[j], d_os[j]
                s_t = st_scr[c, j]
                ds_t = ds_scr[j]
                eb = jnp.exp(b)
                b_last = b[last:last + 1]
                eb_last = jnp.exp(b_last)
                kdec = jnp.exp(b_last - b)
                qe = q * eb
                ke = k * kdec
                dq_tot = _bdot(d_o, s_t, 1, 0) * eb + dq_ins[j]
                dke = _bdot(v, ds_t, 1, 0)
                dk_tot = dke * kdec + dk_ins[j]
                dv = dv_ins[j] + _bdot(ke, ds_t, 1, 1)
                db_last = _rowsum(dke * ke) + eb_last * _rowsum(ds_t * s_t)
                db_scr[rows, lanes] = q * dq_tot - k * dk_tot + jnp.where(row == last, db_last, 0.0)
                dk_scr[rows, lanes] = dk_tot
                dq = dq_tot * Q_SCALE
                if has_prev:
                    dq = dq + pq_ref[rows, lanes]
                    dv = dv + pv_ref[rows, lanes]
                dq_ref[rows, lanes] = dq.astype(out_dt)
                dv_ref[rows, lanes] = dv.astype(out_dt)
                ds_scr[j] = ds_t * eb_last + _bdot(d_o, qe, 0, 0)
            return carry

        lax.fori_loop(0, nC, chunk, 0)

        lb = lb_ref[...]
        sg, sgn, f, _ = _hgrn_gates(z_ref[...], lb)
        g = _dot(tri, db_scr[...], 0, 0, prec=HI) / f - dk_scr[...]
        dz_ref[...] = (g * (1.0 - lb) * sg * sgn).astype(BF16)
        dlb_ref[...] += _rowsum(g * sgn)

        @pl.when(blk == nB - 1)
        def _():
            ds0_ref[...] = ds_scr[...]

    def col(c0):
        return pl.BlockSpec((SCAN_ROWS, W), lambda h, b: (bmap(b), c0 // G + h))

    tile = pl.BlockSpec((SCAN_ROWS, W), lambda h, b: (bmap(b), h))
    state = pl.BlockSpec((G, HG_D, HG_D), lambda h, b: (h, 0, 0))
    in_specs = [col(COL_HQ), col(col_z), col(COL_HI),
                pl.BlockSpec((1, W), lambda h, b: (0, h)),
                pl.BlockSpec((None, G, HG_D, HG_D), lambda h, b: (bmap(b), h, 0, 0)),
                tile, state]
    args = [p, p, p, lb, s_blocks, d_o, ds_fin]
    if has_prev:
        in_specs += [tile, tile]
        args += list(prev)
    return pl.pallas_call(
        body, name=name,
        grid=(HEADS // G, nB),
        in_specs=in_specs,
        out_specs=[tile, tile, tile, pl.BlockSpec((1, W), lambda h, b: (0, h)), state],
        out_shape=[jax.ShapeDtypeStruct((L, D), out_dt), jax.ShapeDtypeStruct((L, D), BF16),
                   jax.ShapeDtypeStruct((L, D), out_dt), jax.ShapeDtypeStruct((1, D), F32),
                   jax.ShapeDtypeStruct((HEADS, HG_D, HG_D), F32)],
        scratch_shapes=[pltpu.VMEM((nC, G, HG_D, HG_D), F32), pltpu.VMEM((G, HG_D, HG_D), F32),
                        pltpu.VMEM((G, HG_D, HG_D), F32)] + [pltpu.VMEM((SCAN_ROWS, W), F32)] * 4,
        compiler_params=_params("parallel", "arbitrary"),
    )(*args)


def _rope(t, cosf, sinf):
    return t * cosf + pltpu.roll(t, RT_DK // 2, 1) * sinf


def _rope_t(d, cosf, sinf):
    return d * cosf + pltpu.roll(d * sinf, RT_DK // 2, 1)


def _ret_decays(lg, reverse):
    C = SCAN_ROWS
    t = lax.broadcasted_iota(jnp.int32, (C, C), 0)
    s = lax.broadcasted_iota(jnp.int32, (C, C), 1)
    delta = ((s - t) if reverse else (t - s)).astype(F32)
    dmat = jnp.where(delta >= 0, jnp.exp(lg * jnp.maximum(delta, 0.0)), 0.0)
    r = lax.broadcasted_iota(jnp.int32, (C, RT_DK), 0)
    pos = ((C - 1 - r) if reverse else r).astype(F32)
    lg1 = lg[:, :RT_DK]
    qdec = jnp.exp(lg1 * (pos + 1.0))
    kdec = jnp.exp(lg1 * (C - 1.0 - pos))
    sdec = jnp.exp(lg1 * float(C))
    return dmat, delta, pos, qdec, kdec, sdec


def ret_scan_fwd(p, cosf, sinf, lg, s0, reverse, name):
    L = p.shape[0]
    C = SCAN_ROWS
    nB = L // C

    def bmap(b):
        return (nB - 1 - b) if reverse else b

    G = RT_GROUP

    def body(q_ref, k_ref, v_ref, cos_ref, sin_ref, lg_ref, s0_ref, o_ref, sfin_ref, sblk_ref, s_scr):
        blk = pl.program_id(1)

        @pl.when(blk == 0)
        def _():
            s_scr[...] = s0_ref[...]

        sblk_ref[...] = s_scr[...]
        cosf, sinf = cos_ref[...], sin_ref[...]
        for j in range(G):
            lk, lv = slice(j * RT_DK, (j + 1) * RT_DK), slice(j * RT_DV, (j + 1) * RT_DV)
            s_t = s_scr[j]
            dmat, _, _, qdec, kdec, sdec = _ret_decays(lg_ref[j], reverse)
            q = _rope(q_ref[:, lk] * Q_SCALE, cosf, sinf)
            k = _rope(k_ref[:, lk], cosf, sinf)
            v = v_ref[:, lv]
            att = _bdot(q, k, 1, 1) * dmat
            o_ref[:, lv] = _bdot(att, v) + _bdot(q * qdec, s_t, 1, 1)
            s_scr[j] = s_t * sdec + _bdot(v, k * kdec, 0, 0)

        @pl.when(blk == nB - 1)
        def _():
            sfin_ref[...] = s_scr[...]

    def col(c0):
        return pl.BlockSpec((C, G * RT_DK), lambda h, b: (bmap(b), c0 // G + h))

    tab = pl.BlockSpec((C, RT_DK), lambda h, b: (bmap(b), 0))
    state = pl.BlockSpec((G, RT_DV, RT_DK), lambda h, b: (h, 0, 0))
    return pl.pallas_call(
        body, name=name,
        grid=(HEADS // G, nB),
        in_specs=[col(COL_RQ), col(COL_RK),
                  pl.BlockSpec((C, G * RT_DV), lambda h, b: (bmap(b), COL_RV // (2 * G) + h)),
                  tab, tab, pl.BlockSpec((G, 1, RT_DV), lambda h, b: (h, 0, 0)), state],
        out_specs=[pl.BlockSpec((C, G * RT_DV), lambda h, b: (bmap(b), h)), state,
                   pl.BlockSpec((None, G, RT_DV, RT_DK), lambda h, b: (bmap(b), h, 0, 0))],
        out_shape=[jax.ShapeDtypeStruct((L, HEADS * RT_DV), F32),
                   jax.ShapeDtypeStruct((HEADS, RT_DV, RT_DK), F32),
                   jax.ShapeDtypeStruct((nB, HEADS, RT_DV, RT_DK), F32)],
        scratch_shapes=[pltpu.VMEM((G, RT_DV, RT_DK), F32)],
        compiler_params=_params("parallel", "arbitrary"),
    )(p, p, p, cosf, sinf, lg, s0)


def ret_scan_bwd(p, cosf, sinf, lg, s_blocks, d_o, ds_fin, prev, reverse, name):
    L = p.shape[0]
    C = SCAN_ROWS
    nB = L // C
    has_prev = prev is not None
    out_dt = BF16 if has_prev else F32
    G = RT_GROUP

    def bmap(b):
        return b if reverse else (nB - 1 - b)

    def body(*refs):
        q_ref, k_ref, v_ref, cos_ref, sin_ref, lg_ref, sblk_ref, do_ref, dsf_ref = refs[:9]
        refs = refs[9:]
        if has_prev:
            pq_ref, pk_ref, pv_ref = refs[:3]
            refs = refs[3:]
        dq_ref, dk_ref, dv_ref, dlg_ref, ds0_ref, ds_scr = refs
        blk = pl.program_id(1)

        @pl.when(blk == 0)
        def _():
            ds_scr[...] = dsf_ref[...]
            dlg_ref[...] = jnp.zeros_like(dlg_ref)

        cosf, sinf = cos_ref[...], sin_ref[...]
        for j in range(G):
            lk, lv = slice(j * RT_DK, (j + 1) * RT_DK), slice(j * RT_DV, (j + 1) * RT_DV)
            s_t = sblk_ref[j]
            ds_t = ds_scr[j]
            dmat, delta, pos, qdec, kdec, sdec = _ret_decays(lg_ref[j], reverse)
            q = _rope(q_ref[:, lk] * Q_SCALE, cosf, sinf)
            k = _rope(k_ref[:, lk], cosf, sinf)
            v = v_ref[:, lv]
            d_o = do_ref[:, lv]
            att_raw = _bdot(q, k, 1, 1)
            datt_m = _bdot(d_o, v, 1, 1) * dmat
            dqd = _bdot(d_o, s_t, 1, 0)
            dkd = _bdot(v, ds_t, 1, 0)
            dq = _bdot(datt_m, k) + dqd * qdec
            dk = _bdot(datt_m, q, 0, 0) + dkd * kdec
            dv = _bdot(att_raw * dmat, d_o, 0, 0) + _bdot(k * kdec, ds_t, 1, 1)
            ds_scr[j] = ds_t * sdec + _bdot(d_o, q * qdec, 0, 0)
            t1 = jnp.sum(_rowsum(datt_m * att_raw * delta), axis=-1, keepdims=True)
            t23 = jnp.sum(_rowsum((pos + 1.0) * qdec * q * dqd + (C - 1.0 - pos) * kdec * k * dkd), axis=-1, keepdims=True)
            t4 = jnp.sum(_rowsum(ds_t * s_t * sdec), axis=-1, keepdims=True) * float(C)
            dlg_ref[j] += jnp.broadcast_to(t1 + t23 + t4, (1, RT_DK))
            if has_prev:
                dq = _rope_t(dq + pq_ref[:, lk], cosf, sinf) * Q_SCALE
                dk = _rope_t(dk + pk_ref[:, lk], cosf, sinf)
                dv = dv + pv_ref[:, lv]
            dq_ref[:, lk] = dq.astype(out_dt)
            dk_ref[:, lk] = dk.astype(out_dt)
            dv_ref[:, lv] = dv.astype(out_dt)

        @pl.when(blk == nB - 1)
        def _():
            ds0_ref[...] = ds_scr[...]

    def col(c0):
        return pl.BlockSpec((C, G * RT_DK), lambda h, b: (bmap(b), c0 // G + h))

    tab = pl.BlockSpec((C, RT_DK), lambda h, b: (bmap(b), 0))
    state = pl.BlockSpec((G, RT_DV, RT_DK), lambda h, b: (h, 0, 0))
    tk = pl.BlockSpec((C, G * RT_DK), lambda h, b: (bmap(b), h))
    tv = pl.BlockSpec((C, G * RT_DV), lambda h, b: (bmap(b), h))
    in_specs = [col(COL_RQ), col(COL_RK),
                pl.BlockSpec((C, G * RT_DV), lambda h, b: (bmap(b), COL_RV // (2 * G) + h)),
                tab, tab, pl.BlockSpec((G, 1, RT_DV), lambda h, b: (h, 0, 0)),
                pl.BlockSpec((None, G, RT_DV, RT_DK), lambda h, b: (bmap(b), h, 0, 0)),
                tv, state]
    args = [p, p, p, cosf, sinf, lg, s_blocks, d_o, ds_fin]
    if has_prev:
        in_specs += [tk, tk, tv]
        args += list(prev)
    return pl.pallas_call(
        body, name=name,
        grid=(HEADS // G, nB),
        in_specs=in_specs,
        out_specs=[tk, tk, tv, pl.BlockSpec((G, 1, RT_DK), lambda h, b: (h, 0, 0)), state],
        out_shape=[jax.ShapeDtypeStruct((L, D), out_dt), jax.ShapeDtypeStruct((L, D), out_dt),
                   jax.ShapeDtypeStruct((L, HEADS * RT_DV), out_dt),
                   jax.ShapeDtypeStruct((HEADS, 1, RT_DK), F32),
                   jax.ShapeDtypeStruct((HEADS, RT_DV, RT_DK), F32)],
        scratch_shapes=[pltpu.VMEM((G, RT_DV, RT_DK), F32)],
        compiler_params=_params("parallel", "arbitrary"),
    )(*args)


def _silu_parts(h):
    s = _sigmoid(h)
    return h * s, s * (1.0 + h * (1.0 - s))


def _head_rms(o):
    outs, rs = [], []
    for h in range(HEADS):
        oh = o[:, h * HG_D:(h + 1) * HG_D]
        r = lax.rsqrt(_lanemean(oh * oh) + EPS)
        outs.append(oh * r)
        rs.append(r)
    return outs, rs


def _group_norm(o):
    outs, rs = [], []
    for h in range(HEADS):
        oh = o[:, h * RT_DV:(h + 1) * RT_DV]
        c = oh - _lanemean(oh)
        r = lax.rsqrt(_lanemean(c * c) + GN_EPS)
        outs.append(c * r)
        rs.append(r)
    return outs, rs


MIX_ROWS = 256
MIX_BWD_ROWS = 128


def _mix_specs(rows):
    def t(w, c=0):
        return pl.BlockSpec((rows, w), lambda i: (i, c))

    return t


def mix_fwd(ohf, ohb, orf, orb, p, x, g1, hgw, w_pa, w_pb, w_out, name):
    L = x.shape[0]
    t = _mix_specs(MIX_ROWS)

    def body(ohf_ref, ohb_ref, orf_ref, orb_ref, hg_ref, rg0_ref, rg1_ref, ga_ref, gb_ref, x_ref, g1_ref, hgw_ref,
             wpa_ref, wpb_ref, wout_ref, x1_ref, xmix_ref, merged_ref, ya_ref, yb_ref):
        nh, _ = _head_rms(ohf_ref[...] + ohb_ref[...])
        ya = jnp.concatenate(nh, axis=1) * hgw_ref[...] * _silu_parts(hg_ref[...])[0]
        gn, _ = _group_norm(orf_ref[...] + orb_ref[...])
        rg = jnp.concatenate([rg0_ref[...], rg1_ref[...]], axis=1)
        yb = jnp.concatenate(gn, axis=1) * _silu_parts(rg)[0]
        ya16, yb16 = ya.astype(BF16), yb.astype(BF16)
        merged = (_sigmoid(ga_ref[...]) * _dot(ya16, wpa_ref[...])
                  + _sigmoid(gb_ref[...]) * _dot(yb16, wpb_ref[...])).astype(BF16)
        x_mix = _dot(merged, wout_ref[...])
        x1_ref[...] = x_ref[...] + g1_ref[...] * x_mix
        xmix_ref[...] = x_mix
        merged_ref[...] = merged
        ya_ref[...] = ya16
        yb_ref[...] = yb16

    vec = pl.BlockSpec((1, D), lambda i: (0, 0))

    def full(a):
        return pl.BlockSpec(a.shape, lambda i: (0, 0), pipeline_mode=pl.Buffered(1))

    return pl.pallas_call(
        body, name=name,
        grid=(L // MIX_ROWS,),
        in_specs=[t(D), t(D), t(2 * D), t(2 * D), t(D, COL_HG // 8), t(D, COL_RG // 8), t(D, COL_RG // 8 + 1),
                  t(D, COL_GA // 8), t(D, COL_GB // 8), t(D), vec, vec, full(w_pa), full(w_pb), full(w_out)],
        out_specs=[t(D), t(D), t(D), t(D), t(2 * D)],
        out_shape=[jax.ShapeDtypeStruct((L, D), F32), jax.ShapeDtypeStruct((L, D), F32),
                   jax.ShapeDtypeStruct((L, D), BF16), jax.ShapeDtypeStruct((L, D), BF16),
                   jax.ShapeDtypeStruct((L, 2 * D), BF16)],
        compiler_params=_params("parallel"),
    )(ohf, ohb, orf, orb, p, p, p, p, p, x, g1, hgw, w_pa, w_pb, w_out)


def mix_bwd(dx1, x_mix, ya, yb, ohf, ohb, orf, orb, p, g1, hgw, w_pa, w_pb, w_out, name):
    L = dx1.shape[0]
    t = _mix_specs(MIX_BWD_ROWS)

    def body(dx1_ref, xmix_ref, ya_ref, yb_ref, ohf_ref, ohb_ref, orf_ref, orb_ref, hg_ref, rg0_ref, rg1_ref,
             ga_ref, gb_ref, g1_ref, hgw_ref, wpa_ref, wpb_ref, wout_ref,
             dxm_ref, da_ref, db_ref, dga_ref, dgb_ref, dhg_ref, drg_ref, dohg_ref, dort_ref, sums_ref):
        @pl.when(pl.program_id(0) == 0)
        def _():
            sums_ref[...] = jnp.zeros_like(sums_ref)

        dx1 = dx1_ref[...]
        dxm = (g1_ref[...] * dx1).astype(BF16)
        dxm_ref[...] = dxm
        dmerged = _dot(dxm, wout_ref[...], 1, 1)
        a = _dot(ya_ref[...], wpa_ref[...])
        bm = _dot(yb_ref[...], wpb_ref[...])
        sa, sb = _sigmoid(ga_ref[...]), _sigmoid(gb_ref[...])
        d_a = (dmerged * sa).astype(BF16)
        d_b = (dmerged * sb).astype(BF16)
        da_ref[...] = d_a
        db_ref[...] = d_b
        dga_ref[...] = (dmerged * a * sa * (1.0 - sa)).astype(BF16)
        dgb_ref[...] = (dmerged * bm * sb * (1.0 - sb)).astype(BF16)
        dya = _dot(d_a, wpa_ref[...], 1, 1)
        dyb = _dot(d_b, wpb_ref[...], 1, 1)

        hgw = hgw_ref[...]
        silu_h, dsilu_h = _silu_parts(hg_ref[...])
        nh, rh = _head_rms(ohf_ref[...] + ohb_ref[...])
        n = jnp.concatenate(nh, axis=1)
        dhg_ref[...] = (dya * n * hgw * dsilu_h).astype(BF16)
        dn = dya * hgw * silu_h
        douts = []
        for h in range(HEADS):
            dnh = dn[:, h * HG_D:(h + 1) * HG_D]
            douts.append(rh[h] * (dnh - nh[h] * _lanemean(dnh * nh[h])))
        dohg_ref[...] = jnp.concatenate(douts, axis=1)

        rg = jnp.concatenate([rg0_ref[...], rg1_ref[...]], axis=1)
        silu_r, dsilu_r = _silu_parts(rg)
        gn, rr = _group_norm(orf_ref[...] + orb_ref[...])
        g = jnp.concatenate(gn, axis=1)
        drg_ref[...] = (dyb * g * dsilu_r).astype(BF16)
        dgn = dyb * silu_r
        douts = []
        for h in range(HEADS):
            dgh = dgn[:, h * RT_DV:(h + 1) * RT_DV]
            douts.append(rr[h] * (dgh - _lanemean(dgh) - gn[h] * _lanemean(dgh * gn[h])))
        dort_ref[...] = jnp.concatenate(douts, axis=1)

        sums_ref[0:1, :] += _rowsum(dx1 * xmix_ref[...])
        sums_ref[1:2, :] += _rowsum(dya * n * silu_h)

    vec = pl.BlockSpec((1, D), lambda i: (0, 0))

    def full(a):
        return pl.BlockSpec(a.shape, lambda i: (0, 0), pipeline_mode=pl.Buffered(1))

    bf = functools.partial(jax.ShapeDtypeStruct, dtype=BF16)
    return pl.pallas_call(
        body, name=name,
        grid=(L // MIX_BWD_ROWS,),
        in_specs=[t(D), t(D), t(D), t(2 * D), t(D), t(D), t(2 * D), t(2 * D),
                  t(D, COL_HG // 8), t(D, COL_RG // 8), t(D, COL_RG // 8 + 1), t(D, COL_GA // 8), t(D, COL_GB // 8),
                  vec, vec, full(w_pa), full(w_pb), full(w_out)],
        out_specs=[t(D), t(D), t(D), t(D), t(D), t(D), t(2 * D), t(D), t(2 * D),
                   pl.BlockSpec((8, D), lambda i: (0, 0))],
        out_shape=[bf((L, D)), bf((L, D)), bf((L, D)), bf((L, D)), bf((L, D)), bf((L, D)), bf((L, 2 * D)),
                   jax.ShapeDtypeStruct((L, D), F32), jax.ShapeDtypeStruct((L, 2 * D), F32),
                   jax.ShapeDtypeStruct((8, D), F32)],
        compiler_params=_params("arbitrary"),
    )(dx1, x_mix, ya, yb, ohf, ohb, orf, orb, p, p, p, p, p, g1, hgw, w_pa, w_pb, w_out)


FFN_ROWS = 512


def ffn_fwd(x1, target, nw2, sh2, sc2, g2, fw, wg, wu, wd, name):
    L = x1.shape[0]
    tm = min(FFN_ROWS, L)

    def body(x1_ref, tgt_ref, nw2_ref, sh2_ref, sc2_ref, g2_ref, fw_ref, wg_ref, wu_ref, wd_ref,
             hx2_ref, g_ref, u_ref, h_ref, f_ref, dx2_ref, sums_ref, hx_scr, acc):
        i, j = pl.program_id(0), pl.program_id(1)

        @pl.when((i == 0) & (j == 0))
        def _():
            sums_ref[...] = jnp.zeros_like(sums_ref)

        @pl.when(j == 0)
        def _():
            xv = x1_ref[...]
            n = xv * lax.rsqrt(_lanemean(xv * xv) + EPS) * nw2_ref[...]
            h = (n * (1.0 + sc2_ref[...]) + sh2_ref[...]).astype(BF16)
            hx_scr[...] = h
            hx2_ref[...] = h
            acc[...] = jnp.zeros_like(acc)

        hx = hx_scr[...]
        g = _dot(hx, wg_ref[...])
        u = _dot(hx, wu_ref[...])
        hh = (_silu_parts(g)[0] * u).astype(BF16)
        g_ref[...] = g
        u_ref[...] = u
        h_ref[...] = hh
        acc[...] += _dot(hh, wd_ref[...])

        @pl.when(j == N_SHARD - 1)
        def _():
            f = acc[...]
            f_ref[...] = f
            x2 = x1_ref[...] + g2_ref[...] * f
            r = lax.rsqrt(_lanemean(x2 * x2) + EPS)
            fw = fw_ref[...]
            e = x2 * r * fw - tgt_ref[...]
            dy = e * (1.0 / D)
            dyw = dy * fw
            dx2_ref[...] = r * dyw - x2 * (r * r * r) * _lanemean(dyw * x2)
            sums_ref[0:1, :] += _rowsum(dy * x2 * r)
            sums_ref[1:2, :] += _rowsum(e * e) * (0.5 / D)

    row = pl.BlockSpec((tm, D), lambda i, j: (i, 0))
    vec = pl.BlockSpec((1, D), lambda i, j: (0, 0))
    sh = pl.BlockSpec((None, tm, FF_SH), lambda i, j: (j, i, 0))
    return pl.pallas_call(
        body, name=name,
        grid=(L // tm, N_SHARD),
        in_specs=[row, row, vec, vec, vec, vec, vec,
                  pl.BlockSpec((None, D, FF_SH), lambda i, j: (j, 0, 0)),
                  pl.BlockSpec((None, D, FF_SH), lambda i, j: (j, 0, 0)),
                  pl.BlockSpec((None, FF_SH, D), lambda i, j: (j, 0, 0))],
        out_specs=[row, sh, sh, sh, row, row, pl.BlockSpec((8, D), lambda i, j: (0, 0))],
        out_shape=[jax.ShapeDtypeStruct((L, D), BF16),
                   jax.ShapeDtypeStruct((N_SHARD, L, FF_SH), F32), jax.ShapeDtypeStruct((N_SHARD, L, FF_SH), F32),
                   jax.ShapeDtypeStruct((N_SHARD, L, FF_SH), BF16),
                   jax.ShapeDtypeStruct((L, D), F32), jax.ShapeDtypeStruct((L, D), F32),
                   jax.ShapeDtypeStruct((8, D), F32)],
        scratch_shapes=[pltpu.VMEM((tm, D), BF16), pltpu.VMEM((tm, D), F32)],
        compiler_params=_params("arbitrary", "arbitrary"),
    )(x1, target, nw2, sh2, sc2, g2, fw, wg, wu, wd)


def ffn_bwd(dx2, x1, f, g, u, nw2, sc2, g2, wg, wu, wd, name):
    L = x1.shape[0]
    tm = min(FFN_ROWS, L)

    def body(dx2_ref, x1_ref, f_ref, g_ref, u_ref, nw2_ref, sc2_ref, g2_ref, wg_ref, wu_ref, wd_ref,
             df_ref, dg_ref, du_ref, dx1_ref, sums_ref, df_scr, acc):
        i, j = pl.program_id(0), pl.program_id(1)

        @pl.when((i == 0) & (j == 0))
        def _():
            sums_ref[...] = jnp.zeros_like(sums_ref)

        @pl.when(j == 0)
        def _():
            dx2 = dx2_ref[...]
            df = (g2_ref[...] * dx2).astype(BF16)
            df_scr[...] = df
            df_ref[...] = df
            sums_ref[0:1, :] += _rowsum(dx2 * f_ref[...])
            acc[...] = jnp.zeros_like(acc)

        dh = _dot(df_scr[...], wd_ref[...], 1, 1)
        gv, uv = g_ref[...], u_ref[...]
        silu_g, dsilu_g = _silu_parts(gv)
        dg = (dh * uv * dsilu_g).astype(BF16)
        du = (dh * silu_g).astype(BF16)
        dg_ref[...] = dg
        du_ref[...] = du
        acc[...] += _dot(dg, wg_ref[...], 1, 1) + _dot(du, wu_ref[...], 1, 1)

        @pl.when(j == N_SHARD - 1)
        def _():
            dhx = acc[...]
            xv = x1_ref[...]
            r = lax.rsqrt(_lanemean(xv * xv) + EPS)
            n0 = xv * r
            nw = nw2_ref[...]
            dn2 = dhx * (1.0 + sc2_ref[...])
            dn0 = dn2 * nw
            dx1_ref[...] = dx2_ref[...] + r * (dn0 - n0 * _lanemean(dn0 * n0))
            sums_ref[1:2, :] += _rowsum(dhx)
            sums_ref[2:3, :] += _rowsum(dhx * n0 * nw)
            sums_ref[3:4, :] += _rowsum(dn2 * n0)

    row = pl.BlockSpec((tm, D), lambda i, j: (i, 0))
    vec = pl.BlockSpec((1, D), lambda i, j: (0, 0))
    sh = pl.BlockSpec((None, tm, FF_SH), lambda i, j: (j, i, 0))
    return pl.pallas_call(
        body, name=name,
        grid=(L // tm, N_SHARD),
        in_specs=[row, row, row, sh, sh, vec, vec, vec,
                  pl.BlockSpec((None, D, FF_SH), lambda i, j: (j, 0, 0)),
                  pl.BlockSpec((None, D, FF_SH), lambda i, j: (j, 0, 0)),
                  pl.BlockSpec((None, FF_SH, D), lambda i, j: (j, 0, 0))],
        out_specs=[row, sh, sh, row, pl.BlockSpec((8, D), lambda i, j: (0, 0))],
        out_shape=[jax.ShapeDtypeStruct((L, D), BF16),
                   jax.ShapeDtypeStruct((N_SHARD, L, FF_SH), BF16), jax.ShapeDtypeStruct((N_SHARD, L, FF_SH), BF16),
                   jax.ShapeDtypeStruct((L, D), F32), jax.ShapeDtypeStruct((8, D), F32)],
        scratch_shapes=[pltpu.VMEM((tm, D), BF16), pltpu.VMEM((tm, D), F32)],
        compiler_params=_params("arbitrary", "arbitrary"),
    )(dx2, x1, f, g, u, nw2, sc2, g2, wg, wu, wd)


def matmul_tn(a, b, name, acc_init=None, to_chips=()):
    na, K, M = a.shape
    nb, _, N = b.shape
    n = max(na, nb)
    tk = min(512, K)
    tn = N if N <= 1024 else N // 2
    nk = K // tk
    grid = (n, N // tn, nk)
    has_init = acc_init is not None
    nx = len(to_chips)

    def body(a_ref, b_ref, *refs):
        init_ref = refs[0] if has_init else None
        refs = refs[1:] if has_init else refs
        o_ref = refs[nx]
        if nx:
            start, finish = _to_chips_phases(refs[:nx], refs[nx + 1:2 * nx + 1], *refs[2 * nx + 1:])
            pos, total = _grid_step(grid)
            pl.when(pos == 0)(start)
        kk = pl.program_id(2)

        @pl.when(kk == 0)
        def _():
            o_ref[...] = init_ref[...] if has_init else jnp.zeros_like(o_ref)

        o_ref[...] += _dot(a_ref[...], b_ref[...], 0, 0)
        if nx:
            pl.when(pos == total - 1)(finish)

    out_spec = pl.BlockSpec((None, M, tn), lambda s, j, kk: (s, 0, j))
    in_specs = [pl.BlockSpec((None, tk, M), lambda s, j, kk: (s if na > 1 else 0, kk, 0)),
                pl.BlockSpec((None, tk, tn), lambda s, j, kk: (s if nb > 1 else 0, kk, j))]
    args = [a, b]
    if has_init:
        in_specs.append(out_spec)
        args.append(acc_init)
    out = pl.pallas_call(
        body, name=name,
        grid=grid,
        in_specs=in_specs + [ANY] * nx,
        out_specs=[out_spec] + [ANY] * nx,
        out_shape=[jax.ShapeDtypeStruct((n, M, N), F32)] + _to_chips_shapes(to_chips),
        scratch_shapes=_to_chips_scratch(nx) if nx else [],
        compiler_params=_params(*(("arbitrary",) * 3 if nx else ("parallel", "parallel", "arbitrary"))),
    )(*args, *to_chips)
    return out if nx else out[0]


def matmul_tn_pair(a, b1, b2, name):
    K, M = a.shape
    n, _, N = b1.shape
    tk = min(512, K)

    def body(a_ref, b1_ref, b2_ref, o1_ref, o2_ref):
        @pl.when(pl.program_id(1) == 0)
        def _():
            o1_ref[...] = jnp.zeros_like(o1_ref)
            o2_ref[...] = jnp.zeros_like(o2_ref)

        at = a_ref[...].T
        o1_ref[...] += _dot(at, b1_ref[...])
        o2_ref[...] += _dot(at, b2_ref[...])

    b_spec = pl.BlockSpec((None, tk, N), lambda s, kk: (s, kk, 0))
    o_spec = pl.BlockSpec((None, M, N), lambda s, kk: (s, 0, 0))
    return pl.pallas_call(
        body, name=name,
        grid=(n, K // tk),
        in_specs=[pl.BlockSpec((tk, M), lambda s, kk: (kk, 0)), b_spec, b_spec],
        out_specs=[o_spec, o_spec],
        out_shape=[jax.ShapeDtypeStruct((n, M, N), F32)] * 2,
        compiler_params=_params("parallel", "arbitrary"),
    )(a, b1, b2)


PIECE_COLS = 1024
N_PIECE_BLOCKS = D_IN // PIECE_COLS


def _piece_blocks(pieces):
    out, col = [], 0
    for arr, width in pieces:
        if arr is not None:
            out.append((arr, col // PIECE_COLS, width // PIECE_COLS))
        col += width
    assert col == D_IN
    return out


def _piece_feed(p_refs, blocks, buf, sems, tile_of, pos, total):
    def present(blk):
        ok = None
        for _, b0, nb in blocks:
            mine = (blk >= b0) & (blk < b0 + nb)
            ok = mine if ok is None else ok | mine
        return ok

    def fetch(step):
        blk, rows = tile_of(step)
        for p_ref, (_, b0, nb) in zip(p_refs, blocks):
            for t in range(nb):
                @pl.when(blk == b0 + t)
                def _(p_ref=p_ref, t=t):
                    pltpu.make_async_copy(p_ref.at[rows, pl.ds(t * PIECE_COLS, PIECE_COLS)], buf.at[step % 2],
                                          sems.at[step % 2]).start()

    @pl.when(pos == 0)
    def _():
        fetch(pos)

    @pl.when(pos + 1 < total)
    def _():
        fetch(pos + 1)

    def landed():
        slot = pos % 2
        pltpu.make_async_copy(p_refs[0].at[pl.ds(0, buf.shape[1]), pl.ds(0, PIECE_COLS)], buf.at[slot],
                              sems.at[slot]).wait()
        return buf.at[slot]

    return present(tile_of(pos)[0]), landed


def matmul_tn_pieces(a, pieces, name, acc_init=None, to_chips=()):
    K, M = a.shape
    blocks = _piece_blocks(pieces)
    tk = min(1024, K)
    nk = K // tk
    grid = (N_PIECE_BLOCKS, nk)
    has_init = acc_init is not None
    nx, npc = len(to_chips), len(blocks)

    def body(a_ref, *refs):
        p_refs = refs[:npc]
        refs = refs[npc:]
        init_ref = refs[0] if has_init else None
        refs = refs[1:] if has_init else refs
        o_ref = refs[nx]
        buf, sems = refs[2 * nx + 1:2 * nx + 3]
        pos, total = _grid_step(grid)
        if nx:
            start, finish = _to_chips_phases(refs[:nx], refs[nx + 1:2 * nx + 1], *refs[2 * nx + 3:])
            pl.when(pos == 0)(start)
        here, landed = _piece_feed(p_refs, blocks, buf, sems,
                                   lambda s: (s // nk, pl.ds(pl.multiple_of((s % nk) * tk, tk), tk)), pos, total)

        @pl.when(pl.program_id(1) == 0)
        def _():
            o_ref[...] = init_ref[...] if has_init else jnp.zeros_like(o_ref)

        @pl.when(here)
        def _():
            o_ref[...] += _dot(a_ref[...], landed()[...], 0, 0)

        if nx:
            pl.when(pos == total - 1)(finish)

    out_spec = pl.BlockSpec((M, PIECE_COLS), lambda blk, kk: (0, blk))
    in_specs = [pl.BlockSpec((tk, M), lambda blk, kk: (kk, 0))] + [ANY] * npc
    args = [a] + [arr for arr, _, _ in blocks]
    if has_init:
        in_specs.append(out_spec)
        args.append(acc_init)
    out = pl.pallas_call(
        body, name=name,
        grid=grid,
        in_specs=in_specs + [ANY] * nx,
        out_specs=[out_spec] + [ANY] * nx,
        out_shape=[jax.ShapeDtypeStruct((M, D_IN), F32)] + _to_chips_shapes(to_chips),
        scratch_shapes=[pltpu.VMEM((2, tk, PIECE_COLS), BF16), pltpu.SemaphoreType.DMA((2,))]
        + (_to_chips_scratch(nx) if nx else []),
        compiler_params=_params("arbitrary", "arbitrary"),
    )(*args, *to_chips)
    return out if nx else out[0]


def dhx_normbwd(pieces, w, x, dx_res, nw, sc, name, to_chips=()):
    L = x.shape[0]
    tm = min(PROJ_ROWS, L)
    blocks = _piece_blocks(pieces)
    grid = (L // tm, N_PIECE_BLOCKS)
    nx, npc = len(to_chips), len(blocks)

    def body(*refs):
        p_refs = refs[:npc]
        w_ref, x_ref, res_ref, nw_ref, sc_ref = refs[npc:npc + 5]
        refs = refs[npc + 5:]
        dx_ref, sums_ref = refs[nx:nx + 2]
        acc, buf, sems = refs[2 * nx + 2:2 * nx + 5]
        pos, total = _grid_step(grid)
        if nx:
            start, finish = _to_chips_phases(refs[:nx], refs[nx + 2:2 * nx + 2], *refs[2 * nx + 5:])
            pl.when(pos == 0)(start)
            pl.when(pos == total - 1)(finish)
        here, landed = _piece_feed(
            p_refs, blocks, buf, sems,
            lambda s: (s % N_PIECE_BLOCKS, pl.ds(pl.multiple_of((s // N_PIECE_BLOCKS) * tm, tm), tm)), pos, total)
        i, blk = pl.program_id(0), pl.program_id(1)

        @pl.when((i == 0) & (blk == 0))
        def _():
            sums_ref[...] = jnp.zeros_like(sums_ref)

        @pl.when(blk == 0)
        def _():
            acc[...] = jnp.zeros_like(acc)

        @pl.when(here)
        def _():
            acc[...] += _dot(landed()[...], w_ref[...], 1, 1)

        @pl.when(blk == N_PIECE_BLOCKS - 1)
        def _():
            dhx = acc[...]
            xv = x_ref[...]
            r = lax.rsqrt(_lanemean(xv * xv) + EPS)
            n0 = xv * r
            nw = nw_ref[...]
            dn = dhx * (1.0 + sc_ref[...])
            dn0 = dn * nw
            dx_ref[...] = res_ref[...] + r * (dn0 - n0 * _lanemean(dn0 * n0))
            sums_ref[0:1, :] += _rowsum(dhx)
            sums_ref[1:2, :] += _rowsum(dhx * n0 * nw)
            sums_ref[2:3, :] += _rowsum(dn * n0)

    row = pl.BlockSpec((tm, D), lambda i, blk: (i, 0))
    vec = pl.BlockSpec((1, D), lambda i, blk: (0, 0))
    return pl.pallas_call(
        body, name=name,
        grid=grid,
        in_specs=[ANY] * npc + [pl.BlockSpec((D, PIECE_COLS), lambda i, blk: (0, blk)), row, row, vec, vec] + [ANY] * nx,
        out_specs=[row, pl.BlockSpec((8, D), lambda i, blk: (0, 0))] + [ANY] * nx,
        out_shape=[jax.ShapeDtypeStruct((L, D), F32), jax.ShapeDtypeStruct((8, D), F32)] + _to_chips_shapes(to_chips),
        scratch_shapes=[pltpu.VMEM((tm, D), F32), pltpu.VMEM((2, tm, PIECE_COLS), BF16), pltpu.SemaphoreType.DMA((2,))]
        + (_to_chips_scratch(nx) if nx else []),
        compiler_params=_params("arbitrary", "arbitrary"),
    )(*[arr for arr, _, _ in blocks], w, x, dx_res, nw, sc, *to_chips)


SMALL_ROWS = 24


def _rope_tables(L):
    rows = L // 64
    freqs = 10000.0 ** (-jnp.arange(RT_DK // 4, dtype=F32) / (RT_DK // 4))
    a_row = jnp.arange(rows, dtype=F32)[:, None] * freqs
    a_col = jnp.arange(64, dtype=F32)[:, None] * freqs

    def spread(f):
        return jnp.concatenate([jnp.repeat(f(a_row), 64, axis=0), jnp.tile(f(a_col), (rows, 1))], axis=-1)

    cos, sin = spread(jnp.cos), spread(jnp.sin)
    return jnp.concatenate([cos, cos], axis=1), jnp.concatenate([-sin, sin], axis=1)


def _pieces(hq, hf_f, hf_b, hi, hg, rq, rk, rv, rg, ga, gb):
    widths = (D, D, D, D, D, D, D, 2 * D, 2 * D, D, D)
    return list(zip((hq, hf_f, hf_b, hi, hg, rq, rk, rv, rg, ga, gb), widths))


def _lane0(a):
    return a[:, 0, 0]


def _pack_small(rows):
    out = [r.reshape(1, D) for r in rows]
    out += [jnp.zeros((1, D), F32)] * (SMALL_ROWS - len(out))
    return jnp.concatenate(out, axis=0)


def _sibling_sums(gs, names, place):
    core, core_arg, _ = place

    def other_half(g):
        axis = g.ndim - 2
        h = g.shape[axis] // 2
        return lax.dynamic_slice_in_dim(g, (1 - core) * h, h, axis=axis).astype(BF16)

    payload = [other_half(g) for g in gs]
    received = rs_to_sibling(payload, "rs_to_sibling_" + names[0])
    return [rs_add_sibling(g, r, core_arg, "rs_add_sibling_" + k) for g, r, k in zip(gs, received, names)]


def _staged_in_proj(x, nw, sh, sc, w_shard, rest, chip):
    cx, cy = chip // 2, chip % 2

    def arg(k):
        return jnp.reshape(k, (1,)).astype(jnp.int32)

    p, hx, w_full = in_proj_own(x, nw, sh, sc, w_shard, arg(chip), "in_proj_own")
    p, w_full = in_proj_next(hx, w_full, arg(2 * (1 - cx) + cy), p, "in_proj_x", diag_from=w_shard)
    w_pa, w_pb, w_out, w_wd = rest[0], rest[1], rest[2], rest[5]
    p, g_pa, g_pb, g_out, g_wd = in_proj_next(hx, w_full, arg(2 * cx + 1 - cy), p, "in_proj_y",
                                              gather=[w_pa, w_pb, w_out, w_wd])
    p, g_wg, g_wu = in_proj_next(hx, w_full, arg(3 - chip), p, "in_proj_diag", gather=[rest[3], rest[4]])
    w = {"w_in": w_full, "w_pa": g_pa.reshape(D, D), "w_pb": g_pb.reshape(2 * D, D), "w_out": g_out.reshape(D, D),
         "wg": g_wg, "wu": g_wu, "wd": g_wd}
    return p, hx, w


def local_step(x, ctx, target, mod_x, mod_c, lb_f, lb_b, lg_f, lg_b, nw1, nw2, hgw, fw, w, rest=None, place=None):
    L, Lc = x.shape[0], ctx.shape[0]
    sh1, sc1, g1, sh2, sc2, g2 = (mod_x[i:i + 1] for i in range(6))
    sh1c, sc1c = mod_c[0:1], mod_c[1:2]
    cosf, sinf = _rope_tables(L)
    cosc, sinc = jnp.ones((Lc, RT_DK), F32), jnp.zeros((Lc, RT_DK), F32)
    zero_h = jnp.zeros((HEADS, HG_D, HG_D), F32)
    zero_r = jnp.zeros((HEADS, RT_DV, RT_DK), F32)

    if rest is None:
        p, hx = normmod_matmul(x, nw1, sh1, sc1, w["w_in"], "in_proj")
    else:
        p, hx, w = _staged_in_proj(x, nw1, sh1, sc1, w["w_in_shard"], rest, place[2][0])
    pc, hxc = normmod_matmul(ctx, nw1, sh1c, sc1c, w["w_in"], "ctx_in_proj")
    _, s_hf, cb_hf = hgrn_scan_fwd(pc, lb_f, zero_h, COL_HFF, False, "ctx_hgrn_f")
    _, s_hb, cb_hb = hgrn_scan_fwd(pc, lb_b, zero_h, COL_HFB, True, "ctx_hgrn_b")
    _, s_rf, cb_rf = ret_scan_fwd(pc, cosc, sinc, lg_f, zero_r, False, "ctx_ret_f")
    _, s_rb, cb_rb = ret_scan_fwd(pc, cosc, sinc, lg_b, zero_r, True, "ctx_ret_b")
    ohf, _, xb_hf = hgrn_scan_fwd(p, lb_f, s_hf, COL_HFF, False, "hgrn_f")
    ohb, _, xb_hb = hgrn_scan_fwd(p, lb_b, s_hb, COL_HFB, True, "hgrn_b")
    orf, _, xb_rf = ret_scan_fwd(p, cosf, sinf, lg_f, s_rf, False, "ret_f")
    orb, _, xb_rb = ret_scan_fwd(p, cosf, sinf, lg_b, s_rb, True, "ret_b")
    x1, x_mix, merged, ya, yb = mix_fwd(ohf, ohb, orf, orb, p, x, g1, hgw, w["w_pa"], w["w_pb"], w["w_out"], "mix_fwd")
    hx2, gg, uu, hh, ff, dx2, sums_f = ffn_fwd(x1, target, nw2, sh2, sc2, g2, fw, w["wg"], w["wu"], w["wd"], "ffn_fwd")

    d_f, d_g, d_u, dx1, sums_fb = ffn_bwd(dx2, x1, ff, gg, uu, nw2, sc2, g2, w["wg"], w["wu"], w["wd"], "ffn_bwd")
    dw_gate, dw_up = matmul_tn_pair(hx2, d_g, d_u, "dw_ffn_gate_up")
    grads = {"wg": dw_gate, "wu": dw_up, "wd": matmul_tn(hh, d_f[None], "dw_ffn_down")}
    dxm, d_a, d_b, dga, dgb, dhg, drg, dohg, dort, sums_m = mix_bwd(
        dx1, x_mix, ya, yb, ohf, ohb, orf, orb, p, g1, hgw, w["w_pa"], w["w_pb"], w["w_out"], "mix_bwd")
    grads["w_out"] = matmul_tn(merged[None], dxm[None], "dw_out").reshape(N_SHARD, D // N_SHARD, D)
    grads["w_pa"] = matmul_tn(ya[None], d_a[None], "dw_proj_hgrn").reshape(N_SHARD, D // N_SHARD, D)
    grads["w_pb"] = matmul_tn(yb[None], d_b[None], "dw_proj_ret").reshape(N_SHARD, 2 * D // N_SHARD, D)

    rq1, rk1, rv1, dlgf_x, ds_rf = ret_scan_bwd(p, cosf, sinf, lg_f, xb_rf, dort, zero_r, None, False, "ret_f_bwd")
    drq, drk, drv, dlgb_x, ds_rb = ret_scan_bwd(p, cosf, sinf, lg_b, xb_rb, dort, zero_r, (rq1, rk1, rv1), True, "ret_b_bwd")
    hq1, dzf, hv1, dlbf_x, ds_hf = hgrn_scan_bwd(p, lb_f, xb_hf, dohg, zero_h, None, COL_HFF, False, "hgrn_f_bwd")
    dhq, dzb, dhv, dlbb_x, ds_hb = hgrn_scan_bwd(p, lb_b, xb_hb, dohg, zero_h, (hq1, hv1), COL_HFB, True, "hgrn_b_bwd")
    dp = _pieces(dhq, dzf, dzb, dhv, dhg, drq, drk, drv, drg, dga, dgb)
    others = ["w_pa", "w_pb", "w_out", "wg", "wu", "wd"]
    if place is None:
        dw_in = matmul_tn_pieces(hx, dp, "dw_in")
    else:
        sums_o = _sibling_sums([grads[k] for k in others], others, place)
        dw_in, *recv_o = matmul_tn_pieces(hx, dp, "dw_in", to_chips=[a16 for _, a16 in sums_o])

    zc = jnp.zeros((Lc, D), F32)
    zc2 = jnp.zeros((Lc, 2 * D), F32)
    crq1, crk1, crv1, dlgf_c, _ = ret_scan_bwd(pc, cosc, sinc, lg_f, cb_rf, zc2, ds_rf, None, False, "ctx_ret_f_bwd")
    cdrq, cdrk, cdrv, dlgb_c, _ = ret_scan_bwd(pc, cosc, sinc, lg_b, cb_rb, zc2, ds_rb, (crq1, crk1, crv1), True, "ctx_ret_b_bwd")
    chq1, cdzf, chv1, dlbf_c, _ = hgrn_scan_bwd(pc, lb_f, cb_hf, zc, ds_hf, None, COL_HFF, False, "ctx_hgrn_f_bwd")
    cdhq, cdzb, cdhv, dlbb_c, _ = hgrn_scan_bwd(pc, lb_b, cb_hb, zc, ds_hb, (chq1, chv1), COL_HFB, True, "ctx_hgrn_b_bwd")
    dpc = _pieces(cdhq, cdzf, cdzb, cdhv, None, cdrq, cdrk, cdrv, None, None, None)
    _, sums_c = dhx_normbwd(dpc, w["w_in"], ctx, zc, nw1, sc1c, "dctx_in_proj")
    grads["w_in"] = matmul_tn_pieces(hxc, dpc, "dw_in_ctx", acc_init=dw_in)
    if place is None:
        dx, sums_x = dhx_normbwd(dp, w["w_in"], x, dx1, nw1, sc1, "dx_in_proj")
    else:
        sums_i = _sibling_sums([grads["w_in"]], ["w_in"], place)
        dx, sums_x, recv_i = dhx_normbwd(dp, w["w_in"], x, dx1, nw1, sc1, "dx_in_proj", to_chips=[sums_i[0][1]])
        names = ["w_in"] + others
        halves = [rs_add_chips(a, r, place[2], "rs_add_chips_" + k)
                  for (a, _), r, k in zip(sums_i + sums_o, [recv_i] + recv_o, names)]
        grads = dict(zip(names, rs_join_halves(halves, "rs_join_halves")))

    def lg_row(f, b):
        return jnp.concatenate([_lane0(f), _lane0(b), jnp.zeros((D - 2 * HEADS,), F32)])

    small = _pack_small([
        sums_x[0], sums_x[1], sums_m[0], sums_fb[1], sums_fb[2], sums_fb[0],
        sums_c[0], sums_c[1],
        sums_x[2], sums_c[2], sums_fb[3], sums_m[1], sums_f[0],
        dlbf_x, dlbf_c, dlbb_x, dlbb_c,
        lg_row(dlgf_x, dlgb_x), lg_row(dlgf_c, dlgb_c),
        sums_f[1],
    ])
    return dx, grads, small


MESH = pl.DeviceIdType.MESH
ANY = pl.BlockSpec(memory_space=pl.ANY)
N_DEV = 8


def _place():
    return lax.axis_index("x"), lax.axis_index("y"), lax.axis_index("c")


def _other_chips(x, y):
    return [(1 - x, y), (x, 1 - y), (1 - x, 1 - y)]


def allgather8(xs, name):
    m, n = xs.shape

    def body(x_ref, out_ref, send_sems, recv_sems, local_sem):
        x, y, c = _place()
        me, sibling = (x, y, c), (x, y, 1 - c)
        chips = _other_chips(x, y)

        def rows(px, py, pc):
            return out_ref.at[pl.ds((4 * px + 2 * py + pc) * m, m), :]

        def copy(k, block, to, src=None):
            return pltpu.make_async_remote_copy(
                src_ref=rows(*block) if src is None else src, dst_ref=rows(*block),
                send_sem=send_sems.at[k], recv_sem=recv_sems.at[k], device_id=to, device_id_type=MESH)

        mine = pltpu.make_async_copy(x_ref, rows(*me), local_sem)
        mine.start()
        first = [copy(0, me, sibling, src=x_ref)]
        first += [copy(1 + j, me, (*chip, c), src=x_ref) for j, chip in enumerate(chips)]
        for cp in first:
            cp.start()
        passed = [copy(4 + j, (*chip, c), sibling) for j, chip in enumerate(chips)]
        for j, chip in enumerate(chips):
            copy(1 + j, (*chip, c), me).wait_recv()
            passed[j].start()
        copy(0, sibling, me).wait_recv()
        for j, chip in enumerate(chips):
            copy(4 + j, (*chip, 1 - c), me).wait_recv()
        for cp in first + passed:
            cp.wait_send()
        mine.wait()

    return pl.pallas_call(
        body, name=name,
        out_shape=jax.ShapeDtypeStruct((N_DEV * m, n), xs.dtype),
        in_specs=[pl.BlockSpec(memory_space=pltpu.VMEM)],
        out_specs=pl.BlockSpec(memory_space=pltpu.VMEM),
        scratch_shapes=[pltpu.SemaphoreType.DMA((7,)), pltpu.SemaphoreType.DMA((7,)), pltpu.SemaphoreType.DMA],
    )(xs)


def _gather_phases(ins, outs, send_sems, recv_sems, local_sems, relations=(0, 1, 2), stage=None):
    n = len(ins)
    x, y, c = _place()
    chips = _other_chips(x, y)

    def rows(i, core):
        h = ins[i].shape[0] // 2
        return pl.ds(pl.multiple_of(core * h, 16), h)

    def region(i, k, rs):
        if len(outs[i].shape) == 2:
            cols = ins[i].shape[1]
            return outs[i].at[rs, pl.ds(pl.multiple_of(k * cols, 128), cols)]
        return outs[i].at[k, rs, :]

    def landed(i, chip, core):
        return region(i, 2 * chip[0] + chip[1], rows(i, core))

    def copy(i, k, src, dst, to):
        return pltpu.make_async_remote_copy(src_ref=src, dst_ref=dst, send_sem=send_sems.at[6 * i + k],
                                            recv_sem=recv_sems.at[6 * i + k], device_id=to, device_id_type=MESH)

    def lift(i):
        return pltpu.make_async_copy(ins[i], stage[i], local_sems.at[i])

    def drop(i):
        return pltpu.make_async_copy(stage[i], region(i, 2 * x + y, pl.ds(0, ins[i].shape[0])), local_sems.at[i])

    def send(i, j):
        return copy(i, j, ins[i].at[rows(i, c), :], landed(i, (x, y), c), (*chips[j], c))

    def arrived(i, j, core, k):
        return copy(i, k, ins[i].at[rows(i, core), :], landed(i, chips[j], core), (x, y, 1 - c))

    def passed(i, j):
        return copy(i, 3 + j, landed(i, chips[j], c), landed(i, chips[j], c), (x, y, 1 - c))

    def start():
        for i in range(n):
            if stage is not None:
                lift(i).start()
            for j in relations:
                send(i, j).start()

    def forward():
        for i in range(n):
            if stage is not None:
                lift(i).wait()
                drop(i).start()
            for j in relations:
                arrived(i, j, c, j).wait_recv()
                passed(i, j).start()

    def finish():
        for i in range(n):
            for j in relations:
                arrived(i, j, 1 - c, 3 + j).wait_recv()
        for i in range(n):
            for j in relations:
                send(i, j).wait_send()
                passed(i, j).wait_send()
            if stage is not None:
                drop(i).wait()

    return start, forward, finish


def _gather_scratch(n):
    return [pltpu.SemaphoreType.DMA((6 * n,)), pltpu.SemaphoreType.DMA((6 * n,)), pltpu.SemaphoreType.DMA((n,))]


def rs_to_sibling(payloads, name):
    n = len(payloads)

    def body(*refs):
        ins, outs = refs[:n], refs[n:2 * n]
        send_sems, recv_sems = refs[2 * n:]
        x, y, c = _place()
        copies = []
        for i in range(n):
            cp = pltpu.make_async_remote_copy(src_ref=ins[i], dst_ref=outs[i], send_sem=send_sems.at[i],
                                              recv_sem=recv_sems.at[i], device_id=(x, y, 1 - c), device_id_type=MESH)
            cp.start()
            copies.append(cp)
        for cp in copies:
            cp.wait()

    return pl.pallas_call(
        body, name=name,
        out_shape=[jax.ShapeDtypeStruct(g.shape, g.dtype) for g in payloads],
        in_specs=[ANY] * n, out_specs=[ANY] * n,
        scratch_shapes=[pltpu.SemaphoreType.DMA((n,)), pltpu.SemaphoreType.DMA((n,))],
    )(*payloads)


def _to_chips_phases(ins, outs, send_sems, recv_sems):
    def copies():
        x, y, c = _place()
        return [pltpu.make_async_remote_copy(
            src_ref=ins[i].at[2 * px + py], dst_ref=outs[i].at[j], send_sem=send_sems.at[3 * i + j],
            recv_sem=recv_sems.at[3 * i + j], device_id=(px, py, c), device_id_type=MESH)
            for i in range(len(ins)) for j, (px, py) in enumerate(_other_chips(x, y))]

    def start():
        for cp in copies():
            cp.start()

    def finish():
        for cp in copies():
            cp.wait()

    return start, finish


def _to_chips_shapes(parts):
    return [jax.ShapeDtypeStruct((3,) + a.shape[1:], a.dtype) for a in parts]


def _to_chips_scratch(n):
    return [pltpu.SemaphoreType.DMA((3 * n,)), pltpu.SemaphoreType.DMA((3 * n,))]


def rs_join_halves(fulls, name):
    n = len(fulls)

    def body(*refs):
        outs = refs[n:2 * n]
        send_sems, recv_sems = refs[2 * n:]
        x, y, c = _place()

        def copy(i, core):
            h = fulls[i].shape[0] // 2
            rows = outs[i].at[pl.ds(pl.multiple_of(core * h, 8), h), :]
            return pltpu.make_async_remote_copy(src_ref=rows, dst_ref=rows, send_sem=send_sems.at[i],
                                                recv_sem=recv_sems.at[i], device_id=(x, y, 1 - c), device_id_type=MESH)

        sent = [copy(i, c) for i in range(n)]
        for cp in sent:
            cp.start()
        for i in range(n):
            copy(i, 1 - c).wait_recv()
        for cp in sent:
            cp.wait_send()

    return pl.pallas_call(
        body, name=name,
        out_shape=[jax.ShapeDtypeStruct(a.shape, a.dtype) for a in fulls],
        in_specs=[ANY] * n, out_specs=[ANY] * n,
        input_output_aliases={i: i for i in range(n)},
        scratch_shapes=[pltpu.SemaphoreType.DMA((n,)), pltpu.SemaphoreType.DMA((n,))],
    )(*fulls)


def _row_tile(rows, cols, limit_bytes=2 * 1024 * 1024, mult=8):
    best = mult
    for t in range(mult, rows + 1, mult):
        if rows % t == 0 and t * cols * 4 <= limit_bytes:
            best = t
    return best


def rs_add_sibling(g, recv, c, name):
    if g.ndim == 2:
        h, C = recv.shape[0], recv.shape[1] // N_SHARD
    else:
        _, h, C = recv.shape
    tr = _row_tile(h, C, mult=16)
    nt = h // tr

    def body(c_ref, g_ref, r_ref, o_ref, o16_ref):
        s = g_ref[...] + r_ref[...].astype(F32)
        o_ref[...] = s
        o16_ref[...] = s.astype(BF16)

    blk = pl.BlockSpec((None, tr, C), lambda k, i, c_ref: (k, i, 0))
    if g.ndim == 2:
        g_spec = pl.BlockSpec((tr, C), lambda k, i, c_ref: (c_ref[0] * nt + i, k))
        r_spec = pl.BlockSpec((tr, C), lambda k, i, c_ref: (i, k))
    else:
        g_spec = pl.BlockSpec((None, tr, C), lambda k, i, c_ref: (k, c_ref[0] * nt + i, 0))
        r_spec = blk
    return pl.pallas_call(
        body, name=name,
        grid_spec=pltpu.PrefetchScalarGridSpec(
            num_scalar_prefetch=1, grid=(N_SHARD, nt),
            in_specs=[g_spec, r_spec],
            out_specs=[blk, blk]),
        out_shape=[jax.ShapeDtypeStruct((N_SHARD, h, C), F32), jax.ShapeDtypeStruct((N_SHARD, h, C), BF16)],
        compiler_params=_params("parallel", "parallel"),
    )(c, g, recv)


def rs_add_chips(part, recv, place, name):
    _, h, C = part.shape
    tr = _row_tile(h, C, mult=16)
    nt = h // tr

    def body(k_ref, p_ref, r_ref, o_ref):
        o_ref[...] = ((p_ref[...] + r_ref[0].astype(F32)) + r_ref[1].astype(F32)) + r_ref[2].astype(F32)

    return pl.pallas_call(
        body, name=name,
        grid_spec=pltpu.PrefetchScalarGridSpec(
            num_scalar_prefetch=1, grid=(nt,),
            in_specs=[pl.BlockSpec((None, tr, C), lambda i, k_ref: (k_ref[0], i, 0)),
                      pl.BlockSpec((3, tr, C), lambda i, k_ref: (0, i, 0))],
            out_specs=pl.BlockSpec((tr, C), lambda i, k_ref: (k_ref[1] * nt + i, 0))),
        out_shape=jax.ShapeDtypeStruct((2 * h, C), F32),
        compiler_params=_params("parallel"),
    )(place, part, recv)


def _adamw_math(w, g, m, v):
    m = ADAM_B1 * m + (1.0 - ADAM_B1) * g
    v = ADAM_B2 * v + (1.0 - ADAM_B2) * (g * g)
    m_hat = m / (1.0 - ADAM_B1 ** ADAM_STEP)
    v_hat = v / (1.0 - ADAM_B2 ** ADAM_STEP)
    delta = -ADAM_LR * (m_hat / (jnp.sqrt(v_hat) + ADAM_EPS) + ADAM_WD * w)
    return delta, m, v


def adamw(w, g, m, v, name):
    R, C = w.shape
    tr = _row_tile(R, C, 1024 * 1024)

    def body(w_ref, g_ref, m_ref, v_ref, d_ref, nm_ref, nv_ref):
        d_ref[...], nm_ref[...], nv_ref[...] = _adamw_math(w_ref[...], g_ref[...], m_ref[...], v_ref[...])

    blk = pl.BlockSpec((tr, C), lambda i: (i, 0))
    return pl.pallas_call(
        body, name=name, grid=(R // tr,), in_specs=[blk] * 4, out_specs=[blk] * 3,
        out_shape=[jax.ShapeDtypeStruct((R, C), F32)] * 3,
        compiler_params=_params("parallel"),
    )(w, g, m, v)


MOD_SH = 6 * D // N_SHARD
PK_ROWS = 16


def mod_fwd(call16, w_sh, b_sh, name):
    def body(c_ref, w_ref, b_ref, o_ref):
        o_ref[...] = _dot(_silu_parts(c_ref[...])[0], w_ref[...], prec=HI) + b_ref[...]

    return pl.pallas_call(body, name=name, out_shape=jax.ShapeDtypeStruct((16, MOD_SH), F32),
                          compiler_params=_params())(call16, w_sh, b_sh)


def prep_small(lbf2, lbb2, theta_row, name):
    def body(f_ref, b_ref, t_ref, lbf_ref, lbb_ref, lg_ref):
        lbf_ref[...] = _sigmoid(f_ref[0:1, :] - f_ref[1:2, :])
        lbb_ref[...] = _sigmoid(b_ref[0:1, :] - b_ref[1:2, :])
        t = t_ref[...]
        lg_ref[...] = jnp.minimum(t, 0.0) - jnp.log(1.0 + jnp.exp(-jnp.abs(t)))

    row = jax.ShapeDtypeStruct((1, D), F32)
    return pl.pallas_call(body, name=name, out_shape=[row, row, row], compiler_params=_params())(lbf2, lbb2, theta_row)


def small_grads(g3, lbf, lbb, theta_row, name):
    def body(g_ref, lbf_ref, lbb_ref, t_ref, pk_ref, aux_ref):
        s = g_ref[0]
        for d in range(1, N_DEV):
            s = s + g_ref[d]
        pk_ref[...] = jnp.zeros_like(pk_ref)
        aux_ref[...] = jnp.zeros_like(aux_ref)
        pk_ref[1:7, :] = s[0:6]
        pk_ref[1:3, :] += s[6:8]
        pk_ref[7:8, :] = s[8:9] + s[9:10]
        pk_ref[8:9, :] = s[10:11]
        lbf, lbb = lbf_ref[...], lbb_ref[...]
        daf = (s[13:14] + s[14:15]) * lbf * (1.0 - lbf)
        dab = (s[15:16] + s[16:17]) * lbb * (1.0 - lbb)
        pk_ref[9:10, :] = daf
        pk_ref[10:11, :] = -daf
        pk_ref[11:12, :] = dab
        pk_ref[12:13, :] = -dab
        pk_ref[13:14, :] = s[11:12]
        pk_ref[14:15, :] = (s[17:18] + s[18:19]) * _sigmoid(-t_ref[...])
        pk_ref[15:16, :] = s[12:13]
        aux_ref[0:2, :] = s[6:8]
        aux_ref[2:3, :] = jnp.broadcast_to(jnp.sum(s[19:20], axis=-1, keepdims=True), (1, D))

    return pl.pallas_call(body, name=name,
                          out_shape=[jax.ShapeDtypeStruct((PK_ROWS, D), F32), jax.ShapeDtypeStruct((8, D), F32)],
                          compiler_params=_params())(g3, lbf, lbb, theta_row)


def mod_bwd(call16, dmod_sh, w_sh, name):
    def body(c_ref, d_ref, w_ref, dw_ref, ds_ref):
        dm = d_ref[...]
        dw_ref[...] = _dot(_silu_parts(c_ref[...])[0], dm, 0, 0, prec=HI)
        ds_ref[...] = jnp.zeros_like(ds_ref)
        ds_ref[0:1, :] = _dot(dm[8:9, :], w_ref[...], 1, 1, prec=HI)

    return pl.pallas_call(body, name=name,
                          out_shape=[jax.ShapeDtypeStruct((D, MOD_SH), F32), jax.ShapeDtypeStruct((8, D), F32)],
                          compiler_params=_params())(call16, dmod_sh, w_sh)


def adamw_small(g4, pk_g, pk_w, pk_m, pk_v, name):
    def body(g4_ref, g_ref, w_ref, m_ref, v_ref, go_ref, d_ref, nm_ref, nv_ref):
        w = w_ref[...]
        ds = ((g4_ref[0:1, :] + g4_ref[16:17, :]) + g4_ref[32:33, :]) + g4_ref[48:49, :]
        row = lax.broadcasted_iota(jnp.int32, (PK_ROWS, D), 0)
        g = jnp.where(row == 0, ds * _silu_parts(w[0:1, :])[1], g_ref[...])
        go_ref[...] = g
        d_ref[...], nm_ref[...], nv_ref[...] = _adamw_math(w, g, m_ref[...], v_ref[...])

    pk = jax.ShapeDtypeStruct((PK_ROWS, D), F32)
    return pl.pallas_call(body, name=name, out_shape=[pk, pk, pk, pk], compiler_params=_params())(g4, pk_g, pk_w, pk_m, pk_v)


def _pack_params(c_ctx, b_mod, n1, n2, lbf, lbb, hgn, th_f, th_b, fin):
    theta = jnp.concatenate([th_f.reshape(HEADS), th_b.reshape(HEADS), jnp.zeros((D - 2 * HEADS,), F32)])
    return jnp.concatenate([c_ctx.reshape(1, D), b_mod.reshape(6, D), n1.reshape(1, D), n2.reshape(1, D), lbf, lbb,
                            hgn.reshape(1, D), theta.reshape(1, D), fin.reshape(1, D)], axis=0)


def _unpack_params(pk):
    return (pk[0], pk[1:7].reshape(1, 6 * D), pk[7:8], pk[8:9], pk[9:11], pk[11:13], pk[13:14],
            pk[14, 0:HEADS].reshape(1, HEADS), pk[14, HEADS:2 * HEADS].reshape(1, HEADS), pk[15])


def kernel(x, c, ctx, c_ctx, w_mod, b_mod, norm1_w, norm2_w, w_in, hg_lb_fwd, hg_lb_bwd, hg_norm_w, rt_theta_fwd, rt_theta_bwd, w_proj_hgrn, w_proj_ret, w_out, w_ffn_gate, w_ffn_up, w_ffn_down, final_norm_w, loss_target, m_c_ctx, m_w_mod, m_b_mod, m_norm1_w, m_norm2_w, m_w_in, m_hg_lb_fwd, m_hg_lb_bwd, m_hg_norm_w, m_rt_theta_fwd, m_rt_theta_bwd, m_w_proj_hgrn, m_w_proj_ret, m_w_out, m_w_ffn_gate, m_w_ffn_up, m_w_ffn_down, m_final_norm_w, v_c_ctx, v_w_mod, v_b_mod, v_norm1_w, v_norm2_w, v_w_in, v_hg_lb_fwd, v_hg_lb_bwd, v_hg_norm_w, v_rt_theta_fwd, v_rt_theta_bwd, v_w_proj_hgrn, v_w_proj_ret, v_w_out, v_w_ffn_gate, v_w_ffn_up, v_w_ffn_down, v_final_norm_w):
    xi, yi, ci = _place()
    dev = 4 * xi + 2 * yi + ci
    chip = 2 * xi + yi
    core_arg = jnp.reshape(ci, (1,)).astype(jnp.int32)
    place_arg = jnp.stack([chip, ci]).astype(jnp.int32)

    c_all = allgather8(jnp.concatenate([c, jnp.zeros((7, D), F32)], axis=0), "gather_c").reshape(N_DEV, 8, D)[:, 0]
    call16 = jnp.concatenate([c_all, c_ctx.reshape(1, D), jnp.zeros((7, D), F32)], axis=0)
    b_sh = lax.dynamic_slice_in_dim(b_mod, chip * MOD_SH, MOD_SH, axis=1)
    mod_sh = mod_fwd(call16, w_mod[0], b_sh, "mod_fwd")
    mod_g = allgather8(mod_sh, "gather_mod").reshape(N_DEV, 16, MOD_SH)
    mod_all = jnp.concatenate([mod_g[0], mod_g[2], mod_g[4], mod_g[6]], axis=1)
    mod_x = lax.dynamic_index_in_dim(mod_all, dev, axis=0, keepdims=False).reshape(6, D)
    mod_c = mod_all[8].reshape(6, D)

    pk_w = _pack_params(c_ctx, b_mod, norm1_w, norm2_w, hg_lb_fwd, hg_lb_bwd, hg_norm_w, rt_theta_fwd, rt_theta_bwd, final_norm_w)
    theta_row = pk_w[14:15]
    lb_f, lb_b, lg_row = prep_small(hg_lb_fwd, hg_lb_bwd, theta_row, "prep_small")
    lg_f = jnp.broadcast_to(lg_row[0, 0:HEADS].reshape(HEADS, 1, 1), (HEADS, 1, RT_DV))
    lg_b = jnp.broadcast_to(lg_row[0, HEADS:2 * HEADS].reshape(HEADS, 1, 1), (HEADS, 1, RT_DV))

    rest = [s[0].astype(BF16) for s in (w_proj_hgrn, w_proj_ret, w_out, w_ffn_gate, w_ffn_up, w_ffn_down)]

    dx, full, small = local_step(x[0], ctx[0], loss_target[0], mod_x, mod_c, lb_f, lb_b, lg_f, lg_b,
                                 norm1_w, norm2_w, hg_norm_w, final_norm_w.reshape(1, D),
                                 {"w_in_shard": w_in[0].astype(BF16)}, rest, (ci, core_arg, place_arg))

    g3 = allgather8(small, "gather_small").reshape(N_DEV, SMALL_ROWS, D)
    pk_g, aux = small_grads(g3, lb_f, lb_b, theta_row, "small_grads")
    loss = aux[2, 0]
    dmod16 = jnp.concatenate([
        g3[:, 0:6, :].reshape(N_DEV, 6 * D),
        jnp.concatenate([aux[0], aux[1], jnp.zeros((4 * D,), F32)]).reshape(1, 6 * D),
        jnp.zeros((7, 6 * D), F32)], axis=0)
    dmod_sh = lax.dynamic_slice_in_dim(dmod16, chip * MOD_SH, MOD_SH, axis=1)
    g_wmod, dsilu = mod_bwd(call16, dmod_sh, w_mod[0], "mod_bwd")
    g4 = allgather8(dsilu, "gather_dsilu")
    pk_m = _pack_params(m_c_ctx, m_b_mod, m_norm1_w, m_norm2_w, m_hg_lb_fwd, m_hg_lb_bwd, m_hg_norm_w, m_rt_theta_fwd, m_rt_theta_bwd, m_final_norm_w)
    pk_v = _pack_params(v_c_ctx, v_b_mod, v_norm1_w, v_norm2_w, v_hg_lb_fwd, v_hg_lb_bwd, v_hg_norm_w, v_rt_theta_fwd, v_rt_theta_bwd, v_final_norm_w)
    pk_g, pk_d, pk_nm, pk_nv = adamw_small(g4, pk_g, pk_w, pk_m, pk_v, "adamw_small")

    big = {
        "w_mod": (g_wmod, w_mod, m_w_mod, v_w_mod),
        "w_in": (full["w_in"], w_in, m_w_in, v_w_in),
        "w_pa": (full["w_pa"], w_proj_hgrn, m_w_proj_hgrn, v_w_proj_hgrn),
        "w_pb": (full["w_pb"], w_proj_ret, m_w_proj_ret, v_w_proj_ret),
        "w_out": (full["w_out"], w_out, m_w_out, v_w_out),
        "wg": (full["wg"], w_ffn_gate, m_w_ffn_gate, v_w_ffn_gate),
        "wu": (full["wu"], w_ffn_up, m_w_ffn_up, v_w_ffn_up),
        "wd": (full["wd"], w_ffn_down, m_w_ffn_down, v_w_ffn_down),
    }
    res = {}
    for k, (g, wt, mt, vt) in big.items():
        d, nm, nv = adamw(wt[0], g, mt[0], vt[0], "adamw_" + k)
        res[k] = (g[None], d[None], nm[None], nv[None])

    sm = [_unpack_params(p) for p in (pk_g, pk_d, pk_nm, pk_nv)]
    outs = []
    for t in range(4):
        (s_cctx, s_bmod, s_n1, s_n2, s_lbf, s_lbb, s_hgn, s_thf, s_thb, s_fin) = sm[t]
        outs.append([s_cctx, res["w_mod"][t], s_bmod, s_n1, s_n2, res["w_in"][t], s_lbf, s_lbb, s_hgn, s_thf, s_thb,
                     res["w_pa"][t], res["w_pb"][t], res["w_out"][t], res["wg"][t], res["wu"][t], res["wd"][t], s_fin])
    return (loss, dx[None], *outs[0], *outs[1], *outs[2], *outs[3])
```

```python
import functools

import jax
import jax.numpy as jnp
from jax import lax
from jax.experimental import pallas as pl
from jax.experimental.pallas import tpu as pltpu

F32 = jnp.float32
BF16 = jnp.bfloat16
HI = lax.Precision.HIGHEST
CUMSUM_PRECISION = lax.Precision.HIGH

D = 1024
HEADS = 8
HG_D = 128
RT_DK = 128
RT_DV = 256
D_FF = 2816
D_IN = 13312
N_SHARD = 4
IN_SH = D_IN // N_SHARD
FF_SH = D_FF // N_SHARD
HG_CHUNK = 32
SCAN_ROWS = 256
HG_GROUP = 8
RT_GROUP = 4
PROJ_ROWS = 1024
EPS = 1e-6
GN_EPS = 1e-5
Q_SCALE = 128.0 ** -0.5
VMEM_LIMIT = 56 * 1024 * 1024

COL_HQ, COL_HFF, COL_HFB, COL_HI, COL_HG = 0, 8, 16, 24, 32
COL_RQ, COL_RK, COL_RV, COL_RG, COL_GA, COL_GB = 40, 48, 56, 72, 88, 96

ADAM_LR, ADAM_B1, ADAM_B2, ADAM_EPS, ADAM_WD, ADAM_STEP = 0.001, 0.9, 0.999, 1e-08, 0.01, 10


def _params(*sem):
    return pltpu.CompilerParams(dimension_semantics=sem, vmem_limit_bytes=VMEM_LIMIT)


def _dot(a, b, ca=1, cb=0, prec=None):
    return lax.dot_general(a, b, (((ca,), (cb,)), ((), ())), precision=prec, preferred_element_type=F32)


def _bdot(a, b, ca=1, cb=0):
    return _dot(a.astype(BF16), b.astype(BF16), ca, cb)


def _sigmoid(z):
    return 1.0 / (1.0 + jnp.exp(-z))


def _rowsum(a):
    return jnp.sum(a, axis=0, keepdims=True)


def _lanemean(a):
    return jnp.mean(a, axis=-1, keepdims=True)


def _grid_step(grid):
    pos, total = 0, 1
    for d, size in enumerate(grid):
        pos = pos * size + pl.program_id(d)
        total *= size
    return pos, total


def normmod_matmul(x, nw, sh, sc, w, name):
    L = x.shape[0]
    tm = min(PROJ_ROWS, L)
    tn = IN_SH // 2

    def body(x_ref, nw_ref, sh_ref, sc_ref, w_ref, p_ref, hx_ref, hx_scr):
        @pl.when(pl.program_id(1) == 0)
        def _():
            xv = x_ref[...]
            n = xv * lax.rsqrt(_lanemean(xv * xv) + EPS) * nw_ref[...]
            h = (n * (1.0 + sc_ref[...]) + sh_ref[...]).astype(BF16)
            hx_scr[...] = h
            hx_ref[...] = h

        p_ref[...] = _dot(hx_scr[...], w_ref[...])

    vec = pl.BlockSpec((1, D), lambda i, j: (0, 0))
    return pl.pallas_call(
        body, name=name,
        grid=(L // tm, D_IN // tn),
        in_specs=[pl.BlockSpec((tm, D), lambda i, j: (i, 0)), vec, vec, vec,
                  pl.BlockSpec((D, tn), lambda i, j: (0, j))],
        out_specs=[pl.BlockSpec((tm, tn), lambda i, j: (i, j)), pl.BlockSpec((tm, D), lambda i, j: (i, 0))],
        out_shape=[jax.ShapeDtypeStruct((L, D_IN), F32), jax.ShapeDtypeStruct((L, D), BF16)],
        scratch_shapes=[pltpu.VMEM((tm, D), BF16)],
        compiler_params=_params("parallel", "arbitrary"),
    )(x, nw, sh, sc, w)


def _w_halves(w_src, col0, tn, wbuf, wsems, pos):
    @pl.when(pos == 0)
    def _():
        for h in range(2):
            pltpu.make_async_copy(w_src.at[:, pl.ds(pl.multiple_of(col0 + h * tn, 128), tn)], wbuf.at[h],
                                  wsems.at[h]).start()

    for h in range(2):
        @pl.when(pos == h)
        def _(h=h):
            pltpu.make_async_copy(w_src.at[:, pl.ds(0, tn)], wbuf.at[h], wsems.at[h]).wait()


def in_proj_own(x, nw, sh, sc, w_shard, shard_arg, name):
    L = x.shape[0]
    tm = min(PROJ_ROWS, L)
    tn = IN_SH // 2
    grid = (L // tm, 2)

    def body(k_ref, x_ref, nw_ref, sh_ref, sc_ref, w_ref, p_ref, hx_ref, wfull_ref, hx_scr, wbuf, wsems, psems,
             *sems):
        pos, total = _grid_step(grid)
        start, forward, finish = _gather_phases([w_ref], [wfull_ref], *sems, relations=(0, 1))
        pl.when(pos == 0)(start)
        _w_halves(w_ref, 0, tn, wbuf, wsems, pos)

        def place(h):
            col = pl.multiple_of(k_ref[0] * IN_SH + h * tn, 128)
            return pltpu.make_async_copy(wbuf.at[h], wfull_ref.at[:, pl.ds(col, tn)], psems.at[h])

        for h in range(2):
            @pl.when(pos == h)
            def _(h=h):
                place(h).start()

        @pl.when(pl.program_id(1) == 0)
        def _():
            xv = x_ref[...]
            n = xv * lax.rsqrt(_lanemean(xv * xv) + EPS) * nw_ref[...]
            h = (n * (1.0 + sc_ref[...]) + sh_ref[...]).astype(BF16)
            hx_scr[...] = h
            hx_ref[...] = h

        p_ref[...] = _dot(hx_scr[...], wbuf[pl.program_id(1)])

        @pl.when(pos == total - 1)
        def _():
            forward()
            finish()
            place(0).wait()
            place(1).wait()

    vec = pl.BlockSpec((1, D), lambda i, j, k: (0, 0))
    return pl.pallas_call(
        body, name=name,
        grid_spec=pltpu.PrefetchScalarGridSpec(
            num_scalar_prefetch=1, grid=grid,
            in_specs=[pl.BlockSpec((tm, D), lambda i, j, k: (i, 0)), vec, vec, vec, ANY],
            out_specs=[pl.BlockSpec((tm, tn), lambda i, j, k: (i, 2 * k[0] + j)),
                       pl.BlockSpec((tm, D), lambda i, j, k: (i, 0)), ANY],
            scratch_shapes=[pltpu.VMEM((tm, D), BF16), pltpu.VMEM((2, D, tn), BF16), pltpu.SemaphoreType.DMA((2,)),
                            pltpu.SemaphoreType.DMA((2,))] + _gather_scratch(1)),
        out_shape=[jax.ShapeDtypeStruct((L, D_IN), F32), jax.ShapeDtypeStruct((L, D), BF16),
                   jax.ShapeDtypeStruct((D, D_IN), BF16)],
        compiler_params=_params("arbitrary", "arbitrary"),
    )(shard_arg, x, nw, sh, sc, w_shard)


def in_proj_next(hx, w_full, shard_arg, p, name, diag_from=None, gather=()):
    L = hx.shape[0]
    tm = min(PROJ_ROWS, L)
    tn = IN_SH // 2
    grid = (L // tm, 2)
    diag = diag_from is not None
    ng = len(gather)
    assert not (diag and ng)

    def body(k_ref, hx_ref, wf_in, p_in, *refs):
        n_src = 1 if diag else ng
        srcs = refs[:n_src]
        p_ref = refs[n_src]
        dsts = refs[n_src + 1:2 * n_src + 1]
        wbuf, wsems = refs[2 * n_src + 1:2 * n_src + 3]
        sems = refs[2 * n_src + 3:2 * n_src + 6]
        stage = refs[2 * n_src + 6:]
        pos, total = _grid_step(grid)
        w_src = dsts[0] if diag else wf_in
        if diag:
            start, forward, finish = _gather_phases(srcs, dsts, *sems, relations=(2,))
        elif ng:
            start, forward, finish = _gather_phases(srcs, dsts, *sems, stage=stage)
        if n_src:
            pl.when(pos == 0)(start)
        _w_halves(w_src, k_ref[0] * IN_SH, tn, wbuf, wsems, pos)
        p_ref[...] = _dot(hx_ref[...], wbuf[pl.program_id(1)])
        if n_src:
            @pl.when(pos == total - 1)
            def _():
                forward()
                finish()

    srcs = [diag_from] if diag else list(gather)
    out_shape = [jax.ShapeDtypeStruct((L, D_IN), F32)]
    if diag:
        out_shape.append(jax.ShapeDtypeStruct(w_full.shape, w_full.dtype))
    out_shape += [jax.ShapeDtypeStruct((N_SHARD,) + s.shape, s.dtype) for s in gather]
    aliases = {3: 0, 2: 1} if diag else {3: 0}
    return pl.pallas_call(
        body, name=name,
        grid_spec=pltpu.PrefetchScalarGridSpec(
            num_scalar_prefetch=1, grid=grid,
            in_specs=[pl.BlockSpec((tm, D), lambda i, j, k: (i, 0)), ANY, ANY] + [ANY] * len(srcs),
            out_specs=[pl.BlockSpec((tm, tn), lambda i, j, k: (i, 2 * k[0] + j))] + [ANY] * len(srcs),
            scratch_shapes=[pltpu.VMEM((2, D, tn), BF16), pltpu.SemaphoreType.DMA((2,))]
            + (_gather_scratch(len(srcs)) if srcs else []) + [pltpu.VMEM(s.shape, s.dtype) for s in gather]),
        out_shape=out_shape,
        input_output_aliases=aliases,
        compiler_params=_params("arbitrary", "arbitrary"),
    )(shard_arg, hx, w_full, p, *srcs)


def _hgrn_gates(z, lb):
    sg = _sigmoid(z)
    sgn = _sigmoid(-z)
    f = lb + (1.0 - lb) * sg
    k = (1.0 - lb) * sgn
    return sg, sgn, f, k


def _tri_chunks(n, chunk, reverse):
    r = lax.broadcasted_iota(jnp.int32, (n, n), 0)
    c = lax.broadcasted_iota(jnp.int32, (n, n), 1)
    same = (r // chunk) == (c // chunk)
    return jnp.where(same & ((r <= c) if reverse else (r >= c)), 1.0, 0.0).astype(F32)


def _decay3(b, reverse, key_major=False):
    C = b.shape[0]
    i0 = lax.broadcasted_iota(jnp.int32, (C, C, 1), 0)
    i1 = lax.broadcasted_iota(jnp.int32, (C, C, 1), 1)
    t, s = (i1, i0) if key_major else (i0, i1)
    mask = (t <= s) if reverse else (t >= s)
    diff = (b[None, :, :] - b[:, None, :]) if key_major else (b[:, None, :] - b[None, :, :])
    return jnp.exp(jnp.where(mask, diff, -jnp.inf))


HG_SUB = 8


def _hgrn_pairs(reverse):
    pairs = []
    size = HG_SUB
    while size < HG_CHUNK:
        for lo in range(0, HG_CHUNK, 2 * size):
            first, second = slice(lo, lo + size), slice(lo + size, lo + 2 * size)
            if reverse:
                pairs.append((first, second, lo + size))
            else:
                pairs.append((second, first, lo + size - 1))
        size *= 2
    return pairs


def _head_mask(g, nq, nk):
    r = lax.broadcasted_iota(jnp.int32, (g * nq, g * nk), 0) // nq
    c = lax.broadcasted_iota(jnp.int32, (g * nq, g * nk), 1) // nk
    return jnp.where(r == c, 1.0, 0.0).astype(F32)


def _hgrn_masks(g, reverse):
    return [_head_mask(g, qr.stop - qr.start, kr.stop - kr.start) for qr, kr, _ in _hgrn_pairs(reverse)]


def _stack(xs):
    return jnp.concatenate(xs, axis=0)


def _unstack(x, g):
    n = x.shape[0] // g
    return [x[h * n:(h + 1) * n] for h in range(g)]


def _add_blocks(acc, rows, part):
    for i in range(part.shape[0] // HG_SUB):
        acc[rows.start // HG_SUB + i] += part[i * HG_SUB:(i + 1) * HG_SUB]


def _hgrn_intra_fwd(qs, ks, vs, bs, masks, reverse):
    g = len(qs)
    blocks = []
    for q, k, v, b in zip(qs, ks, vs, bs):
        mine = []
        for lo in range(0, HG_CHUNK, HG_SUB):
            r = slice(lo, lo + HG_SUB)
            e3 = _decay3(b[r], reverse, key_major=True)
            att3 = jnp.sum(q[r][None, :, :] * k[r][:, None, :] * e3, axis=-1, keepdims=True)
            mine.append(jnp.sum(att3 * v[r][:, None, :], axis=0))
        blocks.append(mine)
    for (qr, kr, ref), mask in zip(_hgrn_pairs(reverse), masks):
        qt = _stack([q[qr] * jnp.exp(b[qr] - b[ref:ref + 1]) for q, b in zip(qs, bs)])
        kt = _stack([k[kr] * jnp.exp(b[ref:ref + 1] - b[kr]) for k, b in zip(ks, bs)])
        att = _bdot(qt, kt, 1, 1) * mask
        for mine, part in zip(blocks, _unstack(_bdot(att, _stack([v[kr] for v in vs])), g)):
            _add_blocks(mine, qr, part)
    return [jnp.concatenate(mine, axis=0) for mine in blocks]


def _hgrn_intra_bwd(qs, ks, vs, bs, d_os, masks, reverse):
    g = len(qs)
    nb = HG_CHUNK // HG_SUB
    dqs, dks, dvs = [], [], []
    for q, k, v, b, d_o in zip(qs, ks, vs, bs, d_os):
        dq, dk, dv = [None] * nb, [None] * nb, [None] * nb
        for i in range(nb):
            r = slice(i * HG_SUB, (i + 1) * HG_SUB)
            e3 = _decay3(b[r], reverse)
            p3 = jnp.sum(d_o[r][:, None, :] * v[r][None, :, :], axis=-1, keepdims=True) * e3
            dq[i] = jnp.sum(p3 * k[r][None, :, :], axis=1)
            dk[i] = jnp.sum(p3 * q[r][:, None, :], axis=0)
            att3 = jnp.sum(q[r][:, None, :] * k[r][None, :, :] * e3, axis=-1, keepdims=True)
            dv[i] = jnp.sum(att3 * d_o[r][:, None, :], axis=0)
        dqs.append(dq)
        dks.append(dk)
        dvs.append(dv)
    for (qr, kr, ref), mask in zip(_hgrn_pairs(reverse), masks):
        fqs = [jnp.exp(b[qr] - b[ref:ref + 1]) for b in bs]
        fks = [jnp.exp(b[ref:ref + 1] - b[kr]) for b in bs]
        qt = _stack([q[qr] * f for q, f in zip(qs, fqs)])
        kt = _stack([k[kr] * f for k, f in zip(ks, fks)])
        do_q = _stack([d_o[qr] for d_o in d_os])
        att = _bdot(qt, kt, 1, 1) * mask
        datt = _bdot(do_q, _stack([v[kr] for v in vs]), 1, 1) * mask
        for dq, part, f in zip(dqs, _unstack(_bdot(datt, kt), g), fqs):
            _add_blocks(dq, qr, part * f)
        for dk, part, f in zip(dks, _unstack(_bdot(datt, qt, 0, 0), g), fks):
            _add_blocks(dk, kr, part * f)
        for dv, part in zip(dvs, _unstack(_bdot(att, do_q, 0, 0), g)):
            _add_blocks(dv, kr, part)

    def cat(parts):
        return [jnp.concatenate(p, axis=0) for p in parts]

    return cat(dqs), cat(dks), cat(dvs)


def _hgrn_state_step(k, v, b, s_t, last):
    b_last = b[last:last + 1]
    return s_t * jnp.exp(b_last) + _bdot(v, k * jnp.exp(b_last - b), 0, 0)


def hgrn_scan_fwd(p, lb, s0, col_z, reverse, name):
    L = p.shape[0]
    nB = L // SCAN_ROWS
    nC = SCAN_ROWS // HG_CHUNK
    C = HG_CHUNK
    G, W = HG_GROUP, HG_GROUP * HG_D
    last = 0 if reverse else C - 1

    def bmap(b):
        return (nB - 1 - b) if reverse else b

    def body(q_ref, z_ref, v_ref, lb_ref, s0_ref, o_ref, sfin_ref, sblk_ref, s_scr, k_scr, b_scr):
        blk = pl.program_id(1)

        @pl.when(blk == 0)
        def _():
            s_scr[...] = s0_ref[...]

        sblk_ref[...] = s_scr[...]
        _, _, f_all, k_all = _hgrn_gates(z_ref[...], lb_ref[...])
        k_scr[...] = k_all
        b_scr[...] = _dot(_tri_chunks(SCAN_ROWS, C, reverse), jnp.log(f_all), prec=CUMSUM_PRECISION)

        masks = _hgrn_masks(G, reverse)
        heads = [slice(j * HG_D, (j + 1) * HG_D) for j in range(G)]

        def chunk(ci, carry):
            c = (nC - 1 - ci) if reverse else ci
            rows = pl.ds(pl.multiple_of(c * C, C), C)
            qs = [q_ref[rows, lanes] * Q_SCALE for lanes in heads]
            vs = [v_ref[rows, lanes] for lanes in heads]
            ks = [k_scr[rows, lanes] for lanes in heads]
            bs = [b_scr[rows, lanes] for lanes in heads]
            o_in = _hgrn_intra_fwd(qs, ks, vs, bs, masks, reverse)
            for j, lanes in enumerate(heads):
                s_t = s_scr[j]
                o_ref[rows, lanes] = o_in[j] + _bdot(qs[j] * jnp.exp(bs[j]), s_t, 1, 1)
                s_scr[j] = _hgrn_state_step(ks[j], vs[j], bs[j], s_t, last)
            return carry

        lax.fori_loop(0, nC, chunk, 0)

        @pl.when(blk == nB - 1)
        def _():
            sfin_ref[...] = s_scr[...]

    def col(c0):
        return pl.BlockSpec((SCAN_ROWS, W), lambda h, b: (bmap(b), c0 // G + h))

    state = pl.BlockSpec((G, HG_D, HG_D), lambda h, b: (h, 0, 0))
    return pl.pallas_call(
        body, name=name,
        grid=(HEADS // G, nB),
        in_specs=[col(COL_HQ), col(col_z), col(COL_HI), pl.BlockSpec((1, W), lambda h, b: (0, h)), state],
        out_specs=[pl.BlockSpec((SCAN_ROWS, W), lambda h, b: (bmap(b), h)), state,
                   pl.BlockSpec((None, G, HG_D, HG_D), lambda h, b: (bmap(b), h, 0, 0))],
        out_shape=[jax.ShapeDtypeStruct((L, D), F32),
                   jax.ShapeDtypeStruct((HEADS, HG_D, HG_D), F32),
                   jax.ShapeDtypeStruct((nB, HEADS, HG_D, HG_D), F32)],
        scratch_shapes=[pltpu.VMEM((G, HG_D, HG_D), F32), pltpu.VMEM((SCAN_ROWS, W), F32),
                        pltpu.VMEM((SCAN_ROWS, W), F32)],
        compiler_params=_params("parallel", "arbitrary"),
    )(p, p, p, lb, s0)


def hgrn_scan_bwd(p, lb, s_blocks, d_o, ds_fin, prev, col_z, reverse, name):
    L = p.shape[0]
    nB = L // SCAN_ROWS
    nC = SCAN_ROWS // HG_CHUNK
    C = HG_CHUNK
    G, W = HG_GROUP, HG_GROUP * HG_D
    last = 0 if reverse else C - 1
    has_prev = prev is not None
    out_dt = BF16 if has_prev else F32

    def bmap(b):
        return b if reverse else (nB - 1 - b)

    def body(*refs):
        q_ref, z_ref, v_ref, lb_ref, sblk_ref, do_ref, dsf_ref = refs[:7]
        refs = refs[7:]
        if has_prev:
            pq_ref, pv_ref = refs[:2]
            refs = refs[2:]
        (dq_ref, dz_ref, dv_ref, dlb_ref, ds0_ref, st_scr, run_scr, ds_scr, k_scr, b_scr, db_scr, dk_scr,
         rf_scr, sgn_scr, dzf_scr) = refs
        blk = pl.program_id(1)

        @pl.when(blk == 0)
        def _():
            ds_scr[...] = dsf_ref[...]
            dlb_ref[...] = jnp.zeros_like(dlb_ref)

        tri = _tri_chunks(SCAN_ROWS, C, reverse)
        row = lax.broadcasted_iota(jnp.int32, (C, HG_D), 0)
        lb_all = lb_ref[...]
        sg_all, sgn_all, f_all, k_all = _hgrn_gates(z_ref[...], lb_all)
        k_scr[...] = k_all
        rf_scr[...] = 1.0 / f_all
        sgn_scr[...] = sgn_all
        dzf_scr[...] = (1.0 - lb_all) * sg_all * sgn_all
        b_scr[...] = _dot(tri, jnp.log(f_all), prec=CUMSUM_PRECISION)
        run_scr[...] = sblk_ref[...]

        def recompute(ci, carry):
            c = (nC - 1 - ci) if reverse else ci
            rows = pl.ds(pl.multiple_of(c * C, C), C)
            for j in range(G):
                lanes = slice(j * HG_D, (j + 1) * HG_D)
                s_t = run_scr[j]
                st_scr[c, j] = s_t
                run_scr[j] = _hgrn_state_step(k_scr[rows, lanes], v_ref[rows, lanes], b_scr[rows, lanes], s_t, last)
            return carry

        lax.fori_loop(0, nC, recompute, 0)

        masks = _hgrn_masks(G, reverse)
        heads = [slice(j * HG_D, (j + 1) * HG_D) for j in range(G)]

        def chunk(ci, carry):
            c = ci if reverse else (nC - 1 - ci)
            rows = pl.ds(pl.multiple_of(c * C, C), C)
            ks = [k_scr[rows, lanes] for lanes in heads]
            bs = [b_scr[rows, lanes] for lanes in heads]
            qs = [q_ref[rows, lanes] * Q_SCALE for lanes in heads]
            vs = [v_ref[rows, lanes] for lanes in heads]
            d_os = [do_ref[rows, lanes] for lanes in heads]
            dq_ins, dk_ins, dv_ins = _hgrn_intra_bwd(qs, ks, vs, bs, d_os, masks, reverse)
            for j, lanes in enumerate(heads):
                k, b, q, v, d_o = ks[j], bs[j], qs[j], vs[j], d_os[j]
                s_t = st_scr[c, j]
                ds_t = ds_scr[j]
                eb = jnp.exp(b)
                b_last = b[last:last + 1]
                eb_last = jnp.exp(b_last)
                kdec = jnp.exp(b_last - b)
                qe = q * eb
                ke = k * kdec
                dq_tot = _bdot(d_o, s_t, 1, 0) * eb + dq_ins[j]
                dke = _bdot(v, ds_t, 1, 0)
                dk_tot = dke * kdec + dk_ins[j]
                dv = dv_ins[j] + _bdot(ke, ds_t, 1, 1)
                db_last = _rowsum(dke * ke) + eb_last * _rowsum(ds_t * s_t)
                db_scr[rows, lanes] = q * dq_tot - k * dk_tot + jnp.where(row == last, db_last, 0.0)
                dk_scr[rows, lanes] = dk_tot
                dq = dq_tot * Q_SCALE
                if has_prev:
                    dq = dq + pq_ref[rows, lanes]
                    dv = dv + pv_ref[rows, lanes]
                dq_ref[rows, lanes] = dq.astype(out_dt)
                dv_ref[rows, lanes] = dv.astype(out_dt)
                ds_scr[j] = ds_t * eb_last + _bdot(d_o, qe, 0, 0)
            return carry

        lax.fori_loop(0, nC, chunk, 0)

        g = _dot(tri, db_scr[...], 0, 0, prec=CUMSUM_PRECISION) * rf_scr[...] - dk_scr[...]
        dz_ref[...] = (g * dzf_scr[...]).astype(BF16)
        dlb_ref[...] += _rowsum(g * sgn_scr[...])

        @pl.when(blk == nB - 1)
        def _():
            ds0_ref[...] = ds_scr[...]

    def col(c0):
        return pl.BlockSpec((SCAN_ROWS, W), lambda h, b: (bmap(b), c0 // G + h))

    tile = pl.BlockSpec((SCAN_ROWS, W), lambda h, b: (bmap(b), h))
    state = pl.BlockSpec((G, HG_D, HG_D), lambda h, b: (h, 0, 0))
    in_specs = [col(COL_HQ), col(col_z), col(COL_HI),
                pl.BlockSpec((1, W), lambda h, b: (0, h)),
                pl.BlockSpec((None, G, HG_D, HG_D), lambda h, b: (bmap(b), h, 0, 0)),
                tile, state]
    args = [p, p, p, lb, s_blocks, d_o, ds_fin]
    if has_prev:
        in_specs += [tile, tile]
        args += list(prev)
    return pl.pallas_call(
        body, name=name,
        grid=(HEADS // G, nB),
        in_specs=in_specs,
        out_specs=[tile, tile, tile, pl.BlockSpec((1, W), lambda h, b: (0, h)), state],
        out_shape=[jax.ShapeDtypeStruct((L, D), out_dt), jax.ShapeDtypeStruct((L, D), BF16),
                   jax.ShapeDtypeStruct((L, D), out_dt), jax.ShapeDtypeStruct((1, D), F32),
                   jax.ShapeDtypeStruct((HEADS, HG_D, HG_D), F32)],
        scratch_shapes=[pltpu.VMEM((nC, G, HG_D, HG_D), F32), pltpu.VMEM((G, HG_D, HG_D), F32),
                        pltpu.VMEM((G, HG_D, HG_D), F32)] + [pltpu.VMEM((SCAN_ROWS, W), F32)] * 7,
        compiler_params=_params("parallel", "arbitrary"),
    )(*args)


def _rope(t, cosf, sinf):
    return t * cosf + pltpu.roll(t, RT_DK // 2, 1) * sinf


def _rope_t(d, cosf, sinf):
    return d * cosf + pltpu.roll(d * sinf, RT_DK // 2, 1)


def _ret_decays(lg, reverse):
    C = SCAN_ROWS
    t = lax.broadcasted_iota(jnp.int32, (C, C), 0)
    s = lax.broadcasted_iota(jnp.int32, (C, C), 1)
    delta = ((s - t) if reverse else (t - s)).astype(F32)
    dmat = jnp.where(delta >= 0, jnp.exp(lg * jnp.maximum(delta, 0.0)), 0.0)
    r = lax.broadcasted_iota(jnp.int32, (C, RT_DK), 0)
    pos = ((C - 1 - r) if reverse else r).astype(F32)
    lg1 = lg[:, :RT_DK]
    qdec = jnp.exp(lg1 * (pos + 1.0))
    kdec = jnp.exp(lg1 * (C - 1.0 - pos))
    sdec = jnp.exp(lg1 * float(C))
    return dmat, delta, pos, qdec, kdec, sdec


def ret_scan_fwd(p, cosf, sinf, lg, s0, reverse, name):
    L = p.shape[0]
    C = SCAN_ROWS
    nB = L // C

    def bmap(b):
        return (nB - 1 - b) if reverse else b

    G = RT_GROUP

    def body(q_ref, k_ref, v_ref, cos_ref, sin_ref, lg_ref, s0_ref, o_ref, sfin_ref, sblk_ref, s_scr):
        blk = pl.program_id(1)

        @pl.when(blk == 0)
        def _():
            s_scr[...] = s0_ref[...]

        sblk_ref[...] = s_scr[...]
        cosf, sinf = cos_ref[...], sin_ref[...]
        for j in range(G):
            lk, lv = slice(j * RT_DK, (j + 1) * RT_DK), slice(j * RT_DV, (j + 1) * RT_DV)
            s_t = s_scr[j]
            dmat, _, _, qdec, kdec, sdec = _ret_decays(lg_ref[j], reverse)
            q = _rope(q_ref[:, lk] * Q_SCALE, cosf, sinf)
            k = _rope(k_ref[:, lk], cosf, sinf)
            v = v_ref[:, lv]
            att = _bdot(q, k, 1, 1) * dmat
            o_ref[:, lv] = _bdot(att, v) + _bdot(q * qdec, s_t, 1, 1)
            s_scr[j] = s_t * sdec + _bdot(v, k * kdec, 0, 0)

        @pl.when(blk == nB - 1)
        def _():
            sfin_ref[...] = s_scr[...]

    def col(c0):
        return pl.BlockSpec((C, G * RT_DK), lambda h, b: (bmap(b), c0 // G + h))

    tab = pl.BlockSpec((C, RT_DK), lambda h, b: (bmap(b), 0))
    state = pl.BlockSpec((G, RT_DV, RT_DK), lambda h, b: (h, 0, 0))
    return pl.pallas_call(
        body, name=name,
        grid=(HEADS // G, nB),
        in_specs=[col(COL_RQ), col(COL_RK),
                  pl.BlockSpec((C, G * RT_DV), lambda h, b: (bmap(b), COL_RV // (2 * G) + h)),
                  tab, tab, pl.BlockSpec((G, 1, RT_DV), lambda h, b: (h, 0, 0)), state],
        out_specs=[pl.BlockSpec((C, G * RT_DV), lambda h, b: (bmap(b), h)), state,
                   pl.BlockSpec((None, G, RT_DV, RT_DK), lambda h, b: (bmap(b), h, 0, 0))],
        out_shape=[jax.ShapeDtypeStruct((L, HEADS * RT_DV), F32),
                   jax.ShapeDtypeStruct((HEADS, RT_DV, RT_DK), F32),
                   jax.ShapeDtypeStruct((nB, HEADS, RT_DV, RT_DK), F32)],
        scratch_shapes=[pltpu.VMEM((G, RT_DV, RT_DK), F32)],
        compiler_params=_params("parallel", "arbitrary"),
    )(p, p, p, cosf, sinf, lg, s0)


def ret_scan_bwd(p, cosf, sinf, lg, s_blocks, d_o, ds_fin, prev, reverse, name):
    L = p.shape[0]
    C = SCAN_ROWS
    nB = L // C
    has_prev = prev is not None
    out_dt = BF16 if has_prev else F32
    G = RT_GROUP

    def bmap(b):
        return b if reverse else (nB - 1 - b)

    def body(*refs):
        q_ref, k_ref, v_ref, cos_ref, sin_ref, lg_ref, sblk_ref, do_ref, dsf_ref = refs[:9]
        refs = refs[9:]
        if has_prev:
            pq_ref, pk_ref, pv_ref = refs[:3]
            refs = refs[3:]
        dq_ref, dk_ref, dv_ref, dlg_ref, ds0_ref, ds_scr = refs
        blk = pl.program_id(1)

        @pl.when(blk == 0)
        def _():
            ds_scr[...] = dsf_ref[...]
            dlg_ref[...] = jnp.zeros_like(dlg_ref)

        cosf, sinf = cos_ref[...], sin_ref[...]
        for j in range(G):
            lk, lv = slice(j * RT_DK, (j + 1) * RT_DK), slice(j * RT_DV, (j + 1) * RT_DV)
            s_t = sblk_ref[j]
            ds_t = ds_scr[j]
            dmat, delta, pos, qdec, kdec, sdec = _ret_decays(lg_ref[j], reverse)
            q = _rope(q_ref[:, lk] * Q_SCALE, cosf, sinf)
            k = _rope(k_ref[:, lk], cosf, sinf)
            v = v_ref[:, lv]
            d_o = do_ref[:, lv]
            att_raw = _bdot(q, k, 1, 1)
            datt_m = _bdot(d_o, v, 1, 1) * dmat
            dqd = _bdot(d_o, s_t, 1, 0)
            dkd = _bdot(v, ds_t, 1, 0)
            dq = _bdot(datt_m, k) + dqd * qdec
            dk = _bdot(datt_m, q, 0, 0) + dkd * kdec
            dv = _bdot(att_raw * dmat, d_o, 0, 0) + _bdot(k * kdec, ds_t, 1, 1)
            ds_scr[j] = ds_t * sdec + _bdot(d_o, q * qdec, 0, 0)
            t1 = jnp.sum(_rowsum(datt_m * att_raw * delta), axis=-1, keepdims=True)
            t23 = jnp.sum(_rowsum((pos + 1.0) * qdec * q * dqd + (C - 1.0 - pos) * kdec * k * dkd), axis=-1, keepdims=True)
            t4 = jnp.sum(_rowsum(ds_t * s_t * sdec), axis=-1, keepdims=True) * float(C)
            dlg_ref[j] += jnp.broadcast_to(t1 + t23 + t4, (1, RT_DK))
            if has_prev:
                dq = _rope_t(dq + pq_ref[:, lk], cosf, sinf) * Q_SCALE
                dk = _rope_t(dk + pk_ref[:, lk], cosf, sinf)
                dv = dv + pv_ref[:, lv]
            dq_ref[:, lk] = dq.astype(out_dt)
            dk_ref[:, lk] = dk.astype(out_dt)
            dv_ref[:, lv] = dv.astype(out_dt)

        @pl.when(blk == nB - 1)
        def _():
            ds0_ref[...] = ds_scr[...]

    def col(c0):
        return pl.BlockSpec((C, G * RT_DK), lambda h, b: (bmap(b), c0 // G + h))

    tab = pl.BlockSpec((C, RT_DK), lambda h, b: (bmap(b), 0))
    state = pl.BlockSpec((G, RT_DV, RT_DK), lambda h, b: (h, 0, 0))
    tk = pl.BlockSpec((C, G * RT_DK), lambda h, b: (bmap(b), h))
    tv = pl.BlockSpec((C, G * RT_DV), lambda h, b: (bmap(b), h))
    in_specs = [col(COL_RQ), col(COL_RK),
                pl.BlockSpec((C, G * RT_DV), lambda h, b: (bmap(b), COL_RV // (2 * G) + h)),
                tab, tab, pl.BlockSpec((G, 1, RT_DV), lambda h, b: (h, 0, 0)),
                pl.BlockSpec((None, G, RT_DV, RT_DK), lambda h, b: (bmap(b), h, 0, 0)),
                tv, state]
    args = [p, p, p, cosf, sinf, lg, s_blocks, d_o, ds_fin]
    if has_prev:
        in_specs += [tk, tk, tv]
        args += list(prev)
    return pl.pallas_call(
        body, name=name,
        grid=(HEADS // G, nB),
        in_specs=in_specs,
        out_specs=[tk, tk, tv, pl.BlockSpec((G, 1, RT_DK), lambda h, b: (h, 0, 0)), state],
        out_shape=[jax.ShapeDtypeStruct((L, D), out_dt), jax.ShapeDtypeStruct((L, D), out_dt),
                   jax.ShapeDtypeStruct((L, HEADS * RT_DV), out_dt),
                   jax.ShapeDtypeStruct((HEADS, 1, RT_DK), F32),
                   jax.ShapeDtypeStruct((HEADS, RT_DV, RT_DK), F32)],
        scratch_shapes=[pltpu.VMEM((G, RT_DV, RT_DK), F32)],
        compiler_params=_params("parallel", "arbitrary"),
    )(*args)


def _silu_parts(h):
    s = _sigmoid(h)
    return h * s, s * (1.0 + h * (1.0 - s))


def _head_rms(o):
    outs, rs = [], []
    for h in range(HEADS):
        oh = o[:, h * HG_D:(h + 1) * HG_D]
        r = lax.rsqrt(_lanemean(oh * oh) + EPS)
        outs.append(oh * r)
        rs.append(r)
    return outs, rs


def _group_norm(o):
    outs, rs = [], []
    for h in range(HEADS):
        oh = o[:, h * RT_DV:(h + 1) * RT_DV]
        c = oh - _lanemean(oh)
        r = lax.rsqrt(_lanemean(c * c) + GN_EPS)
        outs.append(c * r)
        rs.append(r)
    return outs, rs


MIX_ROWS = 256
MIX_BWD_ROWS = 128


def _mix_specs(rows):
    def t(w, c=0):
        return pl.BlockSpec((rows, w), lambda i: (i, c))

    return t


def mix_fwd(ohf, ohb, orf, orb, p, x, g1, hgw, w_pa, w_pb, w_out, name):
    L = x.shape[0]
    t = _mix_specs(MIX_ROWS)

    def body(ohf_ref, ohb_ref, orf_ref, orb_ref, hg_ref, rg0_ref, rg1_ref, ga_ref, gb_ref, x_ref, g1_ref, hgw_ref,
             wpa_ref, wpb_ref, wout_ref, x1_ref, xmix_ref, merged_ref, ya_ref, yb_ref):
        nh, _ = _head_rms(ohf_ref[...] + ohb_ref[...])
        ya = jnp.concatenate(nh, axis=1) * hgw_ref[...] * _silu_parts(hg_ref[...])[0]
        gn, _ = _group_norm(orf_ref[...] + orb_ref[...])
        rg = jnp.concatenate([rg0_ref[...], rg1_ref[...]], axis=1)
        yb = jnp.concatenate(gn, axis=1) * _silu_parts(rg)[0]
        ya16, yb16 = ya.astype(BF16), yb.astype(BF16)
        merged = (_sigmoid(ga_ref[...]) * _dot(ya16, wpa_ref[...])
                  + _sigmoid(gb_ref[...]) * _dot(yb16, wpb_ref[...])).astype(BF16)
        x_mix = _dot(merged, wout_ref[...])
        x1_ref[...] = x_ref[...] + g1_ref[...] * x_mix
        xmix_ref[...] = x_mix
        merged_ref[...] = merged
        ya_ref[...] = ya16
        yb_ref[...] = yb16

    vec = pl.BlockSpec((1, D), lambda i: (0, 0))

    def full(a):
        return pl.BlockSpec(a.shape, lambda i: (0, 0), pipeline_mode=pl.Buffered(1))

    return pl.pallas_call(
        body, name=name,
        grid=(L // MIX_ROWS,),
        in_specs=[t(D), t(D), t(2 * D), t(2 * D), t(D, COL_HG // 8), t(D, COL_RG // 8), t(D, COL_RG // 8 + 1),
                  t(D, COL_GA // 8), t(D, COL_GB // 8), t(D), vec, vec, full(w_pa), full(w_pb), full(w_out)],
        out_specs=[t(D), t(D), t(D), t(D), t(2 * D)],
        out_shape=[jax.ShapeDtypeStruct((L, D), F32), jax.ShapeDtypeStruct((L, D), F32),
                   jax.ShapeDtypeStruct((L, D), BF16), jax.ShapeDtypeStruct((L, D), BF16),
                   jax.ShapeDtypeStruct((L, 2 * D), BF16)],
        compiler_params=_params("parallel"),
    )(ohf, ohb, orf, orb, p, p, p, p, p, x, g1, hgw, w_pa, w_pb, w_out)


def mix_bwd(dx1, x_mix, ya, yb, ohf, ohb, orf, orb, p, g1, hgw, w_pa, w_pb, w_out, name):
    L = dx1.shape[0]
    t = _mix_specs(MIX_BWD_ROWS)

    def body(dx1_ref, xmix_ref, ya_ref, yb_ref, ohf_ref, ohb_ref, orf_ref, orb_ref, hg_ref, rg0_ref, rg1_ref,
             ga_ref, gb_ref, g1_ref, hgw_ref, wpa_ref, wpb_ref, wout_ref,
             dxm_ref, da_ref, db_ref, dga_ref, dgb_ref, dhg_ref, drg_ref, dohg_ref, dort_ref, sums_ref):
        @pl.when(pl.program_id(0) == 0)
        def _():
            sums_ref[...] = jnp.zeros_like(sums_ref)

        dx1 = dx1_ref[...]
        dxm = (g1_ref[...] * dx1).astype(BF16)
        dxm_ref[...] = dxm
        dmerged = _dot(dxm, wout_ref[...], 1, 1)
        a = _dot(ya_ref[...], wpa_ref[...])
        bm = _dot(yb_ref[...], wpb_ref[...])
        sa, sb = _sigmoid(ga_ref[...]), _sigmoid(gb_ref[...])
        d_a = (dmerged * sa).astype(BF16)
        d_b = (dmerged * sb).astype(BF16)
        da_ref[...] = d_a
        db_ref[...] = d_b
        dga_ref[...] = (dmerged * a * sa * (1.0 - sa)).astype(BF16)
        dgb_ref[...] = (dmerged * bm * sb * (1.0 - sb)).astype(BF16)
        dya = _dot(d_a, wpa_ref[...], 1, 1)
        dyb = _dot(d_b, wpb_ref[...], 1, 1)

        hgw = hgw_ref[...]
        silu_h, dsilu_h = _silu_parts(hg_ref[...])
        nh, rh = _head_rms(ohf_ref[...] + ohb_ref[...])
        n = jnp.concatenate(nh, axis=1)
        dhg_ref[...] = (dya * n * hgw * dsilu_h).astype(BF16)
        dn = dya * hgw * silu_h
        douts = []
        for h in range(HEADS):
            dnh = dn[:, h * HG_D:(h + 1) * HG_D]
            douts.append(rh[h] * (dnh - nh[h] * _lanemean(dnh * nh[h])))
        dohg_ref[...] = jnp.concatenate(douts, axis=1)

        rg = jnp.concatenate([rg0_ref[...], rg1_ref[...]], axis=1)
        silu_r, dsilu_r = _silu_parts(rg)
        gn, rr = _group_norm(orf_ref[...] + orb_ref[...])
        g = jnp.concatenate(gn, axis=1)
        drg_ref[...] = (dyb * g * dsilu_r).astype(BF16)
        dgn = dyb * silu_r
        douts = []
        for h in range(HEADS):
            dgh = dgn[:, h * RT_DV:(h + 1) * RT_DV]
            douts.append(rr[h] * (dgh - _lanemean(dgh) - gn[h] * _lanemean(dgh * gn[h])))
        dort_ref[...] = jnp.concatenate(douts, axis=1)

        sums_ref[0:1, :] += _rowsum(dx1 * xmix_ref[...])
        sums_ref[1:2, :] += _rowsum(dya * n * silu_h)

    vec = pl.BlockSpec((1, D), lambda i: (0, 0))

    def full(a):
        return pl.BlockSpec(a.shape, lambda i: (0, 0), pipeline_mode=pl.Buffered(1))

    bf = functools.partial(jax.ShapeDtypeStruct, dtype=BF16)
    return pl.pallas_call(
        body, name=name,
        grid=(L // MIX_BWD_ROWS,),
        in_specs=[t(D), t(D), t(D), t(2 * D), t(D), t(D), t(2 * D), t(2 * D),
                  t(D, COL_HG // 8), t(D, COL_RG // 8), t(D, COL_RG // 8 + 1), t(D, COL_GA // 8), t(D, COL_GB // 8),
                  vec, vec, full(w_pa), full(w_pb), full(w_out)],
        out_specs=[t(D), t(D), t(D), t(D), t(D), t(D), t(2 * D), t(D), t(2 * D),
                   pl.BlockSpec((8, D), lambda i: (0, 0))],
        out_shape=[bf((L, D)), bf((L, D)), bf((L, D)), bf((L, D)), bf((L, D)), bf((L, D)), bf((L, 2 * D)),
                   jax.ShapeDtypeStruct((L, D), F32), jax.ShapeDtypeStruct((L, 2 * D), F32),
                   jax.ShapeDtypeStruct((8, D), F32)],
        compiler_params=_params("arbitrary"),
    )(dx1, x_mix, ya, yb, ohf, ohb, orf, orb, p, p, p, p, p, g1, hgw, w_pa, w_pb, w_out)


FFN_ROWS = 512


def ffn_fwd(x1, target, nw2, sh2, sc2, g2, fw, wg, wu, wd, name):
    L = x1.shape[0]
    tm = min(FFN_ROWS, L)

    def body(x1_ref, tgt_ref, nw2_ref, sh2_ref, sc2_ref, g2_ref, fw_ref, wg_ref, wu_ref, wd_ref,
             hx2_ref, g_ref, u_ref, h_ref, f_ref, dx2_ref, sums_ref, hx_scr, acc):
        i, j = pl.program_id(0), pl.program_id(1)

        @pl.when((i == 0) & (j == 0))
        def _():
            sums_ref[...] = jnp.zeros_like(sums_ref)

        @pl.when(j == 0)
        def _():
            xv = x1_ref[...]
            n = xv * lax.rsqrt(_lanemean(xv * xv) + EPS) * nw2_ref[...]
            h = (n * (1.0 + sc2_ref[...]) + sh2_ref[...]).astype(BF16)
            hx_scr[...] = h
            hx2_ref[...] = h
            acc[...] = jnp.zeros_like(acc)

        hx = hx_scr[...]
        g = _dot(hx, wg_ref[...])
        u = _dot(hx, wu_ref[...])
        hh = (_silu_parts(g)[0] * u).astype(BF16)
        g_ref[...] = g
        u_ref[...] = u
        h_ref[...] = hh
        acc[...] += _dot(hh, wd_ref[...])

        @pl.when(j == N_SHARD - 1)
        def _():
            f = acc[...]
            f_ref[...] = f
            x2 = x1_ref[...] + g2_ref[...] * f
            r = lax.rsqrt(_lanemean(x2 * x2) + EPS)
            fw = fw_ref[...]
            e = x2 * r * fw - tgt_ref[...]
            dy = e * (1.0 / D)
            dyw = dy * fw
            dx2_ref[...] = r * dyw - x2 * (r * r * r) * _lanemean(dyw * x2)
            sums_ref[0:1, :] += _rowsum(dy * x2 * r)
            sums_ref[1:2, :] += _rowsum(e * e) * (0.5 / D)

    row = pl.BlockSpec((tm, D), lambda i, j: (i, 0))
    vec = pl.BlockSpec((1, D), lambda i, j: (0, 0))
    sh = pl.BlockSpec((None, tm, FF_SH), lambda i, j: (j, i, 0))
    return pl.pallas_call(
        body, name=name,
        grid=(L // tm, N_SHARD),
        in_specs=[row, row, vec, vec, vec, vec, vec,
                  pl.BlockSpec((None, D, FF_SH), lambda i, j: (j, 0, 0)),
                  pl.BlockSpec((None, D, FF_SH), lambda i, j: (j, 0, 0)),
                  pl.BlockSpec((None, FF_SH, D), lambda i, j: (j, 0, 0))],
        out_specs=[row, sh, sh, sh, row, row, pl.BlockSpec((8, D), lambda i, j: (0, 0))],
        out_shape=[jax.ShapeDtypeStruct((L, D), BF16),
                   jax.ShapeDtypeStruct((N_SHARD, L, FF_SH), F32), jax.ShapeDtypeStruct((N_SHARD, L, FF_SH), F32),
                   jax.ShapeDtypeStruct((N_SHARD, L, FF_SH), BF16),
                   jax.ShapeDtypeStruct((L, D), F32), jax.ShapeDtypeStruct((L, D), F32),
                   jax.ShapeDtypeStruct((8, D), F32)],
        scratch_shapes=[pltpu.VMEM((tm, D), BF16), pltpu.VMEM((tm, D), F32)],
        compiler_params=_params("arbitrary", "arbitrary"),
    )(x1, target, nw2, sh2, sc2, g2, fw, wg, wu, wd)


def ffn_bwd(dx2, x1, f, g, u, nw2, sc2, g2, wg, wu, wd, name):
    L = x1.shape[0]
    tm = min(FFN_ROWS, L)

    def body(dx2_ref, x1_ref, f_ref, g_ref, u_ref, nw2_ref, sc2_ref, g2_ref, wg_ref, wu_ref, wd_ref,
             df_ref, dg_ref, du_ref, dx1_ref, sums_ref, df_scr, acc):
        i, j = pl.program_id(0), pl.program_id(1)

        @pl.when((i == 0) & (j == 0))
        def _():
            sums_ref[...] = jnp.zeros_like(sums_ref)

        @pl.when(j == 0)
        def _():
            dx2 = dx2_ref[...]
            df = (g2_ref[...] * dx2).astype(BF16)
            df_scr[...] = df
            df_ref[...] = df
            sums_ref[0:1, :] += _rowsum(dx2 * f_ref[...])
            acc[...] = jnp.zeros_like(acc)

        dh = _dot(df_scr[...], wd_ref[...], 1, 1)
        gv, uv = g_ref[...], u_ref[...]
        silu_g, dsilu_g = _silu_parts(gv)
        dg = (dh * uv * dsilu_g).astype(BF16)
        du = (dh * silu_g).astype(BF16)
        dg_ref[...] = dg
        du_ref[...] = du
        acc[...] += _dot(dg, wg_ref[...], 1, 1) + _dot(du, wu_ref[...], 1, 1)

        @pl.when(j == N_SHARD - 1)
        def _():
            dhx = acc[...]
            xv = x1_ref[...]
            r = lax.rsqrt(_lanemean(xv * xv) + EPS)
            n0 = xv * r
            nw = nw2_ref[...]
            dn2 = dhx * (1.0 + sc2_ref[...])
            dn0 = dn2 * nw
            dx1_ref[...] = dx2_ref[...] + r * (dn0 - n0 * _lanemean(dn0 * n0))
            sums_ref[1:2, :] += _rowsum(dhx)
            sums_ref[2:3, :] += _rowsum(dhx * n0 * nw)
            sums_ref[3:4, :] += _rowsum(dn2 * n0)

    row = pl.BlockSpec((tm, D), lambda i, j: (i, 0))
    vec = pl.BlockSpec((1, D), lambda i, j: (0, 0))
    sh = pl.BlockSpec((None, tm, FF_SH), lambda i, j: (j, i, 0))
    return pl.pallas_call(
        body, name=name,
        grid=(L // tm, N_SHARD),
        in_specs=[row, row, row, sh, sh, vec, vec, vec,
                  pl.BlockSpec((None, D, FF_SH), lambda i, j: (j, 0, 0)),
                  pl.BlockSpec((None, D, FF_SH), lambda i, j: (j, 0, 0)),
                  pl.BlockSpec((None, FF_SH, D), lambda i, j: (j, 0, 0))],
        out_specs=[row, sh, sh, row, pl.BlockSpec((8, D), lambda i, j: (0, 0))],
        out_shape=[jax.ShapeDtypeStruct((L, D), BF16),
                   jax.ShapeDtypeStruct((N_SHARD, L, FF_SH), BF16), jax.ShapeDtypeStruct((N_SHARD, L, FF_SH), BF16),
                   jax.ShapeDtypeStruct((L, D), F32), jax.ShapeDtypeStruct((8, D), F32)],
        scratch_shapes=[pltpu.VMEM((tm, D), BF16), pltpu.VMEM((tm, D), F32)],
        compiler_params=_params("arbitrary", "arbitrary"),
    )(dx2, x1, f, g, u, nw2, sc2, g2, wg, wu, wd)


def matmul_tn(a, b, name, acc_init=None, to_chips=()):
    na, K, M = a.shape
    nb, _, N = b.shape
    n = max(na, nb)
    tk = min(512, K)
    tn = N if N <= 1024 else N // 2
    nk = K // tk
    grid = (n, N // tn, nk)
    has_init = acc_init is not None
    nx = len(to_chips)

    def body(a_ref, b_ref, *refs):
        init_ref = refs[0] if has_init else None
        refs = refs[1:] if has_init else refs
        o_ref = refs[nx]
        if nx:
            start, finish = _to_chips_phases(refs[:nx], refs[nx + 1:2 * nx + 1], *refs[2 * nx + 1:])
            pos, total = _grid_step(grid)
            pl.when(pos == 0)(start)
        kk = pl.program_id(2)

        @pl.when(kk == 0)
        def _():
            o_ref[...] = init_ref[...] if has_init else jnp.zeros_like(o_ref)

        o_ref[...] += _dot(a_ref[...], b_ref[...], 0, 0)
        if nx:
            pl.when(pos == total - 1)(finish)

    out_spec = pl.BlockSpec((None, M, tn), lambda s, j, kk: (s, 0, j))
    in_specs = [pl.BlockSpec((None, tk, M), lambda s, j, kk: (s if na > 1 else 0, kk, 0)),
                pl.BlockSpec((None, tk, tn), lambda s, j, kk: (s if nb > 1 else 0, kk, j))]
    args = [a, b]
    if has_init:
        in_specs.append(out_spec)
        args.append(acc_init)
    out = pl.pallas_call(
        body, name=name,
        grid=grid,
        in_specs=in_specs + [ANY] * nx,
        out_specs=[out_spec] + [ANY] * nx,
        out_shape=[jax.ShapeDtypeStruct((n, M, N), F32)] + _to_chips_shapes(to_chips),
        scratch_shapes=_to_chips_scratch(nx) if nx else [],
        compiler_params=_params(*(("arbitrary",) * 3 if nx else ("parallel", "parallel", "arbitrary"))),
    )(*args, *to_chips)
    return out if nx else out[0]


def matmul_tn_pair(a, b1, b2, name):
    K, M = a.shape
    n, _, N = b1.shape
    tk = min(512, K)

    def body(a_ref, b1_ref, b2_ref, o1_ref, o2_ref):
        @pl.when(pl.program_id(1) == 0)
        def _():
            o1_ref[...] = jnp.zeros_like(o1_ref)
            o2_ref[...] = jnp.zeros_like(o2_ref)

        at = a_ref[...].T
        o1_ref[...] += _dot(at, b1_ref[...])
        o2_ref[...] += _dot(at, b2_ref[...])

    b_spec = pl.BlockSpec((None, tk, N), lambda s, kk: (s, kk, 0))
    o_spec = pl.BlockSpec((None, M, N), lambda s, kk: (s, 0, 0))
    return pl.pallas_call(
        body, name=name,
        grid=(n, K // tk),
        in_specs=[pl.BlockSpec((tk, M), lambda s, kk: (kk, 0)), b_spec, b_spec],
        out_specs=[o_spec, o_spec],
        out_shape=[jax.ShapeDtypeStruct((n, M, N), F32)] * 2,
        compiler_params=_params("parallel", "arbitrary"),
    )(a, b1, b2)


PIECE_COLS = 1024
N_PIECE_BLOCKS = D_IN // PIECE_COLS


def _piece_blocks(pieces):
    out, col = [], 0
    for arr, width in pieces:
        if arr is not None:
            out.append((arr, col // PIECE_COLS, width // PIECE_COLS))
        col += width
    assert col == D_IN
    return out


def _piece_feed(p_refs, blocks, buf, sems, tile_of, pos, total):
    def present(blk):
        ok = None
        for _, b0, nb in blocks:
            mine = (blk >= b0) & (blk < b0 + nb)
            ok = mine if ok is None else ok | mine
        return ok

    def fetch(step):
        blk, rows = tile_of(step)
        for p_ref, (_, b0, nb) in zip(p_refs, blocks):
            for t in range(nb):
                @pl.when(blk == b0 + t)
                def _(p_ref=p_ref, t=t):
                    pltpu.make_async_copy(p_ref.at[rows, pl.ds(t * PIECE_COLS, PIECE_COLS)], buf.at[step % 2],
                                          sems.at[step % 2]).start()

    @pl.when(pos == 0)
    def _():
        fetch(pos)

    @pl.when(pos + 1 < total)
    def _():
        fetch(pos + 1)

    def landed():
        slot = pos % 2
        pltpu.make_async_copy(p_refs[0].at[pl.ds(0, buf.shape[1]), pl.ds(0, PIECE_COLS)], buf.at[slot],
                              sems.at[slot]).wait()
        return buf.at[slot]

    return present(tile_of(pos)[0]), landed


def matmul_tn_pieces(a, pieces, name, acc_init=None, to_chips=()):
    K, M = a.shape
    blocks = _piece_blocks(pieces)
    tk = min(1024, K)
    nk = K // tk
    grid = (N_PIECE_BLOCKS, nk)
    has_init = acc_init is not None
    nx, npc = len(to_chips), len(blocks)

    def body(a_ref, *refs):
        p_refs = refs[:npc]
        refs = refs[npc:]
        init_ref = refs[0] if has_init else None
        refs = refs[1:] if has_init else refs
        o_ref = refs[nx]
        buf, sems = refs[2 * nx + 1:2 * nx + 3]
        pos, total = _grid_step(grid)
        if nx:
            start, finish = _to_chips_phases(refs[:nx], refs[nx + 1:2 * nx + 1], *refs[2 * nx + 3:])
            pl.when(pos == 0)(start)
        here, landed = _piece_feed(p_refs, blocks, buf, sems,
                                   lambda s: (s // nk, pl.ds(pl.multiple_of((s % nk) * tk, tk), tk)), pos, total)

        @pl.when(pl.program_id(1) == 0)
        def _():
            o_ref[...] = init_ref[...] if has_init else jnp.zeros_like(o_ref)

        @pl.when(here)
        def _():
            o_ref[...] += _dot(a_ref[...], landed()[...], 0, 0)

        if nx:
            pl.when(pos == total - 1)(finish)

    out_spec = pl.BlockSpec((M, PIECE_COLS), lambda blk, kk: (0, blk))
    in_specs = [pl.BlockSpec((tk, M), lambda blk, kk: (kk, 0))] + [ANY] * npc
    args = [a] + [arr for arr, _, _ in blocks]
    if has_init:
        in_specs.append(out_spec)
        args.append(acc_init)
    out = pl.pallas_call(
        body, name=name,
        grid=grid,
        in_specs=in_specs + [ANY] * nx,
        out_specs=[out_spec] + [ANY] * nx,
        out_shape=[jax.ShapeDtypeStruct((M, D_IN), F32)] + _to_chips_shapes(to_chips),
        scratch_shapes=[pltpu.VMEM((2, tk, PIECE_COLS), BF16), pltpu.SemaphoreType.DMA((2,))]
        + (_to_chips_scratch(nx) if nx else []),
        compiler_params=_params("arbitrary", "arbitrary"),
    )(*args, *to_chips)
    return out if nx else out[0]


def dhx_normbwd(pieces, w, x, dx_res, nw, sc, name, to_chips=()):
    L = x.shape[0]
    tm = min(PROJ_ROWS, L)
    blocks = _piece_blocks(pieces)
    grid = (L // tm, N_PIECE_BLOCKS)
    nx, npc = len(to_chips), len(blocks)

    def body(*refs):
        p_refs = refs[:npc]
        w_ref, x_ref, res_ref, nw_ref, sc_ref = refs[npc:npc + 5]
        refs = refs[npc + 5:]
        dx_ref, sums_ref = refs[nx:nx + 2]
        acc, buf, sems = refs[2 * nx + 2:2 * nx + 5]
        pos, total = _grid_step(grid)
        if nx:
            start, finish = _to_chips_phases(refs[:nx], refs[nx + 2:2 * nx + 2], *refs[2 * nx + 5:])
            pl.when(pos == 0)(start)
            pl.when(pos == total - 1)(finish)
        here, landed = _piece_feed(
            p_refs, blocks, buf, sems,
            lambda s: (s % N_PIECE_BLOCKS, pl.ds(pl.multiple_of((s // N_PIECE_BLOCKS) * tm, tm), tm)), pos, total)
        i, blk = pl.program_id(0), pl.program_id(1)

        @pl.when((i == 0) & (blk == 0))
        def _():
            sums_ref[...] = jnp.zeros_like(sums_ref)

        @pl.when(blk == 0)
        def _():
            acc[...] = jnp.zeros_like(acc)

        @pl.when(here)
        def _():
            acc[...] += _dot(landed()[...], w_ref[...], 1, 1)

        @pl.when(blk == N_PIECE_BLOCKS - 1)
        def _():
            dhx = acc[...]
            xv = x_ref[...]
            r = lax.rsqrt(_lanemean(xv * xv) + EPS)
            n0 = xv * r
            nw = nw_ref[...]
            dn = dhx * (1.0 + sc_ref[...])
            dn0 = dn * nw
            dx_ref[...] = res_ref[...] + r * (dn0 - n0 * _lanemean(dn0 * n0))
            sums_ref[0:1, :] += _rowsum(dhx)
            sums_ref[1:2, :] += _rowsum(dhx * n0 * nw)
            sums_ref[2:3, :] += _rowsum(dn * n0)

    row = pl.BlockSpec((tm, D), lambda i, blk: (i, 0))
    vec = pl.BlockSpec((1, D), lambda i, blk: (0, 0))
    return pl.pallas_call(
        body, name=name,
        grid=grid,
        in_specs=[ANY] * npc + [pl.BlockSpec((D, PIECE_COLS), lambda i, blk: (0, blk)), row, row, vec, vec] + [ANY] * nx,
        out_specs=[row, pl.BlockSpec((8, D), lambda i, blk: (0, 0))] + [ANY] * nx,
        out_shape=[jax.ShapeDtypeStruct((L, D), F32), jax.ShapeDtypeStruct((8, D), F32)] + _to_chips_shapes(to_chips),
        scratch_shapes=[pltpu.VMEM((tm, D), F32), pltpu.VMEM((2, tm, PIECE_COLS), BF16), pltpu.SemaphoreType.DMA((2,))]
        + (_to_chips_scratch(nx) if nx else []),
        compiler_params=_params("arbitrary", "arbitrary"),
    )(*[arr for arr, _, _ in blocks], w, x, dx_res, nw, sc, *to_chips)


SMALL_ROWS = 24


def _rope_tables(L):
    rows = L // 64
    freqs = 10000.0 ** (-jnp.arange(RT_DK // 4, dtype=F32) / (RT_DK // 4))
    a_row = jnp.arange(rows, dtype=F32)[:, None] * freqs
    a_col = jnp.arange(64, dtype=F32)[:, None] * freqs

    def spread(f):
        return jnp.concatenate([jnp.repeat(f(a_row), 64, axis=0), jnp.tile(f(a_col), (rows, 1))], axis=-1)

    cos, sin = spread(jnp.cos), spread(jnp.sin)
    return jnp.concatenate([cos, cos], axis=1), jnp.concatenate([-sin, sin], axis=1)


def _pieces(hq, hf_f, hf_b, hi, hg, rq, rk, rv, rg, ga, gb):
    widths = (D, D, D, D, D, D, D, 2 * D, 2 * D, D, D)
    return list(zip((hq, hf_f, hf_b, hi, hg, rq, rk, rv, rg, ga, gb), widths))


def _lane0(a):
    return a[:, 0, 0]


def _pack_small(rows):
    out = [r.reshape(1, D) for r in rows]
    out += [jnp.zeros((1, D), F32)] * (SMALL_ROWS - len(out))
    return jnp.concatenate(out, axis=0)


def _sibling_sums(gs, names, place):
    core, core_arg, _ = place

    def other_half(g):
        axis = g.ndim - 2
        h = g.shape[axis] // 2
        return lax.dynamic_slice_in_dim(g, (1 - core) * h, h, axis=axis).astype(BF16)

    payload = [other_half(g) for g in gs]
    received = rs_to_sibling(payload, "rs_to_sibling_" + names[0])
    return [rs_add_sibling(g, r, core_arg, "rs_add_sibling_" + k) for g, r, k in zip(gs, received, names)]


def _staged_in_proj(x, nw, sh, sc, w_shard, rest, chip):
    cx, cy = chip // 2, chip % 2

    def arg(k):
        return jnp.reshape(k, (1,)).astype(jnp.int32)

    p, hx, w_full = in_proj_own(x, nw, sh, sc, w_shard, arg(chip), "in_proj_own")
    p, w_full = in_proj_next(hx, w_full, arg(2 * (1 - cx) + cy), p, "in_proj_x", diag_from=w_shard)
    w_pa, w_pb, w_out, w_wd = rest[0], rest[1], rest[2], rest[5]
    p, g_pa, g_pb, g_out, g_wd = in_proj_next(hx, w_full, arg(2 * cx + 1 - cy), p, "in_proj_y",
                                              gather=[w_pa, w_pb, w_out, w_wd])
    p, g_wg, g_wu = in_proj_next(hx, w_full, arg(3 - chip), p, "in_proj_diag", gather=[rest[3], rest[4]])
    w = {"w_in": w_full, "w_pa": g_pa.reshape(D, D), "w_pb": g_pb.reshape(2 * D, D), "w_out": g_out.reshape(D, D),
         "wg": g_wg, "wu": g_wu, "wd": g_wd}
    return p, hx, w


def local_step(x, ctx, target, mod_x, mod_c, lb_f, lb_b, lg_f, lg_b, nw1, nw2, hgw, fw, w, rest=None, place=None):
    L, Lc = x.shape[0], ctx.shape[0]
    sh1, sc1, g1, sh2, sc2, g2 = (mod_x[i:i + 1] for i in range(6))
    sh1c, sc1c = mod_c[0:1], mod_c[1:2]
    cosf, sinf = _rope_tables(L)
    cosc, sinc = jnp.ones((Lc, RT_DK), F32), jnp.zeros((Lc, RT_DK), F32)
    zero_h = jnp.zeros((HEADS, HG_D, HG_D), F32)
    zero_r = jnp.zeros((HEADS, RT_DV, RT_DK), F32)

    if rest is None:
        p, hx = normmod_matmul(x, nw1, sh1, sc1, w["w_in"], "in_proj")
    else:
        p, hx, w = _staged_in_proj(x, nw1, sh1, sc1, w["w_in_shard"], rest, place[2][0])
    pc, hxc = normmod_matmul(ctx, nw1, sh1c, sc1c, w["w_in"], "ctx_in_proj")
    _, s_hf, cb_hf = hgrn_scan_fwd(pc, lb_f, zero_h, COL_HFF, False, "ctx_hgrn_f")
    _, s_hb, cb_hb = hgrn_scan_fwd(pc, lb_b, zero_h, COL_HFB, True, "ctx_hgrn_b")
    _, s_rf, cb_rf = ret_scan_fwd(pc, cosc, sinc, lg_f, zero_r, False, "ctx_ret_f")
    _, s_rb, cb_rb = ret_scan_fwd(pc, cosc, sinc, lg_b, zero_r, True, "ctx_ret_b")
    ohf, _, xb_hf = hgrn_scan_fwd(p, lb_f, s_hf, COL_HFF, False, "hgrn_f")
    ohb, _, xb_hb = hgrn_scan_fwd(p, lb_b, s_hb, COL_HFB, True, "hgrn_b")
    orf, _, xb_rf = ret_scan_fwd(p, cosf, sinf, lg_f, s_rf, False, "ret_f")
    orb, _, xb_rb = ret_scan_fwd(p, cosf, sinf, lg_b, s_rb, True, "ret_b")
    x1, x_mix, merged, ya, yb = mix_fwd(ohf, ohb, orf, orb, p, x, g1, hgw, w["w_pa"], w["w_pb"], w["w_out"], "mix_fwd")
    hx2, gg, uu, hh, ff, dx2, sums_f = ffn_fwd(x1, target, nw2, sh2, sc2, g2, fw, w["wg"], w["wu"], w["wd"], "ffn_fwd")

    d_f, d_g, d_u, dx1, sums_fb = ffn_bwd(dx2, x1, ff, gg, uu, nw2, sc2, g2, w["wg"], w["wu"], w["wd"], "ffn_bwd")
    dw_gate, dw_up = matmul_tn_pair(hx2, d_g, d_u, "dw_ffn_gate_up")
    grads = {"wg": dw_gate, "wu": dw_up, "wd": matmul_tn(hh, d_f[None], "dw_ffn_down")}
    dxm, d_a, d_b, dga, dgb, dhg, drg, dohg, dort, sums_m = mix_bwd(
        dx1, x_mix, ya, yb, ohf, ohb, orf, orb, p, g1, hgw, w["w_pa"], w["w_pb"], w["w_out"], "mix_bwd")
    grads["w_out"] = matmul_tn(merged[None], dxm[None], "dw_out").reshape(N_SHARD, D // N_SHARD, D)
    grads["w_pa"] = matmul_tn(ya[None], d_a[None], "dw_proj_hgrn").reshape(N_SHARD, D // N_SHARD, D)
    grads["w_pb"] = matmul_tn(yb[None], d_b[None], "dw_proj_ret").reshape(N_SHARD, 2 * D // N_SHARD, D)

    rq1, rk1, rv1, dlgf_x, ds_rf = ret_scan_bwd(p, cosf, sinf, lg_f, xb_rf, dort, zero_r, None, False, "ret_f_bwd")
    drq, drk, drv, dlgb_x, ds_rb = ret_scan_bwd(p, cosf, sinf, lg_b, xb_rb, dort, zero_r, (rq1, rk1, rv1), True, "ret_b_bwd")
    hq1, dzf, hv1, dlbf_x, ds_hf = hgrn_scan_bwd(p, lb_f, xb_hf, dohg, zero_h, None, COL_HFF, False, "hgrn_f_bwd")
    dhq, dzb, dhv, dlbb_x, ds_hb = hgrn_scan_bwd(p, lb_b, xb_hb, dohg, zero_h, (hq1, hv1), COL_HFB, True, "hgrn_b_bwd")
    dp = _pieces(dhq, dzf, dzb, dhv, dhg, drq, drk, drv, drg, dga, dgb)
    others = ["w_pa", "w_pb", "w_out", "wg", "wu", "wd"]
    if place is None:
        dw_in = matmul_tn_pieces(hx, dp, "dw_in")
    else:
        sums_o = _sibling_sums([grads[k] for k in others], others, place)
        dw_in, *recv_o = matmul_tn_pieces(hx, dp, "dw_in", to_chips=[a16 for _, a16 in sums_o])

    zc = jnp.zeros((Lc, D), F32)
    zc2 = jnp.zeros((Lc, 2 * D), F32)
    crq1, crk1, crv1, dlgf_c, _ = ret_scan_bwd(pc, cosc, sinc, lg_f, cb_rf, zc2, ds_rf, None, False, "ctx_ret_f_bwd")
    cdrq, cdrk, cdrv, dlgb_c, _ = ret_scan_bwd(pc, cosc, sinc, lg_b, cb_rb, zc2, ds_rb, (crq1, crk1, crv1), True, "ctx_ret_b_bwd")
    chq1, cdzf, chv1, dlbf_c, _ = hgrn_scan_bwd(pc, lb_f, cb_hf, zc, ds_hf, None, COL_HFF, False, "ctx_hgrn_f_bwd")
    cdhq, cdzb, cdhv, dlbb_c, _ = hgrn_scan_bwd(pc, lb_b, cb_hb, zc, ds_hb, (chq1, chv1), COL_HFB, True, "ctx_hgrn_b_bwd")
    dpc = _pieces(cdhq, cdzf, cdzb, cdhv, None, cdrq, cdrk, cdrv, None, None, None)
    _, sums_c = dhx_normbwd(dpc, w["w_in"], ctx, zc, nw1, sc1c, "dctx_in_proj")
    grads["w_in"] = matmul_tn_pieces(hxc, dpc, "dw_in_ctx", acc_init=dw_in)
    if place is None:
        dx, sums_x = dhx_normbwd(dp, w["w_in"], x, dx1, nw1, sc1, "dx_in_proj")
    else:
        sums_i = _sibling_sums([grads["w_in"]], ["w_in"], place)
        dx, sums_x, recv_i = dhx_normbwd(dp, w["w_in"], x, dx1, nw1, sc1, "dx_in_proj", to_chips=[sums_i[0][1]])
        names = ["w_in"] + others
        halves = [rs_add_chips(a, r, place[2], "rs_add_chips_" + k)
                  for (a, _), r, k in zip(sums_i + sums_o, [recv_i] + recv_o, names)]
        grads = dict(zip(names, rs_join_halves(halves, "rs_join_halves")))

    def lg_row(f, b):
        return jnp.concatenate([_lane0(f), _lane0(b), jnp.zeros((D - 2 * HEADS,), F32)])

    small = _pack_small([
        sums_x[0], sums_x[1], sums_m[0], sums_fb[1], sums_fb[2], sums_fb[0],
        sums_c[0], sums_c[1],
        sums_x[2], sums_c[2], sums_fb[3], sums_m[1], sums_f[0],
        dlbf_x, dlbf_c, dlbb_x, dlbb_c,
        lg_row(dlgf_x, dlgb_x), lg_row(dlgf_c, dlgb_c),
        sums_f[1],
    ])
    return dx, grads, small


MESH = pl.DeviceIdType.MESH
ANY = pl.BlockSpec(memory_space=pl.ANY)
N_DEV = 8


def _place():
    return lax.axis_index("x"), lax.axis_index("y"), lax.axis_index("c")


def _other_chips(x, y):
    return [(1 - x, y), (x, 1 - y), (1 - x, 1 - y)]


def allgather8(xs, name):
    m, n = xs.shape

    def body(x_ref, out_ref, send_sems, recv_sems, local_sem):
        x, y, c = _place()
        me, sibling = (x, y, c), (x, y, 1 - c)
        chips = _other_chips(x, y)

        def rows(px, py, pc):
            return out_ref.at[pl.ds((4 * px + 2 * py + pc) * m, m), :]

        def copy(k, block, to, src=None):
            return pltpu.make_async_remote_copy(
                src_ref=rows(*block) if src is None else src, dst_ref=rows(*block),
                send_sem=send_sems.at[k], recv_sem=recv_sems.at[k], device_id=to, device_id_type=MESH)

        mine = pltpu.make_async_copy(x_ref, rows(*me), local_sem)
        mine.start()
        first = [copy(0, me, sibling, src=x_ref)]
        first += [copy(1 + j, me, (*chip, c), src=x_ref) for j, chip in enumerate(chips)]
        for cp in first:
            cp.start()
        passed = [copy(4 + j, (*chip, c), sibling) for j, chip in enumerate(chips)]
        for j, chip in enumerate(chips):
            copy(1 + j, (*chip, c), me).wait_recv()
            passed[j].start()
        copy(0, sibling, me).wait_recv()
        for j, chip in enumerate(chips):
            copy(4 + j, (*chip, 1 - c), me).wait_recv()
        for cp in first + passed:
            cp.wait_send()
        mine.wait()

    return pl.pallas_call(
        body, name=name,
        out_shape=jax.ShapeDtypeStruct((N_DEV * m, n), xs.dtype),
        in_specs=[pl.BlockSpec(memory_space=pltpu.VMEM)],
        out_specs=pl.BlockSpec(memory_space=pltpu.VMEM),
        scratch_shapes=[pltpu.SemaphoreType.DMA((7,)), pltpu.SemaphoreType.DMA((7,)), pltpu.SemaphoreType.DMA],
    )(xs)


def _gather_phases(ins, outs, send_sems, recv_sems, local_sems, relations=(0, 1, 2), stage=None):
    n = len(ins)
    x, y, c = _place()
    chips = _other_chips(x, y)

    def rows(i, core):
        h = ins[i].shape[0] // 2
        return pl.ds(pl.multiple_of(core * h, 16), h)

    def region(i, k, rs):
        if len(outs[i].shape) == 2:
            cols = ins[i].shape[1]
            return outs[i].at[rs, pl.ds(pl.multiple_of(k * cols, 128), cols)]
        return outs[i].at[k, rs, :]

    def landed(i, chip, core):
        return region(i, 2 * chip[0] + chip[1], rows(i, core))

    def copy(i, k, src, dst, to):
        return pltpu.make_async_remote_copy(src_ref=src, dst_ref=dst, send_sem=send_sems.at[6 * i + k],
                                            recv_sem=recv_sems.at[6 * i + k], device_id=to, device_id_type=MESH)

    def lift(i):
        return pltpu.make_async_copy(ins[i], stage[i], local_sems.at[i])

    def drop(i):
        return pltpu.make_async_copy(stage[i], region(i, 2 * x + y, pl.ds(0, ins[i].shape[0])), local_sems.at[i])

    def send(i, j):
        return copy(i, j, ins[i].at[rows(i, c), :], landed(i, (x, y), c), (*chips[j], c))

    def arrived(i, j, core, k):
        return copy(i, k, ins[i].at[rows(i, core), :], landed(i, chips[j], core), (x, y, 1 - c))

    def passed(i, j):
        return copy(i, 3 + j, landed(i, chips[j], c), landed(i, chips[j], c), (x, y, 1 - c))

    def start():
        for i in range(n):
            if stage is not None:
                lift(i).start()
            for j in relations:
                send(i, j).start()

    def forward():
        for i in range(n):
            if stage is not None:
                lift(i).wait()
                drop(i).start()
            for j in relations:
                arrived(i, j, c, j).wait_recv()
                passed(i, j).start()

    def finish():
        for i in range(n):
            for j in relations:
                arrived(i, j, 1 - c, 3 + j).wait_recv()
        for i in range(n):
            for j in relations:
                send(i, j).wait_send()
                passed(i, j).wait_send()
            if stage is not None:
                drop(i).wait()

    return start, forward, finish


def _gather_scratch(n):
    return [pltpu.SemaphoreType.DMA((6 * n,)), pltpu.SemaphoreType.DMA((6 * n,)), pltpu.SemaphoreType.DMA((n,))]


def rs_to_sibling(payloads, name):
    n = len(payloads)

    def body(*refs):
        ins, outs = refs[:n], refs[n:2 * n]
        send_sems, recv_sems = refs[2 * n:]
        x, y, c = _place()
        copies = []
        for i in range(n):
            cp = pltpu.make_async_remote_copy(src_ref=ins[i], dst_ref=outs[i], send_sem=send_sems.at[i],
                                              recv_sem=recv_sems.at[i], device_id=(x, y, 1 - c), device_id_type=MESH)
            cp.start()
            copies.append(cp)
        for cp in copies:
            cp.wait()

    return pl.pallas_call(
        body, name=name,
        out_shape=[jax.ShapeDtypeStruct(g.shape, g.dtype) for g in payloads],
        in_specs=[ANY] * n, out_specs=[ANY] * n,
        scratch_shapes=[pltpu.SemaphoreType.DMA((n,)), pltpu.SemaphoreType.DMA((n,))],
    )(*payloads)


def _to_chips_phases(ins, outs, send_sems, recv_sems):
    def copies():
        x, y, c = _place()
        return [pltpu.make_async_remote_copy(
            src_ref=ins[i].at[2 * px + py], dst_ref=outs[i].at[j], send_sem=send_sems.at[3 * i + j],
            recv_sem=recv_sems.at[3 * i + j], device_id=(px, py, c), device_id_type=MESH)
            for i in range(len(ins)) for j, (px, py) in enumerate(_other_chips(x, y))]

    def start():
        for cp in copies():
            cp.start()

    def finish():
        for cp in copies():
            cp.wait()

    return start, finish


def _to_chips_shapes(parts):
    return [jax.ShapeDtypeStruct((3,) + a.shape[1:], a.dtype) for a in parts]


def _to_chips_scratch(n):
    return [pltpu.SemaphoreType.DMA((3 * n,)), pltpu.SemaphoreType.DMA((3 * n,))]


def rs_join_halves(fulls, name):
    n = len(fulls)

    def body(*refs):
        outs = refs[n:2 * n]
        send_sems, recv_sems = refs[2 * n:]
        x, y, c = _place()

        def copy(i, core):
            h = fulls[i].shape[0] // 2
            rows = outs[i].at[pl.ds(pl.multiple_of(core * h, 8), h), :]
            return pltpu.make_async_remote_copy(src_ref=rows, dst_ref=rows, send_sem=send_sems.at[i],
                                                recv_sem=recv_sems.at[i], device_id=(x, y, 1 - c), device_id_type=MESH)

        sent = [copy(i, c) for i in range(n)]
        for cp in sent:
            cp.start()
        for i in range(n):
            copy(i, 1 - c).wait_recv()
        for cp in sent:
            cp.wait_send()

    return pl.pallas_call(
        body, name=name,
        out_shape=[jax.ShapeDtypeStruct(a.shape, a.dtype) for a in fulls],
        in_specs=[ANY] * n, out_specs=[ANY] * n,
        input_output_aliases={i: i for i in range(n)},
        scratch_shapes=[pltpu.SemaphoreType.DMA((n,)), pltpu.SemaphoreType.DMA((n,))],
    )(*fulls)


def _row_tile(rows, cols, limit_bytes=2 * 1024 * 1024, mult=8):
    best = mult
    for t in range(mult, rows + 1, mult):
        if rows % t == 0 and t * cols * 4 <= limit_bytes:
            best = t
    return best


def rs_add_sibling(g, recv, c, name):
    if g.ndim == 2:
        h, C = recv.shape[0], recv.shape[1] // N_SHARD
    else:
        _, h, C = recv.shape
    tr = _row_tile(h, C, mult=16)
    nt = h // tr

    def body(c_ref, g_ref, r_ref, o_ref, o16_ref):
        s = g_ref[...] + r_ref[...].astype(F32)
        o_ref[...] = s
        o16_ref[...] = s.astype(BF16)

    blk = pl.BlockSpec((None, tr, C), lambda k, i, c_ref: (k, i, 0))
    if g.ndim == 2:
        g_spec = pl.BlockSpec((tr, C), lambda k, i, c_ref: (c_ref[0] * nt + i, k))
        r_spec = pl.BlockSpec((tr, C), lambda k, i, c_ref: (i, k))
    else:
        g_spec = pl.BlockSpec((None, tr, C), lambda k, i, c_ref: (k, c_ref[0] * nt + i, 0))
        r_spec = blk
    return pl.pallas_call(
        body, name=name,
        grid_spec=pltpu.PrefetchScalarGridSpec(
            num_scalar_prefetch=1, grid=(N_SHARD, nt),
            in_specs=[g_spec, r_spec],
            out_specs=[blk, blk]),
        out_shape=[jax.ShapeDtypeStruct((N_SHARD, h, C), F32), jax.ShapeDtypeStruct((N_SHARD, h, C), BF16)],
        compiler_params=_params("parallel", "parallel"),
    )(c, g, recv)


def rs_add_chips(part, recv, place, name):
    _, h, C = part.shape
    tr = _row_tile(h, C, mult=16)
    nt = h // tr

    def body(k_ref, p_ref, r_ref, o_ref):
        o_ref[...] = ((p_ref[...] + r_ref[0].astype(F32)) + r_ref[1].astype(F32)) + r_ref[2].astype(F32)

    return pl.pallas_call(
        body, name=name,
        grid_spec=pltpu.PrefetchScalarGridSpec(
            num_scalar_prefetch=1, grid=(nt,),
            in_specs=[pl.BlockSpec((None, tr, C), lambda i, k_ref: (k_ref[0], i, 0)),
                      pl.BlockSpec((3, tr, C), lambda i, k_ref: (0, i, 0))],
            out_specs=pl.BlockSpec((tr, C), lambda i, k_ref: (k_ref[1] * nt + i, 0))),
        out_shape=jax.ShapeDtypeStruct((2 * h, C), F32),
        compiler_params=_params("parallel"),
    )(place, part, recv)


def _adamw_math(w, g, m, v):
    m = ADAM_B1 * m + (1.0 - ADAM_B1) * g
    v = ADAM_B2 * v + (1.0 - ADAM_B2) * (g * g)
    m_hat = m / (1.0 - ADAM_B1 ** ADAM_STEP)
    v_hat = v / (1.0 - ADAM_B2 ** ADAM_STEP)
    delta = -ADAM_LR * (m_hat / (jnp.sqrt(v_hat) + ADAM_EPS) + ADAM_WD * w)
    return delta, m, v


def adamw(w, g, m, v, name):
    R, C = w.shape
    tr = _row_tile(R, C, 1024 * 1024)

    def body(w_ref, g_ref, m_ref, v_ref, d_ref, nm_ref, nv_ref):
        d_ref[...], nm_ref[...], nv_ref[...] = _adamw_math(w_ref[...], g_ref[...], m_ref[...], v_ref[...])

    blk = pl.BlockSpec((tr, C), lambda i: (i, 0))
    return pl.pallas_call(
        body, name=name, grid=(R // tr,), in_specs=[blk] * 4, out_specs=[blk] * 3,
        out_shape=[jax.ShapeDtypeStruct((R, C), F32)] * 3,
        compiler_params=_params("parallel"),
    )(w, g, m, v)


MOD_SH = 6 * D // N_SHARD
PK_ROWS = 16


def mod_fwd(call16, w_sh, b_sh, name):
    def body(c_ref, w_ref, b_ref, o_ref):
        o_ref[...] = _dot(_silu_parts(c_ref[...])[0], w_ref[...], prec=HI) + b_ref[...]

    return pl.pallas_call(body, name=name, out_shape=jax.ShapeDtypeStruct((16, MOD_SH), F32),
                          compiler_params=_params())(call16, w_sh, b_sh)


def prep_small(lbf2, lbb2, theta_row, name):
    def body(f_ref, b_ref, t_ref, lbf_ref, lbb_ref, lg_ref):
        lbf_ref[...] = _sigmoid(f_ref[0:1, :] - f_ref[1:2, :])
        lbb_ref[...] = _sigmoid(b_ref[0:1, :] - b_ref[1:2, :])
        t = t_ref[...]
        lg_ref[...] = jnp.minimum(t, 0.0) - jnp.log(1.0 + jnp.exp(-jnp.abs(t)))

    row = jax.ShapeDtypeStruct((1, D), F32)
    return pl.pallas_call(body, name=name, out_shape=[row, row, row], compiler_params=_params())(lbf2, lbb2, theta_row)


def small_grads(g3, lbf, lbb, theta_row, name):
    def body(g_ref, lbf_ref, lbb_ref, t_ref, pk_ref, aux_ref):
        s = g_ref[0]
        for d in range(1, N_DEV):
            s = s + g_ref[d]
        pk_ref[...] = jnp.zeros_like(pk_ref)
        aux_ref[...] = jnp.zeros_like(aux_ref)
        pk_ref[1:7, :] = s[0:6]
        pk_ref[1:3, :] += s[6:8]
        pk_ref[7:8, :] = s[8:9] + s[9:10]
        pk_ref[8:9, :] = s[10:11]
        lbf, lbb = lbf_ref[...], lbb_ref[...]
        daf = (s[13:14] + s[14:15]) * lbf * (1.0 - lbf)
        dab = (s[15:16] + s[16:17]) * lbb * (1.0 - lbb)
        pk_ref[9:10, :] = daf
        pk_ref[10:11, :] = -daf
        pk_ref[11:12, :] = dab
        pk_ref[12:13, :] = -dab
        pk_ref[13:14, :] = s[11:12]
        pk_ref[14:15, :] = (s[17:18] + s[18:19]) * _sigmoid(-t_ref[...])
        pk_ref[15:16, :] = s[12:13]
        aux_ref[0:2, :] = s[6:8]
        aux_ref[2:3, :] = jnp.broadcast_to(jnp.sum(s[19:20], axis=-1, keepdims=True), (1, D))

    return pl.pallas_call(body, name=name,
                          out_shape=[jax.ShapeDtypeStruct((PK_ROWS, D), F32), jax.ShapeDtypeStruct((8, D), F32)],
                          compiler_params=_params())(g3, lbf, lbb, theta_row)


def mod_bwd(call16, dmod_sh, w_sh, name):
    def body(c_ref, d_ref, w_ref, dw_ref, ds_ref):
        dm = d_ref[...]
        dw_ref[...] = _dot(_silu_parts(c_ref[...])[0], dm, 0, 0, prec=HI)
        ds_ref[...] = jnp.zeros_like(ds_ref)
        ds_ref[0:1, :] = _dot(dm[8:9, :], w_ref[...], 1, 1, prec=HI)

    return pl.pallas_call(body, name=name,
                          out_shape=[jax.ShapeDtypeStruct((D, MOD_SH), F32), jax.ShapeDtypeStruct((8, D), F32)],
                          compiler_params=_params())(call16, dmod_sh, w_sh)


def adamw_small(g4, pk_g, pk_w, pk_m, pk_v, name):
    def body(g4_ref, g_ref, w_ref, m_ref, v_ref, go_ref, d_ref, nm_ref, nv_ref):
        w = w_ref[...]
        ds = ((g4_ref[0:1, :] + g4_ref[16:17, :]) + g4_ref[32:33, :]) + g4_ref[48:49, :]
        row = lax.broadcasted_iota(jnp.int32, (PK_ROWS, D), 0)
        g = jnp.where(row == 0, ds * _silu_parts(w[0:1, :])[1], g_ref[...])
        go_ref[...] = g
        d_ref[...], nm_ref[...], nv_ref[...] = _adamw_math(w, g, m_ref[...], v_ref[...])

    pk = jax.ShapeDtypeStruct((PK_ROWS, D), F32)
    return pl.pallas_call(body, name=name, out_shape=[pk, pk, pk, pk], compiler_params=_params())(g4, pk_g, pk_w, pk_m, pk_v)


def _pack_params(c_ctx, b_mod, n1, n2, lbf, lbb, hgn, th_f, th_b, fin):
    theta = jnp.concatenate([th_f.reshape(HEADS), th_b.reshape(HEADS), jnp.zeros((D - 2 * HEADS,), F32)])
    return jnp.concatenate([c_ctx.reshape(1, D), b_mod.reshape(6, D), n1.reshape(1, D), n2.reshape(1, D), lbf, lbb,
                            hgn.reshape(1, D), theta.reshape(1, D), fin.reshape(1, D)], axis=0)


def _unpack_params(pk):
    return (pk[0], pk[1:7].reshape(1, 6 * D), pk[7:8], pk[8:9], pk[9:11], pk[11:13], pk[13:14],
            pk[14, 0:HEADS].reshape(1, HEADS), pk[14, HEADS:2 * HEADS].reshape(1, HEADS), pk[15])


def kernel(x, c, ctx, c_ctx, w_mod, b_mod, norm1_w, norm2_w, w_in, hg_lb_fwd, hg_lb_bwd, hg_norm_w, rt_theta_fwd, rt_theta_bwd, w_proj_hgrn, w_proj_ret, w_out, w_ffn_gate, w_ffn_up, w_ffn_down, final_norm_w, loss_target, m_c_ctx, m_w_mod, m_b_mod, m_norm1_w, m_norm2_w, m_w_in, m_hg_lb_fwd, m_hg_lb_bwd, m_hg_norm_w, m_rt_theta_fwd, m_rt_theta_bwd, m_w_proj_hgrn, m_w_proj_ret, m_w_out, m_w_ffn_gate, m_w_ffn_up, m_w_ffn_down, m_final_norm_w, v_c_ctx, v_w_mod, v_b_mod, v_norm1_w, v_norm2_w, v_w_in, v_hg_lb_fwd, v_hg_lb_bwd, v_hg_norm_w, v_rt_theta_fwd, v_rt_theta_bwd, v_w_proj_hgrn, v_w_proj_ret, v_w_out, v_w_ffn_gate, v_w_ffn_up, v_w_ffn_down, v_final_norm_w):
    xi, yi, ci = _place()
    dev = 4 * xi + 2 * yi + ci
    chip = 2 * xi + yi
    core_arg = jnp.reshape(ci, (1,)).astype(jnp.int32)
    place_arg = jnp.stack([chip, ci]).astype(jnp.int32)

    c_all = allgather8(jnp.concatenate([c, jnp.zeros((7, D), F32)], axis=0), "gather_c").reshape(N_DEV, 8, D)[:, 0]
    call16 = jnp.concatenate([c_all, c_ctx.reshape(1, D), jnp.zeros((7, D), F32)], axis=0)
    b_sh = lax.dynamic_slice_in_dim(b_mod, chip * MOD_SH, MOD_SH, axis=1)
    mod_sh = mod_fwd(call16, w_mod[0], b_sh, "mod_fwd")
    mod_g = allgather8(mod_sh, "gather_mod").reshape(N_DEV, 16, MOD_SH)
    mod_all = jnp.concatenate([mod_g[0], mod_g[2], mod_g[4], mod_g[6]], axis=1)
    mod_x = lax.dynamic_index_in_dim(mod_all, dev, axis=0, keepdims=False).reshape(6, D)
    mod_c = mod_all[8].reshape(6, D)

    pk_w = _pack_params(c_ctx, b_mod, norm1_w, norm2_w, hg_lb_fwd, hg_lb_bwd, hg_norm_w, rt_theta_fwd, rt_theta_bwd, final_norm_w)
    theta_row = pk_w[14:15]
    lb_f, lb_b, lg_row = prep_small(hg_lb_fwd, hg_lb_bwd, theta_row, "prep_small")
    lg_f = jnp.broadcast_to(lg_row[0, 0:HEADS].reshape(HEADS, 1, 1), (HEADS, 1, RT_DV))
    lg_b = jnp.broadcast_to(lg_row[0, HEADS:2 * HEADS].reshape(HEADS, 1, 1), (HEADS, 1, RT_DV))

    rest = [s[0].astype(BF16) for s in (w_proj_hgrn, w_proj_ret, w_out, w_ffn_gate, w_ffn_up, w_ffn_down)]

    dx, full, small = local_step(x[0], ctx[0], loss_target[0], mod_x, mod_c, lb_f, lb_b, lg_f, lg_b,
                                 norm1_w, norm2_w, hg_norm_w, final_norm_w.reshape(1, D),
                                 {"w_in_shard": w_in[0].astype(BF16)}, rest, (ci, core_arg, place_arg))

    g3 = allgather8(small, "gather_small").reshape(N_DEV, SMALL_ROWS, D)
    pk_g, aux = small_grads(g3, lb_f, lb_b, theta_row, "small_grads")
    loss = aux[2, 0]
    dmod16 = jnp.concatenate([
        g3[:, 0:6, :].reshape(N_DEV, 6 * D),
        jnp.concatenate([aux[0], aux[1], jnp.zeros((4 * D,), F32)]).reshape(1, 6 * D),
        jnp.zeros((7, 6 * D), F32)], axis=0)
    dmod_sh = lax.dynamic_slice_in_dim(dmod16, chip * MOD_SH, MOD_SH, axis=1)
    g_wmod, dsilu = mod_bwd(call16, dmod_sh, w_mod[0], "mod_bwd")
    g4 = allgather8(dsilu, "gather_dsilu")
    pk_m = _pack_params(m_c_ctx, m_b_mod, m_norm1_w, m_norm2_w, m_hg_lb_fwd, m_hg_lb_bwd, m_hg_norm_w, m_rt_theta_fwd, m_rt_theta_bwd, m_final_norm_w)
    pk_v = _pack_params(v_c_ctx, v_b_mod, v_norm1_w, v_norm2_w, v_hg_lb_fwd, v_hg_lb_bwd, v_hg_norm_w, v_rt_theta_fwd, v_rt_theta_bwd, v_final_norm_w)
    pk_g, pk_d, pk_nm, pk_nv = adamw_small(g4, pk_g, pk_w, pk_m, pk_v, "adamw_small")

    big = {
        "w_mod": (g_wmod, w_mod, m_w_mod, v_w_mod),
        "w_in": (full["w_in"], w_in, m_w_in, v_w_in),
        "w_pa": (full["w_pa"], w_proj_hgrn, m_w_proj_hgrn, v_w_proj_hgrn),
        "w_pb": (full["w_pb"], w_proj_ret, m_w_proj_ret, v_w_proj_ret),
        "w_out": (full["w_out"], w_out, m_w_out, v_w_out),
        "wg": (full["wg"], w_ffn_gate, m_w_ffn_gate, v_w_ffn_gate),
        "wu": (full["wu"], w_ffn_up, m_w_ffn_up, v_w_ffn_up),
        "wd": (full["wd"], w_ffn_down, m_w_ffn_down, v_w_ffn_down),
    }
    res = {}
    for k, (g, wt, mt, vt) in big.items():
        d, nm, nv = adamw(wt[0], g, mt[0], vt[0], "adamw_" + k)
        res[k] = (g[None], d[None], nm[None], nv[None])

    sm = [_unpack_params(p) for p in (pk_g, pk_d, pk_nm, pk_nv)]
    outs = []
    for t in range(4):
        (s_cctx, s_bmod, s_n1, s_n2, s_lbf, s_lbb, s_hgn, s_thf, s_thb, s_fin) = sm[t]
        outs.append([s_cctx, res["w_mod"][t], s_bmod, s_n1, s_n2, res["w_in"][t], s_lbf, s_lbb, s_hgn, s_thf, s_thb,
                     res["w_pa"][t], res["w_pb"][t], res["w_out"][t], res["wg"][t], res["wu"][t], res["wd"][t], s_fin])
    return (loss, dx[None], *outs[0], *outs[1], *outs[2], *outs[3])
```

```python
import functools

import jax
import jax.numpy as jnp
from jax import lax
from jax.experimental import pallas as pl
from jax.experimental.pallas import tpu as pltpu

F32 = jnp.float32
BF16 = jnp.bfloat16
HI = lax.Precision.HIGHEST
CUMSUM_PRECISION = lax.Precision.HIGH

D = 1024
HEADS = 8
HG_D = 128
RT_DK = 128
RT_DV = 256
D_FF = 2816
D_IN = 13312
N_SHARD = 4
IN_SH = D_IN // N_SHARD
FF_SH = D_FF // N_SHARD
HG_CHUNK = 32
SCAN_ROWS = 256
HG_GROUP = 8
RT_GROUP = 4
PROJ_ROWS = 1024
EPS = 1e-6
GN_EPS = 1e-5
Q_SCALE = 128.0 ** -0.5
VMEM_LIMIT = 56 * 1024 * 1024

COL_HQ, COL_HFF, COL_HFB, COL_HI, COL_HG = 0, 8, 16, 24, 32
COL_RQ, COL_RK, COL_RV, COL_RG, COL_GA, COL_GB = 40, 48, 56, 72, 88, 96

ADAM_LR, ADAM_B1, ADAM_B2, ADAM_EPS, ADAM_WD, ADAM_STEP = 0.001, 0.9, 0.999, 1e-08, 0.01, 10


def _params(*sem):
    return pltpu.CompilerParams(dimension_semantics=sem, vmem_limit_bytes=VMEM_LIMIT)


def _dot(a, b, ca=1, cb=0, prec=None):
    return lax.dot_general(a, b, (((ca,), (cb,)), ((), ())), precision=prec, preferred_element_type=F32)


def _bdot(a, b, ca=1, cb=0):
    return _dot(a.astype(BF16), b.astype(BF16), ca, cb)


def _sigmoid(z):
    return 1.0 / (1.0 + jnp.exp(-z))


def _rowsum(a):
    return jnp.sum(a, axis=0, keepdims=True)


def _lanemean(a):
    return jnp.mean(a, axis=-1, keepdims=True)


def _grid_step(grid):
    pos, total = 0, 1
    for d, size in enumerate(grid):
        pos = pos * size + pl.program_id(d)
        total *= size
    return pos, total


def normmod_matmul(x, nw, sh, sc, w, name):
    L = x.shape[0]
    tm = min(PROJ_ROWS, L)
    tn = IN_SH // 2

    def body(x_ref, nw_ref, sh_ref, sc_ref, w_ref, p_ref, hx_ref, hx_scr):
        @pl.when(pl.program_id(1) == 0)
        def _():
            xv = x_ref[...]
            n = xv * lax.rsqrt(_lanemean(xv * xv) + EPS) * nw_ref[...]
            h = (n * (1.0 + sc_ref[...]) + sh_ref[...]).astype(BF16)
            hx_scr[...] = h
            hx_ref[...] = h

        p_ref[...] = _dot(hx_scr[...], w_ref[...])

    vec = pl.BlockSpec((1, D), lambda i, j: (0, 0))
    return pl.pallas_call(
        body, name=name,
        grid=(L // tm, D_IN // tn),
        in_specs=[pl.BlockSpec((tm, D), lambda i, j: (i, 0)), vec, vec, vec,
                  pl.BlockSpec((D, tn), lambda i, j: (0, j))],
        out_specs=[pl.BlockSpec((tm, tn), lambda i, j: (i, j)), pl.BlockSpec((tm, D), lambda i, j: (i, 0))],
        out_shape=[jax.ShapeDtypeStruct((L, D_IN), F32), jax.ShapeDtypeStruct((L, D), BF16)],
        scratch_shapes=[pltpu.VMEM((tm, D), BF16)],
        compiler_params=_params("parallel", "arbitrary"),
    )(x, nw, sh, sc, w)


def _w_halves(w_src, col0, tn, wbuf, wsems, pos):
    @pl.when(pos == 0)
    def _():
        for h in range(2):
            pltpu.make_async_copy(w_src.at[:, pl.ds(pl.multiple_of(col0 + h * tn, 128), tn)], wbuf.at[h],
                                  wsems.at[h]).start()

    for h in range(2):
        @pl.when(pos == h)
        def _(h=h):
            pltpu.make_async_copy(w_src.at[:, pl.ds(0, tn)], wbuf.at[h], wsems.at[h]).wait()


def in_proj_own(x, nw, sh, sc, w_shard, shard_arg, name):
    L = x.shape[0]
    tm = min(PROJ_ROWS, L)
    tn = IN_SH // 2
    grid = (L // tm, 2)

    def body(k_ref, x_ref, nw_ref, sh_ref, sc_ref, w_ref, p_ref, hx_ref, wfull_ref, hx_scr, wbuf, wsems, psems,
             *sems):
        pos, total = _grid_step(grid)
        start, forward, finish = _gather_phases([w_ref], [wfull_ref], *sems, relations=(0, 1))
        pl.when(pos == 0)(start)
        _w_halves(w_ref, 0, tn, wbuf, wsems, pos)

        def place(h):
            col = pl.multiple_of(k_ref[0] * IN_SH + h * tn, 128)
            return pltpu.make_async_copy(wbuf.at[h], wfull_ref.at[:, pl.ds(col, tn)], psems.at[h])

        for h in range(2):
            @pl.when(pos == h)
            def _(h=h):
                place(h).start()

        @pl.when(pl.program_id(1) == 0)
        def _():
            xv = x_ref[...]
            n = xv * lax.rsqrt(_lanemean(xv * xv) + EPS) * nw_ref[...]
            h = (n * (1.0 + sc_ref[...]) + sh_ref[...]).astype(BF16)
            hx_scr[...] = h
            hx_ref[...] = h

        p_ref[...] = _dot(hx_scr[...], wbuf[pl.program_id(1)])

        @pl.when(pos == total - 1)
        def _():
            forward()
            finish()
            place(0).wait()
            place(1).wait()

    vec = pl.BlockSpec((1, D), lambda i, j, k: (0, 0))
    return pl.pallas_call(
        body, name=name,
        grid_spec=pltpu.PrefetchScalarGridSpec(
            num_scalar_prefetch=1, grid=grid,
            in_specs=[pl.BlockSpec((tm, D), lambda i, j, k: (i, 0)), vec, vec, vec, ANY],
            out_specs=[pl.BlockSpec((tm, tn), lambda i, j, k: (i, 2 * k[0] + j)),
                       pl.BlockSpec((tm, D), lambda i, j, k: (i, 0)), ANY],
            scratch_shapes=[pltpu.VMEM((tm, D), BF16), pltpu.VMEM((2, D, tn), BF16), pltpu.SemaphoreType.DMA((2,)),
                            pltpu.SemaphoreType.DMA((2,))] + _gather_scratch(1)),
        out_shape=[jax.ShapeDtypeStruct((L, D_IN), F32), jax.ShapeDtypeStruct((L, D), BF16),
                   jax.ShapeDtypeStruct((D, D_IN), BF16)],
        compiler_params=_params("arbitrary", "arbitrary"),
    )(shard_arg, x, nw, sh, sc, w_shard)


def in_proj_next(hx, w_full, shard_arg, p, name, diag_from=None, gather=()):
    L = hx.shape[0]
    tm = min(PROJ_ROWS, L)
    tn = IN_SH // 2
    grid = (L // tm, 2)
    diag = diag_from is not None
    ng = len(gather)
    assert not (diag and ng)

    def body(k_ref, hx_ref, wf_in, p_in, *refs):
        n_src = 1 if diag else ng
        srcs = refs[:n_src]
        p_ref = refs[n_src]
        dsts = refs[n_src + 1:2 * n_src + 1]
        wbuf, wsems = refs[2 * n_src + 1:2 * n_src + 3]
        sems = refs[2 * n_src + 3:2 * n_src + 6]
        stage = refs[2 * n_src + 6:]
        pos, total = _grid_step(grid)
        w_src = dsts[0] if diag else wf_in
        if diag:
            start, forward, finish = _gather_phases(srcs, dsts, *sems, relations=(2,))
        elif ng:
            start, forward, finish = _gather_phases(srcs, dsts, *sems, stage=stage)
        if n_src:
            pl.when(pos == 0)(start)
        _w_halves(w_src, k_ref[0] * IN_SH, tn, wbuf, wsems, pos)
        p_ref[...] = _dot(hx_ref[...], wbuf[pl.program_id(1)])
        if n_src:
            @pl.when(pos == total - 1)
            def _():
                forward()
                finish()

    srcs = [diag_from] if diag else list(gather)
    out_shape = [jax.ShapeDtypeStruct((L, D_IN), F32)]
    if diag:
        out_shape.append(jax.ShapeDtypeStruct(w_full.shape, w_full.dtype))
    out_shape += [jax.ShapeDtypeStruct((N_SHARD,) + s.shape, s.dtype) for s in gather]
    aliases = {3: 0, 2: 1} if diag else {3: 0}
    return pl.pallas_call(
        body, name=name,
        grid_spec=pltpu.PrefetchScalarGridSpec(
            num_scalar_prefetch=1, grid=grid,
            in_specs=[pl.BlockSpec((tm, D), lambda i, j, k: (i, 0)), ANY, ANY] + [ANY] * len(srcs),
            out_specs=[pl.BlockSpec((tm, tn), lambda i, j, k: (i, 2 * k[0] + j))] + [ANY] * len(srcs),
            scratch_shapes=[pltpu.VMEM((2, D, tn), BF16), pltpu.SemaphoreType.DMA((2,))]
            + (_gather_scratch(len(srcs)) if srcs else []) + [pltpu.VMEM(s.shape, s.dtype) for s in gather]),
        out_shape=out_shape,
        input_output_aliases=aliases,
        compiler_params=_params("arbitrary", "arbitrary"),
    )(shard_arg, hx, w_full, p, *srcs)


def _hgrn_gates(z, lb):
    sg = _sigmoid(z)
    sgn = _sigmoid(-z)
    f = lb + (1.0 - lb) * sg
    k = (1.0 - lb) * sgn
    return sg, sgn, f, k


def _tri_chunks(n, chunk, reverse):
    r = lax.broadcasted_iota(jnp.int32, (n, n), 0)
    c = lax.broadcasted_iota(jnp.int32, (n, n), 1)
    same = (r // chunk) == (c // chunk)
    return jnp.where(same & ((r <= c) if reverse else (r >= c)), 1.0, 0.0).astype(F32)


def _decay3(b, reverse, key_major=False):
    C = b.shape[0]
    i0 = lax.broadcasted_iota(jnp.int32, (C, C, 1), 0)
    i1 = lax.broadcasted_iota(jnp.int32, (C, C, 1), 1)
    t, s = (i1, i0) if key_major else (i0, i1)
    mask = (t <= s) if reverse else (t >= s)
    diff = (b[None, :, :] - b[:, None, :]) if key_major else (b[:, None, :] - b[None, :, :])
    return jnp.exp(jnp.where(mask, diff, -jnp.inf))


HG_SUB = 8


def _hgrn_pairs(reverse):
    pairs = []
    size = HG_SUB
    while size < HG_CHUNK:
        for lo in range(0, HG_CHUNK, 2 * size):
            first, second = slice(lo, lo + size), slice(lo + size, lo + 2 * size)
            if reverse:
                pairs.append((first, second, lo + size))
            else:
                pairs.append((second, first, lo + size - 1))
        size *= 2
    return pairs


def _head_mask(g, nq, nk):
    r = lax.broadcasted_iota(jnp.int32, (g * nq, g * nk), 0) // nq
    c = lax.broadcasted_iota(jnp.int32, (g * nq, g * nk), 1) // nk
    return jnp.where(r == c, 1.0, 0.0).astype(F32)


def _hgrn_masks(g, reverse):
    return [_head_mask(g, qr.stop - qr.start, kr.stop - kr.start) for qr, kr, _ in _hgrn_pairs(reverse)]


def _stack(xs):
    return jnp.concatenate(xs, axis=0)


def _unstack(x, g):
    n = x.shape[0] // g
    return [x[h * n:(h + 1) * n] for h in range(g)]


def _add_blocks(acc, rows, part):
    for i in range(part.shape[0] // HG_SUB):
        acc[rows.start // HG_SUB + i] += part[i * HG_SUB:(i + 1) * HG_SUB]


def _hgrn_intra_fwd(qs, ks, vs, bs, masks, reverse):
    g = len(qs)
    blocks = []
    for q, k, v, b in zip(qs, ks, vs, bs):
        mine = []
        for lo in range(0, HG_CHUNK, HG_SUB):
            r = slice(lo, lo + HG_SUB)
            e3 = _decay3(b[r], reverse, key_major=True)
            att3 = jnp.sum(q[r][None, :, :] * k[r][:, None, :] * e3, axis=-1, keepdims=True)
            mine.append(jnp.sum(att3 * v[r][:, None, :], axis=0))
        blocks.append(mine)
    for (qr, kr, ref), mask in zip(_hgrn_pairs(reverse), masks):
        qt = _stack([q[qr] * jnp.exp(b[qr] - b[ref:ref + 1]) for q, b in zip(qs, bs)])
        kt = _stack([k[kr] * jnp.exp(b[ref:ref + 1] - b[kr]) for k, b in zip(ks, bs)])
        att = _bdot(qt, kt, 1, 1) * mask
        for mine, part in zip(blocks, _unstack(_bdot(att, _stack([v[kr] for v in vs])), g)):
            _add_blocks(mine, qr, part)
    return [jnp.concatenate(mine, axis=0) for mine in blocks]


def _hgrn_intra_bwd(qs, ks, vs, bs, d_os, masks, reverse):
    g = len(qs)
    nb = HG_CHUNK // HG_SUB
    dqs, dks, dvs = [], [], []
    for q, k, v, b, d_o in zip(qs, ks, vs, bs, d_os):
        dq, dk, dv = [None] * nb, [None] * nb, [None] * nb
        for i in range(nb):
            r = slice(i * HG_SUB, (i + 1) * HG_SUB)
            e3 = _decay3(b[r], reverse)
            p3 = jnp.sum(d_o[r][:, None, :] * v[r][None, :, :], axis=-1, keepdims=True) * e3
            dq[i] = jnp.sum(p3 * k[r][None, :, :], axis=1)
            dk[i] = jnp.sum(p3 * q[r][:, None, :], axis=0)
            att3 = jnp.sum(q[r][:, None, :] * k[r][None, :, :] * e3, axis=-1, keepdims=True)
            dv[i] = jnp.sum(att3 * d_o[r][:, None, :], axis=0)
        dqs.append(dq)
        dks.append(dk)
        dvs.append(dv)
    for (qr, kr, ref), mask in zip(_hgrn_pairs(reverse), masks):
        fqs = [jnp.exp(b[qr] - b[ref:ref + 1]) for b in bs]
        fks = [jnp.exp(b[ref:ref + 1] - b[kr]) for b in bs]
        qt = _stack([q[qr] * f for q, f in zip(qs, fqs)])
        kt = _stack([k[kr] * f for k, f in zip(ks, fks)])
        do_q = _stack([d_o[qr] for d_o in d_os])
        att = _bdot(qt, kt, 1, 1) * mask
        datt = _bdot(do_q, _stack([v[kr] for v in vs]), 1, 1) * mask
        for dq, part, f in zip(dqs, _unstack(_bdot(datt, kt), g), fqs):
            _add_blocks(dq, qr, part * f)
        for dk, part, f in zip(dks, _unstack(_bdot(datt, qt, 0, 0), g), fks):
            _add_blocks(dk, kr, part * f)
        for dv, part in zip(dvs, _unstack(_bdot(att, do_q, 0, 0), g)):
            _add_blocks(dv, kr, part)

    def cat(parts):
        return [jnp.concatenate(p, axis=0) for p in parts]

    return cat(dqs), cat(dks), cat(dvs)


def _hgrn_state_step(k, v, b, s_t, last):
    b_last = b[last:last + 1]
    return s_t * jnp.exp(b_last) + _bdot(v, k * jnp.exp(b_last - b), 0, 0)


def hgrn_scan_fwd(p, lb, s0, col_z, reverse, name, prev=None):
    has_prev = prev is not None
    L = p.shape[0]
    nB = L // SCAN_ROWS
    nC = SCAN_ROWS // HG_CHUNK
    C = HG_CHUNK
    G, W = HG_GROUP, HG_GROUP * HG_D
    last = 0 if reverse else C - 1

    def bmap(b):
        return (nB - 1 - b) if reverse else b

    def body(q_ref, z_ref, v_ref, lb_ref, s0_ref, *refs):
        prev_ref = refs[0] if has_prev else None
        o_ref, sfin_ref, sblk_ref, s_scr, k_scr, b_scr = refs[1:] if has_prev else refs
        blk = pl.program_id(1)

        @pl.when(blk == 0)
        def _():
            s_scr[...] = s0_ref[...]

        sblk_ref[...] = s_scr[...]
        _, _, f_all, k_all = _hgrn_gates(z_ref[...], lb_ref[...])
        k_scr[...] = k_all
        b_scr[...] = _dot(_tri_chunks(SCAN_ROWS, C, reverse), jnp.log(f_all), prec=CUMSUM_PRECISION)

        masks = _hgrn_masks(G, reverse)
        heads = [slice(j * HG_D, (j + 1) * HG_D) for j in range(G)]

        def chunk(ci, carry):
            c = (nC - 1 - ci) if reverse else ci
            rows = pl.ds(pl.multiple_of(c * C, C), C)
            qs = [q_ref[rows, lanes] * Q_SCALE for lanes in heads]
            vs = [v_ref[rows, lanes] for lanes in heads]
            ks = [k_scr[rows, lanes] for lanes in heads]
            bs = [b_scr[rows, lanes] for lanes in heads]
            o_in = _hgrn_intra_fwd(qs, ks, vs, bs, masks, reverse)
            for j, lanes in enumerate(heads):
                s_t = s_scr[j]
                o = o_in[j] + _bdot(qs[j] * jnp.exp(bs[j]), s_t, 1, 1)
                o_ref[rows, lanes] = o + prev_ref[rows, lanes] if has_prev else o
                s_scr[j] = _hgrn_state_step(ks[j], vs[j], bs[j], s_t, last)
            return carry

        lax.fori_loop(0, nC, chunk, 0)

        @pl.when(blk == nB - 1)
        def _():
            sfin_ref[...] = s_scr[...]

    def col(c0):
        return pl.BlockSpec((SCAN_ROWS, W), lambda h, b: (bmap(b), c0 // G + h))

    state = pl.BlockSpec((G, HG_D, HG_D), lambda h, b: (h, 0, 0))
    return pl.pallas_call(
        body, name=name,
        grid=(HEADS // G, nB),
        in_specs=[col(COL_HQ), col(col_z), col(COL_HI), pl.BlockSpec((1, W), lambda h, b: (0, h)), state]
        + ([pl.BlockSpec((SCAN_ROWS, W), lambda h, b: (bmap(b), h))] if has_prev else []),
        out_specs=[pl.BlockSpec((SCAN_ROWS, W), lambda h, b: (bmap(b), h)), state,
                   pl.BlockSpec((None, G, HG_D, HG_D), lambda h, b: (bmap(b), h, 0, 0))],
        out_shape=[jax.ShapeDtypeStruct((L, D), F32),
                   jax.ShapeDtypeStruct((HEADS, HG_D, HG_D), F32),
                   jax.ShapeDtypeStruct((nB, HEADS, HG_D, HG_D), F32)],
        scratch_shapes=[pltpu.VMEM((G, HG_D, HG_D), F32), pltpu.VMEM((SCAN_ROWS, W), F32),
                        pltpu.VMEM((SCAN_ROWS, W), F32)],
        compiler_params=_params("parallel", "arbitrary"),
    )(p, p, p, lb, s0, *([prev] if has_prev else []))


def hgrn_scan_bwd(p, lb, s_blocks, d_o, ds_fin, prev, col_z, reverse, name):
    L = p.shape[0]
    nB = L // SCAN_ROWS
    nC = SCAN_ROWS // HG_CHUNK
    C = HG_CHUNK
    G, W = HG_GROUP, HG_GROUP * HG_D
    last = 0 if reverse else C - 1
    has_prev = prev is not None
    out_dt = BF16 if has_prev else F32

    def bmap(b):
        return b if reverse else (nB - 1 - b)

    def body(*refs):
        q_ref, z_ref, v_ref, lb_ref, sblk_ref, do_ref, dsf_ref = refs[:7]
        refs = refs[7:]
        if has_prev:
            pq_ref, pv_ref = refs[:2]
            refs = refs[2:]
        (dq_ref, dz_ref, dv_ref, dlb_ref, ds0_ref, st_scr, run_scr, ds_scr, k_scr, b_scr, db_scr, dk_scr,
         rf_scr, sgn_scr, dzf_scr) = refs
        blk = pl.program_id(1)

        @pl.when(blk == 0)
        def _():
            ds_scr[...] = dsf_ref[...]
            dlb_ref[...] = jnp.zeros_like(dlb_ref)

        tri = _tri_chunks(SCAN_ROWS, C, reverse)
        row = lax.broadcasted_iota(jnp.int32, (C, HG_D), 0)
        lb_all = lb_ref[...]
        sg_all, sgn_all, f_all, k_all = _hgrn_gates(z_ref[...], lb_all)
        k_scr[...] = k_all
        rf_scr[...] = 1.0 / f_all
        sgn_scr[...] = sgn_all
        dzf_scr[...] = (1.0 - lb_all) * sg_all * sgn_all
        b_scr[...] = _dot(tri, jnp.log(f_all), prec=CUMSUM_PRECISION)
        run_scr[...] = sblk_ref[...]

        def recompute(ci, carry):
            c = (nC - 1 - ci) if reverse else ci
            rows = pl.ds(pl.multiple_of(c * C, C), C)
            for j in range(G):
                lanes = slice(j * HG_D, (j + 1) * HG_D)
                s_t = run_scr[j]
                st_scr[c, j] = s_t
                run_scr[j] = _hgrn_state_step(k_scr[rows, lanes], v_ref[rows, lanes], b_scr[rows, lanes], s_t, last)
            return carry

        lax.fori_loop(0, nC, recompute, 0)

        masks = _hgrn_masks(G, reverse)
        heads = [slice(j * HG_D, (j + 1) * HG_D) for j in range(G)]

        def chunk(ci, carry):
            c = ci if reverse else (nC - 1 - ci)
            rows = pl.ds(pl.multiple_of(c * C, C), C)
            ks = [k_scr[rows, lanes] for lanes in heads]
            bs = [b_scr[rows, lanes] for lanes in heads]
            qs = [q_ref[rows, lanes] * Q_SCALE for lanes in heads]
            vs = [v_ref[rows, lanes] for lanes in heads]
            d_os = [do_ref[rows, lanes] for lanes in heads]
            dq_ins, dk_ins, dv_ins = _hgrn_intra_bwd(qs, ks, vs, bs, d_os, masks, reverse)
            for j, lanes in enumerate(heads):
                k, b, q, v, d_o = ks[j], bs[j], qs[j], vs[j], d_os[j]
                s_t = st_scr[c, j]
                ds_t = ds_scr[j]
                eb = jnp.exp(b)
                b_last = b[last:last + 1]
                eb_last = jnp.exp(b_last)
                kdec = jnp.exp(b_last - b)
                qe = q * eb
                ke = k * kdec
                dq_tot = _bdot(d_o, s_t, 1, 0) * eb + dq_ins[j]
                dke = _bdot(v, ds_t, 1, 0)
                dk_tot = dke * kdec + dk_ins[j]
                dv = dv_ins[j] + _bdot(ke, ds_t, 1, 1)
                db_last = _rowsum(dke * ke) + eb_last * _rowsum(ds_t * s_t)
                db_scr[rows, lanes] = q * dq_tot - k * dk_tot + jnp.where(row == last, db_last, 0.0)
                dk_scr[rows, lanes] = dk_tot
                dq = dq_tot * Q_SCALE
                if has_prev:
                    dq = dq + pq_ref[rows, lanes]
                    dv = dv + pv_ref[rows, lanes]
                dq_ref[rows, lanes] = dq.astype(out_dt)
                dv_ref[rows, lanes] = dv.astype(out_dt)
                ds_scr[j] = ds_t * eb_last + _bdot(d_o, qe, 0, 0)
            return carry

        lax.fori_loop(0, nC, chunk, 0)

        g = _dot(tri, db_scr[...], 0, 0, prec=CUMSUM_PRECISION) * rf_scr[...] - dk_scr[...]
        dz_ref[...] = (g * dzf_scr[...]).astype(BF16)
        dlb_ref[...] += _rowsum(g * sgn_scr[...])

        @pl.when(blk == nB - 1)
        def _():
            ds0_ref[...] = ds_scr[...]

    def col(c0):
        return pl.BlockSpec((SCAN_ROWS, W), lambda h, b: (bmap(b), c0 // G + h))

    tile = pl.BlockSpec((SCAN_ROWS, W), lambda h, b: (bmap(b), h))
    state = pl.BlockSpec((G, HG_D, HG_D), lambda h, b: (h, 0, 0))
    in_specs = [col(COL_HQ), col(col_z), col(COL_HI),
                pl.BlockSpec((1, W), lambda h, b: (0, h)),
                pl.BlockSpec((None, G, HG_D, HG_D), lambda h, b: (bmap(b), h, 0, 0)),
                tile, state]
    args = [p, p, p, lb, s_blocks, d_o, ds_fin]
    if has_prev:
        in_specs += [tile, tile]
        args += list(prev)
    return pl.pallas_call(
        body, name=name,
        grid=(HEADS // G, nB),
        in_specs=in_specs,
        out_specs=[tile, tile, tile, pl.BlockSpec((1, W), lambda h, b: (0, h)), state],
        out_shape=[jax.ShapeDtypeStruct((L, D), out_dt), jax.ShapeDtypeStruct((L, D), BF16),
                   jax.ShapeDtypeStruct((L, D), out_dt), jax.ShapeDtypeStruct((1, D), F32),
                   jax.ShapeDtypeStruct((HEADS, HG_D, HG_D), F32)],
        scratch_shapes=[pltpu.VMEM((nC, G, HG_D, HG_D), F32), pltpu.VMEM((G, HG_D, HG_D), F32),
                        pltpu.VMEM((G, HG_D, HG_D), F32)] + [pltpu.VMEM((SCAN_ROWS, W), F32)] * 7,
        compiler_params=_params("parallel", "arbitrary"),
    )(*args)


def _rope(t, cosf, sinf):
    return t * cosf + pltpu.roll(t, RT_DK // 2, 1) * sinf


def _rope_t(d, cosf, sinf):
    return d * cosf + pltpu.roll(d * sinf, RT_DK // 2, 1)


def _ret_decays(lg, reverse):
    C = SCAN_ROWS
    t = lax.broadcasted_iota(jnp.int32, (C, C), 0)
    s = lax.broadcasted_iota(jnp.int32, (C, C), 1)
    delta = ((s - t) if reverse else (t - s)).astype(F32)
    dmat = jnp.where(delta >= 0, jnp.exp(lg * jnp.maximum(delta, 0.0)), 0.0)
    r = lax.broadcasted_iota(jnp.int32, (C, RT_DK), 0)
    pos = ((C - 1 - r) if reverse else r).astype(F32)
    lg1 = lg[:, :RT_DK]
    qdec = jnp.exp(lg1 * (pos + 1.0))
    kdec = jnp.exp(lg1 * (C - 1.0 - pos))
    sdec = jnp.exp(lg1 * float(C))
    return dmat, delta, pos, qdec, kdec, sdec


def ret_scan_fwd(p, cosf, sinf, lg, s0, reverse, name, prev=None):
    has_prev = prev is not None
    L = p.shape[0]
    C = SCAN_ROWS
    nB = L // C

    def bmap(b):
        return (nB - 1 - b) if reverse else b

    G = RT_GROUP

    def body(q_ref, k_ref, v_ref, cos_ref, sin_ref, lg_ref, s0_ref, *refs):
        prev_ref = refs[0] if has_prev else None
        o_ref, sfin_ref, sblk_ref, s_scr = refs[1:] if has_prev else refs
        blk = pl.program_id(1)

        @pl.when(blk == 0)
        def _():
            s_scr[...] = s0_ref[...]

        sblk_ref[...] = s_scr[...]
        cosf, sinf = cos_ref[...], sin_ref[...]
        for j in range(G):
            lk, lv = slice(j * RT_DK, (j + 1) * RT_DK), slice(j * RT_DV, (j + 1) * RT_DV)
            s_t = s_scr[j]
            dmat, _, _, qdec, kdec, sdec = _ret_decays(lg_ref[j], reverse)
            q = _rope(q_ref[:, lk] * Q_SCALE, cosf, sinf)
            k = _rope(k_ref[:, lk], cosf, sinf)
            v = v_ref[:, lv]
            att = _bdot(q, k, 1, 1) * dmat
            o = _bdot(att, v) + _bdot(q * qdec, s_t, 1, 1)
            o_ref[:, lv] = o + prev_ref[:, lv] if has_prev else o
            s_scr[j] = s_t * sdec + _bdot(v, k * kdec, 0, 0)

        @pl.when(blk == nB - 1)
        def _():
            sfin_ref[...] = s_scr[...]

    def col(c0):
        return pl.BlockSpec((C, G * RT_DK), lambda h, b: (bmap(b), c0 // G + h))

    tab = pl.BlockSpec((C, RT_DK), lambda h, b: (bmap(b), 0))
    state = pl.BlockSpec((G, RT_DV, RT_DK), lambda h, b: (h, 0, 0))
    return pl.pallas_call(
        body, name=name,
        grid=(HEADS // G, nB),
        in_specs=[col(COL_RQ), col(COL_RK),
                  pl.BlockSpec((C, G * RT_DV), lambda h, b: (bmap(b), COL_RV // (2 * G) + h)),
                  tab, tab, pl.BlockSpec((G, 1, RT_DV), lambda h, b: (h, 0, 0)), state]
        + ([pl.BlockSpec((C, G * RT_DV), lambda h, b: (bmap(b), h))] if has_prev else []),
        out_specs=[pl.BlockSpec((C, G * RT_DV), lambda h, b: (bmap(b), h)), state,
                   pl.BlockSpec((None, G, RT_DV, RT_DK), lambda h, b: (bmap(b), h, 0, 0))],
        out_shape=[jax.ShapeDtypeStruct((L, HEADS * RT_DV), F32),
                   jax.ShapeDtypeStruct((HEADS, RT_DV, RT_DK), F32),
                   jax.ShapeDtypeStruct((nB, HEADS, RT_DV, RT_DK), F32)],
        scratch_shapes=[pltpu.VMEM((G, RT_DV, RT_DK), F32)],
        compiler_params=_params("parallel", "arbitrary"),
    )(p, p, p, cosf, sinf, lg, s0, *([prev] if has_prev else []))


def ret_scan_bwd(p, cosf, sinf, lg, s_blocks, d_o, ds_fin, prev, reverse, name):
    L = p.shape[0]
    C = SCAN_ROWS
    nB = L // C
    has_prev = prev is not None
    out_dt = BF16
    G = RT_GROUP

    def bmap(b):
        return b if reverse else (nB - 1 - b)

    def body(*refs):
        q_ref, k_ref, v_ref, cos_ref, sin_ref, lg_ref, sblk_ref, do_ref, dsf_ref = refs[:9]
        refs = refs[9:]
        if has_prev:
            pq_ref, pk_ref, pv_ref = refs[:3]
            refs = refs[3:]
        dq_ref, dk_ref, dv_ref, dlg_ref, ds0_ref, ds_scr = refs
        blk = pl.program_id(1)

        @pl.when(blk == 0)
        def _():
            ds_scr[...] = dsf_ref[...]
            dlg_ref[...] = jnp.zeros_like(dlg_ref)

        cosf, sinf = cos_ref[...], sin_ref[...]
        for j in range(G):
            lk, lv = slice(j * RT_DK, (j + 1) * RT_DK), slice(j * RT_DV, (j + 1) * RT_DV)
            s_t = sblk_ref[j]
            ds_t = ds_scr[j]
            dmat, delta, pos, qdec, kdec, sdec = _ret_decays(lg_ref[j], reverse)
            q = _rope(q_ref[:, lk] * Q_SCALE, cosf, sinf)
            k = _rope(k_ref[:, lk], cosf, sinf)
            v = v_ref[:, lv]
            d_o = do_ref[:, lv]
            att_raw = _bdot(q, k, 1, 1)
            datt_m = _bdot(d_o, v, 1, 1) * dmat
            dqd = _bdot(d_o, s_t, 1, 0)
            dkd = _bdot(v, ds_t, 1, 0)
            dq = _bdot(datt_m, k) + dqd * qdec
            dk = _bdot(datt_m, q, 0, 0) + dkd * kdec
            dv = _bdot(att_raw * dmat, d_o, 0, 0) + _bdot(k * kdec, ds_t, 1, 1)
            ds_scr[j] = ds_t * sdec + _bdot(d_o, q * qdec, 0, 0)
            t1 = jnp.sum(_rowsum(datt_m * att_raw * delta), axis=-1, keepdims=True)
            t23 = jnp.sum(_rowsum((pos + 1.0) * qdec * q * dqd + (C - 1.0 - pos) * kdec * k * dkd), axis=-1, keepdims=True)
            t4 = jnp.sum(_rowsum(ds_t * s_t * sdec), axis=-1, keepdims=True) * float(C)
            dlg_ref[j] += jnp.broadcast_to(t1 + t23 + t4, (1, RT_DK))
            if has_prev:
                dq = _rope_t(dq + pq_ref[:, lk].astype(F32), cosf, sinf) * Q_SCALE
                dk = _rope_t(dk + pk_ref[:, lk].astype(F32), cosf, sinf)
                dv = dv + pv_ref[:, lv].astype(F32)
            dq_ref[:, lk] = dq.astype(out_dt)
            dk_ref[:, lk] = dk.astype(out_dt)
            dv_ref[:, lv] = dv.astype(out_dt)

        @pl.when(blk == nB - 1)
        def _():
            ds0_ref[...] = ds_scr[...]

    def col(c0):
        return pl.BlockSpec((C, G * RT_DK), lambda h, b: (bmap(b), c0 // G + h))

    tab = pl.BlockSpec((C, RT_DK), lambda h, b: (bmap(b), 0))
    state = pl.BlockSpec((G, RT_DV, RT_DK), lambda h, b: (h, 0, 0))
    tk = pl.BlockSpec((C, G * RT_DK), lambda h, b: (bmap(b), h))
    tv = pl.BlockSpec((C, G * RT_DV), lambda h, b: (bmap(b), h))
    in_specs = [col(COL_RQ), col(COL_RK),
                pl.BlockSpec((C, G * RT_DV), lambda h, b: (bmap(b), COL_RV // (2 * G) + h)),
                tab, tab, pl.BlockSpec((G, 1, RT_DV), lambda h, b: (h, 0, 0)),
                pl.BlockSpec((None, G, RT_DV, RT_DK), lambda h, b: (bmap(b), h, 0, 0)),
                tv, state]
    args = [p, p, p, cosf, sinf, lg, s_blocks, d_o, ds_fin]
    if has_prev:
        in_specs += [tk, tk, tv]
        args += list(prev)
    return pl.pallas_call(
        body, name=name,
        grid=(HEADS // G, nB),
        in_specs=in_specs,
        out_specs=[tk, tk, tv, pl.BlockSpec((G, 1, RT_DK), lambda h, b: (h, 0, 0)), state],
        out_shape=[jax.ShapeDtypeStruct((L, D), out_dt), jax.ShapeDtypeStruct((L, D), out_dt),
                   jax.ShapeDtypeStruct((L, HEADS * RT_DV), out_dt),
                   jax.ShapeDtypeStruct((HEADS, 1, RT_DK), F32),
                   jax.ShapeDtypeStruct((HEADS, RT_DV, RT_DK), F32)],
        scratch_shapes=[pltpu.VMEM((G, RT_DV, RT_DK), F32)],
        compiler_params=_params("parallel", "arbitrary"),
    )(*args)


def _silu_parts(h):
    s = _sigmoid(h)
    return h * s, s * (1.0 + h * (1.0 - s))


def _head_rms(o):
    outs, rs = [], []
    for h in range(HEADS):
        oh = o[:, h * HG_D:(h + 1) * HG_D]
        r = lax.rsqrt(_lanemean(oh * oh) + EPS)
        outs.append(oh * r)
        rs.append(r)
    return outs, rs


def _group_norm(o):
    outs, rs = [], []
    for h in range(HEADS):
        oh = o[:, h * RT_DV:(h + 1) * RT_DV]
        c = oh - _lanemean(oh)
        r = lax.rsqrt(_lanemean(c * c) + GN_EPS)
        outs.append(c * r)
        rs.append(r)
    return outs, rs


MIX_ROWS = 256


def _mix_specs(rows):
    def t(w, c=0):
        return pl.BlockSpec((rows, w), lambda i: (i, c))

    return t


def mix_fwd(o_hg, o_rt, p, x, g1, hgw, w_pa, w_pb, w_out, name):
    L = x.shape[0]
    t = _mix_specs(MIX_ROWS)

    def body(ohg_ref, ort_ref, hg_ref, rg0_ref, rg1_ref, ga_ref, gb_ref, x_ref, g1_ref, hgw_ref,
             wpa_ref, wpb_ref, wout_ref, x1_ref, xmix_ref, merged_ref, ya_ref, yb_ref):
        nh, _ = _head_rms(ohg_ref[...])
        ya = jnp.concatenate(nh, axis=1) * hgw_ref[...] * _silu_parts(hg_ref[...])[0]
        gn, _ = _group_norm(ort_ref[...])
        rg = jnp.concatenate([rg0_ref[...], rg1_ref[...]], axis=1)
        yb = jnp.concatenate(gn, axis=1) * _silu_parts(rg)[0]
        ya16, yb16 = ya.astype(BF16), yb.astype(BF16)
        merged = (_sigmoid(ga_ref[...]) * _dot(ya16, wpa_ref[...])
                  + _sigmoid(gb_ref[...]) * _dot(yb16, wpb_ref[...])).astype(BF16)
        x_mix = _dot(merged, wout_ref[...])
        x1_ref[...] = x_ref[...] + g1_ref[...] * x_mix
        xmix_ref[...] = x_mix
        merged_ref[...] = merged
        ya_ref[...] = ya16
        yb_ref[...] = yb16

    vec = pl.BlockSpec((1, D), lambda i: (0, 0))

    def full(a):
        return pl.BlockSpec(a.shape, lambda i: (0, 0), pipeline_mode=pl.Buffered(1))

    return pl.pallas_call(
        body, name=name,
        grid=(L // MIX_ROWS,),
        in_specs=[t(D), t(2 * D), t(D, COL_HG // 8), t(D, COL_RG // 8), t(D, COL_RG // 8 + 1),
                  t(D, COL_GA // 8), t(D, COL_GB // 8), t(D), vec, vec, full(w_pa), full(w_pb), full(w_out)],
        out_specs=[t(D), t(D), t(D), t(D), t(2 * D)],
        out_shape=[jax.ShapeDtypeStruct((L, D), F32), jax.ShapeDtypeStruct((L, D), F32),
                   jax.ShapeDtypeStruct((L, D), BF16), jax.ShapeDtypeStruct((L, D), BF16),
                   jax.ShapeDtypeStruct((L, 2 * D), BF16)],
        compiler_params=_params("parallel"),
    )(o_hg, o_rt, p, p, p, p, p, x, g1, hgw, w_pa, w_pb, w_out)


def mix_bwd(dx1, x_mix, ya, yb, o_hg, o_rt, p, g1, hgw, w_pa, w_pb, w_out, name):
    L = dx1.shape[0]
    t = _mix_specs(MIX_ROWS)

    def body(dx1_ref, xmix_ref, ya_ref, yb_ref, ohg_ref, ort_ref, hg_ref, rg0_ref, rg1_ref,
             ga_ref, gb_ref, g1_ref, hgw_ref, wpa_ref, wpb_ref, wout_ref,
             dxm_ref, da_ref, db_ref, dga_ref, dgb_ref, dhg_ref, drg_ref, dohg_ref, dort_ref, sums_ref):
        @pl.when(pl.program_id(0) == 0)
        def _():
            sums_ref[...] = jnp.zeros_like(sums_ref)

        dx1 = dx1_ref[...]
        dxm = (g1_ref[...] * dx1).astype(BF16)
        dxm_ref[...] = dxm
        dmerged = _dot(dxm, wout_ref[...], 1, 1)
        a = _dot(ya_ref[...], wpa_ref[...])
        bm = _dot(yb_ref[...], wpb_ref[...])
        sa, sb = _sigmoid(ga_ref[...]), _sigmoid(gb_ref[...])
        d_a = (dmerged * sa).astype(BF16)
        d_b = (dmerged * sb).astype(BF16)
        da_ref[...] = d_a
        db_ref[...] = d_b
        dga_ref[...] = (dmerged * a * sa * (1.0 - sa)).astype(BF16)
        dgb_ref[...] = (dmerged * bm * sb * (1.0 - sb)).astype(BF16)
        dya = _dot(d_a, wpa_ref[...], 1, 1)
        dyb = _dot(d_b, wpb_ref[...], 1, 1)

        hgw = hgw_ref[...]
        silu_h, dsilu_h = _silu_parts(hg_ref[...])
        nh, rh = _head_rms(ohg_ref[...])
        n = jnp.concatenate(nh, axis=1)
        dhg_ref[...] = (dya * n * hgw * dsilu_h).astype(BF16)
        dn = dya * hgw * silu_h
        douts = []
        for h in range(HEADS):
            dnh = dn[:, h * HG_D:(h + 1) * HG_D]
            douts.append(rh[h] * (dnh - nh[h] * _lanemean(dnh * nh[h])))
        dohg_ref[...] = jnp.concatenate(douts, axis=1)

        rg = jnp.concatenate([rg0_ref[...], rg1_ref[...]], axis=1)
        silu_r, dsilu_r = _silu_parts(rg)
        gn, rr = _group_norm(ort_ref[...])
        g = jnp.concatenate(gn, axis=1)
        drg_ref[...] = (dyb * g * dsilu_r).astype(BF16)
        dgn = dyb * silu_r
        douts = []
        for h in range(HEADS):
            dgh = dgn[:, h * RT_DV:(h + 1) * RT_DV]
            douts.append(rr[h] * (dgh - _lanemean(dgh) - gn[h] * _lanemean(dgh * gn[h])))
        dort_ref[...] = jnp.concatenate(douts, axis=1)

        sums_ref[0:1, :] += _rowsum(dx1 * xmix_ref[...])
        sums_ref[1:2, :] += _rowsum(dya * n * silu_h)

    vec = pl.BlockSpec((1, D), lambda i: (0, 0))

    def full(a):
        return pl.BlockSpec(a.shape, lambda i: (0, 0), pipeline_mode=pl.Buffered(1))

    bf = functools.partial(jax.ShapeDtypeStruct, dtype=BF16)
    return pl.pallas_call(
        body, name=name,
        grid=(L // MIX_ROWS,),
        in_specs=[t(D), t(D), t(D), t(2 * D), t(D), t(2 * D),
                  t(D, COL_HG // 8), t(D, COL_RG // 8), t(D, COL_RG // 8 + 1), t(D, COL_GA // 8), t(D, COL_GB // 8),
                  vec, vec, full(w_pa), full(w_pb), full(w_out)],
        out_specs=[t(D), t(D), t(D), t(D), t(D), t(D), t(2 * D), t(D), t(2 * D),
                   pl.BlockSpec((8, D), lambda i: (0, 0))],
        out_shape=[bf((L, D)), bf((L, D)), bf((L, D)), bf((L, D)), bf((L, D)), bf((L, D)), bf((L, 2 * D)),
                   jax.ShapeDtypeStruct((L, D), F32), jax.ShapeDtypeStruct((L, 2 * D), F32),
                   jax.ShapeDtypeStruct((8, D), F32)],
        compiler_params=_params("arbitrary"),
    )(dx1, x_mix, ya, yb, o_hg, o_rt, p, p, p, p, p, g1, hgw, w_pa, w_pb, w_out)


FFN_ROWS = 512


def ffn_fwd(x1, target, nw2, sh2, sc2, g2, fw, wg, wu, wd, name):
    L = x1.shape[0]
    tm = min(FFN_ROWS, L)

    def body(x1_ref, tgt_ref, nw2_ref, sh2_ref, sc2_ref, g2_ref, fw_ref, wg_ref, wu_ref, wd_ref,
             hx2_ref, g_ref, u_ref, h_ref, f_ref, dx2_ref, sums_ref, hx_scr, acc):
        i, j = pl.program_id(0), pl.program_id(1)

        @pl.when((i == 0) & (j == 0))
        def _():
            sums_ref[...] = jnp.zeros_like(sums_ref)

        @pl.when(j == 0)
        def _():
            xv = x1_ref[...]
            n = xv * lax.rsqrt(_lanemean(xv * xv) + EPS) * nw2_ref[...]
            h = (n * (1.0 + sc2_ref[...]) + sh2_ref[...]).astype(BF16)
            hx_scr[...] = h
            hx2_ref[...] = h
            acc[...] = jnp.zeros_like(acc)

        hx = hx_scr[...]
        g = _dot(hx, wg_ref[...])
        u = _dot(hx, wu_ref[...])
        hh = (_silu_parts(g)[0] * u).astype(BF16)
        g_ref[...] = g
        u_ref[...] = u
        h_ref[...] = hh
        acc[...] += _dot(hh, wd_ref[...])

        @pl.when(j == N_SHARD - 1)
        def _():
            f = acc[...]
            f_ref[...] = f
            x2 = x1_ref[...] + g2_ref[...] * f
            r = lax.rsqrt(_lanemean(x2 * x2) + EPS)
            fw = fw_ref[...]
            e = x2 * r * fw - tgt_ref[...]
            dy = e * (1.0 / D)
            dyw = dy * fw
            dx2_ref[...] = r * dyw - x2 * (r * r * r) * _lanemean(dyw * x2)
            sums_ref[0:1, :] += _rowsum(dy * x2 * r)
            sums_ref[1:2, :] += _rowsum(e * e) * (0.5 / D)

    row = pl.BlockSpec((tm, D), lambda i, j: (i, 0))
    vec = pl.BlockSpec((1, D), lambda i, j: (0, 0))
    sh = pl.BlockSpec((None, tm, FF_SH), lambda i, j: (j, i, 0))
    return pl.pallas_call(
        body, name=name,
        grid=(L // tm, N_SHARD),
        in_specs=[row, row, vec, vec, vec, vec, vec,
                  pl.BlockSpec((None, D, FF_SH), lambda i, j: (j, 0, 0)),
                  pl.BlockSpec((None, D, FF_SH), lambda i, j: (j, 0, 0)),
                  pl.BlockSpec((None, FF_SH, D), lambda i, j: (j, 0, 0))],
        out_specs=[row, sh, sh, sh, row, row, pl.BlockSpec((8, D), lambda i, j: (0, 0))],
        out_shape=[jax.ShapeDtypeStruct((L, D), BF16),
                   jax.ShapeDtypeStruct((N_SHARD, L, FF_SH), F32), jax.ShapeDtypeStruct((N_SHARD, L, FF_SH), F32),
                   jax.ShapeDtypeStruct((N_SHARD, L, FF_SH), BF16),
                   jax.ShapeDtypeStruct((L, D), F32), jax.ShapeDtypeStruct((L, D), F32),
                   jax.ShapeDtypeStruct((8, D), F32)],
        scratch_shapes=[pltpu.VMEM((tm, D), BF16), pltpu.VMEM((tm, D), F32)],
        compiler_params=_params("arbitrary", "arbitrary"),
    )(x1, target, nw2, sh2, sc2, g2, fw, wg, wu, wd)


def ffn_bwd(dx2, x1, f, g, u, nw2, sc2, g2, wg, wu, wd, name):
    L = x1.shape[0]
    tm = min(FFN_ROWS, L)

    def body(dx2_ref, x1_ref, f_ref, g_ref, u_ref, nw2_ref, sc2_ref, g2_ref, wg_ref, wu_ref, wd_ref,
             df_ref, dg_ref, du_ref, dx1_ref, sums_ref, df_scr, acc):
        i, j = pl.program_id(0), pl.program_id(1)

        @pl.when((i == 0) & (j == 0))
        def _():
            sums_ref[...] = jnp.zeros_like(sums_ref)

        @pl.when(j == 0)
        def _():
            dx2 = dx2_ref[...]
            df = (g2_ref[...] * dx2).astype(BF16)
            df_scr[...] = df
            df_ref[...] = df
            sums_ref[0:1, :] += _rowsum(dx2 * f_ref[...])
            acc[...] = jnp.zeros_like(acc)

        dh = _dot(df_scr[...], wd_ref[...], 1, 1)
        gv, uv = g_ref[...], u_ref[...]
        silu_g, dsilu_g = _silu_parts(gv)
        dg = (dh * uv * dsilu_g).astype(BF16)
        du = (dh * silu_g).astype(BF16)
        dg_ref[...] = dg
        du_ref[...] = du
        acc[...] += _dot(dg, wg_ref[...], 1, 1) + _dot(du, wu_ref[...], 1, 1)

        @pl.when(j == N_SHARD - 1)
        def _():
            dhx = acc[...]
            xv = x1_ref[...]
            r = lax.rsqrt(_lanemean(xv * xv) + EPS)
            n0 = xv * r
            nw = nw2_ref[...]
            dn2 = dhx * (1.0 + sc2_ref[...])
            dn0 = dn2 * nw
            dx1_ref[...] = dx2_ref[...] + r * (dn0 - n0 * _lanemean(dn0 * n0))
            sums_ref[1:2, :] += _rowsum(dhx)
            sums_ref[2:3, :] += _rowsum(dhx * n0 * nw)
            sums_ref[3:4, :] += _rowsum(dn2 * n0)

    row = pl.BlockSpec((tm, D), lambda i, j: (i, 0))
    vec = pl.BlockSpec((1, D), lambda i, j: (0, 0))
    sh = pl.BlockSpec((None, tm, FF_SH), lambda i, j: (j, i, 0))
    return pl.pallas_call(
        body, name=name,
        grid=(L // tm, N_SHARD),
        in_specs=[row, row, row, sh, sh, vec, vec, vec,
                  pl.BlockSpec((None, D, FF_SH), lambda i, j: (j, 0, 0)),
                  pl.BlockSpec((None, D, FF_SH), lambda i, j: (j, 0, 0)),
                  pl.BlockSpec((None, FF_SH, D), lambda i, j: (j, 0, 0))],
        out_specs=[row, sh, sh, row, pl.BlockSpec((8, D), lambda i, j: (0, 0))],
        out_shape=[jax.ShapeDtypeStruct((L, D), BF16),
                   jax.ShapeDtypeStruct((N_SHARD, L, FF_SH), BF16), jax.ShapeDtypeStruct((N_SHARD, L, FF_SH), BF16),
                   jax.ShapeDtypeStruct((L, D), F32), jax.ShapeDtypeStruct((8, D), F32)],
        scratch_shapes=[pltpu.VMEM((tm, D), BF16), pltpu.VMEM((tm, D), F32)],
        compiler_params=_params("arbitrary", "arbitrary"),
    )(dx2, x1, f, g, u, nw2, sc2, g2, wg, wu, wd)


def matmul_tn(a, b, name, acc_init=None, to_chips=()):
    na, K, M = a.shape
    nb, _, N = b.shape
    n = max(na, nb)
    tk = min(512, K)
    tn = N if N <= 1024 else N // 2
    nk = K // tk
    grid = (n, N // tn, nk)
    has_init = acc_init is not None
    nx = len(to_chips)

    def body(a_ref, b_ref, *refs):
        init_ref = refs[0] if has_init else None
        refs = refs[1:] if has_init else refs
        o_ref = refs[nx]
        if nx:
            start, finish = _to_chips_phases(refs[:nx], refs[nx + 1:2 * nx + 1], *refs[2 * nx + 1:])
            pos, total = _grid_step(grid)
            pl.when(pos == 0)(start)
        kk = pl.program_id(2)

        @pl.when(kk == 0)
        def _():
            o_ref[...] = init_ref[...] if has_init else jnp.zeros_like(o_ref)

        o_ref[...] += _dot(a_ref[...], b_ref[...], 0, 0)
        if nx:
            pl.when(pos == total - 1)(finish)

    out_spec = pl.BlockSpec((None, M, tn), lambda s, j, kk: (s, 0, j))
    in_specs = [pl.BlockSpec((None, tk, M), lambda s, j, kk: (s if na > 1 else 0, kk, 0)),
                pl.BlockSpec((None, tk, tn), lambda s, j, kk: (s if nb > 1 else 0, kk, j))]
    args = [a, b]
    if has_init:
        in_specs.append(out_spec)
        args.append(acc_init)
    out = pl.pallas_call(
        body, name=name,
        grid=grid,
        in_specs=in_specs + [ANY] * nx,
        out_specs=[out_spec] + [ANY] * nx,
        out_shape=[jax.ShapeDtypeStruct((n, M, N), F32)] + _to_chips_shapes(to_chips),
        scratch_shapes=_to_chips_scratch(nx) if nx else [],
        compiler_params=_params(*(("arbitrary",) * 3 if nx else ("parallel", "parallel", "arbitrary"))),
    )(*args, *to_chips)
    return out if nx else out[0]


def matmul_tn_pair(a, b1, b2, name):
    K, M = a.shape
    n, _, N = b1.shape
    tk = min(512, K)

    def body(a_ref, b1_ref, b2_ref, o1_ref, o2_ref):
        @pl.when(pl.program_id(1) == 0)
        def _():
            o1_ref[...] = jnp.zeros_like(o1_ref)
            o2_ref[...] = jnp.zeros_like(o2_ref)

        at = a_ref[...].T
        o1_ref[...] += _dot(at, b1_ref[...])
        o2_ref[...] += _dot(at, b2_ref[...])

    b_spec = pl.BlockSpec((None, tk, N), lambda s, kk: (s, kk, 0))
    o_spec = pl.BlockSpec((None, M, N), lambda s, kk: (s, 0, 0))
    return pl.pallas_call(
        body, name=name,
        grid=(n, K // tk),
        in_specs=[pl.BlockSpec((tk, M), lambda s, kk: (kk, 0)), b_spec, b_spec],
        out_specs=[o_spec, o_spec],
        out_shape=[jax.ShapeDtypeStruct((n, M, N), F32)] * 2,
        compiler_params=_params("parallel", "arbitrary"),
    )(a, b1, b2)


PIECE_COLS = 1024
N_PIECE_BLOCKS = D_IN // PIECE_COLS


def _piece_blocks(pieces):
    out, col = [], 0
    for arr, width in pieces:
        if arr is not None:
            out.append((arr, col // PIECE_COLS, width // PIECE_COLS))
        col += width
    assert col == D_IN
    return out


def _piece_feed(p_refs, blocks, buf, sems, tile_of, pos, total):
    def present(blk):
        ok = None
        for _, b0, nb in blocks:
            mine = (blk >= b0) & (blk < b0 + nb)
            ok = mine if ok is None else ok | mine
        return ok

    def fetch(step):
        blk, rows = tile_of(step)
        for p_ref, (_, b0, nb) in zip(p_refs, blocks):
            for t in range(nb):
                @pl.when(blk == b0 + t)
                def _(p_ref=p_ref, t=t):
                    pltpu.make_async_copy(p_ref.at[rows, pl.ds(t * PIECE_COLS, PIECE_COLS)], buf.at[step % 2],
                                          sems.at[step % 2]).start()

    @pl.when(pos == 0)
    def _():
        fetch(pos)

    @pl.when(pos + 1 < total)
    def _():
        fetch(pos + 1)

    def landed():
        slot = pos % 2
        pltpu.make_async_copy(p_refs[0].at[pl.ds(0, buf.shape[1]), pl.ds(0, PIECE_COLS)], buf.at[slot],
                              sems.at[slot]).wait()
        return buf.at[slot]

    return present(tile_of(pos)[0]), landed


def matmul_tn_pieces(a, pieces, name, acc_init=None, to_chips=()):
    K, M = a.shape
    blocks = _piece_blocks(pieces)
    tk = min(1024, K)
    nk = K // tk
    grid = (N_PIECE_BLOCKS, nk)
    has_init = acc_init is not None
    nx, npc = len(to_chips), len(blocks)

    def body(a_ref, *refs):
        p_refs = refs[:npc]
        refs = refs[npc:]
        init_ref = refs[0] if has_init else None
        refs = refs[1:] if has_init else refs
        o_ref = refs[nx]
        buf, sems = refs[2 * nx + 1:2 * nx + 3]
        pos, total = _grid_step(grid)
        if nx:
            start, finish = _to_chips_phases(refs[:nx], refs[nx + 1:2 * nx + 1], *refs[2 * nx + 3:])
            pl.when(pos == 0)(start)
        here, landed = _piece_feed(p_refs, blocks, buf, sems,
                                   lambda s: (s // nk, pl.ds(pl.multiple_of((s % nk) * tk, tk), tk)), pos, total)

        @pl.when(pl.program_id(1) == 0)
        def _():
            o_ref[...] = init_ref[...] if has_init else jnp.zeros_like(o_ref)

        @pl.when(here)
        def _():
            o_ref[...] += _dot(a_ref[...], landed()[...], 0, 0)

        if nx:
            pl.when(pos == total - 1)(finish)

    out_spec = pl.BlockSpec((M, PIECE_COLS), lambda blk, kk: (0, blk))
    in_specs = [pl.BlockSpec((tk, M), lambda blk, kk: (kk, 0))] + [ANY] * npc
    args = [a] + [arr for arr, _, _ in blocks]
    if has_init:
        in_specs.append(out_spec)
        args.append(acc_init)
    out = pl.pallas_call(
        body, name=name,
        grid=grid,
        in_specs=in_specs + [ANY] * nx,
        out_specs=[out_spec] + [ANY] * nx,
        out_shape=[jax.ShapeDtypeStruct((M, D_IN), F32)] + _to_chips_shapes(to_chips),
        scratch_shapes=[pltpu.VMEM((2, tk, PIECE_COLS), BF16), pltpu.SemaphoreType.DMA((2,))]
        + (_to_chips_scratch(nx) if nx else []),
        compiler_params=_params("arbitrary", "arbitrary"),
    )(*args, *to_chips)
    return out if nx else out[0]


def dhx_normbwd(pieces, w, x, dx_res, nw, sc, name, to_chips=()):
    L = x.shape[0]
    tm = min(PROJ_ROWS, L)
    blocks = _piece_blocks(pieces)
    grid = (L // tm, N_PIECE_BLOCKS)
    nx, npc = len(to_chips), len(blocks)

    def body(*refs):
        p_refs = refs[:npc]
        w_ref, x_ref, res_ref, nw_ref, sc_ref = refs[npc:npc + 5]
        refs = refs[npc + 5:]
        dx_ref, sums_ref = refs[nx:nx + 2]
        acc, buf, sems = refs[2 * nx + 2:2 * nx + 5]
        pos, total = _grid_step(grid)
        if nx:
            start, finish = _to_chips_phases(refs[:nx], refs[nx + 2:2 * nx + 2], *refs[2 * nx + 5:])
            pl.when(pos == 0)(start)
            pl.when(pos == total - 1)(finish)
        here, landed = _piece_feed(
            p_refs, blocks, buf, sems,
            lambda s: (s % N_PIECE_BLOCKS, pl.ds(pl.multiple_of((s // N_PIECE_BLOCKS) * tm, tm), tm)), pos, total)
        i, blk = pl.program_id(0), pl.program_id(1)

        @pl.when((i == 0) & (blk == 0))
        def _():
            sums_ref[...] = jnp.zeros_like(sums_ref)

        @pl.when(blk == 0)
        def _():
            acc[...] = jnp.zeros_like(acc)

        @pl.when(here)
        def _():
            acc[...] += _dot(landed()[...], w_ref[...], 1, 1)

        @pl.when(blk == N_PIECE_BLOCKS - 1)
        def _():
            dhx = acc[...]
            xv = x_ref[...]
            r = lax.rsqrt(_lanemean(xv * xv) + EPS)
            n0 = xv * r
            nw = nw_ref[...]
            dn = dhx * (1.0 + sc_ref[...])
            dn0 = dn * nw
            dx_ref[...] = res_ref[...] + r * (dn0 - n0 * _lanemean(dn0 * n0))
            sums_ref[0:1, :] += _rowsum(dhx)
            sums_ref[1:2, :] += _rowsum(dhx * n0 * nw)
            sums_ref[2:3, :] += _rowsum(dn * n0)

    row = pl.BlockSpec((tm, D), lambda i, blk: (i, 0))
    vec = pl.BlockSpec((1, D), lambda i, blk: (0, 0))
    return pl.pallas_call(
        body, name=name,
        grid=grid,
        in_specs=[ANY] * npc + [pl.BlockSpec((D, PIECE_COLS), lambda i, blk: (0, blk)), row, row, vec, vec] + [ANY] * nx,
        out_specs=[row, pl.BlockSpec((8, D), lambda i, blk: (0, 0))] + [ANY] * nx,
        out_shape=[jax.ShapeDtypeStruct((L, D), F32), jax.ShapeDtypeStruct((8, D), F32)] + _to_chips_shapes(to_chips),
        scratch_shapes=[pltpu.VMEM((tm, D), F32), pltpu.VMEM((2, tm, PIECE_COLS), BF16), pltpu.SemaphoreType.DMA((2,))]
        + (_to_chips_scratch(nx) if nx else []),
        compiler_params=_params("arbitrary", "arbitrary"),
    )(*[arr for arr, _, _ in blocks], w, x, dx_res, nw, sc, *to_chips)


SMALL_ROWS = 24


def _rope_tables(L):
    rows = L // 64
    freqs = 10000.0 ** (-jnp.arange(RT_DK // 4, dtype=F32) / (RT_DK // 4))
    a_row = jnp.arange(rows, dtype=F32)[:, None] * freqs
    a_col = jnp.arange(64, dtype=F32)[:, None] * freqs

    def spread(f):
        return jnp.concatenate([jnp.repeat(f(a_row), 64, axis=0), jnp.tile(f(a_col), (rows, 1))], axis=-1)

    cos, sin = spread(jnp.cos), spread(jnp.sin)
    return jnp.concatenate([cos, cos], axis=1), jnp.concatenate([-sin, sin], axis=1)


def _pieces(hq, hf_f, hf_b, hi, hg, rq, rk, rv, rg, ga, gb):
    widths = (D, D, D, D, D, D, D, 2 * D, 2 * D, D, D)
    return list(zip((hq, hf_f, hf_b, hi, hg, rq, rk, rv, rg, ga, gb), widths))


def _lane0(a):
    return a[:, 0, 0]


def _pack_small(rows):
    out = [r.reshape(1, D) for r in rows]
    out += [jnp.zeros((1, D), F32)] * (SMALL_ROWS - len(out))
    return jnp.concatenate(out, axis=0)


def _sibling_sums(gs, names, place):
    core, core_arg, _ = place

    def other_half(g):
        axis = g.ndim - 2
        h = g.shape[axis] // 2
        return lax.dynamic_slice_in_dim(g, (1 - core) * h, h, axis=axis).astype(BF16)

    payload = [other_half(g) for g in gs]
    received = rs_to_sibling(payload, "rs_to_sibling_" + names[0])
    return [rs_add_sibling(g, r, core_arg, "rs_add_sibling_" + k) for g, r, k in zip(gs, received, names)]


def _staged_in_proj(x, nw, sh, sc, w_shard, rest, chip):
    cx, cy = chip // 2, chip % 2

    def arg(k):
        return jnp.reshape(k, (1,)).astype(jnp.int32)

    p, hx, w_full = in_proj_own(x, nw, sh, sc, w_shard, arg(chip), "in_proj_own")
    p, w_full = in_proj_next(hx, w_full, arg(2 * (1 - cx) + cy), p, "in_proj_x", diag_from=w_shard)
    w_pa, w_pb, w_out, w_wd = rest[0], rest[1], rest[2], rest[5]
    p, g_pa, g_pb, g_out, g_wd = in_proj_next(hx, w_full, arg(2 * cx + 1 - cy), p, "in_proj_y",
                                              gather=[w_pa, w_pb, w_out, w_wd])
    p, g_wg, g_wu = in_proj_next(hx, w_full, arg(3 - chip), p, "in_proj_diag", gather=[rest[3], rest[4]])
    w = {"w_in": w_full, "w_pa": g_pa.reshape(D, D), "w_pb": g_pb.reshape(2 * D, D), "w_out": g_out.reshape(D, D),
         "wg": g_wg, "wu": g_wu, "wd": g_wd}
    return p, hx, w


def local_step(x, ctx, target, mod_x, mod_c, lb_f, lb_b, lg_f, lg_b, nw1, nw2, hgw, fw, w, rest=None, place=None):
    L, Lc = x.shape[0], ctx.shape[0]
    sh1, sc1, g1, sh2, sc2, g2 = (mod_x[i:i + 1] for i in range(6))
    sh1c, sc1c = mod_c[0:1], mod_c[1:2]
    cosf, sinf = _rope_tables(L)
    cosc, sinc = jnp.ones((Lc, RT_DK), F32), jnp.zeros((Lc, RT_DK), F32)
    zero_h = jnp.zeros((HEADS, HG_D, HG_D), F32)
    zero_r = jnp.zeros((HEADS, RT_DV, RT_DK), F32)

    if rest is None:
        p, hx = normmod_matmul(x, nw1, sh1, sc1, w["w_in"], "in_proj")
    else:
        p, hx, w = _staged_in_proj(x, nw1, sh1, sc1, w["w_in_shard"], rest, place[2][0])
    pc, hxc = normmod_matmul(ctx, nw1, sh1c, sc1c, w["w_in"], "ctx_in_proj")
    _, s_hf, cb_hf = hgrn_scan_fwd(pc, lb_f, zero_h, COL_HFF, False, "ctx_hgrn_f")
    _, s_hb, cb_hb = hgrn_scan_fwd(pc, lb_b, zero_h, COL_HFB, True, "ctx_hgrn_b")
    _, s_rf, cb_rf = ret_scan_fwd(pc, cosc, sinc, lg_f, zero_r, False, "ctx_ret_f")
    _, s_rb, cb_rb = ret_scan_fwd(pc, cosc, sinc, lg_b, zero_r, True, "ctx_ret_b")
    ohf, _, xb_hf = hgrn_scan_fwd(p, lb_f, s_hf, COL_HFF, False, "hgrn_f")
    o_hg, _, xb_hb = hgrn_scan_fwd(p, lb_b, s_hb, COL_HFB, True, "hgrn_b", prev=ohf)
    orf, _, xb_rf = ret_scan_fwd(p, cosf, sinf, lg_f, s_rf, False, "ret_f")
    o_rt, _, xb_rb = ret_scan_fwd(p, cosf, sinf, lg_b, s_rb, True, "ret_b", prev=orf)
    x1, x_mix, merged, ya, yb = mix_fwd(o_hg, o_rt, p, x, g1, hgw, w["w_pa"], w["w_pb"], w["w_out"], "mix_fwd")
    hx2, gg, uu, hh, ff, dx2, sums_f = ffn_fwd(x1, target, nw2, sh2, sc2, g2, fw, w["wg"], w["wu"], w["wd"], "ffn_fwd")

    d_f, d_g, d_u, dx1, sums_fb = ffn_bwd(dx2, x1, ff, gg, uu, nw2, sc2, g2, w["wg"], w["wu"], w["wd"], "ffn_bwd")
    dw_gate, dw_up = matmul_tn_pair(hx2, d_g, d_u, "dw_ffn_gate_up")
    grads = {"wg": dw_gate, "wu": dw_up, "wd": matmul_tn(hh, d_f[None], "dw_ffn_down")}
    dxm, d_a, d_b, dga, dgb, dhg, drg, dohg, dort, sums_m = mix_bwd(
        dx1, x_mix, ya, yb, o_hg, o_rt, p, g1, hgw, w["w_pa"], w["w_pb"], w["w_out"], "mix_bwd")
    grads["w_out"] = matmul_tn(merged[None], dxm[None], "dw_out").reshape(N_SHARD, D // N_SHARD, D)
    grads["w_pa"] = matmul_tn(ya[None], d_a[None], "dw_proj_hgrn").reshape(N_SHARD, D // N_SHARD, D)
    grads["w_pb"] = matmul_tn(yb[None], d_b[None], "dw_proj_ret").reshape(N_SHARD, 2 * D // N_SHARD, D)

    rq1, rk1, rv1, dlgf_x, ds_rf = ret_scan_bwd(p, cosf, sinf, lg_f, xb_rf, dort, zero_r, None, False, "ret_f_bwd")
    drq, drk, drv, dlgb_x, ds_rb = ret_scan_bwd(p, cosf, sinf, lg_b, xb_rb, dort, zero_r, (rq1, rk1, rv1), True, "ret_b_bwd")
    hq1, dzf, hv1, dlbf_x, ds_hf = hgrn_scan_bwd(p, lb_f, xb_hf, dohg, zero_h, None, COL_HFF, False, "hgrn_f_bwd")
    dhq, dzb, dhv, dlbb_x, ds_hb = hgrn_scan_bwd(p, lb_b, xb_hb, dohg, zero_h, (hq1, hv1), COL_HFB, True, "hgrn_b_bwd")
    dp = _pieces(dhq, dzf, dzb, dhv, dhg, drq, drk, drv, drg, dga, dgb)
    others = ["w_pa", "w_pb", "w_out", "wg", "wu", "wd"]
    if place is None:
        dw_in = matmul_tn_pieces(hx, dp, "dw_in")
    else:
        sums_o = _sibling_sums([grads[k] for k in others], others, place)
        dw_in, *recv_o = matmul_tn_pieces(hx, dp, "dw_in", to_chips=[a16 for _, a16 in sums_o])

    zc = jnp.zeros((Lc, D), F32)
    zc2 = jnp.zeros((Lc, 2 * D), F32)
    crq1, crk1, crv1, dlgf_c, _ = ret_scan_bwd(pc, cosc, sinc, lg_f, cb_rf, zc2, ds_rf, None, False, "ctx_ret_f_bwd")
    cdrq, cdrk, cdrv, dlgb_c, _ = ret_scan_bwd(pc, cosc, sinc, lg_b, cb_rb, zc2, ds_rb, (crq1, crk1, crv1), True, "ctx_ret_b_bwd")
    chq1, cdzf, chv1, dlbf_c, _ = hgrn_scan_bwd(pc, lb_f, cb_hf, zc, ds_hf, None, COL_HFF, False, "ctx_hgrn_f_bwd")
    cdhq, cdzb, cdhv, dlbb_c, _ = hgrn_scan_bwd(pc, lb_b, cb_hb, zc, ds_hb, (chq1, chv1), COL_HFB, True, "ctx_hgrn_b_bwd")
    dpc = _pieces(cdhq, cdzf, cdzb, cdhv, None, cdrq, cdrk, cdrv, None, None, None)
    _, sums_c = dhx_normbwd(dpc, w["w_in"], ctx, zc, nw1, sc1c, "dctx_in_proj")
    grads["w_in"] = matmul_tn_pieces(hxc, dpc, "dw_in_ctx", acc_init=dw_in)
    if place is None:
        dx, sums_x = dhx_normbwd(dp, w["w_in"], x, dx1, nw1, sc1, "dx_in_proj")
    else:
        sums_i = _sibling_sums([grads["w_in"]], ["w_in"], place)
        dx, sums_x, recv_i = dhx_normbwd(dp, w["w_in"], x, dx1, nw1, sc1, "dx_in_proj", to_chips=[sums_i[0][1]])
        names = ["w_in"] + others
        halves = [rs_add_chips(a, r, place[2], "rs_add_chips_" + k)
                  for (a, _), r, k in zip(sums_i + sums_o, [recv_i] + recv_o, names)]
        grads = dict(zip(names, rs_join_halves(halves, "rs_join_halves")))

    def lg_row(f, b):
        return jnp.concatenate([_lane0(f), _lane0(b), jnp.zeros((D - 2 * HEADS,), F32)])

    small = _pack_small([
        sums_x[0], sums_x[1], sums_m[0], sums_fb[1], sums_fb[2], sums_fb[0],
        sums_c[0], sums_c[1],
        sums_x[2], sums_c[2], sums_fb[3], sums_m[1], sums_f[0],
        dlbf_x, dlbf_c, dlbb_x, dlbb_c,
        lg_row(dlgf_x, dlgb_x), lg_row(dlgf_c, dlgb_c),
        sums_f[1],
    ])
    return dx, grads, small


MESH = pl.DeviceIdType.MESH
ANY = pl.BlockSpec(memory_space=pl.ANY)
N_DEV = 8


def _place():
    return lax.axis_index("x"), lax.axis_index("y"), lax.axis_index("c")


def _other_chips(x, y):
    return [(1 - x, y), (x, 1 - y), (1 - x, 1 - y)]


def allgather8(xs, name):
    m, n = xs.shape

    def body(x_ref, out_ref, send_sems, recv_sems, local_sem):
        x, y, c = _place()
        me, sibling = (x, y, c), (x, y, 1 - c)
        chips = _other_chips(x, y)

        def rows(px, py, pc):
            return out_ref.at[pl.ds((4 * px + 2 * py + pc) * m, m), :]

        def copy(k, block, to, src=None):
            return pltpu.make_async_remote_copy(
                src_ref=rows(*block) if src is None else src, dst_ref=rows(*block),
                send_sem=send_sems.at[k], recv_sem=recv_sems.at[k], device_id=to, device_id_type=MESH)

        mine = pltpu.make_async_copy(x_ref, rows(*me), local_sem)
        mine.start()
        first = [copy(0, me, sibling, src=x_ref)]
        first += [copy(1 + j, me, (*chip, c), src=x_ref) for j, chip in enumerate(chips)]
        for cp in first:
            cp.start()
        passed = [copy(4 + j, (*chip, c), sibling) for j, chip in enumerate(chips)]
        for j, chip in enumerate(chips):
            copy(1 + j, (*chip, c), me).wait_recv()
            passed[j].start()
        copy(0, sibling, me).wait_recv()
        for j, chip in enumerate(chips):
            copy(4 + j, (*chip, 1 - c), me).wait_recv()
        for cp in first + passed:
            cp.wait_send()
        mine.wait()

    return pl.pallas_call(
        body, name=name,
        out_shape=jax.ShapeDtypeStruct((N_DEV * m, n), xs.dtype),
        in_specs=[pl.BlockSpec(memory_space=pltpu.VMEM)],
        out_specs=pl.BlockSpec(memory_space=pltpu.VMEM),
        scratch_shapes=[pltpu.SemaphoreType.DMA((7,)), pltpu.SemaphoreType.DMA((7,)), pltpu.SemaphoreType.DMA],
    )(xs)


def _gather_phases(ins, outs, send_sems, recv_sems, local_sems, relations=(0, 1, 2), stage=None):
    n = len(ins)
    x, y, c = _place()
    chips = _other_chips(x, y)

    def rows(i, core):
        h = ins[i].shape[0] // 2
        return pl.ds(pl.multiple_of(core * h, 16), h)

    def region(i, k, rs):
        if len(outs[i].shape) == 2:
            cols = ins[i].shape[1]
            return outs[i].at[rs, pl.ds(pl.multiple_of(k * cols, 128), cols)]
        return outs[i].at[k, rs, :]

    def landed(i, chip, core):
        return region(i, 2 * chip[0] + chip[1], rows(i, core))

    def copy(i, k, src, dst, to):
        return pltpu.make_async_remote_copy(src_ref=src, dst_ref=dst, send_sem=send_sems.at[6 * i + k],
                                            recv_sem=recv_sems.at[6 * i + k], device_id=to, device_id_type=MESH)

    def lift(i):
        return pltpu.make_async_copy(ins[i], stage[i], local_sems.at[i])

    def drop(i):
        return pltpu.make_async_copy(stage[i], region(i, 2 * x + y, pl.ds(0, ins[i].shape[0])), local_sems.at[i])

    def send(i, j):
        return copy(i, j, ins[i].at[rows(i, c), :], landed(i, (x, y), c), (*chips[j], c))

    def arrived(i, j, core, k):
        return copy(i, k, ins[i].at[rows(i, core), :], landed(i, chips[j], core), (x, y, 1 - c))

    def passed(i, j):
        return copy(i, 3 + j, landed(i, chips[j], c), landed(i, chips[j], c), (x, y, 1 - c))

    def start():
        for i in range(n):
            if stage is not None:
                lift(i).start()
            for j in relations:
                send(i, j).start()

    def forward():
        for i in range(n):
            if stage is not None:
                lift(i).wait()
                drop(i).start()
            for j in relations:
                arrived(i, j, c, j).wait_recv()
                passed(i, j).start()

    def finish():
        for i in range(n):
            for j in relations:
                arrived(i, j, 1 - c, 3 + j).wait_recv()
        for i in range(n):
            for j in relations:
                send(i, j).wait_send()
                passed(i, j).wait_send()
            if stage is not None:
                drop(i).wait()

    return start, forward, finish


def _gather_scratch(n):
    return [pltpu.SemaphoreType.DMA((6 * n,)), pltpu.SemaphoreType.DMA((6 * n,)), pltpu.SemaphoreType.DMA((n,))]


def rs_to_sibling(payloads, name):
    n = len(payloads)

    def body(*refs):
        ins, outs = refs[:n], refs[n:2 * n]
        send_sems, recv_sems = refs[2 * n:]
        x, y, c = _place()
        copies = []
        for i in range(n):
            cp = pltpu.make_async_remote_copy(src_ref=ins[i], dst_ref=outs[i], send_sem=send_sems.at[i],
                                              recv_sem=recv_sems.at[i], device_id=(x, y, 1 - c), device_id_type=MESH)
            cp.start()
            copies.append(cp)
        for cp in copies:
            cp.wait()

    return pl.pallas_call(
        body, name=name,
        out_shape=[jax.ShapeDtypeStruct(g.shape, g.dtype) for g in payloads],
        in_specs=[ANY] * n, out_specs=[ANY] * n,
        scratch_shapes=[pltpu.SemaphoreType.DMA((n,)), pltpu.SemaphoreType.DMA((n,))],
    )(*payloads)


def _to_chips_phases(ins, outs, send_sems, recv_sems):
    def copies():
        x, y, c = _place()
        return [pltpu.make_async_remote_copy(
            src_ref=ins[i].at[2 * px + py], dst_ref=outs[i].at[j], send_sem=send_sems.at[3 * i + j],
            recv_sem=recv_sems.at[3 * i + j], device_id=(px, py, c), device_id_type=MESH)
            for i in range(len(ins)) for j, (px, py) in enumerate(_other_chips(x, y))]

    def start():
        for cp in copies():
            cp.start()

    def finish():
        for cp in copies():
            cp.wait()

    return start, finish


def _to_chips_shapes(parts):
    return [jax.ShapeDtypeStruct((3,) + a.shape[1:], a.dtype) for a in parts]


def _to_chips_scratch(n):
    return [pltpu.SemaphoreType.DMA((3 * n,)), pltpu.SemaphoreType.DMA((3 * n,))]


def rs_join_halves(fulls, name):
    n = len(fulls)

    def body(*refs):
        outs = refs[n:2 * n]
        send_sems, recv_sems = refs[2 * n:]
        x, y, c = _place()

        def copy(i, core):
            h = fulls[i].shape[0] // 2
            rows = outs[i].at[pl.ds(pl.multiple_of(core * h, 8), h), :]
            return pltpu.make_async_remote_copy(src_ref=rows, dst_ref=rows, send_sem=send_sems.at[i],
                                                recv_sem=recv_sems.at[i], device_id=(x, y, 1 - c), device_id_type=MESH)

        sent = [copy(i, c) for i in range(n)]
        for cp in sent:
            cp.start()
        for i in range(n):
            copy(i, 1 - c).wait_recv()
        for cp in sent:
            cp.wait_send()

    return pl.pallas_call(
        body, name=name,
        out_shape=[jax.ShapeDtypeStruct(a.shape, a.dtype) for a in fulls],
        in_specs=[ANY] * n, out_specs=[ANY] * n,
        input_output_aliases={i: i for i in range(n)},
        scratch_shapes=[pltpu.SemaphoreType.DMA((n,)), pltpu.SemaphoreType.DMA((n,))],
    )(*fulls)


def _row_tile(rows, cols, limit_bytes=2 * 1024 * 1024, mult=8):
    best = mult
    for t in range(mult, rows + 1, mult):
        if rows % t == 0 and t * cols * 4 <= limit_bytes:
            best = t
    return best


def rs_add_sibling(g, recv, c, name):
    if g.ndim == 2:
        h, C = recv.shape[0], recv.shape[1] // N_SHARD
    else:
        _, h, C = recv.shape
    tr = _row_tile(h, C, mult=16)
    nt = h // tr

    def body(c_ref, g_ref, r_ref, o_ref, o16_ref):
        s = g_ref[...] + r_ref[...].astype(F32)
        o_ref[...] = s
        o16_ref[...] = s.astype(BF16)

    blk = pl.BlockSpec((None, tr, C), lambda k, i, c_ref: (k, i, 0))
    if g.ndim == 2:
        g_spec = pl.BlockSpec((tr, C), lambda k, i, c_ref: (c_ref[0] * nt + i, k))
        r_spec = pl.BlockSpec((tr, C), lambda k, i, c_ref: (i, k))
    else:
        g_spec = pl.BlockSpec((None, tr, C), lambda k, i, c_ref: (k, c_ref[0] * nt + i, 0))
        r_spec = blk
    return pl.pallas_call(
        body, name=name,
        grid_spec=pltpu.PrefetchScalarGridSpec(
            num_scalar_prefetch=1, grid=(N_SHARD, nt),
            in_specs=[g_spec, r_spec],
            out_specs=[blk, blk]),
        out_shape=[jax.ShapeDtypeStruct((N_SHARD, h, C), F32), jax.ShapeDtypeStruct((N_SHARD, h, C), BF16)],
        compiler_params=_params("parallel", "parallel"),
    )(c, g, recv)


def rs_add_chips(part, recv, place, name):
    _, h, C = part.shape
    tr = _row_tile(h, C, mult=16)
    nt = h // tr

    def body(k_ref, p_ref, r_ref, o_ref):
        o_ref[...] = ((p_ref[...] + r_ref[0].astype(F32)) + r_ref[1].astype(F32)) + r_ref[2].astype(F32)

    return pl.pallas_call(
        body, name=name,
        grid_spec=pltpu.PrefetchScalarGridSpec(
            num_scalar_prefetch=1, grid=(nt,),
            in_specs=[pl.BlockSpec((None, tr, C), lambda i, k_ref: (k_ref[0], i, 0)),
                      pl.BlockSpec((3, tr, C), lambda i, k_ref: (0, i, 0))],
            out_specs=pl.BlockSpec((tr, C), lambda i, k_ref: (k_ref[1] * nt + i, 0))),
        out_shape=jax.ShapeDtypeStruct((2 * h, C), F32),
        compiler_params=_params("parallel"),
    )(place, part, recv)


def _adamw_math(w, g, m, v):
    m = ADAM_B1 * m + (1.0 - ADAM_B1) * g
    v = ADAM_B2 * v + (1.0 - ADAM_B2) * (g * g)
    m_hat = m / (1.0 - ADAM_B1 ** ADAM_STEP)
    v_hat = v / (1.0 - ADAM_B2 ** ADAM_STEP)
    delta = -ADAM_LR * (m_hat / (jnp.sqrt(v_hat) + ADAM_EPS) + ADAM_WD * w)
    return delta, m, v


def adamw(w, g, m, v, name):
    R, C = w.shape
    tr = _row_tile(R, C, 1024 * 1024)

    def body(w_ref, g_ref, m_ref, v_ref, d_ref, nm_ref, nv_ref):
        d_ref[...], nm_ref[...], nv_ref[...] = _adamw_math(w_ref[...], g_ref[...], m_ref[...], v_ref[...])

    blk = pl.BlockSpec((tr, C), lambda i: (i, 0))
    return pl.pallas_call(
        body, name=name, grid=(R // tr,), in_specs=[blk] * 4, out_specs=[blk] * 3,
        out_shape=[jax.ShapeDtypeStruct((R, C), F32)] * 3,
        compiler_params=_params("parallel"),
    )(w, g, m, v)


MOD_SH = 6 * D // N_SHARD
PK_ROWS = 16


def mod_fwd(call16, w_sh, b_sh, name):
    def body(c_ref, w_ref, b_ref, o_ref):
        o_ref[...] = _dot(_silu_parts(c_ref[...])[0], w_ref[...], prec=HI) + b_ref[...]

    return pl.pallas_call(body, name=name, out_shape=jax.ShapeDtypeStruct((16, MOD_SH), F32),
                          compiler_params=_params())(call16, w_sh, b_sh)


def prep_small(lbf2, lbb2, theta_row, name):
    def body(f_ref, b_ref, t_ref, lbf_ref, lbb_ref, lg_ref):
        lbf_ref[...] = _sigmoid(f_ref[0:1, :] - f_ref[1:2, :])
        lbb_ref[...] = _sigmoid(b_ref[0:1, :] - b_ref[1:2, :])
        t = t_ref[...]
        lg_ref[...] = jnp.minimum(t, 0.0) - jnp.log(1.0 + jnp.exp(-jnp.abs(t)))

    row = jax.ShapeDtypeStruct((1, D), F32)
    return pl.pallas_call(body, name=name, out_shape=[row, row, row], compiler_params=_params())(lbf2, lbb2, theta_row)


def small_grads(g3, lbf, lbb, theta_row, name):
    def body(g_ref, lbf_ref, lbb_ref, t_ref, pk_ref, aux_ref):
        s = g_ref[0]
        for d in range(1, N_DEV):
            s = s + g_ref[d]
        pk_ref[...] = jnp.zeros_like(pk_ref)
        aux_ref[...] = jnp.zeros_like(aux_ref)
        pk_ref[1:7, :] = s[0:6]
        pk_ref[1:3, :] += s[6:8]
        pk_ref[7:8, :] = s[8:9] + s[9:10]
        pk_ref[8:9, :] = s[10:11]
        lbf, lbb = lbf_ref[...], lbb_ref[...]
        daf = (s[13:14] + s[14:15]) * lbf * (1.0 - lbf)
        dab = (s[15:16] + s[16:17]) * lbb * (1.0 - lbb)
        pk_ref[9:10, :] = daf
        pk_ref[10:11, :] = -daf
        pk_ref[11:12, :] = dab
        pk_ref[12:13, :] = -dab
        pk_ref[13:14, :] = s[11:12]
        pk_ref[14:15, :] = (s[17:18] + s[18:19]) * _sigmoid(-t_ref[...])
        pk_ref[15:16, :] = s[12:13]
        aux_ref[0:2, :] = s[6:8]
        aux_ref[2:3, :] = jnp.broadcast_to(jnp.sum(s[19:20], axis=-1, keepdims=True), (1, D))

    return pl.pallas_call(body, name=name,
                          out_shape=[jax.ShapeDtypeStruct((PK_ROWS, D), F32), jax.ShapeDtypeStruct((8, D), F32)],
                          compiler_params=_params())(g3, lbf, lbb, theta_row)


def mod_bwd(call16, dmod_sh, w_sh, name):
    def body(c_ref, d_ref, w_ref, dw_ref, ds_ref):
        dm = d_ref[...]
        dw_ref[...] = _dot(_silu_parts(c_ref[...])[0], dm, 0, 0, prec=HI)
        ds_ref[...] = jnp.zeros_like(ds_ref)
        ds_ref[0:1, :] = _dot(dm[8:9, :], w_ref[...], 1, 1, prec=HI)

    return pl.pallas_call(body, name=name,
                          out_shape=[jax.ShapeDtypeStruct((D, MOD_SH), F32), jax.ShapeDtypeStruct((8, D), F32)],
                          compiler_params=_params())(call16, dmod_sh, w_sh)


def adamw_small(g4, pk_g, pk_w, pk_m, pk_v, name):
    def body(g4_ref, g_ref, w_ref, m_ref, v_ref, go_ref, d_ref, nm_ref, nv_ref):
        w = w_ref[...]
        ds = ((g4_ref[0:1, :] + g4_ref[16:17, :]) + g4_ref[32:33, :]) + g4_ref[48:49, :]
        row = lax.broadcasted_iota(jnp.int32, (PK_ROWS, D), 0)
        g = jnp.where(row == 0, ds * _silu_parts(w[0:1, :])[1], g_ref[...])
        go_ref[...] = g
        d_ref[...], nm_ref[...], nv_ref[...] = _adamw_math(w, g, m_ref[...], v_ref[...])

    pk = jax.ShapeDtypeStruct((PK_ROWS, D), F32)
    return pl.pallas_call(body, name=name, out_shape=[pk, pk, pk, pk], compiler_params=_params())(g4, pk_g, pk_w, pk_m, pk_v)


def _pack_params(c_ctx, b_mod, n1, n2, lbf, lbb, hgn, th_f, th_b, fin):
    theta = jnp.concatenate([th_f.reshape(HEADS), th_b.reshape(HEADS), jnp.zeros((D - 2 * HEADS,), F32)])
    return jnp.concatenate([c_ctx.reshape(1, D), b_mod.reshape(6, D), n1.reshape(1, D), n2.reshape(1, D), lbf, lbb,
                            hgn.reshape(1, D), theta.reshape(1, D), fin.reshape(1, D)], axis=0)


def _unpack_params(pk):
    return (pk[0], pk[1:7].reshape(1, 6 * D), pk[7:8], pk[8:9], pk[9:11], pk[11:13], pk[13:14],
            pk[14, 0:HEADS].reshape(1, HEADS), pk[14, HEADS:2 * HEADS].reshape(1, HEADS), pk[15])


def kernel(x, c, ctx, c_ctx, w_mod, b_mod, norm1_w, norm2_w, w_in, hg_lb_fwd, hg_lb_bwd, hg_norm_w, rt_theta_fwd, rt_theta_bwd, w_proj_hgrn, w_proj_ret, w_out, w_ffn_gate, w_ffn_up, w_ffn_down, final_norm_w, loss_target, m_c_ctx, m_w_mod, m_b_mod, m_norm1_w, m_norm2_w, m_w_in, m_hg_lb_fwd, m_hg_lb_bwd, m_hg_norm_w, m_rt_theta_fwd, m_rt_theta_bwd, m_w_proj_hgrn, m_w_proj_ret, m_w_out, m_w_ffn_gate, m_w_ffn_up, m_w_ffn_down, m_final_norm_w, v_c_ctx, v_w_mod, v_b_mod, v_norm1_w, v_norm2_w, v_w_in, v_hg_lb_fwd, v_hg_lb_bwd, v_hg_norm_w, v_rt_theta_fwd, v_rt_theta_bwd, v_w_proj_hgrn, v_w_proj_ret, v_w_out, v_w_ffn_gate, v_w_ffn_up, v_w_ffn_down, v_final_norm_w):
    xi, yi, ci = _place()
    dev = 4 * xi + 2 * yi + ci
    chip = 2 * xi + yi
    core_arg = jnp.reshape(ci, (1,)).astype(jnp.int32)
    place_arg = jnp.stack([chip, ci]).astype(jnp.int32)

    c_all = allgather8(jnp.concatenate([c, jnp.zeros((7, D), F32)], axis=0), "gather_c").reshape(N_DEV, 8, D)[:, 0]
    call16 = jnp.concatenate([c_all, c_ctx.reshape(1, D), jnp.zeros((7, D), F32)], axis=0)
    b_sh = lax.dynamic_slice_in_dim(b_mod, chip * MOD_SH, MOD_SH, axis=1)
    mod_sh = mod_fwd(call16, w_mod[0], b_sh, "mod_fwd")
    mod_g = allgather8(mod_sh, "gather_mod").reshape(N_DEV, 16, MOD_SH)
    mod_all = jnp.concatenate([mod_g[0], mod_g[2], mod_g[4], mod_g[6]], axis=1)
    mod_x = lax.dynamic_index_in_dim(mod_all, dev, axis=0, keepdims=False).reshape(6, D)
    mod_c = mod_all[8].reshape(6, D)

    pk_w = _pack_params(c_ctx, b_mod, norm1_w, norm2_w, hg_lb_fwd, hg_lb_bwd, hg_norm_w, rt_theta_fwd, rt_theta_bwd, final_norm_w)
    theta_row = pk_w[14:15]
    lb_f, lb_b, lg_row = prep_small(hg_lb_fwd, hg_lb_bwd, theta_row, "prep_small")
    lg_f = jnp.broadcast_to(lg_row[0, 0:HEADS].reshape(HEADS, 1, 1), (HEADS, 1, RT_DV))
    lg_b = jnp.broadcast_to(lg_row[0, HEADS:2 * HEADS].reshape(HEADS, 1, 1), (HEADS, 1, RT_DV))

    rest = [s[0].astype(BF16) for s in (w_proj_hgrn, w_proj_ret, w_out, w_ffn_gate, w_ffn_up, w_ffn_down)]

    dx, full, small = local_step(x[0], ctx[0], loss_target[0], mod_x, mod_c, lb_f, lb_b, lg_f, lg_b,
                                 norm1_w, norm2_w, hg_norm_w, final_norm_w.reshape(1, D),
                                 {"w_in_shard": w_in[0].astype(BF16)}, rest, (ci, core_arg, place_arg))

    g3 = allgather8(small, "gather_small").reshape(N_DEV, SMALL_ROWS, D)
    pk_g, aux = small_grads(g3, lb_f, lb_b, theta_row, "small_grads")
    loss = aux[2, 0]
    dmod16 = jnp.concatenate([
        g3[:, 0:6, :].reshape(N_DEV, 6 * D),
        jnp.concatenate([aux[0], aux[1], jnp.zeros((4 * D,), F32)]).reshape(1, 6 * D),
        jnp.zeros((7, 6 * D), F32)], axis=0)
    dmod_sh = lax.dynamic_slice_in_dim(dmod16, chip * MOD_SH, MOD_SH, axis=1)
    g_wmod, dsilu = mod_bwd(call16, dmod_sh, w_mod[0], "mod_bwd")
    g4 = allgather8(dsilu, "gather_dsilu")
    pk_m = _pack_params(m_c_ctx, m_b_mod, m_norm1_w, m_norm2_w, m_hg_lb_fwd, m_hg_lb_bwd, m_hg_norm_w, m_rt_theta_fwd, m_rt_theta_bwd, m_final_norm_w)
    pk_v = _pack_params(v_c_ctx, v_b_mod, v_norm1_w, v_norm2_w, v_hg_lb_fwd, v_hg_lb_bwd, v_hg_norm_w, v_rt_theta_fwd, v_rt_theta_bwd, v_final_norm_w)
    pk_g, pk_d, pk_nm, pk_nv = adamw_small(g4, pk_g, pk_w, pk_m, pk_v, "adamw_small")

    big = {
        "w_mod": (g_wmod, w_mod, m_w_mod, v_w_mod),
        "w_in": (full["w_in"], w_in, m_w_in, v_w_in),
        "w_pa": (full["w_pa"], w_proj_hgrn, m_w_proj_hgrn, v_w_proj_hgrn),
        "w_pb": (full["w_pb"], w_proj_ret, m_w_proj_ret, v_w_proj_ret),
        "w_out": (full["w_out"], w_out, m_w_out, v_w_out),
        "wg": (full["wg"], w_ffn_gate, m_w_ffn_gate, v_w_ffn_gate),
        "wu": (full["wu"], w_ffn_up, m_w_ffn_up, v_w_ffn_up),
        "wd": (full["wd"], w_ffn_down, m_w_ffn_down, v_w_ffn_down),
    }
    res = {}
    for k, (g, wt, mt, vt) in big.items():
        d, nm, nv = adamw(wt[0], g, mt[0], vt[0], "adamw_" + k)
        res[k] = (g[None], d[None], nm[None], nv[None])

    sm = [_unpack_params(p) for p in (pk_g, pk_d, pk_nm, pk_nv)]
    outs = []
    for t in range(4):
        (s_cctx, s_bmod, s_n1, s_n2, s_lbf, s_lbb, s_hgn, s_thf, s_thb, s_fin) = sm[t]
        outs.append([s_cctx, res["w_mod"][t], s_bmod, s_n1, s_n2, res["w_in"][t], s_lbf, s_lbb, s_hgn, s_thf, s_thb,
                     res["w_pa"][t], res["w_pb"][t], res["w_out"][t], res["wg"][t], res["wu"][t], res["wd"][t], s_fin])
    return (loss, dx[None], *outs[0], *outs[1], *outs[2], *outs[3])
```

```python
import functools

import jax
import jax.numpy as jnp
from jax import lax
from jax.experimental import pallas as pl
from jax.experimental.pallas import tpu as pltpu

F32 = jnp.float32
BF16 = jnp.bfloat16
HI = lax.Precision.HIGHEST
CUMSUM_PRECISION = lax.Precision.HIGH

D = 1024
HEADS = 8
HG_D = 128
RT_DK = 128
RT_DV = 256
D_FF = 2816
D_IN = 13312
N_SHARD = 4
IN_SH = D_IN // N_SHARD
FF_SH = D_FF // N_SHARD
HG_CHUNK = 32
SCAN_ROWS = 256
HG_GROUP = 8
RT_GROUP = 4
PROJ_ROWS = 1024
EPS = 1e-6
GN_EPS = 1e-5
Q_SCALE = 128.0 ** -0.5
VMEM_LIMIT = 56 * 1024 * 1024

COL_HQ, COL_HFF, COL_HFB, COL_HI, COL_HG = 0, 8, 16, 24, 32
COL_RQ, COL_RK, COL_RV, COL_RG, COL_GA, COL_GB = 40, 48, 56, 72, 88, 96

ADAM_LR, ADAM_B1, ADAM_B2, ADAM_EPS, ADAM_WD, ADAM_STEP = 0.001, 0.9, 0.999, 1e-08, 0.01, 10


def _params(*sem):
    return pltpu.CompilerParams(dimension_semantics=sem, vmem_limit_bytes=VMEM_LIMIT)


def _dot(a, b, ca=1, cb=0, prec=None):
    return lax.dot_general(a, b, (((ca,), (cb,)), ((), ())), precision=prec, preferred_element_type=F32)


def _bdot(a, b, ca=1, cb=0):
    return _dot(a.astype(BF16), b.astype(BF16), ca, cb)


def _sigmoid(z):
    return 1.0 / (1.0 + jnp.exp(-z))


def _rowsum(a):
    return jnp.sum(a, axis=0, keepdims=True)


def _lanemean(a):
    return jnp.mean(a, axis=-1, keepdims=True)


def _grid_step(grid):
    pos, total = 0, 1
    for d, size in enumerate(grid):
        pos = pos * size + pl.program_id(d)
        total *= size
    return pos, total


def normmod_matmul(x, nw, sh, sc, w, name):
    L = x.shape[0]
    tm = min(PROJ_ROWS, L)
    tn = IN_SH // 2

    def body(x_ref, nw_ref, sh_ref, sc_ref, w_ref, p_ref, hx_ref, hx_scr):
        @pl.when(pl.program_id(1) == 0)
        def _():
            xv = x_ref[...]
            n = xv * lax.rsqrt(_lanemean(xv * xv) + EPS) * nw_ref[...]
            h = (n * (1.0 + sc_ref[...]) + sh_ref[...]).astype(BF16)
            hx_scr[...] = h
            hx_ref[...] = h

        p_ref[...] = _dot(hx_scr[...], w_ref[...])

    vec = pl.BlockSpec((1, D), lambda i, j: (0, 0))
    return pl.pallas_call(
        body, name=name,
        grid=(L // tm, D_IN // tn),
        in_specs=[pl.BlockSpec((tm, D), lambda i, j: (i, 0)), vec, vec, vec,
                  pl.BlockSpec((D, tn), lambda i, j: (0, j))],
        out_specs=[pl.BlockSpec((tm, tn), lambda i, j: (i, j)), pl.BlockSpec((tm, D), lambda i, j: (i, 0))],
        out_shape=[jax.ShapeDtypeStruct((L, D_IN), F32), jax.ShapeDtypeStruct((L, D), BF16)],
        scratch_shapes=[pltpu.VMEM((tm, D), BF16)],
        compiler_params=_params("parallel", "arbitrary"),
    )(x, nw, sh, sc, w)


def _w_halves(w_src, col0, tn, wbuf, wsems, pos):
    @pl.when(pos == 0)
    def _():
        for h in range(2):
            pltpu.make_async_copy(w_src.at[:, pl.ds(pl.multiple_of(col0 + h * tn, 128), tn)], wbuf.at[h],
                                  wsems.at[h]).start()

    for h in range(2):
        @pl.when(pos == h)
        def _(h=h):
            pltpu.make_async_copy(w_src.at[:, pl.ds(0, tn)], wbuf.at[h], wsems.at[h]).wait()


def in_proj_own(x, nw, sh, sc, w_shard, shard_arg, name):
    L = x.shape[0]
    tm = min(PROJ_ROWS, L)
    tn = IN_SH // 2
    grid = (L // tm, 2)

    def body(k_ref, x_ref, nw_ref, sh_ref, sc_ref, w_ref, p_ref, hx_ref, wfull_ref, hx_scr, wbuf, wsems, psems,
             *sems):
        pos, total = _grid_step(grid)
        start, forward, finish = _gather_phases([w_ref], [wfull_ref], *sems, relations=(0, 1))
        pl.when(pos == 0)(start)
        _w_halves(w_ref, 0, tn, wbuf, wsems, pos)

        def place(h):
            col = pl.multiple_of(k_ref[0] * IN_SH + h * tn, 128)
            return pltpu.make_async_copy(wbuf.at[h], wfull_ref.at[:, pl.ds(col, tn)], psems.at[h])

        for h in range(2):
            @pl.when(pos == h)
            def _(h=h):
                place(h).start()

        @pl.when(pl.program_id(1) == 0)
        def _():
            xv = x_ref[...]
            n = xv * lax.rsqrt(_lanemean(xv * xv) + EPS) * nw_ref[...]
            h = (n * (1.0 + sc_ref[...]) + sh_ref[...]).astype(BF16)
            hx_scr[...] = h
            hx_ref[...] = h

        p_ref[...] = _dot(hx_scr[...], wbuf[pl.program_id(1)])

        @pl.when(pos == total - 1)
        def _():
            forward()
            finish()
            place(0).wait()
            place(1).wait()

    vec = pl.BlockSpec((1, D), lambda i, j, k: (0, 0))
    return pl.pallas_call(
        body, name=name,
        grid_spec=pltpu.PrefetchScalarGridSpec(
            num_scalar_prefetch=1, grid=grid,
            in_specs=[pl.BlockSpec((tm, D), lambda i, j, k: (i, 0)), vec, vec, vec, ANY],
            out_specs=[pl.BlockSpec((tm, tn), lambda i, j, k: (i, 2 * k[0] + j)),
                       pl.BlockSpec((tm, D), lambda i, j, k: (i, 0)), ANY],
            scratch_shapes=[pltpu.VMEM((tm, D), BF16), pltpu.VMEM((2, D, tn), BF16), pltpu.SemaphoreType.DMA((2,)),
                            pltpu.SemaphoreType.DMA((2,))] + _gather_scratch(1)),
        out_shape=[jax.ShapeDtypeStruct((L, D_IN), F32), jax.ShapeDtypeStruct((L, D), BF16),
                   jax.ShapeDtypeStruct((D, D_IN), BF16)],
        compiler_params=_params("arbitrary", "arbitrary"),
    )(shard_arg, x, nw, sh, sc, w_shard)


def in_proj_next(hx, w_full, shard_arg, p, name, diag_from=None, gather=()):
    L = hx.shape[0]
    tm = min(PROJ_ROWS, L)
    tn = IN_SH // 2
    grid = (L // tm, 2)
    diag = diag_from is not None
    ng = len(gather)
    assert not (diag and ng)

    def body(k_ref, hx_ref, wf_in, p_in, *refs):
        n_src = 1 if diag else ng
        srcs = refs[:n_src]
        p_ref = refs[n_src]
        dsts = refs[n_src + 1:2 * n_src + 1]
        wbuf, wsems = refs[2 * n_src + 1:2 * n_src + 3]
        sems = refs[2 * n_src + 3:2 * n_src + 6]
        stage = refs[2 * n_src + 6:]
        pos, total = _grid_step(grid)
        w_src = dsts[0] if diag else wf_in
        if diag:
            start, forward, finish = _gather_phases(srcs, dsts, *sems, relations=(2,))
        elif ng:
            start, forward, finish = _gather_phases(srcs, dsts, *sems, stage=stage)
        if n_src:
            pl.when(pos == 0)(start)
        _w_halves(w_src, k_ref[0] * IN_SH, tn, wbuf, wsems, pos)
        p_ref[...] = _dot(hx_ref[...], wbuf[pl.program_id(1)])
        if n_src:
            @pl.when(pos == total - 1)
            def _():
                forward()
                finish()

    srcs = [diag_from] if diag else list(gather)
    out_shape = [jax.ShapeDtypeStruct((L, D_IN), F32)]
    if diag:
        out_shape.append(jax.ShapeDtypeStruct(w_full.shape, w_full.dtype))
    out_shape += [jax.ShapeDtypeStruct((N_SHARD,) + s.shape, s.dtype) for s in gather]
    aliases = {3: 0, 2: 1} if diag else {3: 0}
    return pl.pallas_call(
        body, name=name,
        grid_spec=pltpu.PrefetchScalarGridSpec(
            num_scalar_prefetch=1, grid=grid,
            in_specs=[pl.BlockSpec((tm, D), lambda i, j, k: (i, 0)), ANY, ANY] + [ANY] * len(srcs),
            out_specs=[pl.BlockSpec((tm, tn), lambda i, j, k: (i, 2 * k[0] + j))] + [ANY] * len(srcs),
            scratch_shapes=[pltpu.VMEM((2, D, tn), BF16), pltpu.SemaphoreType.DMA((2,))]
            + (_gather_scratch(len(srcs)) if srcs else []) + [pltpu.VMEM(s.shape, s.dtype) for s in gather]),
        out_shape=out_shape,
        input_output_aliases=aliases,
        compiler_params=_params("arbitrary", "arbitrary"),
    )(shard_arg, hx, w_full, p, *srcs)


def _hgrn_gates(z, lb):
    sg = _sigmoid(z)
    sgn = _sigmoid(-z)
    f = lb + (1.0 - lb) * sg
    k = (1.0 - lb) * sgn
    return sg, sgn, f, k


def _tri_chunks(n, chunk, reverse):
    r = lax.broadcasted_iota(jnp.int32, (n, n), 0)
    c = lax.broadcasted_iota(jnp.int32, (n, n), 1)
    same = (r // chunk) == (c // chunk)
    return jnp.where(same & ((r <= c) if reverse else (r >= c)), 1.0, 0.0).astype(F32)


def _decay3(b, reverse, key_major=False):
    C = b.shape[0]
    i0 = lax.broadcasted_iota(jnp.int32, (C, C, 1), 0)
    i1 = lax.broadcasted_iota(jnp.int32, (C, C, 1), 1)
    t, s = (i1, i0) if key_major else (i0, i1)
    mask = (t <= s) if reverse else (t >= s)
    diff = (b[None, :, :] - b[:, None, :]) if key_major else (b[:, None, :] - b[None, :, :])
    return jnp.exp(jnp.where(mask, diff, -jnp.inf))


HG_SUB = 8


def _hgrn_pairs(reverse):
    pairs = []
    size = HG_SUB
    while size < HG_CHUNK:
        for lo in range(0, HG_CHUNK, 2 * size):
            first, second = slice(lo, lo + size), slice(lo + size, lo + 2 * size)
            if reverse:
                pairs.append((first, second, lo + size))
            else:
                pairs.append((second, first, lo + size - 1))
        size *= 2
    return pairs


def _head_mask(g, nq, nk):
    r = lax.broadcasted_iota(jnp.int32, (g * nq, g * nk), 0) // nq
    c = lax.broadcasted_iota(jnp.int32, (g * nq, g * nk), 1) // nk
    return jnp.where(r == c, 1.0, 0.0).astype(F32)


def _hgrn_masks(g, reverse):
    return [_head_mask(g, qr.stop - qr.start, kr.stop - kr.start) for qr, kr, _ in _hgrn_pairs(reverse)]


def _stack(xs):
    return jnp.concatenate(xs, axis=0)


def _unstack(x, g):
    n = x.shape[0] // g
    return [x[h * n:(h + 1) * n] for h in range(g)]


def _add_blocks(acc, rows, part):
    for i in range(part.shape[0] // HG_SUB):
        acc[rows.start // HG_SUB + i] += part[i * HG_SUB:(i + 1) * HG_SUB]


def _hgrn_intra_fwd(qs, ks, vs, bs, masks, reverse):
    g = len(qs)
    blocks = []
    for q, k, v, b in zip(qs, ks, vs, bs):
        mine = []
        for lo in range(0, HG_CHUNK, HG_SUB):
            r = slice(lo, lo + HG_SUB)
            e3 = _decay3(b[r], reverse, key_major=True)
            att3 = jnp.sum(q[r][None, :, :] * k[r][:, None, :] * e3, axis=-1, keepdims=True)
            mine.append(jnp.sum(att3 * v[r][:, None, :], axis=0))
        blocks.append(mine)
    for (qr, kr, ref), mask in zip(_hgrn_pairs(reverse), masks):
        qt = _stack([q[qr] * jnp.exp(b[qr] - b[ref:ref + 1]) for q, b in zip(qs, bs)])
        kt = _stack([k[kr] * jnp.exp(b[ref:ref + 1] - b[kr]) for k, b in zip(ks, bs)])
        att = _bdot(qt, kt, 1, 1) * mask
        for mine, part in zip(blocks, _unstack(_bdot(att, _stack([v[kr] for v in vs])), g)):
            _add_blocks(mine, qr, part)
    return [jnp.concatenate(mine, axis=0) for mine in blocks]


def _hgrn_intra_bwd(qs, ks, vs, bs, d_os, masks, reverse):
    g = len(qs)
    nb = HG_CHUNK // HG_SUB
    dqs, dks, dvs = [], [], []
    for q, k, v, b, d_o in zip(qs, ks, vs, bs, d_os):
        dq, dk, dv = [None] * nb, [None] * nb, [None] * nb
        for i in range(nb):
            r = slice(i * HG_SUB, (i + 1) * HG_SUB)
            e3 = _decay3(b[r], reverse)
            p3 = jnp.sum(d_o[r][:, None, :] * v[r][None, :, :], axis=-1, keepdims=True) * e3
            dq[i] = jnp.sum(p3 * k[r][None, :, :], axis=1)
            dk[i] = jnp.sum(p3 * q[r][:, None, :], axis=0)
            att3 = jnp.sum(q[r][:, None, :] * k[r][None, :, :] * e3, axis=-1, keepdims=True)
            dv[i] = jnp.sum(att3 * d_o[r][:, None, :], axis=0)
        dqs.append(dq)
        dks.append(dk)
        dvs.append(dv)
    for (qr, kr, ref), mask in zip(_hgrn_pairs(reverse), masks):
        fqs = [jnp.exp(b[qr] - b[ref:ref + 1]) for b in bs]
        fks = [jnp.exp(b[ref:ref + 1] - b[kr]) for b in bs]
        qt = _stack([q[qr] * f for q, f in zip(qs, fqs)])
        kt = _stack([k[kr] * f for k, f in zip(ks, fks)])
        do_q = _stack([d_o[qr] for d_o in d_os])
        att = _bdot(qt, kt, 1, 1) * mask
        datt = _bdot(do_q, _stack([v[kr] for v in vs]), 1, 1) * mask
        for dq, part, f in zip(dqs, _unstack(_bdot(datt, kt), g), fqs):
            _add_blocks(dq, qr, part * f)
        for dk, part, f in zip(dks, _unstack(_bdot(datt, qt, 0, 0), g), fks):
            _add_blocks(dk, kr, part * f)
        for dv, part in zip(dvs, _unstack(_bdot(att, do_q, 0, 0), g)):
            _add_blocks(dv, kr, part)

    def cat(parts):
        return [jnp.concatenate(p, axis=0) for p in parts]

    return cat(dqs), cat(dks), cat(dvs)


def _hgrn_state_step(k, v, b, s_t, last):
    b_last = b[last:last + 1]
    return s_t * jnp.exp(b_last) + _bdot(v, k * jnp.exp(b_last - b), 0, 0)


def hgrn_scan_fwd(p, lb, s0, col_z, reverse, name, prev=None):
    has_prev = prev is not None
    L = p.shape[0]
    nB = L // SCAN_ROWS
    nC = SCAN_ROWS // HG_CHUNK
    C = HG_CHUNK
    G, W = HG_GROUP, HG_GROUP * HG_D
    last = 0 if reverse else C - 1

    def bmap(b):
        return (nB - 1 - b) if reverse else b

    def body(q_ref, z_ref, v_ref, lb_ref, s0_ref, *refs):
        prev_ref = refs[0] if has_prev else None
        o_ref, sfin_ref, sblk_ref, s_scr, k_scr, b_scr = refs[1:] if has_prev else refs
        blk = pl.program_id(1)

        @pl.when(blk == 0)
        def _():
            s_scr[...] = s0_ref[...]

        sblk_ref[...] = s_scr[...]
        _, _, f_all, k_all = _hgrn_gates(z_ref[...], lb_ref[...])
        k_scr[...] = k_all
        b_scr[...] = _dot(_tri_chunks(SCAN_ROWS, C, reverse), jnp.log(f_all), prec=CUMSUM_PRECISION)

        masks = _hgrn_masks(G, reverse)
        heads = [slice(j * HG_D, (j + 1) * HG_D) for j in range(G)]

        def chunk(ci, carry):
            c = (nC - 1 - ci) if reverse else ci
            rows = pl.ds(pl.multiple_of(c * C, C), C)
            qs = [q_ref[rows, lanes] * Q_SCALE for lanes in heads]
            vs = [v_ref[rows, lanes] for lanes in heads]
            ks = [k_scr[rows, lanes] for lanes in heads]
            bs = [b_scr[rows, lanes] for lanes in heads]
            o_in = _hgrn_intra_fwd(qs, ks, vs, bs, masks, reverse)
            for j, lanes in enumerate(heads):
                s_t = s_scr[j]
                o = o_in[j] + _bdot(qs[j] * jnp.exp(bs[j]), s_t, 1, 1)
                o_ref[rows, lanes] = o + prev_ref[rows, lanes] if has_prev else o
                s_scr[j] = _hgrn_state_step(ks[j], vs[j], bs[j], s_t, last)
            return carry

        lax.fori_loop(0, nC, chunk, 0)

        @pl.when(blk == nB - 1)
        def _():
            sfin_ref[...] = s_scr[...]

    def col(c0):
        return pl.BlockSpec((SCAN_ROWS, W), lambda h, b: (bmap(b), c0 // G + h))

    state = pl.BlockSpec((G, HG_D, HG_D), lambda h, b: (h, 0, 0))
    return pl.pallas_call(
        body, name=name,
        grid=(HEADS // G, nB),
        in_specs=[col(COL_HQ), col(col_z), col(COL_HI), pl.BlockSpec((1, W), lambda h, b: (0, h)), state]
        + ([pl.BlockSpec((SCAN_ROWS, W), lambda h, b: (bmap(b), h))] if has_prev else []),
        out_specs=[pl.BlockSpec((SCAN_ROWS, W), lambda h, b: (bmap(b), h)), state,
                   pl.BlockSpec((None, G, HG_D, HG_D), lambda h, b: (bmap(b), h, 0, 0))],
        out_shape=[jax.ShapeDtypeStruct((L, D), F32),
                   jax.ShapeDtypeStruct((HEADS, HG_D, HG_D), F32),
                   jax.ShapeDtypeStruct((nB, HEADS, HG_D, HG_D), F32)],
        scratch_shapes=[pltpu.VMEM((G, HG_D, HG_D), F32), pltpu.VMEM((SCAN_ROWS, W), F32),
                        pltpu.VMEM((SCAN_ROWS, W), F32)],
        compiler_params=_params("parallel", "arbitrary"),
    )(p, p, p, lb, s0, *([prev] if has_prev else []))


def hgrn_scan_bwd(p, lb, s_blocks, d_o, ds_fin, prev, col_z, reverse, name):
    L = p.shape[0]
    nB = L // SCAN_ROWS
    nC = SCAN_ROWS // HG_CHUNK
    C = HG_CHUNK
    G, W = HG_GROUP, HG_GROUP * HG_D
    last = 0 if reverse else C - 1
    has_prev = prev is not None
    out_dt = BF16 if has_prev else F32

    def bmap(b):
        return b if reverse else (nB - 1 - b)

    def body(*refs):
        q_ref, z_ref, v_ref, lb_ref, sblk_ref, do_ref, dsf_ref = refs[:7]
        refs = refs[7:]
        if has_prev:
            pq_ref, pv_ref = refs[:2]
            refs = refs[2:]
        (dq_ref, dz_ref, dv_ref, dlb_ref, ds0_ref, st_scr, run_scr, ds_scr, k_scr, b_scr, db_scr, dk_scr,
         rf_scr, sgn_scr, dzf_scr) = refs
        blk = pl.program_id(1)

        @pl.when(blk == 0)
        def _():
            ds_scr[...] = dsf_ref[...]
            dlb_ref[...] = jnp.zeros_like(dlb_ref)

        tri = _tri_chunks(SCAN_ROWS, C, reverse)
        row = lax.broadcasted_iota(jnp.int32, (C, HG_D), 0)
        lb_all = lb_ref[...]
        sg_all, sgn_all, f_all, k_all = _hgrn_gates(z_ref[...], lb_all)
        k_scr[...] = k_all
        rf_scr[...] = 1.0 / f_all
        sgn_scr[...] = sgn_all
        dzf_scr[...] = (1.0 - lb_all) * sg_all * sgn_all
        b_scr[...] = _dot(tri, jnp.log(f_all), prec=CUMSUM_PRECISION)
        run_scr[...] = sblk_ref[...]

        def recompute(ci, carry):
            c = (nC - 1 - ci) if reverse else ci
            rows = pl.ds(pl.multiple_of(c * C, C), C)
            for j in range(G):
                lanes = slice(j * HG_D, (j + 1) * HG_D)
                s_t = run_scr[j]
                st_scr[c, j] = s_t
                run_scr[j] = _hgrn_state_step(k_scr[rows, lanes], v_ref[rows, lanes], b_scr[rows, lanes], s_t, last)
            return carry

        lax.fori_loop(0, nC, recompute, 0)

        masks = _hgrn_masks(G, reverse)
        heads = [slice(j * HG_D, (j + 1) * HG_D) for j in range(G)]

        def chunk(ci, carry):
            c = ci if reverse else (nC - 1 - ci)
            rows = pl.ds(pl.multiple_of(c * C, C), C)
            ks = [k_scr[rows, lanes] for lanes in heads]
            bs = [b_scr[rows, lanes] for lanes in heads]
            qs = [q_ref[rows, lanes] * Q_SCALE for lanes in heads]
            vs = [v_ref[rows, lanes] for lanes in heads]
            d_os = [do_ref[rows, lanes] for lanes in heads]
            dq_ins, dk_ins, dv_ins = _hgrn_intra_bwd(qs, ks, vs, bs, d_os, masks, reverse)
            for j, lanes in enumerate(heads):
                k, b, q, v, d_o = ks[j], bs[j], qs[j], vs[j], d_os[j]
                s_t = st_scr[c, j]
                ds_t = ds_scr[j]
                eb = jnp.exp(b)
                b_last = b[last:last + 1]
                eb_last = jnp.exp(b_last)
                kdec = jnp.exp(b_last - b)
                qe = q * eb
                ke = k * kdec
                dq_tot = _bdot(d_o, s_t, 1, 0) * eb + dq_ins[j]
                dke = _bdot(v, ds_t, 1, 0)
                dk_tot = dke * kdec + dk_ins[j]
                dv = dv_ins[j] + _bdot(ke, ds_t, 1, 1)
                db_last = _rowsum(dke * ke) + eb_last * _rowsum(ds_t * s_t)
                db_scr[rows, lanes] = q * dq_tot - k * dk_tot + jnp.where(row == last, db_last, 0.0)
                dk_scr[rows, lanes] = dk_tot
                dq = dq_tot * Q_SCALE
                if has_prev:
                    dq = dq + pq_ref[rows, lanes]
                    dv = dv + pv_ref[rows, lanes]
                dq_ref[rows, lanes] = dq.astype(out_dt)
                dv_ref[rows, lanes] = dv.astype(out_dt)
                ds_scr[j] = ds_t * eb_last + _bdot(d_o, qe, 0, 0)
            return carry

        lax.fori_loop(0, nC, chunk, 0)

        g = _dot(tri, db_scr[...], 0, 0, prec=CUMSUM_PRECISION) * rf_scr[...] - dk_scr[...]
        dz_ref[...] = (g * dzf_scr[...]).astype(BF16)
        dlb_ref[...] += _rowsum(g * sgn_scr[...])

        @pl.when(blk == nB - 1)
        def _():
            ds0_ref[...] = ds_scr[...]

    def col(c0):
        return pl.BlockSpec((SCAN_ROWS, W), lambda h, b: (bmap(b), c0 // G + h))

    tile = pl.BlockSpec((SCAN_ROWS, W), lambda h, b: (bmap(b), h))
    state = pl.BlockSpec((G, HG_D, HG_D), lambda h, b: (h, 0, 0))
    in_specs = [col(COL_HQ), col(col_z), col(COL_HI),
                pl.BlockSpec((1, W), lambda h, b: (0, h)),
                pl.BlockSpec((None, G, HG_D, HG_D), lambda h, b: (bmap(b), h, 0, 0)),
                tile, state]
    args = [p, p, p, lb, s_blocks, d_o, ds_fin]
    if has_prev:
        in_specs += [tile, tile]
        args += list(prev)
    return pl.pallas_call(
        body, name=name,
        grid=(HEADS // G, nB),
        in_specs=in_specs,
        out_specs=[tile, tile, tile, pl.BlockSpec((1, W), lambda h, b: (0, h)), state],
        out_shape=[jax.ShapeDtypeStruct((L, D), out_dt), jax.ShapeDtypeStruct((L, D), BF16),
                   jax.ShapeDtypeStruct((L, D), out_dt), jax.ShapeDtypeStruct((1, D), F32),
                   jax.ShapeDtypeStruct((HEADS, HG_D, HG_D), F32)],
        scratch_shapes=[pltpu.VMEM((nC, G, HG_D, HG_D), F32), pltpu.VMEM((G, HG_D, HG_D), F32),
                        pltpu.VMEM((G, HG_D, HG_D), F32)] + [pltpu.VMEM((SCAN_ROWS, W), F32)] * 7,
        compiler_params=_params("parallel", "arbitrary"),
    )(*args)


def _rope(t, cosf, sinf):
    return t * cosf + pltpu.roll(t, RT_DK // 2, 1) * sinf


def _rope_t(d, cosf, sinf):
    return d * cosf + pltpu.roll(d * sinf, RT_DK // 2, 1)


def _ret_decays(lg, reverse):
    C = SCAN_ROWS
    t = lax.broadcasted_iota(jnp.int32, (C, C), 0)
    s = lax.broadcasted_iota(jnp.int32, (C, C), 1)
    delta = ((s - t) if reverse else (t - s)).astype(F32)
    dmat = jnp.where(delta >= 0, jnp.exp(lg * jnp.maximum(delta, 0.0)), 0.0)
    r = lax.broadcasted_iota(jnp.int32, (C, RT_DK), 0)
    pos = ((C - 1 - r) if reverse else r).astype(F32)
    lg1 = lg[:, :RT_DK]
    qdec = jnp.exp(lg1 * (pos + 1.0))
    kdec = jnp.exp(lg1 * (C - 1.0 - pos))
    sdec = jnp.exp(lg1 * float(C))
    return dmat, delta, pos, qdec, kdec, sdec


def ret_scan_fwd(p, cosf, sinf, lg, s0, reverse, name, prev=None):
    has_prev = prev is not None
    L = p.shape[0]
    C = SCAN_ROWS
    nB = L // C

    def bmap(b):
        return (nB - 1 - b) if reverse else b

    G = RT_GROUP

    def body(q_ref, k_ref, v_ref, cos_ref, sin_ref, lg_ref, s0_ref, *refs):
        prev_ref = refs[0] if has_prev else None
        o_ref, sfin_ref, sblk_ref, s_scr = refs[1:] if has_prev else refs
        blk = pl.program_id(1)

        @pl.when(blk == 0)
        def _():
            s_scr[...] = s0_ref[...]

        sblk_ref[...] = s_scr[...]
        cosf, sinf = cos_ref[...], sin_ref[...]
        for j in range(G):
            lk, lv = slice(j * RT_DK, (j + 1) * RT_DK), slice(j * RT_DV, (j + 1) * RT_DV)
            s_t = s_scr[j]
            dmat, _, _, qdec, kdec, sdec = _ret_decays(lg_ref[j], reverse)
            q = _rope(q_ref[:, lk] * Q_SCALE, cosf, sinf)
            k = _rope(k_ref[:, lk], cosf, sinf)
            v = v_ref[:, lv]
            att = _bdot(q, k, 1, 1) * dmat
            o = _bdot(att, v) + _bdot(q * qdec, s_t, 1, 1)
            o_ref[:, lv] = o + prev_ref[:, lv] if has_prev else o
            s_scr[j] = s_t * sdec + _bdot(v, k * kdec, 0, 0)

        @pl.when(blk == nB - 1)
        def _():
            sfin_ref[...] = s_scr[...]

    def col(c0):
        return pl.BlockSpec((C, G * RT_DK), lambda h, b: (bmap(b), c0 // G + h))

    tab = pl.BlockSpec((C, RT_DK), lambda h, b: (bmap(b), 0))
    state = pl.BlockSpec((G, RT_DV, RT_DK), lambda h, b: (h, 0, 0))
    return pl.pallas_call(
        body, name=name,
        grid=(HEADS // G, nB),
        in_specs=[col(COL_RQ), col(COL_RK),
                  pl.BlockSpec((C, G * RT_DV), lambda h, b: (bmap(b), COL_RV // (2 * G) + h)),
                  tab, tab, pl.BlockSpec((G, 1, RT_DV), lambda h, b: (h, 0, 0)), state]
        + ([pl.BlockSpec((C, G * RT_DV), lambda h, b: (bmap(b), h))] if has_prev else []),
        out_specs=[pl.BlockSpec((C, G * RT_DV), lambda h, b: (bmap(b), h)), state,
                   pl.BlockSpec((None, G, RT_DV, RT_DK), lambda h, b: (bmap(b), h, 0, 0))],
        out_shape=[jax.ShapeDtypeStruct((L, HEADS * RT_DV), F32),
                   jax.ShapeDtypeStruct((HEADS, RT_DV, RT_DK), F32),
                   jax.ShapeDtypeStruct((nB, HEADS, RT_DV, RT_DK), F32)],
        scratch_shapes=[pltpu.VMEM((G, RT_DV, RT_DK), F32)],
        compiler_params=_params("parallel", "arbitrary"),
    )(p, p, p, cosf, sinf, lg, s0, *([prev] if has_prev else []))


def ret_scan_bwd(p, cosf, sinf, lg, s_blocks, d_o, ds_fin, prev, reverse, name):
    L = p.shape[0]
    C = SCAN_ROWS
    nB = L // C
    has_prev = prev is not None
    out_dt = BF16
    G = RT_GROUP

    def bmap(b):
        return b if reverse else (nB - 1 - b)

    def body(*refs):
        q_ref, k_ref, v_ref, cos_ref, sin_ref, lg_ref, sblk_ref, do_ref, dsf_ref = refs[:9]
        refs = refs[9:]
        if has_prev:
            pq_ref, pk_ref, pv_ref = refs[:3]
            refs = refs[3:]
        dq_ref, dk_ref, dv_ref, dlg_ref, ds0_ref, ds_scr = refs
        blk = pl.program_id(1)

        @pl.when(blk == 0)
        def _():
            ds_scr[...] = dsf_ref[...]
            dlg_ref[...] = jnp.zeros_like(dlg_ref)

        cosf, sinf = cos_ref[...], sin_ref[...]
        for j in range(G):
            lk, lv = slice(j * RT_DK, (j + 1) * RT_DK), slice(j * RT_DV, (j + 1) * RT_DV)
            s_t = sblk_ref[j]
            ds_t = ds_scr[j]
            dmat, delta, pos, qdec, kdec, sdec = _ret_decays(lg_ref[j], reverse)
            q = _rope(q_ref[:, lk] * Q_SCALE, cosf, sinf)
            k = _rope(k_ref[:, lk], cosf, sinf)
            v = v_ref[:, lv]
            d_o = do_ref[:, lv]
            att_raw = _bdot(q, k, 1, 1)
            datt_m = _bdot(d_o, v, 1, 1) * dmat
            dqd = _bdot(d_o, s_t, 1, 0)
            dkd = _bdot(v, ds_t, 1, 0)
            dq = _bdot(datt_m, k) + dqd * qdec
            dk = _bdot(datt_m, q, 0, 0) + dkd * kdec
            dv = _bdot(att_raw * dmat, d_o, 0, 0) + _bdot(k * kdec, ds_t, 1, 1)
            ds_scr[j] = ds_t * sdec + _bdot(d_o, q * qdec, 0, 0)
            t1 = jnp.sum(_rowsum(datt_m * att_raw * delta), axis=-1, keepdims=True)
            t23 = jnp.sum(_rowsum((pos + 1.0) * qdec * q * dqd + (C - 1.0 - pos) * kdec * k * dkd), axis=-1, keepdims=True)
            t4 = jnp.sum(_rowsum(ds_t * s_t * sdec), axis=-1, keepdims=True) * float(C)
            dlg_ref[j] += jnp.broadcast_to(t1 + t23 + t4, (1, RT_DK))
            if has_prev:
                dq = _rope_t(dq + pq_ref[:, lk].astype(F32), cosf, sinf) * Q_SCALE
                dk = _rope_t(dk + pk_ref[:, lk].astype(F32), cosf, sinf)
                dv = dv + pv_ref[:, lv].astype(F32)
            dq_ref[:, lk] = dq.astype(out_dt)
            dk_ref[:, lk] = dk.astype(out_dt)
            dv_ref[:, lv] = dv.astype(out_dt)

        @pl.when(blk == nB - 1)
        def _():
            ds0_ref[...] = ds_scr[...]

    def col(c0):
        return pl.BlockSpec((C, G * RT_DK), lambda h, b: (bmap(b), c0 // G + h))

    tab = pl.BlockSpec((C, RT_DK), lambda h, b: (bmap(b), 0))
    state = pl.BlockSpec((G, RT_DV, RT_DK), lambda h, b: (h, 0, 0))
    tk = pl.BlockSpec((C, G * RT_DK), lambda h, b: (bmap(b), h))
    tv = pl.BlockSpec((C, G * RT_DV), lambda h, b: (bmap(b), h))
    in_specs = [col(COL_RQ), col(COL_RK),
                pl.BlockSpec((C, G * RT_DV), lambda h, b: (bmap(b), COL_RV // (2 * G) + h)),
                tab, tab, pl.BlockSpec((G, 1, RT_DV), lambda h, b: (h, 0, 0)),
                pl.BlockSpec((None, G, RT_DV, RT_DK), lambda h, b: (bmap(b), h, 0, 0)),
                tv, state]
    args = [p, p, p, cosf, sinf, lg, s_blocks, d_o, ds_fin]
    if has_prev:
        in_specs += [tk, tk, tv]
        args += list(prev)
    return pl.pallas_call(
        body, name=name,
        grid=(HEADS // G, nB),
        in_specs=in_specs,
        out_specs=[tk, tk, tv, pl.BlockSpec((G, 1, RT_DK), lambda h, b: (h, 0, 0)), state],
        out_shape=[jax.ShapeDtypeStruct((L, D), out_dt), jax.ShapeDtypeStruct((L, D), out_dt),
                   jax.ShapeDtypeStruct((L, HEADS * RT_DV), out_dt),
                   jax.ShapeDtypeStruct((HEADS, 1, RT_DK), F32),
                   jax.ShapeDtypeStruct((HEADS, RT_DV, RT_DK), F32)],
        scratch_shapes=[pltpu.VMEM((G, RT_DV, RT_DK), F32)],
        compiler_params=_params("parallel", "arbitrary"),
    )(*args)


def _silu_parts(h):
    s = _sigmoid(h)
    return h * s, s * (1.0 + h * (1.0 - s))


def _head_rms(o):
    outs, rs = [], []
    for h in range(HEADS):
        oh = o[:, h * HG_D:(h + 1) * HG_D]
        r = lax.rsqrt(_lanemean(oh * oh) + EPS)
        outs.append(oh * r)
        rs.append(r)
    return outs, rs


def _group_norm(o):
    outs, rs = [], []
    for h in range(HEADS):
        oh = o[:, h * RT_DV:(h + 1) * RT_DV]
        c = oh - _lanemean(oh)
        r = lax.rsqrt(_lanemean(c * c) + GN_EPS)
        outs.append(c * r)
        rs.append(r)
    return outs, rs


MIX_ROWS = 256


def _mix_specs(rows):
    def t(w, c=0):
        return pl.BlockSpec((rows, w), lambda i: (i, c))

    return t


def mix_fwd(o_hg, o_rt, p, x, g1, hgw, w_pa, w_pb, w_out, name):
    L = x.shape[0]
    t = _mix_specs(MIX_ROWS)

    def body(ohg_ref, ort_ref, hg_ref, rg0_ref, rg1_ref, ga_ref, gb_ref, x_ref, g1_ref, hgw_ref,
             wpa_ref, wpb_ref, wout_ref, x1_ref, xmix_ref, merged_ref, ya_ref, yb_ref):
        nh, _ = _head_rms(ohg_ref[...])
        ya = jnp.concatenate(nh, axis=1) * hgw_ref[...] * _silu_parts(hg_ref[...])[0]
        gn, _ = _group_norm(ort_ref[...])
        rg = jnp.concatenate([rg0_ref[...], rg1_ref[...]], axis=1)
        yb = jnp.concatenate(gn, axis=1) * _silu_parts(rg)[0]
        ya16, yb16 = ya.astype(BF16), yb.astype(BF16)
        merged = (_sigmoid(ga_ref[...]) * _dot(ya16, wpa_ref[...])
                  + _sigmoid(gb_ref[...]) * _dot(yb16, wpb_ref[...])).astype(BF16)
        x_mix = _dot(merged, wout_ref[...])
        x1_ref[...] = x_ref[...] + g1_ref[...] * x_mix
        xmix_ref[...] = x_mix
        merged_ref[...] = merged
        ya_ref[...] = ya16
        yb_ref[...] = yb16

    vec = pl.BlockSpec((1, D), lambda i: (0, 0))

    def full(a):
        return pl.BlockSpec(a.shape, lambda i: (0, 0), pipeline_mode=pl.Buffered(1))

    return pl.pallas_call(
        body, name=name,
        grid=(L // MIX_ROWS,),
        in_specs=[t(D), t(2 * D), t(D, COL_HG // 8), t(D, COL_RG // 8), t(D, COL_RG // 8 + 1),
                  t(D, COL_GA // 8), t(D, COL_GB // 8), t(D), vec, vec, full(w_pa), full(w_pb), full(w_out)],
        out_specs=[t(D), t(D), t(D), t(D), t(2 * D)],
        out_shape=[jax.ShapeDtypeStruct((L, D), F32), jax.ShapeDtypeStruct((L, D), F32),
                   jax.ShapeDtypeStruct((L, D), BF16), jax.ShapeDtypeStruct((L, D), BF16),
                   jax.ShapeDtypeStruct((L, 2 * D), BF16)],
        compiler_params=_params("parallel"),
    )(o_hg, o_rt, p, p, p, p, p, x, g1, hgw, w_pa, w_pb, w_out)


def mix_bwd(dx1, x_mix, ya, yb, o_hg, o_rt, p, g1, hgw, w_pa, w_pb, w_out, name, to_sibling=()):
    L = dx1.shape[0]
    t = _mix_specs(MIX_ROWS)
    nx = len(to_sibling)
    steps = L // MIX_ROWS

    def body(dx1_ref, xmix_ref, ya_ref, yb_ref, ohg_ref, ort_ref, hg_ref, rg0_ref, rg1_ref,
             ga_ref, gb_ref, g1_ref, hgw_ref, wpa_ref, wpb_ref, wout_ref, *refs):
        (dxm_ref, da_ref, db_ref, dga_ref, dgb_ref, dhg_ref, drg_ref, dohg_ref, dort_ref,
         sums_ref) = refs[nx:nx + 10]
        if nx:
            start, finish = _to_sibling_phases(refs[:nx], refs[nx + 10:2 * nx + 10], *refs[2 * nx + 10:])
            pl.when(pl.program_id(0) == 0)(start)
            pl.when(pl.program_id(0) == steps - 1)(finish)

        @pl.when(pl.program_id(0) == 0)
        def _():
            sums_ref[...] = jnp.zeros_like(sums_ref)

        dx1 = dx1_ref[...]
        dxm = (g1_ref[...] * dx1).astype(BF16)
        dxm_ref[...] = dxm
        dmerged = _dot(dxm, wout_ref[...], 1, 1)
        a = _dot(ya_ref[...], wpa_ref[...])
        bm = _dot(yb_ref[...], wpb_ref[...])
        sa, sb = _sigmoid(ga_ref[...]), _sigmoid(gb_ref[...])
        d_a = (dmerged * sa).astype(BF16)
        d_b = (dmerged * sb).astype(BF16)
        da_ref[...] = d_a
        db_ref[...] = d_b
        dga_ref[...] = (dmerged * a * sa * (1.0 - sa)).astype(BF16)
        dgb_ref[...] = (dmerged * bm * sb * (1.0 - sb)).astype(BF16)
        dya = _dot(d_a, wpa_ref[...], 1, 1)
        dyb = _dot(d_b, wpb_ref[...], 1, 1)

        hgw = hgw_ref[...]
        silu_h, dsilu_h = _silu_parts(hg_ref[...])
        nh, rh = _head_rms(ohg_ref[...])
        n = jnp.concatenate(nh, axis=1)
        dhg_ref[...] = (dya * n * hgw * dsilu_h).astype(BF16)
        dn = dya * hgw * silu_h
        douts = []
        for h in range(HEADS):
            dnh = dn[:, h * HG_D:(h + 1) * HG_D]
            douts.append(rh[h] * (dnh - nh[h] * _lanemean(dnh * nh[h])))
        dohg_ref[...] = jnp.concatenate(douts, axis=1)

        rg = jnp.concatenate([rg0_ref[...], rg1_ref[...]], axis=1)
        silu_r, dsilu_r = _silu_parts(rg)
        gn, rr = _group_norm(ort_ref[...])
        g = jnp.concatenate(gn, axis=1)
        drg_ref[...] = (dyb * g * dsilu_r).astype(BF16)
        dgn = dyb * silu_r
        douts = []
        for h in range(HEADS):
            dgh = dgn[:, h * RT_DV:(h + 1) * RT_DV]
            douts.append(rr[h] * (dgh - _lanemean(dgh) - gn[h] * _lanemean(dgh * gn[h])))
        dort_ref[...] = jnp.concatenate(douts, axis=1)

        sums_ref[0:1, :] += _rowsum(dx1 * xmix_ref[...])
        sums_ref[1:2, :] += _rowsum(dya * n * silu_h)

    vec = pl.BlockSpec((1, D), lambda i: (0, 0))

    def full(a):
        return pl.BlockSpec(a.shape, lambda i: (0, 0), pipeline_mode=pl.Buffered(1))

    bf = functools.partial(jax.ShapeDtypeStruct, dtype=BF16)
    return pl.pallas_call(
        body, name=name,
        grid=(L // MIX_ROWS,),
        in_specs=[t(D), t(D), t(D), t(2 * D), t(D), t(2 * D),
                  t(D, COL_HG // 8), t(D, COL_RG // 8), t(D, COL_RG // 8 + 1), t(D, COL_GA // 8), t(D, COL_GB // 8),
                  vec, vec, full(w_pa), full(w_pb), full(w_out)] + [ANY] * nx,
        out_specs=[t(D), t(D), t(D), t(D), t(D), t(D), t(2 * D), t(D), t(2 * D),
                   pl.BlockSpec((8, D), lambda i: (0, 0))] + [ANY] * nx,
        out_shape=[bf((L, D)), bf((L, D)), bf((L, D)), bf((L, D)), bf((L, D)), bf((L, D)), bf((L, 2 * D)),
                   jax.ShapeDtypeStruct((L, D), F32), jax.ShapeDtypeStruct((L, 2 * D), F32),
                   jax.ShapeDtypeStruct((8, D), F32)] + [jax.ShapeDtypeStruct(a.shape, a.dtype) for a in to_sibling],
        scratch_shapes=_to_sibling_scratch(nx) if nx else [],
        compiler_params=_params("arbitrary"),
    )(dx1, x_mix, ya, yb, o_hg, o_rt, p, p, p, p, p, g1, hgw, w_pa, w_pb, w_out, *to_sibling)


FFN_ROWS = 512


def ffn_fwd(x1, target, nw2, sh2, sc2, g2, fw, wg, wu, wd, name):
    L = x1.shape[0]
    tm = min(FFN_ROWS, L)

    def body(x1_ref, tgt_ref, nw2_ref, sh2_ref, sc2_ref, g2_ref, fw_ref, wg_ref, wu_ref, wd_ref,
             hx2_ref, g_ref, u_ref, h_ref, f_ref, dx2_ref, sums_ref, hx_scr, acc):
        i, j = pl.program_id(0), pl.program_id(1)

        @pl.when((i == 0) & (j == 0))
        def _():
            sums_ref[...] = jnp.zeros_like(sums_ref)

        @pl.when(j == 0)
        def _():
            xv = x1_ref[...]
            n = xv * lax.rsqrt(_lanemean(xv * xv) + EPS) * nw2_ref[...]
            h = (n * (1.0 + sc2_ref[...]) + sh2_ref[...]).astype(BF16)
            hx_scr[...] = h
            hx2_ref[...] = h
            acc[...] = jnp.zeros_like(acc)

        hx = hx_scr[...]
        g = _dot(hx, wg_ref[...])
        u = _dot(hx, wu_ref[...])
        hh = (_silu_parts(g)[0] * u).astype(BF16)
        g_ref[...] = g
        u_ref[...] = u
        h_ref[...] = hh
        acc[...] += _dot(hh, wd_ref[...])

        @pl.when(j == N_SHARD - 1)
        def _():
            f = acc[...]
            f_ref[...] = f
            x2 = x1_ref[...] + g2_ref[...] * f
            r = lax.rsqrt(_lanemean(x2 * x2) + EPS)
            fw = fw_ref[...]
            e = x2 * r * fw - tgt_ref[...]
            dy = e * (1.0 / D)
            dyw = dy * fw
            dx2_ref[...] = r * dyw - x2 * (r * r * r) * _lanemean(dyw * x2)
            sums_ref[0:1, :] += _rowsum(dy * x2 * r)
            sums_ref[1:2, :] += _rowsum(e * e) * (0.5 / D)

    row = pl.BlockSpec((tm, D), lambda i, j: (i, 0))
    vec = pl.BlockSpec((1, D), lambda i, j: (0, 0))
    sh = pl.BlockSpec((None, tm, FF_SH), lambda i, j: (j, i, 0))
    return pl.pallas_call(
        body, name=name,
        grid=(L // tm, N_SHARD),
        in_specs=[row, row, vec, vec, vec, vec, vec,
                  pl.BlockSpec((None, D, FF_SH), lambda i, j: (j, 0, 0)),
                  pl.BlockSpec((None, D, FF_SH), lambda i, j: (j, 0, 0)),
                  pl.BlockSpec((None, FF_SH, D), lambda i, j: (j, 0, 0))],
        out_specs=[row, sh, sh, sh, row, row, pl.BlockSpec((8, D), lambda i, j: (0, 0))],
        out_shape=[jax.ShapeDtypeStruct((L, D), BF16),
                   jax.ShapeDtypeStruct((N_SHARD, L, FF_SH), F32), jax.ShapeDtypeStruct((N_SHARD, L, FF_SH), F32),
                   jax.ShapeDtypeStruct((N_SHARD, L, FF_SH), BF16),
                   jax.ShapeDtypeStruct((L, D), F32), jax.ShapeDtypeStruct((L, D), F32),
                   jax.ShapeDtypeStruct((8, D), F32)],
        scratch_shapes=[pltpu.VMEM((tm, D), BF16), pltpu.VMEM((tm, D), F32)],
        compiler_params=_params("arbitrary", "arbitrary"),
    )(x1, target, nw2, sh2, sc2, g2, fw, wg, wu, wd)


def ffn_bwd(dx2, x1, f, g, u, nw2, sc2, g2, wg, wu, wd, name):
    L = x1.shape[0]
    tm = min(FFN_ROWS, L)

    def body(dx2_ref, x1_ref, f_ref, g_ref, u_ref, nw2_ref, sc2_ref, g2_ref, wg_ref, wu_ref, wd_ref,
             df_ref, dg_ref, du_ref, dx1_ref, sums_ref, df_scr, acc):
        i, j = pl.program_id(0), pl.program_id(1)

        @pl.when((i == 0) & (j == 0))
        def _():
            sums_ref[...] = jnp.zeros_like(sums_ref)

        @pl.when(j == 0)
        def _():
            dx2 = dx2_ref[...]
            df = (g2_ref[...] * dx2).astype(BF16)
            df_scr[...] = df
            df_ref[...] = df
            sums_ref[0:1, :] += _rowsum(dx2 * f_ref[...])
            acc[...] = jnp.zeros_like(acc)

        dh = _dot(df_scr[...], wd_ref[...], 1, 1)
        gv, uv = g_ref[...], u_ref[...]
        silu_g, dsilu_g = _silu_parts(gv)
        dg = (dh * uv * dsilu_g).astype(BF16)
        du = (dh * silu_g).astype(BF16)
        dg_ref[...] = dg
        du_ref[...] = du
        acc[...] += _dot(dg, wg_ref[...], 1, 1) + _dot(du, wu_ref[...], 1, 1)

        @pl.when(j == N_SHARD - 1)
        def _():
            dhx = acc[...]
            xv = x1_ref[...]
            r = lax.rsqrt(_lanemean(xv * xv) + EPS)
            n0 = xv * r
            nw = nw2_ref[...]
            dn2 = dhx * (1.0 + sc2_ref[...])
            dn0 = dn2 * nw
            dx1_ref[...] = dx2_ref[...] + r * (dn0 - n0 * _lanemean(dn0 * n0))
            sums_ref[1:2, :] += _rowsum(dhx)
            sums_ref[2:3, :] += _rowsum(dhx * n0 * nw)
            sums_ref[3:4, :] += _rowsum(dn2 * n0)

    row = pl.BlockSpec((tm, D), lambda i, j: (i, 0))
    vec = pl.BlockSpec((1, D), lambda i, j: (0, 0))
    sh = pl.BlockSpec((None, tm, FF_SH), lambda i, j: (j, i, 0))
    return pl.pallas_call(
        body, name=name,
        grid=(L // tm, N_SHARD),
        in_specs=[row, row, row, sh, sh, vec, vec, vec,
                  pl.BlockSpec((None, D, FF_SH), lambda i, j: (j, 0, 0)),
                  pl.BlockSpec((None, D, FF_SH), lambda i, j: (j, 0, 0)),
                  pl.BlockSpec((None, FF_SH, D), lambda i, j: (j, 0, 0))],
        out_specs=[row, sh, sh, row, pl.BlockSpec((8, D), lambda i, j: (0, 0))],
        out_shape=[jax.ShapeDtypeStruct((L, D), BF16),
                   jax.ShapeDtypeStruct((N_SHARD, L, FF_SH), BF16), jax.ShapeDtypeStruct((N_SHARD, L, FF_SH), BF16),
                   jax.ShapeDtypeStruct((L, D), F32), jax.ShapeDtypeStruct((8, D), F32)],
        scratch_shapes=[pltpu.VMEM((tm, D), BF16), pltpu.VMEM((tm, D), F32)],
        compiler_params=_params("arbitrary", "arbitrary"),
    )(dx2, x1, f, g, u, nw2, sc2, g2, wg, wu, wd)


def matmul_tn(a, b, name, acc_init=None, to_chips=()):
    na, K, M = a.shape
    nb, _, N = b.shape
    n = max(na, nb)
    tk = min(512, K)
    tn = N if N <= 1024 else N // 2
    nk = K // tk
    grid = (n, N // tn, nk)
    has_init = acc_init is not None
    nx = len(to_chips)

    def body(a_ref, b_ref, *refs):
        init_ref = refs[0] if has_init else None
        refs = refs[1:] if has_init else refs
        o_ref = refs[nx]
        if nx:
            start, finish = _to_chips_phases(refs[:nx], refs[nx + 1:2 * nx + 1], *refs[2 * nx + 1:])
            pos, total = _grid_step(grid)
            pl.when(pos == 0)(start)
        kk = pl.program_id(2)

        @pl.when(kk == 0)
        def _():
            o_ref[...] = init_ref[...] if has_init else jnp.zeros_like(o_ref)

        o_ref[...] += _dot(a_ref[...], b_ref[...], 0, 0)
        if nx:
            pl.when(pos == total - 1)(finish)

    out_spec = pl.BlockSpec((None, M, tn), lambda s, j, kk: (s, 0, j))
    in_specs = [pl.BlockSpec((None, tk, M), lambda s, j, kk: (s if na > 1 else 0, kk, 0)),
                pl.BlockSpec((None, tk, tn), lambda s, j, kk: (s if nb > 1 else 0, kk, j))]
    args = [a, b]
    if has_init:
        in_specs.append(out_spec)
        args.append(acc_init)
    out = pl.pallas_call(
        body, name=name,
        grid=grid,
        in_specs=in_specs + [ANY] * nx,
        out_specs=[out_spec] + [ANY] * nx,
        out_shape=[jax.ShapeDtypeStruct((n, M, N), F32)] + _to_chips_shapes(to_chips),
        scratch_shapes=_to_chips_scratch(nx) if nx else [],
        compiler_params=_params(*(("arbitrary",) * 3 if nx else ("parallel", "parallel", "arbitrary"))),
    )(*args, *to_chips)
    return out if nx else out[0]


def matmul_tn_pair(a, b1, b2, name):
    K, M = a.shape
    n, _, N = b1.shape
    tk = min(512, K)

    def body(a_ref, b1_ref, b2_ref, o1_ref, o2_ref):
        @pl.when(pl.program_id(1) == 0)
        def _():
            o1_ref[...] = jnp.zeros_like(o1_ref)
            o2_ref[...] = jnp.zeros_like(o2_ref)

        at = a_ref[...].T
        o1_ref[...] += _dot(at, b1_ref[...])
        o2_ref[...] += _dot(at, b2_ref[...])

    b_spec = pl.BlockSpec((None, tk, N), lambda s, kk: (s, kk, 0))
    o_spec = pl.BlockSpec((None, M, N), lambda s, kk: (s, 0, 0))
    return pl.pallas_call(
        body, name=name,
        grid=(n, K // tk),
        in_specs=[pl.BlockSpec((tk, M), lambda s, kk: (kk, 0)), b_spec, b_spec],
        out_specs=[o_spec, o_spec],
        out_shape=[jax.ShapeDtypeStruct((n, M, N), F32)] * 2,
        compiler_params=_params("parallel", "arbitrary"),
    )(a, b1, b2)


PIECE_COLS = 1024
N_PIECE_BLOCKS = D_IN // PIECE_COLS


def _piece_blocks(pieces):
    out, col = [], 0
    for arr, width in pieces:
        if arr is not None:
            out.append((arr, col // PIECE_COLS, width // PIECE_COLS))
        col += width
    assert col == D_IN
    return out


def _piece_feed(p_refs, blocks, buf, sems, tile_of, pos, total):
    def present(blk):
        ok = None
        for _, b0, nb in blocks:
            mine = (blk >= b0) & (blk < b0 + nb)
            ok = mine if ok is None else ok | mine
        return ok

    def fetch(step):
        blk, rows = tile_of(step)
        for p_ref, (_, b0, nb) in zip(p_refs, blocks):
            for t in range(nb):
                @pl.when(blk == b0 + t)
                def _(p_ref=p_ref, t=t):
                    pltpu.make_async_copy(p_ref.at[rows, pl.ds(t * PIECE_COLS, PIECE_COLS)], buf.at[step % 2],
                                          sems.at[step % 2]).start()

    @pl.when(pos == 0)
    def _():
        fetch(pos)

    @pl.when(pos + 1 < total)
    def _():
        fetch(pos + 1)

    def landed():
        slot = pos % 2
        pltpu.make_async_copy(p_refs[0].at[pl.ds(0, buf.shape[1]), pl.ds(0, PIECE_COLS)], buf.at[slot],
                              sems.at[slot]).wait()
        return buf.at[slot]

    return present(tile_of(pos)[0]), landed


def matmul_tn_pieces(a, pieces, name, acc_init=None, to_chips=()):
    K, M = a.shape
    blocks = _piece_blocks(pieces)
    tk = min(1024, K)
    nk = K // tk
    grid = (N_PIECE_BLOCKS, nk)
    has_init = acc_init is not None
    nx, npc = len(to_chips), len(blocks)

    def body(a_ref, *refs):
        p_refs = refs[:npc]
        refs = refs[npc:]
        init_ref = refs[0] if has_init else None
        refs = refs[1:] if has_init else refs
        o_ref = refs[nx]
        buf, sems = refs[2 * nx + 1:2 * nx + 3]
        pos, total = _grid_step(grid)
        if nx:
            start, finish = _to_chips_phases(refs[:nx], refs[nx + 1:2 * nx + 1], *refs[2 * nx + 3:])
            pl.when(pos == 0)(start)
        here, landed = _piece_feed(p_refs, blocks, buf, sems,
                                   lambda s: (s // nk, pl.ds(pl.multiple_of((s % nk) * tk, tk), tk)), pos, total)

        @pl.when(pl.program_id(1) == 0)
        def _():
            o_ref[...] = init_ref[...] if has_init else jnp.zeros_like(o_ref)

        @pl.when(here)
        def _():
            o_ref[...] += _dot(a_ref[...], landed()[...], 0, 0)

        if nx:
            pl.when(pos == total - 1)(finish)

    out_spec = pl.BlockSpec((M, PIECE_COLS), lambda blk, kk: (0, blk))
    in_specs = [pl.BlockSpec((tk, M), lambda blk, kk: (kk, 0))] + [ANY] * npc
    args = [a] + [arr for arr, _, _ in blocks]
    if has_init:
        in_specs.append(out_spec)
        args.append(acc_init)
    out = pl.pallas_call(
        body, name=name,
        grid=grid,
        in_specs=in_specs + [ANY] * nx,
        out_specs=[out_spec] + [ANY] * nx,
        out_shape=[jax.ShapeDtypeStruct((M, D_IN), F32)] + _to_chips_shapes(to_chips),
        scratch_shapes=[pltpu.VMEM((2, tk, PIECE_COLS), BF16), pltpu.SemaphoreType.DMA((2,))]
        + (_to_chips_scratch(nx) if nx else []),
        compiler_params=_params("arbitrary", "arbitrary"),
    )(*args, *to_chips)
    return out if nx else out[0]


def dhx_normbwd(pieces, w, x, dx_res, nw, sc, name, to_chips=()):
    L = x.shape[0]
    tm = min(PROJ_ROWS, L)
    blocks = _piece_blocks(pieces)
    grid = (L // tm, N_PIECE_BLOCKS)
    nx, npc = len(to_chips), len(blocks)

    def body(*refs):
        p_refs = refs[:npc]
        w_ref, x_ref, res_ref, nw_ref, sc_ref = refs[npc:npc + 5]
        refs = refs[npc + 5:]
        dx_ref, sums_ref = refs[nx:nx + 2]
        acc, buf, sems = refs[2 * nx + 2:2 * nx + 5]
        pos, total = _grid_step(grid)
        if nx:
            start, finish = _to_chips_phases(refs[:nx], refs[nx + 2:2 * nx + 2], *refs[2 * nx + 5:])
            pl.when(pos == 0)(start)
            pl.when(pos == total - 1)(finish)
        here, landed = _piece_feed(
            p_refs, blocks, buf, sems,
            lambda s: (s % N_PIECE_BLOCKS, pl.ds(pl.multiple_of((s // N_PIECE_BLOCKS) * tm, tm), tm)), pos, total)
        i, blk = pl.program_id(0), pl.program_id(1)

        @pl.when((i == 0) & (blk == 0))
        def _():
            sums_ref[...] = jnp.zeros_like(sums_ref)

        @pl.when(blk == 0)
        def _():
            acc[...] = jnp.zeros_like(acc)

        @pl.when(here)
        def _():
            acc[...] += _dot(landed()[...], w_ref[...], 1, 1)

        @pl.when(blk == N_PIECE_BLOCKS - 1)
        def _():
            dhx = acc[...]
            xv = x_ref[...]
            r = lax.rsqrt(_lanemean(xv * xv) + EPS)
            n0 = xv * r
            nw = nw_ref[...]
            dn = dhx * (1.0 + sc_ref[...])
            dn0 = dn * nw
            dx_ref[...] = res_ref[...] + r * (dn0 - n0 * _lanemean(dn0 * n0))
            sums_ref[0:1, :] += _rowsum(dhx)
            sums_ref[1:2, :] += _rowsum(dhx * n0 * nw)
            sums_ref[2:3, :] += _rowsum(dn * n0)

    row = pl.BlockSpec((tm, D), lambda i, blk: (i, 0))
    vec = pl.BlockSpec((1, D), lambda i, blk: (0, 0))
    return pl.pallas_call(
        body, name=name,
        grid=grid,
        in_specs=[ANY] * npc + [pl.BlockSpec((D, PIECE_COLS), lambda i, blk: (0, blk)), row, row, vec, vec] + [ANY] * nx,
        out_specs=[row, pl.BlockSpec((8, D), lambda i, blk: (0, 0))] + [ANY] * nx,
        out_shape=[jax.ShapeDtypeStruct((L, D), F32), jax.ShapeDtypeStruct((8, D), F32)] + _to_chips_shapes(to_chips),
        scratch_shapes=[pltpu.VMEM((tm, D), F32), pltpu.VMEM((2, tm, PIECE_COLS), BF16), pltpu.SemaphoreType.DMA((2,))]
        + (_to_chips_scratch(nx) if nx else []),
        compiler_params=_params("arbitrary", "arbitrary"),
    )(*[arr for arr, _, _ in blocks], w, x, dx_res, nw, sc, *to_chips)


SMALL_ROWS = 24


def _rope_tables(L):
    rows = L // 64
    freqs = 10000.0 ** (-jnp.arange(RT_DK // 4, dtype=F32) / (RT_DK // 4))
    a_row = jnp.arange(rows, dtype=F32)[:, None] * freqs
    a_col = jnp.arange(64, dtype=F32)[:, None] * freqs

    def spread(f):
        return jnp.concatenate([jnp.repeat(f(a_row), 64, axis=0), jnp.tile(f(a_col), (rows, 1))], axis=-1)

    cos, sin = spread(jnp.cos), spread(jnp.sin)
    return jnp.concatenate([cos, cos], axis=1), jnp.concatenate([-sin, sin], axis=1)


def _pieces(hq, hf_f, hf_b, hi, hg, rq, rk, rv, rg, ga, gb):
    widths = (D, D, D, D, D, D, D, 2 * D, 2 * D, D, D)
    return list(zip((hq, hf_f, hf_b, hi, hg, rq, rk, rv, rg, ga, gb), widths))


def _lane0(a):
    return a[:, 0, 0]


def _pack_small(rows):
    out = [r.reshape(1, D) for r in rows]
    out += [jnp.zeros((1, D), F32)] * (SMALL_ROWS - len(out))
    return jnp.concatenate(out, axis=0)


def _other_half(g, core):
    axis = g.ndim - 2
    h = g.shape[axis] // 2
    return lax.dynamic_slice_in_dim(g, (1 - core) * h, h, axis=axis).astype(BF16)


def _sibling_sums(gs, names, place):
    core, core_arg, _ = place
    received = rs_to_sibling([_other_half(g, core) for g in gs], "rs_to_sibling_" + names[0])
    return [rs_add_sibling(g, r, core_arg, "rs_add_sibling_" + k) for g, r, k in zip(gs, received, names)]


def _staged_in_proj(x, nw, sh, sc, w_shard, rest, chip):
    cx, cy = chip // 2, chip % 2

    def arg(k):
        return jnp.reshape(k, (1,)).astype(jnp.int32)

    p, hx, w_full = in_proj_own(x, nw, sh, sc, w_shard, arg(chip), "in_proj_own")
    p, w_full = in_proj_next(hx, w_full, arg(2 * (1 - cx) + cy), p, "in_proj_x", diag_from=w_shard)
    w_pa, w_pb, w_out, w_wd = rest[0], rest[1], rest[2], rest[5]
    p, g_pa, g_pb, g_out, g_wd = in_proj_next(hx, w_full, arg(2 * cx + 1 - cy), p, "in_proj_y",
                                              gather=[w_pa, w_pb, w_out, w_wd])
    p, g_wg, g_wu = in_proj_next(hx, w_full, arg(3 - chip), p, "in_proj_diag", gather=[rest[3], rest[4]])
    w = {"w_in": w_full, "w_pa": g_pa.reshape(D, D), "w_pb": g_pb.reshape(2 * D, D), "w_out": g_out.reshape(D, D),
         "wg": g_wg, "wu": g_wu, "wd": g_wd}
    return p, hx, w


def local_step(x, ctx, target, mod_x, mod_c, lb_f, lb_b, lg_f, lg_b, nw1, nw2, hgw, fw, w, rest=None, place=None):
    L, Lc = x.shape[0], ctx.shape[0]
    sh1, sc1, g1, sh2, sc2, g2 = (mod_x[i:i + 1] for i in range(6))
    sh1c, sc1c = mod_c[0:1], mod_c[1:2]
    cosf, sinf = _rope_tables(L)
    cosc, sinc = jnp.ones((Lc, RT_DK), F32), jnp.zeros((Lc, RT_DK), F32)
    zero_h = jnp.zeros((HEADS, HG_D, HG_D), F32)
    zero_r = jnp.zeros((HEADS, RT_DV, RT_DK), F32)

    if rest is None:
        p, hx = normmod_matmul(x, nw1, sh1, sc1, w["w_in"], "in_proj")
    else:
        p, hx, w = _staged_in_proj(x, nw1, sh1, sc1, w["w_in_shard"], rest, place[2][0])
    pc, hxc = normmod_matmul(ctx, nw1, sh1c, sc1c, w["w_in"], "ctx_in_proj")
    _, s_hf, cb_hf = hgrn_scan_fwd(pc, lb_f, zero_h, COL_HFF, False, "ctx_hgrn_f")
    _, s_hb, cb_hb = hgrn_scan_fwd(pc, lb_b, zero_h, COL_HFB, True, "ctx_hgrn_b")
    _, s_rf, cb_rf = ret_scan_fwd(pc, cosc, sinc, lg_f, zero_r, False, "ctx_ret_f")
    _, s_rb, cb_rb = ret_scan_fwd(pc, cosc, sinc, lg_b, zero_r, True, "ctx_ret_b")
    ohf, _, xb_hf = hgrn_scan_fwd(p, lb_f, s_hf, COL_HFF, False, "hgrn_f")
    o_hg, _, xb_hb = hgrn_scan_fwd(p, lb_b, s_hb, COL_HFB, True, "hgrn_b", prev=ohf)
    orf, _, xb_rf = ret_scan_fwd(p, cosf, sinf, lg_f, s_rf, False, "ret_f")
    o_rt, _, xb_rb = ret_scan_fwd(p, cosf, sinf, lg_b, s_rb, True, "ret_b", prev=orf)
    x1, x_mix, merged, ya, yb = mix_fwd(o_hg, o_rt, p, x, g1, hgw, w["w_pa"], w["w_pb"], w["w_out"], "mix_fwd")
    hx2, gg, uu, hh, ff, dx2, sums_f = ffn_fwd(x1, target, nw2, sh2, sc2, g2, fw, w["wg"], w["wu"], w["wd"], "ffn_fwd")

    d_f, d_g, d_u, dx1, sums_fb = ffn_bwd(dx2, x1, ff, gg, uu, nw2, sc2, g2, w["wg"], w["wu"], w["wd"], "ffn_bwd")
    dw_gate, dw_up = matmul_tn_pair(hx2, d_g, d_u, "dw_ffn_gate_up")
    grads = {"wg": dw_gate, "wu": dw_up, "wd": matmul_tn(hh, d_f[None], "dw_ffn_down")}
    ffn_names = ["wg", "wu", "wd"]
    swap = () if place is None else [_other_half(grads[k], place[0]) for k in ffn_names]
    dxm, d_a, d_b, dga, dgb, dhg, drg, dohg, dort, sums_m, *from_sibling = mix_bwd(
        dx1, x_mix, ya, yb, o_hg, o_rt, p, g1, hgw, w["w_pa"], w["w_pb"], w["w_out"], "mix_bwd", to_sibling=swap)
    grads["w_out"] = matmul_tn(merged[None], dxm[None], "dw_out").reshape(N_SHARD, D // N_SHARD, D)
    grads["w_pa"] = matmul_tn(ya[None], d_a[None], "dw_proj_hgrn").reshape(N_SHARD, D // N_SHARD, D)
    grads["w_pb"] = matmul_tn(yb[None], d_b[None], "dw_proj_ret").reshape(N_SHARD, 2 * D // N_SHARD, D)

    rq1, rk1, rv1, dlgf_x, ds_rf = ret_scan_bwd(p, cosf, sinf, lg_f, xb_rf, dort, zero_r, None, False, "ret_f_bwd")
    drq, drk, drv, dlgb_x, ds_rb = ret_scan_bwd(p, cosf, sinf, lg_b, xb_rb, dort, zero_r, (rq1, rk1, rv1), True, "ret_b_bwd")
    hq1, dzf, hv1, dlbf_x, ds_hf = hgrn_scan_bwd(p, lb_f, xb_hf, dohg, zero_h, None, COL_HFF, False, "hgrn_f_bwd")
    dhq, dzb, dhv, dlbb_x, ds_hb = hgrn_scan_bwd(p, lb_b, xb_hb, dohg, zero_h, (hq1, hv1), COL_HFB, True, "hgrn_b_bwd")
    dp = _pieces(dhq, dzf, dzb, dhv, dhg, drq, drk, drv, drg, dga, dgb)
    others = ["w_pa", "w_pb", "w_out", "wg", "wu", "wd"]
    if place is None:
        dw_in = matmul_tn_pieces(hx, dp, "dw_in")
    else:
        sums_o = _sibling_sums([grads[k] for k in others[:3]], others[:3], place)
        sums_o += [rs_add_sibling(grads[k], r, place[1], "rs_add_sibling_" + k) for k, r in zip(ffn_names, from_sibling)]
        dw_in, *recv_o = matmul_tn_pieces(hx, dp, "dw_in", to_chips=[a16 for _, a16 in sums_o])

    zc = jnp.zeros((Lc, D), F32)
    zc2 = jnp.zeros((Lc, 2 * D), F32)
    crq1, crk1, crv1, dlgf_c, _ = ret_scan_bwd(pc, cosc, sinc, lg_f, cb_rf, zc2, ds_rf, None, False, "ctx_ret_f_bwd")
    cdrq, cdrk, cdrv, dlgb_c, _ = ret_scan_bwd(pc, cosc, sinc, lg_b, cb_rb, zc2, ds_rb, (crq1, crk1, crv1), True, "ctx_ret_b_bwd")
    chq1, cdzf, chv1, dlbf_c, _ = hgrn_scan_bwd(pc, lb_f, cb_hf, zc, ds_hf, None, COL_HFF, False, "ctx_hgrn_f_bwd")
    cdhq, cdzb, cdhv, dlbb_c, _ = hgrn_scan_bwd(pc, lb_b, cb_hb, zc, ds_hb, (chq1, chv1), COL_HFB, True, "ctx_hgrn_b_bwd")
    dpc = _pieces(cdhq, cdzf, cdzb, cdhv, None, cdrq, cdrk, cdrv, None, None, None)
    _, sums_c = dhx_normbwd(dpc, w["w_in"], ctx, zc, nw1, sc1c, "dctx_in_proj")
    grads["w_in"] = matmul_tn_pieces(hxc, dpc, "dw_in_ctx", acc_init=dw_in)
    if place is None:
        dx, sums_x = dhx_normbwd(dp, w["w_in"], x, dx1, nw1, sc1, "dx_in_proj")
    else:
        sums_i = _sibling_sums([grads["w_in"]], ["w_in"], place)
        dx, sums_x, recv_i = dhx_normbwd(dp, w["w_in"], x, dx1, nw1, sc1, "dx_in_proj", to_chips=[sums_i[0][1]])
        names = ["w_in"] + others
        halves = [rs_add_chips(a, r, place[2], "rs_add_chips_" + k)
                  for (a, _), r, k in zip(sums_i + sums_o, [recv_i] + recv_o, names)]
        grads = dict(zip(names, rs_join_halves(halves, "rs_join_halves")))

    def lg_row(f, b):
        return jnp.concatenate([_lane0(f), _lane0(b), jnp.zeros((D - 2 * HEADS,), F32)])

    small = _pack_small([
        sums_x[0], sums_x[1], sums_m[0], sums_fb[1], sums_fb[2], sums_fb[0],
        sums_c[0], sums_c[1],
        sums_x[2], sums_c[2], sums_fb[3], sums_m[1], sums_f[0],
        dlbf_x, dlbf_c, dlbb_x, dlbb_c,
        lg_row(dlgf_x, dlgb_x), lg_row(dlgf_c, dlgb_c),
        sums_f[1],
    ])
    return dx, grads, small


MESH = pl.DeviceIdType.MESH
ANY = pl.BlockSpec(memory_space=pl.ANY)
N_DEV = 8


def _place():
    return lax.axis_index("x"), lax.axis_index("y"), lax.axis_index("c")


def _other_chips(x, y):
    return [(1 - x, y), (x, 1 - y), (1 - x, 1 - y)]


def allgather8(xs, name):
    m, n = xs.shape

    def body(x_ref, out_ref, send_sems, recv_sems, local_sem):
        x, y, c = _place()
        me, sibling = (x, y, c), (x, y, 1 - c)
        chips = _other_chips(x, y)

        def rows(px, py, pc):
            return out_ref.at[pl.ds((4 * px + 2 * py + pc) * m, m), :]

        def copy(k, block, to, src=None):
            return pltpu.make_async_remote_copy(
                src_ref=rows(*block) if src is None else src, dst_ref=rows(*block),
                send_sem=send_sems.at[k], recv_sem=recv_sems.at[k], device_id=to, device_id_type=MESH)

        mine = pltpu.make_async_copy(x_ref, rows(*me), local_sem)
        mine.start()
        first = [copy(0, me, sibling, src=x_ref)]
        first += [copy(1 + j, me, (*chip, c), src=x_ref) for j, chip in enumerate(chips)]
        for cp in first:
            cp.start()
        passed = [copy(4 + j, (*chip, c), sibling) for j, chip in enumerate(chips)]
        for j, chip in enumerate(chips):
            copy(1 + j, (*chip, c), me).wait_recv()
            passed[j].start()
        copy(0, sibling, me).wait_recv()
        for j, chip in enumerate(chips):
            copy(4 + j, (*chip, 1 - c), me).wait_recv()
        for cp in first + passed:
            cp.wait_send()
        mine.wait()

    return pl.pallas_call(
        body, name=name,
        out_shape=jax.ShapeDtypeStruct((N_DEV * m, n), xs.dtype),
        in_specs=[pl.BlockSpec(memory_space=pltpu.VMEM)],
        out_specs=pl.BlockSpec(memory_space=pltpu.VMEM),
        scratch_shapes=[pltpu.SemaphoreType.DMA((7,)), pltpu.SemaphoreType.DMA((7,)), pltpu.SemaphoreType.DMA],
    )(xs)


def _gather_phases(ins, outs, send_sems, recv_sems, local_sems, relations=(0, 1, 2), stage=None):
    n = len(ins)
    x, y, c = _place()
    chips = _other_chips(x, y)

    def rows(i, core):
        h = ins[i].shape[0] // 2
        return pl.ds(pl.multiple_of(core * h, 16), h)

    def region(i, k, rs):
        if len(outs[i].shape) == 2:
            cols = ins[i].shape[1]
            return outs[i].at[rs, pl.ds(pl.multiple_of(k * cols, 128), cols)]
        return outs[i].at[k, rs, :]

    def landed(i, chip, core):
        return region(i, 2 * chip[0] + chip[1], rows(i, core))

    def copy(i, k, src, dst, to):
        return pltpu.make_async_remote_copy(src_ref=src, dst_ref=dst, send_sem=send_sems.at[6 * i + k],
                                            recv_sem=recv_sems.at[6 * i + k], device_id=to, device_id_type=MESH)

    def lift(i):
        return pltpu.make_async_copy(ins[i], stage[i], local_sems.at[i])

    def drop(i):
        return pltpu.make_async_copy(stage[i], region(i, 2 * x + y, pl.ds(0, ins[i].shape[0])), local_sems.at[i])

    def send(i, j):
        return copy(i, j, ins[i].at[rows(i, c), :], landed(i, (x, y), c), (*chips[j], c))

    def arrived(i, j, core, k):
        return copy(i, k, ins[i].at[rows(i, core), :], landed(i, chips[j], core), (x, y, 1 - c))

    def passed(i, j):
        return copy(i, 3 + j, landed(i, chips[j], c), landed(i, chips[j], c), (x, y, 1 - c))

    def start():
        for i in range(n):
            if stage is not None:
                lift(i).start()
            for j in relations:
                send(i, j).start()

    def forward():
        for i in range(n):
            if stage is not None:
                lift(i).wait()
                drop(i).start()
            for j in relations:
                arrived(i, j, c, j).wait_recv()
                passed(i, j).start()

    def finish():
        for i in range(n):
            for j in relations:
                arrived(i, j, 1 - c, 3 + j).wait_recv()
        for i in range(n):
            for j in relations:
                send(i, j).wait_send()
                passed(i, j).wait_send()
            if stage is not None:
                drop(i).wait()

    return start, forward, finish


def _gather_scratch(n):
    return [pltpu.SemaphoreType.DMA((6 * n,)), pltpu.SemaphoreType.DMA((6 * n,)), pltpu.SemaphoreType.DMA((n,))]


def rs_to_sibling(payloads, name):
    n = len(payloads)

    def body(*refs):
        start, finish = _to_sibling_phases(refs[:n], refs[n:2 * n], *refs[2 * n:])
        start()
        finish()

    return pl.pallas_call(
        body, name=name,
        out_shape=[jax.ShapeDtypeStruct(g.shape, g.dtype) for g in payloads],
        in_specs=[ANY] * n, out_specs=[ANY] * n,
        scratch_shapes=_to_sibling_scratch(n),
    )(*payloads)


def _to_sibling_phases(ins, outs, send_sems, recv_sems):
    def copies():
        x, y, c = _place()
        return [pltpu.make_async_remote_copy(src_ref=ins[i], dst_ref=outs[i], send_sem=send_sems.at[i],
                                             recv_sem=recv_sems.at[i], device_id=(x, y, 1 - c), device_id_type=MESH)
                for i in range(len(ins))]

    def start():
        for cp in copies():
            cp.start()

    def finish():
        for cp in copies():
            cp.wait()

    return start, finish


def _to_sibling_scratch(n):
    return [pltpu.SemaphoreType.DMA((n,)), pltpu.SemaphoreType.DMA((n,))]


def _to_chips_phases(ins, outs, send_sems, recv_sems):
    def copies():
        x, y, c = _place()
        return [pltpu.make_async_remote_copy(
            src_ref=ins[i].at[2 * px + py], dst_ref=outs[i].at[j], send_sem=send_sems.at[3 * i + j],
            recv_sem=recv_sems.at[3 * i + j], device_id=(px, py, c), device_id_type=MESH)
            for i in range(len(ins)) for j, (px, py) in enumerate(_other_chips(x, y))]

    def start():
        for cp in copies():
            cp.start()

    def finish():
        for cp in copies():
            cp.wait()

    return start, finish


def _to_chips_shapes(parts):
    return [jax.ShapeDtypeStruct((3,) + a.shape[1:], a.dtype) for a in parts]


def _to_chips_scratch(n):
    return [pltpu.SemaphoreType.DMA((3 * n,)), pltpu.SemaphoreType.DMA((3 * n,))]


def rs_join_halves(fulls, name):
    n = len(fulls)

    def body(*refs):
        outs = refs[n:2 * n]
        send_sems, recv_sems = refs[2 * n:]
        x, y, c = _place()

        def copy(i, core):
            h = fulls[i].shape[0] // 2
            rows = outs[i].at[pl.ds(pl.multiple_of(core * h, 8), h), :]
            return pltpu.make_async_remote_copy(src_ref=rows, dst_ref=rows, send_sem=send_sems.at[i],
                                                recv_sem=recv_sems.at[i], device_id=(x, y, 1 - c), device_id_type=MESH)

        sent = [copy(i, c) for i in range(n)]
        for cp in sent:
            cp.start()
        for i in range(n):
            copy(i, 1 - c).wait_recv()
        for cp in sent:
            cp.wait_send()

    return pl.pallas_call(
        body, name=name,
        out_shape=[jax.ShapeDtypeStruct(a.shape, a.dtype) for a in fulls],
        in_specs=[ANY] * n, out_specs=[ANY] * n,
        input_output_aliases={i: i for i in range(n)},
        scratch_shapes=[pltpu.SemaphoreType.DMA((n,)), pltpu.SemaphoreType.DMA((n,))],
    )(*fulls)


def _row_tile(rows, cols, limit_bytes=2 * 1024 * 1024, mult=8):
    best = mult
    for t in range(mult, rows + 1, mult):
        if rows % t == 0 and t * cols * 4 <= limit_bytes:
            best = t
    return best


def rs_add_sibling(g, recv, c, name):
    if g.ndim == 2:
        h, C = recv.shape[0], recv.shape[1] // N_SHARD
    else:
        _, h, C = recv.shape
    tr = _row_tile(h, C, mult=16)
    nt = h // tr

    def body(c_ref, g_ref, r_ref, o_ref, o16_ref):
        s = g_ref[...] + r_ref[...].astype(F32)
        o_ref[...] = s
        o16_ref[...] = s.astype(BF16)

    blk = pl.BlockSpec((None, tr, C), lambda k, i, c_ref: (k, i, 0))
    if g.ndim == 2:
        g_spec = pl.BlockSpec((tr, C), lambda k, i, c_ref: (c_ref[0] * nt + i, k))
        r_spec = pl.BlockSpec((tr, C), lambda k, i, c_ref: (i, k))
    else:
        g_spec = pl.BlockSpec((None, tr, C), lambda k, i, c_ref: (k, c_ref[0] * nt + i, 0))
        r_spec = blk
    return pl.pallas_call(
        body, name=name,
        grid_spec=pltpu.PrefetchScalarGridSpec(
            num_scalar_prefetch=1, grid=(N_SHARD, nt),
            in_specs=[g_spec, r_spec],
            out_specs=[blk, blk]),
        out_shape=[jax.ShapeDtypeStruct((N_SHARD, h, C), F32), jax.ShapeDtypeStruct((N_SHARD, h, C), BF16)],
        compiler_params=_params("parallel", "parallel"),
    )(c, g, recv)


def rs_add_chips(part, recv, place, name):
    _, h, C = part.shape
    tr = _row_tile(h, C, mult=16)
    nt = h // tr

    def body(k_ref, p_ref, r_ref, o_ref):
        o_ref[...] = ((p_ref[...] + r_ref[0].astype(F32)) + r_ref[1].astype(F32)) + r_ref[2].astype(F32)

    return pl.pallas_call(
        body, name=name,
        grid_spec=pltpu.PrefetchScalarGridSpec(
            num_scalar_prefetch=1, grid=(nt,),
            in_specs=[pl.BlockSpec((None, tr, C), lambda i, k_ref: (k_ref[0], i, 0)),
                      pl.BlockSpec((3, tr, C), lambda i, k_ref: (0, i, 0))],
            out_specs=pl.BlockSpec((tr, C), lambda i, k_ref: (k_ref[1] * nt + i, 0))),
        out_shape=jax.ShapeDtypeStruct((2 * h, C), F32),
        compiler_params=_params("parallel"),
    )(place, part, recv)


def _adamw_math(w, g, m, v):
    m = ADAM_B1 * m + (1.0 - ADAM_B1) * g
    v = ADAM_B2 * v + (1.0 - ADAM_B2) * (g * g)
    m_hat = m / (1.0 - ADAM_B1 ** ADAM_STEP)
    v_hat = v / (1.0 - ADAM_B2 ** ADAM_STEP)
    delta = -ADAM_LR * (m_hat / (jnp.sqrt(v_hat) + ADAM_EPS) + ADAM_WD * w)
    return delta, m, v


def adamw(w, g, m, v, name):
    R, C = w.shape
    tr = _row_tile(R, C, 1024 * 1024)

    def body(w_ref, g_ref, m_ref, v_ref, go_ref, d_ref, nm_ref, nv_ref):
        g = g_ref[...]
        go_ref[...] = g
        d_ref[...], nm_ref[...], nv_ref[...] = _adamw_math(w_ref[...], g, m_ref[...], v_ref[...])

    blk = pl.BlockSpec((tr, C), lambda i: (i, 0))
    return pl.pallas_call(
        body, name=name, grid=(R // tr,), in_specs=[blk] * 4, out_specs=[blk] * 4,
        out_shape=[jax.ShapeDtypeStruct((R, C), F32)] * 4,
        compiler_params=_params("parallel"),
    )(w, g, m, v)


MOD_SH = 6 * D // N_SHARD
PK_ROWS = 16


def mod_fwd(call16, w_sh, b_sh, name):
    def body(c_ref, w_ref, b_ref, o_ref):
        o_ref[...] = _dot(_silu_parts(c_ref[...])[0], w_ref[...], prec=HI) + b_ref[...]

    return pl.pallas_call(body, name=name, out_shape=jax.ShapeDtypeStruct((16, MOD_SH), F32),
                          compiler_params=_params())(call16, w_sh, b_sh)


def prep_small(lbf2, lbb2, theta_row, name):
    def body(f_ref, b_ref, t_ref, lbf_ref, lbb_ref, lg_ref):
        lbf_ref[...] = _sigmoid(f_ref[0:1, :] - f_ref[1:2, :])
        lbb_ref[...] = _sigmoid(b_ref[0:1, :] - b_ref[1:2, :])
        t = t_ref[...]
        lg_ref[...] = jnp.minimum(t, 0.0) - jnp.log(1.0 + jnp.exp(-jnp.abs(t)))

    row = jax.ShapeDtypeStruct((1, D), F32)
    return pl.pallas_call(body, name=name, out_shape=[row, row, row], compiler_params=_params())(lbf2, lbb2, theta_row)


def small_grads(g3, lbf, lbb, theta_row, name):
    def body(g_ref, lbf_ref, lbb_ref, t_ref, pk_ref, aux_ref):
        s = g_ref[0]
        for d in range(1, N_DEV):
            s = s + g_ref[d]
        pk_ref[...] = jnp.zeros_like(pk_ref)
        aux_ref[...] = jnp.zeros_like(aux_ref)
        pk_ref[1:7, :] = s[0:6]
        pk_ref[1:3, :] += s[6:8]
        pk_ref[7:8, :] = s[8:9] + s[9:10]
        pk_ref[8:9, :] = s[10:11]
        lbf, lbb = lbf_ref[...], lbb_ref[...]
        daf = (s[13:14] + s[14:15]) * lbf * (1.0 - lbf)
        dab = (s[15:16] + s[16:17]) * lbb * (1.0 - lbb)
        pk_ref[9:10, :] = daf
        pk_ref[10:11, :] = -daf
        pk_ref[11:12, :] = dab
        pk_ref[12:13, :] = -dab
        pk_ref[13:14, :] = s[11:12]
        pk_ref[14:15, :] = (s[17:18] + s[18:19]) * _sigmoid(-t_ref[...])
        pk_ref[15:16, :] = s[12:13]
        aux_ref[0:2, :] = s[6:8]
        aux_ref[2:3, :] = jnp.broadcast_to(jnp.sum(s[19:20], axis=-1, keepdims=True), (1, D))

    return pl.pallas_call(body, name=name,
                          out_shape=[jax.ShapeDtypeStruct((PK_ROWS, D), F32), jax.ShapeDtypeStruct((8, D), F32)],
                          compiler_params=_params())(g3, lbf, lbb, theta_row)


def mod_bwd(call16, dmod_sh, w_sh, name):
    def body(c_ref, d_ref, w_ref, dw_ref, ds_ref):
        dm = d_ref[...]
        dw_ref[...] = _dot(_silu_parts(c_ref[...])[0], dm, 0, 0, prec=HI)
        ds_ref[...] = jnp.zeros_like(ds_ref)
        ds_ref[0:1, :] = _dot(dm[8:9, :], w_ref[...], 1, 1, prec=HI)

    return pl.pallas_call(body, name=name,
                          out_shape=[jax.ShapeDtypeStruct((D, MOD_SH), F32), jax.ShapeDtypeStruct((8, D), F32)],
                          compiler_params=_params())(call16, dmod_sh, w_sh)


def adamw_small(g4, pk_g, pk_w, pk_m, pk_v, name):
    def body(g4_ref, g_ref, w_ref, m_ref, v_ref, go_ref, d_ref, nm_ref, nv_ref):
        w = w_ref[...]
        ds = ((g4_ref[0:1, :] + g4_ref[16:17, :]) + g4_ref[32:33, :]) + g4_ref[48:49, :]
        row = lax.broadcasted_iota(jnp.int32, (PK_ROWS, D), 0)
        g = jnp.where(row == 0, ds * _silu_parts(w[0:1, :])[1], g_ref[...])
        go_ref[...] = g
        d_ref[...], nm_ref[...], nv_ref[...] = _adamw_math(w, g, m_ref[...], v_ref[...])

    pk = jax.ShapeDtypeStruct((PK_ROWS, D), F32)
    return pl.pallas_call(body, name=name, out_shape=[pk, pk, pk, pk], compiler_params=_params())(g4, pk_g, pk_w, pk_m, pk_v)


def _pack_params(c_ctx, b_mod, n1, n2, lbf, lbb, hgn, th_f, th_b, fin):
    theta = jnp.concatenate([th_f.reshape(HEADS), th_b.reshape(HEADS), jnp.zeros((D - 2 * HEADS,), F32)])
    return jnp.concatenate([c_ctx.reshape(1, D), b_mod.reshape(6, D), n1.reshape(1, D), n2.reshape(1, D), lbf, lbb,
                            hgn.reshape(1, D), theta.reshape(1, D), fin.reshape(1, D)], axis=0)


def _unpack_params(pk):
    return (pk[0], pk[1:7].reshape(1, 6 * D), pk[7:8], pk[8:9], pk[9:11], pk[11:13], pk[13:14],
            pk[14, 0:HEADS].reshape(1, HEADS), pk[14, HEADS:2 * HEADS].reshape(1, HEADS), pk[15])


def kernel(x, c, ctx, c_ctx, w_mod, b_mod, norm1_w, norm2_w, w_in, hg_lb_fwd, hg_lb_bwd, hg_norm_w, rt_theta_fwd, rt_theta_bwd, w_proj_hgrn, w_proj_ret, w_out, w_ffn_gate, w_ffn_up, w_ffn_down, final_norm_w, loss_target, m_c_ctx, m_w_mod, m_b_mod, m_norm1_w, m_norm2_w, m_w_in, m_hg_lb_fwd, m_hg_lb_bwd, m_hg_norm_w, m_rt_theta_fwd, m_rt_theta_bwd, m_w_proj_hgrn, m_w_proj_ret, m_w_out, m_w_ffn_gate, m_w_ffn_up, m_w_ffn_down, m_final_norm_w, v_c_ctx, v_w_mod, v_b_mod, v_norm1_w, v_norm2_w, v_w_in, v_hg_lb_fwd, v_hg_lb_bwd, v_hg_norm_w, v_rt_theta_fwd, v_rt_theta_bwd, v_w_proj_hgrn, v_w_proj_ret, v_w_out, v_w_ffn_gate, v_w_ffn_up, v_w_ffn_down, v_final_norm_w):
    xi, yi, ci = _place()
    dev = 4 * xi + 2 * yi + ci
    chip = 2 * xi + yi
    core_arg = jnp.reshape(ci, (1,)).astype(jnp.int32)
    place_arg = jnp.stack([chip, ci]).astype(jnp.int32)

    c_all = allgather8(jnp.concatenate([c, jnp.zeros((7, D), F32)], axis=0), "gather_c").reshape(N_DEV, 8, D)[:, 0]
    call16 = jnp.concatenate([c_all, c_ctx.reshape(1, D), jnp.zeros((7, D), F32)], axis=0)
    b_sh = lax.dynamic_slice_in_dim(b_mod, chip * MOD_SH, MOD_SH, axis=1)
    mod_sh = mod_fwd(call16, w_mod[0], b_sh, "mod_fwd")
    mod_g = allgather8(mod_sh, "gather_mod").reshape(N_DEV, 16, MOD_SH)
    mod_all = jnp.concatenate([mod_g[0], mod_g[2], mod_g[4], mod_g[6]], axis=1)
    mod_x = lax.dynamic_index_in_dim(mod_all, dev, axis=0, keepdims=False).reshape(6, D)
    mod_c = mod_all[8].reshape(6, D)

    pk_w = _pack_params(c_ctx, b_mod, norm1_w, norm2_w, hg_lb_fwd, hg_lb_bwd, hg_norm_w, rt_theta_fwd, rt_theta_bwd, final_norm_w)
    theta_row = pk_w[14:15]
    lb_f, lb_b, lg_row = prep_small(hg_lb_fwd, hg_lb_bwd, theta_row, "prep_small")
    lg_f = jnp.broadcast_to(lg_row[0, 0:HEADS].reshape(HEADS, 1, 1), (HEADS, 1, RT_DV))
    lg_b = jnp.broadcast_to(lg_row[0, HEADS:2 * HEADS].reshape(HEADS, 1, 1), (HEADS, 1, RT_DV))

    rest = [s[0].astype(BF16) for s in (w_proj_hgrn, w_proj_ret, w_out, w_ffn_gate, w_ffn_up, w_ffn_down)]

    dx, full, small = local_step(x[0], ctx[0], loss_target[0], mod_x, mod_c, lb_f, lb_b, lg_f, lg_b,
                                 norm1_w, norm2_w, hg_norm_w, final_norm_w.reshape(1, D),
                                 {"w_in_shard": w_in[0].astype(BF16)}, rest, (ci, core_arg, place_arg))

    g3 = allgather8(small, "gather_small").reshape(N_DEV, SMALL_ROWS, D)
    pk_g, aux = small_grads(g3, lb_f, lb_b, theta_row, "small_grads")
    loss = aux[2, 0]
    dmod16 = jnp.concatenate([
        g3[:, 0:6, :].reshape(N_DEV, 6 * D),
        jnp.concatenate([aux[0], aux[1], jnp.zeros((4 * D,), F32)]).reshape(1, 6 * D),
        jnp.zeros((7, 6 * D), F32)], axis=0)
    dmod_sh = lax.dynamic_slice_in_dim(dmod16, chip * MOD_SH, MOD_SH, axis=1)
    g_wmod, dsilu = mod_bwd(call16, dmod_sh, w_mod[0], "mod_bwd")
    g4 = allgather8(dsilu, "gather_dsilu")
    pk_m = _pack_params(m_c_ctx, m_b_mod, m_norm1_w, m_norm2_w, m_hg_lb_fwd, m_hg_lb_bwd, m_hg_norm_w, m_rt_theta_fwd, m_rt_theta_bwd, m_final_norm_w)
    pk_v = _pack_params(v_c_ctx, v_b_mod, v_norm1_w, v_norm2_w, v_hg_lb_fwd, v_hg_lb_bwd, v_hg_norm_w, v_rt_theta_fwd, v_rt_theta_bwd, v_final_norm_w)
    pk_g, pk_d, pk_nm, pk_nv = adamw_small(g4, pk_g, pk_w, pk_m, pk_v, "adamw_small")

    big = {
        "w_mod": (g_wmod, w_mod, m_w_mod, v_w_mod),
        "w_in": (full["w_in"], w_in, m_w_in, v_w_in),
        "w_pa": (full["w_pa"], w_proj_hgrn, m_w_proj_hgrn, v_w_proj_hgrn),
        "w_pb": (full["w_pb"], w_proj_ret, m_w_proj_ret, v_w_proj_ret),
        "w_out": (full["w_out"], w_out, m_w_out, v_w_out),
        "wg": (full["wg"], w_ffn_gate, m_w_ffn_gate, v_w_ffn_gate),
        "wu": (full["wu"], w_ffn_up, m_w_ffn_up, v_w_ffn_up),
        "wd": (full["wd"], w_ffn_down, m_w_ffn_down, v_w_ffn_down),
    }
    res = {}
    for k, (g, wt, mt, vt) in big.items():
        res[k] = tuple(a[None] for a in adamw(wt[0], g, mt[0], vt[0], "adamw_" + k))

    sm = [_unpack_params(p) for p in (pk_g, pk_d, pk_nm, pk_nv)]
    outs = []
    for t in range(4):
        (s_cctx, s_bmod, s_n1, s_n2, s_lbf, s_lbb, s_hgn, s_thf, s_thb, s_fin) = sm[t]
        outs.append([s_cctx, res["w_mod"][t], s_bmod, s_n1, s_n2, res["w_in"][t], s_lbf, s_lbb, s_hgn, s_thf, s_thb,
                     res["w_pa"][t], res["w_pb"][t], res["w_out"][t], res["wg"][t], res["wu"][t], res["wd"][t], s_fin])
    return (loss, dx[None], *outs[0], *outs[1], *outs[2], *outs[3])
```

```python
import functools

import jax
import jax.numpy as jnp
from jax import lax
from jax.experimental import pallas as pl
from jax.experimental.pallas import tpu as pltpu

F32 = jnp.float32
BF16 = jnp.bfloat16
HI = lax.Precision.HIGHEST
CUMSUM_PRECISION = lax.Precision.HIGH

D = 1024
HEADS = 8
HG_D = 128
RT_DK = 128
RT_DV = 256
D_FF = 2816
D_IN = 13312
N_SHARD = 4
IN_SH = D_IN // N_SHARD
FF_SH = D_FF // N_SHARD
HG_CHUNK = 32
SCAN_ROWS = 256
HG_GROUP = 8
RT_GROUP = 4
PROJ_ROWS = 1024
EPS = 1e-6
GN_EPS = 1e-5
Q_SCALE = 128.0 ** -0.5
VMEM_LIMIT = 56 * 1024 * 1024

COL_HQ, COL_HFF, COL_HFB, COL_HI, COL_HG = 0, 8, 16, 24, 32
COL_RQ, COL_RK, COL_RV, COL_RG, COL_GA, COL_GB = 40, 48, 56, 72, 88, 96

ADAM_LR, ADAM_B1, ADAM_B2, ADAM_EPS, ADAM_WD, ADAM_STEP = 0.001, 0.9, 0.999, 1e-08, 0.01, 10


def _params(*sem):
    return pltpu.CompilerParams(dimension_semantics=sem, vmem_limit_bytes=VMEM_LIMIT)


def _dot(a, b, ca=1, cb=0, prec=None):
    return lax.dot_general(a, b, (((ca,), (cb,)), ((), ())), precision=prec, preferred_element_type=F32)


def _bdot(a, b, ca=1, cb=0):
    return _dot(a.astype(BF16), b.astype(BF16), ca, cb)


def _sigmoid(z):
    return 1.0 / (1.0 + jnp.exp(-z))


def _rowsum(a):
    return jnp.sum(a, axis=0, keepdims=True)


def _lanemean(a):
    return jnp.mean(a, axis=-1, keepdims=True)


def _grid_step(grid):
    pos, total = 0, 1
    for d, size in enumerate(grid):
        pos = pos * size + pl.program_id(d)
        total *= size
    return pos, total


def normmod_matmul(x, nw, sh, sc, w, name):
    L = x.shape[0]
    tm = min(PROJ_ROWS, L)
    tn = IN_SH // 2

    def body(x_ref, nw_ref, sh_ref, sc_ref, w_ref, p_ref, hx_ref, hx_scr):
        @pl.when(pl.program_id(1) == 0)
        def _():
            xv = x_ref[...]
            n = xv * lax.rsqrt(_lanemean(xv * xv) + EPS) * nw_ref[...]
            h = (n * (1.0 + sc_ref[...]) + sh_ref[...]).astype(BF16)
            hx_scr[...] = h
            hx_ref[...] = h

        p_ref[...] = _dot(hx_scr[...], w_ref[...])

    vec = pl.BlockSpec((1, D), lambda i, j: (0, 0))
    return pl.pallas_call(
        body, name=name,
        grid=(L // tm, D_IN // tn),
        in_specs=[pl.BlockSpec((tm, D), lambda i, j: (i, 0)), vec, vec, vec,
                  pl.BlockSpec((D, tn), lambda i, j: (0, j))],
        out_specs=[pl.BlockSpec((tm, tn), lambda i, j: (i, j)), pl.BlockSpec((tm, D), lambda i, j: (i, 0))],
        out_shape=[jax.ShapeDtypeStruct((L, D_IN), F32), jax.ShapeDtypeStruct((L, D), BF16)],
        scratch_shapes=[pltpu.VMEM((tm, D), BF16)],
        compiler_params=_params("parallel", "arbitrary"),
    )(x, nw, sh, sc, w)


def _w_halves(w_src, col0, tn, wbuf, wsems, pos):
    @pl.when(pos == 0)
    def _():
        for h in range(2):
            pltpu.make_async_copy(w_src.at[:, pl.ds(pl.multiple_of(col0 + h * tn, 128), tn)], wbuf.at[h],
                                  wsems.at[h]).start()

    for h in range(2):
        @pl.when(pos == h)
        def _(h=h):
            pltpu.make_async_copy(w_src.at[:, pl.ds(0, tn)], wbuf.at[h], wsems.at[h]).wait()


def in_proj_own(x, nw, sh, sc, w_shard, shard_arg, name):
    L = x.shape[0]
    tm = min(PROJ_ROWS, L)
    tn = IN_SH // 2
    grid = (L // tm, 2)

    def body(k_ref, x_ref, nw_ref, sh_ref, sc_ref, w_ref, p_ref, hx_ref, wfull_ref, hx_scr, wbuf, wsems, psems,
             *sems):
        pos, total = _grid_step(grid)
        start, forward, finish = _gather_phases([w_ref], [wfull_ref], *sems, relations=(0, 1))
        pl.when(pos == 0)(start)
        _w_halves(w_ref, 0, tn, wbuf, wsems, pos)

        def place(h):
            col = pl.multiple_of(k_ref[0] * IN_SH + h * tn, 128)
            return pltpu.make_async_copy(wbuf.at[h], wfull_ref.at[:, pl.ds(col, tn)], psems.at[h])

        for h in range(2):
            @pl.when(pos == h)
            def _(h=h):
                place(h).start()

        @pl.when(pl.program_id(1) == 0)
        def _():
            xv = x_ref[...]
            n = xv * lax.rsqrt(_lanemean(xv * xv) + EPS) * nw_ref[...]
            h = (n * (1.0 + sc_ref[...]) + sh_ref[...]).astype(BF16)
            hx_scr[...] = h
            hx_ref[...] = h

        p_ref[...] = _dot(hx_scr[...], wbuf[pl.program_id(1)])

        @pl.when(pos == total - 1)
        def _():
            forward()
            finish()
            place(0).wait()
            place(1).wait()

    vec = pl.BlockSpec((1, D), lambda i, j, k: (0, 0))
    return pl.pallas_call(
        body, name=name,
        grid_spec=pltpu.PrefetchScalarGridSpec(
            num_scalar_prefetch=1, grid=grid,
            in_specs=[pl.BlockSpec((tm, D), lambda i, j, k: (i, 0)), vec, vec, vec, ANY],
            out_specs=[pl.BlockSpec((tm, tn), lambda i, j, k: (i, 2 * k[0] + j)),
                       pl.BlockSpec((tm, D), lambda i, j, k: (i, 0)), ANY],
            scratch_shapes=[pltpu.VMEM((tm, D), BF16), pltpu.VMEM((2, D, tn), BF16), pltpu.SemaphoreType.DMA((2,)),
                            pltpu.SemaphoreType.DMA((2,))] + _gather_scratch(1)),
        out_shape=[jax.ShapeDtypeStruct((L, D_IN), F32), jax.ShapeDtypeStruct((L, D), BF16),
                   jax.ShapeDtypeStruct((D, D_IN), BF16)],
        compiler_params=_params("arbitrary", "arbitrary"),
    )(shard_arg, x, nw, sh, sc, w_shard)


def in_proj_next(hx, w_full, shard_arg, p, name, diag_from=None, gather=()):
    L = hx.shape[0]
    tm = min(PROJ_ROWS, L)
    tn = IN_SH // 2
    grid = (L // tm, 2)
    diag = diag_from is not None
    ng = len(gather)
    assert not (diag and ng)

    def body(k_ref, hx_ref, wf_in, p_in, *refs):
        n_src = 1 if diag else ng
        srcs = refs[:n_src]
        p_ref = refs[n_src]
        dsts = refs[n_src + 1:2 * n_src + 1]
        wbuf, wsems = refs[2 * n_src + 1:2 * n_src + 3]
        sems = refs[2 * n_src + 3:2 * n_src + 6]
        stage = refs[2 * n_src + 6:]
        pos, total = _grid_step(grid)
        w_src = dsts[0] if diag else wf_in
        if diag:
            start, forward, finish = _gather_phases(srcs, dsts, *sems, relations=(2,))
        elif ng:
            start, forward, finish = _gather_phases(srcs, dsts, *sems, stage=stage)
        if n_src:
            pl.when(pos == 0)(start)
        _w_halves(w_src, k_ref[0] * IN_SH, tn, wbuf, wsems, pos)
        p_ref[...] = _dot(hx_ref[...], wbuf[pl.program_id(1)])
        if n_src:
            @pl.when(pos == total - 1)
            def _():
                forward()
                finish()

    srcs = [diag_from] if diag else list(gather)
    out_shape = [jax.ShapeDtypeStruct((L, D_IN), F32)]
    if diag:
        out_shape.append(jax.ShapeDtypeStruct(w_full.shape, w_full.dtype))
    out_shape += [jax.ShapeDtypeStruct((N_SHARD,) + s.shape, s.dtype) for s in gather]
    aliases = {3: 0, 2: 1} if diag else {3: 0}
    return pl.pallas_call(
        body, name=name,
        grid_spec=pltpu.PrefetchScalarGridSpec(
            num_scalar_prefetch=1, grid=grid,
            in_specs=[pl.BlockSpec((tm, D), lambda i, j, k: (i, 0)), ANY, ANY] + [ANY] * len(srcs),
            out_specs=[pl.BlockSpec((tm, tn), lambda i, j, k: (i, 2 * k[0] + j))] + [ANY] * len(srcs),
            scratch_shapes=[pltpu.VMEM((2, D, tn), BF16), pltpu.SemaphoreType.DMA((2,))]
            + (_gather_scratch(len(srcs)) if srcs else []) + [pltpu.VMEM(s.shape, s.dtype) for s in gather]),
        out_shape=out_shape,
        input_output_aliases=aliases,
        compiler_params=_params("arbitrary", "arbitrary"),
    )(shard_arg, hx, w_full, p, *srcs)


def _hgrn_gates(z, lb):
    sg = _sigmoid(z)
    sgn = _sigmoid(-z)
    f = lb + (1.0 - lb) * sg
    k = (1.0 - lb) * sgn
    return sg, sgn, f, k


def _tri_chunks(n, chunk, reverse):
    r = lax.broadcasted_iota(jnp.int32, (n, n), 0)
    c = lax.broadcasted_iota(jnp.int32, (n, n), 1)
    same = (r // chunk) == (c // chunk)
    return jnp.where(same & ((r <= c) if reverse else (r >= c)), 1.0, 0.0).astype(F32)


def _decay3(b, reverse, key_major=False):
    C = b.shape[0]
    i0 = lax.broadcasted_iota(jnp.int32, (C, C, 1), 0)
    i1 = lax.broadcasted_iota(jnp.int32, (C, C, 1), 1)
    t, s = (i1, i0) if key_major else (i0, i1)
    mask = (t <= s) if reverse else (t >= s)
    diff = (b[None, :, :] - b[:, None, :]) if key_major else (b[:, None, :] - b[None, :, :])
    return jnp.exp(jnp.where(mask, diff, -jnp.inf))


HG_SUB = 8


def _hgrn_pairs(reverse):
    pairs = []
    size = HG_SUB
    while size < HG_CHUNK:
        for lo in range(0, HG_CHUNK, 2 * size):
            first, second = slice(lo, lo + size), slice(lo + size, lo + 2 * size)
            if reverse:
                pairs.append((first, second, lo + size))
            else:
                pairs.append((second, first, lo + size - 1))
        size *= 2
    return pairs


def _head_mask(g, nq, nk):
    r = lax.broadcasted_iota(jnp.int32, (g * nq, g * nk), 0) // nq
    c = lax.broadcasted_iota(jnp.int32, (g * nq, g * nk), 1) // nk
    return jnp.where(r == c, 1.0, 0.0).astype(F32)


def _hgrn_masks(g, reverse):
    return [_head_mask(g, qr.stop - qr.start, kr.stop - kr.start) for qr, kr, _ in _hgrn_pairs(reverse)]


def _stack(xs):
    return jnp.concatenate(xs, axis=0)


def _unstack(x, g):
    n = x.shape[0] // g
    return [x[h * n:(h + 1) * n] for h in range(g)]


def _add_blocks(acc, rows, part):
    for i in range(part.shape[0] // HG_SUB):
        acc[rows.start // HG_SUB + i] += part[i * HG_SUB:(i + 1) * HG_SUB]


def _hgrn_intra_fwd(qs, ks, vs, bs, masks, reverse):
    g = len(qs)
    blocks = []
    for q, k, v, b in zip(qs, ks, vs, bs):
        mine = []
        for lo in range(0, HG_CHUNK, HG_SUB):
            r = slice(lo, lo + HG_SUB)
            e3 = _decay3(b[r], reverse, key_major=True)
            att3 = jnp.sum(q[r][None, :, :] * k[r][:, None, :] * e3, axis=-1, keepdims=True)
            mine.append(jnp.sum(att3 * v[r][:, None, :], axis=0))
        blocks.append(mine)
    for (qr, kr, ref), mask in zip(_hgrn_pairs(reverse), masks):
        qt = _stack([q[qr] * jnp.exp(b[qr] - b[ref:ref + 1]) for q, b in zip(qs, bs)])
        kt = _stack([k[kr] * jnp.exp(b[ref:ref + 1] - b[kr]) for k, b in zip(ks, bs)])
        att = _bdot(qt, kt, 1, 1) * mask
        for mine, part in zip(blocks, _unstack(_bdot(att, _stack([v[kr] for v in vs])), g)):
            _add_blocks(mine, qr, part)
    return [jnp.concatenate(mine, axis=0) for mine in blocks]


def _hgrn_intra_bwd(qs, ks, vs, bs, d_os, masks, reverse):
    g = len(qs)
    nb = HG_CHUNK // HG_SUB
    dqs, dks, dvs = [], [], []
    for q, k, v, b, d_o in zip(qs, ks, vs, bs, d_os):
        dq, dk, dv = [None] * nb, [None] * nb, [None] * nb
        for i in range(nb):
            r = slice(i * HG_SUB, (i + 1) * HG_SUB)
            e3 = _decay3(b[r], reverse)
            p3 = jnp.sum(d_o[r][:, None, :] * v[r][None, :, :], axis=-1, keepdims=True) * e3
            dq[i] = jnp.sum(p3 * k[r][None, :, :], axis=1)
            dk[i] = jnp.sum(p3 * q[r][:, None, :], axis=0)
            att3 = jnp.sum(q[r][:, None, :] * k[r][None, :, :] * e3, axis=-1, keepdims=True)
            dv[i] = jnp.sum(att3 * d_o[r][:, None, :], axis=0)
        dqs.append(dq)
        dks.append(dk)
        dvs.append(dv)
    for (qr, kr, ref), mask in zip(_hgrn_pairs(reverse), masks):
        fqs = [jnp.exp(b[qr] - b[ref:ref + 1]) for b in bs]
        fks = [jnp.exp(b[ref:ref + 1] - b[kr]) for b in bs]
        qt = _stack([q[qr] * f for q, f in zip(qs, fqs)])
        kt = _stack([k[kr] * f for k, f in zip(ks, fks)])
        do_q = _stack([d_o[qr] for d_o in d_os])
        att = _bdot(qt, kt, 1, 1) * mask
        datt = _bdot(do_q, _stack([v[kr] for v in vs]), 1, 1) * mask
        for dq, part, f in zip(dqs, _unstack(_bdot(datt, kt), g), fqs):
            _add_blocks(dq, qr, part * f)
        for dk, part, f in zip(dks, _unstack(_bdot(datt, qt, 0, 0), g), fks):
            _add_blocks(dk, kr, part * f)
        for dv, part in zip(dvs, _unstack(_bdot(att, do_q, 0, 0), g)):
            _add_blocks(dv, kr, part)

    def cat(parts):
        return [jnp.concatenate(p, axis=0) for p in parts]

    return cat(dqs), cat(dks), cat(dvs)


def _hgrn_state_step(k, v, b, s_t, last):
    b_last = b[last:last + 1]
    return s_t * jnp.exp(b_last) + _bdot(v, k * jnp.exp(b_last - b), 0, 0)


def hgrn_scan_fwd(p, lb, s0, col_z, reverse, name, prev=None):
    has_prev = prev is not None
    L = p.shape[0]
    nB = L // SCAN_ROWS
    nC = SCAN_ROWS // HG_CHUNK
    C = HG_CHUNK
    G, W = HG_GROUP, HG_GROUP * HG_D
    last = 0 if reverse else C - 1

    def bmap(b):
        return (nB - 1 - b) if reverse else b

    def body(q_ref, z_ref, v_ref, lb_ref, s0_ref, *refs):
        prev_ref = refs[0] if has_prev else None
        o_ref, sfin_ref, sblk_ref, s_scr, k_scr, b_scr = refs[1:] if has_prev else refs
        blk = pl.program_id(1)

        @pl.when(blk == 0)
        def _():
            s_scr[...] = s0_ref[...]

        sblk_ref[...] = s_scr[...]
        _, _, f_all, k_all = _hgrn_gates(z_ref[...], lb_ref[...])
        k_scr[...] = k_all
        b_scr[...] = _dot(_tri_chunks(SCAN_ROWS, C, reverse), jnp.log(f_all), prec=CUMSUM_PRECISION)

        masks = _hgrn_masks(G, reverse)
        heads = [slice(j * HG_D, (j + 1) * HG_D) for j in range(G)]

        def chunk(ci, carry):
            c = (nC - 1 - ci) if reverse else ci
            rows = pl.ds(pl.multiple_of(c * C, C), C)
            qs = [q_ref[rows, lanes] * Q_SCALE for lanes in heads]
            vs = [v_ref[rows, lanes] for lanes in heads]
            ks = [k_scr[rows, lanes] for lanes in heads]
            bs = [b_scr[rows, lanes] for lanes in heads]
            o_in = _hgrn_intra_fwd(qs, ks, vs, bs, masks, reverse)
            for j, lanes in enumerate(heads):
                s_t = s_scr[j]
                o = o_in[j] + _bdot(qs[j] * jnp.exp(bs[j]), s_t, 1, 1)
                o_ref[rows, lanes] = o + prev_ref[rows, lanes] if has_prev else o
                s_scr[j] = _hgrn_state_step(ks[j], vs[j], bs[j], s_t, last)
            return carry

        lax.fori_loop(0, nC, chunk, 0)

        @pl.when(blk == nB - 1)
        def _():
            sfin_ref[...] = s_scr[...]

    def col(c0):
        return pl.BlockSpec((SCAN_ROWS, W), lambda h, b: (bmap(b), c0 // G + h))

    state = pl.BlockSpec((G, HG_D, HG_D), lambda h, b: (h, 0, 0))
    return pl.pallas_call(
        body, name=name,
        grid=(HEADS // G, nB),
        in_specs=[col(COL_HQ), col(col_z), col(COL_HI), pl.BlockSpec((1, W), lambda h, b: (0, h)), state]
        + ([pl.BlockSpec((SCAN_ROWS, W), lambda h, b: (bmap(b), h))] if has_prev else []),
        out_specs=[pl.BlockSpec((SCAN_ROWS, W), lambda h, b: (bmap(b), h)), state,
                   pl.BlockSpec((None, G, HG_D, HG_D), lambda h, b: (bmap(b), h, 0, 0))],
        out_shape=[jax.ShapeDtypeStruct((L, D), F32),
                   jax.ShapeDtypeStruct((HEADS, HG_D, HG_D), F32),
                   jax.ShapeDtypeStruct((nB, HEADS, HG_D, HG_D), F32)],
        scratch_shapes=[pltpu.VMEM((G, HG_D, HG_D), F32), pltpu.VMEM((SCAN_ROWS, W), F32),
                        pltpu.VMEM((SCAN_ROWS, W), F32)],
        compiler_params=_params("parallel", "arbitrary"),
    )(p, p, p, lb, s0, *([prev] if has_prev else []))


def hgrn_scan_bwd(p, lb, s_blocks, d_o, ds_fin, prev, col_z, reverse, name):
    L = p.shape[0]
    nB = L // SCAN_ROWS
    nC = SCAN_ROWS // HG_CHUNK
    C = HG_CHUNK
    G, W = HG_GROUP, HG_GROUP * HG_D
    last = 0 if reverse else C - 1
    has_prev = prev is not None
    out_dt = BF16 if has_prev else F32

    def bmap(b):
        return b if reverse else (nB - 1 - b)

    def body(*refs):
        q_ref, z_ref, v_ref, lb_ref, sblk_ref, do_ref, dsf_ref = refs[:7]
        refs = refs[7:]
        if has_prev:
            pq_ref, pv_ref = refs[:2]
            refs = refs[2:]
        (dq_ref, dz_ref, dv_ref, dlb_ref, ds0_ref, st_scr, run_scr, ds_scr, k_scr, b_scr, db_scr, dk_scr,
         rf_scr, sgn_scr, dzf_scr) = refs
        blk = pl.program_id(1)

        @pl.when(blk == 0)
        def _():
            ds_scr[...] = dsf_ref[...]
            dlb_ref[...] = jnp.zeros_like(dlb_ref)

        tri = _tri_chunks(SCAN_ROWS, C, reverse)
        row = lax.broadcasted_iota(jnp.int32, (C, HG_D), 0)
        lb_all = lb_ref[...]
        sg_all, sgn_all, f_all, k_all = _hgrn_gates(z_ref[...], lb_all)
        k_scr[...] = k_all
        rf_scr[...] = 1.0 / f_all
        sgn_scr[...] = sgn_all
        dzf_scr[...] = (1.0 - lb_all) * sg_all * sgn_all
        b_scr[...] = _dot(tri, jnp.log(f_all), prec=CUMSUM_PRECISION)
        run_scr[...] = sblk_ref[...]

        def recompute(ci, carry):
            c = (nC - 1 - ci) if reverse else ci
            rows = pl.ds(pl.multiple_of(c * C, C), C)
            for j in range(G):
                lanes = slice(j * HG_D, (j + 1) * HG_D)
                s_t = run_scr[j]
                st_scr[c, j] = s_t
                run_scr[j] = _hgrn_state_step(k_scr[rows, lanes], v_ref[rows, lanes], b_scr[rows, lanes], s_t, last)
            return carry

        lax.fori_loop(0, nC, recompute, 0)

        masks = _hgrn_masks(G, reverse)
        heads = [slice(j * HG_D, (j + 1) * HG_D) for j in range(G)]

        def chunk(ci, carry):
            c = ci if reverse else (nC - 1 - ci)
            rows = pl.ds(pl.multiple_of(c * C, C), C)
            ks = [k_scr[rows, lanes] for lanes in heads]
            bs = [b_scr[rows, lanes] for lanes in heads]
            qs = [q_ref[rows, lanes] * Q_SCALE for lanes in heads]
            vs = [v_ref[rows, lanes] for lanes in heads]
            d_os = [do_ref[rows, lanes] for lanes in heads]
            dq_ins, dk_ins, dv_ins = _hgrn_intra_bwd(qs, ks, vs, bs, d_os, masks, reverse)
            for j, lanes in enumerate(heads):
                k, b, q, v, d_o = ks[j], bs[j], qs[j], vs[j], d_os[j]
                s_t = st_scr[c, j]
                ds_t = ds_scr[j]
                eb = jnp.exp(b)
                b_last = b[last:last + 1]
                eb_last = jnp.exp(b_last)
                kdec = jnp.exp(b_last - b)
                qe = q * eb
                ke = k * kdec
                dq_tot = _bdot(d_o, s_t, 1, 0) * eb + dq_ins[j]
                dke = _bdot(v, ds_t, 1, 0)
                dk_tot = dke * kdec + dk_ins[j]
                dv = dv_ins[j] + _bdot(ke, ds_t, 1, 1)
                db_last = _rowsum(dke * ke) + eb_last * _rowsum(ds_t * s_t)
                db_scr[rows, lanes] = q * dq_tot - k * dk_tot + jnp.where(row == last, db_last, 0.0)
                dk_scr[rows, lanes] = dk_tot
                dq = dq_tot * Q_SCALE
                if has_prev:
                    dq = dq + pq_ref[rows, lanes]
                    dv = dv + pv_ref[rows, lanes]
                dq_ref[rows, lanes] = dq.astype(out_dt)
                dv_ref[rows, lanes] = dv.astype(out_dt)
                ds_scr[j] = ds_t * eb_last + _bdot(d_o, qe, 0, 0)
            return carry

        lax.fori_loop(0, nC, chunk, 0)

        g = _dot(tri, db_scr[...], 0, 0, prec=CUMSUM_PRECISION) * rf_scr[...] - dk_scr[...]
        dz_ref[...] = (g * dzf_scr[...]).astype(BF16)
        dlb_ref[...] += _rowsum(g * sgn_scr[...])

        @pl.when(blk == nB - 1)
        def _():
            ds0_ref[...] = ds_scr[...]

    def col(c0):
        return pl.BlockSpec((SCAN_ROWS, W), lambda h, b: (bmap(b), c0 // G + h))

    tile = pl.BlockSpec((SCAN_ROWS, W), lambda h, b: (bmap(b), h))
    state = pl.BlockSpec((G, HG_D, HG_D), lambda h, b: (h, 0, 0))
    in_specs = [col(COL_HQ), col(col_z), col(COL_HI),
                pl.BlockSpec((1, W), lambda h, b: (0, h)),
                pl.BlockSpec((None, G, HG_D, HG_D), lambda h, b: (bmap(b), h, 0, 0)),
                tile, state]
    args = [p, p, p, lb, s_blocks, d_o, ds_fin]
    if has_prev:
        in_specs += [tile, tile]
        args += list(prev)
    return pl.pallas_call(
        body, name=name,
        grid=(HEADS // G, nB),
        in_specs=in_specs,
        out_specs=[tile, tile, tile, pl.BlockSpec((1, W), lambda h, b: (0, h)), state],
        out_shape=[jax.ShapeDtypeStruct((L, D), out_dt), jax.ShapeDtypeStruct((L, D), BF16),
                   jax.ShapeDtypeStruct((L, D), out_dt), jax.ShapeDtypeStruct((1, D), F32),
                   jax.ShapeDtypeStruct((HEADS, HG_D, HG_D), F32)],
        scratch_shapes=[pltpu.VMEM((nC, G, HG_D, HG_D), F32), pltpu.VMEM((G, HG_D, HG_D), F32),
                        pltpu.VMEM((G, HG_D, HG_D), F32)] + [pltpu.VMEM((SCAN_ROWS, W), F32)] * 7,
        compiler_params=_params("parallel", "arbitrary"),
    )(*args)


def _rope(t, cosf, sinf):
    return t * cosf + pltpu.roll(t, RT_DK // 2, 1) * sinf


def _rope_t(d, cosf, sinf):
    return d * cosf + pltpu.roll(d * sinf, RT_DK // 2, 1)


def _ret_decays(lg, reverse):
    C = SCAN_ROWS
    t = lax.broadcasted_iota(jnp.int32, (C, C), 0)
    s = lax.broadcasted_iota(jnp.int32, (C, C), 1)
    delta = ((s - t) if reverse else (t - s)).astype(F32)
    dmat = jnp.where(delta >= 0, jnp.exp(lg * jnp.maximum(delta, 0.0)), 0.0)
    r = lax.broadcasted_iota(jnp.int32, (C, RT_DK), 0)
    pos = ((C - 1 - r) if reverse else r).astype(F32)
    lg1 = lg[:, :RT_DK]
    qdec = jnp.exp(lg1 * (pos + 1.0))
    kdec = jnp.exp(lg1 * (C - 1.0 - pos))
    sdec = jnp.exp(lg1 * float(C))
    return dmat, delta, pos, qdec, kdec, sdec


def ret_scan_fwd(p, cosf, sinf, lg, s0, reverse, name, prev=None):
    has_prev = prev is not None
    L = p.shape[0]
    C = SCAN_ROWS
    nB = L // C

    def bmap(b):
        return (nB - 1 - b) if reverse else b

    G = RT_GROUP

    def body(q_ref, k_ref, v_ref, cos_ref, sin_ref, lg_ref, s0_ref, *refs):
        prev_ref = refs[0] if has_prev else None
        o_ref, sfin_ref, sblk_ref, s_scr = refs[1:] if has_prev else refs
        blk = pl.program_id(1)

        @pl.when(blk == 0)
        def _():
            s_scr[...] = s0_ref[...]

        sblk_ref[...] = s_scr[...]
        cosf, sinf = cos_ref[...], sin_ref[...]
        for j in range(G):
            lk, lv = slice(j * RT_DK, (j + 1) * RT_DK), slice(j * RT_DV, (j + 1) * RT_DV)
            s_t = s_scr[j]
            dmat, _, _, qdec, kdec, sdec = _ret_decays(lg_ref[j], reverse)
            q = _rope(q_ref[:, lk] * Q_SCALE, cosf, sinf)
            k = _rope(k_ref[:, lk], cosf, sinf)
            v = v_ref[:, lv]
            att = _bdot(q, k, 1, 1) * dmat
            o = _bdot(att, v) + _bdot(q * qdec, s_t, 1, 1)
            o_ref[:, lv] = o + prev_ref[:, lv] if has_prev else o
            s_scr[j] = s_t * sdec + _bdot(v, k * kdec, 0, 0)

        @pl.when(blk == nB - 1)
        def _():
            sfin_ref[...] = s_scr[...]

    def col(c0):
        return pl.BlockSpec((C, G * RT_DK), lambda h, b: (bmap(b), c0 // G + h))

    tab = pl.BlockSpec((C, RT_DK), lambda h, b: (bmap(b), 0))
    state = pl.BlockSpec((G, RT_DV, RT_DK), lambda h, b: (h, 0, 0))
    return pl.pallas_call(
        body, name=name,
        grid=(HEADS // G, nB),
        in_specs=[col(COL_RQ), col(COL_RK),
                  pl.BlockSpec((C, G * RT_DV), lambda h, b: (bmap(b), COL_RV // (2 * G) + h)),
                  tab, tab, pl.BlockSpec((G, 1, RT_DV), lambda h, b: (h, 0, 0)), state]
        + ([pl.BlockSpec((C, G * RT_DV), lambda h, b: (bmap(b), h))] if has_prev else []),
        out_specs=[pl.BlockSpec((C, G * RT_DV), lambda h, b: (bmap(b), h)), state,
                   pl.BlockSpec((None, G, RT_DV, RT_DK), lambda h, b: (bmap(b), h, 0, 0))],
        out_shape=[jax.ShapeDtypeStruct((L, HEADS * RT_DV), F32),
                   jax.ShapeDtypeStruct((HEADS, RT_DV, RT_DK), F32),
                   jax.ShapeDtypeStruct((nB, HEADS, RT_DV, RT_DK), F32)],
        scratch_shapes=[pltpu.VMEM((G, RT_DV, RT_DK), F32)],
        compiler_params=_params("parallel", "arbitrary"),
    )(p, p, p, cosf, sinf, lg, s0, *([prev] if has_prev else []))


def ret_scan_bwd(p, cosf, sinf, lg, s_blocks, d_o, ds_fin, prev, reverse, name):
    L = p.shape[0]
    C = SCAN_ROWS
    nB = L // C
    has_prev = prev is not None
    out_dt = BF16
    G = RT_GROUP

    def bmap(b):
        return b if reverse else (nB - 1 - b)

    def body(*refs):
        q_ref, k_ref, v_ref, cos_ref, sin_ref, lg_ref, sblk_ref, do_ref, dsf_ref = refs[:9]
        refs = refs[9:]
        if has_prev:
            pq_ref, pk_ref, pv_ref = refs[:3]
            refs = refs[3:]
        dq_ref, dk_ref, dv_ref, dlg_ref, ds0_ref, ds_scr = refs
        blk = pl.program_id(1)

        @pl.when(blk == 0)
        def _():
            ds_scr[...] = dsf_ref[...]
            dlg_ref[...] = jnp.zeros_like(dlg_ref)

        cosf, sinf = cos_ref[...], sin_ref[...]
        for j in range(G):
            lk, lv = slice(j * RT_DK, (j + 1) * RT_DK), slice(j * RT_DV, (j + 1) * RT_DV)
            s_t = sblk_ref[j]
            ds_t = ds_scr[j]
            dmat, delta, pos, qdec, kdec, sdec = _ret_decays(lg_ref[j], reverse)
            q = _rope(q_ref[:, lk] * Q_SCALE, cosf, sinf)
            k = _rope(k_ref[:, lk], cosf, sinf)
            v = v_ref[:, lv]
            d_o = do_ref[:, lv]
            att_raw = _bdot(q, k, 1, 1)
            datt_m = _bdot(d_o, v, 1, 1) * dmat
            dqd = _bdot(d_o, s_t, 1, 0)
            dkd = _bdot(v, ds_t, 1, 0)
            dq = _bdot(datt_m, k) + dqd * qdec
            dk = _bdot(datt_m, q, 0, 0) + dkd * kdec
            dv = _bdot(att_raw * dmat, d_o, 0, 0) + _bdot(k * kdec, ds_t, 1, 1)
            ds_scr[j] = ds_t * sdec + _bdot(d_o, q * qdec, 0, 0)
            t1 = jnp.sum(_rowsum(datt_m * att_raw * delta), axis=-1, keepdims=True)
            t23 = jnp.sum(_rowsum((pos + 1.0) * qdec * q * dqd + (C - 1.0 - pos) * kdec * k * dkd), axis=-1, keepdims=True)
            t4 = jnp.sum(_rowsum(ds_t * s_t * sdec), axis=-1, keepdims=True) * float(C)
            dlg_ref[j] += jnp.broadcast_to(t1 + t23 + t4, (1, RT_DK))
            if has_prev:
                dq = _rope_t(dq + pq_ref[:, lk].astype(F32), cosf, sinf) * Q_SCALE
                dk = _rope_t(dk + pk_ref[:, lk].astype(F32), cosf, sinf)
                dv = dv + pv_ref[:, lv].astype(F32)
            dq_ref[:, lk] = dq.astype(out_dt)
            dk_ref[:, lk] = dk.astype(out_dt)
            dv_ref[:, lv] = dv.astype(out_dt)

        @pl.when(blk == nB - 1)
        def _():
            ds0_ref[...] = ds_scr[...]

    def col(c0):
        return pl.BlockSpec((C, G * RT_DK), lambda h, b: (bmap(b), c0 // G + h))

    tab = pl.BlockSpec((C, RT_DK), lambda h, b: (bmap(b), 0))
    state = pl.BlockSpec((G, RT_DV, RT_DK), lambda h, b: (h, 0, 0))
    tk = pl.BlockSpec((C, G * RT_DK), lambda h, b: (bmap(b), h))
    tv = pl.BlockSpec((C, G * RT_DV), lambda h, b: (bmap(b), h))
    in_specs = [col(COL_RQ), col(COL_RK),
                pl.BlockSpec((C, G * RT_DV), lambda h, b: (bmap(b), COL_RV // (2 * G) + h)),
                tab, tab, pl.BlockSpec((G, 1, RT_DV), lambda h, b: (h, 0, 0)),
                pl.BlockSpec((None, G, RT_DV, RT_DK), lambda h, b: (bmap(b), h, 0, 0)),
                tv, state]
    args = [p, p, p, cosf, sinf, lg, s_blocks, d_o, ds_fin]
    if has_prev:
        in_specs += [tk, tk, tv]
        args += list(prev)
    return pl.pallas_call(
        body, name=name,
        grid=(HEADS // G, nB),
        in_specs=in_specs,
        out_specs=[tk, tk, tv, pl.BlockSpec((G, 1, RT_DK), lambda h, b: (h, 0, 0)), state],
        out_shape=[jax.ShapeDtypeStruct((L, D), out_dt), jax.ShapeDtypeStruct((L, D), out_dt),
                   jax.ShapeDtypeStruct((L, HEADS * RT_DV), out_dt),
                   jax.ShapeDtypeStruct((HEADS, 1, RT_DK), F32),
                   jax.ShapeDtypeStruct((HEADS, RT_DV, RT_DK), F32)],
        scratch_shapes=[pltpu.VMEM((G, RT_DV, RT_DK), F32)],
        compiler_params=_params("parallel", "arbitrary"),
    )(*args)


def _silu_parts(h):
    s = _sigmoid(h)
    return h * s, s * (1.0 + h * (1.0 - s))


def _head_rms(o):
    outs, rs = [], []
    for h in range(HEADS):
        oh = o[:, h * HG_D:(h + 1) * HG_D]
        r = lax.rsqrt(_lanemean(oh * oh) + EPS)
        outs.append(oh * r)
        rs.append(r)
    return outs, rs


def _group_norm(o):
    outs, rs = [], []
    for h in range(HEADS):
        oh = o[:, h * RT_DV:(h + 1) * RT_DV]
        c = oh - _lanemean(oh)
        r = lax.rsqrt(_lanemean(c * c) + GN_EPS)
        outs.append(c * r)
        rs.append(r)
    return outs, rs


MIX_ROWS = 256


def _mix_specs(rows):
    def t(w, c=0):
        return pl.BlockSpec((rows, w), lambda i: (i, c))

    return t


def mix_fwd(o_hg, o_rt, p, x, g1, hgw, w_pa, w_pb, w_out, name):
    L = x.shape[0]
    t = _mix_specs(MIX_ROWS)

    def body(ohg_ref, ort_ref, hg_ref, rg0_ref, rg1_ref, ga_ref, gb_ref, x_ref, g1_ref, hgw_ref,
             wpa_ref, wpb_ref, wout_ref, x1_ref, xmix_ref, merged_ref, ya_ref, yb_ref):
        nh, _ = _head_rms(ohg_ref[...])
        ya = jnp.concatenate(nh, axis=1) * hgw_ref[...] * _silu_parts(hg_ref[...])[0]
        gn, _ = _group_norm(ort_ref[...])
        rg = jnp.concatenate([rg0_ref[...], rg1_ref[...]], axis=1)
        yb = jnp.concatenate(gn, axis=1) * _silu_parts(rg)[0]
        ya16, yb16 = ya.astype(BF16), yb.astype(BF16)
        merged = (_sigmoid(ga_ref[...]) * _dot(ya16, wpa_ref[...])
                  + _sigmoid(gb_ref[...]) * _dot(yb16, wpb_ref[...])).astype(BF16)
        x_mix = _dot(merged, wout_ref[...])
        x1_ref[...] = x_ref[...] + g1_ref[...] * x_mix
        xmix_ref[...] = x_mix
        merged_ref[...] = merged
        ya_ref[...] = ya16
        yb_ref[...] = yb16

    vec = pl.BlockSpec((1, D), lambda i: (0, 0))

    def full(a):
        return pl.BlockSpec(a.shape, lambda i: (0, 0), pipeline_mode=pl.Buffered(1))

    return pl.pallas_call(
        body, name=name,
        grid=(L // MIX_ROWS,),
        in_specs=[t(D), t(2 * D), t(D, COL_HG // 8), t(D, COL_RG // 8), t(D, COL_RG // 8 + 1),
                  t(D, COL_GA // 8), t(D, COL_GB // 8), t(D), vec, vec, full(w_pa), full(w_pb), full(w_out)],
        out_specs=[t(D), t(D), t(D), t(D), t(2 * D)],
        out_shape=[jax.ShapeDtypeStruct((L, D), F32), jax.ShapeDtypeStruct((L, D), F32),
                   jax.ShapeDtypeStruct((L, D), BF16), jax.ShapeDtypeStruct((L, D), BF16),
                   jax.ShapeDtypeStruct((L, 2 * D), BF16)],
        compiler_params=_params("parallel"),
    )(o_hg, o_rt, p, p, p, p, p, x, g1, hgw, w_pa, w_pb, w_out)


def mix_bwd(dx1, x_mix, ya, yb, o_hg, o_rt, p, g1, hgw, w_pa, w_pb, w_out, name, to_sibling=()):
    L = dx1.shape[0]
    t = _mix_specs(MIX_ROWS)
    nx = len(to_sibling)
    steps = L // MIX_ROWS

    def body(dx1_ref, xmix_ref, ya_ref, yb_ref, ohg_ref, ort_ref, hg_ref, rg0_ref, rg1_ref,
             ga_ref, gb_ref, g1_ref, hgw_ref, wpa_ref, wpb_ref, wout_ref, *refs):
        (dxm_ref, da_ref, db_ref, dga_ref, dgb_ref, dhg_ref, drg_ref, dohg_ref, dort_ref,
         sums_ref) = refs[nx:nx + 10]
        if nx:
            start, finish = _to_sibling_phases(refs[:nx], refs[nx + 10:2 * nx + 10], *refs[2 * nx + 10:])
            pl.when(pl.program_id(0) == 0)(start)
            pl.when(pl.program_id(0) == steps - 1)(finish)

        @pl.when(pl.program_id(0) == 0)
        def _():
            sums_ref[...] = jnp.zeros_like(sums_ref)

        dx1 = dx1_ref[...]
        dxm = (g1_ref[...] * dx1).astype(BF16)
        dxm_ref[...] = dxm
        dmerged = _dot(dxm, wout_ref[...], 1, 1)
        a = _dot(ya_ref[...], wpa_ref[...])
        bm = _dot(yb_ref[...], wpb_ref[...])
        sa, sb = _sigmoid(ga_ref[...]), _sigmoid(gb_ref[...])
        d_a = (dmerged * sa).astype(BF16)
        d_b = (dmerged * sb).astype(BF16)
        da_ref[...] = d_a
        db_ref[...] = d_b
        dga_ref[...] = (dmerged * a * sa * (1.0 - sa)).astype(BF16)
        dgb_ref[...] = (dmerged * bm * sb * (1.0 - sb)).astype(BF16)
        dya = _dot(d_a, wpa_ref[...], 1, 1)
        dyb = _dot(d_b, wpb_ref[...], 1, 1)

        hgw = hgw_ref[...]
        silu_h, dsilu_h = _silu_parts(hg_ref[...])
        nh, rh = _head_rms(ohg_ref[...])
        n = jnp.concatenate(nh, axis=1)
        dhg_ref[...] = (dya * n * hgw * dsilu_h).astype(BF16)
        dn = dya * hgw * silu_h
        douts = []
        for h in range(HEADS):
            dnh = dn[:, h * HG_D:(h + 1) * HG_D]
            douts.append(rh[h] * (dnh - nh[h] * _lanemean(dnh * nh[h])))
        dohg_ref[...] = jnp.concatenate(douts, axis=1)

        rg = jnp.concatenate([rg0_ref[...], rg1_ref[...]], axis=1)
        silu_r, dsilu_r = _silu_parts(rg)
        gn, rr = _group_norm(ort_ref[...])
        g = jnp.concatenate(gn, axis=1)
        drg_ref[...] = (dyb * g * dsilu_r).astype(BF16)
        dgn = dyb * silu_r
        douts = []
        for h in range(HEADS):
            dgh = dgn[:, h * RT_DV:(h + 1) * RT_DV]
            douts.append(rr[h] * (dgh - _lanemean(dgh) - gn[h] * _lanemean(dgh * gn[h])))
        dort_ref[...] = jnp.concatenate(douts, axis=1)

        sums_ref[0:1, :] += _rowsum(dx1 * xmix_ref[...])
        sums_ref[1:2, :] += _rowsum(dya * n * silu_h)

    vec = pl.BlockSpec((1, D), lambda i: (0, 0))

    def full(a):
        return pl.BlockSpec(a.shape, lambda i: (0, 0), pipeline_mode=pl.Buffered(1))

    bf = functools.partial(jax.ShapeDtypeStruct, dtype=BF16)
    return pl.pallas_call(
        body, name=name,
        grid=(L // MIX_ROWS,),
        in_specs=[t(D), t(D), t(D), t(2 * D), t(D), t(2 * D),
                  t(D, COL_HG // 8), t(D, COL_RG // 8), t(D, COL_RG // 8 + 1), t(D, COL_GA // 8), t(D, COL_GB // 8),
                  vec, vec, full(w_pa), full(w_pb), full(w_out)] + [ANY] * nx,
        out_specs=[t(D), t(D), t(D), t(D), t(D), t(D), t(2 * D), t(D), t(2 * D),
                   pl.BlockSpec((8, D), lambda i: (0, 0))] + [ANY] * nx,
        out_shape=[bf((L, D)), bf((L, D)), bf((L, D)), bf((L, D)), bf((L, D)), bf((L, D)), bf((L, 2 * D)),
                   jax.ShapeDtypeStruct((L, D), F32), jax.ShapeDtypeStruct((L, 2 * D), F32),
                   jax.ShapeDtypeStruct((8, D), F32)] + [jax.ShapeDtypeStruct(a.shape, a.dtype) for a in to_sibling],
        scratch_shapes=_to_sibling_scratch(nx) if nx else [],
        compiler_params=_params("arbitrary"),
    )(dx1, x_mix, ya, yb, o_hg, o_rt, p, p, p, p, p, g1, hgw, w_pa, w_pb, w_out, *to_sibling)


FFN_ROWS = 512


def ffn_fwd(x1, target, nw2, sh2, sc2, g2, fw, wg, wu, wd, name):
    L = x1.shape[0]
    tm = min(FFN_ROWS, L)

    def body(x1_ref, tgt_ref, nw2_ref, sh2_ref, sc2_ref, g2_ref, fw_ref, wg_ref, wu_ref, wd_ref,
             hx2_ref, g_ref, u_ref, h_ref, f_ref, dx2_ref, sums_ref, hx_scr, acc):
        i, j = pl.program_id(0), pl.program_id(1)

        @pl.when((i == 0) & (j == 0))
        def _():
            sums_ref[...] = jnp.zeros_like(sums_ref)

        @pl.when(j == 0)
        def _():
            xv = x1_ref[...]
            n = xv * lax.rsqrt(_lanemean(xv * xv) + EPS) * nw2_ref[...]
            h = (n * (1.0 + sc2_ref[...]) + sh2_ref[...]).astype(BF16)
            hx_scr[...] = h
            hx2_ref[...] = h
            acc[...] = jnp.zeros_like(acc)

        hx = hx_scr[...]
        g = _dot(hx, wg_ref[...])
        u = _dot(hx, wu_ref[...])
        hh = (_silu_parts(g)[0] * u).astype(BF16)
        g_ref[...] = g
        u_ref[...] = u
        h_ref[...] = hh
        acc[...] += _dot(hh, wd_ref[...])

        @pl.when(j == N_SHARD - 1)
        def _():
            f = acc[...]
            f_ref[...] = f
            x2 = x1_ref[...] + g2_ref[...] * f
            r = lax.rsqrt(_lanemean(x2 * x2) + EPS)
            fw = fw_ref[...]
            e = x2 * r * fw - tgt_ref[...]
            dy = e * (1.0 / D)
            dyw = dy * fw
            dx2_ref[...] = r * dyw - x2 * (r * r * r) * _lanemean(dyw * x2)
            sums_ref[0:1, :] += _rowsum(dy * x2 * r)
            sums_ref[1:2, :] += _rowsum(e * e) * (0.5 / D)

    row = pl.BlockSpec((tm, D), lambda i, j: (i, 0))
    vec = pl.BlockSpec((1, D), lambda i, j: (0, 0))
    sh = pl.BlockSpec((None, tm, FF_SH), lambda i, j: (j, i, 0))
    return pl.pallas_call(
        body, name=name,
        grid=(L // tm, N_SHARD),
        in_specs=[row, row, vec, vec, vec, vec, vec,
                  pl.BlockSpec((None, D, FF_SH), lambda i, j: (j, 0, 0)),
                  pl.BlockSpec((None, D, FF_SH), lambda i, j: (j, 0, 0)),
                  pl.BlockSpec((None, FF_SH, D), lambda i, j: (j, 0, 0))],
        out_specs=[row, sh, sh, sh, row, row, pl.BlockSpec((8, D), lambda i, j: (0, 0))],
        out_shape=[jax.ShapeDtypeStruct((L, D), BF16),
                   jax.ShapeDtypeStruct((N_SHARD, L, FF_SH), F32), jax.ShapeDtypeStruct((N_SHARD, L, FF_SH), F32),
                   jax.ShapeDtypeStruct((N_SHARD, L, FF_SH), BF16),
                   jax.ShapeDtypeStruct((L, D), F32), jax.ShapeDtypeStruct((L, D), F32),
                   jax.ShapeDtypeStruct((8, D), F32)],
        scratch_shapes=[pltpu.VMEM((tm, D), BF16), pltpu.VMEM((tm, D), F32)],
        compiler_params=_params("arbitrary", "arbitrary"),
    )(x1, target, nw2, sh2, sc2, g2, fw, wg, wu, wd)


def ffn_bwd(dx2, x1, f, g, u, nw2, sc2, g2, wg, wu, wd, name):
    L = x1.shape[0]
    tm = min(FFN_ROWS, L)

    def body(dx2_ref, x1_ref, f_ref, g_ref, u_ref, nw2_ref, sc2_ref, g2_ref, wg_ref, wu_ref, wd_ref,
             df_ref, dg_ref, du_ref, dx1_ref, sums_ref, df_scr, acc):
        i, j = pl.program_id(0), pl.program_id(1)

        @pl.when((i == 0) & (j == 0))
        def _():
            sums_ref[...] = jnp.zeros_like(sums_ref)

        @pl.when(j == 0)
        def _():
            dx2 = dx2_ref[...]
            df = (g2_ref[...] * dx2).astype(BF16)
            df_scr[...] = df
            df_ref[...] = df
            sums_ref[0:1, :] += _rowsum(dx2 * f_ref[...])
            acc[...] = jnp.zeros_like(acc)

        dh = _dot(df_scr[...], wd_ref[...], 1, 1)
        gv, uv = g_ref[...], u_ref[...]
        silu_g, dsilu_g = _silu_parts(gv)
        dg = (dh * uv * dsilu_g).astype(BF16)
        du = (dh * silu_g).astype(BF16)
        dg_ref[...] = dg
        du_ref[...] = du
        acc[...] += _dot(dg, wg_ref[...], 1, 1) + _dot(du, wu_ref[...], 1, 1)

        @pl.when(j == N_SHARD - 1)
        def _():
            dhx = acc[...]
            xv = x1_ref[...]
            r = lax.rsqrt(_lanemean(xv * xv) + EPS)
            n0 = xv * r
            nw = nw2_ref[...]
            dn2 = dhx * (1.0 + sc2_ref[...])
            dn0 = dn2 * nw
            dx1_ref[...] = dx2_ref[...] + r * (dn0 - n0 * _lanemean(dn0 * n0))
            sums_ref[1:2, :] += _rowsum(dhx)
            sums_ref[2:3, :] += _rowsum(dhx * n0 * nw)
            sums_ref[3:4, :] += _rowsum(dn2 * n0)

    row = pl.BlockSpec((tm, D), lambda i, j: (i, 0))
    vec = pl.BlockSpec((1, D), lambda i, j: (0, 0))
    sh = pl.BlockSpec((None, tm, FF_SH), lambda i, j: (j, i, 0))
    return pl.pallas_call(
        body, name=name,
        grid=(L // tm, N_SHARD),
        in_specs=[row, row, row, sh, sh, vec, vec, vec,
                  pl.BlockSpec((None, D, FF_SH), lambda i, j: (j, 0, 0)),
                  pl.BlockSpec((None, D, FF_SH), lambda i, j: (j, 0, 0)),
                  pl.BlockSpec((None, FF_SH, D), lambda i, j: (j, 0, 0))],
        out_specs=[row, sh, sh, row, pl.BlockSpec((8, D), lambda i, j: (0, 0))],
        out_shape=[jax.ShapeDtypeStruct((L, D), BF16),
                   jax.ShapeDtypeStruct((N_SHARD, L, FF_SH), BF16), jax.ShapeDtypeStruct((N_SHARD, L, FF_SH), BF16),
                   jax.ShapeDtypeStruct((L, D), F32), jax.ShapeDtypeStruct((8, D), F32)],
        scratch_shapes=[pltpu.VMEM((tm, D), BF16), pltpu.VMEM((tm, D), F32)],
        compiler_params=_params("arbitrary", "arbitrary"),
    )(dx2, x1, f, g, u, nw2, sc2, g2, wg, wu, wd)


def matmul_tn(a, b, name):
    na, K, M = a.shape
    nb, _, N = b.shape
    n = max(na, nb)
    tk = min(1024, K)
    tn = N if N <= 1024 else N // 2

    def body(a_ref, b_ref, o_ref):
        @pl.when(pl.program_id(2) == 0)
        def _():
            o_ref[...] = jnp.zeros_like(o_ref)

        o_ref[...] += _dot(a_ref[...], b_ref[...], 0, 0)

    return pl.pallas_call(
        body, name=name,
        grid=(n, N // tn, K // tk),
        in_specs=[pl.BlockSpec((None, tk, M), lambda s, j, kk: (s if na > 1 else 0, kk, 0)),
                  pl.BlockSpec((None, tk, tn), lambda s, j, kk: (s if nb > 1 else 0, kk, j))],
        out_specs=pl.BlockSpec((None, M, tn), lambda s, j, kk: (s, 0, j)),
        out_shape=jax.ShapeDtypeStruct((n, M, N), F32),
        compiler_params=_params("parallel", "parallel", "arbitrary"),
    )(a, b)


def matmul_tn_pair(a, b1, b2, name):
    K, M = a.shape
    n, _, N = b1.shape
    tk = min(1024, K)

    def body(a_ref, b1_ref, b2_ref, o1_ref, o2_ref):
        @pl.when(pl.program_id(1) == 0)
        def _():
            o1_ref[...] = jnp.zeros_like(o1_ref)
            o2_ref[...] = jnp.zeros_like(o2_ref)

        at = a_ref[...].T
        o1_ref[...] += _dot(at, b1_ref[...])
        o2_ref[...] += _dot(at, b2_ref[...])

    b_spec = pl.BlockSpec((None, tk, N), lambda s, kk: (s, kk, 0))
    o_spec = pl.BlockSpec((None, M, N), lambda s, kk: (s, 0, 0))
    return pl.pallas_call(
        body, name=name,
        grid=(n, K // tk),
        in_specs=[pl.BlockSpec((tk, M), lambda s, kk: (kk, 0)), b_spec, b_spec],
        out_specs=[o_spec, o_spec],
        out_shape=[jax.ShapeDtypeStruct((n, M, N), F32)] * 2,
        compiler_params=_params("parallel", "arbitrary"),
    )(a, b1, b2)


PIECE_COLS = 1024
N_PIECE_BLOCKS = D_IN // PIECE_COLS


def _piece_blocks(pieces):
    out, col = [], 0
    for arr, width in pieces:
        if arr is not None:
            out.append((arr, col // PIECE_COLS, width // PIECE_COLS))
        col += width
    assert col == D_IN
    return out


def _piece_feed(p_refs, blocks, buf, sems, tile_of, pos, total):
    def present(blk):
        ok = None
        for _, b0, nb in blocks:
            mine = (blk >= b0) & (blk < b0 + nb)
            ok = mine if ok is None else ok | mine
        return ok

    def fetch(step):
        blk, rows = tile_of(step)
        for p_ref, (_, b0, nb) in zip(p_refs, blocks):
            for t in range(nb):
                @pl.when(blk == b0 + t)
                def _(p_ref=p_ref, t=t):
                    pltpu.make_async_copy(p_ref.at[rows, pl.ds(t * PIECE_COLS, PIECE_COLS)], buf.at[step % 2],
                                          sems.at[step % 2]).start()

    @pl.when(pos == 0)
    def _():
        fetch(pos)

    @pl.when(pos + 1 < total)
    def _():
        fetch(pos + 1)

    def landed():
        slot = pos % 2
        pltpu.make_async_copy(p_refs[0].at[pl.ds(0, buf.shape[1]), pl.ds(0, PIECE_COLS)], buf.at[slot],
                              sems.at[slot]).wait()
        return buf.at[slot]

    return present(tile_of(pos)[0]), landed


def matmul_tn_pieces(a, pieces, name, acc_init=None, to_chips=()):
    K, M = a.shape
    blocks = _piece_blocks(pieces)
    tk = min(1024, K)
    nk = K // tk
    grid = (N_PIECE_BLOCKS, nk)
    has_init = acc_init is not None
    nx, npc = len(to_chips), len(blocks)

    def body(a_ref, *refs):
        p_refs = refs[:npc]
        refs = refs[npc:]
        init_ref = refs[0] if has_init else None
        refs = refs[1:] if has_init else refs
        o_ref = refs[nx]
        buf, sems = refs[2 * nx + 1:2 * nx + 3]
        pos, total = _grid_step(grid)
        if nx:
            start, finish = _to_chips_phases(refs[:nx], refs[nx + 1:2 * nx + 1], *refs[2 * nx + 3:])
            pl.when(pos == 0)(start)
        here, landed = _piece_feed(p_refs, blocks, buf, sems,
                                   lambda s: (s // nk, pl.ds(pl.multiple_of((s % nk) * tk, tk), tk)), pos, total)

        @pl.when(pl.program_id(1) == 0)
        def _():
            o_ref[...] = init_ref[...] if has_init else jnp.zeros_like(o_ref)

        @pl.when(here)
        def _():
            o_ref[...] += _dot(a_ref[...], landed()[...], 0, 0)

        if nx:
            pl.when(pos == total - 1)(finish)

    out_spec = pl.BlockSpec((M, PIECE_COLS), lambda blk, kk: (0, blk))
    in_specs = [pl.BlockSpec((tk, M), lambda blk, kk: (kk, 0))] + [ANY] * npc
    args = [a] + [arr for arr, _, _ in blocks]
    if has_init:
        in_specs.append(out_spec)
        args.append(acc_init)
    out = pl.pallas_call(
        body, name=name,
        grid=grid,
        in_specs=in_specs + [ANY] * nx,
        out_specs=[out_spec] + [ANY] * nx,
        out_shape=[jax.ShapeDtypeStruct((M, D_IN), F32)] + _to_chips_shapes(to_chips),
        scratch_shapes=[pltpu.VMEM((2, tk, PIECE_COLS), BF16), pltpu.SemaphoreType.DMA((2,))]
        + (_to_chips_scratch(nx) if nx else []),
        compiler_params=_params("arbitrary", "arbitrary"),
    )(*args, *to_chips)
    return out if nx else out[0]


def dhx_normbwd(pieces, w, x, dx_res, nw, sc, name, to_chips=()):
    L = x.shape[0]
    tm = min(PROJ_ROWS, L)
    blocks = _piece_blocks(pieces)
    grid = (L // tm, N_PIECE_BLOCKS)
    nx, npc = len(to_chips), len(blocks)

    def body(*refs):
        p_refs = refs[:npc]
        w_ref, x_ref, res_ref, nw_ref, sc_ref = refs[npc:npc + 5]
        refs = refs[npc + 5:]
        dx_ref, sums_ref = refs[nx:nx + 2]
        acc, buf, sems = refs[2 * nx + 2:2 * nx + 5]
        pos, total = _grid_step(grid)
        if nx:
            start, finish = _to_chips_phases(refs[:nx], refs[nx + 2:2 * nx + 2], *refs[2 * nx + 5:])
            pl.when(pos == 0)(start)
            pl.when(pos == total - 1)(finish)
        here, landed = _piece_feed(
            p_refs, blocks, buf, sems,
            lambda s: (s % N_PIECE_BLOCKS, pl.ds(pl.multiple_of((s // N_PIECE_BLOCKS) * tm, tm), tm)), pos, total)
        i, blk = pl.program_id(0), pl.program_id(1)

        @pl.when((i == 0) & (blk == 0))
        def _():
            sums_ref[...] = jnp.zeros_like(sums_ref)

        @pl.when(blk == 0)
        def _():
            acc[...] = jnp.zeros_like(acc)

        @pl.when(here)
        def _():
            acc[...] += _dot(landed()[...], w_ref[...], 1, 1)

        @pl.when(blk == N_PIECE_BLOCKS - 1)
        def _():
            dhx = acc[...]
            xv = x_ref[...]
            r = lax.rsqrt(_lanemean(xv * xv) + EPS)
            n0 = xv * r
            nw = nw_ref[...]
            dn = dhx * (1.0 + sc_ref[...])
            dn0 = dn * nw
            dx_ref[...] = res_ref[...] + r * (dn0 - n0 * _lanemean(dn0 * n0))
            sums_ref[0:1, :] += _rowsum(dhx)
            sums_ref[1:2, :] += _rowsum(dhx * n0 * nw)
            sums_ref[2:3, :] += _rowsum(dn * n0)

    row = pl.BlockSpec((tm, D), lambda i, blk: (i, 0))
    vec = pl.BlockSpec((1, D), lambda i, blk: (0, 0))
    return pl.pallas_call(
        body, name=name,
        grid=grid,
        in_specs=[ANY] * npc + [pl.BlockSpec((D, PIECE_COLS), lambda i, blk: (0, blk)), row, row, vec, vec] + [ANY] * nx,
        out_specs=[row, pl.BlockSpec((8, D), lambda i, blk: (0, 0))] + [ANY] * nx,
        out_shape=[jax.ShapeDtypeStruct((L, D), F32), jax.ShapeDtypeStruct((8, D), F32)] + _to_chips_shapes(to_chips),
        scratch_shapes=[pltpu.VMEM((tm, D), F32), pltpu.VMEM((2, tm, PIECE_COLS), BF16), pltpu.SemaphoreType.DMA((2,))]
        + (_to_chips_scratch(nx) if nx else []),
        compiler_params=_params("arbitrary", "arbitrary"),
    )(*[arr for arr, _, _ in blocks], w, x, dx_res, nw, sc, *to_chips)


SMALL_ROWS = 24


def _rope_tables(L):
    rows = L // 64
    freqs = 10000.0 ** (-jnp.arange(RT_DK // 4, dtype=F32) / (RT_DK // 4))
    a_row = jnp.arange(rows, dtype=F32)[:, None] * freqs
    a_col = jnp.arange(64, dtype=F32)[:, None] * freqs

    def spread(f):
        return jnp.concatenate([jnp.repeat(f(a_row), 64, axis=0), jnp.tile(f(a_col), (rows, 1))], axis=-1)

    cos, sin = spread(jnp.cos), spread(jnp.sin)
    return jnp.concatenate([cos, cos], axis=1), jnp.concatenate([-sin, sin], axis=1)


def _pieces(hq, hf_f, hf_b, hi, hg, rq, rk, rv, rg, ga, gb):
    widths = (D, D, D, D, D, D, D, 2 * D, 2 * D, D, D)
    return list(zip((hq, hf_f, hf_b, hi, hg, rq, rk, rv, rg, ga, gb), widths))


def _lane0(a):
    return a[:, 0, 0]


def _pack_small(rows):
    out = [r.reshape(1, D) for r in rows]
    out += [jnp.zeros((1, D), F32)] * (SMALL_ROWS - len(out))
    return jnp.concatenate(out, axis=0)


def _other_half(g, core):
    axis = g.ndim - 2
    h = g.shape[axis] // 2
    return lax.dynamic_slice_in_dim(g, (1 - core) * h, h, axis=axis).astype(BF16)


def _sibling_sums(gs, names, place):
    core, core_arg, _ = place
    received = rs_to_sibling([_other_half(g, core) for g in gs], "rs_to_sibling_" + names[0])
    return [rs_add_sibling(g, r, core_arg, "rs_add_sibling_" + k) for g, r, k in zip(gs, received, names)]


def _staged_in_proj(x, nw, sh, sc, w_shard, rest, chip):
    cx, cy = chip // 2, chip % 2

    def arg(k):
        return jnp.reshape(k, (1,)).astype(jnp.int32)

    p, hx, w_full = in_proj_own(x, nw, sh, sc, w_shard, arg(chip), "in_proj_own")
    p, w_full = in_proj_next(hx, w_full, arg(2 * (1 - cx) + cy), p, "in_proj_x", diag_from=w_shard)
    w_pa, w_pb, w_out, w_wd = rest[0], rest[1], rest[2], rest[5]
    p, g_pa, g_pb, g_out, g_wd = in_proj_next(hx, w_full, arg(2 * cx + 1 - cy), p, "in_proj_y",
                                              gather=[w_pa, w_pb, w_out, w_wd])
    p, g_wg, g_wu = in_proj_next(hx, w_full, arg(3 - chip), p, "in_proj_diag", gather=[rest[3], rest[4]])
    w = {"w_in": w_full, "w_pa": g_pa.reshape(D, D), "w_pb": g_pb.reshape(2 * D, D), "w_out": g_out.reshape(D, D),
         "wg": g_wg, "wu": g_wu, "wd": g_wd}
    return p, hx, w


def local_step(x, ctx, target, mod_x, mod_c, lb_f, lb_b, lg_f, lg_b, nw1, nw2, hgw, fw, w, rest=None, place=None):
    L, Lc = x.shape[0], ctx.shape[0]
    sh1, sc1, g1, sh2, sc2, g2 = (mod_x[i:i + 1] for i in range(6))
    sh1c, sc1c = mod_c[0:1], mod_c[1:2]
    cosf, sinf = _rope_tables(L)
    cosc, sinc = jnp.ones((Lc, RT_DK), F32), jnp.zeros((Lc, RT_DK), F32)
    zero_h = jnp.zeros((HEADS, HG_D, HG_D), F32)
    zero_r = jnp.zeros((HEADS, RT_DV, RT_DK), F32)

    if rest is None:
        p, hx = normmod_matmul(x, nw1, sh1, sc1, w["w_in"], "in_proj")
    else:
        p, hx, w = _staged_in_proj(x, nw1, sh1, sc1, w["w_in_shard"], rest, place[2][0])
    pc, hxc = normmod_matmul(ctx, nw1, sh1c, sc1c, w["w_in"], "ctx_in_proj")
    _, s_hf, cb_hf = hgrn_scan_fwd(pc, lb_f, zero_h, COL_HFF, False, "ctx_hgrn_f")
    _, s_hb, cb_hb = hgrn_scan_fwd(pc, lb_b, zero_h, COL_HFB, True, "ctx_hgrn_b")
    _, s_rf, cb_rf = ret_scan_fwd(pc, cosc, sinc, lg_f, zero_r, False, "ctx_ret_f")
    _, s_rb, cb_rb = ret_scan_fwd(pc, cosc, sinc, lg_b, zero_r, True, "ctx_ret_b")
    ohf, _, xb_hf = hgrn_scan_fwd(p, lb_f, s_hf, COL_HFF, False, "hgrn_f")
    o_hg, _, xb_hb = hgrn_scan_fwd(p, lb_b, s_hb, COL_HFB, True, "hgrn_b", prev=ohf)
    orf, _, xb_rf = ret_scan_fwd(p, cosf, sinf, lg_f, s_rf, False, "ret_f")
    o_rt, _, xb_rb = ret_scan_fwd(p, cosf, sinf, lg_b, s_rb, True, "ret_b", prev=orf)
    x1, x_mix, merged, ya, yb = mix_fwd(o_hg, o_rt, p, x, g1, hgw, w["w_pa"], w["w_pb"], w["w_out"], "mix_fwd")
    hx2, gg, uu, hh, ff, dx2, sums_f = ffn_fwd(x1, target, nw2, sh2, sc2, g2, fw, w["wg"], w["wu"], w["wd"], "ffn_fwd")

    d_f, d_g, d_u, dx1, sums_fb = ffn_bwd(dx2, x1, ff, gg, uu, nw2, sc2, g2, w["wg"], w["wu"], w["wd"], "ffn_bwd")
    dw_gate, dw_up = matmul_tn_pair(hx2, d_g, d_u, "dw_ffn_gate_up")
    grads = {"wg": dw_gate, "wu": dw_up, "wd": matmul_tn(hh, d_f[None], "dw_ffn_down")}
    ffn_names = ["wg", "wu", "wd"]
    swap = () if place is None else [_other_half(grads[k], place[0]) for k in ffn_names]
    dxm, d_a, d_b, dga, dgb, dhg, drg, dohg, dort, sums_m, *from_sibling = mix_bwd(
        dx1, x_mix, ya, yb, o_hg, o_rt, p, g1, hgw, w["w_pa"], w["w_pb"], w["w_out"], "mix_bwd", to_sibling=swap)
    grads["w_out"] = matmul_tn(merged[None], dxm[None], "dw_out").reshape(N_SHARD, D // N_SHARD, D)
    grads["w_pa"] = matmul_tn(ya[None], d_a[None], "dw_proj_hgrn").reshape(N_SHARD, D // N_SHARD, D)
    grads["w_pb"] = matmul_tn(yb[None], d_b[None], "dw_proj_ret").reshape(N_SHARD, 2 * D // N_SHARD, D)

    rq1, rk1, rv1, dlgf_x, ds_rf = ret_scan_bwd(p, cosf, sinf, lg_f, xb_rf, dort, zero_r, None, False, "ret_f_bwd")
    drq, drk, drv, dlgb_x, ds_rb = ret_scan_bwd(p, cosf, sinf, lg_b, xb_rb, dort, zero_r, (rq1, rk1, rv1), True, "ret_b_bwd")
    hq1, dzf, hv1, dlbf_x, ds_hf = hgrn_scan_bwd(p, lb_f, xb_hf, dohg, zero_h, None, COL_HFF, False, "hgrn_f_bwd")
    dhq, dzb, dhv, dlbb_x, ds_hb = hgrn_scan_bwd(p, lb_b, xb_hb, dohg, zero_h, (hq1, hv1), COL_HFB, True, "hgrn_b_bwd")
    dp = _pieces(dhq, dzf, dzb, dhv, dhg, drq, drk, drv, drg, dga, dgb)
    others = ["w_pa", "w_pb", "w_out", "wg", "wu", "wd"]
    if place is None:
        dw_in = matmul_tn_pieces(hx, dp, "dw_in")
    else:
        sums_o = _sibling_sums([grads[k] for k in others[:3]], others[:3], place)
        sums_o += [rs_add_sibling(grads[k], r, place[1], "rs_add_sibling_" + k) for k, r in zip(ffn_names, from_sibling)]
        dw_in, *recv_o = matmul_tn_pieces(hx, dp, "dw_in", to_chips=[a16 for _, a16 in sums_o])

    zc = jnp.zeros((Lc, D), F32)
    zc2 = jnp.zeros((Lc, 2 * D), F32)
    crq1, crk1, crv1, dlgf_c, _ = ret_scan_bwd(pc, cosc, sinc, lg_f, cb_rf, zc2, ds_rf, None, False, "ctx_ret_f_bwd")
    cdrq, cdrk, cdrv, dlgb_c, _ = ret_scan_bwd(pc, cosc, sinc, lg_b, cb_rb, zc2, ds_rb, (crq1, crk1, crv1), True, "ctx_ret_b_bwd")
    chq1, cdzf, chv1, dlbf_c, _ = hgrn_scan_bwd(pc, lb_f, cb_hf, zc, ds_hf, None, COL_HFF, False, "ctx_hgrn_f_bwd")
    cdhq, cdzb, cdhv, dlbb_c, _ = hgrn_scan_bwd(pc, lb_b, cb_hb, zc, ds_hb, (chq1, chv1), COL_HFB, True, "ctx_hgrn_b_bwd")
    dpc = _pieces(cdhq, cdzf, cdzb, cdhv, None, cdrq, cdrk, cdrv, None, None, None)
    _, sums_c = dhx_normbwd(dpc, w["w_in"], ctx, zc, nw1, sc1c, "dctx_in_proj")
    grads["w_in"] = matmul_tn_pieces(hxc, dpc, "dw_in_ctx", acc_init=dw_in)
    if place is None:
        dx, sums_x = dhx_normbwd(dp, w["w_in"], x, dx1, nw1, sc1, "dx_in_proj")
    else:
        sums_i = _sibling_sums([grads["w_in"]], ["w_in"], place)
        dx, sums_x, recv_i = dhx_normbwd(dp, w["w_in"], x, dx1, nw1, sc1, "dx_in_proj", to_chips=[sums_i[0][1]])
        names = ["w_in"] + others
        halves = [rs_add_chips(a, r, place[2], "rs_add_chips_" + k)
                  for (a, _), r, k in zip(sums_i + sums_o, [recv_i] + recv_o, names)]
        grads = dict(zip(names, rs_join_halves(halves, "rs_join_halves")))

    def lg_row(f, b):
        return jnp.concatenate([_lane0(f), _lane0(b), jnp.zeros((D - 2 * HEADS,), F32)])

    small = _pack_small([
        sums_x[0], sums_x[1], sums_m[0], sums_fb[1], sums_fb[2], sums_fb[0],
        sums_c[0], sums_c[1],
        sums_x[2], sums_c[2], sums_fb[3], sums_m[1], sums_f[0],
        dlbf_x, dlbf_c, dlbb_x, dlbb_c,
        lg_row(dlgf_x, dlgb_x), lg_row(dlgf_c, dlgb_c),
        sums_f[1],
    ])
    return dx, grads, small


MESH = pl.DeviceIdType.MESH
ANY = pl.BlockSpec(memory_space=pl.ANY)
N_DEV = 8


def _place():
    return lax.axis_index("x"), lax.axis_index("y"), lax.axis_index("c")


def _other_chips(x, y):
    return [(1 - x, y), (x, 1 - y), (1 - x, 1 - y)]


def allgather8(xs, name):
    m, n = xs.shape

    def body(x_ref, out_ref, send_sems, recv_sems, local_sem):
        x, y, c = _place()
        me, sibling = (x, y, c), (x, y, 1 - c)
        chips = _other_chips(x, y)

        def rows(px, py, pc):
            return out_ref.at[pl.ds((4 * px + 2 * py + pc) * m, m), :]

        def copy(k, block, to, src=None):
            return pltpu.make_async_remote_copy(
                src_ref=rows(*block) if src is None else src, dst_ref=rows(*block),
                send_sem=send_sems.at[k], recv_sem=recv_sems.at[k], device_id=to, device_id_type=MESH)

        mine = pltpu.make_async_copy(x_ref, rows(*me), local_sem)
        mine.start()
        first = [copy(0, me, sibling, src=x_ref)]
        first += [copy(1 + j, me, (*chip, c), src=x_ref) for j, chip in enumerate(chips)]
        for cp in first:
            cp.start()
        passed = [copy(4 + j, (*chip, c), sibling) for j, chip in enumerate(chips)]
        for j, chip in enumerate(chips):
            copy(1 + j, (*chip, c), me).wait_recv()
            passed[j].start()
        copy(0, sibling, me).wait_recv()
        for j, chip in enumerate(chips):
            copy(4 + j, (*chip, 1 - c), me).wait_recv()
        for cp in first + passed:
            cp.wait_send()
        mine.wait()

    return pl.pallas_call(
        body, name=name,
        out_shape=jax.ShapeDtypeStruct((N_DEV * m, n), xs.dtype),
        in_specs=[pl.BlockSpec(memory_space=pltpu.VMEM)],
        out_specs=pl.BlockSpec(memory_space=pltpu.VMEM),
        scratch_shapes=[pltpu.SemaphoreType.DMA((7,)), pltpu.SemaphoreType.DMA((7,)), pltpu.SemaphoreType.DMA],
    )(xs)


def _gather_phases(ins, outs, send_sems, recv_sems, local_sems, relations=(0, 1, 2), stage=None):
    n = len(ins)
    x, y, c = _place()
    chips = _other_chips(x, y)

    def rows(i, core):
        h = ins[i].shape[0] // 2
        return pl.ds(pl.multiple_of(core * h, 16), h)

    def region(i, k, rs):
        if len(outs[i].shape) == 2:
            cols = ins[i].shape[1]
            return outs[i].at[rs, pl.ds(pl.multiple_of(k * cols, 128), cols)]
        return outs[i].at[k, rs, :]

    def landed(i, chip, core):
        return region(i, 2 * chip[0] + chip[1], rows(i, core))

    def copy(i, k, src, dst, to):
        return pltpu.make_async_remote_copy(src_ref=src, dst_ref=dst, send_sem=send_sems.at[6 * i + k],
                                            recv_sem=recv_sems.at[6 * i + k], device_id=to, device_id_type=MESH)

    def lift(i):
        return pltpu.make_async_copy(ins[i], stage[i], local_sems.at[i])

    def drop(i):
        return pltpu.make_async_copy(stage[i], region(i, 2 * x + y, pl.ds(0, ins[i].shape[0])), local_sems.at[i])

    def send(i, j):
        return copy(i, j, ins[i].at[rows(i, c), :], landed(i, (x, y), c), (*chips[j], c))

    def arrived(i, j, core, k):
        return copy(i, k, ins[i].at[rows(i, core), :], landed(i, chips[j], core), (x, y, 1 - c))

    def passed(i, j):
        return copy(i, 3 + j, landed(i, chips[j], c), landed(i, chips[j], c), (x, y, 1 - c))

    def start():
        for i in range(n):
            if stage is not None:
                lift(i).start()
            for j in relations:
                send(i, j).start()

    def forward():
        for i in range(n):
            if stage is not None:
                lift(i).wait()
                drop(i).start()
            for j in relations:
                arrived(i, j, c, j).wait_recv()
                passed(i, j).start()

    def finish():
        for i in range(n):
            for j in relations:
                arrived(i, j, 1 - c, 3 + j).wait_recv()
        for i in range(n):
            for j in relations:
                send(i, j).wait_send()
                passed(i, j).wait_send()
            if stage is not None:
                drop(i).wait()

    return start, forward, finish


def _gather_scratch(n):
    return [pltpu.SemaphoreType.DMA((6 * n,)), pltpu.SemaphoreType.DMA((6 * n,)), pltpu.SemaphoreType.DMA((n,))]


def rs_to_sibling(payloads, name):
    n = len(payloads)

    def body(*refs):
        start, finish = _to_sibling_phases(refs[:n], refs[n:2 * n], *refs[2 * n:])
        start()
        finish()

    return pl.pallas_call(
        body, name=name,
        out_shape=[jax.ShapeDtypeStruct(g.shape, g.dtype) for g in payloads],
        in_specs=[ANY] * n, out_specs=[ANY] * n,
        scratch_shapes=_to_sibling_scratch(n),
    )(*payloads)


def _to_sibling_phases(ins, outs, send_sems, recv_sems):
    def copies():
        x, y, c = _place()
        return [pltpu.make_async_remote_copy(src_ref=ins[i], dst_ref=outs[i], send_sem=send_sems.at[i],
                                             recv_sem=recv_sems.at[i], device_id=(x, y, 1 - c), device_id_type=MESH)
                for i in range(len(ins))]

    def start():
        for cp in copies():
            cp.start()

    def finish():
        for cp in copies():
            cp.wait()

    return start, finish


def _to_sibling_scratch(n):
    return [pltpu.SemaphoreType.DMA((n,)), pltpu.SemaphoreType.DMA((n,))]


def _to_chips_phases(ins, outs, send_sems, recv_sems):
    def copies():
        x, y, c = _place()
        return [pltpu.make_async_remote_copy(
            src_ref=ins[i].at[2 * px + py], dst_ref=outs[i].at[j], send_sem=send_sems.at[3 * i + j],
            recv_sem=recv_sems.at[3 * i + j], device_id=(px, py, c), device_id_type=MESH)
            for i in range(len(ins)) for j, (px, py) in enumerate(_other_chips(x, y))]

    def start():
        for cp in copies():
            cp.start()

    def finish():
        for cp in copies():
            cp.wait()

    return start, finish


def _to_chips_shapes(parts):
    return [jax.ShapeDtypeStruct((3,) + a.shape[1:], a.dtype) for a in parts]


def _to_chips_scratch(n):
    return [pltpu.SemaphoreType.DMA((3 * n,)), pltpu.SemaphoreType.DMA((3 * n,))]


def rs_join_halves(fulls, name):
    n = len(fulls)

    def body(*refs):
        outs = refs[n:2 * n]
        send_sems, recv_sems = refs[2 * n:]
        x, y, c = _place()

        def copy(i, core):
            h = fulls[i].shape[0] // 2
            rows = outs[i].at[pl.ds(pl.multiple_of(core * h, 8), h), :]
            return pltpu.make_async_remote_copy(src_ref=rows, dst_ref=rows, send_sem=send_sems.at[i],
                                                recv_sem=recv_sems.at[i], device_id=(x, y, 1 - c), device_id_type=MESH)

        sent = [copy(i, c) for i in range(n)]
        for cp in sent:
            cp.start()
        for i in range(n):
            copy(i, 1 - c).wait_recv()
        for cp in sent:
            cp.wait_send()

    return pl.pallas_call(
        body, name=name,
        out_shape=[jax.ShapeDtypeStruct(a.shape, a.dtype) for a in fulls],
        in_specs=[ANY] * n, out_specs=[ANY] * n,
        input_output_aliases={i: i for i in range(n)},
        scratch_shapes=[pltpu.SemaphoreType.DMA((n,)), pltpu.SemaphoreType.DMA((n,))],
    )(*fulls)


def _row_tile(rows, cols, limit_bytes=2 * 1024 * 1024, mult=8):
    best = mult
    for t in range(mult, rows + 1, mult):
        if rows % t == 0 and t * cols * 4 <= limit_bytes:
            best = t
    return best


def rs_add_sibling(g, recv, c, name):
    if g.ndim == 2:
        h, C = recv.shape[0], recv.shape[1] // N_SHARD
    else:
        _, h, C = recv.shape
    tr = _row_tile(h, C, mult=16)
    nt = h // tr

    def body(c_ref, g_ref, r_ref, o_ref, o16_ref):
        s = g_ref[...] + r_ref[...].astype(F32)
        o_ref[...] = s
        o16_ref[...] = s.astype(BF16)

    blk = pl.BlockSpec((None, tr, C), lambda k, i, c_ref: (k, i, 0))
    if g.ndim == 2:
        g_spec = pl.BlockSpec((tr, C), lambda k, i, c_ref: (c_ref[0] * nt + i, k))
        r_spec = pl.BlockSpec((tr, C), lambda k, i, c_ref: (i, k))
    else:
        g_spec = pl.BlockSpec((None, tr, C), lambda k, i, c_ref: (k, c_ref[0] * nt + i, 0))
        r_spec = blk
    return pl.pallas_call(
        body, name=name,
        grid_spec=pltpu.PrefetchScalarGridSpec(
            num_scalar_prefetch=1, grid=(N_SHARD, nt),
            in_specs=[g_spec, r_spec],
            out_specs=[blk, blk]),
        out_shape=[jax.ShapeDtypeStruct((N_SHARD, h, C), F32), jax.ShapeDtypeStruct((N_SHARD, h, C), BF16)],
        compiler_params=_params("parallel", "parallel"),
    )(c, g, recv)


def rs_add_chips(part, recv, place, name):
    _, h, C = part.shape
    tr = _row_tile(h, C, mult=16)
    nt = h // tr

    def body(k_ref, p_ref, r_ref, o_ref):
        o_ref[...] = ((p_ref[...] + r_ref[0].astype(F32)) + r_ref[1].astype(F32)) + r_ref[2].astype(F32)

    return pl.pallas_call(
        body, name=name,
        grid_spec=pltpu.PrefetchScalarGridSpec(
            num_scalar_prefetch=1, grid=(nt,),
            in_specs=[pl.BlockSpec((None, tr, C), lambda i, k_ref: (k_ref[0], i, 0)),
                      pl.BlockSpec((3, tr, C), lambda i, k_ref: (0, i, 0))],
            out_specs=pl.BlockSpec((tr, C), lambda i, k_ref: (k_ref[1] * nt + i, 0))),
        out_shape=jax.ShapeDtypeStruct((2 * h, C), F32),
        compiler_params=_params("parallel"),
    )(place, part, recv)


def _adamw_math(w, g, m, v):
    m = ADAM_B1 * m + (1.0 - ADAM_B1) * g
    v = ADAM_B2 * v + (1.0 - ADAM_B2) * (g * g)
    m_hat = m / (1.0 - ADAM_B1 ** ADAM_STEP)
    v_hat = v / (1.0 - ADAM_B2 ** ADAM_STEP)
    delta = -ADAM_LR * (m_hat / (jnp.sqrt(v_hat) + ADAM_EPS) + ADAM_WD * w)
    return delta, m, v


def adamw(w, g, m, v, name):
    R, C = w.shape
    tr = _row_tile(R, C, 1024 * 1024)

    def body(w_ref, g_ref, m_ref, v_ref, go_ref, d_ref, nm_ref, nv_ref):
        g = g_ref[...]
        go_ref[...] = g
        d_ref[...], nm_ref[...], nv_ref[...] = _adamw_math(w_ref[...], g, m_ref[...], v_ref[...])

    blk = pl.BlockSpec((tr, C), lambda i: (i, 0))
    return pl.pallas_call(
        body, name=name, grid=(R // tr,), in_specs=[blk] * 4, out_specs=[blk] * 4,
        out_shape=[jax.ShapeDtypeStruct((R, C), F32)] * 4,
        compiler_params=_params("parallel"),
    )(w, g, m, v)


MOD_SH = 6 * D // N_SHARD
PK_ROWS = 16


def mod_fwd(call16, w_sh, b_sh, name):
    def body(c_ref, w_ref, b_ref, o_ref):
        o_ref[...] = _dot(_silu_parts(c_ref[...])[0], w_ref[...], prec=HI) + b_ref[...]

    return pl.pallas_call(body, name=name, out_shape=jax.ShapeDtypeStruct((16, MOD_SH), F32),
                          compiler_params=_params())(call16, w_sh, b_sh)


def prep_small(lbf2, lbb2, theta_row, name):
    def body(f_ref, b_ref, t_ref, lbf_ref, lbb_ref, lg_ref):
        lbf_ref[...] = _sigmoid(f_ref[0:1, :] - f_ref[1:2, :])
        lbb_ref[...] = _sigmoid(b_ref[0:1, :] - b_ref[1:2, :])
        t = t_ref[...]
        lg_ref[...] = jnp.minimum(t, 0.0) - jnp.log(1.0 + jnp.exp(-jnp.abs(t)))

    row = jax.ShapeDtypeStruct((1, D), F32)
    return pl.pallas_call(body, name=name, out_shape=[row, row, row], compiler_params=_params())(lbf2, lbb2, theta_row)


def small_grads(g3, lbf, lbb, theta_row, name):
    def body(g_ref, lbf_ref, lbb_ref, t_ref, pk_ref, aux_ref):
        s = g_ref[0]
        for d in range(1, N_DEV):
            s = s + g_ref[d]
        pk_ref[...] = jnp.zeros_like(pk_ref)
        aux_ref[...] = jnp.zeros_like(aux_ref)
        pk_ref[1:7, :] = s[0:6]
        pk_ref[1:3, :] += s[6:8]
        pk_ref[7:8, :] = s[8:9] + s[9:10]
        pk_ref[8:9, :] = s[10:11]
        lbf, lbb = lbf_ref[...], lbb_ref[...]
        daf = (s[13:14] + s[14:15]) * lbf * (1.0 - lbf)
        dab = (s[15:16] + s[16:17]) * lbb * (1.0 - lbb)
        pk_ref[9:10, :] = daf
        pk_ref[10:11, :] = -daf
        pk_ref[11:12, :] = dab
        pk_ref[12:13, :] = -dab
        pk_ref[13:14, :] = s[11:12]
        pk_ref[14:15, :] = (s[17:18] + s[18:19]) * _sigmoid(-t_ref[...])
        pk_ref[15:16, :] = s[12:13]
        aux_ref[0:2, :] = s[6:8]
        aux_ref[2:3, :] = jnp.broadcast_to(jnp.sum(s[19:20], axis=-1, keepdims=True), (1, D))

    return pl.pallas_call(body, name=name,
                          out_shape=[jax.ShapeDtypeStruct((PK_ROWS, D), F32), jax.ShapeDtypeStruct((8, D), F32)],
                          compiler_params=_params())(g3, lbf, lbb, theta_row)


def mod_bwd(call16, dmod_sh, w_sh, name):
    def body(c_ref, d_ref, w_ref, dw_ref, ds_ref):
        dm = d_ref[...]
        dw_ref[...] = _dot(_silu_parts(c_ref[...])[0], dm, 0, 0, prec=HI)
        ds_ref[...] = jnp.zeros_like(ds_ref)
        ds_ref[0:1, :] = _dot(dm[8:9, :], w_ref[...], 1, 1, prec=HI)

    return pl.pallas_call(body, name=name,
                          out_shape=[jax.ShapeDtypeStruct((D, MOD_SH), F32), jax.ShapeDtypeStruct((8, D), F32)],
                          compiler_params=_params())(call16, dmod_sh, w_sh)


def adamw_small(g4, pk_g, pk_w, pk_m, pk_v, name):
    def body(g4_ref, g_ref, w_ref, m_ref, v_ref, go_ref, d_ref, nm_ref, nv_ref):
        w = w_ref[...]
        ds = ((g4_ref[0:1, :] + g4_ref[16:17, :]) + g4_ref[32:33, :]) + g4_ref[48:49, :]
        row = lax.broadcasted_iota(jnp.int32, (PK_ROWS, D), 0)
        g = jnp.where(row == 0, ds * _silu_parts(w[0:1, :])[1], g_ref[...])
        go_ref[...] = g
        d_ref[...], nm_ref[...], nv_ref[...] = _adamw_math(w, g, m_ref[...], v_ref[...])

    pk = jax.ShapeDtypeStruct((PK_ROWS, D), F32)
    return pl.pallas_call(body, name=name, out_shape=[pk, pk, pk, pk], compiler_params=_params())(g4, pk_g, pk_w, pk_m, pk_v)


def _pack_params(c_ctx, b_mod, n1, n2, lbf, lbb, hgn, th_f, th_b, fin):
    theta = jnp.concatenate([th_f.reshape(HEADS), th_b.reshape(HEADS), jnp.zeros((D - 2 * HEADS,), F32)])
    return jnp.concatenate([c_ctx.reshape(1, D), b_mod.reshape(6, D), n1.reshape(1, D), n2.reshape(1, D), lbf, lbb,
                            hgn.reshape(1, D), theta.reshape(1, D), fin.reshape(1, D)], axis=0)


def _unpack_params(pk):
    return (pk[0], pk[1:7].reshape(1, 6 * D), pk[7:8], pk[8:9], pk[9:11], pk[11:13], pk[13:14],
            pk[14, 0:HEADS].reshape(1, HEADS), pk[14, HEADS:2 * HEADS].reshape(1, HEADS), pk[15])


def kernel(x, c, ctx, c_ctx, w_mod, b_mod, norm1_w, norm2_w, w_in, hg_lb_fwd, hg_lb_bwd, hg_norm_w, rt_theta_fwd, rt_theta_bwd, w_proj_hgrn, w_proj_ret, w_out, w_ffn_gate, w_ffn_up, w_ffn_down, final_norm_w, loss_target, m_c_ctx, m_w_mod, m_b_mod, m_norm1_w, m_norm2_w, m_w_in, m_hg_lb_fwd, m_hg_lb_bwd, m_hg_norm_w, m_rt_theta_fwd, m_rt_theta_bwd, m_w_proj_hgrn, m_w_proj_ret, m_w_out, m_w_ffn_gate, m_w_ffn_up, m_w_ffn_down, m_final_norm_w, v_c_ctx, v_w_mod, v_b_mod, v_norm1_w, v_norm2_w, v_w_in, v_hg_lb_fwd, v_hg_lb_bwd, v_hg_norm_w, v_rt_theta_fwd, v_rt_theta_bwd, v_w_proj_hgrn, v_w_proj_ret, v_w_out, v_w_ffn_gate, v_w_ffn_up, v_w_ffn_down, v_final_norm_w):
    xi, yi, ci = _place()
    dev = 4 * xi + 2 * yi + ci
    chip = 2 * xi + yi
    core_arg = jnp.reshape(ci, (1,)).astype(jnp.int32)
    place_arg = jnp.stack([chip, ci]).astype(jnp.int32)

    c_all = allgather8(jnp.concatenate([c, jnp.zeros((7, D), F32)], axis=0), "gather_c").reshape(N_DEV, 8, D)[:, 0]
    call16 = jnp.concatenate([c_all, c_ctx.reshape(1, D), jnp.zeros((7, D), F32)], axis=0)
    b_sh = lax.dynamic_slice_in_dim(b_mod, chip * MOD_SH, MOD_SH, axis=1)
    mod_sh = mod_fwd(call16, w_mod[0], b_sh, "mod_fwd")
    mod_g = allgather8(mod_sh, "gather_mod").reshape(N_DEV, 16, MOD_SH)
    mod_all = jnp.concatenate([mod_g[0], mod_g[2], mod_g[4], mod_g[6]], axis=1)
    mod_x = lax.dynamic_index_in_dim(mod_all, dev, axis=0, keepdims=False).reshape(6, D)
    mod_c = mod_all[8].reshape(6, D)

    pk_w = _pack_params(c_ctx, b_mod, norm1_w, norm2_w, hg_lb_fwd, hg_lb_bwd, hg_norm_w, rt_theta_fwd, rt_theta_bwd, final_norm_w)
    theta_row = pk_w[14:15]
    lb_f, lb_b, lg_row = prep_small(hg_lb_fwd, hg_lb_bwd, theta_row, "prep_small")
    lg_f = jnp.broadcast_to(lg_row[0, 0:HEADS].reshape(HEADS, 1, 1), (HEADS, 1, RT_DV))
    lg_b = jnp.broadcast_to(lg_row[0, HEADS:2 * HEADS].reshape(HEADS, 1, 1), (HEADS, 1, RT_DV))

    rest = [s[0].astype(BF16) for s in (w_proj_hgrn, w_proj_ret, w_out, w_ffn_gate, w_ffn_up, w_ffn_down)]

    dx, full, small = local_step(x[0], ctx[0], loss_target[0], mod_x, mod_c, lb_f, lb_b, lg_f, lg_b,
                                 norm1_w, norm2_w, hg_norm_w, final_norm_w.reshape(1, D),
                                 {"w_in_shard": w_in[0].astype(BF16)}, rest, (ci, core_arg, place_arg))

    g3 = allgather8(small, "gather_small").reshape(N_DEV, SMALL_ROWS, D)
    pk_g, aux = small_grads(g3, lb_f, lb_b, theta_row, "small_grads")
    loss = aux[2, 0]
    dmod16 = jnp.concatenate([
        g3[:, 0:6, :].reshape(N_DEV, 6 * D),
        jnp.concatenate([aux[0], aux[1], jnp.zeros((4 * D,), F32)]).reshape(1, 6 * D),
        jnp.zeros((7, 6 * D), F32)], axis=0)
    dmod_sh = lax.dynamic_slice_in_dim(dmod16, chip * MOD_SH, MOD_SH, axis=1)
    g_wmod, dsilu = mod_bwd(call16, dmod_sh, w_mod[0], "mod_bwd")
    g4 = allgather8(dsilu, "gather_dsilu")
    pk_m = _pack_params(m_c_ctx, m_b_mod, m_norm1_w, m_norm2_w, m_hg_lb_fwd, m_hg_lb_bwd, m_hg_norm_w, m_rt_theta_fwd, m_rt_theta_bwd, m_final_norm_w)
    pk_v = _pack_params(v_c_ctx, v_b_mod, v_norm1_w, v_norm2_w, v_hg_lb_fwd, v_hg_lb_bwd, v_hg_norm_w, v_rt_theta_fwd, v_rt_theta_bwd, v_final_norm_w)
    pk_g, pk_d, pk_nm, pk_nv = adamw_small(g4, pk_g, pk_w, pk_m, pk_v, "adamw_small")

    big = {
        "w_mod": (g_wmod, w_mod, m_w_mod, v_w_mod),
        "w_in": (full["w_in"], w_in, m_w_in, v_w_in),
        "w_pa": (full["w_pa"], w_proj_hgrn, m_w_proj_hgrn, v_w_proj_hgrn),
        "w_pb": (full["w_pb"], w_proj_ret, m_w_proj_ret, v_w_proj_ret),
        "w_out": (full["w_out"], w_out, m_w_out, v_w_out),
        "wg": (full["wg"], w_ffn_gate, m_w_ffn_gate, v_w_ffn_gate),
        "wu": (full["wu"], w_ffn_up, m_w_ffn_up, v_w_ffn_up),
        "wd": (full["wd"], w_ffn_down, m_w_ffn_down, v_w_ffn_down),
    }
    res = {}
    for k, (g, wt, mt, vt) in big.items():
        res[k] = tuple(a[None] for a in adamw(wt[0], g, mt[0], vt[0], "adamw_" + k))

    sm = [_unpack_params(p) for p in (pk_g, pk_d, pk_nm, pk_nv)]
    outs = []
    for t in range(4):
        (s_cctx, s_bmod, s_n1, s_n2, s_lbf, s_lbb, s_hgn, s_thf, s_thb, s_fin) = sm[t]
        outs.append([s_cctx, res["w_mod"][t], s_bmod, s_n1, s_n2, res["w_in"][t], s_lbf, s_lbb, s_hgn, s_thf, s_thb,
                     res["w_pa"][t], res["w_pb"][t], res["w_out"][t], res["wg"][t], res["wu"][t], res["wd"][t], s_fin])
    return (loss, dx[None], *outs[0], *outs[1], *outs[2], *outs[3])
```

```python
import functools

import jax
import jax.numpy as jnp
from jax import lax
from jax.experimental import pallas as pl
from jax.experimental.pallas import tpu as pltpu

F32 = jnp.float32
BF16 = jnp.bfloat16
HI = lax.Precision.HIGHEST
CUMSUM_PRECISION = lax.Precision.HIGH

D = 1024
HEADS = 8
HG_D = 128
RT_DK = 128
RT_DV = 256
D_FF = 2816
D_IN = 13312
N_SHARD = 4
IN_SH = D_IN // N_SHARD
FF_SH = D_FF // N_SHARD
HG_CHUNK = 32
SCAN_ROWS = 256
HG_GROUP = 8
RT_GROUP = 4
PROJ_ROWS = 1024
EPS = 1e-6
GN_EPS = 1e-5
Q_SCALE = 128.0 ** -0.5
VMEM_LIMIT = 56 * 1024 * 1024

COL_HQ, COL_HFF, COL_HFB, COL_HI, COL_HG = 0, 8, 16, 24, 32
COL_RQ, COL_RK, COL_RV, COL_RG, COL_GA, COL_GB = 40, 48, 56, 72, 88, 96

ADAM_LR, ADAM_B1, ADAM_B2, ADAM_EPS, ADAM_WD, ADAM_STEP = 0.001, 0.9, 0.999, 1e-08, 0.01, 10


def _params(*sem):
    return pltpu.CompilerParams(dimension_semantics=sem, vmem_limit_bytes=VMEM_LIMIT)


def _dot(a, b, ca=1, cb=0, prec=None):
    return lax.dot_general(a, b, (((ca,), (cb,)), ((), ())), precision=prec, preferred_element_type=F32)


def _bdot(a, b, ca=1, cb=0):
    return _dot(a.astype(BF16), b.astype(BF16), ca, cb)


def _sigmoid(z):
    return 1.0 / (1.0 + jnp.exp(-z))


def _rowsum(a):
    return jnp.sum(a, axis=0, keepdims=True)


def _lanemean(a):
    return jnp.mean(a, axis=-1, keepdims=True)


def _grid_step(grid):
    pos, total = 0, 1
    for d, size in enumerate(grid):
        pos = pos * size + pl.program_id(d)
        total *= size
    return pos, total


def normmod_matmul(x, nw, sh, sc, w, name):
    L = x.shape[0]
    tm = min(PROJ_ROWS, L)
    tn = IN_SH // 2

    def body(x_ref, nw_ref, sh_ref, sc_ref, w_ref, p_ref, hx_ref, hx_scr):
        @pl.when(pl.program_id(1) == 0)
        def _():
            xv = x_ref[...]
            n = xv * lax.rsqrt(_lanemean(xv * xv) + EPS) * nw_ref[...]
            h = (n * (1.0 + sc_ref[...]) + sh_ref[...]).astype(BF16)
            hx_scr[...] = h
            hx_ref[...] = h

        p_ref[...] = _dot(hx_scr[...], w_ref[...])

    vec = pl.BlockSpec((1, D), lambda i, j: (0, 0))
    return pl.pallas_call(
        body, name=name,
        grid=(L // tm, D_IN // tn),
        in_specs=[pl.BlockSpec((tm, D), lambda i, j: (i, 0)), vec, vec, vec,
                  pl.BlockSpec((D, tn), lambda i, j: (0, j))],
        out_specs=[pl.BlockSpec((tm, tn), lambda i, j: (i, j)), pl.BlockSpec((tm, D), lambda i, j: (i, 0))],
        out_shape=[jax.ShapeDtypeStruct((L, D_IN), F32), jax.ShapeDtypeStruct((L, D), BF16)],
        scratch_shapes=[pltpu.VMEM((tm, D), BF16)],
        compiler_params=_params("parallel", "arbitrary"),
    )(x, nw, sh, sc, w)


def _w_halves(w_src, col0, tn, wbuf, wsems, pos):
    @pl.when(pos == 0)
    def _():
        for h in range(2):
            pltpu.make_async_copy(w_src.at[:, pl.ds(pl.multiple_of(col0 + h * tn, 128), tn)], wbuf.at[h],
                                  wsems.at[h]).start()

    for h in range(2):
        @pl.when(pos == h)
        def _(h=h):
            pltpu.make_async_copy(w_src.at[:, pl.ds(0, tn)], wbuf.at[h], wsems.at[h]).wait()


def in_proj_own(x, nw, sh, sc, w_shard, shard_arg, name):
    L = x.shape[0]
    tm = min(PROJ_ROWS, L)
    tn = IN_SH // 2
    grid = (L // tm, 2)

    def body(k_ref, x_ref, nw_ref, sh_ref, sc_ref, w_ref, p_ref, hx_ref, wfull_ref, hx_scr, wbuf, wsems, psems,
             *sems):
        pos, total = _grid_step(grid)
        start, forward, finish = _gather_phases([w_ref], [wfull_ref], *sems, relations=(0, 1))
        pl.when(pos == 0)(start)
        _w_halves(w_ref, 0, tn, wbuf, wsems, pos)

        def place(h):
            col = pl.multiple_of(k_ref[0] * IN_SH + h * tn, 128)
            return pltpu.make_async_copy(wbuf.at[h], wfull_ref.at[:, pl.ds(col, tn)], psems.at[h])

        for h in range(2):
            @pl.when(pos == h)
            def _(h=h):
                place(h).start()

        @pl.when(pl.program_id(1) == 0)
        def _():
            xv = x_ref[...]
            n = xv * lax.rsqrt(_lanemean(xv * xv) + EPS) * nw_ref[...]
            h = (n * (1.0 + sc_ref[...]) + sh_ref[...]).astype(BF16)
            hx_scr[...] = h
            hx_ref[...] = h

        p_ref[...] = _dot(hx_scr[...], wbuf[pl.program_id(1)])

        @pl.when(pos == total - 1)
        def _():
            forward()
            finish()
            place(0).wait()
            place(1).wait()

    vec = pl.BlockSpec((1, D), lambda i, j, k: (0, 0))
    return pl.pallas_call(
        body, name=name,
        grid_spec=pltpu.PrefetchScalarGridSpec(
            num_scalar_prefetch=1, grid=grid,
            in_specs=[pl.BlockSpec((tm, D), lambda i, j, k: (i, 0)), vec, vec, vec, ANY],
            out_specs=[pl.BlockSpec((tm, tn), lambda i, j, k: (i, 2 * k[0] + j)),
                       pl.BlockSpec((tm, D), lambda i, j, k: (i, 0)), ANY],
            scratch_shapes=[pltpu.VMEM((tm, D), BF16), pltpu.VMEM((2, D, tn), BF16), pltpu.SemaphoreType.DMA((2,)),
                            pltpu.SemaphoreType.DMA((2,))] + _gather_scratch(1)),
        out_shape=[jax.ShapeDtypeStruct((L, D_IN), F32), jax.ShapeDtypeStruct((L, D), BF16),
                   jax.ShapeDtypeStruct((D, D_IN), BF16)],
        compiler_params=_params("arbitrary", "arbitrary"),
    )(shard_arg, x, nw, sh, sc, w_shard)


def in_proj_next(hx, w_full, shard_arg, p, name, diag_from=None, gather=()):
    L = hx.shape[0]
    tm = min(PROJ_ROWS, L)
    tn = IN_SH // 2
    grid = (L // tm, 2)
    diag = diag_from is not None
    ng = len(gather)
    assert not (diag and ng)

    def body(k_ref, hx_ref, wf_in, p_in, *refs):
        n_src = 1 if diag else ng
        srcs = refs[:n_src]
        p_ref = refs[n_src]
        dsts = refs[n_src + 1:2 * n_src + 1]
        wbuf, wsems = refs[2 * n_src + 1:2 * n_src + 3]
        sems = refs[2 * n_src + 3:2 * n_src + 6]
        stage = refs[2 * n_src + 6:]
        pos, total = _grid_step(grid)
        w_src = dsts[0] if diag else wf_in
        if diag:
            start, forward, finish = _gather_phases(srcs, dsts, *sems, relations=(2,))
        elif ng:
            start, forward, finish = _gather_phases(srcs, dsts, *sems, stage=stage)
        if n_src:
            pl.when(pos == 0)(start)
        _w_halves(w_src, k_ref[0] * IN_SH, tn, wbuf, wsems, pos)
        p_ref[...] = _dot(hx_ref[...], wbuf[pl.program_id(1)])
        if n_src:
            @pl.when(pos == total - 1)
            def _():
                forward()
                finish()

    srcs = [diag_from] if diag else list(gather)
    out_shape = [jax.ShapeDtypeStruct((L, D_IN), F32)]
    if diag:
        out_shape.append(jax.ShapeDtypeStruct(w_full.shape, w_full.dtype))
    out_shape += [jax.ShapeDtypeStruct((N_SHARD,) + s.shape, s.dtype) for s in gather]
    aliases = {3: 0, 2: 1} if diag else {3: 0}
    return pl.pallas_call(
        body, name=name,
        grid_spec=pltpu.PrefetchScalarGridSpec(
            num_scalar_prefetch=1, grid=grid,
            in_specs=[pl.BlockSpec((tm, D), lambda i, j, k: (i, 0)), ANY, ANY] + [ANY] * len(srcs),
            out_specs=[pl.BlockSpec((tm, tn), lambda i, j, k: (i, 2 * k[0] + j))] + [ANY] * len(srcs),
            scratch_shapes=[pltpu.VMEM((2, D, tn), BF16), pltpu.SemaphoreType.DMA((2,))]
            + (_gather_scratch(len(srcs)) if srcs else []) + [pltpu.VMEM(s.shape, s.dtype) for s in gather]),
        out_shape=out_shape,
        input_output_aliases=aliases,
        compiler_params=_params("arbitrary", "arbitrary"),
    )(shard_arg, hx, w_full, p, *srcs)


def _hgrn_gates(z, lb):
    sg = _sigmoid(z)
    sgn = _sigmoid(-z)
    f = lb + (1.0 - lb) * sg
    k = (1.0 - lb) * sgn
    return sg, sgn, f, k


def _tri_chunks(n, chunk, reverse):
    r = lax.broadcasted_iota(jnp.int32, (n, n), 0)
    c = lax.broadcasted_iota(jnp.int32, (n, n), 1)
    same = (r // chunk) == (c // chunk)
    return jnp.where(same & ((r <= c) if reverse else (r >= c)), 1.0, 0.0).astype(F32)


def _decay3(b, reverse, key_major=False):
    C = b.shape[0]
    i0 = lax.broadcasted_iota(jnp.int32, (C, C, 1), 0)
    i1 = lax.broadcasted_iota(jnp.int32, (C, C, 1), 1)
    t, s = (i1, i0) if key_major else (i0, i1)
    mask = (t <= s) if reverse else (t >= s)
    diff = (b[None, :, :] - b[:, None, :]) if key_major else (b[:, None, :] - b[None, :, :])
    return jnp.exp(jnp.where(mask, diff, -jnp.inf))


HG_SUB = 8


def _hgrn_pairs(reverse):
    pairs = []
    size = HG_SUB
    while size < HG_CHUNK:
        for lo in range(0, HG_CHUNK, 2 * size):
            first, second = slice(lo, lo + size), slice(lo + size, lo + 2 * size)
            if reverse:
                pairs.append((first, second, lo + size))
            else:
                pairs.append((second, first, lo + size - 1))
        size *= 2
    return pairs


def _head_mask(g, nq, nk):
    r = lax.broadcasted_iota(jnp.int32, (g * nq, g * nk), 0) // nq
    c = lax.broadcasted_iota(jnp.int32, (g * nq, g * nk), 1) // nk
    return jnp.where(r == c, 1.0, 0.0).astype(F32)


def _hgrn_masks(g, reverse):
    return [_head_mask(g, qr.stop - qr.start, kr.stop - kr.start) for qr, kr, _ in _hgrn_pairs(reverse)]


def _stack(xs):
    return jnp.concatenate(xs, axis=0)


def _unstack(x, g):
    n = x.shape[0] // g
    return [x[h * n:(h + 1) * n] for h in range(g)]


def _add_blocks(acc, rows, part):
    for i in range(part.shape[0] // HG_SUB):
        acc[rows.start // HG_SUB + i] += part[i * HG_SUB:(i + 1) * HG_SUB]


def _hgrn_intra_fwd(qs, ks, vs, bs, masks, reverse):
    g = len(qs)
    blocks = []
    for q, k, v, b in zip(qs, ks, vs, bs):
        mine = []
        for lo in range(0, HG_CHUNK, HG_SUB):
            r = slice(lo, lo + HG_SUB)
            e3 = _decay3(b[r], reverse, key_major=True)
            att3 = jnp.sum(q[r][None, :, :] * k[r][:, None, :] * e3, axis=-1, keepdims=True)
            mine.append(jnp.sum(att3 * v[r][:, None, :], axis=0))
        blocks.append(mine)
    for (qr, kr, ref), mask in zip(_hgrn_pairs(reverse), masks):
        qt = _stack([q[qr] * jnp.exp(b[qr] - b[ref:ref + 1]) for q, b in zip(qs, bs)])
        kt = _stack([k[kr] * jnp.exp(b[ref:ref + 1] - b[kr]) for k, b in zip(ks, bs)])
        att = _bdot(qt, kt, 1, 1) * mask
        for mine, part in zip(blocks, _unstack(_bdot(att, _stack([v[kr] for v in vs])), g)):
            _add_blocks(mine, qr, part)
    return [jnp.concatenate(mine, axis=0) for mine in blocks]


def _hgrn_intra_bwd(qs, ks, vs, bs, d_os, masks, reverse):
    g = len(qs)
    nb = HG_CHUNK // HG_SUB
    dqs, dks, dvs = [], [], []
    for q, k, v, b, d_o in zip(qs, ks, vs, bs, d_os):
        dq, dk, dv = [None] * nb, [None] * nb, [None] * nb
        for i in range(nb):
            r = slice(i * HG_SUB, (i + 1) * HG_SUB)
            e3 = _decay3(b[r], reverse)
            p3 = jnp.sum(d_o[r][:, None, :] * v[r][None, :, :], axis=-1, keepdims=True) * e3
            dq[i] = jnp.sum(p3 * k[r][None, :, :], axis=1)
            dk[i] = jnp.sum(p3 * q[r][:, None, :], axis=0)
            att3 = jnp.sum(q[r][:, None, :] * k[r][None, :, :] * e3, axis=-1, keepdims=True)
            dv[i] = jnp.sum(att3 * d_o[r][:, None, :], axis=0)
        dqs.append(dq)
        dks.append(dk)
        dvs.append(dv)
    for (qr, kr, ref), mask in zip(_hgrn_pairs(reverse), masks):
        fqs = [jnp.exp(b[qr] - b[ref:ref + 1]) for b in bs]
        fks = [jnp.exp(b[ref:ref + 1] - b[kr]) for b in bs]
        qt = _stack([q[qr] * f for q, f in zip(qs, fqs)])
        kt = _stack([k[kr] * f for k, f in zip(ks, fks)])
        do_q = _stack([d_o[qr] for d_o in d_os])
        att = _bdot(qt, kt, 1, 1) * mask
        datt = _bdot(do_q, _stack([v[kr] for v in vs]), 1, 1) * mask
        for dq, part, f in zip(dqs, _unstack(_bdot(datt, kt), g), fqs):
            _add_blocks(dq, qr, part * f)
        for dk, part, f in zip(dks, _unstack(_bdot(datt, qt, 0, 0), g), fks):
            _add_blocks(dk, kr, part * f)
        for dv, part in zip(dvs, _unstack(_bdot(att, do_q, 0, 0), g)):
            _add_blocks(dv, kr, part)

    def cat(parts):
        return [jnp.concatenate(p, axis=0) for p in parts]

    return cat(dqs), cat(dks), cat(dvs)


def _hgrn_state_step(k, v, b, s_t, last):
    b_last = b[last:last + 1]
    return s_t * jnp.exp(b_last) + _bdot(v, k * jnp.exp(b_last - b), 0, 0)


def hgrn_scan_fwd(p, lb, s0, col_z, reverse, name, prev=None):
    has_prev = prev is not None
    L = p.shape[0]
    nB = L // SCAN_ROWS
    nC = SCAN_ROWS // HG_CHUNK
    C = HG_CHUNK
    G, W = HG_GROUP, HG_GROUP * HG_D
    last = 0 if reverse else C - 1

    def bmap(b):
        return (nB - 1 - b) if reverse else b

    def body(q_ref, z_ref, v_ref, lb_ref, s0_ref, *refs):
        prev_ref = refs[0] if has_prev else None
        o_ref, sfin_ref, sblk_ref, s_scr, k_scr, b_scr = refs[1:] if has_prev else refs
        blk = pl.program_id(1)

        @pl.when(blk == 0)
        def _():
            s_scr[...] = s0_ref[...]

        sblk_ref[...] = s_scr[...]
        _, _, f_all, k_all = _hgrn_gates(z_ref[...], lb_ref[...])
        k_scr[...] = k_all
        b_scr[...] = _dot(_tri_chunks(SCAN_ROWS, C, reverse), jnp.log(f_all), prec=CUMSUM_PRECISION)

        masks = _hgrn_masks(G, reverse)
        heads = [slice(j * HG_D, (j + 1) * HG_D) for j in range(G)]

        def chunk(ci, carry):
            c = (nC - 1 - ci) if reverse else ci
            rows = pl.ds(pl.multiple_of(c * C, C), C)
            qs = [q_ref[rows, lanes] * Q_SCALE for lanes in heads]
            vs = [v_ref[rows, lanes] for lanes in heads]
            ks = [k_scr[rows, lanes] for lanes in heads]
            bs = [b_scr[rows, lanes] for lanes in heads]
            o_in = _hgrn_intra_fwd(qs, ks, vs, bs, masks, reverse)
            for j, lanes in enumerate(heads):
                s_t = s_scr[j]
                o = o_in[j] + _bdot(qs[j] * jnp.exp(bs[j]), s_t, 1, 1)
                o_ref[rows, lanes] = o + prev_ref[rows, lanes] if has_prev else o
                s_scr[j] = _hgrn_state_step(ks[j], vs[j], bs[j], s_t, last)
            return carry

        lax.fori_loop(0, nC, chunk, 0)

        @pl.when(blk == nB - 1)
        def _():
            sfin_ref[...] = s_scr[...]

    def col(c0):
        return pl.BlockSpec((SCAN_ROWS, W), lambda h, b: (bmap(b), c0 // G + h))

    state = pl.BlockSpec((G, HG_D, HG_D), lambda h, b: (h, 0, 0))
    return pl.pallas_call(
        body, name=name,
        grid=(HEADS // G, nB),
        in_specs=[col(COL_HQ), col(col_z), col(COL_HI), pl.BlockSpec((1, W), lambda h, b: (0, h)), state]
        + ([pl.BlockSpec((SCAN_ROWS, W), lambda h, b: (bmap(b), h))] if has_prev else []),
        out_specs=[pl.BlockSpec((SCAN_ROWS, W), lambda h, b: (bmap(b), h)), state,
                   pl.BlockSpec((None, G, HG_D, HG_D), lambda h, b: (bmap(b), h, 0, 0))],
        out_shape=[jax.ShapeDtypeStruct((L, D), F32),
                   jax.ShapeDtypeStruct((HEADS, HG_D, HG_D), F32),
                   jax.ShapeDtypeStruct((nB, HEADS, HG_D, HG_D), F32)],
        scratch_shapes=[pltpu.VMEM((G, HG_D, HG_D), F32), pltpu.VMEM((SCAN_ROWS, W), F32),
                        pltpu.VMEM((SCAN_ROWS, W), F32)],
        compiler_params=_params("parallel", "arbitrary"),
    )(p, p, p, lb, s0, *([prev] if has_prev else []))


def hgrn_scan_bwd(p, lb, s_blocks, d_o, ds_fin, prev, col_z, reverse, name):
    L = p.shape[0]
    nB = L // SCAN_ROWS
    nC = SCAN_ROWS // HG_CHUNK
    C = HG_CHUNK
    G, W = HG_GROUP, HG_GROUP * HG_D
    last = 0 if reverse else C - 1
    has_prev = prev is not None
    out_dt = BF16 if has_prev else F32

    def bmap(b):
        return b if reverse else (nB - 1 - b)

    def body(*refs):
        q_ref, z_ref, v_ref, lb_ref, sblk_ref, do_ref, dsf_ref = refs[:7]
        refs = refs[7:]
        if has_prev:
            pq_ref, pv_ref = refs[:2]
            refs = refs[2:]
        (dq_ref, dz_ref, dv_ref, dlb_ref, ds0_ref, st_scr, run_scr, ds_scr, k_scr, b_scr, db_scr, dk_scr,
         rf_scr, sgn_scr, dzf_scr) = refs
        blk = pl.program_id(1)

        @pl.when(blk == 0)
        def _():
            ds_scr[...] = dsf_ref[...]
            dlb_ref[...] = jnp.zeros_like(dlb_ref)

        tri = _tri_chunks(SCAN_ROWS, C, reverse)
        row = lax.broadcasted_iota(jnp.int32, (C, HG_D), 0)
        lb_all = lb_ref[...]
        sg_all, sgn_all, f_all, k_all = _hgrn_gates(z_ref[...], lb_all)
        k_scr[...] = k_all
        rf_scr[...] = 1.0 / f_all
        sgn_scr[...] = sgn_all
        dzf_scr[...] = (1.0 - lb_all) * sg_all * sgn_all
        b_scr[...] = _dot(tri, jnp.log(f_all), prec=CUMSUM_PRECISION)
        run_scr[...] = sblk_ref[...]

        def recompute(ci, carry):
            c = (nC - 1 - ci) if reverse else ci
            rows = pl.ds(pl.multiple_of(c * C, C), C)
            for j in range(G):
                lanes = slice(j * HG_D, (j + 1) * HG_D)
                s_t = run_scr[j]
                st_scr[c, j] = s_t
                run_scr[j] = _hgrn_state_step(k_scr[rows, lanes], v_ref[rows, lanes], b_scr[rows, lanes], s_t, last)
            return carry

        lax.fori_loop(0, nC, recompute, 0)

        masks = _hgrn_masks(G, reverse)
        heads = [slice(j * HG_D, (j + 1) * HG_D) for j in range(G)]

        def chunk(ci, carry):
            c = ci if reverse else (nC - 1 - ci)
            rows = pl.ds(pl.multiple_of(c * C, C), C)
            ks = [k_scr[rows, lanes] for lanes in heads]
            bs = [b_scr[rows, lanes] for lanes in heads]
            qs = [q_ref[rows, lanes] * Q_SCALE for lanes in heads]
            vs = [v_ref[rows, lanes] for lanes in heads]
            d_os = [do_ref[rows, lanes] for lanes in heads]
            dq_ins, dk_ins, dv_ins = _hgrn_intra_bwd(qs, ks, vs, bs, d_os, masks, reverse)
            for j, lanes in enumerate(heads):
                k, b, q, v, d_o = ks[j], bs[j], qs[j], vs[j], d_os[j]
                s_t = st_scr[c, j]
                ds_t = ds_scr[j]
                eb = jnp.exp(b)
                b_last = b[last:last + 1]
                eb_last = jnp.exp(b_last)
                kdec = jnp.exp(b_last - b)
                qe = q * eb
                ke = k * kdec
                dq_tot = _bdot(d_o, s_t, 1, 0) * eb + dq_ins[j]
                dke = _bdot(v, ds_t, 1, 0)
                dk_tot = dke * kdec + dk_ins[j]
                dv = dv_ins[j] + _bdot(ke, ds_t, 1, 1)
                db_last = _rowsum(dke * ke) + eb_last * _rowsum(ds_t * s_t)
                db_scr[rows, lanes] = q * dq_tot - k * dk_tot + jnp.where(row == last, db_last, 0.0)
                dk_scr[rows, lanes] = dk_tot
                dq = dq_tot * Q_SCALE
                if has_prev:
                    dq = dq + pq_ref[rows, lanes]
                    dv = dv + pv_ref[rows, lanes]
                dq_ref[rows, lanes] = dq.astype(out_dt)
                dv_ref[rows, lanes] = dv.astype(out_dt)
                ds_scr[j] = ds_t * eb_last + _bdot(d_o, qe, 0, 0)
            return carry

        lax.fori_loop(0, nC, chunk, 0)

        g = _dot(tri, db_scr[...], 0, 0, prec=CUMSUM_PRECISION) * rf_scr[...] - dk_scr[...]
        dz_ref[...] = (g * dzf_scr[...]).astype(BF16)
        dlb_ref[...] += _rowsum(g * sgn_scr[...])

        @pl.when(blk == nB - 1)
        def _():
            ds0_ref[...] = ds_scr[...]

    def col(c0):
        return pl.BlockSpec((SCAN_ROWS, W), lambda h, b: (bmap(b), c0 // G + h))

    tile = pl.BlockSpec((SCAN_ROWS, W), lambda h, b: (bmap(b), h))
    state = pl.BlockSpec((G, HG_D, HG_D), lambda h, b: (h, 0, 0))
    in_specs = [col(COL_HQ), col(col_z), col(COL_HI),
                pl.BlockSpec((1, W), lambda h, b: (0, h)),
                pl.BlockSpec((None, G, HG_D, HG_D), lambda h, b: (bmap(b), h, 0, 0)),
                tile, state]
    args = [p, p, p, lb, s_blocks, d_o, ds_fin]
    if has_prev:
        in_specs += [tile, tile]
        args += list(prev)
    return pl.pallas_call(
        body, name=name,
        grid=(HEADS // G, nB),
        in_specs=in_specs,
        out_specs=[tile, tile, tile, pl.BlockSpec((1, W), lambda h, b: (0, h)), state],
        out_shape=[jax.ShapeDtypeStruct((L, D), out_dt), jax.ShapeDtypeStruct((L, D), BF16),
                   jax.ShapeDtypeStruct((L, D), out_dt), jax.ShapeDtypeStruct((1, D), F32),
                   jax.ShapeDtypeStruct((HEADS, HG_D, HG_D), F32)],
        scratch_shapes=[pltpu.VMEM((nC, G, HG_D, HG_D), F32), pltpu.VMEM((G, HG_D, HG_D), F32),
                        pltpu.VMEM((G, HG_D, HG_D), F32)] + [pltpu.VMEM((SCAN_ROWS, W), F32)] * 7,
        compiler_params=_params("parallel", "arbitrary"),
    )(*args)


def _rope(t, cosf, sinf):
    return t * cosf + pltpu.roll(t, RT_DK // 2, 1) * sinf


def _rope_t(d, cosf, sinf):
    return d * cosf + pltpu.roll(d * sinf, RT_DK // 2, 1)


def _ret_decays(lg, reverse):
    C = SCAN_ROWS
    t = lax.broadcasted_iota(jnp.int32, (C, C), 0)
    s = lax.broadcasted_iota(jnp.int32, (C, C), 1)
    delta = ((s - t) if reverse else (t - s)).astype(F32)
    dmat = jnp.where(delta >= 0, jnp.exp(lg * jnp.maximum(delta, 0.0)), 0.0)
    r = lax.broadcasted_iota(jnp.int32, (C, RT_DK), 0)
    pos = ((C - 1 - r) if reverse else r).astype(F32)
    lg1 = lg[:, :RT_DK]
    qdec = jnp.exp(lg1 * (pos + 1.0))
    kdec = jnp.exp(lg1 * (C - 1.0 - pos))
    sdec = jnp.exp(lg1 * float(C))
    return dmat, delta, pos, qdec, kdec, sdec


def ret_scan_fwd(p, cosf, sinf, lg, s0, reverse, name, prev=None):
    has_prev = prev is not None
    L = p.shape[0]
    C = SCAN_ROWS
    nB = L // C

    def bmap(b):
        return (nB - 1 - b) if reverse else b

    G = RT_GROUP

    def body(q_ref, k_ref, v_ref, cos_ref, sin_ref, lg_ref, s0_ref, *refs):
        prev_ref = refs[0] if has_prev else None
        o_ref, sfin_ref, sblk_ref, s_scr = refs[1:] if has_prev else refs
        blk = pl.program_id(1)

        @pl.when(blk == 0)
        def _():
            s_scr[...] = s0_ref[...]

        sblk_ref[...] = s_scr[...]
        cosf, sinf = cos_ref[...], sin_ref[...]
        for j in range(G):
            lk, lv = slice(j * RT_DK, (j + 1) * RT_DK), slice(j * RT_DV, (j + 1) * RT_DV)
            s_t = s_scr[j]
            dmat, _, _, qdec, kdec, sdec = _ret_decays(lg_ref[j], reverse)
            q = _rope(q_ref[:, lk] * Q_SCALE, cosf, sinf)
            k = _rope(k_ref[:, lk], cosf, sinf)
            v = v_ref[:, lv]
            att = _bdot(q, k, 1, 1) * dmat
            o = _bdot(att, v) + _bdot(q * qdec, s_t, 1, 1)
            o_ref[:, lv] = o + prev_ref[:, lv] if has_prev else o
            s_scr[j] = s_t * sdec + _bdot(v, k * kdec, 0, 0)

        @pl.when(blk == nB - 1)
        def _():
            sfin_ref[...] = s_scr[...]

    def col(c0):
        return pl.BlockSpec((C, G * RT_DK), lambda h, b: (bmap(b), c0 // G + h))

    tab = pl.BlockSpec((C, RT_DK), lambda h, b: (bmap(b), 0))
    state = pl.BlockSpec((G, RT_DV, RT_DK), lambda h, b: (h, 0, 0))
    return pl.pallas_call(
        body, name=name,
        grid=(HEADS // G, nB),
        in_specs=[col(COL_RQ), col(COL_RK),
                  pl.BlockSpec((C, G * RT_DV), lambda h, b: (bmap(b), COL_RV // (2 * G) + h)),
                  tab, tab, pl.BlockSpec((G, 1, RT_DV), lambda h, b: (h, 0, 0)), state]
        + ([pl.BlockSpec((C, G * RT_DV), lambda h, b: (bmap(b), h))] if has_prev else []),
        out_specs=[pl.BlockSpec((C, G * RT_DV), lambda h, b: (bmap(b), h)), state,
                   pl.BlockSpec((None, G, RT_DV, RT_DK), lambda h, b: (bmap(b), h, 0, 0))],
        out_shape=[jax.ShapeDtypeStruct((L, HEADS * RT_DV), F32),
                   jax.ShapeDtypeStruct((HEADS, RT_DV, RT_DK), F32),
                   jax.ShapeDtypeStruct((nB, HEADS, RT_DV, RT_DK), F32)],
        scratch_shapes=[pltpu.VMEM((G, RT_DV, RT_DK), F32)],
        compiler_params=_params("parallel", "arbitrary"),
    )(p, p, p, cosf, sinf, lg, s0, *([prev] if has_prev else []))


def ret_scan_bwd(p, cosf, sinf, lg, s_blocks, d_o, ds_fin, prev, reverse, name):
    L = p.shape[0]
    C = SCAN_ROWS
    nB = L // C
    has_prev = prev is not None
    out_dt = BF16
    G = RT_GROUP

    def bmap(b):
        return b if reverse else (nB - 1 - b)

    def body(*refs):
        q_ref, k_ref, v_ref, cos_ref, sin_ref, lg_ref, sblk_ref, do_ref, dsf_ref = refs[:9]
        refs = refs[9:]
        if has_prev:
            pq_ref, pk_ref, pv_ref = refs[:3]
            refs = refs[3:]
        dq_ref, dk_ref, dv_ref, dlg_ref, ds0_ref, ds_scr = refs
        blk = pl.program_id(1)

        @pl.when(blk == 0)
        def _():
            ds_scr[...] = dsf_ref[...]
            dlg_ref[...] = jnp.zeros_like(dlg_ref)

        cosf, sinf = cos_ref[...], sin_ref[...]
        for j in range(G):
            lk, lv = slice(j * RT_DK, (j + 1) * RT_DK), slice(j * RT_DV, (j + 1) * RT_DV)
            s_t = sblk_ref[j]
            ds_t = ds_scr[j]
            dmat, delta, pos, qdec, kdec, sdec = _ret_decays(lg_ref[j], reverse)
            q = _rope(q_ref[:, lk] * Q_SCALE, cosf, sinf)
            k = _rope(k_ref[:, lk], cosf, sinf)
            v = v_ref[:, lv]
            d_o = do_ref[:, lv]
            att_raw = _bdot(q, k, 1, 1)
            datt_m = _bdot(d_o, v, 1, 1) * dmat
            dqd = _bdot(d_o, s_t, 1, 0)
            dkd = _bdot(v, ds_t, 1, 0)
            dq = _bdot(datt_m, k) + dqd * qdec
            dk = _bdot(datt_m, q, 0, 0) + dkd * kdec
            dv = _bdot(att_raw * dmat, d_o, 0, 0) + _bdot(k * kdec, ds_t, 1, 1)
            ds_scr[j] = ds_t * sdec + _bdot(d_o, q * qdec, 0, 0)
            t1 = jnp.sum(_rowsum(datt_m * att_raw * delta), axis=-1, keepdims=True)
            t23 = jnp.sum(_rowsum((pos + 1.0) * qdec * q * dqd + (C - 1.0 - pos) * kdec * k * dkd), axis=-1, keepdims=True)
            t4 = jnp.sum(_rowsum(ds_t * s_t * sdec), axis=-1, keepdims=True) * float(C)
            dlg_ref[j] += jnp.broadcast_to(t1 + t23 + t4, (1, RT_DK))
            if has_prev:
                dq = _rope_t(dq + pq_ref[:, lk].astype(F32), cosf, sinf) * Q_SCALE
                dk = _rope_t(dk + pk_ref[:, lk].astype(F32), cosf, sinf)
                dv = dv + pv_ref[:, lv].astype(F32)
            dq_ref[:, lk] = dq.astype(out_dt)
            dk_ref[:, lk] = dk.astype(out_dt)
            dv_ref[:, lv] = dv.astype(out_dt)

        @pl.when(blk == nB - 1)
        def _():
            ds0_ref[...] = ds_scr[...]

    def col(c0):
        return pl.BlockSpec((C, G * RT_DK), lambda h, b: (bmap(b), c0 // G + h))

    tab = pl.BlockSpec((C, RT_DK), lambda h, b: (bmap(b), 0))
    state = pl.BlockSpec((G, RT_DV, RT_DK), lambda h, b: (h, 0, 0))
    tk = pl.BlockSpec((C, G * RT_DK), lambda h, b: (bmap(b), h))
    tv = pl.BlockSpec((C, G * RT_DV), lambda h, b: (bmap(b), h))
    in_specs = [col(COL_RQ), col(COL_RK),
                pl.BlockSpec((C, G * RT_DV), lambda h, b: (bmap(b), COL_RV // (2 * G) + h)),
                tab, tab, pl.BlockSpec((G, 1, RT_DV), lambda h, b: (h, 0, 0)),
                pl.BlockSpec((None, G, RT_DV, RT_DK), lambda h, b: (bmap(b), h, 0, 0)),
                tv, state]
    args = [p, p, p, cosf, sinf, lg, s_blocks, d_o, ds_fin]
    if has_prev:
        in_specs += [tk, tk, tv]
        args += list(prev)
    return pl.pallas_call(
        body, name=name,
        grid=(HEADS // G, nB),
        in_specs=in_specs,
        out_specs=[tk, tk, tv, pl.BlockSpec((G, 1, RT_DK), lambda h, b: (h, 0, 0)), state],
        out_shape=[jax.ShapeDtypeStruct((L, D), out_dt), jax.ShapeDtypeStruct((L, D), out_dt),
                   jax.ShapeDtypeStruct((L, HEADS * RT_DV), out_dt),
                   jax.ShapeDtypeStruct((HEADS, 1, RT_DK), F32),
                   jax.ShapeDtypeStruct((HEADS, RT_DV, RT_DK), F32)],
        scratch_shapes=[pltpu.VMEM((G, RT_DV, RT_DK), F32)],
        compiler_params=_params("parallel", "arbitrary"),
    )(*args)


def _silu_parts(h):
    s = _sigmoid(h)
    return h * s, s * (1.0 + h * (1.0 - s))


def _head_rms(o):
    outs, rs = [], []
    for h in range(HEADS):
        oh = o[:, h * HG_D:(h + 1) * HG_D]
        r = lax.rsqrt(_lanemean(oh * oh) + EPS)
        outs.append(oh * r)
        rs.append(r)
    return outs, rs


def _group_norm(o):
    outs, rs = [], []
    for h in range(HEADS):
        oh = o[:, h * RT_DV:(h + 1) * RT_DV]
        c = oh - _lanemean(oh)
        r = lax.rsqrt(_lanemean(c * c) + GN_EPS)
        outs.append(c * r)
        rs.append(r)
    return outs, rs


MIX_ROWS = 256


def _mix_specs(rows):
    def t(w, c=0):
        return pl.BlockSpec((rows, w), lambda i: (i, c))

    return t


def mix_fwd(o_hg, o_rt, p, x, g1, hgw, w_pa, w_pb, w_out, name):
    L = x.shape[0]
    t = _mix_specs(MIX_ROWS)

    def body(ohg_ref, ort_ref, hg_ref, rg0_ref, rg1_ref, ga_ref, gb_ref, x_ref, g1_ref, hgw_ref,
             wpa_ref, wpb_ref, wout_ref, x1_ref, xmix_ref, merged_ref, ya_ref, yb_ref):
        nh, _ = _head_rms(ohg_ref[...])
        ya = jnp.concatenate(nh, axis=1) * hgw_ref[...] * _silu_parts(hg_ref[...])[0]
        gn, _ = _group_norm(ort_ref[...])
        rg = jnp.concatenate([rg0_ref[...], rg1_ref[...]], axis=1)
        yb = jnp.concatenate(gn, axis=1) * _silu_parts(rg)[0]
        ya16, yb16 = ya.astype(BF16), yb.astype(BF16)
        merged = (_sigmoid(ga_ref[...]) * _dot(ya16, wpa_ref[...])
                  + _sigmoid(gb_ref[...]) * _dot(yb16, wpb_ref[...])).astype(BF16)
        x_mix = _dot(merged, wout_ref[...])
        x1_ref[...] = x_ref[...] + g1_ref[...] * x_mix
        xmix_ref[...] = x_mix
        merged_ref[...] = merged
        ya_ref[...] = ya16
        yb_ref[...] = yb16

    vec = pl.BlockSpec((1, D), lambda i: (0, 0))

    def full(a):
        return pl.BlockSpec(a.shape, lambda i: (0, 0), pipeline_mode=pl.Buffered(1))

    return pl.pallas_call(
        body, name=name,
        grid=(L // MIX_ROWS,),
        in_specs=[t(D), t(2 * D), t(D, COL_HG // 8), t(D, COL_RG // 8), t(D, COL_RG // 8 + 1),
                  t(D, COL_GA // 8), t(D, COL_GB // 8), t(D), vec, vec, full(w_pa), full(w_pb), full(w_out)],
        out_specs=[t(D), t(D), t(D), t(D), t(2 * D)],
        out_shape=[jax.ShapeDtypeStruct((L, D), F32), jax.ShapeDtypeStruct((L, D), F32),
                   jax.ShapeDtypeStruct((L, D), BF16), jax.ShapeDtypeStruct((L, D), BF16),
                   jax.ShapeDtypeStruct((L, 2 * D), BF16)],
        compiler_params=_params("parallel"),
    )(o_hg, o_rt, p, p, p, p, p, x, g1, hgw, w_pa, w_pb, w_out)


def mix_bwd(dx1, x_mix, ya, yb, o_hg, o_rt, p, g1, hgw, w_pa, w_pb, w_out, name, to_sibling=()):
    L = dx1.shape[0]
    t = _mix_specs(MIX_ROWS)
    nx = len(to_sibling)
    steps = L // MIX_ROWS

    def body(dx1_ref, xmix_ref, ya_ref, yb_ref, ohg_ref, ort_ref, hg_ref, rg0_ref, rg1_ref,
             ga_ref, gb_ref, g1_ref, hgw_ref, wpa_ref, wpb_ref, wout_ref, *refs):
        (dxm_ref, da_ref, db_ref, dga_ref, dgb_ref, dhg_ref, drg_ref, dohg_ref, dort_ref,
         sums_ref) = refs[nx:nx + 10]
        if nx:
            start, finish = _to_sibling_phases(refs[:nx], refs[nx + 10:2 * nx + 10], *refs[2 * nx + 10:])
            pl.when(pl.program_id(0) == 0)(start)
            pl.when(pl.program_id(0) == steps - 1)(finish)

        @pl.when(pl.program_id(0) == 0)
        def _():
            sums_ref[...] = jnp.zeros_like(sums_ref)

        dx1 = dx1_ref[...]
        dxm = (g1_ref[...] * dx1).astype(BF16)
        dxm_ref[...] = dxm
        dmerged = _dot(dxm, wout_ref[...], 1, 1)
        a = _dot(ya_ref[...], wpa_ref[...])
        bm = _dot(yb_ref[...], wpb_ref[...])
        sa, sb = _sigmoid(ga_ref[...]), _sigmoid(gb_ref[...])
        d_a = (dmerged * sa).astype(BF16)
        d_b = (dmerged * sb).astype(BF16)
        da_ref[...] = d_a
        db_ref[...] = d_b
        dga_ref[...] = (dmerged * a * sa * (1.0 - sa)).astype(BF16)
        dgb_ref[...] = (dmerged * bm * sb * (1.0 - sb)).astype(BF16)
        dya = _dot(d_a, wpa_ref[...], 1, 1)
        dyb = _dot(d_b, wpb_ref[...], 1, 1)

        hgw = hgw_ref[...]
        silu_h, dsilu_h = _silu_parts(hg_ref[...])
        nh, rh = _head_rms(ohg_ref[...])
        n = jnp.concatenate(nh, axis=1)
        dhg_ref[...] = (dya * n * hgw * dsilu_h).astype(BF16)
        dn = dya * hgw * silu_h
        douts = []
        for h in range(HEADS):
            dnh = dn[:, h * HG_D:(h + 1) * HG_D]
            douts.append(rh[h] * (dnh - nh[h] * _lanemean(dnh * nh[h])))
        dohg_ref[...] = jnp.concatenate(douts, axis=1)

        rg = jnp.concatenate([rg0_ref[...], rg1_ref[...]], axis=1)
        silu_r, dsilu_r = _silu_parts(rg)
        gn, rr = _group_norm(ort_ref[...])
        g = jnp.concatenate(gn, axis=1)
        drg_ref[...] = (dyb * g * dsilu_r).astype(BF16)
        dgn = dyb * silu_r
        douts = []
        for h in range(HEADS):
            dgh = dgn[:, h * RT_DV:(h + 1) * RT_DV]
            douts.append(rr[h] * (dgh - _lanemean(dgh) - gn[h] * _lanemean(dgh * gn[h])))
        dort_ref[...] = jnp.concatenate(douts, axis=1)

        sums_ref[0:1, :] += _rowsum(dx1 * xmix_ref[...])
        sums_ref[1:2, :] += _rowsum(dya * n * silu_h)

    vec = pl.BlockSpec((1, D), lambda i: (0, 0))

    def full(a):
        return pl.BlockSpec(a.shape, lambda i: (0, 0), pipeline_mode=pl.Buffered(1))

    bf = functools.partial(jax.ShapeDtypeStruct, dtype=BF16)
    return pl.pallas_call(
        body, name=name,
        grid=(L // MIX_ROWS,),
        in_specs=[t(D), t(D), t(D), t(2 * D), t(D), t(2 * D),
                  t(D, COL_HG // 8), t(D, COL_RG // 8), t(D, COL_RG // 8 + 1), t(D, COL_GA // 8), t(D, COL_GB // 8),
                  vec, vec, full(w_pa), full(w_pb), full(w_out)] + [ANY] * nx,
        out_specs=[t(D), t(D), t(D), t(D), t(D), t(D), t(2 * D), t(D), t(2 * D),
                   pl.BlockSpec((8, D), lambda i: (0, 0))] + [ANY] * nx,
        out_shape=[bf((L, D)), bf((L, D)), bf((L, D)), bf((L, D)), bf((L, D)), bf((L, D)), bf((L, 2 * D)),
                   jax.ShapeDtypeStruct((L, D), F32), jax.ShapeDtypeStruct((L, 2 * D), F32),
                   jax.ShapeDtypeStruct((8, D), F32)] + [jax.ShapeDtypeStruct(a.shape, a.dtype) for a in to_sibling],
        scratch_shapes=_to_sibling_scratch(nx) if nx else [],
        compiler_params=_params("arbitrary"),
    )(dx1, x_mix, ya, yb, o_hg, o_rt, p, p, p, p, p, g1, hgw, w_pa, w_pb, w_out, *to_sibling)


FFN_ROWS = 512


def ffn_fwd(x1, target, nw2, sh2, sc2, g2, fw, wg, wu, wd, name):
    L = x1.shape[0]
    tm = min(FFN_ROWS, L)

    def body(x1_ref, tgt_ref, nw2_ref, sh2_ref, sc2_ref, g2_ref, fw_ref, wg_ref, wu_ref, wd_ref,
             hx2_ref, g_ref, u_ref, h_ref, f_ref, dx2_ref, sums_ref, hx_scr, acc):
        i, j = pl.program_id(0), pl.program_id(1)

        @pl.when((i == 0) & (j == 0))
        def _():
            sums_ref[...] = jnp.zeros_like(sums_ref)

        @pl.when(j == 0)
        def _():
            xv = x1_ref[...]
            n = xv * lax.rsqrt(_lanemean(xv * xv) + EPS) * nw2_ref[...]
            h = (n * (1.0 + sc2_ref[...]) + sh2_ref[...]).astype(BF16)
            hx_scr[...] = h
            hx2_ref[...] = h
            acc[...] = jnp.zeros_like(acc)

        hx = hx_scr[...]
        g = _dot(hx, wg_ref[...])
        u = _dot(hx, wu_ref[...])
        hh = (_silu_parts(g)[0] * u).astype(BF16)
        g_ref[...] = g
        u_ref[...] = u
        h_ref[...] = hh
        acc[...] += _dot(hh, wd_ref[...])

        @pl.when(j == N_SHARD - 1)
        def _():
            f = acc[...]
            f_ref[...] = f
            x2 = x1_ref[...] + g2_ref[...] * f
            r = lax.rsqrt(_lanemean(x2 * x2) + EPS)
            fw = fw_ref[...]
            e = x2 * r * fw - tgt_ref[...]
            dy = e * (1.0 / D)
            dyw = dy * fw
            dx2_ref[...] = r * dyw - x2 * (r * r * r) * _lanemean(dyw * x2)
            sums_ref[0:1, :] += _rowsum(dy * x2 * r)
            sums_ref[1:2, :] += _rowsum(e * e) * (0.5 / D)

    row = pl.BlockSpec((tm, D), lambda i, j: (i, 0))
    vec = pl.BlockSpec((1, D), lambda i, j: (0, 0))
    sh = pl.BlockSpec((None, tm, FF_SH), lambda i, j: (j, i, 0))
    return pl.pallas_call(
        body, name=name,
        grid=(L // tm, N_SHARD),
        in_specs=[row, row, vec, vec, vec, vec, vec,
                  pl.BlockSpec((None, D, FF_SH), lambda i, j: (j, 0, 0)),
                  pl.BlockSpec((None, D, FF_SH), lambda i, j: (j, 0, 0)),
                  pl.BlockSpec((None, FF_SH, D), lambda i, j: (j, 0, 0))],
        out_specs=[row, sh, sh, sh, row, row, pl.BlockSpec((8, D), lambda i, j: (0, 0))],
        out_shape=[jax.ShapeDtypeStruct((L, D), BF16),
                   jax.ShapeDtypeStruct((N_SHARD, L, FF_SH), F32), jax.ShapeDtypeStruct((N_SHARD, L, FF_SH), F32),
                   jax.ShapeDtypeStruct((N_SHARD, L, FF_SH), BF16),
                   jax.ShapeDtypeStruct((L, D), F32), jax.ShapeDtypeStruct((L, D), F32),
                   jax.ShapeDtypeStruct((8, D), F32)],
        scratch_shapes=[pltpu.VMEM((tm, D), BF16), pltpu.VMEM((tm, D), F32)],
        compiler_params=_params("arbitrary", "arbitrary"),
    )(x1, target, nw2, sh2, sc2, g2, fw, wg, wu, wd)


def ffn_bwd(dx2, x1, f, g, u, nw2, sc2, g2, wg, wu, wd, name):
    L = x1.shape[0]
    tm = min(FFN_ROWS, L)

    def body(dx2_ref, x1_ref, f_ref, g_ref, u_ref, nw2_ref, sc2_ref, g2_ref, wg_ref, wu_ref, wd_ref,
             df_ref, dg_ref, du_ref, dx1_ref, sums_ref, df_scr, acc):
        i, j = pl.program_id(0), pl.program_id(1)

        @pl.when((i == 0) & (j == 0))
        def _():
            sums_ref[...] = jnp.zeros_like(sums_ref)

        @pl.when(j == 0)
        def _():
            dx2 = dx2_ref[...]
            df = (g2_ref[...] * dx2).astype(BF16)
            df_scr[...] = df
            df_ref[...] = df
            sums_ref[0:1, :] += _rowsum(dx2 * f_ref[...])
            acc[...] = jnp.zeros_like(acc)

        dh = _dot(df_scr[...], wd_ref[...], 1, 1)
        gv, uv = g_ref[...], u_ref[...]
        silu_g, dsilu_g = _silu_parts(gv)
        dg = (dh * uv * dsilu_g).astype(BF16)
        du = (dh * silu_g).astype(BF16)
        dg_ref[...] = dg
        du_ref[...] = du
        acc[...] += _dot(dg, wg_ref[...], 1, 1) + _dot(du, wu_ref[...], 1, 1)

        @pl.when(j == N_SHARD - 1)
        def _():
            dhx = acc[...]
            xv = x1_ref[...]
            r = lax.rsqrt(_lanemean(xv * xv) + EPS)
            n0 = xv * r
            nw = nw2_ref[...]
            dn2 = dhx * (1.0 + sc2_ref[...])
            dn0 = dn2 * nw
            dx1_ref[...] = dx2_ref[...] + r * (dn0 - n0 * _lanemean(dn0 * n0))
            sums_ref[1:2, :] += _rowsum(dhx)
            sums_ref[2:3, :] += _rowsum(dhx * n0 * nw)
            sums_ref[3:4, :] += _rowsum(dn2 * n0)

    row = pl.BlockSpec((tm, D), lambda i, j: (i, 0))
    vec = pl.BlockSpec((1, D), lambda i, j: (0, 0))
    sh = pl.BlockSpec((None, tm, FF_SH), lambda i, j: (j, i, 0))
    return pl.pallas_call(
        body, name=name,
        grid=(L // tm, N_SHARD),
        in_specs=[row, row, row, sh, sh, vec, vec, vec,
                  pl.BlockSpec((None, D, FF_SH), lambda i, j: (j, 0, 0)),
                  pl.BlockSpec((None, D, FF_SH), lambda i, j: (j, 0, 0)),
                  pl.BlockSpec((None, FF_SH, D), lambda i, j: (j, 0, 0))],
        out_specs=[row, sh, sh, row, pl.BlockSpec((8, D), lambda i, j: (0, 0))],
        out_shape=[jax.ShapeDtypeStruct((L, D), BF16),
                   jax.ShapeDtypeStruct((N_SHARD, L, FF_SH), BF16), jax.ShapeDtypeStruct((N_SHARD, L, FF_SH), BF16),
                   jax.ShapeDtypeStruct((L, D), F32), jax.ShapeDtypeStruct((8, D), F32)],
        scratch_shapes=[pltpu.VMEM((tm, D), BF16), pltpu.VMEM((tm, D), F32)],
        compiler_params=_params("arbitrary", "arbitrary"),
    )(dx2, x1, f, g, u, nw2, sc2, g2, wg, wu, wd)


def matmul_tn(a, b, name):
    na, K, M = a.shape
    nb, _, N = b.shape
    n = max(na, nb)
    tk = min(1024, K)
    tn = N if N <= 1024 else N // 2

    def body(a_ref, b_ref, o_ref):
        @pl.when(pl.program_id(2) == 0)
        def _():
            o_ref[...] = jnp.zeros_like(o_ref)

        o_ref[...] += _dot(a_ref[...], b_ref[...], 0, 0)

    return pl.pallas_call(
        body, name=name,
        grid=(n, N // tn, K // tk),
        in_specs=[pl.BlockSpec((None, tk, M), lambda s, j, kk: (s if na > 1 else 0, kk, 0)),
                  pl.BlockSpec((None, tk, tn), lambda s, j, kk: (s if nb > 1 else 0, kk, j))],
        out_specs=pl.BlockSpec((None, M, tn), lambda s, j, kk: (s, 0, j)),
        out_shape=jax.ShapeDtypeStruct((n, M, N), F32),
        compiler_params=_params("parallel", "parallel", "arbitrary"),
    )(a, b)


def matmul_tn_pair(a, b1, b2, name):
    K, M = a.shape
    n, _, N = b1.shape
    tk = min(1024, K)

    def body(a_ref, b1_ref, b2_ref, o1_ref, o2_ref):
        @pl.when(pl.program_id(1) == 0)
        def _():
            o1_ref[...] = jnp.zeros_like(o1_ref)
            o2_ref[...] = jnp.zeros_like(o2_ref)

        at = a_ref[...].T
        o1_ref[...] += _dot(at, b1_ref[...])
        o2_ref[...] += _dot(at, b2_ref[...])

    b_spec = pl.BlockSpec((None, tk, N), lambda s, kk: (s, kk, 0))
    o_spec = pl.BlockSpec((None, M, N), lambda s, kk: (s, 0, 0))
    return pl.pallas_call(
        body, name=name,
        grid=(n, K // tk),
        in_specs=[pl.BlockSpec((tk, M), lambda s, kk: (kk, 0)), b_spec, b_spec],
        out_specs=[o_spec, o_spec],
        out_shape=[jax.ShapeDtypeStruct((n, M, N), F32)] * 2,
        compiler_params=_params("parallel", "arbitrary"),
    )(a, b1, b2)


PIECE_COLS = 1024
N_PIECE_BLOCKS = D_IN // PIECE_COLS


def _piece_blocks(pieces):
    out, col = [], 0
    for arr, width in pieces:
        if arr is not None:
            out.append((arr, col // PIECE_COLS, width // PIECE_COLS))
        col += width
    assert col == D_IN
    return out


def _piece_feed(p_refs, blocks, buf, sems, tile_of, pos, total):
    def present(blk):
        ok = None
        for _, b0, nb in blocks:
            mine = (blk >= b0) & (blk < b0 + nb)
            ok = mine if ok is None else ok | mine
        return ok

    def fetch(step):
        blk, rows = tile_of(step)
        for p_ref, (_, b0, nb) in zip(p_refs, blocks):
            for t in range(nb):
                @pl.when(blk == b0 + t)
                def _(p_ref=p_ref, t=t):
                    pltpu.make_async_copy(p_ref.at[rows, pl.ds(t * PIECE_COLS, PIECE_COLS)], buf.at[step % 2],
                                          sems.at[step % 2]).start()

    @pl.when(pos == 0)
    def _():
        fetch(pos)

    @pl.when(pos + 1 < total)
    def _():
        fetch(pos + 1)

    def landed():
        slot = pos % 2
        pltpu.make_async_copy(p_refs[0].at[pl.ds(0, buf.shape[1]), pl.ds(0, PIECE_COLS)], buf.at[slot],
                              sems.at[slot]).wait()
        return buf.at[slot]

    return present(tile_of(pos)[0]), landed


def matmul_tn_pieces(a, pieces, name, extra=None, to_chips=()):
    K, M = a.shape
    blocks = _piece_blocks(pieces)
    tk = min(1024, K)
    nk = K // tk
    grid = (N_PIECE_BLOCKS, nk)
    nx, npc = len(to_chips), len(blocks)
    a2, blocks2 = (extra[0], _piece_blocks(extra[1])) if extra is not None else (None, [])
    npc2 = len(blocks2)

    def body(a_ref, *refs):
        p_refs = refs[:npc]
        refs = refs[npc:]
        a2_ref, p2_refs = (refs[0], refs[1:1 + npc2]) if npc2 else (None, ())
        refs = refs[1 + npc2:] if npc2 else refs
        o_ref = refs[nx]
        buf, sems = refs[2 * nx + 1:2 * nx + 3]
        rest = refs[2 * nx + 3:]
        pos, total = _grid_step(grid)
        if nx:
            start, finish = _to_chips_phases(refs[:nx], refs[nx + 1:2 * nx + 1], *rest[:2])
            pl.when(pos == 0)(start)
        here, landed = _piece_feed(p_refs, blocks, buf, sems,
                                   lambda s: (s // nk, pl.ds(pl.multiple_of((s % nk) * tk, tk), tk)), pos, total)
        blk, kk = pl.program_id(0), pl.program_id(1)

        @pl.when(kk == 0)
        def _():
            o_ref[...] = jnp.zeros_like(o_ref)

        @pl.when(here)
        def _():
            o_ref[...] += _dot(a_ref[...], landed()[...], 0, 0)

        if npc2:
            buf2, sem2 = rest[-2:]

            def tile(p_ref, t):
                return pltpu.make_async_copy(p_ref.at[:, pl.ds(t * PIECE_COLS, PIECE_COLS)], buf2, sem2.at[0])

            for p_ref, (_, b0, nb) in zip(p2_refs, blocks2):
                for t in range(nb):
                    @pl.when((blk == b0 + t) & (kk == 0))
                    def _(p_ref=p_ref, t=t):
                        tile(p_ref, t).start()

                    @pl.when((blk == b0 + t) & (kk == nk - 1))
                    def _(p_ref=p_ref, t=t):
                        tile(p_ref, t).wait()
                        o_ref[...] += _dot(a2_ref[...], buf2[...], 0, 0)

        if nx:
            pl.when(pos == total - 1)(finish)

    out_spec = pl.BlockSpec((M, PIECE_COLS), lambda blk, kk: (0, blk))
    in_specs = [pl.BlockSpec((tk, M), lambda blk, kk: (kk, 0))] + [ANY] * npc
    args = [a] + [arr for arr, _, _ in blocks]
    scratch = [pltpu.VMEM((2, tk, PIECE_COLS), BF16), pltpu.SemaphoreType.DMA((2,))]
    scratch += _to_chips_scratch(nx) if nx else []
    if npc2:
        in_specs += [pl.BlockSpec(a2.shape, lambda blk, kk: (0, 0))] + [ANY] * npc2
        args += [a2] + [arr for arr, _, _ in blocks2]
        scratch += [pltpu.VMEM((a2.shape[0], PIECE_COLS), BF16), pltpu.SemaphoreType.DMA((1,))]
    out = pl.pallas_call(
        body, name=name,
        grid=grid,
        in_specs=in_specs + [ANY] * nx,
        out_specs=[out_spec] + [ANY] * nx,
        out_shape=[jax.ShapeDtypeStruct((M, D_IN), F32)] + _to_chips_shapes(to_chips),
        scratch_shapes=scratch,
        compiler_params=_params("arbitrary", "arbitrary"),
    )(*args, *to_chips)
    return out if nx else out[0]


def dhx_normbwd(pieces, w, x, dx_res, nw, sc, name, to_chips=()):
    L = x.shape[0]
    tm = min(PROJ_ROWS, L)
    blocks = _piece_blocks(pieces)
    grid = (L // tm, N_PIECE_BLOCKS)
    nx, npc = len(to_chips), len(blocks)

    def body(*refs):
        p_refs = refs[:npc]
        w_ref, x_ref, res_ref, nw_ref, sc_ref = refs[npc:npc + 5]
        refs = refs[npc + 5:]
        dx_ref, sums_ref = refs[nx:nx + 2]
        acc, buf, sems = refs[2 * nx + 2:2 * nx + 5]
        pos, total = _grid_step(grid)
        if nx:
            start, finish = _to_chips_phases(refs[:nx], refs[nx + 2:2 * nx + 2], *refs[2 * nx + 5:])
            pl.when(pos == 0)(start)
            pl.when(pos == total - 1)(finish)
        here, landed = _piece_feed(
            p_refs, blocks, buf, sems,
            lambda s: (s % N_PIECE_BLOCKS, pl.ds(pl.multiple_of((s // N_PIECE_BLOCKS) * tm, tm), tm)), pos, total)
        i, blk = pl.program_id(0), pl.program_id(1)

        @pl.when((i == 0) & (blk == 0))
        def _():
            sums_ref[...] = jnp.zeros_like(sums_ref)

        @pl.when(blk == 0)
        def _():
            acc[...] = jnp.zeros_like(acc)

        @pl.when(here)
        def _():
            acc[...] += _dot(landed()[...], w_ref[...], 1, 1)

        @pl.when(blk == N_PIECE_BLOCKS - 1)
        def _():
            dhx = acc[...]
            xv = x_ref[...]
            r = lax.rsqrt(_lanemean(xv * xv) + EPS)
            n0 = xv * r
            nw = nw_ref[...]
            dn = dhx * (1.0 + sc_ref[...])
            dn0 = dn * nw
            dx_ref[...] = res_ref[...] + r * (dn0 - n0 * _lanemean(dn0 * n0))
            sums_ref[0:1, :] += _rowsum(dhx)
            sums_ref[1:2, :] += _rowsum(dhx * n0 * nw)
            sums_ref[2:3, :] += _rowsum(dn * n0)

    row = pl.BlockSpec((tm, D), lambda i, blk: (i, 0))
    vec = pl.BlockSpec((1, D), lambda i, blk: (0, 0))
    return pl.pallas_call(
        body, name=name,
        grid=grid,
        in_specs=[ANY] * npc + [pl.BlockSpec((D, PIECE_COLS), lambda i, blk: (0, blk)), row, row, vec, vec] + [ANY] * nx,
        out_specs=[row, pl.BlockSpec((8, D), lambda i, blk: (0, 0))] + [ANY] * nx,
        out_shape=[jax.ShapeDtypeStruct((L, D), F32), jax.ShapeDtypeStruct((8, D), F32)] + _to_chips_shapes(to_chips),
        scratch_shapes=[pltpu.VMEM((tm, D), F32), pltpu.VMEM((2, tm, PIECE_COLS), BF16), pltpu.SemaphoreType.DMA((2,))]
        + (_to_chips_scratch(nx) if nx else []),
        compiler_params=_params("arbitrary", "arbitrary"),
    )(*[arr for arr, _, _ in blocks], w, x, dx_res, nw, sc, *to_chips)


SMALL_ROWS = 24


def _rope_tables(L):
    rows = L // 64
    freqs = 10000.0 ** (-jnp.arange(RT_DK // 4, dtype=F32) / (RT_DK // 4))
    a_row = jnp.arange(rows, dtype=F32)[:, None] * freqs
    a_col = jnp.arange(64, dtype=F32)[:, None] * freqs

    def spread(f):
        return jnp.concatenate([jnp.repeat(f(a_row), 64, axis=0), jnp.tile(f(a_col), (rows, 1))], axis=-1)

    cos, sin = spread(jnp.cos), spread(jnp.sin)
    return jnp.concatenate([cos, cos], axis=1), jnp.concatenate([-sin, sin], axis=1)


def _pieces(hq, hf_f, hf_b, hi, hg, rq, rk, rv, rg, ga, gb):
    widths = (D, D, D, D, D, D, D, 2 * D, 2 * D, D, D)
    return list(zip((hq, hf_f, hf_b, hi, hg, rq, rk, rv, rg, ga, gb), widths))


def _lane0(a):
    return a[:, 0, 0]


def _pack_small(rows):
    out = [r.reshape(1, D) for r in rows]
    out += [jnp.zeros((1, D), F32)] * (SMALL_ROWS - len(out))
    return jnp.concatenate(out, axis=0)


def _other_half(g, core):
    axis = g.ndim - 2
    h = g.shape[axis] // 2
    return lax.dynamic_slice_in_dim(g, (1 - core) * h, h, axis=axis).astype(BF16)


def _sibling_sums(gs, names, place):
    core, core_arg, _ = place
    received = rs_to_sibling([_other_half(g, core) for g in gs], "rs_to_sibling_" + names[0])
    return [rs_add_sibling(g, r, core_arg, "rs_add_sibling_" + k) for g, r, k in zip(gs, received, names)]


def _staged_in_proj(x, nw, sh, sc, w_shard, rest, chip):
    cx, cy = chip // 2, chip % 2

    def arg(k):
        return jnp.reshape(k, (1,)).astype(jnp.int32)

    p, hx, w_full = in_proj_own(x, nw, sh, sc, w_shard, arg(chip), "in_proj_own")
    p, w_full = in_proj_next(hx, w_full, arg(2 * (1 - cx) + cy), p, "in_proj_x", diag_from=w_shard)
    w_pa, w_pb, w_out, w_wd = rest[0], rest[1], rest[2], rest[5]
    p, g_pa, g_pb, g_out, g_wd = in_proj_next(hx, w_full, arg(2 * cx + 1 - cy), p, "in_proj_y",
                                              gather=[w_pa, w_pb, w_out, w_wd])
    p, g_wg, g_wu = in_proj_next(hx, w_full, arg(3 - chip), p, "in_proj_diag", gather=[rest[3], rest[4]])
    w = {"w_in": w_full, "w_pa": g_pa.reshape(D, D), "w_pb": g_pb.reshape(2 * D, D), "w_out": g_out.reshape(D, D),
         "wg": g_wg, "wu": g_wu, "wd": g_wd}
    return p, hx, w


def local_step(x, ctx, target, mod_x, mod_c, lb_f, lb_b, lg_f, lg_b, nw1, nw2, hgw, fw, w, rest=None, place=None):
    L, Lc = x.shape[0], ctx.shape[0]
    sh1, sc1, g1, sh2, sc2, g2 = (mod_x[i:i + 1] for i in range(6))
    sh1c, sc1c = mod_c[0:1], mod_c[1:2]
    cosf, sinf = _rope_tables(L)
    cosc, sinc = jnp.ones((Lc, RT_DK), F32), jnp.zeros((Lc, RT_DK), F32)
    zero_h = jnp.zeros((HEADS, HG_D, HG_D), F32)
    zero_r = jnp.zeros((HEADS, RT_DV, RT_DK), F32)

    if rest is None:
        p, hx = normmod_matmul(x, nw1, sh1, sc1, w["w_in"], "in_proj")
    else:
        p, hx, w = _staged_in_proj(x, nw1, sh1, sc1, w["w_in_shard"], rest, place[2][0])
    pc, hxc = normmod_matmul(ctx, nw1, sh1c, sc1c, w["w_in"], "ctx_in_proj")
    _, s_hf, cb_hf = hgrn_scan_fwd(pc, lb_f, zero_h, COL_HFF, False, "ctx_hgrn_f")
    _, s_hb, cb_hb = hgrn_scan_fwd(pc, lb_b, zero_h, COL_HFB, True, "ctx_hgrn_b")
    _, s_rf, cb_rf = ret_scan_fwd(pc, cosc, sinc, lg_f, zero_r, False, "ctx_ret_f")
    _, s_rb, cb_rb = ret_scan_fwd(pc, cosc, sinc, lg_b, zero_r, True, "ctx_ret_b")
    ohf, _, xb_hf = hgrn_scan_fwd(p, lb_f, s_hf, COL_HFF, False, "hgrn_f")
    o_hg, _, xb_hb = hgrn_scan_fwd(p, lb_b, s_hb, COL_HFB, True, "hgrn_b", prev=ohf)
    orf, _, xb_rf = ret_scan_fwd(p, cosf, sinf, lg_f, s_rf, False, "ret_f")
    o_rt, _, xb_rb = ret_scan_fwd(p, cosf, sinf, lg_b, s_rb, True, "ret_b", prev=orf)
    x1, x_mix, merged, ya, yb = mix_fwd(o_hg, o_rt, p, x, g1, hgw, w["w_pa"], w["w_pb"], w["w_out"], "mix_fwd")
    hx2, gg, uu, hh, ff, dx2, sums_f = ffn_fwd(x1, target, nw2, sh2, sc2, g2, fw, w["wg"], w["wu"], w["wd"], "ffn_fwd")

    d_f, d_g, d_u, dx1, sums_fb = ffn_bwd(dx2, x1, ff, gg, uu, nw2, sc2, g2, w["wg"], w["wu"], w["wd"], "ffn_bwd")
    dw_gate, dw_up = matmul_tn_pair(hx2, d_g, d_u, "dw_ffn_gate_up")
    grads = {"wg": dw_gate, "wu": dw_up, "wd": matmul_tn(hh, d_f[None], "dw_ffn_down")}
    ffn_names = ["wg", "wu", "wd"]
    swap = () if place is None else [_other_half(grads[k], place[0]) for k in ffn_names]
    dxm, d_a, d_b, dga, dgb, dhg, drg, dohg, dort, sums_m, *from_sibling = mix_bwd(
        dx1, x_mix, ya, yb, o_hg, o_rt, p, g1, hgw, w["w_pa"], w["w_pb"], w["w_out"], "mix_bwd", to_sibling=swap)
    grads["w_out"] = matmul_tn(merged[None], dxm[None], "dw_out").reshape(N_SHARD, D // N_SHARD, D)
    grads["w_pa"] = matmul_tn(ya[None], d_a[None], "dw_proj_hgrn").reshape(N_SHARD, D // N_SHARD, D)
    grads["w_pb"] = matmul_tn(yb[None], d_b[None], "dw_proj_ret").reshape(N_SHARD, 2 * D // N_SHARD, D)

    rq1, rk1, rv1, dlgf_x, ds_rf = ret_scan_bwd(p, cosf, sinf, lg_f, xb_rf, dort, zero_r, None, False, "ret_f_bwd")
    drq, drk, drv, dlgb_x, ds_rb = ret_scan_bwd(p, cosf, sinf, lg_b, xb_rb, dort, zero_r, (rq1, rk1, rv1), True, "ret_b_bwd")
    hq1, dzf, hv1, dlbf_x, ds_hf = hgrn_scan_bwd(p, lb_f, xb_hf, dohg, zero_h, None, COL_HFF, False, "hgrn_f_bwd")
    dhq, dzb, dhv, dlbb_x, ds_hb = hgrn_scan_bwd(p, lb_b, xb_hb, dohg, zero_h, (hq1, hv1), COL_HFB, True, "hgrn_b_bwd")
    dp = _pieces(dhq, dzf, dzb, dhv, dhg, drq, drk, drv, drg, dga, dgb)

    zc = jnp.zeros((Lc, D), F32)
    zc2 = jnp.zeros((Lc, 2 * D), F32)
    crq1, crk1, crv1, dlgf_c, _ = ret_scan_bwd(pc, cosc, sinc, lg_f, cb_rf, zc2, ds_rf, None, False, "ctx_ret_f_bwd")
    cdrq, cdrk, cdrv, dlgb_c, _ = ret_scan_bwd(pc, cosc, sinc, lg_b, cb_rb, zc2, ds_rb, (crq1, crk1, crv1), True, "ctx_ret_b_bwd")
    chq1, cdzf, chv1, dlbf_c, _ = hgrn_scan_bwd(pc, lb_f, cb_hf, zc, ds_hf, None, COL_HFF, False, "ctx_hgrn_f_bwd")
    cdhq, cdzb, cdhv, dlbb_c, _ = hgrn_scan_bwd(pc, lb_b, cb_hb, zc, ds_hb, (chq1, chv1), COL_HFB, True, "ctx_hgrn_b_bwd")
    dpc = _pieces(cdhq, cdzf, cdzb, cdhv, None, cdrq, cdrk, cdrv, None, None, None)
    _, sums_c = dhx_normbwd(dpc, w["w_in"], ctx, zc, nw1, sc1c, "dctx_in_proj")

    others = ["w_pa", "w_pb", "w_out", "wg", "wu", "wd"]
    if place is None:
        grads["w_in"] = matmul_tn_pieces(hx, dp, "dw_in", extra=(hxc, dpc))
        dx, sums_x = dhx_normbwd(dp, w["w_in"], x, dx1, nw1, sc1, "dx_in_proj")
    else:
        sums_o = _sibling_sums([grads[k] for k in others[:3]], others[:3], place)
        sums_o += [rs_add_sibling(grads[k], r, place[1], "rs_add_sibling_" + k) for k, r in zip(ffn_names, from_sibling)]
        grads["w_in"], *recv_o = matmul_tn_pieces(hx, dp, "dw_in", extra=(hxc, dpc),
                                                  to_chips=[a16 for _, a16 in sums_o])
        sums_i = _sibling_sums([grads["w_in"]], ["w_in"], place)
        dx, sums_x, recv_i = dhx_normbwd(dp, w["w_in"], x, dx1, nw1, sc1, "dx_in_proj", to_chips=[sums_i[0][1]])
        names = ["w_in"] + others
        halves = [rs_add_chips(a, r, place[2], "rs_add_chips_" + k)
                  for (a, _), r, k in zip(sums_i + sums_o, [recv_i] + recv_o, names)]
        grads = dict(zip(names, rs_join_halves(halves, "rs_join_halves")))

    def lg_row(f, b):
        return jnp.concatenate([_lane0(f), _lane0(b), jnp.zeros((D - 2 * HEADS,), F32)])

    small = _pack_small([
        sums_x[0], sums_x[1], sums_m[0], sums_fb[1], sums_fb[2], sums_fb[0],
        sums_c[0], sums_c[1],
        sums_x[2], sums_c[2], sums_fb[3], sums_m[1], sums_f[0],
        dlbf_x, dlbf_c, dlbb_x, dlbb_c,
        lg_row(dlgf_x, dlgb_x), lg_row(dlgf_c, dlgb_c),
        sums_f[1],
    ])
    return dx, grads, small


MESH = pl.DeviceIdType.MESH
ANY = pl.BlockSpec(memory_space=pl.ANY)
N_DEV = 8


def _place():
    return lax.axis_index("x"), lax.axis_index("y"), lax.axis_index("c")


def _other_chips(x, y):
    return [(1 - x, y), (x, 1 - y), (1 - x, 1 - y)]


def allgather8(xs, name):
    m, n = xs.shape

    def body(x_ref, out_ref, send_sems, recv_sems, local_sem):
        x, y, c = _place()
        me, sibling = (x, y, c), (x, y, 1 - c)
        chips = _other_chips(x, y)

        def rows(px, py, pc):
            return out_ref.at[pl.ds((4 * px + 2 * py + pc) * m, m), :]

        def copy(k, block, to, src=None):
            return pltpu.make_async_remote_copy(
                src_ref=rows(*block) if src is None else src, dst_ref=rows(*block),
                send_sem=send_sems.at[k], recv_sem=recv_sems.at[k], device_id=to, device_id_type=MESH)

        mine = pltpu.make_async_copy(x_ref, rows(*me), local_sem)
        mine.start()
        first = [copy(0, me, sibling, src=x_ref)]
        first += [copy(1 + j, me, (*chip, c), src=x_ref) for j, chip in enumerate(chips)]
        for cp in first:
            cp.start()
        passed = [copy(4 + j, (*chip, c), sibling) for j, chip in enumerate(chips)]
        for j, chip in enumerate(chips):
            copy(1 + j, (*chip, c), me).wait_recv()
            passed[j].start()
        copy(0, sibling, me).wait_recv()
        for j, chip in enumerate(chips):
            copy(4 + j, (*chip, 1 - c), me).wait_recv()
        for cp in first + passed:
            cp.wait_send()
        mine.wait()

    return pl.pallas_call(
        body, name=name,
        out_shape=jax.ShapeDtypeStruct((N_DEV * m, n), xs.dtype),
        in_specs=[pl.BlockSpec(memory_space=pltpu.VMEM)],
        out_specs=pl.BlockSpec(memory_space=pltpu.VMEM),
        scratch_shapes=[pltpu.SemaphoreType.DMA((7,)), pltpu.SemaphoreType.DMA((7,)), pltpu.SemaphoreType.DMA],
    )(xs)


def _gather_phases(ins, outs, send_sems, recv_sems, local_sems, relations=(0, 1, 2), stage=None):
    n = len(ins)
    x, y, c = _place()
    chips = _other_chips(x, y)

    def rows(i, core):
        h = ins[i].shape[0] // 2
        return pl.ds(pl.multiple_of(core * h, 16), h)

    def region(i, k, rs):
        if len(outs[i].shape) == 2:
            cols = ins[i].shape[1]
            return outs[i].at[rs, pl.ds(pl.multiple_of(k * cols, 128), cols)]
        return outs[i].at[k, rs, :]

    def landed(i, chip, core):
        return region(i, 2 * chip[0] + chip[1], rows(i, core))

    def copy(i, k, src, dst, to):
        return pltpu.make_async_remote_copy(src_ref=src, dst_ref=dst, send_sem=send_sems.at[6 * i + k],
                                            recv_sem=recv_sems.at[6 * i + k], device_id=to, device_id_type=MESH)

    def lift(i):
        return pltpu.make_async_copy(ins[i], stage[i], local_sems.at[i])

    def drop(i):
        return pltpu.make_async_copy(stage[i], region(i, 2 * x + y, pl.ds(0, ins[i].shape[0])), local_sems.at[i])

    def send(i, j):
        return copy(i, j, ins[i].at[rows(i, c), :], landed(i, (x, y), c), (*chips[j], c))

    def arrived(i, j, core, k):
        return copy(i, k, ins[i].at[rows(i, core), :], landed(i, chips[j], core), (x, y, 1 - c))

    def passed(i, j):
        return copy(i, 3 + j, landed(i, chips[j], c), landed(i, chips[j], c), (x, y, 1 - c))

    def start():
        for i in range(n):
            if stage is not None:
                lift(i).start()
            for j in relations:
                send(i, j).start()

    def forward():
        for i in range(n):
            if stage is not None:
                lift(i).wait()
                drop(i).start()
            for j in relations:
                arrived(i, j, c, j).wait_recv()
                passed(i, j).start()

    def finish():
        for i in range(n):
            for j in relations:
                arrived(i, j, 1 - c, 3 + j).wait_recv()
        for i in range(n):
            for j in relations:
                send(i, j).wait_send()
                passed(i, j).wait_send()
            if stage is not None:
                drop(i).wait()

    return start, forward, finish


def _gather_scratch(n):
    return [pltpu.SemaphoreType.DMA((6 * n,)), pltpu.SemaphoreType.DMA((6 * n,)), pltpu.SemaphoreType.DMA((n,))]


def rs_to_sibling(payloads, name):
    n = len(payloads)

    def body(*refs):
        start, finish = _to_sibling_phases(refs[:n], refs[n:2 * n], *refs[2 * n:])
        start()
        finish()

    return pl.pallas_call(
        body, name=name,
        out_shape=[jax.ShapeDtypeStruct(g.shape, g.dtype) for g in payloads],
        in_specs=[ANY] * n, out_specs=[ANY] * n,
        scratch_shapes=_to_sibling_scratch(n),
    )(*payloads)


def _to_sibling_phases(ins, outs, send_sems, recv_sems):
    def copies():
        x, y, c = _place()
        return [pltpu.make_async_remote_copy(src_ref=ins[i], dst_ref=outs[i], send_sem=send_sems.at[i],
                                             recv_sem=recv_sems.at[i], device_id=(x, y, 1 - c), device_id_type=MESH)
                for i in range(len(ins))]

    def start():
        for cp in copies():
            cp.start()

    def finish():
        for cp in copies():
            cp.wait()

    return start, finish


def _to_sibling_scratch(n):
    return [pltpu.SemaphoreType.DMA((n,)), pltpu.SemaphoreType.DMA((n,))]


def _to_chips_phases(ins, outs, send_sems, recv_sems):
    def copies():
        x, y, c = _place()
        return [pltpu.make_async_remote_copy(
            src_ref=ins[i].at[2 * px + py], dst_ref=outs[i].at[j], send_sem=send_sems.at[3 * i + j],
            recv_sem=recv_sems.at[3 * i + j], device_id=(px, py, c), device_id_type=MESH)
            for i in range(len(ins)) for j, (px, py) in enumerate(_other_chips(x, y))]

    def start():
        for cp in copies():
            cp.start()

    def finish():
        for cp in copies():
            cp.wait()

    return start, finish


def _to_chips_shapes(parts):
    return [jax.ShapeDtypeStruct((3,) + a.shape[1:], a.dtype) for a in parts]


def _to_chips_scratch(n):
    return [pltpu.SemaphoreType.DMA((3 * n,)), pltpu.SemaphoreType.DMA((3 * n,))]


def rs_join_halves(fulls, name):
    n = len(fulls)

    def body(*refs):
        outs = refs[n:2 * n]
        send_sems, recv_sems = refs[2 * n:]
        x, y, c = _place()

        def copy(i, core):
            h = fulls[i].shape[0] // 2
            rows = outs[i].at[pl.ds(pl.multiple_of(core * h, 8), h), :]
            return pltpu.make_async_remote_copy(src_ref=rows, dst_ref=rows, send_sem=send_sems.at[i],
                                                recv_sem=recv_sems.at[i], device_id=(x, y, 1 - c), device_id_type=MESH)

        sent = [copy(i, c) for i in range(n)]
        for cp in sent:
            cp.start()
        for i in range(n):
            copy(i, 1 - c).wait_recv()
        for cp in sent:
            cp.wait_send()

    return pl.pallas_call(
        body, name=name,
        out_shape=[jax.ShapeDtypeStruct(a.shape, a.dtype) for a in fulls],
        in_specs=[ANY] * n, out_specs=[ANY] * n,
        input_output_aliases={i: i for i in range(n)},
        scratch_shapes=[pltpu.SemaphoreType.DMA((n,)), pltpu.SemaphoreType.DMA((n,))],
    )(*fulls)


def _row_tile(rows, cols, limit_bytes=2 * 1024 * 1024, mult=8):
    best = mult
    for t in range(mult, rows + 1, mult):
        if rows % t == 0 and t * cols * 4 <= limit_bytes:
            best = t
    return best


def rs_add_sibling(g, recv, c, name):
    if g.ndim == 2:
        h, C = recv.shape[0], recv.shape[1] // N_SHARD
    else:
        _, h, C = recv.shape
    tr = _row_tile(h, C, mult=16)
    nt = h // tr

    def body(c_ref, g_ref, r_ref, o_ref, o16_ref):
        s = g_ref[...] + r_ref[...].astype(F32)
        o_ref[...] = s
        o16_ref[...] = s.astype(BF16)

    blk = pl.BlockSpec((None, tr, C), lambda k, i, c_ref: (k, i, 0))
    if g.ndim == 2:
        g_spec = pl.BlockSpec((tr, C), lambda k, i, c_ref: (c_ref[0] * nt + i, k))
        r_spec = pl.BlockSpec((tr, C), lambda k, i, c_ref: (i, k))
    else:
        g_spec = pl.BlockSpec((None, tr, C), lambda k, i, c_ref: (k, c_ref[0] * nt + i, 0))
        r_spec = blk
    return pl.pallas_call(
        body, name=name,
        grid_spec=pltpu.PrefetchScalarGridSpec(
            num_scalar_prefetch=1, grid=(N_SHARD, nt),
            in_specs=[g_spec, r_spec],
            out_specs=[blk, blk]),
        out_shape=[jax.ShapeDtypeStruct((N_SHARD, h, C), F32), jax.ShapeDtypeStruct((N_SHARD, h, C), BF16)],
        compiler_params=_params("parallel", "parallel"),
    )(c, g, recv)


def rs_add_chips(part, recv, place, name):
    _, h, C = part.shape
    tr = _row_tile(h, C, mult=16)
    nt = h // tr

    def body(k_ref, p_ref, r_ref, o_ref):
        o_ref[...] = ((p_ref[...] + r_ref[0].astype(F32)) + r_ref[1].astype(F32)) + r_ref[2].astype(F32)

    return pl.pallas_call(
        body, name=name,
        grid_spec=pltpu.PrefetchScalarGridSpec(
            num_scalar_prefetch=1, grid=(nt,),
            in_specs=[pl.BlockSpec((None, tr, C), lambda i, k_ref: (k_ref[0], i, 0)),
                      pl.BlockSpec((3, tr, C), lambda i, k_ref: (0, i, 0))],
            out_specs=pl.BlockSpec((tr, C), lambda i, k_ref: (k_ref[1] * nt + i, 0))),
        out_shape=jax.ShapeDtypeStruct((2 * h, C), F32),
        compiler_params=_params("parallel"),
    )(place, part, recv)


def _adamw_math(w, g, m, v):
    m = ADAM_B1 * m + (1.0 - ADAM_B1) * g
    v = ADAM_B2 * v + (1.0 - ADAM_B2) * (g * g)
    m_hat = m / (1.0 - ADAM_B1 ** ADAM_STEP)
    v_hat = v / (1.0 - ADAM_B2 ** ADAM_STEP)
    delta = -ADAM_LR * (m_hat / (jnp.sqrt(v_hat) + ADAM_EPS) + ADAM_WD * w)
    return delta, m, v


def adamw(w, g, m, v, name):
    R, C = w.shape
    tr = _row_tile(R, C, 1024 * 1024)

    def body(w_ref, g_ref, m_ref, v_ref, go_ref, d_ref, nm_ref, nv_ref):
        g = g_ref[...]
        go_ref[...] = g
        d_ref[...], nm_ref[...], nv_ref[...] = _adamw_math(w_ref[...], g, m_ref[...], v_ref[...])

    blk = pl.BlockSpec((tr, C), lambda i: (i, 0))
    return pl.pallas_call(
        body, name=name, grid=(R // tr,), in_specs=[blk] * 4, out_specs=[blk] * 4,
        out_shape=[jax.ShapeDtypeStruct((R, C), F32)] * 4,
        compiler_params=_params("parallel"),
    )(w, g, m, v)


MOD_SH = 6 * D // N_SHARD
PK_ROWS = 16


def mod_fwd(call16, w_sh, b_sh, name):
    def body(c_ref, w_ref, b_ref, o_ref):
        o_ref[...] = _dot(_silu_parts(c_ref[...])[0], w_ref[...], prec=HI) + b_ref[...]

    return pl.pallas_call(body, name=name, out_shape=jax.ShapeDtypeStruct((16, MOD_SH), F32),
                          compiler_params=_params())(call16, w_sh, b_sh)


def prep_small(lbf2, lbb2, theta_row, name):
    def body(f_ref, b_ref, t_ref, lbf_ref, lbb_ref, lg_ref):
        lbf_ref[...] = _sigmoid(f_ref[0:1, :] - f_ref[1:2, :])
        lbb_ref[...] = _sigmoid(b_ref[0:1, :] - b_ref[1:2, :])
        t = t_ref[...]
        lg_ref[...] = jnp.minimum(t, 0.0) - jnp.log(1.0 + jnp.exp(-jnp.abs(t)))

    row = jax.ShapeDtypeStruct((1, D), F32)
    return pl.pallas_call(body, name=name, out_shape=[row, row, row], compiler_params=_params())(lbf2, lbb2, theta_row)


def small_grads(g3, lbf, lbb, theta_row, name):
    def body(g_ref, lbf_ref, lbb_ref, t_ref, pk_ref, aux_ref):
        s = g_ref[0]
        for d in range(1, N_DEV):
            s = s + g_ref[d]
        pk_ref[...] = jnp.zeros_like(pk_ref)
        aux_ref[...] = jnp.zeros_like(aux_ref)
        pk_ref[1:7, :] = s[0:6]
        pk_ref[1:3, :] += s[6:8]
        pk_ref[7:8, :] = s[8:9] + s[9:10]
        pk_ref[8:9, :] = s[10:11]
        lbf, lbb = lbf_ref[...], lbb_ref[...]
        daf = (s[13:14] + s[14:15]) * lbf * (1.0 - lbf)
        dab = (s[15:16] + s[16:17]) * lbb * (1.0 - lbb)
        pk_ref[9:10, :] = daf
        pk_ref[10:11, :] = -daf
        pk_ref[11:12, :] = dab
        pk_ref[12:13, :] = -dab
        pk_ref[13:14, :] = s[11:12]
        pk_ref[14:15, :] = (s[17:18] + s[18:19]) * _sigmoid(-t_ref[...])
        pk_ref[15:16, :] = s[12:13]
        aux_ref[0:2, :] = s[6:8]
        aux_ref[2:3, :] = jnp.broadcast_to(jnp.sum(s[19:20], axis=-1, keepdims=True), (1, D))

    return pl.pallas_call(body, name=name,
                          out_shape=[jax.ShapeDtypeStruct((PK_ROWS, D), F32), jax.ShapeDtypeStruct((8, D), F32)],
                          compiler_params=_params())(g3, lbf, lbb, theta_row)


def mod_bwd(call16, dmod_sh, w_sh, name):
    def body(c_ref, d_ref, w_ref, dw_ref, ds_ref):
        dm = d_ref[...]
        dw_ref[...] = _dot(_silu_parts(c_ref[...])[0], dm, 0, 0, prec=HI)
        ds_ref[...] = jnp.zeros_like(ds_ref)
        ds_ref[0:1, :] = _dot(dm[8:9, :], w_ref[...], 1, 1, prec=HI)

    return pl.pallas_call(body, name=name,
                          out_shape=[jax.ShapeDtypeStruct((D, MOD_SH), F32), jax.ShapeDtypeStruct((8, D), F32)],
                          compiler_params=_params())(call16, dmod_sh, w_sh)


def adamw_small(g4, pk_g, pk_w, pk_m, pk_v, name):
    def body(g4_ref, g_ref, w_ref, m_ref, v_ref, go_ref, d_ref, nm_ref, nv_ref):
        w = w_ref[...]
        ds = ((g4_ref[0:1, :] + g4_ref[16:17, :]) + g4_ref[32:33, :]) + g4_ref[48:49, :]
        row = lax.broadcasted_iota(jnp.int32, (PK_ROWS, D), 0)
        g = jnp.where(row == 0, ds * _silu_parts(w[0:1, :])[1], g_ref[...])
        go_ref[...] = g
        d_ref[...], nm_ref[...], nv_ref[...] = _adamw_math(w, g, m_ref[...], v_ref[...])

    pk = jax.ShapeDtypeStruct((PK_ROWS, D), F32)
    return pl.pallas_call(body, name=name, out_shape=[pk, pk, pk, pk], compiler_params=_params())(g4, pk_g, pk_w, pk_m, pk_v)


def _pack_params(c_ctx, b_mod, n1, n2, lbf, lbb, hgn, th_f, th_b, fin):
    theta = jnp.concatenate([th_f.reshape(HEADS), th_b.reshape(HEADS), jnp.zeros((D - 2 * HEADS,), F32)])
    return jnp.concatenate([c_ctx.reshape(1, D), b_mod.reshape(6, D), n1.reshape(1, D), n2.reshape(1, D), lbf, lbb,
                            hgn.reshape(1, D), theta.reshape(1, D), fin.reshape(1, D)], axis=0)


def _unpack_params(pk):
    return (pk[0], pk[1:7].reshape(1, 6 * D), pk[7:8], pk[8:9], pk[9:11], pk[11:13], pk[13:14],
            pk[14, 0:HEADS].reshape(1, HEADS), pk[14, HEADS:2 * HEADS].reshape(1, HEADS), pk[15])


def kernel(x, c, ctx, c_ctx, w_mod, b_mod, norm1_w, norm2_w, w_in, hg_lb_fwd, hg_lb_bwd, hg_norm_w, rt_theta_fwd, rt_theta_bwd, w_proj_hgrn, w_proj_ret, w_out, w_ffn_gate, w_ffn_up, w_ffn_down, final_norm_w, loss_target, m_c_ctx, m_w_mod, m_b_mod, m_norm1_w, m_norm2_w, m_w_in, m_hg_lb_fwd, m_hg_lb_bwd, m_hg_norm_w, m_rt_theta_fwd, m_rt_theta_bwd, m_w_proj_hgrn, m_w_proj_ret, m_w_out, m_w_ffn_gate, m_w_ffn_up, m_w_ffn_down, m_final_norm_w, v_c_ctx, v_w_mod, v_b_mod, v_norm1_w, v_norm2_w, v_w_in, v_hg_lb_fwd, v_hg_lb_bwd, v_hg_norm_w, v_rt_theta_fwd, v_rt_theta_bwd, v_w_proj_hgrn, v_w_proj_ret, v_w_out, v_w_ffn_gate, v_w_ffn_up, v_w_ffn_down, v_final_norm_w):
    xi, yi, ci = _place()
    dev = 4 * xi + 2 * yi + ci
    chip = 2 * xi + yi
    core_arg = jnp.reshape(ci, (1,)).astype(jnp.int32)
    place_arg = jnp.stack([chip, ci]).astype(jnp.int32)

    c_all = allgather8(jnp.concatenate([c, jnp.zeros((7, D), F32)], axis=0), "gather_c").reshape(N_DEV, 8, D)[:, 0]
    call16 = jnp.concatenate([c_all, c_ctx.reshape(1, D), jnp.zeros((7, D), F32)], axis=0)
    b_sh = lax.dynamic_slice_in_dim(b_mod, chip * MOD_SH, MOD_SH, axis=1)
    mod_sh = mod_fwd(call16, w_mod[0], b_sh, "mod_fwd")
    mod_g = allgather8(mod_sh, "gather_mod").reshape(N_DEV, 16, MOD_SH)
    mod_all = jnp.concatenate([mod_g[0], mod_g[2], mod_g[4], mod_g[6]], axis=1)
    mod_x = lax.dynamic_index_in_dim(mod_all, dev, axis=0, keepdims=False).reshape(6, D)
    mod_c = mod_all[8].reshape(6, D)

    pk_w = _pack_params(c_ctx, b_mod, norm1_w, norm2_w, hg_lb_fwd, hg_lb_bwd, hg_norm_w, rt_theta_fwd, rt_theta_bwd, final_norm_w)
    theta_row = pk_w[14:15]
    lb_f, lb_b, lg_row = prep_small(hg_lb_fwd, hg_lb_bwd, theta_row, "prep_small")
    lg_f = jnp.broadcast_to(lg_row[0, 0:HEADS].reshape(HEADS, 1, 1), (HEADS, 1, RT_DV))
    lg_b = jnp.broadcast_to(lg_row[0, HEADS:2 * HEADS].reshape(HEADS, 1, 1), (HEADS, 1, RT_DV))

    rest = [s[0].astype(BF16) for s in (w_proj_hgrn, w_proj_ret, w_out, w_ffn_gate, w_ffn_up, w_ffn_down)]

    dx, full, small = local_step(x[0], ctx[0], loss_target[0], mod_x, mod_c, lb_f, lb_b, lg_f, lg_b,
                                 norm1_w, norm2_w, hg_norm_w, final_norm_w.reshape(1, D),
                                 {"w_in_shard": w_in[0].astype(BF16)}, rest, (ci, core_arg, place_arg))

    g3 = allgather8(small, "gather_small").reshape(N_DEV, SMALL_ROWS, D)
    pk_g, aux = small_grads(g3, lb_f, lb_b, theta_row, "small_grads")
    loss = aux[2, 0]
    dmod16 = jnp.concatenate([
        g3[:, 0:6, :].reshape(N_DEV, 6 * D),
        jnp.concatenate([aux[0], aux[1], jnp.zeros((4 * D,), F32)]).reshape(1, 6 * D),
        jnp.zeros((7, 6 * D), F32)], axis=0)
    dmod_sh = lax.dynamic_slice_in_dim(dmod16, chip * MOD_SH, MOD_SH, axis=1)
    g_wmod, dsilu = mod_bwd(call16, dmod_sh, w_mod[0], "mod_bwd")
    g4 = allgather8(dsilu, "gather_dsilu")
    pk_m = _pack_params(m_c_ctx, m_b_mod, m_norm1_w, m_norm2_w, m_hg_lb_fwd, m_hg_lb_bwd, m_hg_norm_w, m_rt_theta_fwd, m_rt_theta_bwd, m_final_norm_w)
    pk_v = _pack_params(v_c_ctx, v_b_mod, v_norm1_w, v_norm2_w, v_hg_lb_fwd, v_hg_lb_bwd, v_hg_norm_w, v_rt_theta_fwd, v_rt_theta_bwd, v_final_norm_w)
    pk_g, pk_d, pk_nm, pk_nv = adamw_small(g4, pk_g, pk_w, pk_m, pk_v, "adamw_small")

    big = {
        "w_mod": (g_wmod, w_mod, m_w_mod, v_w_mod),
        "w_in": (full["w_in"], w_in, m_w_in, v_w_in),
        "w_pa": (full["w_pa"], w_proj_hgrn, m_w_proj_hgrn, v_w_proj_hgrn),
        "w_pb": (full["w_pb"], w_proj_ret, m_w_proj_ret, v_w_proj_ret),
        "w_out": (full["w_out"], w_out, m_w_out, v_w_out),
        "wg": (full["wg"], w_ffn_gate, m_w_ffn_gate, v_w_ffn_gate),
        "wu": (full["wu"], w_ffn_up, m_w_ffn_up, v_w_ffn_up),
        "wd": (full["wd"], w_ffn_down, m_w_ffn_down, v_w_ffn_down),
    }
    res = {}
    for k, (g, wt, mt, vt) in big.items():
        res[k] = tuple(a[None] for a in adamw(wt[0], g, mt[0], vt[0], "adamw_" + k))

    sm = [_unpack_params(p) for p in (pk_g, pk_d, pk_nm, pk_nv)]
    outs = []
    for t in range(4):
        (s_cctx, s_bmod, s_n1, s_n2, s_lbf, s_lbb, s_hgn, s_thf, s_thb, s_fin) = sm[t]
        outs.append([s_cctx, res["w_mod"][t], s_bmod, s_n1, s_n2, res["w_in"][t], s_lbf, s_lbb, s_hgn, s_thf, s_thb,
                     res["w_pa"][t], res["w_pb"][t], res["w_out"][t], res["wg"][t], res["wu"][t], res["wd"][t], s_fin])
    return (loss, dx[None], *outs[0], *outs[1], *outs[2], *outs[3])
```

```python
import functools

import jax
import jax.numpy as jnp
from jax import lax
from jax.experimental import pallas as pl
from jax.experimental.pallas import tpu as pltpu

F32 = jnp.float32
BF16 = jnp.bfloat16
HI = lax.Precision.HIGHEST
CUMSUM_PRECISION = lax.Precision.HIGH

D = 1024
HEADS = 8
HG_D = 128
RT_DK = 128
RT_DV = 256
D_FF = 2816
D_IN = 13312
N_SHARD = 4
IN_SH = D_IN // N_SHARD
FF_SH = D_FF // N_SHARD
HG_CHUNK = 32
SCAN_ROWS = 256
HG_GROUP = 8
RT_GROUP = 4
PROJ_ROWS = 1024
EPS = 1e-6
GN_EPS = 1e-5
Q_SCALE = 128.0 ** -0.5
VMEM_LIMIT = 56 * 1024 * 1024

COL_HQ, COL_HFF, COL_HFB, COL_HI, COL_HG = 0, 8, 16, 24, 32
COL_RQ, COL_RK, COL_RV, COL_RG, COL_GA, COL_GB = 40, 48, 56, 72, 88, 96

ADAM_LR, ADAM_B1, ADAM_B2, ADAM_EPS, ADAM_WD, ADAM_STEP = 0.001, 0.9, 0.999, 1e-08, 0.01, 10


def _params(*sem):
    return pltpu.CompilerParams(dimension_semantics=sem, vmem_limit_bytes=VMEM_LIMIT)


def _dot(a, b, ca=1, cb=0, prec=None):
    return lax.dot_general(a, b, (((ca,), (cb,)), ((), ())), precision=prec, preferred_element_type=F32)


def _bdot(a, b, ca=1, cb=0):
    return _dot(a.astype(BF16), b.astype(BF16), ca, cb)


def _sigmoid(z):
    return 1.0 / (1.0 + jnp.exp(-z))


def _rowsum(a):
    return jnp.sum(a, axis=0, keepdims=True)


def _lanemean(a):
    return jnp.mean(a, axis=-1, keepdims=True)


def _grid_step(grid):
    pos, total = 0, 1
    for d, size in enumerate(grid):
        pos = pos * size + pl.program_id(d)
        total *= size
    return pos, total


def normmod_matmul(x, nw, sh, sc, w, name):
    L = x.shape[0]
    tm = min(PROJ_ROWS, L)
    tn = IN_SH // 2

    def body(x_ref, nw_ref, sh_ref, sc_ref, w_ref, p_ref, hx_ref, hx_scr):
        @pl.when(pl.program_id(1) == 0)
        def _():
            xv = x_ref[...]
            n = xv * lax.rsqrt(_lanemean(xv * xv) + EPS) * nw_ref[...]
            h = (n * (1.0 + sc_ref[...]) + sh_ref[...]).astype(BF16)
            hx_scr[...] = h
            hx_ref[...] = h

        p_ref[...] = _dot(hx_scr[...], w_ref[...])

    vec = pl.BlockSpec((1, D), lambda i, j: (0, 0))
    return pl.pallas_call(
        body, name=name,
        grid=(L // tm, D_IN // tn),
        in_specs=[pl.BlockSpec((tm, D), lambda i, j: (i, 0)), vec, vec, vec,
                  pl.BlockSpec((D, tn), lambda i, j: (0, j))],
        out_specs=[pl.BlockSpec((tm, tn), lambda i, j: (i, j)), pl.BlockSpec((tm, D), lambda i, j: (i, 0))],
        out_shape=[jax.ShapeDtypeStruct((L, D_IN), F32), jax.ShapeDtypeStruct((L, D), BF16)],
        scratch_shapes=[pltpu.VMEM((tm, D), BF16)],
        compiler_params=_params("parallel", "arbitrary"),
    )(x, nw, sh, sc, w)


def _w_halves(w_src, col0, tn, wbuf, wsems, pos):
    @pl.when(pos == 0)
    def _():
        for h in range(2):
            pltpu.make_async_copy(w_src.at[:, pl.ds(pl.multiple_of(col0 + h * tn, 128), tn)], wbuf.at[h],
                                  wsems.at[h]).start()

    for h in range(2):
        @pl.when(pos == h)
        def _(h=h):
            pltpu.make_async_copy(w_src.at[:, pl.ds(0, tn)], wbuf.at[h], wsems.at[h]).wait()


def in_proj_own(x, nw, sh, sc, w_shard, shard_arg, name):
    L = x.shape[0]
    tm = min(PROJ_ROWS, L)
    tn = IN_SH // 2
    grid = (L // tm, 2)

    def body(k_ref, x_ref, nw_ref, sh_ref, sc_ref, w_ref, p_ref, hx_ref, wfull_ref, hx_scr, wbuf, wsems, psems,
             *sems):
        pos, total = _grid_step(grid)
        start, forward, finish = _gather_phases([w_ref], [wfull_ref], *sems, relations=(0, 1))
        pl.when(pos == 0)(start)
        _w_halves(w_ref, 0, tn, wbuf, wsems, pos)

        def place(h):
            col = pl.multiple_of(k_ref[0] * IN_SH + h * tn, 128)
            return pltpu.make_async_copy(wbuf.at[h], wfull_ref.at[:, pl.ds(col, tn)], psems.at[h])

        for h in range(2):
            @pl.when(pos == h)
            def _(h=h):
                place(h).start()

        @pl.when(pl.program_id(1) == 0)
        def _():
            xv = x_ref[...]
            n = xv * lax.rsqrt(_lanemean(xv * xv) + EPS) * nw_ref[...]
            h = (n * (1.0 + sc_ref[...]) + sh_ref[...]).astype(BF16)
            hx_scr[...] = h
            hx_ref[...] = h

        p_ref[...] = _dot(hx_scr[...], wbuf[pl.program_id(1)])

        @pl.when(pos == total - 1)
        def _():
            forward()
            finish()
            place(0).wait()
            place(1).wait()

    vec = pl.BlockSpec((1, D), lambda i, j, k: (0, 0))
    return pl.pallas_call(
        body, name=name,
        grid_spec=pltpu.PrefetchScalarGridSpec(
            num_scalar_prefetch=1, grid=grid,
            in_specs=[pl.BlockSpec((tm, D), lambda i, j, k: (i, 0)), vec, vec, vec, ANY],
            out_specs=[pl.BlockSpec((tm, tn), lambda i, j, k: (i, 2 * k[0] + j)),
                       pl.BlockSpec((tm, D), lambda i, j, k: (i, 0)), ANY],
            scratch_shapes=[pltpu.VMEM((tm, D), BF16), pltpu.VMEM((2, D, tn), BF16), pltpu.SemaphoreType.DMA((2,)),
                            pltpu.SemaphoreType.DMA((2,))] + _gather_scratch(1)),
        out_shape=[jax.ShapeDtypeStruct((L, D_IN), F32), jax.ShapeDtypeStruct((L, D), BF16),
                   jax.ShapeDtypeStruct((D, D_IN), BF16)],
        compiler_params=_params("arbitrary", "arbitrary"),
    )(shard_arg, x, nw, sh, sc, w_shard)


def in_proj_next(hx, w_full, shard_arg, p, name, diag_from=None, gather=()):
    L = hx.shape[0]
    tm = min(PROJ_ROWS, L)
    tn = IN_SH // 2
    grid = (L // tm, 2)
    diag = diag_from is not None
    ng = len(gather)
    assert not (diag and ng)

    def body(k_ref, hx_ref, wf_in, p_in, *refs):
        n_src = 1 if diag else ng
        srcs = refs[:n_src]
        p_ref = refs[n_src]
        dsts = refs[n_src + 1:2 * n_src + 1]
        wbuf, wsems = refs[2 * n_src + 1:2 * n_src + 3]
        sems = refs[2 * n_src + 3:2 * n_src + 6]
        stage = refs[2 * n_src + 6:]
        pos, total = _grid_step(grid)
        w_src = dsts[0] if diag else wf_in
        if diag:
            start, forward, finish = _gather_phases(srcs, dsts, *sems, relations=(2,))
        elif ng:
            start, forward, finish = _gather_phases(srcs, dsts, *sems, stage=stage)
        if n_src:
            pl.when(pos == 0)(start)
        _w_halves(w_src, k_ref[0] * IN_SH, tn, wbuf, wsems, pos)
        p_ref[...] = _dot(hx_ref[...], wbuf[pl.program_id(1)])
        if n_src:
            @pl.when(pos == total - 1)
            def _():
                forward()
                finish()

    srcs = [diag_from] if diag else list(gather)
    out_shape = [jax.ShapeDtypeStruct((L, D_IN), F32)]
    if diag:
        out_shape.append(jax.ShapeDtypeStruct(w_full.shape, w_full.dtype))
    out_shape += [jax.ShapeDtypeStruct((N_SHARD,) + s.shape, s.dtype) for s in gather]
    aliases = {3: 0, 2: 1} if diag else {3: 0}
    return pl.pallas_call(
        body, name=name,
        grid_spec=pltpu.PrefetchScalarGridSpec(
            num_scalar_prefetch=1, grid=grid,
            in_specs=[pl.BlockSpec((tm, D), lambda i, j, k: (i, 0)), ANY, ANY] + [ANY] * len(srcs),
            out_specs=[pl.BlockSpec((tm, tn), lambda i, j, k: (i, 2 * k[0] + j))] + [ANY] * len(srcs),
            scratch_shapes=[pltpu.VMEM((2, D, tn), BF16), pltpu.SemaphoreType.DMA((2,))]
            + (_gather_scratch(len(srcs)) if srcs else []) + [pltpu.VMEM(s.shape, s.dtype) for s in gather]),
        out_shape=out_shape,
        input_output_aliases=aliases,
        compiler_params=_params("arbitrary", "arbitrary"),
    )(shard_arg, hx, w_full, p, *srcs)


def _hgrn_gates(z, lb):
    sg = _sigmoid(z)
    sgn = _sigmoid(-z)
    f = lb + (1.0 - lb) * sg
    k = (1.0 - lb) * sgn
    return sg, sgn, f, k


def _tri_chunks(n, chunk, reverse):
    r = lax.broadcasted_iota(jnp.int32, (n, n), 0)
    c = lax.broadcasted_iota(jnp.int32, (n, n), 1)
    same = (r // chunk) == (c // chunk)
    return jnp.where(same & ((r <= c) if reverse else (r >= c)), 1.0, 0.0).astype(F32)


def _decay3(b, reverse, key_major=False):
    C = b.shape[0]
    i0 = lax.broadcasted_iota(jnp.int32, (C, C, 1), 0)
    i1 = lax.broadcasted_iota(jnp.int32, (C, C, 1), 1)
    t, s = (i1, i0) if key_major else (i0, i1)
    mask = (t <= s) if reverse else (t >= s)
    diff = (b[None, :, :] - b[:, None, :]) if key_major else (b[:, None, :] - b[None, :, :])
    return jnp.exp(jnp.where(mask, diff, -jnp.inf))


HG_SUB = 8


def _hgrn_pairs(reverse):
    pairs = []
    size = HG_SUB
    while size < HG_CHUNK:
        for lo in range(0, HG_CHUNK, 2 * size):
            first, second = slice(lo, lo + size), slice(lo + size, lo + 2 * size)
            if reverse:
                pairs.append((first, second, lo + size))
            else:
                pairs.append((second, first, lo + size - 1))
        size *= 2
    return pairs


def _head_mask(g, nq, nk):
    r = lax.broadcasted_iota(jnp.int32, (g * nq, g * nk), 0) // nq
    c = lax.broadcasted_iota(jnp.int32, (g * nq, g * nk), 1) // nk
    return jnp.where(r == c, 1.0, 0.0).astype(F32)


def _hgrn_masks(g, reverse):
    return [_head_mask(g, qr.stop - qr.start, kr.stop - kr.start) for qr, kr, _ in _hgrn_pairs(reverse)]


def _stack(xs):
    return jnp.concatenate(xs, axis=0)


def _unstack(x, g):
    n = x.shape[0] // g
    return [x[h * n:(h + 1) * n] for h in range(g)]


def _add_blocks(acc, rows, part):
    for i in range(part.shape[0] // HG_SUB):
        acc[rows.start // HG_SUB + i] += part[i * HG_SUB:(i + 1) * HG_SUB]


def _hgrn_intra_fwd(qs, ks, vs, bs, masks, reverse):
    g = len(qs)
    blocks = []
    for q, k, v, b in zip(qs, ks, vs, bs):
        mine = []
        for lo in range(0, HG_CHUNK, HG_SUB):
            r = slice(lo, lo + HG_SUB)
            e3 = _decay3(b[r], reverse, key_major=True)
            att3 = jnp.sum(q[r][None, :, :] * k[r][:, None, :] * e3, axis=-1, keepdims=True)
            mine.append(jnp.sum(att3 * v[r][:, None, :], axis=0))
        blocks.append(mine)
    for (qr, kr, ref), mask in zip(_hgrn_pairs(reverse), masks):
        qt = _stack([q[qr] * jnp.exp(b[qr] - b[ref:ref + 1]) for q, b in zip(qs, bs)])
        kt = _stack([k[kr] * jnp.exp(b[ref:ref + 1] - b[kr]) for k, b in zip(ks, bs)])
        att = _bdot(qt, kt, 1, 1) * mask
        for mine, part in zip(blocks, _unstack(_bdot(att, _stack([v[kr] for v in vs])), g)):
            _add_blocks(mine, qr, part)
    return [jnp.concatenate(mine, axis=0) for mine in blocks]


def _hgrn_intra_bwd(qs, ks, vs, bs, d_os, masks, reverse):
    g = len(qs)
    nb = HG_CHUNK // HG_SUB
    dqs, dks, dvs = [], [], []
    for q, k, v, b, d_o in zip(qs, ks, vs, bs, d_os):
        dq, dk, dv = [None] * nb, [None] * nb, [None] * nb
        for i in range(nb):
            r = slice(i * HG_SUB, (i + 1) * HG_SUB)
            e3 = _decay3(b[r], reverse)
            p3 = jnp.sum(d_o[r][:, None, :] * v[r][None, :, :], axis=-1, keepdims=True) * e3
            dq[i] = jnp.sum(p3 * k[r][None, :, :], axis=1)
            dk[i] = jnp.sum(p3 * q[r][:, None, :], axis=0)
            att3 = jnp.sum(q[r][:, None, :] * k[r][None, :, :] * e3, axis=-1, keepdims=True)
            dv[i] = jnp.sum(att3 * d_o[r][:, None, :], axis=0)
        dqs.append(dq)
        dks.append(dk)
        dvs.append(dv)
    for (qr, kr, ref), mask in zip(_hgrn_pairs(reverse), masks):
        fqs = [jnp.exp(b[qr] - b[ref:ref + 1]) for b in bs]
        fks = [jnp.exp(b[ref:ref + 1] - b[kr]) for b in bs]
        qt = _stack([q[qr] * f for q, f in zip(qs, fqs)])
        kt = _stack([k[kr] * f for k, f in zip(ks, fks)])
        do_q = _stack([d_o[qr] for d_o in d_os])
        att = _bdot(qt, kt, 1, 1) * mask
        datt = _bdot(do_q, _stack([v[kr] for v in vs]), 1, 1) * mask
        for dq, part, f in zip(dqs, _unstack(_bdot(datt, kt), g), fqs):
            _add_blocks(dq, qr, part * f)
        for dk, part, f in zip(dks, _unstack(_bdot(datt, qt, 0, 0), g), fks):
            _add_blocks(dk, kr, part * f)
        for dv, part in zip(dvs, _unstack(_bdot(att, do_q, 0, 0), g)):
            _add_blocks(dv, kr, part)

    def cat(parts):
        return [jnp.concatenate(p, axis=0) for p in parts]

    return cat(dqs), cat(dks), cat(dvs)


def _hgrn_state_step(k, v, b, s_t, last):
    b_last = b[last:last + 1]
    return s_t * jnp.exp(b_last) + _bdot(v, k * jnp.exp(b_last - b), 0, 0)


def hgrn_scan_fwd(p, lb, s0, col_z, reverse, name, prev=None):
    has_prev = prev is not None
    L = p.shape[0]
    nB = L // SCAN_ROWS
    nC = SCAN_ROWS // HG_CHUNK
    C = HG_CHUNK
    G, W = HG_GROUP, HG_GROUP * HG_D
    last = 0 if reverse else C - 1

    def bmap(b):
        return (nB - 1 - b) if reverse else b

    def body(q_ref, z_ref, v_ref, lb_ref, s0_ref, *refs):
        prev_ref = refs[0] if has_prev else None
        o_ref, sfin_ref, sblk_ref, s_scr, k_scr, b_scr = refs[1:] if has_prev else refs
        blk = pl.program_id(1)

        @pl.when(blk == 0)
        def _():
            s_scr[...] = s0_ref[...]

        sblk_ref[...] = s_scr[...]
        _, _, f_all, k_all = _hgrn_gates(z_ref[...], lb_ref[...])
        k_scr[...] = k_all
        b_scr[...] = _dot(_tri_chunks(SCAN_ROWS, C, reverse), jnp.log(f_all), prec=CUMSUM_PRECISION)

        masks = _hgrn_masks(G, reverse)
        heads = [slice(j * HG_D, (j + 1) * HG_D) for j in range(G)]

        def chunk(ci, carry):
            c = (nC - 1 - ci) if reverse else ci
            rows = pl.ds(pl.multiple_of(c * C, C), C)
            qs = [q_ref[rows, lanes] * Q_SCALE for lanes in heads]
            vs = [v_ref[rows, lanes] for lanes in heads]
            ks = [k_scr[rows, lanes] for lanes in heads]
            bs = [b_scr[rows, lanes] for lanes in heads]
            o_in = _hgrn_intra_fwd(qs, ks, vs, bs, masks, reverse)
            for j, lanes in enumerate(heads):
                s_t = s_scr[j]
                o = o_in[j] + _bdot(qs[j] * jnp.exp(bs[j]), s_t, 1, 1)
                o_ref[rows, lanes] = o + prev_ref[rows, lanes] if has_prev else o
                s_scr[j] = _hgrn_state_step(ks[j], vs[j], bs[j], s_t, last)
            return carry

        lax.fori_loop(0, nC, chunk, 0)

        @pl.when(blk == nB - 1)
        def _():
            sfin_ref[...] = s_scr[...]

    def col(c0):
        return pl.BlockSpec((SCAN_ROWS, W), lambda h, b: (bmap(b), c0 // G + h))

    state = pl.BlockSpec((G, HG_D, HG_D), lambda h, b: (h, 0, 0))
    return pl.pallas_call(
        body, name=name,
        grid=(HEADS // G, nB),
        in_specs=[col(COL_HQ), col(col_z), col(COL_HI), pl.BlockSpec((1, W), lambda h, b: (0, h)), state]
        + ([pl.BlockSpec((SCAN_ROWS, W), lambda h, b: (bmap(b), h))] if has_prev else []),
        out_specs=[pl.BlockSpec((SCAN_ROWS, W), lambda h, b: (bmap(b), h)), state,
                   pl.BlockSpec((None, G, HG_D, HG_D), lambda h, b: (bmap(b), h, 0, 0))],
        out_shape=[jax.ShapeDtypeStruct((L, D), F32),
                   jax.ShapeDtypeStruct((HEADS, HG_D, HG_D), F32),
                   jax.ShapeDtypeStruct((nB, HEADS, HG_D, HG_D), F32)],
        scratch_shapes=[pltpu.VMEM((G, HG_D, HG_D), F32), pltpu.VMEM((SCAN_ROWS, W), F32),
                        pltpu.VMEM((SCAN_ROWS, W), F32)],
        compiler_params=_params("parallel", "arbitrary"),
    )(p, p, p, lb, s0, *([prev] if has_prev else []))


def hgrn_scan_bwd(p, lb, s_blocks, d_o, ds_fin, prev, col_z, reverse, name):
    L = p.shape[0]
    nB = L // SCAN_ROWS
    nC = SCAN_ROWS // HG_CHUNK
    C = HG_CHUNK
    G, W = HG_GROUP, HG_GROUP * HG_D
    last = 0 if reverse else C - 1
    has_prev = prev is not None
    out_dt = BF16 if has_prev else F32

    def bmap(b):
        return b if reverse else (nB - 1 - b)

    def body(*refs):
        q_ref, z_ref, v_ref, lb_ref, sblk_ref, do_ref, dsf_ref = refs[:7]
        refs = refs[7:]
        if has_prev:
            pq_ref, pv_ref = refs[:2]
            refs = refs[2:]
        (dq_ref, dz_ref, dv_ref, dlb_ref, ds0_ref, st_scr, run_scr, ds_scr, k_scr, b_scr, db_scr, dk_scr,
         rf_scr, sgn_scr, dzf_scr) = refs
        blk = pl.program_id(1)

        @pl.when(blk == 0)
        def _():
            ds_scr[...] = dsf_ref[...]
            dlb_ref[...] = jnp.zeros_like(dlb_ref)

        tri = _tri_chunks(SCAN_ROWS, C, reverse)
        row = lax.broadcasted_iota(jnp.int32, (C, HG_D), 0)
        lb_all = lb_ref[...]
        sg_all, sgn_all, f_all, k_all = _hgrn_gates(z_ref[...], lb_all)
        k_scr[...] = k_all
        rf_scr[...] = 1.0 / f_all
        sgn_scr[...] = sgn_all
        dzf_scr[...] = (1.0 - lb_all) * sg_all * sgn_all
        b_scr[...] = _dot(tri, jnp.log(f_all), prec=CUMSUM_PRECISION)
        run_scr[...] = sblk_ref[...]

        def recompute(ci, carry):
            c = (nC - 1 - ci) if reverse else ci
            rows = pl.ds(pl.multiple_of(c * C, C), C)
            for j in range(G):
                lanes = slice(j * HG_D, (j + 1) * HG_D)
                s_t = run_scr[j]
                st_scr[c, j] = s_t
                run_scr[j] = _hgrn_state_step(k_scr[rows, lanes], v_ref[rows, lanes], b_scr[rows, lanes], s_t, last)
            return carry

        lax.fori_loop(0, nC, recompute, 0)

        masks = _hgrn_masks(G, reverse)
        heads = [slice(j * HG_D, (j + 1) * HG_D) for j in range(G)]

        def chunk(ci, carry):
            c = ci if reverse else (nC - 1 - ci)
            rows = pl.ds(pl.multiple_of(c * C, C), C)
            ks = [k_scr[rows, lanes] for lanes in heads]
            bs = [b_scr[rows, lanes] for lanes in heads]
            qs = [q_ref[rows, lanes] * Q_SCALE for lanes in heads]
            vs = [v_ref[rows, lanes] for lanes in heads]
            d_os = [do_ref[rows, lanes] for lanes in heads]
            dq_ins, dk_ins, dv_ins = _hgrn_intra_bwd(qs, ks, vs, bs, d_os, masks, reverse)
            for j, lanes in enumerate(heads):
                k, b, q, v, d_o = ks[j], bs[j], qs[j], vs[j], d_os[j]
                s_t = st_scr[c, j]
                ds_t = ds_scr[j]
                eb = jnp.exp(b)
                b_last = b[last:last + 1]
                eb_last = jnp.exp(b_last)
                kdec = jnp.exp(b_last - b)
                qe = q * eb
                ke = k * kdec
                dq_tot = _bdot(d_o, s_t, 1, 0) * eb + dq_ins[j]
                dke = _bdot(v, ds_t, 1, 0)
                dk_tot = dke * kdec + dk_ins[j]
                dv = dv_ins[j] + _bdot(ke, ds_t, 1, 1)
                db_last = _rowsum(dke * ke) + eb_last * _rowsum(ds_t * s_t)
                db_scr[rows, lanes] = q * dq_tot - k * dk_tot + jnp.where(row == last, db_last, 0.0)
                dk_scr[rows, lanes] = dk_tot
                dq = dq_tot * Q_SCALE
                if has_prev:
                    dq = dq + pq_ref[rows, lanes]
                    dv = dv + pv_ref[rows, lanes]
                dq_ref[rows, lanes] = dq.astype(out_dt)
                dv_ref[rows, lanes] = dv.astype(out_dt)
                ds_scr[j] = ds_t * eb_last + _bdot(d_o, qe, 0, 0)
            return carry

        lax.fori_loop(0, nC, chunk, 0)

        g = _dot(tri, db_scr[...], 0, 0, prec=CUMSUM_PRECISION) * rf_scr[...] - dk_scr[...]
        dz_ref[...] = (g * dzf_scr[...]).astype(BF16)
        dlb_ref[...] += _rowsum(g * sgn_scr[...])

        @pl.when(blk == nB - 1)
        def _():
            ds0_ref[...] = ds_scr[...]

    def col(c0):
        return pl.BlockSpec((SCAN_ROWS, W), lambda h, b: (bmap(b), c0 // G + h))

    tile = pl.BlockSpec((SCAN_ROWS, W), lambda h, b: (bmap(b), h))
    state = pl.BlockSpec((G, HG_D, HG_D), lambda h, b: (h, 0, 0))
    in_specs = [col(COL_HQ), col(col_z), col(COL_HI),
                pl.BlockSpec((1, W), lambda h, b: (0, h)),
                pl.BlockSpec((None, G, HG_D, HG_D), lambda h, b: (bmap(b), h, 0, 0)),
                tile, state]
    args = [p, p, p, lb, s_blocks, d_o, ds_fin]
    if has_prev:
        in_specs += [tile, tile]
        args += list(prev)
    return pl.pallas_call(
        body, name=name,
        grid=(HEADS // G, nB),
        in_specs=in_specs,
        out_specs=[tile, tile, tile, pl.BlockSpec((1, W), lambda h, b: (0, h)), state],
        out_shape=[jax.ShapeDtypeStruct((L, D), out_dt), jax.ShapeDtypeStruct((L, D), BF16),
                   jax.ShapeDtypeStruct((L, D), out_dt), jax.ShapeDtypeStruct((1, D), F32),
                   jax.ShapeDtypeStruct((HEADS, HG_D, HG_D), F32)],
        scratch_shapes=[pltpu.VMEM((nC, G, HG_D, HG_D), F32), pltpu.VMEM((G, HG_D, HG_D), F32),
                        pltpu.VMEM((G, HG_D, HG_D), F32)] + [pltpu.VMEM((SCAN_ROWS, W), F32)] * 7,
        compiler_params=_params("parallel", "arbitrary"),
    )(*args)


def _rope(t, cosf, sinf):
    return t * cosf + pltpu.roll(t, RT_DK // 2, 1) * sinf


def _rope_t(d, cosf, sinf):
    return d * cosf + pltpu.roll(d * sinf, RT_DK // 2, 1)


def _ret_decays(lg, reverse):
    C = SCAN_ROWS
    t = lax.broadcasted_iota(jnp.int32, (C, C), 0)
    s = lax.broadcasted_iota(jnp.int32, (C, C), 1)
    delta = ((s - t) if reverse else (t - s)).astype(F32)
    dmat = jnp.where(delta >= 0, jnp.exp(lg * jnp.maximum(delta, 0.0)), 0.0)
    r = lax.broadcasted_iota(jnp.int32, (C, RT_DK), 0)
    pos = ((C - 1 - r) if reverse else r).astype(F32)
    lg1 = lg[:, :RT_DK]
    qdec = jnp.exp(lg1 * (pos + 1.0))
    kdec = jnp.exp(lg1 * (C - 1.0 - pos))
    sdec = jnp.exp(lg1 * float(C))
    return dmat, delta, pos, qdec, kdec, sdec


def ret_scan_fwd(p, cosf, sinf, lg, s0, reverse, name, prev=None):
    has_prev = prev is not None
    L = p.shape[0]
    C = SCAN_ROWS
    nB = L // C

    def bmap(b):
        return (nB - 1 - b) if reverse else b

    G = RT_GROUP

    def body(q_ref, k_ref, v_ref, cos_ref, sin_ref, lg_ref, s0_ref, *refs):
        prev_ref = refs[0] if has_prev else None
        o_ref, sfin_ref, sblk_ref, s_scr = refs[1:] if has_prev else refs
        blk = pl.program_id(1)

        @pl.when(blk == 0)
        def _():
            s_scr[...] = s0_ref[...]

        sblk_ref[...] = s_scr[...].astype(BF16)
        cosf, sinf = cos_ref[...], sin_ref[...]
        for j in range(G):
            lk, lv = slice(j * RT_DK, (j + 1) * RT_DK), slice(j * RT_DV, (j + 1) * RT_DV)
            s_t = s_scr[j]
            dmat, _, _, qdec, kdec, sdec = _ret_decays(lg_ref[j], reverse)
            q = _rope(q_ref[:, lk] * Q_SCALE, cosf, sinf)
            k = _rope(k_ref[:, lk], cosf, sinf)
            v = v_ref[:, lv]
            att = _bdot(q, k, 1, 1) * dmat
            o = _bdot(att, v) + _bdot(q * qdec, s_t, 1, 1)
            o_ref[:, lv] = o + prev_ref[:, lv] if has_prev else o
            s_scr[j] = s_t * sdec + _bdot(v, k * kdec, 0, 0)

        @pl.when(blk == nB - 1)
        def _():
            sfin_ref[...] = s_scr[...]

    def col(c0):
        return pl.BlockSpec((C, G * RT_DK), lambda h, b: (bmap(b), c0 // G + h))

    tab = pl.BlockSpec((C, RT_DK), lambda h, b: (bmap(b), 0))
    state = pl.BlockSpec((G, RT_DV, RT_DK), lambda h, b: (h, 0, 0))
    return pl.pallas_call(
        body, name=name,
        grid=(HEADS // G, nB),
        in_specs=[col(COL_RQ), col(COL_RK),
                  pl.BlockSpec((C, G * RT_DV), lambda h, b: (bmap(b), COL_RV // (2 * G) + h)),
                  tab, tab, pl.BlockSpec((G, 1, RT_DV), lambda h, b: (h, 0, 0)), state]
        + ([pl.BlockSpec((C, G * RT_DV), lambda h, b: (bmap(b), h))] if has_prev else []),
        out_specs=[pl.BlockSpec((C, G * RT_DV), lambda h, b: (bmap(b), h)), state,
                   pl.BlockSpec((None, G, RT_DV, RT_DK), lambda h, b: (bmap(b), h, 0, 0))],
        out_shape=[jax.ShapeDtypeStruct((L, HEADS * RT_DV), F32),
                   jax.ShapeDtypeStruct((HEADS, RT_DV, RT_DK), F32),
                   jax.ShapeDtypeStruct((nB, HEADS, RT_DV, RT_DK), BF16)],
        scratch_shapes=[pltpu.VMEM((G, RT_DV, RT_DK), F32)],
        compiler_params=_params("parallel", "arbitrary"),
    )(p, p, p, cosf, sinf, lg, s0, *([prev] if has_prev else []))


def ret_scan_bwd(p, cosf, sinf, lg, s_blocks, d_o, ds_fin, prev, reverse, name):
    L = p.shape[0]
    C = SCAN_ROWS
    nB = L // C
    has_prev = prev is not None
    out_dt = BF16
    G = RT_GROUP

    def bmap(b):
        return b if reverse else (nB - 1 - b)

    def body(*refs):
        q_ref, k_ref, v_ref, cos_ref, sin_ref, lg_ref, sblk_ref, do_ref, dsf_ref = refs[:9]
        refs = refs[9:]
        if has_prev:
            pq_ref, pk_ref, pv_ref = refs[:3]
            refs = refs[3:]
        dq_ref, dk_ref, dv_ref, dlg_ref, ds0_ref, ds_scr = refs
        blk = pl.program_id(1)

        @pl.when(blk == 0)
        def _():
            ds_scr[...] = dsf_ref[...]
            dlg_ref[...] = jnp.zeros_like(dlg_ref)

        cosf, sinf = cos_ref[...], sin_ref[...]
        for j in range(G):
            lk, lv = slice(j * RT_DK, (j + 1) * RT_DK), slice(j * RT_DV, (j + 1) * RT_DV)
            s_t = sblk_ref[j].astype(F32)
            ds_t = ds_scr[j]
            dmat, delta, pos, qdec, kdec, sdec = _ret_decays(lg_ref[j], reverse)
            q = _rope(q_ref[:, lk] * Q_SCALE, cosf, sinf)
            k = _rope(k_ref[:, lk], cosf, sinf)
            v = v_ref[:, lv]
            d_o = do_ref[:, lv]
            att_raw = _bdot(q, k, 1, 1)
            datt_m = _bdot(d_o, v, 1, 1) * dmat
            dqd = _bdot(d_o, s_t, 1, 0)
            dkd = _bdot(v, ds_t, 1, 0)
            dq = _bdot(datt_m, k) + dqd * qdec
            dk = _bdot(datt_m, q, 0, 0) + dkd * kdec
            dv = _bdot(att_raw * dmat, d_o, 0, 0) + _bdot(k * kdec, ds_t, 1, 1)
            ds_scr[j] = ds_t * sdec + _bdot(d_o, q * qdec, 0, 0)
            t1 = jnp.sum(_rowsum(datt_m * att_raw * delta), axis=-1, keepdims=True)
            t23 = jnp.sum(_rowsum((pos + 1.0) * qdec * q * dqd + (C - 1.0 - pos) * kdec * k * dkd), axis=-1, keepdims=True)
            t4 = jnp.sum(_rowsum(ds_t * s_t * sdec), axis=-1, keepdims=True) * float(C)
            dlg_ref[j] += jnp.broadcast_to(t1 + t23 + t4, (1, RT_DK))
            if has_prev:
                dq = _rope_t(dq + pq_ref[:, lk].astype(F32), cosf, sinf) * Q_SCALE
                dk = _rope_t(dk + pk_ref[:, lk].astype(F32), cosf, sinf)
                dv = dv + pv_ref[:, lv].astype(F32)
            dq_ref[:, lk] = dq.astype(out_dt)
            dk_ref[:, lk] = dk.astype(out_dt)
            dv_ref[:, lv] = dv.astype(out_dt)

        @pl.when(blk == nB - 1)
        def _():
            ds0_ref[...] = ds_scr[...]

    def col(c0):
        return pl.BlockSpec((C, G * RT_DK), lambda h, b: (bmap(b), c0 // G + h))

    tab = pl.BlockSpec((C, RT_DK), lambda h, b: (bmap(b), 0))
    state = pl.BlockSpec((G, RT_DV, RT_DK), lambda h, b: (h, 0, 0))
    tk = pl.BlockSpec((C, G * RT_DK), lambda h, b: (bmap(b), h))
    tv = pl.BlockSpec((C, G * RT_DV), lambda h, b: (bmap(b), h))
    in_specs = [col(COL_RQ), col(COL_RK),
                pl.BlockSpec((C, G * RT_DV), lambda h, b: (bmap(b), COL_RV // (2 * G) + h)),
                tab, tab, pl.BlockSpec((G, 1, RT_DV), lambda h, b: (h, 0, 0)),
                pl.BlockSpec((None, G, RT_DV, RT_DK), lambda h, b: (bmap(b), h, 0, 0)),
                tv, state]
    args = [p, p, p, cosf, sinf, lg, s_blocks, d_o, ds_fin]
    if has_prev:
        in_specs += [tk, tk, tv]
        args += list(prev)
    return pl.pallas_call(
        body, name=name,
        grid=(HEADS // G, nB),
        in_specs=in_specs,
        out_specs=[tk, tk, tv, pl.BlockSpec((G, 1, RT_DK), lambda h, b: (h, 0, 0)), state],
        out_shape=[jax.ShapeDtypeStruct((L, D), out_dt), jax.ShapeDtypeStruct((L, D), out_dt),
                   jax.ShapeDtypeStruct((L, HEADS * RT_DV), out_dt),
                   jax.ShapeDtypeStruct((HEADS, 1, RT_DK), F32),
                   jax.ShapeDtypeStruct((HEADS, RT_DV, RT_DK), F32)],
        scratch_shapes=[pltpu.VMEM((G, RT_DV, RT_DK), F32)],
        compiler_params=_params("parallel", "arbitrary"),
    )(*args)


def _silu_parts(h):
    s = _sigmoid(h)
    return h * s, s * (1.0 + h * (1.0 - s))


def _head_rms(o):
    outs, rs = [], []
    for h in range(HEADS):
        oh = o[:, h * HG_D:(h + 1) * HG_D]
        r = lax.rsqrt(_lanemean(oh * oh) + EPS)
        outs.append(oh * r)
        rs.append(r)
    return outs, rs


def _group_norm(o):
    outs, rs = [], []
    for h in range(HEADS):
        oh = o[:, h * RT_DV:(h + 1) * RT_DV]
        c = oh - _lanemean(oh)
        r = lax.rsqrt(_lanemean(c * c) + GN_EPS)
        outs.append(c * r)
        rs.append(r)
    return outs, rs


MIX_ROWS = 256


def _mix_specs(rows):
    def t(w, c=0):
        return pl.BlockSpec((rows, w), lambda i: (i, c))

    return t


def mix_fwd(o_hg, o_rt, p, x, g1, hgw, w_pa, w_pb, w_out, name):
    L = x.shape[0]
    t = _mix_specs(MIX_ROWS)

    def body(ohg_ref, ort_ref, hg_ref, rg0_ref, rg1_ref, ga_ref, gb_ref, x_ref, g1_ref, hgw_ref,
             wpa_ref, wpb_ref, wout_ref, x1_ref, xmix_ref, merged_ref, ya_ref, yb_ref):
        nh, _ = _head_rms(ohg_ref[...])
        ya = jnp.concatenate(nh, axis=1) * hgw_ref[...] * _silu_parts(hg_ref[...])[0]
        gn, _ = _group_norm(ort_ref[...])
        rg = jnp.concatenate([rg0_ref[...], rg1_ref[...]], axis=1)
        yb = jnp.concatenate(gn, axis=1) * _silu_parts(rg)[0]
        ya16, yb16 = ya.astype(BF16), yb.astype(BF16)
        merged = (_sigmoid(ga_ref[...]) * _dot(ya16, wpa_ref[...])
                  + _sigmoid(gb_ref[...]) * _dot(yb16, wpb_ref[...])).astype(BF16)
        x_mix = _dot(merged, wout_ref[...])
        x1_ref[...] = x_ref[...] + g1_ref[...] * x_mix
        xmix_ref[...] = x_mix
        merged_ref[...] = merged
        ya_ref[...] = ya16
        yb_ref[...] = yb16

    vec = pl.BlockSpec((1, D), lambda i: (0, 0))

    def full(a):
        return pl.BlockSpec(a.shape, lambda i: (0, 0), pipeline_mode=pl.Buffered(1))

    return pl.pallas_call(
        body, name=name,
        grid=(L // MIX_ROWS,),
        in_specs=[t(D), t(2 * D), t(D, COL_HG // 8), t(D, COL_RG // 8), t(D, COL_RG // 8 + 1),
                  t(D, COL_GA // 8), t(D, COL_GB // 8), t(D), vec, vec, full(w_pa), full(w_pb), full(w_out)],
        out_specs=[t(D), t(D), t(D), t(D), t(2 * D)],
        out_shape=[jax.ShapeDtypeStruct((L, D), F32), jax.ShapeDtypeStruct((L, D), F32),
                   jax.ShapeDtypeStruct((L, D), BF16), jax.ShapeDtypeStruct((L, D), BF16),
                   jax.ShapeDtypeStruct((L, 2 * D), BF16)],
        compiler_params=_params("parallel"),
    )(o_hg, o_rt, p, p, p, p, p, x, g1, hgw, w_pa, w_pb, w_out)


def mix_bwd(dx1, x_mix, ya, yb, o_hg, o_rt, p, g1, hgw, w_pa, w_pb, w_out, name, to_sibling=()):
    L = dx1.shape[0]
    t = _mix_specs(MIX_ROWS)
    nx = len(to_sibling)
    steps = L // MIX_ROWS

    def body(dx1_ref, xmix_ref, ya_ref, yb_ref, ohg_ref, ort_ref, hg_ref, rg0_ref, rg1_ref,
             ga_ref, gb_ref, g1_ref, hgw_ref, wpa_ref, wpb_ref, wout_ref, *refs):
        (dxm_ref, da_ref, db_ref, dga_ref, dgb_ref, dhg_ref, drg_ref, dohg_ref, dort_ref,
         sums_ref) = refs[nx:nx + 10]
        if nx:
            start, finish = _to_sibling_phases(refs[:nx], refs[nx + 10:2 * nx + 10], *refs[2 * nx + 10:])
            pl.when(pl.program_id(0) == 0)(start)
            pl.when(pl.program_id(0) == steps - 1)(finish)

        @pl.when(pl.program_id(0) == 0)
        def _():
            sums_ref[...] = jnp.zeros_like(sums_ref)

        dx1 = dx1_ref[...]
        dxm = (g1_ref[...] * dx1).astype(BF16)
        dxm_ref[...] = dxm
        dmerged = _dot(dxm, wout_ref[...], 1, 1)
        a = _dot(ya_ref[...], wpa_ref[...])
        bm = _dot(yb_ref[...], wpb_ref[...])
        sa, sb = _sigmoid(ga_ref[...]), _sigmoid(gb_ref[...])
        d_a = (dmerged * sa).astype(BF16)
        d_b = (dmerged * sb).astype(BF16)
        da_ref[...] = d_a
        db_ref[...] = d_b
        dga_ref[...] = (dmerged * a * sa * (1.0 - sa)).astype(BF16)
        dgb_ref[...] = (dmerged * bm * sb * (1.0 - sb)).astype(BF16)
        dya = _dot(d_a, wpa_ref[...], 1, 1)
        dyb = _dot(d_b, wpb_ref[...], 1, 1)

        hgw = hgw_ref[...]
        silu_h, dsilu_h = _silu_parts(hg_ref[...])
        nh, rh = _head_rms(ohg_ref[...])
        n = jnp.concatenate(nh, axis=1)
        dhg_ref[...] = (dya * n * hgw * dsilu_h).astype(BF16)
        dn = dya * hgw * silu_h
        douts = []
        for h in range(HEADS):
            dnh = dn[:, h * HG_D:(h + 1) * HG_D]
            douts.append(rh[h] * (dnh - nh[h] * _lanemean(dnh * nh[h])))
        dohg_ref[...] = jnp.concatenate(douts, axis=1)

        rg = jnp.concatenate([rg0_ref[...], rg1_ref[...]], axis=1)
        silu_r, dsilu_r = _silu_parts(rg)
        gn, rr = _group_norm(ort_ref[...])
        g = jnp.concatenate(gn, axis=1)
        drg_ref[...] = (dyb * g * dsilu_r).astype(BF16)
        dgn = dyb * silu_r
        douts = []
        for h in range(HEADS):
            dgh = dgn[:, h * RT_DV:(h + 1) * RT_DV]
            douts.append(rr[h] * (dgh - _lanemean(dgh) - gn[h] * _lanemean(dgh * gn[h])))
        dort_ref[...] = jnp.concatenate(douts, axis=1)

        sums_ref[0:1, :] += _rowsum(dx1 * xmix_ref[...])
        sums_ref[1:2, :] += _rowsum(dya * n * silu_h)

    vec = pl.BlockSpec((1, D), lambda i: (0, 0))

    def full(a):
        return pl.BlockSpec(a.shape, lambda i: (0, 0), pipeline_mode=pl.Buffered(1))

    bf = functools.partial(jax.ShapeDtypeStruct, dtype=BF16)
    return pl.pallas_call(
        body, name=name,
        grid=(L // MIX_ROWS,),
        in_specs=[t(D), t(D), t(D), t(2 * D), t(D), t(2 * D),
                  t(D, COL_HG // 8), t(D, COL_RG // 8), t(D, COL_RG // 8 + 1), t(D, COL_GA // 8), t(D, COL_GB // 8),
                  vec, vec, full(w_pa), full(w_pb), full(w_out)] + [ANY] * nx,
        out_specs=[t(D), t(D), t(D), t(D), t(D), t(D), t(2 * D), t(D), t(2 * D),
                   pl.BlockSpec((8, D), lambda i: (0, 0))] + [ANY] * nx,
        out_shape=[bf((L, D)), bf((L, D)), bf((L, D)), bf((L, D)), bf((L, D)), bf((L, D)), bf((L, 2 * D)),
                   jax.ShapeDtypeStruct((L, D), F32), jax.ShapeDtypeStruct((L, 2 * D), F32),
                   jax.ShapeDtypeStruct((8, D), F32)] + [jax.ShapeDtypeStruct(a.shape, a.dtype) for a in to_sibling],
        scratch_shapes=_to_sibling_scratch(nx) if nx else [],
        compiler_params=_params("arbitrary"),
    )(dx1, x_mix, ya, yb, o_hg, o_rt, p, p, p, p, p, g1, hgw, w_pa, w_pb, w_out, *to_sibling)


FFN_ROWS = 512


def ffn_fwd(x1, target, nw2, sh2, sc2, g2, fw, wg, wu, wd, name):
    L = x1.shape[0]
    tm = min(FFN_ROWS, L)

    def body(x1_ref, tgt_ref, nw2_ref, sh2_ref, sc2_ref, g2_ref, fw_ref, wg_ref, wu_ref, wd_ref,
             hx2_ref, g_ref, u_ref, h_ref, f_ref, dx2_ref, sums_ref, hx_scr, acc):
        i, j = pl.program_id(0), pl.program_id(1)

        @pl.when((i == 0) & (j == 0))
        def _():
            sums_ref[...] = jnp.zeros_like(sums_ref)

        @pl.when(j == 0)
        def _():
            xv = x1_ref[...]
            n = xv * lax.rsqrt(_lanemean(xv * xv) + EPS) * nw2_ref[...]
            h = (n * (1.0 + sc2_ref[...]) + sh2_ref[...]).astype(BF16)
            hx_scr[...] = h
            hx2_ref[...] = h
            acc[...] = jnp.zeros_like(acc)

        hx = hx_scr[...]
        g = _dot(hx, wg_ref[...])
        u = _dot(hx, wu_ref[...])
        hh = (_silu_parts(g)[0] * u).astype(BF16)
        g_ref[...] = g
        u_ref[...] = u
        h_ref[...] = hh
        acc[...] += _dot(hh, wd_ref[...])

        @pl.when(j == N_SHARD - 1)
        def _():
            f = acc[...]
            f_ref[...] = f
            x2 = x1_ref[...] + g2_ref[...] * f
            r = lax.rsqrt(_lanemean(x2 * x2) + EPS)
            fw = fw_ref[...]
            e = x2 * r * fw - tgt_ref[...]
            dy = e * (1.0 / D)
            dyw = dy * fw
            dx2_ref[...] = r * dyw - x2 * (r * r * r) * _lanemean(dyw * x2)
            sums_ref[0:1, :] += _rowsum(dy * x2 * r)
            sums_ref[1:2, :] += _rowsum(e * e) * (0.5 / D)

    row = pl.BlockSpec((tm, D), lambda i, j: (i, 0))
    vec = pl.BlockSpec((1, D), lambda i, j: (0, 0))
    sh = pl.BlockSpec((None, tm, FF_SH), lambda i, j: (j, i, 0))
    return pl.pallas_call(
        body, name=name,
        grid=(L // tm, N_SHARD),
        in_specs=[row, row, vec, vec, vec, vec, vec,
                  pl.BlockSpec((None, D, FF_SH), lambda i, j: (j, 0, 0)),
                  pl.BlockSpec((None, D, FF_SH), lambda i, j: (j, 0, 0)),
                  pl.BlockSpec((None, FF_SH, D), lambda i, j: (j, 0, 0))],
        out_specs=[row, sh, sh, sh, row, row, pl.BlockSpec((8, D), lambda i, j: (0, 0))],
        out_shape=[jax.ShapeDtypeStruct((L, D), BF16),
                   jax.ShapeDtypeStruct((N_SHARD, L, FF_SH), F32), jax.ShapeDtypeStruct((N_SHARD, L, FF_SH), F32),
                   jax.ShapeDtypeStruct((N_SHARD, L, FF_SH), BF16),
                   jax.ShapeDtypeStruct((L, D), F32), jax.ShapeDtypeStruct((L, D), F32),
                   jax.ShapeDtypeStruct((8, D), F32)],
        scratch_shapes=[pltpu.VMEM((tm, D), BF16), pltpu.VMEM((tm, D), F32)],
        compiler_params=_params("arbitrary", "arbitrary"),
    )(x1, target, nw2, sh2, sc2, g2, fw, wg, wu, wd)


def ffn_bwd(dx2, x1, f, g, u, nw2, sc2, g2, wg, wu, wd, name):
    L = x1.shape[0]
    tm = min(FFN_ROWS, L)

    def body(dx2_ref, x1_ref, f_ref, g_ref, u_ref, nw2_ref, sc2_ref, g2_ref, wg_ref, wu_ref, wd_ref,
             df_ref, dg_ref, du_ref, dx1_ref, sums_ref, df_scr, acc):
        i, j = pl.program_id(0), pl.program_id(1)

        @pl.when((i == 0) & (j == 0))
        def _():
            sums_ref[...] = jnp.zeros_like(sums_ref)

        @pl.when(j == 0)
        def _():
            dx2 = dx2_ref[...]
            df = (g2_ref[...] * dx2).astype(BF16)
            df_scr[...] = df
            df_ref[...] = df
            sums_ref[0:1, :] += _rowsum(dx2 * f_ref[...])
            acc[...] = jnp.zeros_like(acc)

        dh = _dot(df_scr[...], wd_ref[...], 1, 1)
        gv, uv = g_ref[...], u_ref[...]
        silu_g, dsilu_g = _silu_parts(gv)
        dg = (dh * uv * dsilu_g).astype(BF16)
        du = (dh * silu_g).astype(BF16)
        dg_ref[...] = dg
        du_ref[...] = du
        acc[...] += _dot(dg, wg_ref[...], 1, 1) + _dot(du, wu_ref[...], 1, 1)

        @pl.when(j == N_SHARD - 1)
        def _():
            dhx = acc[...]
            xv = x1_ref[...]
            r = lax.rsqrt(_lanemean(xv * xv) + EPS)
            n0 = xv * r
            nw = nw2_ref[...]
            dn2 = dhx * (1.0 + sc2_ref[...])
            dn0 = dn2 * nw
            dx1_ref[...] = dx2_ref[...] + r * (dn0 - n0 * _lanemean(dn0 * n0))
            sums_ref[1:2, :] += _rowsum(dhx)
            sums_ref[2:3, :] += _rowsum(dhx * n0 * nw)
            sums_ref[3:4, :] += _rowsum(dn2 * n0)

    row = pl.BlockSpec((tm, D), lambda i, j: (i, 0))
    vec = pl.BlockSpec((1, D), lambda i, j: (0, 0))
    sh = pl.BlockSpec((None, tm, FF_SH), lambda i, j: (j, i, 0))
    return pl.pallas_call(
        body, name=name,
        grid=(L // tm, N_SHARD),
        in_specs=[row, row, row, sh, sh, vec, vec, vec,
                  pl.BlockSpec((None, D, FF_SH), lambda i, j: (j, 0, 0)),
                  pl.BlockSpec((None, D, FF_SH), lambda i, j: (j, 0, 0)),
                  pl.BlockSpec((None, FF_SH, D), lambda i, j: (j, 0, 0))],
        out_specs=[row, sh, sh, row, pl.BlockSpec((8, D), lambda i, j: (0, 0))],
        out_shape=[jax.ShapeDtypeStruct((L, D), BF16),
                   jax.ShapeDtypeStruct((N_SHARD, L, FF_SH), BF16), jax.ShapeDtypeStruct((N_SHARD, L, FF_SH), BF16),
                   jax.ShapeDtypeStruct((L, D), F32), jax.ShapeDtypeStruct((8, D), F32)],
        scratch_shapes=[pltpu.VMEM((tm, D), BF16), pltpu.VMEM((tm, D), F32)],
        compiler_params=_params("arbitrary", "arbitrary"),
    )(dx2, x1, f, g, u, nw2, sc2, g2, wg, wu, wd)


def matmul_tn(a, b, name):
    na, K, M = a.shape
    nb, _, N = b.shape
    n = max(na, nb)
    tk = min(1024, K)
    tn = N if N <= 1024 else N // 2

    def body(a_ref, b_ref, o_ref):
        @pl.when(pl.program_id(2) == 0)
        def _():
            o_ref[...] = jnp.zeros_like(o_ref)

        o_ref[...] += _dot(a_ref[...], b_ref[...], 0, 0)

    return pl.pallas_call(
        body, name=name,
        grid=(n, N // tn, K // tk),
        in_specs=[pl.BlockSpec((None, tk, M), lambda s, j, kk: (s if na > 1 else 0, kk, 0)),
                  pl.BlockSpec((None, tk, tn), lambda s, j, kk: (s if nb > 1 else 0, kk, j))],
        out_specs=pl.BlockSpec((None, M, tn), lambda s, j, kk: (s, 0, j)),
        out_shape=jax.ShapeDtypeStruct((n, M, N), F32),
        compiler_params=_params("parallel", "parallel", "arbitrary"),
    )(a, b)


def matmul_tn_pair(a, b1, b2, name):
    K, M = a.shape
    n, _, N = b1.shape
    tk = min(1024, K)

    def body(a_ref, b1_ref, b2_ref, o1_ref, o2_ref):
        @pl.when(pl.program_id(1) == 0)
        def _():
            o1_ref[...] = jnp.zeros_like(o1_ref)
            o2_ref[...] = jnp.zeros_like(o2_ref)

        at = a_ref[...].T
        o1_ref[...] += _dot(at, b1_ref[...])
        o2_ref[...] += _dot(at, b2_ref[...])

    b_spec = pl.BlockSpec((None, tk, N), lambda s, kk: (s, kk, 0))
    o_spec = pl.BlockSpec((None, M, N), lambda s, kk: (s, 0, 0))
    return pl.pallas_call(
        body, name=name,
        grid=(n, K // tk),
        in_specs=[pl.BlockSpec((tk, M), lambda s, kk: (kk, 0)), b_spec, b_spec],
        out_specs=[o_spec, o_spec],
        out_shape=[jax.ShapeDtypeStruct((n, M, N), F32)] * 2,
        compiler_params=_params("parallel", "arbitrary"),
    )(a, b1, b2)


PIECE_COLS = 1024
N_PIECE_BLOCKS = D_IN // PIECE_COLS


def _piece_blocks(pieces):
    out, col = [], 0
    for arr, width in pieces:
        if arr is not None:
            out.append((arr, col // PIECE_COLS, width // PIECE_COLS))
        col += width
    assert col == D_IN
    return out


def _piece_feed(p_refs, blocks, buf, sems, tile_of, pos, total):
    def present(blk):
        ok = None
        for _, b0, nb in blocks:
            mine = (blk >= b0) & (blk < b0 + nb)
            ok = mine if ok is None else ok | mine
        return ok

    def fetch(step):
        blk, rows = tile_of(step)
        for p_ref, (_, b0, nb) in zip(p_refs, blocks):
            for t in range(nb):
                @pl.when(blk == b0 + t)
                def _(p_ref=p_ref, t=t):
                    pltpu.make_async_copy(p_ref.at[rows, pl.ds(t * PIECE_COLS, PIECE_COLS)], buf.at[step % 2],
                                          sems.at[step % 2]).start()

    @pl.when(pos == 0)
    def _():
        fetch(pos)

    @pl.when(pos + 1 < total)
    def _():
        fetch(pos + 1)

    def landed():
        slot = pos % 2
        pltpu.make_async_copy(p_refs[0].at[pl.ds(0, buf.shape[1]), pl.ds(0, PIECE_COLS)], buf.at[slot],
                              sems.at[slot]).wait()
        return buf.at[slot]

    return present(tile_of(pos)[0]), landed


def matmul_tn_pieces(a, pieces, name, extra=None, to_chips=()):
    K, M = a.shape
    blocks = _piece_blocks(pieces)
    tk = min(1024, K)
    nk = K // tk
    grid = (N_PIECE_BLOCKS, nk)
    nx, npc = len(to_chips), len(blocks)
    a2, blocks2 = (extra[0], _piece_blocks(extra[1])) if extra is not None else (None, [])
    npc2 = len(blocks2)

    def body(a_ref, *refs):
        p_refs = refs[:npc]
        refs = refs[npc:]
        a2_ref, p2_refs = (refs[0], refs[1:1 + npc2]) if npc2 else (None, ())
        refs = refs[1 + npc2:] if npc2 else refs
        o_ref = refs[nx]
        buf, sems = refs[2 * nx + 1:2 * nx + 3]
        rest = refs[2 * nx + 3:]
        pos, total = _grid_step(grid)
        if nx:
            start, finish = _to_chips_phases(refs[:nx], refs[nx + 1:2 * nx + 1], *rest[:2])
            pl.when(pos == 0)(start)
        here, landed = _piece_feed(p_refs, blocks, buf, sems,
                                   lambda s: (s // nk, pl.ds(pl.multiple_of((s % nk) * tk, tk), tk)), pos, total)
        blk, kk = pl.program_id(0), pl.program_id(1)

        @pl.when(kk == 0)
        def _():
            o_ref[...] = jnp.zeros_like(o_ref)

        @pl.when(here)
        def _():
            o_ref[...] += _dot(a_ref[...], landed()[...], 0, 0)

        if npc2:
            buf2, sem2 = rest[-2:]

            def tile(p_ref, t):
                return pltpu.make_async_copy(p_ref.at[:, pl.ds(t * PIECE_COLS, PIECE_COLS)], buf2, sem2.at[0])

            for p_ref, (_, b0, nb) in zip(p2_refs, blocks2):
                for t in range(nb):
                    @pl.when((blk == b0 + t) & (kk == 0))
                    def _(p_ref=p_ref, t=t):
                        tile(p_ref, t).start()

                    @pl.when((blk == b0 + t) & (kk == nk - 1))
                    def _(p_ref=p_ref, t=t):
                        tile(p_ref, t).wait()
                        o_ref[...] += _dot(a2_ref[...], buf2[...], 0, 0)

        if nx:
            pl.when(pos == total - 1)(finish)

    out_spec = pl.BlockSpec((M, PIECE_COLS), lambda blk, kk: (0, blk))
    in_specs = [pl.BlockSpec((tk, M), lambda blk, kk: (kk, 0))] + [ANY] * npc
    args = [a] + [arr for arr, _, _ in blocks]
    scratch = [pltpu.VMEM((2, tk, PIECE_COLS), BF16), pltpu.SemaphoreType.DMA((2,))]
    scratch += _to_chips_scratch(nx) if nx else []
    if npc2:
        in_specs += [pl.BlockSpec(a2.shape, lambda blk, kk: (0, 0))] + [ANY] * npc2
        args += [a2] + [arr for arr, _, _ in blocks2]
        scratch += [pltpu.VMEM((a2.shape[0], PIECE_COLS), BF16), pltpu.SemaphoreType.DMA((1,))]
    out = pl.pallas_call(
        body, name=name,
        grid=grid,
        in_specs=in_specs + [ANY] * nx,
        out_specs=[out_spec] + [ANY] * nx,
        out_shape=[jax.ShapeDtypeStruct((M, D_IN), F32)] + _to_chips_shapes(to_chips),
        scratch_shapes=scratch,
        compiler_params=_params("arbitrary", "arbitrary"),
    )(*args, *to_chips)
    return out if nx else out[0]


def dhx_normbwd(pieces, w, x, dx_res, nw, sc, name, to_chips=()):
    L = x.shape[0]
    tm = min(PROJ_ROWS, L)
    blocks = _piece_blocks(pieces)
    grid = (L // tm, N_PIECE_BLOCKS)
    nx, npc = len(to_chips), len(blocks)

    def body(*refs):
        p_refs = refs[:npc]
        w_ref, x_ref, res_ref, nw_ref, sc_ref = refs[npc:npc + 5]
        refs = refs[npc + 5:]
        dx_ref, sums_ref = refs[nx:nx + 2]
        acc, buf, sems = refs[2 * nx + 2:2 * nx + 5]
        pos, total = _grid_step(grid)
        if nx:
            start, finish = _to_chips_phases(refs[:nx], refs[nx + 2:2 * nx + 2], *refs[2 * nx + 5:])
            pl.when(pos == 0)(start)
            pl.when(pos == total - 1)(finish)
        here, landed = _piece_feed(
            p_refs, blocks, buf, sems,
            lambda s: (s % N_PIECE_BLOCKS, pl.ds(pl.multiple_of((s // N_PIECE_BLOCKS) * tm, tm), tm)), pos, total)
        i, blk = pl.program_id(0), pl.program_id(1)

        @pl.when((i == 0) & (blk == 0))
        def _():
            sums_ref[...] = jnp.zeros_like(sums_ref)

        @pl.when(blk == 0)
        def _():
            acc[...] = jnp.zeros_like(acc)

        @pl.when(here)
        def _():
            acc[...] += _dot(landed()[...], w_ref[...], 1, 1)

        @pl.when(blk == N_PIECE_BLOCKS - 1)
        def _():
            dhx = acc[...]
            xv = x_ref[...]
            r = lax.rsqrt(_lanemean(xv * xv) + EPS)
            n0 = xv * r
            nw = nw_ref[...]
            dn = dhx * (1.0 + sc_ref[...])
            dn0 = dn * nw
            dx_ref[...] = res_ref[...] + r * (dn0 - n0 * _lanemean(dn0 * n0))
            sums_ref[0:1, :] += _rowsum(dhx)
            sums_ref[1:2, :] += _rowsum(dhx * n0 * nw)
            sums_ref[2:3, :] += _rowsum(dn * n0)

    row = pl.BlockSpec((tm, D), lambda i, blk: (i, 0))
    vec = pl.BlockSpec((1, D), lambda i, blk: (0, 0))
    return pl.pallas_call(
        body, name=name,
        grid=grid,
        in_specs=[ANY] * npc + [pl.BlockSpec((D, PIECE_COLS), lambda i, blk: (0, blk)), row, row, vec, vec] + [ANY] * nx,
        out_specs=[row, pl.BlockSpec((8, D), lambda i, blk: (0, 0))] + [ANY] * nx,
        out_shape=[jax.ShapeDtypeStruct((L, D), F32), jax.ShapeDtypeStruct((8, D), F32)] + _to_chips_shapes(to_chips),
        scratch_shapes=[pltpu.VMEM((tm, D), F32), pltpu.VMEM((2, tm, PIECE_COLS), BF16), pltpu.SemaphoreType.DMA((2,))]
        + (_to_chips_scratch(nx) if nx else []),
        compiler_params=_params("arbitrary", "arbitrary"),
    )(*[arr for arr, _, _ in blocks], w, x, dx_res, nw, sc, *to_chips)


SMALL_ROWS = 24


def _rope_tables(L):
    rows = L // 64
    freqs = 10000.0 ** (-jnp.arange(RT_DK // 4, dtype=F32) / (RT_DK // 4))
    a_row = jnp.arange(rows, dtype=F32)[:, None] * freqs
    a_col = jnp.arange(64, dtype=F32)[:, None] * freqs

    def spread(f):
        return jnp.concatenate([jnp.repeat(f(a_row), 64, axis=0), jnp.tile(f(a_col), (rows, 1))], axis=-1)

    cos, sin = spread(jnp.cos), spread(jnp.sin)
    return jnp.concatenate([cos, cos], axis=1), jnp.concatenate([-sin, sin], axis=1)


def _pieces(hq, hf_f, hf_b, hi, hg, rq, rk, rv, rg, ga, gb):
    widths = (D, D, D, D, D, D, D, 2 * D, 2 * D, D, D)
    return list(zip((hq, hf_f, hf_b, hi, hg, rq, rk, rv, rg, ga, gb), widths))


def _lane0(a):
    return a[:, 0, 0]


def _pack_small(rows):
    out = [r.reshape(1, D) for r in rows]
    out += [jnp.zeros((1, D), F32)] * (SMALL_ROWS - len(out))
    return jnp.concatenate(out, axis=0)


def _other_half(g, core):
    axis = g.ndim - 2
    h = g.shape[axis] // 2
    return lax.dynamic_slice_in_dim(g, (1 - core) * h, h, axis=axis).astype(BF16)


def _sibling_sums(gs, names, place):
    core, core_arg, _ = place
    received = rs_to_sibling([_other_half(g, core) for g in gs], "rs_to_sibling_" + names[0])
    return [rs_add_sibling(g, r, core_arg, "rs_add_sibling_" + k) for g, r, k in zip(gs, received, names)]


def _staged_in_proj(x, nw, sh, sc, w_shard, rest, chip):
    cx, cy = chip // 2, chip % 2

    def arg(k):
        return jnp.reshape(k, (1,)).astype(jnp.int32)

    p, hx, w_full = in_proj_own(x, nw, sh, sc, w_shard, arg(chip), "in_proj_own")
    p, w_full = in_proj_next(hx, w_full, arg(2 * (1 - cx) + cy), p, "in_proj_x", diag_from=w_shard)
    w_pa, w_pb, w_out, w_wd = rest[0], rest[1], rest[2], rest[5]
    p, g_pa, g_pb, g_out, g_wd = in_proj_next(hx, w_full, arg(2 * cx + 1 - cy), p, "in_proj_y",
                                              gather=[w_pa, w_pb, w_out, w_wd])
    p, g_wg, g_wu = in_proj_next(hx, w_full, arg(3 - chip), p, "in_proj_diag", gather=[rest[3], rest[4]])
    w = {"w_in": w_full, "w_pa": g_pa.reshape(D, D), "w_pb": g_pb.reshape(2 * D, D), "w_out": g_out.reshape(D, D),
         "wg": g_wg, "wu": g_wu, "wd": g_wd}
    return p, hx, w


def local_step(x, ctx, target, mod_x, mod_c, lb_f, lb_b, lg_f, lg_b, nw1, nw2, hgw, fw, w, rest=None, place=None):
    L, Lc = x.shape[0], ctx.shape[0]
    sh1, sc1, g1, sh2, sc2, g2 = (mod_x[i:i + 1] for i in range(6))
    sh1c, sc1c = mod_c[0:1], mod_c[1:2]
    cosf, sinf = _rope_tables(L)
    cosc, sinc = jnp.ones((Lc, RT_DK), F32), jnp.zeros((Lc, RT_DK), F32)
    zero_h = jnp.zeros((HEADS, HG_D, HG_D), F32)
    zero_r = jnp.zeros((HEADS, RT_DV, RT_DK), F32)

    if rest is None:
        p, hx = normmod_matmul(x, nw1, sh1, sc1, w["w_in"], "in_proj")
    else:
        p, hx, w = _staged_in_proj(x, nw1, sh1, sc1, w["w_in_shard"], rest, place[2][0])
    pc, hxc = normmod_matmul(ctx, nw1, sh1c, sc1c, w["w_in"], "ctx_in_proj")
    _, s_hf, cb_hf = hgrn_scan_fwd(pc, lb_f, zero_h, COL_HFF, False, "ctx_hgrn_f")
    _, s_hb, cb_hb = hgrn_scan_fwd(pc, lb_b, zero_h, COL_HFB, True, "ctx_hgrn_b")
    _, s_rf, cb_rf = ret_scan_fwd(pc, cosc, sinc, lg_f, zero_r, False, "ctx_ret_f")
    _, s_rb, cb_rb = ret_scan_fwd(pc, cosc, sinc, lg_b, zero_r, True, "ctx_ret_b")
    ohf, _, xb_hf = hgrn_scan_fwd(p, lb_f, s_hf, COL_HFF, False, "hgrn_f")
    o_hg, _, xb_hb = hgrn_scan_fwd(p, lb_b, s_hb, COL_HFB, True, "hgrn_b", prev=ohf)
    orf, _, xb_rf = ret_scan_fwd(p, cosf, sinf, lg_f, s_rf, False, "ret_f")
    o_rt, _, xb_rb = ret_scan_fwd(p, cosf, sinf, lg_b, s_rb, True, "ret_b", prev=orf)
    x1, x_mix, merged, ya, yb = mix_fwd(o_hg, o_rt, p, x, g1, hgw, w["w_pa"], w["w_pb"], w["w_out"], "mix_fwd")
    hx2, gg, uu, hh, ff, dx2, sums_f = ffn_fwd(x1, target, nw2, sh2, sc2, g2, fw, w["wg"], w["wu"], w["wd"], "ffn_fwd")

    d_f, d_g, d_u, dx1, sums_fb = ffn_bwd(dx2, x1, ff, gg, uu, nw2, sc2, g2, w["wg"], w["wu"], w["wd"], "ffn_bwd")
    dw_gate, dw_up = matmul_tn_pair(hx2, d_g, d_u, "dw_ffn_gate_up")
    grads = {"wg": dw_gate, "wu": dw_up, "wd": matmul_tn(hh, d_f[None], "dw_ffn_down")}
    ffn_names = ["wg", "wu", "wd"]
    swap = () if place is None else [_other_half(grads[k], place[0]) for k in ffn_names]
    dxm, d_a, d_b, dga, dgb, dhg, drg, dohg, dort, sums_m, *from_sibling = mix_bwd(
        dx1, x_mix, ya, yb, o_hg, o_rt, p, g1, hgw, w["w_pa"], w["w_pb"], w["w_out"], "mix_bwd", to_sibling=swap)
    grads["w_out"] = matmul_tn(merged[None], dxm[None], "dw_out").reshape(N_SHARD, D // N_SHARD, D)
    grads["w_pa"] = matmul_tn(ya[None], d_a[None], "dw_proj_hgrn").reshape(N_SHARD, D // N_SHARD, D)
    grads["w_pb"] = matmul_tn(yb[None], d_b[None], "dw_proj_ret").reshape(N_SHARD, 2 * D // N_SHARD, D)

    rq1, rk1, rv1, dlgf_x, ds_rf = ret_scan_bwd(p, cosf, sinf, lg_f, xb_rf, dort, zero_r, None, False, "ret_f_bwd")
    drq, drk, drv, dlgb_x, ds_rb = ret_scan_bwd(p, cosf, sinf, lg_b, xb_rb, dort, zero_r, (rq1, rk1, rv1), True, "ret_b_bwd")
    hq1, dzf, hv1, dlbf_x, ds_hf = hgrn_scan_bwd(p, lb_f, xb_hf, dohg, zero_h, None, COL_HFF, False, "hgrn_f_bwd")
    dhq, dzb, dhv, dlbb_x, ds_hb = hgrn_scan_bwd(p, lb_b, xb_hb, dohg, zero_h, (hq1, hv1), COL_HFB, True, "hgrn_b_bwd")
    dp = _pieces(dhq, dzf, dzb, dhv, dhg, drq, drk, drv, drg, dga, dgb)

    zc = jnp.zeros((Lc, D), F32)
    zc2 = jnp.zeros((Lc, 2 * D), F32)
    crq1, crk1, crv1, dlgf_c, _ = ret_scan_bwd(pc, cosc, sinc, lg_f, cb_rf, zc2, ds_rf, None, False, "ctx_ret_f_bwd")
    cdrq, cdrk, cdrv, dlgb_c, _ = ret_scan_bwd(pc, cosc, sinc, lg_b, cb_rb, zc2, ds_rb, (crq1, crk1, crv1), True, "ctx_ret_b_bwd")
    chq1, cdzf, chv1, dlbf_c, _ = hgrn_scan_bwd(pc, lb_f, cb_hf, zc, ds_hf, None, COL_HFF, False, "ctx_hgrn_f_bwd")
    cdhq, cdzb, cdhv, dlbb_c, _ = hgrn_scan_bwd(pc, lb_b, cb_hb, zc, ds_hb, (chq1, chv1), COL_HFB, True, "ctx_hgrn_b_bwd")
    dpc = _pieces(cdhq, cdzf, cdzb, cdhv, None, cdrq, cdrk, cdrv, None, None, None)
    _, sums_c = dhx_normbwd(dpc, w["w_in"], ctx, zc, nw1, sc1c, "dctx_in_proj")

    others = ["w_pa", "w_pb", "w_out", "wg", "wu", "wd"]
    if place is None:
        grads["w_in"] = matmul_tn_pieces(hx, dp, "dw_in", extra=(hxc, dpc))
        dx, sums_x = dhx_normbwd(dp, w["w_in"], x, dx1, nw1, sc1, "dx_in_proj")
    else:
        sums_o = _sibling_sums([grads[k] for k in others[:3]], others[:3], place)
        sums_o += [rs_add_sibling(grads[k], r, place[1], "rs_add_sibling_" + k) for k, r in zip(ffn_names, from_sibling)]
        grads["w_in"], *recv_o = matmul_tn_pieces(hx, dp, "dw_in", extra=(hxc, dpc),
                                                  to_chips=[a16 for _, a16 in sums_o])
        sums_i = _sibling_sums([grads["w_in"]], ["w_in"], place)
        dx, sums_x, recv_i = dhx_normbwd(dp, w["w_in"], x, dx1, nw1, sc1, "dx_in_proj", to_chips=[sums_i[0][1]])
        names = ["w_in"] + others
        halves = [rs_add_chips(a, r, place[2], "rs_add_chips_" + k)
                  for (a, _), r, k in zip(sums_i + sums_o, [recv_i] + recv_o, names)]
        grads = dict(zip(names, rs_join_halves(halves, "rs_join_halves")))

    def lg_row(f, b):
        return jnp.concatenate([_lane0(f), _lane0(b), jnp.zeros((D - 2 * HEADS,), F32)])

    small = _pack_small([
        sums_x[0], sums_x[1], sums_m[0], sums_fb[1], sums_fb[2], sums_fb[0],
        sums_c[0], sums_c[1],
        sums_x[2], sums_c[2], sums_fb[3], sums_m[1], sums_f[0],
        dlbf_x, dlbf_c, dlbb_x, dlbb_c,
        lg_row(dlgf_x, dlgb_x), lg_row(dlgf_c, dlgb_c),
        sums_f[1],
    ])
    return dx, grads, small


MESH = pl.DeviceIdType.MESH
ANY = pl.BlockSpec(memory_space=pl.ANY)
N_DEV = 8


def _place():
    return lax.axis_index("x"), lax.axis_index("y"), lax.axis_index("c")


def _other_chips(x, y):
    return [(1 - x, y), (x, 1 - y), (1 - x, 1 - y)]


def allgather8(xs, name):
    m, n = xs.shape

    def body(x_ref, out_ref, send_sems, recv_sems, local_sem):
        x, y, c = _place()
        me, sibling = (x, y, c), (x, y, 1 - c)
        chips = _other_chips(x, y)

        def rows(px, py, pc):
            return out_ref.at[pl.ds((4 * px + 2 * py + pc) * m, m), :]

        def copy(k, block, to, src=None):
            return pltpu.make_async_remote_copy(
                src_ref=rows(*block) if src is None else src, dst_ref=rows(*block),
                send_sem=send_sems.at[k], recv_sem=recv_sems.at[k], device_id=to, device_id_type=MESH)

        mine = pltpu.make_async_copy(x_ref, rows(*me), local_sem)
        mine.start()
        first = [copy(0, me, sibling, src=x_ref)]
        first += [copy(1 + j, me, (*chip, c), src=x_ref) for j, chip in enumerate(chips)]
        for cp in first:
            cp.start()
        passed = [copy(4 + j, (*chip, c), sibling) for j, chip in enumerate(chips)]
        for j, chip in enumerate(chips):
            copy(1 + j, (*chip, c), me).wait_recv()
            passed[j].start()
        copy(0, sibling, me).wait_recv()
        for j, chip in enumerate(chips):
            copy(4 + j, (*chip, 1 - c), me).wait_recv()
        for cp in first + passed:
            cp.wait_send()
        mine.wait()

    return pl.pallas_call(
        body, name=name,
        out_shape=jax.ShapeDtypeStruct((N_DEV * m, n), xs.dtype),
        in_specs=[pl.BlockSpec(memory_space=pltpu.VMEM)],
        out_specs=pl.BlockSpec(memory_space=pltpu.VMEM),
        scratch_shapes=[pltpu.SemaphoreType.DMA((7,)), pltpu.SemaphoreType.DMA((7,)), pltpu.SemaphoreType.DMA],
    )(xs)


def _gather_phases(ins, outs, send_sems, recv_sems, local_sems, relations=(0, 1, 2), stage=None):
    n = len(ins)
    x, y, c = _place()
    chips = _other_chips(x, y)

    def rows(i, core):
        h = ins[i].shape[0] // 2
        return pl.ds(pl.multiple_of(core * h, 16), h)

    def region(i, k, rs):
        if len(outs[i].shape) == 2:
            cols = ins[i].shape[1]
            return outs[i].at[rs, pl.ds(pl.multiple_of(k * cols, 128), cols)]
        return outs[i].at[k, rs, :]

    def landed(i, chip, core):
        return region(i, 2 * chip[0] + chip[1], rows(i, core))

    def copy(i, k, src, dst, to):
        return pltpu.make_async_remote_copy(src_ref=src, dst_ref=dst, send_sem=send_sems.at[6 * i + k],
                                            recv_sem=recv_sems.at[6 * i + k], device_id=to, device_id_type=MESH)

    def lift(i):
        return pltpu.make_async_copy(ins[i], stage[i], local_sems.at[i])

    def drop(i):
        return pltpu.make_async_copy(stage[i], region(i, 2 * x + y, pl.ds(0, ins[i].shape[0])), local_sems.at[i])

    def send(i, j):
        return copy(i, j, ins[i].at[rows(i, c), :], landed(i, (x, y), c), (*chips[j], c))

    def arrived(i, j, core, k):
        return copy(i, k, ins[i].at[rows(i, core), :], landed(i, chips[j], core), (x, y, 1 - c))

    def passed(i, j):
        return copy(i, 3 + j, landed(i, chips[j], c), landed(i, chips[j], c), (x, y, 1 - c))

    def start():
        for i in range(n):
            if stage is not None:
                lift(i).start()
            for j in relations:
                send(i, j).start()

    def forward():
        for i in range(n):
            if stage is not None:
                lift(i).wait()
                drop(i).start()
            for j in relations:
                arrived(i, j, c, j).wait_recv()
                passed(i, j).start()

    def finish():
        for i in range(n):
            for j in relations:
                arrived(i, j, 1 - c, 3 + j).wait_recv()
        for i in range(n):
            for j in relations:
                send(i, j).wait_send()
                passed(i, j).wait_send()
            if stage is not None:
                drop(i).wait()

    return start, forward, finish


def _gather_scratch(n):
    return [pltpu.SemaphoreType.DMA((6 * n,)), pltpu.SemaphoreType.DMA((6 * n,)), pltpu.SemaphoreType.DMA((n,))]


def rs_to_sibling(payloads, name):
    n = len(payloads)

    def body(*refs):
        start, finish = _to_sibling_phases(refs[:n], refs[n:2 * n], *refs[2 * n:])
        start()
        finish()

    return pl.pallas_call(
        body, name=name,
        out_shape=[jax.ShapeDtypeStruct(g.shape, g.dtype) for g in payloads],
        in_specs=[ANY] * n, out_specs=[ANY] * n,
        scratch_shapes=_to_sibling_scratch(n),
    )(*payloads)


def _to_sibling_phases(ins, outs, send_sems, recv_sems):
    def copies():
        x, y, c = _place()
        return [pltpu.make_async_remote_copy(src_ref=ins[i], dst_ref=outs[i], send_sem=send_sems.at[i],
                                             recv_sem=recv_sems.at[i], device_id=(x, y, 1 - c), device_id_type=MESH)
                for i in range(len(ins))]

    def start():
        for cp in copies():
            cp.start()

    def finish():
        for cp in copies():
            cp.wait()

    return start, finish


def _to_sibling_scratch(n):
    return [pltpu.SemaphoreType.DMA((n,)), pltpu.SemaphoreType.DMA((n,))]


def _to_chips_phases(ins, outs, send_sems, recv_sems):
    def copies():
        x, y, c = _place()
        return [pltpu.make_async_remote_copy(
            src_ref=ins[i].at[2 * px + py], dst_ref=outs[i].at[j], send_sem=send_sems.at[3 * i + j],
            recv_sem=recv_sems.at[3 * i + j], device_id=(px, py, c), device_id_type=MESH)
            for i in range(len(ins)) for j, (px, py) in enumerate(_other_chips(x, y))]

    def start():
        for cp in copies():
            cp.start()

    def finish():
        for cp in copies():
            cp.wait()

    return start, finish


def _to_chips_shapes(parts):
    return [jax.ShapeDtypeStruct((3,) + a.shape[1:], a.dtype) for a in parts]


def _to_chips_scratch(n):
    return [pltpu.SemaphoreType.DMA((3 * n,)), pltpu.SemaphoreType.DMA((3 * n,))]


def rs_join_halves(fulls, name):
    n = len(fulls)

    def body(*refs):
        outs = refs[n:2 * n]
        send_sems, recv_sems = refs[2 * n:]
        x, y, c = _place()

        def copy(i, core):
            h = fulls[i].shape[0] // 2
            rows = outs[i].at[pl.ds(pl.multiple_of(core * h, 8), h), :]
            return pltpu.make_async_remote_copy(src_ref=rows, dst_ref=rows, send_sem=send_sems.at[i],
                                                recv_sem=recv_sems.at[i], device_id=(x, y, 1 - c), device_id_type=MESH)

        sent = [copy(i, c) for i in range(n)]
        for cp in sent:
            cp.start()
        for i in range(n):
            copy(i, 1 - c).wait_recv()
        for cp in sent:
            cp.wait_send()

    return pl.pallas_call(
        body, name=name,
        out_shape=[jax.ShapeDtypeStruct(a.shape, a.dtype) for a in fulls],
        in_specs=[ANY] * n, out_specs=[ANY] * n,
        input_output_aliases={i: i for i in range(n)},
        scratch_shapes=[pltpu.SemaphoreType.DMA((n,)), pltpu.SemaphoreType.DMA((n,))],
    )(*fulls)


def _row_tile(rows, cols, limit_bytes=2 * 1024 * 1024, mult=8):
    best = mult
    for t in range(mult, rows + 1, mult):
        if rows % t == 0 and t * cols * 4 <= limit_bytes:
            best = t
    return best


def rs_add_sibling(g, recv, c, name):
    if g.ndim == 2:
        h, C = recv.shape[0], recv.shape[1] // N_SHARD
    else:
        _, h, C = recv.shape
    tr = _row_tile(h, C, mult=16)
    nt = h // tr

    def body(c_ref, g_ref, r_ref, o_ref, o16_ref):
        s = g_ref[...] + r_ref[...].astype(F32)
        o_ref[...] = s
        o16_ref[...] = s.astype(BF16)

    blk = pl.BlockSpec((None, tr, C), lambda k, i, c_ref: (k, i, 0))
    if g.ndim == 2:
        g_spec = pl.BlockSpec((tr, C), lambda k, i, c_ref: (c_ref[0] * nt + i, k))
        r_spec = pl.BlockSpec((tr, C), lambda k, i, c_ref: (i, k))
    else:
        g_spec = pl.BlockSpec((None, tr, C), lambda k, i, c_ref: (k, c_ref[0] * nt + i, 0))
        r_spec = blk
    return pl.pallas_call(
        body, name=name,
        grid_spec=pltpu.PrefetchScalarGridSpec(
            num_scalar_prefetch=1, grid=(N_SHARD, nt),
            in_specs=[g_spec, r_spec],
            out_specs=[blk, blk]),
        out_shape=[jax.ShapeDtypeStruct((N_SHARD, h, C), F32), jax.ShapeDtypeStruct((N_SHARD, h, C), BF16)],
        compiler_params=_params("parallel", "parallel"),
    )(c, g, recv)


def rs_add_chips(part, recv, place, name):
    _, h, C = part.shape
    tr = _row_tile(h, C, mult=16)
    nt = h // tr

    def body(k_ref, p_ref, r_ref, o_ref):
        o_ref[...] = ((p_ref[...] + r_ref[0].astype(F32)) + r_ref[1].astype(F32)) + r_ref[2].astype(F32)

    return pl.pallas_call(
        body, name=name,
        grid_spec=pltpu.PrefetchScalarGridSpec(
            num_scalar_prefetch=1, grid=(nt,),
            in_specs=[pl.BlockSpec((None, tr, C), lambda i, k_ref: (k_ref[0], i, 0)),
                      pl.BlockSpec((3, tr, C), lambda i, k_ref: (0, i, 0))],
            out_specs=pl.BlockSpec((tr, C), lambda i, k_ref: (k_ref[1] * nt + i, 0))),
        out_shape=jax.ShapeDtypeStruct((2 * h, C), F32),
        compiler_params=_params("parallel"),
    )(place, part, recv)


def _adamw_math(w, g, m, v):
    m = ADAM_B1 * m + (1.0 - ADAM_B1) * g
    v = ADAM_B2 * v + (1.0 - ADAM_B2) * (g * g)
    m_hat = m / (1.0 - ADAM_B1 ** ADAM_STEP)
    v_hat = v / (1.0 - ADAM_B2 ** ADAM_STEP)
    delta = -ADAM_LR * (m_hat / (jnp.sqrt(v_hat) + ADAM_EPS) + ADAM_WD * w)
    return delta, m, v


def adamw(w, g, m, v, name):
    R, C = w.shape
    tr = _row_tile(R, C, 1024 * 1024)

    def body(w_ref, g_ref, m_ref, v_ref, go_ref, d_ref, nm_ref, nv_ref):
        g = g_ref[...]
        go_ref[...] = g
        d_ref[...], nm_ref[...], nv_ref[...] = _adamw_math(w_ref[...], g, m_ref[...], v_ref[...])

    blk = pl.BlockSpec((tr, C), lambda i: (i, 0))
    return pl.pallas_call(
        body, name=name, grid=(R // tr,), in_specs=[blk] * 4, out_specs=[blk] * 4,
        out_shape=[jax.ShapeDtypeStruct((R, C), F32)] * 4,
        compiler_params=_params("parallel"),
    )(w, g, m, v)


MOD_SH = 6 * D // N_SHARD
PK_ROWS = 16


def mod_fwd(call16, w_sh, b_sh, name):
    def body(c_ref, w_ref, b_ref, o_ref):
        o_ref[...] = _dot(_silu_parts(c_ref[...])[0], w_ref[...], prec=HI) + b_ref[...]

    return pl.pallas_call(body, name=name, out_shape=jax.ShapeDtypeStruct((16, MOD_SH), F32),
                          compiler_params=_params())(call16, w_sh, b_sh)


def prep_small(lbf2, lbb2, theta_row, name):
    def body(f_ref, b_ref, t_ref, lbf_ref, lbb_ref, lg_ref):
        lbf_ref[...] = _sigmoid(f_ref[0:1, :] - f_ref[1:2, :])
        lbb_ref[...] = _sigmoid(b_ref[0:1, :] - b_ref[1:2, :])
        t = t_ref[...]
        lg_ref[...] = jnp.minimum(t, 0.0) - jnp.log(1.0 + jnp.exp(-jnp.abs(t)))

    row = jax.ShapeDtypeStruct((1, D), F32)
    return pl.pallas_call(body, name=name, out_shape=[row, row, row], compiler_params=_params())(lbf2, lbb2, theta_row)


def small_grads(g3, lbf, lbb, theta_row, name):
    def body(g_ref, lbf_ref, lbb_ref, t_ref, pk_ref, aux_ref):
        s = g_ref[0]
        for d in range(1, N_DEV):
            s = s + g_ref[d]
        pk_ref[...] = jnp.zeros_like(pk_ref)
        aux_ref[...] = jnp.zeros_like(aux_ref)
        pk_ref[1:7, :] = s[0:6]
        pk_ref[1:3, :] += s[6:8]
        pk_ref[7:8, :] = s[8:9] + s[9:10]
        pk_ref[8:9, :] = s[10:11]
        lbf, lbb = lbf_ref[...], lbb_ref[...]
        daf = (s[13:14] + s[14:15]) * lbf * (1.0 - lbf)
        dab = (s[15:16] + s[16:17]) * lbb * (1.0 - lbb)
        pk_ref[9:10, :] = daf
        pk_ref[10:11, :] = -daf
        pk_ref[11:12, :] = dab
        pk_ref[12:13, :] = -dab
        pk_ref[13:14, :] = s[11:12]
        pk_ref[14:15, :] = (s[17:18] + s[18:19]) * _sigmoid(-t_ref[...])
        pk_ref[15:16, :] = s[12:13]
        aux_ref[0:2, :] = s[6:8]
        aux_ref[2:3, :] = jnp.broadcast_to(jnp.sum(s[19:20], axis=-1, keepdims=True), (1, D))

    return pl.pallas_call(body, name=name,
                          out_shape=[jax.ShapeDtypeStruct((PK_ROWS, D), F32), jax.ShapeDtypeStruct((8, D), F32)],
                          compiler_params=_params())(g3, lbf, lbb, theta_row)


def mod_bwd(call16, dmod_sh, w_sh, name):
    def body(c_ref, d_ref, w_ref, dw_ref, ds_ref):
        dm = d_ref[...]
        dw_ref[...] = _dot(_silu_parts(c_ref[...])[0], dm, 0, 0, prec=HI)
        ds_ref[...] = jnp.zeros_like(ds_ref)
        ds_ref[0:1, :] = _dot(dm[8:9, :], w_ref[...], 1, 1, prec=HI)

    return pl.pallas_call(body, name=name,
                          out_shape=[jax.ShapeDtypeStruct((D, MOD_SH), F32), jax.ShapeDtypeStruct((8, D), F32)],
                          compiler_params=_params())(call16, dmod_sh, w_sh)


def adamw_small(g4, pk_g, pk_w, pk_m, pk_v, name):
    def body(g4_ref, g_ref, w_ref, m_ref, v_ref, go_ref, d_ref, nm_ref, nv_ref):
        w = w_ref[...]
        ds = ((g4_ref[0:1, :] + g4_ref[16:17, :]) + g4_ref[32:33, :]) + g4_ref[48:49, :]
        row = lax.broadcasted_iota(jnp.int32, (PK_ROWS, D), 0)
        g = jnp.where(row == 0, ds * _silu_parts(w[0:1, :])[1], g_ref[...])
        go_ref[...] = g
        d_ref[...], nm_ref[...], nv_ref[...] = _adamw_math(w, g, m_ref[...], v_ref[...])

    pk = jax.ShapeDtypeStruct((PK_ROWS, D), F32)
    return pl.pallas_call(body, name=name, out_shape=[pk, pk, pk, pk], compiler_params=_params())(g4, pk_g, pk_w, pk_m, pk_v)


def _pack_params(c_ctx, b_mod, n1, n2, lbf, lbb, hgn, th_f, th_b, fin):
    theta = jnp.concatenate([th_f.reshape(HEADS), th_b.reshape(HEADS), jnp.zeros((D - 2 * HEADS,), F32)])
    return jnp.concatenate([c_ctx.reshape(1, D), b_mod.reshape(6, D), n1.reshape(1, D), n2.reshape(1, D), lbf, lbb,
                            hgn.reshape(1, D), theta.reshape(1, D), fin.reshape(1, D)], axis=0)


def _unpack_params(pk):
    return (pk[0], pk[1:7].reshape(1, 6 * D), pk[7:8], pk[8:9], pk[9:11], pk[11:13], pk[13:14],
            pk[14, 0:HEADS].reshape(1, HEADS), pk[14, HEADS:2 * HEADS].reshape(1, HEADS), pk[15])


def kernel(x, c, ctx, c_ctx, w_mod, b_mod, norm1_w, norm2_w, w_in, hg_lb_fwd, hg_lb_bwd, hg_norm_w, rt_theta_fwd, rt_theta_bwd, w_proj_hgrn, w_proj_ret, w_out, w_ffn_gate, w_ffn_up, w_ffn_down, final_norm_w, loss_target, m_c_ctx, m_w_mod, m_b_mod, m_norm1_w, m_norm2_w, m_w_in, m_hg_lb_fwd, m_hg_lb_bwd, m_hg_norm_w, m_rt_theta_fwd, m_rt_theta_bwd, m_w_proj_hgrn, m_w_proj_ret, m_w_out, m_w_ffn_gate, m_w_ffn_up, m_w_ffn_down, m_final_norm_w, v_c_ctx, v_w_mod, v_b_mod, v_norm1_w, v_norm2_w, v_w_in, v_hg_lb_fwd, v_hg_lb_bwd, v_hg_norm_w, v_rt_theta_fwd, v_rt_theta_bwd, v_w_proj_hgrn, v_w_proj_ret, v_w_out, v_w_ffn_gate, v_w_ffn_up, v_w_ffn_down, v_final_norm_w):
    xi, yi, ci = _place()
    dev = 4 * xi + 2 * yi + ci
    chip = 2 * xi + yi
    core_arg = jnp.reshape(ci, (1,)).astype(jnp.int32)
    place_arg = jnp.stack([chip, ci]).astype(jnp.int32)

    c_all = allgather8(jnp.concatenate([c, jnp.zeros((7, D), F32)], axis=0), "gather_c").reshape(N_DEV, 8, D)[:, 0]
    call16 = jnp.concatenate([c_all, c_ctx.reshape(1, D), jnp.zeros((7, D), F32)], axis=0)
    b_sh = lax.dynamic_slice_in_dim(b_mod, chip * MOD_SH, MOD_SH, axis=1)
    mod_sh = mod_fwd(call16, w_mod[0], b_sh, "mod_fwd")
    mod_g = allgather8(mod_sh, "gather_mod").reshape(N_DEV, 16, MOD_SH)
    mod_all = jnp.concatenate([mod_g[0], mod_g[2], mod_g[4], mod_g[6]], axis=1)
    mod_x = lax.dynamic_index_in_dim(mod_all, dev, axis=0, keepdims=False).reshape(6, D)
    mod_c = mod_all[8].reshape(6, D)

    pk_w = _pack_params(c_ctx, b_mod, norm1_w, norm2_w, hg_lb_fwd, hg_lb_bwd, hg_norm_w, rt_theta_fwd, rt_theta_bwd, final_norm_w)
    theta_row = pk_w[14:15]
    lb_f, lb_b, lg_row = prep_small(hg_lb_fwd, hg_lb_bwd, theta_row, "prep_small")
    lg_f = jnp.broadcast_to(lg_row[0, 0:HEADS].reshape(HEADS, 1, 1), (HEADS, 1, RT_DV))
    lg_b = jnp.broadcast_to(lg_row[0, HEADS:2 * HEADS].reshape(HEADS, 1, 1), (HEADS, 1, RT_DV))

    rest = [s[0].astype(BF16) for s in (w_proj_hgrn, w_proj_ret, w_out, w_ffn_gate, w_ffn_up, w_ffn_down)]

    dx, full, small = local_step(x[0], ctx[0], loss_target[0], mod_x, mod_c, lb_f, lb_b, lg_f, lg_b,
                                 norm1_w, norm2_w, hg_norm_w, final_norm_w.reshape(1, D),
                                 {"w_in_shard": w_in[0].astype(BF16)}, rest, (ci, core_arg, place_arg))

    g3 = allgather8(small, "gather_small").reshape(N_DEV, SMALL_ROWS, D)
    pk_g, aux = small_grads(g3, lb_f, lb_b, theta_row, "small_grads")
    loss = aux[2, 0]
    dmod16 = jnp.concatenate([
        g3[:, 0:6, :].reshape(N_DEV, 6 * D),
        jnp.concatenate([aux[0], aux[1], jnp.zeros((4 * D,), F32)]).reshape(1, 6 * D),
        jnp.zeros((7, 6 * D), F32)], axis=0)
    dmod_sh = lax.dynamic_slice_in_dim(dmod16, chip * MOD_SH, MOD_SH, axis=1)
    g_wmod, dsilu = mod_bwd(call16, dmod_sh, w_mod[0], "mod_bwd")
    g4 = allgather8(dsilu, "gather_dsilu")
    pk_m = _pack_params(m_c_ctx, m_b_mod, m_norm1_w, m_norm2_w, m_hg_lb_fwd, m_hg_lb_bwd, m_hg_norm_w, m_rt_theta_fwd, m_rt_theta_bwd, m_final_norm_w)
    pk_v = _pack_params(v_c_ctx, v_b_mod, v_norm1_w, v_norm2_w, v_hg_lb_fwd, v_hg_lb_bwd, v_hg_norm_w, v_rt_theta_fwd, v_rt_theta_bwd, v_final_norm_w)
    pk_g, pk_d, pk_nm, pk_nv = adamw_small(g4, pk_g, pk_w, pk_m, pk_v, "adamw_small")

    big = {
        "w_mod": (g_wmod, w_mod, m_w_mod, v_w_mod),
        "w_in": (full["w_in"], w_in, m_w_in, v_w_in),
        "w_pa": (full["w_pa"], w_proj_hgrn, m_w_proj_hgrn, v_w_proj_hgrn),
        "w_pb": (full["w_pb"], w_proj_ret, m_w_proj_ret, v_w_proj_ret),
        "w_out": (full["w_out"], w_out, m_w_out, v_w_out),
        "wg": (full["wg"], w_ffn_gate, m_w_ffn_gate, v_w_ffn_gate),
        "wu": (full["wu"], w_ffn_up, m_w_ffn_up, v_w_ffn_up),
        "wd": (full["wd"], w_ffn_down, m_w_ffn_down, v_w_ffn_down),
    }
    res = {}
    for k, (g, wt, mt, vt) in big.items():
        res[k] = tuple(a[None] for a in adamw(wt[0], g, mt[0], vt[0], "adamw_" + k))

    sm = [_unpack_params(p) for p in (pk_g, pk_d, pk_nm, pk_nv)]
    outs = []
    for t in range(4):
        (s_cctx, s_bmod, s_n1, s_n2, s_lbf, s_lbb, s_hgn, s_thf, s_thb, s_fin) = sm[t]
        outs.append([s_cctx, res["w_mod"][t], s_bmod, s_n1, s_n2, res["w_in"][t], s_lbf, s_lbb, s_hgn, s_thf, s_thb,
                     res["w_pa"][t], res["w_pb"][t], res["w_out"][t], res["wg"][t], res["wu"][t], res["wd"][t], s_fin])
    return (loss, dx[None], *outs[0], *outs[1], *outs[2], *outs[3])
```

```python
import functools

import jax
import jax.numpy as jnp
from jax import lax
from jax.experimental import pallas as pl
from jax.experimental.pallas import tpu as pltpu

F32 = jnp.float32
BF16 = jnp.bfloat16
HI = lax.Precision.HIGHEST
CUMSUM_PRECISION = lax.Precision.HIGH

D = 1024
HEADS = 8
HG_D = 128
RT_DK = 128
RT_DV = 256
D_FF = 2816
D_IN = 13312
N_SHARD = 4
IN_SH = D_IN // N_SHARD
FF_SH = D_FF // N_SHARD
HG_CHUNK = 32
SCAN_ROWS = 256
HG_GROUP = 8
RT_GROUP = 4
PROJ_ROWS = 1024
EPS = 1e-6
GN_EPS = 1e-5
Q_SCALE = 128.0 ** -0.5
VMEM_LIMIT = 56 * 1024 * 1024

COL_HQ, COL_HFF, COL_HFB, COL_HI, COL_HG = 0, 8, 16, 24, 32
COL_RQ, COL_RK, COL_RV, COL_RG, COL_GA, COL_GB = 40, 48, 56, 72, 88, 96

ADAM_LR, ADAM_B1, ADAM_B2, ADAM_EPS, ADAM_WD, ADAM_STEP = 0.001, 0.9, 0.999, 1e-08, 0.01, 10


def _params(*sem):
    return pltpu.CompilerParams(dimension_semantics=sem, vmem_limit_bytes=VMEM_LIMIT)


def _dot(a, b, ca=1, cb=0, prec=None):
    return lax.dot_general(a, b, (((ca,), (cb,)), ((), ())), precision=prec, preferred_element_type=F32)


def _bdot(a, b, ca=1, cb=0):
    return _dot(a.astype(BF16), b.astype(BF16), ca, cb)


def _sigmoid(z):
    return 1.0 / (1.0 + jnp.exp(-z))


def _rowsum(a):
    return jnp.sum(a, axis=0, keepdims=True)


def _lanemean(a):
    return jnp.mean(a, axis=-1, keepdims=True)


def _grid_step(grid):
    pos, total = 0, 1
    for d, size in enumerate(grid):
        pos = pos * size + pl.program_id(d)
        total *= size
    return pos, total


def normmod_matmul(x, nw, sh, sc, w, name):
    L = x.shape[0]
    tm = min(PROJ_ROWS, L)
    tn = IN_SH // 2

    def body(x_ref, nw_ref, sh_ref, sc_ref, w_ref, p_ref, hx_ref, p16_ref, hx_scr):
        @pl.when(pl.program_id(1) == 0)
        def _():
            xv = x_ref[...]
            n = xv * lax.rsqrt(_lanemean(xv * xv) + EPS) * nw_ref[...]
            h = (n * (1.0 + sc_ref[...]) + sh_ref[...]).astype(BF16)
            hx_scr[...] = h
            hx_ref[...] = h

        acc = _dot(hx_scr[...], w_ref[...])
        p_ref[...] = acc
        p16_ref[...] = acc.astype(BF16)

    vec = pl.BlockSpec((1, D), lambda i, j: (0, 0))
    return pl.pallas_call(
        body, name=name,
        grid=(L // tm, D_IN // tn),
        in_specs=[pl.BlockSpec((tm, D), lambda i, j: (i, 0)), vec, vec, vec,
                  pl.BlockSpec((D, tn), lambda i, j: (0, j))],
        out_specs=[pl.BlockSpec((tm, tn), lambda i, j: (i, j)), pl.BlockSpec((tm, D), lambda i, j: (i, 0)),
                   pl.BlockSpec((tm, tn), lambda i, j: (i, j))],
        out_shape=[jax.ShapeDtypeStruct((L, D_IN), F32), jax.ShapeDtypeStruct((L, D), BF16),
                   jax.ShapeDtypeStruct((L, D_IN), BF16)],
        scratch_shapes=[pltpu.VMEM((tm, D), BF16)],
        compiler_params=_params("parallel", "arbitrary"),
    )(x, nw, sh, sc, w)


def _w_halves(w_src, col0, tn, wbuf, wsems, pos):
    @pl.when(pos == 0)
    def _():
        for h in range(2):
            pltpu.make_async_copy(w_src.at[:, pl.ds(pl.multiple_of(col0 + h * tn, 128), tn)], wbuf.at[h],
                                  wsems.at[h]).start()

    for h in range(2):
        @pl.when(pos == h)
        def _(h=h):
            pltpu.make_async_copy(w_src.at[:, pl.ds(0, tn)], wbuf.at[h], wsems.at[h]).wait()


def in_proj_own(x, nw, sh, sc, w_shard, shard_arg, name):
    L = x.shape[0]
    tm = min(PROJ_ROWS, L)
    tn = IN_SH // 2
    grid = (L // tm, 2)

    def body(k_ref, x_ref, nw_ref, sh_ref, sc_ref, w_ref, p_ref, p16_ref, hx_ref, wfull_ref, hx_scr, wbuf, wsems,
             psems, *sems):
        pos, total = _grid_step(grid)
        start, forward, finish = _gather_phases([w_ref], [wfull_ref], *sems, relations=(0, 1))
        pl.when(pos == 0)(start)
        _w_halves(w_ref, 0, tn, wbuf, wsems, pos)

        def place(h):
            col = pl.multiple_of(k_ref[0] * IN_SH + h * tn, 128)
            return pltpu.make_async_copy(wbuf.at[h], wfull_ref.at[:, pl.ds(col, tn)], psems.at[h])

        for h in range(2):
            @pl.when(pos == h)
            def _(h=h):
                place(h).start()

        @pl.when(pl.program_id(1) == 0)
        def _():
            xv = x_ref[...]
            n = xv * lax.rsqrt(_lanemean(xv * xv) + EPS) * nw_ref[...]
            h = (n * (1.0 + sc_ref[...]) + sh_ref[...]).astype(BF16)
            hx_scr[...] = h
            hx_ref[...] = h

        acc = _dot(hx_scr[...], wbuf[pl.program_id(1)])
        p_ref[...] = acc
        p16_ref[...] = acc.astype(BF16)

        @pl.when(pos == total - 1)
        def _():
            forward()
            finish()
            place(0).wait()
            place(1).wait()

    vec = pl.BlockSpec((1, D), lambda i, j, k: (0, 0))
    return pl.pallas_call(
        body, name=name,
        grid_spec=pltpu.PrefetchScalarGridSpec(
            num_scalar_prefetch=1, grid=grid,
            in_specs=[pl.BlockSpec((tm, D), lambda i, j, k: (i, 0)), vec, vec, vec, ANY],
            out_specs=[pl.BlockSpec((tm, tn), lambda i, j, k: (i, 2 * k[0] + j)),
                       pl.BlockSpec((tm, tn), lambda i, j, k: (i, 2 * k[0] + j)),
                       pl.BlockSpec((tm, D), lambda i, j, k: (i, 0)), ANY],
            scratch_shapes=[pltpu.VMEM((tm, D), BF16), pltpu.VMEM((2, D, tn), BF16), pltpu.SemaphoreType.DMA((2,)),
                            pltpu.SemaphoreType.DMA((2,))] + _gather_scratch(1)),
        out_shape=[jax.ShapeDtypeStruct((L, D_IN), F32), jax.ShapeDtypeStruct((L, D_IN), BF16),
                   jax.ShapeDtypeStruct((L, D), BF16),
                   jax.ShapeDtypeStruct((D, D_IN), BF16)],
        compiler_params=_params("arbitrary", "arbitrary"),
    )(shard_arg, x, nw, sh, sc, w_shard)


def in_proj_next(hx, w_full, shard_arg, p, name, diag_from=None, gather=()):
    L = hx.shape[0]
    tm = min(PROJ_ROWS, L)
    tn = IN_SH // 2
    grid = (L // tm, 2)
    diag = diag_from is not None
    ng = len(gather)
    assert not (diag and ng)

    def body(k_ref, hx_ref, wf_in, p_in, p16_in, *refs):
        n_src = 1 if diag else ng
        srcs = refs[:n_src]
        p_ref, p16_ref = refs[n_src], refs[n_src + 1]
        dsts = refs[n_src + 2:2 * n_src + 2]
        wbuf, wsems = refs[2 * n_src + 2:2 * n_src + 4]
        sems = refs[2 * n_src + 4:2 * n_src + 7]
        stage = refs[2 * n_src + 7:]
        pos, total = _grid_step(grid)
        w_src = dsts[0] if diag else wf_in
        if diag:
            start, forward, finish = _gather_phases(srcs, dsts, *sems, relations=(2,))
        elif ng:
            start, forward, finish = _gather_phases(srcs, dsts, *sems, stage=stage)
        if n_src:
            pl.when(pos == 0)(start)
        _w_halves(w_src, k_ref[0] * IN_SH, tn, wbuf, wsems, pos)
        acc = _dot(hx_ref[...], wbuf[pl.program_id(1)])
        p_ref[...] = acc
        p16_ref[...] = acc.astype(BF16)
        if n_src:
            @pl.when(pos == total - 1)
            def _():
                forward()
                finish()

    srcs = [diag_from] if diag else list(gather)
    out_shape = [jax.ShapeDtypeStruct((L, D_IN), F32), jax.ShapeDtypeStruct((L, D_IN), BF16)]
    if diag:
        out_shape.append(jax.ShapeDtypeStruct(w_full.shape, w_full.dtype))
    out_shape += [jax.ShapeDtypeStruct((N_SHARD,) + s.shape, s.dtype) for s in gather]
    aliases = {3: 0, 4: 1, 2: 2} if diag else {3: 0, 4: 1}
    pblock = pl.BlockSpec((tm, tn), lambda i, j, k: (i, 2 * k[0] + j))
    return pl.pallas_call(
        body, name=name,
        grid_spec=pltpu.PrefetchScalarGridSpec(
            num_scalar_prefetch=1, grid=grid,
            in_specs=[pl.BlockSpec((tm, D), lambda i, j, k: (i, 0)), ANY, ANY, ANY] + [ANY] * len(srcs),
            out_specs=[pblock, pblock] + [ANY] * len(srcs),
            scratch_shapes=[pltpu.VMEM((2, D, tn), BF16), pltpu.SemaphoreType.DMA((2,))]
            + (_gather_scratch(len(srcs)) if srcs else []) + [pltpu.VMEM(s.shape, s.dtype) for s in gather]),
        out_shape=out_shape,
        input_output_aliases=aliases,
        compiler_params=_params("arbitrary", "arbitrary"),
    )(shard_arg, hx, w_full, p[0], p[1], *srcs)


def _hgrn_gates(z, lb):
    sg = _sigmoid(z)
    sgn = _sigmoid(-z)
    f = lb + (1.0 - lb) * sg
    k = (1.0 - lb) * sgn
    return sg, sgn, f, k


def _tri_chunks(n, chunk, reverse):
    r = lax.broadcasted_iota(jnp.int32, (n, n), 0)
    c = lax.broadcasted_iota(jnp.int32, (n, n), 1)
    same = (r // chunk) == (c // chunk)
    return jnp.where(same & ((r <= c) if reverse else (r >= c)), 1.0, 0.0).astype(F32)


def _decay3(b, reverse, key_major=False):
    C = b.shape[0]
    i0 = lax.broadcasted_iota(jnp.int32, (C, C, 1), 0)
    i1 = lax.broadcasted_iota(jnp.int32, (C, C, 1), 1)
    t, s = (i1, i0) if key_major else (i0, i1)
    mask = (t <= s) if reverse else (t >= s)
    diff = (b[None, :, :] - b[:, None, :]) if key_major else (b[:, None, :] - b[None, :, :])
    return jnp.exp(jnp.where(mask, diff, -jnp.inf))


HG_SUB = 8


def _hgrn_pairs(reverse):
    pairs = []
    size = HG_SUB
    while size < HG_CHUNK:
        for lo in range(0, HG_CHUNK, 2 * size):
            first, second = slice(lo, lo + size), slice(lo + size, lo + 2 * size)
            if reverse:
                pairs.append((first, second, lo + size))
            else:
                pairs.append((second, first, lo + size - 1))
        size *= 2
    return pairs


def _head_mask(g, nq, nk):
    r = lax.broadcasted_iota(jnp.int32, (g * nq, g * nk), 0) // nq
    c = lax.broadcasted_iota(jnp.int32, (g * nq, g * nk), 1) // nk
    return jnp.where(r == c, 1.0, 0.0).astype(F32)


def _hgrn_masks(g, reverse):
    return [_head_mask(g, qr.stop - qr.start, kr.stop - kr.start) for qr, kr, _ in _hgrn_pairs(reverse)]


def _stack(xs):
    return jnp.concatenate(xs, axis=0)


def _unstack(x, g):
    n = x.shape[0] // g
    return [x[h * n:(h + 1) * n] for h in range(g)]


def _add_blocks(acc, rows, part):
    for i in range(part.shape[0] // HG_SUB):
        acc[rows.start // HG_SUB + i] += part[i * HG_SUB:(i + 1) * HG_SUB]


def _hgrn_intra_fwd(qs, ks, vs, bs, masks, reverse):
    g = len(qs)
    blocks = []
    for q, k, v, b in zip(qs, ks, vs, bs):
        mine = []
        for lo in range(0, HG_CHUNK, HG_SUB):
            r = slice(lo, lo + HG_SUB)
            e3 = _decay3(b[r], reverse, key_major=True)
            att3 = jnp.sum(q[r][None, :, :] * k[r][:, None, :] * e3, axis=-1, keepdims=True)
            mine.append(jnp.sum(att3 * v[r][:, None, :], axis=0))
        blocks.append(mine)
    for (qr, kr, ref), mask in zip(_hgrn_pairs(reverse), masks):
        qt = _stack([q[qr] * jnp.exp(b[qr] - b[ref:ref + 1]) for q, b in zip(qs, bs)])
        kt = _stack([k[kr] * jnp.exp(b[ref:ref + 1] - b[kr]) for k, b in zip(ks, bs)])
        att = _bdot(qt, kt, 1, 1) * mask
        for mine, part in zip(blocks, _unstack(_bdot(att, _stack([v[kr] for v in vs])), g)):
            _add_blocks(mine, qr, part)
    return [jnp.concatenate(mine, axis=0) for mine in blocks]


def _hgrn_intra_bwd(qs, ks, vs, bs, d_os, masks, reverse):
    g = len(qs)
    nb = HG_CHUNK // HG_SUB
    dqs, dks, dvs = [], [], []
    for q, k, v, b, d_o in zip(qs, ks, vs, bs, d_os):
        dq, dk, dv = [None] * nb, [None] * nb, [None] * nb
        for i in range(nb):
            r = slice(i * HG_SUB, (i + 1) * HG_SUB)
            e3 = _decay3(b[r], reverse)
            p3 = jnp.sum(d_o[r][:, None, :] * v[r][None, :, :], axis=-1, keepdims=True) * e3
            dq[i] = jnp.sum(p3 * k[r][None, :, :], axis=1)
            dk[i] = jnp.sum(p3 * q[r][:, None, :], axis=0)
            att3 = jnp.sum(q[r][:, None, :] * k[r][None, :, :] * e3, axis=-1, keepdims=True)
            dv[i] = jnp.sum(att3 * d_o[r][:, None, :], axis=0)
        dqs.append(dq)
        dks.append(dk)
        dvs.append(dv)
    for (qr, kr, ref), mask in zip(_hgrn_pairs(reverse), masks):
        fqs = [jnp.exp(b[qr] - b[ref:ref + 1]) for b in bs]
        fks = [jnp.exp(b[ref:ref + 1] - b[kr]) for b in bs]
        qt = _stack([q[qr] * f for q, f in zip(qs, fqs)])
        kt = _stack([k[kr] * f for k, f in zip(ks, fks)])
        do_q = _stack([d_o[qr] for d_o in d_os])
        att = _bdot(qt, kt, 1, 1) * mask
        datt = _bdot(do_q, _stack([v[kr] for v in vs]), 1, 1) * mask
        for dq, part, f in zip(dqs, _unstack(_bdot(datt, kt), g), fqs):
            _add_blocks(dq, qr, part * f)
        for dk, part, f in zip(dks, _unstack(_bdot(datt, qt, 0, 0), g), fks):
            _add_blocks(dk, kr, part * f)
        for dv, part in zip(dvs, _unstack(_bdot(att, do_q, 0, 0), g)):
            _add_blocks(dv, kr, part)

    def cat(parts):
        return [jnp.concatenate(p, axis=0) for p in parts]

    return cat(dqs), cat(dks), cat(dvs)


def _hgrn_state_step(k, v, b, s_t, last):
    b_last = b[last:last + 1]
    return s_t * jnp.exp(b_last) + _bdot(v, k * jnp.exp(b_last - b), 0, 0)


def hgrn_scan_fwd(p, lb, s0, col_z, reverse, name, prev=None):
    has_prev = prev is not None
    L = p.shape[0]
    nB = L // SCAN_ROWS
    nC = SCAN_ROWS // HG_CHUNK
    C = HG_CHUNK
    G, W = HG_GROUP, HG_GROUP * HG_D
    last = 0 if reverse else C - 1

    def bmap(b):
        return (nB - 1 - b) if reverse else b

    def body(q_ref, z_ref, v_ref, lb_ref, s0_ref, *refs):
        prev_ref = refs[0] if has_prev else None
        o_ref, sfin_ref, sblk_ref, s_scr, k_scr, b_scr = refs[1:] if has_prev else refs
        blk = pl.program_id(1)

        @pl.when(blk == 0)
        def _():
            s_scr[...] = s0_ref[...]

        sblk_ref[...] = s_scr[...]
        _, _, f_all, k_all = _hgrn_gates(z_ref[...], lb_ref[...])
        k_scr[...] = k_all
        b_scr[...] = _dot(_tri_chunks(SCAN_ROWS, C, reverse), jnp.log(f_all), prec=CUMSUM_PRECISION)

        masks = _hgrn_masks(G, reverse)
        heads = [slice(j * HG_D, (j + 1) * HG_D) for j in range(G)]

        def chunk(ci, carry):
            c = (nC - 1 - ci) if reverse else ci
            rows = pl.ds(pl.multiple_of(c * C, C), C)
            qs = [q_ref[rows, lanes] * Q_SCALE for lanes in heads]
            vs = [v_ref[rows, lanes] for lanes in heads]
            ks = [k_scr[rows, lanes] for lanes in heads]
            bs = [b_scr[rows, lanes] for lanes in heads]
            o_in = _hgrn_intra_fwd(qs, ks, vs, bs, masks, reverse)
            for j, lanes in enumerate(heads):
                s_t = s_scr[j]
                o = o_in[j] + _bdot(qs[j] * jnp.exp(bs[j]), s_t, 1, 1)
                o_ref[rows, lanes] = o + prev_ref[rows, lanes] if has_prev else o
                s_scr[j] = _hgrn_state_step(ks[j], vs[j], bs[j], s_t, last)
            return carry

        lax.fori_loop(0, nC, chunk, 0)

        @pl.when(blk == nB - 1)
        def _():
            sfin_ref[...] = s_scr[...]

    def col(c0):
        return pl.BlockSpec((SCAN_ROWS, W), lambda h, b: (bmap(b), c0 // G + h))

    state = pl.BlockSpec((G, HG_D, HG_D), lambda h, b: (h, 0, 0))
    return pl.pallas_call(
        body, name=name,
        grid=(HEADS // G, nB),
        in_specs=[col(COL_HQ), col(col_z), col(COL_HI), pl.BlockSpec((1, W), lambda h, b: (0, h)), state]
        + ([pl.BlockSpec((SCAN_ROWS, W), lambda h, b: (bmap(b), h))] if has_prev else []),
        out_specs=[pl.BlockSpec((SCAN_ROWS, W), lambda h, b: (bmap(b), h)), state,
                   pl.BlockSpec((None, G, HG_D, HG_D), lambda h, b: (bmap(b), h, 0, 0))],
        out_shape=[jax.ShapeDtypeStruct((L, D), F32),
                   jax.ShapeDtypeStruct((HEADS, HG_D, HG_D), F32),
                   jax.ShapeDtypeStruct((nB, HEADS, HG_D, HG_D), F32)],
        scratch_shapes=[pltpu.VMEM((G, HG_D, HG_D), F32), pltpu.VMEM((SCAN_ROWS, W), F32),
                        pltpu.VMEM((SCAN_ROWS, W), F32)],
        compiler_params=_params("parallel", "arbitrary"),
    )(p, p, p, lb, s0, *([prev] if has_prev else []))


def hgrn_scan_bwd(p, lb, s_blocks, d_o, ds_fin, prev, col_z, reverse, name):
    L = p.shape[0]
    nB = L // SCAN_ROWS
    nC = SCAN_ROWS // HG_CHUNK
    C = HG_CHUNK
    G, W = HG_GROUP, HG_GROUP * HG_D
    last = 0 if reverse else C - 1
    has_prev = prev is not None
    out_dt = BF16 if has_prev else F32

    def bmap(b):
        return b if reverse else (nB - 1 - b)

    def body(*refs):
        q_ref, z_ref, v_ref, lb_ref, sblk_ref, do_ref, dsf_ref = refs[:7]
        refs = refs[7:]
        if has_prev:
            pq_ref, pv_ref = refs[:2]
            refs = refs[2:]
        (dq_ref, dz_ref, dv_ref, dlb_ref, ds0_ref, st_scr, run_scr, ds_scr, k_scr, b_scr, db_scr, dk_scr,
         rf_scr, sgn_scr, dzf_scr) = refs
        blk = pl.program_id(1)

        @pl.when(blk == 0)
        def _():
            ds_scr[...] = dsf_ref[...]
            dlb_ref[...] = jnp.zeros_like(dlb_ref)

        tri = _tri_chunks(SCAN_ROWS, C, reverse)
        row = lax.broadcasted_iota(jnp.int32, (C, HG_D), 0)
        lb_all = lb_ref[...]
        sg_all, sgn_all, f_all, k_all = _hgrn_gates(z_ref[...], lb_all)
        k_scr[...] = k_all
        rf_scr[...] = 1.0 / f_all
        sgn_scr[...] = sgn_all
        dzf_scr[...] = (1.0 - lb_all) * sg_all * sgn_all
        b_scr[...] = _dot(tri, jnp.log(f_all), prec=CUMSUM_PRECISION)
        run_scr[...] = sblk_ref[...]

        def recompute(ci, carry):
            c = (nC - 1 - ci) if reverse else ci
            rows = pl.ds(pl.multiple_of(c * C, C), C)
            for j in range(G):
                lanes = slice(j * HG_D, (j + 1) * HG_D)
                s_t = run_scr[j]
                st_scr[c, j] = s_t
                run_scr[j] = _hgrn_state_step(k_scr[rows, lanes], v_ref[rows, lanes], b_scr[rows, lanes], s_t, last)
            return carry

        lax.fori_loop(0, nC, recompute, 0)

        masks = _hgrn_masks(G, reverse)
        heads = [slice(j * HG_D, (j + 1) * HG_D) for j in range(G)]

        def chunk(ci, carry):
            c = ci if reverse else (nC - 1 - ci)
            rows = pl.ds(pl.multiple_of(c * C, C), C)
            ks = [k_scr[rows, lanes] for lanes in heads]
            bs = [b_scr[rows, lanes] for lanes in heads]
            qs = [q_ref[rows, lanes] * Q_SCALE for lanes in heads]
            vs = [v_ref[rows, lanes] for lanes in heads]
            d_os = [do_ref[rows, lanes] for lanes in heads]
            dq_ins, dk_ins, dv_ins = _hgrn_intra_bwd(qs, ks, vs, bs, d_os, masks, reverse)
            for j, lanes in enumerate(heads):
                k, b, q, v, d_o = ks[j], bs[j], qs[j], vs[j], d_os[j]
                s_t = st_scr[c, j]
                ds_t = ds_scr[j]
                eb = jnp.exp(b)
                b_last = b[last:last + 1]
                eb_last = jnp.exp(b_last)
                kdec = jnp.exp(b_last - b)
                qe = q * eb
                ke = k * kdec
                dq_tot = _bdot(d_o, s_t, 1, 0) * eb + dq_ins[j]
                dke = _bdot(v, ds_t, 1, 0)
                dk_tot = dke * kdec + dk_ins[j]
                dv = dv_ins[j] + _bdot(ke, ds_t, 1, 1)
                db_last = _rowsum(dke * ke) + eb_last * _rowsum(ds_t * s_t)
                db_scr[rows, lanes] = q * dq_tot - k * dk_tot + jnp.where(row == last, db_last, 0.0)
                dk_scr[rows, lanes] = dk_tot
                dq = dq_tot * Q_SCALE
                if has_prev:
                    dq = dq + pq_ref[rows, lanes]
                    dv = dv + pv_ref[rows, lanes]
                dq_ref[rows, lanes] = dq.astype(out_dt)
                dv_ref[rows, lanes] = dv.astype(out_dt)
                ds_scr[j] = ds_t * eb_last + _bdot(d_o, qe, 0, 0)
            return carry

        lax.fori_loop(0, nC, chunk, 0)

        g = _dot(tri, db_scr[...], 0, 0, prec=CUMSUM_PRECISION) * rf_scr[...] - dk_scr[...]
        dz_ref[...] = (g * dzf_scr[...]).astype(BF16)
        dlb_ref[...] += _rowsum(g * sgn_scr[...])

        @pl.when(blk == nB - 1)
        def _():
            ds0_ref[...] = ds_scr[...]

    def col(c0):
        return pl.BlockSpec((SCAN_ROWS, W), lambda h, b: (bmap(b), c0 // G + h))

    tile = pl.BlockSpec((SCAN_ROWS, W), lambda h, b: (bmap(b), h))
    state = pl.BlockSpec((G, HG_D, HG_D), lambda h, b: (h, 0, 0))
    in_specs = [col(COL_HQ), col(col_z), col(COL_HI),
                pl.BlockSpec((1, W), lambda h, b: (0, h)),
                pl.BlockSpec((None, G, HG_D, HG_D), lambda h, b: (bmap(b), h, 0, 0)),
                tile, state]
    args = [p, p, p, lb, s_blocks, d_o, ds_fin]
    if has_prev:
        in_specs += [tile, tile]
        args += list(prev)
    return pl.pallas_call(
        body, name=name,
        grid=(HEADS // G, nB),
        in_specs=in_specs,
        out_specs=[tile, tile, tile, pl.BlockSpec((1, W), lambda h, b: (0, h)), state],
        out_shape=[jax.ShapeDtypeStruct((L, D), out_dt), jax.ShapeDtypeStruct((L, D), BF16),
                   jax.ShapeDtypeStruct((L, D), out_dt), jax.ShapeDtypeStruct((1, D), F32),
                   jax.ShapeDtypeStruct((HEADS, HG_D, HG_D), F32)],
        scratch_shapes=[pltpu.VMEM((nC, G, HG_D, HG_D), F32), pltpu.VMEM((G, HG_D, HG_D), F32),
                        pltpu.VMEM((G, HG_D, HG_D), F32)] + [pltpu.VMEM((SCAN_ROWS, W), F32)] * 7,
        compiler_params=_params("parallel", "arbitrary"),
    )(*args)


def _rope(t, cosf, sinf):
    return t * cosf + pltpu.roll(t, RT_DK // 2, 1) * sinf


def _rope_t(d, cosf, sinf):
    return d * cosf + pltpu.roll(d * sinf, RT_DK // 2, 1)


def _ret_decays(lg, reverse):
    C = SCAN_ROWS
    t = lax.broadcasted_iota(jnp.int32, (C, C), 0)
    s = lax.broadcasted_iota(jnp.int32, (C, C), 1)
    delta = ((s - t) if reverse else (t - s)).astype(F32)
    dmat = jnp.where(delta >= 0, jnp.exp(lg * jnp.maximum(delta, 0.0)), 0.0)
    r = lax.broadcasted_iota(jnp.int32, (C, RT_DK), 0)
    pos = ((C - 1 - r) if reverse else r).astype(F32)
    lg1 = lg[:, :RT_DK]
    qdec = jnp.exp(lg1 * (pos + 1.0))
    kdec = jnp.exp(lg1 * (C - 1.0 - pos))
    sdec = jnp.exp(lg1 * float(C))
    return dmat, delta, pos, qdec, kdec, sdec


def ret_scan_fwd(p, cosf, sinf, lg, s0, reverse, name, prev=None):
    has_prev = prev is not None
    L = p.shape[0]
    C = SCAN_ROWS
    nB = L // C

    def bmap(b):
        return (nB - 1 - b) if reverse else b

    G = RT_GROUP

    def body(q_ref, k_ref, v_ref, cos_ref, sin_ref, lg_ref, s0_ref, *refs):
        prev_ref = refs[0] if has_prev else None
        o_ref, sfin_ref, sblk_ref, s_scr = refs[1:] if has_prev else refs
        blk = pl.program_id(1)

        @pl.when(blk == 0)
        def _():
            s_scr[...] = s0_ref[...]

        sblk_ref[...] = s_scr[...]
        cosf, sinf = cos_ref[...], sin_ref[...]
        for j in range(G):
            lk, lv = slice(j * RT_DK, (j + 1) * RT_DK), slice(j * RT_DV, (j + 1) * RT_DV)
            s_t = s_scr[j]
            dmat, _, _, qdec, kdec, sdec = _ret_decays(lg_ref[j], reverse)
            q = _rope(q_ref[:, lk].astype(F32) * Q_SCALE, cosf, sinf)
            k = _rope(k_ref[:, lk].astype(F32), cosf, sinf)
            v = v_ref[:, lv].astype(F32)
            att = _bdot(q, k, 1, 1) * dmat
            o = _bdot(att, v) + _bdot(q * qdec, s_t, 1, 1)
            o_ref[:, lv] = o + prev_ref[:, lv] if has_prev else o
            s_scr[j] = s_t * sdec + _bdot(v, k * kdec, 0, 0)

        @pl.when(blk == nB - 1)
        def _():
            sfin_ref[...] = s_scr[...]

    def col(c0):
        return pl.BlockSpec((C, G * RT_DK), lambda h, b: (bmap(b), c0 // G + h))

    tab = pl.BlockSpec((C, RT_DK), lambda h, b: (bmap(b), 0))
    state = pl.BlockSpec((G, RT_DV, RT_DK), lambda h, b: (h, 0, 0))
    return pl.pallas_call(
        body, name=name,
        grid=(HEADS // G, nB),
        in_specs=[col(COL_RQ), col(COL_RK),
                  pl.BlockSpec((C, G * RT_DV), lambda h, b: (bmap(b), COL_RV // (2 * G) + h)),
                  tab, tab, pl.BlockSpec((G, 1, RT_DV), lambda h, b: (h, 0, 0)), state]
        + ([pl.BlockSpec((C, G * RT_DV), lambda h, b: (bmap(b), h))] if has_prev else []),
        out_specs=[pl.BlockSpec((C, G * RT_DV), lambda h, b: (bmap(b), h)), state,
                   pl.BlockSpec((None, G, RT_DV, RT_DK), lambda h, b: (bmap(b), h, 0, 0))],
        out_shape=[jax.ShapeDtypeStruct((L, HEADS * RT_DV), F32),
                   jax.ShapeDtypeStruct((HEADS, RT_DV, RT_DK), F32),
                   jax.ShapeDtypeStruct((nB, HEADS, RT_DV, RT_DK), F32)],
        scratch_shapes=[pltpu.VMEM((G, RT_DV, RT_DK), F32)],
        compiler_params=_params("parallel", "arbitrary"),
    )(p, p, p, cosf, sinf, lg, s0, *([prev] if has_prev else []))


def ret_scan_bwd(p, cosf, sinf, lg, s_blocks, d_o, ds_fin, prev, reverse, name):
    L = p.shape[0]
    C = SCAN_ROWS
    nB = L // C
    has_prev = prev is not None
    out_dt = BF16
    G = RT_GROUP

    def bmap(b):
        return b if reverse else (nB - 1 - b)

    def body(*refs):
        q_ref, k_ref, v_ref, cos_ref, sin_ref, lg_ref, sblk_ref, do_ref, dsf_ref = refs[:9]
        refs = refs[9:]
        if has_prev:
            pq_ref, pk_ref, pv_ref = refs[:3]
            refs = refs[3:]
        dq_ref, dk_ref, dv_ref, dlg_ref, ds0_ref, ds_scr = refs
        blk = pl.program_id(1)

        @pl.when(blk == 0)
        def _():
            ds_scr[...] = dsf_ref[...]
            dlg_ref[...] = jnp.zeros_like(dlg_ref)

        cosf, sinf = cos_ref[...], sin_ref[...]
        for j in range(G):
            lk, lv = slice(j * RT_DK, (j + 1) * RT_DK), slice(j * RT_DV, (j + 1) * RT_DV)
            s_t = sblk_ref[j]
            ds_t = ds_scr[j]
            dmat, delta, pos, qdec, kdec, sdec = _ret_decays(lg_ref[j], reverse)
            q = _rope(q_ref[:, lk].astype(F32) * Q_SCALE, cosf, sinf)
            k = _rope(k_ref[:, lk].astype(F32), cosf, sinf)
            v = v_ref[:, lv].astype(F32)
            d_o = do_ref[:, lv]
            att_raw = _bdot(q, k, 1, 1)
            datt_m = _bdot(d_o, v, 1, 1) * dmat
            dqd = _bdot(d_o, s_t, 1, 0)
            dkd = _bdot(v, ds_t, 1, 0)
            dq = _bdot(datt_m, k) + dqd * qdec
            dk = _bdot(datt_m, q, 0, 0) + dkd * kdec
            dv = _bdot(att_raw * dmat, d_o, 0, 0) + _bdot(k * kdec, ds_t, 1, 1)
            ds_scr[j] = ds_t * sdec + _bdot(d_o, q * qdec, 0, 0)
            t1 = jnp.sum(_rowsum(datt_m * att_raw * delta), axis=-1, keepdims=True)
            t23 = jnp.sum(_rowsum((pos + 1.0) * qdec * q * dqd + (C - 1.0 - pos) * kdec * k * dkd), axis=-1, keepdims=True)
            t4 = jnp.sum(_rowsum(ds_t * s_t * sdec), axis=-1, keepdims=True) * float(C)
            dlg_ref[j] += jnp.broadcast_to(t1 + t23 + t4, (1, RT_DK))
            if has_prev:
                dq = _rope_t(dq + pq_ref[:, lk].astype(F32), cosf, sinf) * Q_SCALE
                dk = _rope_t(dk + pk_ref[:, lk].astype(F32), cosf, sinf)
                dv = dv + pv_ref[:, lv].astype(F32)
            dq_ref[:, lk] = dq.astype(out_dt)
            dk_ref[:, lk] = dk.astype(out_dt)
            dv_ref[:, lv] = dv.astype(out_dt)

        @pl.when(blk == nB - 1)
        def _():
            ds0_ref[...] = ds_scr[...]

    def col(c0):
        return pl.BlockSpec((C, G * RT_DK), lambda h, b: (bmap(b), c0 // G + h))

    tab = pl.BlockSpec((C, RT_DK), lambda h, b: (bmap(b), 0))
    state = pl.BlockSpec((G, RT_DV, RT_DK), lambda h, b: (h, 0, 0))
    tk = pl.BlockSpec((C, G * RT_DK), lambda h, b: (bmap(b), h))
    tv = pl.BlockSpec((C, G * RT_DV), lambda h, b: (bmap(b), h))
    in_specs = [col(COL_RQ), col(COL_RK),
                pl.BlockSpec((C, G * RT_DV), lambda h, b: (bmap(b), COL_RV // (2 * G) + h)),
                tab, tab, pl.BlockSpec((G, 1, RT_DV), lambda h, b: (h, 0, 0)),
                pl.BlockSpec((None, G, RT_DV, RT_DK), lambda h, b: (bmap(b), h, 0, 0)),
                tv, state]
    args = [p, p, p, cosf, sinf, lg, s_blocks, d_o, ds_fin]
    if has_prev:
        in_specs += [tk, tk, tv]
        args += list(prev)
    return pl.pallas_call(
        body, name=name,
        grid=(HEADS // G, nB),
        in_specs=in_specs,
        out_specs=[tk, tk, tv, pl.BlockSpec((G, 1, RT_DK), lambda h, b: (h, 0, 0)), state],
        out_shape=[jax.ShapeDtypeStruct((L, D), out_dt), jax.ShapeDtypeStruct((L, D), out_dt),
                   jax.ShapeDtypeStruct((L, HEADS * RT_DV), out_dt),
                   jax.ShapeDtypeStruct((HEADS, 1, RT_DK), F32),
                   jax.ShapeDtypeStruct((HEADS, RT_DV, RT_DK), F32)],
        scratch_shapes=[pltpu.VMEM((G, RT_DV, RT_DK), F32)],
        compiler_params=_params("parallel", "arbitrary"),
    )(*args)


def _silu_parts(h):
    s = _sigmoid(h)
    return h * s, s * (1.0 + h * (1.0 - s))


def _head_rms(o):
    outs, rs = [], []
    for h in range(HEADS):
        oh = o[:, h * HG_D:(h + 1) * HG_D]
        r = lax.rsqrt(_lanemean(oh * oh) + EPS)
        outs.append(oh * r)
        rs.append(r)
    return outs, rs


def _group_norm(o):
    outs, rs = [], []
    for h in range(HEADS):
        oh = o[:, h * RT_DV:(h + 1) * RT_DV]
        c = oh - _lanemean(oh)
        r = lax.rsqrt(_lanemean(c * c) + GN_EPS)
        outs.append(c * r)
        rs.append(r)
    return outs, rs


MIX_ROWS = 256


def _mix_specs(rows):
    def t(w, c=0):
        return pl.BlockSpec((rows, w), lambda i: (i, c))

    return t


def mix_fwd(o_hg, o_rt, p, x, g1, hgw, w_pa, w_pb, w_out, name):
    L = x.shape[0]
    t = _mix_specs(MIX_ROWS)

    def body(ohg_ref, ort_ref, hg_ref, rg0_ref, rg1_ref, ga_ref, gb_ref, x_ref, g1_ref, hgw_ref,
             wpa_ref, wpb_ref, wout_ref, x1_ref, xmix_ref, merged_ref, ya_ref, yb_ref):
        nh, _ = _head_rms(ohg_ref[...])
        ya = jnp.concatenate(nh, axis=1) * hgw_ref[...] * _silu_parts(hg_ref[...])[0]
        gn, _ = _group_norm(ort_ref[...])
        rg = jnp.concatenate([rg0_ref[...], rg1_ref[...]], axis=1)
        yb = jnp.concatenate(gn, axis=1) * _silu_parts(rg)[0]
        ya16, yb16 = ya.astype(BF16), yb.astype(BF16)
        merged = (_sigmoid(ga_ref[...]) * _dot(ya16, wpa_ref[...])
                  + _sigmoid(gb_ref[...]) * _dot(yb16, wpb_ref[...])).astype(BF16)
        x_mix = _dot(merged, wout_ref[...])
        x1_ref[...] = x_ref[...] + g1_ref[...] * x_mix
        xmix_ref[...] = x_mix
        merged_ref[...] = merged
        ya_ref[...] = ya16
        yb_ref[...] = yb16

    vec = pl.BlockSpec((1, D), lambda i: (0, 0))

    def full(a):
        return pl.BlockSpec(a.shape, lambda i: (0, 0), pipeline_mode=pl.Buffered(1))

    return pl.pallas_call(
        body, name=name,
        grid=(L // MIX_ROWS,),
        in_specs=[t(D), t(2 * D), t(D, COL_HG // 8), t(D, COL_RG // 8), t(D, COL_RG // 8 + 1),
                  t(D, COL_GA // 8), t(D, COL_GB // 8), t(D), vec, vec, full(w_pa), full(w_pb), full(w_out)],
        out_specs=[t(D), t(D), t(D), t(D), t(2 * D)],
        out_shape=[jax.ShapeDtypeStruct((L, D), F32), jax.ShapeDtypeStruct((L, D), F32),
                   jax.ShapeDtypeStruct((L, D), BF16), jax.ShapeDtypeStruct((L, D), BF16),
                   jax.ShapeDtypeStruct((L, 2 * D), BF16)],
        compiler_params=_params("parallel"),
    )(o_hg, o_rt, p, p, p, p, p, x, g1, hgw, w_pa, w_pb, w_out)


def mix_bwd(dx1, x_mix, ya, yb, o_hg, o_rt, p, g1, hgw, w_pa, w_pb, w_out, name, to_sibling=()):
    L = dx1.shape[0]
    t = _mix_specs(MIX_ROWS)
    nx = len(to_sibling)
    steps = L // MIX_ROWS

    def body(dx1_ref, xmix_ref, ya_ref, yb_ref, ohg_ref, ort_ref, hg_ref, rg0_ref, rg1_ref,
             ga_ref, gb_ref, g1_ref, hgw_ref, wpa_ref, wpb_ref, wout_ref, *refs):
        (dxm_ref, da_ref, db_ref, dga_ref, dgb_ref, dhg_ref, drg_ref, dohg_ref, dort_ref,
         sums_ref) = refs[nx:nx + 10]
        if nx:
            start, finish = _to_sibling_phases(refs[:nx], refs[nx + 10:2 * nx + 10], *refs[2 * nx + 10:])
            pl.when(pl.program_id(0) == 0)(start)
            pl.when(pl.program_id(0) == steps - 1)(finish)

        @pl.when(pl.program_id(0) == 0)
        def _():
            sums_ref[...] = jnp.zeros_like(sums_ref)

        dx1 = dx1_ref[...]
        dxm = (g1_ref[...] * dx1).astype(BF16)
        dxm_ref[...] = dxm
        dmerged = _dot(dxm, wout_ref[...], 1, 1)
        a = _dot(ya_ref[...], wpa_ref[...])
        bm = _dot(yb_ref[...], wpb_ref[...])
        sa, sb = _sigmoid(ga_ref[...]), _sigmoid(gb_ref[...])
        d_a = (dmerged * sa).astype(BF16)
        d_b = (dmerged * sb).astype(BF16)
        da_ref[...] = d_a
        db_ref[...] = d_b
        dga_ref[...] = (dmerged * a * sa * (1.0 - sa)).astype(BF16)
        dgb_ref[...] = (dmerged * bm * sb * (1.0 - sb)).astype(BF16)
        dya = _dot(d_a, wpa_ref[...], 1, 1)
        dyb = _dot(d_b, wpb_ref[...], 1, 1)

        hgw = hgw_ref[...]
        silu_h, dsilu_h = _silu_parts(hg_ref[...])
        nh, rh = _head_rms(ohg_ref[...])
        n = jnp.concatenate(nh, axis=1)
        dhg_ref[...] = (dya * n * hgw * dsilu_h).astype(BF16)
        dn = dya * hgw * silu_h
        douts = []
        for h in range(HEADS):
            dnh = dn[:, h * HG_D:(h + 1) * HG_D]
            douts.append(rh[h] * (dnh - nh[h] * _lanemean(dnh * nh[h])))
        dohg_ref[...] = jnp.concatenate(douts, axis=1)

        rg = jnp.concatenate([rg0_ref[...], rg1_ref[...]], axis=1)
        silu_r, dsilu_r = _silu_parts(rg)
        gn, rr = _group_norm(ort_ref[...])
        g = jnp.concatenate(gn, axis=1)
        drg_ref[...] = (dyb * g * dsilu_r).astype(BF16)
        dgn = dyb * silu_r
        douts = []
        for h in range(HEADS):
            dgh = dgn[:, h * RT_DV:(h + 1) * RT_DV]
            douts.append(rr[h] * (dgh - _lanemean(dgh) - gn[h] * _lanemean(dgh * gn[h])))
        dort_ref[...] = jnp.concatenate(douts, axis=1)

        sums_ref[0:1, :] += _rowsum(dx1 * xmix_ref[...])
        sums_ref[1:2, :] += _rowsum(dya * n * silu_h)

    vec = pl.BlockSpec((1, D), lambda i: (0, 0))

    def full(a):
        return pl.BlockSpec(a.shape, lambda i: (0, 0), pipeline_mode=pl.Buffered(1))

    bf = functools.partial(jax.ShapeDtypeStruct, dtype=BF16)
    return pl.pallas_call(
        body, name=name,
        grid=(L // MIX_ROWS,),
        in_specs=[t(D), t(D), t(D), t(2 * D), t(D), t(2 * D),
                  t(D, COL_HG // 8), t(D, COL_RG // 8), t(D, COL_RG // 8 + 1), t(D, COL_GA // 8), t(D, COL_GB // 8),
                  vec, vec, full(w_pa), full(w_pb), full(w_out)] + [ANY] * nx,
        out_specs=[t(D), t(D), t(D), t(D), t(D), t(D), t(2 * D), t(D), t(2 * D),
                   pl.BlockSpec((8, D), lambda i: (0, 0))] + [ANY] * nx,
        out_shape=[bf((L, D)), bf((L, D)), bf((L, D)), bf((L, D)), bf((L, D)), bf((L, D)), bf((L, 2 * D)),
                   jax.ShapeDtypeStruct((L, D), F32), jax.ShapeDtypeStruct((L, 2 * D), F32),
                   jax.ShapeDtypeStruct((8, D), F32)] + [jax.ShapeDtypeStruct(a.shape, a.dtype) for a in to_sibling],
        scratch_shapes=_to_sibling_scratch(nx) if nx else [],
        compiler_params=_params("arbitrary"),
    )(dx1, x_mix, ya, yb, o_hg, o_rt, p, p, p, p, p, g1, hgw, w_pa, w_pb, w_out, *to_sibling)


FFN_ROWS = 512


def ffn_fwd(x1, target, nw2, sh2, sc2, g2, fw, wg, wu, wd, name):
    L = x1.shape[0]
    tm = min(FFN_ROWS, L)

    def body(x1_ref, tgt_ref, nw2_ref, sh2_ref, sc2_ref, g2_ref, fw_ref, wg_ref, wu_ref, wd_ref,
             hx2_ref, g_ref, u_ref, h_ref, f_ref, dx2_ref, sums_ref, hx_scr, acc):
        i, j = pl.program_id(0), pl.program_id(1)

        @pl.when((i == 0) & (j == 0))
        def _():
            sums_ref[...] = jnp.zeros_like(sums_ref)

        @pl.when(j == 0)
        def _():
            xv = x1_ref[...]
            n = xv * lax.rsqrt(_lanemean(xv * xv) + EPS) * nw2_ref[...]
            h = (n * (1.0 + sc2_ref[...]) + sh2_ref[...]).astype(BF16)
            hx_scr[...] = h
            hx2_ref[...] = h
            acc[...] = jnp.zeros_like(acc)

        hx = hx_scr[...]
        g = _dot(hx, wg_ref[...])
        u = _dot(hx, wu_ref[...])
        hh = (_silu_parts(g)[0] * u).astype(BF16)
        g_ref[...] = g
        u_ref[...] = u
        h_ref[...] = hh
        acc[...] += _dot(hh, wd_ref[...])

        @pl.when(j == N_SHARD - 1)
        def _():
            f = acc[...]
            f_ref[...] = f
            x2 = x1_ref[...] + g2_ref[...] * f
            r = lax.rsqrt(_lanemean(x2 * x2) + EPS)
            fw = fw_ref[...]
            e = x2 * r * fw - tgt_ref[...]
            dy = e * (1.0 / D)
            dyw = dy * fw
            dx2_ref[...] = r * dyw - x2 * (r * r * r) * _lanemean(dyw * x2)
            sums_ref[0:1, :] += _rowsum(dy * x2 * r)
            sums_ref[1:2, :] += _rowsum(e * e) * (0.5 / D)

    row = pl.BlockSpec((tm, D), lambda i, j: (i, 0))
    vec = pl.BlockSpec((1, D), lambda i, j: (0, 0))
    sh = pl.BlockSpec((None, tm, FF_SH), lambda i, j: (j, i, 0))
    return pl.pallas_call(
        body, name=name,
        grid=(L // tm, N_SHARD),
        in_specs=[row, row, vec, vec, vec, vec, vec,
                  pl.BlockSpec((None, D, FF_SH), lambda i, j: (j, 0, 0)),
                  pl.BlockSpec((None, D, FF_SH), lambda i, j: (j, 0, 0)),
                  pl.BlockSpec((None, FF_SH, D), lambda i, j: (j, 0, 0))],
        out_specs=[row, sh, sh, sh, row, row, pl.BlockSpec((8, D), lambda i, j: (0, 0))],
        out_shape=[jax.ShapeDtypeStruct((L, D), BF16),
                   jax.ShapeDtypeStruct((N_SHARD, L, FF_SH), F32), jax.ShapeDtypeStruct((N_SHARD, L, FF_SH), F32),
                   jax.ShapeDtypeStruct((N_SHARD, L, FF_SH), BF16),
                   jax.ShapeDtypeStruct((L, D), F32), jax.ShapeDtypeStruct((L, D), F32),
                   jax.ShapeDtypeStruct((8, D), F32)],
        scratch_shapes=[pltpu.VMEM((tm, D), BF16), pltpu.VMEM((tm, D), F32)],
        compiler_params=_params("arbitrary", "arbitrary"),
    )(x1, target, nw2, sh2, sc2, g2, fw, wg, wu, wd)


def ffn_bwd(dx2, x1, f, g, u, nw2, sc2, g2, wg, wu, wd, name):
    L = x1.shape[0]
    tm = min(FFN_ROWS, L)

    def body(dx2_ref, x1_ref, f_ref, g_ref, u_ref, nw2_ref, sc2_ref, g2_ref, wg_ref, wu_ref, wd_ref,
             df_ref, dg_ref, du_ref, dx1_ref, sums_ref, df_scr, acc):
        i, j = pl.program_id(0), pl.program_id(1)

        @pl.when((i == 0) & (j == 0))
        def _():
            sums_ref[...] = jnp.zeros_like(sums_ref)

        @pl.when(j == 0)
        def _():
            dx2 = dx2_ref[...]
            df = (g2_ref[...] * dx2).astype(BF16)
            df_scr[...] = df
            df_ref[...] = df
            sums_ref[0:1, :] += _rowsum(dx2 * f_ref[...])
            acc[...] = jnp.zeros_like(acc)

        dh = _dot(df_scr[...], wd_ref[...], 1, 1)
        gv, uv = g_ref[...], u_ref[...]
        silu_g, dsilu_g = _silu_parts(gv)
        dg = (dh * uv * dsilu_g).astype(BF16)
        du = (dh * silu_g).astype(BF16)
        dg_ref[...] = dg
        du_ref[...] = du
        acc[...] += _dot(dg, wg_ref[...], 1, 1) + _dot(du, wu_ref[...], 1, 1)

        @pl.when(j == N_SHARD - 1)
        def _():
            dhx = acc[...]
            xv = x1_ref[...]
            r = lax.rsqrt(_lanemean(xv * xv) + EPS)
            n0 = xv * r
            nw = nw2_ref[...]
            dn2 = dhx * (1.0 + sc2_ref[...])
            dn0 = dn2 * nw
            dx1_ref[...] = dx2_ref[...] + r * (dn0 - n0 * _lanemean(dn0 * n0))
            sums_ref[1:2, :] += _rowsum(dhx)
            sums_ref[2:3, :] += _rowsum(dhx * n0 * nw)
            sums_ref[3:4, :] += _rowsum(dn2 * n0)

    row = pl.BlockSpec((tm, D), lambda i, j: (i, 0))
    vec = pl.BlockSpec((1, D), lambda i, j: (0, 0))
    sh = pl.BlockSpec((None, tm, FF_SH), lambda i, j: (j, i, 0))
    return pl.pallas_call(
        body, name=name,
        grid=(L // tm, N_SHARD),
        in_specs=[row, row, row, sh, sh, vec, vec, vec,
                  pl.BlockSpec((None, D, FF_SH), lambda i, j: (j, 0, 0)),
                  pl.BlockSpec((None, D, FF_SH), lambda i, j: (j, 0, 0)),
                  pl.BlockSpec((None, FF_SH, D), lambda i, j: (j, 0, 0))],
        out_specs=[row, sh, sh, row, pl.BlockSpec((8, D), lambda i, j: (0, 0))],
        out_shape=[jax.ShapeDtypeStruct((L, D), BF16),
                   jax.ShapeDtypeStruct((N_SHARD, L, FF_SH), BF16), jax.ShapeDtypeStruct((N_SHARD, L, FF_SH), BF16),
                   jax.ShapeDtypeStruct((L, D), F32), jax.ShapeDtypeStruct((8, D), F32)],
        scratch_shapes=[pltpu.VMEM((tm, D), BF16), pltpu.VMEM((tm, D), F32)],
        compiler_params=_params("arbitrary", "arbitrary"),
    )(dx2, x1, f, g, u, nw2, sc2, g2, wg, wu, wd)


def matmul_tn(a, b, name):
    na, K, M = a.shape
    nb, _, N = b.shape
    n = max(na, nb)
    tk = min(1024, K)
    tn = N if N <= 1024 else N // 2

    def body(a_ref, b_ref, o_ref):
        @pl.when(pl.program_id(2) == 0)
        def _():
            o_ref[...] = jnp.zeros_like(o_ref)

        o_ref[...] += _dot(a_ref[...], b_ref[...], 0, 0)

    return pl.pallas_call(
        body, name=name,
        grid=(n, N // tn, K // tk),
        in_specs=[pl.BlockSpec((None, tk, M), lambda s, j, kk: (s if na > 1 else 0, kk, 0)),
                  pl.BlockSpec((None, tk, tn), lambda s, j, kk: (s if nb > 1 else 0, kk, j))],
        out_specs=pl.BlockSpec((None, M, tn), lambda s, j, kk: (s, 0, j)),
        out_shape=jax.ShapeDtypeStruct((n, M, N), F32),
        compiler_params=_params("parallel", "parallel", "arbitrary"),
    )(a, b)


def matmul_tn_pair(a, b1, b2, name):
    K, M = a.shape
    n, _, N = b1.shape
    tk = min(1024, K)

    def body(a_ref, b1_ref, b2_ref, o1_ref, o2_ref):
        @pl.when(pl.program_id(1) == 0)
        def _():
            o1_ref[...] = jnp.zeros_like(o1_ref)
            o2_ref[...] = jnp.zeros_like(o2_ref)

        at = a_ref[...].T
        o1_ref[...] += _dot(at, b1_ref[...])
        o2_ref[...] += _dot(at, b2_ref[...])

    b_spec = pl.BlockSpec((None, tk, N), lambda s, kk: (s, kk, 0))
    o_spec = pl.BlockSpec((None, M, N), lambda s, kk: (s, 0, 0))
    return pl.pallas_call(
        body, name=name,
        grid=(n, K // tk),
        in_specs=[pl.BlockSpec((tk, M), lambda s, kk: (kk, 0)), b_spec, b_spec],
        out_specs=[o_spec, o_spec],
        out_shape=[jax.ShapeDtypeStruct((n, M, N), F32)] * 2,
        compiler_params=_params("parallel", "arbitrary"),
    )(a, b1, b2)


PIECE_COLS = 1024
N_PIECE_BLOCKS = D_IN // PIECE_COLS


def _piece_blocks(pieces):
    out, col = [], 0
    for arr, width in pieces:
        if arr is not None:
            out.append((arr, col // PIECE_COLS, width // PIECE_COLS))
        col += width
    assert col == D_IN
    return out


def _piece_feed(p_refs, blocks, buf, sems, tile_of, pos, total):
    def present(blk):
        ok = None
        for _, b0, nb in blocks:
            mine = (blk >= b0) & (blk < b0 + nb)
            ok = mine if ok is None else ok | mine
        return ok

    def fetch(step):
        blk, rows = tile_of(step)
        for p_ref, (_, b0, nb) in zip(p_refs, blocks):
            for t in range(nb):
                @pl.when(blk == b0 + t)
                def _(p_ref=p_ref, t=t):
                    pltpu.make_async_copy(p_ref.at[rows, pl.ds(t * PIECE_COLS, PIECE_COLS)], buf.at[step % 2],
                                          sems.at[step % 2]).start()

    @pl.when(pos == 0)
    def _():
        fetch(pos)

    @pl.when(pos + 1 < total)
    def _():
        fetch(pos + 1)

    def landed():
        slot = pos % 2
        pltpu.make_async_copy(p_refs[0].at[pl.ds(0, buf.shape[1]), pl.ds(0, PIECE_COLS)], buf.at[slot],
                              sems.at[slot]).wait()
        return buf.at[slot]

    return present(tile_of(pos)[0]), landed


def matmul_tn_pieces(a, pieces, name, extra=None, to_chips=()):
    K, M = a.shape
    blocks = _piece_blocks(pieces)
    tk = min(1024, K)
    nk = K // tk
    grid = (N_PIECE_BLOCKS, nk)
    nx, npc = len(to_chips), len(blocks)
    a2, blocks2 = (extra[0], _piece_blocks(extra[1])) if extra is not None else (None, [])
    npc2 = len(blocks2)

    def body(a_ref, *refs):
        p_refs = refs[:npc]
        refs = refs[npc:]
        a2_ref, p2_refs = (refs[0], refs[1:1 + npc2]) if npc2 else (None, ())
        refs = refs[1 + npc2:] if npc2 else refs
        o_ref = refs[nx]
        buf, sems = refs[2 * nx + 1:2 * nx + 3]
        rest = refs[2 * nx + 3:]
        pos, total = _grid_step(grid)
        if nx:
            start, finish = _to_chips_phases(refs[:nx], refs[nx + 1:2 * nx + 1], *rest[:2])
            pl.when(pos == 0)(start)
        here, landed = _piece_feed(p_refs, blocks, buf, sems,
                                   lambda s: (s // nk, pl.ds(pl.multiple_of((s % nk) * tk, tk), tk)), pos, total)
        blk, kk = pl.program_id(0), pl.program_id(1)

        @pl.when(kk == 0)
        def _():
            o_ref[...] = jnp.zeros_like(o_ref)

        @pl.when(here)
        def _():
            o_ref[...] += _dot(a_ref[...], landed()[...], 0, 0)

        if npc2:
            buf2, sem2 = rest[-2:]

            def tile(p_ref, t):
                return pltpu.make_async_copy(p_ref.at[:, pl.ds(t * PIECE_COLS, PIECE_COLS)], buf2, sem2.at[0])

            for p_ref, (_, b0, nb) in zip(p2_refs, blocks2):
                for t in range(nb):
                    @pl.when((blk == b0 + t) & (kk == 0))
                    def _(p_ref=p_ref, t=t):
                        tile(p_ref, t).start()

                    @pl.when((blk == b0 + t) & (kk == nk - 1))
                    def _(p_ref=p_ref, t=t):
                        tile(p_ref, t).wait()
                        o_ref[...] += _dot(a2_ref[...], buf2[...], 0, 0)

        if nx:
            pl.when(pos == total - 1)(finish)

    out_spec = pl.BlockSpec((M, PIECE_COLS), lambda blk, kk: (0, blk))
    in_specs = [pl.BlockSpec((tk, M), lambda blk, kk: (kk, 0))] + [ANY] * npc
    args = [a] + [arr for arr, _, _ in blocks]
    scratch = [pltpu.VMEM((2, tk, PIECE_COLS), BF16), pltpu.SemaphoreType.DMA((2,))]
    scratch += _to_chips_scratch(nx) if nx else []
    if npc2:
        in_specs += [pl.BlockSpec(a2.shape, lambda blk, kk: (0, 0))] + [ANY] * npc2
        args += [a2] + [arr for arr, _, _ in blocks2]
        scratch += [pltpu.VMEM((a2.shape[0], PIECE_COLS), BF16), pltpu.SemaphoreType.DMA((1,))]
    out = pl.pallas_call(
        body, name=name,
        grid=grid,
        in_specs=in_specs + [ANY] * nx,
        out_specs=[out_spec] + [ANY] * nx,
        out_shape=[jax.ShapeDtypeStruct((M, D_IN), F32)] + _to_chips_shapes(to_chips),
        scratch_shapes=scratch,
        compiler_params=_params("arbitrary", "arbitrary"),
    )(*args, *to_chips)
    return out if nx else out[0]


def dhx_normbwd(pieces, w, x, dx_res, nw, sc, name, to_chips=()):
    L = x.shape[0]
    tm = min(PROJ_ROWS, L)
    blocks = _piece_blocks(pieces)
    grid = (L // tm, N_PIECE_BLOCKS)
    nx, npc = len(to_chips), len(blocks)

    def body(*refs):
        p_refs = refs[:npc]
        w_ref, x_ref, res_ref, nw_ref, sc_ref = refs[npc:npc + 5]
        refs = refs[npc + 5:]
        dx_ref, sums_ref = refs[nx:nx + 2]
        acc, buf, sems = refs[2 * nx + 2:2 * nx + 5]
        pos, total = _grid_step(grid)
        if nx:
            start, finish = _to_chips_phases(refs[:nx], refs[nx + 2:2 * nx + 2], *refs[2 * nx + 5:])
            pl.when(pos == 0)(start)
            pl.when(pos == total - 1)(finish)
        here, landed = _piece_feed(
            p_refs, blocks, buf, sems,
            lambda s: (s % N_PIECE_BLOCKS, pl.ds(pl.multiple_of((s // N_PIECE_BLOCKS) * tm, tm), tm)), pos, total)
        i, blk = pl.program_id(0), pl.program_id(1)

        @pl.when((i == 0) & (blk == 0))
        def _():
            sums_ref[...] = jnp.zeros_like(sums_ref)

        @pl.when(blk == 0)
        def _():
            acc[...] = jnp.zeros_like(acc)

        @pl.when(here)
        def _():
            acc[...] += _dot(landed()[...], w_ref[...], 1, 1)

        @pl.when(blk == N_PIECE_BLOCKS - 1)
        def _():
            dhx = acc[...]
            xv = x_ref[...]
            r = lax.rsqrt(_lanemean(xv * xv) + EPS)
            n0 = xv * r
            nw = nw_ref[...]
            dn = dhx * (1.0 + sc_ref[...])
            dn0 = dn * nw
            dx_ref[...] = res_ref[...] + r * (dn0 - n0 * _lanemean(dn0 * n0))
            sums_ref[0:1, :] += _rowsum(dhx)
            sums_ref[1:2, :] += _rowsum(dhx * n0 * nw)
            sums_ref[2:3, :] += _rowsum(dn * n0)

    row = pl.BlockSpec((tm, D), lambda i, blk: (i, 0))
    vec = pl.BlockSpec((1, D), lambda i, blk: (0, 0))
    return pl.pallas_call(
        body, name=name,
        grid=grid,
        in_specs=[ANY] * npc + [pl.BlockSpec((D, PIECE_COLS), lambda i, blk: (0, blk)), row, row, vec, vec] + [ANY] * nx,
        out_specs=[row, pl.BlockSpec((8, D), lambda i, blk: (0, 0))] + [ANY] * nx,
        out_shape=[jax.ShapeDtypeStruct((L, D), F32), jax.ShapeDtypeStruct((8, D), F32)] + _to_chips_shapes(to_chips),
        scratch_shapes=[pltpu.VMEM((tm, D), F32), pltpu.VMEM((2, tm, PIECE_COLS), BF16), pltpu.SemaphoreType.DMA((2,))]
        + (_to_chips_scratch(nx) if nx else []),
        compiler_params=_params("arbitrary", "arbitrary"),
    )(*[arr for arr, _, _ in blocks], w, x, dx_res, nw, sc, *to_chips)


SMALL_ROWS = 24


def _rope_tables(L):
    rows = L // 64
    freqs = 10000.0 ** (-jnp.arange(RT_DK // 4, dtype=F32) / (RT_DK // 4))
    a_row = jnp.arange(rows, dtype=F32)[:, None] * freqs
    a_col = jnp.arange(64, dtype=F32)[:, None] * freqs

    def spread(f):
        return jnp.concatenate([jnp.repeat(f(a_row), 64, axis=0), jnp.tile(f(a_col), (rows, 1))], axis=-1)

    cos, sin = spread(jnp.cos), spread(jnp.sin)
    return jnp.concatenate([cos, cos], axis=1), jnp.concatenate([-sin, sin], axis=1)


def _pieces(hq, hf_f, hf_b, hi, hg, rq, rk, rv, rg, ga, gb):
    widths = (D, D, D, D, D, D, D, 2 * D, 2 * D, D, D)
    return list(zip((hq, hf_f, hf_b, hi, hg, rq, rk, rv, rg, ga, gb), widths))


def _lane0(a):
    return a[:, 0, 0]


def _pack_small(rows):
    out = [r.reshape(1, D) for r in rows]
    out += [jnp.zeros((1, D), F32)] * (SMALL_ROWS - len(out))
    return jnp.concatenate(out, axis=0)


def _other_half(g, core):
    axis = g.ndim - 2
    h = g.shape[axis] // 2
    return lax.dynamic_slice_in_dim(g, (1 - core) * h, h, axis=axis).astype(BF16)


def _sibling_sums(gs, names, place):
    core, core_arg, _ = place
    received = rs_to_sibling([_other_half(g, core) for g in gs], "rs_to_sibling_" + names[0])
    return [rs_add_sibling(g, r, core_arg, "rs_add_sibling_" + k) for g, r, k in zip(gs, received, names)]


def _staged_in_proj(x, nw, sh, sc, w_shard, rest, chip):
    cx, cy = chip // 2, chip % 2

    def arg(k):
        return jnp.reshape(k, (1,)).astype(jnp.int32)

    p, p16, hx, w_full = in_proj_own(x, nw, sh, sc, w_shard, arg(chip), "in_proj_own")
    p, p16, w_full = in_proj_next(hx, w_full, arg(2 * (1 - cx) + cy), (p, p16), "in_proj_x", diag_from=w_shard)
    w_pa, w_pb, w_out, w_wd = rest[0], rest[1], rest[2], rest[5]
    p, p16, g_pa, g_pb, g_out, g_wd = in_proj_next(hx, w_full, arg(2 * cx + 1 - cy), (p, p16), "in_proj_y",
                                                   gather=[w_pa, w_pb, w_out, w_wd])
    p, p16, g_wg, g_wu = in_proj_next(hx, w_full, arg(3 - chip), (p, p16), "in_proj_diag", gather=[rest[3], rest[4]])
    w = {"w_in": w_full, "w_pa": g_pa.reshape(D, D), "w_pb": g_pb.reshape(2 * D, D), "w_out": g_out.reshape(D, D),
         "wg": g_wg, "wu": g_wu, "wd": g_wd}
    return p, p16, hx, w


def local_step(x, ctx, target, mod_x, mod_c, lb_f, lb_b, lg_f, lg_b, nw1, nw2, hgw, fw, w, rest=None, place=None):
    L, Lc = x.shape[0], ctx.shape[0]
    sh1, sc1, g1, sh2, sc2, g2 = (mod_x[i:i + 1] for i in range(6))
    sh1c, sc1c = mod_c[0:1], mod_c[1:2]
    cosf, sinf = _rope_tables(L)
    cosc, sinc = jnp.ones((Lc, RT_DK), F32), jnp.zeros((Lc, RT_DK), F32)
    zero_h = jnp.zeros((HEADS, HG_D, HG_D), F32)
    zero_r = jnp.zeros((HEADS, RT_DV, RT_DK), F32)

    if rest is None:
        p, hx, p16 = normmod_matmul(x, nw1, sh1, sc1, w["w_in"], "in_proj")
    else:
        p, p16, hx, w = _staged_in_proj(x, nw1, sh1, sc1, w["w_in_shard"], rest, place[2][0])
    pc, hxc, pc16 = normmod_matmul(ctx, nw1, sh1c, sc1c, w["w_in"], "ctx_in_proj")
    _, s_hf, cb_hf = hgrn_scan_fwd(pc, lb_f, zero_h, COL_HFF, False, "ctx_hgrn_f")
    _, s_hb, cb_hb = hgrn_scan_fwd(pc, lb_b, zero_h, COL_HFB, True, "ctx_hgrn_b")
    _, s_rf, cb_rf = ret_scan_fwd(pc16,cosc, sinc, lg_f, zero_r, False, "ctx_ret_f")
    _, s_rb, cb_rb = ret_scan_fwd(pc16,cosc, sinc, lg_b, zero_r, True, "ctx_ret_b")
    ohf, _, xb_hf = hgrn_scan_fwd(p, lb_f, s_hf, COL_HFF, False, "hgrn_f")
    o_hg, _, xb_hb = hgrn_scan_fwd(p, lb_b, s_hb, COL_HFB, True, "hgrn_b", prev=ohf)
    orf, _, xb_rf = ret_scan_fwd(p16,cosf, sinf, lg_f, s_rf, False, "ret_f")
    o_rt, _, xb_rb = ret_scan_fwd(p16,cosf, sinf, lg_b, s_rb, True, "ret_b", prev=orf)
    x1, x_mix, merged, ya, yb = mix_fwd(o_hg, o_rt, p, x, g1, hgw, w["w_pa"], w["w_pb"], w["w_out"], "mix_fwd")
    hx2, gg, uu, hh, ff, dx2, sums_f = ffn_fwd(x1, target, nw2, sh2, sc2, g2, fw, w["wg"], w["wu"], w["wd"], "ffn_fwd")

    d_f, d_g, d_u, dx1, sums_fb = ffn_bwd(dx2, x1, ff, gg, uu, nw2, sc2, g2, w["wg"], w["wu"], w["wd"], "ffn_bwd")
    dw_gate, dw_up = matmul_tn_pair(hx2, d_g, d_u, "dw_ffn_gate_up")
    grads = {"wg": dw_gate, "wu": dw_up, "wd": matmul_tn(hh, d_f[None], "dw_ffn_down")}
    ffn_names = ["wg", "wu", "wd"]
    swap = () if place is None else [_other_half(grads[k], place[0]) for k in ffn_names]
    dxm, d_a, d_b, dga, dgb, dhg, drg, dohg, dort, sums_m, *from_sibling = mix_bwd(
        dx1, x_mix, ya, yb, o_hg, o_rt, p, g1, hgw, w["w_pa"], w["w_pb"], w["w_out"], "mix_bwd", to_sibling=swap)
    grads["w_out"] = matmul_tn(merged[None], dxm[None], "dw_out").reshape(N_SHARD, D // N_SHARD, D)
    grads["w_pa"] = matmul_tn(ya[None], d_a[None], "dw_proj_hgrn").reshape(N_SHARD, D // N_SHARD, D)
    grads["w_pb"] = matmul_tn(yb[None], d_b[None], "dw_proj_ret").reshape(N_SHARD, 2 * D // N_SHARD, D)

    rq1, rk1, rv1, dlgf_x, ds_rf = ret_scan_bwd(p16,cosf, sinf, lg_f, xb_rf, dort, zero_r, None, False, "ret_f_bwd")
    drq, drk, drv, dlgb_x, ds_rb = ret_scan_bwd(p16,cosf, sinf, lg_b, xb_rb, dort, zero_r, (rq1, rk1, rv1), True, "ret_b_bwd")
    hq1, dzf, hv1, dlbf_x, ds_hf = hgrn_scan_bwd(p, lb_f, xb_hf, dohg, zero_h, None, COL_HFF, False, "hgrn_f_bwd")
    dhq, dzb, dhv, dlbb_x, ds_hb = hgrn_scan_bwd(p, lb_b, xb_hb, dohg, zero_h, (hq1, hv1), COL_HFB, True, "hgrn_b_bwd")
    dp = _pieces(dhq, dzf, dzb, dhv, dhg, drq, drk, drv, drg, dga, dgb)

    zc = jnp.zeros((Lc, D), F32)
    zc2 = jnp.zeros((Lc, 2 * D), F32)
    crq1, crk1, crv1, dlgf_c, _ = ret_scan_bwd(pc16,cosc, sinc, lg_f, cb_rf, zc2, ds_rf, None, False, "ctx_ret_f_bwd")
    cdrq, cdrk, cdrv, dlgb_c, _ = ret_scan_bwd(pc16,cosc, sinc, lg_b, cb_rb, zc2, ds_rb, (crq1, crk1, crv1), True, "ctx_ret_b_bwd")
    chq1, cdzf, chv1, dlbf_c, _ = hgrn_scan_bwd(pc, lb_f, cb_hf, zc, ds_hf, None, COL_HFF, False, "ctx_hgrn_f_bwd")
    cdhq, cdzb, cdhv, dlbb_c, _ = hgrn_scan_bwd(pc, lb_b, cb_hb, zc, ds_hb, (chq1, chv1), COL_HFB, True, "ctx_hgrn_b_bwd")
    dpc = _pieces(cdhq, cdzf, cdzb, cdhv, None, cdrq, cdrk, cdrv, None, None, None)
    _, sums_c = dhx_normbwd(dpc, w["w_in"], ctx, zc, nw1, sc1c, "dctx_in_proj")

    others = ["w_pa", "w_pb", "w_out", "wg", "wu", "wd"]
    if place is None:
        grads["w_in"] = matmul_tn_pieces(hx, dp, "dw_in", extra=(hxc, dpc))
        dx, sums_x = dhx_normbwd(dp, w["w_in"], x, dx1, nw1, sc1, "dx_in_proj")
    else:
        sums_o = _sibling_sums([grads[k] for k in others[:3]], others[:3], place)
        sums_o += [rs_add_sibling(grads[k], r, place[1], "rs_add_sibling_" + k) for k, r in zip(ffn_names, from_sibling)]
        grads["w_in"], *recv_o = matmul_tn_pieces(hx, dp, "dw_in", extra=(hxc, dpc),
                                                  to_chips=[a16 for _, a16 in sums_o])
        sums_i = _sibling_sums([grads["w_in"]], ["w_in"], place)
        dx, sums_x, recv_i = dhx_normbwd(dp, w["w_in"], x, dx1, nw1, sc1, "dx_in_proj", to_chips=[sums_i[0][1]])
        names = ["w_in"] + others
        halves = [rs_add_chips(a, r, place[2], "rs_add_chips_" + k)
                  for (a, _), r, k in zip(sums_i + sums_o, [recv_i] + recv_o, names)]
        grads = dict(zip(names, rs_join_halves(halves, "rs_join_halves")))

    def lg_row(f, b):
        return jnp.concatenate([_lane0(f), _lane0(b), jnp.zeros((D - 2 * HEADS,), F32)])

    small = _pack_small([
        sums_x[0], sums_x[1], sums_m[0], sums_fb[1], sums_fb[2], sums_fb[0],
        sums_c[0], sums_c[1],
        sums_x[2], sums_c[2], sums_fb[3], sums_m[1], sums_f[0],
        dlbf_x, dlbf_c, dlbb_x, dlbb_c,
        lg_row(dlgf_x, dlgb_x), lg_row(dlgf_c, dlgb_c),
        sums_f[1],
    ])
    return dx, grads, small


MESH = pl.DeviceIdType.MESH
ANY = pl.BlockSpec(memory_space=pl.ANY)
N_DEV = 8


def _place():
    return lax.axis_index("x"), lax.axis_index("y"), lax.axis_index("c")


def _other_chips(x, y):
    return [(1 - x, y), (x, 1 - y), (1 - x, 1 - y)]


def allgather8(xs, name):
    m, n = xs.shape

    def body(x_ref, out_ref, send_sems, recv_sems, local_sem):
        x, y, c = _place()
        me, sibling = (x, y, c), (x, y, 1 - c)
        chips = _other_chips(x, y)

        def rows(px, py, pc):
            return out_ref.at[pl.ds((4 * px + 2 * py + pc) * m, m), :]

        def copy(k, block, to, src=None):
            return pltpu.make_async_remote_copy(
                src_ref=rows(*block) if src is None else src, dst_ref=rows(*block),
                send_sem=send_sems.at[k], recv_sem=recv_sems.at[k], device_id=to, device_id_type=MESH)

        mine = pltpu.make_async_copy(x_ref, rows(*me), local_sem)
        mine.start()
        first = [copy(0, me, sibling, src=x_ref)]
        first += [copy(1 + j, me, (*chip, c), src=x_ref) for j, chip in enumerate(chips)]
        for cp in first:
            cp.start()
        passed = [copy(4 + j, (*chip, c), sibling) for j, chip in enumerate(chips)]
        for j, chip in enumerate(chips):
            copy(1 + j, (*chip, c), me).wait_recv()
            passed[j].start()
        copy(0, sibling, me).wait_recv()
        for j, chip in enumerate(chips):
            copy(4 + j, (*chip, 1 - c), me).wait_recv()
        for cp in first + passed:
            cp.wait_send()
        mine.wait()

    return pl.pallas_call(
        body, name=name,
        out_shape=jax.ShapeDtypeStruct((N_DEV * m, n), xs.dtype),
        in_specs=[pl.BlockSpec(memory_space=pltpu.VMEM)],
        out_specs=pl.BlockSpec(memory_space=pltpu.VMEM),
        scratch_shapes=[pltpu.SemaphoreType.DMA((7,)), pltpu.SemaphoreType.DMA((7,)), pltpu.SemaphoreType.DMA],
    )(xs)


def _gather_phases(ins, outs, send_sems, recv_sems, local_sems, relations=(0, 1, 2), stage=None):
    n = len(ins)
    x, y, c = _place()
    chips = _other_chips(x, y)

    def rows(i, core):
        h = ins[i].shape[0] // 2
        return pl.ds(pl.multiple_of(core * h, 16), h)

    def region(i, k, rs):
        if len(outs[i].shape) == 2:
            cols = ins[i].shape[1]
            return outs[i].at[rs, pl.ds(pl.multiple_of(k * cols, 128), cols)]
        return outs[i].at[k, rs, :]

    def landed(i, chip, core):
        return region(i, 2 * chip[0] + chip[1], rows(i, core))

    def copy(i, k, src, dst, to):
        return pltpu.make_async_remote_copy(src_ref=src, dst_ref=dst, send_sem=send_sems.at[6 * i + k],
                                            recv_sem=recv_sems.at[6 * i + k], device_id=to, device_id_type=MESH)

    def lift(i):
        return pltpu.make_async_copy(ins[i], stage[i], local_sems.at[i])

    def drop(i):
        return pltpu.make_async_copy(stage[i], region(i, 2 * x + y, pl.ds(0, ins[i].shape[0])), local_sems.at[i])

    def send(i, j):
        return copy(i, j, ins[i].at[rows(i, c), :], landed(i, (x, y), c), (*chips[j], c))

    def arrived(i, j, core, k):
        return copy(i, k, ins[i].at[rows(i, core), :], landed(i, chips[j], core), (x, y, 1 - c))

    def passed(i, j):
        return copy(i, 3 + j, landed(i, chips[j], c), landed(i, chips[j], c), (x, y, 1 - c))

    def start():
        for i in range(n):
            if stage is not None:
                lift(i).start()
            for j in relations:
                send(i, j).start()

    def forward():
        for i in range(n):
            if stage is not None:
                lift(i).wait()
                drop(i).start()
            for j in relations:
                arrived(i, j, c, j).wait_recv()
                passed(i, j).start()

    def finish():
        for i in range(n):
            for j in relations:
                arrived(i, j, 1 - c, 3 + j).wait_recv()
        for i in range(n):
            for j in relations:
                send(i, j).wait_send()
                passed(i, j).wait_send()
            if stage is not None:
                drop(i).wait()

    return start, forward, finish


def _gather_scratch(n):
    return [pltpu.SemaphoreType.DMA((6 * n,)), pltpu.SemaphoreType.DMA((6 * n,)), pltpu.SemaphoreType.DMA((n,))]


def rs_to_sibling(payloads, name):
    n = len(payloads)

    def body(*refs):
        start, finish = _to_sibling_phases(refs[:n], refs[n:2 * n], *refs[2 * n:])
        start()
        finish()

    return pl.pallas_call(
        body, name=name,
        out_shape=[jax.ShapeDtypeStruct(g.shape, g.dtype) for g in payloads],
        in_specs=[ANY] * n, out_specs=[ANY] * n,
        scratch_shapes=_to_sibling_scratch(n),
    )(*payloads)


def _to_sibling_phases(ins, outs, send_sems, recv_sems):
    def copies():
        x, y, c = _place()
        return [pltpu.make_async_remote_copy(src_ref=ins[i], dst_ref=outs[i], send_sem=send_sems.at[i],
                                             recv_sem=recv_sems.at[i], device_id=(x, y, 1 - c), device_id_type=MESH)
                for i in range(len(ins))]

    def start():
        for cp in copies():
            cp.start()

    def finish():
        for cp in copies():
            cp.wait()

    return start, finish


def _to_sibling_scratch(n):
    return [pltpu.SemaphoreType.DMA((n,)), pltpu.SemaphoreType.DMA((n,))]


def _to_chips_phases(ins, outs, send_sems, recv_sems):
    def copies():
        x, y, c = _place()
        return [pltpu.make_async_remote_copy(
            src_ref=ins[i].at[2 * px + py], dst_ref=outs[i].at[j], send_sem=send_sems.at[3 * i + j],
            recv_sem=recv_sems.at[3 * i + j], device_id=(px, py, c), device_id_type=MESH)
            for i in range(len(ins)) for j, (px, py) in enumerate(_other_chips(x, y))]

    def start():
        for cp in copies():
            cp.start()

    def finish():
        for cp in copies():
            cp.wait()

    return start, finish


def _to_chips_shapes(parts):
    return [jax.ShapeDtypeStruct((3,) + a.shape[1:], a.dtype) for a in parts]


def _to_chips_scratch(n):
    return [pltpu.SemaphoreType.DMA((3 * n,)), pltpu.SemaphoreType.DMA((3 * n,))]


def rs_join_halves(fulls, name):
    n = len(fulls)

    def body(*refs):
        outs = refs[n:2 * n]
        send_sems, recv_sems = refs[2 * n:]
        x, y, c = _place()

        def copy(i, core):
            h = fulls[i].shape[0] // 2
            rows = outs[i].at[pl.ds(pl.multiple_of(core * h, 8), h), :]
            return pltpu.make_async_remote_copy(src_ref=rows, dst_ref=rows, send_sem=send_sems.at[i],
                                                recv_sem=recv_sems.at[i], device_id=(x, y, 1 - c), device_id_type=MESH)

        sent = [copy(i, c) for i in range(n)]
        for cp in sent:
            cp.start()
        for i in range(n):
            copy(i, 1 - c).wait_recv()
        for cp in sent:
            cp.wait_send()

    return pl.pallas_call(
        body, name=name,
        out_shape=[jax.ShapeDtypeStruct(a.shape, a.dtype) for a in fulls],
        in_specs=[ANY] * n, out_specs=[ANY] * n,
        input_output_aliases={i: i for i in range(n)},
        scratch_shapes=[pltpu.SemaphoreType.DMA((n,)), pltpu.SemaphoreType.DMA((n,))],
    )(*fulls)


def _row_tile(rows, cols, limit_bytes=2 * 1024 * 1024, mult=8):
    best = mult
    for t in range(mult, rows + 1, mult):
        if rows % t == 0 and t * cols * 4 <= limit_bytes:
            best = t
    return best


def rs_add_sibling(g, recv, c, name):
    if g.ndim == 2:
        h, C = recv.shape[0], recv.shape[1] // N_SHARD
    else:
        _, h, C = recv.shape
    tr = _row_tile(h, C, mult=16)
    nt = h // tr

    def body(c_ref, g_ref, r_ref, o_ref, o16_ref):
        s = g_ref[...] + r_ref[...].astype(F32)
        o_ref[...] = s
        o16_ref[...] = s.astype(BF16)

    blk = pl.BlockSpec((None, tr, C), lambda k, i, c_ref: (k, i, 0))
    if g.ndim == 2:
        g_spec = pl.BlockSpec((tr, C), lambda k, i, c_ref: (c_ref[0] * nt + i, k))
        r_spec = pl.BlockSpec((tr, C), lambda k, i, c_ref: (i, k))
    else:
        g_spec = pl.BlockSpec((None, tr, C), lambda k, i, c_ref: (k, c_ref[0] * nt + i, 0))
        r_spec = blk
    return pl.pallas_call(
        body, name=name,
        grid_spec=pltpu.PrefetchScalarGridSpec(
            num_scalar_prefetch=1, grid=(N_SHARD, nt),
            in_specs=[g_spec, r_spec],
            out_specs=[blk, blk]),
        out_shape=[jax.ShapeDtypeStruct((N_SHARD, h, C), F32), jax.ShapeDtypeStruct((N_SHARD, h, C), BF16)],
        compiler_params=_params("parallel", "parallel"),
    )(c, g, recv)


def rs_add_chips(part, recv, place, name):
    _, h, C = part.shape
    tr = _row_tile(h, C, mult=16)
    nt = h // tr

    def body(k_ref, p_ref, r_ref, o_ref):
        o_ref[...] = ((p_ref[...] + r_ref[0].astype(F32)) + r_ref[1].astype(F32)) + r_ref[2].astype(F32)

    return pl.pallas_call(
        body, name=name,
        grid_spec=pltpu.PrefetchScalarGridSpec(
            num_scalar_prefetch=1, grid=(nt,),
            in_specs=[pl.BlockSpec((None, tr, C), lambda i, k_ref: (k_ref[0], i, 0)),
                      pl.BlockSpec((3, tr, C), lambda i, k_ref: (0, i, 0))],
            out_specs=pl.BlockSpec((tr, C), lambda i, k_ref: (k_ref[1] * nt + i, 0))),
        out_shape=jax.ShapeDtypeStruct((2 * h, C), F32),
        compiler_params=_params("parallel"),
    )(place, part, recv)


def _adamw_math(w, g, m, v):
    m = ADAM_B1 * m + (1.0 - ADAM_B1) * g
    v = ADAM_B2 * v + (1.0 - ADAM_B2) * (g * g)
    m_hat = m / (1.0 - ADAM_B1 ** ADAM_STEP)
    v_hat = v / (1.0 - ADAM_B2 ** ADAM_STEP)
    delta = -ADAM_LR * (m_hat / (jnp.sqrt(v_hat) + ADAM_EPS) + ADAM_WD * w)
    return delta, m, v


def adamw(w, g, m, v, name):
    R, C = w.shape
    tr = _row_tile(R, C, 1024 * 1024)

    def body(w_ref, g_ref, m_ref, v_ref, go_ref, d_ref, nm_ref, nv_ref):
        g = g_ref[...]
        go_ref[...] = g
        d_ref[...], nm_ref[...], nv_ref[...] = _adamw_math(w_ref[...], g, m_ref[...], v_ref[...])

    blk = pl.BlockSpec((tr, C), lambda i: (i, 0))
    return pl.pallas_call(
        body, name=name, grid=(R // tr,), in_specs=[blk] * 4, out_specs=[blk] * 4,
        out_shape=[jax.ShapeDtypeStruct((R, C), F32)] * 4,
        compiler_params=_params("parallel"),
    )(w, g, m, v)


MOD_SH = 6 * D // N_SHARD
PK_ROWS = 16


def mod_fwd(call16, w_sh, b_sh, name):
    def body(c_ref, w_ref, b_ref, o_ref):
        o_ref[...] = _dot(_silu_parts(c_ref[...])[0], w_ref[...], prec=HI) + b_ref[...]

    return pl.pallas_call(body, name=name, out_shape=jax.ShapeDtypeStruct((16, MOD_SH), F32),
                          compiler_params=_params())(call16, w_sh, b_sh)


def prep_small(lbf2, lbb2, theta_row, name):
    def body(f_ref, b_ref, t_ref, lbf_ref, lbb_ref, lg_ref):
        lbf_ref[...] = _sigmoid(f_ref[0:1, :] - f_ref[1:2, :])
        lbb_ref[...] = _sigmoid(b_ref[0:1, :] - b_ref[1:2, :])
        t = t_ref[...]
        lg_ref[...] = jnp.minimum(t, 0.0) - jnp.log(1.0 + jnp.exp(-jnp.abs(t)))

    row = jax.ShapeDtypeStruct((1, D), F32)
    return pl.pallas_call(body, name=name, out_shape=[row, row, row], compiler_params=_params())(lbf2, lbb2, theta_row)


def small_grads(g3, lbf, lbb, theta_row, name):
    def body(g_ref, lbf_ref, lbb_ref, t_ref, pk_ref, aux_ref):
        s = g_ref[0]
        for d in range(1, N_DEV):
            s = s + g_ref[d]
        pk_ref[...] = jnp.zeros_like(pk_ref)
        aux_ref[...] = jnp.zeros_like(aux_ref)
        pk_ref[1:7, :] = s[0:6]
        pk_ref[1:3, :] += s[6:8]
        pk_ref[7:8, :] = s[8:9] + s[9:10]
        pk_ref[8:9, :] = s[10:11]
        lbf, lbb = lbf_ref[...], lbb_ref[...]
        daf = (s[13:14] + s[14:15]) * lbf * (1.0 - lbf)
        dab = (s[15:16] + s[16:17]) * lbb * (1.0 - lbb)
        pk_ref[9:10, :] = daf
        pk_ref[10:11, :] = -daf
        pk_ref[11:12, :] = dab
        pk_ref[12:13, :] = -dab
        pk_ref[13:14, :] = s[11:12]
        pk_ref[14:15, :] = (s[17:18] + s[18:19]) * _sigmoid(-t_ref[...])
        pk_ref[15:16, :] = s[12:13]
        aux_ref[0:2, :] = s[6:8]
        aux_ref[2:3, :] = jnp.broadcast_to(jnp.sum(s[19:20], axis=-1, keepdims=True), (1, D))

    return pl.pallas_call(body, name=name,
                          out_shape=[jax.ShapeDtypeStruct((PK_ROWS, D), F32), jax.ShapeDtypeStruct((8, D), F32)],
                          compiler_params=_params())(g3, lbf, lbb, theta_row)


def mod_bwd(call16, dmod_sh, w_sh, name):
    def body(c_ref, d_ref, w_ref, dw_ref, ds_ref):
        dm = d_ref[...]
        dw_ref[...] = _dot(_silu_parts(c_ref[...])[0], dm, 0, 0, prec=HI)
        ds_ref[...] = jnp.zeros_like(ds_ref)
        ds_ref[0:1, :] = _dot(dm[8:9, :], w_ref[...], 1, 1, prec=HI)

    return pl.pallas_call(body, name=name,
                          out_shape=[jax.ShapeDtypeStruct((D, MOD_SH), F32), jax.ShapeDtypeStruct((8, D), F32)],
                          compiler_params=_params())(call16, dmod_sh, w_sh)


def adamw_small(g4, pk_g, pk_w, pk_m, pk_v, name):
    def body(g4_ref, g_ref, w_ref, m_ref, v_ref, go_ref, d_ref, nm_ref, nv_ref):
        w = w_ref[...]
        ds = ((g4_ref[0:1, :] + g4_ref[16:17, :]) + g4_ref[32:33, :]) + g4_ref[48:49, :]
        row = lax.broadcasted_iota(jnp.int32, (PK_ROWS, D), 0)
        g = jnp.where(row == 0, ds * _silu_parts(w[0:1, :])[1], g_ref[...])
        go_ref[...] = g
        d_ref[...], nm_ref[...], nv_ref[...] = _adamw_math(w, g, m_ref[...], v_ref[...])

    pk = jax.ShapeDtypeStruct((PK_ROWS, D), F32)
    return pl.pallas_call(body, name=name, out_shape=[pk, pk, pk, pk], compiler_params=_params())(g4, pk_g, pk_w, pk_m, pk_v)


def _pack_params(c_ctx, b_mod, n1, n2, lbf, lbb, hgn, th_f, th_b, fin):
    theta = jnp.concatenate([th_f.reshape(HEADS), th_b.reshape(HEADS), jnp.zeros((D - 2 * HEADS,), F32)])
    return jnp.concatenate([c_ctx.reshape(1, D), b_mod.reshape(6, D), n1.reshape(1, D), n2.reshape(1, D), lbf, lbb,
                            hgn.reshape(1, D), theta.reshape(1, D), fin.reshape(1, D)], axis=0)


def _unpack_params(pk):
    return (pk[0], pk[1:7].reshape(1, 6 * D), pk[7:8], pk[8:9], pk[9:11], pk[11:13], pk[13:14],
            pk[14, 0:HEADS].reshape(1, HEADS), pk[14, HEADS:2 * HEADS].reshape(1, HEADS), pk[15])


def kernel(x, c, ctx, c_ctx, w_mod, b_mod, norm1_w, norm2_w, w_in, hg_lb_fwd, hg_lb_bwd, hg_norm_w, rt_theta_fwd, rt_theta_bwd, w_proj_hgrn, w_proj_ret, w_out, w_ffn_gate, w_ffn_up, w_ffn_down, final_norm_w, loss_target, m_c_ctx, m_w_mod, m_b_mod, m_norm1_w, m_norm2_w, m_w_in, m_hg_lb_fwd, m_hg_lb_bwd, m_hg_norm_w, m_rt_theta_fwd, m_rt_theta_bwd, m_w_proj_hgrn, m_w_proj_ret, m_w_out, m_w_ffn_gate, m_w_ffn_up, m_w_ffn_down, m_final_norm_w, v_c_ctx, v_w_mod, v_b_mod, v_norm1_w, v_norm2_w, v_w_in, v_hg_lb_fwd, v_hg_lb_bwd, v_hg_norm_w, v_rt_theta_fwd, v_rt_theta_bwd, v_w_proj_hgrn, v_w_proj_ret, v_w_out, v_w_ffn_gate, v_w_ffn_up, v_w_ffn_down, v_final_norm_w):
    xi, yi, ci = _place()
    dev = 4 * xi + 2 * yi + ci
    chip = 2 * xi + yi
    core_arg = jnp.reshape(ci, (1,)).astype(jnp.int32)
    place_arg = jnp.stack([chip, ci]).astype(jnp.int32)

    c_all = allgather8(jnp.concatenate([c, jnp.zeros((7, D), F32)], axis=0), "gather_c").reshape(N_DEV, 8, D)[:, 0]
    call16 = jnp.concatenate([c_all, c_ctx.reshape(1, D), jnp.zeros((7, D), F32)], axis=0)
    b_sh = lax.dynamic_slice_in_dim(b_mod, chip * MOD_SH, MOD_SH, axis=1)
    mod_sh = mod_fwd(call16, w_mod[0], b_sh, "mod_fwd")
    mod_g = allgather8(mod_sh, "gather_mod").reshape(N_DEV, 16, MOD_SH)
    mod_all = jnp.concatenate([mod_g[0], mod_g[2], mod_g[4], mod_g[6]], axis=1)
    mod_x = lax.dynamic_index_in_dim(mod_all, dev, axis=0, keepdims=False).reshape(6, D)
    mod_c = mod_all[8].reshape(6, D)

    pk_w = _pack_params(c_ctx, b_mod, norm1_w, norm2_w, hg_lb_fwd, hg_lb_bwd, hg_norm_w, rt_theta_fwd, rt_theta_bwd, final_norm_w)
    theta_row = pk_w[14:15]
    lb_f, lb_b, lg_row = prep_small(hg_lb_fwd, hg_lb_bwd, theta_row, "prep_small")
    lg_f = jnp.broadcast_to(lg_row[0, 0:HEADS].reshape(HEADS, 1, 1), (HEADS, 1, RT_DV))
    lg_b = jnp.broadcast_to(lg_row[0, HEADS:2 * HEADS].reshape(HEADS, 1, 1), (HEADS, 1, RT_DV))

    rest = [s[0].astype(BF16) for s in (w_proj_hgrn, w_proj_ret, w_out, w_ffn_gate, w_ffn_up, w_ffn_down)]

    dx, full, small = local_step(x[0], ctx[0], loss_target[0], mod_x, mod_c, lb_f, lb_b, lg_f, lg_b,
                                 norm1_w, norm2_w, hg_norm_w, final_norm_w.reshape(1, D),
                                 {"w_in_shard": w_in[0].astype(BF16)}, rest, (ci, core_arg, place_arg))

    g3 = allgather8(small, "gather_small").reshape(N_DEV, SMALL_ROWS, D)
    pk_g, aux = small_grads(g3, lb_f, lb_b, theta_row, "small_grads")
    loss = aux[2, 0]
    dmod16 = jnp.concatenate([
        g3[:, 0:6, :].reshape(N_DEV, 6 * D),
        jnp.concatenate([aux[0], aux[1], jnp.zeros((4 * D,), F32)]).reshape(1, 6 * D),
        jnp.zeros((7, 6 * D), F32)], axis=0)
    dmod_sh = lax.dynamic_slice_in_dim(dmod16, chip * MOD_SH, MOD_SH, axis=1)
    g_wmod, dsilu = mod_bwd(call16, dmod_sh, w_mod[0], "mod_bwd")
    g4 = allgather8(dsilu, "gather_dsilu")
    pk_m = _pack_params(m_c_ctx, m_b_mod, m_norm1_w, m_norm2_w, m_hg_lb_fwd, m_hg_lb_bwd, m_hg_norm_w, m_rt_theta_fwd, m_rt_theta_bwd, m_final_norm_w)
    pk_v = _pack_params(v_c_ctx, v_b_mod, v_norm1_w, v_norm2_w, v_hg_lb_fwd, v_hg_lb_bwd, v_hg_norm_w, v_rt_theta_fwd, v_rt_theta_bwd, v_final_norm_w)
    pk_g, pk_d, pk_nm, pk_nv = adamw_small(g4, pk_g, pk_w, pk_m, pk_v, "adamw_small")

    big = {
        "w_mod": (g_wmod, w_mod, m_w_mod, v_w_mod),
        "w_in": (full["w_in"], w_in, m_w_in, v_w_in),
        "w_pa": (full["w_pa"], w_proj_hgrn, m_w_proj_hgrn, v_w_proj_hgrn),
        "w_pb": (full["w_pb"], w_proj_ret, m_w_proj_ret, v_w_proj_ret),
        "w_out": (full["w_out"], w_out, m_w_out, v_w_out),
        "wg": (full["wg"], w_ffn_gate, m_w_ffn_gate, v_w_ffn_gate),
        "wu": (full["wu"], w_ffn_up, m_w_ffn_up, v_w_ffn_up),
        "wd": (full["wd"], w_ffn_down, m_w_ffn_down, v_w_ffn_down),
    }
    res = {}
    for k, (g, wt, mt, vt) in big.items():
        res[k] = tuple(a[None] for a in adamw(wt[0], g, mt[0], vt[0], "adamw_" + k))

    sm = [_unpack_params(p) for p in (pk_g, pk_d, pk_nm, pk_nv)]
    outs = []
    for t in range(4):
        (s_cctx, s_bmod, s_n1, s_n2, s_lbf, s_lbb, s_hgn, s_thf, s_thb, s_fin) = sm[t]
        outs.append([s_cctx, res["w_mod"][t], s_bmod, s_n1, s_n2, res["w_in"][t], s_lbf, s_lbb, s_hgn, s_thf, s_thb,
                     res["w_pa"][t], res["w_pb"][t], res["w_out"][t], res["wg"][t], res["wu"][t], res["wd"][t], s_fin])
    return (loss, dx[None], *outs[0], *outs[1], *outs[2], *outs[3])
```

```python
import functools

import jax
import jax.numpy as jnp
from jax import lax
from jax.experimental import pallas as pl
from jax.experimental.pallas import tpu as pltpu

F32 = jnp.float32
BF16 = jnp.bfloat16
HI = lax.Precision.HIGHEST
CUMSUM_PRECISION = lax.Precision.HIGH

D = 1024
HEADS = 8
HG_D = 128
RT_DK = 128
RT_DV = 256
D_FF = 2816
D_IN = 13312
N_SHARD = 4
IN_SH = D_IN // N_SHARD
FF_SH = D_FF // N_SHARD
HG_CHUNK = 32
SCAN_ROWS = 256
HG_GROUP = 8
RT_GROUP = 4
PROJ_ROWS = 1024
EPS = 1e-6
GN_EPS = 1e-5
Q_SCALE = 128.0 ** -0.5
VMEM_LIMIT = 56 * 1024 * 1024

COL_HQ, COL_HFF, COL_HFB, COL_HI, COL_HG = 0, 8, 16, 24, 32
COL_RQ, COL_RK, COL_RV, COL_RG, COL_GA, COL_GB = 40, 48, 56, 72, 88, 96

ADAM_LR, ADAM_B1, ADAM_B2, ADAM_EPS, ADAM_WD, ADAM_STEP = 0.001, 0.9, 0.999, 1e-08, 0.01, 10


def _params(*sem):
    return pltpu.CompilerParams(dimension_semantics=sem, vmem_limit_bytes=VMEM_LIMIT)


def _dot(a, b, ca=1, cb=0, prec=None):
    return lax.dot_general(a, b, (((ca,), (cb,)), ((), ())), precision=prec, preferred_element_type=F32)


def _bdot(a, b, ca=1, cb=0):
    return _dot(a.astype(BF16), b.astype(BF16), ca, cb)


def _sigmoid(z):
    return 1.0 / (1.0 + jnp.exp(-z))


def _rowsum(a):
    return jnp.sum(a, axis=0, keepdims=True)


def _lanemean(a):
    return jnp.mean(a, axis=-1, keepdims=True)


def _grid_step(grid):
    pos, total = 0, 1
    for d, size in enumerate(grid):
        pos = pos * size + pl.program_id(d)
        total *= size
    return pos, total


def normmod_matmul(x, nw, sh, sc, w, name):
    L = x.shape[0]
    tm = min(PROJ_ROWS, L)
    tn = IN_SH // 2

    def body(x_ref, nw_ref, sh_ref, sc_ref, w_ref, p_ref, hx_ref, p16_ref, hx_scr):
        @pl.when(pl.program_id(1) == 0)
        def _():
            xv = x_ref[...]
            n = xv * lax.rsqrt(_lanemean(xv * xv) + EPS) * nw_ref[...]
            h = (n * (1.0 + sc_ref[...]) + sh_ref[...]).astype(BF16)
            hx_scr[...] = h
            hx_ref[...] = h

        acc = _dot(hx_scr[...], w_ref[...])
        p_ref[...] = acc
        p16_ref[...] = acc.astype(BF16)

    vec = pl.BlockSpec((1, D), lambda i, j: (0, 0))
    return pl.pallas_call(
        body, name=name,
        grid=(L // tm, D_IN // tn),
        in_specs=[pl.BlockSpec((tm, D), lambda i, j: (i, 0)), vec, vec, vec,
                  pl.BlockSpec((D, tn), lambda i, j: (0, j))],
        out_specs=[pl.BlockSpec((tm, tn), lambda i, j: (i, j)), pl.BlockSpec((tm, D), lambda i, j: (i, 0)),
                   pl.BlockSpec((tm, tn), lambda i, j: (i, j))],
        out_shape=[jax.ShapeDtypeStruct((L, D_IN), F32), jax.ShapeDtypeStruct((L, D), BF16),
                   jax.ShapeDtypeStruct((L, D_IN), BF16)],
        scratch_shapes=[pltpu.VMEM((tm, D), BF16)],
        compiler_params=_params("parallel", "arbitrary"),
    )(x, nw, sh, sc, w)


def _w_halves(w_src, col0, tn, wbuf, wsems, pos):
    @pl.when(pos == 0)
    def _():
        for h in range(2):
            pltpu.make_async_copy(w_src.at[:, pl.ds(pl.multiple_of(col0 + h * tn, 128), tn)], wbuf.at[h],
                                  wsems.at[h]).start()

    for h in range(2):
        @pl.when(pos == h)
        def _(h=h):
            pltpu.make_async_copy(w_src.at[:, pl.ds(0, tn)], wbuf.at[h], wsems.at[h]).wait()


def in_proj_own(x, nw, sh, sc, w_shard, shard_arg, name):
    L = x.shape[0]
    tm = min(PROJ_ROWS, L)
    tn = IN_SH // 2
    grid = (L // tm, 2)

    def body(k_ref, x_ref, nw_ref, sh_ref, sc_ref, w_ref, p_ref, p16_ref, hx_ref, wfull_ref, hx_scr, wbuf, wsems,
             psems, *sems):
        pos, total = _grid_step(grid)
        start, forward, finish = _gather_phases([w_ref], [wfull_ref], *sems, relations=(0, 1))
        pl.when(pos == 0)(start)
        _w_halves(w_ref, 0, tn, wbuf, wsems, pos)

        def place(h):
            col = pl.multiple_of(k_ref[0] * IN_SH + h * tn, 128)
            return pltpu.make_async_copy(wbuf.at[h], wfull_ref.at[:, pl.ds(col, tn)], psems.at[h])

        for h in range(2):
            @pl.when(pos == h)
            def _(h=h):
                place(h).start()

        @pl.when(pl.program_id(1) == 0)
        def _():
            xv = x_ref[...]
            n = xv * lax.rsqrt(_lanemean(xv * xv) + EPS) * nw_ref[...]
            h = (n * (1.0 + sc_ref[...]) + sh_ref[...]).astype(BF16)
            hx_scr[...] = h
            hx_ref[...] = h

        acc = _dot(hx_scr[...], wbuf[pl.program_id(1)])
        p_ref[...] = acc
        p16_ref[...] = acc.astype(BF16)

        @pl.when(pos == total - 1)
        def _():
            forward()
            finish()
            place(0).wait()
            place(1).wait()

    vec = pl.BlockSpec((1, D), lambda i, j, k: (0, 0))
    return pl.pallas_call(
        body, name=name,
        grid_spec=pltpu.PrefetchScalarGridSpec(
            num_scalar_prefetch=1, grid=grid,
            in_specs=[pl.BlockSpec((tm, D), lambda i, j, k: (i, 0)), vec, vec, vec, ANY],
            out_specs=[pl.BlockSpec((tm, tn), lambda i, j, k: (i, 2 * k[0] + j)),
                       pl.BlockSpec((tm, tn), lambda i, j, k: (i, 2 * k[0] + j)),
                       pl.BlockSpec((tm, D), lambda i, j, k: (i, 0)), ANY],
            scratch_shapes=[pltpu.VMEM((tm, D), BF16), pltpu.VMEM((2, D, tn), BF16), pltpu.SemaphoreType.DMA((2,)),
                            pltpu.SemaphoreType.DMA((2,))] + _gather_scratch(1)),
        out_shape=[jax.ShapeDtypeStruct((L, D_IN), F32), jax.ShapeDtypeStruct((L, D_IN), BF16),
                   jax.ShapeDtypeStruct((L, D), BF16),
                   jax.ShapeDtypeStruct((D, D_IN), BF16)],
        compiler_params=_params("arbitrary", "arbitrary"),
    )(shard_arg, x, nw, sh, sc, w_shard)


def in_proj_next(hx, w_full, shard_arg, p, name, diag_from=None, gather=()):
    L = hx.shape[0]
    tm = min(PROJ_ROWS, L)
    tn = IN_SH // 2
    grid = (L // tm, 2)
    diag = diag_from is not None
    ng = len(gather)
    assert not (diag and ng)

    def body(k_ref, hx_ref, wf_in, p_in, p16_in, *refs):
        n_src = 1 if diag else ng
        srcs = refs[:n_src]
        p_ref, p16_ref = refs[n_src], refs[n_src + 1]
        dsts = refs[n_src + 2:2 * n_src + 2]
        wbuf, wsems = refs[2 * n_src + 2:2 * n_src + 4]
        sems = refs[2 * n_src + 4:2 * n_src + 7]
        stage = refs[2 * n_src + 7:]
        pos, total = _grid_step(grid)
        w_src = dsts[0] if diag else wf_in
        if diag:
            start, forward, finish = _gather_phases(srcs, dsts, *sems, relations=(2,))
        elif ng:
            start, forward, finish = _gather_phases(srcs, dsts, *sems, stage=stage)
        if n_src:
            pl.when(pos == 0)(start)
        _w_halves(w_src, k_ref[0] * IN_SH, tn, wbuf, wsems, pos)
        acc = _dot(hx_ref[...], wbuf[pl.program_id(1)])
        p_ref[...] = acc
        p16_ref[...] = acc.astype(BF16)
        if n_src:
            @pl.when(pos == total - 1)
            def _():
                forward()
                finish()

    srcs = [diag_from] if diag else list(gather)
    out_shape = [jax.ShapeDtypeStruct((L, D_IN), F32), jax.ShapeDtypeStruct((L, D_IN), BF16)]
    if diag:
        out_shape.append(jax.ShapeDtypeStruct(w_full.shape, w_full.dtype))
    out_shape += [jax.ShapeDtypeStruct((N_SHARD,) + s.shape, s.dtype) for s in gather]
    aliases = {3: 0, 4: 1, 2: 2} if diag else {3: 0, 4: 1}
    pblock = pl.BlockSpec((tm, tn), lambda i, j, k: (i, 2 * k[0] + j))
    return pl.pallas_call(
        body, name=name,
        grid_spec=pltpu.PrefetchScalarGridSpec(
            num_scalar_prefetch=1, grid=grid,
            in_specs=[pl.BlockSpec((tm, D), lambda i, j, k: (i, 0)), ANY, ANY, ANY] + [ANY] * len(srcs),
            out_specs=[pblock, pblock] + [ANY] * len(srcs),
            scratch_shapes=[pltpu.VMEM((2, D, tn), BF16), pltpu.SemaphoreType.DMA((2,))]
            + (_gather_scratch(len(srcs)) if srcs else []) + [pltpu.VMEM(s.shape, s.dtype) for s in gather]),
        out_shape=out_shape,
        input_output_aliases=aliases,
        compiler_params=_params("arbitrary", "arbitrary"),
    )(shard_arg, hx, w_full, p[0], p[1], *srcs)


def _hgrn_gates(z, lb):
    sg = _sigmoid(z)
    sgn = _sigmoid(-z)
    f = lb + (1.0 - lb) * sg
    k = (1.0 - lb) * sgn
    return sg, sgn, f, k


def _tri_chunks(n, chunk, reverse):
    r = lax.broadcasted_iota(jnp.int32, (n, n), 0)
    c = lax.broadcasted_iota(jnp.int32, (n, n), 1)
    same = (r // chunk) == (c // chunk)
    return jnp.where(same & ((r <= c) if reverse else (r >= c)), 1.0, 0.0).astype(F32)


def _decay3(b, reverse, key_major=False):
    C = b.shape[0]
    i0 = lax.broadcasted_iota(jnp.int32, (C, C, 1), 0)
    i1 = lax.broadcasted_iota(jnp.int32, (C, C, 1), 1)
    t, s = (i1, i0) if key_major else (i0, i1)
    mask = (t <= s) if reverse else (t >= s)
    diff = (b[None, :, :] - b[:, None, :]) if key_major else (b[:, None, :] - b[None, :, :])
    return jnp.exp(jnp.where(mask, diff, -jnp.inf))


HG_SUB = 8


def _hgrn_pairs(reverse):
    pairs = []
    size = HG_SUB
    while size < HG_CHUNK:
        for lo in range(0, HG_CHUNK, 2 * size):
            first, second = slice(lo, lo + size), slice(lo + size, lo + 2 * size)
            if reverse:
                pairs.append((first, second, lo + size))
            else:
                pairs.append((second, first, lo + size - 1))
        size *= 2
    return pairs


def _head_mask(g, nq, nk):
    r = lax.broadcasted_iota(jnp.int32, (g * nq, g * nk), 0) // nq
    c = lax.broadcasted_iota(jnp.int32, (g * nq, g * nk), 1) // nk
    return jnp.where(r == c, 1.0, 0.0).astype(F32)


def _hgrn_masks(g, reverse):
    return [_head_mask(g, qr.stop - qr.start, kr.stop - kr.start) for qr, kr, _ in _hgrn_pairs(reverse)]


def _stack(xs):
    return jnp.concatenate(xs, axis=0)


def _unstack(x, g):
    n = x.shape[0] // g
    return [x[h * n:(h + 1) * n] for h in range(g)]


def _add_blocks(acc, rows, part):
    for i in range(part.shape[0] // HG_SUB):
        acc[rows.start // HG_SUB + i] += part[i * HG_SUB:(i + 1) * HG_SUB]


def _hgrn_intra_fwd(qs, ks, vs, bs, masks, reverse):
    g = len(qs)
    blocks = []
    for q, k, v, b in zip(qs, ks, vs, bs):
        mine = []
        for lo in range(0, HG_CHUNK, HG_SUB):
            r = slice(lo, lo + HG_SUB)
            e3 = _decay3(b[r], reverse, key_major=True)
            att3 = jnp.sum(q[r][None, :, :] * k[r][:, None, :] * e3, axis=-1, keepdims=True)
            mine.append(jnp.sum(att3 * v[r][:, None, :], axis=0))
        blocks.append(mine)
    for (qr, kr, ref), mask in zip(_hgrn_pairs(reverse), masks):
        qt = _stack([q[qr] * jnp.exp(b[qr] - b[ref:ref + 1]) for q, b in zip(qs, bs)])
        kt = _stack([k[kr] * jnp.exp(b[ref:ref + 1] - b[kr]) for k, b in zip(ks, bs)])
        att = _bdot(qt, kt, 1, 1) * mask
        for mine, part in zip(blocks, _unstack(_bdot(att, _stack([v[kr] for v in vs])), g)):
            _add_blocks(mine, qr, part)
    return [jnp.concatenate(mine, axis=0) for mine in blocks]


def _hgrn_intra_bwd(qs, ks, vs, bs, d_os, masks, reverse):
    g = len(qs)
    nb = HG_CHUNK // HG_SUB
    dqs, dks, dvs = [], [], []
    for q, k, v, b, d_o in zip(qs, ks, vs, bs, d_os):
        dq, dk, dv = [None] * nb, [None] * nb, [None] * nb
        for i in range(nb):
            r = slice(i * HG_SUB, (i + 1) * HG_SUB)
            e3 = _decay3(b[r], reverse)
            p3 = jnp.sum(d_o[r][:, None, :] * v[r][None, :, :], axis=-1, keepdims=True) * e3
            dq[i] = jnp.sum(p3 * k[r][None, :, :], axis=1)
            dk[i] = jnp.sum(p3 * q[r][:, None, :], axis=0)
            att3 = jnp.sum(q[r][:, None, :] * k[r][None, :, :] * e3, axis=-1, keepdims=True)
            dv[i] = jnp.sum(att3 * d_o[r][:, None, :], axis=0)
        dqs.append(dq)
        dks.append(dk)
        dvs.append(dv)
    for (qr, kr, ref), mask in zip(_hgrn_pairs(reverse), masks):
        fqs = [jnp.exp(b[qr] - b[ref:ref + 1]) for b in bs]
        fks = [jnp.exp(b[ref:ref + 1] - b[kr]) for b in bs]
        qt = _stack([q[qr] * f for q, f in zip(qs, fqs)])
        kt = _stack([k[kr] * f for k, f in zip(ks, fks)])
        do_q = _stack([d_o[qr] for d_o in d_os])
        att = _bdot(qt, kt, 1, 1) * mask
        datt = _bdot(do_q, _stack([v[kr] for v in vs]), 1, 1) * mask
        for dq, part, f in zip(dqs, _unstack(_bdot(datt, kt), g), fqs):
            _add_blocks(dq, qr, part * f)
        for dk, part, f in zip(dks, _unstack(_bdot(datt, qt, 0, 0), g), fks):
            _add_blocks(dk, kr, part * f)
        for dv, part in zip(dvs, _unstack(_bdot(att, do_q, 0, 0), g)):
            _add_blocks(dv, kr, part)

    def cat(parts):
        return [jnp.concatenate(p, axis=0) for p in parts]

    return cat(dqs), cat(dks), cat(dvs)


def _hgrn_state_step(k, v, b, s_t, last):
    b_last = b[last:last + 1]
    return s_t * jnp.exp(b_last) + _bdot(v, k * jnp.exp(b_last - b), 0, 0)


def hgrn_scan_fwd(p, lb, s0, col_z, reverse, name, prev=None):
    has_prev = prev is not None
    L = p.shape[0]
    nB = L // SCAN_ROWS
    nC = SCAN_ROWS // HG_CHUNK
    C = HG_CHUNK
    G, W = HG_GROUP, HG_GROUP * HG_D
    last = 0 if reverse else C - 1

    def bmap(b):
        return (nB - 1 - b) if reverse else b

    def body(q_ref, z_ref, v_ref, lb_ref, s0_ref, *refs):
        prev_ref = refs[0] if has_prev else None
        o_ref, sfin_ref, sblk_ref, s_scr, k_scr, b_scr = refs[1:] if has_prev else refs
        blk = pl.program_id(1)

        @pl.when(blk == 0)
        def _():
            s_scr[...] = s0_ref[...]

        sblk_ref[...] = s_scr[...]
        _, _, f_all, k_all = _hgrn_gates(z_ref[...], lb_ref[...])
        k_scr[...] = k_all
        b_scr[...] = _dot(_tri_chunks(SCAN_ROWS, C, reverse), jnp.log(f_all), prec=CUMSUM_PRECISION)

        masks = _hgrn_masks(G, reverse)
        heads = [slice(j * HG_D, (j + 1) * HG_D) for j in range(G)]

        def chunk(ci, carry):
            c = (nC - 1 - ci) if reverse else ci
            rows = pl.ds(pl.multiple_of(c * C, C), C)
            qs = [q_ref[rows, lanes] * Q_SCALE for lanes in heads]
            vs = [v_ref[rows, lanes] for lanes in heads]
            ks = [k_scr[rows, lanes] for lanes in heads]
            bs = [b_scr[rows, lanes] for lanes in heads]
            o_in = _hgrn_intra_fwd(qs, ks, vs, bs, masks, reverse)
            for j, lanes in enumerate(heads):
                s_t = s_scr[j]
                o = o_in[j] + _bdot(qs[j] * jnp.exp(bs[j]), s_t, 1, 1)
                o_ref[rows, lanes] = o + prev_ref[rows, lanes] if has_prev else o
                s_scr[j] = _hgrn_state_step(ks[j], vs[j], bs[j], s_t, last)
            return carry

        lax.fori_loop(0, nC, chunk, 0)

        @pl.when(blk == nB - 1)
        def _():
            sfin_ref[...] = s_scr[...]

    def col(c0):
        return pl.BlockSpec((SCAN_ROWS, W), lambda h, b: (bmap(b), c0 // G + h))

    state = pl.BlockSpec((G, HG_D, HG_D), lambda h, b: (h, 0, 0))
    return pl.pallas_call(
        body, name=name,
        grid=(HEADS // G, nB),
        in_specs=[col(COL_HQ), col(col_z), col(COL_HI), pl.BlockSpec((1, W), lambda h, b: (0, h)), state]
        + ([pl.BlockSpec((SCAN_ROWS, W), lambda h, b: (bmap(b), h))] if has_prev else []),
        out_specs=[pl.BlockSpec((SCAN_ROWS, W), lambda h, b: (bmap(b), h)), state,
                   pl.BlockSpec((None, G, HG_D, HG_D), lambda h, b: (bmap(b), h, 0, 0))],
        out_shape=[jax.ShapeDtypeStruct((L, D), F32),
                   jax.ShapeDtypeStruct((HEADS, HG_D, HG_D), F32),
                   jax.ShapeDtypeStruct((nB, HEADS, HG_D, HG_D), F32)],
        scratch_shapes=[pltpu.VMEM((G, HG_D, HG_D), F32), pltpu.VMEM((SCAN_ROWS, W), F32),
                        pltpu.VMEM((SCAN_ROWS, W), F32)],
        compiler_params=_params("parallel", "arbitrary"),
    )(p, p, p, lb, s0, *([prev] if has_prev else []))


def hgrn_scan_bwd(p, lb, s_blocks, d_o, ds_fin, prev, col_z, reverse, name):
    L = p.shape[0]
    nB = L // SCAN_ROWS
    nC = SCAN_ROWS // HG_CHUNK
    C = HG_CHUNK
    G, W = HG_GROUP, HG_GROUP * HG_D
    last = 0 if reverse else C - 1
    has_prev = prev is not None
    out_dt = BF16 if has_prev else F32

    def bmap(b):
        return b if reverse else (nB - 1 - b)

    def body(*refs):
        q_ref, z_ref, v_ref, lb_ref, sblk_ref, do_ref, dsf_ref = refs[:7]
        refs = refs[7:]
        if has_prev:
            pq_ref, pv_ref = refs[:2]
            refs = refs[2:]
        (dq_ref, dz_ref, dv_ref, dlb_ref, ds0_ref, st_scr, run_scr, ds_scr, k_scr, b_scr, db_scr, dk_scr,
         rf_scr, sgn_scr, dzf_scr) = refs
        blk = pl.program_id(1)

        @pl.when(blk == 0)
        def _():
            ds_scr[...] = dsf_ref[...]
            dlb_ref[...] = jnp.zeros_like(dlb_ref)

        tri = _tri_chunks(SCAN_ROWS, C, reverse)
        row = lax.broadcasted_iota(jnp.int32, (C, HG_D), 0)
        lb_all = lb_ref[...]
        sg_all, sgn_all, f_all, k_all = _hgrn_gates(z_ref[...], lb_all)
        k_scr[...] = k_all
        rf_scr[...] = 1.0 / f_all
        sgn_scr[...] = sgn_all
        dzf_scr[...] = (1.0 - lb_all) * sg_all * sgn_all
        b_scr[...] = _dot(tri, jnp.log(f_all), prec=CUMSUM_PRECISION)
        run_scr[...] = sblk_ref[...]

        def recompute(ci, carry):
            c = (nC - 1 - ci) if reverse else ci
            rows = pl.ds(pl.multiple_of(c * C, C), C)
            for j in range(G):
                lanes = slice(j * HG_D, (j + 1) * HG_D)
                s_t = run_scr[j]
                st_scr[c, j] = s_t
                run_scr[j] = _hgrn_state_step(k_scr[rows, lanes], v_ref[rows, lanes], b_scr[rows, lanes], s_t, last)
            return carry

        lax.fori_loop(0, nC, recompute, 0)

        masks = _hgrn_masks(G, reverse)
        heads = [slice(j * HG_D, (j + 1) * HG_D) for j in range(G)]

        def chunk(ci, carry):
            c = ci if reverse else (nC - 1 - ci)
            rows = pl.ds(pl.multiple_of(c * C, C), C)
            ks = [k_scr[rows, lanes] for lanes in heads]
            bs = [b_scr[rows, lanes] for lanes in heads]
            qs = [q_ref[rows, lanes] * Q_SCALE for lanes in heads]
            vs = [v_ref[rows, lanes] for lanes in heads]
            d_os = [do_ref[rows, lanes] for lanes in heads]
            dq_ins, dk_ins, dv_ins = _hgrn_intra_bwd(qs, ks, vs, bs, d_os, masks, reverse)
            for j, lanes in enumerate(heads):
                k, b, q, v, d_o = ks[j], bs[j], qs[j], vs[j], d_os[j]
                s_t = st_scr[c, j]
                ds_t = ds_scr[j]
                eb = jnp.exp(b)
                b_last = b[last:last + 1]
                eb_last = jnp.exp(b_last)
                kdec = jnp.exp(b_last - b)
                qe = q * eb
                ke = k * kdec
                dq_tot = _bdot(d_o, s_t, 1, 0) * eb + dq_ins[j]
                dke = _bdot(v, ds_t, 1, 0)
                dk_tot = dke * kdec + dk_ins[j]
                dv = dv_ins[j] + _bdot(ke, ds_t, 1, 1)
                db_last = _rowsum(dke * ke) + eb_last * _rowsum(ds_t * s_t)
                db_scr[rows, lanes] = q * dq_tot - k * dk_tot + jnp.where(row == last, db_last, 0.0)
                dk_scr[rows, lanes] = dk_tot
                dq = dq_tot * Q_SCALE
                if has_prev:
                    dq = dq + pq_ref[rows, lanes]
                    dv = dv + pv_ref[rows, lanes]
                dq_ref[rows, lanes] = dq.astype(out_dt)
                dv_ref[rows, lanes] = dv.astype(out_dt)
                ds_scr[j] = ds_t * eb_last + _bdot(d_o, qe, 0, 0)
            return carry

        lax.fori_loop(0, nC, chunk, 0)

        g = _dot(tri, db_scr[...], 0, 0, prec=CUMSUM_PRECISION) * rf_scr[...] - dk_scr[...]
        dz_ref[...] = (g * dzf_scr[...]).astype(BF16)
        dlb_ref[...] += _rowsum(g * sgn_scr[...])

        @pl.when(blk == nB - 1)
        def _():
            ds0_ref[...] = ds_scr[...]

    def col(c0):
        return pl.BlockSpec((SCAN_ROWS, W), lambda h, b: (bmap(b), c0 // G + h))

    tile = pl.BlockSpec((SCAN_ROWS, W), lambda h, b: (bmap(b), h))
    state = pl.BlockSpec((G, HG_D, HG_D), lambda h, b: (h, 0, 0))
    in_specs = [col(COL_HQ), col(col_z), col(COL_HI),
                pl.BlockSpec((1, W), lambda h, b: (0, h)),
                pl.BlockSpec((None, G, HG_D, HG_D), lambda h, b: (bmap(b), h, 0, 0)),
                tile, state]
    args = [p, p, p, lb, s_blocks, d_o, ds_fin]
    if has_prev:
        in_specs += [tile, tile]
        args += list(prev)
    return pl.pallas_call(
        body, name=name,
        grid=(HEADS // G, nB),
        in_specs=in_specs,
        out_specs=[tile, tile, tile, pl.BlockSpec((1, W), lambda h, b: (0, h)), state],
        out_shape=[jax.ShapeDtypeStruct((L, D), out_dt), jax.ShapeDtypeStruct((L, D), BF16),
                   jax.ShapeDtypeStruct((L, D), out_dt), jax.ShapeDtypeStruct((1, D), F32),
                   jax.ShapeDtypeStruct((HEADS, HG_D, HG_D), F32)],
        scratch_shapes=[pltpu.VMEM((nC, G, HG_D, HG_D), F32), pltpu.VMEM((G, HG_D, HG_D), F32),
                        pltpu.VMEM((G, HG_D, HG_D), F32)] + [pltpu.VMEM((SCAN_ROWS, W), F32)] * 7,
        compiler_params=_params("parallel", "arbitrary"),
    )(*args)


def _rope(t, cosf, sinf):
    return t * cosf + pltpu.roll(t, RT_DK // 2, 1) * sinf


def _rope_t(d, cosf, sinf):
    return d * cosf + pltpu.roll(d * sinf, RT_DK // 2, 1)


def _ret_decays(lg, reverse):
    C = SCAN_ROWS
    t = lax.broadcasted_iota(jnp.int32, (C, C), 0)
    s = lax.broadcasted_iota(jnp.int32, (C, C), 1)
    delta = ((s - t) if reverse else (t - s)).astype(F32)
    dmat = jnp.where(delta >= 0, jnp.exp(lg * jnp.maximum(delta, 0.0)), 0.0)
    r = lax.broadcasted_iota(jnp.int32, (C, RT_DK), 0)
    pos = ((C - 1 - r) if reverse else r).astype(F32)
    lg1 = lg[:, :RT_DK]
    qdec = jnp.exp(lg1 * (pos + 1.0))
    kdec = jnp.exp(lg1 * (C - 1.0 - pos))
    sdec = jnp.exp(lg1 * float(C))
    return dmat, delta, pos, qdec, kdec, sdec


def ret_scan_fwd(p, cosf, sinf, lg, s0, reverse, name, prev=None):
    has_prev = prev is not None
    L = p.shape[0]
    C = SCAN_ROWS
    nB = L // C

    def bmap(b):
        return (nB - 1 - b) if reverse else b

    G = RT_GROUP

    def body(q_ref, k_ref, v_ref, cos_ref, sin_ref, lg_ref, s0_ref, *refs):
        prev_ref = refs[0] if has_prev else None
        o_ref, sfin_ref, sblk_ref, s_scr = refs[1:] if has_prev else refs
        blk = pl.program_id(1)

        @pl.when(blk == 0)
        def _():
            s_scr[...] = s0_ref[...]

        sblk_ref[...] = s_scr[...]
        cosf, sinf = cos_ref[...], sin_ref[...]
        for j in range(G):
            lk, lv = slice(j * RT_DK, (j + 1) * RT_DK), slice(j * RT_DV, (j + 1) * RT_DV)
            s_t = s_scr[j]
            dmat, _, _, qdec, kdec, sdec = _ret_decays(lg_ref[j], reverse)
            q = _rope(q_ref[:, lk].astype(F32) * Q_SCALE, cosf, sinf)
            k = _rope(k_ref[:, lk].astype(F32), cosf, sinf)
            v = v_ref[:, lv].astype(F32)
            att = _bdot(q, k, 1, 1) * dmat
            o = _bdot(att, v) + _bdot(q * qdec, s_t, 1, 1)
            o_ref[:, lv] = o + prev_ref[:, lv] if has_prev else o
            s_scr[j] = s_t * sdec + _bdot(v, k * kdec, 0, 0)

        @pl.when(blk == nB - 1)
        def _():
            sfin_ref[...] = s_scr[...]

    def col(c0):
        return pl.BlockSpec((C, G * RT_DK), lambda h, b: (bmap(b), c0 // G + h))

    tab = pl.BlockSpec((C, RT_DK), lambda h, b: (bmap(b), 0))
    state = pl.BlockSpec((G, RT_DV, RT_DK), lambda h, b: (h, 0, 0))
    return pl.pallas_call(
        body, name=name,
        grid=(HEADS // G, nB),
        in_specs=[col(COL_RQ), col(COL_RK),
                  pl.BlockSpec((C, G * RT_DV), lambda h, b: (bmap(b), COL_RV // (2 * G) + h)),
                  tab, tab, pl.BlockSpec((G, 1, RT_DV), lambda h, b: (h, 0, 0)), state]
        + ([pl.BlockSpec((C, G * RT_DV), lambda h, b: (bmap(b), h))] if has_prev else []),
        out_specs=[pl.BlockSpec((C, G * RT_DV), lambda h, b: (bmap(b), h)), state,
                   pl.BlockSpec((None, G, RT_DV, RT_DK), lambda h, b: (bmap(b), h, 0, 0))],
        out_shape=[jax.ShapeDtypeStruct((L, HEADS * RT_DV), F32),
                   jax.ShapeDtypeStruct((HEADS, RT_DV, RT_DK), F32),
                   jax.ShapeDtypeStruct((nB, HEADS, RT_DV, RT_DK), F32)],
        scratch_shapes=[pltpu.VMEM((G, RT_DV, RT_DK), F32)],
        compiler_params=_params("parallel", "arbitrary"),
    )(p, p, p, cosf, sinf, lg, s0, *([prev] if has_prev else []))


def ret_scan_bwd(p, cosf, sinf, lg, s_blocks, d_o, ds_fin, prev, reverse, name):
    L = p.shape[0]
    C = SCAN_ROWS
    nB = L // C
    has_prev = prev is not None
    out_dt = BF16
    G = RT_GROUP

    def bmap(b):
        return b if reverse else (nB - 1 - b)

    def body(*refs):
        q_ref, k_ref, v_ref, cos_ref, sin_ref, lg_ref, sblk_ref, do_ref, dsf_ref = refs[:9]
        refs = refs[9:]
        if has_prev:
            pq_ref, pk_ref, pv_ref = refs[:3]
            refs = refs[3:]
        dq_ref, dk_ref, dv_ref, dlg_ref, ds0_ref, ds_scr = refs
        blk = pl.program_id(1)

        @pl.when(blk == 0)
        def _():
            ds_scr[...] = dsf_ref[...]
            dlg_ref[...] = jnp.zeros_like(dlg_ref)

        cosf, sinf = cos_ref[...], sin_ref[...]
        for j in range(G):
            lk, lv = slice(j * RT_DK, (j + 1) * RT_DK), slice(j * RT_DV, (j + 1) * RT_DV)
            s_t = sblk_ref[j]
            ds_t = ds_scr[j]
            dmat, delta, pos, qdec, kdec, sdec = _ret_decays(lg_ref[j], reverse)
            q = _rope(q_ref[:, lk].astype(F32) * Q_SCALE, cosf, sinf)
            k = _rope(k_ref[:, lk].astype(F32), cosf, sinf)
            v = v_ref[:, lv].astype(F32)
            d_o = do_ref[:, lv]
            att_raw = _bdot(q, k, 1, 1)
            datt_m = _bdot(d_o, v, 1, 1) * dmat
            dqd = _bdot(d_o, s_t, 1, 0)
            dkd = _bdot(v, ds_t, 1, 0)
            dq = _bdot(datt_m, k) + dqd * qdec
            dk = _bdot(datt_m, q, 0, 0) + dkd * kdec
            dv = _bdot(att_raw * dmat, d_o, 0, 0) + _bdot(k * kdec, ds_t, 1, 1)
            ds_scr[j] = ds_t * sdec + _bdot(d_o, q * qdec, 0, 0)
            t1 = jnp.sum(_rowsum(datt_m * att_raw * delta), axis=-1, keepdims=True)
            t23 = jnp.sum(_rowsum((pos + 1.0) * qdec * q * dqd + (C - 1.0 - pos) * kdec * k * dkd), axis=-1, keepdims=True)
            t4 = jnp.sum(_rowsum(ds_t * s_t * sdec), axis=-1, keepdims=True) * float(C)
            dlg_ref[j] += jnp.broadcast_to(t1 + t23 + t4, (1, RT_DK))
            if has_prev:
                dq = _rope_t(dq + pq_ref[:, lk].astype(F32), cosf, sinf) * Q_SCALE
                dk = _rope_t(dk + pk_ref[:, lk].astype(F32), cosf, sinf)
                dv = dv + pv_ref[:, lv].astype(F32)
            dq_ref[:, lk] = dq.astype(out_dt)
            dk_ref[:, lk] = dk.astype(out_dt)
            dv_ref[:, lv] = dv.astype(out_dt)

        @pl.when(blk == nB - 1)
        def _():
            ds0_ref[...] = ds_scr[...]

    def col(c0):
        return pl.BlockSpec((C, G * RT_DK), lambda h, b: (bmap(b), c0 // G + h))

    tab = pl.BlockSpec((C, RT_DK), lambda h, b: (bmap(b), 0))
    state = pl.BlockSpec((G, RT_DV, RT_DK), lambda h, b: (h, 0, 0))
    tk = pl.BlockSpec((C, G * RT_DK), lambda h, b: (bmap(b), h))
    tv = pl.BlockSpec((C, G * RT_DV), lambda h, b: (bmap(b), h))
    in_specs = [col(COL_RQ), col(COL_RK),
                pl.BlockSpec((C, G * RT_DV), lambda h, b: (bmap(b), COL_RV // (2 * G) + h)),
                tab, tab, pl.BlockSpec((G, 1, RT_DV), lambda h, b: (h, 0, 0)),
                pl.BlockSpec((None, G, RT_DV, RT_DK), lambda h, b: (bmap(b), h, 0, 0)),
                tv, state]
    args = [p, p, p, cosf, sinf, lg, s_blocks, d_o, ds_fin]
    if has_prev:
        in_specs += [tk, tk, tv]
        args += list(prev)
    return pl.pallas_call(
        body, name=name,
        grid=(HEADS // G, nB),
        in_specs=in_specs,
        out_specs=[tk, tk, tv, pl.BlockSpec((G, 1, RT_DK), lambda h, b: (h, 0, 0)), state],
        out_shape=[jax.ShapeDtypeStruct((L, D), out_dt), jax.ShapeDtypeStruct((L, D), out_dt),
                   jax.ShapeDtypeStruct((L, HEADS * RT_DV), out_dt),
                   jax.ShapeDtypeStruct((HEADS, 1, RT_DK), F32),
                   jax.ShapeDtypeStruct((HEADS, RT_DV, RT_DK), F32)],
        scratch_shapes=[pltpu.VMEM((G, RT_DV, RT_DK), F32)],
        compiler_params=_params("parallel", "arbitrary"),
    )(*args)


def _silu_parts(h):
    s = _sigmoid(h)
    return h * s, s * (1.0 + h * (1.0 - s))


def _head_rms(o):
    outs, rs = [], []
    for h in range(HEADS):
        oh = o[:, h * HG_D:(h + 1) * HG_D]
        r = lax.rsqrt(_lanemean(oh * oh) + EPS)
        outs.append(oh * r)
        rs.append(r)
    return outs, rs


def _group_norm(o):
    outs, rs = [], []
    for h in range(HEADS):
        oh = o[:, h * RT_DV:(h + 1) * RT_DV]
        c = oh - _lanemean(oh)
        r = lax.rsqrt(_lanemean(c * c) + GN_EPS)
        outs.append(c * r)
        rs.append(r)
    return outs, rs


MIX_ROWS = 256


def _mix_specs(rows):
    def t(w, c=0):
        return pl.BlockSpec((rows, w), lambda i: (i, c))

    return t


def mix_fwd(o_hg, o_rt, p, x, g1, hgw, w_pa, w_pb, w_out, name):
    L = x.shape[0]
    t = _mix_specs(MIX_ROWS)

    def body(ohg_ref, ort_ref, hg_ref, rg0_ref, rg1_ref, ga_ref, gb_ref, x_ref, g1_ref, hgw_ref,
             wpa_ref, wpb_ref, wout_ref, x1_ref, xmix_ref, merged_ref, ya_ref, yb_ref):
        nh, _ = _head_rms(ohg_ref[...])
        ya = jnp.concatenate(nh, axis=1) * hgw_ref[...] * _silu_parts(hg_ref[...].astype(F32))[0]
        gn, _ = _group_norm(ort_ref[...])
        rg = jnp.concatenate([rg0_ref[...], rg1_ref[...]], axis=1).astype(F32)
        yb = jnp.concatenate(gn, axis=1) * _silu_parts(rg)[0]
        ya16, yb16 = ya.astype(BF16), yb.astype(BF16)
        merged = (_sigmoid(ga_ref[...].astype(F32)) * _dot(ya16, wpa_ref[...])
                  + _sigmoid(gb_ref[...].astype(F32)) * _dot(yb16, wpb_ref[...])).astype(BF16)
        x_mix = _dot(merged, wout_ref[...])
        x1_ref[...] = x_ref[...] + g1_ref[...] * x_mix
        xmix_ref[...] = x_mix
        merged_ref[...] = merged
        ya_ref[...] = ya16
        yb_ref[...] = yb16

    vec = pl.BlockSpec((1, D), lambda i: (0, 0))

    def full(a):
        return pl.BlockSpec(a.shape, lambda i: (0, 0), pipeline_mode=pl.Buffered(1))

    return pl.pallas_call(
        body, name=name,
        grid=(L // MIX_ROWS,),
        in_specs=[t(D), t(2 * D), t(D, COL_HG // 8), t(D, COL_RG // 8), t(D, COL_RG // 8 + 1),
                  t(D, COL_GA // 8), t(D, COL_GB // 8), t(D), vec, vec, full(w_pa), full(w_pb), full(w_out)],
        out_specs=[t(D), t(D), t(D), t(D), t(2 * D)],
        out_shape=[jax.ShapeDtypeStruct((L, D), F32), jax.ShapeDtypeStruct((L, D), F32),
                   jax.ShapeDtypeStruct((L, D), BF16), jax.ShapeDtypeStruct((L, D), BF16),
                   jax.ShapeDtypeStruct((L, 2 * D), BF16)],
        compiler_params=_params("parallel"),
    )(o_hg, o_rt, p, p, p, p, p, x, g1, hgw, w_pa, w_pb, w_out)


def mix_bwd(dx1, x_mix, ya, yb, o_hg, o_rt, p, g1, hgw, w_pa, w_pb, w_out, name, to_sibling=()):
    L = dx1.shape[0]
    t = _mix_specs(MIX_ROWS)
    nx = len(to_sibling)
    steps = L // MIX_ROWS

    def body(dx1_ref, xmix_ref, ya_ref, yb_ref, ohg_ref, ort_ref, hg_ref, rg0_ref, rg1_ref,
             ga_ref, gb_ref, g1_ref, hgw_ref, wpa_ref, wpb_ref, wout_ref, *refs):
        (dxm_ref, da_ref, db_ref, dga_ref, dgb_ref, dhg_ref, drg_ref, dohg_ref, dort_ref,
         sums_ref) = refs[nx:nx + 10]
        if nx:
            start, finish = _to_sibling_phases(refs[:nx], refs[nx + 10:2 * nx + 10], *refs[2 * nx + 10:])
            pl.when(pl.program_id(0) == 0)(start)
            pl.when(pl.program_id(0) == steps - 1)(finish)

        @pl.when(pl.program_id(0) == 0)
        def _():
            sums_ref[...] = jnp.zeros_like(sums_ref)

        dx1 = dx1_ref[...]
        dxm = (g1_ref[...] * dx1).astype(BF16)
        dxm_ref[...] = dxm
        dmerged = _dot(dxm, wout_ref[...], 1, 1)
        a = _dot(ya_ref[...], wpa_ref[...])
        bm = _dot(yb_ref[...], wpb_ref[...])
        sa, sb = _sigmoid(ga_ref[...].astype(F32)), _sigmoid(gb_ref[...].astype(F32))
        d_a = (dmerged * sa).astype(BF16)
        d_b = (dmerged * sb).astype(BF16)
        da_ref[...] = d_a
        db_ref[...] = d_b
        dga_ref[...] = (dmerged * a * sa * (1.0 - sa)).astype(BF16)
        dgb_ref[...] = (dmerged * bm * sb * (1.0 - sb)).astype(BF16)
        dya = _dot(d_a, wpa_ref[...], 1, 1)
        dyb = _dot(d_b, wpb_ref[...], 1, 1)

        hgw = hgw_ref[...]
        silu_h, dsilu_h = _silu_parts(hg_ref[...].astype(F32))
        nh, rh = _head_rms(ohg_ref[...])
        n = jnp.concatenate(nh, axis=1)
        dhg_ref[...] = (dya * n * hgw * dsilu_h).astype(BF16)
        dn = dya * hgw * silu_h
        douts = []
        for h in range(HEADS):
            dnh = dn[:, h * HG_D:(h + 1) * HG_D]
            douts.append(rh[h] * (dnh - nh[h] * _lanemean(dnh * nh[h])))
        dohg_ref[...] = jnp.concatenate(douts, axis=1)

        rg = jnp.concatenate([rg0_ref[...], rg1_ref[...]], axis=1).astype(F32)
        silu_r, dsilu_r = _silu_parts(rg)
        gn, rr = _group_norm(ort_ref[...])
        g = jnp.concatenate(gn, axis=1)
        drg_ref[...] = (dyb * g * dsilu_r).astype(BF16)
        dgn = dyb * silu_r
        douts = []
        for h in range(HEADS):
            dgh = dgn[:, h * RT_DV:(h + 1) * RT_DV]
            douts.append(rr[h] * (dgh - _lanemean(dgh) - gn[h] * _lanemean(dgh * gn[h])))
        dort_ref[...] = jnp.concatenate(douts, axis=1)

        sums_ref[0:1, :] += _rowsum(dx1 * xmix_ref[...])
        sums_ref[1:2, :] += _rowsum(dya * n * silu_h)

    vec = pl.BlockSpec((1, D), lambda i: (0, 0))

    def full(a):
        return pl.BlockSpec(a.shape, lambda i: (0, 0), pipeline_mode=pl.Buffered(1))

    bf = functools.partial(jax.ShapeDtypeStruct, dtype=BF16)
    return pl.pallas_call(
        body, name=name,
        grid=(L // MIX_ROWS,),
        in_specs=[t(D), t(D), t(D), t(2 * D), t(D), t(2 * D),
                  t(D, COL_HG // 8), t(D, COL_RG // 8), t(D, COL_RG // 8 + 1), t(D, COL_GA // 8), t(D, COL_GB // 8),
                  vec, vec, full(w_pa), full(w_pb), full(w_out)] + [ANY] * nx,
        out_specs=[t(D), t(D), t(D), t(D), t(D), t(D), t(2 * D), t(D), t(2 * D),
                   pl.BlockSpec((8, D), lambda i: (0, 0))] + [ANY] * nx,
        out_shape=[bf((L, D)), bf((L, D)), bf((L, D)), bf((L, D)), bf((L, D)), bf((L, D)), bf((L, 2 * D)),
                   jax.ShapeDtypeStruct((L, D), F32), jax.ShapeDtypeStruct((L, 2 * D), F32),
                   jax.ShapeDtypeStruct((8, D), F32)] + [jax.ShapeDtypeStruct(a.shape, a.dtype) for a in to_sibling],
        scratch_shapes=_to_sibling_scratch(nx) if nx else [],
        compiler_params=_params("arbitrary"),
    )(dx1, x_mix, ya, yb, o_hg, o_rt, p, p, p, p, p, g1, hgw, w_pa, w_pb, w_out, *to_sibling)


FFN_ROWS = 512


def ffn_fwd(x1, target, nw2, sh2, sc2, g2, fw, wg, wu, wd, name):
    L = x1.shape[0]
    tm = min(FFN_ROWS, L)

    def body(x1_ref, tgt_ref, nw2_ref, sh2_ref, sc2_ref, g2_ref, fw_ref, wg_ref, wu_ref, wd_ref,
             hx2_ref, g_ref, u_ref, h_ref, f_ref, dx2_ref, sums_ref, hx_scr, acc):
        i, j = pl.program_id(0), pl.program_id(1)

        @pl.when((i == 0) & (j == 0))
        def _():
            sums_ref[...] = jnp.zeros_like(sums_ref)

        @pl.when(j == 0)
        def _():
            xv = x1_ref[...]
            n = xv * lax.rsqrt(_lanemean(xv * xv) + EPS) * nw2_ref[...]
            h = (n * (1.0 + sc2_ref[...]) + sh2_ref[...]).astype(BF16)
            hx_scr[...] = h
            hx2_ref[...] = h
            acc[...] = jnp.zeros_like(acc)

        hx = hx_scr[...]
        g = _dot(hx, wg_ref[...])
        u = _dot(hx, wu_ref[...])
        hh = (_silu_parts(g)[0] * u).astype(BF16)
        g_ref[...] = g
        u_ref[...] = u
        h_ref[...] = hh
        acc[...] += _dot(hh, wd_ref[...])

        @pl.when(j == N_SHARD - 1)
        def _():
            f = acc[...]
            f_ref[...] = f
            x2 = x1_ref[...] + g2_ref[...] * f
            r = lax.rsqrt(_lanemean(x2 * x2) + EPS)
            fw = fw_ref[...]
            e = x2 * r * fw - tgt_ref[...]
            dy = e * (1.0 / D)
            dyw = dy * fw
            dx2_ref[...] = r * dyw - x2 * (r * r * r) * _lanemean(dyw * x2)
            sums_ref[0:1, :] += _rowsum(dy * x2 * r)
            sums_ref[1:2, :] += _rowsum(e * e) * (0.5 / D)

    row = pl.BlockSpec((tm, D), lambda i, j: (i, 0))
    vec = pl.BlockSpec((1, D), lambda i, j: (0, 0))
    sh = pl.BlockSpec((None, tm, FF_SH), lambda i, j: (j, i, 0))
    return pl.pallas_call(
        body, name=name,
        grid=(L // tm, N_SHARD),
        in_specs=[row, row, vec, vec, vec, vec, vec,
                  pl.BlockSpec((None, D, FF_SH), lambda i, j: (j, 0, 0)),
                  pl.BlockSpec((None, D, FF_SH), lambda i, j: (j, 0, 0)),
                  pl.BlockSpec((None, FF_SH, D), lambda i, j: (j, 0, 0))],
        out_specs=[row, sh, sh, sh, row, row, pl.BlockSpec((8, D), lambda i, j: (0, 0))],
        out_shape=[jax.ShapeDtypeStruct((L, D), BF16),
                   jax.ShapeDtypeStruct((N_SHARD, L, FF_SH), F32), jax.ShapeDtypeStruct((N_SHARD, L, FF_SH), F32),
                   jax.ShapeDtypeStruct((N_SHARD, L, FF_SH), BF16),
                   jax.ShapeDtypeStruct((L, D), F32), jax.ShapeDtypeStruct((L, D), F32),
                   jax.ShapeDtypeStruct((8, D), F32)],
        scratch_shapes=[pltpu.VMEM((tm, D), BF16), pltpu.VMEM((tm, D), F32)],
        compiler_params=_params("arbitrary", "arbitrary"),
    )(x1, target, nw2, sh2, sc2, g2, fw, wg, wu, wd)


def ffn_bwd(dx2, x1, f, g, u, nw2, sc2, g2, wg, wu, wd, name):
    L = x1.shape[0]
    tm = min(FFN_ROWS, L)

    def body(dx2_ref, x1_ref, f_ref, g_ref, u_ref, nw2_ref, sc2_ref, g2_ref, wg_ref, wu_ref, wd_ref,
             df_ref, dg_ref, du_ref, dx1_ref, sums_ref, df_scr, acc):
        i, j = pl.program_id(0), pl.program_id(1)

        @pl.when((i == 0) & (j == 0))
        def _():
            sums_ref[...] = jnp.zeros_like(sums_ref)

        @pl.when(j == 0)
        def _():
            dx2 = dx2_ref[...]
            df = (g2_ref[...] * dx2).astype(BF16)
            df_scr[...] = df
            df_ref[...] = df
            sums_ref[0:1, :] += _rowsum(dx2 * f_ref[...])
            acc[...] = jnp.zeros_like(acc)

        dh = _dot(df_scr[...], wd_ref[...], 1, 1)
        gv, uv = g_ref[...], u_ref[...]
        silu_g, dsilu_g = _silu_parts(gv)
        dg = (dh * uv * dsilu_g).astype(BF16)
        du = (dh * silu_g).astype(BF16)
        dg_ref[...] = dg
        du_ref[...] = du
        acc[...] += _dot(dg, wg_ref[...], 1, 1) + _dot(du, wu_ref[...], 1, 1)

        @pl.when(j == N_SHARD - 1)
        def _():
            dhx = acc[...]
            xv = x1_ref[...]
            r = lax.rsqrt(_lanemean(xv * xv) + EPS)
            n0 = xv * r
            nw = nw2_ref[...]
            dn2 = dhx * (1.0 + sc2_ref[...])
            dn0 = dn2 * nw
            dx1_ref[...] = dx2_ref[...] + r * (dn0 - n0 * _lanemean(dn0 * n0))
            sums_ref[1:2, :] += _rowsum(dhx)
            sums_ref[2:3, :] += _rowsum(dhx * n0 * nw)
            sums_ref[3:4, :] += _rowsum(dn2 * n0)

    row = pl.BlockSpec((tm, D), lambda i, j: (i, 0))
    vec = pl.BlockSpec((1, D), lambda i, j: (0, 0))
    sh = pl.BlockSpec((None, tm, FF_SH), lambda i, j: (j, i, 0))
    return pl.pallas_call(
        body, name=name,
        grid=(L // tm, N_SHARD),
        in_specs=[row, row, row, sh, sh, vec, vec, vec,
                  pl.BlockSpec((None, D, FF_SH), lambda i, j: (j, 0, 0)),
                  pl.BlockSpec((None, D, FF_SH), lambda i, j: (j, 0, 0)),
                  pl.BlockSpec((None, FF_SH, D), lambda i, j: (j, 0, 0))],
        out_specs=[row, sh, sh, row, pl.BlockSpec((8, D), lambda i, j: (0, 0))],
        out_shape=[jax.ShapeDtypeStruct((L, D), BF16),
                   jax.ShapeDtypeStruct((N_SHARD, L, FF_SH), BF16), jax.ShapeDtypeStruct((N_SHARD, L, FF_SH), BF16),
                   jax.ShapeDtypeStruct((L, D), F32), jax.ShapeDtypeStruct((8, D), F32)],
        scratch_shapes=[pltpu.VMEM((tm, D), BF16), pltpu.VMEM((tm, D), F32)],
        compiler_params=_params("arbitrary", "arbitrary"),
    )(dx2, x1, f, g, u, nw2, sc2, g2, wg, wu, wd)


def matmul_tn(a, b, name):
    na, K, M = a.shape
    nb, _, N = b.shape
    n = max(na, nb)
    tk = min(1024, K)
    tn = N if N <= 1024 else N // 2

    def body(a_ref, b_ref, o_ref):
        @pl.when(pl.program_id(2) == 0)
        def _():
            o_ref[...] = jnp.zeros_like(o_ref)

        o_ref[...] += _dot(a_ref[...], b_ref[...], 0, 0)

    return pl.pallas_call(
        body, name=name,
        grid=(n, N // tn, K // tk),
        in_specs=[pl.BlockSpec((None, tk, M), lambda s, j, kk: (s if na > 1 else 0, kk, 0)),
                  pl.BlockSpec((None, tk, tn), lambda s, j, kk: (s if nb > 1 else 0, kk, j))],
        out_specs=pl.BlockSpec((None, M, tn), lambda s, j, kk: (s, 0, j)),
        out_shape=jax.ShapeDtypeStruct((n, M, N), F32),
        compiler_params=_params("parallel", "parallel", "arbitrary"),
    )(a, b)


def matmul_tn_pair(a, b1, b2, name):
    K, M = a.shape
    n, _, N = b1.shape
    tk = min(1024, K)

    def body(a_ref, b1_ref, b2_ref, o1_ref, o2_ref):
        @pl.when(pl.program_id(1) == 0)
        def _():
            o1_ref[...] = jnp.zeros_like(o1_ref)
            o2_ref[...] = jnp.zeros_like(o2_ref)

        at = a_ref[...].T
        o1_ref[...] += _dot(at, b1_ref[...])
        o2_ref[...] += _dot(at, b2_ref[...])

    b_spec = pl.BlockSpec((None, tk, N), lambda s, kk: (s, kk, 0))
    o_spec = pl.BlockSpec((None, M, N), lambda s, kk: (s, 0, 0))
    return pl.pallas_call(
        body, name=name,
        grid=(n, K // tk),
        in_specs=[pl.BlockSpec((tk, M), lambda s, kk: (kk, 0)), b_spec, b_spec],
        out_specs=[o_spec, o_spec],
        out_shape=[jax.ShapeDtypeStruct((n, M, N), F32)] * 2,
        compiler_params=_params("parallel", "arbitrary"),
    )(a, b1, b2)


PIECE_COLS = 1024
N_PIECE_BLOCKS = D_IN // PIECE_COLS


def _piece_blocks(pieces):
    out, col = [], 0
    for arr, width in pieces:
        if arr is not None:
            out.append((arr, col // PIECE_COLS, width // PIECE_COLS))
        col += width
    assert col == D_IN
    return out


def _piece_feed(p_refs, blocks, buf, sems, tile_of, pos, total):
    def present(blk):
        ok = None
        for _, b0, nb in blocks:
            mine = (blk >= b0) & (blk < b0 + nb)
            ok = mine if ok is None else ok | mine
        return ok

    def fetch(step):
        blk, rows = tile_of(step)
        for p_ref, (_, b0, nb) in zip(p_refs, blocks):
            for t in range(nb):
                @pl.when(blk == b0 + t)
                def _(p_ref=p_ref, t=t):
                    pltpu.make_async_copy(p_ref.at[rows, pl.ds(t * PIECE_COLS, PIECE_COLS)], buf.at[step % 2],
                                          sems.at[step % 2]).start()

    @pl.when(pos == 0)
    def _():
        fetch(pos)

    @pl.when(pos + 1 < total)
    def _():
        fetch(pos + 1)

    def landed():
        slot = pos % 2
        pltpu.make_async_copy(p_refs[0].at[pl.ds(0, buf.shape[1]), pl.ds(0, PIECE_COLS)], buf.at[slot],
                              sems.at[slot]).wait()
        return buf.at[slot]

    return present(tile_of(pos)[0]), landed


def matmul_tn_pieces(a, pieces, name, extra=None, to_chips=()):
    K, M = a.shape
    blocks = _piece_blocks(pieces)
    tk = min(1024, K)
    nk = K // tk
    grid = (N_PIECE_BLOCKS, nk)
    nx, npc = len(to_chips), len(blocks)
    a2, blocks2 = (extra[0], _piece_blocks(extra[1])) if extra is not None else (None, [])
    npc2 = len(blocks2)

    def body(a_ref, *refs):
        p_refs = refs[:npc]
        refs = refs[npc:]
        a2_ref, p2_refs = (refs[0], refs[1:1 + npc2]) if npc2 else (None, ())
        refs = refs[1 + npc2:] if npc2 else refs
        o_ref = refs[nx]
        buf, sems = refs[2 * nx + 1:2 * nx + 3]
        rest = refs[2 * nx + 3:]
        pos, total = _grid_step(grid)
        if nx:
            start, finish = _to_chips_phases(refs[:nx], refs[nx + 1:2 * nx + 1], *rest[:2])
            pl.when(pos == 0)(start)
        here, landed = _piece_feed(p_refs, blocks, buf, sems,
                                   lambda s: (s // nk, pl.ds(pl.multiple_of((s % nk) * tk, tk), tk)), pos, total)
        blk, kk = pl.program_id(0), pl.program_id(1)

        @pl.when(kk == 0)
        def _():
            o_ref[...] = jnp.zeros_like(o_ref)

        @pl.when(here)
        def _():
            o_ref[...] += _dot(a_ref[...], landed()[...], 0, 0)

        if npc2:
            buf2, sem2 = rest[-2:]

            def tile(p_ref, t):
                return pltpu.make_async_copy(p_ref.at[:, pl.ds(t * PIECE_COLS, PIECE_COLS)], buf2, sem2.at[0])

            for p_ref, (_, b0, nb) in zip(p2_refs, blocks2):
                for t in range(nb):
                    @pl.when((blk == b0 + t) & (kk == 0))
                    def _(p_ref=p_ref, t=t):
                        tile(p_ref, t).start()

                    @pl.when((blk == b0 + t) & (kk == nk - 1))
                    def _(p_ref=p_ref, t=t):
                        tile(p_ref, t).wait()
                        o_ref[...] += _dot(a2_ref[...], buf2[...], 0, 0)

        if nx:
            pl.when(pos == total - 1)(finish)

    out_spec = pl.BlockSpec((M, PIECE_COLS), lambda blk, kk: (0, blk))
    in_specs = [pl.BlockSpec((tk, M), lambda blk, kk: (kk, 0))] + [ANY] * npc
    args = [a] + [arr for arr, _, _ in blocks]
    scratch = [pltpu.VMEM((2, tk, PIECE_COLS), BF16), pltpu.SemaphoreType.DMA((2,))]
    scratch += _to_chips_scratch(nx) if nx else []
    if npc2:
        in_specs += [pl.BlockSpec(a2.shape, lambda blk, kk: (0, 0))] + [ANY] * npc2
        args += [a2] + [arr for arr, _, _ in blocks2]
        scratch += [pltpu.VMEM((a2.shape[0], PIECE_COLS), BF16), pltpu.SemaphoreType.DMA((1,))]
    out = pl.pallas_call(
        body, name=name,
        grid=grid,
        in_specs=in_specs + [ANY] * nx,
        out_specs=[out_spec] + [ANY] * nx,
        out_shape=[jax.ShapeDtypeStruct((M, D_IN), F32)] + _to_chips_shapes(to_chips),
        scratch_shapes=scratch,
        compiler_params=_params("arbitrary", "arbitrary"),
    )(*args, *to_chips)
    return out if nx else out[0]


def dhx_normbwd(pieces, w, x, dx_res, nw, sc, name, to_chips=()):
    L = x.shape[0]
    tm = min(PROJ_ROWS, L)
    blocks = _piece_blocks(pieces)
    grid = (L // tm, N_PIECE_BLOCKS)
    nx, npc = len(to_chips), len(blocks)

    def body(*refs):
        p_refs = refs[:npc]
        w_ref, x_ref, res_ref, nw_ref, sc_ref = refs[npc:npc + 5]
        refs = refs[npc + 5:]
        dx_ref, sums_ref = refs[nx:nx + 2]
        acc, buf, sems = refs[2 * nx + 2:2 * nx + 5]
        pos, total = _grid_step(grid)
        if nx:
            start, finish = _to_chips_phases(refs[:nx], refs[nx + 2:2 * nx + 2], *refs[2 * nx + 5:])
            pl.when(pos == 0)(start)
            pl.when(pos == total - 1)(finish)
        here, landed = _piece_feed(
            p_refs, blocks, buf, sems,
            lambda s: (s % N_PIECE_BLOCKS, pl.ds(pl.multiple_of((s // N_PIECE_BLOCKS) * tm, tm), tm)), pos, total)
        i, blk = pl.program_id(0), pl.program_id(1)

        @pl.when((i == 0) & (blk == 0))
        def _():
            sums_ref[...] = jnp.zeros_like(sums_ref)

        @pl.when(blk == 0)
        def _():
            acc[...] = jnp.zeros_like(acc)

        @pl.when(here)
        def _():
            acc[...] += _dot(landed()[...], w_ref[...], 1, 1)

        @pl.when(blk == N_PIECE_BLOCKS - 1)
        def _():
            dhx = acc[...]
            xv = x_ref[...]
            r = lax.rsqrt(_lanemean(xv * xv) + EPS)
            n0 = xv * r
            nw = nw_ref[...]
            dn = dhx * (1.0 + sc_ref[...])
            dn0 = dn * nw
            dx_ref[...] = res_ref[...] + r * (dn0 - n0 * _lanemean(dn0 * n0))
            sums_ref[0:1, :] += _rowsum(dhx)
            sums_ref[1:2, :] += _rowsum(dhx * n0 * nw)
            sums_ref[2:3, :] += _rowsum(dn * n0)

    row = pl.BlockSpec((tm, D), lambda i, blk: (i, 0))
    vec = pl.BlockSpec((1, D), lambda i, blk: (0, 0))
    return pl.pallas_call(
        body, name=name,
        grid=grid,
        in_specs=[ANY] * npc + [pl.BlockSpec((D, PIECE_COLS), lambda i, blk: (0, blk)), row, row, vec, vec] + [ANY] * nx,
        out_specs=[row, pl.BlockSpec((8, D), lambda i, blk: (0, 0))] + [ANY] * nx,
        out_shape=[jax.ShapeDtypeStruct((L, D), F32), jax.ShapeDtypeStruct((8, D), F32)] + _to_chips_shapes(to_chips),
        scratch_shapes=[pltpu.VMEM((tm, D), F32), pltpu.VMEM((2, tm, PIECE_COLS), BF16), pltpu.SemaphoreType.DMA((2,))]
        + (_to_chips_scratch(nx) if nx else []),
        compiler_params=_params("arbitrary", "arbitrary"),
    )(*[arr for arr, _, _ in blocks], w, x, dx_res, nw, sc, *to_chips)


SMALL_ROWS = 24


def _rope_tables(L):
    rows = L // 64
    freqs = 10000.0 ** (-jnp.arange(RT_DK // 4, dtype=F32) / (RT_DK // 4))
    a_row = jnp.arange(rows, dtype=F32)[:, None] * freqs
    a_col = jnp.arange(64, dtype=F32)[:, None] * freqs

    def spread(f):
        return jnp.concatenate([jnp.repeat(f(a_row), 64, axis=0), jnp.tile(f(a_col), (rows, 1))], axis=-1)

    cos, sin = spread(jnp.cos), spread(jnp.sin)
    return jnp.concatenate([cos, cos], axis=1), jnp.concatenate([-sin, sin], axis=1)


def _pieces(hq, hf_f, hf_b, hi, hg, rq, rk, rv, rg, ga, gb):
    widths = (D, D, D, D, D, D, D, 2 * D, 2 * D, D, D)
    return list(zip((hq, hf_f, hf_b, hi, hg, rq, rk, rv, rg, ga, gb), widths))


def _lane0(a):
    return a[:, 0, 0]


def _pack_small(rows):
    out = [r.reshape(1, D) for r in rows]
    out += [jnp.zeros((1, D), F32)] * (SMALL_ROWS - len(out))
    return jnp.concatenate(out, axis=0)


def _other_half(g, core):
    axis = g.ndim - 2
    h = g.shape[axis] // 2
    return lax.dynamic_slice_in_dim(g, (1 - core) * h, h, axis=axis).astype(BF16)


def _sibling_sums(gs, names, place):
    core, core_arg, _ = place
    received = rs_to_sibling([_other_half(g, core) for g in gs], "rs_to_sibling_" + names[0])
    return [rs_add_sibling(g, r, core_arg, "rs_add_sibling_" + k) for g, r, k in zip(gs, received, names)]


def _staged_in_proj(x, nw, sh, sc, w_shard, rest, chip):
    cx, cy = chip // 2, chip % 2

    def arg(k):
        return jnp.reshape(k, (1,)).astype(jnp.int32)

    p, p16, hx, w_full = in_proj_own(x, nw, sh, sc, w_shard, arg(chip), "in_proj_own")
    p, p16, w_full = in_proj_next(hx, w_full, arg(2 * (1 - cx) + cy), (p, p16), "in_proj_x", diag_from=w_shard)
    w_pa, w_pb, w_out, w_wd = rest[0], rest[1], rest[2], rest[5]
    p, p16, g_pa, g_pb, g_out, g_wd = in_proj_next(hx, w_full, arg(2 * cx + 1 - cy), (p, p16), "in_proj_y",
                                                   gather=[w_pa, w_pb, w_out, w_wd])
    p, p16, g_wg, g_wu = in_proj_next(hx, w_full, arg(3 - chip), (p, p16), "in_proj_diag", gather=[rest[3], rest[4]])
    w = {"w_in": w_full, "w_pa": g_pa.reshape(D, D), "w_pb": g_pb.reshape(2 * D, D), "w_out": g_out.reshape(D, D),
         "wg": g_wg, "wu": g_wu, "wd": g_wd}
    return p, p16, hx, w


def local_step(x, ctx, target, mod_x, mod_c, lb_f, lb_b, lg_f, lg_b, nw1, nw2, hgw, fw, w, rest=None, place=None):
    L, Lc = x.shape[0], ctx.shape[0]
    sh1, sc1, g1, sh2, sc2, g2 = (mod_x[i:i + 1] for i in range(6))
    sh1c, sc1c = mod_c[0:1], mod_c[1:2]
    cosf, sinf = _rope_tables(L)
    cosc, sinc = jnp.ones((Lc, RT_DK), F32), jnp.zeros((Lc, RT_DK), F32)
    zero_h = jnp.zeros((HEADS, HG_D, HG_D), F32)
    zero_r = jnp.zeros((HEADS, RT_DV, RT_DK), F32)

    if rest is None:
        p, hx, p16 = normmod_matmul(x, nw1, sh1, sc1, w["w_in"], "in_proj")
    else:
        p, p16, hx, w = _staged_in_proj(x, nw1, sh1, sc1, w["w_in_shard"], rest, place[2][0])
    pc, hxc, pc16 = normmod_matmul(ctx, nw1, sh1c, sc1c, w["w_in"], "ctx_in_proj")
    _, s_hf, cb_hf = hgrn_scan_fwd(pc, lb_f, zero_h, COL_HFF, False, "ctx_hgrn_f")
    _, s_hb, cb_hb = hgrn_scan_fwd(pc, lb_b, zero_h, COL_HFB, True, "ctx_hgrn_b")
    _, s_rf, cb_rf = ret_scan_fwd(pc16,cosc, sinc, lg_f, zero_r, False, "ctx_ret_f")
    _, s_rb, cb_rb = ret_scan_fwd(pc16,cosc, sinc, lg_b, zero_r, True, "ctx_ret_b")
    ohf, _, xb_hf = hgrn_scan_fwd(p, lb_f, s_hf, COL_HFF, False, "hgrn_f")
    o_hg, _, xb_hb = hgrn_scan_fwd(p, lb_b, s_hb, COL_HFB, True, "hgrn_b", prev=ohf)
    orf, _, xb_rf = ret_scan_fwd(p16,cosf, sinf, lg_f, s_rf, False, "ret_f")
    o_rt, _, xb_rb = ret_scan_fwd(p16,cosf, sinf, lg_b, s_rb, True, "ret_b", prev=orf)
    x1, x_mix, merged, ya, yb = mix_fwd(o_hg, o_rt, p16, x, g1, hgw, w["w_pa"], w["w_pb"], w["w_out"], "mix_fwd")
    hx2, gg, uu, hh, ff, dx2, sums_f = ffn_fwd(x1, target, nw2, sh2, sc2, g2, fw, w["wg"], w["wu"], w["wd"], "ffn_fwd")

    d_f, d_g, d_u, dx1, sums_fb = ffn_bwd(dx2, x1, ff, gg, uu, nw2, sc2, g2, w["wg"], w["wu"], w["wd"], "ffn_bwd")
    dw_gate, dw_up = matmul_tn_pair(hx2, d_g, d_u, "dw_ffn_gate_up")
    grads = {"wg": dw_gate, "wu": dw_up, "wd": matmul_tn(hh, d_f[None], "dw_ffn_down")}
    ffn_names = ["wg", "wu", "wd"]
    swap = () if place is None else [_other_half(grads[k], place[0]) for k in ffn_names]
    dxm, d_a, d_b, dga, dgb, dhg, drg, dohg, dort, sums_m, *from_sibling = mix_bwd(
        dx1, x_mix, ya, yb, o_hg, o_rt, p16, g1, hgw, w["w_pa"], w["w_pb"], w["w_out"], "mix_bwd", to_sibling=swap)
    grads["w_out"] = matmul_tn(merged[None], dxm[None], "dw_out").reshape(N_SHARD, D // N_SHARD, D)
    grads["w_pa"] = matmul_tn(ya[None], d_a[None], "dw_proj_hgrn").reshape(N_SHARD, D // N_SHARD, D)
    grads["w_pb"] = matmul_tn(yb[None], d_b[None], "dw_proj_ret").reshape(N_SHARD, 2 * D // N_SHARD, D)

    rq1, rk1, rv1, dlgf_x, ds_rf = ret_scan_bwd(p16,cosf, sinf, lg_f, xb_rf, dort, zero_r, None, False, "ret_f_bwd")
    drq, drk, drv, dlgb_x, ds_rb = ret_scan_bwd(p16,cosf, sinf, lg_b, xb_rb, dort, zero_r, (rq1, rk1, rv1), True, "ret_b_bwd")
    hq1, dzf, hv1, dlbf_x, ds_hf = hgrn_scan_bwd(p, lb_f, xb_hf, dohg, zero_h, None, COL_HFF, False, "hgrn_f_bwd")
    dhq, dzb, dhv, dlbb_x, ds_hb = hgrn_scan_bwd(p, lb_b, xb_hb, dohg, zero_h, (hq1, hv1), COL_HFB, True, "hgrn_b_bwd")
    dp = _pieces(dhq, dzf, dzb, dhv, dhg, drq, drk, drv, drg, dga, dgb)

    zc = jnp.zeros((Lc, D), F32)
    zc2 = jnp.zeros((Lc, 2 * D), F32)
    crq1, crk1, crv1, dlgf_c, _ = ret_scan_bwd(pc16,cosc, sinc, lg_f, cb_rf, zc2, ds_rf, None, False, "ctx_ret_f_bwd")
    cdrq, cdrk, cdrv, dlgb_c, _ = ret_scan_bwd(pc16,cosc, sinc, lg_b, cb_rb, zc2, ds_rb, (crq1, crk1, crv1), True, "ctx_ret_b_bwd")
    chq1, cdzf, chv1, dlbf_c, _ = hgrn_scan_bwd(pc, lb_f, cb_hf, zc, ds_hf, None, COL_HFF, False, "ctx_hgrn_f_bwd")
    cdhq, cdzb, cdhv, dlbb_c, _ = hgrn_scan_bwd(pc, lb_b, cb_hb, zc, ds_hb, (chq1, chv1), COL_HFB, True, "ctx_hgrn_b_bwd")
    dpc = _pieces(cdhq, cdzf, cdzb, cdhv, None, cdrq, cdrk, cdrv, None, None, None)
    _, sums_c = dhx_normbwd(dpc, w["w_in"], ctx, zc, nw1, sc1c, "dctx_in_proj")

    others = ["w_pa", "w_pb", "w_out", "wg", "wu", "wd"]
    if place is None:
        grads["w_in"] = matmul_tn_pieces(hx, dp, "dw_in", extra=(hxc, dpc))
        dx, sums_x = dhx_normbwd(dp, w["w_in"], x, dx1, nw1, sc1, "dx_in_proj")
    else:
        sums_o = _sibling_sums([grads[k] for k in others[:3]], others[:3], place)
        sums_o += [rs_add_sibling(grads[k], r, place[1], "rs_add_sibling_" + k) for k, r in zip(ffn_names, from_sibling)]
        grads["w_in"], *recv_o = matmul_tn_pieces(hx, dp, "dw_in", extra=(hxc, dpc),
                                                  to_chips=[a16 for _, a16 in sums_o])
        sums_i = _sibling_sums([grads["w_in"]], ["w_in"], place)
        dx, sums_x, recv_i = dhx_normbwd(dp, w["w_in"], x, dx1, nw1, sc1, "dx_in_proj", to_chips=[sums_i[0][1]])
        names = ["w_in"] + others
        halves = [rs_add_chips(a, r, place[2], "rs_add_chips_" + k)
                  for (a, _), r, k in zip(sums_i + sums_o, [recv_i] + recv_o, names)]
        grads = dict(zip(names, rs_join_halves(halves, "rs_join_halves")))

    def lg_row(f, b):
        return jnp.concatenate([_lane0(f), _lane0(b), jnp.zeros((D - 2 * HEADS,), F32)])

    small = _pack_small([
        sums_x[0], sums_x[1], sums_m[0], sums_fb[1], sums_fb[2], sums_fb[0],
        sums_c[0], sums_c[1],
        sums_x[2], sums_c[2], sums_fb[3], sums_m[1], sums_f[0],
        dlbf_x, dlbf_c, dlbb_x, dlbb_c,
        lg_row(dlgf_x, dlgb_x), lg_row(dlgf_c, dlgb_c),
        sums_f[1],
    ])
    return dx, grads, small


MESH = pl.DeviceIdType.MESH
ANY = pl.BlockSpec(memory_space=pl.ANY)
N_DEV = 8


def _place():
    return lax.axis_index("x"), lax.axis_index("y"), lax.axis_index("c")


def _other_chips(x, y):
    return [(1 - x, y), (x, 1 - y), (1 - x, 1 - y)]


def allgather8(xs, name):
    m, n = xs.shape

    def body(x_ref, out_ref, send_sems, recv_sems, local_sem):
        x, y, c = _place()
        me, sibling = (x, y, c), (x, y, 1 - c)
        chips = _other_chips(x, y)

        def rows(px, py, pc):
            return out_ref.at[pl.ds((4 * px + 2 * py + pc) * m, m), :]

        def copy(k, block, to, src=None):
            return pltpu.make_async_remote_copy(
                src_ref=rows(*block) if src is None else src, dst_ref=rows(*block),
                send_sem=send_sems.at[k], recv_sem=recv_sems.at[k], device_id=to, device_id_type=MESH)

        mine = pltpu.make_async_copy(x_ref, rows(*me), local_sem)
        mine.start()
        first = [copy(0, me, sibling, src=x_ref)]
        first += [copy(1 + j, me, (*chip, c), src=x_ref) for j, chip in enumerate(chips)]
        for cp in first:
            cp.start()
        passed = [copy(4 + j, (*chip, c), sibling) for j, chip in enumerate(chips)]
        for j, chip in enumerate(chips):
            copy(1 + j, (*chip, c), me).wait_recv()
            passed[j].start()
        copy(0, sibling, me).wait_recv()
        for j, chip in enumerate(chips):
            copy(4 + j, (*chip, 1 - c), me).wait_recv()
        for cp in first + passed:
            cp.wait_send()
        mine.wait()

    return pl.pallas_call(
        body, name=name,
        out_shape=jax.ShapeDtypeStruct((N_DEV * m, n), xs.dtype),
        in_specs=[pl.BlockSpec(memory_space=pltpu.VMEM)],
        out_specs=pl.BlockSpec(memory_space=pltpu.VMEM),
        scratch_shapes=[pltpu.SemaphoreType.DMA((7,)), pltpu.SemaphoreType.DMA((7,)), pltpu.SemaphoreType.DMA],
    )(xs)


def _gather_phases(ins, outs, send_sems, recv_sems, local_sems, relations=(0, 1, 2), stage=None):
    n = len(ins)
    x, y, c = _place()
    chips = _other_chips(x, y)

    def rows(i, core):
        h = ins[i].shape[0] // 2
        return pl.ds(pl.multiple_of(core * h, 16), h)

    def region(i, k, rs):
        if len(outs[i].shape) == 2:
            cols = ins[i].shape[1]
            return outs[i].at[rs, pl.ds(pl.multiple_of(k * cols, 128), cols)]
        return outs[i].at[k, rs, :]

    def landed(i, chip, core):
        return region(i, 2 * chip[0] + chip[1], rows(i, core))

    def copy(i, k, src, dst, to):
        return pltpu.make_async_remote_copy(src_ref=src, dst_ref=dst, send_sem=send_sems.at[6 * i + k],
                                            recv_sem=recv_sems.at[6 * i + k], device_id=to, device_id_type=MESH)

    def lift(i):
        return pltpu.make_async_copy(ins[i], stage[i], local_sems.at[i])

    def drop(i):
        return pltpu.make_async_copy(stage[i], region(i, 2 * x + y, pl.ds(0, ins[i].shape[0])), local_sems.at[i])

    def send(i, j):
        return copy(i, j, ins[i].at[rows(i, c), :], landed(i, (x, y), c), (*chips[j], c))

    def arrived(i, j, core, k):
        return copy(i, k, ins[i].at[rows(i, core), :], landed(i, chips[j], core), (x, y, 1 - c))

    def passed(i, j):
        return copy(i, 3 + j, landed(i, chips[j], c), landed(i, chips[j], c), (x, y, 1 - c))

    def start():
        for i in range(n):
            if stage is not None:
                lift(i).start()
            for j in relations:
                send(i, j).start()

    def forward():
        for i in range(n):
            if stage is not None:
                lift(i).wait()
                drop(i).start()
            for j in relations:
                arrived(i, j, c, j).wait_recv()
                passed(i, j).start()

    def finish():
        for i in range(n):
            for j in relations:
                arrived(i, j, 1 - c, 3 + j).wait_recv()
        for i in range(n):
            for j in relations:
                send(i, j).wait_send()
                passed(i, j).wait_send()
            if stage is not None:
                drop(i).wait()

    return start, forward, finish


def _gather_scratch(n):
    return [pltpu.SemaphoreType.DMA((6 * n,)), pltpu.SemaphoreType.DMA((6 * n,)), pltpu.SemaphoreType.DMA((n,))]


def rs_to_sibling(payloads, name):
    n = len(payloads)

    def body(*refs):
        start, finish = _to_sibling_phases(refs[:n], refs[n:2 * n], *refs[2 * n:])
        start()
        finish()

    return pl.pallas_call(
        body, name=name,
        out_shape=[jax.ShapeDtypeStruct(g.shape, g.dtype) for g in payloads],
        in_specs=[ANY] * n, out_specs=[ANY] * n,
        scratch_shapes=_to_sibling_scratch(n),
    )(*payloads)


def _to_sibling_phases(ins, outs, send_sems, recv_sems):
    def copies():
        x, y, c = _place()
        return [pltpu.make_async_remote_copy(src_ref=ins[i], dst_ref=outs[i], send_sem=send_sems.at[i],
                                             recv_sem=recv_sems.at[i], device_id=(x, y, 1 - c), device_id_type=MESH)
                for i in range(len(ins))]

    def start():
        for cp in copies():
            cp.start()

    def finish():
        for cp in copies():
            cp.wait()

    return start, finish


def _to_sibling_scratch(n):
    return [pltpu.SemaphoreType.DMA((n,)), pltpu.SemaphoreType.DMA((n,))]


def _to_chips_phases(ins, outs, send_sems, recv_sems):
    def copies():
        x, y, c = _place()
        return [pltpu.make_async_remote_copy(
            src_ref=ins[i].at[2 * px + py], dst_ref=outs[i].at[j], send_sem=send_sems.at[3 * i + j],
            recv_sem=recv_sems.at[3 * i + j], device_id=(px, py, c), device_id_type=MESH)
            for i in range(len(ins)) for j, (px, py) in enumerate(_other_chips(x, y))]

    def start():
        for cp in copies():
            cp.start()

    def finish():
        for cp in copies():
            cp.wait()

    return start, finish


def _to_chips_shapes(parts):
    return [jax.ShapeDtypeStruct((3,) + a.shape[1:], a.dtype) for a in parts]


def _to_chips_scratch(n):
    return [pltpu.SemaphoreType.DMA((3 * n,)), pltpu.SemaphoreType.DMA((3 * n,))]


def rs_join_halves(fulls, name):
    n = len(fulls)

    def body(*refs):
        outs = refs[n:2 * n]
        send_sems, recv_sems = refs[2 * n:]
        x, y, c = _place()

        def copy(i, core):
            h = fulls[i].shape[0] // 2
            rows = outs[i].at[pl.ds(pl.multiple_of(core * h, 8), h), :]
            return pltpu.make_async_remote_copy(src_ref=rows, dst_ref=rows, send_sem=send_sems.at[i],
                                                recv_sem=recv_sems.at[i], device_id=(x, y, 1 - c), device_id_type=MESH)

        sent = [copy(i, c) for i in range(n)]
        for cp in sent:
            cp.start()
        for i in range(n):
            copy(i, 1 - c).wait_recv()
        for cp in sent:
            cp.wait_send()

    return pl.pallas_call(
        body, name=name,
        out_shape=[jax.ShapeDtypeStruct(a.shape, a.dtype) for a in fulls],
        in_specs=[ANY] * n, out_specs=[ANY] * n,
        input_output_aliases={i: i for i in range(n)},
        scratch_shapes=[pltpu.SemaphoreType.DMA((n,)), pltpu.SemaphoreType.DMA((n,))],
    )(*fulls)


def _row_tile(rows, cols, limit_bytes=2 * 1024 * 1024, mult=8):
    best = mult
    for t in range(mult, rows + 1, mult):
        if rows % t == 0 and t * cols * 4 <= limit_bytes:
            best = t
    return best


def rs_add_sibling(g, recv, c, name):
    if g.ndim == 2:
        h, C = recv.shape[0], recv.shape[1] // N_SHARD
    else:
        _, h, C = recv.shape
    tr = _row_tile(h, C, mult=16)
    nt = h // tr

    def body(c_ref, g_ref, r_ref, o_ref, o16_ref):
        s = g_ref[...] + r_ref[...].astype(F32)
        o_ref[...] = s
        o16_ref[...] = s.astype(BF16)

    blk = pl.BlockSpec((None, tr, C), lambda k, i, c_ref: (k, i, 0))
    if g.ndim == 2:
        g_spec = pl.BlockSpec((tr, C), lambda k, i, c_ref: (c_ref[0] * nt + i, k))
        r_spec = pl.BlockSpec((tr, C), lambda k, i, c_ref: (i, k))
    else:
        g_spec = pl.BlockSpec((None, tr, C), lambda k, i, c_ref: (k, c_ref[0] * nt + i, 0))
        r_spec = blk
    return pl.pallas_call(
        body, name=name,
        grid_spec=pltpu.PrefetchScalarGridSpec(
            num_scalar_prefetch=1, grid=(N_SHARD, nt),
            in_specs=[g_spec, r_spec],
            out_specs=[blk, blk]),
        out_shape=[jax.ShapeDtypeStruct((N_SHARD, h, C), F32), jax.ShapeDtypeStruct((N_SHARD, h, C), BF16)],
        compiler_params=_params("parallel", "parallel"),
    )(c, g, recv)


def rs_add_chips(part, recv, place, name):
    _, h, C = part.shape
    tr = _row_tile(h, C, mult=16)
    nt = h // tr

    def body(k_ref, p_ref, r_ref, o_ref):
        o_ref[...] = ((p_ref[...] + r_ref[0].astype(F32)) + r_ref[1].astype(F32)) + r_ref[2].astype(F32)

    return pl.pallas_call(
        body, name=name,
        grid_spec=pltpu.PrefetchScalarGridSpec(
            num_scalar_prefetch=1, grid=(nt,),
            in_specs=[pl.BlockSpec((None, tr, C), lambda i, k_ref: (k_ref[0], i, 0)),
                      pl.BlockSpec((3, tr, C), lambda i, k_ref: (0, i, 0))],
            out_specs=pl.BlockSpec((tr, C), lambda i, k_ref: (k_ref[1] * nt + i, 0))),
        out_shape=jax.ShapeDtypeStruct((2 * h, C), F32),
        compiler_params=_params("parallel"),
    )(place, part, recv)


def _adamw_math(w, g, m, v):
    m = ADAM_B1 * m + (1.0 - ADAM_B1) * g
    v = ADAM_B2 * v + (1.0 - ADAM_B2) * (g * g)
    m_hat = m / (1.0 - ADAM_B1 ** ADAM_STEP)
    v_hat = v / (1.0 - ADAM_B2 ** ADAM_STEP)
    delta = -ADAM_LR * (m_hat / (jnp.sqrt(v_hat) + ADAM_EPS) + ADAM_WD * w)
    return delta, m, v


def adamw(w, g, m, v, name):
    R, C = w.shape
    tr = _row_tile(R, C, 1024 * 1024)

    def body(w_ref, g_ref, m_ref, v_ref, go_ref, d_ref, nm_ref, nv_ref):
        g = g_ref[...]
        go_ref[...] = g
        d_ref[...], nm_ref[...], nv_ref[...] = _adamw_math(w_ref[...], g, m_ref[...], v_ref[...])

    blk = pl.BlockSpec((tr, C), lambda i: (i, 0))
    return pl.pallas_call(
        body, name=name, grid=(R // tr,), in_specs=[blk] * 4, out_specs=[blk] * 4,
        out_shape=[jax.ShapeDtypeStruct((R, C), F32)] * 4,
        compiler_params=_params("parallel"),
    )(w, g, m, v)


MOD_SH = 6 * D // N_SHARD
PK_ROWS = 16


def mod_fwd(call16, w_sh, b_sh, name):
    def body(c_ref, w_ref, b_ref, o_ref):
        o_ref[...] = _dot(_silu_parts(c_ref[...])[0], w_ref[...], prec=HI) + b_ref[...]

    return pl.pallas_call(body, name=name, out_shape=jax.ShapeDtypeStruct((16, MOD_SH), F32),
                          compiler_params=_params())(call16, w_sh, b_sh)


def prep_small(lbf2, lbb2, theta_row, name):
    def body(f_ref, b_ref, t_ref, lbf_ref, lbb_ref, lg_ref):
        lbf_ref[...] = _sigmoid(f_ref[0:1, :] - f_ref[1:2, :])
        lbb_ref[...] = _sigmoid(b_ref[0:1, :] - b_ref[1:2, :])
        t = t_ref[...]
        lg_ref[...] = jnp.minimum(t, 0.0) - jnp.log(1.0 + jnp.exp(-jnp.abs(t)))

    row = jax.ShapeDtypeStruct((1, D), F32)
    return pl.pallas_call(body, name=name, out_shape=[row, row, row], compiler_params=_params())(lbf2, lbb2, theta_row)


def small_grads(g3, lbf, lbb, theta_row, name):
    def body(g_ref, lbf_ref, lbb_ref, t_ref, pk_ref, aux_ref):
        s = g_ref[0]
        for d in range(1, N_DEV):
            s = s + g_ref[d]
        pk_ref[...] = jnp.zeros_like(pk_ref)
        aux_ref[...] = jnp.zeros_like(aux_ref)
        pk_ref[1:7, :] = s[0:6]
        pk_ref[1:3, :] += s[6:8]
        pk_ref[7:8, :] = s[8:9] + s[9:10]
        pk_ref[8:9, :] = s[10:11]
        lbf, lbb = lbf_ref[...], lbb_ref[...]
        daf = (s[13:14] + s[14:15]) * lbf * (1.0 - lbf)
        dab = (s[15:16] + s[16:17]) * lbb * (1.0 - lbb)
        pk_ref[9:10, :] = daf
        pk_ref[10:11, :] = -daf
        pk_ref[11:12, :] = dab
        pk_ref[12:13, :] = -dab
        pk_ref[13:14, :] = s[11:12]
        pk_ref[14:15, :] = (s[17:18] + s[18:19]) * _sigmoid(-t_ref[...])
        pk_ref[15:16, :] = s[12:13]
        aux_ref[0:2, :] = s[6:8]
        aux_ref[2:3, :] = jnp.broadcast_to(jnp.sum(s[19:20], axis=-1, keepdims=True), (1, D))

    return pl.pallas_call(body, name=name,
                          out_shape=[jax.ShapeDtypeStruct((PK_ROWS, D), F32), jax.ShapeDtypeStruct((8, D), F32)],
                          compiler_params=_params())(g3, lbf, lbb, theta_row)


def mod_bwd(call16, dmod_sh, w_sh, name):
    def body(c_ref, d_ref, w_ref, dw_ref, ds_ref):
        dm = d_ref[...]
        dw_ref[...] = _dot(_silu_parts(c_ref[...])[0], dm, 0, 0, prec=HI)
        ds_ref[...] = jnp.zeros_like(ds_ref)
        ds_ref[0:1, :] = _dot(dm[8:9, :], w_ref[...], 1, 1, prec=HI)

    return pl.pallas_call(body, name=name,
                          out_shape=[jax.ShapeDtypeStruct((D, MOD_SH), F32), jax.ShapeDtypeStruct((8, D), F32)],
                          compiler_params=_params())(call16, dmod_sh, w_sh)


def adamw_small(g4, pk_g, pk_w, pk_m, pk_v, name):
    def body(g4_ref, g_ref, w_ref, m_ref, v_ref, go_ref, d_ref, nm_ref, nv_ref):
        w = w_ref[...]
        ds = ((g4_ref[0:1, :] + g4_ref[16:17, :]) + g4_ref[32:33, :]) + g4_ref[48:49, :]
        row = lax.broadcasted_iota(jnp.int32, (PK_ROWS, D), 0)
        g = jnp.where(row == 0, ds * _silu_parts(w[0:1, :])[1], g_ref[...])
        go_ref[...] = g
        d_ref[...], nm_ref[...], nv_ref[...] = _adamw_math(w, g, m_ref[...], v_ref[...])

    pk = jax.ShapeDtypeStruct((PK_ROWS, D), F32)
    return pl.pallas_call(body, name=name, out_shape=[pk, pk, pk, pk], compiler_params=_params())(g4, pk_g, pk_w, pk_m, pk_v)


def _pack_params(c_ctx, b_mod, n1, n2, lbf, lbb, hgn, th_f, th_b, fin):
    theta = jnp.concatenate([th_f.reshape(HEADS), th_b.reshape(HEADS), jnp.zeros((D - 2 * HEADS,), F32)])
    return jnp.concatenate([c_ctx.reshape(1, D), b_mod.reshape(6, D), n1.reshape(1, D), n2.reshape(1, D), lbf, lbb,
                            hgn.reshape(1, D), theta.reshape(1, D), fin.reshape(1, D)], axis=0)


def _unpack_params(pk):
    return (pk[0], pk[1:7].reshape(1, 6 * D), pk[7:8], pk[8:9], pk[9:11], pk[11:13], pk[13:14],
            pk[14, 0:HEADS].reshape(1, HEADS), pk[14, HEADS:2 * HEADS].reshape(1, HEADS), pk[15])


def kernel(x, c, ctx, c_ctx, w_mod, b_mod, norm1_w, norm2_w, w_in, hg_lb_fwd, hg_lb_bwd, hg_norm_w, rt_theta_fwd, rt_theta_bwd, w_proj_hgrn, w_proj_ret, w_out, w_ffn_gate, w_ffn_up, w_ffn_down, final_norm_w, loss_target, m_c_ctx, m_w_mod, m_b_mod, m_norm1_w, m_norm2_w, m_w_in, m_hg_lb_fwd, m_hg_lb_bwd, m_hg_norm_w, m_rt_theta_fwd, m_rt_theta_bwd, m_w_proj_hgrn, m_w_proj_ret, m_w_out, m_w_ffn_gate, m_w_ffn_up, m_w_ffn_down, m_final_norm_w, v_c_ctx, v_w_mod, v_b_mod, v_norm1_w, v_norm2_w, v_w_in, v_hg_lb_fwd, v_hg_lb_bwd, v_hg_norm_w, v_rt_theta_fwd, v_rt_theta_bwd, v_w_proj_hgrn, v_w_proj_ret, v_w_out, v_w_ffn_gate, v_w_ffn_up, v_w_ffn_down, v_final_norm_w):
    xi, yi, ci = _place()
    dev = 4 * xi + 2 * yi + ci
    chip = 2 * xi + yi
    core_arg = jnp.reshape(ci, (1,)).astype(jnp.int32)
    place_arg = jnp.stack([chip, ci]).astype(jnp.int32)

    c_all = allgather8(jnp.concatenate([c, jnp.zeros((7, D), F32)], axis=0), "gather_c").reshape(N_DEV, 8, D)[:, 0]
    call16 = jnp.concatenate([c_all, c_ctx.reshape(1, D), jnp.zeros((7, D), F32)], axis=0)
    b_sh = lax.dynamic_slice_in_dim(b_mod, chip * MOD_SH, MOD_SH, axis=1)
    mod_sh = mod_fwd(call16, w_mod[0], b_sh, "mod_fwd")
    mod_g = allgather8(mod_sh, "gather_mod").reshape(N_DEV, 16, MOD_SH)
    mod_all = jnp.concatenate([mod_g[0], mod_g[2], mod_g[4], mod_g[6]], axis=1)
    mod_x = lax.dynamic_index_in_dim(mod_all, dev, axis=0, keepdims=False).reshape(6, D)
    mod_c = mod_all[8].reshape(6, D)

    pk_w = _pack_params(c_ctx, b_mod, norm1_w, norm2_w, hg_lb_fwd, hg_lb_bwd, hg_norm_w, rt_theta_fwd, rt_theta_bwd, final_norm_w)
    theta_row = pk_w[14:15]
    lb_f, lb_b, lg_row = prep_small(hg_lb_fwd, hg_lb_bwd, theta_row, "prep_small")
    lg_f = jnp.broadcast_to(lg_row[0, 0:HEADS].reshape(HEADS, 1, 1), (HEADS, 1, RT_DV))
    lg_b = jnp.broadcast_to(lg_row[0, HEADS:2 * HEADS].reshape(HEADS, 1, 1), (HEADS, 1, RT_DV))

    rest = [s[0].astype(BF16) for s in (w_proj_hgrn, w_proj_ret, w_out, w_ffn_gate, w_ffn_up, w_ffn_down)]

    dx, full, small = local_step(x[0], ctx[0], loss_target[0], mod_x, mod_c, lb_f, lb_b, lg_f, lg_b,
                                 norm1_w, norm2_w, hg_norm_w, final_norm_w.reshape(1, D),
                                 {"w_in_shard": w_in[0].astype(BF16)}, rest, (ci, core_arg, place_arg))

    g3 = allgather8(small, "gather_small").reshape(N_DEV, SMALL_ROWS, D)
    pk_g, aux = small_grads(g3, lb_f, lb_b, theta_row, "small_grads")
    loss = aux[2, 0]
    dmod16 = jnp.concatenate([
        g3[:, 0:6, :].reshape(N_DEV, 6 * D),
        jnp.concatenate([aux[0], aux[1], jnp.zeros((4 * D,), F32)]).reshape(1, 6 * D),
        jnp.zeros((7, 6 * D), F32)], axis=0)
    dmod_sh = lax.dynamic_slice_in_dim(dmod16, chip * MOD_SH, MOD_SH, axis=1)
    g_wmod, dsilu = mod_bwd(call16, dmod_sh, w_mod[0], "mod_bwd")
    g4 = allgather8(dsilu, "gather_dsilu")
    pk_m = _pack_params(m_c_ctx, m_b_mod, m_norm1_w, m_norm2_w, m_hg_lb_fwd, m_hg_lb_bwd, m_hg_norm_w, m_rt_theta_fwd, m_rt_theta_bwd, m_final_norm_w)
    pk_v = _pack_params(v_c_ctx, v_b_mod, v_norm1_w, v_norm2_w, v_hg_lb_fwd, v_hg_lb_bwd, v_hg_norm_w, v_rt_theta_fwd, v_rt_theta_bwd, v_final_norm_w)
    pk_g, pk_d, pk_nm, pk_nv = adamw_small(g4, pk_g, pk_w, pk_m, pk_v, "adamw_small")

    big = {
        "w_mod": (g_wmod, w_mod, m_w_mod, v_w_mod),
        "w_in": (full["w_in"], w_in, m_w_in, v_w_in),
        "w_pa": (full["w_pa"], w_proj_hgrn, m_w_proj_hgrn, v_w_proj_hgrn),
        "w_pb": (full["w_pb"], w_proj_ret, m_w_proj_ret, v_w_proj_ret),
        "w_out": (full["w_out"], w_out, m_w_out, v_w_out),
        "wg": (full["wg"], w_ffn_gate, m_w_ffn_gate, v_w_ffn_gate),
        "wu": (full["wu"], w_ffn_up, m_w_ffn_up, v_w_ffn_up),
        "wd": (full["wd"], w_ffn_down, m_w_ffn_down, v_w_ffn_down),
    }
    res = {}
    for k, (g, wt, mt, vt) in big.items():
        res[k] = tuple(a[None] for a in adamw(wt[0], g, mt[0], vt[0], "adamw_" + k))

    sm = [_unpack_params(p) for p in (pk_g, pk_d, pk_nm, pk_nv)]
    outs = []
    for t in range(4):
        (s_cctx, s_bmod, s_n1, s_n2, s_lbf, s_lbb, s_hgn, s_thf, s_thb, s_fin) = sm[t]
        outs.append([s_cctx, res["w_mod"][t], s_bmod, s_n1, s_n2, res["w_in"][t], s_lbf, s_lbb, s_hgn, s_thf, s_thb,
                     res["w_pa"][t], res["w_pb"][t], res["w_out"][t], res["wg"][t], res["wu"][t], res["wd"][t], s_fin])
    return (loss, dx[None], *outs[0], *outs[1], *outs[2], *outs[3])
```
